```python
import math
import jax, jax.numpy as jnp
from jax import lax
import numpy as np

D_MODEL = 1024
BATCH = 8
SEQ = 4096
DEPTH = 4

GRID_W = 64
CTX_LEN = 256
N_EVEN = (DEPTH + 1) // 2
N_ODD = DEPTH // 2
EPS = 1e-6

CHUNK = 128
A_WIDTH = D_MODEL // 2
A_GROUPS = 4
A_GROUP_W = A_WIDTH // A_GROUPS
B_WIDTH = D_MODEL // 2
B_CH = 16
B_GROUPS = B_WIDTH // B_CH
B_STATE = 64
EVEN_IN = 3 * A_WIDTH + 2 * B_WIDTH
EVEN_MIX = A_WIDTH + B_WIDTH
HEAD_DIM = 128
N_Q = D_MODEL // HEAD_DIM
N_KV = 2
Q_PER_KV = N_Q // N_KV
C_WIDTH = N_Q * HEAD_DIM
KV_WIDTH = N_KV * HEAD_DIM
ODD_IN = 2 * C_WIDTH + 2 * KV_WIDTH
Q_BLOCK = 128
ROPE_THETA = 10000.0
ROPE_PAIRS = HEAD_DIM // 4

kernel_name = "hybrid_gmlp_s5_gqa_prefix_trunk"


def _rms_norm(x, g):
    xf = x.astype(jnp.float32)
    y = xf * lax.rsqrt(jnp.mean(xf * xf, axis=-1, keepdims=True) + EPS)
    return (y * g.astype(jnp.float32)).astype(x.dtype)


def _layer_norm(x, g):
    xf = x.astype(jnp.float32)
    mu = jnp.mean(xf, axis=-1, keepdims=True)
    xc = xf - mu
    var = jnp.mean(xc * xc, axis=-1, keepdims=True)
    return (xc * lax.rsqrt(var + EPS) * g.astype(jnp.float32)).astype(x.dtype)


def _ada(cond, w, b):
    m = jax.nn.silu(cond) @ w + b
    return jnp.split(m, 3, axis=-1)


def _prologue(x, ctx, c, c_ctx, g, w_mod, b_mod):
    sh, sc, gt = _ada(c, w_mod, b_mod)
    shc, scc, gtc = _ada(c_ctx, w_mod, b_mod)
    hx = _rms_norm(x, g) * (1 + sc[:, None]) + sh[:, None]
    hc = _rms_norm(ctx, g) * (1 + scc) + shc
    h = jnp.concatenate([hc, hx], axis=1)
    return h, gt[:, None], gtc


def _chunk_gmlp(u, v, v_g, w_s, b_s):
    Bn, L, _ = v.shape
    v = _layer_norm(v, v_g)
    vc = v.reshape(Bn, L // CHUNK, CHUNK, A_GROUPS, A_GROUP_W)
    mixed = jnp.einsum('gpq,bnqgc->bnpgc', w_s, vc) + b_s.T[None, None, :, :, None]
    return u * mixed.reshape(Bn, L, A_WIDTH)


def _s5_discretize(lam_re, lam_im, log_dt, b_re, b_im):
    lam_re = lam_re.astype(jnp.float32)
    lam_im = lam_im.astype(jnp.float32)
    dt = jnp.exp(log_dt.astype(jnp.float32))[:, None]
    mag = jnp.exp(lam_re * dt)
    lb_re = mag * jnp.cos(lam_im * dt)
    lb_im = mag * jnp.sin(lam_im * dt)
    den = lam_re * lam_re + lam_im * lam_im
    n_re, n_im = lb_re - 1.0, lb_im
    f_re = (n_re * lam_re + n_im * lam_im) / den
    f_im = (n_im * lam_re - n_re * lam_im) / den
    b_re = b_re.astype(jnp.float32)
    b_im = b_im.astype(jnp.float32)
    bb_re = f_re[..., None] * b_re - f_im[..., None] * b_im
    bb_im = f_re[..., None] * b_im + f_im[..., None] * b_re
    return lb_re, lb_im, bb_re, bb_im


def _ssm_combine(e1, e2):
    a1r, a1i, b1r, b1i = e1
    a2r, a2i, b2r, b2i = e2
    return (a1r * a2r - a1i * a2i,
            a1r * a2i + a1i * a2r,
            a2r * b1r - a2i * b1i + b2r,
            a2r * b1i + a2i * b1r + b2i)


def _s5_direction(seq, lam_re, lam_im, log_dt, b_re, b_im, c_re, c_im):
    Bn, L, W = seq.shape
    lb_re, lb_im, bb_re, bb_im = _s5_discretize(lam_re, lam_im, log_dt, b_re, b_im)
    u = seq.reshape(Bn, L, B_GROUPS, B_CH)
    bu_re = jnp.einsum('blgh,gph->blgp', u, bb_re)
    bu_im = jnp.einsum('blgh,gph->blgp', u, bb_im)
    a_re = jnp.broadcast_to(lb_re, (1, L, B_GROUPS, B_STATE))
    a_im = jnp.broadcast_to(lb_im, (1, L, B_GROUPS, B_STATE))
    _, _, h_re, h_im = lax.associative_scan(_ssm_combine, (a_re, a_im, bu_re, bu_im), axis=1)
    y = (jnp.einsum('blgp,ghp->blgh', h_re, c_re.astype(jnp.float32))
         - jnp.einsum('blgp,ghp->blgh', h_im, c_im.astype(jnp.float32)))
    return y.reshape(Bn, L, W)


def _s5_bidir(xs, n_ctx, lam_re, lam_im, log_dt, b_re, b_im, c_re, c_im, d_skip, w_glu, b_glu):
    dtype = xs.dtype
    xf = xs.astype(jnp.float32)
    x_ctx, x_lat = xf[:, :n_ctx], xf[:, n_ctx:]
    fwd = _s5_direction(xf, lam_re[0], lam_im[0], log_dt[0], b_re[0], b_im[0], c_re[0], c_im[0])
    bwd_seq = jnp.concatenate([x_ctx[:, ::-1], x_lat[:, ::-1]], axis=1)
    bwd = _s5_direction(bwd_seq, lam_re[1], lam_im[1], log_dt[1], b_re[1], b_im[1], c_re[1], c_im[1])
    bwd = jnp.concatenate([bwd[:, :n_ctx][:, ::-1], bwd[:, n_ctx:][:, ::-1]], axis=1)
    y = fwd + bwd + d_skip.astype(jnp.float32) * xf
    y = jax.nn.gelu(y)
    y = y * jax.nn.sigmoid(y @ w_glu.astype(jnp.float32) + b_glu.astype(jnp.float32))
    return y.astype(dtype)


def _axial_rope(n_lat):
    rows = n_lat // GRID_W
    row = jnp.repeat(jnp.arange(rows), GRID_W)
    col = jnp.tile(jnp.arange(GRID_W), rows)
    freqs = ROPE_THETA ** (-jnp.arange(ROPE_PAIRS, dtype=jnp.float32) / ROPE_PAIRS)
    ang = jnp.stack([row[:, None] * freqs, col[:, None] * freqs], axis=1)
    return jnp.cos(ang), jnp.sin(ang)


def _apply_rope(x, cos, sin):
    lead = x.shape[:-1]
    xs = x.reshape(lead + (2, 2, ROPE_PAIRS)).astype(jnp.float32)
    bshape = (1, x.shape[1]) + (1,) * (x.ndim - 3) + (2, ROPE_PAIRS)
    cos = cos.reshape(bshape)
    sin = sin.reshape(bshape)
    x1, x2 = xs[..., 0, :], xs[..., 1, :]
    out = jnp.stack([x1 * cos - x2 * sin, x1 * sin + x2 * cos], axis=-2)
    return out.reshape(x.shape).astype(x.dtype)


def _attend(q, k, v):
    s = jnp.einsum('btkgd,bskd->bkgts', q, k).astype(jnp.float32) * (HEAD_DIM ** -0.5)
    p = jax.nn.softmax(s, axis=-1).astype(v.dtype)
    return jnp.einsum('bkgts,bskd->btkgd', p, v)


def _even_layer(x, ctx, c, c_ctx, norm_g, w_mod, b_mod, w_in, w_out, gm_v_g, gm_w_s, gm_b_s,
                lam_re, lam_im, log_dt, b_re, b_im, c_re, c_im, d_skip, w_glu, b_glu, update_ctx):
    n_ctx = ctx.shape[1]
    h, gt, gtc = _prologue(x, ctx, c, c_ctx, norm_g, w_mod, b_mod)
    u, v, g_a, xs, g_b = jnp.split(
        h @ w_in, [A_WIDTH, 2 * A_WIDTH, 3 * A_WIDTH, 3 * A_WIDTH + B_WIDTH], axis=-1)
    y_a = jnp.concatenate([
        _chunk_gmlp(u[:, :n_ctx], v[:, :n_ctx], gm_v_g, gm_w_s, gm_b_s),
        _chunk_gmlp(u[:, n_ctx:], v[:, n_ctx:], gm_v_g, gm_w_s, gm_b_s)], axis=1)
    y_a = y_a * jax.nn.silu(g_a)
    y_b = _s5_bidir(xs, n_ctx, lam_re, lam_im, log_dt, b_re, b_im, c_re, c_im, d_skip, w_glu, b_glu)
    y_b = y_b * jax.nn.silu(g_b)
    mix = jnp.concatenate([y_a, y_b], axis=-1)
    x = x + gt * (mix[:, n_ctx:] @ w_out)
    if update_ctx:
        ctx = ctx + gtc * (mix[:, :n_ctx] @ w_out)
    return x, ctx


def _odd_layer(x, ctx, c, c_ctx, norm_g, w_mod, b_mod, w_in, w_out, q_g, k_g, update_ctx):
    n_ctx = ctx.shape[1]
    Bn, L, _ = x.shape
    n_tot = n_ctx + L
    h, gt, gtc = _prologue(x, ctx, c, c_ctx, norm_g, w_mod, b_mod)
    q, k, v, gate = jnp.split(h @ w_in, [C_WIDTH, C_WIDTH + KV_WIDTH, C_WIDTH + 2 * KV_WIDTH], axis=-1)
    q = _rms_norm(q.reshape(Bn, n_tot, N_KV, Q_PER_KV, HEAD_DIM), q_g)
    k = _rms_norm(k.reshape(Bn, n_tot, N_KV, HEAD_DIM), k_g)
    v = v.reshape(Bn, n_tot, N_KV, HEAD_DIM)
    cos, sin = _axial_rope(L)
    q_lat = _apply_rope(q[:, n_ctx:], cos, sin)
    k_all = jnp.concatenate([k[:, :n_ctx], _apply_rope(k[:, n_ctx:], cos, sin)], axis=1)
    nb = L // Q_BLOCK
    qb = jnp.moveaxis(q_lat.reshape(Bn, nb, Q_BLOCK, N_KV, Q_PER_KV, HEAD_DIM), 1, 0)
    o = lax.map(lambda qi: _attend(qi, k_all, v), qb)
    o_lat = jnp.moveaxis(o, 0, 1).reshape(Bn, L, C_WIDTH)
    x = x + gt * ((o_lat * jax.nn.silu(gate[:, n_ctx:])) @ w_out)
    if update_ctx:
        o_ctx = _attend(q[:, :n_ctx], k[:, :n_ctx], v[:, :n_ctx]).reshape(Bn, n_ctx, C_WIDTH)
        ctx = ctx + gtc * ((o_ctx * jax.nn.silu(gate[:, :n_ctx])) @ w_out)
    return x, ctx


def _fwd_setup_inputs(seed: int = 0) -> dict:
    key = jax.random.key(seed)
    ks = iter(jax.random.split(key, 40))
    f32 = jnp.float32
    nrm = lambda shape, s=1.0: jax.random.normal(next(ks), shape, f32) * s
    lam_re = -0.5 + nrm((N_EVEN, 2, B_GROUPS, B_STATE), 0.01)
    lam_im = (jnp.pi * jnp.arange(B_STATE, dtype=f32)) + nrm((N_EVEN, 2, B_GROUPS, B_STATE), 0.01)
    return {
        "x": nrm((BATCH, SEQ, D_MODEL)),
        "c": nrm((BATCH, D_MODEL)),
        "ctx": nrm((BATCH, CTX_LEN, D_MODEL)),
        "c_ctx": nrm((D_MODEL,)),
        "norm_g": 1.0 + nrm((DEPTH, D_MODEL), 0.02),
        "w_mod": nrm((DEPTH, D_MODEL, 3 * D_MODEL), 0.5 * D_MODEL ** -0.5),
        "b_mod": nrm((DEPTH, 3 * D_MODEL), 0.02),
        "we_in": nrm((N_EVEN, D_MODEL, EVEN_IN), D_MODEL ** -0.5),
        "we_out": nrm((N_EVEN, EVEN_MIX, D_MODEL), EVEN_MIX ** -0.5),
        "gm_v_g": 1.0 + nrm((N_EVEN, A_WIDTH), 0.02),
        "gm_w_s": nrm((N_EVEN, A_GROUPS, CHUNK, CHUNK), CHUNK ** -0.5),
        "gm_b_s": 1.0 + nrm((N_EVEN, A_GROUPS, CHUNK), 0.1),
        "s5_lam_re": lam_re,
        "s5_lam_im": lam_im,
        "s5_log_dt": jax.random.uniform(next(ks), (N_EVEN, 2, B_GROUPS), f32,
                                        minval=math.log(1e-3), maxval=math.log(1e-1)),
        "s5_b_re": nrm((N_EVEN, 2, B_GROUPS, B_STATE, B_CH), (2 * B_CH) ** -0.5),
        "s5_b_im": nrm((N_EVEN, 2, B_GROUPS, B_STATE, B_CH), (2 * B_CH) ** -0.5),
        "s5_c_re": nrm((N_EVEN, 2, B_GROUPS, B_CH, B_STATE), B_STATE ** -0.5),
        "s5_c_im": nrm((N_EVEN, 2, B_GROUPS, B_CH, B_STATE), B_STATE ** -0.5),
        "s5_d": nrm((N_EVEN, B_WIDTH)),
        "s5_w_glu": nrm((N_EVEN, B_WIDTH, B_WIDTH), B_WIDTH ** -0.5),
        "s5_b_glu": nrm((N_EVEN, B_WIDTH), 0.02),
        "wo_in": nrm((N_ODD, D_MODEL, ODD_IN), D_MODEL ** -0.5),
        "wo_out": nrm((N_ODD, C_WIDTH, D_MODEL), C_WIDTH ** -0.5),
        "q_norm_g": 1.0 + nrm((N_ODD, HEAD_DIM), 0.02),
        "k_norm_g": 1.0 + nrm((N_ODD, HEAD_DIM), 0.02),
        "final_g": 1.0 + nrm((D_MODEL,), 0.02),
    }


def _fwd_reference(x, c, ctx, c_ctx, norm_g, w_mod, b_mod, we_in, we_out, gm_v_g, gm_w_s, gm_b_s,
              s5_lam_re, s5_lam_im, s5_log_dt, s5_b_re, s5_b_im, s5_c_re, s5_c_im, s5_d,
              s5_w_glu, s5_b_glu, wo_in, wo_out, q_norm_g, k_norm_g, final_g):
    for layer in range(DEPTH):
        update_ctx = layer < DEPTH - 1
        i = layer // 2
        if layer % 2 == 0:
            x, ctx = _even_layer(x, ctx, c, c_ctx, norm_g[layer], w_mod[layer], b_mod[layer],
                                 we_in[i], we_out[i], gm_v_g[i], gm_w_s[i], gm_b_s[i],
                                 s5_lam_re[i], s5_lam_im[i], s5_log_dt[i], s5_b_re[i], s5_b_im[i],
                                 s5_c_re[i], s5_c_im[i], s5_d[i], s5_w_glu[i], s5_b_glu[i], update_ctx)
        else:
            x, ctx = _odd_layer(x, ctx, c, c_ctx, norm_g[layer], w_mod[layer], b_mod[layer],
                                wo_in[i], wo_out[i], q_norm_g[i], k_norm_g[i], update_ctx)
    return _rms_norm(x, final_g)


import jax as _jax
import jax.numpy as _jnp

TWIN_FORMAT = 'train_step'
FWD_PARAMS = ['x', 'c', 'ctx', 'c_ctx', 'norm_g', 'w_mod', 'b_mod', 'we_in', 'we_out', 'gm_v_g', 'gm_w_s', 'gm_b_s', 's5_lam_re', 's5_lam_im', 's5_log_dt', 's5_b_re', 's5_b_im', 's5_c_re', 's5_c_im', 's5_d', 's5_w_glu', 's5_b_glu', 'wo_in', 'wo_out', 'q_norm_g', 'k_norm_g', 'final_g']
TWIN_WEIGHTS = ['c_ctx', 'norm_g', 'w_mod', 'b_mod', 'we_in', 'we_out', 'gm_v_g', 'gm_w_s', 'gm_b_s', 's5_lam_re', 's5_lam_im', 's5_log_dt', 's5_b_re', 's5_b_im', 's5_c_re', 's5_c_im', 's5_d', 's5_w_glu', 's5_b_glu', 'wo_in', 'wo_out', 'q_norm_g', 'k_norm_g', 'final_g']
TWIN_DIFF_INPUT = 'x'
TWIN_INPUTS = ['x', 'c', 'ctx', 'c_ctx', 'norm_g', 'w_mod', 'b_mod', 'we_in', 'we_out', 'gm_v_g', 'gm_w_s', 'gm_b_s', 's5_lam_re', 's5_lam_im', 's5_log_dt', 's5_b_re', 's5_b_im', 's5_c_re', 's5_c_im', 's5_d', 's5_w_glu', 's5_b_glu', 'wo_in', 'wo_out', 'q_norm_g', 'k_norm_g', 'final_g', 'loss_target', 'm_c_ctx', 'm_norm_g', 'm_w_mod', 'm_b_mod', 'm_we_in', 'm_we_out', 'm_gm_v_g', 'm_gm_w_s', 'm_gm_b_s', 'm_s5_lam_re', 'm_s5_lam_im', 'm_s5_log_dt', 'm_s5_b_re', 'm_s5_b_im', 'm_s5_c_re', 'm_s5_c_im', 'm_s5_d', 'm_s5_w_glu', 'm_s5_b_glu', 'm_wo_in', 'm_wo_out', 'm_q_norm_g', 'm_k_norm_g', 'm_final_g', 'v_c_ctx', 'v_norm_g', 'v_w_mod', 'v_b_mod', 'v_we_in', 'v_we_out', 'v_gm_v_g', 'v_gm_w_s', 'v_gm_b_s', 'v_s5_lam_re', 'v_s5_lam_im', 'v_s5_log_dt', 'v_s5_b_re', 'v_s5_b_im', 'v_s5_c_re', 'v_s5_c_im', 'v_s5_d', 'v_s5_w_glu', 'v_s5_b_glu', 'v_wo_in', 'v_wo_out', 'v_q_norm_g', 'v_k_norm_g', 'v_final_g']
TWIN_OUTPUTS = ['loss', 'grad_x', 'grad_c_ctx', 'grad_norm_g', 'grad_w_mod', 'grad_b_mod', 'grad_we_in', 'grad_we_out', 'grad_gm_v_g', 'grad_gm_w_s', 'grad_gm_b_s', 'grad_s5_lam_re', 'grad_s5_lam_im', 'grad_s5_log_dt', 'grad_s5_b_re', 'grad_s5_b_im', 'grad_s5_c_re', 'grad_s5_c_im', 'grad_s5_d', 'grad_s5_w_glu', 'grad_s5_b_glu', 'grad_wo_in', 'grad_wo_out', 'grad_q_norm_g', 'grad_k_norm_g', 'grad_final_g', 'delta_c_ctx', 'delta_norm_g', 'delta_w_mod', 'delta_b_mod', 'delta_we_in', 'delta_we_out', 'delta_gm_v_g', 'delta_gm_w_s', 'delta_gm_b_s', 'delta_s5_lam_re', 'delta_s5_lam_im', 'delta_s5_log_dt', 'delta_s5_b_re', 'delta_s5_b_im', 'delta_s5_c_re', 'delta_s5_c_im', 'delta_s5_d', 'delta_s5_w_glu', 'delta_s5_b_glu', 'delta_wo_in', 'delta_wo_out', 'delta_q_norm_g', 'delta_k_norm_g', 'delta_final_g', 'new_m_c_ctx', 'new_m_norm_g', 'new_m_w_mod', 'new_m_b_mod', 'new_m_we_in', 'new_m_we_out', 'new_m_gm_v_g', 'new_m_gm_w_s', 'new_m_gm_b_s', 'new_m_s5_lam_re', 'new_m_s5_lam_im', 'new_m_s5_log_dt', 'new_m_s5_b_re', 'new_m_s5_b_im', 'new_m_s5_c_re', 'new_m_s5_c_im', 'new_m_s5_d', 'new_m_s5_w_glu', 'new_m_s5_b_glu', 'new_m_wo_in', 'new_m_wo_out', 'new_m_q_norm_g', 'new_m_k_norm_g', 'new_m_final_g', 'new_v_c_ctx', 'new_v_norm_g', 'new_v_w_mod', 'new_v_b_mod', 'new_v_we_in', 'new_v_we_out', 'new_v_gm_v_g', 'new_v_gm_w_s', 'new_v_gm_b_s', 'new_v_s5_lam_re', 'new_v_s5_lam_im', 'new_v_s5_log_dt', 'new_v_s5_b_re', 'new_v_s5_b_im', 'new_v_s5_c_re', 'new_v_s5_c_im', 'new_v_s5_d', 'new_v_s5_w_glu', 'new_v_s5_b_glu', 'new_v_wo_in', 'new_v_wo_out', 'new_v_q_norm_g', 'new_v_k_norm_g', 'new_v_final_g']
TWIN_LEAF_KINDS = {'loss': 'loss', 'grad_x': 'grad_x', 'grad_c_ctx': 'grad_w', 'grad_norm_g': 'grad_w', 'grad_w_mod': 'grad_w', 'grad_b_mod': 'grad_w', 'grad_we_in': 'grad_w', 'grad_we_out': 'grad_w', 'grad_gm_v_g': 'grad_w', 'grad_gm_w_s': 'grad_w', 'grad_gm_b_s': 'grad_w', 'grad_s5_lam_re': 'grad_w', 'grad_s5_lam_im': 'grad_w', 'grad_s5_log_dt': 'grad_w', 'grad_s5_b_re': 'grad_w', 'grad_s5_b_im': 'grad_w', 'grad_s5_c_re': 'grad_w', 'grad_s5_c_im': 'grad_w', 'grad_s5_d': 'grad_w', 'grad_s5_w_glu': 'grad_w', 'grad_s5_b_glu': 'grad_w', 'grad_wo_in': 'grad_w', 'grad_wo_out': 'grad_w', 'grad_q_norm_g': 'grad_w', 'grad_k_norm_g': 'grad_w', 'grad_final_g': 'grad_w', 'delta_c_ctx': 'delta_w', 'delta_norm_g': 'delta_w', 'delta_w_mod': 'delta_w', 'delta_b_mod': 'delta_w', 'delta_we_in': 'delta_w', 'delta_we_out': 'delta_w', 'delta_gm_v_g': 'delta_w', 'delta_gm_w_s': 'delta_w', 'delta_gm_b_s': 'delta_w', 'delta_s5_lam_re': 'delta_w', 'delta_s5_lam_im': 'delta_w', 'delta_s5_log_dt': 'delta_w', 'delta_s5_b_re': 'delta_w', 'delta_s5_b_im': 'delta_w', 'delta_s5_c_re': 'delta_w', 'delta_s5_c_im': 'delta_w', 'delta_s5_d': 'delta_w', 'delta_s5_w_glu': 'delta_w', 'delta_s5_b_glu': 'delta_w', 'delta_wo_in': 'delta_w', 'delta_wo_out': 'delta_w', 'delta_q_norm_g': 'delta_w', 'delta_k_norm_g': 'delta_w', 'delta_final_g': 'delta_w', 'new_m_c_ctx': 'new_m', 'new_m_norm_g': 'new_m', 'new_m_w_mod': 'new_m', 'new_m_b_mod': 'new_m', 'new_m_we_in': 'new_m', 'new_m_we_out': 'new_m', 'new_m_gm_v_g': 'new_m', 'new_m_gm_w_s': 'new_m', 'new_m_gm_b_s': 'new_m', 'new_m_s5_lam_re': 'new_m', 'new_m_s5_lam_im': 'new_m', 'new_m_s5_log_dt': 'new_m', 'new_m_s5_b_re': 'new_m', 'new_m_s5_b_im': 'new_m', 'new_m_s5_c_re': 'new_m', 'new_m_s5_c_im': 'new_m', 'new_m_s5_d': 'new_m', 'new_m_s5_w_glu': 'new_m', 'new_m_s5_b_glu': 'new_m', 'new_m_wo_in': 'new_m', 'new_m_wo_out': 'new_m', 'new_m_q_norm_g': 'new_m', 'new_m_k_norm_g': 'new_m', 'new_m_final_g': 'new_m', 'new_v_c_ctx': 'new_v', 'new_v_norm_g': 'new_v', 'new_v_w_mod': 'new_v', 'new_v_b_mod': 'new_v', 'new_v_we_in': 'new_v', 'new_v_we_out': 'new_v', 'new_v_gm_v_g': 'new_v', 'new_v_gm_w_s': 'new_v', 'new_v_gm_b_s': 'new_v', 'new_v_s5_lam_re': 'new_v', 'new_v_s5_lam_im': 'new_v', 'new_v_s5_log_dt': 'new_v', 'new_v_s5_b_re': 'new_v', 'new_v_s5_b_im': 'new_v', 'new_v_s5_c_re': 'new_v', 'new_v_s5_c_im': 'new_v', 'new_v_s5_d': 'new_v', 'new_v_s5_w_glu': 'new_v', 'new_v_s5_b_glu': 'new_v', 'new_v_wo_in': 'new_v', 'new_v_wo_out': 'new_v', 'new_v_q_norm_g': 'new_v', 'new_v_k_norm_g': 'new_v', 'new_v_final_g': 'new_v'}


def _forward(args):
    return _fwd_reference(*[args[k] for k in FWD_PARAMS])


def _output_shape():
    def fwd():
        inp = _fwd_setup_inputs(0)
        return _fwd_reference(*[inp[k] for k in FWD_PARAMS])
    out = _jax.eval_shape(fwd)
    return out.shape, out.dtype

N_MICROBATCH = 1
ADAM_LR = 0.001
ADAM_B1 = 0.9
ADAM_B2 = 0.999
ADAM_EPS = 1e-08
ADAM_WD = 0.01
ADAM_STEP = 10
PER_EXAMPLE_BATCH_AXIS = {'x': 0, 'c': 0, 'ctx': 0, 'loss_target': 0}
SHARED_INPUTS = []
_WEIGHT_DTYPES = {'c_ctx': _jnp.float32, 'norm_g': _jnp.float32, 'w_mod': _jnp.float32, 'b_mod': _jnp.float32, 'we_in': _jnp.float32, 'we_out': _jnp.float32, 'gm_v_g': _jnp.float32, 'gm_w_s': _jnp.float32, 'gm_b_s': _jnp.float32, 's5_lam_re': _jnp.float32, 's5_lam_im': _jnp.float32, 's5_log_dt': _jnp.float32, 's5_b_re': _jnp.float32, 's5_b_im': _jnp.float32, 's5_c_re': _jnp.float32, 's5_c_im': _jnp.float32, 's5_d': _jnp.float32, 's5_w_glu': _jnp.float32, 's5_b_glu': _jnp.float32, 'wo_in': _jnp.float32, 'wo_out': _jnp.float32, 'q_norm_g': _jnp.float32, 'k_norm_g': _jnp.float32, 'final_g': _jnp.float32}
MOMENT_SCALE = {'c_ctx': 8.680267e-03, 'norm_g': 4.226264e-02, 'w_mod': 4.031224e-02, 'b_mod': 6.829932e-02, 'we_in': 3.878915e-02, 'we_out': 3.839155e-02, 'gm_v_g': 3.740634e-02, 'gm_w_s': 3.762845e-02, 'gm_b_s': 3.765640e-02, 's5_lam_re': 1.234847e-03, 's5_lam_im': 1.093741e-03, 's5_log_dt': 4.942214e-01, 's5_b_re': 7.275574e-04, 's5_b_im': 7.194155e-04, 's5_c_re': 1.026643e-03, 's5_c_im': 1.060764e-03, 's5_d': 1.436835e-02, 's5_w_glu': 4.247229e-03, 's5_b_glu': 6.014982e-03, 'wo_in': 8.722767e-03, 'wo_out': 1.004591e-02, 'q_norm_g': 8.853450e-03, 'k_norm_g': 8.789198e-03, 'final_g': 3.198219e+01}


def _to_microbatches(a, axis):
    t = _jnp.moveaxis(a, axis, 0)
    t = t.reshape((N_MICROBATCH, t.shape[0] // N_MICROBATCH) + t.shape[1:])
    return _jnp.moveaxis(t, 1, axis + 1)


def setup_inputs(seed: int = 0) -> dict:
    inp = _fwd_setup_inputs(seed)
    key = _jax.random.fold_in(_jax.random.key(seed), 7919)
    shape, _ = _output_shape()
    out = dict(inp)
    out["loss_target"] = _jax.random.normal(_jax.random.fold_in(key, 0), shape, _jnp.float32)
    for i, name in enumerate(TWIN_WEIGHTS):
        w = inp[name].astype(_jnp.float32)
        if MOMENT_SCALE is None:
            s = _jnp.sqrt(_jnp.mean(_jnp.square(w)) + 1e-30)
        else:
            s = MOMENT_SCALE[name]
        km, kv = _jax.random.split(_jax.random.fold_in(key, i + 1))
        out[name] = w
        out["m_" + name] = s * _jax.random.normal(km, w.shape, _jnp.float32)
        out["v_" + name] = (s * s) * _jax.random.uniform(kv, w.shape, _jnp.float32, 0.5, 1.5)
    if N_MICROBATCH > 1:
        for name, axis in PER_EXAMPLE_BATCH_AXIS.items():
            out[name] = _to_microbatches(out[name], axis)
    return {'x': out['x'], 'c': out['c'], 'ctx': out['ctx'], 'c_ctx': out['c_ctx'], 'norm_g': out['norm_g'], 'w_mod': out['w_mod'], 'b_mod': out['b_mod'], 'we_in': out['we_in'], 'we_out': out['we_out'], 'gm_v_g': out['gm_v_g'], 'gm_w_s': out['gm_w_s'], 'gm_b_s': out['gm_b_s'], 's5_lam_re': out['s5_lam_re'], 's5_lam_im': out['s5_lam_im'], 's5_log_dt': out['s5_log_dt'], 's5_b_re': out['s5_b_re'], 's5_b_im': out['s5_b_im'], 's5_c_re': out['s5_c_re'], 's5_c_im': out['s5_c_im'], 's5_d': out['s5_d'], 's5_w_glu': out['s5_w_glu'], 's5_b_glu': out['s5_b_glu'], 'wo_in': out['wo_in'], 'wo_out': out['wo_out'], 'q_norm_g': out['q_norm_g'], 'k_norm_g': out['k_norm_g'], 'final_g': out['final_g'], 'loss_target': out['loss_target'], 'm_c_ctx': out['m_c_ctx'], 'm_norm_g': out['m_norm_g'], 'm_w_mod': out['m_w_mod'], 'm_b_mod': out['m_b_mod'], 'm_we_in': out['m_we_in'], 'm_we_out': out['m_we_out'], 'm_gm_v_g': out['m_gm_v_g'], 'm_gm_w_s': out['m_gm_w_s'], 'm_gm_b_s': out['m_gm_b_s'], 'm_s5_lam_re': out['m_s5_lam_re'], 'm_s5_lam_im': out['m_s5_lam_im'], 'm_s5_log_dt': out['m_s5_log_dt'], 'm_s5_b_re': out['m_s5_b_re'], 'm_s5_b_im': out['m_s5_b_im'], 'm_s5_c_re': out['m_s5_c_re'], 'm_s5_c_im': out['m_s5_c_im'], 'm_s5_d': out['m_s5_d'], 'm_s5_w_glu': out['m_s5_w_glu'], 'm_s5_b_glu': out['m_s5_b_glu'], 'm_wo_in': out['m_wo_in'], 'm_wo_out': out['m_wo_out'], 'm_q_norm_g': out['m_q_norm_g'], 'm_k_norm_g': out['m_k_norm_g'], 'm_final_g': out['m_final_g'], 'v_c_ctx': out['v_c_ctx'], 'v_norm_g': out['v_norm_g'], 'v_w_mod': out['v_w_mod'], 'v_b_mod': out['v_b_mod'], 'v_we_in': out['v_we_in'], 'v_we_out': out['v_we_out'], 'v_gm_v_g': out['v_gm_v_g'], 'v_gm_w_s': out['v_gm_w_s'], 'v_gm_b_s': out['v_gm_b_s'], 'v_s5_lam_re': out['v_s5_lam_re'], 'v_s5_lam_im': out['v_s5_lam_im'], 'v_s5_log_dt': out['v_s5_log_dt'], 'v_s5_b_re': out['v_s5_b_re'], 'v_s5_b_im': out['v_s5_b_im'], 'v_s5_c_re': out['v_s5_c_re'], 'v_s5_c_im': out['v_s5_c_im'], 'v_s5_d': out['v_s5_d'], 'v_s5_w_glu': out['v_s5_w_glu'], 'v_s5_b_glu': out['v_s5_b_glu'], 'v_wo_in': out['v_wo_in'], 'v_wo_out': out['v_wo_out'], 'v_q_norm_g': out['v_q_norm_g'], 'v_k_norm_g': out['v_k_norm_g'], 'v_final_g': out['v_final_g']}


def _loss(weights, diff, rest, loss_target):
    with _jax.named_scope("forward"):
        args = {**rest, TWIN_DIFF_INPUT: diff, **{k: w.astype(_WEIGHT_DTYPES[k]) for k, w in weights.items()}}
        y = _forward(args)
    with _jax.named_scope("loss_head"):
        err = _jnp.square(y.astype(_jnp.float32) - loss_target)
        return 0.5 * _jnp.sum(_jnp.mean(err, axis=-1)) if err.ndim else 0.5 * err


def _adamw(w, g, m, v):
    m = ADAM_B1 * m + (1.0 - ADAM_B1) * g
    v = ADAM_B2 * v + (1.0 - ADAM_B2) * _jnp.square(g)
    m_hat = m / (1.0 - ADAM_B1 ** ADAM_STEP)
    v_hat = v / (1.0 - ADAM_B2 ** ADAM_STEP)
    delta = -ADAM_LR * (m_hat / (_jnp.sqrt(v_hat) + ADAM_EPS) + ADAM_WD * w)
    return delta, m, v


def reference(x, c, ctx, c_ctx, norm_g, w_mod, b_mod, we_in, we_out, gm_v_g, gm_w_s, gm_b_s, s5_lam_re, s5_lam_im, s5_log_dt, s5_b_re, s5_b_im, s5_c_re, s5_c_im, s5_d, s5_w_glu, s5_b_glu, wo_in, wo_out, q_norm_g, k_norm_g, final_g, loss_target, m_c_ctx, m_norm_g, m_w_mod, m_b_mod, m_we_in, m_we_out, m_gm_v_g, m_gm_w_s, m_gm_b_s, m_s5_lam_re, m_s5_lam_im, m_s5_log_dt, m_s5_b_re, m_s5_b_im, m_s5_c_re, m_s5_c_im, m_s5_d, m_s5_w_glu, m_s5_b_glu, m_wo_in, m_wo_out, m_q_norm_g, m_k_norm_g, m_final_g, v_c_ctx, v_norm_g, v_w_mod, v_b_mod, v_we_in, v_we_out, v_gm_v_g, v_gm_w_s, v_gm_b_s, v_s5_lam_re, v_s5_lam_im, v_s5_log_dt, v_s5_b_re, v_s5_b_im, v_s5_c_re, v_s5_c_im, v_s5_d, v_s5_w_glu, v_s5_b_glu, v_wo_in, v_wo_out, v_q_norm_g, v_k_norm_g, v_final_g):
    given = dict(x=x, c=c, ctx=ctx, c_ctx=c_ctx, norm_g=norm_g, w_mod=w_mod, b_mod=b_mod, we_in=we_in, we_out=we_out, gm_v_g=gm_v_g, gm_w_s=gm_w_s, gm_b_s=gm_b_s, s5_lam_re=s5_lam_re, s5_lam_im=s5_lam_im, s5_log_dt=s5_log_dt, s5_b_re=s5_b_re, s5_b_im=s5_b_im, s5_c_re=s5_c_re, s5_c_im=s5_c_im, s5_d=s5_d, s5_w_glu=s5_w_glu, s5_b_glu=s5_b_glu, wo_in=wo_in, wo_out=wo_out, q_norm_g=q_norm_g, k_norm_g=k_norm_g, final_g=final_g, loss_target=loss_target, m_c_ctx=m_c_ctx, m_norm_g=m_norm_g, m_w_mod=m_w_mod, m_b_mod=m_b_mod, m_we_in=m_we_in, m_we_out=m_we_out, m_gm_v_g=m_gm_v_g, m_gm_w_s=m_gm_w_s, m_gm_b_s=m_gm_b_s, m_s5_lam_re=m_s5_lam_re, m_s5_lam_im=m_s5_lam_im, m_s5_log_dt=m_s5_log_dt, m_s5_b_re=m_s5_b_re, m_s5_b_im=m_s5_b_im, m_s5_c_re=m_s5_c_re, m_s5_c_im=m_s5_c_im, m_s5_d=m_s5_d, m_s5_w_glu=m_s5_w_glu, m_s5_b_glu=m_s5_b_glu, m_wo_in=m_wo_in, m_wo_out=m_wo_out, m_q_norm_g=m_q_norm_g, m_k_norm_g=m_k_norm_g, m_final_g=m_final_g, v_c_ctx=v_c_ctx, v_norm_g=v_norm_g, v_w_mod=v_w_mod, v_b_mod=v_b_mod, v_we_in=v_we_in, v_we_out=v_we_out, v_gm_v_g=v_gm_v_g, v_gm_w_s=v_gm_w_s, v_gm_b_s=v_gm_b_s, v_s5_lam_re=v_s5_lam_re, v_s5_lam_im=v_s5_lam_im, v_s5_log_dt=v_s5_log_dt, v_s5_b_re=v_s5_b_re, v_s5_b_im=v_s5_b_im, v_s5_c_re=v_s5_c_re, v_s5_c_im=v_s5_c_im, v_s5_d=v_s5_d, v_s5_w_glu=v_s5_w_glu, v_s5_b_glu=v_s5_b_glu, v_wo_in=v_wo_in, v_wo_out=v_wo_out, v_q_norm_g=v_q_norm_g, v_k_norm_g=v_k_norm_g, v_final_g=v_final_g)
    weights = {n: given[n] for n in TWIN_WEIGHTS}
    shared = {n: given[n] for n in SHARED_INPUTS}
    per_example = {n: given[n] for n in ['x', 'c', 'ctx']}
    grad_fn = _jax.value_and_grad(_loss, argnums=(0, 1))

    def one_microbatch(ex, loss_target):
        ex = dict(ex)
        diff = ex.pop(TWIN_DIFF_INPUT)
        return grad_fn(weights, diff, {**shared, **ex}, loss_target)

    if N_MICROBATCH == 1:
        loss, (grad_w, grad_x) = one_microbatch(per_example, given["loss_target"])
    else:
        def body(carry, xs):
            loss_sum, grad_sum = carry
            l_k, (gw_k, gx_k) = one_microbatch(xs[0], xs[1])
            with _jax.named_scope("update"):
                return (loss_sum + l_k, _jax.tree.map(_jnp.add, grad_sum, gw_k)), gx_k

        init = (_jnp.zeros((), _jnp.float32), _jax.tree.map(_jnp.zeros_like, weights))
        (loss, grad_w), grad_x = _jax.lax.scan(body, init, (per_example, given["loss_target"]))
    with _jax.named_scope("update"):
        delta_w, new_m, new_v = {}, {}, {}
        for n in TWIN_WEIGHTS:
            delta_w[n], new_m[n], new_v[n] = _adamw(weights[n], grad_w[n], given["m_" + n], given["v_" + n])
    return (loss, grad_x, *[grad_w[n] for n in TWIN_WEIGHTS], *[delta_w[n] for n in TWIN_WEIGHTS],
            *[new_m[n] for n in TWIN_WEIGHTS], *[new_v[n] for n in TWIN_WEIGHTS])
```

```python
import functools
import math

import numpy as np
import jax
import jax.numpy as jnp
from jax import lax
from jax.experimental import pallas as pl
from jax.experimental.pallas import tpu as pltpu

F32 = jnp.float32
BF = jnp.bfloat16
MESH = pl.DeviceIdType.MESH

D = 1024
NC = 256
SEQ = 4096
GRID_W = 64
TM = 256
CHUNK = 128
EPS = 1e-6
HD = 128
NQ = 8
NKV = 2
ROPE_THETA = 10000.0
SG = 32
SP = 64
SH = 16
SW = SG * SP
GELU_K = math.sqrt(2.0 / math.pi)
GELU_C = 0.044715
VMEM_LIMIT_BYTES = 56 * 1024 * 1024

ADAM_LR = 0.001
ADAM_B1 = 0.9
ADAM_B2 = 0.999
ADAM_EPS = 1e-08
ADAM_WD = 0.01
ADAM_STEP = 10


def _call(body, *, name, out_shape, grid=None, in_specs=None, out_specs=None, scratch=()):
    kw = {}
    if grid is not None:
        kw["grid"] = grid
    if in_specs is not None:
        kw["in_specs"] = in_specs
    if out_specs is not None:
        kw["out_specs"] = out_specs
    return pl.pallas_call(
        body, name=name, out_shape=out_shape, scratch_shapes=list(scratch),
        compiler_params=pltpu.CompilerParams(vmem_limit_bytes=VMEM_LIMIT_BYTES), **kw)


def _dot(a, b, ca=1, cb=0):
    return lax.dot_general(a, b, (((ca,), (cb,)), ((), ())), preferred_element_type=F32)


def _sig(x):
    return 1.0 / (1.0 + jnp.exp(-x))


def _full(shape):
    n = len(shape)
    return pl.BlockSpec(shape, lambda *_: (0,) * n)


def mm_nn(a, w3, name, out_dtype=F32):
    t, k = a.shape
    j, _, nb = w3.shape

    def body(a_ref, w_ref, o_ref):
        o_ref[...] = _dot(a_ref[...], w_ref[0]).astype(o_ref.dtype)

    return _call(body, name=name, grid=(j, t // TM),
                 in_specs=[pl.BlockSpec((TM, k), lambda jj, i: (i, 0)),
                           pl.BlockSpec((1, k, nb), lambda jj, i: (jj, 0, 0))],
                 out_specs=pl.BlockSpec((TM, nb), lambda jj, i: (i, jj)),
                 out_shape=jax.ShapeDtypeStruct((t, j * nb), out_dtype))(a, w3)


def mm_nt(a, w3, name, out_dtype=F32):
    t, _ = a.shape
    j, k, nb = w3.shape

    def body(a_ref, w_ref, o_ref):
        acc = _dot(a_ref[:, 0:nb], w_ref[0], 1, 1)
        for jj in range(1, j):
            acc = acc + _dot(a_ref[:, jj * nb:(jj + 1) * nb], w_ref[jj], 1, 1)
        o_ref[...] = acc.astype(o_ref.dtype)

    return _call(body, name=name, grid=(t // TM,),
                 in_specs=[pl.BlockSpec((TM, j * nb), lambda i: (i, 0)), _full((j, k, nb))],
                 out_specs=pl.BlockSpec((TM, k), lambda i: (i, 0)),
                 out_shape=jax.ShapeDtypeStruct((t, k), out_dtype))(a, w3)


def mm_tn(a, b, j, name):
    t, m = a.shape
    nb = b.shape[1] // j

    def body(a_ref, b_ref, o_ref):
        @pl.when(pl.program_id(1) == 0)
        def _():
            o_ref[...] = jnp.zeros_like(o_ref)
        o_ref[0] += _dot(a_ref[...], b_ref[...], 0, 0)

    return _call(body, name=name, grid=(j, t // TM),
                 in_specs=[pl.BlockSpec((TM, m), lambda jj, i: (i, 0)),
                           pl.BlockSpec((TM, nb), lambda jj, i: (i, jj))],
                 out_specs=pl.BlockSpec((1, m, nb), lambda jj, i: (jj, 0, 0)),
                 out_shape=jax.ShapeDtypeStruct((j, m, nb), F32))(a, b)


def _mod_rows(mod_ref, i):
    ctx = i == 0
    sh = jnp.where(ctx, mod_ref[0, 0:1, :], mod_ref[1, 0:1, :])
    sc = jnp.where(ctx, mod_ref[0, 1:2, :], mod_ref[1, 1:2, :])
    gt = jnp.where(ctx, mod_ref[0, 2:3, :], mod_ref[1, 2:3, :])
    return sh, sc, gt


def pro_fwd(x, g, mod, name):
    t = x.shape[0]

    def body(x_ref, g_ref, mod_ref, h_ref):
        sh, sc, _ = _mod_rows(mod_ref, pl.program_id(0))
        xv = x_ref[...]
        r = lax.rsqrt(jnp.mean(xv * xv, axis=-1, keepdims=True) + EPS)
        h_ref[...] = ((xv * r) * g_ref[...] * (1.0 + sc) + sh).astype(BF)

    return _call(body, name=name, grid=(t // TM,),
                 in_specs=[pl.BlockSpec((TM, D), lambda i: (i, 0)), _full((1, D)), _full((2, 3, D))],
                 out_specs=pl.BlockSpec((TM, D), lambda i: (i, 0)),
                 out_shape=jax.ShapeDtypeStruct((t, D), BF))(x, g, mod)


def pro_bwd(x, dh, dxn, g, mod, name):
    t = x.shape[0]

    def body(x_ref, dh_ref, dxn_ref, g_ref, mod_ref, dx_ref, dmod_ref, dg_ref):
        i = pl.program_id(0)

        @pl.when(i == 0)
        def _():
            dmod_ref[...] = jnp.zeros_like(dmod_ref)
            dg_ref[...] = jnp.zeros_like(dg_ref)

        _, sc, _ = _mod_rows(mod_ref, i)
        xv = x_ref[...]
        gv = g_ref[...]
        r = lax.rsqrt(jnp.mean(xv * xv, axis=-1, keepdims=True) + EPS)
        xn = xv * r
        dh_v = dh_ref[...]
        e = dh_v * (1.0 + sc)
        dsh = jnp.sum(dh_v, axis=0, keepdims=True)
        dsc = jnp.sum(dh_v * xn * gv, axis=0, keepdims=True)
        dg_ref[...] += jnp.sum(e * xn, axis=0, keepdims=True)
        dxh = e * gv
        dx_ref[...] = dxn_ref[...] + r * (dxh - xn * jnp.mean(dxh * xn, axis=-1, keepdims=True))

        @pl.when(i == 0)
        def _():
            dmod_ref[0, 0:1, :] += dsh
            dmod_ref[0, 1:2, :] += dsc

        @pl.when(i > 0)
        def _():
            dmod_ref[1, 0:1, :] += dsh
            dmod_ref[1, 1:2, :] += dsc

    tile = pl.BlockSpec((TM, D), lambda i: (i, 0))
    return _call(body, name=name, grid=(t // TM,),
                 in_specs=[tile, tile, tile, _full((1, D)), _full((2, 3, D))],
                 out_specs=[tile, _full((2, 2, D)), _full((1, D))],
                 out_shape=[jax.ShapeDtypeStruct((t, D), F32), jax.ShapeDtypeStruct((2, 2, D), F32),
                            jax.ShapeDtypeStruct((1, D), F32)])(x, dh, dxn, g, mod)


def res_fwd(x, o, mod, update_ctx, name):
    t = x.shape[0]

    def body(x_ref, o_ref, mod_ref, y_ref):
        i = pl.program_id(0)
        _, _, gt = _mod_rows(mod_ref, i)
        upd = x_ref[...] + gt * o_ref[...]
        if update_ctx:
            y_ref[...] = upd
        else:
            y_ref[...] = jnp.where(i == 0, x_ref[...], upd)

    tile = pl.BlockSpec((TM, D), lambda i: (i, 0))
    return _call(body, name=name, grid=(t // TM,), in_specs=[tile, tile, _full((2, 3, D))],
                 out_specs=tile, out_shape=jax.ShapeDtypeStruct((t, D), F32))(x, o, mod)


def res_bwd(dxn, o, mod, update_ctx, name):
    t = dxn.shape[0]

    def body(dxn_ref, o_ref, mod_ref, do_ref, dgt_ref):
        i = pl.program_id(0)

        @pl.when(i == 0)
        def _():
            dgt_ref[...] = jnp.zeros_like(dgt_ref)

        _, _, gt = _mod_rows(mod_ref, i)
        dv = dxn_ref[...]
        do = gt * dv
        dgt = jnp.sum(dv * o_ref[...], axis=0, keepdims=True)
        if update_ctx:
            do_ref[...] = do.astype(BF)
        else:
            do_ref[...] = jnp.where(i == 0, jnp.zeros_like(do), do).astype(BF)

        if update_ctx:
            @pl.when(i == 0)
            def _():
                dgt_ref[0:1, :] += dgt

        @pl.when(i > 0)
        def _():
            dgt_ref[1:2, :] += dgt

    tile = pl.BlockSpec((TM, D), lambda i: (i, 0))
    return _call(body, name=name, grid=(t // TM,), in_specs=[tile, tile, _full((2, 3, D))],
                 out_specs=[tile, _full((2, D))],
                 out_shape=[jax.ShapeDtypeStruct((t, D), BF), jax.ShapeDtypeStruct((2, D), F32)])(dxn, o, mod)


def final_loss(x, target, g):
    t = x.shape[0]

    def body(x_ref, t_ref, g_ref, loss_ref, dx_ref, dg_ref):
        i = pl.program_id(0)

        @pl.when(i == 0)
        def _():
            loss_ref[...] = jnp.zeros_like(loss_ref)
            dg_ref[...] = jnp.zeros_like(dg_ref)
            dx_ref[...] = jnp.zeros_like(dx_ref)

        @pl.when(i > 0)
        def _():
            xv = x_ref[...]
            gv = g_ref[...]
            r = lax.rsqrt(jnp.mean(xv * xv, axis=-1, keepdims=True) + EPS)
            xn = xv * r
            err = xn * gv - t_ref[...]
            loss_ref[...] += (0.5 / D) * jnp.sum(jnp.sum(err * err, axis=1, keepdims=True), axis=0, keepdims=True)
            dy = err * (1.0 / D)
            dg_ref[...] += jnp.sum(dy * xn, axis=0, keepdims=True)
            dxh = dy * gv
            dx_ref[...] = r * (dxh - xn * jnp.mean(dxh * xn, axis=-1, keepdims=True))

    tile = pl.BlockSpec((TM, D), lambda i: (i, 0))
    return _call(body, name="final_loss", grid=(t // TM,),
                 in_specs=[tile, pl.BlockSpec((TM, D), lambda i: (jnp.maximum(i - 1, 0), 0)), _full((1, D))],
                 out_specs=[_full((1, 1)), tile, _full((1, D))],
                 out_shape=[jax.ShapeDtypeStruct((1, 1), F32), jax.ShapeDtypeStruct((t, D), F32),
                            jax.ShapeDtypeStruct((1, D), F32)])(x, target, g)


def ada_fwd(cond, w_mod, b_mod):
    nl, _, nw = w_mod.shape

    def body(c_ref, w_ref, b_ref, o_ref):
        cv = c_ref[...]
        s = (cv * _sig(cv)).astype(BF)
        o_ref[0] = _dot(s, w_ref[0].astype(BF)) + b_ref[0]

    return _call(body, name="ada_fwd", grid=(nl,),
                 in_specs=[_full((8, D)), pl.BlockSpec((1, D, nw), lambda l: (l, 0, 0)),
                           pl.BlockSpec((1, 1, nw), lambda l: (l, 0, 0))],
                 out_specs=pl.BlockSpec((1, 8, nw), lambda l: (l, 0, 0)),
                 out_shape=jax.ShapeDtypeStruct((nl, 8, nw), F32))(cond, w_mod, b_mod)


def ada_bwd(cond, dm, w_mod):
    nl, _, nw = w_mod.shape

    def body(c_ref, dm_ref, w_ref, gw_ref, dcc_ref, dc_ref):
        l = pl.program_id(0)

        @pl.when(l == 0)
        def _():
            dc_ref[...] = jnp.zeros_like(dc_ref)

        cv = c_ref[...]
        sg = _sig(cv)
        s = (cv * sg).astype(BF)
        dmv = dm_ref[0].astype(BF)
        gw_ref[0] = _dot(s, dmv, 0, 0)
        dc_ref[...] += _dot(dmv, w_ref[0].astype(BF), 1, 1)

        @pl.when(l == nl - 1)
        def _():
            dcond = dc_ref[...] * (sg * (1.0 + cv * (1.0 - sg)))
            dcc_ref[...] = jnp.sum(dcond[4:8], axis=0, keepdims=True)

    return _call(body, name="ada_bwd", grid=(nl,),
                 in_specs=[_full((8, D)), pl.BlockSpec((1, 8, nw), lambda l: (l, 0, 0)),
                           pl.BlockSpec((1, D, nw), lambda l: (l, 0, 0))],
                 out_specs=[pl.BlockSpec((1, D, nw), lambda l: (l, 0, 0)), _full((1, D))],
                 out_shape=[jax.ShapeDtypeStruct((nl, D, nw), F32), jax.ShapeDtypeStruct((1, D), F32)],
                 scratch=[pltpu.VMEM((8, D), F32)])(cond, dm, w_mod)


def add2(a, b, name):
    def body(a_ref, b_ref, o_ref):
        o_ref[...] = a_ref[...] + b_ref[...]

    return _call(body, name=name, out_shape=jax.ShapeDtypeStruct(a.shape, a.dtype))(a, b)


AW = 512
NGRP = 4


def _gelu(y):
    t = jnp.tanh(GELU_K * (y + GELU_C * y * y * y))
    return 0.5 * y * (1.0 + t), t


def _layer_norm_stats(v):
    mu = jnp.mean(v, axis=-1, keepdims=True)
    vc = v - mu
    rstd = lax.rsqrt(jnp.mean(vc * vc, axis=-1, keepdims=True) + EPS)
    return vc * rstd, rstd


def _spatial_mix(vn_ref, ws_ref, bs_ref, mixed_ref):
    for ch in range(TM // CHUNK):
        rows = slice(ch * CHUNK, (ch + 1) * CHUNK)
        for g in range(NGRP):
            cols = slice(g * CHUNK, (g + 1) * CHUNK)
            mixed_ref[rows, cols] = _dot(ws_ref[g], vn_ref[rows, cols]) + bs_ref[g]


def mix_fwd(p, yf, yb, vg, ws, bs, dsk, wglu, bglu, name):
    t = p.shape[0]

    def body(p_ref, yf_ref, yb_ref, vg_ref, ws_ref, bs_ref, d_ref, wg_ref, bg_ref, o_ref, vn_ref, mixed_ref):
        vhat, _ = _layer_norm_stats(p_ref[:, AW:2 * AW])
        vn_ref[...] = (vhat * vg_ref[...]).astype(BF)
        _spatial_mix(vn_ref, ws_ref, bs_ref, mixed_ref)
        ga = p_ref[:, 2 * AW:3 * AW]
        o_ref[:, 0:AW] = (p_ref[:, 0:AW] * mixed_ref[...] * (ga * _sig(ga))).astype(BF)
        y = yf_ref[...] + yb_ref[...] + d_ref[...] * p_ref[:, 3 * AW:4 * AW]
        y2, _ = _gelu(y)
        z = _dot(y2.astype(BF), wg_ref[...]) + bg_ref[...]
        gb = p_ref[:, 4 * AW:5 * AW]
        o_ref[:, AW:2 * AW] = (y2 * _sig(z) * (gb * _sig(gb))).astype(BF)

    tile = lambda w: pl.BlockSpec((TM, w), lambda i: (i, 0))
    return _call(body, name=name, grid=(t // TM,),
                 in_specs=[tile(5 * AW), tile(AW), tile(AW), _full((1, AW)), _full((NGRP, CHUNK, CHUNK)),
                           _full((NGRP, CHUNK, 1)), _full((1, AW)), _full((AW, AW)), _full((1, AW))],
                 out_specs=tile(2 * AW), out_shape=jax.ShapeDtypeStruct((t, 2 * AW), BF),
                 scratch=[pltpu.VMEM((TM, AW), BF), pltpu.VMEM((TM, AW), F32)])(p, yf, yb, vg, ws, bs, dsk, wglu, bglu)


def mix_bwd(p, yf, yb, dmix, vg, ws, bs, dsk, wglu, bglu, name):
    t = p.shape[0]

    def body(p_ref, yf_ref, yb_ref, dm_ref, vg_ref, ws_ref, bs_ref, d_ref, wg_ref, bg_ref,
             dpa_ref, dpgb_ref, dy_ref, dws_ref, dbs_ref, dvg_ref, dd_ref, dwg_ref, dbg_ref,
             vn_ref, mixed_ref, dmx_ref, dvn_ref):
        @pl.when(pl.program_id(0) == 0)
        def _():
            for r in (dws_ref, dbs_ref, dvg_ref, dd_ref, dwg_ref, dbg_ref):
                r[...] = jnp.zeros_like(r)

        vhat, rstd = _layer_norm_stats(p_ref[:, AW:2 * AW])
        vgv = vg_ref[...]
        vn_ref[...] = (vhat * vgv).astype(BF)
        _spatial_mix(vn_ref, ws_ref, bs_ref, mixed_ref)
        u = p_ref[:, 0:AW]
        ga = p_ref[:, 2 * AW:3 * AW]
        sga = _sig(ga)
        dya = dm_ref[:, 0:AW]
        mixed = mixed_ref[...]
        dpa_ref[:, 0:AW] = (dya * mixed * (ga * sga)).astype(BF)
        dpa_ref[:, 2 * AW:3 * AW] = (dya * u * mixed * (sga * (1.0 + ga * (1.0 - sga)))).astype(BF)
        dmx_ref[...] = dya * u * (ga * sga)
        for ch in range(TM // CHUNK):
            rows = slice(ch * CHUNK, (ch + 1) * CHUNK)
            for g in range(NGRP):
                cols = slice(g * CHUNK, (g + 1) * CHUNK)
                dmx = dmx_ref[rows, cols]
                dmxb = dmx.astype(BF)
                dws_ref[g] += _dot(dmxb, vn_ref[rows, cols], 1, 1)
                dbs_ref[g] += jnp.sum(dmx, axis=1, keepdims=True)
                dvn_ref[rows, cols] = _dot(ws_ref[g], dmxb, 0, 0)
        dvn = dvn_ref[...]
        dvg_ref[...] += jnp.sum(dvn * vhat, axis=0, keepdims=True)
        dvh = dvn * vgv
        dpa_ref[:, AW:2 * AW] = (rstd * (dvh - jnp.mean(dvh, axis=-1, keepdims=True)
                                         - vhat * jnp.mean(dvh * vhat, axis=-1, keepdims=True))).astype(BF)

        xs = p_ref[:, 3 * AW:4 * AW]
        y = yf_ref[...] + yb_ref[...] + d_ref[...] * xs
        y2, th = _gelu(y)
        y2b = y2.astype(BF)
        z = _dot(y2b, wg_ref[...]) + bg_ref[...]
        sz = _sig(z)
        gb = p_ref[:, 4 * AW:5 * AW]
        sgb = _sig(gb)
        dyb = dm_ref[:, AW:2 * AW]
        dpgb_ref[...] = (dyb * (y2 * sz) * (sgb * (1.0 + gb * (1.0 - sgb)))).astype(BF)
        dy3 = dyb * (gb * sgb)
        dz = dy3 * y2 * sz * (1.0 - sz)
        dzb = dz.astype(BF)
        dwg_ref[...] += _dot(y2b, dzb, 0, 0)
        dbg_ref[...] += jnp.sum(dz, axis=0, keepdims=True)
        dy2 = dy3 * sz + _dot(dzb, wg_ref[...], 1, 1)
        dgelu = 0.5 * (1.0 + th) + 0.5 * y * (1.0 - th * th) * GELU_K * (1.0 + 3.0 * GELU_C * y * y)
        dy = dy2 * dgelu
        dd_ref[...] += jnp.sum(dy * xs, axis=0, keepdims=True)
        dy_ref[...] = dy

    tile = lambda w: pl.BlockSpec((TM, w), lambda i: (i, 0))
    return _call(body, name=name, grid=(t // TM,),
                 in_specs=[tile(5 * AW), tile(AW), tile(AW), tile(2 * AW), _full((1, AW)), _full((NGRP, CHUNK, CHUNK)),
                           _full((NGRP, CHUNK, 1)), _full((1, AW)), _full((AW, AW)), _full((1, AW))],
                 out_specs=[tile(3 * AW), tile(AW), tile(AW), _full((NGRP, CHUNK, CHUNK)), _full((NGRP, CHUNK, 1)),
                            _full((1, AW)), _full((1, AW)), _full((AW, AW)), _full((1, AW))],
                 out_shape=[jax.ShapeDtypeStruct((t, 3 * AW), BF), jax.ShapeDtypeStruct((t, AW), BF),
                            jax.ShapeDtypeStruct((t, AW), F32), jax.ShapeDtypeStruct((NGRP, CHUNK, CHUNK), F32),
                            jax.ShapeDtypeStruct((NGRP, CHUNK, 1), F32), jax.ShapeDtypeStruct((1, AW), F32),
                            jax.ShapeDtypeStruct((1, AW), F32), jax.ShapeDtypeStruct((AW, AW), F32),
                            jax.ShapeDtypeStruct((1, AW), F32)],
                 scratch=[pltpu.VMEM((TM, AW), BF), pltpu.VMEM((TM, AW), F32), pltpu.VMEM((TM, AW), F32),
                          pltpu.VMEM((TM, AW), F32)])(p, yf, yb, dmix, vg, ws, bs, dsk, wglu, bglu)


LN = 512
NBLK = SW // LN
UB = AW // NBLK
PW_ROWS = 32
POW_EXP = [1, 2, 4, 8, 16, 32, 64, 128, 256, 0, 0, 0, 0, 0, 0, 0,
           1, 2, 3, 4, 5, 6, 7, 8, 8, 7, 6, 5, 4, 3, 2, 1]
GROUPS_PER_TILE = TM // 8


def s5_disc(lam_re, lam_im, dt, lam_re_r, lam_im_r, dt_r, b_re, b_im):
    nexp = jnp.asarray(np.array(POW_EXP, np.float32).reshape(PW_ROWS, 1))

    def body(n_ref, lr_ref, li_ref, dt_ref, lrr_ref, lir_ref, dtr_ref, br_ref, bi_ref,
             pr_ref, pi_ref, bbr_ref, bbi_ref):
        for dr in range(2):
            dtl = jnp.exp(dt_ref[dr:dr + 1, :])
            zr = lr_ref[dr:dr + 1, :] * dtl
            zi = li_ref[dr:dr + 1, :] * dtl
            mag = jnp.exp(n_ref[...] * zr)
            ang = n_ref[...] * zi
            pr_ref[dr] = mag * jnp.cos(ang)
            pi_ref[dr] = mag * jnp.sin(ang)
        lr, li, dtv = lrr_ref[...], lir_ref[...], jnp.exp(dtr_ref[...])
        mag = jnp.exp(lr * dtv)
        nr = mag * jnp.cos(li * dtv) - 1.0
        ni = mag * jnp.sin(li * dtv)
        den = lr * lr + li * li
        fr = (nr * lr + ni * li) / den
        fi = (ni * lr - nr * li) / den
        bbr_ref[...] = fr * br_ref[...] - fi * bi_ref[...]
        bbi_ref[...] = fr * bi_ref[...] + fi * br_ref[...]

    rows = lam_re_r.shape[0]
    return _call(body, name="s5_disc",
                 out_shape=[jax.ShapeDtypeStruct((2, PW_ROWS, SW), F32), jax.ShapeDtypeStruct((2, PW_ROWS, SW), F32),
                            jax.ShapeDtypeStruct((rows, SP), F32), jax.ShapeDtypeStruct((rows, SP), F32)])(
        nexp, lam_re, lam_im, dt, lam_re_r, lam_im_r, dt_r, b_re, b_im)


def s5_param_bwd(lam_re_r, lam_im_r, dt_r, b_re, b_im, da_re, da_im, dbb_re, dbb_im):
    rows = lam_re_r.shape[0]
    ng = rows // SH
    seg = jnp.asarray(np.kron(np.eye(ng, dtype=np.float32), np.ones((1, SH), np.float32)))

    def body(seg_ref, lr_ref, li_ref, dt_ref, br_ref, bi_ref, dar_ref, dai_ref, dbbr_ref, dbbi_ref,
             dlr_ref, dli_ref, ddt_ref, dbr_ref, dbi_ref):
        lr, li, dtv = lr_ref[...], li_ref[...], jnp.exp(dt_ref[...])
        mag = jnp.exp(lr * dtv)
        lbr = mag * jnp.cos(li * dtv)
        lbi = mag * jnp.sin(li * dtv)
        den = lr * lr + li * li
        nr, ni = lbr - 1.0, lbi
        fr = (nr * lr + ni * li) / den
        fi = (ni * lr - nr * li) / den
        br, bi = br_ref[...], bi_ref[...]
        gbr, gbi = dbbr_ref[...], dbbi_ref[...]
        dbr_ref[...] = gbr * fr + gbi * fi
        dbi_ref[...] = gbi * fr - gbr * fi
        gfr = gbr * br + gbi * bi
        gfi = gbi * br - gbr * bi
        ilr, ili = lr / den, -li / den
        gnr = gfr * ilr + gfi * ili
        gni = gfi * ilr - gfr * ili
        qr = -(fr * ilr - fi * ili)
        qi = -(fr * ili + fi * ilr)
        glr = gfr * qr + gfi * qi
        gli = gfi * qr - gfr * qi
        first = (lax.broadcasted_iota(jnp.int32, (rows, 1), 0) % SH) == 0
        glbr = gnr + jnp.where(first, dar_ref[...], 0.0)
        glbi = gni + jnp.where(first, dai_ref[...], 0.0)
        gzr = glbr * lbr + glbi * lbi
        gzi = glbi * lbr - glbr * lbi
        glr = glr + gzr * dtv
        gli = gli + gzi * dtv
        gdt = (gzr * lr + gzi * li) * dtv
        hi = lax.Precision.HIGHEST
        sg = seg_ref[...]
        dlr_ref[...] = jnp.dot(sg, glr, precision=hi, preferred_element_type=F32)
        dli_ref[...] = jnp.dot(sg, gli, precision=hi, preferred_element_type=F32)
        ddt_ref[...] = jnp.sum(jnp.dot(sg, gdt, precision=hi, preferred_element_type=F32), axis=1, keepdims=True)

    return _call(body, name="s5_param_bwd",
                 out_shape=[jax.ShapeDtypeStruct((ng, SP), F32), jax.ShapeDtypeStruct((ng, SP), F32),
                            jax.ShapeDtypeStruct((ng, 1), F32), jax.ShapeDtypeStruct((rows, SP), F32),
                            jax.ShapeDtypeStruct((rows, SP), F32)])(
        seg, lam_re_r, lam_im_r, dt_r, b_re, b_im, da_re, da_im, dbb_re, dbb_im)


def _scan_tile(hr_ref, hi_ref, er_ref, ei_ref, st_ref, cr_ref, ci_ref, pr_ref, pi_ref, reverse):
    row8 = lax.broadcasted_iota(jnp.int32, (TM, 1), 0) % 8
    rowe = lax.broadcasted_iota(jnp.int32, (2 * GROUPS_PER_TILE, 1), 0)
    ne = 2 * GROUPS_PER_TILE
    for blk in range(NBLK):
        cols = slice(blk * LN, (blk + 1) * LN)
        hr, hi = hr_ref[:, cols], hi_ref[:, cols]
        for k, s in enumerate((1, 2, 4)):
            ar, ai = pr_ref[k:k + 1, cols], pi_ref[k:k + 1, cols]
            if reverse:
                m = row8 < 8 - s
                sr, si = pltpu.roll(hr, TM - s, 0), pltpu.roll(hi, TM - s, 0)
            else:
                m = row8 >= s
                sr, si = pltpu.roll(hr, s, 0), pltpu.roll(hi, s, 0)
            sr, si = jnp.where(m, sr, 0.0), jnp.where(m, si, 0.0)
            hr, hi = hr + ar * sr - ai * si, hi + ar * si + ai * sr
        hr_ref[:, cols] = hr
        hi_ref[:, cols] = hi
        edge = 0 if reverse else 7
        nq = LN // 128
        for q in range(nq):
            st_ref[q] = hr[:, q * 128:(q + 1) * 128]
            st_ref[nq + q] = hi[:, q * 128:(q + 1) * 128]
        gr = jnp.concatenate([st_ref[q, pl.ds(edge, GROUPS_PER_TILE, stride=8), :] for q in range(nq)], axis=1)
        gi = jnp.concatenate([st_ref[nq + q, pl.ds(edge, GROUPS_PER_TILE, stride=8), :] for q in range(nq)], axis=1)
        zero = jnp.zeros((GROUPS_PER_TILE, LN), F32)
        if reverse:
            er_ref[0:GROUPS_PER_TILE, :] = gr
            ei_ref[0:GROUPS_PER_TILE, :] = gi
            er_ref[GROUPS_PER_TILE:ne, :] = zero
            ei_ref[GROUPS_PER_TILE:ne, :] = zero
            er_ref[GROUPS_PER_TILE:GROUPS_PER_TILE + 1, :] = cr_ref[:, cols]
            ei_ref[GROUPS_PER_TILE:GROUPS_PER_TILE + 1, :] = ci_ref[:, cols]
        else:
            er_ref[0:GROUPS_PER_TILE, :] = zero
            ei_ref[0:GROUPS_PER_TILE, :] = zero
            er_ref[GROUPS_PER_TILE - 1:GROUPS_PER_TILE, :] = cr_ref[:, cols]
            ei_ref[GROUPS_PER_TILE - 1:GROUPS_PER_TILE, :] = ci_ref[:, cols]
            er_ref[GROUPS_PER_TILE:ne, :] = gr
            ei_ref[GROUPS_PER_TILE:ne, :] = gi
        evr, evi = er_ref[...], ei_ref[...]
        s = 1
        k = 3
        while s < ne:
            ar, ai = pr_ref[k:k + 1, cols], pi_ref[k:k + 1, cols]
            if reverse:
                m = rowe < ne - s
                sr, si = pltpu.roll(evr, ne - s, 0), pltpu.roll(evi, ne - s, 0)
            else:
                m = rowe >= s
                sr, si = pltpu.roll(evr, s, 0), pltpu.roll(evi, s, 0)
            sr, si = jnp.where(m, sr, 0.0), jnp.where(m, si, 0.0)
            evr, evi = evr + ar * sr - ai * si, evi + ar * si + ai * sr
            s *= 2
            k += 1
        er_ref[...] = evr
        ei_ref[...] = evi
        if reverse:
            cr_ref[:, cols] = er_ref[0:1, :]
            ci_ref[:, cols] = ei_ref[0:1, :]
            apr, api = pr_ref[24:32, cols], pi_ref[24:32, cols]
        else:
            cr_ref[:, cols] = er_ref[ne - 1:ne, :]
            ci_ref[:, cols] = ei_ref[ne - 1:ne, :]
            apr, api = pr_ref[16:24, cols], pi_ref[16:24, cols]
        for g in range(GROUPS_PER_TILE):
            e_row = g + 1 if reverse else GROUPS_PER_TILE - 1 + g
            kr, ki = er_ref[e_row:e_row + 1, :], ei_ref[e_row:e_row + 1, :]
            rows = slice(8 * g, 8 * g + 8)
            hr_ref[rows, cols] = hr_ref[rows, cols] + apr * kr - api * ki
            hi_ref[rows, cols] = hi_ref[rows, cols] + apr * ki + api * kr


def _tile_order(kind, nt):
    if kind == "fwd":
        return lambda i: i
    if kind == "bwd":
        return lambda i: jnp.where(i == 0, 0, nt - i)
    if kind == "fwd_adj":
        return lambda i: nt - 1 - i
    if kind == "bwd_adj":
        return lambda i: jnp.where(i == nt - 1, 0, i + 1)
    raise ValueError(kind)


def s5_fwd(p, bb_re, bb_im, ct_re, ct_im, pw_re, pw_im, reverse, name):
    t = p.shape[0]
    nt = t // TM
    order = _tile_order("bwd" if reverse else "fwd", nt)

    def body(x_ref, bbr_ref, bbi_ref, ctr_ref, cti_ref, pr_ref, pi_ref, y_ref, hpr_ref, hpi_ref,
             hr_ref, hi_ref, er_ref, ei_ref, st_ref, cr_ref, ci_ref, c0r_ref, c0i_ref):
        @pl.when(pl.program_id(0) == 0)
        def _():
            cr_ref[...] = jnp.zeros_like(cr_ref)
            ci_ref[...] = jnp.zeros_like(ci_ref)

        c0r_ref[...] = cr_ref[...]
        c0i_ref[...] = ci_ref[...]
        xb = x_ref[...].astype(BF)
        for j in range(NBLK):
            cols = slice(j * LN, (j + 1) * LN)
            hr_ref[:, cols] = _dot(xb[:, j * UB:(j + 1) * UB], bbr_ref[j])
            hi_ref[:, cols] = _dot(xb[:, j * UB:(j + 1) * UB], bbi_ref[j])
        _scan_tile(hr_ref, hi_ref, er_ref, ei_ref, st_ref, cr_ref, ci_ref, pr_ref, pi_ref, reverse)
        rowi = lax.broadcasted_iota(jnp.int32, (TM, 1), 0)
        for j in range(NBLK):
            cols = slice(j * LN, (j + 1) * LN)
            hr, hi = hr_ref[:, cols], hi_ref[:, cols]
            y_ref[:, j * UB:(j + 1) * UB] = _dot(hr.astype(BF), ctr_ref[j]) - _dot(hi.astype(BF), cti_ref[j])
            if reverse:
                first = rowi == TM - 1
                sr, si = pltpu.roll(hr, TM - 1, 0), pltpu.roll(hi, TM - 1, 0)
            else:
                first = rowi == 0
                sr, si = pltpu.roll(hr, 1, 0), pltpu.roll(hi, 1, 0)
            hpr_ref[:, cols] = jnp.where(first, c0r_ref[:, cols], sr)
            hpi_ref[:, cols] = jnp.where(first, c0i_ref[:, cols], si)

    state = lambda: pl.BlockSpec((TM, SW), lambda i: (order(i), 0))
    return _call(body, name=name, grid=(nt,),
                 in_specs=[pl.BlockSpec((TM, AW), lambda i: (order(i), 3)),
                           _full((NBLK, UB, LN)), _full((NBLK, UB, LN)), _full((NBLK, LN, UB)), _full((NBLK, LN, UB)),
                           _full((PW_ROWS, SW)), _full((PW_ROWS, SW))],
                 out_specs=[pl.BlockSpec((TM, AW), lambda i: (order(i), 0)), state(), state()],
                 out_shape=[jax.ShapeDtypeStruct((t, AW), F32), jax.ShapeDtypeStruct((t, SW), F32),
                            jax.ShapeDtypeStruct((t, SW), F32)],
                 scratch=[pltpu.VMEM((TM, SW), F32), pltpu.VMEM((TM, SW), F32),
                          pltpu.VMEM((2 * GROUPS_PER_TILE, LN), F32), pltpu.VMEM((2 * GROUPS_PER_TILE, LN), F32),
                          pltpu.VMEM((2 * LN // 128, TM, 128), F32),
                          pltpu.VMEM((1, SW), F32), pltpu.VMEM((1, SW), F32),
                          pltpu.VMEM((1, SW), F32), pltpu.VMEM((1, SW), F32)])(
        p, bb_re, bb_im, ct_re, ct_im, pw_re, pw_im)


def s5_bwd(p, hp_re, hp_im, dy, bb_re, bb_im, ct_re, ct_im, pw_re, pw_im_conj, a_re, a_im, reverse, name):
    t = p.shape[0]
    nt = t // TM
    order = _tile_order("bwd_adj" if reverse else "fwd_adj", nt)

    def body(x_ref, hpr_ref, hpi_ref, dy_ref, bbr_ref, bbi_ref, ctr_ref, cti_ref, pr_ref, pi_ref, ar_ref, ai_ref,
             dx_ref, dar_ref, dai_ref, dbbr_ref, dbbi_ref, dcr_ref, dci_ref,
             gr_ref, gi_ref, er_ref, ei_ref, st_ref, cr_ref, ci_ref):
        @pl.when(pl.program_id(0) == 0)
        def _():
            for r in (cr_ref, ci_ref, dar_ref, dai_ref, dbbr_ref, dbbi_ref, dcr_ref, dci_ref):
                r[...] = jnp.zeros_like(r)

        xb = x_ref[...].astype(BF)
        dyb = dy_ref[...].astype(BF)
        for j in range(NBLK):
            cols = slice(j * LN, (j + 1) * LN)
            gr_ref[:, cols] = _dot(dyb[:, j * UB:(j + 1) * UB], ctr_ref[j], 1, 1)
            gi_ref[:, cols] = -_dot(dyb[:, j * UB:(j + 1) * UB], cti_ref[j], 1, 1)
        _scan_tile(gr_ref, gi_ref, er_ref, ei_ref, st_ref, cr_ref, ci_ref, pr_ref, pi_ref, not reverse)
        for j in range(NBLK):
            cols = slice(j * LN, (j + 1) * LN)
            xj = xb[:, j * UB:(j + 1) * UB]
            dyj = dyb[:, j * UB:(j + 1) * UB]
            hpr, hpi = hpr_ref[:, cols], hpi_ref[:, cols]
            gr, gi = gr_ref[:, cols], gi_ref[:, cols]
            ar, ai = ar_ref[:, cols], ai_ref[:, cols]
            hr = ar * hpr - ai * hpi + _dot(xj, bbr_ref[j])
            hi = ar * hpi + ai * hpr + _dot(xj, bbi_ref[j])
            dar_ref[:, cols] += jnp.sum(gr * hpr + gi * hpi, axis=0, keepdims=True)
            dai_ref[:, cols] += jnp.sum(gi * hpr - gr * hpi, axis=0, keepdims=True)
            grb, gib = gr.astype(BF), gi.astype(BF)
            dcr_ref[j] += _dot(dyj, hr.astype(BF), 0, 0)
            dci_ref[j] += -_dot(dyj, hi.astype(BF), 0, 0)
            dbbr_ref[j] += _dot(xj, grb, 0, 0)
            dbbi_ref[j] += _dot(xj, gib, 0, 0)
            dx_ref[:, j * UB:(j + 1) * UB] = _dot(grb, bbr_ref[j], 1, 1) + _dot(gib, bbi_ref[j], 1, 1)

    state = lambda: pl.BlockSpec((TM, SW), lambda i: (order(i), 0))
    blockd = lambda: _full((NBLK, UB, LN))
    return _call(body, name=name, grid=(nt,),
                 in_specs=[pl.BlockSpec((TM, AW), lambda i: (order(i), 3)), state(), state(),
                           pl.BlockSpec((TM, AW), lambda i: (order(i), 0)),
                           blockd(), blockd(), _full((NBLK, LN, UB)), _full((NBLK, LN, UB)),
                           _full((PW_ROWS, SW)), _full((PW_ROWS, SW)), _full((1, SW)), _full((1, SW))],
                 out_specs=[pl.BlockSpec((TM, AW), lambda i: (order(i), 0)), _full((1, SW)), _full((1, SW)),
                            blockd(), blockd(), blockd(), blockd()],
                 out_shape=[jax.ShapeDtypeStruct((t, AW), F32), jax.ShapeDtypeStruct((1, SW), F32),
                            jax.ShapeDtypeStruct((1, SW), F32)] + [jax.ShapeDtypeStruct((NBLK, UB, LN), F32)] * 4,
                 scratch=[pltpu.VMEM((TM, SW), F32), pltpu.VMEM((TM, SW), F32),
                          pltpu.VMEM((2 * GROUPS_PER_TILE, LN), F32), pltpu.VMEM((2 * GROUPS_PER_TILE, LN), F32),
                          pltpu.VMEM((2 * LN // 128, TM, 128), F32),
                          pltpu.VMEM((1, SW), F32), pltpu.VMEM((1, SW), F32)])(
        p, hp_re, hp_im, dy, bb_re, bb_im, ct_re, ct_im, pw_re, pw_im_conj, a_re, a_im)


def s5_dx_sum(dy, dsk, dxf, dxb, name):
    t = dy.shape[0]

    def body(dy_ref, d_ref, f_ref, b_ref, o_ref):
        o_ref[...] = (dy_ref[...] * d_ref[...] + f_ref[...] + b_ref[...]).astype(BF)

    tile = pl.BlockSpec((TM, AW), lambda i: (i, 0))
    return _call(body, name=name, grid=(t // TM,), in_specs=[tile, _full((1, AW)), tile, tile], out_specs=tile,
                 out_shape=jax.ShapeDtypeStruct((t, AW), BF))(dy, dsk, dxf, dxb)


SCALE = HD ** -0.5
NHEAD_NORM = NQ + NKV


def _partner(x):
    half0 = (lax.broadcasted_iota(jnp.int32, (1, HD), 1) % 64) < 32
    return jnp.where(half0, pltpu.roll(x, HD - 32, 1), pltpu.roll(x, 32, 1))


def attn_prep(p, qg, kg, cos, sins, name):
    t = p.shape[0]

    def body(p_ref, qg_ref, kg_ref, cos_ref, sin_ref, o_ref):
        cv, sv = cos_ref[...], sin_ref[...]
        for h in range(NHEAD_NORM):
            cols = slice(h * HD, (h + 1) * HD)
            blk = p_ref[:, cols]
            r = lax.rsqrt(jnp.mean(blk * blk, axis=-1, keepdims=True) + EPS)
            xn = blk * r * (qg_ref[...] if h < NQ else kg_ref[...])
            rot = xn * cv + _partner(xn) * sv
            o_ref[:, cols] = ((rot * SCALE) if h < NQ else rot).astype(BF)
        vcols = slice(NHEAD_NORM * HD, (NHEAD_NORM + NKV) * HD)
        o_ref[:, vcols] = p_ref[:, vcols].astype(BF)

    w = (NHEAD_NORM + NKV) * HD
    tile = lambda ww: pl.BlockSpec((TM, ww), lambda i: (i, 0))
    return _call(body, name=name, grid=(t // TM,),
                 in_specs=[tile(w), _full((1, HD)), _full((1, HD)), tile(HD), tile(HD)],
                 out_specs=tile(w), out_shape=jax.ShapeDtypeStruct((t, w), BF))(p, qg, kg, cos, sins)


def attn_prep_bwd(p, dq, dk, dv, qg, kg, cos, sins, name):
    t = p.shape[0]

    def body(p_ref, dq_ref, dk_ref, dv_ref, qg_ref, kg_ref, cos_ref, sin_ref, o_ref, dqg_ref, dkg_ref):
        @pl.when(pl.program_id(0) == 0)
        def _():
            dqg_ref[...] = jnp.zeros_like(dqg_ref)
            dkg_ref[...] = jnp.zeros_like(dkg_ref)

        cv, sv = cos_ref[...], sin_ref[...]
        for h in range(NHEAD_NORM):
            cols = slice(h * HD, (h + 1) * HD)
            blk = p_ref[:, cols]
            r = lax.rsqrt(jnp.mean(blk * blk, axis=-1, keepdims=True) + EPS)
            xh = blk * r
            if h < NQ:
                drot = dq_ref[:, cols] * SCALE
                gv, dg_ref = qg_ref[...], dqg_ref
            else:
                drot = dk_ref[:, (h - NQ) * HD:(h - NQ + 1) * HD]
                gv, dg_ref = kg_ref[...], dkg_ref
            dxn = drot * cv + _partner(drot * sv)
            dg_ref[...] += jnp.sum(dxn * xh, axis=0, keepdims=True)
            dxh = dxn * gv
            o_ref[:, cols] = (r * (dxh - xh * jnp.mean(dxh * xh, axis=-1, keepdims=True))).astype(BF)
        o_ref[:, NHEAD_NORM * HD:(NHEAD_NORM + NKV) * HD] = dv_ref[...].astype(BF)

    w = (NHEAD_NORM + NKV) * HD
    tile = lambda ww: pl.BlockSpec((TM, ww), lambda i: (i, 0))
    return _call(body, name=name, grid=(t // TM,),
                 in_specs=[tile(w), tile(NQ * HD), tile(NKV * HD), tile(NKV * HD), _full((1, HD)), _full((1, HD)),
                           tile(HD), tile(HD)],
                 out_specs=[tile(w), _full((1, HD)), _full((1, HD))],
                 out_shape=[jax.ShapeDtypeStruct((t, w), BF), jax.ShapeDtypeStruct((1, HD), F32),
                            jax.ShapeDtypeStruct((1, HD), F32)])(p, dq, dk, dv, qg, kg, cos, sins)


KCOL = NQ
VCOL = NQ + NKV
GCOL = (NQ + 2 * NKV)
QPK = NQ // NKV


def attn_fwd(qkv, p, name):
    t = qkv.shape[0]

    def body(q_ref, k_ref, v_ref, g_ref, o_ref, mix_ref, lse_ref):
        def attend(nk):
            s = _dot(q_ref[...], k_ref[0:nk, :], 1, 1)
            m = jnp.max(s, axis=-1, keepdims=True)
            pe = jnp.exp(s - m)
            l = jnp.sum(pe, axis=-1, keepdims=True)
            o = _dot(pe.astype(BF), v_ref[0:nk, :]) / l
            gt = g_ref[...]
            o_ref[...] = o
            mix_ref[...] = (o * (gt * _sig(gt))).astype(BF)
            lse_ref[...] = jnp.broadcast_to(m + jnp.log(l), (TM, HD))

        pl.when(pl.program_id(1) == 0)(lambda: attend(NC))
        pl.when(pl.program_id(1) > 0)(lambda: attend(t))

    blk = pl.BlockSpec((TM, HD), lambda h, i: (i, h))
    return _call(body, name=name, grid=(NQ, t // TM),
                 in_specs=[blk, pl.BlockSpec((t, HD), lambda h, i: (0, KCOL + h // QPK)),
                           pl.BlockSpec((t, HD), lambda h, i: (0, VCOL + h // QPK)),
                           pl.BlockSpec((TM, HD), lambda h, i: (i, GCOL + h))],
                 out_specs=[blk, blk, blk],
                 out_shape=[jax.ShapeDtypeStruct((t, NQ * HD), F32), jax.ShapeDtypeStruct((t, NQ * HD), BF),
                            jax.ShapeDtypeStruct((t, NQ * HD), F32)])(qkv, qkv, qkv, p)


def attn_bwd(qkv, p, dmix, o, lse, name):
    t = qkv.shape[0]

    def body(q_ref, k_ref, v_ref, g_ref, dm_ref, o_ref, lse_ref, dq_ref, dg_ref, dk_ref, dv_ref):
        i = pl.program_id(2)

        @pl.when((pl.program_id(1) == 0) & (i == 0))
        def _():
            dk_ref[...] = jnp.zeros_like(dk_ref)
            dv_ref[...] = jnp.zeros_like(dv_ref)

        gt = g_ref[...]
        sg = _sig(gt)
        ov = o_ref[...]
        dmv = dm_ref[...]
        dg_ref[...] = (dmv * ov * (sg * (1.0 + gt * (1.0 - sg)))).astype(BF)
        do = dmv * (gt * sg)
        dr = jnp.sum(do * ov, axis=-1, keepdims=True)
        dob = do.astype(BF)

        def bwd(nk):
            q = q_ref[...]
            s = _dot(q, k_ref[0:nk, :], 1, 1)
            pe = jnp.exp(s - lse_ref[:, 0:1])
            dp = _dot(dob, v_ref[0:nk, :], 1, 1)
            dsb = (pe * (dp - dr)).astype(BF)
            dq_ref[...] = _dot(dsb, k_ref[0:nk, :])
            dv_ref[0:nk, :] += _dot(pe.astype(BF), dob, 0, 0)
            dk_ref[0:nk, :] += _dot(dsb, q, 0, 0)

        pl.when(i == 0)(lambda: bwd(NC))
        pl.when(i > 0)(lambda: bwd(t))

    blk = pl.BlockSpec((TM, HD), lambda kv, g, i: (i, kv * QPK + g))
    acc = pl.BlockSpec((t, HD), lambda kv, g, i: (0, kv))
    return _call(body, name=name, grid=(NKV, QPK, t // TM),
                 in_specs=[blk, pl.BlockSpec((t, HD), lambda kv, g, i: (0, KCOL + kv)),
                           pl.BlockSpec((t, HD), lambda kv, g, i: (0, VCOL + kv)),
                           pl.BlockSpec((TM, HD), lambda kv, g, i: (i, GCOL + kv * QPK + g)), blk, blk, blk],
                 out_specs=[blk, blk, acc, acc],
                 out_shape=[jax.ShapeDtypeStruct((t, NQ * HD), F32), jax.ShapeDtypeStruct((t, NQ * HD), BF),
                            jax.ShapeDtypeStruct((t, NKV * HD), F32), jax.ShapeDtypeStruct((t, NKV * HD), F32)])(
        qkv, qkv, qkv, p, dmix, o, lse)


def _row_tile(rows, row_bytes, cap=2 * 1024 * 1024):
    if rows * row_bytes <= cap or rows % 8:
        return rows
    tr = rows
    while tr * row_bytes > cap and tr % 16 == 0:
        tr //= 2
    return tr


def adamw(w, g, m, v, name):
    r, cdim = w.shape
    tr = _row_tile(r, 4 * max(cdim, 128))

    def body(w_ref, g_ref, m_ref, v_ref, d_ref, nm_ref, nv_ref):
        gv = g_ref[...]
        m2 = ADAM_B1 * m_ref[...] + (1.0 - ADAM_B1) * gv
        v2 = ADAM_B2 * v_ref[...] + (1.0 - ADAM_B2) * (gv * gv)
        mh = m2 / (1.0 - ADAM_B1 ** ADAM_STEP)
        vh = v2 / (1.0 - ADAM_B2 ** ADAM_STEP)
        d_ref[...] = -ADAM_LR * (mh / (jnp.sqrt(vh) + ADAM_EPS) + ADAM_WD * w_ref[...])
        nm_ref[...] = m2
        nv_ref[...] = v2

    tile = pl.BlockSpec((tr, cdim), lambda i: (i, 0))
    sh = jax.ShapeDtypeStruct((r, cdim), F32)
    return _call(body, name=name, grid=(r // tr,), in_specs=[tile] * 4, out_specs=[tile] * 3,
                 out_shape=[sh, sh, sh])(w, g, m, v)


def sum_lead(a, name):
    n, r, cdim = a.shape
    tr = _row_tile(r, 4 * n * max(cdim, 128))

    def body(a_ref, o_ref):
        acc = a_ref[0]
        for k in range(1, n):
            acc = acc + a_ref[k]
        o_ref[...] = acc

    return _call(body, name=name, grid=(r // tr,),
                 in_specs=[pl.BlockSpec((n, tr, cdim), lambda i: (0, i, 0))],
                 out_specs=pl.BlockSpec((tr, cdim), lambda i: (i, 0)),
                 out_shape=jax.ShapeDtypeStruct((r, cdim), F32))(a)


_FLIPS = {"xy": [(1, 0, 0), (0, 1, 0), (1, 1, 0)], "c": [(0, 0, 1)],
          "all": [(0, 0, 1), (0, 1, 0), (0, 1, 1), (1, 0, 0), (1, 0, 1), (1, 1, 0), (1, 1, 1)]}
_GROUP_SIZE = {"xy": 4, "c": 2, "all": 8}


def _group_index(group, x, y, c):
    return {"xy": 2 * x + y, "c": c, "all": 4 * x + 2 * y + c}[group]


def exchange(items, name):
    plan = []
    for arr, group, kind in items:
        chunk = arr.shape if kind == "gather" else arr.shape[1:]
        plan.append((group, kind, chunk))
    ncopy = sum(len(_FLIPS[g]) for g, _, _ in plan)
    nitem = len(plan)

    def body(*refs):
        srcs, dsts = refs[:nitem], refs[nitem:2 * nitem]
        send_sems, recv_sems, local_sems = refs[2 * nitem:]
        x, y, c = lax.axis_index("x"), lax.axis_index("y"), lax.axis_index("c")
        sends, recvs, locals_ = [], [], []
        n = 0
        for k, (group, kind, _) in enumerate(plan):
            me = _group_index(group, x, y, c)
            own = srcs[k] if kind == "gather" else srcs[k].at[me]
            locals_.append(pltpu.make_async_copy(own, dsts[k].at[me], local_sems.at[k]))
            for fx, fy, fc in _FLIPS[group]:
                px, py, pc = (1 - x if fx else x), (1 - y if fy else y), (1 - c if fc else c)
                peer = _group_index(group, px, py, pc)
                src = srcs[k] if kind == "gather" else srcs[k].at[peer]
                sends.append(pltpu.make_async_remote_copy(
                    src_ref=src, dst_ref=dsts[k].at[me], send_sem=send_sems.at[n], recv_sem=recv_sems.at[n],
                    device_id=(px, py, pc), device_id_type=MESH))
                recvs.append(pltpu.make_async_remote_copy(
                    src_ref=src, dst_ref=dsts[k].at[peer], send_sem=send_sems.at[n], recv_sem=recv_sems.at[n],
                    device_id=(px, py, pc), device_id_type=MESH))
                n += 1
        for cp in locals_ + sends:
            cp.start()
        for cp in recvs:
            cp.wait_recv()
        for cp in sends:
            cp.wait_send()
        for cp in locals_:
            cp.wait()

    anyspec = pl.BlockSpec(memory_space=pl.ANY)
    outs = [jax.ShapeDtypeStruct((_GROUP_SIZE[g],) + tuple(chunk), arr.dtype)
            for (arr, _, _), (g, _, chunk) in zip(items, plan)]
    res = pl.pallas_call(
        body, name=name, out_shape=outs, in_specs=[anyspec] * nitem, out_specs=[anyspec] * nitem,
        scratch_shapes=[pltpu.SemaphoreType.DMA((ncopy,)), pltpu.SemaphoreType.DMA((ncopy,)),
                        pltpu.SemaphoreType.DMA((nitem,))],
        compiler_params=pltpu.CompilerParams(has_side_effects=True))(*[a for a, _, _ in items])
    return list(res)


_SMALL = ["c_ctx", "norm_g", "b_mod", "gm_v_g", "gm_w_s", "gm_b_s", "s5_lam_re", "s5_lam_im", "s5_log_dt",
          "s5_b_re", "s5_b_im", "s5_c_re", "s5_c_im", "s5_d", "s5_b_glu", "q_norm_g", "k_norm_g", "final_g"]
_BIG = ["we_in", "we_out", "s5_w_glu", "wo_in", "wo_out"]
_WEIGHTS = ["c_ctx", "norm_g", "w_mod", "b_mod", "we_in", "we_out", "gm_v_g", "gm_w_s", "gm_b_s", "s5_lam_re",
            "s5_lam_im", "s5_log_dt", "s5_b_re", "s5_b_im", "s5_c_re", "s5_c_im", "s5_d", "s5_w_glu", "s5_b_glu",
            "wo_in", "wo_out", "q_norm_g", "k_norm_g", "final_g"]
_SMALL_ALIGN = 8 * 8 * 128


def _rope_tables(n_lat):
    rows = n_lat // GRID_W
    row = jnp.repeat(jnp.arange(rows), GRID_W)
    col = jnp.tile(jnp.arange(GRID_W), rows)
    freqs = ROPE_THETA ** (-jnp.arange(HD // 4, dtype=F32) / (HD // 4))
    ar, ac = row[:, None] * freqs, col[:, None] * freqs
    cos = jnp.concatenate([jnp.cos(ar), jnp.cos(ar), jnp.cos(ac), jnp.cos(ac)], axis=1)
    sins = jnp.concatenate([-jnp.sin(ar), jnp.sin(ar), -jnp.sin(ac), jnp.sin(ac)], axis=1)
    cos = jnp.concatenate([jnp.ones((NC, HD), F32), cos], axis=0)
    sins = jnp.concatenate([jnp.zeros((NC, HD), F32), sins], axis=0)
    return cos, sins


def _block_diag(v, transpose):
    gpb = SG // NBLK
    v = v.reshape(2, NBLK, gpb, SH, SP)
    eye = jnp.eye(gpb, dtype=v.dtype)
    if transpose:
        return jnp.einsum("djahp,ab->djapbh", v, eye).reshape(2, NBLK, LN, UB)
    return jnp.einsum("djahp,ab->djahbp", v, eye).reshape(2, NBLK, UB, LN)


def _diag_blocks(m):
    gpb = SG // NBLK
    return jnp.einsum("jahap->jahp", m.reshape(NBLK, gpb, SH, gpb, SP)).reshape(SG, SH, SP)


def _view2d(a):
    if a.ndim == 1:
        return a.reshape(1, -1)
    if a.shape[-1] < 64 and a.size % 1024 == 0:
        return a.reshape(-1, 1024)
    return a.reshape(-1, a.shape[-1])


def kernel(x, c, ctx, c_ctx, norm_g, w_mod, b_mod, we_in, we_out, gm_v_g, gm_w_s, gm_b_s, s5_lam_re, s5_lam_im, s5_log_dt, s5_b_re, s5_b_im, s5_c_re, s5_c_im, s5_d, s5_w_glu, s5_b_glu, wo_in, wo_out, q_norm_g, k_norm_g, final_g, loss_target, m_c_ctx, m_norm_g, m_w_mod, m_b_mod, m_we_in, m_we_out, m_gm_v_g, m_gm_w_s, m_gm_b_s, m_s5_lam_re, m_s5_lam_im, m_s5_log_dt, m_s5_b_re, m_s5_b_im, m_s5_c_re, m_s5_c_im, m_s5_d, m_s5_w_glu, m_s5_b_glu, m_wo_in, m_wo_out, m_q_norm_g, m_k_norm_g, m_final_g, v_c_ctx, v_norm_g, v_w_mod, v_b_mod, v_we_in, v_we_out, v_gm_v_g, v_gm_w_s, v_gm_b_s, v_s5_lam_re, v_s5_lam_im, v_s5_log_dt, v_s5_b_re, v_s5_b_im, v_s5_c_re, v_s5_c_im, v_s5_d, v_s5_w_glu, v_s5_b_glu, v_wo_in, v_wo_out, v_q_norm_g, v_k_norm_g, v_final_g):
    weights = dict(c_ctx=c_ctx, norm_g=norm_g, w_mod=w_mod, b_mod=b_mod, we_in=we_in, we_out=we_out, gm_v_g=gm_v_g,
                   gm_w_s=gm_w_s, gm_b_s=gm_b_s, s5_lam_re=s5_lam_re, s5_lam_im=s5_lam_im, s5_log_dt=s5_log_dt,
                   s5_b_re=s5_b_re, s5_b_im=s5_b_im, s5_c_re=s5_c_re, s5_c_im=s5_c_im, s5_d=s5_d, s5_w_glu=s5_w_glu,
                   s5_b_glu=s5_b_glu, wo_in=wo_in, wo_out=wo_out, q_norm_g=q_norm_g, k_norm_g=k_norm_g,
                   final_g=final_g)
    mom_m = dict(c_ctx=m_c_ctx, norm_g=m_norm_g, w_mod=m_w_mod, b_mod=m_b_mod, we_in=m_we_in, we_out=m_we_out,
                 gm_v_g=m_gm_v_g, gm_w_s=m_gm_w_s, gm_b_s=m_gm_b_s, s5_lam_re=m_s5_lam_re, s5_lam_im=m_s5_lam_im,
                 s5_log_dt=m_s5_log_dt, s5_b_re=m_s5_b_re, s5_b_im=m_s5_b_im, s5_c_re=m_s5_c_re, s5_c_im=m_s5_c_im,
                 s5_d=m_s5_d, s5_w_glu=m_s5_w_glu, s5_b_glu=m_s5_b_glu, wo_in=m_wo_in, wo_out=m_wo_out,
                 q_norm_g=m_q_norm_g, k_norm_g=m_k_norm_g, final_g=m_final_g)
    mom_v = dict(c_ctx=v_c_ctx, norm_g=v_norm_g, w_mod=v_w_mod, b_mod=v_b_mod, we_in=v_we_in, we_out=v_we_out,
                 gm_v_g=v_gm_v_g, gm_w_s=v_gm_w_s, gm_b_s=v_gm_b_s, s5_lam_re=v_s5_lam_re, s5_lam_im=v_s5_lam_im,
                 s5_log_dt=v_s5_log_dt, s5_b_re=v_s5_b_re, s5_b_im=v_s5_b_im, s5_c_re=v_s5_c_re, s5_c_im=v_s5_c_im,
                 s5_d=v_s5_d, s5_w_glu=v_s5_w_glu, s5_b_glu=v_s5_b_glu, wo_in=v_wo_in, wo_out=v_wo_out,
                 q_norm_g=v_q_norm_g, k_norm_g=v_k_norm_g, final_g=v_final_g)

    ixy = 2 * lax.axis_index("x") + lax.axis_index("y")
    n_lat = x.shape[1]
    nl = norm_g.shape[0]
    nmod = w_mod.shape[2]
    xin = jnp.concatenate([ctx[0], x[0]], axis=0)

    items = []
    for l in range(2):
        items += [(we_in[l].astype(BF), "xy", "gather"), (we_out[l].astype(BF), "xy", "gather"),
                  (s5_w_glu[l].astype(BF), "xy", "gather"), (wo_in[l].astype(BF), "xy", "gather"),
                  (wo_out[l].astype(BF), "xy", "gather")]
    items.append((c, "xy", "gather"))
    got = exchange(items, "gather_weights")
    wein = [got[0], got[5]]
    weout = [got[1].reshape(1, D, D), got[6].reshape(1, D, D)]
    wglu = [got[2].reshape(AW, AW), got[7].reshape(AW, AW)]
    woin = [got[3], got[8]]
    woout = [got[4].reshape(1, D, D), got[9].reshape(1, D, D)]
    c_group = got[10].reshape(4, D)

    cond = jnp.concatenate([c_group, jnp.broadcast_to(c_ctx.reshape(1, D), (4, D))], axis=0)
    b_shard = lax.dynamic_slice(b_mod, (0, ixy * nmod), (nl, nmod)).reshape(nl, 1, nmod)
    mpart = ada_fwd(cond, w_mod, b_shard)
    m_lat, m_ctx = exchange([(jnp.transpose(mpart[:, 0:4], (1, 0, 2)), "xy", "scatter"),
                             (mpart[:, 4], "xy", "gather")], "exchange_mod")
    m_lat = jnp.transpose(m_lat, (1, 0, 2)).reshape(nl, 3, D)
    m_ctx = jnp.transpose(m_ctx, (1, 0, 2)).reshape(nl, 3, D)
    mods = [jnp.stack([m_ctx[l], m_lat[l]], axis=0) for l in range(nl)]

    loss_part, dx, g, d_norm_g, d_mod_lat, d_mod_ctx, d_final_g = _local_step(
        xin, loss_target[0], mods, wein, weout, wglu, woin, woout, weights)
    grad_x = dx[NC:].reshape(1, n_lat, D)

    d_mod_lat, d_mod_ctx = jnp.stack(d_mod_lat), jnp.stack(d_mod_ctx)
    dm_send = jnp.stack([d_mod_lat.reshape(nl, 4, nmod), d_mod_ctx.reshape(nl, 4, nmod)])
    (dm_got,) = exchange([(jnp.transpose(dm_send, (2, 0, 1, 3)), "xy", "scatter")], "exchange_dmod")
    dm_rows = jnp.concatenate([dm_got[:, 0], dm_got[:, 1]], axis=0)
    gw_mod, d_cctx = ada_bwd(cond, jnp.transpose(dm_rows, (1, 0, 2)), w_mod)
    g_small = dict(c_ctx=d_cctx.reshape(D), norm_g=jnp.stack(d_norm_g), b_mod=add2(d_mod_lat, d_mod_ctx, "add_dbmod"),
                   final_g=d_final_g.reshape(D))
    for name in _SMALL:
        if name not in g_small:
            g_small[name] = jnp.stack(g[name])

    flat = jnp.concatenate([g_small[n].reshape(-1) for n in _SMALL])
    nflat = flat.shape[0]
    npad = -(-nflat // _SMALL_ALIGN) * _SMALL_ALIGN
    flat = jnp.concatenate([flat, jnp.zeros((npad - nflat,), F32)]).reshape(8, npad // (8 * 128), 128)
    items = [(flat, "all", "scatter"), (gw_mod.reshape(2, nl // 2 * D, nmod), "c", "scatter")]
    for name in _BIG:
        st = jnp.stack(g[name], axis=1)
        items.append((st.reshape(8, st.shape[2], st.shape[3]), "all", "scatter"))
    parts = exchange(items, "reduce_scatter")
    sums = [sum_lead(pt, f"sum_parts{k}") for k, pt in enumerate(parts)]
    items = [(sums[0], "all", "gather"), (sums[1], "c", "gather")] + [(s_, "c", "gather") for s_ in sums[2:]]
    full = exchange(items, "all_gather")
    flat = full[0].reshape(-1)
    grads = {}
    off = 0
    for name in _SMALL:
        sz = weights[name].size
        grads[name] = flat[off:off + sz].reshape(weights[name].shape)
        off += sz
    grads["w_mod"] = full[1].reshape(w_mod.shape)
    for k, name in enumerate(_BIG):
        grads[name] = full[2 + k].reshape(weights[name].shape)

    delta, new_m, new_v = {}, {}, {}
    for name in _WEIGHTS:
        w2 = _view2d(weights[name])
        d2, m2, v2 = adamw(w2, grads[name].reshape(w2.shape), mom_m[name].reshape(w2.shape),
                           mom_v[name].reshape(w2.shape), f"adamw_{name}")
        shp = weights[name].shape
        delta[name], new_m[name], new_v[name] = d2.reshape(shp), m2.reshape(shp), v2.reshape(shp)

    loss = lax.psum(loss_part[0, 0], ("x", "y", "c"))
    return (loss, grad_x, *[grads[n] for n in _WEIGHTS], *[delta[n] for n in _WEIGHTS],
            *[new_m[n] for n in _WEIGHTS], *[new_v[n] for n in _WEIGHTS])


def _local_step(xin, target, mods, wein, weout, wglu, woin, woout, w):
    norm_g, gm_v_g, gm_w_s, gm_b_s = w["norm_g"], w["gm_v_g"], w["gm_w_s"], w["gm_b_s"]
    s5_lam_re, s5_lam_im, s5_log_dt = w["s5_lam_re"], w["s5_lam_im"], w["s5_log_dt"]
    s5_b_re, s5_b_im, s5_c_re, s5_c_im = w["s5_b_re"], w["s5_b_im"], w["s5_c_re"], w["s5_c_im"]
    s5_d, s5_b_glu, q_norm_g, k_norm_g, final_g = w["s5_d"], w["s5_b_glu"], w["q_norm_g"], w["k_norm_g"], w["final_g"]
    nl = norm_g.shape[0]
    n_lat = xin.shape[0] - NC

    cos, sins = _rope_tables(n_lat)

    s5p = []
    for i in range(2):
        lam_l = (s5_lam_re[i].reshape(2, SW), s5_lam_im[i].reshape(2, SW),
                 jnp.repeat(s5_log_dt[i], SP, axis=1))
        lam_r = (jnp.repeat(s5_lam_re[i].reshape(2 * SG, SP), SH, axis=0),
                 jnp.repeat(s5_lam_im[i].reshape(2 * SG, SP), SH, axis=0),
                 jnp.repeat(s5_log_dt[i].reshape(2 * SG, 1), SH, axis=0))
        b_r = (jnp.transpose(s5_b_re[i], (0, 1, 3, 2)).reshape(2 * SG * SH, SP),
               jnp.transpose(s5_b_im[i], (0, 1, 3, 2)).reshape(2 * SG * SH, SP))
        pw_re, pw_im, bbr, bbi = s5_disc(*lam_l, *lam_r, *b_r)
        s5p.append(dict(
            lam_r=lam_r, b_r=b_r, pw_re=pw_re, pw_im=pw_im,
            bb_re=_block_diag(bbr.reshape(2, SG, SH, SP), False).astype(BF),
            bb_im=_block_diag(bbi.reshape(2, SG, SH, SP), False).astype(BF),
            ct_re=_block_diag(s5_c_re[i], True).astype(BF), ct_im=_block_diag(s5_c_im[i], True).astype(BF)))

    saved = []
    xcur = xin
    for l in range(nl):
        i = l // 2
        h = pro_fwd(xcur, norm_g[l].reshape(1, D), mods[l], f"pro_fwd{l}")
        sv = dict(x=xcur, h=h)
        if l % 2 == 0:
            p = mm_nn(h, wein[i], f"in_proj{l}")
            sp = s5p[i]
            for dr, rev in ((0, False), (1, True)):
                sv[f"y{dr}"], sv[f"hpr{dr}"], sv[f"hpi{dr}"] = s5_fwd(
                    p, sp["bb_re"][dr], sp["bb_im"][dr], sp["ct_re"][dr], sp["ct_im"][dr],
                    sp["pw_re"][dr], sp["pw_im"][dr], rev, f"s5_fwd{l}_{dr}")
            mix = mix_fwd(p, sv["y0"], sv["y1"], gm_v_g[i].reshape(1, AW), gm_w_s[i].astype(BF),
                          gm_b_s[i].reshape(NGRP, CHUNK, 1), s5_d[i].reshape(1, AW), wglu[i],
                          s5_b_glu[i].reshape(1, AW), f"mix_fwd{l}")
            o = mm_nn(mix, weout[i], f"out_proj{l}")
        else:
            p = mm_nn(h, woin[i], f"in_proj{l}")
            sv["qkv"] = attn_prep(p, q_norm_g[i].reshape(1, HD), k_norm_g[i].reshape(1, HD), cos, sins, f"attn_prep{l}")
            sv["o_att"], mix, sv["lse"] = attn_fwd(sv["qkv"], p, f"attn_fwd{l}")
            o = mm_nn(mix, woout[i], f"out_proj{l}")
        sv.update(p=p, mix=mix, o=o)
        saved.append(sv)
        xcur = res_fwd(xcur, o, mods[l], l < nl - 1, f"res_fwd{l}")

    loss_part, dx, d_final_g = final_loss(xcur, target, final_g.reshape(1, D))

    g = {}
    d_norm_g, d_mod_lat, d_mod_ctx = [None] * nl, [None] * nl, [None] * nl
    for name in ("we_in", "we_out", "s5_w_glu", "wo_in", "wo_out", "gm_v_g", "gm_w_s", "gm_b_s", "s5_lam_re", "s5_lam_im",
                 "s5_log_dt", "s5_b_re", "s5_b_im", "s5_c_re", "s5_c_im", "s5_d", "s5_b_glu", "q_norm_g", "k_norm_g"):
        g[name] = [None, None]
    for l in reversed(range(nl)):
        i = l // 2
        sv = saved[l]
        do, dgt = res_bwd(dx, sv["o"], mods[l], l < nl - 1, f"res_bwd{l}")
        w_out = weout[i] if l % 2 == 0 else woout[i]
        dmix = mm_nt(do, w_out, f"out_dgrad{l}")
        dw_out = mm_tn(sv["mix"], do, 1, f"out_wgrad{l}").reshape(4, D // 4, D)
        if l % 2 == 0:
            sp = s5p[i]
            (dpa, dpgb, dy, g["gm_w_s"][i], dbs, dvg, dd, g["s5_w_glu"][i], dbg) = mix_bwd(
                sv["p"], sv["y0"], sv["y1"], dmix, gm_v_g[i].reshape(1, AW), gm_w_s[i].astype(BF),
                gm_b_s[i].reshape(NGRP, CHUNK, 1), s5_d[i].reshape(1, AW), wglu[i], s5_b_glu[i].reshape(1, AW),
                f"mix_bwd{l}")
            g["gm_b_s"][i], g["gm_v_g"][i] = dbs.reshape(NGRP, CHUNK), dvg.reshape(AW)
            g["s5_d"][i], g["s5_b_glu"][i] = dd.reshape(AW), dbg.reshape(AW)
            g["s5_w_glu"][i] = g["s5_w_glu"][i].reshape(4, AW // 4, AW)
            g["we_out"][i] = dw_out
            dxd, das_r, das_i, dbbs_r, dbbs_i, dcs_r, dcs_i = [], [], [], [], [], [], []
            for dr, rev in ((0, False), (1, True)):
                dxs_d, da_r, da_i, dbb_r, dbb_i, dc_r, dc_i = s5_bwd(
                    sv["p"], sv[f"hpr{dr}"], sv[f"hpi{dr}"], dy, sp["bb_re"][dr], sp["bb_im"][dr],
                    sp["ct_re"][dr], sp["ct_im"][dr], sp["pw_re"][dr], -sp["pw_im"][dr],
                    sp["pw_re"][dr][0:1], sp["pw_im"][dr][0:1], rev, f"s5_bwd{l}_{dr}")
                dxd.append(dxs_d)
                das_r.append(jnp.repeat(da_r.reshape(SG, SP), SH, axis=0))
                das_i.append(jnp.repeat(da_i.reshape(SG, SP), SH, axis=0))
                dbbs_r.append(_diag_blocks(dbb_r).reshape(SG * SH, SP))
                dbbs_i.append(_diag_blocks(dbb_i).reshape(SG * SH, SP))
                dcs_r.append(_diag_blocks(dc_r))
                dcs_i.append(_diag_blocks(dc_i))
            cat = lambda parts: jnp.concatenate(parts, axis=0)
            dlr, dli, dldt, dbr, dbi = s5_param_bwd(*sp["lam_r"], *sp["b_r"], cat(das_r), cat(das_i),
                                                    cat(dbbs_r), cat(dbbs_i))
            g["s5_lam_re"][i], g["s5_lam_im"][i] = dlr.reshape(2, SG, SP), dli.reshape(2, SG, SP)
            g["s5_log_dt"][i] = dldt.reshape(2, SG)
            g["s5_b_re"][i] = jnp.transpose(dbr.reshape(2, SG, SH, SP), (0, 1, 3, 2))
            g["s5_b_im"][i] = jnp.transpose(dbi.reshape(2, SG, SH, SP), (0, 1, 3, 2))
            g["s5_c_re"][i], g["s5_c_im"][i] = jnp.stack(dcs_r), jnp.stack(dcs_i)
            dxs = s5_dx_sum(dy, s5_d[i].reshape(1, AW), dxd[0], dxd[1], f"s5_dx_sum{l}")
            dp = jnp.concatenate([dpa, dxs, dpgb], axis=1)
            w_in = wein[i]
        else:
            dq, dgate, dk, dv = attn_bwd(sv["qkv"], sv["p"], dmix, sv["o_att"], sv["lse"], f"attn_bwd{l}")
            dpqkv, dqg, dkg = attn_prep_bwd(sv["p"], dq, dk, dv, q_norm_g[i].reshape(1, HD),
                                            k_norm_g[i].reshape(1, HD), cos, sins, f"attn_prep_bwd{l}")
            g["q_norm_g"][i], g["k_norm_g"][i] = dqg.reshape(HD), dkg.reshape(HD)
            g["wo_out"][i] = dw_out
            dp = jnp.concatenate([dpqkv, dgate], axis=1)
            w_in = woin[i]
        dh = mm_nt(dp, w_in, f"in_dgrad{l}")
        g["we_in" if l % 2 == 0 else "wo_in"][i] = mm_tn(sv["h"], dp, 4, f"in_wgrad{l}")
        dx, dmod2, dng = pro_bwd(sv["x"], dh, dx, norm_g[l].reshape(1, D), mods[l], f"pro_bwd{l}")
        d_norm_g[l] = dng.reshape(D)
        d_mod_ctx[l] = jnp.concatenate([dmod2[0, 0], dmod2[0, 1], dgt[0]])
        d_mod_lat[l] = jnp.concatenate([dmod2[1, 0], dmod2[1, 1], dgt[1]])
    return loss_part, dx, g, d_norm_g, d_mod_lat, d_mod_ctx, d_final_g
```

```python
import functools
import math

import numpy as np
import jax
import jax.numpy as jnp
from jax import lax
from jax.experimental import pallas as pl
from jax.experimental.pallas import tpu as pltpu

F32 = jnp.float32
BF = jnp.bfloat16
MESH = pl.DeviceIdType.MESH

D = 1024
NC = 256
SEQ = 4096
GRID_W = 64
TM = 256
CHUNK = 128
EPS = 1e-6
HD = 128
NQ = 8
NKV = 2
ROPE_THETA = 10000.0
SG = 32
SP = 64
SH = 16
SW = SG * SP
GELU_K = math.sqrt(2.0 / math.pi)
GELU_C = 0.044715
VMEM_LIMIT_BYTES = 56 * 1024 * 1024

ADAM_LR = 0.001
ADAM_B1 = 0.9
ADAM_B2 = 0.999
ADAM_EPS = 1e-08
ADAM_WD = 0.01
ADAM_STEP = 10


def _call(body, *, name, out_shape, grid=None, in_specs=None, out_specs=None, scratch=()):
    kw = {}
    if grid is not None:
        kw["grid"] = grid
    if in_specs is not None:
        kw["in_specs"] = in_specs
    if out_specs is not None:
        kw["out_specs"] = out_specs
    return pl.pallas_call(
        body, name=name, out_shape=out_shape, scratch_shapes=list(scratch),
        compiler_params=pltpu.CompilerParams(vmem_limit_bytes=VMEM_LIMIT_BYTES), **kw)


def _dot(a, b, ca=1, cb=0):
    return lax.dot_general(a, b, (((ca,), (cb,)), ((), ())), preferred_element_type=F32)


def _sig(x):
    return 1.0 / (1.0 + jnp.exp(-x))


def _full(shape):
    n = len(shape)
    return pl.BlockSpec(shape, lambda *_: (0,) * n)


def _mm_rows(t):
    for rows in (1088, 1024, 768, 512, 256):
        if t % rows == 0:
            return rows
    raise ValueError(t)


def mm_nn(a, w3, name, out_dtype=F32):
    t, k = a.shape
    j, _, nb = w3.shape
    tr = _mm_rows(t)

    def body(a_ref, w_ref, o_ref):
        o_ref[...] = _dot(a_ref[...], w_ref[0]).astype(o_ref.dtype)

    return _call(body, name=name, grid=(j, t // tr),
                 in_specs=[pl.BlockSpec((tr, k), lambda jj, i: (i, 0)),
                           pl.BlockSpec((1, k, nb), lambda jj, i: (jj, 0, 0))],
                 out_specs=pl.BlockSpec((tr, nb), lambda jj, i: (i, jj)),
                 out_shape=jax.ShapeDtypeStruct((t, j * nb), out_dtype))(a, w3)


def mm_nt(a, w3, name, out_dtype=F32):
    t, _ = a.shape
    j, k, nb = w3.shape
    tr = _mm_rows(t)

    def body(a_ref, w_ref, o_ref):
        acc = _dot(a_ref[:, 0:nb], w_ref[0], 1, 1)
        for jj in range(1, j):
            acc = acc + _dot(a_ref[:, jj * nb:(jj + 1) * nb], w_ref[jj], 1, 1)
        o_ref[...] = acc.astype(o_ref.dtype)

    return _call(body, name=name, grid=(t // tr,),
                 in_specs=[pl.BlockSpec((tr, j * nb), lambda i: (i, 0)), _full((j, k, nb))],
                 out_specs=pl.BlockSpec((tr, k), lambda i: (i, 0)),
                 out_shape=jax.ShapeDtypeStruct((t, k), out_dtype))(a, w3)


def mm_tn(a, b, j, name, slot=0, into=None):
    t, m = a.shape
    nb = b.shape[1] // j
    tr = _mm_rows(t)

    def body(a_ref, b_ref, *rest):
        o_ref = rest[-1]

        @pl.when(pl.program_id(1) == 0)
        def _():
            o_ref[...] = jnp.zeros_like(o_ref)
        o_ref[0, 0] += _dot(a_ref[...], b_ref[...], 0, 0)

    in_specs = [pl.BlockSpec((tr, m), lambda jj, i: (i, 0)), pl.BlockSpec((tr, nb), lambda jj, i: (i, jj))]
    args = [a, b]
    alias = {}
    if into is not None:
        in_specs.append(pl.BlockSpec(memory_space=pl.ANY))
        args.append(into)
        alias = {2: 0}
    return pl.pallas_call(
        body, name=name, grid=(j, t // tr), in_specs=in_specs,
        out_specs=pl.BlockSpec((1, 1, m, nb), lambda jj, i: (slot, jj, 0, 0)),
        out_shape=jax.ShapeDtypeStruct((2, j, m, nb), F32), input_output_aliases=alias,
        compiler_params=pltpu.CompilerParams(vmem_limit_bytes=VMEM_LIMIT_BYTES))(*args)


def _mod_rows(mod_ref, i):
    ctx = i == 0
    sh = jnp.where(ctx, mod_ref[0, 0:1, :], mod_ref[1, 0:1, :])
    sc = jnp.where(ctx, mod_ref[0, 1:2, :], mod_ref[1, 1:2, :])
    gt = jnp.where(ctx, mod_ref[0, 2:3, :], mod_ref[1, 2:3, :])
    return sh, sc, gt


def pro_fwd(x, g, mod, name):
    t = x.shape[0]

    def body(x_ref, g_ref, mod_ref, h_ref):
        sh, sc, _ = _mod_rows(mod_ref, pl.program_id(0))
        xv = x_ref[...]
        r = lax.rsqrt(jnp.mean(xv * xv, axis=-1, keepdims=True) + EPS)
        h_ref[...] = ((xv * r) * g_ref[...] * (1.0 + sc) + sh).astype(BF)

    return _call(body, name=name, grid=(t // TM,),
                 in_specs=[pl.BlockSpec((TM, D), lambda i: (i, 0)), _full((1, D)), _full((2, 3, D))],
                 out_specs=pl.BlockSpec((TM, D), lambda i: (i, 0)),
                 out_shape=jax.ShapeDtypeStruct((t, D), BF))(x, g, mod)


def pro_bwd(x, dh, dxn, g, mod, name):
    t = x.shape[0]

    def body(x_ref, dh_ref, dxn_ref, g_ref, mod_ref, dx_ref, dmod_ref, dg_ref):
        i = pl.program_id(0)

        @pl.when(i == 0)
        def _():
            dmod_ref[...] = jnp.zeros_like(dmod_ref)
            dg_ref[...] = jnp.zeros_like(dg_ref)

        _, sc, _ = _mod_rows(mod_ref, i)
        xv = x_ref[...]
        gv = g_ref[...]
        r = lax.rsqrt(jnp.mean(xv * xv, axis=-1, keepdims=True) + EPS)
        xn = xv * r
        dh_v = dh_ref[...]
        e = dh_v * (1.0 + sc)
        dsh = jnp.sum(dh_v, axis=0, keepdims=True)
        dsc = jnp.sum(dh_v * xn * gv, axis=0, keepdims=True)
        dg_ref[...] += jnp.sum(e * xn, axis=0, keepdims=True)
        dxh = e * gv
        dx_ref[...] = dxn_ref[...] + r * (dxh - xn * jnp.mean(dxh * xn, axis=-1, keepdims=True))

        @pl.when(i == 0)
        def _():
            dmod_ref[0, 0:1, :] += dsh
            dmod_ref[0, 1:2, :] += dsc

        @pl.when(i > 0)
        def _():
            dmod_ref[1, 0:1, :] += dsh
            dmod_ref[1, 1:2, :] += dsc

    tile = pl.BlockSpec((TM, D), lambda i: (i, 0))
    return _call(body, name=name, grid=(t // TM,),
                 in_specs=[tile, tile, tile, _full((1, D)), _full((2, 3, D))],
                 out_specs=[tile, _full((2, 2, D)), _full((1, D))],
                 out_shape=[jax.ShapeDtypeStruct((t, D), F32), jax.ShapeDtypeStruct((2, 2, D), F32),
                            jax.ShapeDtypeStruct((1, D), F32)])(x, dh, dxn, g, mod)


def res_fwd(x, o, mod, update_ctx, name):
    t = x.shape[0]

    def body(x_ref, o_ref, mod_ref, y_ref):
        i = pl.program_id(0)
        _, _, gt = _mod_rows(mod_ref, i)
        upd = x_ref[...] + gt * o_ref[...]
        if update_ctx:
            y_ref[...] = upd
        else:
            y_ref[...] = jnp.where(i == 0, x_ref[...], upd)

    tile = pl.BlockSpec((TM, D), lambda i: (i, 0))
    return _call(body, name=name, grid=(t // TM,), in_specs=[tile, tile, _full((2, 3, D))],
                 out_specs=tile, out_shape=jax.ShapeDtypeStruct((t, D), F32))(x, o, mod)


def res_bwd(dxn, o, mod, update_ctx, name):
    t = dxn.shape[0]

    def body(dxn_ref, o_ref, mod_ref, do_ref, dgt_ref):
        i = pl.program_id(0)

        @pl.when(i == 0)
        def _():
            dgt_ref[...] = jnp.zeros_like(dgt_ref)

        _, _, gt = _mod_rows(mod_ref, i)
        dv = dxn_ref[...]
        do = gt * dv
        dgt = jnp.sum(dv * o_ref[...], axis=0, keepdims=True)
        if update_ctx:
            do_ref[...] = do.astype(BF)
        else:
            do_ref[...] = jnp.where(i == 0, jnp.zeros_like(do), do).astype(BF)

        if update_ctx:
            @pl.when(i == 0)
            def _():
                dgt_ref[0:1, :] += dgt

        @pl.when(i > 0)
        def _():
            dgt_ref[1:2, :] += dgt

    tile = pl.BlockSpec((TM, D), lambda i: (i, 0))
    return _call(body, name=name, grid=(t // TM,), in_specs=[tile, tile, _full((2, 3, D))],
                 out_specs=[tile, _full((2, D))],
                 out_shape=[jax.ShapeDtypeStruct((t, D), BF), jax.ShapeDtypeStruct((2, D), F32)])(dxn, o, mod)


def final_loss(x, target, g):
    t = x.shape[0]

    def body(x_ref, t_ref, g_ref, loss_ref, dx_ref, dg_ref):
        i = pl.program_id(0)

        @pl.when(i == 0)
        def _():
            loss_ref[...] = jnp.zeros_like(loss_ref)
            dg_ref[...] = jnp.zeros_like(dg_ref)
            dx_ref[...] = jnp.zeros_like(dx_ref)

        @pl.when(i > 0)
        def _():
            xv = x_ref[...]
            gv = g_ref[...]
            r = lax.rsqrt(jnp.mean(xv * xv, axis=-1, keepdims=True) + EPS)
            xn = xv * r
            err = xn * gv - t_ref[...]
            loss_ref[...] += (0.5 / D) * jnp.sum(jnp.sum(err * err, axis=1, keepdims=True), axis=0, keepdims=True)
            dy = err * (1.0 / D)
            dg_ref[...] += jnp.sum(dy * xn, axis=0, keepdims=True)
            dxh = dy * gv
            dx_ref[...] = r * (dxh - xn * jnp.mean(dxh * xn, axis=-1, keepdims=True))

    tile = pl.BlockSpec((TM, D), lambda i: (i, 0))
    return _call(body, name="final_loss", grid=(t // TM,),
                 in_specs=[tile, pl.BlockSpec((TM, D), lambda i: (jnp.maximum(i - 1, 0), 0)), _full((1, D))],
                 out_specs=[_full((1, 1)), tile, _full((1, D))],
                 out_shape=[jax.ShapeDtypeStruct((1, 1), F32), jax.ShapeDtypeStruct((t, D), F32),
                            jax.ShapeDtypeStruct((1, D), F32)])(x, target, g)


def ada_fwd(cond, w_mod, b_mod):
    nl, _, nw = w_mod.shape

    def body(c_ref, w_ref, b_ref, o_ref):
        cv = c_ref[...]
        s = (cv * _sig(cv)).astype(BF)
        o_ref[0] = _dot(s, w_ref[0].astype(BF)) + b_ref[0]

    return _call(body, name="ada_fwd", grid=(nl,),
                 in_specs=[_full((8, D)), pl.BlockSpec((1, D, nw), lambda l: (l, 0, 0)),
                           pl.BlockSpec((1, 1, nw), lambda l: (l, 0, 0))],
                 out_specs=pl.BlockSpec((1, 8, nw), lambda l: (l, 0, 0)),
                 out_shape=jax.ShapeDtypeStruct((nl, 8, nw), F32))(cond, w_mod, b_mod)


def ada_bwd(cond, dm, w_mod):
    nl, _, nw = w_mod.shape

    def body(c_ref, dm_ref, w_ref, gw_ref, dcc_ref, dc_ref):
        l = pl.program_id(0)

        @pl.when(l == 0)
        def _():
            dc_ref[...] = jnp.zeros_like(dc_ref)

        cv = c_ref[...]
        sg = _sig(cv)
        s = (cv * sg).astype(BF)
        dmv = dm_ref[0].astype(BF)
        gw_ref[0] = _dot(s, dmv, 0, 0)
        dc_ref[...] += _dot(dmv, w_ref[0].astype(BF), 1, 1)

        @pl.when(l == nl - 1)
        def _():
            dcond = dc_ref[...] * (sg * (1.0 + cv * (1.0 - sg)))
            dcc_ref[...] = jnp.sum(dcond[4:8], axis=0, keepdims=True)

    return _call(body, name="ada_bwd", grid=(nl,),
                 in_specs=[_full((8, D)), pl.BlockSpec((1, 8, nw), lambda l: (l, 0, 0)),
                           pl.BlockSpec((1, D, nw), lambda l: (l, 0, 0))],
                 out_specs=[pl.BlockSpec((1, D, nw), lambda l: (l, 0, 0)), _full((1, D))],
                 out_shape=[jax.ShapeDtypeStruct((nl, D, nw), F32), jax.ShapeDtypeStruct((1, D), F32)],
                 scratch=[pltpu.VMEM((8, D), F32)])(cond, dm, w_mod)


def add2(a, b, name):
    def body(a_ref, b_ref, o_ref):
        o_ref[...] = a_ref[...] + b_ref[...]

    return _call(body, name=name, out_shape=jax.ShapeDtypeStruct(a.shape, a.dtype))(a, b)


AW = 512
NGRP = 4


def _gelu(y):
    t = jnp.tanh(GELU_K * (y + GELU_C * y * y * y))
    return 0.5 * y * (1.0 + t), t


def _layer_norm_stats(v):
    mu = jnp.mean(v, axis=-1, keepdims=True)
    vc = v - mu
    rstd = lax.rsqrt(jnp.mean(vc * vc, axis=-1, keepdims=True) + EPS)
    return vc * rstd, rstd


def _spatial_mix(vn_ref, ws_ref, bs_ref, mixed_ref):
    for ch in range(TM // CHUNK):
        rows = slice(ch * CHUNK, (ch + 1) * CHUNK)
        for g in range(NGRP):
            cols = slice(g * CHUNK, (g + 1) * CHUNK)
            mixed_ref[rows, cols] = _dot(ws_ref[g], vn_ref[rows, cols]) + bs_ref[g]


def mix_fwd(p, yf, yb, vg, ws, bs, dsk, wglu, bglu, name):
    t = p.shape[0]

    def body(p_ref, yf_ref, yb_ref, vg_ref, ws_ref, bs_ref, d_ref, wg_ref, bg_ref, o_ref, vn_ref, mixed_ref):
        vhat, _ = _layer_norm_stats(p_ref[:, AW:2 * AW])
        vn_ref[...] = (vhat * vg_ref[...]).astype(BF)
        _spatial_mix(vn_ref, ws_ref, bs_ref, mixed_ref)
        ga = p_ref[:, 2 * AW:3 * AW]
        o_ref[:, 0:AW] = (p_ref[:, 0:AW] * mixed_ref[...] * (ga * _sig(ga))).astype(BF)
        y = yf_ref[...] + yb_ref[...] + d_ref[...] * p_ref[:, 3 * AW:4 * AW]
        y2, _ = _gelu(y)
        z = _dot(y2.astype(BF), wg_ref[...]) + bg_ref[...]
        gb = p_ref[:, 4 * AW:5 * AW]
        o_ref[:, AW:2 * AW] = (y2 * _sig(z) * (gb * _sig(gb))).astype(BF)

    tile = lambda w: pl.BlockSpec((TM, w), lambda i: (i, 0))
    return _call(body, name=name, grid=(t // TM,),
                 in_specs=[tile(5 * AW), tile(AW), tile(AW), _full((1, AW)), _full((NGRP, CHUNK, CHUNK)),
                           _full((NGRP, CHUNK, 1)), _full((1, AW)), _full((AW, AW)), _full((1, AW))],
                 out_specs=tile(2 * AW), out_shape=jax.ShapeDtypeStruct((t, 2 * AW), BF),
                 scratch=[pltpu.VMEM((TM, AW), BF), pltpu.VMEM((TM, AW), F32)])(p, yf, yb, vg, ws, bs, dsk, wglu, bglu)


def mix_bwd(p, yf, yb, dmix, vg, ws, bs, dsk, wglu, bglu, name):
    t = p.shape[0]

    def body(p_ref, yf_ref, yb_ref, dm_ref, vg_ref, ws_ref, bs_ref, d_ref, wg_ref, bg_ref,
             dpa_ref, dpgb_ref, dy_ref, dws_ref, dbs_ref, dvg_ref, dd_ref, dwg_ref, dbg_ref,
             vn_ref, mixed_ref, dmx_ref, dvn_ref):
        @pl.when(pl.program_id(0) == 0)
        def _():
            for r in (dws_ref, dbs_ref, dvg_ref, dd_ref, dwg_ref, dbg_ref):
                r[...] = jnp.zeros_like(r)

        vhat, rstd = _layer_norm_stats(p_ref[:, AW:2 * AW])
        vgv = vg_ref[...]
        vn_ref[...] = (vhat * vgv).astype(BF)
        _spatial_mix(vn_ref, ws_ref, bs_ref, mixed_ref)
        u = p_ref[:, 0:AW]
        ga = p_ref[:, 2 * AW:3 * AW]
        sga = _sig(ga)
        dya = dm_ref[:, 0:AW]
        mixed = mixed_ref[...]
        dpa_ref[:, 0:AW] = (dya * mixed * (ga * sga)).astype(BF)
        dpa_ref[:, 2 * AW:3 * AW] = (dya * u * mixed * (sga * (1.0 + ga * (1.0 - sga)))).astype(BF)
        dmx_ref[...] = dya * u * (ga * sga)
        for ch in range(TM // CHUNK):
            rows = slice(ch * CHUNK, (ch + 1) * CHUNK)
            for g in range(NGRP):
                cols = slice(g * CHUNK, (g + 1) * CHUNK)
                dmx = dmx_ref[rows, cols]
                dmxb = dmx.astype(BF)
                dws_ref[g] += _dot(dmxb, vn_ref[rows, cols], 1, 1)
                dbs_ref[g] += jnp.sum(dmx, axis=1, keepdims=True)
                dvn_ref[rows, cols] = _dot(ws_ref[g], dmxb, 0, 0)
        dvn = dvn_ref[...]
        dvg_ref[...] += jnp.sum(dvn * vhat, axis=0, keepdims=True)
        dvh = dvn * vgv
        dpa_ref[:, AW:2 * AW] = (rstd * (dvh - jnp.mean(dvh, axis=-1, keepdims=True)
                                         - vhat * jnp.mean(dvh * vhat, axis=-1, keepdims=True))).astype(BF)

        xs = p_ref[:, 3 * AW:4 * AW]
        y = yf_ref[...] + yb_ref[...] + d_ref[...] * xs
        y2, th = _gelu(y)
        y2b = y2.astype(BF)
        z = _dot(y2b, wg_ref[...]) + bg_ref[...]
        sz = _sig(z)
        gb = p_ref[:, 4 * AW:5 * AW]
        sgb = _sig(gb)
        dyb = dm_ref[:, AW:2 * AW]
        dpgb_ref[...] = (dyb * (y2 * sz) * (sgb * (1.0 + gb * (1.0 - sgb)))).astype(BF)
        dy3 = dyb * (gb * sgb)
        dz = dy3 * y2 * sz * (1.0 - sz)
        dzb = dz.astype(BF)
        dwg_ref[...] += _dot(y2b, dzb, 0, 0)
        dbg_ref[...] += jnp.sum(dz, axis=0, keepdims=True)
        dy2 = dy3 * sz + _dot(dzb, wg_ref[...], 1, 1)
        dgelu = 0.5 * (1.0 + th) + 0.5 * y * (1.0 - th * th) * GELU_K * (1.0 + 3.0 * GELU_C * y * y)
        dy = dy2 * dgelu
        dd_ref[...] += jnp.sum(dy * xs, axis=0, keepdims=True)
        dy_ref[...] = dy

    tile = lambda w: pl.BlockSpec((TM, w), lambda i: (i, 0))
    return _call(body, name=name, grid=(t // TM,),
                 in_specs=[tile(5 * AW), tile(AW), tile(AW), tile(2 * AW), _full((1, AW)), _full((NGRP, CHUNK, CHUNK)),
                           _full((NGRP, CHUNK, 1)), _full((1, AW)), _full((AW, AW)), _full((1, AW))],
                 out_specs=[tile(3 * AW), tile(AW), tile(AW), _full((NGRP, CHUNK, CHUNK)), _full((NGRP, CHUNK, 1)),
                            _full((1, AW)), _full((1, AW)), _full((AW, AW)), _full((1, AW))],
                 out_shape=[jax.ShapeDtypeStruct((t, 3 * AW), BF), jax.ShapeDtypeStruct((t, AW), BF),
                            jax.ShapeDtypeStruct((t, AW), F32), jax.ShapeDtypeStruct((NGRP, CHUNK, CHUNK), F32),
                            jax.ShapeDtypeStruct((NGRP, CHUNK, 1), F32), jax.ShapeDtypeStruct((1, AW), F32),
                            jax.ShapeDtypeStruct((1, AW), F32), jax.ShapeDtypeStruct((AW, AW), F32),
                            jax.ShapeDtypeStruct((1, AW), F32)],
                 scratch=[pltpu.VMEM((TM, AW), BF), pltpu.VMEM((TM, AW), F32), pltpu.VMEM((TM, AW), F32),
                          pltpu.VMEM((TM, AW), F32)])(p, yf, yb, dmix, vg, ws, bs, dsk, wglu, bglu)


LN = 512
NBLK = SW // LN
UB = AW // NBLK
PW_ROWS = 32
POW_EXP = [1, 2, 4, 8, 16, 32, 64, 128, 256, 0, 0, 0, 0, 0, 0, 0,
           1, 2, 3, 4, 5, 6, 7, 8, 8, 7, 6, 5, 4, 3, 2, 1]
GROUPS_PER_TILE = TM // 8


def s5_disc(lam_re, lam_im, dt, lam_re_r, lam_im_r, dt_r, b_re, b_im):
    nexp = jnp.asarray(np.array(POW_EXP, np.float32).reshape(PW_ROWS, 1))

    def body(n_ref, lr_ref, li_ref, dt_ref, lrr_ref, lir_ref, dtr_ref, br_ref, bi_ref,
             pr_ref, pi_ref, bbr_ref, bbi_ref):
        for dr in range(2):
            dtl = jnp.exp(dt_ref[dr:dr + 1, :])
            zr = lr_ref[dr:dr + 1, :] * dtl
            zi = li_ref[dr:dr + 1, :] * dtl
            mag = jnp.exp(n_ref[...] * zr)
            ang = n_ref[...] * zi
            pr_ref[dr] = mag * jnp.cos(ang)
            pi_ref[dr] = mag * jnp.sin(ang)
        lr, li, dtv = lrr_ref[...], lir_ref[...], jnp.exp(dtr_ref[...])
        mag = jnp.exp(lr * dtv)
        nr = mag * jnp.cos(li * dtv) - 1.0
        ni = mag * jnp.sin(li * dtv)
        den = lr * lr + li * li
        fr = (nr * lr + ni * li) / den
        fi = (ni * lr - nr * li) / den
        bbr_ref[...] = fr * br_ref[...] - fi * bi_ref[...]
        bbi_ref[...] = fr * bi_ref[...] + fi * br_ref[...]

    rows = lam_re_r.shape[0]
    return _call(body, name="s5_disc",
                 out_shape=[jax.ShapeDtypeStruct((2, PW_ROWS, SW), F32), jax.ShapeDtypeStruct((2, PW_ROWS, SW), F32),
                            jax.ShapeDtypeStruct((rows, SP), F32), jax.ShapeDtypeStruct((rows, SP), F32)])(
        nexp, lam_re, lam_im, dt, lam_re_r, lam_im_r, dt_r, b_re, b_im)


def s5_param_bwd(lam_re_r, lam_im_r, dt_r, b_re, b_im, da_re, da_im, dbb_re, dbb_im):
    rows = lam_re_r.shape[0]
    ng = rows // SH
    seg = jnp.asarray(np.kron(np.eye(ng, dtype=np.float32), np.ones((1, SH), np.float32)))

    def body(seg_ref, lr_ref, li_ref, dt_ref, br_ref, bi_ref, dar_ref, dai_ref, dbbr_ref, dbbi_ref,
             dlr_ref, dli_ref, ddt_ref, dbr_ref, dbi_ref):
        lr, li, dtv = lr_ref[...], li_ref[...], jnp.exp(dt_ref[...])
        mag = jnp.exp(lr * dtv)
        lbr = mag * jnp.cos(li * dtv)
        lbi = mag * jnp.sin(li * dtv)
        den = lr * lr + li * li
        nr, ni = lbr - 1.0, lbi
        fr = (nr * lr + ni * li) / den
        fi = (ni * lr - nr * li) / den
        br, bi = br_ref[...], bi_ref[...]
        gbr, gbi = dbbr_ref[...], dbbi_ref[...]
        dbr_ref[...] = gbr * fr + gbi * fi
        dbi_ref[...] = gbi * fr - gbr * fi
        gfr = gbr * br + gbi * bi
        gfi = gbi * br - gbr * bi
        ilr, ili = lr / den, -li / den
        gnr = gfr * ilr + gfi * ili
        gni = gfi * ilr - gfr * ili
        qr = -(fr * ilr - fi * ili)
        qi = -(fr * ili + fi * ilr)
        glr = gfr * qr + gfi * qi
        gli = gfi * qr - gfr * qi
        first = (lax.broadcasted_iota(jnp.int32, (rows, 1), 0) % SH) == 0
        glbr = gnr + jnp.where(first, dar_ref[...], 0.0)
        glbi = gni + jnp.where(first, dai_ref[...], 0.0)
        gzr = glbr * lbr + glbi * lbi
        gzi = glbi * lbr - glbr * lbi
        glr = glr + gzr * dtv
        gli = gli + gzi * dtv
        gdt = (gzr * lr + gzi * li) * dtv
        hi = lax.Precision.HIGHEST
        sg = seg_ref[...]
        dlr_ref[...] = jnp.dot(sg, glr, precision=hi, preferred_element_type=F32)
        dli_ref[...] = jnp.dot(sg, gli, precision=hi, preferred_element_type=F32)
        ddt_ref[...] = jnp.sum(jnp.dot(sg, gdt, precision=hi, preferred_element_type=F32), axis=1, keepdims=True)

    return _call(body, name="s5_param_bwd",
                 out_shape=[jax.ShapeDtypeStruct((ng, SP), F32), jax.ShapeDtypeStruct((ng, SP), F32),
                            jax.ShapeDtypeStruct((ng, 1), F32), jax.ShapeDtypeStruct((rows, SP), F32),
                            jax.ShapeDtypeStruct((rows, SP), F32)])(
        seg, lam_re_r, lam_im_r, dt_r, b_re, b_im, da_re, da_im, dbb_re, dbb_im)


def _scan_tile(hr_ref, hi_ref, er_ref, ei_ref, st_ref, cr_ref, ci_ref, pr_ref, pi_ref, reverse):
    row8 = lax.broadcasted_iota(jnp.int32, (TM, 1), 0) % 8
    rowe = lax.broadcasted_iota(jnp.int32, (2 * GROUPS_PER_TILE, 1), 0)
    ne = 2 * GROUPS_PER_TILE
    for blk in range(NBLK):
        cols = slice(blk * LN, (blk + 1) * LN)
        hr, hi = hr_ref[:, cols], hi_ref[:, cols]
        for k, s in enumerate((1, 2, 4)):
            ar, ai = pr_ref[k:k + 1, cols], pi_ref[k:k + 1, cols]
            if reverse:
                m = row8 < 8 - s
                sr, si = pltpu.roll(hr, TM - s, 0), pltpu.roll(hi, TM - s, 0)
            else:
                m = row8 >= s
                sr, si = pltpu.roll(hr, s, 0), pltpu.roll(hi, s, 0)
            sr, si = jnp.where(m, sr, 0.0), jnp.where(m, si, 0.0)
            hr, hi = hr + ar * sr - ai * si, hi + ar * si + ai * sr
        hr_ref[:, cols] = hr
        hi_ref[:, cols] = hi
        edge = 0 if reverse else 7
        nq = LN // 128
        for q in range(nq):
            st_ref[q] = hr[:, q * 128:(q + 1) * 128]
            st_ref[nq + q] = hi[:, q * 128:(q + 1) * 128]
        gr = jnp.concatenate([st_ref[q, pl.ds(edge, GROUPS_PER_TILE, stride=8), :] for q in range(nq)], axis=1)
        gi = jnp.concatenate([st_ref[nq + q, pl.ds(edge, GROUPS_PER_TILE, stride=8), :] for q in range(nq)], axis=1)
        zero = jnp.zeros((GROUPS_PER_TILE, LN), F32)
        if reverse:
            er_ref[0:GROUPS_PER_TILE, :] = gr
            ei_ref[0:GROUPS_PER_TILE, :] = gi
            er_ref[GROUPS_PER_TILE:ne, :] = zero
            ei_ref[GROUPS_PER_TILE:ne, :] = zero
            er_ref[GROUPS_PER_TILE:GROUPS_PER_TILE + 1, :] = cr_ref[:, cols]
            ei_ref[GROUPS_PER_TILE:GROUPS_PER_TILE + 1, :] = ci_ref[:, cols]
        else:
            er_ref[0:GROUPS_PER_TILE, :] = zero
            ei_ref[0:GROUPS_PER_TILE, :] = zero
            er_ref[GROUPS_PER_TILE - 1:GROUPS_PER_TILE, :] = cr_ref[:, cols]
            ei_ref[GROUPS_PER_TILE - 1:GROUPS_PER_TILE, :] = ci_ref[:, cols]
            er_ref[GROUPS_PER_TILE:ne, :] = gr
            ei_ref[GROUPS_PER_TILE:ne, :] = gi
        evr, evi = er_ref[...], ei_ref[...]
        s = 1
        k = 3
        while s < ne:
            ar, ai = pr_ref[k:k + 1, cols], pi_ref[k:k + 1, cols]
            if reverse:
                m = rowe < ne - s
                sr, si = pltpu.roll(evr, ne - s, 0), pltpu.roll(evi, ne - s, 0)
            else:
                m = rowe >= s
                sr, si = pltpu.roll(evr, s, 0), pltpu.roll(evi, s, 0)
            sr, si = jnp.where(m, sr, 0.0), jnp.where(m, si, 0.0)
            evr, evi = evr + ar * sr - ai * si, evi + ar * si + ai * sr
            s *= 2
            k += 1
        er_ref[...] = evr
        ei_ref[...] = evi
        if reverse:
            cr_ref[:, cols] = er_ref[0:1, :]
            ci_ref[:, cols] = ei_ref[0:1, :]
            apr, api = pr_ref[24:32, cols], pi_ref[24:32, cols]
        else:
            cr_ref[:, cols] = er_ref[ne - 1:ne, :]
            ci_ref[:, cols] = ei_ref[ne - 1:ne, :]
            apr, api = pr_ref[16:24, cols], pi_ref[16:24, cols]
        for g in range(GROUPS_PER_TILE):
            e_row = g + 1 if reverse else GROUPS_PER_TILE - 1 + g
            kr, ki = er_ref[e_row:e_row + 1, :], ei_ref[e_row:e_row + 1, :]
            rows = slice(8 * g, 8 * g + 8)
            hr_ref[rows, cols] = hr_ref[rows, cols] + apr * kr - api * ki
            hi_ref[rows, cols] = hi_ref[rows, cols] + apr * ki + api * kr


def _tile_order(kind, nt):
    if kind == "fwd":
        return lambda i: i
    if kind == "bwd":
        return lambda i: jnp.where(i == 0, 0, nt - i)
    if kind == "fwd_adj":
        return lambda i: nt - 1 - i
    if kind == "bwd_adj":
        return lambda i: jnp.where(i == nt - 1, 0, i + 1)
    raise ValueError(kind)


def s5_fwd(p, bb_re, bb_im, ct_re, ct_im, pw_re, pw_im, reverse, name):
    t = p.shape[0]
    nt = t // TM
    order = _tile_order("bwd" if reverse else "fwd", nt)

    def body(x_ref, bbr_ref, bbi_ref, ctr_ref, cti_ref, pr_ref, pi_ref, y_ref, hpr_ref, hpi_ref,
             hr_ref, hi_ref, er_ref, ei_ref, st_ref, cr_ref, ci_ref, c0r_ref, c0i_ref):
        @pl.when(pl.program_id(0) == 0)
        def _():
            cr_ref[...] = jnp.zeros_like(cr_ref)
            ci_ref[...] = jnp.zeros_like(ci_ref)

        c0r_ref[...] = cr_ref[...]
        c0i_ref[...] = ci_ref[...]
        xb = x_ref[...].astype(BF)
        for j in range(NBLK):
            cols = slice(j * LN, (j + 1) * LN)
            hr_ref[:, cols] = _dot(xb[:, j * UB:(j + 1) * UB], bbr_ref[j])
            hi_ref[:, cols] = _dot(xb[:, j * UB:(j + 1) * UB], bbi_ref[j])
        _scan_tile(hr_ref, hi_ref, er_ref, ei_ref, st_ref, cr_ref, ci_ref, pr_ref, pi_ref, reverse)
        rowi = lax.broadcasted_iota(jnp.int32, (TM, 1), 0)
        for j in range(NBLK):
            cols = slice(j * LN, (j + 1) * LN)
            hr, hi = hr_ref[:, cols], hi_ref[:, cols]
            y_ref[:, j * UB:(j + 1) * UB] = _dot(hr.astype(BF), ctr_ref[j]) - _dot(hi.astype(BF), cti_ref[j])
            if reverse:
                first = rowi == TM - 1
                sr, si = pltpu.roll(hr, TM - 1, 0), pltpu.roll(hi, TM - 1, 0)
            else:
                first = rowi == 0
                sr, si = pltpu.roll(hr, 1, 0), pltpu.roll(hi, 1, 0)
            hpr_ref[:, cols] = jnp.where(first, c0r_ref[:, cols], sr)
            hpi_ref[:, cols] = jnp.where(first, c0i_ref[:, cols], si)

    state = lambda: pl.BlockSpec((TM, SW), lambda i: (order(i), 0))
    return _call(body, name=name, grid=(nt,),
                 in_specs=[pl.BlockSpec((TM, AW), lambda i: (order(i), 3)),
                           _full((NBLK, UB, LN)), _full((NBLK, UB, LN)), _full((NBLK, LN, UB)), _full((NBLK, LN, UB)),
                           _full((PW_ROWS, SW)), _full((PW_ROWS, SW))],
                 out_specs=[pl.BlockSpec((TM, AW), lambda i: (order(i), 0)), state(), state()],
                 out_shape=[jax.ShapeDtypeStruct((t, AW), F32), jax.ShapeDtypeStruct((t, SW), F32),
                            jax.ShapeDtypeStruct((t, SW), F32)],
                 scratch=[pltpu.VMEM((TM, SW), F32), pltpu.VMEM((TM, SW), F32),
                          pltpu.VMEM((2 * GROUPS_PER_TILE, LN), F32), pltpu.VMEM((2 * GROUPS_PER_TILE, LN), F32),
                          pltpu.VMEM((2 * LN // 128, TM, 128), F32),
                          pltpu.VMEM((1, SW), F32), pltpu.VMEM((1, SW), F32),
                          pltpu.VMEM((1, SW), F32), pltpu.VMEM((1, SW), F32)])(
        p, bb_re, bb_im, ct_re, ct_im, pw_re, pw_im)


def s5_bwd(p, hp_re, hp_im, dy, bb_re, bb_im, ct_re, ct_im, pw_re, pw_im_conj, a_re, a_im, reverse, name):
    t = p.shape[0]
    nt = t // TM
    order = _tile_order("bwd_adj" if reverse else "fwd_adj", nt)

    def body(x_ref, hpr_ref, hpi_ref, dy_ref, bbr_ref, bbi_ref, ctr_ref, cti_ref, pr_ref, pi_ref, ar_ref, ai_ref,
             dx_ref, dar_ref, dai_ref, dbbr_ref, dbbi_ref, dcr_ref, dci_ref,
             gr_ref, gi_ref, er_ref, ei_ref, st_ref, cr_ref, ci_ref):
        @pl.when(pl.program_id(0) == 0)
        def _():
            for r in (cr_ref, ci_ref, dar_ref, dai_ref, dbbr_ref, dbbi_ref, dcr_ref, dci_ref):
                r[...] = jnp.zeros_like(r)

        xb = x_ref[...].astype(BF)
        dyb = dy_ref[...].astype(BF)
        for j in range(NBLK):
            cols = slice(j * LN, (j + 1) * LN)
            gr_ref[:, cols] = _dot(dyb[:, j * UB:(j + 1) * UB], ctr_ref[j], 1, 1)
            gi_ref[:, cols] = -_dot(dyb[:, j * UB:(j + 1) * UB], cti_ref[j], 1, 1)
        _scan_tile(gr_ref, gi_ref, er_ref, ei_ref, st_ref, cr_ref, ci_ref, pr_ref, pi_ref, not reverse)
        for j in range(NBLK):
            cols = slice(j * LN, (j + 1) * LN)
            xj = xb[:, j * UB:(j + 1) * UB]
            dyj = dyb[:, j * UB:(j + 1) * UB]
            hpr, hpi = hpr_ref[:, cols], hpi_ref[:, cols]
            gr, gi = gr_ref[:, cols], gi_ref[:, cols]
            ar, ai = ar_ref[:, cols], ai_ref[:, cols]
            hr = ar * hpr - ai * hpi + _dot(xj, bbr_ref[j])
            hi = ar * hpi + ai * hpr + _dot(xj, bbi_ref[j])
            dar_ref[:, cols] += jnp.sum(gr * hpr + gi * hpi, axis=0, keepdims=True)
            dai_ref[:, cols] += jnp.sum(gi * hpr - gr * hpi, axis=0, keepdims=True)
            grb, gib = gr.astype(BF), gi.astype(BF)
            dcr_ref[j] += _dot(dyj, hr.astype(BF), 0, 0)
            dci_ref[j] += -_dot(dyj, hi.astype(BF), 0, 0)
            dbbr_ref[j] += _dot(xj, grb, 0, 0)
            dbbi_ref[j] += _dot(xj, gib, 0, 0)
            dx_ref[:, j * UB:(j + 1) * UB] = _dot(grb, bbr_ref[j], 1, 1) + _dot(gib, bbi_ref[j], 1, 1)

    state = lambda: pl.BlockSpec((TM, SW), lambda i: (order(i), 0))
    blockd = lambda: _full((NBLK, UB, LN))
    return _call(body, name=name, grid=(nt,),
                 in_specs=[pl.BlockSpec((TM, AW), lambda i: (order(i), 3)), state(), state(),
                           pl.BlockSpec((TM, AW), lambda i: (order(i), 0)),
                           blockd(), blockd(), _full((NBLK, LN, UB)), _full((NBLK, LN, UB)),
                           _full((PW_ROWS, SW)), _full((PW_ROWS, SW)), _full((1, SW)), _full((1, SW))],
                 out_specs=[pl.BlockSpec((TM, AW), lambda i: (order(i), 0)), _full((1, SW)), _full((1, SW)),
                            blockd(), blockd(), blockd(), blockd()],
                 out_shape=[jax.ShapeDtypeStruct((t, AW), F32), jax.ShapeDtypeStruct((1, SW), F32),
                            jax.ShapeDtypeStruct((1, SW), F32)] + [jax.ShapeDtypeStruct((NBLK, UB, LN), F32)] * 4,
                 scratch=[pltpu.VMEM((TM, SW), F32), pltpu.VMEM((TM, SW), F32),
                          pltpu.VMEM((2 * GROUPS_PER_TILE, LN), F32), pltpu.VMEM((2 * GROUPS_PER_TILE, LN), F32),
                          pltpu.VMEM((2 * LN // 128, TM, 128), F32),
                          pltpu.VMEM((1, SW), F32), pltpu.VMEM((1, SW), F32)])(
        p, hp_re, hp_im, dy, bb_re, bb_im, ct_re, ct_im, pw_re, pw_im_conj, a_re, a_im)


def s5_dx_sum(dy, dsk, dxf, dxb, name):
    t = dy.shape[0]

    def body(dy_ref, d_ref, f_ref, b_ref, o_ref):
        o_ref[...] = (dy_ref[...] * d_ref[...] + f_ref[...] + b_ref[...]).astype(BF)

    tile = pl.BlockSpec((TM, AW), lambda i: (i, 0))
    return _call(body, name=name, grid=(t // TM,), in_specs=[tile, _full((1, AW)), tile, tile], out_specs=tile,
                 out_shape=jax.ShapeDtypeStruct((t, AW), BF))(dy, dsk, dxf, dxb)


SCALE = HD ** -0.5
NHEAD_NORM = NQ + NKV


def _partner(x):
    half0 = (lax.broadcasted_iota(jnp.int32, (1, HD), 1) % 64) < 32
    return jnp.where(half0, pltpu.roll(x, HD - 32, 1), pltpu.roll(x, 32, 1))


def attn_prep(p, qg, kg, cos, sins, name):
    t = p.shape[0]

    def body(p_ref, qg_ref, kg_ref, cos_ref, sin_ref, o_ref):
        cv, sv = cos_ref[...], sin_ref[...]
        for h in range(NHEAD_NORM):
            cols = slice(h * HD, (h + 1) * HD)
            blk = p_ref[:, cols]
            r = lax.rsqrt(jnp.mean(blk * blk, axis=-1, keepdims=True) + EPS)
            xn = blk * r * (qg_ref[...] if h < NQ else kg_ref[...])
            rot = xn * cv + _partner(xn) * sv
            o_ref[:, cols] = ((rot * SCALE) if h < NQ else rot).astype(BF)
        vcols = slice(NHEAD_NORM * HD, (NHEAD_NORM + NKV) * HD)
        o_ref[:, vcols] = p_ref[:, vcols].astype(BF)

    w = (NHEAD_NORM + NKV) * HD
    tile = lambda ww: pl.BlockSpec((TM, ww), lambda i: (i, 0))
    return _call(body, name=name, grid=(t // TM,),
                 in_specs=[tile(w), _full((1, HD)), _full((1, HD)), tile(HD), tile(HD)],
                 out_specs=tile(w), out_shape=jax.ShapeDtypeStruct((t, w), BF))(p, qg, kg, cos, sins)


def attn_prep_bwd(p, dq, dk, dv, qg, kg, cos, sins, name):
    t = p.shape[0]

    def body(p_ref, dq_ref, dk_ref, dv_ref, qg_ref, kg_ref, cos_ref, sin_ref, o_ref, dqg_ref, dkg_ref):
        @pl.when(pl.program_id(0) == 0)
        def _():
            dqg_ref[...] = jnp.zeros_like(dqg_ref)
            dkg_ref[...] = jnp.zeros_like(dkg_ref)

        cv, sv = cos_ref[...], sin_ref[...]
        for h in range(NHEAD_NORM):
            cols = slice(h * HD, (h + 1) * HD)
            blk = p_ref[:, cols]
            r = lax.rsqrt(jnp.mean(blk * blk, axis=-1, keepdims=True) + EPS)
            xh = blk * r
            if h < NQ:
                drot = dq_ref[:, cols] * SCALE
                gv, dg_ref = qg_ref[...], dqg_ref
            else:
                drot = dk_ref[:, (h - NQ) * HD:(h - NQ + 1) * HD]
                gv, dg_ref = kg_ref[...], dkg_ref
            dxn = drot * cv + _partner(drot * sv)
            dg_ref[...] += jnp.sum(dxn * xh, axis=0, keepdims=True)
            dxh = dxn * gv
            o_ref[:, cols] = (r * (dxh - xh * jnp.mean(dxh * xh, axis=-1, keepdims=True))).astype(BF)
        o_ref[:, NHEAD_NORM * HD:(NHEAD_NORM + NKV) * HD] = dv_ref[...].astype(BF)

    w = (NHEAD_NORM + NKV) * HD
    tile = lambda ww: pl.BlockSpec((TM, ww), lambda i: (i, 0))
    return _call(body, name=name, grid=(t // TM,),
                 in_specs=[tile(w), tile(NQ * HD), tile(NKV * HD), tile(NKV * HD), _full((1, HD)), _full((1, HD)),
                           tile(HD), tile(HD)],
                 out_specs=[tile(w), _full((1, HD)), _full((1, HD))],
                 out_shape=[jax.ShapeDtypeStruct((t, w), BF), jax.ShapeDtypeStruct((1, HD), F32),
                            jax.ShapeDtypeStruct((1, HD), F32)])(p, dq, dk, dv, qg, kg, cos, sins)


KCOL = NQ
VCOL = NQ + NKV
GCOL = (NQ + 2 * NKV)
QPK = NQ // NKV


def attn_fwd(qkv, p, name):
    t = qkv.shape[0]

    def body(q_ref, k_ref, v_ref, g_ref, o_ref, mix_ref, lse_ref):
        def attend(nk):
            s = _dot(q_ref[...], k_ref[0:nk, :], 1, 1)
            m = jnp.max(s, axis=-1, keepdims=True)
            pe = jnp.exp(s - m)
            l = jnp.sum(pe, axis=-1, keepdims=True)
            o = _dot(pe.astype(BF), v_ref[0:nk, :]) / l
            gt = g_ref[...]
            o_ref[...] = o
            mix_ref[...] = (o * (gt * _sig(gt))).astype(BF)
            lse_ref[...] = jnp.broadcast_to(m + jnp.log(l), (TM, HD))

        pl.when(pl.program_id(1) == 0)(lambda: attend(NC))
        pl.when(pl.program_id(1) > 0)(lambda: attend(t))

    blk = pl.BlockSpec((TM, HD), lambda h, i: (i, h))
    return _call(body, name=name, grid=(NQ, t // TM),
                 in_specs=[blk, pl.BlockSpec((t, HD), lambda h, i: (0, KCOL + h // QPK)),
                           pl.BlockSpec((t, HD), lambda h, i: (0, VCOL + h // QPK)),
                           pl.BlockSpec((TM, HD), lambda h, i: (i, GCOL + h))],
                 out_specs=[blk, blk, blk],
                 out_shape=[jax.ShapeDtypeStruct((t, NQ * HD), F32), jax.ShapeDtypeStruct((t, NQ * HD), BF),
                            jax.ShapeDtypeStruct((t, NQ * HD), F32)])(qkv, qkv, qkv, p)


def attn_bwd(qkv, p, dmix, o, lse, name):
    t = qkv.shape[0]

    def body(q_ref, k_ref, v_ref, g_ref, dm_ref, o_ref, lse_ref, dq_ref, dg_ref, dk_ref, dv_ref):
        i = pl.program_id(2)

        @pl.when((pl.program_id(1) == 0) & (i == 0))
        def _():
            dk_ref[...] = jnp.zeros_like(dk_ref)
            dv_ref[...] = jnp.zeros_like(dv_ref)

        gt = g_ref[...]
        sg = _sig(gt)
        ov = o_ref[...]
        dmv = dm_ref[...]
        dg_ref[...] = (dmv * ov * (sg * (1.0 + gt * (1.0 - sg)))).astype(BF)
        do = dmv * (gt * sg)
        dr = jnp.sum(do * ov, axis=-1, keepdims=True)
        dob = do.astype(BF)

        def bwd(nk):
            q = q_ref[...]
            s = _dot(q, k_ref[0:nk, :], 1, 1)
            pe = jnp.exp(s - lse_ref[:, 0:1])
            dp = _dot(dob, v_ref[0:nk, :], 1, 1)
            dsb = (pe * (dp - dr)).astype(BF)
            dq_ref[...] = _dot(dsb, k_ref[0:nk, :])
            dv_ref[0:nk, :] += _dot(pe.astype(BF), dob, 0, 0)
            dk_ref[0:nk, :] += _dot(dsb, q, 0, 0)

        pl.when(i == 0)(lambda: bwd(NC))
        pl.when(i > 0)(lambda: bwd(t))

    blk = pl.BlockSpec((TM, HD), lambda kv, g, i: (i, kv * QPK + g))
    acc = pl.BlockSpec((t, HD), lambda kv, g, i: (0, kv))
    return _call(body, name=name, grid=(NKV, QPK, t // TM),
                 in_specs=[blk, pl.BlockSpec((t, HD), lambda kv, g, i: (0, KCOL + kv)),
                           pl.BlockSpec((t, HD), lambda kv, g, i: (0, VCOL + kv)),
                           pl.BlockSpec((TM, HD), lambda kv, g, i: (i, GCOL + kv * QPK + g)), blk, blk, blk],
                 out_specs=[blk, blk, acc, acc],
                 out_shape=[jax.ShapeDtypeStruct((t, NQ * HD), F32), jax.ShapeDtypeStruct((t, NQ * HD), BF),
                            jax.ShapeDtypeStruct((t, NKV * HD), F32), jax.ShapeDtypeStruct((t, NKV * HD), F32)])(
        qkv, qkv, qkv, p, dmix, o, lse)


def _row_tile(rows, row_bytes, cap=2 * 1024 * 1024):
    if rows * row_bytes <= cap or rows % 8:
        return rows
    tr = rows
    while tr * row_bytes > cap and tr % 16 == 0:
        tr //= 2
    return tr


def adamw(w, g, m, v, name):
    r, cdim = w.shape
    tr = _row_tile(r, 4 * max(cdim, 128))

    def body(w_ref, g_ref, m_ref, v_ref, d_ref, nm_ref, nv_ref):
        gv = g_ref[...]
        m2 = ADAM_B1 * m_ref[...] + (1.0 - ADAM_B1) * gv
        v2 = ADAM_B2 * v_ref[...] + (1.0 - ADAM_B2) * (gv * gv)
        mh = m2 / (1.0 - ADAM_B1 ** ADAM_STEP)
        vh = v2 / (1.0 - ADAM_B2 ** ADAM_STEP)
        d_ref[...] = -ADAM_LR * (mh / (jnp.sqrt(vh) + ADAM_EPS) + ADAM_WD * w_ref[...])
        nm_ref[...] = m2
        nv_ref[...] = v2

    tile = pl.BlockSpec((tr, cdim), lambda i: (i, 0))
    sh = jax.ShapeDtypeStruct((r, cdim), F32)
    return _call(body, name=name, grid=(r // tr,), in_specs=[tile] * 4, out_specs=[tile] * 3,
                 out_shape=[sh, sh, sh])(w, g, m, v)


def sum_lead(a, name, out_dtype=F32):
    n, r, cdim = a.shape
    tr = _row_tile(r, 4 * n * max(cdim, 128))

    def body(a_ref, o_ref):
        acc = a_ref[0].astype(F32)
        for k in range(1, n):
            acc = acc + a_ref[k].astype(F32)
        o_ref[...] = acc.astype(o_ref.dtype)

    return _call(body, name=name, grid=(r // tr,),
                 in_specs=[pl.BlockSpec((n, tr, cdim), lambda i: (0, i, 0))],
                 out_specs=pl.BlockSpec((tr, cdim), lambda i: (i, 0)),
                 out_shape=jax.ShapeDtypeStruct((r, cdim), out_dtype))(a)


_FLIPS = {"xy": [(1, 0, 0), (0, 1, 0), (1, 1, 0)], "c": [(0, 0, 1)],
          "all": [(0, 0, 1), (0, 1, 0), (0, 1, 1), (1, 0, 0), (1, 0, 1), (1, 1, 0), (1, 1, 1)]}
_GROUP_SIZE = {"xy": 4, "c": 2, "all": 8}


def _group_index(group, x, y, c):
    return {"xy": 2 * x + y, "c": c, "all": 4 * x + 2 * y + c}[group]


def exchange(items, name):
    plan = []
    for arr, group, kind in items:
        chunk = arr.shape if kind == "gather" else arr.shape[1:]
        plan.append((group, kind, chunk))
    ncopy = sum(len(_FLIPS[g]) for g, _, _ in plan)
    nitem = len(plan)

    def body(*refs):
        srcs, dsts = refs[:nitem], refs[nitem:2 * nitem]
        send_sems, recv_sems, local_sems = refs[2 * nitem:]
        x, y, c = lax.axis_index("x"), lax.axis_index("y"), lax.axis_index("c")
        sends, recvs, locals_ = [], [], []
        n = 0
        for k, (group, kind, _) in enumerate(plan):
            me = _group_index(group, x, y, c)
            own = srcs[k] if kind == "gather" else srcs[k].at[me]
            locals_.append(pltpu.make_async_copy(own, dsts[k].at[me], local_sems.at[k]))
            for fx, fy, fc in _FLIPS[group]:
                px, py, pc = (1 - x if fx else x), (1 - y if fy else y), (1 - c if fc else c)
                peer = _group_index(group, px, py, pc)
                src = srcs[k] if kind == "gather" else srcs[k].at[peer]
                sends.append(pltpu.make_async_remote_copy(
                    src_ref=src, dst_ref=dsts[k].at[me], send_sem=send_sems.at[n], recv_sem=recv_sems.at[n],
                    device_id=(px, py, pc), device_id_type=MESH))
                recvs.append(pltpu.make_async_remote_copy(
                    src_ref=src, dst_ref=dsts[k].at[peer], send_sem=send_sems.at[n], recv_sem=recv_sems.at[n],
                    device_id=(px, py, pc), device_id_type=MESH))
                n += 1
        for cp in locals_ + sends:
            cp.start()
        for cp in recvs:
            cp.wait_recv()
        for cp in sends:
            cp.wait_send()
        for cp in locals_:
            cp.wait()

    anyspec = pl.BlockSpec(memory_space=pl.ANY)
    outs = [jax.ShapeDtypeStruct((_GROUP_SIZE[g],) + tuple(chunk), arr.dtype)
            for (arr, _, _), (g, _, chunk) in zip(items, plan)]
    res = pl.pallas_call(
        body, name=name, out_shape=outs, in_specs=[anyspec] * nitem, out_specs=[anyspec] * nitem,
        scratch_shapes=[pltpu.SemaphoreType.DMA((ncopy,)), pltpu.SemaphoreType.DMA((ncopy,)),
                        pltpu.SemaphoreType.DMA((nitem,))],
        compiler_params=pltpu.CompilerParams(has_side_effects=True))(*[a for a, _, _ in items])
    return list(res)


_SMALL = ["c_ctx", "norm_g", "b_mod", "gm_v_g", "gm_w_s", "gm_b_s", "s5_lam_re", "s5_lam_im", "s5_log_dt",
          "s5_b_re", "s5_b_im", "s5_c_re", "s5_c_im", "s5_d", "s5_b_glu", "q_norm_g", "k_norm_g", "final_g"]
_BIG = ["we_in", "we_out", "s5_w_glu", "wo_in", "wo_out"]
_WEIGHTS = ["c_ctx", "norm_g", "w_mod", "b_mod", "we_in", "we_out", "gm_v_g", "gm_w_s", "gm_b_s", "s5_lam_re",
            "s5_lam_im", "s5_log_dt", "s5_b_re", "s5_b_im", "s5_c_re", "s5_c_im", "s5_d", "s5_w_glu", "s5_b_glu",
            "wo_in", "wo_out", "q_norm_g", "k_norm_g", "final_g"]
_SMALL_ALIGN = 8 * 8 * 128


def _rope_tables(n_lat):
    rows = n_lat // GRID_W
    row = jnp.repeat(jnp.arange(rows), GRID_W)
    col = jnp.tile(jnp.arange(GRID_W), rows)
    freqs = ROPE_THETA ** (-jnp.arange(HD // 4, dtype=F32) / (HD // 4))
    ar, ac = row[:, None] * freqs, col[:, None] * freqs
    cos = jnp.concatenate([jnp.cos(ar), jnp.cos(ar), jnp.cos(ac), jnp.cos(ac)], axis=1)
    sins = jnp.concatenate([-jnp.sin(ar), jnp.sin(ar), -jnp.sin(ac), jnp.sin(ac)], axis=1)
    cos = jnp.concatenate([jnp.ones((NC, HD), F32), cos], axis=0)
    sins = jnp.concatenate([jnp.zeros((NC, HD), F32), sins], axis=0)
    return cos, sins


def _block_diag(v, transpose):
    gpb = SG // NBLK
    v = v.reshape(2, NBLK, gpb, SH, SP)
    eye = jnp.eye(gpb, dtype=v.dtype)
    if transpose:
        return jnp.einsum("djahp,ab->djapbh", v, eye).reshape(2, NBLK, LN, UB)
    return jnp.einsum("djahp,ab->djahbp", v, eye).reshape(2, NBLK, UB, LN)


def _diag_blocks(m):
    gpb = SG // NBLK
    return jnp.einsum("jahap->jahp", m.reshape(NBLK, gpb, SH, gpb, SP)).reshape(SG, SH, SP)


def _view2d(a):
    if a.ndim == 1:
        return a.reshape(1, -1)
    if a.shape[-1] < 64 and a.size % 1024 == 0:
        return a.reshape(-1, 1024)
    return a.reshape(-1, a.shape[-1])


def kernel(x, c, ctx, c_ctx, norm_g, w_mod, b_mod, we_in, we_out, gm_v_g, gm_w_s, gm_b_s, s5_lam_re, s5_lam_im, s5_log_dt, s5_b_re, s5_b_im, s5_c_re, s5_c_im, s5_d, s5_w_glu, s5_b_glu, wo_in, wo_out, q_norm_g, k_norm_g, final_g, loss_target, m_c_ctx, m_norm_g, m_w_mod, m_b_mod, m_we_in, m_we_out, m_gm_v_g, m_gm_w_s, m_gm_b_s, m_s5_lam_re, m_s5_lam_im, m_s5_log_dt, m_s5_b_re, m_s5_b_im, m_s5_c_re, m_s5_c_im, m_s5_d, m_s5_w_glu, m_s5_b_glu, m_wo_in, m_wo_out, m_q_norm_g, m_k_norm_g, m_final_g, v_c_ctx, v_norm_g, v_w_mod, v_b_mod, v_we_in, v_we_out, v_gm_v_g, v_gm_w_s, v_gm_b_s, v_s5_lam_re, v_s5_lam_im, v_s5_log_dt, v_s5_b_re, v_s5_b_im, v_s5_c_re, v_s5_c_im, v_s5_d, v_s5_w_glu, v_s5_b_glu, v_wo_in, v_wo_out, v_q_norm_g, v_k_norm_g, v_final_g):
    weights = dict(c_ctx=c_ctx, norm_g=norm_g, w_mod=w_mod, b_mod=b_mod, we_in=we_in, we_out=we_out, gm_v_g=gm_v_g,
                   gm_w_s=gm_w_s, gm_b_s=gm_b_s, s5_lam_re=s5_lam_re, s5_lam_im=s5_lam_im, s5_log_dt=s5_log_dt,
                   s5_b_re=s5_b_re, s5_b_im=s5_b_im, s5_c_re=s5_c_re, s5_c_im=s5_c_im, s5_d=s5_d, s5_w_glu=s5_w_glu,
                   s5_b_glu=s5_b_glu, wo_in=wo_in, wo_out=wo_out, q_norm_g=q_norm_g, k_norm_g=k_norm_g,
                   final_g=final_g)
    mom_m = dict(c_ctx=m_c_ctx, norm_g=m_norm_g, w_mod=m_w_mod, b_mod=m_b_mod, we_in=m_we_in, we_out=m_we_out,
                 gm_v_g=m_gm_v_g, gm_w_s=m_gm_w_s, gm_b_s=m_gm_b_s, s5_lam_re=m_s5_lam_re, s5_lam_im=m_s5_lam_im,
                 s5_log_dt=m_s5_log_dt, s5_b_re=m_s5_b_re, s5_b_im=m_s5_b_im, s5_c_re=m_s5_c_re, s5_c_im=m_s5_c_im,
                 s5_d=m_s5_d, s5_w_glu=m_s5_w_glu, s5_b_glu=m_s5_b_glu, wo_in=m_wo_in, wo_out=m_wo_out,
                 q_norm_g=m_q_norm_g, k_norm_g=m_k_norm_g, final_g=m_final_g)
    mom_v = dict(c_ctx=v_c_ctx, norm_g=v_norm_g, w_mod=v_w_mod, b_mod=v_b_mod, we_in=v_we_in, we_out=v_we_out,
                 gm_v_g=v_gm_v_g, gm_w_s=v_gm_w_s, gm_b_s=v_gm_b_s, s5_lam_re=v_s5_lam_re, s5_lam_im=v_s5_lam_im,
                 s5_log_dt=v_s5_log_dt, s5_b_re=v_s5_b_re, s5_b_im=v_s5_b_im, s5_c_re=v_s5_c_re, s5_c_im=v_s5_c_im,
                 s5_d=v_s5_d, s5_w_glu=v_s5_w_glu, s5_b_glu=v_s5_b_glu, wo_in=v_wo_in, wo_out=v_wo_out,
                 q_norm_g=v_q_norm_g, k_norm_g=v_k_norm_g, final_g=v_final_g)

    ixy = 2 * lax.axis_index("x") + lax.axis_index("y")
    n_lat = x.shape[1]
    nl = norm_g.shape[0]
    nmod = w_mod.shape[2]
    xin = jnp.concatenate([ctx[0], x[0]], axis=0)

    ic = lax.axis_index("c")
    mine = [lax.dynamic_index_in_dim(weights[n], ic, 0, keepdims=False).astype(BF) for n in _BIG]
    got = exchange([(m_, "xy", "gather") for m_ in mine] + [(c, "xy", "gather")], "gather_weights")
    both = exchange([(g_, "c", "gather") for g_ in got[:len(_BIG)]], "swap_weights")
    wein = [both[0][0], both[0][1]]
    weout = [both[1][l].reshape(1, D, D) for l in range(2)]
    wglu = [both[2][l].reshape(AW, AW) for l in range(2)]
    woin = [both[3][0], both[3][1]]
    woout = [both[4][l].reshape(1, D, D) for l in range(2)]
    c_group = got[len(_BIG)].reshape(4, D)

    cond = jnp.concatenate([c_group, jnp.broadcast_to(c_ctx.reshape(1, D), (4, D))], axis=0)
    b_shard = lax.dynamic_slice(b_mod, (0, ixy * nmod), (nl, nmod)).reshape(nl, 1, nmod)
    mpart = ada_fwd(cond, w_mod, b_shard)
    m_lat, m_ctx = exchange([(jnp.transpose(mpart[:, 0:4], (1, 0, 2)), "xy", "scatter"),
                             (mpart[:, 4], "xy", "gather")], "exchange_mod")
    m_lat = jnp.transpose(m_lat, (1, 0, 2)).reshape(nl, 3, D)
    m_ctx = jnp.transpose(m_ctx, (1, 0, 2)).reshape(nl, 3, D)
    mods = [jnp.stack([m_ctx[l], m_lat[l]], axis=0) for l in range(nl)]

    loss_part, dx, g, d_norm_g, d_mod_lat, d_mod_ctx, d_final_g = _local_step(
        xin, loss_target[0], mods, wein, weout, wglu, woin, woout, weights)
    grad_x = dx[NC:].reshape(1, n_lat, D)

    d_mod_lat, d_mod_ctx = jnp.stack(d_mod_lat), jnp.stack(d_mod_ctx)
    dm_send = jnp.stack([d_mod_lat.reshape(nl, 4, nmod), d_mod_ctx.reshape(nl, 4, nmod)])
    (dm_got,) = exchange([(jnp.transpose(dm_send, (2, 0, 1, 3)), "xy", "scatter")], "exchange_dmod")
    dm_rows = jnp.concatenate([dm_got[:, 0], dm_got[:, 1]], axis=0)
    gw_mod, d_cctx = ada_bwd(cond, jnp.transpose(dm_rows, (1, 0, 2)), w_mod)
    g_small = dict(c_ctx=d_cctx.reshape(D), norm_g=jnp.stack(d_norm_g), b_mod=add2(d_mod_lat, d_mod_ctx, "add_dbmod"),
                   final_g=d_final_g.reshape(D))
    for name in _SMALL:
        if name not in g_small:
            g_small[name] = jnp.stack(g[name])

    flat = jnp.concatenate([g_small[n].reshape(-1) for n in _SMALL])
    nflat = flat.shape[0]
    npad = -(-nflat // _SMALL_ALIGN) * _SMALL_ALIGN
    flat = jnp.concatenate([flat, jnp.zeros((npad - nflat,), F32)]).reshape(8, npad // (8 * 128), 128)
    items = [(flat, "all", "scatter"), (gw_mod.reshape(2, nl // 2 * D, nmod), "c", "scatter")]
    for name in _BIG:
        st = jnp.stack(g[name]) if isinstance(g[name], list) else g[name]
        items.append((st.reshape(2, -1, st.shape[-1]), "c", "scatter"))
    parts = exchange(items, "reduce_chip")
    sums = [sum_lead(pt, f"sum_chip{k}", BF if k >= 2 else F32) for k, pt in enumerate(parts)]
    parts = exchange([(s_.reshape(4, s_.shape[0] // 4, s_.shape[1]), "xy", "scatter") for s_ in sums[2:]], "reduce_scatter")
    sums = sums[:2] + [sum_lead(pt, f"sum_shard{k}") for k, pt in enumerate(parts)]
    items = [(sums[0], "all", "gather"), (sums[1], "c", "gather")] + [(s_, "c", "gather") for s_ in sums[2:]]
    full = exchange(items, "all_gather")
    flat = full[0].reshape(-1)
    grads = {}
    off = 0
    for name in _SMALL:
        sz = weights[name].size
        grads[name] = flat[off:off + sz].reshape(weights[name].shape)
        off += sz
    grads["w_mod"] = full[1].reshape(w_mod.shape)
    for k, name in enumerate(_BIG):
        grads[name] = full[2 + k].reshape(weights[name].shape)

    delta, new_m, new_v = {}, {}, {}
    for name in _WEIGHTS:
        w2 = _view2d(weights[name])
        d2, m2, v2 = adamw(w2, grads[name].reshape(w2.shape), mom_m[name].reshape(w2.shape),
                           mom_v[name].reshape(w2.shape), f"adamw_{name}")
        shp = weights[name].shape
        delta[name], new_m[name], new_v[name] = d2.reshape(shp), m2.reshape(shp), v2.reshape(shp)

    loss = lax.psum(loss_part[0, 0], ("x", "y", "c"))
    return (loss, grad_x, *[grads[n] for n in _WEIGHTS], *[delta[n] for n in _WEIGHTS],
            *[new_m[n] for n in _WEIGHTS], *[new_v[n] for n in _WEIGHTS])


def _local_step(xin, target, mods, wein, weout, wglu, woin, woout, w):
    norm_g, gm_v_g, gm_w_s, gm_b_s = w["norm_g"], w["gm_v_g"], w["gm_w_s"], w["gm_b_s"]
    s5_lam_re, s5_lam_im, s5_log_dt = w["s5_lam_re"], w["s5_lam_im"], w["s5_log_dt"]
    s5_b_re, s5_b_im, s5_c_re, s5_c_im = w["s5_b_re"], w["s5_b_im"], w["s5_c_re"], w["s5_c_im"]
    s5_d, s5_b_glu, q_norm_g, k_norm_g, final_g = w["s5_d"], w["s5_b_glu"], w["q_norm_g"], w["k_norm_g"], w["final_g"]
    nl = norm_g.shape[0]
    n_lat = xin.shape[0] - NC

    cos, sins = _rope_tables(n_lat)

    s5p = []
    for i in range(2):
        lam_l = (s5_lam_re[i].reshape(2, SW), s5_lam_im[i].reshape(2, SW),
                 jnp.repeat(s5_log_dt[i], SP, axis=1))
        lam_r = (jnp.repeat(s5_lam_re[i].reshape(2 * SG, SP), SH, axis=0),
                 jnp.repeat(s5_lam_im[i].reshape(2 * SG, SP), SH, axis=0),
                 jnp.repeat(s5_log_dt[i].reshape(2 * SG, 1), SH, axis=0))
        b_r = (jnp.transpose(s5_b_re[i], (0, 1, 3, 2)).reshape(2 * SG * SH, SP),
               jnp.transpose(s5_b_im[i], (0, 1, 3, 2)).reshape(2 * SG * SH, SP))
        pw_re, pw_im, bbr, bbi = s5_disc(*lam_l, *lam_r, *b_r)
        s5p.append(dict(
            lam_r=lam_r, b_r=b_r, pw_re=pw_re, pw_im=pw_im,
            bb_re=_block_diag(bbr.reshape(2, SG, SH, SP), False).astype(BF),
            bb_im=_block_diag(bbi.reshape(2, SG, SH, SP), False).astype(BF),
            ct_re=_block_diag(s5_c_re[i], True).astype(BF), ct_im=_block_diag(s5_c_im[i], True).astype(BF)))

    saved = []
    xcur = xin
    for l in range(nl):
        i = l // 2
        h = pro_fwd(xcur, norm_g[l].reshape(1, D), mods[l], f"pro_fwd{l}")
        sv = dict(x=xcur, h=h)
        if l % 2 == 0:
            p = mm_nn(h, wein[i], f"in_proj{l}")
            sp = s5p[i]
            for dr, rev in ((0, False), (1, True)):
                sv[f"y{dr}"], sv[f"hpr{dr}"], sv[f"hpi{dr}"] = s5_fwd(
                    p, sp["bb_re"][dr], sp["bb_im"][dr], sp["ct_re"][dr], sp["ct_im"][dr],
                    sp["pw_re"][dr], sp["pw_im"][dr], rev, f"s5_fwd{l}_{dr}")
            mix = mix_fwd(p, sv["y0"], sv["y1"], gm_v_g[i].reshape(1, AW), gm_w_s[i].astype(BF),
                          gm_b_s[i].reshape(NGRP, CHUNK, 1), s5_d[i].reshape(1, AW), wglu[i],
                          s5_b_glu[i].reshape(1, AW), f"mix_fwd{l}")
            o = mm_nn(mix, weout[i], f"out_proj{l}")
        else:
            p = mm_nn(h, woin[i], f"in_proj{l}")
            sv["qkv"] = attn_prep(p, q_norm_g[i].reshape(1, HD), k_norm_g[i].reshape(1, HD), cos, sins, f"attn_prep{l}")
            sv["o_att"], mix, sv["lse"] = attn_fwd(sv["qkv"], p, f"attn_fwd{l}")
            o = mm_nn(mix, woout[i], f"out_proj{l}")
        sv.update(p=p, mix=mix, o=o)
        saved.append(sv)
        xcur = res_fwd(xcur, o, mods[l], l < nl - 1, f"res_fwd{l}")

    loss_part, dx, d_final_g = final_loss(xcur, target, final_g.reshape(1, D))

    g = {}
    gbuf = {}
    d_norm_g, d_mod_lat, d_mod_ctx = [None] * nl, [None] * nl, [None] * nl
    for name in ("s5_w_glu", "gm_v_g", "gm_w_s", "gm_b_s", "s5_lam_re", "s5_lam_im",
                 "s5_log_dt", "s5_b_re", "s5_b_im", "s5_c_re", "s5_c_im", "s5_d", "s5_b_glu", "q_norm_g", "k_norm_g"):
        g[name] = [None, None]
    for l in reversed(range(nl)):
        i = l // 2
        sv = saved[l]
        do, dgt = res_bwd(dx, sv["o"], mods[l], l < nl - 1, f"res_bwd{l}")
        w_out = weout[i] if l % 2 == 0 else woout[i]
        dmix = mm_nt(do, w_out, f"out_dgrad{l}")
        out_name, in_name = ("we_out", "we_in") if l % 2 == 0 else ("wo_out", "wo_in")
        gbuf[out_name] = mm_tn(sv["mix"], do, 1, f"out_wgrad{l}", slot=i, into=gbuf.get(out_name))
        if l % 2 == 0:
            sp = s5p[i]
            (dpa, dpgb, dy, g["gm_w_s"][i], dbs, dvg, dd, g["s5_w_glu"][i], dbg) = mix_bwd(
                sv["p"], sv["y0"], sv["y1"], dmix, gm_v_g[i].reshape(1, AW), gm_w_s[i].astype(BF),
                gm_b_s[i].reshape(NGRP, CHUNK, 1), s5_d[i].reshape(1, AW), wglu[i], s5_b_glu[i].reshape(1, AW),
                f"mix_bwd{l}")
            g["gm_b_s"][i], g["gm_v_g"][i] = dbs.reshape(NGRP, CHUNK), dvg.reshape(AW)
            g["s5_d"][i], g["s5_b_glu"][i] = dd.reshape(AW), dbg.reshape(AW)
            g["s5_w_glu"][i] = g["s5_w_glu"][i].reshape(4, AW // 4, AW)
            dxd, das_r, das_i, dbbs_r, dbbs_i, dcs_r, dcs_i = [], [], [], [], [], [], []
            for dr, rev in ((0, False), (1, True)):
                dxs_d, da_r, da_i, dbb_r, dbb_i, dc_r, dc_i = s5_bwd(
                    sv["p"], sv[f"hpr{dr}"], sv[f"hpi{dr}"], dy, sp["bb_re"][dr], sp["bb_im"][dr],
                    sp["ct_re"][dr], sp["ct_im"][dr], sp["pw_re"][dr], -sp["pw_im"][dr],
                    sp["pw_re"][dr][0:1], sp["pw_im"][dr][0:1], rev, f"s5_bwd{l}_{dr}")
                dxd.append(dxs_d)
                das_r.append(jnp.repeat(da_r.reshape(SG, SP), SH, axis=0))
                das_i.append(jnp.repeat(da_i.reshape(SG, SP), SH, axis=0))
                dbbs_r.append(_diag_blocks(dbb_r).reshape(SG * SH, SP))
                dbbs_i.append(_diag_blocks(dbb_i).reshape(SG * SH, SP))
                dcs_r.append(_diag_blocks(dc_r))
                dcs_i.append(_diag_blocks(dc_i))
            cat = lambda parts: jnp.concatenate(parts, axis=0)
            dlr, dli, dldt, dbr, dbi = s5_param_bwd(*sp["lam_r"], *sp["b_r"], cat(das_r), cat(das_i),
                                                    cat(dbbs_r), cat(dbbs_i))
            g["s5_lam_re"][i], g["s5_lam_im"][i] = dlr.reshape(2, SG, SP), dli.reshape(2, SG, SP)
            g["s5_log_dt"][i] = dldt.reshape(2, SG)
            g["s5_b_re"][i] = jnp.transpose(dbr.reshape(2, SG, SH, SP), (0, 1, 3, 2))
            g["s5_b_im"][i] = jnp.transpose(dbi.reshape(2, SG, SH, SP), (0, 1, 3, 2))
            g["s5_c_re"][i], g["s5_c_im"][i] = jnp.stack(dcs_r), jnp.stack(dcs_i)
            dxs = s5_dx_sum(dy, s5_d[i].reshape(1, AW), dxd[0], dxd[1], f"s5_dx_sum{l}")
            dp = jnp.concatenate([dpa, dxs, dpgb], axis=1)
            w_in = wein[i]
        else:
            dq, dgate, dk, dv = attn_bwd(sv["qkv"], sv["p"], dmix, sv["o_att"], sv["lse"], f"attn_bwd{l}")
            dpqkv, dqg, dkg = attn_prep_bwd(sv["p"], dq, dk, dv, q_norm_g[i].reshape(1, HD),
                                            k_norm_g[i].reshape(1, HD), cos, sins, f"attn_prep_bwd{l}")
            g["q_norm_g"][i], g["k_norm_g"][i] = dqg.reshape(HD), dkg.reshape(HD)
            dp = jnp.concatenate([dpqkv, dgate], axis=1)
            w_in = woin[i]
        dh = mm_nt(dp, w_in, f"in_dgrad{l}")
        gbuf[in_name] = mm_tn(sv["h"], dp, 4, f"in_wgrad{l}", slot=i, into=gbuf.get(in_name))
        dx, dmod2, dng = pro_bwd(sv["x"], dh, dx, norm_g[l].reshape(1, D), mods[l], f"pro_bwd{l}")
        d_norm_g[l] = dng.reshape(D)
        d_mod_ctx[l] = jnp.concatenate([dmod2[0, 0], dmod2[0, 1], dgt[0]])
        d_mod_lat[l] = jnp.concatenate([dmod2[1, 0], dmod2[1, 1], dgt[1]])
    g.update(gbuf)
    return loss_part, dx, g, d_norm_g, d_mod_lat, d_mod_ctx, d_final_g
```

```python
import functools
import math

import numpy as np
import jax
import jax.numpy as jnp
from jax import lax
from jax.experimental import pallas as pl
from jax.experimental.pallas import tpu as pltpu

F32 = jnp.float32
BF = jnp.bfloat16
MESH = pl.DeviceIdType.MESH

D = 1024
NC = 256
SEQ = 4096
GRID_W = 64
TM = 256
CHUNK = 128
EPS = 1e-6
HD = 128
NQ = 8
NKV = 2
ROPE_THETA = 10000.0
SG = 32
SP = 64
SH = 16
SW = SG * SP
GELU_K = math.sqrt(2.0 / math.pi)
GELU_C = 0.044715
VMEM_LIMIT_BYTES = 56 * 1024 * 1024

ADAM_LR = 0.001
ADAM_B1 = 0.9
ADAM_B2 = 0.999
ADAM_EPS = 1e-08
ADAM_WD = 0.01
ADAM_STEP = 10


def _call(body, *, name, out_shape, grid=None, in_specs=None, out_specs=None, scratch=()):
    kw = {}
    if grid is not None:
        kw["grid"] = grid
    if in_specs is not None:
        kw["in_specs"] = in_specs
    if out_specs is not None:
        kw["out_specs"] = out_specs
    return pl.pallas_call(
        body, name=name, out_shape=out_shape, scratch_shapes=list(scratch),
        compiler_params=pltpu.CompilerParams(vmem_limit_bytes=VMEM_LIMIT_BYTES), **kw)


def _dot(a, b, ca=1, cb=0):
    return lax.dot_general(a, b, (((ca,), (cb,)), ((), ())), preferred_element_type=F32)


def _sig(x):
    return 1.0 / (1.0 + jnp.exp(-x))


def _full(shape):
    n = len(shape)
    return pl.BlockSpec(shape, lambda *_: (0,) * n)


def _mm_rows(t):
    for rows in (1088, 1024, 768, 512, 256):
        if t % rows == 0:
            return rows
    raise ValueError(t)


def mm_nn(a, w3, name, out_dtype=F32):
    t, k = a.shape
    j, _, nb = w3.shape
    tr = _mm_rows(t)

    def body(a_ref, w_ref, o_ref):
        o_ref[...] = _dot(a_ref[...], w_ref[0]).astype(o_ref.dtype)

    return _call(body, name=name, grid=(j, t // tr),
                 in_specs=[pl.BlockSpec((tr, k), lambda jj, i: (i, 0)),
                           pl.BlockSpec((1, k, nb), lambda jj, i: (jj, 0, 0))],
                 out_specs=pl.BlockSpec((tr, nb), lambda jj, i: (i, jj)),
                 out_shape=jax.ShapeDtypeStruct((t, j * nb), out_dtype))(a, w3)


def mm_nt(a, w3, name, out_dtype=F32):
    t, _ = a.shape
    j, k, nb = w3.shape
    tr = _mm_rows(t)

    def body(a_ref, w_ref, o_ref):
        acc = _dot(a_ref[:, 0:nb], w_ref[0], 1, 1)
        for jj in range(1, j):
            acc = acc + _dot(a_ref[:, jj * nb:(jj + 1) * nb], w_ref[jj], 1, 1)
        o_ref[...] = acc.astype(o_ref.dtype)

    return _call(body, name=name, grid=(t // tr,),
                 in_specs=[pl.BlockSpec((tr, j * nb), lambda i: (i, 0)), _full((j, k, nb))],
                 out_specs=pl.BlockSpec((tr, k), lambda i: (i, 0)),
                 out_shape=jax.ShapeDtypeStruct((t, k), out_dtype))(a, w3)


def mm_tn(a, b, j, name, slot=0, into=None):
    t, m = a.shape
    nb = b.shape[1] // j
    tr = _mm_rows(t)

    def body(a_ref, b_ref, *rest):
        o_ref = rest[-1]

        @pl.when(pl.program_id(1) == 0)
        def _():
            o_ref[...] = jnp.zeros_like(o_ref)
        o_ref[0, 0] += _dot(a_ref[...], b_ref[...], 0, 0)

    in_specs = [pl.BlockSpec((tr, m), lambda jj, i: (i, 0)), pl.BlockSpec((tr, nb), lambda jj, i: (i, jj))]
    args = [a, b]
    alias = {}
    if into is not None:
        in_specs.append(pl.BlockSpec(memory_space=pl.ANY))
        args.append(into)
        alias = {2: 0}
    return pl.pallas_call(
        body, name=name, grid=(j, t // tr), in_specs=in_specs,
        out_specs=pl.BlockSpec((1, 1, m, nb), lambda jj, i: (slot, jj, 0, 0)),
        out_shape=jax.ShapeDtypeStruct((2, j, m, nb), F32), input_output_aliases=alias,
        compiler_params=pltpu.CompilerParams(vmem_limit_bytes=VMEM_LIMIT_BYTES))(*args)


def _mod_rows(mod_ref, i):
    ctx = i == 0
    sh = jnp.where(ctx, mod_ref[0, 0:1, :], mod_ref[1, 0:1, :])
    sc = jnp.where(ctx, mod_ref[0, 1:2, :], mod_ref[1, 1:2, :])
    gt = jnp.where(ctx, mod_ref[0, 2:3, :], mod_ref[1, 2:3, :])
    return sh, sc, gt


def pro_fwd(x, g, mod, name):
    t = x.shape[0]

    def body(x_ref, g_ref, mod_ref, h_ref):
        sh, sc, _ = _mod_rows(mod_ref, pl.program_id(0))
        xv = x_ref[...]
        r = lax.rsqrt(jnp.mean(xv * xv, axis=-1, keepdims=True) + EPS)
        h_ref[...] = ((xv * r) * g_ref[...] * (1.0 + sc) + sh).astype(BF)

    return _call(body, name=name, grid=(t // TM,),
                 in_specs=[pl.BlockSpec((TM, D), lambda i: (i, 0)), _full((1, D)), _full((2, 3, D))],
                 out_specs=pl.BlockSpec((TM, D), lambda i: (i, 0)),
                 out_shape=jax.ShapeDtypeStruct((t, D), BF))(x, g, mod)


def pro_bwd(x, dh, dxn, g, mod, name):
    t = x.shape[0]

    def body(x_ref, dh_ref, dxn_ref, g_ref, mod_ref, dx_ref, dmod_ref, dg_ref):
        i = pl.program_id(0)

        @pl.when(i == 0)
        def _():
            dmod_ref[...] = jnp.zeros_like(dmod_ref)
            dg_ref[...] = jnp.zeros_like(dg_ref)

        _, sc, _ = _mod_rows(mod_ref, i)
        xv = x_ref[...]
        gv = g_ref[...]
        r = lax.rsqrt(jnp.mean(xv * xv, axis=-1, keepdims=True) + EPS)
        xn = xv * r
        dh_v = dh_ref[...]
        e = dh_v * (1.0 + sc)
        dsh = jnp.sum(dh_v, axis=0, keepdims=True)
        dsc = jnp.sum(dh_v * xn * gv, axis=0, keepdims=True)
        dg_ref[...] += jnp.sum(e * xn, axis=0, keepdims=True)
        dxh = e * gv
        dx_ref[...] = dxn_ref[...] + r * (dxh - xn * jnp.mean(dxh * xn, axis=-1, keepdims=True))

        @pl.when(i == 0)
        def _():
            dmod_ref[0, 0:1, :] += dsh
            dmod_ref[0, 1:2, :] += dsc

        @pl.when(i > 0)
        def _():
            dmod_ref[1, 0:1, :] += dsh
            dmod_ref[1, 1:2, :] += dsc

    tile = pl.BlockSpec((TM, D), lambda i: (i, 0))
    return _call(body, name=name, grid=(t // TM,),
                 in_specs=[tile, tile, tile, _full((1, D)), _full((2, 3, D))],
                 out_specs=[tile, _full((2, 2, D)), _full((1, D))],
                 out_shape=[jax.ShapeDtypeStruct((t, D), F32), jax.ShapeDtypeStruct((2, 2, D), F32),
                            jax.ShapeDtypeStruct((1, D), F32)])(x, dh, dxn, g, mod)


def res_fwd(x, o, mod, update_ctx, name):
    t = x.shape[0]

    def body(x_ref, o_ref, mod_ref, y_ref):
        i = pl.program_id(0)
        _, _, gt = _mod_rows(mod_ref, i)
        upd = x_ref[...] + gt * o_ref[...]
        if update_ctx:
            y_ref[...] = upd
        else:
            y_ref[...] = jnp.where(i == 0, x_ref[...], upd)

    tile = pl.BlockSpec((TM, D), lambda i: (i, 0))
    return _call(body, name=name, grid=(t // TM,), in_specs=[tile, tile, _full((2, 3, D))],
                 out_specs=tile, out_shape=jax.ShapeDtypeStruct((t, D), F32))(x, o, mod)


def res_bwd(dxn, o, mod, update_ctx, name):
    t = dxn.shape[0]

    def body(dxn_ref, o_ref, mod_ref, do_ref, dgt_ref):
        i = pl.program_id(0)

        @pl.when(i == 0)
        def _():
            dgt_ref[...] = jnp.zeros_like(dgt_ref)

        _, _, gt = _mod_rows(mod_ref, i)
        dv = dxn_ref[...]
        do = gt * dv
        dgt = jnp.sum(dv * o_ref[...], axis=0, keepdims=True)
        if update_ctx:
            do_ref[...] = do.astype(BF)
        else:
            do_ref[...] = jnp.where(i == 0, jnp.zeros_like(do), do).astype(BF)

        if update_ctx:
            @pl.when(i == 0)
            def _():
                dgt_ref[0:1, :] += dgt

        @pl.when(i > 0)
        def _():
            dgt_ref[1:2, :] += dgt

    tile = pl.BlockSpec((TM, D), lambda i: (i, 0))
    return _call(body, name=name, grid=(t // TM,), in_specs=[tile, tile, _full((2, 3, D))],
                 out_specs=[tile, _full((2, D))],
                 out_shape=[jax.ShapeDtypeStruct((t, D), BF), jax.ShapeDtypeStruct((2, D), F32)])(dxn, o, mod)


def final_loss(x, target, g):
    t = x.shape[0]

    def body(x_ref, t_ref, g_ref, loss_ref, dx_ref, dg_ref):
        i = pl.program_id(0)

        @pl.when(i == 0)
        def _():
            loss_ref[...] = jnp.zeros_like(loss_ref)
            dg_ref[...] = jnp.zeros_like(dg_ref)
            dx_ref[...] = jnp.zeros_like(dx_ref)

        @pl.when(i > 0)
        def _():
            xv = x_ref[...]
            gv = g_ref[...]
            r = lax.rsqrt(jnp.mean(xv * xv, axis=-1, keepdims=True) + EPS)
            xn = xv * r
            err = xn * gv - t_ref[...]
            loss_ref[...] += (0.5 / D) * jnp.sum(jnp.sum(err * err, axis=1, keepdims=True), axis=0, keepdims=True)
            dy = err * (1.0 / D)
            dg_ref[...] += jnp.sum(dy * xn, axis=0, keepdims=True)
            dxh = dy * gv
            dx_ref[...] = r * (dxh - xn * jnp.mean(dxh * xn, axis=-1, keepdims=True))

    tile = pl.BlockSpec((TM, D), lambda i: (i, 0))
    return _call(body, name="final_loss", grid=(t // TM,),
                 in_specs=[tile, pl.BlockSpec((TM, D), lambda i: (jnp.maximum(i - 1, 0), 0)), _full((1, D))],
                 out_specs=[_full((1, 1)), tile, _full((1, D))],
                 out_shape=[jax.ShapeDtypeStruct((1, 1), F32), jax.ShapeDtypeStruct((t, D), F32),
                            jax.ShapeDtypeStruct((1, D), F32)])(x, target, g)


def ada_fwd(cond, w_mod, b_mod):
    nl, _, nw = w_mod.shape

    def body(c_ref, w_ref, b_ref, o_ref):
        cv = c_ref[...]
        s = (cv * _sig(cv)).astype(BF)
        o_ref[0] = _dot(s, w_ref[0].astype(BF)) + b_ref[0]

    return _call(body, name="ada_fwd", grid=(nl,),
                 in_specs=[_full((8, D)), pl.BlockSpec((1, D, nw), lambda l: (l, 0, 0)),
                           pl.BlockSpec((1, 1, nw), lambda l: (l, 0, 0))],
                 out_specs=pl.BlockSpec((1, 8, nw), lambda l: (l, 0, 0)),
                 out_shape=jax.ShapeDtypeStruct((nl, 8, nw), F32))(cond, w_mod, b_mod)


def ada_bwd(cond, dm, w_mod):
    nl, _, nw = w_mod.shape

    def body(c_ref, dm_ref, w_ref, gw_ref, dcc_ref, dc_ref):
        l = pl.program_id(0)

        @pl.when(l == 0)
        def _():
            dc_ref[...] = jnp.zeros_like(dc_ref)

        cv = c_ref[...]
        sg = _sig(cv)
        s = (cv * sg).astype(BF)
        dmv = dm_ref[0].astype(BF)
        gw_ref[0] = _dot(s, dmv, 0, 0)
        dc_ref[...] += _dot(dmv, w_ref[0].astype(BF), 1, 1)

        @pl.when(l == nl - 1)
        def _():
            dcond = dc_ref[...] * (sg * (1.0 + cv * (1.0 - sg)))
            dcc_ref[...] = jnp.sum(dcond[4:8], axis=0, keepdims=True)

    return _call(body, name="ada_bwd", grid=(nl,),
                 in_specs=[_full((8, D)), pl.BlockSpec((1, 8, nw), lambda l: (l, 0, 0)),
                           pl.BlockSpec((1, D, nw), lambda l: (l, 0, 0))],
                 out_specs=[pl.BlockSpec((1, D, nw), lambda l: (l, 0, 0)), _full((1, D))],
                 out_shape=[jax.ShapeDtypeStruct((nl, D, nw), F32), jax.ShapeDtypeStruct((1, D), F32)],
                 scratch=[pltpu.VMEM((8, D), F32)])(cond, dm, w_mod)


def add2(a, b, name):
    def body(a_ref, b_ref, o_ref):
        o_ref[...] = a_ref[...] + b_ref[...]

    return _call(body, name=name, out_shape=jax.ShapeDtypeStruct(a.shape, a.dtype))(a, b)


AW = 512
NGRP = 4


def _gelu(y):
    t = jnp.tanh(GELU_K * (y + GELU_C * y * y * y))
    return 0.5 * y * (1.0 + t), t


def _layer_norm_stats(v):
    mu = jnp.mean(v, axis=-1, keepdims=True)
    vc = v - mu
    rstd = lax.rsqrt(jnp.mean(vc * vc, axis=-1, keepdims=True) + EPS)
    return vc * rstd, rstd


def _spatial_mix(vn_ref, ws_ref, bs_ref, mixed_ref):
    for ch in range(TM // CHUNK):
        rows = slice(ch * CHUNK, (ch + 1) * CHUNK)
        for g in range(NGRP):
            cols = slice(g * CHUNK, (g + 1) * CHUNK)
            mixed_ref[rows, cols] = _dot(ws_ref[g], vn_ref[rows, cols]) + bs_ref[g]


def mix_fwd(p, yf, yb, vg, ws, bs, dsk, wglu, bglu, name):
    t = p.shape[0]

    def body(p_ref, yf_ref, yb_ref, vg_ref, ws_ref, bs_ref, d_ref, wg_ref, bg_ref, o_ref, vn_ref, mixed_ref):
        vhat, _ = _layer_norm_stats(p_ref[:, AW:2 * AW])
        vn_ref[...] = (vhat * vg_ref[...]).astype(BF)
        _spatial_mix(vn_ref, ws_ref, bs_ref, mixed_ref)
        ga = p_ref[:, 2 * AW:3 * AW]
        o_ref[:, 0:AW] = (p_ref[:, 0:AW] * mixed_ref[...] * (ga * _sig(ga))).astype(BF)
        y = yf_ref[...] + yb_ref[...] + d_ref[...] * p_ref[:, 3 * AW:4 * AW]
        y2, _ = _gelu(y)
        z = _dot(y2.astype(BF), wg_ref[...]) + bg_ref[...]
        gb = p_ref[:, 4 * AW:5 * AW]
        o_ref[:, AW:2 * AW] = (y2 * _sig(z) * (gb * _sig(gb))).astype(BF)

    tile = lambda w: pl.BlockSpec((TM, w), lambda i: (i, 0))
    return _call(body, name=name, grid=(t // TM,),
                 in_specs=[tile(5 * AW), tile(AW), tile(AW), _full((1, AW)), _full((NGRP, CHUNK, CHUNK)),
                           _full((NGRP, CHUNK, 1)), _full((1, AW)), _full((AW, AW)), _full((1, AW))],
                 out_specs=tile(2 * AW), out_shape=jax.ShapeDtypeStruct((t, 2 * AW), BF),
                 scratch=[pltpu.VMEM((TM, AW), BF), pltpu.VMEM((TM, AW), F32)])(p, yf, yb, vg, ws, bs, dsk, wglu, bglu)


def mix_bwd(p, yf, yb, dmix, vg, ws, bs, dsk, wglu, bglu, name):
    t = p.shape[0]

    def body(p_ref, yf_ref, yb_ref, dm_ref, vg_ref, ws_ref, bs_ref, d_ref, wg_ref, bg_ref,
             dpa_ref, dpgb_ref, dy_ref, dws_ref, dbs_ref, dvg_ref, dd_ref, dwg_ref, dbg_ref,
             vn_ref, mixed_ref, dmx_ref, dvn_ref):
        @pl.when(pl.program_id(0) == 0)
        def _():
            for r in (dws_ref, dbs_ref, dvg_ref, dd_ref, dwg_ref, dbg_ref):
                r[...] = jnp.zeros_like(r)

        vhat, rstd = _layer_norm_stats(p_ref[:, AW:2 * AW])
        vgv = vg_ref[...]
        vn_ref[...] = (vhat * vgv).astype(BF)
        _spatial_mix(vn_ref, ws_ref, bs_ref, mixed_ref)
        u = p_ref[:, 0:AW]
        ga = p_ref[:, 2 * AW:3 * AW]
        sga = _sig(ga)
        dya = dm_ref[:, 0:AW]
        mixed = mixed_ref[...]
        dpa_ref[:, 0:AW] = (dya * mixed * (ga * sga)).astype(BF)
        dpa_ref[:, 2 * AW:3 * AW] = (dya * u * mixed * (sga * (1.0 + ga * (1.0 - sga)))).astype(BF)
        dmx_ref[...] = dya * u * (ga * sga)
        for ch in range(TM // CHUNK):
            rows = slice(ch * CHUNK, (ch + 1) * CHUNK)
            for g in range(NGRP):
                cols = slice(g * CHUNK, (g + 1) * CHUNK)
                dmx = dmx_ref[rows, cols]
                dmxb = dmx.astype(BF)
                dws_ref[g] += _dot(dmxb, vn_ref[rows, cols], 1, 1)
                dbs_ref[g] += jnp.sum(dmx, axis=1, keepdims=True)
                dvn_ref[rows, cols] = _dot(ws_ref[g], dmxb, 0, 0)
        dvn = dvn_ref[...]
        dvg_ref[...] += jnp.sum(dvn * vhat, axis=0, keepdims=True)
        dvh = dvn * vgv
        dpa_ref[:, AW:2 * AW] = (rstd * (dvh - jnp.mean(dvh, axis=-1, keepdims=True)
                                         - vhat * jnp.mean(dvh * vhat, axis=-1, keepdims=True))).astype(BF)

        xs = p_ref[:, 3 * AW:4 * AW]
        y = yf_ref[...] + yb_ref[...] + d_ref[...] * xs
        y2, th = _gelu(y)
        y2b = y2.astype(BF)
        z = _dot(y2b, wg_ref[...]) + bg_ref[...]
        sz = _sig(z)
        gb = p_ref[:, 4 * AW:5 * AW]
        sgb = _sig(gb)
        dyb = dm_ref[:, AW:2 * AW]
        dpgb_ref[...] = (dyb * (y2 * sz) * (sgb * (1.0 + gb * (1.0 - sgb)))).astype(BF)
        dy3 = dyb * (gb * sgb)
        dz = dy3 * y2 * sz * (1.0 - sz)
        dzb = dz.astype(BF)
        dwg_ref[...] += _dot(y2b, dzb, 0, 0)
        dbg_ref[...] += jnp.sum(dz, axis=0, keepdims=True)
        dy2 = dy3 * sz + _dot(dzb, wg_ref[...], 1, 1)
        dgelu = 0.5 * (1.0 + th) + 0.5 * y * (1.0 - th * th) * GELU_K * (1.0 + 3.0 * GELU_C * y * y)
        dy = dy2 * dgelu
        dd_ref[...] += jnp.sum(dy * xs, axis=0, keepdims=True)
        dy_ref[...] = dy

    tile = lambda w: pl.BlockSpec((TM, w), lambda i: (i, 0))
    return _call(body, name=name, grid=(t // TM,),
                 in_specs=[tile(5 * AW), tile(AW), tile(AW), tile(2 * AW), _full((1, AW)), _full((NGRP, CHUNK, CHUNK)),
                           _full((NGRP, CHUNK, 1)), _full((1, AW)), _full((AW, AW)), _full((1, AW))],
                 out_specs=[tile(3 * AW), tile(AW), tile(AW), _full((NGRP, CHUNK, CHUNK)), _full((NGRP, CHUNK, 1)),
                            _full((1, AW)), _full((1, AW)), _full((AW, AW)), _full((1, AW))],
                 out_shape=[jax.ShapeDtypeStruct((t, 3 * AW), BF), jax.ShapeDtypeStruct((t, AW), BF),
                            jax.ShapeDtypeStruct((t, AW), F32), jax.ShapeDtypeStruct((NGRP, CHUNK, CHUNK), F32),
                            jax.ShapeDtypeStruct((NGRP, CHUNK, 1), F32), jax.ShapeDtypeStruct((1, AW), F32),
                            jax.ShapeDtypeStruct((1, AW), F32), jax.ShapeDtypeStruct((AW, AW), F32),
                            jax.ShapeDtypeStruct((1, AW), F32)],
                 scratch=[pltpu.VMEM((TM, AW), BF), pltpu.VMEM((TM, AW), F32), pltpu.VMEM((TM, AW), F32),
                          pltpu.VMEM((TM, AW), F32)])(p, yf, yb, dmix, vg, ws, bs, dsk, wglu, bglu)


LN = 512
NBLK = SW // LN
UB = AW // NBLK
PW_ROWS = 32
POW_EXP = [1, 2, 4, 8, 16, 32, 64, 128, 256, 0, 0, 0, 0, 0, 0, 0,
           1, 2, 3, 4, 5, 6, 7, 8, 8, 7, 6, 5, 4, 3, 2, 1]
GROUPS_PER_TILE = TM // 8


def s5_disc(lam_re, lam_im, dt, lam_re_r, lam_im_r, dt_r, b_re, b_im):
    nexp = jnp.asarray(np.array(POW_EXP, np.float32).reshape(PW_ROWS, 1))

    def body(n_ref, lr_ref, li_ref, dt_ref, lrr_ref, lir_ref, dtr_ref, br_ref, bi_ref,
             pr_ref, pi_ref, bbr_ref, bbi_ref):
        for dr in range(2):
            dtl = jnp.exp(dt_ref[dr:dr + 1, :])
            zr = lr_ref[dr:dr + 1, :] * dtl
            zi = li_ref[dr:dr + 1, :] * dtl
            mag = jnp.exp(n_ref[...] * zr)
            ang = n_ref[...] * zi
            pr_ref[dr] = mag * jnp.cos(ang)
            pi_ref[dr] = mag * jnp.sin(ang)
        lr, li, dtv = lrr_ref[...], lir_ref[...], jnp.exp(dtr_ref[...])
        mag = jnp.exp(lr * dtv)
        nr = mag * jnp.cos(li * dtv) - 1.0
        ni = mag * jnp.sin(li * dtv)
        den = lr * lr + li * li
        fr = (nr * lr + ni * li) / den
        fi = (ni * lr - nr * li) / den
        bbr_ref[...] = fr * br_ref[...] - fi * bi_ref[...]
        bbi_ref[...] = fr * bi_ref[...] + fi * br_ref[...]

    rows = lam_re_r.shape[0]
    return _call(body, name="s5_disc",
                 out_shape=[jax.ShapeDtypeStruct((2, PW_ROWS, SW), F32), jax.ShapeDtypeStruct((2, PW_ROWS, SW), F32),
                            jax.ShapeDtypeStruct((rows, SP), F32), jax.ShapeDtypeStruct((rows, SP), F32)])(
        nexp, lam_re, lam_im, dt, lam_re_r, lam_im_r, dt_r, b_re, b_im)


def s5_param_bwd(lam_re_r, lam_im_r, dt_r, b_re, b_im, da_re, da_im, dbb_re, dbb_im):
    rows = lam_re_r.shape[0]
    ng = rows // SH
    seg = jnp.asarray(np.kron(np.eye(ng, dtype=np.float32), np.ones((1, SH), np.float32)))

    def body(seg_ref, lr_ref, li_ref, dt_ref, br_ref, bi_ref, dar_ref, dai_ref, dbbr_ref, dbbi_ref,
             dlr_ref, dli_ref, ddt_ref, dbr_ref, dbi_ref):
        lr, li, dtv = lr_ref[...], li_ref[...], jnp.exp(dt_ref[...])
        mag = jnp.exp(lr * dtv)
        lbr = mag * jnp.cos(li * dtv)
        lbi = mag * jnp.sin(li * dtv)
        den = lr * lr + li * li
        nr, ni = lbr - 1.0, lbi
        fr = (nr * lr + ni * li) / den
        fi = (ni * lr - nr * li) / den
        br, bi = br_ref[...], bi_ref[...]
        gbr, gbi = dbbr_ref[...], dbbi_ref[...]
        dbr_ref[...] = gbr * fr + gbi * fi
        dbi_ref[...] = gbi * fr - gbr * fi
        gfr = gbr * br + gbi * bi
        gfi = gbi * br - gbr * bi
        ilr, ili = lr / den, -li / den
        gnr = gfr * ilr + gfi * ili
        gni = gfi * ilr - gfr * ili
        qr = -(fr * ilr - fi * ili)
        qi = -(fr * ili + fi * ilr)
        glr = gfr * qr + gfi * qi
        gli = gfi * qr - gfr * qi
        first = (lax.broadcasted_iota(jnp.int32, (rows, 1), 0) % SH) == 0
        glbr = gnr + jnp.where(first, dar_ref[...], 0.0)
        glbi = gni + jnp.where(first, dai_ref[...], 0.0)
        gzr = glbr * lbr + glbi * lbi
        gzi = glbi * lbr - glbr * lbi
        glr = glr + gzr * dtv
        gli = gli + gzi * dtv
        gdt = (gzr * lr + gzi * li) * dtv
        hi = lax.Precision.HIGHEST
        sg = seg_ref[...]
        dlr_ref[...] = jnp.dot(sg, glr, precision=hi, preferred_element_type=F32)
        dli_ref[...] = jnp.dot(sg, gli, precision=hi, preferred_element_type=F32)
        ddt_ref[...] = jnp.sum(jnp.dot(sg, gdt, precision=hi, preferred_element_type=F32), axis=1, keepdims=True)

    return _call(body, name="s5_param_bwd",
                 out_shape=[jax.ShapeDtypeStruct((ng, SP), F32), jax.ShapeDtypeStruct((ng, SP), F32),
                            jax.ShapeDtypeStruct((ng, 1), F32), jax.ShapeDtypeStruct((rows, SP), F32),
                            jax.ShapeDtypeStruct((rows, SP), F32)])(
        seg, lam_re_r, lam_im_r, dt_r, b_re, b_im, da_re, da_im, dbb_re, dbb_im)


def _scan_tile(hr_ref, hi_ref, er_ref, ei_ref, st_ref, cr_ref, ci_ref, pr_ref, pi_ref, reverse):
    row8 = lax.broadcasted_iota(jnp.int32, (TM, 1), 0) % 8
    rowe = lax.broadcasted_iota(jnp.int32, (2 * GROUPS_PER_TILE, 1), 0)
    ne = 2 * GROUPS_PER_TILE
    for blk in range(NBLK):
        cols = slice(blk * LN, (blk + 1) * LN)
        hr, hi = hr_ref[:, cols], hi_ref[:, cols]
        for k, s in enumerate((1, 2, 4)):
            ar, ai = pr_ref[k:k + 1, cols], pi_ref[k:k + 1, cols]
            if reverse:
                m = row8 < 8 - s
                sr, si = pltpu.roll(hr, TM - s, 0), pltpu.roll(hi, TM - s, 0)
            else:
                m = row8 >= s
                sr, si = pltpu.roll(hr, s, 0), pltpu.roll(hi, s, 0)
            sr, si = jnp.where(m, sr, 0.0), jnp.where(m, si, 0.0)
            hr, hi = hr + ar * sr - ai * si, hi + ar * si + ai * sr
        hr_ref[:, cols] = hr
        hi_ref[:, cols] = hi
        edge = 0 if reverse else 7
        nq = LN // 128
        for q in range(nq):
            st_ref[q] = hr[:, q * 128:(q + 1) * 128]
            st_ref[nq + q] = hi[:, q * 128:(q + 1) * 128]
        gr = jnp.concatenate([st_ref[q, pl.ds(edge, GROUPS_PER_TILE, stride=8), :] for q in range(nq)], axis=1)
        gi = jnp.concatenate([st_ref[nq + q, pl.ds(edge, GROUPS_PER_TILE, stride=8), :] for q in range(nq)], axis=1)
        zero = jnp.zeros((GROUPS_PER_TILE, LN), F32)
        if reverse:
            er_ref[0:GROUPS_PER_TILE, :] = gr
            ei_ref[0:GROUPS_PER_TILE, :] = gi
            er_ref[GROUPS_PER_TILE:ne, :] = zero
            ei_ref[GROUPS_PER_TILE:ne, :] = zero
            er_ref[GROUPS_PER_TILE:GROUPS_PER_TILE + 1, :] = cr_ref[:, cols]
            ei_ref[GROUPS_PER_TILE:GROUPS_PER_TILE + 1, :] = ci_ref[:, cols]
        else:
            er_ref[0:GROUPS_PER_TILE, :] = zero
            ei_ref[0:GROUPS_PER_TILE, :] = zero
            er_ref[GROUPS_PER_TILE - 1:GROUPS_PER_TILE, :] = cr_ref[:, cols]
            ei_ref[GROUPS_PER_TILE - 1:GROUPS_PER_TILE, :] = ci_ref[:, cols]
            er_ref[GROUPS_PER_TILE:ne, :] = gr
            ei_ref[GROUPS_PER_TILE:ne, :] = gi
        evr, evi = er_ref[...], ei_ref[...]
        s = 1
        k = 3
        while s < ne:
            ar, ai = pr_ref[k:k + 1, cols], pi_ref[k:k + 1, cols]
            if reverse:
                m = rowe < ne - s
                sr, si = pltpu.roll(evr, ne - s, 0), pltpu.roll(evi, ne - s, 0)
            else:
                m = rowe >= s
                sr, si = pltpu.roll(evr, s, 0), pltpu.roll(evi, s, 0)
            sr, si = jnp.where(m, sr, 0.0), jnp.where(m, si, 0.0)
            evr, evi = evr + ar * sr - ai * si, evi + ar * si + ai * sr
            s *= 2
            k += 1
        er_ref[...] = evr
        ei_ref[...] = evi
        if reverse:
            cr_ref[:, cols] = er_ref[0:1, :]
            ci_ref[:, cols] = ei_ref[0:1, :]
            apr, api = pr_ref[24:32, cols], pi_ref[24:32, cols]
        else:
            cr_ref[:, cols] = er_ref[ne - 1:ne, :]
            ci_ref[:, cols] = ei_ref[ne - 1:ne, :]
            apr, api = pr_ref[16:24, cols], pi_ref[16:24, cols]
        for g in range(GROUPS_PER_TILE):
            e_row = g + 1 if reverse else GROUPS_PER_TILE - 1 + g
            kr, ki = er_ref[e_row:e_row + 1, :], ei_ref[e_row:e_row + 1, :]
            rows = slice(8 * g, 8 * g + 8)
            hr_ref[rows, cols] = hr_ref[rows, cols] + apr * kr - api * ki
            hi_ref[rows, cols] = hi_ref[rows, cols] + apr * ki + api * kr


def _tile_order(kind, nt):
    if kind == "fwd":
        return lambda i: i
    if kind == "bwd":
        return lambda i: jnp.where(i == 0, 0, nt - i)
    if kind == "fwd_adj":
        return lambda i: nt - 1 - i
    if kind == "bwd_adj":
        return lambda i: jnp.where(i == nt - 1, 0, i + 1)
    raise ValueError(kind)


def s5_fwd(p, bb_re, bb_im, ct_re, ct_im, pw_re, pw_im, reverse, name):
    t = p.shape[0]
    nt = t // TM
    order = _tile_order("bwd" if reverse else "fwd", nt)

    def body(x_ref, bbr_ref, bbi_ref, ctr_ref, cti_ref, pr_ref, pi_ref, y_ref, hpr_ref, hpi_ref,
             hr_ref, hi_ref, er_ref, ei_ref, st_ref, cr_ref, ci_ref, c0r_ref, c0i_ref):
        @pl.when(pl.program_id(0) == 0)
        def _():
            cr_ref[...] = jnp.zeros_like(cr_ref)
            ci_ref[...] = jnp.zeros_like(ci_ref)

        c0r_ref[...] = cr_ref[...]
        c0i_ref[...] = ci_ref[...]
        xb = x_ref[...].astype(BF)
        for j in range(NBLK):
            cols = slice(j * LN, (j + 1) * LN)
            hr_ref[:, cols] = _dot(xb[:, j * UB:(j + 1) * UB], bbr_ref[j])
            hi_ref[:, cols] = _dot(xb[:, j * UB:(j + 1) * UB], bbi_ref[j])
        _scan_tile(hr_ref, hi_ref, er_ref, ei_ref, st_ref, cr_ref, ci_ref, pr_ref, pi_ref, reverse)
        rowi = lax.broadcasted_iota(jnp.int32, (TM, 1), 0)
        for j in range(NBLK):
            cols = slice(j * LN, (j + 1) * LN)
            hr, hi = hr_ref[:, cols], hi_ref[:, cols]
            y_ref[:, j * UB:(j + 1) * UB] = _dot(hr.astype(BF), ctr_ref[j]) - _dot(hi.astype(BF), cti_ref[j])
            if reverse:
                first = rowi == TM - 1
                sr, si = pltpu.roll(hr, TM - 1, 0), pltpu.roll(hi, TM - 1, 0)
            else:
                first = rowi == 0
                sr, si = pltpu.roll(hr, 1, 0), pltpu.roll(hi, 1, 0)
            hpr_ref[:, cols] = jnp.where(first, c0r_ref[:, cols], sr)
            hpi_ref[:, cols] = jnp.where(first, c0i_ref[:, cols], si)

    state = lambda: pl.BlockSpec((TM, SW), lambda i: (order(i), 0))
    return _call(body, name=name, grid=(nt,),
                 in_specs=[pl.BlockSpec((TM, AW), lambda i: (order(i), 3)),
                           _full((NBLK, UB, LN)), _full((NBLK, UB, LN)), _full((NBLK, LN, UB)), _full((NBLK, LN, UB)),
                           _full((PW_ROWS, SW)), _full((PW_ROWS, SW))],
                 out_specs=[pl.BlockSpec((TM, AW), lambda i: (order(i), 0)), state(), state()],
                 out_shape=[jax.ShapeDtypeStruct((t, AW), F32), jax.ShapeDtypeStruct((t, SW), F32),
                            jax.ShapeDtypeStruct((t, SW), F32)],
                 scratch=[pltpu.VMEM((TM, SW), F32), pltpu.VMEM((TM, SW), F32),
                          pltpu.VMEM((2 * GROUPS_PER_TILE, LN), F32), pltpu.VMEM((2 * GROUPS_PER_TILE, LN), F32),
                          pltpu.VMEM((2 * LN // 128, TM, 128), F32),
                          pltpu.VMEM((1, SW), F32), pltpu.VMEM((1, SW), F32),
                          pltpu.VMEM((1, SW), F32), pltpu.VMEM((1, SW), F32)])(
        p, bb_re, bb_im, ct_re, ct_im, pw_re, pw_im)


def s5_bwd(p, hp_re, hp_im, dy, bb_re, bb_im, ct_re, ct_im, pw_re, pw_im_conj, a_re, a_im, reverse, name):
    t = p.shape[0]
    nt = t // TM
    order = _tile_order("bwd_adj" if reverse else "fwd_adj", nt)

    def body(x_ref, hpr_ref, hpi_ref, dy_ref, bbr_ref, bbi_ref, ctr_ref, cti_ref, pr_ref, pi_ref, ar_ref, ai_ref,
             dx_ref, dar_ref, dai_ref, dbbr_ref, dbbi_ref, dcr_ref, dci_ref,
             gr_ref, gi_ref, er_ref, ei_ref, st_ref, cr_ref, ci_ref):
        @pl.when(pl.program_id(0) == 0)
        def _():
            for r in (cr_ref, ci_ref, dar_ref, dai_ref, dbbr_ref, dbbi_ref, dcr_ref, dci_ref):
                r[...] = jnp.zeros_like(r)

        xb = x_ref[...].astype(BF)
        dyb = dy_ref[...].astype(BF)
        for j in range(NBLK):
            cols = slice(j * LN, (j + 1) * LN)
            gr_ref[:, cols] = _dot(dyb[:, j * UB:(j + 1) * UB], ctr_ref[j], 1, 1)
            gi_ref[:, cols] = -_dot(dyb[:, j * UB:(j + 1) * UB], cti_ref[j], 1, 1)
        _scan_tile(gr_ref, gi_ref, er_ref, ei_ref, st_ref, cr_ref, ci_ref, pr_ref, pi_ref, not reverse)
        for j in range(NBLK):
            cols = slice(j * LN, (j + 1) * LN)
            xj = xb[:, j * UB:(j + 1) * UB]
            dyj = dyb[:, j * UB:(j + 1) * UB]
            hpr, hpi = hpr_ref[:, cols], hpi_ref[:, cols]
            gr, gi = gr_ref[:, cols], gi_ref[:, cols]
            ar, ai = ar_ref[:, cols], ai_ref[:, cols]
            hr = ar * hpr - ai * hpi + _dot(xj, bbr_ref[j])
            hi = ar * hpi + ai * hpr + _dot(xj, bbi_ref[j])
            dar_ref[:, cols] += jnp.sum(gr * hpr + gi * hpi, axis=0, keepdims=True)
            dai_ref[:, cols] += jnp.sum(gi * hpr - gr * hpi, axis=0, keepdims=True)
            grb, gib = gr.astype(BF), gi.astype(BF)
            dcr_ref[j] += _dot(dyj, hr.astype(BF), 0, 0)
            dci_ref[j] += -_dot(dyj, hi.astype(BF), 0, 0)
            dbbr_ref[j] += _dot(xj, grb, 0, 0)
            dbbi_ref[j] += _dot(xj, gib, 0, 0)
            dx_ref[:, j * UB:(j + 1) * UB] = _dot(grb, bbr_ref[j], 1, 1) + _dot(gib, bbi_ref[j], 1, 1)

    state = lambda: pl.BlockSpec((TM, SW), lambda i: (order(i), 0))
    blockd = lambda: _full((NBLK, UB, LN))
    return _call(body, name=name, grid=(nt,),
                 in_specs=[pl.BlockSpec((TM, AW), lambda i: (order(i), 3)), state(), state(),
                           pl.BlockSpec((TM, AW), lambda i: (order(i), 0)),
                           blockd(), blockd(), _full((NBLK, LN, UB)), _full((NBLK, LN, UB)),
                           _full((PW_ROWS, SW)), _full((PW_ROWS, SW)), _full((1, SW)), _full((1, SW))],
                 out_specs=[pl.BlockSpec((TM, AW), lambda i: (order(i), 0)), _full((1, SW)), _full((1, SW)),
                            blockd(), blockd(), blockd(), blockd()],
                 out_shape=[jax.ShapeDtypeStruct((t, AW), F32), jax.ShapeDtypeStruct((1, SW), F32),
                            jax.ShapeDtypeStruct((1, SW), F32)] + [jax.ShapeDtypeStruct((NBLK, UB, LN), F32)] * 4,
                 scratch=[pltpu.VMEM((TM, SW), F32), pltpu.VMEM((TM, SW), F32),
                          pltpu.VMEM((2 * GROUPS_PER_TILE, LN), F32), pltpu.VMEM((2 * GROUPS_PER_TILE, LN), F32),
                          pltpu.VMEM((2 * LN // 128, TM, 128), F32),
                          pltpu.VMEM((1, SW), F32), pltpu.VMEM((1, SW), F32)])(
        p, hp_re, hp_im, dy, bb_re, bb_im, ct_re, ct_im, pw_re, pw_im_conj, a_re, a_im)


def s5_dx_sum(dy, dsk, dxf, dxb, name):
    t = dy.shape[0]

    def body(dy_ref, d_ref, f_ref, b_ref, o_ref):
        o_ref[...] = (dy_ref[...] * d_ref[...] + f_ref[...] + b_ref[...]).astype(BF)

    tile = pl.BlockSpec((TM, AW), lambda i: (i, 0))
    return _call(body, name=name, grid=(t // TM,), in_specs=[tile, _full((1, AW)), tile, tile], out_specs=tile,
                 out_shape=jax.ShapeDtypeStruct((t, AW), BF))(dy, dsk, dxf, dxb)


SCALE = HD ** -0.5
NHEAD_NORM = NQ + NKV


def _partner(x):
    half0 = (lax.broadcasted_iota(jnp.int32, (1, HD), 1) % 64) < 32
    return jnp.where(half0, pltpu.roll(x, HD - 32, 1), pltpu.roll(x, 32, 1))


def attn_prep(p, qg, kg, cos, sins, name):
    t = p.shape[0]

    def body(p_ref, qg_ref, kg_ref, cos_ref, sin_ref, o_ref):
        cv, sv = cos_ref[...], sin_ref[...]
        for h in range(NHEAD_NORM):
            cols = slice(h * HD, (h + 1) * HD)
            blk = p_ref[:, cols]
            r = lax.rsqrt(jnp.mean(blk * blk, axis=-1, keepdims=True) + EPS)
            xn = blk * r * (qg_ref[...] if h < NQ else kg_ref[...])
            rot = xn * cv + _partner(xn) * sv
            o_ref[:, cols] = ((rot * SCALE) if h < NQ else rot).astype(BF)
        vcols = slice(NHEAD_NORM * HD, (NHEAD_NORM + NKV) * HD)
        o_ref[:, vcols] = p_ref[:, vcols].astype(BF)

    w = (NHEAD_NORM + NKV) * HD
    tile = lambda ww: pl.BlockSpec((TM, ww), lambda i: (i, 0))
    return _call(body, name=name, grid=(t // TM,),
                 in_specs=[tile(w), _full((1, HD)), _full((1, HD)), tile(HD), tile(HD)],
                 out_specs=tile(w), out_shape=jax.ShapeDtypeStruct((t, w), BF))(p, qg, kg, cos, sins)


def attn_prep_bwd(p, dq, dk, dv, qg, kg, cos, sins, name):
    t = p.shape[0]

    def body(p_ref, dq_ref, dk_ref, dv_ref, qg_ref, kg_ref, cos_ref, sin_ref, o_ref, dqg_ref, dkg_ref):
        @pl.when(pl.program_id(0) == 0)
        def _():
            dqg_ref[...] = jnp.zeros_like(dqg_ref)
            dkg_ref[...] = jnp.zeros_like(dkg_ref)

        cv, sv = cos_ref[...], sin_ref[...]
        for h in range(NHEAD_NORM):
            cols = slice(h * HD, (h + 1) * HD)
            blk = p_ref[:, cols]
            r = lax.rsqrt(jnp.mean(blk * blk, axis=-1, keepdims=True) + EPS)
            xh = blk * r
            if h < NQ:
                drot = dq_ref[:, cols] * SCALE
                gv, dg_ref = qg_ref[...], dqg_ref
            else:
                drot = dk_ref[:, (h - NQ) * HD:(h - NQ + 1) * HD]
                gv, dg_ref = kg_ref[...], dkg_ref
            dxn = drot * cv + _partner(drot * sv)
            dg_ref[...] += jnp.sum(dxn * xh, axis=0, keepdims=True)
            dxh = dxn * gv
            o_ref[:, cols] = (r * (dxh - xh * jnp.mean(dxh * xh, axis=-1, keepdims=True))).astype(BF)
        o_ref[:, NHEAD_NORM * HD:(NHEAD_NORM + NKV) * HD] = dv_ref[...].astype(BF)

    w = (NHEAD_NORM + NKV) * HD
    tile = lambda ww: pl.BlockSpec((TM, ww), lambda i: (i, 0))
    return _call(body, name=name, grid=(t // TM,),
                 in_specs=[tile(w), tile(NQ * HD), tile(NKV * HD), tile(NKV * HD), _full((1, HD)), _full((1, HD)),
                           tile(HD), tile(HD)],
                 out_specs=[tile(w), _full((1, HD)), _full((1, HD))],
                 out_shape=[jax.ShapeDtypeStruct((t, w), BF), jax.ShapeDtypeStruct((1, HD), F32),
                            jax.ShapeDtypeStruct((1, HD), F32)])(p, dq, dk, dv, qg, kg, cos, sins)


KCOL = NQ
VCOL = NQ + NKV
GCOL = (NQ + 2 * NKV)
QPK = NQ // NKV


def attn_fwd(qkv, p, name):
    t = qkv.shape[0]

    def body(q_ref, k_ref, v_ref, g_ref, o_ref, mix_ref, lse_ref):
        def attend(nk):
            s = _dot(q_ref[...], k_ref[0:nk, :], 1, 1)
            m = jnp.max(s, axis=-1, keepdims=True)
            pe = jnp.exp(s - m)
            l = jnp.sum(pe, axis=-1, keepdims=True)
            o = _dot(pe.astype(BF), v_ref[0:nk, :]) / l
            gt = g_ref[...]
            o_ref[...] = o
            mix_ref[...] = (o * (gt * _sig(gt))).astype(BF)
            lse_ref[...] = jnp.broadcast_to(m + jnp.log(l), (TM, HD))

        pl.when(pl.program_id(1) == 0)(lambda: attend(NC))
        pl.when(pl.program_id(1) > 0)(lambda: attend(t))

    blk = pl.BlockSpec((TM, HD), lambda h, i: (i, h))
    return _call(body, name=name, grid=(NQ, t // TM),
                 in_specs=[blk, pl.BlockSpec((t, HD), lambda h, i: (0, KCOL + h // QPK)),
                           pl.BlockSpec((t, HD), lambda h, i: (0, VCOL + h // QPK)),
                           pl.BlockSpec((TM, HD), lambda h, i: (i, GCOL + h))],
                 out_specs=[blk, blk, blk],
                 out_shape=[jax.ShapeDtypeStruct((t, NQ * HD), F32), jax.ShapeDtypeStruct((t, NQ * HD), BF),
                            jax.ShapeDtypeStruct((t, NQ * HD), F32)])(qkv, qkv, qkv, p)


def attn_bwd(qkv, p, dmix, o, lse, name):
    t = qkv.shape[0]

    def body(q_ref, k_ref, v_ref, g_ref, dm_ref, o_ref, lse_ref, dq_ref, dg_ref, dk_ref, dv_ref):
        i = pl.program_id(2)

        @pl.when((pl.program_id(1) == 0) & (i == 0))
        def _():
            dk_ref[...] = jnp.zeros_like(dk_ref)
            dv_ref[...] = jnp.zeros_like(dv_ref)

        gt = g_ref[...]
        sg = _sig(gt)
        ov = o_ref[...]
        dmv = dm_ref[...]
        dg_ref[...] = (dmv * ov * (sg * (1.0 + gt * (1.0 - sg)))).astype(BF)
        do = dmv * (gt * sg)
        dr = jnp.sum(do * ov, axis=-1, keepdims=True)
        dob = do.astype(BF)

        def bwd(nk):
            q = q_ref[...]
            s = _dot(q, k_ref[0:nk, :], 1, 1)
            pe = jnp.exp(s - lse_ref[:, 0:1])
            dp = _dot(dob, v_ref[0:nk, :], 1, 1)
            dsb = (pe * (dp - dr)).astype(BF)
            dq_ref[...] = _dot(dsb, k_ref[0:nk, :])
            dv_ref[0:nk, :] += _dot(pe.astype(BF), dob, 0, 0)
            dk_ref[0:nk, :] += _dot(dsb, q, 0, 0)

        pl.when(i == 0)(lambda: bwd(NC))
        pl.when(i > 0)(lambda: bwd(t))

    blk = pl.BlockSpec((TM, HD), lambda kv, g, i: (i, kv * QPK + g))
    acc = pl.BlockSpec((t, HD), lambda kv, g, i: (0, kv))
    return _call(body, name=name, grid=(NKV, QPK, t // TM),
                 in_specs=[blk, pl.BlockSpec((t, HD), lambda kv, g, i: (0, KCOL + kv)),
                           pl.BlockSpec((t, HD), lambda kv, g, i: (0, VCOL + kv)),
                           pl.BlockSpec((TM, HD), lambda kv, g, i: (i, GCOL + kv * QPK + g)), blk, blk, blk],
                 out_specs=[blk, blk, acc, acc],
                 out_shape=[jax.ShapeDtypeStruct((t, NQ * HD), F32), jax.ShapeDtypeStruct((t, NQ * HD), BF),
                            jax.ShapeDtypeStruct((t, NKV * HD), F32), jax.ShapeDtypeStruct((t, NKV * HD), F32)])(
        qkv, qkv, qkv, p, dmix, o, lse)


def _row_tile(rows, row_bytes, cap=2 * 1024 * 1024):
    if rows * row_bytes <= cap or rows % 8:
        return rows
    tr = rows
    while tr * row_bytes > cap and tr % 16 == 0:
        tr //= 2
    return tr


def adamw(w, g, m, v, name):
    r, cdim = w.shape
    tr = _row_tile(r, 4 * max(cdim, 128))

    def body(w_ref, g_ref, m_ref, v_ref, d_ref, nm_ref, nv_ref):
        gv = g_ref[...]
        m2 = ADAM_B1 * m_ref[...] + (1.0 - ADAM_B1) * gv
        v2 = ADAM_B2 * v_ref[...] + (1.0 - ADAM_B2) * (gv * gv)
        mh = m2 / (1.0 - ADAM_B1 ** ADAM_STEP)
        vh = v2 / (1.0 - ADAM_B2 ** ADAM_STEP)
        d_ref[...] = -ADAM_LR * (mh / (jnp.sqrt(vh) + ADAM_EPS) + ADAM_WD * w_ref[...])
        nm_ref[...] = m2
        nv_ref[...] = v2

    tile = pl.BlockSpec((tr, cdim), lambda i: (i, 0))
    sh = jax.ShapeDtypeStruct((r, cdim), F32)
    return _call(body, name=name, grid=(r // tr,), in_specs=[tile] * 4, out_specs=[tile] * 3,
                 out_shape=[sh, sh, sh])(w, g, m, v)


def sum_lead(a, name, out_dtype=F32):
    n, r, cdim = a.shape
    tr = _row_tile(r, 4 * n * max(cdim, 128))

    def body(a_ref, o_ref):
        acc = a_ref[0].astype(F32)
        for k in range(1, n):
            acc = acc + a_ref[k].astype(F32)
        o_ref[...] = acc.astype(o_ref.dtype)

    return _call(body, name=name, grid=(r // tr,),
                 in_specs=[pl.BlockSpec((n, tr, cdim), lambda i: (0, i, 0))],
                 out_specs=pl.BlockSpec((tr, cdim), lambda i: (i, 0)),
                 out_shape=jax.ShapeDtypeStruct((r, cdim), out_dtype))(a)


_FLIPS = {"xy": [(1, 0, 0), (0, 1, 0), (1, 1, 0)], "c": [(0, 0, 1)],
          "all": [(0, 0, 1), (0, 1, 0), (0, 1, 1), (1, 0, 0), (1, 0, 1), (1, 1, 0), (1, 1, 1)]}
_GROUP_SIZE = {"xy": 4, "c": 2, "all": 8}


def _group_index(group, x, y, c):
    return {"xy": 2 * x + y, "c": c, "all": 4 * x + 2 * y + c}[group]


def exchange(items, name):
    plan = []
    for arr, group, kind in items:
        chunk = arr.shape if kind == "gather" else arr.shape[1:]
        plan.append((group, kind, chunk))
    ncopy = sum(len(_FLIPS[g]) for g, _, _ in plan)
    nitem = len(plan)

    def body(*refs):
        srcs, dsts = refs[:nitem], refs[nitem:2 * nitem]
        send_sems, recv_sems, local_sems = refs[2 * nitem:]
        x, y, c = lax.axis_index("x"), lax.axis_index("y"), lax.axis_index("c")
        sends, recvs, locals_ = [], [], []
        n = 0
        for k, (group, kind, _) in enumerate(plan):
            me = _group_index(group, x, y, c)
            own = srcs[k] if kind == "gather" else srcs[k].at[me]
            locals_.append(pltpu.make_async_copy(own, dsts[k].at[me], local_sems.at[k]))
            for fx, fy, fc in _FLIPS[group]:
                px, py, pc = (1 - x if fx else x), (1 - y if fy else y), (1 - c if fc else c)
                peer = _group_index(group, px, py, pc)
                src = srcs[k] if kind == "gather" else srcs[k].at[peer]
                sends.append(pltpu.make_async_remote_copy(
                    src_ref=src, dst_ref=dsts[k].at[me], send_sem=send_sems.at[n], recv_sem=recv_sems.at[n],
                    device_id=(px, py, pc), device_id_type=MESH))
                recvs.append(pltpu.make_async_remote_copy(
                    src_ref=src, dst_ref=dsts[k].at[peer], send_sem=send_sems.at[n], recv_sem=recv_sems.at[n],
                    device_id=(px, py, pc), device_id_type=MESH))
                n += 1
        for cp in locals_ + sends:
            cp.start()
        for cp in recvs:
            cp.wait_recv()
        for cp in sends:
            cp.wait_send()
        for cp in locals_:
            cp.wait()

    anyspec = pl.BlockSpec(memory_space=pl.ANY)
    outs = [jax.ShapeDtypeStruct((_GROUP_SIZE[g],) + tuple(chunk), arr.dtype)
            for (arr, _, _), (g, _, chunk) in zip(items, plan)]
    res = pl.pallas_call(
        body, name=name, out_shape=outs, in_specs=[anyspec] * nitem, out_specs=[anyspec] * nitem,
        scratch_shapes=[pltpu.SemaphoreType.DMA((ncopy,)), pltpu.SemaphoreType.DMA((ncopy,)),
                        pltpu.SemaphoreType.DMA((nitem,))],
        compiler_params=pltpu.CompilerParams(has_side_effects=True))(*[a for a, _, _ in items])
    return list(res)


D2D_PIECES = 4


def d2d(items, name):
    n = len(items)
    swaps = [k for k, (_, kind) in enumerate(items) if kind == "swap"]

    def pieces_of(rows):
        npc = D2D_PIECES if rows % (8 * D2D_PIECES) == 0 else 1
        return npc, rows // npc

    ncopy = sum(pieces_of(a.shape[0] if kind == "gather" else a.shape[1])[0] for a, kind in items)

    def body(*refs):
        srcs, outs = refs[:n], refs[n:2 * n]
        stages = dict(zip(swaps, refs[2 * n:2 * n + len(swaps)]))
        send_sems, recv_sems, local_sems = refs[2 * n + len(swaps):]
        x, y, c = lax.axis_index("x"), lax.axis_index("y"), lax.axis_index("c")
        sib = (x, y, 1 - c)

        def remote(src, dst, q):
            return pltpu.make_async_remote_copy(src_ref=src, dst_ref=dst, send_sem=send_sems.at[q],
                                                recv_sem=recv_sems.at[q], device_id=sib, device_id_type=MESH)

        copies = []
        q = 0
        for k, (arr, kind) in enumerate(items):
            npc, pr = pieces_of(arr.shape[0] if kind == "gather" else arr.shape[1])
            for pc in range(npc):
                rs = pl.ds(pc * pr, pr)
                if kind == "gather":
                    mine, theirs = outs[k].at[c, rs], outs[k].at[1 - c, rs]
                    copies.append((pltpu.make_async_copy(srcs[k].at[rs], mine, local_sems.at[q]),
                                   remote(mine, mine, q), remote(theirs, theirs, q)))
                else:
                    stage, land = stages[k].at[rs], outs[k].at[rs]
                    copies.append((pltpu.make_async_copy(srcs[k].at[1 - c, rs], stage, local_sems.at[q]),
                                   remote(stage, land, q), remote(stage, land, q)))
                q += 1
        for loc, _, _ in copies:
            loc.start()
        for loc, send, _ in copies:
            loc.wait()
            send.start()
        for _, _, recv in copies:
            recv.wait_recv()
        for _, send, _ in copies:
            send.wait_send()

    outs = [jax.ShapeDtypeStruct((2,) + a.shape if kind == "gather" else a.shape[1:], a.dtype) for a, kind in items]
    res = pl.pallas_call(
        body, name=name, out_shape=outs, in_specs=[pl.BlockSpec(memory_space=pl.ANY)] * n,
        out_specs=[pl.BlockSpec(memory_space=pltpu.VMEM)] * n,
        scratch_shapes=[pltpu.VMEM(items[k][0].shape[1:], items[k][0].dtype) for k in swaps]
        + [pltpu.SemaphoreType.DMA((ncopy,)), pltpu.SemaphoreType.DMA((ncopy,)), pltpu.SemaphoreType.DMA((ncopy,))],
        compiler_params=pltpu.CompilerParams(has_side_effects=True, vmem_limit_bytes=VMEM_LIMIT_BYTES))(
        *[a for a, _ in items])
    return list(res)


def sum_own(pair, got, name, out_dtype=F32):
    _, r, cdim = pair.shape
    tr = _row_tile(r, 4 * 3 * max(cdim, 128))

    def body(p_ref, g_ref, o_ref):
        me = lax.axis_index("c")
        o_ref[...] = (p_ref[me] + g_ref[...]).astype(o_ref.dtype)

    return _call(body, name=name, grid=(r // tr,),
                 in_specs=[pl.BlockSpec((2, tr, cdim), lambda i: (0, i, 0)), pl.BlockSpec((tr, cdim), lambda i: (i, 0))],
                 out_specs=pl.BlockSpec((tr, cdim), lambda i: (i, 0)),
                 out_shape=jax.ShapeDtypeStruct((r, cdim), out_dtype))(pair, got)


_SMALL = ["c_ctx", "norm_g", "b_mod", "gm_v_g", "gm_w_s", "gm_b_s", "s5_lam_re", "s5_lam_im", "s5_log_dt",
          "s5_b_re", "s5_b_im", "s5_c_re", "s5_c_im", "s5_d", "s5_b_glu", "q_norm_g", "k_norm_g", "final_g"]
_BIG = ["we_in", "we_out", "s5_w_glu", "wo_in", "wo_out"]
_WEIGHTS = ["c_ctx", "norm_g", "w_mod", "b_mod", "we_in", "we_out", "gm_v_g", "gm_w_s", "gm_b_s", "s5_lam_re",
            "s5_lam_im", "s5_log_dt", "s5_b_re", "s5_b_im", "s5_c_re", "s5_c_im", "s5_d", "s5_w_glu", "s5_b_glu",
            "wo_in", "wo_out", "q_norm_g", "k_norm_g", "final_g"]
_SMALL_ALIGN = 8 * 8 * 128


def _rope_tables(n_lat):
    rows = n_lat // GRID_W
    row = jnp.repeat(jnp.arange(rows), GRID_W)
    col = jnp.tile(jnp.arange(GRID_W), rows)
    freqs = ROPE_THETA ** (-jnp.arange(HD // 4, dtype=F32) / (HD // 4))
    ar, ac = row[:, None] * freqs, col[:, None] * freqs
    cos = jnp.concatenate([jnp.cos(ar), jnp.cos(ar), jnp.cos(ac), jnp.cos(ac)], axis=1)
    sins = jnp.concatenate([-jnp.sin(ar), jnp.sin(ar), -jnp.sin(ac), jnp.sin(ac)], axis=1)
    cos = jnp.concatenate([jnp.ones((NC, HD), F32), cos], axis=0)
    sins = jnp.concatenate([jnp.zeros((NC, HD), F32), sins], axis=0)
    return cos, sins


def _block_diag(v, transpose):
    gpb = SG // NBLK
    v = v.reshape(2, NBLK, gpb, SH, SP)
    eye = jnp.eye(gpb, dtype=v.dtype)
    if transpose:
        return jnp.einsum("djahp,ab->djapbh", v, eye).reshape(2, NBLK, LN, UB)
    return jnp.einsum("djahp,ab->djahbp", v, eye).reshape(2, NBLK, UB, LN)


def _diag_blocks(m):
    gpb = SG // NBLK
    return jnp.einsum("jahap->jahp", m.reshape(NBLK, gpb, SH, gpb, SP)).reshape(SG, SH, SP)


def _view2d(a):
    if a.ndim == 1:
        return a.reshape(1, -1)
    if a.shape[-1] < 64 and a.size % 1024 == 0:
        return a.reshape(-1, 1024)
    return a.reshape(-1, a.shape[-1])


def kernel(x, c, ctx, c_ctx, norm_g, w_mod, b_mod, we_in, we_out, gm_v_g, gm_w_s, gm_b_s, s5_lam_re, s5_lam_im, s5_log_dt, s5_b_re, s5_b_im, s5_c_re, s5_c_im, s5_d, s5_w_glu, s5_b_glu, wo_in, wo_out, q_norm_g, k_norm_g, final_g, loss_target, m_c_ctx, m_norm_g, m_w_mod, m_b_mod, m_we_in, m_we_out, m_gm_v_g, m_gm_w_s, m_gm_b_s, m_s5_lam_re, m_s5_lam_im, m_s5_log_dt, m_s5_b_re, m_s5_b_im, m_s5_c_re, m_s5_c_im, m_s5_d, m_s5_w_glu, m_s5_b_glu, m_wo_in, m_wo_out, m_q_norm_g, m_k_norm_g, m_final_g, v_c_ctx, v_norm_g, v_w_mod, v_b_mod, v_we_in, v_we_out, v_gm_v_g, v_gm_w_s, v_gm_b_s, v_s5_lam_re, v_s5_lam_im, v_s5_log_dt, v_s5_b_re, v_s5_b_im, v_s5_c_re, v_s5_c_im, v_s5_d, v_s5_w_glu, v_s5_b_glu, v_wo_in, v_wo_out, v_q_norm_g, v_k_norm_g, v_final_g):
    weights = dict(c_ctx=c_ctx, norm_g=norm_g, w_mod=w_mod, b_mod=b_mod, we_in=we_in, we_out=we_out, gm_v_g=gm_v_g,
                   gm_w_s=gm_w_s, gm_b_s=gm_b_s, s5_lam_re=s5_lam_re, s5_lam_im=s5_lam_im, s5_log_dt=s5_log_dt,
                   s5_b_re=s5_b_re, s5_b_im=s5_b_im, s5_c_re=s5_c_re, s5_c_im=s5_c_im, s5_d=s5_d, s5_w_glu=s5_w_glu,
                   s5_b_glu=s5_b_glu, wo_in=wo_in, wo_out=wo_out, q_norm_g=q_norm_g, k_norm_g=k_norm_g,
                   final_g=final_g)
    mom_m = dict(c_ctx=m_c_ctx, norm_g=m_norm_g, w_mod=m_w_mod, b_mod=m_b_mod, we_in=m_we_in, we_out=m_we_out,
                 gm_v_g=m_gm_v_g, gm_w_s=m_gm_w_s, gm_b_s=m_gm_b_s, s5_lam_re=m_s5_lam_re, s5_lam_im=m_s5_lam_im,
                 s5_log_dt=m_s5_log_dt, s5_b_re=m_s5_b_re, s5_b_im=m_s5_b_im, s5_c_re=m_s5_c_re, s5_c_im=m_s5_c_im,
                 s5_d=m_s5_d, s5_w_glu=m_s5_w_glu, s5_b_glu=m_s5_b_glu, wo_in=m_wo_in, wo_out=m_wo_out,
                 q_norm_g=m_q_norm_g, k_norm_g=m_k_norm_g, final_g=m_final_g)
    mom_v = dict(c_ctx=v_c_ctx, norm_g=v_norm_g, w_mod=v_w_mod, b_mod=v_b_mod, we_in=v_we_in, we_out=v_we_out,
                 gm_v_g=v_gm_v_g, gm_w_s=v_gm_w_s, gm_b_s=v_gm_b_s, s5_lam_re=v_s5_lam_re, s5_lam_im=v_s5_lam_im,
                 s5_log_dt=v_s5_log_dt, s5_b_re=v_s5_b_re, s5_b_im=v_s5_b_im, s5_c_re=v_s5_c_re, s5_c_im=v_s5_c_im,
                 s5_d=v_s5_d, s5_w_glu=v_s5_w_glu, s5_b_glu=v_s5_b_glu, wo_in=v_wo_in, wo_out=v_wo_out,
                 q_norm_g=v_q_norm_g, k_norm_g=v_k_norm_g, final_g=v_final_g)

    ixy = 2 * lax.axis_index("x") + lax.axis_index("y")
    n_lat = x.shape[1]
    nl = norm_g.shape[0]
    nmod = w_mod.shape[2]
    xin = jnp.concatenate([ctx[0], x[0]], axis=0)

    ic = lax.axis_index("c")
    mine = [lax.dynamic_index_in_dim(weights[n], ic, 0, keepdims=False).astype(BF) for n in _BIG]
    got = exchange([(m_, "xy", "gather") for m_ in mine] + [(c, "xy", "gather")], "gather_weights")
    both = d2d([(g_.reshape(-1, g_.shape[-1]), "gather") for g_ in got[:len(_BIG)]], "swap_weights")
    both = [b_.reshape((2,) + g_.shape) for b_, g_ in zip(both, got)]
    wein = [both[0][0], both[0][1]]
    weout = [both[1][l].reshape(1, D, D) for l in range(2)]
    wglu = [both[2][l].reshape(AW, AW) for l in range(2)]
    woin = [both[3][0], both[3][1]]
    woout = [both[4][l].reshape(1, D, D) for l in range(2)]
    c_group = got[len(_BIG)].reshape(4, D)

    cond = jnp.concatenate([c_group, jnp.broadcast_to(c_ctx.reshape(1, D), (4, D))], axis=0)
    b_shard = lax.dynamic_slice(b_mod, (0, ixy * nmod), (nl, nmod)).reshape(nl, 1, nmod)
    mpart = ada_fwd(cond, w_mod, b_shard)
    m_lat, m_ctx = exchange([(jnp.transpose(mpart[:, 0:4], (1, 0, 2)), "xy", "scatter"),
                             (mpart[:, 4], "xy", "gather")], "exchange_mod")
    m_lat = jnp.transpose(m_lat, (1, 0, 2)).reshape(nl, 3, D)
    m_ctx = jnp.transpose(m_ctx, (1, 0, 2)).reshape(nl, 3, D)
    mods = [jnp.stack([m_ctx[l], m_lat[l]], axis=0) for l in range(nl)]

    loss_part, dx, g, d_norm_g, d_mod_lat, d_mod_ctx, d_final_g = _local_step(
        xin, loss_target[0], mods, wein, weout, wglu, woin, woout, weights)
    grad_x = dx[NC:].reshape(1, n_lat, D)

    d_mod_lat, d_mod_ctx = jnp.stack(d_mod_lat), jnp.stack(d_mod_ctx)
    dm_send = jnp.stack([d_mod_lat.reshape(nl, 4, nmod), d_mod_ctx.reshape(nl, 4, nmod)])
    (dm_got,) = exchange([(jnp.transpose(dm_send, (2, 0, 1, 3)), "xy", "scatter")], "exchange_dmod")
    dm_rows = jnp.concatenate([dm_got[:, 0], dm_got[:, 1]], axis=0)
    gw_mod, d_cctx = ada_bwd(cond, jnp.transpose(dm_rows, (1, 0, 2)), w_mod)
    g_small = dict(c_ctx=d_cctx.reshape(D), norm_g=jnp.stack(d_norm_g), b_mod=add2(d_mod_lat, d_mod_ctx, "add_dbmod"),
                   final_g=d_final_g.reshape(D))
    for name in _SMALL:
        if name not in g_small:
            g_small[name] = jnp.stack(g[name])

    flat = jnp.concatenate([g_small[n].reshape(-1) for n in _SMALL])
    nflat = flat.shape[0]
    npad = -(-nflat // _SMALL_ALIGN) * _SMALL_ALIGN
    flat = jnp.concatenate([flat, jnp.zeros((npad - nflat,), F32)]).reshape(8, npad // (8 * 128), 128)
    pairs = [gw_mod.reshape(2, nl // 2 * D, nmod)]
    for name in _BIG:
        st = jnp.stack(g[name]) if isinstance(g[name], list) else g[name]
        pairs.append(st.reshape(2, -1, st.shape[-1]))
    got_a = d2d([(pairs[k], "swap") for k in (1, 2, 3)], "reduce_chip_a")
    got_b = d2d([(pairs[k], "swap") for k in (0, 4, 5)], "reduce_chip_b")
    theirs = [got_b[0]] + got_a + got_b[1:]
    chip = [sum_own(pairs[k], theirs[k], f"sum_chip{k}", F32 if k == 0 else BF) for k in range(len(pairs))]
    parts = exchange([(flat, "all", "scatter")]
                     + [(s_.reshape(4, s_.shape[0] // 4, s_.shape[1]), "xy", "scatter") for s_ in chip[1:]],
                     "reduce_scatter")
    sums = [sum_lead(pt, f"sum_shard{k}") for k, pt in enumerate(parts)]
    (flat_full,) = exchange([(sums[0], "all", "gather")], "gather_small")
    full = [flat_full] + d2d([(chip[0], "gather")] + [(s_, "gather") for s_ in sums[1:]], "all_gather")
    flat = full[0].reshape(-1)
    grads = {}
    off = 0
    for name in _SMALL:
        sz = weights[name].size
        grads[name] = flat[off:off + sz].reshape(weights[name].shape)
        off += sz
    grads["w_mod"] = full[1].reshape(w_mod.shape)
    for k, name in enumerate(_BIG):
        grads[name] = full[2 + k].reshape(weights[name].shape)

    delta, new_m, new_v = {}, {}, {}
    for name in _WEIGHTS:
        w2 = _view2d(weights[name])
        d2, m2, v2 = adamw(w2, grads[name].reshape(w2.shape), mom_m[name].reshape(w2.shape),
                           mom_v[name].reshape(w2.shape), f"adamw_{name}")
        shp = weights[name].shape
        delta[name], new_m[name], new_v[name] = d2.reshape(shp), m2.reshape(shp), v2.reshape(shp)

    loss = lax.psum(loss_part[0, 0], ("x", "y", "c"))
    return (loss, grad_x, *[grads[n] for n in _WEIGHTS], *[delta[n] for n in _WEIGHTS],
            *[new_m[n] for n in _WEIGHTS], *[new_v[n] for n in _WEIGHTS])


def _local_step(xin, target, mods, wein, weout, wglu, woin, woout, w):
    norm_g, gm_v_g, gm_w_s, gm_b_s = w["norm_g"], w["gm_v_g"], w["gm_w_s"], w["gm_b_s"]
    s5_lam_re, s5_lam_im, s5_log_dt = w["s5_lam_re"], w["s5_lam_im"], w["s5_log_dt"]
    s5_b_re, s5_b_im, s5_c_re, s5_c_im = w["s5_b_re"], w["s5_b_im"], w["s5_c_re"], w["s5_c_im"]
    s5_d, s5_b_glu, q_norm_g, k_norm_g, final_g = w["s5_d"], w["s5_b_glu"], w["q_norm_g"], w["k_norm_g"], w["final_g"]
    nl = norm_g.shape[0]
    n_lat = xin.shape[0] - NC

    cos, sins = _rope_tables(n_lat)

    s5p = []
    for i in range(2):
        lam_l = (s5_lam_re[i].reshape(2, SW), s5_lam_im[i].reshape(2, SW),
                 jnp.repeat(s5_log_dt[i], SP, axis=1))
        lam_r = (jnp.repeat(s5_lam_re[i].reshape(2 * SG, SP), SH, axis=0),
                 jnp.repeat(s5_lam_im[i].reshape(2 * SG, SP), SH, axis=0),
                 jnp.repeat(s5_log_dt[i].reshape(2 * SG, 1), SH, axis=0))
        b_r = (jnp.transpose(s5_b_re[i], (0, 1, 3, 2)).reshape(2 * SG * SH, SP),
               jnp.transpose(s5_b_im[i], (0, 1, 3, 2)).reshape(2 * SG * SH, SP))
        pw_re, pw_im, bbr, bbi = s5_disc(*lam_l, *lam_r, *b_r)
        s5p.append(dict(
            lam_r=lam_r, b_r=b_r, pw_re=pw_re, pw_im=pw_im,
            bb_re=_block_diag(bbr.reshape(2, SG, SH, SP), False).astype(BF),
            bb_im=_block_diag(bbi.reshape(2, SG, SH, SP), False).astype(BF),
            ct_re=_block_diag(s5_c_re[i], True).astype(BF), ct_im=_block_diag(s5_c_im[i], True).astype(BF)))

    saved = []
    xcur = xin
    for l in range(nl):
        i = l // 2
        h = pro_fwd(xcur, norm_g[l].reshape(1, D), mods[l], f"pro_fwd{l}")
        sv = dict(x=xcur, h=h)
        if l % 2 == 0:
            p = mm_nn(h, wein[i], f"in_proj{l}")
            sp = s5p[i]
            for dr, rev in ((0, False), (1, True)):
                sv[f"y{dr}"], sv[f"hpr{dr}"], sv[f"hpi{dr}"] = s5_fwd(
                    p, sp["bb_re"][dr], sp["bb_im"][dr], sp["ct_re"][dr], sp["ct_im"][dr],
                    sp["pw_re"][dr], sp["pw_im"][dr], rev, f"s5_fwd{l}_{dr}")
            mix = mix_fwd(p, sv["y0"], sv["y1"], gm_v_g[i].reshape(1, AW), gm_w_s[i].astype(BF),
                          gm_b_s[i].reshape(NGRP, CHUNK, 1), s5_d[i].reshape(1, AW), wglu[i],
                          s5_b_glu[i].reshape(1, AW), f"mix_fwd{l}")
            o = mm_nn(mix, weout[i], f"out_proj{l}")
        else:
            p = mm_nn(h, woin[i], f"in_proj{l}")
            sv["qkv"] = attn_prep(p, q_norm_g[i].reshape(1, HD), k_norm_g[i].reshape(1, HD), cos, sins, f"attn_prep{l}")
            sv["o_att"], mix, sv["lse"] = attn_fwd(sv["qkv"], p, f"attn_fwd{l}")
            o = mm_nn(mix, woout[i], f"out_proj{l}")
        sv.update(p=p, mix=mix, o=o)
        saved.append(sv)
        xcur = res_fwd(xcur, o, mods[l], l < nl - 1, f"res_fwd{l}")

    loss_part, dx, d_final_g = final_loss(xcur, target, final_g.reshape(1, D))

    g = {}
    gbuf = {}
    d_norm_g, d_mod_lat, d_mod_ctx = [None] * nl, [None] * nl, [None] * nl
    for name in ("s5_w_glu", "gm_v_g", "gm_w_s", "gm_b_s", "s5_lam_re", "s5_lam_im",
                 "s5_log_dt", "s5_b_re", "s5_b_im", "s5_c_re", "s5_c_im", "s5_d", "s5_b_glu", "q_norm_g", "k_norm_g"):
        g[name] = [None, None]
    for l in reversed(range(nl)):
        i = l // 2
        sv = saved[l]
        do, dgt = res_bwd(dx, sv["o"], mods[l], l < nl - 1, f"res_bwd{l}")
        w_out = weout[i] if l % 2 == 0 else woout[i]
        dmix = mm_nt(do, w_out, f"out_dgrad{l}")
        out_name, in_name = ("we_out", "we_in") if l % 2 == 0 else ("wo_out", "wo_in")
        gbuf[out_name] = mm_tn(sv["mix"], do, 1, f"out_wgrad{l}", slot=i, into=gbuf.get(out_name))
        if l % 2 == 0:
            sp = s5p[i]
            (dpa, dpgb, dy, g["gm_w_s"][i], dbs, dvg, dd, g["s5_w_glu"][i], dbg) = mix_bwd(
                sv["p"], sv["y0"], sv["y1"], dmix, gm_v_g[i].reshape(1, AW), gm_w_s[i].astype(BF),
                gm_b_s[i].reshape(NGRP, CHUNK, 1), s5_d[i].reshape(1, AW), wglu[i], s5_b_glu[i].reshape(1, AW),
                f"mix_bwd{l}")
            g["gm_b_s"][i], g["gm_v_g"][i] = dbs.reshape(NGRP, CHUNK), dvg.reshape(AW)
            g["s5_d"][i], g["s5_b_glu"][i] = dd.reshape(AW), dbg.reshape(AW)
            g["s5_w_glu"][i] = g["s5_w_glu"][i].reshape(4, AW // 4, AW)
            dxd, das_r, das_i, dbbs_r, dbbs_i, dcs_r, dcs_i = [], [], [], [], [], [], []
            for dr, rev in ((0, False), (1, True)):
                dxs_d, da_r, da_i, dbb_r, dbb_i, dc_r, dc_i = s5_bwd(
                    sv["p"], sv[f"hpr{dr}"], sv[f"hpi{dr}"], dy, sp["bb_re"][dr], sp["bb_im"][dr],
                    sp["ct_re"][dr], sp["ct_im"][dr], sp["pw_re"][dr], -sp["pw_im"][dr],
                    sp["pw_re"][dr][0:1], sp["pw_im"][dr][0:1], rev, f"s5_bwd{l}_{dr}")
                dxd.append(dxs_d)
                das_r.append(jnp.repeat(da_r.reshape(SG, SP), SH, axis=0))
                das_i.append(jnp.repeat(da_i.reshape(SG, SP), SH, axis=0))
                dbbs_r.append(_diag_blocks(dbb_r).reshape(SG * SH, SP))
                dbbs_i.append(_diag_blocks(dbb_i).reshape(SG * SH, SP))
                dcs_r.append(_diag_blocks(dc_r))
                dcs_i.append(_diag_blocks(dc_i))
            cat = lambda parts: jnp.concatenate(parts, axis=0)
            dlr, dli, dldt, dbr, dbi = s5_param_bwd(*sp["lam_r"], *sp["b_r"], cat(das_r), cat(das_i),
                                                    cat(dbbs_r), cat(dbbs_i))
            g["s5_lam_re"][i], g["s5_lam_im"][i] = dlr.reshape(2, SG, SP), dli.reshape(2, SG, SP)
            g["s5_log_dt"][i] = dldt.reshape(2, SG)
            g["s5_b_re"][i] = jnp.transpose(dbr.reshape(2, SG, SH, SP), (0, 1, 3, 2))
            g["s5_b_im"][i] = jnp.transpose(dbi.reshape(2, SG, SH, SP), (0, 1, 3, 2))
            g["s5_c_re"][i], g["s5_c_im"][i] = jnp.stack(dcs_r), jnp.stack(dcs_i)
            dxs = s5_dx_sum(dy, s5_d[i].reshape(1, AW), dxd[0], dxd[1], f"s5_dx_sum{l}")
            dp = jnp.concatenate([dpa, dxs, dpgb], axis=1)
            w_in = wein[i]
        else:
            dq, dgate, dk, dv = attn_bwd(sv["qkv"], sv["p"], dmix, sv["o_att"], sv["lse"], f"attn_bwd{l}")
            dpqkv, dqg, dkg = attn_prep_bwd(sv["p"], dq, dk, dv, q_norm_g[i].reshape(1, HD),
                                            k_norm_g[i].reshape(1, HD), cos, sins, f"attn_prep_bwd{l}")
            g["q_norm_g"][i], g["k_norm_g"][i] = dqg.reshape(HD), dkg.reshape(HD)
            dp = jnp.concatenate([dpqkv, dgate], axis=1)
            w_in = woin[i]
        dh = mm_nt(dp, w_in, f"in_dgrad{l}")
        gbuf[in_name] = mm_tn(sv["h"], dp, 4, f"in_wgrad{l}", slot=i, into=gbuf.get(in_name))
        dx, dmod2, dng = pro_bwd(sv["x"], dh, dx, norm_g[l].reshape(1, D), mods[l], f"pro_bwd{l}")
        d_norm_g[l] = dng.reshape(D)
        d_mod_ctx[l] = jnp.concatenate([dmod2[0, 0], dmod2[0, 1], dgt[0]])
        d_mod_lat[l] = jnp.concatenate([dmod2[1, 0], dmod2[1, 1], dgt[1]])
    g.update(gbuf)
    return loss_part, dx, g, d_norm_g, d_mod_lat, d_mod_ctx, d_final_g
```

```python
import functools
import math

import numpy as np
import jax
import jax.numpy as jnp
from jax import lax
from jax.experimental import pallas as pl
from jax.experimental.pallas import tpu as pltpu

F32 = jnp.float32
BF = jnp.bfloat16
MESH = pl.DeviceIdType.MESH

D = 1024
NC = 256
SEQ = 4096
GRID_W = 64
TM = 256
CHUNK = 128
EPS = 1e-6
HD = 128
NQ = 8
NKV = 2
ROPE_THETA = 10000.0
SG = 32
SP = 64
SH = 16
SW = SG * SP
GELU_K = math.sqrt(2.0 / math.pi)
GELU_C = 0.044715
VMEM_LIMIT_BYTES = 56 * 1024 * 1024

ADAM_LR = 0.001
ADAM_B1 = 0.9
ADAM_B2 = 0.999
ADAM_EPS = 1e-08
ADAM_WD = 0.01
ADAM_STEP = 10


def _call(body, *, name, out_shape, grid=None, in_specs=None, out_specs=None, scratch=()):
    kw = {}
    if grid is not None:
        kw["grid"] = grid
    if in_specs is not None:
        kw["in_specs"] = in_specs
    if out_specs is not None:
        kw["out_specs"] = out_specs
    return pl.pallas_call(
        body, name=name, out_shape=out_shape, scratch_shapes=list(scratch),
        compiler_params=pltpu.CompilerParams(vmem_limit_bytes=VMEM_LIMIT_BYTES), **kw)


def _dot(a, b, ca=1, cb=0):
    return lax.dot_general(a, b, (((ca,), (cb,)), ((), ())), preferred_element_type=F32)


def _sig(x):
    return 1.0 / (1.0 + jnp.exp(-x))


def _full(shape):
    n = len(shape)
    return pl.BlockSpec(shape, lambda *_: (0,) * n)


def _mm_rows(t):
    for rows in (1088, 1024, 768, 512, 256):
        if t % rows == 0:
            return rows
    raise ValueError(t)


def mm_nn(a, w3, name, out_dtype=F32):
    t, k = a.shape
    j, _, nb = w3.shape
    tr = _mm_rows(t)

    def body(a_ref, w_ref, o_ref):
        o_ref[...] = _dot(a_ref[...], w_ref[0]).astype(o_ref.dtype)

    return _call(body, name=name, grid=(j, t // tr),
                 in_specs=[pl.BlockSpec((tr, k), lambda jj, i: (i, 0)),
                           pl.BlockSpec((1, k, nb), lambda jj, i: (jj, 0, 0))],
                 out_specs=pl.BlockSpec((tr, nb), lambda jj, i: (i, jj)),
                 out_shape=jax.ShapeDtypeStruct((t, j * nb), out_dtype))(a, w3)


def mm_nt(a, w3, name, out_dtype=F32):
    t, _ = a.shape
    j, k, nb = w3.shape
    tr = _mm_rows(t)

    def body(a_ref, w_ref, o_ref):
        acc = _dot(a_ref[:, 0:nb], w_ref[0], 1, 1)
        for jj in range(1, j):
            acc = acc + _dot(a_ref[:, jj * nb:(jj + 1) * nb], w_ref[jj], 1, 1)
        o_ref[...] = acc.astype(o_ref.dtype)

    return _call(body, name=name, grid=(t // tr,),
                 in_specs=[pl.BlockSpec((tr, j * nb), lambda i: (i, 0)), _full((j, k, nb))],
                 out_specs=pl.BlockSpec((tr, k), lambda i: (i, 0)),
                 out_shape=jax.ShapeDtypeStruct((t, k), out_dtype))(a, w3)


def mm_tn(a, b, j, name, slot=0, into=None):
    t, m = a.shape
    nb = b.shape[1] // j
    tr = _mm_rows(t)

    def body(a_ref, b_ref, *rest):
        o_ref = rest[-1]

        @pl.when(pl.program_id(1) == 0)
        def _():
            o_ref[...] = jnp.zeros_like(o_ref)
        o_ref[0, 0] += _dot(a_ref[...], b_ref[...], 0, 0)

    in_specs = [pl.BlockSpec((tr, m), lambda jj, i: (i, 0)), pl.BlockSpec((tr, nb), lambda jj, i: (i, jj))]
    args = [a, b]
    alias = {}
    if into is not None:
        in_specs.append(pl.BlockSpec(memory_space=pl.ANY))
        args.append(into)
        alias = {2: 0}
    return pl.pallas_call(
        body, name=name, grid=(j, t // tr), in_specs=in_specs,
        out_specs=pl.BlockSpec((1, 1, m, nb), lambda jj, i: (slot, jj, 0, 0)),
        out_shape=jax.ShapeDtypeStruct((2, j, m, nb), F32), input_output_aliases=alias,
        compiler_params=pltpu.CompilerParams(vmem_limit_bytes=VMEM_LIMIT_BYTES))(*args)


def _mod_rows(mod_ref, i):
    ctx = i == 0
    sh = jnp.where(ctx, mod_ref[0, 0:1, :], mod_ref[1, 0:1, :])
    sc = jnp.where(ctx, mod_ref[0, 1:2, :], mod_ref[1, 1:2, :])
    gt = jnp.where(ctx, mod_ref[0, 2:3, :], mod_ref[1, 2:3, :])
    return sh, sc, gt


def pro_fwd(x, g, mod, name):
    t = x.shape[0]

    def body(x_ref, g_ref, mod_ref, h_ref):
        sh, sc, _ = _mod_rows(mod_ref, pl.program_id(0))
        xv = x_ref[...]
        r = lax.rsqrt(jnp.mean(xv * xv, axis=-1, keepdims=True) + EPS)
        h_ref[...] = ((xv * r) * g_ref[...] * (1.0 + sc) + sh).astype(BF)

    return _call(body, name=name, grid=(t // TM,),
                 in_specs=[pl.BlockSpec((TM, D), lambda i: (i, 0)), _full((1, D)), _full((2, 3, D))],
                 out_specs=pl.BlockSpec((TM, D), lambda i: (i, 0)),
                 out_shape=jax.ShapeDtypeStruct((t, D), BF))(x, g, mod)


def pro_bwd(x, dh, dxn, g, mod, name):
    t = x.shape[0]

    def body(x_ref, dh_ref, dxn_ref, g_ref, mod_ref, dx_ref, dmod_ref, dg_ref):
        i = pl.program_id(0)

        @pl.when(i == 0)
        def _():
            dmod_ref[...] = jnp.zeros_like(dmod_ref)
            dg_ref[...] = jnp.zeros_like(dg_ref)

        _, sc, _ = _mod_rows(mod_ref, i)
        xv = x_ref[...]
        gv = g_ref[...]
        r = lax.rsqrt(jnp.mean(xv * xv, axis=-1, keepdims=True) + EPS)
        xn = xv * r
        dh_v = dh_ref[...]
        e = dh_v * (1.0 + sc)
        dsh = jnp.sum(dh_v, axis=0, keepdims=True)
        dsc = jnp.sum(dh_v * xn * gv, axis=0, keepdims=True)
        dg_ref[...] += jnp.sum(e * xn, axis=0, keepdims=True)
        dxh = e * gv
        dx_ref[...] = dxn_ref[...] + r * (dxh - xn * jnp.mean(dxh * xn, axis=-1, keepdims=True))

        @pl.when(i == 0)
        def _():
            dmod_ref[0, 0:1, :] += dsh
            dmod_ref[0, 1:2, :] += dsc

        @pl.when(i > 0)
        def _():
            dmod_ref[1, 0:1, :] += dsh
            dmod_ref[1, 1:2, :] += dsc

    tile = pl.BlockSpec((TM, D), lambda i: (i, 0))
    return _call(body, name=name, grid=(t // TM,),
                 in_specs=[tile, tile, tile, _full((1, D)), _full((2, 3, D))],
                 out_specs=[tile, _full((2, 2, D)), _full((1, D))],
                 out_shape=[jax.ShapeDtypeStruct((t, D), F32), jax.ShapeDtypeStruct((2, 2, D), F32),
                            jax.ShapeDtypeStruct((1, D), F32)])(x, dh, dxn, g, mod)


def res_fwd(x, o, mod, update_ctx, name):
    t = x.shape[0]

    def body(x_ref, o_ref, mod_ref, y_ref):
        i = pl.program_id(0)
        _, _, gt = _mod_rows(mod_ref, i)
        upd = x_ref[...] + gt * o_ref[...]
        if update_ctx:
            y_ref[...] = upd
        else:
            y_ref[...] = jnp.where(i == 0, x_ref[...], upd)

    tile = pl.BlockSpec((TM, D), lambda i: (i, 0))
    return _call(body, name=name, grid=(t // TM,), in_specs=[tile, tile, _full((2, 3, D))],
                 out_specs=tile, out_shape=jax.ShapeDtypeStruct((t, D), F32))(x, o, mod)


def res_bwd(dxn, o, mod, update_ctx, name):
    t = dxn.shape[0]

    def body(dxn_ref, o_ref, mod_ref, do_ref, dgt_ref):
        i = pl.program_id(0)

        @pl.when(i == 0)
        def _():
            dgt_ref[...] = jnp.zeros_like(dgt_ref)

        _, _, gt = _mod_rows(mod_ref, i)
        dv = dxn_ref[...]
        do = gt * dv
        dgt = jnp.sum(dv * o_ref[...], axis=0, keepdims=True)
        if update_ctx:
            do_ref[...] = do.astype(BF)
        else:
            do_ref[...] = jnp.where(i == 0, jnp.zeros_like(do), do).astype(BF)

        if update_ctx:
            @pl.when(i == 0)
            def _():
                dgt_ref[0:1, :] += dgt

        @pl.when(i > 0)
        def _():
            dgt_ref[1:2, :] += dgt

    tile = pl.BlockSpec((TM, D), lambda i: (i, 0))
    return _call(body, name=name, grid=(t // TM,), in_specs=[tile, tile, _full((2, 3, D))],
                 out_specs=[tile, _full((2, D))],
                 out_shape=[jax.ShapeDtypeStruct((t, D), BF), jax.ShapeDtypeStruct((2, D), F32)])(dxn, o, mod)


def final_loss(x, target, g):
    t = x.shape[0]

    def body(x_ref, t_ref, g_ref, loss_ref, dx_ref, dg_ref):
        i = pl.program_id(0)

        @pl.when(i == 0)
        def _():
            loss_ref[...] = jnp.zeros_like(loss_ref)
            dg_ref[...] = jnp.zeros_like(dg_ref)
            dx_ref[...] = jnp.zeros_like(dx_ref)

        @pl.when(i > 0)
        def _():
            xv = x_ref[...]
            gv = g_ref[...]
            r = lax.rsqrt(jnp.mean(xv * xv, axis=-1, keepdims=True) + EPS)
            xn = xv * r
            err = xn * gv - t_ref[...]
            loss_ref[...] += (0.5 / D) * jnp.sum(jnp.sum(err * err, axis=1, keepdims=True), axis=0, keepdims=True)
            dy = err * (1.0 / D)
            dg_ref[...] += jnp.sum(dy * xn, axis=0, keepdims=True)
            dxh = dy * gv
            dx_ref[...] = r * (dxh - xn * jnp.mean(dxh * xn, axis=-1, keepdims=True))

    tile = pl.BlockSpec((TM, D), lambda i: (i, 0))
    return _call(body, name="final_loss", grid=(t // TM,),
                 in_specs=[tile, pl.BlockSpec((TM, D), lambda i: (jnp.maximum(i - 1, 0), 0)), _full((1, D))],
                 out_specs=[_full((1, 1)), tile, _full((1, D))],
                 out_shape=[jax.ShapeDtypeStruct((1, 1), F32), jax.ShapeDtypeStruct((t, D), F32),
                            jax.ShapeDtypeStruct((1, D), F32)])(x, target, g)


def ada_fwd(cond, w_mod, b_mod):
    nl, _, nw = w_mod.shape

    def body(c_ref, w_ref, b_ref, o_ref):
        cv = c_ref[...]
        s = (cv * _sig(cv)).astype(BF)
        o_ref[0] = _dot(s, w_ref[0].astype(BF)) + b_ref[0]

    return _call(body, name="ada_fwd", grid=(nl,),
                 in_specs=[_full((8, D)), pl.BlockSpec((1, D, nw), lambda l: (l, 0, 0)),
                           pl.BlockSpec((1, 1, nw), lambda l: (l, 0, 0))],
                 out_specs=pl.BlockSpec((1, 8, nw), lambda l: (l, 0, 0)),
                 out_shape=jax.ShapeDtypeStruct((nl, 8, nw), F32))(cond, w_mod, b_mod)


def ada_bwd(cond, dm, w_mod):
    nl, _, nw = w_mod.shape

    def body(c_ref, dm_ref, w_ref, gw_ref, dcc_ref, dc_ref):
        l = pl.program_id(0)

        @pl.when(l == 0)
        def _():
            dc_ref[...] = jnp.zeros_like(dc_ref)

        cv = c_ref[...]
        sg = _sig(cv)
        s = (cv * sg).astype(BF)
        dmv = dm_ref[0].astype(BF)
        gw_ref[0] = _dot(s, dmv, 0, 0)
        dc_ref[...] += _dot(dmv, w_ref[0].astype(BF), 1, 1)

        @pl.when(l == nl - 1)
        def _():
            dcond = dc_ref[...] * (sg * (1.0 + cv * (1.0 - sg)))
            dcc_ref[...] = jnp.sum(dcond[4:8], axis=0, keepdims=True)

    return _call(body, name="ada_bwd", grid=(nl,),
                 in_specs=[_full((8, D)), pl.BlockSpec((1, 8, nw), lambda l: (l, 0, 0)),
                           pl.BlockSpec((1, D, nw), lambda l: (l, 0, 0))],
                 out_specs=[pl.BlockSpec((1, D, nw), lambda l: (l, 0, 0)), _full((1, D))],
                 out_shape=[jax.ShapeDtypeStruct((nl, D, nw), F32), jax.ShapeDtypeStruct((1, D), F32)],
                 scratch=[pltpu.VMEM((8, D), F32)])(cond, dm, w_mod)


def add2(a, b, name):
    def body(a_ref, b_ref, o_ref):
        o_ref[...] = a_ref[...] + b_ref[...]

    return _call(body, name=name, out_shape=jax.ShapeDtypeStruct(a.shape, a.dtype))(a, b)


AW = 512
NGRP = 4


def Y4_SPEC():
    return pl.BlockSpec((AW // 128, TM, 128), lambda i: (0, i, 0))


def _cat_lanes(ref):
    return jnp.concatenate([ref[q] for q in range(ref.shape[0])], axis=1)


def _gelu(y):
    t = jnp.tanh(GELU_K * (y + GELU_C * y * y * y))
    return 0.5 * y * (1.0 + t), t


def _layer_norm_stats(v):
    mu = jnp.mean(v, axis=-1, keepdims=True)
    vc = v - mu
    rstd = lax.rsqrt(jnp.mean(vc * vc, axis=-1, keepdims=True) + EPS)
    return vc * rstd, rstd


def _spatial_mix(vn_ref, ws_ref, bs_ref, mixed_ref):
    for ch in range(TM // CHUNK):
        rows = slice(ch * CHUNK, (ch + 1) * CHUNK)
        for g in range(NGRP):
            cols = slice(g * CHUNK, (g + 1) * CHUNK)
            mixed_ref[rows, cols] = _dot(ws_ref[g], vn_ref[rows, cols]) + bs_ref[g]


def mix_fwd(p, yf, yb, vg, ws, bs, dsk, wglu, bglu, name):
    t = p.shape[0]

    def body(p_ref, yf_ref, yb_ref, vg_ref, ws_ref, bs_ref, d_ref, wg_ref, bg_ref, o_ref, vn_ref, mixed_ref):
        vhat, _ = _layer_norm_stats(p_ref[:, AW:2 * AW])
        vn_ref[...] = (vhat * vg_ref[...]).astype(BF)
        _spatial_mix(vn_ref, ws_ref, bs_ref, mixed_ref)
        ga = p_ref[:, 2 * AW:3 * AW]
        o_ref[:, 0:AW] = (p_ref[:, 0:AW] * mixed_ref[...] * (ga * _sig(ga))).astype(BF)
        y = _cat_lanes(yf_ref) + _cat_lanes(yb_ref) + d_ref[...] * p_ref[:, 3 * AW:4 * AW]
        y2, _ = _gelu(y)
        z = _dot(y2.astype(BF), wg_ref[...]) + bg_ref[...]
        gb = p_ref[:, 4 * AW:5 * AW]
        o_ref[:, AW:2 * AW] = (y2 * _sig(z) * (gb * _sig(gb))).astype(BF)

    tile = lambda w: pl.BlockSpec((TM, w), lambda i: (i, 0))
    return _call(body, name=name, grid=(t // TM,),
                 in_specs=[tile(5 * AW), Y4_SPEC(), Y4_SPEC(), _full((1, AW)), _full((NGRP, CHUNK, CHUNK)),
                           _full((NGRP, CHUNK, 1)), _full((1, AW)), _full((AW, AW)), _full((1, AW))],
                 out_specs=tile(2 * AW), out_shape=jax.ShapeDtypeStruct((t, 2 * AW), BF),
                 scratch=[pltpu.VMEM((TM, AW), BF), pltpu.VMEM((TM, AW), F32)])(p, yf, yb, vg, ws, bs, dsk, wglu, bglu)


def mix_bwd(p, yf, yb, dmix, vg, ws, bs, dsk, wglu, bglu, name):
    t = p.shape[0]

    def body(p_ref, yf_ref, yb_ref, dm_ref, vg_ref, ws_ref, bs_ref, d_ref, wg_ref, bg_ref,
             dpa_ref, dpgb_ref, dy_ref, dws_ref, dbs_ref, dvg_ref, dd_ref, dwg_ref, dbg_ref,
             vn_ref, mixed_ref, dmx_ref, dvn_ref):
        @pl.when(pl.program_id(0) == 0)
        def _():
            for r in (dws_ref, dbs_ref, dvg_ref, dd_ref, dwg_ref, dbg_ref):
                r[...] = jnp.zeros_like(r)

        vhat, rstd = _layer_norm_stats(p_ref[:, AW:2 * AW])
        vgv = vg_ref[...]
        vn_ref[...] = (vhat * vgv).astype(BF)
        _spatial_mix(vn_ref, ws_ref, bs_ref, mixed_ref)
        u = p_ref[:, 0:AW]
        ga = p_ref[:, 2 * AW:3 * AW]
        sga = _sig(ga)
        dya = dm_ref[:, 0:AW]
        mixed = mixed_ref[...]
        dpa_ref[:, 0:AW] = (dya * mixed * (ga * sga)).astype(BF)
        dpa_ref[:, 2 * AW:3 * AW] = (dya * u * mixed * (sga * (1.0 + ga * (1.0 - sga)))).astype(BF)
        dmx_ref[...] = dya * u * (ga * sga)
        for ch in range(TM // CHUNK):
            rows = slice(ch * CHUNK, (ch + 1) * CHUNK)
            for g in range(NGRP):
                cols = slice(g * CHUNK, (g + 1) * CHUNK)
                dmx = dmx_ref[rows, cols]
                dmxb = dmx.astype(BF)
                dws_ref[g] += _dot(dmxb, vn_ref[rows, cols], 1, 1)
                dbs_ref[g] += jnp.sum(dmx, axis=1, keepdims=True)
                dvn_ref[rows, cols] = _dot(ws_ref[g], dmxb, 0, 0)
        dvn = dvn_ref[...]
        dvg_ref[...] += jnp.sum(dvn * vhat, axis=0, keepdims=True)
        dvh = dvn * vgv
        dpa_ref[:, AW:2 * AW] = (rstd * (dvh - jnp.mean(dvh, axis=-1, keepdims=True)
                                         - vhat * jnp.mean(dvh * vhat, axis=-1, keepdims=True))).astype(BF)

        xs = p_ref[:, 3 * AW:4 * AW]
        y = _cat_lanes(yf_ref) + _cat_lanes(yb_ref) + d_ref[...] * xs
        y2, th = _gelu(y)
        y2b = y2.astype(BF)
        z = _dot(y2b, wg_ref[...]) + bg_ref[...]
        sz = _sig(z)
        gb = p_ref[:, 4 * AW:5 * AW]
        sgb = _sig(gb)
        dyb = dm_ref[:, AW:2 * AW]
        dpgb_ref[...] = (dyb * (y2 * sz) * (sgb * (1.0 + gb * (1.0 - sgb)))).astype(BF)
        dy3 = dyb * (gb * sgb)
        dz = dy3 * y2 * sz * (1.0 - sz)
        dzb = dz.astype(BF)
        dwg_ref[...] += _dot(y2b, dzb, 0, 0)
        dbg_ref[...] += jnp.sum(dz, axis=0, keepdims=True)
        dy2 = dy3 * sz + _dot(dzb, wg_ref[...], 1, 1)
        dgelu = 0.5 * (1.0 + th) + 0.5 * y * (1.0 - th * th) * GELU_K * (1.0 + 3.0 * GELU_C * y * y)
        dy = dy2 * dgelu
        dd_ref[...] += jnp.sum(dy * xs, axis=0, keepdims=True)
        dy_ref[...] = dy

    tile = lambda w: pl.BlockSpec((TM, w), lambda i: (i, 0))
    return _call(body, name=name, grid=(t // TM,),
                 in_specs=[tile(5 * AW), Y4_SPEC(), Y4_SPEC(), tile(2 * AW), _full((1, AW)), _full((NGRP, CHUNK, CHUNK)),
                           _full((NGRP, CHUNK, 1)), _full((1, AW)), _full((AW, AW)), _full((1, AW))],
                 out_specs=[tile(3 * AW), tile(AW), tile(AW), _full((NGRP, CHUNK, CHUNK)), _full((NGRP, CHUNK, 1)),
                            _full((1, AW)), _full((1, AW)), _full((AW, AW)), _full((1, AW))],
                 out_shape=[jax.ShapeDtypeStruct((t, 3 * AW), BF), jax.ShapeDtypeStruct((t, AW), BF),
                            jax.ShapeDtypeStruct((t, AW), F32), jax.ShapeDtypeStruct((NGRP, CHUNK, CHUNK), F32),
                            jax.ShapeDtypeStruct((NGRP, CHUNK, 1), F32), jax.ShapeDtypeStruct((1, AW), F32),
                            jax.ShapeDtypeStruct((1, AW), F32), jax.ShapeDtypeStruct((AW, AW), F32),
                            jax.ShapeDtypeStruct((1, AW), F32)],
                 scratch=[pltpu.VMEM((TM, AW), BF), pltpu.VMEM((TM, AW), F32), pltpu.VMEM((TM, AW), F32),
                          pltpu.VMEM((TM, AW), F32)])(p, yf, yb, dmix, vg, ws, bs, dsk, wglu, bglu)


LN = 512
NBLK = SW // LN
UB = AW // NBLK
PW_ROWS = 32
POW_EXP = [1, 2, 4, 8, 16, 32, 64, 128, 256, 0, 0, 0, 0, 0, 0, 0,
           1, 2, 3, 4, 5, 6, 7, 8, 8, 7, 6, 5, 4, 3, 2, 1]
GROUPS_PER_TILE = TM // 8


def s5_disc(lam_re, lam_im, dt, lam_re_r, lam_im_r, dt_r, b_re, b_im):
    nexp = jnp.asarray(np.array(POW_EXP, np.float32).reshape(PW_ROWS, 1))

    def body(n_ref, lr_ref, li_ref, dt_ref, lrr_ref, lir_ref, dtr_ref, br_ref, bi_ref,
             pr_ref, pi_ref, bbr_ref, bbi_ref):
        for dr in range(2):
            dtl = jnp.exp(dt_ref[dr:dr + 1, :])
            zr = lr_ref[dr:dr + 1, :] * dtl
            zi = li_ref[dr:dr + 1, :] * dtl
            mag = jnp.exp(n_ref[...] * zr)
            ang = n_ref[...] * zi
            pr_ref[dr] = mag * jnp.cos(ang)
            pi_ref[dr] = mag * jnp.sin(ang)
        lr, li, dtv = lrr_ref[...], lir_ref[...], jnp.exp(dtr_ref[...])
        mag = jnp.exp(lr * dtv)
        nr = mag * jnp.cos(li * dtv) - 1.0
        ni = mag * jnp.sin(li * dtv)
        den = lr * lr + li * li
        fr = (nr * lr + ni * li) / den
        fi = (ni * lr - nr * li) / den
        bbr_ref[...] = fr * br_ref[...] - fi * bi_ref[...]
        bbi_ref[...] = fr * bi_ref[...] + fi * br_ref[...]

    rows = lam_re_r.shape[0]
    return _call(body, name="s5_disc",
                 out_shape=[jax.ShapeDtypeStruct((2, PW_ROWS, SW), F32), jax.ShapeDtypeStruct((2, PW_ROWS, SW), F32),
                            jax.ShapeDtypeStruct((rows, SP), F32), jax.ShapeDtypeStruct((rows, SP), F32)])(
        nexp, lam_re, lam_im, dt, lam_re_r, lam_im_r, dt_r, b_re, b_im)


def s5_param_bwd(lam_re_r, lam_im_r, dt_r, b_re, b_im, da_re, da_im, dbb_re, dbb_im):
    rows = lam_re_r.shape[0]
    ng = rows // SH
    seg = jnp.asarray(np.kron(np.eye(ng, dtype=np.float32), np.ones((1, SH), np.float32)))

    def body(seg_ref, lr_ref, li_ref, dt_ref, br_ref, bi_ref, dar_ref, dai_ref, dbbr_ref, dbbi_ref,
             dlr_ref, dli_ref, ddt_ref, dbr_ref, dbi_ref):
        lr, li, dtv = lr_ref[...], li_ref[...], jnp.exp(dt_ref[...])
        mag = jnp.exp(lr * dtv)
        lbr = mag * jnp.cos(li * dtv)
        lbi = mag * jnp.sin(li * dtv)
        den = lr * lr + li * li
        nr, ni = lbr - 1.0, lbi
        fr = (nr * lr + ni * li) / den
        fi = (ni * lr - nr * li) / den
        br, bi = br_ref[...], bi_ref[...]
        gbr, gbi = dbbr_ref[...], dbbi_ref[...]
        dbr_ref[...] = gbr * fr + gbi * fi
        dbi_ref[...] = gbi * fr - gbr * fi
        gfr = gbr * br + gbi * bi
        gfi = gbi * br - gbr * bi
        ilr, ili = lr / den, -li / den
        gnr = gfr * ilr + gfi * ili
        gni = gfi * ilr - gfr * ili
        qr = -(fr * ilr - fi * ili)
        qi = -(fr * ili + fi * ilr)
        glr = gfr * qr + gfi * qi
        gli = gfi * qr - gfr * qi
        first = (lax.broadcasted_iota(jnp.int32, (rows, 1), 0) % SH) == 0
        glbr = gnr + jnp.where(first, dar_ref[...], 0.0)
        glbi = gni + jnp.where(first, dai_ref[...], 0.0)
        gzr = glbr * lbr + glbi * lbi
        gzi = glbi * lbr - glbr * lbi
        glr = glr + gzr * dtv
        gli = gli + gzi * dtv
        gdt = (gzr * lr + gzi * li) * dtv
        hi = lax.Precision.HIGHEST
        sg = seg_ref[...]
        dlr_ref[...] = jnp.dot(sg, glr, precision=hi, preferred_element_type=F32)
        dli_ref[...] = jnp.dot(sg, gli, precision=hi, preferred_element_type=F32)
        ddt_ref[...] = jnp.sum(jnp.dot(sg, gdt, precision=hi, preferred_element_type=F32), axis=1, keepdims=True)

    return _call(body, name="s5_param_bwd",
                 out_shape=[jax.ShapeDtypeStruct((ng, SP), F32), jax.ShapeDtypeStruct((ng, SP), F32),
                            jax.ShapeDtypeStruct((ng, 1), F32), jax.ShapeDtypeStruct((rows, SP), F32),
                            jax.ShapeDtypeStruct((rows, SP), F32)])(
        seg, lam_re_r, lam_im_r, dt_r, b_re, b_im, da_re, da_im, dbb_re, dbb_im)


def _scan_tile(hr_ref, hi_ref, er_ref, ei_ref, st_ref, cr_ref, ci_ref, pr_ref, pi_ref, reverse):
    row8 = lax.broadcasted_iota(jnp.int32, (TM, 1), 0) % 8
    rowe = lax.broadcasted_iota(jnp.int32, (2 * GROUPS_PER_TILE, 1), 0)
    ne = 2 * GROUPS_PER_TILE
    for blk in range(NBLK):
        cols = slice(blk * LN, (blk + 1) * LN)
        hr, hi = hr_ref[:, cols], hi_ref[:, cols]
        for k, s in enumerate((1, 2, 4)):
            ar, ai = pr_ref[k:k + 1, cols], pi_ref[k:k + 1, cols]
            if reverse:
                m = row8 < 8 - s
                sr, si = pltpu.roll(hr, TM - s, 0), pltpu.roll(hi, TM - s, 0)
            else:
                m = row8 >= s
                sr, si = pltpu.roll(hr, s, 0), pltpu.roll(hi, s, 0)
            sr, si = jnp.where(m, sr, 0.0), jnp.where(m, si, 0.0)
            hr, hi = hr + ar * sr - ai * si, hi + ar * si + ai * sr
        hr_ref[:, cols] = hr
        hi_ref[:, cols] = hi
        edge = 0 if reverse else 7
        nq = LN // 128
        for q in range(nq):
            st_ref[q] = hr[:, q * 128:(q + 1) * 128]
            st_ref[nq + q] = hi[:, q * 128:(q + 1) * 128]
        gr = jnp.concatenate([st_ref[q, pl.ds(edge, GROUPS_PER_TILE, stride=8), :] for q in range(nq)], axis=1)
        gi = jnp.concatenate([st_ref[nq + q, pl.ds(edge, GROUPS_PER_TILE, stride=8), :] for q in range(nq)], axis=1)
        zero = jnp.zeros((GROUPS_PER_TILE, LN), F32)
        if reverse:
            er_ref[0:GROUPS_PER_TILE, :] = gr
            ei_ref[0:GROUPS_PER_TILE, :] = gi
            er_ref[GROUPS_PER_TILE:ne, :] = zero
            ei_ref[GROUPS_PER_TILE:ne, :] = zero
            er_ref[GROUPS_PER_TILE:GROUPS_PER_TILE + 1, :] = cr_ref[:, cols]
            ei_ref[GROUPS_PER_TILE:GROUPS_PER_TILE + 1, :] = ci_ref[:, cols]
        else:
            er_ref[0:GROUPS_PER_TILE, :] = zero
            ei_ref[0:GROUPS_PER_TILE, :] = zero
            er_ref[GROUPS_PER_TILE - 1:GROUPS_PER_TILE, :] = cr_ref[:, cols]
            ei_ref[GROUPS_PER_TILE - 1:GROUPS_PER_TILE, :] = ci_ref[:, cols]
            er_ref[GROUPS_PER_TILE:ne, :] = gr
            ei_ref[GROUPS_PER_TILE:ne, :] = gi
        evr, evi = er_ref[...], ei_ref[...]
        s = 1
        k = 3
        while s < ne:
            ar, ai = pr_ref[k:k + 1, cols], pi_ref[k:k + 1, cols]
            if reverse:
                m = rowe < ne - s
                sr, si = pltpu.roll(evr, ne - s, 0), pltpu.roll(evi, ne - s, 0)
            else:
                m = rowe >= s
                sr, si = pltpu.roll(evr, s, 0), pltpu.roll(evi, s, 0)
            sr, si = jnp.where(m, sr, 0.0), jnp.where(m, si, 0.0)
            evr, evi = evr + ar * sr - ai * si, evi + ar * si + ai * sr
            s *= 2
            k += 1
        er_ref[...] = evr
        ei_ref[...] = evi
        if reverse:
            cr_ref[:, cols] = er_ref[0:1, :]
            ci_ref[:, cols] = ei_ref[0:1, :]
            apr, api = pr_ref[24:32, cols], pi_ref[24:32, cols]
        else:
            cr_ref[:, cols] = er_ref[ne - 1:ne, :]
            ci_ref[:, cols] = ei_ref[ne - 1:ne, :]
            apr, api = pr_ref[16:24, cols], pi_ref[16:24, cols]
        for g in range(GROUPS_PER_TILE):
            e_row = g + 1 if reverse else GROUPS_PER_TILE - 1 + g
            kr, ki = er_ref[e_row:e_row + 1, :], ei_ref[e_row:e_row + 1, :]
            rows = slice(8 * g, 8 * g + 8)
            hr_ref[rows, cols] = hr_ref[rows, cols] + apr * kr - api * ki
            hi_ref[rows, cols] = hi_ref[rows, cols] + apr * ki + api * kr


def _tile_order(kind, nt):
    if kind == "fwd":
        return lambda i: i
    if kind == "bwd":
        return lambda i: jnp.where(i == 0, 0, nt - i)
    if kind == "fwd_adj":
        return lambda i: nt - 1 - i
    if kind == "bwd_adj":
        return lambda i: jnp.where(i == nt - 1, 0, i + 1)
    raise ValueError(kind)


def s5_fwd(p, bb_re, bb_im, ct_re, ct_im, pw_re, pw_im, reverse, name):
    t = p.shape[0]
    nt = t // TM
    order = _tile_order("bwd" if reverse else "fwd", nt)

    def body(x_ref, bbr_ref, bbi_ref, ctr_ref, cti_ref, pr_ref, pi_ref, y_ref, hpr_ref, hpi_ref,
             hr_ref, hi_ref, er_ref, ei_ref, st_ref, cr_ref, ci_ref, c0r_ref, c0i_ref):
        @pl.when(pl.program_id(0) == 0)
        def _():
            cr_ref[...] = jnp.zeros_like(cr_ref)
            ci_ref[...] = jnp.zeros_like(ci_ref)

        c0r_ref[...] = cr_ref[...]
        c0i_ref[...] = ci_ref[...]
        xb = x_ref[...].astype(BF)
        for j in range(NBLK):
            cols = slice(j * LN, (j + 1) * LN)
            hr_ref[:, cols] = _dot(xb[:, j * UB:(j + 1) * UB], bbr_ref[j])
            hi_ref[:, cols] = _dot(xb[:, j * UB:(j + 1) * UB], bbi_ref[j])
        _scan_tile(hr_ref, hi_ref, er_ref, ei_ref, st_ref, cr_ref, ci_ref, pr_ref, pi_ref, reverse)
        rowi = lax.broadcasted_iota(jnp.int32, (TM, 1), 0)
        for j in range(NBLK):
            cols = slice(j * LN, (j + 1) * LN)
            hr, hi = hr_ref[:, cols], hi_ref[:, cols]
            y_ref[:, j * UB:(j + 1) * UB] = _dot(hr.astype(BF), ctr_ref[j]) - _dot(hi.astype(BF), cti_ref[j])
            if reverse:
                first = rowi == TM - 1
                sr, si = pltpu.roll(hr, TM - 1, 0), pltpu.roll(hi, TM - 1, 0)
            else:
                first = rowi == 0
                sr, si = pltpu.roll(hr, 1, 0), pltpu.roll(hi, 1, 0)
            hpr_ref[:, cols] = jnp.where(first, c0r_ref[:, cols], sr)
            hpi_ref[:, cols] = jnp.where(first, c0i_ref[:, cols], si)

    state = lambda: pl.BlockSpec((TM, SW), lambda i: (order(i), 0))
    return _call(body, name=name, grid=(nt,),
                 in_specs=[pl.BlockSpec((TM, AW), lambda i: (order(i), 3)),
                           _full((NBLK, UB, LN)), _full((NBLK, UB, LN)), _full((NBLK, LN, UB)), _full((NBLK, LN, UB)),
                           _full((PW_ROWS, SW)), _full((PW_ROWS, SW))],
                 out_specs=[pl.BlockSpec((TM, AW), lambda i: (order(i), 0)), state(), state()],
                 out_shape=[jax.ShapeDtypeStruct((t, AW), F32), jax.ShapeDtypeStruct((t, SW), F32),
                            jax.ShapeDtypeStruct((t, SW), F32)],
                 scratch=[pltpu.VMEM((TM, SW), F32), pltpu.VMEM((TM, SW), F32),
                          pltpu.VMEM((2 * GROUPS_PER_TILE, LN), F32), pltpu.VMEM((2 * GROUPS_PER_TILE, LN), F32),
                          pltpu.VMEM((2 * LN // 128, TM, 128), F32),
                          pltpu.VMEM((1, SW), F32), pltpu.VMEM((1, SW), F32),
                          pltpu.VMEM((1, SW), F32), pltpu.VMEM((1, SW), F32)])(
        p, bb_re, bb_im, ct_re, ct_im, pw_re, pw_im)


def s5_bwd(p, hp_re, hp_im, dy, bb_re, bb_im, ct_re, ct_im, pw_re, pw_im_conj, a_re, a_im, reverse, name):
    t = p.shape[0]
    nt = t // TM
    order = _tile_order("bwd_adj" if reverse else "fwd_adj", nt)

    def body(x_ref, hpr_ref, hpi_ref, dy_ref, bbr_ref, bbi_ref, ctr_ref, cti_ref, pr_ref, pi_ref, ar_ref, ai_ref,
             dx_ref, dar_ref, dai_ref, dbbr_ref, dbbi_ref, dcr_ref, dci_ref,
             gr_ref, gi_ref, er_ref, ei_ref, st_ref, cr_ref, ci_ref):
        @pl.when(pl.program_id(0) == 0)
        def _():
            for r in (cr_ref, ci_ref, dar_ref, dai_ref, dbbr_ref, dbbi_ref, dcr_ref, dci_ref):
                r[...] = jnp.zeros_like(r)

        xb = x_ref[...].astype(BF)
        dyb = dy_ref[...].astype(BF)
        for j in range(NBLK):
            cols = slice(j * LN, (j + 1) * LN)
            gr_ref[:, cols] = _dot(dyb[:, j * UB:(j + 1) * UB], ctr_ref[j], 1, 1)
            gi_ref[:, cols] = -_dot(dyb[:, j * UB:(j + 1) * UB], cti_ref[j], 1, 1)
        _scan_tile(gr_ref, gi_ref, er_ref, ei_ref, st_ref, cr_ref, ci_ref, pr_ref, pi_ref, not reverse)
        for j in range(NBLK):
            cols = slice(j * LN, (j + 1) * LN)
            xj = xb[:, j * UB:(j + 1) * UB]
            dyj = dyb[:, j * UB:(j + 1) * UB]
            hpr, hpi = hpr_ref[:, cols], hpi_ref[:, cols]
            gr, gi = gr_ref[:, cols], gi_ref[:, cols]
            ar, ai = ar_ref[:, cols], ai_ref[:, cols]
            hr = ar * hpr - ai * hpi + _dot(xj, bbr_ref[j])
            hi = ar * hpi + ai * hpr + _dot(xj, bbi_ref[j])
            dar_ref[:, cols] += jnp.sum(gr * hpr + gi * hpi, axis=0, keepdims=True)
            dai_ref[:, cols] += jnp.sum(gi * hpr - gr * hpi, axis=0, keepdims=True)
            grb, gib = gr.astype(BF), gi.astype(BF)
            dcr_ref[j] += _dot(dyj, hr.astype(BF), 0, 0)
            dci_ref[j] += -_dot(dyj, hi.astype(BF), 0, 0)
            dbbr_ref[j] += _dot(xj, grb, 0, 0)
            dbbi_ref[j] += _dot(xj, gib, 0, 0)
            dx_ref[:, j * UB:(j + 1) * UB] = _dot(grb, bbr_ref[j], 1, 1) + _dot(gib, bbi_ref[j], 1, 1)

    state = lambda: pl.BlockSpec((TM, SW), lambda i: (order(i), 0))
    blockd = lambda: _full((NBLK, UB, LN))
    return _call(body, name=name, grid=(nt,),
                 in_specs=[pl.BlockSpec((TM, AW), lambda i: (order(i), 3)), state(), state(),
                           pl.BlockSpec((TM, AW), lambda i: (order(i), 0)),
                           blockd(), blockd(), _full((NBLK, LN, UB)), _full((NBLK, LN, UB)),
                           _full((PW_ROWS, SW)), _full((PW_ROWS, SW)), _full((1, SW)), _full((1, SW))],
                 out_specs=[pl.BlockSpec((TM, AW), lambda i: (order(i), 0)), _full((1, SW)), _full((1, SW)),
                            blockd(), blockd(), blockd(), blockd()],
                 out_shape=[jax.ShapeDtypeStruct((t, AW), F32), jax.ShapeDtypeStruct((1, SW), F32),
                            jax.ShapeDtypeStruct((1, SW), F32)] + [jax.ShapeDtypeStruct((NBLK, UB, LN), F32)] * 4,
                 scratch=[pltpu.VMEM((TM, SW), F32), pltpu.VMEM((TM, SW), F32),
                          pltpu.VMEM((2 * GROUPS_PER_TILE, LN), F32), pltpu.VMEM((2 * GROUPS_PER_TILE, LN), F32),
                          pltpu.VMEM((2 * LN // 128, TM, 128), F32),
                          pltpu.VMEM((1, SW), F32), pltpu.VMEM((1, SW), F32)])(
        p, hp_re, hp_im, dy, bb_re, bb_im, ct_re, ct_im, pw_re, pw_im_conj, a_re, a_im)


def s5_dx_sum(dy, dsk, dxf, dxb, name):
    t = dy.shape[0]

    def body(dy_ref, d_ref, f_ref, b_ref, o_ref):
        o_ref[...] = (dy_ref[...] * d_ref[...] + f_ref[...] + b_ref[...]).astype(BF)

    tile = pl.BlockSpec((TM, AW), lambda i: (i, 0))
    return _call(body, name=name, grid=(t // TM,), in_specs=[tile, _full((1, AW)), tile, tile], out_specs=tile,
                 out_shape=jax.ShapeDtypeStruct((t, AW), BF))(dy, dsk, dxf, dxb)


XS_BLK = 3 * AW // 128


def _load_perm(refs):
    return jnp.concatenate(
        [jnp.concatenate([ref[pl.ds(r, GROUPS_PER_TILE, stride=8), :] for ref in refs], axis=1) for r in range(8)], axis=0)


def _store_perm(out_ref, val):
    for r in range(8):
        for q in range(AW // 128):
            out_ref[q, pl.ds(r, GROUPS_PER_TILE, stride=8), :] = val[r * GROUPS_PER_TILE:(r + 1) * GROUPS_PER_TILE,
                                                                     q * 128:(q + 1) * 128]


def _scan_perm(hr_ref, hi_ref, er_ref, ei_ref, cr_ref, ci_ref, pr_ref, pi_ref, reverse, hpr_ref=None, hpi_ref=None):
    gpt = GROUPS_PER_TILE
    offsets = list(range(8))[::-1] if reverse else list(range(8))
    for blk in range(NBLK):
        cols = slice(blk * LN, (blk + 1) * LN)
        a1r, a1i = pr_ref[0:1, cols], pi_ref[0:1, cols]
        a8r, a8i = pr_ref[3:4, cols], pi_ref[3:4, cols]
        xr = xi = None
        for r in offsets:
            rows = slice(r * gpt, (r + 1) * gpt)
            if xr is None:
                xr, xi = hr_ref[rows, cols], hi_ref[rows, cols]
            else:
                xr, xi = hr_ref[rows, cols] + a1r * xr - a1i * xi, hi_ref[rows, cols] + a1r * xi + a1i * xr
                hr_ref[rows, cols] = xr
                hi_ref[rows, cols] = xi
        kr, ki = cr_ref[:, cols], ci_ref[:, cols]
        for g in (range(gpt - 1, -1, -1) if reverse else range(gpt)):
            er_ref[g:g + 1, :] = kr
            ei_ref[g:g + 1, :] = ki
            kr, ki = xr[g:g + 1, :] + a8r * kr - a8i * ki, xi[g:g + 1, :] + a8r * ki + a8i * kr
        cr_ref[:, cols] = kr
        ci_ref[:, cols] = ki
        cinr, cini = er_ref[...], ei_ref[...]
        for r in range(8):
            rows = slice(r * gpt, (r + 1) * gpt)
            prow = 16 + (7 - r if reverse else r)
            apr, api = pr_ref[prow:prow + 1, cols], pi_ref[prow:prow + 1, cols]
            hr_ref[rows, cols] = hr_ref[rows, cols] + apr * cinr - api * cini
            hi_ref[rows, cols] = hi_ref[rows, cols] + apr * cini + api * cinr
        if hpr_ref is not None:
            for r in range(8):
                rows = slice(r * gpt, (r + 1) * gpt)
                src = r + 1 if reverse else r - 1
                if 0 <= src < 8:
                    hpr_ref[rows, cols] = hr_ref[src * gpt:(src + 1) * gpt, cols]
                    hpi_ref[rows, cols] = hi_ref[src * gpt:(src + 1) * gpt, cols]
                else:
                    hpr_ref[rows, cols] = cinr
                    hpi_ref[rows, cols] = cini


def s5p_fwd(p, bb_re, bb_im, ct_re, ct_im, pw_re, pw_im, reverse, name):
    t = p.shape[0]
    nt = t // TM
    order = _tile_order("bwd" if reverse else "fwd", nt)
    nq = AW // 128

    def body(*refs):
        x_refs = refs[:nq]
        bbr_ref, bbi_ref, ctr_ref, cti_ref, pr_ref, pi_ref, y_ref, hpr_ref, hpi_ref = refs[nq:nq + 9]
        hr_ref, hi_ref, er_ref, ei_ref, cr_ref, ci_ref = refs[nq + 9:]

        @pl.when(pl.program_id(0) == 0)
        def _():
            cr_ref[...] = jnp.zeros_like(cr_ref)
            ci_ref[...] = jnp.zeros_like(ci_ref)

        xb = _load_perm(x_refs).astype(BF)
        for j in range(NBLK):
            cols = slice(j * LN, (j + 1) * LN)
            hr_ref[:, cols] = _dot(xb[:, j * UB:(j + 1) * UB], bbr_ref[j])
            hi_ref[:, cols] = _dot(xb[:, j * UB:(j + 1) * UB], bbi_ref[j])
        _scan_perm(hr_ref, hi_ref, er_ref, ei_ref, cr_ref, ci_ref, pr_ref, pi_ref, reverse, hpr_ref, hpi_ref)
        y = jnp.concatenate(
            [_dot(hr_ref[:, j * LN:(j + 1) * LN].astype(BF), ctr_ref[j])
             - _dot(hi_ref[:, j * LN:(j + 1) * LN].astype(BF), cti_ref[j]) for j in range(NBLK)], axis=1)
        _store_perm(y_ref, y)

    state = lambda: pl.BlockSpec((TM, SW), lambda i: (order(i), 0))
    xspec = lambda q: pl.BlockSpec((TM, 128), lambda i: (order(i), XS_BLK + q))
    return _call(body, name=name, grid=(nt,),
                 in_specs=[xspec(q) for q in range(nq)]
                 + [_full((NBLK, UB, LN)), _full((NBLK, UB, LN)), _full((NBLK, LN, UB)), _full((NBLK, LN, UB)),
                    _full((PW_ROWS, SW)), _full((PW_ROWS, SW))],
                 out_specs=[pl.BlockSpec((nq, TM, 128), lambda i: (0, order(i), 0)), state(), state()],
                 out_shape=[jax.ShapeDtypeStruct((nq, t, 128), F32), jax.ShapeDtypeStruct((t, SW), F32),
                            jax.ShapeDtypeStruct((t, SW), F32)],
                 scratch=[pltpu.VMEM((TM, SW), F32), pltpu.VMEM((TM, SW), F32),
                          pltpu.VMEM((GROUPS_PER_TILE, LN), F32), pltpu.VMEM((GROUPS_PER_TILE, LN), F32),
                          pltpu.VMEM((1, SW), F32), pltpu.VMEM((1, SW), F32)])(
        *([p] * nq), bb_re, bb_im, ct_re, ct_im, pw_re, pw_im)


def s5p_bwd(p, hp_re, hp_im, dy, bb_re, bb_im, ct_re, ct_im, pw_re, pw_im_conj, a_re, a_im, reverse, name):
    t = p.shape[0]
    nt = t // TM
    order = _tile_order("bwd_adj" if reverse else "fwd_adj", nt)
    nq = AW // 128

    def body(*refs):
        x_refs, dy_refs = refs[:nq], refs[nq:2 * nq]
        (hpr_ref, hpi_ref, bbr_ref, bbi_ref, ctr_ref, cti_ref, pr_ref, pi_ref, ar_ref, ai_ref,
         dx_ref, dar_ref, dai_ref, dbbr_ref, dbbi_ref, dcr_ref, dci_ref,
         gr_ref, gi_ref, er_ref, ei_ref, cr_ref, ci_ref) = refs[2 * nq:]

        @pl.when(pl.program_id(0) == 0)
        def _():
            for r in (cr_ref, ci_ref, dar_ref, dai_ref, dbbr_ref, dbbi_ref, dcr_ref, dci_ref):
                r[...] = jnp.zeros_like(r)

        xb = _load_perm(x_refs).astype(BF)
        dyb = _load_perm(dy_refs).astype(BF)
        for j in range(NBLK):
            cols = slice(j * LN, (j + 1) * LN)
            gr_ref[:, cols] = _dot(dyb[:, j * UB:(j + 1) * UB], ctr_ref[j], 1, 1)
            gi_ref[:, cols] = -_dot(dyb[:, j * UB:(j + 1) * UB], cti_ref[j], 1, 1)
        _scan_perm(gr_ref, gi_ref, er_ref, ei_ref, cr_ref, ci_ref, pr_ref, pi_ref, not reverse)
        dxs = []
        for j in range(NBLK):
            cols = slice(j * LN, (j + 1) * LN)
            xj = xb[:, j * UB:(j + 1) * UB]
            dyj = dyb[:, j * UB:(j + 1) * UB]
            hpr, hpi = hpr_ref[:, cols], hpi_ref[:, cols]
            gr, gi = gr_ref[:, cols], gi_ref[:, cols]
            ar, ai = ar_ref[:, cols], ai_ref[:, cols]
            hr = ar * hpr - ai * hpi + _dot(xj, bbr_ref[j])
            hi = ar * hpi + ai * hpr + _dot(xj, bbi_ref[j])
            dar_ref[:, cols] += jnp.sum(gr * hpr + gi * hpi, axis=0, keepdims=True)
            dai_ref[:, cols] += jnp.sum(gi * hpr - gr * hpi, axis=0, keepdims=True)
            grb, gib = gr.astype(BF), gi.astype(BF)
            dcr_ref[j] += _dot(dyj, hr.astype(BF), 0, 0)
            dci_ref[j] += -_dot(dyj, hi.astype(BF), 0, 0)
            dbbr_ref[j] += _dot(xj, grb, 0, 0)
            dbbi_ref[j] += _dot(xj, gib, 0, 0)
            dxs.append(_dot(grb, bbr_ref[j], 1, 1) + _dot(gib, bbi_ref[j], 1, 1))
        _store_perm(dx_ref, jnp.concatenate(dxs, axis=1))

    state = lambda: pl.BlockSpec((TM, SW), lambda i: (order(i), 0))
    blockd = lambda: _full((NBLK, UB, LN))
    xspec = lambda q: pl.BlockSpec((TM, 128), lambda i: (order(i), XS_BLK + q))
    dyspec = lambda q: pl.BlockSpec((TM, 128), lambda i: (order(i), q))
    return _call(body, name=name, grid=(nt,),
                 in_specs=[xspec(q) for q in range(nq)] + [dyspec(q) for q in range(nq)]
                 + [state(), state(), blockd(), blockd(), _full((NBLK, LN, UB)), _full((NBLK, LN, UB)),
                    _full((PW_ROWS, SW)), _full((PW_ROWS, SW)), _full((1, SW)), _full((1, SW))],
                 out_specs=[pl.BlockSpec((nq, TM, 128), lambda i: (0, order(i), 0)), _full((1, SW)), _full((1, SW)),
                            blockd(), blockd(), blockd(), blockd()],
                 out_shape=[jax.ShapeDtypeStruct((nq, t, 128), F32), jax.ShapeDtypeStruct((1, SW), F32),
                            jax.ShapeDtypeStruct((1, SW), F32)] + [jax.ShapeDtypeStruct((NBLK, UB, LN), F32)] * 4,
                 scratch=[pltpu.VMEM((TM, SW), F32), pltpu.VMEM((TM, SW), F32),
                          pltpu.VMEM((GROUPS_PER_TILE, LN), F32), pltpu.VMEM((GROUPS_PER_TILE, LN), F32),
                          pltpu.VMEM((1, SW), F32), pltpu.VMEM((1, SW), F32)])(
        *([p] * nq), *([dy] * nq), hp_re, hp_im, bb_re, bb_im, ct_re, ct_im, pw_re, pw_im_conj, a_re, a_im)


def s5p_dx_sum(dy, dsk, dxf, dxb, name):
    t = dy.shape[0]
    nq = AW // 128

    def body(dy_ref, d_ref, f_ref, b_ref, o_ref):
        o_ref[...] = (dy_ref[...] * d_ref[...] + _cat_lanes(f_ref) + _cat_lanes(b_ref)).astype(BF)

    tile = pl.BlockSpec((TM, AW), lambda i: (i, 0))
    blk4 = pl.BlockSpec((nq, TM, 128), lambda i: (0, i, 0))
    return _call(body, name=name, grid=(t // TM,), in_specs=[tile, _full((1, AW)), blk4, blk4], out_specs=tile,
                 out_shape=jax.ShapeDtypeStruct((t, AW), BF))(dy, dsk, dxf, dxb)


SCALE = HD ** -0.5
NHEAD_NORM = NQ + NKV


def _partner(x):
    half0 = (lax.broadcasted_iota(jnp.int32, (1, HD), 1) % 64) < 32
    return jnp.where(half0, pltpu.roll(x, HD - 32, 1), pltpu.roll(x, 32, 1))


def attn_prep(p, qg, kg, cos, sins, name):
    t = p.shape[0]

    def body(p_ref, qg_ref, kg_ref, cos_ref, sin_ref, o_ref):
        cv, sv = cos_ref[...], sin_ref[...]
        for h in range(NHEAD_NORM):
            cols = slice(h * HD, (h + 1) * HD)
            blk = p_ref[:, cols]
            r = lax.rsqrt(jnp.mean(blk * blk, axis=-1, keepdims=True) + EPS)
            xn = blk * r * (qg_ref[...] if h < NQ else kg_ref[...])
            rot = xn * cv + _partner(xn) * sv
            o_ref[:, cols] = ((rot * SCALE) if h < NQ else rot).astype(BF)
        vcols = slice(NHEAD_NORM * HD, (NHEAD_NORM + NKV) * HD)
        o_ref[:, vcols] = p_ref[:, vcols].astype(BF)

    w = (NHEAD_NORM + NKV) * HD
    tile = lambda ww: pl.BlockSpec((TM, ww), lambda i: (i, 0))
    return _call(body, name=name, grid=(t // TM,),
                 in_specs=[tile(w), _full((1, HD)), _full((1, HD)), tile(HD), tile(HD)],
                 out_specs=tile(w), out_shape=jax.ShapeDtypeStruct((t, w), BF))(p, qg, kg, cos, sins)


def attn_prep_bwd(p, dq, dk, dv, qg, kg, cos, sins, name):
    t = p.shape[0]

    def body(p_ref, dq_ref, dk_ref, dv_ref, qg_ref, kg_ref, cos_ref, sin_ref, o_ref, dqg_ref, dkg_ref):
        @pl.when(pl.program_id(0) == 0)
        def _():
            dqg_ref[...] = jnp.zeros_like(dqg_ref)
            dkg_ref[...] = jnp.zeros_like(dkg_ref)

        cv, sv = cos_ref[...], sin_ref[...]
        for h in range(NHEAD_NORM):
            cols = slice(h * HD, (h + 1) * HD)
            blk = p_ref[:, cols]
            r = lax.rsqrt(jnp.mean(blk * blk, axis=-1, keepdims=True) + EPS)
            xh = blk * r
            if h < NQ:
                drot = dq_ref[:, cols] * SCALE
                gv, dg_ref = qg_ref[...], dqg_ref
            else:
                drot = dk_ref[:, (h - NQ) * HD:(h - NQ + 1) * HD]
                gv, dg_ref = kg_ref[...], dkg_ref
            dxn = drot * cv + _partner(drot * sv)
            dg_ref[...] += jnp.sum(dxn * xh, axis=0, keepdims=True)
            dxh = dxn * gv
            o_ref[:, cols] = (r * (dxh - xh * jnp.mean(dxh * xh, axis=-1, keepdims=True))).astype(BF)
        o_ref[:, NHEAD_NORM * HD:(NHEAD_NORM + NKV) * HD] = dv_ref[...].astype(BF)

    w = (NHEAD_NORM + NKV) * HD
    tile = lambda ww: pl.BlockSpec((TM, ww), lambda i: (i, 0))
    return _call(body, name=name, grid=(t // TM,),
                 in_specs=[tile(w), tile(NQ * HD), tile(NKV * HD), tile(NKV * HD), _full((1, HD)), _full((1, HD)),
                           tile(HD), tile(HD)],
                 out_specs=[tile(w), _full((1, HD)), _full((1, HD))],
                 out_shape=[jax.ShapeDtypeStruct((t, w), BF), jax.ShapeDtypeStruct((1, HD), F32),
                            jax.ShapeDtypeStruct((1, HD), F32)])(p, dq, dk, dv, qg, kg, cos, sins)


KCOL = NQ
VCOL = NQ + NKV
GCOL = (NQ + 2 * NKV)
QPK = NQ // NKV
ATT_KCHUNK = 512


def attn_fwd(qkv, p, name):
    t = qkv.shape[0]

    def body(q_ref, k_ref, v_ref, g_ref, o_ref, mix_ref, lse_ref):
        def attend(nk):
            q = q_ref[...]
            chunks = [(k0, min(k0 + 2 * ATT_KCHUNK, nk)) for k0 in range(0, nk, 2 * ATT_KCHUNK)]
            s_next = _dot(q, k_ref[chunks[0][0]:chunks[0][1], :], 1, 1)
            m = l = acc = None
            for ci, (k0, k1) in enumerate(chunks):
                s = s_next
                if ci + 1 < len(chunks):
                    s_next = _dot(q, k_ref[chunks[ci + 1][0]:chunks[ci + 1][1], :], 1, 1)
                mc = jnp.max(s, axis=-1, keepdims=True)
                m_new = mc if m is None else jnp.maximum(m, mc)
                pe = jnp.exp(s - m_new)
                lc = jnp.sum(pe, axis=-1, keepdims=True)
                pv = _dot(pe.astype(BF), v_ref[k0:k1, :])
                if m is None:
                    l, acc = lc, pv
                else:
                    alpha = jnp.exp(m - m_new)
                    l, acc = alpha * l + lc, alpha * acc + pv
                m = m_new
            o = acc / l
            gt = g_ref[...]
            o_ref[...] = o
            mix_ref[...] = (o * (gt * _sig(gt))).astype(BF)
            lse_ref[...] = jnp.broadcast_to(m + jnp.log(l), (TM, HD))

        pl.when(pl.program_id(1) == 0)(lambda: attend(NC))
        pl.when(pl.program_id(1) > 0)(lambda: attend(t))

    blk = pl.BlockSpec((TM, HD), lambda h, i: (i, h))
    return _call(body, name=name, grid=(NQ, t // TM),
                 in_specs=[blk, pl.BlockSpec((t, HD), lambda h, i: (0, KCOL + h // QPK)),
                           pl.BlockSpec((t, HD), lambda h, i: (0, VCOL + h // QPK)),
                           pl.BlockSpec((TM, HD), lambda h, i: (i, GCOL + h))],
                 out_specs=[blk, blk, blk],
                 out_shape=[jax.ShapeDtypeStruct((t, NQ * HD), F32), jax.ShapeDtypeStruct((t, NQ * HD), BF),
                            jax.ShapeDtypeStruct((t, NQ * HD), F32)])(qkv, qkv, qkv, p)


def attn_bwd(qkv, p, dmix, o, lse, name):
    t = qkv.shape[0]

    def body(q_ref, k_ref, v_ref, g_ref, dm_ref, o_ref, lse_ref, dq_ref, dg_ref, dk_ref, dv_ref):
        i = pl.program_id(2)

        @pl.when((pl.program_id(1) == 0) & (i == 0))
        def _():
            dk_ref[...] = jnp.zeros_like(dk_ref)
            dv_ref[...] = jnp.zeros_like(dv_ref)

        gt = g_ref[...]
        sg = _sig(gt)
        ov = o_ref[...]
        dmv = dm_ref[...]
        dg_ref[...] = (dmv * ov * (sg * (1.0 + gt * (1.0 - sg)))).astype(BF)
        do = dmv * (gt * sg)
        dr = jnp.sum(do * ov, axis=-1, keepdims=True)
        dob = do.astype(BF)

        def bwd(nk):
            q = q_ref[...]
            lse = lse_ref[:, 0:1]
            chunks = [slice(k0, min(k0 + ATT_KCHUNK, nk)) for k0 in range(0, nk, ATT_KCHUNK)]
            nxt = (_dot(q, k_ref[chunks[0], :], 1, 1), _dot(dob, v_ref[chunks[0], :], 1, 1))
            dq = None
            for ci, keys in enumerate(chunks):
                s, dp = nxt
                if ci + 1 < len(chunks):
                    nxt = (_dot(q, k_ref[chunks[ci + 1], :], 1, 1), _dot(dob, v_ref[chunks[ci + 1], :], 1, 1))
                pe = jnp.exp(s - lse)
                dsb = (pe * (dp - dr)).astype(BF)
                part = _dot(dsb, k_ref[keys, :])
                dq = part if dq is None else dq + part
                dv_ref[keys, :] += _dot(pe.astype(BF), dob, 0, 0)
                dk_ref[keys, :] += _dot(dsb, q, 0, 0)
            dq_ref[...] = dq

        pl.when(i == 0)(lambda: bwd(NC))
        pl.when(i > 0)(lambda: bwd(t))

    blk = pl.BlockSpec((TM, HD), lambda kv, g, i: (i, kv * QPK + g))
    acc = pl.BlockSpec((t, HD), lambda kv, g, i: (0, kv))
    return _call(body, name=name, grid=(NKV, QPK, t // TM),
                 in_specs=[blk, pl.BlockSpec((t, HD), lambda kv, g, i: (0, KCOL + kv)),
                           pl.BlockSpec((t, HD), lambda kv, g, i: (0, VCOL + kv)),
                           pl.BlockSpec((TM, HD), lambda kv, g, i: (i, GCOL + kv * QPK + g)), blk, blk, blk],
                 out_specs=[blk, blk, acc, acc],
                 out_shape=[jax.ShapeDtypeStruct((t, NQ * HD), F32), jax.ShapeDtypeStruct((t, NQ * HD), BF),
                            jax.ShapeDtypeStruct((t, NKV * HD), F32), jax.ShapeDtypeStruct((t, NKV * HD), F32)])(
        qkv, qkv, qkv, p, dmix, o, lse)


def _row_tile(rows, row_bytes, cap=2 * 1024 * 1024):
    if rows * row_bytes <= cap or rows % 8:
        return rows
    tr = rows
    while tr * row_bytes > cap and tr % 16 == 0:
        tr //= 2
    return tr


def adamw(w, g, m, v, name):
    r, cdim = w.shape
    tr = _row_tile(r, 4 * max(cdim, 128))

    def body(w_ref, g_ref, m_ref, v_ref, d_ref, nm_ref, nv_ref):
        gv = g_ref[...]
        m2 = ADAM_B1 * m_ref[...] + (1.0 - ADAM_B1) * gv
        v2 = ADAM_B2 * v_ref[...] + (1.0 - ADAM_B2) * (gv * gv)
        mh = m2 / (1.0 - ADAM_B1 ** ADAM_STEP)
        vh = v2 / (1.0 - ADAM_B2 ** ADAM_STEP)
        d_ref[...] = -ADAM_LR * (mh / (jnp.sqrt(vh) + ADAM_EPS) + ADAM_WD * w_ref[...])
        nm_ref[...] = m2
        nv_ref[...] = v2

    tile = pl.BlockSpec((tr, cdim), lambda i: (i, 0))
    sh = jax.ShapeDtypeStruct((r, cdim), F32)
    return _call(body, name=name, grid=(r // tr,), in_specs=[tile] * 4, out_specs=[tile] * 3,
                 out_shape=[sh, sh, sh])(w, g, m, v)


def sum_lead(a, name, out_dtype=F32):
    n, r, cdim = a.shape
    tr = _row_tile(r, 4 * n * max(cdim, 128))

    def body(a_ref, o_ref):
        acc = a_ref[0].astype(F32)
        for k in range(1, n):
            acc = acc + a_ref[k].astype(F32)
        o_ref[...] = acc.astype(o_ref.dtype)

    return _call(body, name=name, grid=(r // tr,),
                 in_specs=[pl.BlockSpec((n, tr, cdim), lambda i: (0, i, 0))],
                 out_specs=pl.BlockSpec((tr, cdim), lambda i: (i, 0)),
                 out_shape=jax.ShapeDtypeStruct((r, cdim), out_dtype))(a)


_FLIPS = {"xy": [(1, 0, 0), (0, 1, 0), (1, 1, 0)], "c": [(0, 0, 1)],
          "all": [(0, 0, 1), (0, 1, 0), (0, 1, 1), (1, 0, 0), (1, 0, 1), (1, 1, 0), (1, 1, 1)]}
_GROUP_SIZE = {"xy": 4, "c": 2, "all": 8}


def _group_index(group, x, y, c):
    return {"xy": 2 * x + y, "c": c, "all": 4 * x + 2 * y + c}[group]


def exchange(items, name):
    plan = []
    for arr, group, kind in items:
        chunk = arr.shape if kind == "gather" else arr.shape[1:]
        plan.append((group, kind, chunk))
    ncopy = sum(len(_FLIPS[g]) for g, _, _ in plan)
    nitem = len(plan)

    def body(*refs):
        srcs, dsts = refs[:nitem], refs[nitem:2 * nitem]
        send_sems, recv_sems, local_sems = refs[2 * nitem:]
        x, y, c = lax.axis_index("x"), lax.axis_index("y"), lax.axis_index("c")
        sends, recvs, locals_ = [], [], []
        n = 0
        for k, (group, kind, _) in enumerate(plan):
            me = _group_index(group, x, y, c)
            own = srcs[k] if kind == "gather" else srcs[k].at[me]
            locals_.append(pltpu.make_async_copy(own, dsts[k].at[me], local_sems.at[k]))
            for fx, fy, fc in _FLIPS[group]:
                px, py, pc = (1 - x if fx else x), (1 - y if fy else y), (1 - c if fc else c)
                peer = _group_index(group, px, py, pc)
                src = srcs[k] if kind == "gather" else srcs[k].at[peer]
                sends.append(pltpu.make_async_remote_copy(
                    src_ref=src, dst_ref=dsts[k].at[me], send_sem=send_sems.at[n], recv_sem=recv_sems.at[n],
                    device_id=(px, py, pc), device_id_type=MESH))
                recvs.append(pltpu.make_async_remote_copy(
                    src_ref=src, dst_ref=dsts[k].at[peer], send_sem=send_sems.at[n], recv_sem=recv_sems.at[n],
                    device_id=(px, py, pc), device_id_type=MESH))
                n += 1
        for cp in locals_ + sends:
            cp.start()
        for cp in recvs:
            cp.wait_recv()
        for cp in sends:
            cp.wait_send()
        for cp in locals_:
            cp.wait()

    anyspec = pl.BlockSpec(memory_space=pl.ANY)
    outs = [jax.ShapeDtypeStruct((_GROUP_SIZE[g],) + tuple(chunk), arr.dtype)
            for (arr, _, _), (g, _, chunk) in zip(items, plan)]
    res = pl.pallas_call(
        body, name=name, out_shape=outs, in_specs=[anyspec] * nitem, out_specs=[anyspec] * nitem,
        scratch_shapes=[pltpu.SemaphoreType.DMA((ncopy,)), pltpu.SemaphoreType.DMA((ncopy,)),
                        pltpu.SemaphoreType.DMA((nitem,))],
        compiler_params=pltpu.CompilerParams(has_side_effects=True))(*[a for a, _, _ in items])
    return list(res)


D2D_PIECES = 4


def d2d(items, name):
    n = len(items)
    swaps = [k for k, (_, kind) in enumerate(items) if kind == "swap"]

    def pieces_of(rows):
        npc = D2D_PIECES if rows % (8 * D2D_PIECES) == 0 else 1
        return npc, rows // npc

    ncopy = sum(pieces_of(a.shape[0] if kind == "gather" else a.shape[1])[0] for a, kind in items)

    def body(*refs):
        srcs, outs = refs[:n], refs[n:2 * n]
        stages = dict(zip(swaps, refs[2 * n:2 * n + len(swaps)]))
        send_sems, recv_sems, local_sems = refs[2 * n + len(swaps):]
        x, y, c = lax.axis_index("x"), lax.axis_index("y"), lax.axis_index("c")
        sib = (x, y, 1 - c)

        def remote(src, dst, q):
            return pltpu.make_async_remote_copy(src_ref=src, dst_ref=dst, send_sem=send_sems.at[q],
                                                recv_sem=recv_sems.at[q], device_id=sib, device_id_type=MESH)

        copies = []
        q = 0
        for k, (arr, kind) in enumerate(items):
            npc, pr = pieces_of(arr.shape[0] if kind == "gather" else arr.shape[1])
            for pc in range(npc):
                rs = pl.ds(pc * pr, pr)
                if kind == "gather":
                    mine, theirs = outs[k].at[c, rs], outs[k].at[1 - c, rs]
                    copies.append((pltpu.make_async_copy(srcs[k].at[rs], mine, local_sems.at[q]),
                                   remote(mine, mine, q), remote(theirs, theirs, q)))
                else:
                    stage, land = stages[k].at[rs], outs[k].at[rs]
                    copies.append((pltpu.make_async_copy(srcs[k].at[1 - c, rs], stage, local_sems.at[q]),
                                   remote(stage, land, q), remote(stage, land, q)))
                q += 1
        for loc, _, _ in copies:
            loc.start()
        for loc, send, _ in copies:
            loc.wait()
            send.start()
        for _, _, recv in copies:
            recv.wait_recv()
        for _, send, _ in copies:
            send.wait_send()

    outs = [jax.ShapeDtypeStruct((2,) + a.shape if kind == "gather" else a.shape[1:], a.dtype) for a, kind in items]
    res = pl.pallas_call(
        body, name=name, out_shape=outs, in_specs=[pl.BlockSpec(memory_space=pl.ANY)] * n,
        out_specs=[pl.BlockSpec(memory_space=pltpu.VMEM)] * n,
        scratch_shapes=[pltpu.VMEM(items[k][0].shape[1:], items[k][0].dtype) for k in swaps]
        + [pltpu.SemaphoreType.DMA((ncopy,)), pltpu.SemaphoreType.DMA((ncopy,)), pltpu.SemaphoreType.DMA((ncopy,))],
        compiler_params=pltpu.CompilerParams(has_side_effects=True, vmem_limit_bytes=VMEM_LIMIT_BYTES))(
        *[a for a, _ in items])
    return list(res)


def sum_own(pair, got, name, out_dtype=F32):
    _, r, cdim = pair.shape
    tr = _row_tile(r, 4 * 3 * max(cdim, 128))

    def body(p_ref, g_ref, o_ref):
        me = lax.axis_index("c")
        o_ref[...] = (p_ref[me] + g_ref[...]).astype(o_ref.dtype)

    return _call(body, name=name, grid=(r // tr,),
                 in_specs=[pl.BlockSpec((2, tr, cdim), lambda i: (0, i, 0)), pl.BlockSpec((tr, cdim), lambda i: (i, 0))],
                 out_specs=pl.BlockSpec((tr, cdim), lambda i: (i, 0)),
                 out_shape=jax.ShapeDtypeStruct((r, cdim), out_dtype))(pair, got)


_SMALL = ["c_ctx", "norm_g", "b_mod", "gm_v_g", "gm_w_s", "gm_b_s", "s5_lam_re", "s5_lam_im", "s5_log_dt",
          "s5_b_re", "s5_b_im", "s5_c_re", "s5_c_im", "s5_d", "s5_b_glu", "q_norm_g", "k_norm_g", "final_g"]
_BIG = ["we_in", "we_out", "s5_w_glu", "wo_in", "wo_out"]
_WEIGHTS = ["c_ctx", "norm_g", "w_mod", "b_mod", "we_in", "we_out", "gm_v_g", "gm_w_s", "gm_b_s", "s5_lam_re",
            "s5_lam_im", "s5_log_dt", "s5_b_re", "s5_b_im", "s5_c_re", "s5_c_im", "s5_d", "s5_w_glu", "s5_b_glu",
            "wo_in", "wo_out", "q_norm_g", "k_norm_g", "final_g"]
_SMALL_ALIGN = 8 * 8 * 128


def _rope_tables(n_lat):
    rows = n_lat // GRID_W
    row = jnp.repeat(jnp.arange(rows), GRID_W)
    col = jnp.tile(jnp.arange(GRID_W), rows)
    freqs = ROPE_THETA ** (-jnp.arange(HD // 4, dtype=F32) / (HD // 4))
    ar, ac = row[:, None] * freqs, col[:, None] * freqs
    cos = jnp.concatenate([jnp.cos(ar), jnp.cos(ar), jnp.cos(ac), jnp.cos(ac)], axis=1)
    sins = jnp.concatenate([-jnp.sin(ar), jnp.sin(ar), -jnp.sin(ac), jnp.sin(ac)], axis=1)
    cos = jnp.concatenate([jnp.ones((NC, HD), F32), cos], axis=0)
    sins = jnp.concatenate([jnp.zeros((NC, HD), F32), sins], axis=0)
    return cos, sins


def _block_diag(v, transpose):
    gpb = SG // NBLK
    v = v.reshape(2, NBLK, gpb, SH, SP)
    eye = jnp.eye(gpb, dtype=v.dtype)
    if transpose:
        return jnp.einsum("djahp,ab->djapbh", v, eye).reshape(2, NBLK, LN, UB)
    return jnp.einsum("djahp,ab->djahbp", v, eye).reshape(2, NBLK, UB, LN)


def _diag_blocks(m):
    gpb = SG // NBLK
    return jnp.einsum("jahap->jahp", m.reshape(NBLK, gpb, SH, gpb, SP)).reshape(SG, SH, SP)


def _view2d(a):
    if a.ndim == 1:
        return a.reshape(1, -1)
    if a.shape[-1] < 64 and a.size % 1024 == 0:
        return a.reshape(-1, 1024)
    return a.reshape(-1, a.shape[-1])


def kernel(x, c, ctx, c_ctx, norm_g, w_mod, b_mod, we_in, we_out, gm_v_g, gm_w_s, gm_b_s, s5_lam_re, s5_lam_im, s5_log_dt, s5_b_re, s5_b_im, s5_c_re, s5_c_im, s5_d, s5_w_glu, s5_b_glu, wo_in, wo_out, q_norm_g, k_norm_g, final_g, loss_target, m_c_ctx, m_norm_g, m_w_mod, m_b_mod, m_we_in, m_we_out, m_gm_v_g, m_gm_w_s, m_gm_b_s, m_s5_lam_re, m_s5_lam_im, m_s5_log_dt, m_s5_b_re, m_s5_b_im, m_s5_c_re, m_s5_c_im, m_s5_d, m_s5_w_glu, m_s5_b_glu, m_wo_in, m_wo_out, m_q_norm_g, m_k_norm_g, m_final_g, v_c_ctx, v_norm_g, v_w_mod, v_b_mod, v_we_in, v_we_out, v_gm_v_g, v_gm_w_s, v_gm_b_s, v_s5_lam_re, v_s5_lam_im, v_s5_log_dt, v_s5_b_re, v_s5_b_im, v_s5_c_re, v_s5_c_im, v_s5_d, v_s5_w_glu, v_s5_b_glu, v_wo_in, v_wo_out, v_q_norm_g, v_k_norm_g, v_final_g):
    weights = dict(c_ctx=c_ctx, norm_g=norm_g, w_mod=w_mod, b_mod=b_mod, we_in=we_in, we_out=we_out, gm_v_g=gm_v_g,
                   gm_w_s=gm_w_s, gm_b_s=gm_b_s, s5_lam_re=s5_lam_re, s5_lam_im=s5_lam_im, s5_log_dt=s5_log_dt,
                   s5_b_re=s5_b_re, s5_b_im=s5_b_im, s5_c_re=s5_c_re, s5_c_im=s5_c_im, s5_d=s5_d, s5_w_glu=s5_w_glu,
                   s5_b_glu=s5_b_glu, wo_in=wo_in, wo_out=wo_out, q_norm_g=q_norm_g, k_norm_g=k_norm_g,
                   final_g=final_g)
    mom_m = dict(c_ctx=m_c_ctx, norm_g=m_norm_g, w_mod=m_w_mod, b_mod=m_b_mod, we_in=m_we_in, we_out=m_we_out,
                 gm_v_g=m_gm_v_g, gm_w_s=m_gm_w_s, gm_b_s=m_gm_b_s, s5_lam_re=m_s5_lam_re, s5_lam_im=m_s5_lam_im,
                 s5_log_dt=m_s5_log_dt, s5_b_re=m_s5_b_re, s5_b_im=m_s5_b_im, s5_c_re=m_s5_c_re, s5_c_im=m_s5_c_im,
                 s5_d=m_s5_d, s5_w_glu=m_s5_w_glu, s5_b_glu=m_s5_b_glu, wo_in=m_wo_in, wo_out=m_wo_out,
                 q_norm_g=m_q_norm_g, k_norm_g=m_k_norm_g, final_g=m_final_g)
    mom_v = dict(c_ctx=v_c_ctx, norm_g=v_norm_g, w_mod=v_w_mod, b_mod=v_b_mod, we_in=v_we_in, we_out=v_we_out,
                 gm_v_g=v_gm_v_g, gm_w_s=v_gm_w_s, gm_b_s=v_gm_b_s, s5_lam_re=v_s5_lam_re, s5_lam_im=v_s5_lam_im,
                 s5_log_dt=v_s5_log_dt, s5_b_re=v_s5_b_re, s5_b_im=v_s5_b_im, s5_c_re=v_s5_c_re, s5_c_im=v_s5_c_im,
                 s5_d=v_s5_d, s5_w_glu=v_s5_w_glu, s5_b_glu=v_s5_b_glu, wo_in=v_wo_in, wo_out=v_wo_out,
                 q_norm_g=v_q_norm_g, k_norm_g=v_k_norm_g, final_g=v_final_g)

    ixy = 2 * lax.axis_index("x") + lax.axis_index("y")
    n_lat = x.shape[1]
    nl = norm_g.shape[0]
    nmod = w_mod.shape[2]
    xin = jnp.concatenate([ctx[0], x[0]], axis=0)

    ic = lax.axis_index("c")
    mine = [lax.dynamic_index_in_dim(weights[n], ic, 0, keepdims=False).astype(BF) for n in _BIG]
    got = exchange([(m_, "xy", "gather") for m_ in mine] + [(c, "xy", "gather")], "gather_weights")
    both = d2d([(g_.reshape(-1, g_.shape[-1]), "gather") for g_ in got[:len(_BIG)]], "swap_weights")
    both = [b_.reshape((2,) + g_.shape) for b_, g_ in zip(both, got)]
    wein = [both[0][0], both[0][1]]
    weout = [both[1][l].reshape(1, D, D) for l in range(2)]
    wglu = [both[2][l].reshape(AW, AW) for l in range(2)]
    woin = [both[3][0], both[3][1]]
    woout = [both[4][l].reshape(1, D, D) for l in range(2)]
    c_group = got[len(_BIG)].reshape(4, D)

    cond = jnp.concatenate([c_group, jnp.broadcast_to(c_ctx.reshape(1, D), (4, D))], axis=0)
    b_shard = lax.dynamic_slice(b_mod, (0, ixy * nmod), (nl, nmod)).reshape(nl, 1, nmod)
    mpart = ada_fwd(cond, w_mod, b_shard)
    m_lat, m_ctx = exchange([(jnp.transpose(mpart[:, 0:4], (1, 0, 2)), "xy", "scatter"),
                             (mpart[:, 4], "xy", "gather")], "exchange_mod")
    m_lat = jnp.transpose(m_lat, (1, 0, 2)).reshape(nl, 3, D)
    m_ctx = jnp.transpose(m_ctx, (1, 0, 2)).reshape(nl, 3, D)
    mods = [jnp.stack([m_ctx[l], m_lat[l]], axis=0) for l in range(nl)]

    loss_part, dx, g, d_norm_g, d_mod_lat, d_mod_ctx, d_final_g = _local_step(
        xin, loss_target[0], mods, wein, weout, wglu, woin, woout, weights)
    grad_x = dx[NC:].reshape(1, n_lat, D)

    d_mod_lat, d_mod_ctx = jnp.stack(d_mod_lat), jnp.stack(d_mod_ctx)
    dm_send = jnp.stack([d_mod_lat.reshape(nl, 4, nmod), d_mod_ctx.reshape(nl, 4, nmod)])
    (dm_got,) = exchange([(jnp.transpose(dm_send, (2, 0, 1, 3)), "xy", "scatter")], "exchange_dmod")
    dm_rows = jnp.concatenate([dm_got[:, 0], dm_got[:, 1]], axis=0)
    gw_mod, d_cctx = ada_bwd(cond, jnp.transpose(dm_rows, (1, 0, 2)), w_mod)
    g_small = dict(c_ctx=d_cctx.reshape(D), norm_g=jnp.stack(d_norm_g), b_mod=add2(d_mod_lat, d_mod_ctx, "add_dbmod"),
                   final_g=d_final_g.reshape(D))
    for name in _SMALL:
        if name not in g_small:
            g_small[name] = jnp.stack(g[name])

    flat = jnp.concatenate([g_small[n].reshape(-1) for n in _SMALL])
    nflat = flat.shape[0]
    npad = -(-nflat // _SMALL_ALIGN) * _SMALL_ALIGN
    flat = jnp.concatenate([flat, jnp.zeros((npad - nflat,), F32)]).reshape(8, npad // (8 * 128), 128)
    pairs = [gw_mod.reshape(2, nl // 2 * D, nmod)]
    for name in _BIG:
        st = jnp.stack(g[name]) if isinstance(g[name], list) else g[name]
        pairs.append(st.reshape(2, -1, st.shape[-1]))
    got_a = d2d([(pairs[k], "swap") for k in (1, 2, 3)], "reduce_chip_a")
    got_b = d2d([(pairs[k], "swap") for k in (0, 4, 5)], "reduce_chip_b")
    theirs = [got_b[0]] + got_a + got_b[1:]
    chip = [sum_own(pairs[k], theirs[k], f"sum_chip{k}", F32 if k == 0 else BF) for k in range(len(pairs))]
    parts = exchange([(flat, "all", "scatter")]
                     + [(s_.reshape(4, s_.shape[0] // 4, s_.shape[1]), "xy", "scatter") for s_ in chip[1:]],
                     "reduce_scatter")
    sums = [sum_lead(pt, f"sum_shard{k}") for k, pt in enumerate(parts)]
    (flat_full,) = exchange([(sums[0], "all", "gather")], "gather_small")
    full = [flat_full] + d2d([(chip[0], "gather")] + [(s_, "gather") for s_ in sums[1:]], "all_gather")
    flat = full[0].reshape(-1)
    grads = {}
    off = 0
    for name in _SMALL:
        sz = weights[name].size
        grads[name] = flat[off:off + sz].reshape(weights[name].shape)
        off += sz
    grads["w_mod"] = full[1].reshape(w_mod.shape)
    for k, name in enumerate(_BIG):
        grads[name] = full[2 + k].reshape(weights[name].shape)

    delta, new_m, new_v = {}, {}, {}
    for name in _WEIGHTS:
        w2 = _view2d(weights[name])
        d2, m2, v2 = adamw(w2, grads[name].reshape(w2.shape), mom_m[name].reshape(w2.shape),
                           mom_v[name].reshape(w2.shape), f"adamw_{name}")
        shp = weights[name].shape
        delta[name], new_m[name], new_v[name] = d2.reshape(shp), m2.reshape(shp), v2.reshape(shp)

    loss = lax.psum(loss_part[0, 0], ("x", "y", "c"))
    return (loss, grad_x, *[grads[n] for n in _WEIGHTS], *[delta[n] for n in _WEIGHTS],
            *[new_m[n] for n in _WEIGHTS], *[new_v[n] for n in _WEIGHTS])


def _local_step(xin, target, mods, wein, weout, wglu, woin, woout, w):
    norm_g, gm_v_g, gm_w_s, gm_b_s = w["norm_g"], w["gm_v_g"], w["gm_w_s"], w["gm_b_s"]
    s5_lam_re, s5_lam_im, s5_log_dt = w["s5_lam_re"], w["s5_lam_im"], w["s5_log_dt"]
    s5_b_re, s5_b_im, s5_c_re, s5_c_im = w["s5_b_re"], w["s5_b_im"], w["s5_c_re"], w["s5_c_im"]
    s5_d, s5_b_glu, q_norm_g, k_norm_g, final_g = w["s5_d"], w["s5_b_glu"], w["q_norm_g"], w["k_norm_g"], w["final_g"]
    nl = norm_g.shape[0]
    n_lat = xin.shape[0] - NC

    cos, sins = _rope_tables(n_lat)

    s5p = []
    for i in range(2):
        lam_l = (s5_lam_re[i].reshape(2, SW), s5_lam_im[i].reshape(2, SW),
                 jnp.repeat(s5_log_dt[i], SP, axis=1))
        lam_r = (jnp.repeat(s5_lam_re[i].reshape(2 * SG, SP), SH, axis=0),
                 jnp.repeat(s5_lam_im[i].reshape(2 * SG, SP), SH, axis=0),
                 jnp.repeat(s5_log_dt[i].reshape(2 * SG, 1), SH, axis=0))
        b_r = (jnp.transpose(s5_b_re[i], (0, 1, 3, 2)).reshape(2 * SG * SH, SP),
               jnp.transpose(s5_b_im[i], (0, 1, 3, 2)).reshape(2 * SG * SH, SP))
        pw_re, pw_im, bbr, bbi = s5_disc(*lam_l, *lam_r, *b_r)
        s5p.append(dict(
            lam_r=lam_r, b_r=b_r, pw_re=pw_re, pw_im=pw_im,
            bb_re=_block_diag(bbr.reshape(2, SG, SH, SP), False).astype(BF),
            bb_im=_block_diag(bbi.reshape(2, SG, SH, SP), False).astype(BF),
            ct_re=_block_diag(s5_c_re[i], True).astype(BF), ct_im=_block_diag(s5_c_im[i], True).astype(BF)))

    saved = []
    xcur = xin
    for l in range(nl):
        i = l // 2
        h = pro_fwd(xcur, norm_g[l].reshape(1, D), mods[l], f"pro_fwd{l}")
        sv = dict(x=xcur, h=h)
        if l % 2 == 0:
            p = mm_nn(h, wein[i], f"in_proj{l}")
            sp = s5p[i]
            for dr, rev in ((0, False), (1, True)):
                sv[f"y{dr}"], sv[f"hpr{dr}"], sv[f"hpi{dr}"] = s5p_fwd(
                    p, sp["bb_re"][dr], sp["bb_im"][dr], sp["ct_re"][dr], sp["ct_im"][dr],
                    sp["pw_re"][dr], sp["pw_im"][dr], rev, f"s5_fwd{l}_{dr}")
            mix = mix_fwd(p, sv["y0"], sv["y1"], gm_v_g[i].reshape(1, AW), gm_w_s[i].astype(BF),
                          gm_b_s[i].reshape(NGRP, CHUNK, 1), s5_d[i].reshape(1, AW), wglu[i],
                          s5_b_glu[i].reshape(1, AW), f"mix_fwd{l}")
            o = mm_nn(mix, weout[i], f"out_proj{l}")
        else:
            p = mm_nn(h, woin[i], f"in_proj{l}")
            sv["qkv"] = attn_prep(p, q_norm_g[i].reshape(1, HD), k_norm_g[i].reshape(1, HD), cos, sins, f"attn_prep{l}")
            sv["o_att"], mix, sv["lse"] = attn_fwd(sv["qkv"], p, f"attn_fwd{l}")
            o = mm_nn(mix, woout[i], f"out_proj{l}")
        sv.update(p=p, mix=mix, o=o)
        saved.append(sv)
        xcur = res_fwd(xcur, o, mods[l], l < nl - 1, f"res_fwd{l}")

    loss_part, dx, d_final_g = final_loss(xcur, target, final_g.reshape(1, D))

    g = {}
    gbuf = {}
    d_norm_g, d_mod_lat, d_mod_ctx = [None] * nl, [None] * nl, [None] * nl
    for name in ("s5_w_glu", "gm_v_g", "gm_w_s", "gm_b_s", "s5_lam_re", "s5_lam_im",
                 "s5_log_dt", "s5_b_re", "s5_b_im", "s5_c_re", "s5_c_im", "s5_d", "s5_b_glu", "q_norm_g", "k_norm_g"):
        g[name] = [None, None]
    for l in reversed(range(nl)):
        i = l // 2
        sv = saved[l]
        do, dgt = res_bwd(dx, sv["o"], mods[l], l < nl - 1, f"res_bwd{l}")
        w_out = weout[i] if l % 2 == 0 else woout[i]
        dmix = mm_nt(do, w_out, f"out_dgrad{l}")
        out_name, in_name = ("we_out", "we_in") if l % 2 == 0 else ("wo_out", "wo_in")
        gbuf[out_name] = mm_tn(sv["mix"], do, 1, f"out_wgrad{l}", slot=i, into=gbuf.get(out_name))
        if l % 2 == 0:
            sp = s5p[i]
            (dpa, dpgb, dy, g["gm_w_s"][i], dbs, dvg, dd, g["s5_w_glu"][i], dbg) = mix_bwd(
                sv["p"], sv["y0"], sv["y1"], dmix, gm_v_g[i].reshape(1, AW), gm_w_s[i].astype(BF),
                gm_b_s[i].reshape(NGRP, CHUNK, 1), s5_d[i].reshape(1, AW), wglu[i], s5_b_glu[i].reshape(1, AW),
                f"mix_bwd{l}")
            g["gm_b_s"][i], g["gm_v_g"][i] = dbs.reshape(NGRP, CHUNK), dvg.reshape(AW)
            g["s5_d"][i], g["s5_b_glu"][i] = dd.reshape(AW), dbg.reshape(AW)
            g["s5_w_glu"][i] = g["s5_w_glu"][i].reshape(4, AW // 4, AW)
            dxd, das_r, das_i, dbbs_r, dbbs_i, dcs_r, dcs_i = [], [], [], [], [], [], []
            for dr, rev in ((0, False), (1, True)):
                dxs_d, da_r, da_i, dbb_r, dbb_i, dc_r, dc_i = s5p_bwd(
                    sv["p"], sv[f"hpr{dr}"], sv[f"hpi{dr}"], dy, sp["bb_re"][dr], sp["bb_im"][dr],
                    sp["ct_re"][dr], sp["ct_im"][dr], sp["pw_re"][dr], -sp["pw_im"][dr],
                    sp["pw_re"][dr][0:1], sp["pw_im"][dr][0:1], rev, f"s5_bwd{l}_{dr}")
                dxd.append(dxs_d)
                das_r.append(jnp.repeat(da_r.reshape(SG, SP), SH, axis=0))
                das_i.append(jnp.repeat(da_i.reshape(SG, SP), SH, axis=0))
                dbbs_r.append(_diag_blocks(dbb_r).reshape(SG * SH, SP))
                dbbs_i.append(_diag_blocks(dbb_i).reshape(SG * SH, SP))
                dcs_r.append(_diag_blocks(dc_r))
                dcs_i.append(_diag_blocks(dc_i))
            cat = lambda parts: jnp.concatenate(parts, axis=0)
            dlr, dli, dldt, dbr, dbi = s5_param_bwd(*sp["lam_r"], *sp["b_r"], cat(das_r), cat(das_i),
                                                    cat(dbbs_r), cat(dbbs_i))
            g["s5_lam_re"][i], g["s5_lam_im"][i] = dlr.reshape(2, SG, SP), dli.reshape(2, SG, SP)
            g["s5_log_dt"][i] = dldt.reshape(2, SG)
            g["s5_b_re"][i] = jnp.transpose(dbr.reshape(2, SG, SH, SP), (0, 1, 3, 2))
            g["s5_b_im"][i] = jnp.transpose(dbi.reshape(2, SG, SH, SP), (0, 1, 3, 2))
            g["s5_c_re"][i], g["s5_c_im"][i] = jnp.stack(dcs_r), jnp.stack(dcs_i)
            dxs = s5p_dx_sum(dy, s5_d[i].reshape(1, AW), dxd[0], dxd[1], f"s5_dx_sum{l}")
            dp = jnp.concatenate([dpa, dxs, dpgb], axis=1)
            w_in = wein[i]
        else:
            dq, dgate, dk, dv = attn_bwd(sv["qkv"], sv["p"], dmix, sv["o_att"], sv["lse"], f"attn_bwd{l}")
            dpqkv, dqg, dkg = attn_prep_bwd(sv["p"], dq, dk, dv, q_norm_g[i].reshape(1, HD),
                                            k_norm_g[i].reshape(1, HD), cos, sins, f"attn_prep_bwd{l}")
            g["q_norm_g"][i], g["k_norm_g"][i] = dqg.reshape(HD), dkg.reshape(HD)
            dp = jnp.concatenate([dpqkv, dgate], axis=1)
            w_in = woin[i]
        dh = mm_nt(dp, w_in, f"in_dgrad{l}")
        gbuf[in_name] = mm_tn(sv["h"], dp, 4, f"in_wgrad{l}", slot=i, into=gbuf.get(in_name))
        dx, dmod2, dng = pro_bwd(sv["x"], dh, dx, norm_g[l].reshape(1, D), mods[l], f"pro_bwd{l}")
        d_norm_g[l] = dng.reshape(D)
        d_mod_ctx[l] = jnp.concatenate([dmod2[0, 0], dmod2[0, 1], dgt[0]])
        d_mod_lat[l] = jnp.concatenate([dmod2[1, 0], dmod2[1, 1], dgt[1]])
    g.update(gbuf)
    return loss_part, dx, g, d_norm_g, d_mod_lat, d_mod_ctx, d_final_g
```

```python
import functools
import math

import numpy as np
import jax
import jax.numpy as jnp
from jax import lax
from jax.experimental import pallas as pl
from jax.experimental.pallas import tpu as pltpu

F32 = jnp.float32
BF = jnp.bfloat16
MESH = pl.DeviceIdType.MESH

D = 1024
NC = 256
SEQ = 4096
GRID_W = 64
TM = 256
CHUNK = 128
EPS = 1e-6
HD = 128
NQ = 8
NKV = 2
ROPE_THETA = 10000.0
SG = 32
SP = 64
SH = 16
SW = SG * SP
GELU_K = math.sqrt(2.0 / math.pi)
GELU_C = 0.044715
VMEM_LIMIT_BYTES = 56 * 1024 * 1024

ADAM_LR = 0.001
ADAM_B1 = 0.9
ADAM_B2 = 0.999
ADAM_EPS = 1e-08
ADAM_WD = 0.01
ADAM_STEP = 10


def _call(body, *, name, out_shape, grid=None, in_specs=None, out_specs=None, scratch=()):
    kw = {}
    if grid is not None:
        kw["grid"] = grid
    if in_specs is not None:
        kw["in_specs"] = in_specs
    if out_specs is not None:
        kw["out_specs"] = out_specs
    return pl.pallas_call(
        body, name=name, out_shape=out_shape, scratch_shapes=list(scratch),
        compiler_params=pltpu.CompilerParams(vmem_limit_bytes=VMEM_LIMIT_BYTES), **kw)


def _dot(a, b, ca=1, cb=0):
    return lax.dot_general(a, b, (((ca,), (cb,)), ((), ())), preferred_element_type=F32)


def _sig(x):
    return 1.0 / (1.0 + jnp.exp(-x))


def _full(shape):
    n = len(shape)
    return pl.BlockSpec(shape, lambda *_: (0,) * n)


def _mm_rows(t):
    for rows in (1088, 1024, 768, 512, 256):
        if t % rows == 0:
            return rows
    raise ValueError(t)


def mm_nn(a, w3, name, out_dtype=F32):
    t, k = a.shape
    j, _, nb = w3.shape
    tr = _mm_rows(t)

    def body(a_ref, w_ref, o_ref):
        o_ref[...] = _dot(a_ref[...], w_ref[0]).astype(o_ref.dtype)

    return _call(body, name=name, grid=(j, t // tr),
                 in_specs=[pl.BlockSpec((tr, k), lambda jj, i: (i, 0)),
                           pl.BlockSpec((1, k, nb), lambda jj, i: (jj, 0, 0))],
                 out_specs=pl.BlockSpec((tr, nb), lambda jj, i: (i, jj)),
                 out_shape=jax.ShapeDtypeStruct((t, j * nb), out_dtype))(a, w3)


def mm_nt(a, w3, name, out_dtype=F32):
    t, _ = a.shape
    j, k, nb = w3.shape
    tr = _mm_rows(t)

    def body(a_ref, w_ref, o_ref):
        acc = _dot(a_ref[:, 0:nb], w_ref[0], 1, 1)
        for jj in range(1, j):
            acc = acc + _dot(a_ref[:, jj * nb:(jj + 1) * nb], w_ref[jj], 1, 1)
        o_ref[...] = acc.astype(o_ref.dtype)

    return _call(body, name=name, grid=(t // tr,),
                 in_specs=[pl.BlockSpec((tr, j * nb), lambda i: (i, 0)), _full((j, k, nb))],
                 out_specs=pl.BlockSpec((tr, k), lambda i: (i, 0)),
                 out_shape=jax.ShapeDtypeStruct((t, k), out_dtype))(a, w3)


def mm_tn(a, b, j, name, slot=0, into=None):
    t, m = a.shape
    nb = b.shape[1] // j
    tr = _mm_rows(t)

    def body(a_ref, b_ref, *rest):
        o_ref = rest[-1]

        @pl.when(pl.program_id(1) == 0)
        def _():
            o_ref[...] = jnp.zeros_like(o_ref)
        o_ref[0, 0] += _dot(a_ref[...], b_ref[...], 0, 0)

    in_specs = [pl.BlockSpec((tr, m), lambda jj, i: (i, 0)), pl.BlockSpec((tr, nb), lambda jj, i: (i, jj))]
    args = [a, b]
    alias = {}
    if into is not None:
        in_specs.append(pl.BlockSpec(memory_space=pl.ANY))
        args.append(into)
        alias = {2: 0}
    return pl.pallas_call(
        body, name=name, grid=(j, t // tr), in_specs=in_specs,
        out_specs=pl.BlockSpec((1, 1, m, nb), lambda jj, i: (slot, jj, 0, 0)),
        out_shape=jax.ShapeDtypeStruct((2, j, m, nb), F32), input_output_aliases=alias,
        compiler_params=pltpu.CompilerParams(vmem_limit_bytes=VMEM_LIMIT_BYTES))(*args)


def _mod_rows(mod_ref, i):
    ctx = i == 0
    sh = jnp.where(ctx, mod_ref[0, 0:1, :], mod_ref[1, 0:1, :])
    sc = jnp.where(ctx, mod_ref[0, 1:2, :], mod_ref[1, 1:2, :])
    gt = jnp.where(ctx, mod_ref[0, 2:3, :], mod_ref[1, 2:3, :])
    return sh, sc, gt


def pro_fwd(x, g, mod, name):
    t = x.shape[0]

    def body(x_ref, g_ref, mod_ref, h_ref):
        sh, sc, _ = _mod_rows(mod_ref, pl.program_id(0))
        xv = x_ref[...]
        r = lax.rsqrt(jnp.mean(xv * xv, axis=-1, keepdims=True) + EPS)
        h_ref[...] = ((xv * r) * g_ref[...] * (1.0 + sc) + sh).astype(BF)

    return _call(body, name=name, grid=(t // TM,),
                 in_specs=[pl.BlockSpec((TM, D), lambda i: (i, 0)), _full((1, D)), _full((2, 3, D))],
                 out_specs=pl.BlockSpec((TM, D), lambda i: (i, 0)),
                 out_shape=jax.ShapeDtypeStruct((t, D), BF))(x, g, mod)


def pro_bwd(x, dh, dxn, g, mod, name):
    t = x.shape[0]

    def body(x_ref, dh_ref, dxn_ref, g_ref, mod_ref, dx_ref, dmod_ref, dg_ref):
        i = pl.program_id(0)

        @pl.when(i == 0)
        def _():
            dmod_ref[...] = jnp.zeros_like(dmod_ref)
            dg_ref[...] = jnp.zeros_like(dg_ref)

        _, sc, _ = _mod_rows(mod_ref, i)
        xv = x_ref[...]
        gv = g_ref[...]
        r = lax.rsqrt(jnp.mean(xv * xv, axis=-1, keepdims=True) + EPS)
        xn = xv * r
        dh_v = dh_ref[...]
        e = dh_v * (1.0 + sc)
        dsh = jnp.sum(dh_v, axis=0, keepdims=True)
        dsc = jnp.sum(dh_v * xn * gv, axis=0, keepdims=True)
        dg_ref[...] += jnp.sum(e * xn, axis=0, keepdims=True)
        dxh = e * gv
        dx_ref[...] = dxn_ref[...] + r * (dxh - xn * jnp.mean(dxh * xn, axis=-1, keepdims=True))

        @pl.when(i == 0)
        def _():
            dmod_ref[0, 0:1, :] += dsh
            dmod_ref[0, 1:2, :] += dsc

        @pl.when(i > 0)
        def _():
            dmod_ref[1, 0:1, :] += dsh
            dmod_ref[1, 1:2, :] += dsc

    tile = pl.BlockSpec((TM, D), lambda i: (i, 0))
    return _call(body, name=name, grid=(t // TM,),
                 in_specs=[tile, tile, tile, _full((1, D)), _full((2, 3, D))],
                 out_specs=[tile, _full((2, 2, D)), _full((1, D))],
                 out_shape=[jax.ShapeDtypeStruct((t, D), F32), jax.ShapeDtypeStruct((2, 2, D), F32),
                            jax.ShapeDtypeStruct((1, D), F32)])(x, dh, dxn, g, mod)


def res_fwd(x, o, mod, update_ctx, name):
    t = x.shape[0]

    def body(x_ref, o_ref, mod_ref, y_ref):
        i = pl.program_id(0)
        _, _, gt = _mod_rows(mod_ref, i)
        upd = x_ref[...] + gt * o_ref[...]
        if update_ctx:
            y_ref[...] = upd
        else:
            y_ref[...] = jnp.where(i == 0, x_ref[...], upd)

    tile = pl.BlockSpec((TM, D), lambda i: (i, 0))
    return _call(body, name=name, grid=(t // TM,), in_specs=[tile, tile, _full((2, 3, D))],
                 out_specs=tile, out_shape=jax.ShapeDtypeStruct((t, D), F32))(x, o, mod)


def res_bwd(dxn, o, mod, update_ctx, name):
    t = dxn.shape[0]

    def body(dxn_ref, o_ref, mod_ref, do_ref, dgt_ref):
        i = pl.program_id(0)

        @pl.when(i == 0)
        def _():
            dgt_ref[...] = jnp.zeros_like(dgt_ref)

        _, _, gt = _mod_rows(mod_ref, i)
        dv = dxn_ref[...]
        do = gt * dv
        dgt = jnp.sum(dv * o_ref[...], axis=0, keepdims=True)
        if update_ctx:
            do_ref[...] = do.astype(BF)
        else:
            do_ref[...] = jnp.where(i == 0, jnp.zeros_like(do), do).astype(BF)

        if update_ctx:
            @pl.when(i == 0)
            def _():
                dgt_ref[0:1, :] += dgt

        @pl.when(i > 0)
        def _():
            dgt_ref[1:2, :] += dgt

    tile = pl.BlockSpec((TM, D), lambda i: (i, 0))
    return _call(body, name=name, grid=(t // TM,), in_specs=[tile, tile, _full((2, 3, D))],
                 out_specs=[tile, _full((2, D))],
                 out_shape=[jax.ShapeDtypeStruct((t, D), BF), jax.ShapeDtypeStruct((2, D), F32)])(dxn, o, mod)


def res_pro_fwd(x, o, mod, update_ctx, g_next, mod_next, name):
    t = x.shape[0]

    def body(x_ref, o_ref, mod_ref, g_ref, modn_ref, y_ref, h_ref):
        i = pl.program_id(0)
        _, _, gt = _mod_rows(mod_ref, i)
        xn = x_ref[...] + gt * o_ref[...]
        if not update_ctx:
            xn = jnp.where(i == 0, x_ref[...], xn)
        y_ref[...] = xn
        sh, sc, _ = _mod_rows(modn_ref, i)
        r = lax.rsqrt(jnp.mean(xn * xn, axis=-1, keepdims=True) + EPS)
        h_ref[...] = ((xn * r) * g_ref[...] * (1.0 + sc) + sh).astype(BF)

    tile = pl.BlockSpec((TM, D), lambda i: (i, 0))
    return _call(body, name=name, grid=(t // TM,),
                 in_specs=[tile, tile, _full((2, 3, D)), _full((1, D)), _full((2, 3, D))], out_specs=[tile, tile],
                 out_shape=[jax.ShapeDtypeStruct((t, D), F32), jax.ShapeDtypeStruct((t, D), BF)])(x, o, mod, g_next, mod_next)


def _res_bwd_part(dx, o_ref, modp_ref, do_ref, dgt_ref, i, update_ctx):
    _, _, gtp = _mod_rows(modp_ref, i)
    do = gtp * dx
    dgt = jnp.sum(dx * o_ref[...], axis=0, keepdims=True)
    if update_ctx:
        do_ref[...] = do.astype(BF)

        @pl.when(i == 0)
        def _():
            dgt_ref[0:1, :] += dgt
    else:
        do_ref[...] = jnp.where(i == 0, jnp.zeros_like(do), do).astype(BF)

    @pl.when(i > 0)
    def _():
        dgt_ref[1:2, :] += dgt


def pro_res_bwd(x, dh, dxn, g, mod, o_prev, mod_prev, name):
    t = x.shape[0]

    def body(x_ref, dh_ref, dxn_ref, g_ref, mod_ref, o_ref, modp_ref, dx_ref, dmod_ref, dg_ref, do_ref, dgt_ref):
        i = pl.program_id(0)

        @pl.when(i == 0)
        def _():
            dmod_ref[...] = jnp.zeros_like(dmod_ref)
            dg_ref[...] = jnp.zeros_like(dg_ref)
            dgt_ref[...] = jnp.zeros_like(dgt_ref)

        _, sc, _ = _mod_rows(mod_ref, i)
        xv = x_ref[...]
        gv = g_ref[...]
        r = lax.rsqrt(jnp.mean(xv * xv, axis=-1, keepdims=True) + EPS)
        xn = xv * r
        dh_v = dh_ref[...]
        e = dh_v * (1.0 + sc)
        dsh = jnp.sum(dh_v, axis=0, keepdims=True)
        dsc = jnp.sum(dh_v * xn * gv, axis=0, keepdims=True)
        dg_ref[...] += jnp.sum(e * xn, axis=0, keepdims=True)
        dxh = e * gv
        dx = dxn_ref[...] + r * (dxh - xn * jnp.mean(dxh * xn, axis=-1, keepdims=True))
        dx_ref[...] = dx

        @pl.when(i == 0)
        def _():
            dmod_ref[0, 0:1, :] += dsh
            dmod_ref[0, 1:2, :] += dsc

        @pl.when(i > 0)
        def _():
            dmod_ref[1, 0:1, :] += dsh
            dmod_ref[1, 1:2, :] += dsc

        _res_bwd_part(dx, o_ref, modp_ref, do_ref, dgt_ref, i, True)

    tile = pl.BlockSpec((TM, D), lambda i: (i, 0))
    return _call(body, name=name, grid=(t // TM,),
                 in_specs=[tile, tile, tile, _full((1, D)), _full((2, 3, D)), tile, _full((2, 3, D))],
                 out_specs=[tile, _full((2, 2, D)), _full((1, D)), tile, _full((2, D))],
                 out_shape=[jax.ShapeDtypeStruct((t, D), F32), jax.ShapeDtypeStruct((2, 2, D), F32),
                            jax.ShapeDtypeStruct((1, D), F32), jax.ShapeDtypeStruct((t, D), BF),
                            jax.ShapeDtypeStruct((2, D), F32)])(x, dh, dxn, g, mod, o_prev, mod_prev)


def final_loss(x, target, g, o_last, mod_last):
    t = x.shape[0]

    def body(x_ref, t_ref, g_ref, o_ref, modp_ref, loss_ref, dx_ref, dg_ref, do_ref, dgt_ref):
        i = pl.program_id(0)

        @pl.when(i == 0)
        def _():
            loss_ref[...] = jnp.zeros_like(loss_ref)
            dg_ref[...] = jnp.zeros_like(dg_ref)
            dx_ref[...] = jnp.zeros_like(dx_ref)
            do_ref[...] = jnp.zeros_like(do_ref)
            dgt_ref[...] = jnp.zeros_like(dgt_ref)

        @pl.when(i > 0)
        def _():
            xv = x_ref[...]
            gv = g_ref[...]
            r = lax.rsqrt(jnp.mean(xv * xv, axis=-1, keepdims=True) + EPS)
            xn = xv * r
            err = xn * gv - t_ref[...]
            loss_ref[...] += (0.5 / D) * jnp.sum(jnp.sum(err * err, axis=1, keepdims=True), axis=0, keepdims=True)
            dy = err * (1.0 / D)
            dg_ref[...] += jnp.sum(dy * xn, axis=0, keepdims=True)
            dxh = dy * gv
            dx = r * (dxh - xn * jnp.mean(dxh * xn, axis=-1, keepdims=True))
            dx_ref[...] = dx
            do_ref[...] = (modp_ref[1, 2:3, :] * dx).astype(BF)
            dgt_ref[1:2, :] += jnp.sum(dx * o_ref[...], axis=0, keepdims=True)

    tile = pl.BlockSpec((TM, D), lambda i: (i, 0))
    return _call(body, name="final_loss", grid=(t // TM,),
                 in_specs=[tile, pl.BlockSpec((TM, D), lambda i: (jnp.maximum(i - 1, 0), 0)), _full((1, D)), tile,
                           _full((2, 3, D))],
                 out_specs=[_full((1, 1)), tile, _full((1, D)), tile, _full((2, D))],
                 out_shape=[jax.ShapeDtypeStruct((1, 1), F32), jax.ShapeDtypeStruct((t, D), F32),
                            jax.ShapeDtypeStruct((1, D), F32), jax.ShapeDtypeStruct((t, D), BF),
                            jax.ShapeDtypeStruct((2, D), F32)])(x, target, g, o_last, mod_last)


def ada_fwd(cond, w_mod, b_mod):
    nl, _, nw = w_mod.shape

    def body(c_ref, w_ref, b_ref, o_ref):
        cv = c_ref[...]
        s = (cv * _sig(cv)).astype(BF)
        o_ref[0] = _dot(s, w_ref[0].astype(BF)) + b_ref[0]

    return _call(body, name="ada_fwd", grid=(nl,),
                 in_specs=[_full((8, D)), pl.BlockSpec((1, D, nw), lambda l: (l, 0, 0)),
                           pl.BlockSpec((1, 1, nw), lambda l: (l, 0, 0))],
                 out_specs=pl.BlockSpec((1, 8, nw), lambda l: (l, 0, 0)),
                 out_shape=jax.ShapeDtypeStruct((nl, 8, nw), F32))(cond, w_mod, b_mod)


def ada_bwd(cond, dm, w_mod):
    nl, _, nw = w_mod.shape

    def body(c_ref, dm_ref, w_ref, gw_ref, dcc_ref, dc_ref):
        l = pl.program_id(0)

        @pl.when(l == 0)
        def _():
            dc_ref[...] = jnp.zeros_like(dc_ref)

        cv = c_ref[...]
        sg = _sig(cv)
        s = (cv * sg).astype(BF)
        dmv = dm_ref[0].astype(BF)
        gw_ref[0] = _dot(s, dmv, 0, 0)
        dc_ref[...] += _dot(dmv, w_ref[0].astype(BF), 1, 1)

        @pl.when(l == nl - 1)
        def _():
            dcond = dc_ref[...] * (sg * (1.0 + cv * (1.0 - sg)))
            dcc_ref[...] = jnp.sum(dcond[4:8], axis=0, keepdims=True)

    return _call(body, name="ada_bwd", grid=(nl,),
                 in_specs=[_full((8, D)), pl.BlockSpec((1, 8, nw), lambda l: (l, 0, 0)),
                           pl.BlockSpec((1, D, nw), lambda l: (l, 0, 0))],
                 out_specs=[pl.BlockSpec((1, D, nw), lambda l: (l, 0, 0)), _full((1, D))],
                 out_shape=[jax.ShapeDtypeStruct((nl, D, nw), F32), jax.ShapeDtypeStruct((1, D), F32)],
                 scratch=[pltpu.VMEM((8, D), F32)])(cond, dm, w_mod)


def add2(a, b, name):
    def body(a_ref, b_ref, o_ref):
        o_ref[...] = a_ref[...] + b_ref[...]

    return _call(body, name=name, out_shape=jax.ShapeDtypeStruct(a.shape, a.dtype))(a, b)


AW = 512
NGRP = 4


def Y4_SPEC():
    return pl.BlockSpec((AW // 128, TM, 128), lambda i: (0, i, 0))


def _cat_lanes(ref):
    return jnp.concatenate([ref[q] for q in range(ref.shape[0])], axis=1)


def _gelu(y):
    t = jnp.tanh(GELU_K * (y + GELU_C * y * y * y))
    return 0.5 * y * (1.0 + t), t


def _layer_norm_stats(v):
    mu = jnp.mean(v, axis=-1, keepdims=True)
    vc = v - mu
    rstd = lax.rsqrt(jnp.mean(vc * vc, axis=-1, keepdims=True) + EPS)
    return vc * rstd, rstd


def _spatial_mix(vn_ref, ws_ref, bs_ref, mixed_ref):
    for ch in range(TM // CHUNK):
        rows = slice(ch * CHUNK, (ch + 1) * CHUNK)
        for g in range(NGRP):
            cols = slice(g * CHUNK, (g + 1) * CHUNK)
            mixed_ref[rows, cols] = _dot(ws_ref[g], vn_ref[rows, cols]) + bs_ref[g]


def mix_fwd(p, yf, yb, vg, ws, bs, dsk, wglu, bglu, name):
    t = p.shape[0]

    def body(p_ref, yf_ref, yb_ref, vg_ref, ws_ref, bs_ref, d_ref, wg_ref, bg_ref, o_ref, vn_ref, mixed_ref):
        vhat, _ = _layer_norm_stats(p_ref[:, AW:2 * AW])
        vn_ref[...] = (vhat * vg_ref[...]).astype(BF)
        _spatial_mix(vn_ref, ws_ref, bs_ref, mixed_ref)
        ga = p_ref[:, 2 * AW:3 * AW]
        o_ref[:, 0:AW] = (p_ref[:, 0:AW] * mixed_ref[...] * (ga * _sig(ga))).astype(BF)
        y = _cat_lanes(yf_ref) + _cat_lanes(yb_ref) + d_ref[...] * p_ref[:, 3 * AW:4 * AW]
        y2, _ = _gelu(y)
        z = _dot(y2.astype(BF), wg_ref[...]) + bg_ref[...]
        gb = p_ref[:, 4 * AW:5 * AW]
        o_ref[:, AW:2 * AW] = (y2 * _sig(z) * (gb * _sig(gb))).astype(BF)

    tile = lambda w: pl.BlockSpec((TM, w), lambda i: (i, 0))
    return _call(body, name=name, grid=(t // TM,),
                 in_specs=[tile(5 * AW), Y4_SPEC(), Y4_SPEC(), _full((1, AW)), _full((NGRP, CHUNK, CHUNK)),
                           _full((NGRP, CHUNK, 1)), _full((1, AW)), _full((AW, AW)), _full((1, AW))],
                 out_specs=tile(2 * AW), out_shape=jax.ShapeDtypeStruct((t, 2 * AW), BF),
                 scratch=[pltpu.VMEM((TM, AW), BF), pltpu.VMEM((TM, AW), F32)])(p, yf, yb, vg, ws, bs, dsk, wglu, bglu)


def mix_bwd(p, yf, yb, dmix, vg, ws, bs, dsk, wglu, bglu, name):
    t = p.shape[0]

    def body(p_ref, yf_ref, yb_ref, dm_ref, vg_ref, ws_ref, bs_ref, d_ref, wg_ref, bg_ref,
             dpa_ref, dy_ref, dws_ref, dbs_ref, dvg_ref, dd_ref, dwg_ref, dbg_ref,
             vn_ref, mixed_ref, dmx_ref, dvn_ref):
        @pl.when(pl.program_id(0) == 0)
        def _():
            for r in (dws_ref, dbs_ref, dvg_ref, dd_ref, dwg_ref, dbg_ref):
                r[...] = jnp.zeros_like(r)

        vhat, rstd = _layer_norm_stats(p_ref[:, AW:2 * AW])
        vgv = vg_ref[...]
        vn_ref[...] = (vhat * vgv).astype(BF)
        _spatial_mix(vn_ref, ws_ref, bs_ref, mixed_ref)
        u = p_ref[:, 0:AW]
        ga = p_ref[:, 2 * AW:3 * AW]
        sga = _sig(ga)
        dya = dm_ref[:, 0:AW]
        mixed = mixed_ref[...]
        dpa_ref[:, 0:AW] = (dya * mixed * (ga * sga)).astype(BF)
        dpa_ref[:, 2 * AW:3 * AW] = (dya * u * mixed * (sga * (1.0 + ga * (1.0 - sga)))).astype(BF)
        dmx_ref[...] = dya * u * (ga * sga)
        for ch in range(TM // CHUNK):
            rows = slice(ch * CHUNK, (ch + 1) * CHUNK)
            for g in range(NGRP):
                cols = slice(g * CHUNK, (g + 1) * CHUNK)
                dmx = dmx_ref[rows, cols]
                dmxb = dmx.astype(BF)
                dws_ref[g] += _dot(dmxb, vn_ref[rows, cols], 1, 1)
                dbs_ref[g] += jnp.sum(dmx, axis=1, keepdims=True)
                dvn_ref[rows, cols] = _dot(ws_ref[g], dmxb, 0, 0)
        dvn = dvn_ref[...]
        dvg_ref[...] += jnp.sum(dvn * vhat, axis=0, keepdims=True)
        dvh = dvn * vgv
        dpa_ref[:, AW:2 * AW] = (rstd * (dvh - jnp.mean(dvh, axis=-1, keepdims=True)
                                         - vhat * jnp.mean(dvh * vhat, axis=-1, keepdims=True))).astype(BF)

        xs = p_ref[:, 3 * AW:4 * AW]
        y = _cat_lanes(yf_ref) + _cat_lanes(yb_ref) + d_ref[...] * xs
        y2, th = _gelu(y)
        y2b = y2.astype(BF)
        z = _dot(y2b, wg_ref[...]) + bg_ref[...]
        sz = _sig(z)
        gb = p_ref[:, 4 * AW:5 * AW]
        sgb = _sig(gb)
        dyb = dm_ref[:, AW:2 * AW]
        dpa_ref[:, 4 * AW:5 * AW] = (dyb * (y2 * sz) * (sgb * (1.0 + gb * (1.0 - sgb)))).astype(BF)
        dy3 = dyb * (gb * sgb)
        dz = dy3 * y2 * sz * (1.0 - sz)
        dzb = dz.astype(BF)
        dwg_ref[...] += _dot(y2b, dzb, 0, 0)
        dbg_ref[...] += jnp.sum(dz, axis=0, keepdims=True)
        dy2 = dy3 * sz + _dot(dzb, wg_ref[...], 1, 1)
        dgelu = 0.5 * (1.0 + th) + 0.5 * y * (1.0 - th * th) * GELU_K * (1.0 + 3.0 * GELU_C * y * y)
        dy = dy2 * dgelu
        dd_ref[...] += jnp.sum(dy * xs, axis=0, keepdims=True)
        dy_ref[...] = dy

    tile = lambda w: pl.BlockSpec((TM, w), lambda i: (i, 0))
    return _call(body, name=name, grid=(t // TM,),
                 in_specs=[tile(5 * AW), Y4_SPEC(), Y4_SPEC(), tile(2 * AW), _full((1, AW)), _full((NGRP, CHUNK, CHUNK)),
                           _full((NGRP, CHUNK, 1)), _full((1, AW)), _full((AW, AW)), _full((1, AW))],
                 out_specs=[tile(5 * AW), tile(AW), _full((NGRP, CHUNK, CHUNK)), _full((NGRP, CHUNK, 1)),
                            _full((1, AW)), _full((1, AW)), _full((AW, AW)), _full((1, AW))],
                 out_shape=[jax.ShapeDtypeStruct((t, 5 * AW), BF),
                            jax.ShapeDtypeStruct((t, AW), F32), jax.ShapeDtypeStruct((NGRP, CHUNK, CHUNK), F32),
                            jax.ShapeDtypeStruct((NGRP, CHUNK, 1), F32), jax.ShapeDtypeStruct((1, AW), F32),
                            jax.ShapeDtypeStruct((1, AW), F32), jax.ShapeDtypeStruct((AW, AW), F32),
                            jax.ShapeDtypeStruct((1, AW), F32)],
                 scratch=[pltpu.VMEM((TM, AW), BF), pltpu.VMEM((TM, AW), F32), pltpu.VMEM((TM, AW), F32),
                          pltpu.VMEM((TM, AW), F32)])(p, yf, yb, dmix, vg, ws, bs, dsk, wglu, bglu)


LN = 512
NBLK = SW // LN
UB = AW // NBLK
PW_ROWS = 32
POW_EXP = [1, 2, 4, 8, 16, 32, 64, 128, 256, 0, 0, 0, 0, 0, 0, 0,
           1, 2, 3, 4, 5, 6, 7, 8, 8, 7, 6, 5, 4, 3, 2, 1]
GROUPS_PER_TILE = TM // 8


def s5_disc(lam_re, lam_im, dt, lam_re_r, lam_im_r, dt_r, b_re, b_im):
    nexp = jnp.asarray(np.array(POW_EXP, np.float32).reshape(PW_ROWS, 1))

    def body(n_ref, lr_ref, li_ref, dt_ref, lrr_ref, lir_ref, dtr_ref, br_ref, bi_ref,
             pr_ref, pi_ref, bbr_ref, bbi_ref):
        for dr in range(2):
            dtl = jnp.exp(dt_ref[dr:dr + 1, :])
            zr = lr_ref[dr:dr + 1, :] * dtl
            zi = li_ref[dr:dr + 1, :] * dtl
            mag = jnp.exp(n_ref[...] * zr)
            ang = n_ref[...] * zi
            pr_ref[dr] = mag * jnp.cos(ang)
            pi_ref[dr] = mag * jnp.sin(ang)
        lr, li, dtv = lrr_ref[...], lir_ref[...], jnp.exp(dtr_ref[...])
        mag = jnp.exp(lr * dtv)
        nr = mag * jnp.cos(li * dtv) - 1.0
        ni = mag * jnp.sin(li * dtv)
        den = lr * lr + li * li
        fr = (nr * lr + ni * li) / den
        fi = (ni * lr - nr * li) / den
        bbr_ref[...] = fr * br_ref[...] - fi * bi_ref[...]
        bbi_ref[...] = fr * bi_ref[...] + fi * br_ref[...]

    rows = lam_re_r.shape[0]
    return _call(body, name="s5_disc",
                 out_shape=[jax.ShapeDtypeStruct((2, PW_ROWS, SW), F32), jax.ShapeDtypeStruct((2, PW_ROWS, SW), F32),
                            jax.ShapeDtypeStruct((rows, SP), F32), jax.ShapeDtypeStruct((rows, SP), F32)])(
        nexp, lam_re, lam_im, dt, lam_re_r, lam_im_r, dt_r, b_re, b_im)


def s5_param_bwd(lam_re_r, lam_im_r, dt_r, b_re, b_im, da_re, da_im, dbb_re, dbb_im):
    rows = lam_re_r.shape[0]
    ng = rows // SH
    seg = jnp.asarray(np.kron(np.eye(ng, dtype=np.float32), np.ones((1, SH), np.float32)))

    def body(seg_ref, lr_ref, li_ref, dt_ref, br_ref, bi_ref, dar_ref, dai_ref, dbbr_ref, dbbi_ref,
             dlr_ref, dli_ref, ddt_ref, dbr_ref, dbi_ref):
        lr, li, dtv = lr_ref[...], li_ref[...], jnp.exp(dt_ref[...])
        mag = jnp.exp(lr * dtv)
        lbr = mag * jnp.cos(li * dtv)
        lbi = mag * jnp.sin(li * dtv)
        den = lr * lr + li * li
        nr, ni = lbr - 1.0, lbi
        fr = (nr * lr + ni * li) / den
        fi = (ni * lr - nr * li) / den
        br, bi = br_ref[...], bi_ref[...]
        gbr, gbi = dbbr_ref[...], dbbi_ref[...]
        dbr_ref[...] = gbr * fr + gbi * fi
        dbi_ref[...] = gbi * fr - gbr * fi
        gfr = gbr * br + gbi * bi
        gfi = gbi * br - gbr * bi
        ilr, ili = lr / den, -li / den
        gnr = gfr * ilr + gfi * ili
        gni = gfi * ilr - gfr * ili
        qr = -(fr * ilr - fi * ili)
        qi = -(fr * ili + fi * ilr)
        glr = gfr * qr + gfi * qi
        gli = gfi * qr - gfr * qi
        first = (lax.broadcasted_iota(jnp.int32, (rows, 1), 0) % SH) == 0
        glbr = gnr + jnp.where(first, dar_ref[...], 0.0)
        glbi = gni + jnp.where(first, dai_ref[...], 0.0)
        gzr = glbr * lbr + glbi * lbi
        gzi = glbi * lbr - glbr * lbi
        glr = glr + gzr * dtv
        gli = gli + gzi * dtv
        gdt = (gzr * lr + gzi * li) * dtv
        hi = lax.Precision.HIGHEST
        sg = seg_ref[...]
        dlr_ref[...] = jnp.dot(sg, glr, precision=hi, preferred_element_type=F32)
        dli_ref[...] = jnp.dot(sg, gli, precision=hi, preferred_element_type=F32)
        ddt_ref[...] = jnp.sum(jnp.dot(sg, gdt, precision=hi, preferred_element_type=F32), axis=1, keepdims=True)

    return _call(body, name="s5_param_bwd",
                 out_shape=[jax.ShapeDtypeStruct((ng, SP), F32), jax.ShapeDtypeStruct((ng, SP), F32),
                            jax.ShapeDtypeStruct((ng, 1), F32), jax.ShapeDtypeStruct((rows, SP), F32),
                            jax.ShapeDtypeStruct((rows, SP), F32)])(
        seg, lam_re_r, lam_im_r, dt_r, b_re, b_im, da_re, da_im, dbb_re, dbb_im)


def _scan_tile(hr_ref, hi_ref, er_ref, ei_ref, st_ref, cr_ref, ci_ref, pr_ref, pi_ref, reverse):
    row8 = lax.broadcasted_iota(jnp.int32, (TM, 1), 0) % 8
    rowe = lax.broadcasted_iota(jnp.int32, (2 * GROUPS_PER_TILE, 1), 0)
    ne = 2 * GROUPS_PER_TILE
    for blk in range(NBLK):
        cols = slice(blk * LN, (blk + 1) * LN)
        hr, hi = hr_ref[:, cols], hi_ref[:, cols]
        for k, s in enumerate((1, 2, 4)):
            ar, ai = pr_ref[k:k + 1, cols], pi_ref[k:k + 1, cols]
            if reverse:
                m = row8 < 8 - s
                sr, si = pltpu.roll(hr, TM - s, 0), pltpu.roll(hi, TM - s, 0)
            else:
                m = row8 >= s
                sr, si = pltpu.roll(hr, s, 0), pltpu.roll(hi, s, 0)
            sr, si = jnp.where(m, sr, 0.0), jnp.where(m, si, 0.0)
            hr, hi = hr + ar * sr - ai * si, hi + ar * si + ai * sr
        hr_ref[:, cols] = hr
        hi_ref[:, cols] = hi
        edge = 0 if reverse else 7
        nq = LN // 128
        for q in range(nq):
            st_ref[q] = hr[:, q * 128:(q + 1) * 128]
            st_ref[nq + q] = hi[:, q * 128:(q + 1) * 128]
        gr = jnp.concatenate([st_ref[q, pl.ds(edge, GROUPS_PER_TILE, stride=8), :] for q in range(nq)], axis=1)
        gi = jnp.concatenate([st_ref[nq + q, pl.ds(edge, GROUPS_PER_TILE, stride=8), :] for q in range(nq)], axis=1)
        zero = jnp.zeros((GROUPS_PER_TILE, LN), F32)
        if reverse:
            er_ref[0:GROUPS_PER_TILE, :] = gr
            ei_ref[0:GROUPS_PER_TILE, :] = gi
            er_ref[GROUPS_PER_TILE:ne, :] = zero
            ei_ref[GROUPS_PER_TILE:ne, :] = zero
            er_ref[GROUPS_PER_TILE:GROUPS_PER_TILE + 1, :] = cr_ref[:, cols]
            ei_ref[GROUPS_PER_TILE:GROUPS_PER_TILE + 1, :] = ci_ref[:, cols]
        else:
            er_ref[0:GROUPS_PER_TILE, :] = zero
            ei_ref[0:GROUPS_PER_TILE, :] = zero
            er_ref[GROUPS_PER_TILE - 1:GROUPS_PER_TILE, :] = cr_ref[:, cols]
            ei_ref[GROUPS_PER_TILE - 1:GROUPS_PER_TILE, :] = ci_ref[:, cols]
            er_ref[GROUPS_PER_TILE:ne, :] = gr
            ei_ref[GROUPS_PER_TILE:ne, :] = gi
        evr, evi = er_ref[...], ei_ref[...]
        s = 1
        k = 3
        while s < ne:
            ar, ai = pr_ref[k:k + 1, cols], pi_ref[k:k + 1, cols]
            if reverse:
                m = rowe < ne - s
                sr, si = pltpu.roll(evr, ne - s, 0), pltpu.roll(evi, ne - s, 0)
            else:
                m = rowe >= s
                sr, si = pltpu.roll(evr, s, 0), pltpu.roll(evi, s, 0)
            sr, si = jnp.where(m, sr, 0.0), jnp.where(m, si, 0.0)
            evr, evi = evr + ar * sr - ai * si, evi + ar * si + ai * sr
            s *= 2
            k += 1
        er_ref[...] = evr
        ei_ref[...] = evi
        if reverse:
            cr_ref[:, cols] = er_ref[0:1, :]
            ci_ref[:, cols] = ei_ref[0:1, :]
            apr, api = pr_ref[24:32, cols], pi_ref[24:32, cols]
        else:
            cr_ref[:, cols] = er_ref[ne - 1:ne, :]
            ci_ref[:, cols] = ei_ref[ne - 1:ne, :]
            apr, api = pr_ref[16:24, cols], pi_ref[16:24, cols]
        for g in range(GROUPS_PER_TILE):
            e_row = g + 1 if reverse else GROUPS_PER_TILE - 1 + g
            kr, ki = er_ref[e_row:e_row + 1, :], ei_ref[e_row:e_row + 1, :]
            rows = slice(8 * g, 8 * g + 8)
            hr_ref[rows, cols] = hr_ref[rows, cols] + apr * kr - api * ki
            hi_ref[rows, cols] = hi_ref[rows, cols] + apr * ki + api * kr


def _tile_order(kind, nt):
    if kind == "fwd":
        return lambda i: i
    if kind == "bwd":
        return lambda i: jnp.where(i == 0, 0, nt - i)
    if kind == "fwd_adj":
        return lambda i: nt - 1 - i
    if kind == "bwd_adj":
        return lambda i: jnp.where(i == nt - 1, 0, i + 1)
    raise ValueError(kind)


def s5_fwd(p, bb_re, bb_im, ct_re, ct_im, pw_re, pw_im, reverse, name):
    t = p.shape[0]
    nt = t // TM
    order = _tile_order("bwd" if reverse else "fwd", nt)

    def body(x_ref, bbr_ref, bbi_ref, ctr_ref, cti_ref, pr_ref, pi_ref, y_ref, hpr_ref, hpi_ref,
             hr_ref, hi_ref, er_ref, ei_ref, st_ref, cr_ref, ci_ref, c0r_ref, c0i_ref):
        @pl.when(pl.program_id(0) == 0)
        def _():
            cr_ref[...] = jnp.zeros_like(cr_ref)
            ci_ref[...] = jnp.zeros_like(ci_ref)

        c0r_ref[...] = cr_ref[...]
        c0i_ref[...] = ci_ref[...]
        xb = x_ref[...].astype(BF)
        for j in range(NBLK):
            cols = slice(j * LN, (j + 1) * LN)
            hr_ref[:, cols] = _dot(xb[:, j * UB:(j + 1) * UB], bbr_ref[j])
            hi_ref[:, cols] = _dot(xb[:, j * UB:(j + 1) * UB], bbi_ref[j])
        _scan_tile(hr_ref, hi_ref, er_ref, ei_ref, st_ref, cr_ref, ci_ref, pr_ref, pi_ref, reverse)
        rowi = lax.broadcasted_iota(jnp.int32, (TM, 1), 0)
        for j in range(NBLK):
            cols = slice(j * LN, (j + 1) * LN)
            hr, hi = hr_ref[:, cols], hi_ref[:, cols]
            y_ref[:, j * UB:(j + 1) * UB] = _dot(hr.astype(BF), ctr_ref[j]) - _dot(hi.astype(BF), cti_ref[j])
            if reverse:
                first = rowi == TM - 1
                sr, si = pltpu.roll(hr, TM - 1, 0), pltpu.roll(hi, TM - 1, 0)
            else:
                first = rowi == 0
                sr, si = pltpu.roll(hr, 1, 0), pltpu.roll(hi, 1, 0)
            hpr_ref[:, cols] = jnp.where(first, c0r_ref[:, cols], sr)
            hpi_ref[:, cols] = jnp.where(first, c0i_ref[:, cols], si)

    state = lambda: pl.BlockSpec((TM, SW), lambda i: (order(i), 0))
    return _call(body, name=name, grid=(nt,),
                 in_specs=[pl.BlockSpec((TM, AW), lambda i: (order(i), 3)),
                           _full((NBLK, UB, LN)), _full((NBLK, UB, LN)), _full((NBLK, LN, UB)), _full((NBLK, LN, UB)),
                           _full((PW_ROWS, SW)), _full((PW_ROWS, SW))],
                 out_specs=[pl.BlockSpec((TM, AW), lambda i: (order(i), 0)), state(), state()],
                 out_shape=[jax.ShapeDtypeStruct((t, AW), F32), jax.ShapeDtypeStruct((t, SW), F32),
                            jax.ShapeDtypeStruct((t, SW), F32)],
                 scratch=[pltpu.VMEM((TM, SW), F32), pltpu.VMEM((TM, SW), F32),
                          pltpu.VMEM((2 * GROUPS_PER_TILE, LN), F32), pltpu.VMEM((2 * GROUPS_PER_TILE, LN), F32),
                          pltpu.VMEM((2 * LN // 128, TM, 128), F32),
                          pltpu.VMEM((1, SW), F32), pltpu.VMEM((1, SW), F32),
                          pltpu.VMEM((1, SW), F32), pltpu.VMEM((1, SW), F32)])(
        p, bb_re, bb_im, ct_re, ct_im, pw_re, pw_im)


def s5_bwd(p, hp_re, hp_im, dy, bb_re, bb_im, ct_re, ct_im, pw_re, pw_im_conj, a_re, a_im, reverse, name):
    t = p.shape[0]
    nt = t // TM
    order = _tile_order("bwd_adj" if reverse else "fwd_adj", nt)

    def body(x_ref, hpr_ref, hpi_ref, dy_ref, bbr_ref, bbi_ref, ctr_ref, cti_ref, pr_ref, pi_ref, ar_ref, ai_ref,
             dx_ref, dar_ref, dai_ref, dbbr_ref, dbbi_ref, dcr_ref, dci_ref,
             gr_ref, gi_ref, er_ref, ei_ref, st_ref, cr_ref, ci_ref):
        @pl.when(pl.program_id(0) == 0)
        def _():
            for r in (cr_ref, ci_ref, dar_ref, dai_ref, dbbr_ref, dbbi_ref, dcr_ref, dci_ref):
                r[...] = jnp.zeros_like(r)

        xb = x_ref[...].astype(BF)
        dyb = dy_ref[...].astype(BF)
        for j in range(NBLK):
            cols = slice(j * LN, (j + 1) * LN)
            gr_ref[:, cols] = _dot(dyb[:, j * UB:(j + 1) * UB], ctr_ref[j], 1, 1)
            gi_ref[:, cols] = -_dot(dyb[:, j * UB:(j + 1) * UB], cti_ref[j], 1, 1)
        _scan_tile(gr_ref, gi_ref, er_ref, ei_ref, st_ref, cr_ref, ci_ref, pr_ref, pi_ref, not reverse)
        for j in range(NBLK):
            cols = slice(j * LN, (j + 1) * LN)
            xj = xb[:, j * UB:(j + 1) * UB]
            dyj = dyb[:, j * UB:(j + 1) * UB]
            hpr, hpi = hpr_ref[:, cols], hpi_ref[:, cols]
            gr, gi = gr_ref[:, cols], gi_ref[:, cols]
            ar, ai = ar_ref[:, cols], ai_ref[:, cols]
            hr = ar * hpr - ai * hpi + _dot(xj, bbr_ref[j])
            hi = ar * hpi + ai * hpr + _dot(xj, bbi_ref[j])
            dar_ref[:, cols] += jnp.sum(gr * hpr + gi * hpi, axis=0, keepdims=True)
            dai_ref[:, cols] += jnp.sum(gi * hpr - gr * hpi, axis=0, keepdims=True)
            grb, gib = gr.astype(BF), gi.astype(BF)
            dcr_ref[j] += _dot(dyj, hr.astype(BF), 0, 0)
            dci_ref[j] += -_dot(dyj, hi.astype(BF), 0, 0)
            dbbr_ref[j] += _dot(xj, grb, 0, 0)
            dbbi_ref[j] += _dot(xj, gib, 0, 0)
            dx_ref[:, j * UB:(j + 1) * UB] = _dot(grb, bbr_ref[j], 1, 1) + _dot(gib, bbi_ref[j], 1, 1)

    state = lambda: pl.BlockSpec((TM, SW), lambda i: (order(i), 0))
    blockd = lambda: _full((NBLK, UB, LN))
    return _call(body, name=name, grid=(nt,),
                 in_specs=[pl.BlockSpec((TM, AW), lambda i: (order(i), 3)), state(), state(),
                           pl.BlockSpec((TM, AW), lambda i: (order(i), 0)),
                           blockd(), blockd(), _full((NBLK, LN, UB)), _full((NBLK, LN, UB)),
                           _full((PW_ROWS, SW)), _full((PW_ROWS, SW)), _full((1, SW)), _full((1, SW))],
                 out_specs=[pl.BlockSpec((TM, AW), lambda i: (order(i), 0)), _full((1, SW)), _full((1, SW)),
                            blockd(), blockd(), blockd(), blockd()],
                 out_shape=[jax.ShapeDtypeStruct((t, AW), F32), jax.ShapeDtypeStruct((1, SW), F32),
                            jax.ShapeDtypeStruct((1, SW), F32)] + [jax.ShapeDtypeStruct((NBLK, UB, LN), F32)] * 4,
                 scratch=[pltpu.VMEM((TM, SW), F32), pltpu.VMEM((TM, SW), F32),
                          pltpu.VMEM((2 * GROUPS_PER_TILE, LN), F32), pltpu.VMEM((2 * GROUPS_PER_TILE, LN), F32),
                          pltpu.VMEM((2 * LN // 128, TM, 128), F32),
                          pltpu.VMEM((1, SW), F32), pltpu.VMEM((1, SW), F32)])(
        p, hp_re, hp_im, dy, bb_re, bb_im, ct_re, ct_im, pw_re, pw_im_conj, a_re, a_im)


def s5_dx_sum(dy, dsk, dxf, dxb, name):
    t = dy.shape[0]

    def body(dy_ref, d_ref, f_ref, b_ref, o_ref):
        o_ref[...] = (dy_ref[...] * d_ref[...] + f_ref[...] + b_ref[...]).astype(BF)

    tile = pl.BlockSpec((TM, AW), lambda i: (i, 0))
    return _call(body, name=name, grid=(t // TM,), in_specs=[tile, _full((1, AW)), tile, tile], out_specs=tile,
                 out_shape=jax.ShapeDtypeStruct((t, AW), BF))(dy, dsk, dxf, dxb)


XS_BLK = 3 * AW // 128


def _load_perm(refs):
    return jnp.concatenate(
        [jnp.concatenate([ref[pl.ds(r, GROUPS_PER_TILE, stride=8), :] for ref in refs], axis=1) for r in range(8)], axis=0)


def _store_perm(out_ref, val):
    for r in range(8):
        for q in range(AW // 128):
            out_ref[q, pl.ds(r, GROUPS_PER_TILE, stride=8), :] = val[r * GROUPS_PER_TILE:(r + 1) * GROUPS_PER_TILE,
                                                                     q * 128:(q + 1) * 128]


def _scan_perm(hr_ref, hi_ref, er_ref, ei_ref, cr_ref, ci_ref, pr_ref, pi_ref, reverse, hpr_ref=None, hpi_ref=None):
    gpt = GROUPS_PER_TILE
    offsets = list(range(8))[::-1] if reverse else list(range(8))
    for blk in range(NBLK):
        cols = slice(blk * LN, (blk + 1) * LN)
        a1r, a1i = pr_ref[0:1, cols], pi_ref[0:1, cols]
        a8r, a8i = pr_ref[3:4, cols], pi_ref[3:4, cols]
        xr = xi = None
        for r in offsets:
            rows = slice(r * gpt, (r + 1) * gpt)
            if xr is None:
                xr, xi = hr_ref[rows, cols], hi_ref[rows, cols]
            else:
                xr, xi = hr_ref[rows, cols] + a1r * xr - a1i * xi, hi_ref[rows, cols] + a1r * xi + a1i * xr
                hr_ref[rows, cols] = xr
                hi_ref[rows, cols] = xi
        kr, ki = cr_ref[:, cols], ci_ref[:, cols]
        for g in (range(gpt - 1, -1, -1) if reverse else range(gpt)):
            er_ref[g:g + 1, :] = kr
            ei_ref[g:g + 1, :] = ki
            kr, ki = xr[g:g + 1, :] + a8r * kr - a8i * ki, xi[g:g + 1, :] + a8r * ki + a8i * kr
        cr_ref[:, cols] = kr
        ci_ref[:, cols] = ki
        cinr, cini = er_ref[...], ei_ref[...]
        for r in range(8):
            rows = slice(r * gpt, (r + 1) * gpt)
            prow = 16 + (7 - r if reverse else r)
            apr, api = pr_ref[prow:prow + 1, cols], pi_ref[prow:prow + 1, cols]
            hr_ref[rows, cols] = hr_ref[rows, cols] + apr * cinr - api * cini
            hi_ref[rows, cols] = hi_ref[rows, cols] + apr * cini + api * cinr
        if hpr_ref is not None:
            for r in range(8):
                rows = slice(r * gpt, (r + 1) * gpt)
                src = r + 1 if reverse else r - 1
                if 0 <= src < 8:
                    hpr_ref[rows, cols] = hr_ref[src * gpt:(src + 1) * gpt, cols]
                    hpi_ref[rows, cols] = hi_ref[src * gpt:(src + 1) * gpt, cols]
                else:
                    hpr_ref[rows, cols] = cinr
                    hpi_ref[rows, cols] = cini


def s5p_fwd(p, bb_re, bb_im, ct_re, ct_im, pw_re, pw_im, reverse, name):
    t = p.shape[0]
    nt = t // TM
    order = _tile_order("bwd" if reverse else "fwd", nt)
    nq = AW // 128

    def body(*refs):
        x_refs = refs[:nq]
        bbr_ref, bbi_ref, ctr_ref, cti_ref, pr_ref, pi_ref, y_ref, hpr_ref, hpi_ref = refs[nq:nq + 9]
        hr_ref, hi_ref, er_ref, ei_ref, cr_ref, ci_ref = refs[nq + 9:]

        @pl.when(pl.program_id(0) == 0)
        def _():
            cr_ref[...] = jnp.zeros_like(cr_ref)
            ci_ref[...] = jnp.zeros_like(ci_ref)

        xb = _load_perm(x_refs).astype(BF)
        for j in range(NBLK):
            cols = slice(j * LN, (j + 1) * LN)
            hr_ref[:, cols] = _dot(xb[:, j * UB:(j + 1) * UB], bbr_ref[j])
            hi_ref[:, cols] = _dot(xb[:, j * UB:(j + 1) * UB], bbi_ref[j])
        _scan_perm(hr_ref, hi_ref, er_ref, ei_ref, cr_ref, ci_ref, pr_ref, pi_ref, reverse, hpr_ref, hpi_ref)
        y = jnp.concatenate(
            [_dot(hr_ref[:, j * LN:(j + 1) * LN].astype(BF), ctr_ref[j])
             - _dot(hi_ref[:, j * LN:(j + 1) * LN].astype(BF), cti_ref[j]) for j in range(NBLK)], axis=1)
        _store_perm(y_ref, y)

    state = lambda: pl.BlockSpec((TM, SW), lambda i: (order(i), 0))
    xspec = lambda q: pl.BlockSpec((TM, 128), lambda i: (order(i), XS_BLK + q))
    return _call(body, name=name, grid=(nt,),
                 in_specs=[xspec(q) for q in range(nq)]
                 + [_full((NBLK, UB, LN)), _full((NBLK, UB, LN)), _full((NBLK, LN, UB)), _full((NBLK, LN, UB)),
                    _full((PW_ROWS, SW)), _full((PW_ROWS, SW))],
                 out_specs=[pl.BlockSpec((nq, TM, 128), lambda i: (0, order(i), 0)), state(), state()],
                 out_shape=[jax.ShapeDtypeStruct((nq, t, 128), F32), jax.ShapeDtypeStruct((t, SW), F32),
                            jax.ShapeDtypeStruct((t, SW), F32)],
                 scratch=[pltpu.VMEM((TM, SW), F32), pltpu.VMEM((TM, SW), F32),
                          pltpu.VMEM((GROUPS_PER_TILE, LN), F32), pltpu.VMEM((GROUPS_PER_TILE, LN), F32),
                          pltpu.VMEM((1, SW), F32), pltpu.VMEM((1, SW), F32)])(
        *([p] * nq), bb_re, bb_im, ct_re, ct_im, pw_re, pw_im)


def s5p_bwd(p, hp_re, hp_im, dy, bb_re, bb_im, ct_re, ct_im, pw_re, pw_im_conj, a_re, a_im, reverse, name):
    t = p.shape[0]
    nt = t // TM
    order = _tile_order("bwd_adj" if reverse else "fwd_adj", nt)
    nq = AW // 128

    def body(*refs):
        x_refs, dy_refs = refs[:nq], refs[nq:2 * nq]
        (hpr_ref, hpi_ref, bbr_ref, bbi_ref, ctr_ref, cti_ref, pr_ref, pi_ref, ar_ref, ai_ref,
         dx_ref, dar_ref, dai_ref, dbbr_ref, dbbi_ref, dcr_ref, dci_ref,
         gr_ref, gi_ref, er_ref, ei_ref, cr_ref, ci_ref) = refs[2 * nq:]

        @pl.when(pl.program_id(0) == 0)
        def _():
            for r in (cr_ref, ci_ref, dar_ref, dai_ref, dbbr_ref, dbbi_ref, dcr_ref, dci_ref):
                r[...] = jnp.zeros_like(r)

        xb = _load_perm(x_refs).astype(BF)
        dyb = _load_perm(dy_refs).astype(BF)
        for j in range(NBLK):
            cols = slice(j * LN, (j + 1) * LN)
            gr_ref[:, cols] = _dot(dyb[:, j * UB:(j + 1) * UB], ctr_ref[j], 1, 1)
            gi_ref[:, cols] = -_dot(dyb[:, j * UB:(j + 1) * UB], cti_ref[j], 1, 1)
        _scan_perm(gr_ref, gi_ref, er_ref, ei_ref, cr_ref, ci_ref, pr_ref, pi_ref, not reverse)
        dxs = []
        for j in range(NBLK):
            cols = slice(j * LN, (j + 1) * LN)
            xj = xb[:, j * UB:(j + 1) * UB]
            dyj = dyb[:, j * UB:(j + 1) * UB]
            hpr, hpi = hpr_ref[:, cols], hpi_ref[:, cols]
            gr, gi = gr_ref[:, cols], gi_ref[:, cols]
            ar, ai = ar_ref[:, cols], ai_ref[:, cols]
            hr = ar * hpr - ai * hpi + _dot(xj, bbr_ref[j])
            hi = ar * hpi + ai * hpr + _dot(xj, bbi_ref[j])
            dar_ref[:, cols] += jnp.sum(gr * hpr + gi * hpi, axis=0, keepdims=True)
            dai_ref[:, cols] += jnp.sum(gi * hpr - gr * hpi, axis=0, keepdims=True)
            grb, gib = gr.astype(BF), gi.astype(BF)
            dcr_ref[j] += _dot(dyj, hr.astype(BF), 0, 0)
            dci_ref[j] += -_dot(dyj, hi.astype(BF), 0, 0)
            dbbr_ref[j] += _dot(xj, grb, 0, 0)
            dbbi_ref[j] += _dot(xj, gib, 0, 0)
            dxs.append(_dot(grb, bbr_ref[j], 1, 1) + _dot(gib, bbi_ref[j], 1, 1))
        _store_perm(dx_ref, jnp.concatenate(dxs, axis=1))

    state = lambda: pl.BlockSpec((TM, SW), lambda i: (order(i), 0))
    blockd = lambda: _full((NBLK, UB, LN))
    xspec = lambda q: pl.BlockSpec((TM, 128), lambda i: (order(i), XS_BLK + q))
    dyspec = lambda q: pl.BlockSpec((TM, 128), lambda i: (order(i), q))
    return _call(body, name=name, grid=(nt,),
                 in_specs=[xspec(q) for q in range(nq)] + [dyspec(q) for q in range(nq)]
                 + [state(), state(), blockd(), blockd(), _full((NBLK, LN, UB)), _full((NBLK, LN, UB)),
                    _full((PW_ROWS, SW)), _full((PW_ROWS, SW)), _full((1, SW)), _full((1, SW))],
                 out_specs=[pl.BlockSpec((nq, TM, 128), lambda i: (0, order(i), 0)), _full((1, SW)), _full((1, SW)),
                            blockd(), blockd(), blockd(), blockd()],
                 out_shape=[jax.ShapeDtypeStruct((nq, t, 128), F32), jax.ShapeDtypeStruct((1, SW), F32),
                            jax.ShapeDtypeStruct((1, SW), F32)] + [jax.ShapeDtypeStruct((NBLK, UB, LN), F32)] * 4,
                 scratch=[pltpu.VMEM((TM, SW), F32), pltpu.VMEM((TM, SW), F32),
                          pltpu.VMEM((GROUPS_PER_TILE, LN), F32), pltpu.VMEM((GROUPS_PER_TILE, LN), F32),
                          pltpu.VMEM((1, SW), F32), pltpu.VMEM((1, SW), F32)])(
        *([p] * nq), *([dy] * nq), hp_re, hp_im, bb_re, bb_im, ct_re, ct_im, pw_re, pw_im_conj, a_re, a_im)


def s5p_dx_sum(dy, dsk, dxf, dxb, dp, name):
    t = dy.shape[0]
    nq = AW // 128

    def body(dy_ref, d_ref, f_ref, b_ref, dp_ref, o_ref):
        o_ref[...] = (dy_ref[...] * d_ref[...] + _cat_lanes(f_ref) + _cat_lanes(b_ref)).astype(BF)

    tile = pl.BlockSpec((TM, AW), lambda i: (i, 0))
    blk4 = pl.BlockSpec((nq, TM, 128), lambda i: (0, i, 0))
    return pl.pallas_call(
        body, name=name, grid=(t // TM,),
        in_specs=[tile, _full((1, AW)), blk4, blk4, pl.BlockSpec(memory_space=pl.ANY)],
        out_specs=pl.BlockSpec((TM, AW), lambda i: (i, 3)), out_shape=jax.ShapeDtypeStruct(dp.shape, dp.dtype),
        input_output_aliases={4: 0},
        compiler_params=pltpu.CompilerParams(vmem_limit_bytes=VMEM_LIMIT_BYTES))(dy, dsk, dxf, dxb, dp)


SCALE = HD ** -0.5
NHEAD_NORM = NQ + NKV


def _partner(x):
    half0 = (lax.broadcasted_iota(jnp.int32, (1, HD), 1) % 64) < 32
    return jnp.where(half0, pltpu.roll(x, HD - 32, 1), pltpu.roll(x, 32, 1))


def attn_prep(p, qg, kg, cos, sins, name):
    t = p.shape[0]

    def body(p_ref, qg_ref, kg_ref, cos_ref, sin_ref, o_ref):
        cv, sv = cos_ref[...], sin_ref[...]
        for h in range(NHEAD_NORM):
            cols = slice(h * HD, (h + 1) * HD)
            blk = p_ref[:, cols]
            r = lax.rsqrt(jnp.mean(blk * blk, axis=-1, keepdims=True) + EPS)
            xn = blk * r * (qg_ref[...] if h < NQ else kg_ref[...])
            rot = xn * cv + _partner(xn) * sv
            o_ref[:, cols] = ((rot * SCALE) if h < NQ else rot).astype(BF)
        vcols = slice(NHEAD_NORM * HD, (NHEAD_NORM + NKV) * HD)
        o_ref[:, vcols] = p_ref[:, vcols].astype(BF)

    w = (NHEAD_NORM + NKV) * HD
    tile = lambda ww: pl.BlockSpec((TM, ww), lambda i: (i, 0))
    return _call(body, name=name, grid=(t // TM,),
                 in_specs=[tile(w), _full((1, HD)), _full((1, HD)), tile(HD), tile(HD)],
                 out_specs=tile(w), out_shape=jax.ShapeDtypeStruct((t, w), BF))(p, qg, kg, cos, sins)


def attn_prep_bwd(p, dq, dk, dv, qg, kg, cos, sins, dp, name):
    t = p.shape[0]

    def body(p_ref, dq_ref, dk_ref, dv_ref, qg_ref, kg_ref, cos_ref, sin_ref, dp_ref, o_ref, dqg_ref, dkg_ref):
        @pl.when(pl.program_id(0) == 0)
        def _():
            dqg_ref[...] = jnp.zeros_like(dqg_ref)
            dkg_ref[...] = jnp.zeros_like(dkg_ref)

        cv, sv = cos_ref[...], sin_ref[...]
        for h in range(NHEAD_NORM):
            cols = slice(h * HD, (h + 1) * HD)
            blk = p_ref[:, cols]
            r = lax.rsqrt(jnp.mean(blk * blk, axis=-1, keepdims=True) + EPS)
            xh = blk * r
            if h < NQ:
                drot = dq_ref[:, cols] * SCALE
                gv, dg_ref = qg_ref[...], dqg_ref
            else:
                drot = dk_ref[:, (h - NQ) * HD:(h - NQ + 1) * HD]
                gv, dg_ref = kg_ref[...], dkg_ref
            dxn = drot * cv + _partner(drot * sv)
            dg_ref[...] += jnp.sum(dxn * xh, axis=0, keepdims=True)
            dxh = dxn * gv
            o_ref[:, cols] = (r * (dxh - xh * jnp.mean(dxh * xh, axis=-1, keepdims=True))).astype(BF)
        o_ref[:, NHEAD_NORM * HD:(NHEAD_NORM + NKV) * HD] = dv_ref[...].astype(BF)

    w = (NHEAD_NORM + NKV) * HD
    tile = lambda ww: pl.BlockSpec((TM, ww), lambda i: (i, 0))
    return pl.pallas_call(
        body, name=name, grid=(t // TM,),
        in_specs=[tile(w), tile(NQ * HD), tile(NKV * HD), tile(NKV * HD), _full((1, HD)), _full((1, HD)),
                  tile(HD), tile(HD), pl.BlockSpec(memory_space=pl.ANY)],
        out_specs=[tile(w), _full((1, HD)), _full((1, HD))],
        out_shape=[jax.ShapeDtypeStruct(dp.shape, dp.dtype), jax.ShapeDtypeStruct((1, HD), F32),
                   jax.ShapeDtypeStruct((1, HD), F32)],
        input_output_aliases={8: 0},
        compiler_params=pltpu.CompilerParams(vmem_limit_bytes=VMEM_LIMIT_BYTES))(p, dq, dk, dv, qg, kg, cos, sins, dp)


KCOL = NQ
VCOL = NQ + NKV
GCOL = (NQ + 2 * NKV)
QPK = NQ // NKV
ATT_KCHUNK = 512


def attn_fwd(qkv, p, name):
    t = qkv.shape[0]

    def body(q_ref, k_ref, v_ref, g_ref, o_ref, mix_ref, lse_ref):
        def attend(nk):
            q = q_ref[...]
            chunks = [(k0, min(k0 + 2 * ATT_KCHUNK, nk)) for k0 in range(0, nk, 2 * ATT_KCHUNK)]
            s_next = _dot(q, k_ref[chunks[0][0]:chunks[0][1], :], 1, 1)
            m = l = acc = None
            for ci, (k0, k1) in enumerate(chunks):
                s = s_next
                if ci + 1 < len(chunks):
                    s_next = _dot(q, k_ref[chunks[ci + 1][0]:chunks[ci + 1][1], :], 1, 1)
                mc = jnp.max(s, axis=-1, keepdims=True)
                m_new = mc if m is None else jnp.maximum(m, mc)
                pe = jnp.exp(s - m_new)
                lc = jnp.sum(pe, axis=-1, keepdims=True)
                pv = _dot(pe.astype(BF), v_ref[k0:k1, :])
                if m is None:
                    l, acc = lc, pv
                else:
                    alpha = jnp.exp(m - m_new)
                    l, acc = alpha * l + lc, alpha * acc + pv
                m = m_new
            o = acc / l
            gt = g_ref[...]
            o_ref[...] = o
            mix_ref[...] = (o * (gt * _sig(gt))).astype(BF)
            lse_ref[...] = jnp.broadcast_to(m + jnp.log(l), (TM, HD))

        pl.when(pl.program_id(1) == 0)(lambda: attend(NC))
        pl.when(pl.program_id(1) > 0)(lambda: attend(t))

    blk = pl.BlockSpec((TM, HD), lambda h, i: (i, h))
    return _call(body, name=name, grid=(NQ, t // TM),
                 in_specs=[blk, pl.BlockSpec((t, HD), lambda h, i: (0, KCOL + h // QPK)),
                           pl.BlockSpec((t, HD), lambda h, i: (0, VCOL + h // QPK)),
                           pl.BlockSpec((TM, HD), lambda h, i: (i, GCOL + h))],
                 out_specs=[blk, blk, blk],
                 out_shape=[jax.ShapeDtypeStruct((t, NQ * HD), F32), jax.ShapeDtypeStruct((t, NQ * HD), BF),
                            jax.ShapeDtypeStruct((t, NQ * HD), F32)])(qkv, qkv, qkv, p)


def attn_bwd(qkv, p, dmix, o, lse, name):
    t = qkv.shape[0]

    def body(q_ref, k_ref, v_ref, g_ref, dm_ref, o_ref, lse_ref, dq_ref, dg_ref, dk_ref, dv_ref):
        i = pl.program_id(2)

        @pl.when((pl.program_id(1) == 0) & (i == 0))
        def _():
            dk_ref[...] = jnp.zeros_like(dk_ref)
            dv_ref[...] = jnp.zeros_like(dv_ref)

        gt = g_ref[...]
        sg = _sig(gt)
        ov = o_ref[...]
        dmv = dm_ref[...]
        dg_ref[...] = (dmv * ov * (sg * (1.0 + gt * (1.0 - sg)))).astype(BF)
        do = dmv * (gt * sg)
        dr = jnp.sum(do * ov, axis=-1, keepdims=True)
        dob = do.astype(BF)

        def bwd(nk):
            q = q_ref[...]
            lse = lse_ref[:, 0:1]
            chunks = [slice(k0, min(k0 + ATT_KCHUNK, nk)) for k0 in range(0, nk, ATT_KCHUNK)]
            nxt = (_dot(q, k_ref[chunks[0], :], 1, 1), _dot(dob, v_ref[chunks[0], :], 1, 1))
            dq = None
            for ci, keys in enumerate(chunks):
                s, dp = nxt
                if ci + 1 < len(chunks):
                    nxt = (_dot(q, k_ref[chunks[ci + 1], :], 1, 1), _dot(dob, v_ref[chunks[ci + 1], :], 1, 1))
                pe = jnp.exp(s - lse)
                dsb = (pe * (dp - dr)).astype(BF)
                part = _dot(dsb, k_ref[keys, :])
                dq = part if dq is None else dq + part
                dv_ref[keys, :] += _dot(pe.astype(BF), dob, 0, 0)
                dk_ref[keys, :] += _dot(dsb, q, 0, 0)
            dq_ref[...] = dq

        pl.when(i == 0)(lambda: bwd(NC))
        pl.when(i > 0)(lambda: bwd(t))

    blk = pl.BlockSpec((TM, HD), lambda kv, g, i: (i, kv * QPK + g))
    acc = pl.BlockSpec((t, HD), lambda kv, g, i: (0, kv))
    return _call(body, name=name, grid=(NKV, QPK, t // TM),
                 in_specs=[blk, pl.BlockSpec((t, HD), lambda kv, g, i: (0, KCOL + kv)),
                           pl.BlockSpec((t, HD), lambda kv, g, i: (0, VCOL + kv)),
                           pl.BlockSpec((TM, HD), lambda kv, g, i: (i, GCOL + kv * QPK + g)), blk, blk, blk],
                 out_specs=[blk, pl.BlockSpec((TM, HD), lambda kv, g, i: (i, GCOL + kv * QPK + g)), acc, acc],
                 out_shape=[jax.ShapeDtypeStruct((t, NQ * HD), F32), jax.ShapeDtypeStruct((t, (GCOL + NQ) * HD), BF),
                            jax.ShapeDtypeStruct((t, NKV * HD), F32), jax.ShapeDtypeStruct((t, NKV * HD), F32)])(
        qkv, qkv, qkv, p, dmix, o, lse)


def _row_tile(rows, row_bytes, cap=2 * 1024 * 1024):
    if rows * row_bytes <= cap or rows % 8:
        return rows
    tr = rows
    while tr * row_bytes > cap and tr % 16 == 0:
        tr //= 2
    return tr


def adamw(w, g, m, v, name):
    r, cdim = w.shape
    tr = _row_tile(r, 4 * max(cdim, 128))

    def body(w_ref, g_ref, m_ref, v_ref, d_ref, nm_ref, nv_ref):
        gv = g_ref[...]
        m2 = ADAM_B1 * m_ref[...] + (1.0 - ADAM_B1) * gv
        v2 = ADAM_B2 * v_ref[...] + (1.0 - ADAM_B2) * (gv * gv)
        mh = m2 / (1.0 - ADAM_B1 ** ADAM_STEP)
        vh = v2 / (1.0 - ADAM_B2 ** ADAM_STEP)
        d_ref[...] = -ADAM_LR * (mh / (jnp.sqrt(vh) + ADAM_EPS) + ADAM_WD * w_ref[...])
        nm_ref[...] = m2
        nv_ref[...] = v2

    tile = pl.BlockSpec((tr, cdim), lambda i: (i, 0))
    sh = jax.ShapeDtypeStruct((r, cdim), F32)
    return _call(body, name=name, grid=(r // tr,), in_specs=[tile] * 4, out_specs=[tile] * 3,
                 out_shape=[sh, sh, sh])(w, g, m, v)


def sum_lead(a, name, out_dtype=F32):
    n, r, cdim = a.shape
    tr = _row_tile(r, 4 * n * max(cdim, 128))

    def body(a_ref, o_ref):
        acc = a_ref[0].astype(F32)
        for k in range(1, n):
            acc = acc + a_ref[k].astype(F32)
        o_ref[...] = acc.astype(o_ref.dtype)

    return _call(body, name=name, grid=(r // tr,),
                 in_specs=[pl.BlockSpec((n, tr, cdim), lambda i: (0, i, 0))],
                 out_specs=pl.BlockSpec((tr, cdim), lambda i: (i, 0)),
                 out_shape=jax.ShapeDtypeStruct((r, cdim), out_dtype))(a)


_FLIPS = {"xy": [(1, 0, 0), (0, 1, 0), (1, 1, 0)], "c": [(0, 0, 1)],
          "all": [(0, 0, 1), (0, 1, 0), (0, 1, 1), (1, 0, 0), (1, 0, 1), (1, 1, 0), (1, 1, 1)]}
_GROUP_SIZE = {"xy": 4, "c": 2, "all": 8}


def _group_index(group, x, y, c):
    return {"xy": 2 * x + y, "c": c, "all": 4 * x + 2 * y + c}[group]


def exchange(items, name):
    plan = []
    for arr, group, kind in items:
        chunk = arr.shape if kind == "gather" else arr.shape[1:]
        plan.append((group, kind, chunk))
    ncopy = sum(len(_FLIPS[g]) for g, _, _ in plan)
    nitem = len(plan)

    def body(*refs):
        srcs, dsts = refs[:nitem], refs[nitem:2 * nitem]
        send_sems, recv_sems, local_sems = refs[2 * nitem:]
        x, y, c = lax.axis_index("x"), lax.axis_index("y"), lax.axis_index("c")
        sends, recvs, locals_ = [], [], []
        n = 0
        for k, (group, kind, _) in enumerate(plan):
            me = _group_index(group, x, y, c)
            own = srcs[k] if kind == "gather" else srcs[k].at[me]
            locals_.append(pltpu.make_async_copy(own, dsts[k].at[me], local_sems.at[k]))
            for fx, fy, fc in _FLIPS[group]:
                px, py, pc = (1 - x if fx else x), (1 - y if fy else y), (1 - c if fc else c)
                peer = _group_index(group, px, py, pc)
                src = srcs[k] if kind == "gather" else srcs[k].at[peer]
                sends.append(pltpu.make_async_remote_copy(
                    src_ref=src, dst_ref=dsts[k].at[me], send_sem=send_sems.at[n], recv_sem=recv_sems.at[n],
                    device_id=(px, py, pc), device_id_type=MESH))
                recvs.append(pltpu.make_async_remote_copy(
                    src_ref=src, dst_ref=dsts[k].at[peer], send_sem=send_sems.at[n], recv_sem=recv_sems.at[n],
                    device_id=(px, py, pc), device_id_type=MESH))
                n += 1
        for cp in locals_ + sends:
            cp.start()
        for cp in recvs:
            cp.wait_recv()
        for cp in sends:
            cp.wait_send()
        for cp in locals_:
            cp.wait()

    anyspec = pl.BlockSpec(memory_space=pl.ANY)
    outs = [jax.ShapeDtypeStruct((_GROUP_SIZE[g],) + tuple(chunk), arr.dtype)
            for (arr, _, _), (g, _, chunk) in zip(items, plan)]
    res = pl.pallas_call(
        body, name=name, out_shape=outs, in_specs=[anyspec] * nitem, out_specs=[anyspec] * nitem,
        scratch_shapes=[pltpu.SemaphoreType.DMA((ncopy,)), pltpu.SemaphoreType.DMA((ncopy,)),
                        pltpu.SemaphoreType.DMA((nitem,))],
        compiler_params=pltpu.CompilerParams(has_side_effects=True))(*[a for a, _, _ in items])
    return list(res)


D2D_PIECES = 4


def d2d(items, name):
    n = len(items)
    swaps = [k for k, (_, kind) in enumerate(items) if kind == "swap"]

    def pieces_of(rows):
        npc = D2D_PIECES if rows % (8 * D2D_PIECES) == 0 else 1
        return npc, rows // npc

    ncopy = sum(pieces_of(a.shape[0] if kind == "gather" else a.shape[1])[0] for a, kind in items)

    def body(*refs):
        srcs, outs = refs[:n], refs[n:2 * n]
        stages = dict(zip(swaps, refs[2 * n:2 * n + len(swaps)]))
        send_sems, recv_sems, local_sems = refs[2 * n + len(swaps):]
        x, y, c = lax.axis_index("x"), lax.axis_index("y"), lax.axis_index("c")
        sib = (x, y, 1 - c)

        def remote(src, dst, q):
            return pltpu.make_async_remote_copy(src_ref=src, dst_ref=dst, send_sem=send_sems.at[q],
                                                recv_sem=recv_sems.at[q], device_id=sib, device_id_type=MESH)

        copies = []
        q = 0
        for k, (arr, kind) in enumerate(items):
            npc, pr = pieces_of(arr.shape[0] if kind == "gather" else arr.shape[1])
            for pc in range(npc):
                rs = pl.ds(pc * pr, pr)
                if kind == "gather":
                    mine, theirs = outs[k].at[c, rs], outs[k].at[1 - c, rs]
                    copies.append((pltpu.make_async_copy(srcs[k].at[rs], mine, local_sems.at[q]),
                                   remote(mine, mine, q), remote(theirs, theirs, q)))
                else:
                    stage, land = stages[k].at[rs], outs[k].at[rs]
                    copies.append((pltpu.make_async_copy(srcs[k].at[1 - c, rs], stage, local_sems.at[q]),
                                   remote(stage, land, q), remote(stage, land, q)))
                q += 1
        for loc, _, _ in copies:
            loc.start()
        for loc, send, _ in copies:
            loc.wait()
            send.start()
        for _, _, recv in copies:
            recv.wait_recv()
        for _, send, _ in copies:
            send.wait_send()

    outs = [jax.ShapeDtypeStruct((2,) + a.shape if kind == "gather" else a.shape[1:], a.dtype) for a, kind in items]
    res = pl.pallas_call(
        body, name=name, out_shape=outs, in_specs=[pl.BlockSpec(memory_space=pl.ANY)] * n,
        out_specs=[pl.BlockSpec(memory_space=pltpu.VMEM)] * n,
        scratch_shapes=[pltpu.VMEM(items[k][0].shape[1:], items[k][0].dtype) for k in swaps]
        + [pltpu.SemaphoreType.DMA((ncopy,)), pltpu.SemaphoreType.DMA((ncopy,)), pltpu.SemaphoreType.DMA((ncopy,))],
        compiler_params=pltpu.CompilerParams(has_side_effects=True, vmem_limit_bytes=VMEM_LIMIT_BYTES))(
        *[a for a, _ in items])
    return list(res)


def sum_own(pair, got, name, out_dtype=F32):
    _, r, cdim = pair.shape
    tr = _row_tile(r, 4 * 2 * max(cdim, 128))

    def body(c_ref, p_ref, g_ref, o_ref):
        o_ref[...] = (p_ref[0] + g_ref[...]).astype(o_ref.dtype)

    me = lax.axis_index("c").astype(jnp.int32).reshape(1)
    return pl.pallas_call(
        body, name=name, out_shape=jax.ShapeDtypeStruct((r, cdim), out_dtype),
        grid_spec=pltpu.PrefetchScalarGridSpec(
            num_scalar_prefetch=1, grid=(r // tr,),
            in_specs=[pl.BlockSpec((1, tr, cdim), lambda i, c_ref: (c_ref[0], i, 0)),
                      pl.BlockSpec((tr, cdim), lambda i, c_ref: (i, 0))],
            out_specs=pl.BlockSpec((tr, cdim), lambda i, c_ref: (i, 0))),
        compiler_params=pltpu.CompilerParams(vmem_limit_bytes=VMEM_LIMIT_BYTES))(me, pair, got)


_SMALL = ["c_ctx", "norm_g", "b_mod", "gm_v_g", "gm_w_s", "gm_b_s", "s5_lam_re", "s5_lam_im", "s5_log_dt",
          "s5_b_re", "s5_b_im", "s5_c_re", "s5_c_im", "s5_d", "s5_b_glu", "q_norm_g", "k_norm_g", "final_g"]
_BIG = ["we_in", "we_out", "s5_w_glu", "wo_in", "wo_out"]
_WEIGHTS = ["c_ctx", "norm_g", "w_mod", "b_mod", "we_in", "we_out", "gm_v_g", "gm_w_s", "gm_b_s", "s5_lam_re",
            "s5_lam_im", "s5_log_dt", "s5_b_re", "s5_b_im", "s5_c_re", "s5_c_im", "s5_d", "s5_w_glu", "s5_b_glu",
            "wo_in", "wo_out", "q_norm_g", "k_norm_g", "final_g"]
_SMALL_ALIGN = 8 * 8 * 128


def _rope_tables(n_lat):
    rows = n_lat // GRID_W
    row = jnp.repeat(jnp.arange(rows), GRID_W)
    col = jnp.tile(jnp.arange(GRID_W), rows)
    freqs = ROPE_THETA ** (-jnp.arange(HD // 4, dtype=F32) / (HD // 4))
    ar, ac = row[:, None] * freqs, col[:, None] * freqs
    cos = jnp.concatenate([jnp.cos(ar), jnp.cos(ar), jnp.cos(ac), jnp.cos(ac)], axis=1)
    sins = jnp.concatenate([-jnp.sin(ar), jnp.sin(ar), -jnp.sin(ac), jnp.sin(ac)], axis=1)
    cos = jnp.concatenate([jnp.ones((NC, HD), F32), cos], axis=0)
    sins = jnp.concatenate([jnp.zeros((NC, HD), F32), sins], axis=0)
    return cos, sins


def _block_diag(v, transpose):
    gpb = SG // NBLK
    v = v.reshape(2, NBLK, gpb, SH, SP)
    eye = jnp.eye(gpb, dtype=v.dtype)
    if transpose:
        return jnp.einsum("djahp,ab->djapbh", v, eye).reshape(2, NBLK, LN, UB)
    return jnp.einsum("djahp,ab->djahbp", v, eye).reshape(2, NBLK, UB, LN)


def _diag_blocks(m):
    gpb = SG // NBLK
    return jnp.einsum("jahap->jahp", m.reshape(NBLK, gpb, SH, gpb, SP)).reshape(SG, SH, SP)


def _view2d(a):
    if a.ndim == 1:
        return a.reshape(1, -1)
    if a.shape[-1] < 64 and a.size % 1024 == 0:
        return a.reshape(-1, 1024)
    return a.reshape(-1, a.shape[-1])


def kernel(x, c, ctx, c_ctx, norm_g, w_mod, b_mod, we_in, we_out, gm_v_g, gm_w_s, gm_b_s, s5_lam_re, s5_lam_im, s5_log_dt, s5_b_re, s5_b_im, s5_c_re, s5_c_im, s5_d, s5_w_glu, s5_b_glu, wo_in, wo_out, q_norm_g, k_norm_g, final_g, loss_target, m_c_ctx, m_norm_g, m_w_mod, m_b_mod, m_we_in, m_we_out, m_gm_v_g, m_gm_w_s, m_gm_b_s, m_s5_lam_re, m_s5_lam_im, m_s5_log_dt, m_s5_b_re, m_s5_b_im, m_s5_c_re, m_s5_c_im, m_s5_d, m_s5_w_glu, m_s5_b_glu, m_wo_in, m_wo_out, m_q_norm_g, m_k_norm_g, m_final_g, v_c_ctx, v_norm_g, v_w_mod, v_b_mod, v_we_in, v_we_out, v_gm_v_g, v_gm_w_s, v_gm_b_s, v_s5_lam_re, v_s5_lam_im, v_s5_log_dt, v_s5_b_re, v_s5_b_im, v_s5_c_re, v_s5_c_im, v_s5_d, v_s5_w_glu, v_s5_b_glu, v_wo_in, v_wo_out, v_q_norm_g, v_k_norm_g, v_final_g):
    weights = dict(c_ctx=c_ctx, norm_g=norm_g, w_mod=w_mod, b_mod=b_mod, we_in=we_in, we_out=we_out, gm_v_g=gm_v_g,
                   gm_w_s=gm_w_s, gm_b_s=gm_b_s, s5_lam_re=s5_lam_re, s5_lam_im=s5_lam_im, s5_log_dt=s5_log_dt,
                   s5_b_re=s5_b_re, s5_b_im=s5_b_im, s5_c_re=s5_c_re, s5_c_im=s5_c_im, s5_d=s5_d, s5_w_glu=s5_w_glu,
                   s5_b_glu=s5_b_glu, wo_in=wo_in, wo_out=wo_out, q_norm_g=q_norm_g, k_norm_g=k_norm_g,
                   final_g=final_g)
    mom_m = dict(c_ctx=m_c_ctx, norm_g=m_norm_g, w_mod=m_w_mod, b_mod=m_b_mod, we_in=m_we_in, we_out=m_we_out,
                 gm_v_g=m_gm_v_g, gm_w_s=m_gm_w_s, gm_b_s=m_gm_b_s, s5_lam_re=m_s5_lam_re, s5_lam_im=m_s5_lam_im,
                 s5_log_dt=m_s5_log_dt, s5_b_re=m_s5_b_re, s5_b_im=m_s5_b_im, s5_c_re=m_s5_c_re, s5_c_im=m_s5_c_im,
                 s5_d=m_s5_d, s5_w_glu=m_s5_w_glu, s5_b_glu=m_s5_b_glu, wo_in=m_wo_in, wo_out=m_wo_out,
                 q_norm_g=m_q_norm_g, k_norm_g=m_k_norm_g, final_g=m_final_g)
    mom_v = dict(c_ctx=v_c_ctx, norm_g=v_norm_g, w_mod=v_w_mod, b_mod=v_b_mod, we_in=v_we_in, we_out=v_we_out,
                 gm_v_g=v_gm_v_g, gm_w_s=v_gm_w_s, gm_b_s=v_gm_b_s, s5_lam_re=v_s5_lam_re, s5_lam_im=v_s5_lam_im,
                 s5_log_dt=v_s5_log_dt, s5_b_re=v_s5_b_re, s5_b_im=v_s5_b_im, s5_c_re=v_s5_c_re, s5_c_im=v_s5_c_im,
                 s5_d=v_s5_d, s5_w_glu=v_s5_w_glu, s5_b_glu=v_s5_b_glu, wo_in=v_wo_in, wo_out=v_wo_out,
                 q_norm_g=v_q_norm_g, k_norm_g=v_k_norm_g, final_g=v_final_g)

    ixy = 2 * lax.axis_index("x") + lax.axis_index("y")
    n_lat = x.shape[1]
    nl = norm_g.shape[0]
    nmod = w_mod.shape[2]
    xin = jnp.concatenate([ctx[0], x[0]], axis=0)

    ic = lax.axis_index("c")
    mine = [lax.dynamic_index_in_dim(weights[n], ic, 0, keepdims=False).astype(BF) for n in _BIG]
    got = exchange([(m_, "xy", "gather") for m_ in mine] + [(c, "xy", "gather")], "gather_weights")
    both = d2d([(g_.reshape(-1, g_.shape[-1]), "gather") for g_ in got[:len(_BIG)]], "swap_weights")
    both = [b_.reshape((2,) + g_.shape) for b_, g_ in zip(both, got)]
    wein = [both[0][0], both[0][1]]
    weout = [both[1][l].reshape(1, D, D) for l in range(2)]
    wglu = [both[2][l].reshape(AW, AW) for l in range(2)]
    woin = [both[3][0], both[3][1]]
    woout = [both[4][l].reshape(1, D, D) for l in range(2)]
    c_group = got[len(_BIG)].reshape(4, D)

    cond = jnp.concatenate([c_group, jnp.broadcast_to(c_ctx.reshape(1, D), (4, D))], axis=0)
    b_shard = lax.dynamic_slice(b_mod, (0, ixy * nmod), (nl, nmod)).reshape(nl, 1, nmod)
    mpart = ada_fwd(cond, w_mod, b_shard)
    m_lat, m_ctx = exchange([(jnp.transpose(mpart[:, 0:4], (1, 0, 2)), "xy", "scatter"),
                             (mpart[:, 4], "xy", "gather")], "exchange_mod")
    m_lat = jnp.transpose(m_lat, (1, 0, 2)).reshape(nl, 3, D)
    m_ctx = jnp.transpose(m_ctx, (1, 0, 2)).reshape(nl, 3, D)
    mods = [jnp.stack([m_ctx[l], m_lat[l]], axis=0) for l in range(nl)]

    loss_part, dx, g, d_norm_g, d_mod_lat, d_mod_ctx, d_final_g = _local_step(
        xin, loss_target[0], mods, wein, weout, wglu, woin, woout, weights)
    grad_x = dx[NC:].reshape(1, n_lat, D)

    d_mod_lat, d_mod_ctx = jnp.stack(d_mod_lat), jnp.stack(d_mod_ctx)
    dm_send = jnp.stack([d_mod_lat.reshape(nl, 4, nmod), d_mod_ctx.reshape(nl, 4, nmod)])
    (dm_got,) = exchange([(jnp.transpose(dm_send, (2, 0, 1, 3)), "xy", "scatter")], "exchange_dmod")
    dm_rows = jnp.concatenate([dm_got[:, 0], dm_got[:, 1]], axis=0)
    gw_mod, d_cctx = ada_bwd(cond, jnp.transpose(dm_rows, (1, 0, 2)), w_mod)
    g_small = dict(c_ctx=d_cctx.reshape(D), norm_g=jnp.stack(d_norm_g), b_mod=add2(d_mod_lat, d_mod_ctx, "add_dbmod"),
                   final_g=d_final_g.reshape(D))
    for name in _SMALL:
        if name not in g_small:
            g_small[name] = jnp.stack(g[name])

    flat = jnp.concatenate([g_small[n].reshape(-1) for n in _SMALL])
    nflat = flat.shape[0]
    npad = -(-nflat // _SMALL_ALIGN) * _SMALL_ALIGN
    flat = jnp.concatenate([flat, jnp.zeros((npad - nflat,), F32)]).reshape(8, npad // (8 * 128), 128)
    pairs = [gw_mod.reshape(2, nl // 2 * D, nmod)]
    for name in _BIG:
        st = jnp.stack(g[name]) if isinstance(g[name], list) else g[name]
        pairs.append(st.reshape(2, -1, st.shape[-1]))
    got_a = d2d([(pairs[k], "swap") for k in (1, 2, 3)], "reduce_chip_a")
    got_b = d2d([(pairs[k], "swap") for k in (0, 4, 5)], "reduce_chip_b")
    theirs = [got_b[0]] + got_a + got_b[1:]
    chip = [sum_own(pairs[k], theirs[k], f"sum_chip{k}", F32 if k == 0 else BF) for k in range(len(pairs))]
    parts = exchange([(flat, "all", "scatter")]
                     + [(s_.reshape(4, s_.shape[0] // 4, s_.shape[1]), "xy", "scatter") for s_ in chip[1:]],
                     "reduce_scatter")
    sums = [sum_lead(pt, f"sum_shard{k}") for k, pt in enumerate(parts)]
    (flat_full,) = exchange([(sums[0], "all", "gather")], "gather_small")
    full = [flat_full] + d2d([(chip[0], "gather")] + [(s_, "gather") for s_ in sums[1:]], "all_gather")
    flat = full[0].reshape(-1)
    grads = {}
    off = 0
    for name in _SMALL:
        sz = weights[name].size
        grads[name] = flat[off:off + sz].reshape(weights[name].shape)
        off += sz
    grads["w_mod"] = full[1].reshape(w_mod.shape)
    for k, name in enumerate(_BIG):
        grads[name] = full[2 + k].reshape(weights[name].shape)

    delta, new_m, new_v = {}, {}, {}
    for name in _WEIGHTS:
        w2 = _view2d(weights[name])
        d2, m2, v2 = adamw(w2, grads[name].reshape(w2.shape), mom_m[name].reshape(w2.shape),
                           mom_v[name].reshape(w2.shape), f"adamw_{name}")
        shp = weights[name].shape
        delta[name], new_m[name], new_v[name] = d2.reshape(shp), m2.reshape(shp), v2.reshape(shp)

    loss = lax.psum(loss_part[0, 0], ("x", "y", "c"))
    return (loss, grad_x, *[grads[n] for n in _WEIGHTS], *[delta[n] for n in _WEIGHTS],
            *[new_m[n] for n in _WEIGHTS], *[new_v[n] for n in _WEIGHTS])


def _local_step(xin, target, mods, wein, weout, wglu, woin, woout, w):
    norm_g, gm_v_g, gm_w_s, gm_b_s = w["norm_g"], w["gm_v_g"], w["gm_w_s"], w["gm_b_s"]
    s5_lam_re, s5_lam_im, s5_log_dt = w["s5_lam_re"], w["s5_lam_im"], w["s5_log_dt"]
    s5_b_re, s5_b_im, s5_c_re, s5_c_im = w["s5_b_re"], w["s5_b_im"], w["s5_c_re"], w["s5_c_im"]
    s5_d, s5_b_glu, q_norm_g, k_norm_g, final_g = w["s5_d"], w["s5_b_glu"], w["q_norm_g"], w["k_norm_g"], w["final_g"]
    nl = norm_g.shape[0]
    n_lat = xin.shape[0] - NC

    cos, sins = _rope_tables(n_lat)

    s5p = []
    for i in range(2):
        lam_l = (s5_lam_re[i].reshape(2, SW), s5_lam_im[i].reshape(2, SW),
                 jnp.repeat(s5_log_dt[i], SP, axis=1))
        lam_r = (jnp.repeat(s5_lam_re[i].reshape(2 * SG, SP), SH, axis=0),
                 jnp.repeat(s5_lam_im[i].reshape(2 * SG, SP), SH, axis=0),
                 jnp.repeat(s5_log_dt[i].reshape(2 * SG, 1), SH, axis=0))
        b_r = (jnp.transpose(s5_b_re[i], (0, 1, 3, 2)).reshape(2 * SG * SH, SP),
               jnp.transpose(s5_b_im[i], (0, 1, 3, 2)).reshape(2 * SG * SH, SP))
        pw_re, pw_im, bbr, bbi = s5_disc(*lam_l, *lam_r, *b_r)
        s5p.append(dict(
            lam_r=lam_r, b_r=b_r, pw_re=pw_re, pw_im=pw_im,
            bb_re=_block_diag(bbr.reshape(2, SG, SH, SP), False).astype(BF),
            bb_im=_block_diag(bbi.reshape(2, SG, SH, SP), False).astype(BF),
            ct_re=_block_diag(s5_c_re[i], True).astype(BF), ct_im=_block_diag(s5_c_im[i], True).astype(BF)))

    saved = []
    xcur = xin
    h = pro_fwd(xcur, norm_g[0].reshape(1, D), mods[0], "pro_fwd0")
    for l in range(nl):
        i = l // 2
        sv = dict(x=xcur, h=h)
        if l % 2 == 0:
            p = mm_nn(h, wein[i], f"in_proj{l}")
            sp = s5p[i]
            for dr, rev in ((0, False), (1, True)):
                sv[f"y{dr}"], sv[f"hpr{dr}"], sv[f"hpi{dr}"] = s5p_fwd(
                    p, sp["bb_re"][dr], sp["bb_im"][dr], sp["ct_re"][dr], sp["ct_im"][dr],
                    sp["pw_re"][dr], sp["pw_im"][dr], rev, f"s5_fwd{l}_{dr}")
            mix = mix_fwd(p, sv["y0"], sv["y1"], gm_v_g[i].reshape(1, AW), gm_w_s[i].astype(BF),
                          gm_b_s[i].reshape(NGRP, CHUNK, 1), s5_d[i].reshape(1, AW), wglu[i],
                          s5_b_glu[i].reshape(1, AW), f"mix_fwd{l}")
            o = mm_nn(mix, weout[i], f"out_proj{l}")
        else:
            p = mm_nn(h, woin[i], f"in_proj{l}")
            sv["qkv"] = attn_prep(p, q_norm_g[i].reshape(1, HD), k_norm_g[i].reshape(1, HD), cos, sins, f"attn_prep{l}")
            sv["o_att"], mix, sv["lse"] = attn_fwd(sv["qkv"], p, f"attn_fwd{l}")
            o = mm_nn(mix, woout[i], f"out_proj{l}")
        sv.update(p=p, mix=mix, o=o)
        saved.append(sv)
        if l < nl - 1:
            xcur, h = res_pro_fwd(xcur, o, mods[l], True, norm_g[l + 1].reshape(1, D), mods[l + 1], f"res_pro_fwd{l}")
        else:
            xcur = res_fwd(xcur, o, mods[l], False, f"res_fwd{l}")

    loss_part, dx, d_final_g, do, dgt = final_loss(xcur, target, final_g.reshape(1, D), saved[-1]["o"], mods[-1])

    g = {}
    gbuf = {}
    d_norm_g, d_mod_lat, d_mod_ctx = [None] * nl, [None] * nl, [None] * nl
    for name in ("s5_w_glu", "gm_v_g", "gm_w_s", "gm_b_s", "s5_lam_re", "s5_lam_im",
                 "s5_log_dt", "s5_b_re", "s5_b_im", "s5_c_re", "s5_c_im", "s5_d", "s5_b_glu", "q_norm_g", "k_norm_g"):
        g[name] = [None, None]
    for l in reversed(range(nl)):
        i = l // 2
        sv = saved[l]
        w_out = weout[i] if l % 2 == 0 else woout[i]
        dmix = mm_nt(do, w_out, f"out_dgrad{l}")
        out_name, in_name = ("we_out", "we_in") if l % 2 == 0 else ("wo_out", "wo_in")
        gbuf[out_name] = mm_tn(sv["mix"], do, 1, f"out_wgrad{l}", slot=i, into=gbuf.get(out_name))
        if l % 2 == 0:
            sp = s5p[i]
            (dp, dy, g["gm_w_s"][i], dbs, dvg, dd, g["s5_w_glu"][i], dbg) = mix_bwd(
                sv["p"], sv["y0"], sv["y1"], dmix, gm_v_g[i].reshape(1, AW), gm_w_s[i].astype(BF),
                gm_b_s[i].reshape(NGRP, CHUNK, 1), s5_d[i].reshape(1, AW), wglu[i], s5_b_glu[i].reshape(1, AW),
                f"mix_bwd{l}")
            g["gm_b_s"][i], g["gm_v_g"][i] = dbs.reshape(NGRP, CHUNK), dvg.reshape(AW)
            g["s5_d"][i], g["s5_b_glu"][i] = dd.reshape(AW), dbg.reshape(AW)
            g["s5_w_glu"][i] = g["s5_w_glu"][i].reshape(4, AW // 4, AW)
            dxd, das_r, das_i, dbbs_r, dbbs_i, dcs_r, dcs_i = [], [], [], [], [], [], []
            for dr, rev in ((0, False), (1, True)):
                dxs_d, da_r, da_i, dbb_r, dbb_i, dc_r, dc_i = s5p_bwd(
                    sv["p"], sv[f"hpr{dr}"], sv[f"hpi{dr}"], dy, sp["bb_re"][dr], sp["bb_im"][dr],
                    sp["ct_re"][dr], sp["ct_im"][dr], sp["pw_re"][dr], -sp["pw_im"][dr],
                    sp["pw_re"][dr][0:1], sp["pw_im"][dr][0:1], rev, f"s5_bwd{l}_{dr}")
                dxd.append(dxs_d)
                das_r.append(jnp.repeat(da_r.reshape(SG, SP), SH, axis=0))
                das_i.append(jnp.repeat(da_i.reshape(SG, SP), SH, axis=0))
                dbbs_r.append(_diag_blocks(dbb_r).reshape(SG * SH, SP))
                dbbs_i.append(_diag_blocks(dbb_i).reshape(SG * SH, SP))
                dcs_r.append(_diag_blocks(dc_r))
                dcs_i.append(_diag_blocks(dc_i))
            cat = lambda parts: jnp.concatenate(parts, axis=0)
            dlr, dli, dldt, dbr, dbi = s5_param_bwd(*sp["lam_r"], *sp["b_r"], cat(das_r), cat(das_i),
                                                    cat(dbbs_r), cat(dbbs_i))
            g["s5_lam_re"][i], g["s5_lam_im"][i] = dlr.reshape(2, SG, SP), dli.reshape(2, SG, SP)
            g["s5_log_dt"][i] = dldt.reshape(2, SG)
            g["s5_b_re"][i] = jnp.transpose(dbr.reshape(2, SG, SH, SP), (0, 1, 3, 2))
            g["s5_b_im"][i] = jnp.transpose(dbi.reshape(2, SG, SH, SP), (0, 1, 3, 2))
            g["s5_c_re"][i], g["s5_c_im"][i] = jnp.stack(dcs_r), jnp.stack(dcs_i)
            dp = s5p_dx_sum(dy, s5_d[i].reshape(1, AW), dxd[0], dxd[1], dp, f"s5_dx_sum{l}")
            w_in = wein[i]
        else:
            dq, dp, dk, dv = attn_bwd(sv["qkv"], sv["p"], dmix, sv["o_att"], sv["lse"], f"attn_bwd{l}")
            dp, dqg, dkg = attn_prep_bwd(sv["p"], dq, dk, dv, q_norm_g[i].reshape(1, HD),
                                         k_norm_g[i].reshape(1, HD), cos, sins, dp, f"attn_prep_bwd{l}")
            g["q_norm_g"][i], g["k_norm_g"][i] = dqg.reshape(HD), dkg.reshape(HD)
            w_in = woin[i]
        dh = mm_nt(dp, w_in, f"in_dgrad{l}")
        gbuf[in_name] = mm_tn(sv["h"], dp, 4, f"in_wgrad{l}", slot=i, into=gbuf.get(in_name))
        dgt_l = dgt
        if l > 0:
            dx, dmod2, dng, do, dgt = pro_res_bwd(sv["x"], dh, dx, norm_g[l].reshape(1, D), mods[l],
                                                  saved[l - 1]["o"], mods[l - 1], f"pro_res_bwd{l}")
        else:
            dx, dmod2, dng = pro_bwd(sv["x"], dh, dx, norm_g[l].reshape(1, D), mods[l], f"pro_bwd{l}")
        d_norm_g[l] = dng.reshape(D)
        d_mod_ctx[l] = jnp.concatenate([dmod2[0, 0], dmod2[0, 1], dgt_l[0]])
        d_mod_lat[l] = jnp.concatenate([dmod2[1, 0], dmod2[1, 1], dgt_l[1]])
    g.update(gbuf)
    return loss_part, dx, g, d_norm_g, d_mod_lat, d_mod_ctx, d_final_g
```

```python
import functools
import math

import numpy as np
import jax
import jax.numpy as jnp
from jax import lax
from jax.experimental import pallas as pl
from jax.experimental.pallas import tpu as pltpu

F32 = jnp.float32
BF = jnp.bfloat16
MESH = pl.DeviceIdType.MESH

D = 1024
NC = 256
SEQ = 4096
GRID_W = 64
TM = 256
CHUNK = 128
EPS = 1e-6
HD = 128
NQ = 8
NKV = 2
ROPE_THETA = 10000.0
SG = 32
SP = 64
SH = 16
SW = SG * SP
GELU_K = math.sqrt(2.0 / math.pi)
GELU_C = 0.044715
VMEM_LIMIT_BYTES = 56 * 1024 * 1024

ADAM_LR = 0.001
ADAM_B1 = 0.9
ADAM_B2 = 0.999
ADAM_EPS = 1e-08
ADAM_WD = 0.01
ADAM_STEP = 10


def _call(body, *, name, out_shape, grid=None, in_specs=None, out_specs=None, scratch=()):
    kw = {}
    if grid is not None:
        kw["grid"] = grid
    if in_specs is not None:
        kw["in_specs"] = in_specs
    if out_specs is not None:
        kw["out_specs"] = out_specs
    return pl.pallas_call(
        body, name=name, out_shape=out_shape, scratch_shapes=list(scratch),
        compiler_params=pltpu.CompilerParams(vmem_limit_bytes=VMEM_LIMIT_BYTES), **kw)


def _dot(a, b, ca=1, cb=0):
    return lax.dot_general(a, b, (((ca,), (cb,)), ((), ())), preferred_element_type=F32)


def _sig(x):
    return 1.0 / (1.0 + jnp.exp(-x))


def _full(shape):
    n = len(shape)
    return pl.BlockSpec(shape, lambda *_: (0,) * n)


def _mm_rows(t):
    for rows in (1088, 1024, 768, 512, 256):
        if t % rows == 0:
            return rows
    raise ValueError(t)


def mm_nn(a, w3, name, out_dtype=F32):
    t, k = a.shape
    j, _, nb = w3.shape
    tr = _mm_rows(t)

    def body(a_ref, w_ref, o_ref):
        o_ref[...] = _dot(a_ref[...], w_ref[0]).astype(o_ref.dtype)

    return _call(body, name=name, grid=(j, t // tr),
                 in_specs=[pl.BlockSpec((tr, k), lambda jj, i: (i, 0)),
                           pl.BlockSpec((1, k, nb), lambda jj, i: (jj, 0, 0))],
                 out_specs=pl.BlockSpec((tr, nb), lambda jj, i: (i, jj)),
                 out_shape=jax.ShapeDtypeStruct((t, j * nb), out_dtype))(a, w3)


def mm_nt(a, w3, name, out_dtype=F32):
    t, _ = a.shape
    j, k, nb = w3.shape
    tr = _mm_rows(t)

    def body(a_ref, w_ref, o_ref):
        acc = _dot(a_ref[:, 0:nb], w_ref[0], 1, 1)
        for jj in range(1, j):
            acc = acc + _dot(a_ref[:, jj * nb:(jj + 1) * nb], w_ref[jj], 1, 1)
        o_ref[...] = acc.astype(o_ref.dtype)

    return _call(body, name=name, grid=(t // tr,),
                 in_specs=[pl.BlockSpec((tr, j * nb), lambda i: (i, 0)), _full((j, k, nb))],
                 out_specs=pl.BlockSpec((tr, k), lambda i: (i, 0)),
                 out_shape=jax.ShapeDtypeStruct((t, k), out_dtype))(a, w3)


def mm_tn(a, b, j, name, slot=0, into=None):
    t, m = a.shape
    nb = b.shape[1] // j
    tr = _mm_rows(t)

    def body(a_ref, b_ref, *rest):
        o_ref = rest[-1]

        @pl.when(pl.program_id(1) == 0)
        def _():
            o_ref[...] = jnp.zeros_like(o_ref)
        o_ref[0, 0] += _dot(a_ref[...], b_ref[...], 0, 0)

    in_specs = [pl.BlockSpec((tr, m), lambda jj, i: (i, 0)), pl.BlockSpec((tr, nb), lambda jj, i: (i, jj))]
    args = [a, b]
    alias = {}
    if into is not None:
        in_specs.append(pl.BlockSpec(memory_space=pl.ANY))
        args.append(into)
        alias = {2: 0}
    return pl.pallas_call(
        body, name=name, grid=(j, t // tr), in_specs=in_specs,
        out_specs=pl.BlockSpec((1, 1, m, nb), lambda jj, i: (slot, jj, 0, 0)),
        out_shape=jax.ShapeDtypeStruct((2, j, m, nb), F32), input_output_aliases=alias,
        compiler_params=pltpu.CompilerParams(vmem_limit_bytes=VMEM_LIMIT_BYTES))(*args)


def _mod_rows(mod_ref, i):
    ctx = i == 0
    sh = jnp.where(ctx, mod_ref[0, 0:1, :], mod_ref[1, 0:1, :])
    sc = jnp.where(ctx, mod_ref[0, 1:2, :], mod_ref[1, 1:2, :])
    gt = jnp.where(ctx, mod_ref[0, 2:3, :], mod_ref[1, 2:3, :])
    return sh, sc, gt


def pro_fwd(x, g, mod, name):
    t = x.shape[0]

    def body(x_ref, g_ref, mod_ref, h_ref):
        sh, sc, _ = _mod_rows(mod_ref, pl.program_id(0))
        xv = x_ref[...]
        r = lax.rsqrt(jnp.mean(xv * xv, axis=-1, keepdims=True) + EPS)
        h_ref[...] = ((xv * r) * g_ref[...] * (1.0 + sc) + sh).astype(BF)

    return _call(body, name=name, grid=(t // TM,),
                 in_specs=[pl.BlockSpec((TM, D), lambda i: (i, 0)), _full((1, D)), _full((2, 3, D))],
                 out_specs=pl.BlockSpec((TM, D), lambda i: (i, 0)),
                 out_shape=jax.ShapeDtypeStruct((t, D), BF))(x, g, mod)


def pro_bwd(x, dh, dxn, g, mod, name):
    t = x.shape[0]

    def body(x_ref, dh_ref, dxn_ref, g_ref, mod_ref, dx_ref, dmod_ref, dg_ref):
        i = pl.program_id(0)

        @pl.when(i == 0)
        def _():
            dmod_ref[...] = jnp.zeros_like(dmod_ref)
            dg_ref[...] = jnp.zeros_like(dg_ref)

        _, sc, _ = _mod_rows(mod_ref, i)
        xv = x_ref[...]
        gv = g_ref[...]
        r = lax.rsqrt(jnp.mean(xv * xv, axis=-1, keepdims=True) + EPS)
        xn = xv * r
        dh_v = dh_ref[...]
        e = dh_v * (1.0 + sc)
        dsh = jnp.sum(dh_v, axis=0, keepdims=True)
        dsc = jnp.sum(dh_v * xn * gv, axis=0, keepdims=True)
        dg_ref[...] += jnp.sum(e * xn, axis=0, keepdims=True)
        dxh = e * gv
        dx_ref[...] = dxn_ref[...] + r * (dxh - xn * jnp.mean(dxh * xn, axis=-1, keepdims=True))

        @pl.when(i == 0)
        def _():
            dmod_ref[0, 0:1, :] += dsh
            dmod_ref[0, 1:2, :] += dsc

        @pl.when(i > 0)
        def _():
            dmod_ref[1, 0:1, :] += dsh
            dmod_ref[1, 1:2, :] += dsc

    tile = pl.BlockSpec((TM, D), lambda i: (i, 0))
    return _call(body, name=name, grid=(t // TM,),
                 in_specs=[tile, tile, tile, _full((1, D)), _full((2, 3, D))],
                 out_specs=[tile, _full((2, 2, D)), _full((1, D))],
                 out_shape=[jax.ShapeDtypeStruct((t, D), F32), jax.ShapeDtypeStruct((2, 2, D), F32),
                            jax.ShapeDtypeStruct((1, D), F32)])(x, dh, dxn, g, mod)


def res_fwd(x, o, mod, update_ctx, name):
    t = x.shape[0]

    def body(x_ref, o_ref, mod_ref, y_ref):
        i = pl.program_id(0)
        _, _, gt = _mod_rows(mod_ref, i)
        upd = x_ref[...] + gt * o_ref[...]
        if update_ctx:
            y_ref[...] = upd
        else:
            y_ref[...] = jnp.where(i == 0, x_ref[...], upd)

    tile = pl.BlockSpec((TM, D), lambda i: (i, 0))
    return _call(body, name=name, grid=(t // TM,), in_specs=[tile, tile, _full((2, 3, D))],
                 out_specs=tile, out_shape=jax.ShapeDtypeStruct((t, D), F32))(x, o, mod)


def res_bwd(dxn, o, mod, update_ctx, name):
    t = dxn.shape[0]

    def body(dxn_ref, o_ref, mod_ref, do_ref, dgt_ref):
        i = pl.program_id(0)

        @pl.when(i == 0)
        def _():
            dgt_ref[...] = jnp.zeros_like(dgt_ref)

        _, _, gt = _mod_rows(mod_ref, i)
        dv = dxn_ref[...]
        do = gt * dv
        dgt = jnp.sum(dv * o_ref[...], axis=0, keepdims=True)
        if update_ctx:
            do_ref[...] = do.astype(BF)
        else:
            do_ref[...] = jnp.where(i == 0, jnp.zeros_like(do), do).astype(BF)

        if update_ctx:
            @pl.when(i == 0)
            def _():
                dgt_ref[0:1, :] += dgt

        @pl.when(i > 0)
        def _():
            dgt_ref[1:2, :] += dgt

    tile = pl.BlockSpec((TM, D), lambda i: (i, 0))
    return _call(body, name=name, grid=(t // TM,), in_specs=[tile, tile, _full((2, 3, D))],
                 out_specs=[tile, _full((2, D))],
                 out_shape=[jax.ShapeDtypeStruct((t, D), BF), jax.ShapeDtypeStruct((2, D), F32)])(dxn, o, mod)


def res_pro_fwd(x, o, mod, update_ctx, g_next, mod_next, name):
    t = x.shape[0]

    def body(x_ref, o_ref, mod_ref, g_ref, modn_ref, y_ref, h_ref):
        i = pl.program_id(0)
        _, _, gt = _mod_rows(mod_ref, i)
        xn = x_ref[...] + gt * o_ref[...]
        if not update_ctx:
            xn = jnp.where(i == 0, x_ref[...], xn)
        y_ref[...] = xn
        sh, sc, _ = _mod_rows(modn_ref, i)
        r = lax.rsqrt(jnp.mean(xn * xn, axis=-1, keepdims=True) + EPS)
        h_ref[...] = ((xn * r) * g_ref[...] * (1.0 + sc) + sh).astype(BF)

    tile = pl.BlockSpec((TM, D), lambda i: (i, 0))
    return _call(body, name=name, grid=(t // TM,),
                 in_specs=[tile, tile, _full((2, 3, D)), _full((1, D)), _full((2, 3, D))], out_specs=[tile, tile],
                 out_shape=[jax.ShapeDtypeStruct((t, D), F32), jax.ShapeDtypeStruct((t, D), BF)])(x, o, mod, g_next, mod_next)


def _res_bwd_part(dx, o_ref, modp_ref, do_ref, dgt_ref, i, update_ctx):
    _, _, gtp = _mod_rows(modp_ref, i)
    do = gtp * dx
    dgt = jnp.sum(dx * o_ref[...], axis=0, keepdims=True)
    if update_ctx:
        do_ref[...] = do.astype(BF)

        @pl.when(i == 0)
        def _():
            dgt_ref[0:1, :] += dgt
    else:
        do_ref[...] = jnp.where(i == 0, jnp.zeros_like(do), do).astype(BF)

    @pl.when(i > 0)
    def _():
        dgt_ref[1:2, :] += dgt


def pro_res_bwd(x, dh, dxn, g, mod, o_prev, mod_prev, name):
    t = x.shape[0]

    def body(x_ref, dh_ref, dxn_ref, g_ref, mod_ref, o_ref, modp_ref, dx_ref, dmod_ref, dg_ref, do_ref, dgt_ref):
        i = pl.program_id(0)

        @pl.when(i == 0)
        def _():
            dmod_ref[...] = jnp.zeros_like(dmod_ref)
            dg_ref[...] = jnp.zeros_like(dg_ref)
            dgt_ref[...] = jnp.zeros_like(dgt_ref)

        _, sc, _ = _mod_rows(mod_ref, i)
        xv = x_ref[...]
        gv = g_ref[...]
        r = lax.rsqrt(jnp.mean(xv * xv, axis=-1, keepdims=True) + EPS)
        xn = xv * r
        dh_v = dh_ref[...]
        e = dh_v * (1.0 + sc)
        dsh = jnp.sum(dh_v, axis=0, keepdims=True)
        dsc = jnp.sum(dh_v * xn * gv, axis=0, keepdims=True)
        dg_ref[...] += jnp.sum(e * xn, axis=0, keepdims=True)
        dxh = e * gv
        dx = dxn_ref[...] + r * (dxh - xn * jnp.mean(dxh * xn, axis=-1, keepdims=True))
        dx_ref[...] = dx

        @pl.when(i == 0)
        def _():
            dmod_ref[0, 0:1, :] += dsh
            dmod_ref[0, 1:2, :] += dsc

        @pl.when(i > 0)
        def _():
            dmod_ref[1, 0:1, :] += dsh
            dmod_ref[1, 1:2, :] += dsc

        _res_bwd_part(dx, o_ref, modp_ref, do_ref, dgt_ref, i, True)

    tile = pl.BlockSpec((TM, D), lambda i: (i, 0))
    return _call(body, name=name, grid=(t // TM,),
                 in_specs=[tile, tile, tile, _full((1, D)), _full((2, 3, D)), tile, _full((2, 3, D))],
                 out_specs=[tile, _full((2, 2, D)), _full((1, D)), tile, _full((2, D))],
                 out_shape=[jax.ShapeDtypeStruct((t, D), F32), jax.ShapeDtypeStruct((2, 2, D), F32),
                            jax.ShapeDtypeStruct((1, D), F32), jax.ShapeDtypeStruct((t, D), BF),
                            jax.ShapeDtypeStruct((2, D), F32)])(x, dh, dxn, g, mod, o_prev, mod_prev)


def final_loss(x, target, g, o_last, mod_last):
    t = x.shape[0]

    def body(x_ref, t_ref, g_ref, o_ref, modp_ref, loss_ref, dx_ref, dg_ref, do_ref, dgt_ref):
        i = pl.program_id(0)

        @pl.when(i == 0)
        def _():
            loss_ref[...] = jnp.zeros_like(loss_ref)
            dg_ref[...] = jnp.zeros_like(dg_ref)
            dx_ref[...] = jnp.zeros_like(dx_ref)
            do_ref[...] = jnp.zeros_like(do_ref)
            dgt_ref[...] = jnp.zeros_like(dgt_ref)

        @pl.when(i > 0)
        def _():
            xv = x_ref[...]
            gv = g_ref[...]
            r = lax.rsqrt(jnp.mean(xv * xv, axis=-1, keepdims=True) + EPS)
            xn = xv * r
            err = xn * gv - t_ref[...]
            loss_ref[...] += (0.5 / D) * jnp.sum(jnp.sum(err * err, axis=1, keepdims=True), axis=0, keepdims=True)
            dy = err * (1.0 / D)
            dg_ref[...] += jnp.sum(dy * xn, axis=0, keepdims=True)
            dxh = dy * gv
            dx = r * (dxh - xn * jnp.mean(dxh * xn, axis=-1, keepdims=True))
            dx_ref[...] = dx
            do_ref[...] = (modp_ref[1, 2:3, :] * dx).astype(BF)
            dgt_ref[1:2, :] += jnp.sum(dx * o_ref[...], axis=0, keepdims=True)

    tile = pl.BlockSpec((TM, D), lambda i: (i, 0))
    return _call(body, name="final_loss", grid=(t // TM,),
                 in_specs=[tile, pl.BlockSpec((TM, D), lambda i: (jnp.maximum(i - 1, 0), 0)), _full((1, D)), tile,
                           _full((2, 3, D))],
                 out_specs=[_full((1, 1)), tile, _full((1, D)), tile, _full((2, D))],
                 out_shape=[jax.ShapeDtypeStruct((1, 1), F32), jax.ShapeDtypeStruct((t, D), F32),
                            jax.ShapeDtypeStruct((1, D), F32), jax.ShapeDtypeStruct((t, D), BF),
                            jax.ShapeDtypeStruct((2, D), F32)])(x, target, g, o_last, mod_last)


def ada_fwd(cond, w_mod, b_mod):
    nl, _, nw = w_mod.shape

    def body(c_ref, w_ref, b_ref, o_ref):
        cv = c_ref[...]
        s = (cv * _sig(cv)).astype(BF)
        o_ref[0] = _dot(s, w_ref[0].astype(BF)) + b_ref[0]

    return _call(body, name="ada_fwd", grid=(nl,),
                 in_specs=[_full((8, D)), pl.BlockSpec((1, D, nw), lambda l: (l, 0, 0)),
                           pl.BlockSpec((1, 1, nw), lambda l: (l, 0, 0))],
                 out_specs=pl.BlockSpec((1, 8, nw), lambda l: (l, 0, 0)),
                 out_shape=jax.ShapeDtypeStruct((nl, 8, nw), F32))(cond, w_mod, b_mod)


def ada_bwd(cond, dm, w_mod):
    nl, _, nw = w_mod.shape

    def body(c_ref, dm_ref, w_ref, gw_ref, dcc_ref, dc_ref):
        l = pl.program_id(0)

        @pl.when(l == 0)
        def _():
            dc_ref[...] = jnp.zeros_like(dc_ref)

        cv = c_ref[...]
        sg = _sig(cv)
        s = (cv * sg).astype(BF)
        dmv = dm_ref[0].astype(BF)
        gw_ref[0] = _dot(s, dmv, 0, 0)
        dc_ref[...] += _dot(dmv, w_ref[0].astype(BF), 1, 1)

        @pl.when(l == nl - 1)
        def _():
            dcond = dc_ref[...] * (sg * (1.0 + cv * (1.0 - sg)))
            dcc_ref[...] = jnp.sum(dcond[4:8], axis=0, keepdims=True)

    return _call(body, name="ada_bwd", grid=(nl,),
                 in_specs=[_full((8, D)), pl.BlockSpec((1, 8, nw), lambda l: (l, 0, 0)),
                           pl.BlockSpec((1, D, nw), lambda l: (l, 0, 0))],
                 out_specs=[pl.BlockSpec((1, D, nw), lambda l: (l, 0, 0)), _full((1, D))],
                 out_shape=[jax.ShapeDtypeStruct((nl, D, nw), F32), jax.ShapeDtypeStruct((1, D), F32)],
                 scratch=[pltpu.VMEM((8, D), F32)])(cond, dm, w_mod)


def add2(a, b, name):
    def body(a_ref, b_ref, o_ref):
        o_ref[...] = a_ref[...] + b_ref[...]

    return _call(body, name=name, out_shape=jax.ShapeDtypeStruct(a.shape, a.dtype))(a, b)


AW = 512
NGRP = 4


def Y4_SPEC():
    return pl.BlockSpec((AW // 128, TM, 128), lambda i: (0, i, 0))


def _cat_lanes(ref):
    return jnp.concatenate([ref[q] for q in range(ref.shape[0])], axis=1)


def _gelu(y):
    t = jnp.tanh(GELU_K * (y + GELU_C * y * y * y))
    return 0.5 * y * (1.0 + t), t


def _layer_norm_stats(v):
    mu = jnp.mean(v, axis=-1, keepdims=True)
    vc = v - mu
    rstd = lax.rsqrt(jnp.mean(vc * vc, axis=-1, keepdims=True) + EPS)
    return vc * rstd, rstd


def _spatial_mix(vn_ref, ws_ref, bs_ref, mixed_ref):
    for ch in range(TM // CHUNK):
        rows = slice(ch * CHUNK, (ch + 1) * CHUNK)
        for g in range(NGRP):
            cols = slice(g * CHUNK, (g + 1) * CHUNK)
            mixed_ref[rows, cols] = _dot(ws_ref[g], vn_ref[rows, cols]) + bs_ref[g]


def mix_fwd(p, yf, yb, vg, ws, bs, dsk, wglu, bglu, name):
    t = p.shape[0]

    def body(p_ref, yf_ref, yb_ref, vg_ref, ws_ref, bs_ref, d_ref, wg_ref, bg_ref, o_ref, vn_ref, mixed_ref):
        vhat, _ = _layer_norm_stats(p_ref[:, AW:2 * AW])
        vn_ref[...] = (vhat * vg_ref[...]).astype(BF)
        _spatial_mix(vn_ref, ws_ref, bs_ref, mixed_ref)
        ga = p_ref[:, 2 * AW:3 * AW]
        o_ref[:, 0:AW] = (p_ref[:, 0:AW] * mixed_ref[...] * (ga * _sig(ga))).astype(BF)
        y = _cat_lanes(yf_ref) + _cat_lanes(yb_ref) + d_ref[...] * p_ref[:, 3 * AW:4 * AW]
        y2, _ = _gelu(y)
        z = _dot(y2.astype(BF), wg_ref[...]) + bg_ref[...]
        gb = p_ref[:, 4 * AW:5 * AW]
        o_ref[:, AW:2 * AW] = (y2 * _sig(z) * (gb * _sig(gb))).astype(BF)

    tile = lambda w: pl.BlockSpec((TM, w), lambda i: (i, 0))
    return _call(body, name=name, grid=(t // TM,),
                 in_specs=[tile(5 * AW), Y4_SPEC(), Y4_SPEC(), _full((1, AW)), _full((NGRP, CHUNK, CHUNK)),
                           _full((NGRP, CHUNK, 1)), _full((1, AW)), _full((AW, AW)), _full((1, AW))],
                 out_specs=tile(2 * AW), out_shape=jax.ShapeDtypeStruct((t, 2 * AW), BF),
                 scratch=[pltpu.VMEM((TM, AW), BF), pltpu.VMEM((TM, AW), F32)])(p, yf, yb, vg, ws, bs, dsk, wglu, bglu)


def mix_bwd(p, yf, yb, dmix, vg, ws, bs, dsk, wglu, bglu, name):
    t = p.shape[0]

    def body(p_ref, yf_ref, yb_ref, dm_ref, vg_ref, ws_ref, bs_ref, d_ref, wg_ref, bg_ref,
             dpa_ref, dy_ref, dws_ref, dbs_ref, dvg_ref, dd_ref, dwg_ref, dbg_ref,
             vn_ref, mixed_ref, dmx_ref, dvn_ref):
        @pl.when(pl.program_id(0) == 0)
        def _():
            for r in (dws_ref, dbs_ref, dvg_ref, dd_ref, dwg_ref, dbg_ref):
                r[...] = jnp.zeros_like(r)

        vhat, rstd = _layer_norm_stats(p_ref[:, AW:2 * AW])
        vgv = vg_ref[...]
        vn_ref[...] = (vhat * vgv).astype(BF)
        _spatial_mix(vn_ref, ws_ref, bs_ref, mixed_ref)
        u = p_ref[:, 0:AW]
        ga = p_ref[:, 2 * AW:3 * AW]
        sga = _sig(ga)
        dya = dm_ref[:, 0:AW]
        mixed = mixed_ref[...]
        dpa_ref[:, 0:AW] = (dya * mixed * (ga * sga)).astype(BF)
        dpa_ref[:, 2 * AW:3 * AW] = (dya * u * mixed * (sga * (1.0 + ga * (1.0 - sga)))).astype(BF)
        dmx_ref[...] = dya * u * (ga * sga)
        for ch in range(TM // CHUNK):
            rows = slice(ch * CHUNK, (ch + 1) * CHUNK)
            for g in range(NGRP):
                cols = slice(g * CHUNK, (g + 1) * CHUNK)
                dmx = dmx_ref[rows, cols]
                dmxb = dmx.astype(BF)
                dws_ref[g] += _dot(dmxb, vn_ref[rows, cols], 1, 1)
                dbs_ref[g] += jnp.sum(dmx, axis=1, keepdims=True)
                dvn_ref[rows, cols] = _dot(ws_ref[g], dmxb, 0, 0)
        dvn = dvn_ref[...]
        dvg_ref[...] += jnp.sum(dvn * vhat, axis=0, keepdims=True)
        dvh = dvn * vgv
        dpa_ref[:, AW:2 * AW] = (rstd * (dvh - jnp.mean(dvh, axis=-1, keepdims=True)
                                         - vhat * jnp.mean(dvh * vhat, axis=-1, keepdims=True))).astype(BF)

        xs = p_ref[:, 3 * AW:4 * AW]
        y = _cat_lanes(yf_ref) + _cat_lanes(yb_ref) + d_ref[...] * xs
        y2, th = _gelu(y)
        y2b = y2.astype(BF)
        z = _dot(y2b, wg_ref[...]) + bg_ref[...]
        sz = _sig(z)
        gb = p_ref[:, 4 * AW:5 * AW]
        sgb = _sig(gb)
        dyb = dm_ref[:, AW:2 * AW]
        dpa_ref[:, 4 * AW:5 * AW] = (dyb * (y2 * sz) * (sgb * (1.0 + gb * (1.0 - sgb)))).astype(BF)
        dy3 = dyb * (gb * sgb)
        dz = dy3 * y2 * sz * (1.0 - sz)
        dzb = dz.astype(BF)
        dwg_ref[...] += _dot(y2b, dzb, 0, 0)
        dbg_ref[...] += jnp.sum(dz, axis=0, keepdims=True)
        dy2 = dy3 * sz + _dot(dzb, wg_ref[...], 1, 1)
        dgelu = 0.5 * (1.0 + th) + 0.5 * y * (1.0 - th * th) * GELU_K * (1.0 + 3.0 * GELU_C * y * y)
        dy = dy2 * dgelu
        dd_ref[...] += jnp.sum(dy * xs, axis=0, keepdims=True)
        dy_ref[...] = dy

    tile = lambda w: pl.BlockSpec((TM, w), lambda i: (i, 0))
    return _call(body, name=name, grid=(t // TM,),
                 in_specs=[tile(5 * AW), Y4_SPEC(), Y4_SPEC(), tile(2 * AW), _full((1, AW)), _full((NGRP, CHUNK, CHUNK)),
                           _full((NGRP, CHUNK, 1)), _full((1, AW)), _full((AW, AW)), _full((1, AW))],
                 out_specs=[tile(5 * AW), tile(AW), _full((NGRP, CHUNK, CHUNK)), _full((NGRP, CHUNK, 1)),
                            _full((1, AW)), _full((1, AW)), _full((AW, AW)), _full((1, AW))],
                 out_shape=[jax.ShapeDtypeStruct((t, 5 * AW), BF),
                            jax.ShapeDtypeStruct((t, AW), F32), jax.ShapeDtypeStruct((NGRP, CHUNK, CHUNK), F32),
                            jax.ShapeDtypeStruct((NGRP, CHUNK, 1), F32), jax.ShapeDtypeStruct((1, AW), F32),
                            jax.ShapeDtypeStruct((1, AW), F32), jax.ShapeDtypeStruct((AW, AW), F32),
                            jax.ShapeDtypeStruct((1, AW), F32)],
                 scratch=[pltpu.VMEM((TM, AW), BF), pltpu.VMEM((TM, AW), F32), pltpu.VMEM((TM, AW), F32),
                          pltpu.VMEM((TM, AW), F32)])(p, yf, yb, dmix, vg, ws, bs, dsk, wglu, bglu)


LN = 512
NBLK = SW // LN
UB = AW // NBLK
SCAN_R = 32
SCAN_G = TM // SCAN_R
PW_ROWS = SCAN_R
POW_EXP = list(range(1, SCAN_R + 1))
GROUPS_PER_TILE = TM // 8


def s5_disc(lam_re, lam_im, dt, lam_re_r, lam_im_r, dt_r, b_re, b_im):
    nexp = jnp.asarray(np.array(POW_EXP, np.float32).reshape(PW_ROWS, 1))

    def body(n_ref, lr_ref, li_ref, dt_ref, lrr_ref, lir_ref, dtr_ref, br_ref, bi_ref,
             pr_ref, pi_ref, bbr_ref, bbi_ref):
        for dr in range(2):
            dtl = jnp.exp(dt_ref[dr:dr + 1, :])
            zr = lr_ref[dr:dr + 1, :] * dtl
            zi = li_ref[dr:dr + 1, :] * dtl
            mag = jnp.exp(n_ref[...] * zr)
            ang = n_ref[...] * zi
            pr_ref[dr] = mag * jnp.cos(ang)
            pi_ref[dr] = mag * jnp.sin(ang)
        lr, li, dtv = lrr_ref[...], lir_ref[...], jnp.exp(dtr_ref[...])
        mag = jnp.exp(lr * dtv)
        nr = mag * jnp.cos(li * dtv) - 1.0
        ni = mag * jnp.sin(li * dtv)
        den = lr * lr + li * li
        fr = (nr * lr + ni * li) / den
        fi = (ni * lr - nr * li) / den
        bbr_ref[...] = fr * br_ref[...] - fi * bi_ref[...]
        bbi_ref[...] = fr * bi_ref[...] + fi * br_ref[...]

    rows = lam_re_r.shape[0]
    return _call(body, name="s5_disc",
                 out_shape=[jax.ShapeDtypeStruct((2, PW_ROWS, SW), F32), jax.ShapeDtypeStruct((2, PW_ROWS, SW), F32),
                            jax.ShapeDtypeStruct((rows, SP), F32), jax.ShapeDtypeStruct((rows, SP), F32)])(
        nexp, lam_re, lam_im, dt, lam_re_r, lam_im_r, dt_r, b_re, b_im)


def s5_param_bwd(lam_re_r, lam_im_r, dt_r, b_re, b_im, da_re, da_im, dbb_re, dbb_im):
    rows = lam_re_r.shape[0]
    ng = rows // SH
    seg = jnp.asarray(np.kron(np.eye(ng, dtype=np.float32), np.ones((1, SH), np.float32)))

    def body(seg_ref, lr_ref, li_ref, dt_ref, br_ref, bi_ref, dar_ref, dai_ref, dbbr_ref, dbbi_ref,
             dlr_ref, dli_ref, ddt_ref, dbr_ref, dbi_ref):
        lr, li, dtv = lr_ref[...], li_ref[...], jnp.exp(dt_ref[...])
        mag = jnp.exp(lr * dtv)
        lbr = mag * jnp.cos(li * dtv)
        lbi = mag * jnp.sin(li * dtv)
        den = lr * lr + li * li
        nr, ni = lbr - 1.0, lbi
        fr = (nr * lr + ni * li) / den
        fi = (ni * lr - nr * li) / den
        br, bi = br_ref[...], bi_ref[...]
        gbr, gbi = dbbr_ref[...], dbbi_ref[...]
        dbr_ref[...] = gbr * fr + gbi * fi
        dbi_ref[...] = gbi * fr - gbr * fi
        gfr = gbr * br + gbi * bi
        gfi = gbi * br - gbr * bi
        ilr, ili = lr / den, -li / den
        gnr = gfr * ilr + gfi * ili
        gni = gfi * ilr - gfr * ili
        qr = -(fr * ilr - fi * ili)
        qi = -(fr * ili + fi * ilr)
        glr = gfr * qr + gfi * qi
        gli = gfi * qr - gfr * qi
        first = (lax.broadcasted_iota(jnp.int32, (rows, 1), 0) % SH) == 0
        glbr = gnr + jnp.where(first, dar_ref[...], 0.0)
        glbi = gni + jnp.where(first, dai_ref[...], 0.0)
        gzr = glbr * lbr + glbi * lbi
        gzi = glbi * lbr - glbr * lbi
        glr = glr + gzr * dtv
        gli = gli + gzi * dtv
        gdt = (gzr * lr + gzi * li) * dtv
        hi = lax.Precision.HIGHEST
        sg = seg_ref[...]
        dlr_ref[...] = jnp.dot(sg, glr, precision=hi, preferred_element_type=F32)
        dli_ref[...] = jnp.dot(sg, gli, precision=hi, preferred_element_type=F32)
        ddt_ref[...] = jnp.sum(jnp.dot(sg, gdt, precision=hi, preferred_element_type=F32), axis=1, keepdims=True)

    return _call(body, name="s5_param_bwd",
                 out_shape=[jax.ShapeDtypeStruct((ng, SP), F32), jax.ShapeDtypeStruct((ng, SP), F32),
                            jax.ShapeDtypeStruct((ng, 1), F32), jax.ShapeDtypeStruct((rows, SP), F32),
                            jax.ShapeDtypeStruct((rows, SP), F32)])(
        seg, lam_re_r, lam_im_r, dt_r, b_re, b_im, da_re, da_im, dbb_re, dbb_im)


def _scan_tile(hr_ref, hi_ref, er_ref, ei_ref, st_ref, cr_ref, ci_ref, pr_ref, pi_ref, reverse):
    row8 = lax.broadcasted_iota(jnp.int32, (TM, 1), 0) % 8
    rowe = lax.broadcasted_iota(jnp.int32, (2 * GROUPS_PER_TILE, 1), 0)
    ne = 2 * GROUPS_PER_TILE
    for blk in range(NBLK):
        cols = slice(blk * LN, (blk + 1) * LN)
        hr, hi = hr_ref[:, cols], hi_ref[:, cols]
        for k, s in enumerate((1, 2, 4)):
            ar, ai = pr_ref[k:k + 1, cols], pi_ref[k:k + 1, cols]
            if reverse:
                m = row8 < 8 - s
                sr, si = pltpu.roll(hr, TM - s, 0), pltpu.roll(hi, TM - s, 0)
            else:
                m = row8 >= s
                sr, si = pltpu.roll(hr, s, 0), pltpu.roll(hi, s, 0)
            sr, si = jnp.where(m, sr, 0.0), jnp.where(m, si, 0.0)
            hr, hi = hr + ar * sr - ai * si, hi + ar * si + ai * sr
        hr_ref[:, cols] = hr
        hi_ref[:, cols] = hi
        edge = 0 if reverse else 7
        nq = LN // 128
        for q in range(nq):
            st_ref[q] = hr[:, q * 128:(q + 1) * 128]
            st_ref[nq + q] = hi[:, q * 128:(q + 1) * 128]
        gr = jnp.concatenate([st_ref[q, pl.ds(edge, GROUPS_PER_TILE, stride=8), :] for q in range(nq)], axis=1)
        gi = jnp.concatenate([st_ref[nq + q, pl.ds(edge, GROUPS_PER_TILE, stride=8), :] for q in range(nq)], axis=1)
        zero = jnp.zeros((GROUPS_PER_TILE, LN), F32)
        if reverse:
            er_ref[0:GROUPS_PER_TILE, :] = gr
            ei_ref[0:GROUPS_PER_TILE, :] = gi
            er_ref[GROUPS_PER_TILE:ne, :] = zero
            ei_ref[GROUPS_PER_TILE:ne, :] = zero
            er_ref[GROUPS_PER_TILE:GROUPS_PER_TILE + 1, :] = cr_ref[:, cols]
            ei_ref[GROUPS_PER_TILE:GROUPS_PER_TILE + 1, :] = ci_ref[:, cols]
        else:
            er_ref[0:GROUPS_PER_TILE, :] = zero
            ei_ref[0:GROUPS_PER_TILE, :] = zero
            er_ref[GROUPS_PER_TILE - 1:GROUPS_PER_TILE, :] = cr_ref[:, cols]
            ei_ref[GROUPS_PER_TILE - 1:GROUPS_PER_TILE, :] = ci_ref[:, cols]
            er_ref[GROUPS_PER_TILE:ne, :] = gr
            ei_ref[GROUPS_PER_TILE:ne, :] = gi
        evr, evi = er_ref[...], ei_ref[...]
        s = 1
        k = 3
        while s < ne:
            ar, ai = pr_ref[k:k + 1, cols], pi_ref[k:k + 1, cols]
            if reverse:
                m = rowe < ne - s
                sr, si = pltpu.roll(evr, ne - s, 0), pltpu.roll(evi, ne - s, 0)
            else:
                m = rowe >= s
                sr, si = pltpu.roll(evr, s, 0), pltpu.roll(evi, s, 0)
            sr, si = jnp.where(m, sr, 0.0), jnp.where(m, si, 0.0)
            evr, evi = evr + ar * sr - ai * si, evi + ar * si + ai * sr
            s *= 2
            k += 1
        er_ref[...] = evr
        ei_ref[...] = evi
        if reverse:
            cr_ref[:, cols] = er_ref[0:1, :]
            ci_ref[:, cols] = ei_ref[0:1, :]
            apr, api = pr_ref[24:32, cols], pi_ref[24:32, cols]
        else:
            cr_ref[:, cols] = er_ref[ne - 1:ne, :]
            ci_ref[:, cols] = ei_ref[ne - 1:ne, :]
            apr, api = pr_ref[16:24, cols], pi_ref[16:24, cols]
        for g in range(GROUPS_PER_TILE):
            e_row = g + 1 if reverse else GROUPS_PER_TILE - 1 + g
            kr, ki = er_ref[e_row:e_row + 1, :], ei_ref[e_row:e_row + 1, :]
            rows = slice(8 * g, 8 * g + 8)
            hr_ref[rows, cols] = hr_ref[rows, cols] + apr * kr - api * ki
            hi_ref[rows, cols] = hi_ref[rows, cols] + apr * ki + api * kr


def _tile_order(kind, nt):
    if kind == "fwd":
        return lambda i: i
    if kind == "bwd":
        return lambda i: jnp.where(i == 0, 0, nt - i)
    if kind == "fwd_adj":
        return lambda i: nt - 1 - i
    if kind == "bwd_adj":
        return lambda i: jnp.where(i == nt - 1, 0, i + 1)
    raise ValueError(kind)


def s5_fwd(p, bb_re, bb_im, ct_re, ct_im, pw_re, pw_im, reverse, name):
    t = p.shape[0]
    nt = t // TM
    order = _tile_order("bwd" if reverse else "fwd", nt)

    def body(x_ref, bbr_ref, bbi_ref, ctr_ref, cti_ref, pr_ref, pi_ref, y_ref, hpr_ref, hpi_ref,
             hr_ref, hi_ref, er_ref, ei_ref, st_ref, cr_ref, ci_ref, c0r_ref, c0i_ref):
        @pl.when(pl.program_id(0) == 0)
        def _():
            cr_ref[...] = jnp.zeros_like(cr_ref)
            ci_ref[...] = jnp.zeros_like(ci_ref)

        c0r_ref[...] = cr_ref[...]
        c0i_ref[...] = ci_ref[...]
        xb = x_ref[...].astype(BF)
        for j in range(NBLK):
            cols = slice(j * LN, (j + 1) * LN)
            hr_ref[:, cols] = _dot(xb[:, j * UB:(j + 1) * UB], bbr_ref[j])
            hi_ref[:, cols] = _dot(xb[:, j * UB:(j + 1) * UB], bbi_ref[j])
        _scan_tile(hr_ref, hi_ref, er_ref, ei_ref, st_ref, cr_ref, ci_ref, pr_ref, pi_ref, reverse)
        rowi = lax.broadcasted_iota(jnp.int32, (TM, 1), 0)
        for j in range(NBLK):
            cols = slice(j * LN, (j + 1) * LN)
            hr, hi = hr_ref[:, cols], hi_ref[:, cols]
            y_ref[:, j * UB:(j + 1) * UB] = _dot(hr.astype(BF), ctr_ref[j]) - _dot(hi.astype(BF), cti_ref[j])
            if reverse:
                first = rowi == TM - 1
                sr, si = pltpu.roll(hr, TM - 1, 0), pltpu.roll(hi, TM - 1, 0)
            else:
                first = rowi == 0
                sr, si = pltpu.roll(hr, 1, 0), pltpu.roll(hi, 1, 0)
            hpr_ref[:, cols] = jnp.where(first, c0r_ref[:, cols], sr)
            hpi_ref[:, cols] = jnp.where(first, c0i_ref[:, cols], si)

    state = lambda: pl.BlockSpec((TM, SW), lambda i: (order(i), 0))
    return _call(body, name=name, grid=(nt,),
                 in_specs=[pl.BlockSpec((TM, AW), lambda i: (order(i), 3)),
                           _full((NBLK, UB, LN)), _full((NBLK, UB, LN)), _full((NBLK, LN, UB)), _full((NBLK, LN, UB)),
                           _full((PW_ROWS, SW)), _full((PW_ROWS, SW))],
                 out_specs=[pl.BlockSpec((TM, AW), lambda i: (order(i), 0)), state(), state()],
                 out_shape=[jax.ShapeDtypeStruct((t, AW), F32), jax.ShapeDtypeStruct((t, SW), F32),
                            jax.ShapeDtypeStruct((t, SW), F32)],
                 scratch=[pltpu.VMEM((TM, SW), F32), pltpu.VMEM((TM, SW), F32),
                          pltpu.VMEM((2 * GROUPS_PER_TILE, LN), F32), pltpu.VMEM((2 * GROUPS_PER_TILE, LN), F32),
                          pltpu.VMEM((2 * LN // 128, TM, 128), F32),
                          pltpu.VMEM((1, SW), F32), pltpu.VMEM((1, SW), F32),
                          pltpu.VMEM((1, SW), F32), pltpu.VMEM((1, SW), F32)])(
        p, bb_re, bb_im, ct_re, ct_im, pw_re, pw_im)


def s5_bwd(p, hp_re, hp_im, dy, bb_re, bb_im, ct_re, ct_im, pw_re, pw_im_conj, a_re, a_im, reverse, name):
    t = p.shape[0]
    nt = t // TM
    order = _tile_order("bwd_adj" if reverse else "fwd_adj", nt)

    def body(x_ref, hpr_ref, hpi_ref, dy_ref, bbr_ref, bbi_ref, ctr_ref, cti_ref, pr_ref, pi_ref, ar_ref, ai_ref,
             dx_ref, dar_ref, dai_ref, dbbr_ref, dbbi_ref, dcr_ref, dci_ref,
             gr_ref, gi_ref, er_ref, ei_ref, st_ref, cr_ref, ci_ref):
        @pl.when(pl.program_id(0) == 0)
        def _():
            for r in (cr_ref, ci_ref, dar_ref, dai_ref, dbbr_ref, dbbi_ref, dcr_ref, dci_ref):
                r[...] = jnp.zeros_like(r)

        xb = x_ref[...].astype(BF)
        dyb = dy_ref[...].astype(BF)
        for j in range(NBLK):
            cols = slice(j * LN, (j + 1) * LN)
            gr_ref[:, cols] = _dot(dyb[:, j * UB:(j + 1) * UB], ctr_ref[j], 1, 1)
            gi_ref[:, cols] = -_dot(dyb[:, j * UB:(j + 1) * UB], cti_ref[j], 1, 1)
        _scan_tile(gr_ref, gi_ref, er_ref, ei_ref, st_ref, cr_ref, ci_ref, pr_ref, pi_ref, not reverse)
        for j in range(NBLK):
            cols = slice(j * LN, (j + 1) * LN)
            xj = xb[:, j * UB:(j + 1) * UB]
            dyj = dyb[:, j * UB:(j + 1) * UB]
            hpr, hpi = hpr_ref[:, cols], hpi_ref[:, cols]
            gr, gi = gr_ref[:, cols], gi_ref[:, cols]
            ar, ai = ar_ref[:, cols], ai_ref[:, cols]
            hr = ar * hpr - ai * hpi + _dot(xj, bbr_ref[j])
            hi = ar * hpi + ai * hpr + _dot(xj, bbi_ref[j])
            dar_ref[:, cols] += jnp.sum(gr * hpr + gi * hpi, axis=0, keepdims=True)
            dai_ref[:, cols] += jnp.sum(gi * hpr - gr * hpi, axis=0, keepdims=True)
            grb, gib = gr.astype(BF), gi.astype(BF)
            dcr_ref[j] += _dot(dyj, hr.astype(BF), 0, 0)
            dci_ref[j] += -_dot(dyj, hi.astype(BF), 0, 0)
            dbbr_ref[j] += _dot(xj, grb, 0, 0)
            dbbi_ref[j] += _dot(xj, gib, 0, 0)
            dx_ref[:, j * UB:(j + 1) * UB] = _dot(grb, bbr_ref[j], 1, 1) + _dot(gib, bbi_ref[j], 1, 1)

    state = lambda: pl.BlockSpec((TM, SW), lambda i: (order(i), 0))
    blockd = lambda: _full((NBLK, UB, LN))
    return _call(body, name=name, grid=(nt,),
                 in_specs=[pl.BlockSpec((TM, AW), lambda i: (order(i), 3)), state(), state(),
                           pl.BlockSpec((TM, AW), lambda i: (order(i), 0)),
                           blockd(), blockd(), _full((NBLK, LN, UB)), _full((NBLK, LN, UB)),
                           _full((PW_ROWS, SW)), _full((PW_ROWS, SW)), _full((1, SW)), _full((1, SW))],
                 out_specs=[pl.BlockSpec((TM, AW), lambda i: (order(i), 0)), _full((1, SW)), _full((1, SW)),
                            blockd(), blockd(), blockd(), blockd()],
                 out_shape=[jax.ShapeDtypeStruct((t, AW), F32), jax.ShapeDtypeStruct((1, SW), F32),
                            jax.ShapeDtypeStruct((1, SW), F32)] + [jax.ShapeDtypeStruct((NBLK, UB, LN), F32)] * 4,
                 scratch=[pltpu.VMEM((TM, SW), F32), pltpu.VMEM((TM, SW), F32),
                          pltpu.VMEM((2 * GROUPS_PER_TILE, LN), F32), pltpu.VMEM((2 * GROUPS_PER_TILE, LN), F32),
                          pltpu.VMEM((2 * LN // 128, TM, 128), F32),
                          pltpu.VMEM((1, SW), F32), pltpu.VMEM((1, SW), F32)])(
        p, hp_re, hp_im, dy, bb_re, bb_im, ct_re, ct_im, pw_re, pw_im_conj, a_re, a_im)


def s5_dx_sum(dy, dsk, dxf, dxb, name):
    t = dy.shape[0]

    def body(dy_ref, d_ref, f_ref, b_ref, o_ref):
        o_ref[...] = (dy_ref[...] * d_ref[...] + f_ref[...] + b_ref[...]).astype(BF)

    tile = pl.BlockSpec((TM, AW), lambda i: (i, 0))
    return _call(body, name=name, grid=(t // TM,), in_specs=[tile, _full((1, AW)), tile, tile], out_specs=tile,
                 out_shape=jax.ShapeDtypeStruct((t, AW), BF))(dy, dsk, dxf, dxb)


XS_BLK = 3 * AW // 128


def _load_perm(refs):
    return jnp.concatenate(
        [jnp.concatenate([ref[pl.ds(r, SCAN_G, stride=SCAN_R), :] for ref in refs], axis=1) for r in range(SCAN_R)],
        axis=0)


def _store_perm(out_ref, val):
    for r in range(SCAN_R):
        for q in range(AW // 128):
            out_ref[q, pl.ds(r, SCAN_G, stride=SCAN_R), :] = val[r * SCAN_G:(r + 1) * SCAN_G, q * 128:(q + 1) * 128]


def _scan_perm(hr_ref, hi_ref, er_ref, ei_ref, cr_ref, ci_ref, pr_ref, pi_ref, reverse, hpr_ref=None, hpi_ref=None):
    gpt = SCAN_G
    nr = SCAN_R
    offsets = list(range(nr))[::-1] if reverse else list(range(nr))
    blocks = [slice(b * LN, (b + 1) * LN) for b in range(NBLK)]
    slab = lambda r: slice(r * gpt, (r + 1) * gpt)
    a1 = [(pr_ref[0:1, c], pi_ref[0:1, c]) for c in blocks]
    x = [None] * NBLK
    for r in offsets:
        for b, c in enumerate(blocks):
            if x[b] is None:
                x[b] = (hr_ref[slab(r), c], hi_ref[slab(r), c])
            else:
                (ar, ai), (xr, xi) = a1[b], x[b]
                x[b] = (hr_ref[slab(r), c] + ar * xr - ai * xi, hi_ref[slab(r), c] + ar * xi + ai * xr)
                hr_ref[slab(r), c] = x[b][0]
                hi_ref[slab(r), c] = x[b][1]
    an = [(pr_ref[nr - 1:nr, c], pi_ref[nr - 1:nr, c]) for c in blocks]
    k = [(cr_ref[:, c], ci_ref[:, c]) for c in blocks]
    for g in (range(gpt - 1, -1, -1) if reverse else range(gpt)):
        for b, c in enumerate(blocks):
            (ar, ai), (kr, ki), (xr, xi) = an[b], k[b], x[b]
            er_ref[g:g + 1, c] = kr
            ei_ref[g:g + 1, c] = ki
            k[b] = (xr[g:g + 1, :] + ar * kr - ai * ki, xi[g:g + 1, :] + ar * ki + ai * kr)
    for b, c in enumerate(blocks):
        cr_ref[:, c] = k[b][0]
        ci_ref[:, c] = k[b][1]
    for r in range(nr):
        prow = nr - 1 - r if reverse else r
        dst = r - 1 if reverse else r + 1
        for c in blocks:
            apr, api = pr_ref[prow:prow + 1, c], pi_ref[prow:prow + 1, c]
            cinr, cini = er_ref[:, c], ei_ref[:, c]
            hr = hr_ref[slab(r), c] + apr * cinr - api * cini
            hi = hi_ref[slab(r), c] + apr * cini + api * cinr
            hr_ref[slab(r), c] = hr
            hi_ref[slab(r), c] = hi
            if hpr_ref is not None:
                if 0 <= dst < nr:
                    hpr_ref[slab(dst), c] = hr
                    hpi_ref[slab(dst), c] = hi
                if r == (nr - 1 if reverse else 0):
                    hpr_ref[slab(r), c] = cinr
                    hpi_ref[slab(r), c] = cini


def s5p_fwd(p, bb_re, bb_im, ct_re, ct_im, pw_re, pw_im, reverse, name):
    t = p.shape[0]
    nt = t // TM
    order = _tile_order("bwd" if reverse else "fwd", nt)
    nq = AW // 128

    def body(*refs):
        x_refs = refs[:nq]
        bbr_ref, bbi_ref, ctr_ref, cti_ref, pr_ref, pi_ref, y_ref, hpr_ref, hpi_ref = refs[nq:nq + 9]
        hr_ref, hi_ref, er_ref, ei_ref, cr_ref, ci_ref = refs[nq + 9:]

        @pl.when(pl.program_id(0) == 0)
        def _():
            cr_ref[...] = jnp.zeros_like(cr_ref)
            ci_ref[...] = jnp.zeros_like(ci_ref)

        xb = _load_perm(x_refs).astype(BF)
        for j in range(NBLK):
            cols = slice(j * LN, (j + 1) * LN)
            hr_ref[:, cols] = _dot(xb[:, j * UB:(j + 1) * UB], bbr_ref[j])
            hi_ref[:, cols] = _dot(xb[:, j * UB:(j + 1) * UB], bbi_ref[j])
        _scan_perm(hr_ref, hi_ref, er_ref, ei_ref, cr_ref, ci_ref, pr_ref, pi_ref, reverse, hpr_ref, hpi_ref)
        y = jnp.concatenate(
            [_dot(hr_ref[:, j * LN:(j + 1) * LN].astype(BF), ctr_ref[j])
             - _dot(hi_ref[:, j * LN:(j + 1) * LN].astype(BF), cti_ref[j]) for j in range(NBLK)], axis=1)
        _store_perm(y_ref, y)

    state = lambda: pl.BlockSpec((TM, SW), lambda i: (order(i), 0))
    xspec = lambda q: pl.BlockSpec((TM, 128), lambda i: (order(i), XS_BLK + q))
    return _call(body, name=name, grid=(nt,),
                 in_specs=[xspec(q) for q in range(nq)]
                 + [_full((NBLK, UB, LN)), _full((NBLK, UB, LN)), _full((NBLK, LN, UB)), _full((NBLK, LN, UB)),
                    _full((PW_ROWS, SW)), _full((PW_ROWS, SW))],
                 out_specs=[pl.BlockSpec((nq, TM, 128), lambda i: (0, order(i), 0)), state(), state()],
                 out_shape=[jax.ShapeDtypeStruct((nq, t, 128), F32), jax.ShapeDtypeStruct((t, SW), F32),
                            jax.ShapeDtypeStruct((t, SW), F32)],
                 scratch=[pltpu.VMEM((TM, SW), F32), pltpu.VMEM((TM, SW), F32),
                          pltpu.VMEM((SCAN_G, SW), F32), pltpu.VMEM((SCAN_G, SW), F32),
                          pltpu.VMEM((1, SW), F32), pltpu.VMEM((1, SW), F32)])(
        *([p] * nq), bb_re, bb_im, ct_re, ct_im, pw_re, pw_im)


def s5p_bwd(p, hp_re, hp_im, dy, bb_re, bb_im, ct_re, ct_im, pw_re, pw_im_conj, a_re, a_im, reverse, name):
    t = p.shape[0]
    nt = t // TM
    order = _tile_order("bwd_adj" if reverse else "fwd_adj", nt)
    nq = AW // 128

    def body(*refs):
        x_refs, dy_refs = refs[:nq], refs[nq:2 * nq]
        (hpr_ref, hpi_ref, bbr_ref, bbi_ref, ctr_ref, cti_ref, pr_ref, pi_ref, ar_ref, ai_ref,
         dx_ref, dar_ref, dai_ref, dbbr_ref, dbbi_ref, dcr_ref, dci_ref,
         gr_ref, gi_ref, er_ref, ei_ref, cr_ref, ci_ref) = refs[2 * nq:]

        @pl.when(pl.program_id(0) == 0)
        def _():
            for r in (cr_ref, ci_ref, dar_ref, dai_ref, dbbr_ref, dbbi_ref, dcr_ref, dci_ref):
                r[...] = jnp.zeros_like(r)

        xb = _load_perm(x_refs).astype(BF)
        dyb = _load_perm(dy_refs).astype(BF)
        for j in range(NBLK):
            cols = slice(j * LN, (j + 1) * LN)
            gr_ref[:, cols] = _dot(dyb[:, j * UB:(j + 1) * UB], ctr_ref[j], 1, 1)
            gi_ref[:, cols] = -_dot(dyb[:, j * UB:(j + 1) * UB], cti_ref[j], 1, 1)
        _scan_perm(gr_ref, gi_ref, er_ref, ei_ref, cr_ref, ci_ref, pr_ref, pi_ref, not reverse)
        dxs = []
        for j in range(NBLK):
            cols = slice(j * LN, (j + 1) * LN)
            xj = xb[:, j * UB:(j + 1) * UB]
            dyj = dyb[:, j * UB:(j + 1) * UB]
            hpr, hpi = hpr_ref[:, cols], hpi_ref[:, cols]
            gr, gi = gr_ref[:, cols], gi_ref[:, cols]
            ar, ai = ar_ref[:, cols], ai_ref[:, cols]
            hr = ar * hpr - ai * hpi + _dot(xj, bbr_ref[j])
            hi = ar * hpi + ai * hpr + _dot(xj, bbi_ref[j])
            dar_ref[:, cols] += jnp.sum(gr * hpr + gi * hpi, axis=0, keepdims=True)
            dai_ref[:, cols] += jnp.sum(gi * hpr - gr * hpi, axis=0, keepdims=True)
            grb, gib = gr.astype(BF), gi.astype(BF)
            dcr_ref[j] += _dot(dyj, hr.astype(BF), 0, 0)
            dci_ref[j] += -_dot(dyj, hi.astype(BF), 0, 0)
            dbbr_ref[j] += _dot(xj, grb, 0, 0)
            dbbi_ref[j] += _dot(xj, gib, 0, 0)
            dxs.append(_dot(grb, bbr_ref[j], 1, 1) + _dot(gib, bbi_ref[j], 1, 1))
        _store_perm(dx_ref, jnp.concatenate(dxs, axis=1))

    state = lambda: pl.BlockSpec((TM, SW), lambda i: (order(i), 0))
    blockd = lambda: _full((NBLK, UB, LN))
    xspec = lambda q: pl.BlockSpec((TM, 128), lambda i: (order(i), XS_BLK + q))
    dyspec = lambda q: pl.BlockSpec((TM, 128), lambda i: (order(i), q))
    return _call(body, name=name, grid=(nt,),
                 in_specs=[xspec(q) for q in range(nq)] + [dyspec(q) for q in range(nq)]
                 + [state(), state(), blockd(), blockd(), _full((NBLK, LN, UB)), _full((NBLK, LN, UB)),
                    _full((PW_ROWS, SW)), _full((PW_ROWS, SW)), _full((1, SW)), _full((1, SW))],
                 out_specs=[pl.BlockSpec((nq, TM, 128), lambda i: (0, order(i), 0)), _full((1, SW)), _full((1, SW)),
                            blockd(), blockd(), blockd(), blockd()],
                 out_shape=[jax.ShapeDtypeStruct((nq, t, 128), F32), jax.ShapeDtypeStruct((1, SW), F32),
                            jax.ShapeDtypeStruct((1, SW), F32)] + [jax.ShapeDtypeStruct((NBLK, UB, LN), F32)] * 4,
                 scratch=[pltpu.VMEM((TM, SW), F32), pltpu.VMEM((TM, SW), F32),
                          pltpu.VMEM((SCAN_G, SW), F32), pltpu.VMEM((SCAN_G, SW), F32),
                          pltpu.VMEM((1, SW), F32), pltpu.VMEM((1, SW), F32)])(
        *([p] * nq), *([dy] * nq), hp_re, hp_im, bb_re, bb_im, ct_re, ct_im, pw_re, pw_im_conj, a_re, a_im)


def _scan2(br_ref, bi_ref, or_ref, oi_ref, h_off, cin_off, er_ref, ei_ref, cr_ref, ci_ref, pr_ref, pi_ref, reverse):
    gpt = SCAN_G
    nr = SCAN_R
    offsets = list(range(nr))[::-1] if reverse else list(range(nr))
    blocks = [slice(b * LN, (b + 1) * LN) for b in range(NBLK)]
    slab = lambda r: slice(r * gpt, (r + 1) * gpt)
    a1 = [(pr_ref[0:1, c], pi_ref[0:1, c]) for c in blocks]
    x = [None] * NBLK
    for r in offsets:
        for b, c in enumerate(blocks):
            if x[b] is None:
                x[b] = (br_ref[slab(r), c], bi_ref[slab(r), c])
            else:
                (ar, ai), (xr, xi) = a1[b], x[b]
                x[b] = (br_ref[slab(r), c] + ar * xr - ai * xi, bi_ref[slab(r), c] + ar * xi + ai * xr)
    an = [(pr_ref[nr - 1:nr, c], pi_ref[nr - 1:nr, c]) for c in blocks]
    k = [(cr_ref[:, c], ci_ref[:, c]) for c in blocks]
    for g in (range(gpt - 1, -1, -1) if reverse else range(gpt)):
        for b, c in enumerate(blocks):
            (ar, ai), (kr, ki), (xr, xi) = an[b], k[b], x[b]
            er_ref[g:g + 1, c] = kr
            ei_ref[g:g + 1, c] = ki
            k[b] = (xr[g:g + 1, :] + ar * kr - ai * ki, xi[g:g + 1, :] + ar * ki + ai * kr)
    for b, c in enumerate(blocks):
        cr_ref[:, c] = k[b][0]
        ci_ref[:, c] = k[b][1]
        x[b] = (er_ref[:, c], ei_ref[:, c])
        if cin_off is not None:
            or_ref[cin_off:cin_off + gpt, c] = x[b][0]
            oi_ref[cin_off:cin_off + gpt, c] = x[b][1]
    for r in offsets:
        for b, c in enumerate(blocks):
            (ar, ai), (xr, xi) = a1[b], x[b]
            x[b] = (br_ref[slab(r), c] + ar * xr - ai * xi, bi_ref[slab(r), c] + ar * xi + ai * xr)
            or_ref[h_off + r * gpt:h_off + (r + 1) * gpt, c] = x[b][0]
            oi_ref[h_off + r * gpt:h_off + (r + 1) * gpt, c] = x[b][1]


HS_ROWS = TM + SCAN_G


def _hs_offsets(reverse):
    return (0, SCAN_G) if reverse else (SCAN_G, 0)


def s5q_fwd(p, bb_re, bb_im, ct_re, ct_im, pw_re, pw_im, reverse, name):
    t = p.shape[0]
    nt = t // TM
    order = _tile_order("bwd" if reverse else "fwd", nt)
    nq = AW // 128
    h_off, p_off = _hs_offsets(reverse)

    def body(*refs):
        x_refs = refs[:nq]
        bbr_ref, bbi_ref, ctr_ref, cti_ref, pr_ref, pi_ref, y_ref, hsr_ref, hsi_ref = refs[nq:nq + 9]
        br_ref, bi_ref, er_ref, ei_ref, cr_ref, ci_ref = refs[nq + 9:]

        @pl.when(pl.program_id(0) == 0)
        def _():
            cr_ref[...] = jnp.zeros_like(cr_ref)
            ci_ref[...] = jnp.zeros_like(ci_ref)

        xb = _load_perm(x_refs).astype(BF)
        for j in range(NBLK):
            cols = slice(j * LN, (j + 1) * LN)
            br_ref[:, cols] = _dot(xb[:, j * UB:(j + 1) * UB], bbr_ref[j])
            bi_ref[:, cols] = _dot(xb[:, j * UB:(j + 1) * UB], bbi_ref[j])
        _scan2(br_ref, bi_ref, hsr_ref, hsi_ref, h_off, TM if reverse else 0, er_ref, ei_ref, cr_ref, ci_ref,
               pr_ref, pi_ref, reverse)
        y = jnp.concatenate(
            [_dot(hsr_ref[h_off:h_off + TM, j * LN:(j + 1) * LN].astype(BF), ctr_ref[j])
             - _dot(hsi_ref[h_off:h_off + TM, j * LN:(j + 1) * LN].astype(BF), cti_ref[j]) for j in range(NBLK)], axis=1)
        _store_perm(y_ref, y)

    state = lambda: pl.BlockSpec((HS_ROWS, SW), lambda i: (order(i), 0))
    xspec = lambda q: pl.BlockSpec((TM, 128), lambda i: (order(i), XS_BLK + q))
    return _call(body, name=name, grid=(nt,),
                 in_specs=[xspec(q) for q in range(nq)]
                 + [_full((NBLK, UB, LN)), _full((NBLK, UB, LN)), _full((NBLK, LN, UB)), _full((NBLK, LN, UB)),
                    _full((PW_ROWS, SW)), _full((PW_ROWS, SW))],
                 out_specs=[pl.BlockSpec((nq, TM, 128), lambda i: (0, order(i), 0)), state(), state()],
                 out_shape=[jax.ShapeDtypeStruct((nq, t, 128), F32), jax.ShapeDtypeStruct((nt * HS_ROWS, SW), F32),
                            jax.ShapeDtypeStruct((nt * HS_ROWS, SW), F32)],
                 scratch=[pltpu.VMEM((TM, SW), F32), pltpu.VMEM((TM, SW), F32),
                          pltpu.VMEM((SCAN_G, SW), F32), pltpu.VMEM((SCAN_G, SW), F32),
                          pltpu.VMEM((1, SW), F32), pltpu.VMEM((1, SW), F32)])(
        *([p] * nq), bb_re, bb_im, ct_re, ct_im, pw_re, pw_im)


def s5q_bwd(p, hs_re, hs_im, dy, bb_re, bb_im, ct_re, ct_im, pw_re, pw_im_conj, reverse, name):
    t = p.shape[0]
    nt = t // TM
    order = _tile_order("bwd_adj" if reverse else "fwd_adj", nt)
    nq = AW // 128
    h_off, p_off = _hs_offsets(reverse)

    def body(*refs):
        x_refs, dy_refs = refs[:nq], refs[nq:2 * nq]
        (hsr_ref, hsi_ref, bbr_ref, bbi_ref, ctr_ref, cti_ref, pr_ref, pi_ref,
         dx_ref, dar_ref, dai_ref, dbbr_ref, dbbi_ref, dcr_ref, dci_ref,
         qr_ref, qi_ref, gr_ref, gi_ref, er_ref, ei_ref, cr_ref, ci_ref) = refs[2 * nq:]

        @pl.when(pl.program_id(0) == 0)
        def _():
            for r in (cr_ref, ci_ref, dar_ref, dai_ref, dbbr_ref, dbbi_ref, dcr_ref, dci_ref):
                r[...] = jnp.zeros_like(r)

        xb = _load_perm(x_refs).astype(BF)
        dyb = _load_perm(dy_refs).astype(BF)
        for j in range(NBLK):
            cols = slice(j * LN, (j + 1) * LN)
            qr_ref[:, cols] = _dot(dyb[:, j * UB:(j + 1) * UB], ctr_ref[j], 1, 1)
            qi_ref[:, cols] = -_dot(dyb[:, j * UB:(j + 1) * UB], cti_ref[j], 1, 1)
        _scan2(qr_ref, qi_ref, gr_ref, gi_ref, 0, None, er_ref, ei_ref, cr_ref, ci_ref, pr_ref, pi_ref, not reverse)
        dxs = []
        for j in range(NBLK):
            cols = slice(j * LN, (j + 1) * LN)
            xj = xb[:, j * UB:(j + 1) * UB]
            dyj = dyb[:, j * UB:(j + 1) * UB]
            hpr, hpi = hsr_ref[p_off:p_off + TM, cols], hsi_ref[p_off:p_off + TM, cols]
            gr, gi = gr_ref[:, cols], gi_ref[:, cols]
            dar_ref[:, cols] += jnp.sum(gr * hpr + gi * hpi, axis=0, keepdims=True)
            dai_ref[:, cols] += jnp.sum(gi * hpr - gr * hpi, axis=0, keepdims=True)
            grb, gib = gr.astype(BF), gi.astype(BF)
            dcr_ref[j] += _dot(dyj, hsr_ref[h_off:h_off + TM, cols].astype(BF), 0, 0)
            dci_ref[j] += -_dot(dyj, hsi_ref[h_off:h_off + TM, cols].astype(BF), 0, 0)
            dbbr_ref[j] += _dot(xj, grb, 0, 0)
            dbbi_ref[j] += _dot(xj, gib, 0, 0)
            dxs.append(_dot(grb, bbr_ref[j], 1, 1) + _dot(gib, bbi_ref[j], 1, 1))
        _store_perm(dx_ref, jnp.concatenate(dxs, axis=1))

    state = lambda: pl.BlockSpec((HS_ROWS, SW), lambda i: (order(i), 0))
    blockd = lambda: _full((NBLK, UB, LN))
    xspec = lambda q: pl.BlockSpec((TM, 128), lambda i: (order(i), XS_BLK + q))
    dyspec = lambda q: pl.BlockSpec((TM, 128), lambda i: (order(i), q))
    return _call(body, name=name, grid=(nt,),
                 in_specs=[xspec(q) for q in range(nq)] + [dyspec(q) for q in range(nq)]
                 + [state(), state(), blockd(), blockd(), _full((NBLK, LN, UB)), _full((NBLK, LN, UB)),
                    _full((PW_ROWS, SW)), _full((PW_ROWS, SW))],
                 out_specs=[pl.BlockSpec((nq, TM, 128), lambda i: (0, order(i), 0)), _full((1, SW)), _full((1, SW)),
                            blockd(), blockd(), blockd(), blockd()],
                 out_shape=[jax.ShapeDtypeStruct((nq, t, 128), F32), jax.ShapeDtypeStruct((1, SW), F32),
                            jax.ShapeDtypeStruct((1, SW), F32)] + [jax.ShapeDtypeStruct((NBLK, UB, LN), F32)] * 4,
                 scratch=[pltpu.VMEM((TM, SW), F32), pltpu.VMEM((TM, SW), F32),
                          pltpu.VMEM((TM, SW), F32), pltpu.VMEM((TM, SW), F32),
                          pltpu.VMEM((SCAN_G, SW), F32), pltpu.VMEM((SCAN_G, SW), F32),
                          pltpu.VMEM((1, SW), F32), pltpu.VMEM((1, SW), F32)])(
        *([p] * nq), *([dy] * nq), hs_re, hs_im, bb_re, bb_im, ct_re, ct_im, pw_re, pw_im_conj)


def s5p_dx_sum(dy, dsk, dxf, dxb, dp, name):
    t = dy.shape[0]
    nq = AW // 128

    def body(dy_ref, d_ref, f_ref, b_ref, dp_ref, o_ref):
        o_ref[...] = (dy_ref[...] * d_ref[...] + _cat_lanes(f_ref) + _cat_lanes(b_ref)).astype(BF)

    tile = pl.BlockSpec((TM, AW), lambda i: (i, 0))
    blk4 = pl.BlockSpec((nq, TM, 128), lambda i: (0, i, 0))
    return pl.pallas_call(
        body, name=name, grid=(t // TM,),
        in_specs=[tile, _full((1, AW)), blk4, blk4, pl.BlockSpec(memory_space=pl.ANY)],
        out_specs=pl.BlockSpec((TM, AW), lambda i: (i, 3)), out_shape=jax.ShapeDtypeStruct(dp.shape, dp.dtype),
        input_output_aliases={4: 0},
        compiler_params=pltpu.CompilerParams(vmem_limit_bytes=VMEM_LIMIT_BYTES))(dy, dsk, dxf, dxb, dp)


SCALE = HD ** -0.5
NHEAD_NORM = NQ + NKV


def _partner(x):
    half0 = (lax.broadcasted_iota(jnp.int32, (1, HD), 1) % 64) < 32
    return jnp.where(half0, pltpu.roll(x, HD - 32, 1), pltpu.roll(x, 32, 1))


def attn_prep(p, qg, kg, cos, sins, name):
    t = p.shape[0]

    def body(p_ref, qg_ref, kg_ref, cos_ref, sin_ref, o_ref):
        cv, sv = cos_ref[...], sin_ref[...]
        for h in range(NHEAD_NORM):
            cols = slice(h * HD, (h + 1) * HD)
            blk = p_ref[:, cols]
            r = lax.rsqrt(jnp.mean(blk * blk, axis=-1, keepdims=True) + EPS)
            xn = blk * r * (qg_ref[...] if h < NQ else kg_ref[...])
            rot = xn * cv + _partner(xn) * sv
            o_ref[:, cols] = ((rot * SCALE) if h < NQ else rot).astype(BF)
        vcols = slice(NHEAD_NORM * HD, (NHEAD_NORM + NKV) * HD)
        o_ref[:, vcols] = p_ref[:, vcols].astype(BF)

    w = (NHEAD_NORM + NKV) * HD
    tile = lambda ww: pl.BlockSpec((TM, ww), lambda i: (i, 0))
    return _call(body, name=name, grid=(t // TM,),
                 in_specs=[tile(w), _full((1, HD)), _full((1, HD)), tile(HD), tile(HD)],
                 out_specs=tile(w), out_shape=jax.ShapeDtypeStruct((t, w), BF))(p, qg, kg, cos, sins)


def attn_prep_bwd(p, dq, dk, dv, qg, kg, cos, sins, dp, name):
    t = p.shape[0]

    def body(p_ref, dq_ref, dk_ref, dv_ref, qg_ref, kg_ref, cos_ref, sin_ref, dp_ref, o_ref, dqg_ref, dkg_ref):
        @pl.when(pl.program_id(0) == 0)
        def _():
            dqg_ref[...] = jnp.zeros_like(dqg_ref)
            dkg_ref[...] = jnp.zeros_like(dkg_ref)

        cv, sv = cos_ref[...], sin_ref[...]
        for h in range(NHEAD_NORM):
            cols = slice(h * HD, (h + 1) * HD)
            blk = p_ref[:, cols]
            r = lax.rsqrt(jnp.mean(blk * blk, axis=-1, keepdims=True) + EPS)
            xh = blk * r
            if h < NQ:
                drot = dq_ref[:, cols] * SCALE
                gv, dg_ref = qg_ref[...], dqg_ref
            else:
                drot = dk_ref[:, (h - NQ) * HD:(h - NQ + 1) * HD]
                gv, dg_ref = kg_ref[...], dkg_ref
            dxn = drot * cv + _partner(drot * sv)
            dg_ref[...] += jnp.sum(dxn * xh, axis=0, keepdims=True)
            dxh = dxn * gv
            o_ref[:, cols] = (r * (dxh - xh * jnp.mean(dxh * xh, axis=-1, keepdims=True))).astype(BF)
        o_ref[:, NHEAD_NORM * HD:(NHEAD_NORM + NKV) * HD] = dv_ref[...].astype(BF)

    w = (NHEAD_NORM + NKV) * HD
    tile = lambda ww: pl.BlockSpec((TM, ww), lambda i: (i, 0))
    return pl.pallas_call(
        body, name=name, grid=(t // TM,),
        in_specs=[tile(w), tile(NQ * HD), tile(NKV * HD), tile(NKV * HD), _full((1, HD)), _full((1, HD)),
                  tile(HD), tile(HD), pl.BlockSpec(memory_space=pl.ANY)],
        out_specs=[tile(w), _full((1, HD)), _full((1, HD))],
        out_shape=[jax.ShapeDtypeStruct(dp.shape, dp.dtype), jax.ShapeDtypeStruct((1, HD), F32),
                   jax.ShapeDtypeStruct((1, HD), F32)],
        input_output_aliases={8: 0},
        compiler_params=pltpu.CompilerParams(vmem_limit_bytes=VMEM_LIMIT_BYTES))(p, dq, dk, dv, qg, kg, cos, sins, dp)


KCOL = NQ
VCOL = NQ + NKV
GCOL = (NQ + 2 * NKV)
QPK = NQ // NKV
ATT_KCHUNK = 512


def attn_fwd(qkv, p, name):
    t = qkv.shape[0]

    def body(q_ref, k_ref, v_ref, g_ref, o_ref, mix_ref, lse_ref):
        def attend(nk):
            q = q_ref[...]
            chunks = [(k0, min(k0 + 2 * ATT_KCHUNK, nk)) for k0 in range(0, nk, 2 * ATT_KCHUNK)]
            s_next = _dot(q, k_ref[chunks[0][0]:chunks[0][1], :], 1, 1)
            m = l = acc = None
            for ci, (k0, k1) in enumerate(chunks):
                s = s_next
                if ci + 1 < len(chunks):
                    s_next = _dot(q, k_ref[chunks[ci + 1][0]:chunks[ci + 1][1], :], 1, 1)
                mc = jnp.max(s, axis=-1, keepdims=True)
                m_new = mc if m is None else jnp.maximum(m, mc)
                pe = jnp.exp(s - m_new)
                lc = jnp.sum(pe, axis=-1, keepdims=True)
                pv = _dot(pe.astype(BF), v_ref[k0:k1, :])
                if m is None:
                    l, acc = lc, pv
                else:
                    alpha = jnp.exp(m - m_new)
                    l, acc = alpha * l + lc, alpha * acc + pv
                m = m_new
            o = acc / l
            gt = g_ref[...]
            o_ref[...] = o
            mix_ref[...] = (o * (gt * _sig(gt))).astype(BF)
            lse_ref[...] = jnp.broadcast_to(m + jnp.log(l), (TM, HD))

        pl.when(pl.program_id(1) == 0)(lambda: attend(NC))
        pl.when(pl.program_id(1) > 0)(lambda: attend(t))

    blk = pl.BlockSpec((TM, HD), lambda h, i: (i, h))
    return _call(body, name=name, grid=(NQ, t // TM),
                 in_specs=[blk, pl.BlockSpec((t, HD), lambda h, i: (0, KCOL + h // QPK)),
                           pl.BlockSpec((t, HD), lambda h, i: (0, VCOL + h // QPK)),
                           pl.BlockSpec((TM, HD), lambda h, i: (i, GCOL + h))],
                 out_specs=[blk, blk, blk],
                 out_shape=[jax.ShapeDtypeStruct((t, NQ * HD), F32), jax.ShapeDtypeStruct((t, NQ * HD), BF),
                            jax.ShapeDtypeStruct((t, NQ * HD), F32)])(qkv, qkv, qkv, p)


def attn_bwd(qkv, p, dmix, o, lse, name):
    t = qkv.shape[0]

    def body(q_ref, k_ref, v_ref, g_ref, dm_ref, o_ref, lse_ref, dq_ref, dg_ref, dk_ref, dv_ref):
        i = pl.program_id(2)

        @pl.when((pl.program_id(1) == 0) & (i == 0))
        def _():
            dk_ref[...] = jnp.zeros_like(dk_ref)
            dv_ref[...] = jnp.zeros_like(dv_ref)

        gt = g_ref[...]
        sg = _sig(gt)
        ov = o_ref[...]
        dmv = dm_ref[...]
        dg_ref[...] = (dmv * ov * (sg * (1.0 + gt * (1.0 - sg)))).astype(BF)
        do = dmv * (gt * sg)
        dr = jnp.sum(do * ov, axis=-1, keepdims=True)
        dob = do.astype(BF)

        def bwd(nk):
            q = q_ref[...]
            lse = lse_ref[:, 0:1]
            chunks = [slice(k0, min(k0 + ATT_KCHUNK, nk)) for k0 in range(0, nk, ATT_KCHUNK)]
            nxt = (_dot(q, k_ref[chunks[0], :], 1, 1), _dot(dob, v_ref[chunks[0], :], 1, 1))
            dq = None
            for ci, keys in enumerate(chunks):
                s, dp = nxt
                if ci + 1 < len(chunks):
                    nxt = (_dot(q, k_ref[chunks[ci + 1], :], 1, 1), _dot(dob, v_ref[chunks[ci + 1], :], 1, 1))
                pe = jnp.exp(s - lse)
                dsb = (pe * (dp - dr)).astype(BF)
                part = _dot(dsb, k_ref[keys, :])
                dq = part if dq is None else dq + part
                dv_ref[keys, :] += _dot(pe.astype(BF), dob, 0, 0)
                dk_ref[keys, :] += _dot(dsb, q, 0, 0)
            dq_ref[...] = dq

        pl.when(i == 0)(lambda: bwd(NC))
        pl.when(i > 0)(lambda: bwd(t))

    blk = pl.BlockSpec((TM, HD), lambda kv, g, i: (i, kv * QPK + g))
    acc = pl.BlockSpec((t, HD), lambda kv, g, i: (0, kv))
    return _call(body, name=name, grid=(NKV, QPK, t // TM),
                 in_specs=[blk, pl.BlockSpec((t, HD), lambda kv, g, i: (0, KCOL + kv)),
                           pl.BlockSpec((t, HD), lambda kv, g, i: (0, VCOL + kv)),
                           pl.BlockSpec((TM, HD), lambda kv, g, i: (i, GCOL + kv * QPK + g)), blk, blk, blk],
                 out_specs=[blk, pl.BlockSpec((TM, HD), lambda kv, g, i: (i, GCOL + kv * QPK + g)), acc, acc],
                 out_shape=[jax.ShapeDtypeStruct((t, NQ * HD), F32), jax.ShapeDtypeStruct((t, (GCOL + NQ) * HD), BF),
                            jax.ShapeDtypeStruct((t, NKV * HD), F32), jax.ShapeDtypeStruct((t, NKV * HD), F32)])(
        qkv, qkv, qkv, p, dmix, o, lse)


def _row_tile(rows, row_bytes, cap=2 * 1024 * 1024):
    if rows * row_bytes <= cap or rows % 8:
        return rows
    tr = rows
    while tr * row_bytes > cap and tr % 16 == 0:
        tr //= 2
    return tr


def adamw(w, g, m, v, name):
    r, cdim = w.shape
    tr = _row_tile(r, 4 * max(cdim, 128))

    def body(w_ref, g_ref, m_ref, v_ref, d_ref, nm_ref, nv_ref):
        gv = g_ref[...]
        m2 = ADAM_B1 * m_ref[...] + (1.0 - ADAM_B1) * gv
        v2 = ADAM_B2 * v_ref[...] + (1.0 - ADAM_B2) * (gv * gv)
        mh = m2 / (1.0 - ADAM_B1 ** ADAM_STEP)
        vh = v2 / (1.0 - ADAM_B2 ** ADAM_STEP)
        d_ref[...] = -ADAM_LR * (mh / (jnp.sqrt(vh) + ADAM_EPS) + ADAM_WD * w_ref[...])
        nm_ref[...] = m2
        nv_ref[...] = v2

    tile = pl.BlockSpec((tr, cdim), lambda i: (i, 0))
    sh = jax.ShapeDtypeStruct((r, cdim), F32)
    return _call(body, name=name, grid=(r // tr,), in_specs=[tile] * 4, out_specs=[tile] * 3,
                 out_shape=[sh, sh, sh])(w, g, m, v)


def sum_lead(a, name, out_dtype=F32):
    n, r, cdim = a.shape
    tr = _row_tile(r, 4 * n * max(cdim, 128))

    def body(a_ref, o_ref):
        acc = a_ref[0].astype(F32)
        for k in range(1, n):
            acc = acc + a_ref[k].astype(F32)
        o_ref[...] = acc.astype(o_ref.dtype)

    return _call(body, name=name, grid=(r // tr,),
                 in_specs=[pl.BlockSpec((n, tr, cdim), lambda i: (0, i, 0))],
                 out_specs=pl.BlockSpec((tr, cdim), lambda i: (i, 0)),
                 out_shape=jax.ShapeDtypeStruct((r, cdim), out_dtype))(a)


_FLIPS = {"xy": [(1, 0, 0), (0, 1, 0), (1, 1, 0)], "c": [(0, 0, 1)],
          "all": [(0, 0, 1), (0, 1, 0), (0, 1, 1), (1, 0, 0), (1, 0, 1), (1, 1, 0), (1, 1, 1)]}
_GROUP_SIZE = {"xy": 4, "c": 2, "all": 8}


def _group_index(group, x, y, c):
    return {"xy": 2 * x + y, "c": c, "all": 4 * x + 2 * y + c}[group]


def exchange(items, name):
    plan = []
    for arr, group, kind in items:
        chunk = arr.shape if kind == "gather" else arr.shape[1:]
        plan.append((group, kind, chunk))
    ncopy = sum(len(_FLIPS[g]) for g, _, _ in plan)
    nitem = len(plan)

    def body(*refs):
        srcs, dsts = refs[:nitem], refs[nitem:2 * nitem]
        send_sems, recv_sems, local_sems = refs[2 * nitem:]
        x, y, c = lax.axis_index("x"), lax.axis_index("y"), lax.axis_index("c")
        sends, recvs, locals_ = [], [], []
        n = 0
        for k, (group, kind, _) in enumerate(plan):
            me = _group_index(group, x, y, c)
            own = srcs[k] if kind == "gather" else srcs[k].at[me]
            locals_.append(pltpu.make_async_copy(own, dsts[k].at[me], local_sems.at[k]))
            for fx, fy, fc in _FLIPS[group]:
                px, py, pc = (1 - x if fx else x), (1 - y if fy else y), (1 - c if fc else c)
                peer = _group_index(group, px, py, pc)
                src = srcs[k] if kind == "gather" else srcs[k].at[peer]
                sends.append(pltpu.make_async_remote_copy(
                    src_ref=src, dst_ref=dsts[k].at[me], send_sem=send_sems.at[n], recv_sem=recv_sems.at[n],
                    device_id=(px, py, pc), device_id_type=MESH))
                recvs.append(pltpu.make_async_remote_copy(
                    src_ref=src, dst_ref=dsts[k].at[peer], send_sem=send_sems.at[n], recv_sem=recv_sems.at[n],
                    device_id=(px, py, pc), device_id_type=MESH))
                n += 1
        for cp in locals_ + sends:
            cp.start()
        for cp in recvs:
            cp.wait_recv()
        for cp in sends:
            cp.wait_send()
        for cp in locals_:
            cp.wait()

    anyspec = pl.BlockSpec(memory_space=pl.ANY)
    outs = [jax.ShapeDtypeStruct((_GROUP_SIZE[g],) + tuple(chunk), arr.dtype)
            for (arr, _, _), (g, _, chunk) in zip(items, plan)]
    res = pl.pallas_call(
        body, name=name, out_shape=outs, in_specs=[anyspec] * nitem, out_specs=[anyspec] * nitem,
        scratch_shapes=[pltpu.SemaphoreType.DMA((ncopy,)), pltpu.SemaphoreType.DMA((ncopy,)),
                        pltpu.SemaphoreType.DMA((nitem,))],
        compiler_params=pltpu.CompilerParams(has_side_effects=True))(*[a for a, _, _ in items])
    return list(res)


D2D_PIECES = 4


def d2d(items, name):
    n = len(items)
    swaps = [k for k, (_, kind) in enumerate(items) if kind == "swap"]

    def pieces_of(rows):
        npc = D2D_PIECES if rows % (8 * D2D_PIECES) == 0 else 1
        return npc, rows // npc

    ncopy = sum(pieces_of(a.shape[0] if kind == "gather" else a.shape[1])[0] for a, kind in items)

    def body(*refs):
        srcs, outs = refs[:n], refs[n:2 * n]
        stages = dict(zip(swaps, refs[2 * n:2 * n + len(swaps)]))
        send_sems, recv_sems, local_sems = refs[2 * n + len(swaps):]
        x, y, c = lax.axis_index("x"), lax.axis_index("y"), lax.axis_index("c")
        sib = (x, y, 1 - c)

        def remote(src, dst, q):
            return pltpu.make_async_remote_copy(src_ref=src, dst_ref=dst, send_sem=send_sems.at[q],
                                                recv_sem=recv_sems.at[q], device_id=sib, device_id_type=MESH)

        copies = []
        q = 0
        for k, (arr, kind) in enumerate(items):
            npc, pr = pieces_of(arr.shape[0] if kind == "gather" else arr.shape[1])
            for pc in range(npc):
                rs = pl.ds(pc * pr, pr)
                if kind == "gather":
                    mine, theirs = outs[k].at[c, rs], outs[k].at[1 - c, rs]
                    copies.append((pltpu.make_async_copy(srcs[k].at[rs], mine, local_sems.at[q]),
                                   remote(mine, mine, q), remote(theirs, theirs, q)))
                else:
                    stage, land = stages[k].at[rs], outs[k].at[rs]
                    copies.append((pltpu.make_async_copy(srcs[k].at[1 - c, rs], stage, local_sems.at[q]),
                                   remote(stage, land, q), remote(stage, land, q)))
                q += 1
        for loc, _, _ in copies:
            loc.start()
        for loc, send, _ in copies:
            loc.wait()
            send.start()
        for _, _, recv in copies:
            recv.wait_recv()
        for _, send, _ in copies:
            send.wait_send()

    outs = [jax.ShapeDtypeStruct((2,) + a.shape if kind == "gather" else a.shape[1:], a.dtype) for a, kind in items]
    res = pl.pallas_call(
        body, name=name, out_shape=outs, in_specs=[pl.BlockSpec(memory_space=pl.ANY)] * n,
        out_specs=[pl.BlockSpec(memory_space=pltpu.VMEM)] * n,
        scratch_shapes=[pltpu.VMEM(items[k][0].shape[1:], items[k][0].dtype) for k in swaps]
        + [pltpu.SemaphoreType.DMA((ncopy,)), pltpu.SemaphoreType.DMA((ncopy,)), pltpu.SemaphoreType.DMA((ncopy,))],
        compiler_params=pltpu.CompilerParams(has_side_effects=True, vmem_limit_bytes=VMEM_LIMIT_BYTES))(
        *[a for a, _ in items])
    return list(res)


def sum_own(pair, got, name, out_dtype=F32):
    _, r, cdim = pair.shape
    tr = _row_tile(r, 4 * 2 * max(cdim, 128))

    def body(c_ref, p_ref, g_ref, o_ref):
        o_ref[...] = (p_ref[0] + g_ref[...]).astype(o_ref.dtype)

    me = lax.axis_index("c").astype(jnp.int32).reshape(1)
    return pl.pallas_call(
        body, name=name, out_shape=jax.ShapeDtypeStruct((r, cdim), out_dtype),
        grid_spec=pltpu.PrefetchScalarGridSpec(
            num_scalar_prefetch=1, grid=(r // tr,),
            in_specs=[pl.BlockSpec((1, tr, cdim), lambda i, c_ref: (c_ref[0], i, 0)),
                      pl.BlockSpec((tr, cdim), lambda i, c_ref: (i, 0))],
            out_specs=pl.BlockSpec((tr, cdim), lambda i, c_ref: (i, 0))),
        compiler_params=pltpu.CompilerParams(vmem_limit_bytes=VMEM_LIMIT_BYTES))(me, pair, got)


_SMALL = ["c_ctx", "norm_g", "b_mod", "gm_v_g", "gm_w_s", "gm_b_s", "s5_lam_re", "s5_lam_im", "s5_log_dt",
          "s5_b_re", "s5_b_im", "s5_c_re", "s5_c_im", "s5_d", "s5_b_glu", "q_norm_g", "k_norm_g", "final_g"]
_BIG = ["we_in", "we_out", "s5_w_glu", "wo_in", "wo_out"]
_WEIGHTS = ["c_ctx", "norm_g", "w_mod", "b_mod", "we_in", "we_out", "gm_v_g", "gm_w_s", "gm_b_s", "s5_lam_re",
            "s5_lam_im", "s5_log_dt", "s5_b_re", "s5_b_im", "s5_c_re", "s5_c_im", "s5_d", "s5_w_glu", "s5_b_glu",
            "wo_in", "wo_out", "q_norm_g", "k_norm_g", "final_g"]
_SMALL_ALIGN = 8 * 8 * 128


def _rope_tables(n_lat):
    rows = n_lat // GRID_W
    row = jnp.repeat(jnp.arange(rows), GRID_W)
    col = jnp.tile(jnp.arange(GRID_W), rows)
    freqs = ROPE_THETA ** (-jnp.arange(HD // 4, dtype=F32) / (HD // 4))
    ar, ac = row[:, None] * freqs, col[:, None] * freqs
    cos = jnp.concatenate([jnp.cos(ar), jnp.cos(ar), jnp.cos(ac), jnp.cos(ac)], axis=1)
    sins = jnp.concatenate([-jnp.sin(ar), jnp.sin(ar), -jnp.sin(ac), jnp.sin(ac)], axis=1)
    cos = jnp.concatenate([jnp.ones((NC, HD), F32), cos], axis=0)
    sins = jnp.concatenate([jnp.zeros((NC, HD), F32), sins], axis=0)
    return cos, sins


def _block_diag(v, transpose):
    gpb = SG // NBLK
    v = v.reshape(2, NBLK, gpb, SH, SP)
    eye = jnp.eye(gpb, dtype=v.dtype)
    if transpose:
        return jnp.einsum("djahp,ab->djapbh", v, eye).reshape(2, NBLK, LN, UB)
    return jnp.einsum("djahp,ab->djahbp", v, eye).reshape(2, NBLK, UB, LN)


def _diag_blocks(m):
    gpb = SG // NBLK
    return jnp.einsum("jahap->jahp", m.reshape(NBLK, gpb, SH, gpb, SP)).reshape(SG, SH, SP)


def _view2d(a):
    if a.ndim == 1:
        return a.reshape(1, -1)
    if a.shape[-1] < 64 and a.size % 1024 == 0:
        return a.reshape(-1, 1024)
    return a.reshape(-1, a.shape[-1])


def kernel(x, c, ctx, c_ctx, norm_g, w_mod, b_mod, we_in, we_out, gm_v_g, gm_w_s, gm_b_s, s5_lam_re, s5_lam_im, s5_log_dt, s5_b_re, s5_b_im, s5_c_re, s5_c_im, s5_d, s5_w_glu, s5_b_glu, wo_in, wo_out, q_norm_g, k_norm_g, final_g, loss_target, m_c_ctx, m_norm_g, m_w_mod, m_b_mod, m_we_in, m_we_out, m_gm_v_g, m_gm_w_s, m_gm_b_s, m_s5_lam_re, m_s5_lam_im, m_s5_log_dt, m_s5_b_re, m_s5_b_im, m_s5_c_re, m_s5_c_im, m_s5_d, m_s5_w_glu, m_s5_b_glu, m_wo_in, m_wo_out, m_q_norm_g, m_k_norm_g, m_final_g, v_c_ctx, v_norm_g, v_w_mod, v_b_mod, v_we_in, v_we_out, v_gm_v_g, v_gm_w_s, v_gm_b_s, v_s5_lam_re, v_s5_lam_im, v_s5_log_dt, v_s5_b_re, v_s5_b_im, v_s5_c_re, v_s5_c_im, v_s5_d, v_s5_w_glu, v_s5_b_glu, v_wo_in, v_wo_out, v_q_norm_g, v_k_norm_g, v_final_g):
    weights = dict(c_ctx=c_ctx, norm_g=norm_g, w_mod=w_mod, b_mod=b_mod, we_in=we_in, we_out=we_out, gm_v_g=gm_v_g,
                   gm_w_s=gm_w_s, gm_b_s=gm_b_s, s5_lam_re=s5_lam_re, s5_lam_im=s5_lam_im, s5_log_dt=s5_log_dt,
                   s5_b_re=s5_b_re, s5_b_im=s5_b_im, s5_c_re=s5_c_re, s5_c_im=s5_c_im, s5_d=s5_d, s5_w_glu=s5_w_glu,
                   s5_b_glu=s5_b_glu, wo_in=wo_in, wo_out=wo_out, q_norm_g=q_norm_g, k_norm_g=k_norm_g,
                   final_g=final_g)
    mom_m = dict(c_ctx=m_c_ctx, norm_g=m_norm_g, w_mod=m_w_mod, b_mod=m_b_mod, we_in=m_we_in, we_out=m_we_out,
                 gm_v_g=m_gm_v_g, gm_w_s=m_gm_w_s, gm_b_s=m_gm_b_s, s5_lam_re=m_s5_lam_re, s5_lam_im=m_s5_lam_im,
                 s5_log_dt=m_s5_log_dt, s5_b_re=m_s5_b_re, s5_b_im=m_s5_b_im, s5_c_re=m_s5_c_re, s5_c_im=m_s5_c_im,
                 s5_d=m_s5_d, s5_w_glu=m_s5_w_glu, s5_b_glu=m_s5_b_glu, wo_in=m_wo_in, wo_out=m_wo_out,
                 q_norm_g=m_q_norm_g, k_norm_g=m_k_norm_g, final_g=m_final_g)
    mom_v = dict(c_ctx=v_c_ctx, norm_g=v_norm_g, w_mod=v_w_mod, b_mod=v_b_mod, we_in=v_we_in, we_out=v_we_out,
                 gm_v_g=v_gm_v_g, gm_w_s=v_gm_w_s, gm_b_s=v_gm_b_s, s5_lam_re=v_s5_lam_re, s5_lam_im=v_s5_lam_im,
                 s5_log_dt=v_s5_log_dt, s5_b_re=v_s5_b_re, s5_b_im=v_s5_b_im, s5_c_re=v_s5_c_re, s5_c_im=v_s5_c_im,
                 s5_d=v_s5_d, s5_w_glu=v_s5_w_glu, s5_b_glu=v_s5_b_glu, wo_in=v_wo_in, wo_out=v_wo_out,
                 q_norm_g=v_q_norm_g, k_norm_g=v_k_norm_g, final_g=v_final_g)

    ixy = 2 * lax.axis_index("x") + lax.axis_index("y")
    n_lat = x.shape[1]
    nl = norm_g.shape[0]
    nmod = w_mod.shape[2]
    xin = jnp.concatenate([ctx[0], x[0]], axis=0)

    ic = lax.axis_index("c")
    mine = [lax.dynamic_index_in_dim(weights[n], ic, 0, keepdims=False).astype(BF) for n in _BIG]
    got = exchange([(m_, "xy", "gather") for m_ in mine] + [(c, "xy", "gather")], "gather_weights")
    both = d2d([(g_.reshape(-1, g_.shape[-1]), "gather") for g_ in got[:len(_BIG)]], "swap_weights")
    both = [b_.reshape((2,) + g_.shape) for b_, g_ in zip(both, got)]
    wein = [both[0][0], both[0][1]]
    weout = [both[1][l].reshape(1, D, D) for l in range(2)]
    wglu = [both[2][l].reshape(AW, AW) for l in range(2)]
    woin = [both[3][0], both[3][1]]
    woout = [both[4][l].reshape(1, D, D) for l in range(2)]
    c_group = got[len(_BIG)].reshape(4, D)

    cond = jnp.concatenate([c_group, jnp.broadcast_to(c_ctx.reshape(1, D), (4, D))], axis=0)
    b_shard = lax.dynamic_slice(b_mod, (0, ixy * nmod), (nl, nmod)).reshape(nl, 1, nmod)
    mpart = ada_fwd(cond, w_mod, b_shard)
    m_lat, m_ctx = exchange([(jnp.transpose(mpart[:, 0:4], (1, 0, 2)), "xy", "scatter"),
                             (mpart[:, 4], "xy", "gather")], "exchange_mod")
    m_lat = jnp.transpose(m_lat, (1, 0, 2)).reshape(nl, 3, D)
    m_ctx = jnp.transpose(m_ctx, (1, 0, 2)).reshape(nl, 3, D)
    mods = [jnp.stack([m_ctx[l], m_lat[l]], axis=0) for l in range(nl)]

    loss_part, dx, g, d_norm_g, d_mod_lat, d_mod_ctx, d_final_g = _local_step(
        xin, loss_target[0], mods, wein, weout, wglu, woin, woout, weights)
    grad_x = dx[NC:].reshape(1, n_lat, D)

    d_mod_lat, d_mod_ctx = jnp.stack(d_mod_lat), jnp.stack(d_mod_ctx)
    dm_send = jnp.stack([d_mod_lat.reshape(nl, 4, nmod), d_mod_ctx.reshape(nl, 4, nmod)])
    (dm_got,) = exchange([(jnp.transpose(dm_send, (2, 0, 1, 3)), "xy", "scatter")], "exchange_dmod")
    dm_rows = jnp.concatenate([dm_got[:, 0], dm_got[:, 1]], axis=0)
    gw_mod, d_cctx = ada_bwd(cond, jnp.transpose(dm_rows, (1, 0, 2)), w_mod)
    g_small = dict(c_ctx=d_cctx.reshape(D), norm_g=jnp.stack(d_norm_g), b_mod=add2(d_mod_lat, d_mod_ctx, "add_dbmod"),
                   final_g=d_final_g.reshape(D))
    for name in _SMALL:
        if name not in g_small:
            g_small[name] = jnp.stack(g[name])

    flat = jnp.concatenate([g_small[n].reshape(-1) for n in _SMALL])
    nflat = flat.shape[0]
    npad = -(-nflat // _SMALL_ALIGN) * _SMALL_ALIGN
    flat = jnp.concatenate([flat, jnp.zeros((npad - nflat,), F32)]).reshape(8, npad // (8 * 128), 128)
    pairs = [gw_mod.reshape(2, nl // 2 * D, nmod)]
    for name in _BIG:
        st = jnp.stack(g[name]) if isinstance(g[name], list) else g[name]
        pairs.append(st.reshape(2, -1, st.shape[-1]))
    got_a = d2d([(pairs[k], "swap") for k in (1, 2, 3)], "reduce_chip_a")
    got_b = d2d([(pairs[k], "swap") for k in (0, 4, 5)], "reduce_chip_b")
    theirs = [got_b[0]] + got_a + got_b[1:]
    chip = [sum_own(pairs[k], theirs[k], f"sum_chip{k}", F32 if k == 0 else BF) for k in range(len(pairs))]
    parts = exchange([(flat, "all", "scatter")]
                     + [(s_.reshape(4, s_.shape[0] // 4, s_.shape[1]), "xy", "scatter") for s_ in chip[1:]],
                     "reduce_scatter")
    sums = [sum_lead(pt, f"sum_shard{k}") for k, pt in enumerate(parts)]
    (flat_full,) = exchange([(sums[0], "all", "gather")], "gather_small")
    full = [flat_full] + d2d([(chip[0], "gather")] + [(s_, "gather") for s_ in sums[1:]], "all_gather")
    flat = full[0].reshape(-1)
    grads = {}
    off = 0
    for name in _SMALL:
        sz = weights[name].size
        grads[name] = flat[off:off + sz].reshape(weights[name].shape)
        off += sz
    grads["w_mod"] = full[1].reshape(w_mod.shape)
    for k, name in enumerate(_BIG):
        grads[name] = full[2 + k].reshape(weights[name].shape)

    delta, new_m, new_v = {}, {}, {}
    for name in _WEIGHTS:
        w2 = _view2d(weights[name])
        d2, m2, v2 = adamw(w2, grads[name].reshape(w2.shape), mom_m[name].reshape(w2.shape),
                           mom_v[name].reshape(w2.shape), f"adamw_{name}")
        shp = weights[name].shape
        delta[name], new_m[name], new_v[name] = d2.reshape(shp), m2.reshape(shp), v2.reshape(shp)

    loss = lax.psum(loss_part[0, 0], ("x", "y", "c"))
    return (loss, grad_x, *[grads[n] for n in _WEIGHTS], *[delta[n] for n in _WEIGHTS],
            *[new_m[n] for n in _WEIGHTS], *[new_v[n] for n in _WEIGHTS])


def _local_step(xin, target, mods, wein, weout, wglu, woin, woout, w):
    norm_g, gm_v_g, gm_w_s, gm_b_s = w["norm_g"], w["gm_v_g"], w["gm_w_s"], w["gm_b_s"]
    s5_lam_re, s5_lam_im, s5_log_dt = w["s5_lam_re"], w["s5_lam_im"], w["s5_log_dt"]
    s5_b_re, s5_b_im, s5_c_re, s5_c_im = w["s5_b_re"], w["s5_b_im"], w["s5_c_re"], w["s5_c_im"]
    s5_d, s5_b_glu, q_norm_g, k_norm_g, final_g = w["s5_d"], w["s5_b_glu"], w["q_norm_g"], w["k_norm_g"], w["final_g"]
    nl = norm_g.shape[0]
    n_lat = xin.shape[0] - NC

    cos, sins = _rope_tables(n_lat)

    s5p = []
    for i in range(2):
        lam_l = (s5_lam_re[i].reshape(2, SW), s5_lam_im[i].reshape(2, SW),
                 jnp.repeat(s5_log_dt[i], SP, axis=1))
        lam_r = (jnp.repeat(s5_lam_re[i].reshape(2 * SG, SP), SH, axis=0),
                 jnp.repeat(s5_lam_im[i].reshape(2 * SG, SP), SH, axis=0),
                 jnp.repeat(s5_log_dt[i].reshape(2 * SG, 1), SH, axis=0))
        b_r = (jnp.transpose(s5_b_re[i], (0, 1, 3, 2)).reshape(2 * SG * SH, SP),
               jnp.transpose(s5_b_im[i], (0, 1, 3, 2)).reshape(2 * SG * SH, SP))
        pw_re, pw_im, bbr, bbi = s5_disc(*lam_l, *lam_r, *b_r)
        s5p.append(dict(
            lam_r=lam_r, b_r=b_r, pw_re=pw_re, pw_im=pw_im,
            bb_re=_block_diag(bbr.reshape(2, SG, SH, SP), False).astype(BF),
            bb_im=_block_diag(bbi.reshape(2, SG, SH, SP), False).astype(BF),
            ct_re=_block_diag(s5_c_re[i], True).astype(BF), ct_im=_block_diag(s5_c_im[i], True).astype(BF)))

    saved = []
    xcur = xin
    h = pro_fwd(xcur, norm_g[0].reshape(1, D), mods[0], "pro_fwd0")
    for l in range(nl):
        i = l // 2
        sv = dict(x=xcur, h=h)
        if l % 2 == 0:
            p = mm_nn(h, wein[i], f"in_proj{l}")
            sp = s5p[i]
            for dr, rev in ((0, False), (1, True)):
                sv[f"y{dr}"], sv[f"hpr{dr}"], sv[f"hpi{dr}"] = s5q_fwd(
                    p, sp["bb_re"][dr], sp["bb_im"][dr], sp["ct_re"][dr], sp["ct_im"][dr],
                    sp["pw_re"][dr], sp["pw_im"][dr], rev, f"s5_fwd{l}_{dr}")
            mix = mix_fwd(p, sv["y0"], sv["y1"], gm_v_g[i].reshape(1, AW), gm_w_s[i].astype(BF),
                          gm_b_s[i].reshape(NGRP, CHUNK, 1), s5_d[i].reshape(1, AW), wglu[i],
                          s5_b_glu[i].reshape(1, AW), f"mix_fwd{l}")
            o = mm_nn(mix, weout[i], f"out_proj{l}")
        else:
            p = mm_nn(h, woin[i], f"in_proj{l}")
            sv["qkv"] = attn_prep(p, q_norm_g[i].reshape(1, HD), k_norm_g[i].reshape(1, HD), cos, sins, f"attn_prep{l}")
            sv["o_att"], mix, sv["lse"] = attn_fwd(sv["qkv"], p, f"attn_fwd{l}")
            o = mm_nn(mix, woout[i], f"out_proj{l}")
        sv.update(p=p, mix=mix, o=o)
        saved.append(sv)
        if l < nl - 1:
            xcur, h = res_pro_fwd(xcur, o, mods[l], True, norm_g[l + 1].reshape(1, D), mods[l + 1], f"res_pro_fwd{l}")
        else:
            xcur = res_fwd(xcur, o, mods[l], False, f"res_fwd{l}")

    loss_part, dx, d_final_g, do, dgt = final_loss(xcur, target, final_g.reshape(1, D), saved[-1]["o"], mods[-1])

    g = {}
    gbuf = {}
    d_norm_g, d_mod_lat, d_mod_ctx = [None] * nl, [None] * nl, [None] * nl
    for name in ("s5_w_glu", "gm_v_g", "gm_w_s", "gm_b_s", "s5_lam_re", "s5_lam_im",
                 "s5_log_dt", "s5_b_re", "s5_b_im", "s5_c_re", "s5_c_im", "s5_d", "s5_b_glu", "q_norm_g", "k_norm_g"):
        g[name] = [None, None]
    for l in reversed(range(nl)):
        i = l // 2
        sv = saved[l]
        w_out = weout[i] if l % 2 == 0 else woout[i]
        dmix = mm_nt(do, w_out, f"out_dgrad{l}")
        out_name, in_name = ("we_out", "we_in") if l % 2 == 0 else ("wo_out", "wo_in")
        gbuf[out_name] = mm_tn(sv["mix"], do, 1, f"out_wgrad{l}", slot=i, into=gbuf.get(out_name))
        if l % 2 == 0:
            sp = s5p[i]
            (dp, dy, g["gm_w_s"][i], dbs, dvg, dd, g["s5_w_glu"][i], dbg) = mix_bwd(
                sv["p"], sv["y0"], sv["y1"], dmix, gm_v_g[i].reshape(1, AW), gm_w_s[i].astype(BF),
                gm_b_s[i].reshape(NGRP, CHUNK, 1), s5_d[i].reshape(1, AW), wglu[i], s5_b_glu[i].reshape(1, AW),
                f"mix_bwd{l}")
            g["gm_b_s"][i], g["gm_v_g"][i] = dbs.reshape(NGRP, CHUNK), dvg.reshape(AW)
            g["s5_d"][i], g["s5_b_glu"][i] = dd.reshape(AW), dbg.reshape(AW)
            g["s5_w_glu"][i] = g["s5_w_glu"][i].reshape(4, AW // 4, AW)
            dxd, das_r, das_i, dbbs_r, dbbs_i, dcs_r, dcs_i = [], [], [], [], [], [], []
            for dr, rev in ((0, False), (1, True)):
                dxs_d, da_r, da_i, dbb_r, dbb_i, dc_r, dc_i = s5q_bwd(
                    sv["p"], sv[f"hpr{dr}"], sv[f"hpi{dr}"], dy, sp["bb_re"][dr], sp["bb_im"][dr],
                    sp["ct_re"][dr], sp["ct_im"][dr], sp["pw_re"][dr], -sp["pw_im"][dr], rev, f"s5_bwd{l}_{dr}")
                dxd.append(dxs_d)
                das_r.append(jnp.repeat(da_r.reshape(SG, SP), SH, axis=0))
                das_i.append(jnp.repeat(da_i.reshape(SG, SP), SH, axis=0))
                dbbs_r.append(_diag_blocks(dbb_r).reshape(SG * SH, SP))
                dbbs_i.append(_diag_blocks(dbb_i).reshape(SG * SH, SP))
                dcs_r.append(_diag_blocks(dc_r))
                dcs_i.append(_diag_blocks(dc_i))
            cat = lambda parts: jnp.concatenate(parts, axis=0)
            dlr, dli, dldt, dbr, dbi = s5_param_bwd(*sp["lam_r"], *sp["b_r"], cat(das_r), cat(das_i),
                                                    cat(dbbs_r), cat(dbbs_i))
            g["s5_lam_re"][i], g["s5_lam_im"][i] = dlr.reshape(2, SG, SP), dli.reshape(2, SG, SP)
            g["s5_log_dt"][i] = dldt.reshape(2, SG)
            g["s5_b_re"][i] = jnp.transpose(dbr.reshape(2, SG, SH, SP), (0, 1, 3, 2))
            g["s5_b_im"][i] = jnp.transpose(dbi.reshape(2, SG, SH, SP), (0, 1, 3, 2))
            g["s5_c_re"][i], g["s5_c_im"][i] = jnp.stack(dcs_r), jnp.stack(dcs_i)
            dp = s5p_dx_sum(dy, s5_d[i].reshape(1, AW), dxd[0], dxd[1], dp, f"s5_dx_sum{l}")
            w_in = wein[i]
        else:
            dq, dp, dk, dv = attn_bwd(sv["qkv"], sv["p"], dmix, sv["o_att"], sv["lse"], f"attn_bwd{l}")
            dp, dqg, dkg = attn_prep_bwd(sv["p"], dq, dk, dv, q_norm_g[i].reshape(1, HD),
                                         k_norm_g[i].reshape(1, HD), cos, sins, dp, f"attn_prep_bwd{l}")
            g["q_norm_g"][i], g["k_norm_g"][i] = dqg.reshape(HD), dkg.reshape(HD)
            w_in = woin[i]
        dh = mm_nt(dp, w_in, f"in_dgrad{l}")
        gbuf[in_name] = mm_tn(sv["h"], dp, 4, f"in_wgrad{l}", slot=i, into=gbuf.get(in_name))
        dgt_l = dgt
        if l > 0:
            dx, dmod2, dng, do, dgt = pro_res_bwd(sv["x"], dh, dx, norm_g[l].reshape(1, D), mods[l],
                                                  saved[l - 1]["o"], mods[l - 1], f"pro_res_bwd{l}")
        else:
            dx, dmod2, dng = pro_bwd(sv["x"], dh, dx, norm_g[l].reshape(1, D), mods[l], f"pro_bwd{l}")
        d_norm_g[l] = dng.reshape(D)
        d_mod_ctx[l] = jnp.concatenate([dmod2[0, 0], dmod2[0, 1], dgt_l[0]])
        d_mod_lat[l] = jnp.concatenate([dmod2[1, 0], dmod2[1, 1], dgt_l[1]])
    g.update(gbuf)
    return loss_part, dx, g, d_norm_g, d_mod_lat, d_mod_ctx, d_final_g
```

```python
import functools
import math

import numpy as np
import jax
import jax.numpy as jnp
from jax import lax
from jax.experimental import pallas as pl
from jax.experimental.pallas import tpu as pltpu

F32 = jnp.float32
BF = jnp.bfloat16
MESH = pl.DeviceIdType.MESH

D = 1024
NC = 256
SEQ = 4096
GRID_W = 64
TM = 256
CHUNK = 128
EPS = 1e-6
HD = 128
NQ = 8
NKV = 2
ROPE_THETA = 10000.0
SG = 32
SP = 64
SH = 16
SW = SG * SP
GELU_K = math.sqrt(2.0 / math.pi)
GELU_C = 0.044715
VMEM_LIMIT_BYTES = 56 * 1024 * 1024

ADAM_LR = 0.001
ADAM_B1 = 0.9
ADAM_B2 = 0.999
ADAM_EPS = 1e-08
ADAM_WD = 0.01
ADAM_STEP = 10


def _call(body, *, name, out_shape, grid=None, in_specs=None, out_specs=None, scratch=()):
    kw = {}
    if grid is not None:
        kw["grid"] = grid
    if in_specs is not None:
        kw["in_specs"] = in_specs
    if out_specs is not None:
        kw["out_specs"] = out_specs
    return pl.pallas_call(
        body, name=name, out_shape=out_shape, scratch_shapes=list(scratch),
        compiler_params=pltpu.CompilerParams(vmem_limit_bytes=VMEM_LIMIT_BYTES), **kw)


def _dot(a, b, ca=1, cb=0):
    return lax.dot_general(a, b, (((ca,), (cb,)), ((), ())), preferred_element_type=F32)


def _sig(x):
    return 1.0 / (1.0 + jnp.exp(-x))


def _full(shape):
    n = len(shape)
    return pl.BlockSpec(shape, lambda *_: (0,) * n)


def _mm_rows(t):
    for rows in (1088, 1024, 768, 512, 256):
        if t % rows == 0:
            return rows
    raise ValueError(t)


def mm_nn(a, w3, name, out_dtype=F32):
    t, k = a.shape
    j, _, nb = w3.shape
    tr = _mm_rows(t)

    def body(a_ref, w_ref, o_ref):
        o_ref[...] = _dot(a_ref[...], w_ref[0]).astype(o_ref.dtype)

    return _call(body, name=name, grid=(j, t // tr),
                 in_specs=[pl.BlockSpec((tr, k), lambda jj, i: (i, 0)),
                           pl.BlockSpec((1, k, nb), lambda jj, i: (jj, 0, 0))],
                 out_specs=pl.BlockSpec((tr, nb), lambda jj, i: (i, jj)),
                 out_shape=jax.ShapeDtypeStruct((t, j * nb), out_dtype))(a, w3)


def mm_nt(a, w3, name, out_dtype=F32):
    t, _ = a.shape
    j, k, nb = w3.shape
    tr = _mm_rows(t)

    def body(a_ref, w_ref, o_ref):
        acc = _dot(a_ref[:, 0:nb], w_ref[0], 1, 1)
        for jj in range(1, j):
            acc = acc + _dot(a_ref[:, jj * nb:(jj + 1) * nb], w_ref[jj], 1, 1)
        o_ref[...] = acc.astype(o_ref.dtype)

    return _call(body, name=name, grid=(t // tr,),
                 in_specs=[pl.BlockSpec((tr, j * nb), lambda i: (i, 0)), _full((j, k, nb))],
                 out_specs=pl.BlockSpec((tr, k), lambda i: (i, 0)),
                 out_shape=jax.ShapeDtypeStruct((t, k), out_dtype))(a, w3)


def mm_tn(a, b, j, name, slot=0, into=None):
    t, m = a.shape
    nb = b.shape[1] // j
    tr = _mm_rows(t)

    def body(a_ref, b_ref, *rest):
        o_ref = rest[-1]

        @pl.when(pl.program_id(1) == 0)
        def _():
            o_ref[...] = jnp.zeros_like(o_ref)
        o_ref[0, 0] += _dot(a_ref[...], b_ref[...], 0, 0)

    in_specs = [pl.BlockSpec((tr, m), lambda jj, i: (i, 0)), pl.BlockSpec((tr, nb), lambda jj, i: (i, jj))]
    args = [a, b]
    alias = {}
    if into is not None:
        in_specs.append(pl.BlockSpec(memory_space=pl.ANY))
        args.append(into)
        alias = {2: 0}
    return pl.pallas_call(
        body, name=name, grid=(j, t // tr), in_specs=in_specs,
        out_specs=pl.BlockSpec((1, 1, m, nb), lambda jj, i: (slot, jj, 0, 0)),
        out_shape=jax.ShapeDtypeStruct((2, j, m, nb), F32), input_output_aliases=alias,
        compiler_params=pltpu.CompilerParams(vmem_limit_bytes=VMEM_LIMIT_BYTES))(*args)


def _mod_rows(mod_ref, i):
    ctx = i == 0
    sh = jnp.where(ctx, mod_ref[0, 0:1, :], mod_ref[1, 0:1, :])
    sc = jnp.where(ctx, mod_ref[0, 1:2, :], mod_ref[1, 1:2, :])
    gt = jnp.where(ctx, mod_ref[0, 2:3, :], mod_ref[1, 2:3, :])
    return sh, sc, gt


def _split_specs():
    return [pl.BlockSpec((TM, D), lambda i: (0, 0)), pl.BlockSpec((TM, D), lambda i: (jnp.maximum(i - 1, 0), 0))]


def _split_tile(c_ref, l_ref, i):
    return jnp.where(i == 0, c_ref[...], l_ref[...])


def pro_fwd(ctx, lat, g, mod, name):
    t = ctx.shape[0] + lat.shape[0]

    def body(c_ref, l_ref, g_ref, mod_ref, h_ref):
        i = pl.program_id(0)
        sh, sc, _ = _mod_rows(mod_ref, i)
        xv = _split_tile(c_ref, l_ref, i)
        r = lax.rsqrt(jnp.mean(xv * xv, axis=-1, keepdims=True) + EPS)
        h_ref[...] = ((xv * r) * g_ref[...] * (1.0 + sc) + sh).astype(BF)

    return _call(body, name=name, grid=(t // TM,),
                 in_specs=_split_specs() + [_full((1, D)), _full((2, 3, D))],
                 out_specs=pl.BlockSpec((TM, D), lambda i: (i, 0)),
                 out_shape=jax.ShapeDtypeStruct((t, D), BF))(ctx, lat, g, mod)


def pro_bwd(ctx, lat, dh, dxn, g, mod, name):
    t = ctx.shape[0] + lat.shape[0]

    def body(c_ref, l_ref, dh_ref, dxn_ref, g_ref, mod_ref, dx_ref, dmod_ref, dg_ref):
        i = pl.program_id(0)

        @pl.when(i == 0)
        def _():
            dmod_ref[...] = jnp.zeros_like(dmod_ref)
            dg_ref[...] = jnp.zeros_like(dg_ref)

        _, sc, _ = _mod_rows(mod_ref, i)
        xv = _split_tile(c_ref, l_ref, i)
        gv = g_ref[...]
        r = lax.rsqrt(jnp.mean(xv * xv, axis=-1, keepdims=True) + EPS)
        xn = xv * r
        dh_v = dh_ref[...]
        e = dh_v * (1.0 + sc)
        dsh = jnp.sum(dh_v, axis=0, keepdims=True)
        dsc = jnp.sum(dh_v * xn * gv, axis=0, keepdims=True)
        dg_ref[...] += jnp.sum(e * xn, axis=0, keepdims=True)
        dxh = e * gv

        @pl.when(i == 0)
        def _():
            dmod_ref[0, 0:1, :] += dsh
            dmod_ref[0, 1:2, :] += dsc

        @pl.when(i > 0)
        def _():
            dx_ref[...] = dxn_ref[...] + r * (dxh - xn * jnp.mean(dxh * xn, axis=-1, keepdims=True))
            dmod_ref[1, 0:1, :] += dsh
            dmod_ref[1, 1:2, :] += dsc

    tile = pl.BlockSpec((TM, D), lambda i: (i, 0))
    return _call(body, name=name, grid=(t // TM,),
                 in_specs=_split_specs() + [tile, tile, _full((1, D)), _full((2, 3, D))],
                 out_specs=[_split_specs()[1], _full((2, 2, D)), _full((1, D))],
                 out_shape=[jax.ShapeDtypeStruct(lat.shape, F32), jax.ShapeDtypeStruct((2, 2, D), F32),
                            jax.ShapeDtypeStruct((1, D), F32)])(ctx, lat, dh, dxn, g, mod)


def res_fwd(x, o, mod, update_ctx, name):
    t = x.shape[0]

    def body(x_ref, o_ref, mod_ref, y_ref):
        i = pl.program_id(0)
        _, _, gt = _mod_rows(mod_ref, i)
        upd = x_ref[...] + gt * o_ref[...]
        if update_ctx:
            y_ref[...] = upd
        else:
            y_ref[...] = jnp.where(i == 0, x_ref[...], upd)

    tile = pl.BlockSpec((TM, D), lambda i: (i, 0))
    return _call(body, name=name, grid=(t // TM,), in_specs=[tile, tile, _full((2, 3, D))],
                 out_specs=tile, out_shape=jax.ShapeDtypeStruct((t, D), F32))(x, o, mod)


def res_bwd(dxn, o, mod, update_ctx, name):
    t = dxn.shape[0]

    def body(dxn_ref, o_ref, mod_ref, do_ref, dgt_ref):
        i = pl.program_id(0)

        @pl.when(i == 0)
        def _():
            dgt_ref[...] = jnp.zeros_like(dgt_ref)

        _, _, gt = _mod_rows(mod_ref, i)
        dv = dxn_ref[...]
        do = gt * dv
        dgt = jnp.sum(dv * o_ref[...], axis=0, keepdims=True)
        if update_ctx:
            do_ref[...] = do.astype(BF)
        else:
            do_ref[...] = jnp.where(i == 0, jnp.zeros_like(do), do).astype(BF)

        if update_ctx:
            @pl.when(i == 0)
            def _():
                dgt_ref[0:1, :] += dgt

        @pl.when(i > 0)
        def _():
            dgt_ref[1:2, :] += dgt

    tile = pl.BlockSpec((TM, D), lambda i: (i, 0))
    return _call(body, name=name, grid=(t // TM,), in_specs=[tile, tile, _full((2, 3, D))],
                 out_specs=[tile, _full((2, D))],
                 out_shape=[jax.ShapeDtypeStruct((t, D), BF), jax.ShapeDtypeStruct((2, D), F32)])(dxn, o, mod)


def res_pro_fwd(x, o, mod, update_ctx, g_next, mod_next, name):
    split = isinstance(x, tuple)
    xs = list(x) if split else [x]
    t = o.shape[0]

    def body(*refs):
        x_refs = refs[:len(xs)]
        o_ref, mod_ref, g_ref, modn_ref, y_ref, h_ref = refs[len(xs):]
        i = pl.program_id(0)
        _, _, gt = _mod_rows(mod_ref, i)
        xv = _split_tile(x_refs[0], x_refs[1], i) if split else x_refs[0][...]
        xn = xv + gt * o_ref[...]
        if not update_ctx:
            xn = jnp.where(i == 0, xv, xn)
        y_ref[...] = xn
        sh, sc, _ = _mod_rows(modn_ref, i)
        r = lax.rsqrt(jnp.mean(xn * xn, axis=-1, keepdims=True) + EPS)
        h_ref[...] = ((xn * r) * g_ref[...] * (1.0 + sc) + sh).astype(BF)

    tile = pl.BlockSpec((TM, D), lambda i: (i, 0))
    return _call(body, name=name, grid=(t // TM,),
                 in_specs=(_split_specs() if split else [tile]) + [tile, _full((2, 3, D)), _full((1, D)), _full((2, 3, D))],
                 out_specs=[tile, tile],
                 out_shape=[jax.ShapeDtypeStruct((t, D), F32), jax.ShapeDtypeStruct((t, D), BF)])(
        *xs, o, mod, g_next, mod_next)


def _res_bwd_part(dx, o_ref, modp_ref, do_ref, dgt_ref, i, update_ctx):
    _, _, gtp = _mod_rows(modp_ref, i)
    do = gtp * dx
    dgt = jnp.sum(dx * o_ref[...], axis=0, keepdims=True)
    if update_ctx:
        do_ref[...] = do.astype(BF)

        @pl.when(i == 0)
        def _():
            dgt_ref[0:1, :] += dgt
    else:
        do_ref[...] = jnp.where(i == 0, jnp.zeros_like(do), do).astype(BF)

    @pl.when(i > 0)
    def _():
        dgt_ref[1:2, :] += dgt


def pro_res_bwd(x, dh, dxn, g, mod, o_prev, mod_prev, name):
    t = x.shape[0]

    def body(x_ref, dh_ref, dxn_ref, g_ref, mod_ref, o_ref, modp_ref, dx_ref, dmod_ref, dg_ref, do_ref, dgt_ref):
        i = pl.program_id(0)

        @pl.when(i == 0)
        def _():
            dmod_ref[...] = jnp.zeros_like(dmod_ref)
            dg_ref[...] = jnp.zeros_like(dg_ref)
            dgt_ref[...] = jnp.zeros_like(dgt_ref)

        _, sc, _ = _mod_rows(mod_ref, i)
        xv = x_ref[...]
        gv = g_ref[...]
        r = lax.rsqrt(jnp.mean(xv * xv, axis=-1, keepdims=True) + EPS)
        xn = xv * r
        dh_v = dh_ref[...]
        e = dh_v * (1.0 + sc)
        dsh = jnp.sum(dh_v, axis=0, keepdims=True)
        dsc = jnp.sum(dh_v * xn * gv, axis=0, keepdims=True)
        dg_ref[...] += jnp.sum(e * xn, axis=0, keepdims=True)
        dxh = e * gv
        dx = dxn_ref[...] + r * (dxh - xn * jnp.mean(dxh * xn, axis=-1, keepdims=True))
        dx_ref[...] = dx

        @pl.when(i == 0)
        def _():
            dmod_ref[0, 0:1, :] += dsh
            dmod_ref[0, 1:2, :] += dsc

        @pl.when(i > 0)
        def _():
            dmod_ref[1, 0:1, :] += dsh
            dmod_ref[1, 1:2, :] += dsc

        _res_bwd_part(dx, o_ref, modp_ref, do_ref, dgt_ref, i, True)

    tile = pl.BlockSpec((TM, D), lambda i: (i, 0))
    return _call(body, name=name, grid=(t // TM,),
                 in_specs=[tile, tile, tile, _full((1, D)), _full((2, 3, D)), tile, _full((2, 3, D))],
                 out_specs=[tile, _full((2, 2, D)), _full((1, D)), tile, _full((2, D))],
                 out_shape=[jax.ShapeDtypeStruct((t, D), F32), jax.ShapeDtypeStruct((2, 2, D), F32),
                            jax.ShapeDtypeStruct((1, D), F32), jax.ShapeDtypeStruct((t, D), BF),
                            jax.ShapeDtypeStruct((2, D), F32)])(x, dh, dxn, g, mod, o_prev, mod_prev)


def final_loss(x, target, g, o_last, mod_last):
    t = x.shape[0]

    def body(x_ref, t_ref, g_ref, o_ref, modp_ref, loss_ref, dx_ref, dg_ref, do_ref, dgt_ref):
        i = pl.program_id(0)

        @pl.when(i == 0)
        def _():
            loss_ref[...] = jnp.zeros_like(loss_ref)
            dg_ref[...] = jnp.zeros_like(dg_ref)
            dx_ref[...] = jnp.zeros_like(dx_ref)
            do_ref[...] = jnp.zeros_like(do_ref)
            dgt_ref[...] = jnp.zeros_like(dgt_ref)

        @pl.when(i > 0)
        def _():
            xv = x_ref[...]
            gv = g_ref[...]
            r = lax.rsqrt(jnp.mean(xv * xv, axis=-1, keepdims=True) + EPS)
            xn = xv * r
            err = xn * gv - t_ref[...]
            loss_ref[...] += (0.5 / D) * jnp.sum(jnp.sum(err * err, axis=1, keepdims=True), axis=0, keepdims=True)
            dy = err * (1.0 / D)
            dg_ref[...] += jnp.sum(dy * xn, axis=0, keepdims=True)
            dxh = dy * gv
            dx = r * (dxh - xn * jnp.mean(dxh * xn, axis=-1, keepdims=True))
            dx_ref[...] = dx
            do_ref[...] = (modp_ref[1, 2:3, :] * dx).astype(BF)
            dgt_ref[1:2, :] += jnp.sum(dx * o_ref[...], axis=0, keepdims=True)

    tile = pl.BlockSpec((TM, D), lambda i: (i, 0))
    return _call(body, name="final_loss", grid=(t // TM,),
                 in_specs=[tile, pl.BlockSpec((TM, D), lambda i: (jnp.maximum(i - 1, 0), 0)), _full((1, D)), tile,
                           _full((2, 3, D))],
                 out_specs=[_full((1, 1)), tile, _full((1, D)), tile, _full((2, D))],
                 out_shape=[jax.ShapeDtypeStruct((1, 1), F32), jax.ShapeDtypeStruct((t, D), F32),
                            jax.ShapeDtypeStruct((1, D), F32), jax.ShapeDtypeStruct((t, D), BF),
                            jax.ShapeDtypeStruct((2, D), F32)])(x, target, g, o_last, mod_last)


def ada_fwd(cond, w_mod, b_mod):
    nl, _, nw = w_mod.shape

    def body(c_ref, w_ref, b_ref, o_ref):
        cv = c_ref[...]
        s = (cv * _sig(cv)).astype(BF)
        o_ref[0] = _dot(s, w_ref[0].astype(BF)) + b_ref[0]

    return _call(body, name="ada_fwd", grid=(nl,),
                 in_specs=[_full((8, D)), pl.BlockSpec((1, D, nw), lambda l: (l, 0, 0)),
                           pl.BlockSpec((1, 1, nw), lambda l: (l, 0, 0))],
                 out_specs=pl.BlockSpec((1, 8, nw), lambda l: (l, 0, 0)),
                 out_shape=jax.ShapeDtypeStruct((nl, 8, nw), F32))(cond, w_mod, b_mod)


def ada_bwd(cond, dm, w_mod):
    nl, _, nw = w_mod.shape

    def body(c_ref, dm_ref, w_ref, gw_ref, dcc_ref, dc_ref):
        l = pl.program_id(0)

        @pl.when(l == 0)
        def _():
            dc_ref[...] = jnp.zeros_like(dc_ref)

        cv = c_ref[...]
        sg = _sig(cv)
        s = (cv * sg).astype(BF)
        dmv = dm_ref[0].astype(BF)
        gw_ref[0] = _dot(s, dmv, 0, 0)
        dc_ref[...] += _dot(dmv, w_ref[0].astype(BF), 1, 1)

        @pl.when(l == nl - 1)
        def _():
            dcond = dc_ref[...] * (sg * (1.0 + cv * (1.0 - sg)))
            dcc_ref[...] = jnp.sum(dcond[4:8], axis=0, keepdims=True)

    return _call(body, name="ada_bwd", grid=(nl,),
                 in_specs=[_full((8, D)), pl.BlockSpec((1, 8, nw), lambda l: (l, 0, 0)),
                           pl.BlockSpec((1, D, nw), lambda l: (l, 0, 0))],
                 out_specs=[pl.BlockSpec((1, D, nw), lambda l: (l, 0, 0)), _full((1, D))],
                 out_shape=[jax.ShapeDtypeStruct((nl, D, nw), F32), jax.ShapeDtypeStruct((1, D), F32)],
                 scratch=[pltpu.VMEM((8, D), F32)])(cond, dm, w_mod)


def add2(a, b, name):
    def body(a_ref, b_ref, o_ref):
        o_ref[...] = a_ref[...] + b_ref[...]

    return _call(body, name=name, out_shape=jax.ShapeDtypeStruct(a.shape, a.dtype))(a, b)


AW = 512
NGRP = 4


def Y4_SPEC():
    return pl.BlockSpec((AW // 128, TM, 128), lambda i: (0, i, 0))


def _cat_lanes(ref):
    return jnp.concatenate([ref[q] for q in range(ref.shape[0])], axis=1)


def _gelu(y):
    t = jnp.tanh(GELU_K * (y + GELU_C * y * y * y))
    return 0.5 * y * (1.0 + t), t


def _layer_norm_stats(v):
    mu = jnp.mean(v, axis=-1, keepdims=True)
    vc = v - mu
    rstd = lax.rsqrt(jnp.mean(vc * vc, axis=-1, keepdims=True) + EPS)
    return vc * rstd, rstd


def _spatial_mix(vn_ref, ws_ref, bs_ref, mixed_ref):
    for ch in range(TM // CHUNK):
        rows = slice(ch * CHUNK, (ch + 1) * CHUNK)
        for g in range(NGRP):
            cols = slice(g * CHUNK, (g + 1) * CHUNK)
            mixed_ref[rows, cols] = _dot(ws_ref[g], vn_ref[rows, cols]) + bs_ref[g]


def mix_fwd(p, yf, yb, vg, ws, bs, dsk, wglu, bglu, name):
    t = p.shape[0]

    def body(p_ref, yf_ref, yb_ref, vg_ref, ws_ref, bs_ref, d_ref, wg_ref, bg_ref, o_ref, vn_ref, mixed_ref):
        vhat, _ = _layer_norm_stats(p_ref[:, AW:2 * AW])
        vn_ref[...] = (vhat * vg_ref[...]).astype(BF)
        _spatial_mix(vn_ref, ws_ref, bs_ref, mixed_ref)
        ga = p_ref[:, 2 * AW:3 * AW]
        o_ref[:, 0:AW] = (p_ref[:, 0:AW] * mixed_ref[...] * (ga * _sig(ga))).astype(BF)
        y = _cat_lanes(yf_ref) + _cat_lanes(yb_ref) + d_ref[...] * p_ref[:, 3 * AW:4 * AW]
        y2, _ = _gelu(y)
        z = _dot(y2.astype(BF), wg_ref[...]) + bg_ref[...]
        gb = p_ref[:, 4 * AW:5 * AW]
        o_ref[:, AW:2 * AW] = (y2 * _sig(z) * (gb * _sig(gb))).astype(BF)

    tile = lambda w: pl.BlockSpec((TM, w), lambda i: (i, 0))
    return _call(body, name=name, grid=(t // TM,),
                 in_specs=[tile(5 * AW), Y4_SPEC(), Y4_SPEC(), _full((1, AW)), _full((NGRP, CHUNK, CHUNK)),
                           _full((NGRP, CHUNK, 1)), _full((1, AW)), _full((AW, AW)), _full((1, AW))],
                 out_specs=tile(2 * AW), out_shape=jax.ShapeDtypeStruct((t, 2 * AW), BF),
                 scratch=[pltpu.VMEM((TM, AW), BF), pltpu.VMEM((TM, AW), F32)])(p, yf, yb, vg, ws, bs, dsk, wglu, bglu)


def mix_bwd(p, yf, yb, dmix, vg, ws, bs, dsk, wglu, bglu, name):
    t = p.shape[0]

    def body(p_ref, yf_ref, yb_ref, dm_ref, vg_ref, ws_ref, bs_ref, d_ref, wg_ref, bg_ref,
             dpa_ref, dy_ref, dws_ref, dbs_ref, dvg_ref, dd_ref, dwg_ref, dbg_ref,
             vn_ref, mixed_ref, dmx_ref, dvn_ref):
        @pl.when(pl.program_id(0) == 0)
        def _():
            for r in (dws_ref, dbs_ref, dvg_ref, dd_ref, dwg_ref, dbg_ref):
                r[...] = jnp.zeros_like(r)

        vhat, rstd = _layer_norm_stats(p_ref[:, AW:2 * AW])
        vgv = vg_ref[...]
        vn_ref[...] = (vhat * vgv).astype(BF)
        _spatial_mix(vn_ref, ws_ref, bs_ref, mixed_ref)
        u = p_ref[:, 0:AW]
        ga = p_ref[:, 2 * AW:3 * AW]
        sga = _sig(ga)
        dya = dm_ref[:, 0:AW]
        mixed = mixed_ref[...]
        dpa_ref[:, 0:AW] = (dya * mixed * (ga * sga)).astype(BF)
        dpa_ref[:, 2 * AW:3 * AW] = (dya * u * mixed * (sga * (1.0 + ga * (1.0 - sga)))).astype(BF)
        dmx_ref[...] = dya * u * (ga * sga)
        for ch in range(TM // CHUNK):
            rows = slice(ch * CHUNK, (ch + 1) * CHUNK)
            for g in range(NGRP):
                cols = slice(g * CHUNK, (g + 1) * CHUNK)
                dmx = dmx_ref[rows, cols]
                dmxb = dmx.astype(BF)
                dws_ref[g] += _dot(dmxb, vn_ref[rows, cols], 1, 1)
                dbs_ref[g] += jnp.sum(dmx, axis=1, keepdims=True)
                dvn_ref[rows, cols] = _dot(ws_ref[g], dmxb, 0, 0)
        dvn = dvn_ref[...]
        dvg_ref[...] += jnp.sum(dvn * vhat, axis=0, keepdims=True)
        dvh = dvn * vgv
        dpa_ref[:, AW:2 * AW] = (rstd * (dvh - jnp.mean(dvh, axis=-1, keepdims=True)
                                         - vhat * jnp.mean(dvh * vhat, axis=-1, keepdims=True))).astype(BF)

        xs = p_ref[:, 3 * AW:4 * AW]
        y = _cat_lanes(yf_ref) + _cat_lanes(yb_ref) + d_ref[...] * xs
        y2, th = _gelu(y)
        y2b = y2.astype(BF)
        z = _dot(y2b, wg_ref[...]) + bg_ref[...]
        sz = _sig(z)
        gb = p_ref[:, 4 * AW:5 * AW]
        sgb = _sig(gb)
        dyb = dm_ref[:, AW:2 * AW]
        dpa_ref[:, 4 * AW:5 * AW] = (dyb * (y2 * sz) * (sgb * (1.0 + gb * (1.0 - sgb)))).astype(BF)
        dy3 = dyb * (gb * sgb)
        dz = dy3 * y2 * sz * (1.0 - sz)
        dzb = dz.astype(BF)
        dwg_ref[...] += _dot(y2b, dzb, 0, 0)
        dbg_ref[...] += jnp.sum(dz, axis=0, keepdims=True)
        dy2 = dy3 * sz + _dot(dzb, wg_ref[...], 1, 1)
        dgelu = 0.5 * (1.0 + th) + 0.5 * y * (1.0 - th * th) * GELU_K * (1.0 + 3.0 * GELU_C * y * y)
        dy = dy2 * dgelu
        dd_ref[...] += jnp.sum(dy * xs, axis=0, keepdims=True)
        dy_ref[...] = dy

    tile = lambda w: pl.BlockSpec((TM, w), lambda i: (i, 0))
    return _call(body, name=name, grid=(t // TM,),
                 in_specs=[tile(5 * AW), Y4_SPEC(), Y4_SPEC(), tile(2 * AW), _full((1, AW)), _full((NGRP, CHUNK, CHUNK)),
                           _full((NGRP, CHUNK, 1)), _full((1, AW)), _full((AW, AW)), _full((1, AW))],
                 out_specs=[tile(5 * AW), tile(AW), _full((NGRP, CHUNK, CHUNK)), _full((NGRP, CHUNK, 1)),
                            _full((1, AW)), _full((1, AW)), _full((AW, AW)), _full((1, AW))],
                 out_shape=[jax.ShapeDtypeStruct((t, 5 * AW), BF),
                            jax.ShapeDtypeStruct((t, AW), F32), jax.ShapeDtypeStruct((NGRP, CHUNK, CHUNK), F32),
                            jax.ShapeDtypeStruct((NGRP, CHUNK, 1), F32), jax.ShapeDtypeStruct((1, AW), F32),
                            jax.ShapeDtypeStruct((1, AW), F32), jax.ShapeDtypeStruct((AW, AW), F32),
                            jax.ShapeDtypeStruct((1, AW), F32)],
                 scratch=[pltpu.VMEM((TM, AW), BF), pltpu.VMEM((TM, AW), F32), pltpu.VMEM((TM, AW), F32),
                          pltpu.VMEM((TM, AW), F32)])(p, yf, yb, dmix, vg, ws, bs, dsk, wglu, bglu)


LN = 512
NBLK = SW // LN
UB = AW // NBLK
SCAN_R = 32
SCAN_G = TM // SCAN_R
PW_ROWS = SCAN_R
POW_EXP = list(range(1, SCAN_R + 1))
GROUPS_PER_TILE = TM // 8


def s5_disc(lam_re, lam_im, dt, lam_re_r, lam_im_r, dt_r, b_re, b_im):
    nexp = jnp.asarray(np.array(POW_EXP, np.float32).reshape(PW_ROWS, 1))

    def body(n_ref, lr_ref, li_ref, dt_ref, lrr_ref, lir_ref, dtr_ref, br_ref, bi_ref,
             pr_ref, pi_ref, bbr_ref, bbi_ref):
        for dr in range(2):
            dtl = jnp.exp(dt_ref[dr:dr + 1, :])
            zr = lr_ref[dr:dr + 1, :] * dtl
            zi = li_ref[dr:dr + 1, :] * dtl
            mag = jnp.exp(n_ref[...] * zr)
            ang = n_ref[...] * zi
            pr_ref[dr] = mag * jnp.cos(ang)
            pi_ref[dr] = mag * jnp.sin(ang)
        lr, li, dtv = lrr_ref[...], lir_ref[...], jnp.exp(dtr_ref[...])
        mag = jnp.exp(lr * dtv)
        nr = mag * jnp.cos(li * dtv) - 1.0
        ni = mag * jnp.sin(li * dtv)
        den = lr * lr + li * li
        fr = (nr * lr + ni * li) / den
        fi = (ni * lr - nr * li) / den
        bbr_ref[...] = fr * br_ref[...] - fi * bi_ref[...]
        bbi_ref[...] = fr * bi_ref[...] + fi * br_ref[...]

    rows = lam_re_r.shape[0]
    return _call(body, name="s5_disc",
                 out_shape=[jax.ShapeDtypeStruct((2, PW_ROWS, SW), F32), jax.ShapeDtypeStruct((2, PW_ROWS, SW), F32),
                            jax.ShapeDtypeStruct((rows, SP), F32), jax.ShapeDtypeStruct((rows, SP), F32)])(
        nexp, lam_re, lam_im, dt, lam_re_r, lam_im_r, dt_r, b_re, b_im)


def s5_param_bwd(lam_re_r, lam_im_r, dt_r, b_re, b_im, da_re, da_im, dbb_re, dbb_im):
    rows = lam_re_r.shape[0]
    ng = rows // SH
    seg = jnp.asarray(np.kron(np.eye(ng, dtype=np.float32), np.ones((1, SH), np.float32)))

    def body(seg_ref, lr_ref, li_ref, dt_ref, br_ref, bi_ref, dar_ref, dai_ref, dbbr_ref, dbbi_ref,
             dlr_ref, dli_ref, ddt_ref, dbr_ref, dbi_ref):
        lr, li, dtv = lr_ref[...], li_ref[...], jnp.exp(dt_ref[...])
        mag = jnp.exp(lr * dtv)
        lbr = mag * jnp.cos(li * dtv)
        lbi = mag * jnp.sin(li * dtv)
        den = lr * lr + li * li
        nr, ni = lbr - 1.0, lbi
        fr = (nr * lr + ni * li) / den
        fi = (ni * lr - nr * li) / den
        br, bi = br_ref[...], bi_ref[...]
        gbr, gbi = dbbr_ref[...], dbbi_ref[...]
        dbr_ref[...] = gbr * fr + gbi * fi
        dbi_ref[...] = gbi * fr - gbr * fi
        gfr = gbr * br + gbi * bi
        gfi = gbi * br - gbr * bi
        ilr, ili = lr / den, -li / den
        gnr = gfr * ilr + gfi * ili
        gni = gfi * ilr - gfr * ili
        qr = -(fr * ilr - fi * ili)
        qi = -(fr * ili + fi * ilr)
        glr = gfr * qr + gfi * qi
        gli = gfi * qr - gfr * qi
        first = (lax.broadcasted_iota(jnp.int32, (rows, 1), 0) % SH) == 0
        glbr = gnr + jnp.where(first, dar_ref[...], 0.0)
        glbi = gni + jnp.where(first, dai_ref[...], 0.0)
        gzr = glbr * lbr + glbi * lbi
        gzi = glbi * lbr - glbr * lbi
        glr = glr + gzr * dtv
        gli = gli + gzi * dtv
        gdt = (gzr * lr + gzi * li) * dtv
        hi = lax.Precision.HIGHEST
        sg = seg_ref[...]
        dlr_ref[...] = jnp.dot(sg, glr, precision=hi, preferred_element_type=F32)
        dli_ref[...] = jnp.dot(sg, gli, precision=hi, preferred_element_type=F32)
        ddt_ref[...] = jnp.sum(jnp.dot(sg, gdt, precision=hi, preferred_element_type=F32), axis=1, keepdims=True)

    return _call(body, name="s5_param_bwd",
                 out_shape=[jax.ShapeDtypeStruct((ng, SP), F32), jax.ShapeDtypeStruct((ng, SP), F32),
                            jax.ShapeDtypeStruct((ng, 1), F32), jax.ShapeDtypeStruct((rows, SP), F32),
                            jax.ShapeDtypeStruct((rows, SP), F32)])(
        seg, lam_re_r, lam_im_r, dt_r, b_re, b_im, da_re, da_im, dbb_re, dbb_im)


def _scan_tile(hr_ref, hi_ref, er_ref, ei_ref, st_ref, cr_ref, ci_ref, pr_ref, pi_ref, reverse):
    row8 = lax.broadcasted_iota(jnp.int32, (TM, 1), 0) % 8
    rowe = lax.broadcasted_iota(jnp.int32, (2 * GROUPS_PER_TILE, 1), 0)
    ne = 2 * GROUPS_PER_TILE
    for blk in range(NBLK):
        cols = slice(blk * LN, (blk + 1) * LN)
        hr, hi = hr_ref[:, cols], hi_ref[:, cols]
        for k, s in enumerate((1, 2, 4)):
            ar, ai = pr_ref[k:k + 1, cols], pi_ref[k:k + 1, cols]
            if reverse:
                m = row8 < 8 - s
                sr, si = pltpu.roll(hr, TM - s, 0), pltpu.roll(hi, TM - s, 0)
            else:
                m = row8 >= s
                sr, si = pltpu.roll(hr, s, 0), pltpu.roll(hi, s, 0)
            sr, si = jnp.where(m, sr, 0.0), jnp.where(m, si, 0.0)
            hr, hi = hr + ar * sr - ai * si, hi + ar * si + ai * sr
        hr_ref[:, cols] = hr
        hi_ref[:, cols] = hi
        edge = 0 if reverse else 7
        nq = LN // 128
        for q in range(nq):
            st_ref[q] = hr[:, q * 128:(q + 1) * 128]
            st_ref[nq + q] = hi[:, q * 128:(q + 1) * 128]
        gr = jnp.concatenate([st_ref[q, pl.ds(edge, GROUPS_PER_TILE, stride=8), :] for q in range(nq)], axis=1)
        gi = jnp.concatenate([st_ref[nq + q, pl.ds(edge, GROUPS_PER_TILE, stride=8), :] for q in range(nq)], axis=1)
        zero = jnp.zeros((GROUPS_PER_TILE, LN), F32)
        if reverse:
            er_ref[0:GROUPS_PER_TILE, :] = gr
            ei_ref[0:GROUPS_PER_TILE, :] = gi
            er_ref[GROUPS_PER_TILE:ne, :] = zero
            ei_ref[GROUPS_PER_TILE:ne, :] = zero
            er_ref[GROUPS_PER_TILE:GROUPS_PER_TILE + 1, :] = cr_ref[:, cols]
            ei_ref[GROUPS_PER_TILE:GROUPS_PER_TILE + 1, :] = ci_ref[:, cols]
        else:
            er_ref[0:GROUPS_PER_TILE, :] = zero
            ei_ref[0:GROUPS_PER_TILE, :] = zero
            er_ref[GROUPS_PER_TILE - 1:GROUPS_PER_TILE, :] = cr_ref[:, cols]
            ei_ref[GROUPS_PER_TILE - 1:GROUPS_PER_TILE, :] = ci_ref[:, cols]
            er_ref[GROUPS_PER_TILE:ne, :] = gr
            ei_ref[GROUPS_PER_TILE:ne, :] = gi
        evr, evi = er_ref[...], ei_ref[...]
        s = 1
        k = 3
        while s < ne:
            ar, ai = pr_ref[k:k + 1, cols], pi_ref[k:k + 1, cols]
            if reverse:
                m = rowe < ne - s
                sr, si = pltpu.roll(evr, ne - s, 0), pltpu.roll(evi, ne - s, 0)
            else:
                m = rowe >= s
                sr, si = pltpu.roll(evr, s, 0), pltpu.roll(evi, s, 0)
            sr, si = jnp.where(m, sr, 0.0), jnp.where(m, si, 0.0)
            evr, evi = evr + ar * sr - ai * si, evi + ar * si + ai * sr
            s *= 2
            k += 1
        er_ref[...] = evr
        ei_ref[...] = evi
        if reverse:
            cr_ref[:, cols] = er_ref[0:1, :]
            ci_ref[:, cols] = ei_ref[0:1, :]
            apr, api = pr_ref[24:32, cols], pi_ref[24:32, cols]
        else:
            cr_ref[:, cols] = er_ref[ne - 1:ne, :]
            ci_ref[:, cols] = ei_ref[ne - 1:ne, :]
            apr, api = pr_ref[16:24, cols], pi_ref[16:24, cols]
        for g in range(GROUPS_PER_TILE):
            e_row = g + 1 if reverse else GROUPS_PER_TILE - 1 + g
            kr, ki = er_ref[e_row:e_row + 1, :], ei_ref[e_row:e_row + 1, :]
            rows = slice(8 * g, 8 * g + 8)
            hr_ref[rows, cols] = hr_ref[rows, cols] + apr * kr - api * ki
            hi_ref[rows, cols] = hi_ref[rows, cols] + apr * ki + api * kr


def _tile_order(kind, nt):
    if kind == "fwd":
        return lambda i: i
    if kind == "bwd":
        return lambda i: jnp.where(i == 0, 0, nt - i)
    if kind == "fwd_adj":
        return lambda i: nt - 1 - i
    if kind == "bwd_adj":
        return lambda i: jnp.where(i == nt - 1, 0, i + 1)
    raise ValueError(kind)


def s5_fwd(p, bb_re, bb_im, ct_re, ct_im, pw_re, pw_im, reverse, name):
    t = p.shape[0]
    nt = t // TM
    order = _tile_order("bwd" if reverse else "fwd", nt)

    def body(x_ref, bbr_ref, bbi_ref, ctr_ref, cti_ref, pr_ref, pi_ref, y_ref, hpr_ref, hpi_ref,
             hr_ref, hi_ref, er_ref, ei_ref, st_ref, cr_ref, ci_ref, c0r_ref, c0i_ref):
        @pl.when(pl.program_id(0) == 0)
        def _():
            cr_ref[...] = jnp.zeros_like(cr_ref)
            ci_ref[...] = jnp.zeros_like(ci_ref)

        c0r_ref[...] = cr_ref[...]
        c0i_ref[...] = ci_ref[...]
        xb = x_ref[...].astype(BF)
        for j in range(NBLK):
            cols = slice(j * LN, (j + 1) * LN)
            hr_ref[:, cols] = _dot(xb[:, j * UB:(j + 1) * UB], bbr_ref[j])
            hi_ref[:, cols] = _dot(xb[:, j * UB:(j + 1) * UB], bbi_ref[j])
        _scan_tile(hr_ref, hi_ref, er_ref, ei_ref, st_ref, cr_ref, ci_ref, pr_ref, pi_ref, reverse)
        rowi = lax.broadcasted_iota(jnp.int32, (TM, 1), 0)
        for j in range(NBLK):
            cols = slice(j * LN, (j + 1) * LN)
            hr, hi = hr_ref[:, cols], hi_ref[:, cols]
            y_ref[:, j * UB:(j + 1) * UB] = _dot(hr.astype(BF), ctr_ref[j]) - _dot(hi.astype(BF), cti_ref[j])
            if reverse:
                first = rowi == TM - 1
                sr, si = pltpu.roll(hr, TM - 1, 0), pltpu.roll(hi, TM - 1, 0)
            else:
                first = rowi == 0
                sr, si = pltpu.roll(hr, 1, 0), pltpu.roll(hi, 1, 0)
            hpr_ref[:, cols] = jnp.where(first, c0r_ref[:, cols], sr)
            hpi_ref[:, cols] = jnp.where(first, c0i_ref[:, cols], si)

    state = lambda: pl.BlockSpec((TM, SW), lambda i: (order(i), 0))
    return _call(body, name=name, grid=(nt,),
                 in_specs=[pl.BlockSpec((TM, AW), lambda i: (order(i), 3)),
                           _full((NBLK, UB, LN)), _full((NBLK, UB, LN)), _full((NBLK, LN, UB)), _full((NBLK, LN, UB)),
                           _full((PW_ROWS, SW)), _full((PW_ROWS, SW))],
                 out_specs=[pl.BlockSpec((TM, AW), lambda i: (order(i), 0)), state(), state()],
                 out_shape=[jax.ShapeDtypeStruct((t, AW), F32), jax.ShapeDtypeStruct((t, SW), F32),
                            jax.ShapeDtypeStruct((t, SW), F32)],
                 scratch=[pltpu.VMEM((TM, SW), F32), pltpu.VMEM((TM, SW), F32),
                          pltpu.VMEM((2 * GROUPS_PER_TILE, LN), F32), pltpu.VMEM((2 * GROUPS_PER_TILE, LN), F32),
                          pltpu.VMEM((2 * LN // 128, TM, 128), F32),
                          pltpu.VMEM((1, SW), F32), pltpu.VMEM((1, SW), F32),
                          pltpu.VMEM((1, SW), F32), pltpu.VMEM((1, SW), F32)])(
        p, bb_re, bb_im, ct_re, ct_im, pw_re, pw_im)


def s5_bwd(p, hp_re, hp_im, dy, bb_re, bb_im, ct_re, ct_im, pw_re, pw_im_conj, a_re, a_im, reverse, name):
    t = p.shape[0]
    nt = t // TM
    order = _tile_order("bwd_adj" if reverse else "fwd_adj", nt)

    def body(x_ref, hpr_ref, hpi_ref, dy_ref, bbr_ref, bbi_ref, ctr_ref, cti_ref, pr_ref, pi_ref, ar_ref, ai_ref,
             dx_ref, dar_ref, dai_ref, dbbr_ref, dbbi_ref, dcr_ref, dci_ref,
             gr_ref, gi_ref, er_ref, ei_ref, st_ref, cr_ref, ci_ref):
        @pl.when(pl.program_id(0) == 0)
        def _():
            for r in (cr_ref, ci_ref, dar_ref, dai_ref, dbbr_ref, dbbi_ref, dcr_ref, dci_ref):
                r[...] = jnp.zeros_like(r)

        xb = x_ref[...].astype(BF)
        dyb = dy_ref[...].astype(BF)
        for j in range(NBLK):
            cols = slice(j * LN, (j + 1) * LN)
            gr_ref[:, cols] = _dot(dyb[:, j * UB:(j + 1) * UB], ctr_ref[j], 1, 1)
            gi_ref[:, cols] = -_dot(dyb[:, j * UB:(j + 1) * UB], cti_ref[j], 1, 1)
        _scan_tile(gr_ref, gi_ref, er_ref, ei_ref, st_ref, cr_ref, ci_ref, pr_ref, pi_ref, not reverse)
        for j in range(NBLK):
            cols = slice(j * LN, (j + 1) * LN)
            xj = xb[:, j * UB:(j + 1) * UB]
            dyj = dyb[:, j * UB:(j + 1) * UB]
            hpr, hpi = hpr_ref[:, cols], hpi_ref[:, cols]
            gr, gi = gr_ref[:, cols], gi_ref[:, cols]
            ar, ai = ar_ref[:, cols], ai_ref[:, cols]
            hr = ar * hpr - ai * hpi + _dot(xj, bbr_ref[j])
            hi = ar * hpi + ai * hpr + _dot(xj, bbi_ref[j])
            dar_ref[:, cols] += jnp.sum(gr * hpr + gi * hpi, axis=0, keepdims=True)
            dai_ref[:, cols] += jnp.sum(gi * hpr - gr * hpi, axis=0, keepdims=True)
            grb, gib = gr.astype(BF), gi.astype(BF)
            dcr_ref[j] += _dot(dyj, hr.astype(BF), 0, 0)
            dci_ref[j] += -_dot(dyj, hi.astype(BF), 0, 0)
            dbbr_ref[j] += _dot(xj, grb, 0, 0)
            dbbi_ref[j] += _dot(xj, gib, 0, 0)
            dx_ref[:, j * UB:(j + 1) * UB] = _dot(grb, bbr_ref[j], 1, 1) + _dot(gib, bbi_ref[j], 1, 1)

    state = lambda: pl.BlockSpec((TM, SW), lambda i: (order(i), 0))
    blockd = lambda: _full((NBLK, UB, LN))
    return _call(body, name=name, grid=(nt,),
                 in_specs=[pl.BlockSpec((TM, AW), lambda i: (order(i), 3)), state(), state(),
                           pl.BlockSpec((TM, AW), lambda i: (order(i), 0)),
                           blockd(), blockd(), _full((NBLK, LN, UB)), _full((NBLK, LN, UB)),
                           _full((PW_ROWS, SW)), _full((PW_ROWS, SW)), _full((1, SW)), _full((1, SW))],
                 out_specs=[pl.BlockSpec((TM, AW), lambda i: (order(i), 0)), _full((1, SW)), _full((1, SW)),
                            blockd(), blockd(), blockd(), blockd()],
                 out_shape=[jax.ShapeDtypeStruct((t, AW), F32), jax.ShapeDtypeStruct((1, SW), F32),
                            jax.ShapeDtypeStruct((1, SW), F32)] + [jax.ShapeDtypeStruct((NBLK, UB, LN), F32)] * 4,
                 scratch=[pltpu.VMEM((TM, SW), F32), pltpu.VMEM((TM, SW), F32),
                          pltpu.VMEM((2 * GROUPS_PER_TILE, LN), F32), pltpu.VMEM((2 * GROUPS_PER_TILE, LN), F32),
                          pltpu.VMEM((2 * LN // 128, TM, 128), F32),
                          pltpu.VMEM((1, SW), F32), pltpu.VMEM((1, SW), F32)])(
        p, hp_re, hp_im, dy, bb_re, bb_im, ct_re, ct_im, pw_re, pw_im_conj, a_re, a_im)


def s5_dx_sum(dy, dsk, dxf, dxb, name):
    t = dy.shape[0]

    def body(dy_ref, d_ref, f_ref, b_ref, o_ref):
        o_ref[...] = (dy_ref[...] * d_ref[...] + f_ref[...] + b_ref[...]).astype(BF)

    tile = pl.BlockSpec((TM, AW), lambda i: (i, 0))
    return _call(body, name=name, grid=(t // TM,), in_specs=[tile, _full((1, AW)), tile, tile], out_specs=tile,
                 out_shape=jax.ShapeDtypeStruct((t, AW), BF))(dy, dsk, dxf, dxb)


XS_BLK = 3 * AW // 128


def _load_perm(refs):
    return jnp.concatenate(
        [jnp.concatenate([ref[pl.ds(r, SCAN_G, stride=SCAN_R), :] for ref in refs], axis=1) for r in range(SCAN_R)],
        axis=0)


def _store_perm(out_ref, val):
    for r in range(SCAN_R):
        for q in range(AW // 128):
            out_ref[q, pl.ds(r, SCAN_G, stride=SCAN_R), :] = val[r * SCAN_G:(r + 1) * SCAN_G, q * 128:(q + 1) * 128]


def _scan_perm(hr_ref, hi_ref, er_ref, ei_ref, cr_ref, ci_ref, pr_ref, pi_ref, reverse, hpr_ref=None, hpi_ref=None):
    gpt = SCAN_G
    nr = SCAN_R
    offsets = list(range(nr))[::-1] if reverse else list(range(nr))
    blocks = [slice(b * LN, (b + 1) * LN) for b in range(NBLK)]
    slab = lambda r: slice(r * gpt, (r + 1) * gpt)
    a1 = [(pr_ref[0:1, c], pi_ref[0:1, c]) for c in blocks]
    x = [None] * NBLK
    for r in offsets:
        for b, c in enumerate(blocks):
            if x[b] is None:
                x[b] = (hr_ref[slab(r), c], hi_ref[slab(r), c])
            else:
                (ar, ai), (xr, xi) = a1[b], x[b]
                x[b] = (hr_ref[slab(r), c] + ar * xr - ai * xi, hi_ref[slab(r), c] + ar * xi + ai * xr)
                hr_ref[slab(r), c] = x[b][0]
                hi_ref[slab(r), c] = x[b][1]
    an = [(pr_ref[nr - 1:nr, c], pi_ref[nr - 1:nr, c]) for c in blocks]
    k = [(cr_ref[:, c], ci_ref[:, c]) for c in blocks]
    for g in (range(gpt - 1, -1, -1) if reverse else range(gpt)):
        for b, c in enumerate(blocks):
            (ar, ai), (kr, ki), (xr, xi) = an[b], k[b], x[b]
            er_ref[g:g + 1, c] = kr
            ei_ref[g:g + 1, c] = ki
            k[b] = (xr[g:g + 1, :] + ar * kr - ai * ki, xi[g:g + 1, :] + ar * ki + ai * kr)
    for b, c in enumerate(blocks):
        cr_ref[:, c] = k[b][0]
        ci_ref[:, c] = k[b][1]
    for r in range(nr):
        prow = nr - 1 - r if reverse else r
        dst = r - 1 if reverse else r + 1
        for c in blocks:
            apr, api = pr_ref[prow:prow + 1, c], pi_ref[prow:prow + 1, c]
            cinr, cini = er_ref[:, c], ei_ref[:, c]
            hr = hr_ref[slab(r), c] + apr * cinr - api * cini
            hi = hi_ref[slab(r), c] + apr * cini + api * cinr
            hr_ref[slab(r), c] = hr
            hi_ref[slab(r), c] = hi
            if hpr_ref is not None:
                if 0 <= dst < nr:
                    hpr_ref[slab(dst), c] = hr
                    hpi_ref[slab(dst), c] = hi
                if r == (nr - 1 if reverse else 0):
                    hpr_ref[slab(r), c] = cinr
                    hpi_ref[slab(r), c] = cini


def s5p_fwd(p, bb_re, bb_im, ct_re, ct_im, pw_re, pw_im, reverse, name):
    t = p.shape[0]
    nt = t // TM
    order = _tile_order("bwd" if reverse else "fwd", nt)
    nq = AW // 128

    def body(*refs):
        x_refs = refs[:nq]
        bbr_ref, bbi_ref, ctr_ref, cti_ref, pr_ref, pi_ref, y_ref, hpr_ref, hpi_ref = refs[nq:nq + 9]
        hr_ref, hi_ref, er_ref, ei_ref, cr_ref, ci_ref = refs[nq + 9:]

        @pl.when(pl.program_id(0) == 0)
        def _():
            cr_ref[...] = jnp.zeros_like(cr_ref)
            ci_ref[...] = jnp.zeros_like(ci_ref)

        xb = _load_perm(x_refs).astype(BF)
        for j in range(NBLK):
            cols = slice(j * LN, (j + 1) * LN)
            hr_ref[:, cols] = _dot(xb[:, j * UB:(j + 1) * UB], bbr_ref[j])
            hi_ref[:, cols] = _dot(xb[:, j * UB:(j + 1) * UB], bbi_ref[j])
        _scan_perm(hr_ref, hi_ref, er_ref, ei_ref, cr_ref, ci_ref, pr_ref, pi_ref, reverse, hpr_ref, hpi_ref)
        y = jnp.concatenate(
            [_dot(hr_ref[:, j * LN:(j + 1) * LN].astype(BF), ctr_ref[j])
             - _dot(hi_ref[:, j * LN:(j + 1) * LN].astype(BF), cti_ref[j]) for j in range(NBLK)], axis=1)
        _store_perm(y_ref, y)

    state = lambda: pl.BlockSpec((TM, SW), lambda i: (order(i), 0))
    xspec = lambda q: pl.BlockSpec((TM, 128), lambda i: (order(i), XS_BLK + q))
    return _call(body, name=name, grid=(nt,),
                 in_specs=[xspec(q) for q in range(nq)]
                 + [_full((NBLK, UB, LN)), _full((NBLK, UB, LN)), _full((NBLK, LN, UB)), _full((NBLK, LN, UB)),
                    _full((PW_ROWS, SW)), _full((PW_ROWS, SW))],
                 out_specs=[pl.BlockSpec((nq, TM, 128), lambda i: (0, order(i), 0)), state(), state()],
                 out_shape=[jax.ShapeDtypeStruct((nq, t, 128), F32), jax.ShapeDtypeStruct((t, SW), F32),
                            jax.ShapeDtypeStruct((t, SW), F32)],
                 scratch=[pltpu.VMEM((TM, SW), F32), pltpu.VMEM((TM, SW), F32),
                          pltpu.VMEM((SCAN_G, SW), F32), pltpu.VMEM((SCAN_G, SW), F32),
                          pltpu.VMEM((1, SW), F32), pltpu.VMEM((1, SW), F32)])(
        *([p] * nq), bb_re, bb_im, ct_re, ct_im, pw_re, pw_im)


def s5p_bwd(p, hp_re, hp_im, dy, bb_re, bb_im, ct_re, ct_im, pw_re, pw_im_conj, a_re, a_im, reverse, name):
    t = p.shape[0]
    nt = t // TM
    order = _tile_order("bwd_adj" if reverse else "fwd_adj", nt)
    nq = AW // 128

    def body(*refs):
        x_refs, dy_refs = refs[:nq], refs[nq:2 * nq]
        (hpr_ref, hpi_ref, bbr_ref, bbi_ref, ctr_ref, cti_ref, pr_ref, pi_ref, ar_ref, ai_ref,
         dx_ref, dar_ref, dai_ref, dbbr_ref, dbbi_ref, dcr_ref, dci_ref,
         gr_ref, gi_ref, er_ref, ei_ref, cr_ref, ci_ref) = refs[2 * nq:]

        @pl.when(pl.program_id(0) == 0)
        def _():
            for r in (cr_ref, ci_ref, dar_ref, dai_ref, dbbr_ref, dbbi_ref, dcr_ref, dci_ref):
                r[...] = jnp.zeros_like(r)

        xb = _load_perm(x_refs).astype(BF)
        dyb = _load_perm(dy_refs).astype(BF)
        for j in range(NBLK):
            cols = slice(j * LN, (j + 1) * LN)
            gr_ref[:, cols] = _dot(dyb[:, j * UB:(j + 1) * UB], ctr_ref[j], 1, 1)
            gi_ref[:, cols] = -_dot(dyb[:, j * UB:(j + 1) * UB], cti_ref[j], 1, 1)
        _scan_perm(gr_ref, gi_ref, er_ref, ei_ref, cr_ref, ci_ref, pr_ref, pi_ref, not reverse)
        dxs = []
        for j in range(NBLK):
            cols = slice(j * LN, (j + 1) * LN)
            xj = xb[:, j * UB:(j + 1) * UB]
            dyj = dyb[:, j * UB:(j + 1) * UB]
            hpr, hpi = hpr_ref[:, cols], hpi_ref[:, cols]
            gr, gi = gr_ref[:, cols], gi_ref[:, cols]
            ar, ai = ar_ref[:, cols], ai_ref[:, cols]
            hr = ar * hpr - ai * hpi + _dot(xj, bbr_ref[j])
            hi = ar * hpi + ai * hpr + _dot(xj, bbi_ref[j])
            dar_ref[:, cols] += jnp.sum(gr * hpr + gi * hpi, axis=0, keepdims=True)
            dai_ref[:, cols] += jnp.sum(gi * hpr - gr * hpi, axis=0, keepdims=True)
            grb, gib = gr.astype(BF), gi.astype(BF)
            dcr_ref[j] += _dot(dyj, hr.astype(BF), 0, 0)
            dci_ref[j] += -_dot(dyj, hi.astype(BF), 0, 0)
            dbbr_ref[j] += _dot(xj, grb, 0, 0)
            dbbi_ref[j] += _dot(xj, gib, 0, 0)
            dxs.append(_dot(grb, bbr_ref[j], 1, 1) + _dot(gib, bbi_ref[j], 1, 1))
        _store_perm(dx_ref, jnp.concatenate(dxs, axis=1))

    state = lambda: pl.BlockSpec((TM, SW), lambda i: (order(i), 0))
    blockd = lambda: _full((NBLK, UB, LN))
    xspec = lambda q: pl.BlockSpec((TM, 128), lambda i: (order(i), XS_BLK + q))
    dyspec = lambda q: pl.BlockSpec((TM, 128), lambda i: (order(i), q))
    return _call(body, name=name, grid=(nt,),
                 in_specs=[xspec(q) for q in range(nq)] + [dyspec(q) for q in range(nq)]
                 + [state(), state(), blockd(), blockd(), _full((NBLK, LN, UB)), _full((NBLK, LN, UB)),
                    _full((PW_ROWS, SW)), _full((PW_ROWS, SW)), _full((1, SW)), _full((1, SW))],
                 out_specs=[pl.BlockSpec((nq, TM, 128), lambda i: (0, order(i), 0)), _full((1, SW)), _full((1, SW)),
                            blockd(), blockd(), blockd(), blockd()],
                 out_shape=[jax.ShapeDtypeStruct((nq, t, 128), F32), jax.ShapeDtypeStruct((1, SW), F32),
                            jax.ShapeDtypeStruct((1, SW), F32)] + [jax.ShapeDtypeStruct((NBLK, UB, LN), F32)] * 4,
                 scratch=[pltpu.VMEM((TM, SW), F32), pltpu.VMEM((TM, SW), F32),
                          pltpu.VMEM((SCAN_G, SW), F32), pltpu.VMEM((SCAN_G, SW), F32),
                          pltpu.VMEM((1, SW), F32), pltpu.VMEM((1, SW), F32)])(
        *([p] * nq), *([dy] * nq), hp_re, hp_im, bb_re, bb_im, ct_re, ct_im, pw_re, pw_im_conj, a_re, a_im)


def _scan2(br_ref, bi_ref, or_ref, oi_ref, h_off, cin_off, er_ref, ei_ref, cr_ref, ci_ref, pr_ref, pi_ref, reverse):
    gpt = SCAN_G
    nr = SCAN_R
    offsets = list(range(nr))[::-1] if reverse else list(range(nr))
    blocks = [slice(b * LN, (b + 1) * LN) for b in range(NBLK)]
    slab = lambda r: slice(r * gpt, (r + 1) * gpt)
    a1 = [(pr_ref[0:1, c], pi_ref[0:1, c]) for c in blocks]
    x = [None] * NBLK
    for r in offsets:
        for b, c in enumerate(blocks):
            if x[b] is None:
                x[b] = (br_ref[slab(r), c], bi_ref[slab(r), c])
            else:
                (ar, ai), (xr, xi) = a1[b], x[b]
                x[b] = (br_ref[slab(r), c] + ar * xr - ai * xi, bi_ref[slab(r), c] + ar * xi + ai * xr)
    an = [(pr_ref[nr - 1:nr, c], pi_ref[nr - 1:nr, c]) for c in blocks]
    k = [(cr_ref[:, c], ci_ref[:, c]) for c in blocks]
    for g in (range(gpt - 1, -1, -1) if reverse else range(gpt)):
        for b, c in enumerate(blocks):
            (ar, ai), (kr, ki), (xr, xi) = an[b], k[b], x[b]
            er_ref[g:g + 1, c] = kr
            ei_ref[g:g + 1, c] = ki
            k[b] = (xr[g:g + 1, :] + ar * kr - ai * ki, xi[g:g + 1, :] + ar * ki + ai * kr)
    for b, c in enumerate(blocks):
        cr_ref[:, c] = k[b][0]
        ci_ref[:, c] = k[b][1]
        x[b] = (er_ref[:, c], ei_ref[:, c])
        if cin_off is not None:
            or_ref[cin_off:cin_off + gpt, c] = x[b][0]
            oi_ref[cin_off:cin_off + gpt, c] = x[b][1]
    for r in offsets:
        for b, c in enumerate(blocks):
            (ar, ai), (xr, xi) = a1[b], x[b]
            x[b] = (br_ref[slab(r), c] + ar * xr - ai * xi, bi_ref[slab(r), c] + ar * xi + ai * xr)
            or_ref[h_off + r * gpt:h_off + (r + 1) * gpt, c] = x[b][0]
            oi_ref[h_off + r * gpt:h_off + (r + 1) * gpt, c] = x[b][1]


HS_ROWS = TM + SCAN_G


def _hs_offsets(reverse):
    return (0, SCAN_G) if reverse else (SCAN_G, 0)


def s5q_fwd(p, bb_re, bb_im, ct_re, ct_im, pw_re, pw_im, reverse, name):
    t = p.shape[0]
    nt = t // TM
    order = _tile_order("bwd" if reverse else "fwd", nt)
    nq = AW // 128
    h_off, p_off = _hs_offsets(reverse)

    def body(*refs):
        x_refs = refs[:nq]
        bbr_ref, bbi_ref, ctr_ref, cti_ref, pr_ref, pi_ref, y_ref, hsr_ref, hsi_ref = refs[nq:nq + 9]
        br_ref, bi_ref, er_ref, ei_ref, cr_ref, ci_ref = refs[nq + 9:]

        @pl.when(pl.program_id(0) == 0)
        def _():
            cr_ref[...] = jnp.zeros_like(cr_ref)
            ci_ref[...] = jnp.zeros_like(ci_ref)

        xb = _load_perm(x_refs).astype(BF)
        for j in range(NBLK):
            cols = slice(j * LN, (j + 1) * LN)
            br_ref[:, cols] = _dot(xb[:, j * UB:(j + 1) * UB], bbr_ref[j])
            bi_ref[:, cols] = _dot(xb[:, j * UB:(j + 1) * UB], bbi_ref[j])
        _scan2(br_ref, bi_ref, hsr_ref, hsi_ref, h_off, TM if reverse else 0, er_ref, ei_ref, cr_ref, ci_ref,
               pr_ref, pi_ref, reverse)
        y = jnp.concatenate(
            [_dot(hsr_ref[h_off:h_off + TM, j * LN:(j + 1) * LN].astype(BF), ctr_ref[j])
             - _dot(hsi_ref[h_off:h_off + TM, j * LN:(j + 1) * LN].astype(BF), cti_ref[j]) for j in range(NBLK)], axis=1)
        _store_perm(y_ref, y)

    state = lambda: pl.BlockSpec((HS_ROWS, SW), lambda i: (order(i), 0))
    xspec = lambda q: pl.BlockSpec((TM, 128), lambda i: (order(i), XS_BLK + q))
    return _call(body, name=name, grid=(nt,),
                 in_specs=[xspec(q) for q in range(nq)]
                 + [_full((NBLK, UB, LN)), _full((NBLK, UB, LN)), _full((NBLK, LN, UB)), _full((NBLK, LN, UB)),
                    _full((PW_ROWS, SW)), _full((PW_ROWS, SW))],
                 out_specs=[pl.BlockSpec((nq, TM, 128), lambda i: (0, order(i), 0)), state(), state()],
                 out_shape=[jax.ShapeDtypeStruct((nq, t, 128), F32), jax.ShapeDtypeStruct((nt * HS_ROWS, SW), F32),
                            jax.ShapeDtypeStruct((nt * HS_ROWS, SW), F32)],
                 scratch=[pltpu.VMEM((TM, SW), F32), pltpu.VMEM((TM, SW), F32),
                          pltpu.VMEM((SCAN_G, SW), F32), pltpu.VMEM((SCAN_G, SW), F32),
                          pltpu.VMEM((1, SW), F32), pltpu.VMEM((1, SW), F32)])(
        *([p] * nq), bb_re, bb_im, ct_re, ct_im, pw_re, pw_im)


def s5q_bwd(p, hs_re, hs_im, dy, bb_re, bb_im, ct_re, ct_im, pw_re, pw_im_conj, reverse, name):
    t = p.shape[0]
    nt = t // TM
    order = _tile_order("bwd_adj" if reverse else "fwd_adj", nt)
    nq = AW // 128
    h_off, p_off = _hs_offsets(reverse)

    def body(*refs):
        x_refs, dy_refs = refs[:nq], refs[nq:2 * nq]
        (hsr_ref, hsi_ref, bbr_ref, bbi_ref, ctr_ref, cti_ref, pr_ref, pi_ref,
         dx_ref, dar_ref, dai_ref, dbbr_ref, dbbi_ref, dcr_ref, dci_ref,
         qr_ref, qi_ref, gr_ref, gi_ref, er_ref, ei_ref, cr_ref, ci_ref) = refs[2 * nq:]

        @pl.when(pl.program_id(0) == 0)
        def _():
            for r in (cr_ref, ci_ref, dar_ref, dai_ref, dbbr_ref, dbbi_ref, dcr_ref, dci_ref):
                r[...] = jnp.zeros_like(r)

        xb = _load_perm(x_refs).astype(BF)
        dyb = _load_perm(dy_refs).astype(BF)
        for j in range(NBLK):
            cols = slice(j * LN, (j + 1) * LN)
            qr_ref[:, cols] = _dot(dyb[:, j * UB:(j + 1) * UB], ctr_ref[j], 1, 1)
            qi_ref[:, cols] = -_dot(dyb[:, j * UB:(j + 1) * UB], cti_ref[j], 1, 1)
        _scan2(qr_ref, qi_ref, gr_ref, gi_ref, 0, None, er_ref, ei_ref, cr_ref, ci_ref, pr_ref, pi_ref, not reverse)
        dxs = []
        for j in range(NBLK):
            cols = slice(j * LN, (j + 1) * LN)
            xj = xb[:, j * UB:(j + 1) * UB]
            dyj = dyb[:, j * UB:(j + 1) * UB]
            hpr, hpi = hsr_ref[p_off:p_off + TM, cols], hsi_ref[p_off:p_off + TM, cols]
            gr, gi = gr_ref[:, cols], gi_ref[:, cols]
            dar_ref[:, cols] += jnp.sum(gr * hpr + gi * hpi, axis=0, keepdims=True)
            dai_ref[:, cols] += jnp.sum(gi * hpr - gr * hpi, axis=0, keepdims=True)
            grb, gib = gr.astype(BF), gi.astype(BF)
            dcr_ref[j] += _dot(dyj, hsr_ref[h_off:h_off + TM, cols].astype(BF), 0, 0)
            dci_ref[j] += -_dot(dyj, hsi_ref[h_off:h_off + TM, cols].astype(BF), 0, 0)
            dbbr_ref[j] += _dot(xj, grb, 0, 0)
            dbbi_ref[j] += _dot(xj, gib, 0, 0)
            dxs.append(_dot(grb, bbr_ref[j], 1, 1) + _dot(gib, bbi_ref[j], 1, 1))
        _store_perm(dx_ref, jnp.concatenate(dxs, axis=1))

    state = lambda: pl.BlockSpec((HS_ROWS, SW), lambda i: (order(i), 0))
    blockd = lambda: _full((NBLK, UB, LN))
    xspec = lambda q: pl.BlockSpec((TM, 128), lambda i: (order(i), XS_BLK + q))
    dyspec = lambda q: pl.BlockSpec((TM, 128), lambda i: (order(i), q))
    return _call(body, name=name, grid=(nt,),
                 in_specs=[xspec(q) for q in range(nq)] + [dyspec(q) for q in range(nq)]
                 + [state(), state(), blockd(), blockd(), _full((NBLK, LN, UB)), _full((NBLK, LN, UB)),
                    _full((PW_ROWS, SW)), _full((PW_ROWS, SW))],
                 out_specs=[pl.BlockSpec((nq, TM, 128), lambda i: (0, order(i), 0)), _full((1, SW)), _full((1, SW)),
                            blockd(), blockd(), blockd(), blockd()],
                 out_shape=[jax.ShapeDtypeStruct((nq, t, 128), F32), jax.ShapeDtypeStruct((1, SW), F32),
                            jax.ShapeDtypeStruct((1, SW), F32)] + [jax.ShapeDtypeStruct((NBLK, UB, LN), F32)] * 4,
                 scratch=[pltpu.VMEM((TM, SW), F32), pltpu.VMEM((TM, SW), F32),
                          pltpu.VMEM((TM, SW), F32), pltpu.VMEM((TM, SW), F32),
                          pltpu.VMEM((SCAN_G, SW), F32), pltpu.VMEM((SCAN_G, SW), F32),
                          pltpu.VMEM((1, SW), F32), pltpu.VMEM((1, SW), F32)])(
        *([p] * nq), *([dy] * nq), hs_re, hs_im, bb_re, bb_im, ct_re, ct_im, pw_re, pw_im_conj)


def s5p_dx_sum(dy, dsk, dxf, dxb, dp, name):
    t = dy.shape[0]
    nq = AW // 128

    def body(dy_ref, d_ref, f_ref, b_ref, dp_ref, o_ref):
        o_ref[...] = (dy_ref[...] * d_ref[...] + _cat_lanes(f_ref) + _cat_lanes(b_ref)).astype(BF)

    tile = pl.BlockSpec((TM, AW), lambda i: (i, 0))
    blk4 = pl.BlockSpec((nq, TM, 128), lambda i: (0, i, 0))
    return pl.pallas_call(
        body, name=name, grid=(t // TM,),
        in_specs=[tile, _full((1, AW)), blk4, blk4, pl.BlockSpec(memory_space=pl.ANY)],
        out_specs=pl.BlockSpec((TM, AW), lambda i: (i, 3)), out_shape=jax.ShapeDtypeStruct(dp.shape, dp.dtype),
        input_output_aliases={4: 0},
        compiler_params=pltpu.CompilerParams(vmem_limit_bytes=VMEM_LIMIT_BYTES))(dy, dsk, dxf, dxb, dp)


SCALE = HD ** -0.5
NHEAD_NORM = NQ + NKV


def _partner(x):
    half0 = (lax.broadcasted_iota(jnp.int32, (1, HD), 1) % 64) < 32
    return jnp.where(half0, pltpu.roll(x, HD - 32, 1), pltpu.roll(x, 32, 1))


def attn_prep(p, qg, kg, cos, sins, name):
    t = p.shape[0]

    def body(p_ref, qg_ref, kg_ref, cos_ref, sin_ref, o_ref):
        cv, sv = cos_ref[...], sin_ref[...]
        for h in range(NHEAD_NORM):
            cols = slice(h * HD, (h + 1) * HD)
            blk = p_ref[:, cols]
            r = lax.rsqrt(jnp.mean(blk * blk, axis=-1, keepdims=True) + EPS)
            xn = blk * r * (qg_ref[...] if h < NQ else kg_ref[...])
            rot = xn * cv + _partner(xn) * sv
            o_ref[:, cols] = ((rot * SCALE) if h < NQ else rot).astype(BF)
        vcols = slice(NHEAD_NORM * HD, (NHEAD_NORM + NKV) * HD)
        o_ref[:, vcols] = p_ref[:, vcols].astype(BF)

    w = (NHEAD_NORM + NKV) * HD
    tile = lambda ww: pl.BlockSpec((TM, ww), lambda i: (i, 0))
    return _call(body, name=name, grid=(t // TM,),
                 in_specs=[tile(w), _full((1, HD)), _full((1, HD)), tile(HD), tile(HD)],
                 out_specs=tile(w), out_shape=jax.ShapeDtypeStruct((t, w), BF))(p, qg, kg, cos, sins)


def attn_prep_bwd(p, dq, dk, dv, qg, kg, cos, sins, dp, name):
    t = p.shape[0]

    def body(p_ref, dq_ref, dk_ref, dv_ref, qg_ref, kg_ref, cos_ref, sin_ref, dp_ref, o_ref, dqg_ref, dkg_ref):
        @pl.when(pl.program_id(0) == 0)
        def _():
            dqg_ref[...] = jnp.zeros_like(dqg_ref)
            dkg_ref[...] = jnp.zeros_like(dkg_ref)

        cv, sv = cos_ref[...], sin_ref[...]
        for h in range(NHEAD_NORM):
            cols = slice(h * HD, (h + 1) * HD)
            blk = p_ref[:, cols]
            r = lax.rsqrt(jnp.mean(blk * blk, axis=-1, keepdims=True) + EPS)
            xh = blk * r
            if h < NQ:
                drot = dq_ref[:, cols] * SCALE
                gv, dg_ref = qg_ref[...], dqg_ref
            else:
                drot = dk_ref[:, (h - NQ) * HD:(h - NQ + 1) * HD]
                gv, dg_ref = kg_ref[...], dkg_ref
            dxn = drot * cv + _partner(drot * sv)
            dg_ref[...] += jnp.sum(dxn * xh, axis=0, keepdims=True)
            dxh = dxn * gv
            o_ref[:, cols] = (r * (dxh - xh * jnp.mean(dxh * xh, axis=-1, keepdims=True))).astype(BF)
        o_ref[:, NHEAD_NORM * HD:(NHEAD_NORM + NKV) * HD] = dv_ref[...].astype(BF)

    w = (NHEAD_NORM + NKV) * HD
    tile = lambda ww: pl.BlockSpec((TM, ww), lambda i: (i, 0))
    return pl.pallas_call(
        body, name=name, grid=(t // TM,),
        in_specs=[tile(w), tile(NQ * HD), tile(NKV * HD), tile(NKV * HD), _full((1, HD)), _full((1, HD)),
                  tile(HD), tile(HD), pl.BlockSpec(memory_space=pl.ANY)],
        out_specs=[tile(w), _full((1, HD)), _full((1, HD))],
        out_shape=[jax.ShapeDtypeStruct(dp.shape, dp.dtype), jax.ShapeDtypeStruct((1, HD), F32),
                   jax.ShapeDtypeStruct((1, HD), F32)],
        input_output_aliases={8: 0},
        compiler_params=pltpu.CompilerParams(vmem_limit_bytes=VMEM_LIMIT_BYTES))(p, dq, dk, dv, qg, kg, cos, sins, dp)


KCOL = NQ
VCOL = NQ + NKV
GCOL = (NQ + 2 * NKV)
QPK = NQ // NKV
ATT_KCHUNK = 512


def attn_fwd(qkv, p, name):
    t = qkv.shape[0]

    def body(q_ref, k_ref, v_ref, g_ref, o_ref, mix_ref, lse_ref):
        def attend(nk):
            q = q_ref[...]
            chunks = [(k0, min(k0 + 2 * ATT_KCHUNK, nk)) for k0 in range(0, nk, 2 * ATT_KCHUNK)]
            s_next = _dot(q, k_ref[chunks[0][0]:chunks[0][1], :], 1, 1)
            m = l = acc = None
            for ci, (k0, k1) in enumerate(chunks):
                s = s_next
                if ci + 1 < len(chunks):
                    s_next = _dot(q, k_ref[chunks[ci + 1][0]:chunks[ci + 1][1], :], 1, 1)
                mc = jnp.max(s, axis=-1, keepdims=True)
                m_new = mc if m is None else jnp.maximum(m, mc)
                pe = jnp.exp(s - m_new)
                lc = jnp.sum(pe, axis=-1, keepdims=True)
                pv = _dot(pe.astype(BF), v_ref[k0:k1, :])
                if m is None:
                    l, acc = lc, pv
                else:
                    alpha = jnp.exp(m - m_new)
                    l, acc = alpha * l + lc, alpha * acc + pv
                m = m_new
            o = acc / l
            gt = g_ref[...]
            o_ref[...] = o
            mix_ref[...] = (o * (gt * _sig(gt))).astype(BF)
            lse_ref[...] = jnp.broadcast_to(m + jnp.log(l), (TM, HD))

        pl.when(pl.program_id(1) == 0)(lambda: attend(NC))
        pl.when(pl.program_id(1) > 0)(lambda: attend(t))

    blk = pl.BlockSpec((TM, HD), lambda h, i: (i, h))
    return _call(body, name=name, grid=(NQ, t // TM),
                 in_specs=[blk, pl.BlockSpec((t, HD), lambda h, i: (0, KCOL + h // QPK)),
                           pl.BlockSpec((t, HD), lambda h, i: (0, VCOL + h // QPK)),
                           pl.BlockSpec((TM, HD), lambda h, i: (i, GCOL + h))],
                 out_specs=[blk, blk, blk],
                 out_shape=[jax.ShapeDtypeStruct((t, NQ * HD), F32), jax.ShapeDtypeStruct((t, NQ * HD), BF),
                            jax.ShapeDtypeStruct((t, NQ * HD), F32)])(qkv, qkv, qkv, p)


def attn_bwd(qkv, p, dmix, o, lse, name):
    t = qkv.shape[0]

    def body(q_ref, k_ref, v_ref, g_ref, dm_ref, o_ref, lse_ref, dq_ref, dg_ref, dk_ref, dv_ref):
        i = pl.program_id(2)

        @pl.when((pl.program_id(1) == 0) & (i == 0))
        def _():
            dk_ref[...] = jnp.zeros_like(dk_ref)
            dv_ref[...] = jnp.zeros_like(dv_ref)

        gt = g_ref[...]
        sg = _sig(gt)
        ov = o_ref[...]
        dmv = dm_ref[...]
        dg_ref[...] = (dmv * ov * (sg * (1.0 + gt * (1.0 - sg)))).astype(BF)
        do = dmv * (gt * sg)
        dr = jnp.sum(do * ov, axis=-1, keepdims=True)
        dob = do.astype(BF)

        def bwd(nk):
            q = q_ref[...]
            lse = lse_ref[:, 0:1]
            chunks = [slice(k0, min(k0 + ATT_KCHUNK, nk)) for k0 in range(0, nk, ATT_KCHUNK)]
            nxt = (_dot(q, k_ref[chunks[0], :], 1, 1), _dot(dob, v_ref[chunks[0], :], 1, 1))
            dq = None
            for ci, keys in enumerate(chunks):
                s, dp = nxt
                if ci + 1 < len(chunks):
                    nxt = (_dot(q, k_ref[chunks[ci + 1], :], 1, 1), _dot(dob, v_ref[chunks[ci + 1], :], 1, 1))
                pe = jnp.exp(s - lse)
                dsb = (pe * (dp - dr)).astype(BF)
                part = _dot(dsb, k_ref[keys, :])
                dq = part if dq is None else dq + part
                dv_ref[keys, :] += _dot(pe.astype(BF), dob, 0, 0)
                dk_ref[keys, :] += _dot(dsb, q, 0, 0)
            dq_ref[...] = dq

        pl.when(i == 0)(lambda: bwd(NC))
        pl.when(i > 0)(lambda: bwd(t))

    blk = pl.BlockSpec((TM, HD), lambda kv, g, i: (i, kv * QPK + g))
    acc = pl.BlockSpec((t, HD), lambda kv, g, i: (0, kv))
    return _call(body, name=name, grid=(NKV, QPK, t // TM),
                 in_specs=[blk, pl.BlockSpec((t, HD), lambda kv, g, i: (0, KCOL + kv)),
                           pl.BlockSpec((t, HD), lambda kv, g, i: (0, VCOL + kv)),
                           pl.BlockSpec((TM, HD), lambda kv, g, i: (i, GCOL + kv * QPK + g)), blk, blk, blk],
                 out_specs=[blk, pl.BlockSpec((TM, HD), lambda kv, g, i: (i, GCOL + kv * QPK + g)), acc, acc],
                 out_shape=[jax.ShapeDtypeStruct((t, NQ * HD), F32), jax.ShapeDtypeStruct((t, (GCOL + NQ) * HD), BF),
                            jax.ShapeDtypeStruct((t, NKV * HD), F32), jax.ShapeDtypeStruct((t, NKV * HD), F32)])(
        qkv, qkv, qkv, p, dmix, o, lse)


def _row_tile(rows, row_bytes, cap=2 * 1024 * 1024):
    if rows * row_bytes <= cap or rows % 8:
        return rows
    tr = rows
    while tr * row_bytes > cap and tr % 16 == 0:
        tr //= 2
    return tr


def _adamw_update(w_ref, g_ref, m_ref, v_ref, d_ref, nm_ref, nv_ref):
    gv = g_ref[...]
    m2 = ADAM_B1 * m_ref[...] + (1.0 - ADAM_B1) * gv
    v2 = ADAM_B2 * v_ref[...] + (1.0 - ADAM_B2) * (gv * gv)
    mh = m2 / (1.0 - ADAM_B1 ** ADAM_STEP)
    vh = v2 / (1.0 - ADAM_B2 ** ADAM_STEP)
    d_ref[...] = -ADAM_LR * (mh / (jnp.sqrt(vh) + ADAM_EPS) + ADAM_WD * w_ref[...])
    nm_ref[...] = m2
    nv_ref[...] = v2


def adamw_many(ws, gs, ms, vs, name):
    n = len(ws)

    def body(*refs):
        for k in range(n):
            _adamw_update(*[refs[j * n + k] for j in range(7)])

    shapes = [jax.ShapeDtypeStruct(w.shape, F32) for w in ws]
    res = _call(body, name=name, out_shape=shapes * 3)(*ws, *gs, *ms, *vs)
    return res[:n], res[n:2 * n], res[2 * n:]


def adamw(w, g, m, v, name):
    r, cdim = w.shape
    tr = _row_tile(r, 4 * max(cdim, 128))

    def body(w_ref, g_ref, m_ref, v_ref, d_ref, nm_ref, nv_ref):
        _adamw_update(w_ref, g_ref, m_ref, v_ref, d_ref, nm_ref, nv_ref)

    tile = pl.BlockSpec((tr, cdim), lambda i: (i, 0))
    sh = jax.ShapeDtypeStruct((r, cdim), F32)
    return _call(body, name=name, grid=(r // tr,), in_specs=[tile] * 4, out_specs=[tile] * 3,
                 out_shape=[sh, sh, sh])(w, g, m, v)


def sum_lead(a, name, out_dtype=F32):
    n, r, cdim = a.shape
    tr = _row_tile(r, 4 * n * max(cdim, 128))

    def body(a_ref, o_ref):
        acc = a_ref[0].astype(F32)
        for k in range(1, n):
            acc = acc + a_ref[k].astype(F32)
        o_ref[...] = acc.astype(o_ref.dtype)

    return _call(body, name=name, grid=(r // tr,),
                 in_specs=[pl.BlockSpec((n, tr, cdim), lambda i: (0, i, 0))],
                 out_specs=pl.BlockSpec((tr, cdim), lambda i: (i, 0)),
                 out_shape=jax.ShapeDtypeStruct((r, cdim), out_dtype))(a)


_FLIPS = {"xy": [(1, 0, 0), (0, 1, 0), (1, 1, 0)], "c": [(0, 0, 1)],
          "all": [(0, 0, 1), (0, 1, 0), (0, 1, 1), (1, 0, 0), (1, 0, 1), (1, 1, 0), (1, 1, 1)]}
_GROUP_SIZE = {"xy": 4, "c": 2, "all": 8}


def _group_index(group, x, y, c):
    return {"xy": 2 * x + y, "c": c, "all": 4 * x + 2 * y + c}[group]


def exchange(items, name):
    plan = []
    for arr, group, kind in items:
        chunk = arr.shape if kind == "gather" else arr.shape[1:]
        plan.append((group, kind, chunk))
    ncopy = sum(len(_FLIPS[g]) for g, _, _ in plan)
    nitem = len(plan)

    def body(*refs):
        srcs, dsts = refs[:nitem], refs[nitem:2 * nitem]
        send_sems, recv_sems, local_sems = refs[2 * nitem:]
        x, y, c = lax.axis_index("x"), lax.axis_index("y"), lax.axis_index("c")
        sends, recvs, locals_ = [], [], []
        n = 0
        for k, (group, kind, _) in enumerate(plan):
            me = _group_index(group, x, y, c)
            own = srcs[k] if kind == "gather" else srcs[k].at[me]
            locals_.append(pltpu.make_async_copy(own, dsts[k].at[me], local_sems.at[k]))
            for fx, fy, fc in _FLIPS[group]:
                px, py, pc = (1 - x if fx else x), (1 - y if fy else y), (1 - c if fc else c)
                peer = _group_index(group, px, py, pc)
                src = srcs[k] if kind == "gather" else srcs[k].at[peer]
                sends.append(pltpu.make_async_remote_copy(
                    src_ref=src, dst_ref=dsts[k].at[me], send_sem=send_sems.at[n], recv_sem=recv_sems.at[n],
                    device_id=(px, py, pc), device_id_type=MESH))
                recvs.append(pltpu.make_async_remote_copy(
                    src_ref=src, dst_ref=dsts[k].at[peer], send_sem=send_sems.at[n], recv_sem=recv_sems.at[n],
                    device_id=(px, py, pc), device_id_type=MESH))
                n += 1
        for cp in locals_ + sends:
            cp.start()
        for cp in recvs:
            cp.wait_recv()
        for cp in sends:
            cp.wait_send()
        for cp in locals_:
            cp.wait()

    anyspec = pl.BlockSpec(memory_space=pl.ANY)
    outs = [jax.ShapeDtypeStruct((_GROUP_SIZE[g],) + tuple(chunk), arr.dtype)
            for (arr, _, _), (g, _, chunk) in zip(items, plan)]
    res = pl.pallas_call(
        body, name=name, out_shape=outs, in_specs=[anyspec] * nitem, out_specs=[anyspec] * nitem,
        scratch_shapes=[pltpu.SemaphoreType.DMA((ncopy,)), pltpu.SemaphoreType.DMA((ncopy,)),
                        pltpu.SemaphoreType.DMA((nitem,))],
        compiler_params=pltpu.CompilerParams(has_side_effects=True))(*[a for a, _, _ in items])
    return list(res)


D2D_PIECES = 4


def d2d(items, name):
    n = len(items)
    swaps = [k for k, (_, kind) in enumerate(items) if kind == "swap"]

    def pieces_of(rows):
        npc = D2D_PIECES if rows % (8 * D2D_PIECES) == 0 else 1
        return npc, rows // npc

    ncopy = sum(pieces_of(a.shape[0] if kind == "gather" else a.shape[1])[0] for a, kind in items)

    def body(*refs):
        srcs, outs = refs[:n], refs[n:2 * n]
        stages = dict(zip(swaps, refs[2 * n:2 * n + len(swaps)]))
        send_sems, recv_sems, local_sems = refs[2 * n + len(swaps):]
        x, y, c = lax.axis_index("x"), lax.axis_index("y"), lax.axis_index("c")
        sib = (x, y, 1 - c)

        def remote(src, dst, q):
            return pltpu.make_async_remote_copy(src_ref=src, dst_ref=dst, send_sem=send_sems.at[q],
                                                recv_sem=recv_sems.at[q], device_id=sib, device_id_type=MESH)

        copies = []
        q = 0
        for k, (arr, kind) in enumerate(items):
            npc, pr = pieces_of(arr.shape[0] if kind == "gather" else arr.shape[1])
            for pc in range(npc):
                rs = pl.ds(pc * pr, pr)
                if kind == "gather":
                    mine, theirs = outs[k].at[c, rs], outs[k].at[1 - c, rs]
                    copies.append((pltpu.make_async_copy(srcs[k].at[rs], mine, local_sems.at[q]),
                                   remote(mine, mine, q), remote(theirs, theirs, q)))
                else:
                    stage, land = stages[k].at[rs], outs[k].at[rs]
                    copies.append((pltpu.make_async_copy(srcs[k].at[1 - c, rs], stage, local_sems.at[q]),
                                   remote(stage, land, q), remote(stage, land, q)))
                q += 1
        for loc, _, _ in copies:
            loc.start()
        for loc, send, _ in copies:
            loc.wait()
            send.start()
        for _, _, recv in copies:
            recv.wait_recv()
        for _, send, _ in copies:
            send.wait_send()

    outs = [jax.ShapeDtypeStruct((2,) + a.shape if kind == "gather" else a.shape[1:], a.dtype) for a, kind in items]
    res = pl.pallas_call(
        body, name=name, out_shape=outs, in_specs=[pl.BlockSpec(memory_space=pl.ANY)] * n,
        out_specs=[pl.BlockSpec(memory_space=pltpu.VMEM)] * n,
        scratch_shapes=[pltpu.VMEM(items[k][0].shape[1:], items[k][0].dtype) for k in swaps]
        + [pltpu.SemaphoreType.DMA((ncopy,)), pltpu.SemaphoreType.DMA((ncopy,)), pltpu.SemaphoreType.DMA((ncopy,))],
        compiler_params=pltpu.CompilerParams(has_side_effects=True, vmem_limit_bytes=VMEM_LIMIT_BYTES))(
        *[a for a, _ in items])
    return list(res)


def sum_own(pair, got, name, out_dtype=F32):
    _, r, cdim = pair.shape
    tr = _row_tile(r, 4 * 2 * max(cdim, 128))

    def body(c_ref, p_ref, g_ref, o_ref):
        o_ref[...] = (p_ref[0] + g_ref[...]).astype(o_ref.dtype)

    me = lax.axis_index("c").astype(jnp.int32).reshape(1)
    return pl.pallas_call(
        body, name=name, out_shape=jax.ShapeDtypeStruct((r, cdim), out_dtype),
        grid_spec=pltpu.PrefetchScalarGridSpec(
            num_scalar_prefetch=1, grid=(r // tr,),
            in_specs=[pl.BlockSpec((1, tr, cdim), lambda i, c_ref: (c_ref[0], i, 0)),
                      pl.BlockSpec((tr, cdim), lambda i, c_ref: (i, 0))],
            out_specs=pl.BlockSpec((tr, cdim), lambda i, c_ref: (i, 0))),
        compiler_params=pltpu.CompilerParams(vmem_limit_bytes=VMEM_LIMIT_BYTES))(me, pair, got)


_SMALL = ["c_ctx", "norm_g", "b_mod", "gm_v_g", "gm_w_s", "gm_b_s", "s5_lam_re", "s5_lam_im", "s5_log_dt",
          "s5_b_re", "s5_b_im", "s5_c_re", "s5_c_im", "s5_d", "s5_b_glu", "q_norm_g", "k_norm_g", "final_g"]
_BIG = ["we_in", "we_out", "s5_w_glu", "wo_in", "wo_out"]
_WEIGHTS = ["c_ctx", "norm_g", "w_mod", "b_mod", "we_in", "we_out", "gm_v_g", "gm_w_s", "gm_b_s", "s5_lam_re",
            "s5_lam_im", "s5_log_dt", "s5_b_re", "s5_b_im", "s5_c_re", "s5_c_im", "s5_d", "s5_w_glu", "s5_b_glu",
            "wo_in", "wo_out", "q_norm_g", "k_norm_g", "final_g"]
_SMALL_ALIGN = 8 * 8 * 128


def _rope_tables(n_lat):
    rows = n_lat // GRID_W
    row = jnp.repeat(jnp.arange(rows), GRID_W)
    col = jnp.tile(jnp.arange(GRID_W), rows)
    freqs = ROPE_THETA ** (-jnp.arange(HD // 4, dtype=F32) / (HD // 4))
    ar, ac = row[:, None] * freqs, col[:, None] * freqs
    cos = jnp.concatenate([jnp.cos(ar), jnp.cos(ar), jnp.cos(ac), jnp.cos(ac)], axis=1)
    sins = jnp.concatenate([-jnp.sin(ar), jnp.sin(ar), -jnp.sin(ac), jnp.sin(ac)], axis=1)
    cos = jnp.concatenate([jnp.ones((NC, HD), F32), cos], axis=0)
    sins = jnp.concatenate([jnp.zeros((NC, HD), F32), sins], axis=0)
    return cos, sins


def _block_diag(v, transpose):
    gpb = SG // NBLK
    v = v.reshape(2, NBLK, gpb, SH, SP)
    eye = jnp.eye(gpb, dtype=v.dtype)
    if transpose:
        return jnp.einsum("djahp,ab->djapbh", v, eye).reshape(2, NBLK, LN, UB)
    return jnp.einsum("djahp,ab->djahbp", v, eye).reshape(2, NBLK, UB, LN)


def _diag_blocks(m):
    gpb = SG // NBLK
    return jnp.einsum("jahap->jahp", m.reshape(NBLK, gpb, SH, gpb, SP)).reshape(SG, SH, SP)


def _view2d(a):
    if a.ndim == 1:
        return a.reshape(1, -1)
    if a.shape[-1] < 64 and a.size % 1024 == 0:
        return a.reshape(-1, 1024)
    return a.reshape(-1, a.shape[-1])


def kernel(x, c, ctx, c_ctx, norm_g, w_mod, b_mod, we_in, we_out, gm_v_g, gm_w_s, gm_b_s, s5_lam_re, s5_lam_im, s5_log_dt, s5_b_re, s5_b_im, s5_c_re, s5_c_im, s5_d, s5_w_glu, s5_b_glu, wo_in, wo_out, q_norm_g, k_norm_g, final_g, loss_target, m_c_ctx, m_norm_g, m_w_mod, m_b_mod, m_we_in, m_we_out, m_gm_v_g, m_gm_w_s, m_gm_b_s, m_s5_lam_re, m_s5_lam_im, m_s5_log_dt, m_s5_b_re, m_s5_b_im, m_s5_c_re, m_s5_c_im, m_s5_d, m_s5_w_glu, m_s5_b_glu, m_wo_in, m_wo_out, m_q_norm_g, m_k_norm_g, m_final_g, v_c_ctx, v_norm_g, v_w_mod, v_b_mod, v_we_in, v_we_out, v_gm_v_g, v_gm_w_s, v_gm_b_s, v_s5_lam_re, v_s5_lam_im, v_s5_log_dt, v_s5_b_re, v_s5_b_im, v_s5_c_re, v_s5_c_im, v_s5_d, v_s5_w_glu, v_s5_b_glu, v_wo_in, v_wo_out, v_q_norm_g, v_k_norm_g, v_final_g):
    weights = dict(c_ctx=c_ctx, norm_g=norm_g, w_mod=w_mod, b_mod=b_mod, we_in=we_in, we_out=we_out, gm_v_g=gm_v_g,
                   gm_w_s=gm_w_s, gm_b_s=gm_b_s, s5_lam_re=s5_lam_re, s5_lam_im=s5_lam_im, s5_log_dt=s5_log_dt,
                   s5_b_re=s5_b_re, s5_b_im=s5_b_im, s5_c_re=s5_c_re, s5_c_im=s5_c_im, s5_d=s5_d, s5_w_glu=s5_w_glu,
                   s5_b_glu=s5_b_glu, wo_in=wo_in, wo_out=wo_out, q_norm_g=q_norm_g, k_norm_g=k_norm_g,
                   final_g=final_g)
    mom_m = dict(c_ctx=m_c_ctx, norm_g=m_norm_g, w_mod=m_w_mod, b_mod=m_b_mod, we_in=m_we_in, we_out=m_we_out,
                 gm_v_g=m_gm_v_g, gm_w_s=m_gm_w_s, gm_b_s=m_gm_b_s, s5_lam_re=m_s5_lam_re, s5_lam_im=m_s5_lam_im,
                 s5_log_dt=m_s5_log_dt, s5_b_re=m_s5_b_re, s5_b_im=m_s5_b_im, s5_c_re=m_s5_c_re, s5_c_im=m_s5_c_im,
                 s5_d=m_s5_d, s5_w_glu=m_s5_w_glu, s5_b_glu=m_s5_b_glu, wo_in=m_wo_in, wo_out=m_wo_out,
                 q_norm_g=m_q_norm_g, k_norm_g=m_k_norm_g, final_g=m_final_g)
    mom_v = dict(c_ctx=v_c_ctx, norm_g=v_norm_g, w_mod=v_w_mod, b_mod=v_b_mod, we_in=v_we_in, we_out=v_we_out,
                 gm_v_g=v_gm_v_g, gm_w_s=v_gm_w_s, gm_b_s=v_gm_b_s, s5_lam_re=v_s5_lam_re, s5_lam_im=v_s5_lam_im,
                 s5_log_dt=v_s5_log_dt, s5_b_re=v_s5_b_re, s5_b_im=v_s5_b_im, s5_c_re=v_s5_c_re, s5_c_im=v_s5_c_im,
                 s5_d=v_s5_d, s5_w_glu=v_s5_w_glu, s5_b_glu=v_s5_b_glu, wo_in=v_wo_in, wo_out=v_wo_out,
                 q_norm_g=v_q_norm_g, k_norm_g=v_k_norm_g, final_g=v_final_g)

    ixy = 2 * lax.axis_index("x") + lax.axis_index("y")
    n_lat = x.shape[1]
    nl = norm_g.shape[0]
    nmod = w_mod.shape[2]
    xin = (ctx[0], x[0])

    ic = lax.axis_index("c")
    mine = [lax.dynamic_index_in_dim(weights[n], ic, 0, keepdims=False).astype(BF) for n in _BIG]
    got = exchange([(m_, "xy", "gather") for m_ in mine] + [(c, "xy", "gather")], "gather_weights")
    both = d2d([(g_.reshape(-1, g_.shape[-1]), "gather") for g_ in got[:len(_BIG)]], "swap_weights")
    both = [b_.reshape((2,) + g_.shape) for b_, g_ in zip(both, got)]
    wein = [both[0][0], both[0][1]]
    weout = [both[1][l].reshape(1, D, D) for l in range(2)]
    wglu = [both[2][l].reshape(AW, AW) for l in range(2)]
    woin = [both[3][0], both[3][1]]
    woout = [both[4][l].reshape(1, D, D) for l in range(2)]
    c_group = got[len(_BIG)].reshape(4, D)

    cond = jnp.concatenate([c_group, jnp.broadcast_to(c_ctx.reshape(1, D), (4, D))], axis=0)
    b_shard = lax.dynamic_slice(b_mod, (0, ixy * nmod), (nl, nmod)).reshape(nl, 1, nmod)
    mpart = ada_fwd(cond, w_mod, b_shard)
    m_lat, m_ctx = exchange([(jnp.transpose(mpart[:, 0:4], (1, 0, 2)), "xy", "scatter"),
                             (mpart[:, 4], "xy", "gather")], "exchange_mod")
    m_lat = jnp.transpose(m_lat, (1, 0, 2)).reshape(nl, 3, D)
    m_ctx = jnp.transpose(m_ctx, (1, 0, 2)).reshape(nl, 3, D)
    mods = [jnp.stack([m_ctx[l], m_lat[l]], axis=0) for l in range(nl)]

    loss_part, dx, g, d_norm_g, d_mod_lat, d_mod_ctx, d_final_g = _local_step(
        xin, loss_target[0], mods, wein, weout, wglu, woin, woout, weights)
    grad_x = dx.reshape(1, n_lat, D)

    d_mod_lat, d_mod_ctx = jnp.stack(d_mod_lat), jnp.stack(d_mod_ctx)
    dm_send = jnp.stack([d_mod_lat.reshape(nl, 4, nmod), d_mod_ctx.reshape(nl, 4, nmod)])
    (dm_got,) = exchange([(jnp.transpose(dm_send, (2, 0, 1, 3)), "xy", "scatter")], "exchange_dmod")
    dm_rows = jnp.concatenate([dm_got[:, 0], dm_got[:, 1]], axis=0)
    gw_mod, d_cctx = ada_bwd(cond, jnp.transpose(dm_rows, (1, 0, 2)), w_mod)
    g_small = dict(c_ctx=d_cctx.reshape(D), norm_g=jnp.stack(d_norm_g), b_mod=add2(d_mod_lat, d_mod_ctx, "add_dbmod"),
                   final_g=d_final_g.reshape(D))
    for name in _SMALL:
        if name not in g_small:
            g_small[name] = jnp.stack(g[name])

    flat = jnp.concatenate([g_small[n].reshape(-1) for n in _SMALL])
    nflat = flat.shape[0]
    npad = -(-nflat // _SMALL_ALIGN) * _SMALL_ALIGN
    flat = jnp.concatenate([flat, jnp.zeros((npad - nflat,), F32)]).reshape(8, npad // (8 * 128), 128)
    pairs = [gw_mod.reshape(2, nl // 2 * D, nmod)]
    for name in _BIG:
        st = jnp.stack(g[name]) if isinstance(g[name], list) else g[name]
        pairs.append(st.reshape(2, -1, st.shape[-1]))
    got_a = d2d([(pairs[k], "swap") for k in (1, 2, 3)], "reduce_chip_a")
    got_b = d2d([(pairs[k], "swap") for k in (0, 4, 5)], "reduce_chip_b")
    theirs = [got_b[0]] + got_a + got_b[1:]
    chip = [sum_own(pairs[k], theirs[k], f"sum_chip{k}", F32 if k == 0 else BF) for k in range(len(pairs))]
    parts = exchange([(flat, "all", "scatter")]
                     + [(s_.reshape(4, s_.shape[0] // 4, s_.shape[1]), "xy", "scatter") for s_ in chip[1:]],
                     "reduce_scatter")
    sums = [sum_lead(pt, f"sum_shard{k}") for k, pt in enumerate(parts)]
    (flat_full,) = exchange([(sums[0], "all", "gather")], "gather_small")
    full = [flat_full] + d2d([(chip[0], "gather")] + [(s_, "gather") for s_ in sums[1:]], "all_gather")
    flat = full[0].reshape(-1)
    grads = {}
    off = 0
    for name in _SMALL:
        sz = weights[name].size
        grads[name] = flat[off:off + sz].reshape(weights[name].shape)
        off += sz
    grads["w_mod"] = full[1].reshape(w_mod.shape)
    for k, name in enumerate(_BIG):
        grads[name] = full[2 + k].reshape(weights[name].shape)

    delta, new_m, new_v = {}, {}, {}
    views = [_view2d(weights[n]) for n in _SMALL]
    ds, nms, nvs = adamw_many(views, [grads[n].reshape(w2.shape) for n, w2 in zip(_SMALL, views)],
                              [mom_m[n].reshape(w2.shape) for n, w2 in zip(_SMALL, views)],
                              [mom_v[n].reshape(w2.shape) for n, w2 in zip(_SMALL, views)], "adamw_small")
    for n, d2, m2, v2 in zip(_SMALL, ds, nms, nvs):
        shp = weights[n].shape
        delta[n], new_m[n], new_v[n] = d2.reshape(shp), m2.reshape(shp), v2.reshape(shp)
    for name in ["w_mod"] + _BIG:
        w2 = _view2d(weights[name])
        d2, m2, v2 = adamw(w2, grads[name].reshape(w2.shape), mom_m[name].reshape(w2.shape),
                           mom_v[name].reshape(w2.shape), f"adamw_{name}")
        shp = weights[name].shape
        delta[name], new_m[name], new_v[name] = d2.reshape(shp), m2.reshape(shp), v2.reshape(shp)

    loss = lax.psum(loss_part[0, 0], ("x", "y", "c"))
    return (loss, grad_x, *[grads[n] for n in _WEIGHTS], *[delta[n] for n in _WEIGHTS],
            *[new_m[n] for n in _WEIGHTS], *[new_v[n] for n in _WEIGHTS])


def _local_step(xin, target, mods, wein, weout, wglu, woin, woout, w):
    norm_g, gm_v_g, gm_w_s, gm_b_s = w["norm_g"], w["gm_v_g"], w["gm_w_s"], w["gm_b_s"]
    s5_lam_re, s5_lam_im, s5_log_dt = w["s5_lam_re"], w["s5_lam_im"], w["s5_log_dt"]
    s5_b_re, s5_b_im, s5_c_re, s5_c_im = w["s5_b_re"], w["s5_b_im"], w["s5_c_re"], w["s5_c_im"]
    s5_d, s5_b_glu, q_norm_g, k_norm_g, final_g = w["s5_d"], w["s5_b_glu"], w["q_norm_g"], w["k_norm_g"], w["final_g"]
    nl = norm_g.shape[0]
    n_lat = xin[1].shape[0]

    cos, sins = _rope_tables(n_lat)

    s5p = []
    for i in range(2):
        lam_l = (s5_lam_re[i].reshape(2, SW), s5_lam_im[i].reshape(2, SW),
                 jnp.repeat(s5_log_dt[i], SP, axis=1))
        lam_r = (jnp.repeat(s5_lam_re[i].reshape(2 * SG, SP), SH, axis=0),
                 jnp.repeat(s5_lam_im[i].reshape(2 * SG, SP), SH, axis=0),
                 jnp.repeat(s5_log_dt[i].reshape(2 * SG, 1), SH, axis=0))
        b_r = (jnp.transpose(s5_b_re[i], (0, 1, 3, 2)).reshape(2 * SG * SH, SP),
               jnp.transpose(s5_b_im[i], (0, 1, 3, 2)).reshape(2 * SG * SH, SP))
        pw_re, pw_im, bbr, bbi = s5_disc(*lam_l, *lam_r, *b_r)
        s5p.append(dict(
            lam_r=lam_r, b_r=b_r, pw_re=pw_re, pw_im=pw_im,
            bb_re=_block_diag(bbr.reshape(2, SG, SH, SP), False).astype(BF),
            bb_im=_block_diag(bbi.reshape(2, SG, SH, SP), False).astype(BF),
            ct_re=_block_diag(s5_c_re[i], True).astype(BF), ct_im=_block_diag(s5_c_im[i], True).astype(BF)))

    saved = []
    xcur = xin
    h = pro_fwd(xin[0], xin[1], norm_g[0].reshape(1, D), mods[0], "pro_fwd0")
    for l in range(nl):
        i = l // 2
        sv = dict(x=xcur, h=h)
        if l % 2 == 0:
            p = mm_nn(h, wein[i], f"in_proj{l}")
            sp = s5p[i]
            for dr, rev in ((0, False), (1, True)):
                sv[f"y{dr}"], sv[f"hpr{dr}"], sv[f"hpi{dr}"] = s5q_fwd(
                    p, sp["bb_re"][dr], sp["bb_im"][dr], sp["ct_re"][dr], sp["ct_im"][dr],
                    sp["pw_re"][dr], sp["pw_im"][dr], rev, f"s5_fwd{l}_{dr}")
            mix = mix_fwd(p, sv["y0"], sv["y1"], gm_v_g[i].reshape(1, AW), gm_w_s[i].astype(BF),
                          gm_b_s[i].reshape(NGRP, CHUNK, 1), s5_d[i].reshape(1, AW), wglu[i],
                          s5_b_glu[i].reshape(1, AW), f"mix_fwd{l}")
            o = mm_nn(mix, weout[i], f"out_proj{l}")
        else:
            p = mm_nn(h, woin[i], f"in_proj{l}")
            sv["qkv"] = attn_prep(p, q_norm_g[i].reshape(1, HD), k_norm_g[i].reshape(1, HD), cos, sins, f"attn_prep{l}")
            sv["o_att"], mix, sv["lse"] = attn_fwd(sv["qkv"], p, f"attn_fwd{l}")
            o = mm_nn(mix, woout[i], f"out_proj{l}")
        sv.update(p=p, mix=mix, o=o)
        saved.append(sv)
        if l < nl - 1:
            xcur, h = res_pro_fwd(xcur, o, mods[l], True, norm_g[l + 1].reshape(1, D), mods[l + 1], f"res_pro_fwd{l}")
        else:
            xcur = res_fwd(xcur, o, mods[l], False, f"res_fwd{l}")

    loss_part, dx, d_final_g, do, dgt = final_loss(xcur, target, final_g.reshape(1, D), saved[-1]["o"], mods[-1])

    g = {}
    gbuf = {}
    d_norm_g, d_mod_lat, d_mod_ctx = [None] * nl, [None] * nl, [None] * nl
    for name in ("s5_w_glu", "gm_v_g", "gm_w_s", "gm_b_s", "s5_lam_re", "s5_lam_im",
                 "s5_log_dt", "s5_b_re", "s5_b_im", "s5_c_re", "s5_c_im", "s5_d", "s5_b_glu", "q_norm_g", "k_norm_g"):
        g[name] = [None, None]
    for l in reversed(range(nl)):
        i = l // 2
        sv = saved[l]
        w_out = weout[i] if l % 2 == 0 else woout[i]
        dmix = mm_nt(do, w_out, f"out_dgrad{l}")
        out_name, in_name = ("we_out", "we_in") if l % 2 == 0 else ("wo_out", "wo_in")
        gbuf[out_name] = mm_tn(sv["mix"], do, 1, f"out_wgrad{l}", slot=i, into=gbuf.get(out_name))
        if l % 2 == 0:
            sp = s5p[i]
            (dp, dy, g["gm_w_s"][i], dbs, dvg, dd, g["s5_w_glu"][i], dbg) = mix_bwd(
                sv["p"], sv["y0"], sv["y1"], dmix, gm_v_g[i].reshape(1, AW), gm_w_s[i].astype(BF),
                gm_b_s[i].reshape(NGRP, CHUNK, 1), s5_d[i].reshape(1, AW), wglu[i], s5_b_glu[i].reshape(1, AW),
                f"mix_bwd{l}")
            g["gm_b_s"][i], g["gm_v_g"][i] = dbs.reshape(NGRP, CHUNK), dvg.reshape(AW)
            g["s5_d"][i], g["s5_b_glu"][i] = dd.reshape(AW), dbg.reshape(AW)
            g["s5_w_glu"][i] = g["s5_w_glu"][i].reshape(4, AW // 4, AW)
            dxd, das_r, das_i, dbbs_r, dbbs_i, dcs_r, dcs_i = [], [], [], [], [], [], []
            for dr, rev in ((0, False), (1, True)):
                dxs_d, da_r, da_i, dbb_r, dbb_i, dc_r, dc_i = s5q_bwd(
                    sv["p"], sv[f"hpr{dr}"], sv[f"hpi{dr}"], dy, sp["bb_re"][dr], sp["bb_im"][dr],
                    sp["ct_re"][dr], sp["ct_im"][dr], sp["pw_re"][dr], -sp["pw_im"][dr], rev, f"s5_bwd{l}_{dr}")
                dxd.append(dxs_d)
                das_r.append(jnp.repeat(da_r.reshape(SG, SP), SH, axis=0))
                das_i.append(jnp.repeat(da_i.reshape(SG, SP), SH, axis=0))
                dbbs_r.append(_diag_blocks(dbb_r).reshape(SG * SH, SP))
                dbbs_i.append(_diag_blocks(dbb_i).reshape(SG * SH, SP))
                dcs_r.append(_diag_blocks(dc_r))
                dcs_i.append(_diag_blocks(dc_i))
            cat = lambda parts: jnp.concatenate(parts, axis=0)
            dlr, dli, dldt, dbr, dbi = s5_param_bwd(*sp["lam_r"], *sp["b_r"], cat(das_r), cat(das_i),
                                                    cat(dbbs_r), cat(dbbs_i))
            g["s5_lam_re"][i], g["s5_lam_im"][i] = dlr.reshape(2, SG, SP), dli.reshape(2, SG, SP)
            g["s5_log_dt"][i] = dldt.reshape(2, SG)
            g["s5_b_re"][i] = jnp.transpose(dbr.reshape(2, SG, SH, SP), (0, 1, 3, 2))
            g["s5_b_im"][i] = jnp.transpose(dbi.reshape(2, SG, SH, SP), (0, 1, 3, 2))
            g["s5_c_re"][i], g["s5_c_im"][i] = jnp.stack(dcs_r), jnp.stack(dcs_i)
            dp = s5p_dx_sum(dy, s5_d[i].reshape(1, AW), dxd[0], dxd[1], dp, f"s5_dx_sum{l}")
            w_in = wein[i]
        else:
            dq, dp, dk, dv = attn_bwd(sv["qkv"], sv["p"], dmix, sv["o_att"], sv["lse"], f"attn_bwd{l}")
            dp, dqg, dkg = attn_prep_bwd(sv["p"], dq, dk, dv, q_norm_g[i].reshape(1, HD),
                                         k_norm_g[i].reshape(1, HD), cos, sins, dp, f"attn_prep_bwd{l}")
            g["q_norm_g"][i], g["k_norm_g"][i] = dqg.reshape(HD), dkg.reshape(HD)
            w_in = woin[i]
        dh = mm_nt(dp, w_in, f"in_dgrad{l}")
        gbuf[in_name] = mm_tn(sv["h"], dp, 4, f"in_wgrad{l}", slot=i, into=gbuf.get(in_name))
        dgt_l = dgt
        if l > 0:
            dx, dmod2, dng, do, dgt = pro_res_bwd(sv["x"], dh, dx, norm_g[l].reshape(1, D), mods[l],
                                                  saved[l - 1]["o"], mods[l - 1], f"pro_res_bwd{l}")
        else:
            dx, dmod2, dng = pro_bwd(xin[0], xin[1], dh, dx, norm_g[l].reshape(1, D), mods[l], f"pro_bwd{l}")
        d_norm_g[l] = dng.reshape(D)
        d_mod_ctx[l] = jnp.concatenate([dmod2[0, 0], dmod2[0, 1], dgt_l[0]])
        d_mod_lat[l] = jnp.concatenate([dmod2[1, 0], dmod2[1, 1], dgt_l[1]])
    g.update(gbuf)
    return loss_part, dx, g, d_norm_g, d_mod_lat, d_mod_ctx, d_final_g
```

```python
import functools
import math

import numpy as np
import jax
import jax.numpy as jnp
from jax import lax
from jax.experimental import pallas as pl
from jax.experimental.pallas import tpu as pltpu

F32 = jnp.float32
BF = jnp.bfloat16
MESH = pl.DeviceIdType.MESH

D = 1024
NC = 256
SEQ = 4096
GRID_W = 64
TM = 256
CHUNK = 128
EPS = 1e-6
HD = 128
NQ = 8
NKV = 2
ROPE_THETA = 10000.0
SG = 32
SP = 64
SH = 16
SW = SG * SP
GELU_K = math.sqrt(2.0 / math.pi)
GELU_C = 0.044715
VMEM_LIMIT_BYTES = 56 * 1024 * 1024

ADAM_LR = 0.001
ADAM_B1 = 0.9
ADAM_B2 = 0.999
ADAM_EPS = 1e-08
ADAM_WD = 0.01
ADAM_STEP = 10


def _call(body, *, name, out_shape, grid=None, in_specs=None, out_specs=None, scratch=()):
    kw = {}
    if grid is not None:
        kw["grid"] = grid
    if in_specs is not None:
        kw["in_specs"] = in_specs
    if out_specs is not None:
        kw["out_specs"] = out_specs
    return pl.pallas_call(
        body, name=name, out_shape=out_shape, scratch_shapes=list(scratch),
        compiler_params=pltpu.CompilerParams(vmem_limit_bytes=VMEM_LIMIT_BYTES), **kw)


def _dot(a, b, ca=1, cb=0):
    return lax.dot_general(a, b, (((ca,), (cb,)), ((), ())), preferred_element_type=F32)


def _sig(x):
    return 1.0 / (1.0 + jnp.exp(-x))


def _full(shape):
    n = len(shape)
    return pl.BlockSpec(shape, lambda *_: (0,) * n)


def _mm_rows(t):
    for rows in (1088, 1024, 768, 512, 256):
        if t % rows == 0:
            return rows
    raise ValueError(t)


def _layer_of(w):
    return w if isinstance(w, tuple) else (w[None], 0)


def mm_nn(a, w, name, out_dtype=F32):
    w4, layer = _layer_of(w)
    t, k = a.shape
    _, j, _, nb = w4.shape
    tr = _mm_rows(t)

    def body(a_ref, w_ref, o_ref):
        o_ref[...] = _dot(a_ref[...], w_ref[0, 0]).astype(o_ref.dtype)

    return _call(body, name=name, grid=(j, t // tr),
                 in_specs=[pl.BlockSpec((tr, k), lambda jj, i: (i, 0)),
                           pl.BlockSpec((1, 1, k, nb), lambda jj, i: (layer, jj, 0, 0))],
                 out_specs=pl.BlockSpec((tr, nb), lambda jj, i: (i, jj)),
                 out_shape=jax.ShapeDtypeStruct((t, j * nb), out_dtype))(a, w4)


def mm_nt(a, w, name, out_dtype=F32):
    w4, layer = _layer_of(w)
    t, _ = a.shape
    _, j, k, nb = w4.shape
    tr = _mm_rows(t)

    def body(a_ref, w_ref, o_ref):
        acc = _dot(a_ref[:, 0:nb], w_ref[0, 0], 1, 1)
        for jj in range(1, j):
            acc = acc + _dot(a_ref[:, jj * nb:(jj + 1) * nb], w_ref[0, jj], 1, 1)
        o_ref[...] = acc.astype(o_ref.dtype)

    return _call(body, name=name, grid=(t // tr,),
                 in_specs=[pl.BlockSpec((tr, j * nb), lambda i: (i, 0)),
                           pl.BlockSpec((1, j, k, nb), lambda i: (layer, 0, 0, 0))],
                 out_specs=pl.BlockSpec((tr, k), lambda i: (i, 0)),
                 out_shape=jax.ShapeDtypeStruct((t, k), out_dtype))(a, w4)


def mm_tn(a, b, j, name, slot=0, into=None):
    t, m = a.shape
    nb = b.shape[1] // j
    tr = _mm_rows(t)

    def body(a_ref, b_ref, *rest):
        o_ref = rest[-1]

        @pl.when(pl.program_id(1) == 0)
        def _():
            o_ref[...] = jnp.zeros_like(o_ref)
        o_ref[0, 0] += _dot(a_ref[...], b_ref[...], 0, 0)

    in_specs = [pl.BlockSpec((tr, m), lambda jj, i: (i, 0)), pl.BlockSpec((tr, nb), lambda jj, i: (i, jj))]
    args = [a, b]
    alias = {}
    if into is not None:
        in_specs.append(pl.BlockSpec(memory_space=pl.ANY))
        args.append(into)
        alias = {2: 0}
    return pl.pallas_call(
        body, name=name, grid=(j, t // tr), in_specs=in_specs,
        out_specs=pl.BlockSpec((1, 1, m, nb), lambda jj, i: (slot, jj, 0, 0)),
        out_shape=jax.ShapeDtypeStruct((2, j, m, nb), F32), input_output_aliases=alias,
        compiler_params=pltpu.CompilerParams(vmem_limit_bytes=VMEM_LIMIT_BYTES))(*args)


def _mod_rows(mod_ref, i):
    ctx = i == 0
    sh = jnp.where(ctx, mod_ref[0, 0:1, :], mod_ref[1, 0:1, :])
    sc = jnp.where(ctx, mod_ref[0, 1:2, :], mod_ref[1, 1:2, :])
    gt = jnp.where(ctx, mod_ref[0, 2:3, :], mod_ref[1, 2:3, :])
    return sh, sc, gt


def _split_specs():
    return [pl.BlockSpec((TM, D), lambda i: (0, 0)), pl.BlockSpec((TM, D), lambda i: (jnp.maximum(i - 1, 0), 0))]


def _split_tile(c_ref, l_ref, i):
    return jnp.where(i == 0, c_ref[...], l_ref[...])


def pro_fwd(ctx, lat, g, mod, name):
    t = ctx.shape[0] + lat.shape[0]

    def body(c_ref, l_ref, g_ref, mod_ref, h_ref):
        i = pl.program_id(0)
        sh, sc, _ = _mod_rows(mod_ref, i)
        xv = _split_tile(c_ref, l_ref, i)
        r = lax.rsqrt(jnp.mean(xv * xv, axis=-1, keepdims=True) + EPS)
        h_ref[...] = ((xv * r) * g_ref[...] * (1.0 + sc) + sh).astype(BF)

    return _call(body, name=name, grid=(t // TM,),
                 in_specs=_split_specs() + [_full((1, D)), _full((2, 3, D))],
                 out_specs=pl.BlockSpec((TM, D), lambda i: (i, 0)),
                 out_shape=jax.ShapeDtypeStruct((t, D), BF))(ctx, lat, g, mod)


def pro_bwd(ctx, lat, dh, dxn, g, mod, name):
    t = ctx.shape[0] + lat.shape[0]

    def body(c_ref, l_ref, dh_ref, dxn_ref, g_ref, mod_ref, dx_ref, dmod_ref, dg_ref):
        i = pl.program_id(0)

        @pl.when(i == 0)
        def _():
            dmod_ref[...] = jnp.zeros_like(dmod_ref)
            dg_ref[...] = jnp.zeros_like(dg_ref)

        _, sc, _ = _mod_rows(mod_ref, i)
        xv = _split_tile(c_ref, l_ref, i)
        gv = g_ref[...]
        r = lax.rsqrt(jnp.mean(xv * xv, axis=-1, keepdims=True) + EPS)
        xn = xv * r
        dh_v = dh_ref[...]
        e = dh_v * (1.0 + sc)
        dsh = jnp.sum(dh_v, axis=0, keepdims=True)
        dsc = jnp.sum(dh_v * xn * gv, axis=0, keepdims=True)
        dg_ref[...] += jnp.sum(e * xn, axis=0, keepdims=True)
        dxh = e * gv

        @pl.when(i == 0)
        def _():
            dmod_ref[0, 0:1, :] += dsh
            dmod_ref[0, 1:2, :] += dsc

        @pl.when(i > 0)
        def _():
            dx_ref[...] = dxn_ref[...] + r * (dxh - xn * jnp.mean(dxh * xn, axis=-1, keepdims=True))
            dmod_ref[1, 0:1, :] += dsh
            dmod_ref[1, 1:2, :] += dsc

    tile = pl.BlockSpec((TM, D), lambda i: (i, 0))
    return _call(body, name=name, grid=(t // TM,),
                 in_specs=_split_specs() + [tile, tile, _full((1, D)), _full((2, 3, D))],
                 out_specs=[_split_specs()[1], _full((2, 2, D)), _full((1, D))],
                 out_shape=[jax.ShapeDtypeStruct(lat.shape, F32), jax.ShapeDtypeStruct((2, 2, D), F32),
                            jax.ShapeDtypeStruct((1, D), F32)])(ctx, lat, dh, dxn, g, mod)


def res_fwd(x, o, mod, update_ctx, name):
    t = x.shape[0]

    def body(x_ref, o_ref, mod_ref, y_ref):
        i = pl.program_id(0)
        _, _, gt = _mod_rows(mod_ref, i)
        upd = x_ref[...] + gt * o_ref[...]
        if update_ctx:
            y_ref[...] = upd
        else:
            y_ref[...] = jnp.where(i == 0, x_ref[...], upd)

    tile = pl.BlockSpec((TM, D), lambda i: (i, 0))
    return _call(body, name=name, grid=(t // TM,), in_specs=[tile, tile, _full((2, 3, D))],
                 out_specs=tile, out_shape=jax.ShapeDtypeStruct((t, D), F32))(x, o, mod)


def res_bwd(dxn, o, mod, update_ctx, name):
    t = dxn.shape[0]

    def body(dxn_ref, o_ref, mod_ref, do_ref, dgt_ref):
        i = pl.program_id(0)

        @pl.when(i == 0)
        def _():
            dgt_ref[...] = jnp.zeros_like(dgt_ref)

        _, _, gt = _mod_rows(mod_ref, i)
        dv = dxn_ref[...]
        do = gt * dv
        dgt = jnp.sum(dv * o_ref[...], axis=0, keepdims=True)
        if update_ctx:
            do_ref[...] = do.astype(BF)
        else:
            do_ref[...] = jnp.where(i == 0, jnp.zeros_like(do), do).astype(BF)

        if update_ctx:
            @pl.when(i == 0)
            def _():
                dgt_ref[0:1, :] += dgt

        @pl.when(i > 0)
        def _():
            dgt_ref[1:2, :] += dgt

    tile = pl.BlockSpec((TM, D), lambda i: (i, 0))
    return _call(body, name=name, grid=(t // TM,), in_specs=[tile, tile, _full((2, 3, D))],
                 out_specs=[tile, _full((2, D))],
                 out_shape=[jax.ShapeDtypeStruct((t, D), BF), jax.ShapeDtypeStruct((2, D), F32)])(dxn, o, mod)


def res_pro_fwd(x, o, mod, update_ctx, g_next, mod_next, name):
    split = isinstance(x, tuple)
    xs = list(x) if split else [x]
    t = o.shape[0]

    def body(*refs):
        x_refs = refs[:len(xs)]
        o_ref, mod_ref, g_ref, modn_ref, y_ref, h_ref = refs[len(xs):]
        i = pl.program_id(0)
        _, _, gt = _mod_rows(mod_ref, i)
        xv = _split_tile(x_refs[0], x_refs[1], i) if split else x_refs[0][...]
        xn = xv + gt * o_ref[...]
        if not update_ctx:
            xn = jnp.where(i == 0, xv, xn)
        y_ref[...] = xn
        sh, sc, _ = _mod_rows(modn_ref, i)
        r = lax.rsqrt(jnp.mean(xn * xn, axis=-1, keepdims=True) + EPS)
        h_ref[...] = ((xn * r) * g_ref[...] * (1.0 + sc) + sh).astype(BF)

    tile = pl.BlockSpec((TM, D), lambda i: (i, 0))
    return _call(body, name=name, grid=(t // TM,),
                 in_specs=(_split_specs() if split else [tile]) + [tile, _full((2, 3, D)), _full((1, D)), _full((2, 3, D))],
                 out_specs=[tile, tile],
                 out_shape=[jax.ShapeDtypeStruct((t, D), F32), jax.ShapeDtypeStruct((t, D), BF)])(
        *xs, o, mod, g_next, mod_next)


def _res_bwd_part(dx, o_ref, modp_ref, do_ref, dgt_ref, i, update_ctx):
    _, _, gtp = _mod_rows(modp_ref, i)
    do = gtp * dx
    dgt = jnp.sum(dx * o_ref[...], axis=0, keepdims=True)
    if update_ctx:
        do_ref[...] = do.astype(BF)

        @pl.when(i == 0)
        def _():
            dgt_ref[0:1, :] += dgt
    else:
        do_ref[...] = jnp.where(i == 0, jnp.zeros_like(do), do).astype(BF)

    @pl.when(i > 0)
    def _():
        dgt_ref[1:2, :] += dgt


def pro_res_bwd(x, dh, dxn, g, mod, o_prev, mod_prev, name):
    t = x.shape[0]

    def body(x_ref, dh_ref, dxn_ref, g_ref, mod_ref, o_ref, modp_ref, dx_ref, dmod_ref, dg_ref, do_ref, dgt_ref):
        i = pl.program_id(0)

        @pl.when(i == 0)
        def _():
            dmod_ref[...] = jnp.zeros_like(dmod_ref)
            dg_ref[...] = jnp.zeros_like(dg_ref)
            dgt_ref[...] = jnp.zeros_like(dgt_ref)

        _, sc, _ = _mod_rows(mod_ref, i)
        xv = x_ref[...]
        gv = g_ref[...]
        r = lax.rsqrt(jnp.mean(xv * xv, axis=-1, keepdims=True) + EPS)
        xn = xv * r
        dh_v = dh_ref[...]
        e = dh_v * (1.0 + sc)
        dsh = jnp.sum(dh_v, axis=0, keepdims=True)
        dsc = jnp.sum(dh_v * xn * gv, axis=0, keepdims=True)
        dg_ref[...] += jnp.sum(e * xn, axis=0, keepdims=True)
        dxh = e * gv
        dx = dxn_ref[...] + r * (dxh - xn * jnp.mean(dxh * xn, axis=-1, keepdims=True))
        dx_ref[...] = dx

        @pl.when(i == 0)
        def _():
            dmod_ref[0, 0:1, :] += dsh
            dmod_ref[0, 1:2, :] += dsc

        @pl.when(i > 0)
        def _():
            dmod_ref[1, 0:1, :] += dsh
            dmod_ref[1, 1:2, :] += dsc

        _res_bwd_part(dx, o_ref, modp_ref, do_ref, dgt_ref, i, True)

    tile = pl.BlockSpec((TM, D), lambda i: (i, 0))
    return _call(body, name=name, grid=(t // TM,),
                 in_specs=[tile, tile, tile, _full((1, D)), _full((2, 3, D)), tile, _full((2, 3, D))],
                 out_specs=[tile, _full((2, 2, D)), _full((1, D)), tile, _full((2, D))],
                 out_shape=[jax.ShapeDtypeStruct((t, D), F32), jax.ShapeDtypeStruct((2, 2, D), F32),
                            jax.ShapeDtypeStruct((1, D), F32), jax.ShapeDtypeStruct((t, D), BF),
                            jax.ShapeDtypeStruct((2, D), F32)])(x, dh, dxn, g, mod, o_prev, mod_prev)


def final_loss(x, target, g, o_last, mod_last):
    t = x.shape[0]

    def body(x_ref, t_ref, g_ref, o_ref, modp_ref, loss_ref, dx_ref, dg_ref, do_ref, dgt_ref):
        i = pl.program_id(0)

        @pl.when(i == 0)
        def _():
            loss_ref[...] = jnp.zeros_like(loss_ref)
            dg_ref[...] = jnp.zeros_like(dg_ref)
            dx_ref[...] = jnp.zeros_like(dx_ref)
            do_ref[...] = jnp.zeros_like(do_ref)
            dgt_ref[...] = jnp.zeros_like(dgt_ref)

        @pl.when(i > 0)
        def _():
            xv = x_ref[...]
            gv = g_ref[...]
            r = lax.rsqrt(jnp.mean(xv * xv, axis=-1, keepdims=True) + EPS)
            xn = xv * r
            err = xn * gv - t_ref[...]
            loss_ref[...] += (0.5 / D) * jnp.sum(jnp.sum(err * err, axis=1, keepdims=True), axis=0, keepdims=True)
            dy = err * (1.0 / D)
            dg_ref[...] += jnp.sum(dy * xn, axis=0, keepdims=True)
            dxh = dy * gv
            dx = r * (dxh - xn * jnp.mean(dxh * xn, axis=-1, keepdims=True))
            dx_ref[...] = dx
            do_ref[...] = (modp_ref[1, 2:3, :] * dx).astype(BF)
            dgt_ref[1:2, :] += jnp.sum(dx * o_ref[...], axis=0, keepdims=True)

    tile = pl.BlockSpec((TM, D), lambda i: (i, 0))
    return _call(body, name="final_loss", grid=(t // TM,),
                 in_specs=[tile, pl.BlockSpec((TM, D), lambda i: (jnp.maximum(i - 1, 0), 0)), _full((1, D)), tile,
                           _full((2, 3, D))],
                 out_specs=[_full((1, 1)), tile, _full((1, D)), tile, _full((2, D))],
                 out_shape=[jax.ShapeDtypeStruct((1, 1), F32), jax.ShapeDtypeStruct((t, D), F32),
                            jax.ShapeDtypeStruct((1, D), F32), jax.ShapeDtypeStruct((t, D), BF),
                            jax.ShapeDtypeStruct((2, D), F32)])(x, target, g, o_last, mod_last)


def ada_fwd(cond, w_mod, b_mod):
    nl, _, nw = w_mod.shape

    def body(c_ref, w_ref, b_ref, o_ref):
        cv = c_ref[...]
        s = (cv * _sig(cv)).astype(BF)
        o_ref[0] = _dot(s, w_ref[0].astype(BF)) + b_ref[0]

    return _call(body, name="ada_fwd", grid=(nl,),
                 in_specs=[_full((8, D)), pl.BlockSpec((1, D, nw), lambda l: (l, 0, 0)),
                           pl.BlockSpec((1, 1, nw), lambda l: (l, 0, 0))],
                 out_specs=pl.BlockSpec((1, 8, nw), lambda l: (l, 0, 0)),
                 out_shape=jax.ShapeDtypeStruct((nl, 8, nw), F32))(cond, w_mod, b_mod)


def ada_bwd(cond, dm, w_mod):
    nl, _, nw = w_mod.shape

    def body(c_ref, dm_ref, w_ref, gw_ref, dcc_ref, dc_ref):
        l = pl.program_id(0)

        @pl.when(l == 0)
        def _():
            dc_ref[...] = jnp.zeros_like(dc_ref)

        cv = c_ref[...]
        sg = _sig(cv)
        s = (cv * sg).astype(BF)
        dmv = dm_ref[0].astype(BF)
        gw_ref[0] = _dot(s, dmv, 0, 0)
        dc_ref[...] += _dot(dmv, w_ref[0].astype(BF), 1, 1)

        @pl.when(l == nl - 1)
        def _():
            dcond = dc_ref[...] * (sg * (1.0 + cv * (1.0 - sg)))
            dcc_ref[...] = jnp.sum(dcond[4:8], axis=0, keepdims=True)

    return _call(body, name="ada_bwd", grid=(nl,),
                 in_specs=[_full((8, D)), pl.BlockSpec((1, 8, nw), lambda l: (l, 0, 0)),
                           pl.BlockSpec((1, D, nw), lambda l: (l, 0, 0))],
                 out_specs=[pl.BlockSpec((1, D, nw), lambda l: (l, 0, 0)), _full((1, D))],
                 out_shape=[jax.ShapeDtypeStruct((nl, D, nw), F32), jax.ShapeDtypeStruct((1, D), F32)],
                 scratch=[pltpu.VMEM((8, D), F32)])(cond, dm, w_mod)


def add2(a, b, name):
    def body(a_ref, b_ref, o_ref):
        o_ref[...] = a_ref[...] + b_ref[...]

    return _call(body, name=name, out_shape=jax.ShapeDtypeStruct(a.shape, a.dtype))(a, b)


AW = 512
NGRP = 4


def Y4_SPEC():
    return pl.BlockSpec((AW // 128, TM, 128), lambda i: (0, i, 0))


def _cat_lanes(ref):
    return jnp.concatenate([ref[q] for q in range(ref.shape[0])], axis=1)


def _gelu(y):
    t = jnp.tanh(GELU_K * (y + GELU_C * y * y * y))
    return 0.5 * y * (1.0 + t), t


def _layer_norm_stats(v):
    mu = jnp.mean(v, axis=-1, keepdims=True)
    vc = v - mu
    rstd = lax.rsqrt(jnp.mean(vc * vc, axis=-1, keepdims=True) + EPS)
    return vc * rstd, rstd


def _spatial_mix(vn_ref, ws_ref, bs_ref, mixed_ref):
    for ch in range(TM // CHUNK):
        rows = slice(ch * CHUNK, (ch + 1) * CHUNK)
        for g in range(NGRP):
            cols = slice(g * CHUNK, (g + 1) * CHUNK)
            mixed_ref[rows, cols] = _dot(ws_ref[g], vn_ref[rows, cols]) + bs_ref[g]


def mix_fwd(p, yf, yb, vg, ws, bs, dsk, wglu, bglu, name):
    t = p.shape[0]
    wglu, glu_layer = _layer_of(wglu)

    def body(p_ref, yf_ref, yb_ref, vg_ref, ws_ref, bs_ref, d_ref, wg_ref, bg_ref, o_ref, vn_ref, mixed_ref):
        vhat, _ = _layer_norm_stats(p_ref[:, AW:2 * AW])
        vn_ref[...] = (vhat * vg_ref[...]).astype(BF)
        _spatial_mix(vn_ref, ws_ref, bs_ref, mixed_ref)
        ga = p_ref[:, 2 * AW:3 * AW]
        o_ref[:, 0:AW] = (p_ref[:, 0:AW] * mixed_ref[...] * (ga * _sig(ga))).astype(BF)
        y = _cat_lanes(yf_ref) + _cat_lanes(yb_ref) + d_ref[...] * p_ref[:, 3 * AW:4 * AW]
        y2, _ = _gelu(y)
        z = _dot(y2.astype(BF), wg_ref[0]) + bg_ref[...]
        gb = p_ref[:, 4 * AW:5 * AW]
        o_ref[:, AW:2 * AW] = (y2 * _sig(z) * (gb * _sig(gb))).astype(BF)

    tile = lambda w: pl.BlockSpec((TM, w), lambda i: (i, 0))
    return _call(body, name=name, grid=(t // TM,),
                 in_specs=[tile(5 * AW), Y4_SPEC(), Y4_SPEC(), _full((1, AW)), _full((NGRP, CHUNK, CHUNK)),
                           _full((NGRP, CHUNK, 1)), _full((1, AW)),
                           pl.BlockSpec((1, AW, AW), lambda i: (glu_layer, 0, 0)), _full((1, AW))],
                 out_specs=tile(2 * AW), out_shape=jax.ShapeDtypeStruct((t, 2 * AW), BF),
                 scratch=[pltpu.VMEM((TM, AW), BF), pltpu.VMEM((TM, AW), F32)])(p, yf, yb, vg, ws, bs, dsk, wglu, bglu)


def mix_bwd(p, yf, yb, dmix, vg, ws, bs, dsk, wglu, bglu, name):
    t = p.shape[0]
    wglu, glu_layer = _layer_of(wglu)

    def body(p_ref, yf_ref, yb_ref, dm_ref, vg_ref, ws_ref, bs_ref, d_ref, wg_ref, bg_ref,
             dpa_ref, dy_ref, dws_ref, dbs_ref, dvg_ref, dd_ref, dwg_ref, dbg_ref,
             vn_ref, mixed_ref, dmx_ref, dvn_ref):
        @pl.when(pl.program_id(0) == 0)
        def _():
            for r in (dws_ref, dbs_ref, dvg_ref, dd_ref, dwg_ref, dbg_ref):
                r[...] = jnp.zeros_like(r)

        vhat, rstd = _layer_norm_stats(p_ref[:, AW:2 * AW])
        vgv = vg_ref[...]
        vn_ref[...] = (vhat * vgv).astype(BF)
        _spatial_mix(vn_ref, ws_ref, bs_ref, mixed_ref)
        u = p_ref[:, 0:AW]
        ga = p_ref[:, 2 * AW:3 * AW]
        sga = _sig(ga)
        dya = dm_ref[:, 0:AW]
        mixed = mixed_ref[...]
        dpa_ref[:, 0:AW] = (dya * mixed * (ga * sga)).astype(BF)
        dpa_ref[:, 2 * AW:3 * AW] = (dya * u * mixed * (sga * (1.0 + ga * (1.0 - sga)))).astype(BF)
        dmx_ref[...] = dya * u * (ga * sga)
        for ch in range(TM // CHUNK):
            rows = slice(ch * CHUNK, (ch + 1) * CHUNK)
            for g in range(NGRP):
                cols = slice(g * CHUNK, (g + 1) * CHUNK)
                dmx = dmx_ref[rows, cols]
                dmxb = dmx.astype(BF)
                dws_ref[g] += _dot(dmxb, vn_ref[rows, cols], 1, 1)
                dbs_ref[g] += jnp.sum(dmx, axis=1, keepdims=True)
                dvn_ref[rows, cols] = _dot(ws_ref[g], dmxb, 0, 0)
        dvn = dvn_ref[...]
        dvg_ref[...] += jnp.sum(dvn * vhat, axis=0, keepdims=True)
        dvh = dvn * vgv
        dpa_ref[:, AW:2 * AW] = (rstd * (dvh - jnp.mean(dvh, axis=-1, keepdims=True)
                                         - vhat * jnp.mean(dvh * vhat, axis=-1, keepdims=True))).astype(BF)

        xs = p_ref[:, 3 * AW:4 * AW]
        y = _cat_lanes(yf_ref) + _cat_lanes(yb_ref) + d_ref[...] * xs
        y2, th = _gelu(y)
        y2b = y2.astype(BF)
        z = _dot(y2b, wg_ref[0]) + bg_ref[...]
        sz = _sig(z)
        gb = p_ref[:, 4 * AW:5 * AW]
        sgb = _sig(gb)
        dyb = dm_ref[:, AW:2 * AW]
        dpa_ref[:, 4 * AW:5 * AW] = (dyb * (y2 * sz) * (sgb * (1.0 + gb * (1.0 - sgb)))).astype(BF)
        dy3 = dyb * (gb * sgb)
        dz = dy3 * y2 * sz * (1.0 - sz)
        dzb = dz.astype(BF)
        dwg_ref[...] += _dot(y2b, dzb, 0, 0)
        dbg_ref[...] += jnp.sum(dz, axis=0, keepdims=True)
        dy2 = dy3 * sz + _dot(dzb, wg_ref[0], 1, 1)
        dgelu = 0.5 * (1.0 + th) + 0.5 * y * (1.0 - th * th) * GELU_K * (1.0 + 3.0 * GELU_C * y * y)
        dy = dy2 * dgelu
        dd_ref[...] += jnp.sum(dy * xs, axis=0, keepdims=True)
        dy_ref[...] = dy

    tile = lambda w: pl.BlockSpec((TM, w), lambda i: (i, 0))
    return _call(body, name=name, grid=(t // TM,),
                 in_specs=[tile(5 * AW), Y4_SPEC(), Y4_SPEC(), tile(2 * AW), _full((1, AW)), _full((NGRP, CHUNK, CHUNK)),
                           _full((NGRP, CHUNK, 1)), _full((1, AW)),
                           pl.BlockSpec((1, AW, AW), lambda i: (glu_layer, 0, 0)), _full((1, AW))],
                 out_specs=[tile(5 * AW), tile(AW), _full((NGRP, CHUNK, CHUNK)), _full((NGRP, CHUNK, 1)),
                            _full((1, AW)), _full((1, AW)), _full((AW, AW)), _full((1, AW))],
                 out_shape=[jax.ShapeDtypeStruct((t, 5 * AW), BF),
                            jax.ShapeDtypeStruct((t, AW), F32), jax.ShapeDtypeStruct((NGRP, CHUNK, CHUNK), F32),
                            jax.ShapeDtypeStruct((NGRP, CHUNK, 1), F32), jax.ShapeDtypeStruct((1, AW), F32),
                            jax.ShapeDtypeStruct((1, AW), F32), jax.ShapeDtypeStruct((AW, AW), F32),
                            jax.ShapeDtypeStruct((1, AW), F32)],
                 scratch=[pltpu.VMEM((TM, AW), BF), pltpu.VMEM((TM, AW), F32), pltpu.VMEM((TM, AW), F32),
                          pltpu.VMEM((TM, AW), F32)])(p, yf, yb, dmix, vg, ws, bs, dsk, wglu, bglu)


LN = 512
NBLK = SW // LN
UB = AW // NBLK
SCAN_R = 32
SCAN_G = TM // SCAN_R
PW_ROWS = SCAN_R
POW_EXP = list(range(1, SCAN_R + 1))
GROUPS_PER_TILE = TM // 8


def s5_disc(lam_re, lam_im, dt, lam_re_r, lam_im_r, dt_r, b_re, b_im):
    nexp = jnp.asarray(np.array(POW_EXP, np.float32).reshape(PW_ROWS, 1))

    def body(n_ref, lr_ref, li_ref, dt_ref, lrr_ref, lir_ref, dtr_ref, br_ref, bi_ref,
             pr_ref, pi_ref, bbr_ref, bbi_ref):
        for dr in range(2):
            dtl = jnp.exp(dt_ref[dr:dr + 1, :])
            zr = lr_ref[dr:dr + 1, :] * dtl
            zi = li_ref[dr:dr + 1, :] * dtl
            mag = jnp.exp(n_ref[...] * zr)
            ang = n_ref[...] * zi
            pr_ref[dr] = mag * jnp.cos(ang)
            pi_ref[dr] = mag * jnp.sin(ang)
        lr, li, dtv = lrr_ref[...], lir_ref[...], jnp.exp(dtr_ref[...])
        mag = jnp.exp(lr * dtv)
        nr = mag * jnp.cos(li * dtv) - 1.0
        ni = mag * jnp.sin(li * dtv)
        den = lr * lr + li * li
        fr = (nr * lr + ni * li) / den
        fi = (ni * lr - nr * li) / den
        bbr_ref[...] = fr * br_ref[...] - fi * bi_ref[...]
        bbi_ref[...] = fr * bi_ref[...] + fi * br_ref[...]

    rows = lam_re_r.shape[0]
    return _call(body, name="s5_disc",
                 out_shape=[jax.ShapeDtypeStruct((2, PW_ROWS, SW), F32), jax.ShapeDtypeStruct((2, PW_ROWS, SW), F32),
                            jax.ShapeDtypeStruct((rows, SP), F32), jax.ShapeDtypeStruct((rows, SP), F32)])(
        nexp, lam_re, lam_im, dt, lam_re_r, lam_im_r, dt_r, b_re, b_im)


def s5_param_bwd(lam_re_r, lam_im_r, dt_r, b_re, b_im, da_re, da_im, dbb_re, dbb_im):
    rows = lam_re_r.shape[0]
    ng = rows // SH
    seg = jnp.asarray(np.kron(np.eye(ng, dtype=np.float32), np.ones((1, SH), np.float32)))

    def body(seg_ref, lr_ref, li_ref, dt_ref, br_ref, bi_ref, dar_ref, dai_ref, dbbr_ref, dbbi_ref,
             dlr_ref, dli_ref, ddt_ref, dbr_ref, dbi_ref):
        lr, li, dtv = lr_ref[...], li_ref[...], jnp.exp(dt_ref[...])
        mag = jnp.exp(lr * dtv)
        lbr = mag * jnp.cos(li * dtv)
        lbi = mag * jnp.sin(li * dtv)
        den = lr * lr + li * li
        nr, ni = lbr - 1.0, lbi
        fr = (nr * lr + ni * li) / den
        fi = (ni * lr - nr * li) / den
        br, bi = br_ref[...], bi_ref[...]
        gbr, gbi = dbbr_ref[...], dbbi_ref[...]
        dbr_ref[...] = gbr * fr + gbi * fi
        dbi_ref[...] = gbi * fr - gbr * fi
        gfr = gbr * br + gbi * bi
        gfi = gbi * br - gbr * bi
        ilr, ili = lr / den, -li / den
        gnr = gfr * ilr + gfi * ili
        gni = gfi * ilr - gfr * ili
        qr = -(fr * ilr - fi * ili)
        qi = -(fr * ili + fi * ilr)
        glr = gfr * qr + gfi * qi
        gli = gfi * qr - gfr * qi
        first = (lax.broadcasted_iota(jnp.int32, (rows, 1), 0) % SH) == 0
        glbr = gnr + jnp.where(first, dar_ref[...], 0.0)
        glbi = gni + jnp.where(first, dai_ref[...], 0.0)
        gzr = glbr * lbr + glbi * lbi
        gzi = glbi * lbr - glbr * lbi
        glr = glr + gzr * dtv
        gli = gli + gzi * dtv
        gdt = (gzr * lr + gzi * li) * dtv
        hi = lax.Precision.HIGHEST
        sg = seg_ref[...]
        dlr_ref[...] = jnp.dot(sg, glr, precision=hi, preferred_element_type=F32)
        dli_ref[...] = jnp.dot(sg, gli, precision=hi, preferred_element_type=F32)
        ddt_ref[...] = jnp.sum(jnp.dot(sg, gdt, precision=hi, preferred_element_type=F32), axis=1, keepdims=True)

    return _call(body, name="s5_param_bwd",
                 out_shape=[jax.ShapeDtypeStruct((ng, SP), F32), jax.ShapeDtypeStruct((ng, SP), F32),
                            jax.ShapeDtypeStruct((ng, 1), F32), jax.ShapeDtypeStruct((rows, SP), F32),
                            jax.ShapeDtypeStruct((rows, SP), F32)])(
        seg, lam_re_r, lam_im_r, dt_r, b_re, b_im, da_re, da_im, dbb_re, dbb_im)


def _scan_tile(hr_ref, hi_ref, er_ref, ei_ref, st_ref, cr_ref, ci_ref, pr_ref, pi_ref, reverse):
    row8 = lax.broadcasted_iota(jnp.int32, (TM, 1), 0) % 8
    rowe = lax.broadcasted_iota(jnp.int32, (2 * GROUPS_PER_TILE, 1), 0)
    ne = 2 * GROUPS_PER_TILE
    for blk in range(NBLK):
        cols = slice(blk * LN, (blk + 1) * LN)
        hr, hi = hr_ref[:, cols], hi_ref[:, cols]
        for k, s in enumerate((1, 2, 4)):
            ar, ai = pr_ref[k:k + 1, cols], pi_ref[k:k + 1, cols]
            if reverse:
                m = row8 < 8 - s
                sr, si = pltpu.roll(hr, TM - s, 0), pltpu.roll(hi, TM - s, 0)
            else:
                m = row8 >= s
                sr, si = pltpu.roll(hr, s, 0), pltpu.roll(hi, s, 0)
            sr, si = jnp.where(m, sr, 0.0), jnp.where(m, si, 0.0)
            hr, hi = hr + ar * sr - ai * si, hi + ar * si + ai * sr
        hr_ref[:, cols] = hr
        hi_ref[:, cols] = hi
        edge = 0 if reverse else 7
        nq = LN // 128
        for q in range(nq):
            st_ref[q] = hr[:, q * 128:(q + 1) * 128]
            st_ref[nq + q] = hi[:, q * 128:(q + 1) * 128]
        gr = jnp.concatenate([st_ref[q, pl.ds(edge, GROUPS_PER_TILE, stride=8), :] for q in range(nq)], axis=1)
        gi = jnp.concatenate([st_ref[nq + q, pl.ds(edge, GROUPS_PER_TILE, stride=8), :] for q in range(nq)], axis=1)
        zero = jnp.zeros((GROUPS_PER_TILE, LN), F32)
        if reverse:
            er_ref[0:GROUPS_PER_TILE, :] = gr
            ei_ref[0:GROUPS_PER_TILE, :] = gi
            er_ref[GROUPS_PER_TILE:ne, :] = zero
            ei_ref[GROUPS_PER_TILE:ne, :] = zero
            er_ref[GROUPS_PER_TILE:GROUPS_PER_TILE + 1, :] = cr_ref[:, cols]
            ei_ref[GROUPS_PER_TILE:GROUPS_PER_TILE + 1, :] = ci_ref[:, cols]
        else:
            er_ref[0:GROUPS_PER_TILE, :] = zero
            ei_ref[0:GROUPS_PER_TILE, :] = zero
            er_ref[GROUPS_PER_TILE - 1:GROUPS_PER_TILE, :] = cr_ref[:, cols]
            ei_ref[GROUPS_PER_TILE - 1:GROUPS_PER_TILE, :] = ci_ref[:, cols]
            er_ref[GROUPS_PER_TILE:ne, :] = gr
            ei_ref[GROUPS_PER_TILE:ne, :] = gi
        evr, evi = er_ref[...], ei_ref[...]
        s = 1
        k = 3
        while s < ne:
            ar, ai = pr_ref[k:k + 1, cols], pi_ref[k:k + 1, cols]
            if reverse:
                m = rowe < ne - s
                sr, si = pltpu.roll(evr, ne - s, 0), pltpu.roll(evi, ne - s, 0)
            else:
                m = rowe >= s
                sr, si = pltpu.roll(evr, s, 0), pltpu.roll(evi, s, 0)
            sr, si = jnp.where(m, sr, 0.0), jnp.where(m, si, 0.0)
            evr, evi = evr + ar * sr - ai * si, evi + ar * si + ai * sr
            s *= 2
            k += 1
        er_ref[...] = evr
        ei_ref[...] = evi
        if reverse:
            cr_ref[:, cols] = er_ref[0:1, :]
            ci_ref[:, cols] = ei_ref[0:1, :]
            apr, api = pr_ref[24:32, cols], pi_ref[24:32, cols]
        else:
            cr_ref[:, cols] = er_ref[ne - 1:ne, :]
            ci_ref[:, cols] = ei_ref[ne - 1:ne, :]
            apr, api = pr_ref[16:24, cols], pi_ref[16:24, cols]
        for g in range(GROUPS_PER_TILE):
            e_row = g + 1 if reverse else GROUPS_PER_TILE - 1 + g
            kr, ki = er_ref[e_row:e_row + 1, :], ei_ref[e_row:e_row + 1, :]
            rows = slice(8 * g, 8 * g + 8)
            hr_ref[rows, cols] = hr_ref[rows, cols] + apr * kr - api * ki
            hi_ref[rows, cols] = hi_ref[rows, cols] + apr * ki + api * kr


def _tile_order(kind, nt):
    if kind == "fwd":
        return lambda i: i
    if kind == "bwd":
        return lambda i: jnp.where(i == 0, 0, nt - i)
    if kind == "fwd_adj":
        return lambda i: nt - 1 - i
    if kind == "bwd_adj":
        return lambda i: jnp.where(i == nt - 1, 0, i + 1)
    raise ValueError(kind)


def s5_fwd(p, bb_re, bb_im, ct_re, ct_im, pw_re, pw_im, reverse, name):
    t = p.shape[0]
    nt = t // TM
    order = _tile_order("bwd" if reverse else "fwd", nt)

    def body(x_ref, bbr_ref, bbi_ref, ctr_ref, cti_ref, pr_ref, pi_ref, y_ref, hpr_ref, hpi_ref,
             hr_ref, hi_ref, er_ref, ei_ref, st_ref, cr_ref, ci_ref, c0r_ref, c0i_ref):
        @pl.when(pl.program_id(0) == 0)
        def _():
            cr_ref[...] = jnp.zeros_like(cr_ref)
            ci_ref[...] = jnp.zeros_like(ci_ref)

        c0r_ref[...] = cr_ref[...]
        c0i_ref[...] = ci_ref[...]
        xb = x_ref[...].astype(BF)
        for j in range(NBLK):
            cols = slice(j * LN, (j + 1) * LN)
            hr_ref[:, cols] = _dot(xb[:, j * UB:(j + 1) * UB], bbr_ref[j])
            hi_ref[:, cols] = _dot(xb[:, j * UB:(j + 1) * UB], bbi_ref[j])
        _scan_tile(hr_ref, hi_ref, er_ref, ei_ref, st_ref, cr_ref, ci_ref, pr_ref, pi_ref, reverse)
        rowi = lax.broadcasted_iota(jnp.int32, (TM, 1), 0)
        for j in range(NBLK):
            cols = slice(j * LN, (j + 1) * LN)
            hr, hi = hr_ref[:, cols], hi_ref[:, cols]
            y_ref[:, j * UB:(j + 1) * UB] = _dot(hr.astype(BF), ctr_ref[j]) - _dot(hi.astype(BF), cti_ref[j])
            if reverse:
                first = rowi == TM - 1
                sr, si = pltpu.roll(hr, TM - 1, 0), pltpu.roll(hi, TM - 1, 0)
            else:
                first = rowi == 0
                sr, si = pltpu.roll(hr, 1, 0), pltpu.roll(hi, 1, 0)
            hpr_ref[:, cols] = jnp.where(first, c0r_ref[:, cols], sr)
            hpi_ref[:, cols] = jnp.where(first, c0i_ref[:, cols], si)

    state = lambda: pl.BlockSpec((TM, SW), lambda i: (order(i), 0))
    return _call(body, name=name, grid=(nt,),
                 in_specs=[pl.BlockSpec((TM, AW), lambda i: (order(i), 3)),
                           _full((NBLK, UB, LN)), _full((NBLK, UB, LN)), _full((NBLK, LN, UB)), _full((NBLK, LN, UB)),
                           _full((PW_ROWS, SW)), _full((PW_ROWS, SW))],
                 out_specs=[pl.BlockSpec((TM, AW), lambda i: (order(i), 0)), state(), state()],
                 out_shape=[jax.ShapeDtypeStruct((t, AW), F32), jax.ShapeDtypeStruct((t, SW), F32),
                            jax.ShapeDtypeStruct((t, SW), F32)],
                 scratch=[pltpu.VMEM((TM, SW), F32), pltpu.VMEM((TM, SW), F32),
                          pltpu.VMEM((2 * GROUPS_PER_TILE, LN), F32), pltpu.VMEM((2 * GROUPS_PER_TILE, LN), F32),
                          pltpu.VMEM((2 * LN // 128, TM, 128), F32),
                          pltpu.VMEM((1, SW), F32), pltpu.VMEM((1, SW), F32),
                          pltpu.VMEM((1, SW), F32), pltpu.VMEM((1, SW), F32)])(
        p, bb_re, bb_im, ct_re, ct_im, pw_re, pw_im)


def s5_bwd(p, hp_re, hp_im, dy, bb_re, bb_im, ct_re, ct_im, pw_re, pw_im_conj, a_re, a_im, reverse, name):
    t = p.shape[0]
    nt = t // TM
    order = _tile_order("bwd_adj" if reverse else "fwd_adj", nt)

    def body(x_ref, hpr_ref, hpi_ref, dy_ref, bbr_ref, bbi_ref, ctr_ref, cti_ref, pr_ref, pi_ref, ar_ref, ai_ref,
             dx_ref, dar_ref, dai_ref, dbbr_ref, dbbi_ref, dcr_ref, dci_ref,
             gr_ref, gi_ref, er_ref, ei_ref, st_ref, cr_ref, ci_ref):
        @pl.when(pl.program_id(0) == 0)
        def _():
            for r in (cr_ref, ci_ref, dar_ref, dai_ref, dbbr_ref, dbbi_ref, dcr_ref, dci_ref):
                r[...] = jnp.zeros_like(r)

        xb = x_ref[...].astype(BF)
        dyb = dy_ref[...].astype(BF)
        for j in range(NBLK):
            cols = slice(j * LN, (j + 1) * LN)
            gr_ref[:, cols] = _dot(dyb[:, j * UB:(j + 1) * UB], ctr_ref[j], 1, 1)
            gi_ref[:, cols] = -_dot(dyb[:, j * UB:(j + 1) * UB], cti_ref[j], 1, 1)
        _scan_tile(gr_ref, gi_ref, er_ref, ei_ref, st_ref, cr_ref, ci_ref, pr_ref, pi_ref, not reverse)
        for j in range(NBLK):
            cols = slice(j * LN, (j + 1) * LN)
            xj = xb[:, j * UB:(j + 1) * UB]
            dyj = dyb[:, j * UB:(j + 1) * UB]
            hpr, hpi = hpr_ref[:, cols], hpi_ref[:, cols]
            gr, gi = gr_ref[:, cols], gi_ref[:, cols]
            ar, ai = ar_ref[:, cols], ai_ref[:, cols]
            hr = ar * hpr - ai * hpi + _dot(xj, bbr_ref[j])
            hi = ar * hpi + ai * hpr + _dot(xj, bbi_ref[j])
            dar_ref[:, cols] += jnp.sum(gr * hpr + gi * hpi, axis=0, keepdims=True)
            dai_ref[:, cols] += jnp.sum(gi * hpr - gr * hpi, axis=0, keepdims=True)
            grb, gib = gr.astype(BF), gi.astype(BF)
            dcr_ref[j] += _dot(dyj, hr.astype(BF), 0, 0)
            dci_ref[j] += -_dot(dyj, hi.astype(BF), 0, 0)
            dbbr_ref[j] += _dot(xj, grb, 0, 0)
            dbbi_ref[j] += _dot(xj, gib, 0, 0)
            dx_ref[:, j * UB:(j + 1) * UB] = _dot(grb, bbr_ref[j], 1, 1) + _dot(gib, bbi_ref[j], 1, 1)

    state = lambda: pl.BlockSpec((TM, SW), lambda i: (order(i), 0))
    blockd = lambda: _full((NBLK, UB, LN))
    return _call(body, name=name, grid=(nt,),
                 in_specs=[pl.BlockSpec((TM, AW), lambda i: (order(i), 3)), state(), state(),
                           pl.BlockSpec((TM, AW), lambda i: (order(i), 0)),
                           blockd(), blockd(), _full((NBLK, LN, UB)), _full((NBLK, LN, UB)),
                           _full((PW_ROWS, SW)), _full((PW_ROWS, SW)), _full((1, SW)), _full((1, SW))],
                 out_specs=[pl.BlockSpec((TM, AW), lambda i: (order(i), 0)), _full((1, SW)), _full((1, SW)),
                            blockd(), blockd(), blockd(), blockd()],
                 out_shape=[jax.ShapeDtypeStruct((t, AW), F32), jax.ShapeDtypeStruct((1, SW), F32),
                            jax.ShapeDtypeStruct((1, SW), F32)] + [jax.ShapeDtypeStruct((NBLK, UB, LN), F32)] * 4,
                 scratch=[pltpu.VMEM((TM, SW), F32), pltpu.VMEM((TM, SW), F32),
                          pltpu.VMEM((2 * GROUPS_PER_TILE, LN), F32), pltpu.VMEM((2 * GROUPS_PER_TILE, LN), F32),
                          pltpu.VMEM((2 * LN // 128, TM, 128), F32),
                          pltpu.VMEM((1, SW), F32), pltpu.VMEM((1, SW), F32)])(
        p, hp_re, hp_im, dy, bb_re, bb_im, ct_re, ct_im, pw_re, pw_im_conj, a_re, a_im)


def s5_dx_sum(dy, dsk, dxf, dxb, name):
    t = dy.shape[0]

    def body(dy_ref, d_ref, f_ref, b_ref, o_ref):
        o_ref[...] = (dy_ref[...] * d_ref[...] + f_ref[...] + b_ref[...]).astype(BF)

    tile = pl.BlockSpec((TM, AW), lambda i: (i, 0))
    return _call(body, name=name, grid=(t // TM,), in_specs=[tile, _full((1, AW)), tile, tile], out_specs=tile,
                 out_shape=jax.ShapeDtypeStruct((t, AW), BF))(dy, dsk, dxf, dxb)


XS_BLK = 3 * AW // 128


def _load_perm(refs):
    return jnp.concatenate(
        [jnp.concatenate([ref[pl.ds(r, SCAN_G, stride=SCAN_R), :] for ref in refs], axis=1) for r in range(SCAN_R)],
        axis=0)


def _store_perm(out_ref, val):
    for r in range(SCAN_R):
        for q in range(AW // 128):
            out_ref[q, pl.ds(r, SCAN_G, stride=SCAN_R), :] = val[r * SCAN_G:(r + 1) * SCAN_G, q * 128:(q + 1) * 128]


def _scan_perm(hr_ref, hi_ref, er_ref, ei_ref, cr_ref, ci_ref, pr_ref, pi_ref, reverse, hpr_ref=None, hpi_ref=None):
    gpt = SCAN_G
    nr = SCAN_R
    offsets = list(range(nr))[::-1] if reverse else list(range(nr))
    blocks = [slice(b * LN, (b + 1) * LN) for b in range(NBLK)]
    slab = lambda r: slice(r * gpt, (r + 1) * gpt)
    a1 = [(pr_ref[0:1, c], pi_ref[0:1, c]) for c in blocks]
    x = [None] * NBLK
    for r in offsets:
        for b, c in enumerate(blocks):
            if x[b] is None:
                x[b] = (hr_ref[slab(r), c], hi_ref[slab(r), c])
            else:
                (ar, ai), (xr, xi) = a1[b], x[b]
                x[b] = (hr_ref[slab(r), c] + ar * xr - ai * xi, hi_ref[slab(r), c] + ar * xi + ai * xr)
                hr_ref[slab(r), c] = x[b][0]
                hi_ref[slab(r), c] = x[b][1]
    an = [(pr_ref[nr - 1:nr, c], pi_ref[nr - 1:nr, c]) for c in blocks]
    k = [(cr_ref[:, c], ci_ref[:, c]) for c in blocks]
    for g in (range(gpt - 1, -1, -1) if reverse else range(gpt)):
        for b, c in enumerate(blocks):
            (ar, ai), (kr, ki), (xr, xi) = an[b], k[b], x[b]
            er_ref[g:g + 1, c] = kr
            ei_ref[g:g + 1, c] = ki
            k[b] = (xr[g:g + 1, :] + ar * kr - ai * ki, xi[g:g + 1, :] + ar * ki + ai * kr)
    for b, c in enumerate(blocks):
        cr_ref[:, c] = k[b][0]
        ci_ref[:, c] = k[b][1]
    for r in range(nr):
        prow = nr - 1 - r if reverse else r
        dst = r - 1 if reverse else r + 1
        for c in blocks:
            apr, api = pr_ref[prow:prow + 1, c], pi_ref[prow:prow + 1, c]
            cinr, cini = er_ref[:, c], ei_ref[:, c]
            hr = hr_ref[slab(r), c] + apr * cinr - api * cini
            hi = hi_ref[slab(r), c] + apr * cini + api * cinr
            hr_ref[slab(r), c] = hr
            hi_ref[slab(r), c] = hi
            if hpr_ref is not None:
                if 0 <= dst < nr:
                    hpr_ref[slab(dst), c] = hr
                    hpi_ref[slab(dst), c] = hi
                if r == (nr - 1 if reverse else 0):
                    hpr_ref[slab(r), c] = cinr
                    hpi_ref[slab(r), c] = cini


def s5p_fwd(p, bb_re, bb_im, ct_re, ct_im, pw_re, pw_im, reverse, name):
    t = p.shape[0]
    nt = t // TM
    order = _tile_order("bwd" if reverse else "fwd", nt)
    nq = AW // 128

    def body(*refs):
        x_refs = refs[:nq]
        bbr_ref, bbi_ref, ctr_ref, cti_ref, pr_ref, pi_ref, y_ref, hpr_ref, hpi_ref = refs[nq:nq + 9]
        hr_ref, hi_ref, er_ref, ei_ref, cr_ref, ci_ref = refs[nq + 9:]

        @pl.when(pl.program_id(0) == 0)
        def _():
            cr_ref[...] = jnp.zeros_like(cr_ref)
            ci_ref[...] = jnp.zeros_like(ci_ref)

        xb = _load_perm(x_refs).astype(BF)
        for j in range(NBLK):
            cols = slice(j * LN, (j + 1) * LN)
            hr_ref[:, cols] = _dot(xb[:, j * UB:(j + 1) * UB], bbr_ref[j])
            hi_ref[:, cols] = _dot(xb[:, j * UB:(j + 1) * UB], bbi_ref[j])
        _scan_perm(hr_ref, hi_ref, er_ref, ei_ref, cr_ref, ci_ref, pr_ref, pi_ref, reverse, hpr_ref, hpi_ref)
        y = jnp.concatenate(
            [_dot(hr_ref[:, j * LN:(j + 1) * LN].astype(BF), ctr_ref[j])
             - _dot(hi_ref[:, j * LN:(j + 1) * LN].astype(BF), cti_ref[j]) for j in range(NBLK)], axis=1)
        _store_perm(y_ref, y)

    state = lambda: pl.BlockSpec((TM, SW), lambda i: (order(i), 0))
    xspec = lambda q: pl.BlockSpec((TM, 128), lambda i: (order(i), XS_BLK + q))
    return _call(body, name=name, grid=(nt,),
                 in_specs=[xspec(q) for q in range(nq)]
                 + [_full((NBLK, UB, LN)), _full((NBLK, UB, LN)), _full((NBLK, LN, UB)), _full((NBLK, LN, UB)),
                    _full((PW_ROWS, SW)), _full((PW_ROWS, SW))],
                 out_specs=[pl.BlockSpec((nq, TM, 128), lambda i: (0, order(i), 0)), state(), state()],
                 out_shape=[jax.ShapeDtypeStruct((nq, t, 128), F32), jax.ShapeDtypeStruct((t, SW), F32),
                            jax.ShapeDtypeStruct((t, SW), F32)],
                 scratch=[pltpu.VMEM((TM, SW), F32), pltpu.VMEM((TM, SW), F32),
                          pltpu.VMEM((SCAN_G, SW), F32), pltpu.VMEM((SCAN_G, SW), F32),
                          pltpu.VMEM((1, SW), F32), pltpu.VMEM((1, SW), F32)])(
        *([p] * nq), bb_re, bb_im, ct_re, ct_im, pw_re, pw_im)


def s5p_bwd(p, hp_re, hp_im, dy, bb_re, bb_im, ct_re, ct_im, pw_re, pw_im_conj, a_re, a_im, reverse, name):
    t = p.shape[0]
    nt = t // TM
    order = _tile_order("bwd_adj" if reverse else "fwd_adj", nt)
    nq = AW // 128

    def body(*refs):
        x_refs, dy_refs = refs[:nq], refs[nq:2 * nq]
        (hpr_ref, hpi_ref, bbr_ref, bbi_ref, ctr_ref, cti_ref, pr_ref, pi_ref, ar_ref, ai_ref,
         dx_ref, dar_ref, dai_ref, dbbr_ref, dbbi_ref, dcr_ref, dci_ref,
         gr_ref, gi_ref, er_ref, ei_ref, cr_ref, ci_ref) = refs[2 * nq:]

        @pl.when(pl.program_id(0) == 0)
        def _():
            for r in (cr_ref, ci_ref, dar_ref, dai_ref, dbbr_ref, dbbi_ref, dcr_ref, dci_ref):
                r[...] = jnp.zeros_like(r)

        xb = _load_perm(x_refs).astype(BF)
        dyb = _load_perm(dy_refs).astype(BF)
        for j in range(NBLK):
            cols = slice(j * LN, (j + 1) * LN)
            gr_ref[:, cols] = _dot(dyb[:, j * UB:(j + 1) * UB], ctr_ref[j], 1, 1)
            gi_ref[:, cols] = -_dot(dyb[:, j * UB:(j + 1) * UB], cti_ref[j], 1, 1)
        _scan_perm(gr_ref, gi_ref, er_ref, ei_ref, cr_ref, ci_ref, pr_ref, pi_ref, not reverse)
        dxs = []
        for j in range(NBLK):
            cols = slice(j * LN, (j + 1) * LN)
            xj = xb[:, j * UB:(j + 1) * UB]
            dyj = dyb[:, j * UB:(j + 1) * UB]
            hpr, hpi = hpr_ref[:, cols], hpi_ref[:, cols]
            gr, gi = gr_ref[:, cols], gi_ref[:, cols]
            ar, ai = ar_ref[:, cols], ai_ref[:, cols]
            hr = ar * hpr - ai * hpi + _dot(xj, bbr_ref[j])
            hi = ar * hpi + ai * hpr + _dot(xj, bbi_ref[j])
            dar_ref[:, cols] += jnp.sum(gr * hpr + gi * hpi, axis=0, keepdims=True)
            dai_ref[:, cols] += jnp.sum(gi * hpr - gr * hpi, axis=0, keepdims=True)
            grb, gib = gr.astype(BF), gi.astype(BF)
            dcr_ref[j] += _dot(dyj, hr.astype(BF), 0, 0)
            dci_ref[j] += -_dot(dyj, hi.astype(BF), 0, 0)
            dbbr_ref[j] += _dot(xj, grb, 0, 0)
            dbbi_ref[j] += _dot(xj, gib, 0, 0)
            dxs.append(_dot(grb, bbr_ref[j], 1, 1) + _dot(gib, bbi_ref[j], 1, 1))
        _store_perm(dx_ref, jnp.concatenate(dxs, axis=1))

    state = lambda: pl.BlockSpec((TM, SW), lambda i: (order(i), 0))
    blockd = lambda: _full((NBLK, UB, LN))
    xspec = lambda q: pl.BlockSpec((TM, 128), lambda i: (order(i), XS_BLK + q))
    dyspec = lambda q: pl.BlockSpec((TM, 128), lambda i: (order(i), q))
    return _call(body, name=name, grid=(nt,),
                 in_specs=[xspec(q) for q in range(nq)] + [dyspec(q) for q in range(nq)]
                 + [state(), state(), blockd(), blockd(), _full((NBLK, LN, UB)), _full((NBLK, LN, UB)),
                    _full((PW_ROWS, SW)), _full((PW_ROWS, SW)), _full((1, SW)), _full((1, SW))],
                 out_specs=[pl.BlockSpec((nq, TM, 128), lambda i: (0, order(i), 0)), _full((1, SW)), _full((1, SW)),
                            blockd(), blockd(), blockd(), blockd()],
                 out_shape=[jax.ShapeDtypeStruct((nq, t, 128), F32), jax.ShapeDtypeStruct((1, SW), F32),
                            jax.ShapeDtypeStruct((1, SW), F32)] + [jax.ShapeDtypeStruct((NBLK, UB, LN), F32)] * 4,
                 scratch=[pltpu.VMEM((TM, SW), F32), pltpu.VMEM((TM, SW), F32),
                          pltpu.VMEM((SCAN_G, SW), F32), pltpu.VMEM((SCAN_G, SW), F32),
                          pltpu.VMEM((1, SW), F32), pltpu.VMEM((1, SW), F32)])(
        *([p] * nq), *([dy] * nq), hp_re, hp_im, bb_re, bb_im, ct_re, ct_im, pw_re, pw_im_conj, a_re, a_im)


def _scan2(br_ref, bi_ref, or_ref, oi_ref, h_off, cin_off, er_ref, ei_ref, cr_ref, ci_ref, pr_ref, pi_ref, reverse):
    gpt = SCAN_G
    nr = SCAN_R
    offsets = list(range(nr))[::-1] if reverse else list(range(nr))
    blocks = [slice(b * LN, (b + 1) * LN) for b in range(NBLK)]
    slab = lambda r: slice(r * gpt, (r + 1) * gpt)
    a1 = [(pr_ref[0:1, c], pi_ref[0:1, c]) for c in blocks]
    x = [None] * NBLK
    for r in offsets:
        for b, c in enumerate(blocks):
            if x[b] is None:
                x[b] = (br_ref[slab(r), c], bi_ref[slab(r), c])
            else:
                (ar, ai), (xr, xi) = a1[b], x[b]
                x[b] = (br_ref[slab(r), c] + ar * xr - ai * xi, bi_ref[slab(r), c] + ar * xi + ai * xr)
    an = [(pr_ref[nr - 1:nr, c], pi_ref[nr - 1:nr, c]) for c in blocks]
    k = [(cr_ref[:, c], ci_ref[:, c]) for c in blocks]
    for g in (range(gpt - 1, -1, -1) if reverse else range(gpt)):
        for b, c in enumerate(blocks):
            (ar, ai), (kr, ki), (xr, xi) = an[b], k[b], x[b]
            er_ref[g:g + 1, c] = kr
            ei_ref[g:g + 1, c] = ki
            k[b] = (xr[g:g + 1, :] + ar * kr - ai * ki, xi[g:g + 1, :] + ar * ki + ai * kr)
    for b, c in enumerate(blocks):
        cr_ref[:, c] = k[b][0]
        ci_ref[:, c] = k[b][1]
        x[b] = (er_ref[:, c], ei_ref[:, c])
        if cin_off is not None:
            or_ref[cin_off:cin_off + gpt, c] = x[b][0]
            oi_ref[cin_off:cin_off + gpt, c] = x[b][1]
    for r in offsets:
        for b, c in enumerate(blocks):
            (ar, ai), (xr, xi) = a1[b], x[b]
            x[b] = (br_ref[slab(r), c] + ar * xr - ai * xi, bi_ref[slab(r), c] + ar * xi + ai * xr)
            or_ref[h_off + r * gpt:h_off + (r + 1) * gpt, c] = x[b][0]
            oi_ref[h_off + r * gpt:h_off + (r + 1) * gpt, c] = x[b][1]


HS_ROWS = TM + SCAN_G


def _hs_offsets(reverse):
    return (0, SCAN_G) if reverse else (SCAN_G, 0)


def s5q_fwd(p, bb_re, bb_im, ct_re, ct_im, pw_re, pw_im, reverse, name):
    t = p.shape[0]
    nt = t // TM
    order = _tile_order("bwd" if reverse else "fwd", nt)
    nq = AW // 128
    h_off, p_off = _hs_offsets(reverse)

    def body(*refs):
        x_refs = refs[:nq]
        bbr_ref, bbi_ref, ctr_ref, cti_ref, pr_ref, pi_ref, y_ref, hsr_ref, hsi_ref = refs[nq:nq + 9]
        br_ref, bi_ref, er_ref, ei_ref, cr_ref, ci_ref = refs[nq + 9:]

        @pl.when(pl.program_id(0) == 0)
        def _():
            cr_ref[...] = jnp.zeros_like(cr_ref)
            ci_ref[...] = jnp.zeros_like(ci_ref)

        xb = _load_perm(x_refs).astype(BF)
        for j in range(NBLK):
            cols = slice(j * LN, (j + 1) * LN)
            br_ref[:, cols] = _dot(xb[:, j * UB:(j + 1) * UB], bbr_ref[j])
            bi_ref[:, cols] = _dot(xb[:, j * UB:(j + 1) * UB], bbi_ref[j])
        _scan2(br_ref, bi_ref, hsr_ref, hsi_ref, h_off, TM if reverse else 0, er_ref, ei_ref, cr_ref, ci_ref,
               pr_ref, pi_ref, reverse)
        y = jnp.concatenate(
            [_dot(hsr_ref[h_off:h_off + TM, j * LN:(j + 1) * LN].astype(BF), ctr_ref[j])
             - _dot(hsi_ref[h_off:h_off + TM, j * LN:(j + 1) * LN].astype(BF), cti_ref[j]) for j in range(NBLK)], axis=1)
        _store_perm(y_ref, y)

    state = lambda: pl.BlockSpec((HS_ROWS, SW), lambda i: (order(i), 0))
    xspec = lambda q: pl.BlockSpec((TM, 128), lambda i: (order(i), XS_BLK + q))
    return _call(body, name=name, grid=(nt,),
                 in_specs=[xspec(q) for q in range(nq)]
                 + [_full((NBLK, UB, LN)), _full((NBLK, UB, LN)), _full((NBLK, LN, UB)), _full((NBLK, LN, UB)),
                    _full((PW_ROWS, SW)), _full((PW_ROWS, SW))],
                 out_specs=[pl.BlockSpec((nq, TM, 128), lambda i: (0, order(i), 0)), state(), state()],
                 out_shape=[jax.ShapeDtypeStruct((nq, t, 128), F32), jax.ShapeDtypeStruct((nt * HS_ROWS, SW), F32),
                            jax.ShapeDtypeStruct((nt * HS_ROWS, SW), F32)],
                 scratch=[pltpu.VMEM((TM, SW), F32), pltpu.VMEM((TM, SW), F32),
                          pltpu.VMEM((SCAN_G, SW), F32), pltpu.VMEM((SCAN_G, SW), F32),
                          pltpu.VMEM((1, SW), F32), pltpu.VMEM((1, SW), F32)])(
        *([p] * nq), bb_re, bb_im, ct_re, ct_im, pw_re, pw_im)


def s5q_bwd(p, hs_re, hs_im, dy, bb_re, bb_im, ct_re, ct_im, pw_re, pw_im_conj, reverse, name):
    t = p.shape[0]
    nt = t // TM
    order = _tile_order("bwd_adj" if reverse else "fwd_adj", nt)
    nq = AW // 128
    h_off, p_off = _hs_offsets(reverse)

    def body(*refs):
        x_refs, dy_refs = refs[:nq], refs[nq:2 * nq]
        (hsr_ref, hsi_ref, bbr_ref, bbi_ref, ctr_ref, cti_ref, pr_ref, pi_ref,
         dx_ref, dar_ref, dai_ref, dbbr_ref, dbbi_ref, dcr_ref, dci_ref,
         qr_ref, qi_ref, gr_ref, gi_ref, er_ref, ei_ref, cr_ref, ci_ref) = refs[2 * nq:]

        @pl.when(pl.program_id(0) == 0)
        def _():
            for r in (cr_ref, ci_ref, dar_ref, dai_ref, dbbr_ref, dbbi_ref, dcr_ref, dci_ref):
                r[...] = jnp.zeros_like(r)

        xb = _load_perm(x_refs).astype(BF)
        dyb = _load_perm(dy_refs).astype(BF)
        for j in range(NBLK):
            cols = slice(j * LN, (j + 1) * LN)
            qr_ref[:, cols] = _dot(dyb[:, j * UB:(j + 1) * UB], ctr_ref[j], 1, 1)
            qi_ref[:, cols] = -_dot(dyb[:, j * UB:(j + 1) * UB], cti_ref[j], 1, 1)
        _scan2(qr_ref, qi_ref, gr_ref, gi_ref, 0, None, er_ref, ei_ref, cr_ref, ci_ref, pr_ref, pi_ref, not reverse)
        dxs = []
        for j in range(NBLK):
            cols = slice(j * LN, (j + 1) * LN)
            xj = xb[:, j * UB:(j + 1) * UB]
            dyj = dyb[:, j * UB:(j + 1) * UB]
            hpr, hpi = hsr_ref[p_off:p_off + TM, cols], hsi_ref[p_off:p_off + TM, cols]
            gr, gi = gr_ref[:, cols], gi_ref[:, cols]
            dar_ref[:, cols] += jnp.sum(gr * hpr + gi * hpi, axis=0, keepdims=True)
            dai_ref[:, cols] += jnp.sum(gi * hpr - gr * hpi, axis=0, keepdims=True)
            grb, gib = gr.astype(BF), gi.astype(BF)
            dcr_ref[j] += _dot(dyj, hsr_ref[h_off:h_off + TM, cols].astype(BF), 0, 0)
            dci_ref[j] += -_dot(dyj, hsi_ref[h_off:h_off + TM, cols].astype(BF), 0, 0)
            dbbr_ref[j] += _dot(xj, grb, 0, 0)
            dbbi_ref[j] += _dot(xj, gib, 0, 0)
            dxs.append(_dot(grb, bbr_ref[j], 1, 1) + _dot(gib, bbi_ref[j], 1, 1))
        _store_perm(dx_ref, jnp.concatenate(dxs, axis=1))

    state = lambda: pl.BlockSpec((HS_ROWS, SW), lambda i: (order(i), 0))
    blockd = lambda: _full((NBLK, UB, LN))
    xspec = lambda q: pl.BlockSpec((TM, 128), lambda i: (order(i), XS_BLK + q))
    dyspec = lambda q: pl.BlockSpec((TM, 128), lambda i: (order(i), q))
    return _call(body, name=name, grid=(nt,),
                 in_specs=[xspec(q) for q in range(nq)] + [dyspec(q) for q in range(nq)]
                 + [state(), state(), blockd(), blockd(), _full((NBLK, LN, UB)), _full((NBLK, LN, UB)),
                    _full((PW_ROWS, SW)), _full((PW_ROWS, SW))],
                 out_specs=[pl.BlockSpec((nq, TM, 128), lambda i: (0, order(i), 0)), _full((1, SW)), _full((1, SW)),
                            blockd(), blockd(), blockd(), blockd()],
                 out_shape=[jax.ShapeDtypeStruct((nq, t, 128), F32), jax.ShapeDtypeStruct((1, SW), F32),
                            jax.ShapeDtypeStruct((1, SW), F32)] + [jax.ShapeDtypeStruct((NBLK, UB, LN), F32)] * 4,
                 scratch=[pltpu.VMEM((TM, SW), F32), pltpu.VMEM((TM, SW), F32),
                          pltpu.VMEM((TM, SW), F32), pltpu.VMEM((TM, SW), F32),
                          pltpu.VMEM((SCAN_G, SW), F32), pltpu.VMEM((SCAN_G, SW), F32),
                          pltpu.VMEM((1, SW), F32), pltpu.VMEM((1, SW), F32)])(
        *([p] * nq), *([dy] * nq), hs_re, hs_im, bb_re, bb_im, ct_re, ct_im, pw_re, pw_im_conj)


def s5p_dx_sum(dy, dsk, dxf, dxb, dp, name):
    t = dy.shape[0]
    nq = AW // 128

    def body(dy_ref, d_ref, f_ref, b_ref, dp_ref, o_ref):
        o_ref[...] = (dy_ref[...] * d_ref[...] + _cat_lanes(f_ref) + _cat_lanes(b_ref)).astype(BF)

    tile = pl.BlockSpec((TM, AW), lambda i: (i, 0))
    blk4 = pl.BlockSpec((nq, TM, 128), lambda i: (0, i, 0))
    return pl.pallas_call(
        body, name=name, grid=(t // TM,),
        in_specs=[tile, _full((1, AW)), blk4, blk4, pl.BlockSpec(memory_space=pl.ANY)],
        out_specs=pl.BlockSpec((TM, AW), lambda i: (i, 3)), out_shape=jax.ShapeDtypeStruct(dp.shape, dp.dtype),
        input_output_aliases={4: 0},
        compiler_params=pltpu.CompilerParams(vmem_limit_bytes=VMEM_LIMIT_BYTES))(dy, dsk, dxf, dxb, dp)


SCALE = HD ** -0.5
NHEAD_NORM = NQ + NKV


def _partner(x):
    half0 = (lax.broadcasted_iota(jnp.int32, (1, HD), 1) % 64) < 32
    return jnp.where(half0, pltpu.roll(x, HD - 32, 1), pltpu.roll(x, 32, 1))


def attn_prep(p, qg, kg, cos, sins, name):
    t = p.shape[0]

    def body(p_ref, qg_ref, kg_ref, cos_ref, sin_ref, o_ref):
        cv, sv = cos_ref[...], sin_ref[...]
        for h in range(NHEAD_NORM):
            cols = slice(h * HD, (h + 1) * HD)
            blk = p_ref[:, cols]
            r = lax.rsqrt(jnp.mean(blk * blk, axis=-1, keepdims=True) + EPS)
            xn = blk * r * (qg_ref[...] if h < NQ else kg_ref[...])
            rot = xn * cv + _partner(xn) * sv
            o_ref[:, cols] = ((rot * SCALE) if h < NQ else rot).astype(BF)
        vcols = slice(NHEAD_NORM * HD, (NHEAD_NORM + NKV) * HD)
        o_ref[:, vcols] = p_ref[:, vcols].astype(BF)

    w = (NHEAD_NORM + NKV) * HD
    tile = lambda ww: pl.BlockSpec((TM, ww), lambda i: (i, 0))
    return _call(body, name=name, grid=(t // TM,),
                 in_specs=[tile(w), _full((1, HD)), _full((1, HD)), tile(HD), tile(HD)],
                 out_specs=tile(w), out_shape=jax.ShapeDtypeStruct((t, w), BF))(p, qg, kg, cos, sins)


def attn_prep_bwd(p, dq, dk, dv, qg, kg, cos, sins, dp, name):
    t = p.shape[0]

    def body(p_ref, dq_ref, dk_ref, dv_ref, qg_ref, kg_ref, cos_ref, sin_ref, dp_ref, o_ref, dqg_ref, dkg_ref):
        @pl.when(pl.program_id(0) == 0)
        def _():
            dqg_ref[...] = jnp.zeros_like(dqg_ref)
            dkg_ref[...] = jnp.zeros_like(dkg_ref)

        cv, sv = cos_ref[...], sin_ref[...]
        for h in range(NHEAD_NORM):
            cols = slice(h * HD, (h + 1) * HD)
            blk = p_ref[:, cols]
            r = lax.rsqrt(jnp.mean(blk * blk, axis=-1, keepdims=True) + EPS)
            xh = blk * r
            if h < NQ:
                drot = dq_ref[:, cols] * SCALE
                gv, dg_ref = qg_ref[...], dqg_ref
            else:
                drot = dk_ref[:, (h - NQ) * HD:(h - NQ + 1) * HD]
                gv, dg_ref = kg_ref[...], dkg_ref
            dxn = drot * cv + _partner(drot * sv)
            dg_ref[...] += jnp.sum(dxn * xh, axis=0, keepdims=True)
            dxh = dxn * gv
            o_ref[:, cols] = (r * (dxh - xh * jnp.mean(dxh * xh, axis=-1, keepdims=True))).astype(BF)
        o_ref[:, NHEAD_NORM * HD:(NHEAD_NORM + NKV) * HD] = dv_ref[...].astype(BF)

    w = (NHEAD_NORM + NKV) * HD
    tile = lambda ww: pl.BlockSpec((TM, ww), lambda i: (i, 0))
    return pl.pallas_call(
        body, name=name, grid=(t // TM,),
        in_specs=[tile(w), tile(NQ * HD), tile(NKV * HD), tile(NKV * HD), _full((1, HD)), _full((1, HD)),
                  tile(HD), tile(HD), pl.BlockSpec(memory_space=pl.ANY)],
        out_specs=[tile(w), _full((1, HD)), _full((1, HD))],
        out_shape=[jax.ShapeDtypeStruct(dp.shape, dp.dtype), jax.ShapeDtypeStruct((1, HD), F32),
                   jax.ShapeDtypeStruct((1, HD), F32)],
        input_output_aliases={8: 0},
        compiler_params=pltpu.CompilerParams(vmem_limit_bytes=VMEM_LIMIT_BYTES))(p, dq, dk, dv, qg, kg, cos, sins, dp)


KCOL = NQ
VCOL = NQ + NKV
GCOL = (NQ + 2 * NKV)
QPK = NQ // NKV
ATT_KCHUNK = 512


def attn_fwd(qkv, p, name):
    t = qkv.shape[0]

    def body(q_ref, k_ref, v_ref, g_ref, o_ref, mix_ref, lse_ref):
        def attend(nk):
            q = q_ref[...]
            chunks = [(k0, min(k0 + 2 * ATT_KCHUNK, nk)) for k0 in range(0, nk, 2 * ATT_KCHUNK)]
            s_next = _dot(q, k_ref[chunks[0][0]:chunks[0][1], :], 1, 1)
            m = l = acc = None
            for ci, (k0, k1) in enumerate(chunks):
                s = s_next
                if ci + 1 < len(chunks):
                    s_next = _dot(q, k_ref[chunks[ci + 1][0]:chunks[ci + 1][1], :], 1, 1)
                mc = jnp.max(s, axis=-1, keepdims=True)
                m_new = mc if m is None else jnp.maximum(m, mc)
                pe = jnp.exp(s - m_new)
                lc = jnp.sum(pe, axis=-1, keepdims=True)
                pv = _dot(pe.astype(BF), v_ref[k0:k1, :])
                if m is None:
                    l, acc = lc, pv
                else:
                    alpha = jnp.exp(m - m_new)
                    l, acc = alpha * l + lc, alpha * acc + pv
                m = m_new
            o = acc / l
            gt = g_ref[...]
            o_ref[...] = o
            mix_ref[...] = (o * (gt * _sig(gt))).astype(BF)
            lse_ref[...] = jnp.broadcast_to(m + jnp.log(l), (TM, HD))

        pl.when(pl.program_id(1) == 0)(lambda: attend(NC))
        pl.when(pl.program_id(1) > 0)(lambda: attend(t))

    blk = pl.BlockSpec((TM, HD), lambda h, i: (i, h))
    return _call(body, name=name, grid=(NQ, t // TM),
                 in_specs=[blk, pl.BlockSpec((t, HD), lambda h, i: (0, KCOL + h // QPK)),
                           pl.BlockSpec((t, HD), lambda h, i: (0, VCOL + h // QPK)),
                           pl.BlockSpec((TM, HD), lambda h, i: (i, GCOL + h))],
                 out_specs=[blk, blk, blk],
                 out_shape=[jax.ShapeDtypeStruct((t, NQ * HD), F32), jax.ShapeDtypeStruct((t, NQ * HD), BF),
                            jax.ShapeDtypeStruct((t, NQ * HD), F32)])(qkv, qkv, qkv, p)


def attn_bwd(qkv, p, dmix, o, lse, name):
    t = qkv.shape[0]

    def body(q_ref, k_ref, v_ref, g_ref, dm_ref, o_ref, lse_ref, dq_ref, dg_ref, dk_ref, dv_ref):
        i = pl.program_id(2)

        @pl.when((pl.program_id(1) == 0) & (i == 0))
        def _():
            dk_ref[...] = jnp.zeros_like(dk_ref)
            dv_ref[...] = jnp.zeros_like(dv_ref)

        gt = g_ref[...]
        sg = _sig(gt)
        ov = o_ref[...]
        dmv = dm_ref[...]
        dg_ref[...] = (dmv * ov * (sg * (1.0 + gt * (1.0 - sg)))).astype(BF)
        do = dmv * (gt * sg)
        dr = jnp.sum(do * ov, axis=-1, keepdims=True)
        dob = do.astype(BF)

        def bwd(nk):
            q = q_ref[...]
            lse = lse_ref[:, 0:1]
            chunks = [slice(k0, min(k0 + ATT_KCHUNK, nk)) for k0 in range(0, nk, ATT_KCHUNK)]
            nxt = (_dot(q, k_ref[chunks[0], :], 1, 1), _dot(dob, v_ref[chunks[0], :], 1, 1))
            dq = None
            for ci, keys in enumerate(chunks):
                s, dp = nxt
                if ci + 1 < len(chunks):
                    nxt = (_dot(q, k_ref[chunks[ci + 1], :], 1, 1), _dot(dob, v_ref[chunks[ci + 1], :], 1, 1))
                pe = jnp.exp(s - lse)
                dsb = (pe * (dp - dr)).astype(BF)
                part = _dot(dsb, k_ref[keys, :])
                dq = part if dq is None else dq + part
                dv_ref[keys, :] += _dot(pe.astype(BF), dob, 0, 0)
                dk_ref[keys, :] += _dot(dsb, q, 0, 0)
            dq_ref[...] = dq

        pl.when(i == 0)(lambda: bwd(NC))
        pl.when(i > 0)(lambda: bwd(t))

    blk = pl.BlockSpec((TM, HD), lambda kv, g, i: (i, kv * QPK + g))
    acc = pl.BlockSpec((t, HD), lambda kv, g, i: (0, kv))
    return _call(body, name=name, grid=(NKV, QPK, t // TM),
                 in_specs=[blk, pl.BlockSpec((t, HD), lambda kv, g, i: (0, KCOL + kv)),
                           pl.BlockSpec((t, HD), lambda kv, g, i: (0, VCOL + kv)),
                           pl.BlockSpec((TM, HD), lambda kv, g, i: (i, GCOL + kv * QPK + g)), blk, blk, blk],
                 out_specs=[blk, pl.BlockSpec((TM, HD), lambda kv, g, i: (i, GCOL + kv * QPK + g)), acc, acc],
                 out_shape=[jax.ShapeDtypeStruct((t, NQ * HD), F32), jax.ShapeDtypeStruct((t, (GCOL + NQ) * HD), BF),
                            jax.ShapeDtypeStruct((t, NKV * HD), F32), jax.ShapeDtypeStruct((t, NKV * HD), F32)])(
        qkv, qkv, qkv, p, dmix, o, lse)


def _row_tile(rows, row_bytes, cap=2 * 1024 * 1024):
    if rows * row_bytes <= cap or rows % 8:
        return rows
    tr = rows
    while tr * row_bytes > cap and tr % 16 == 0:
        tr //= 2
    return tr


def _adamw_update(w_ref, g_ref, m_ref, v_ref, d_ref, nm_ref, nv_ref):
    gv = g_ref[...]
    m2 = ADAM_B1 * m_ref[...] + (1.0 - ADAM_B1) * gv
    v2 = ADAM_B2 * v_ref[...] + (1.0 - ADAM_B2) * (gv * gv)
    mh = m2 / (1.0 - ADAM_B1 ** ADAM_STEP)
    vh = v2 / (1.0 - ADAM_B2 ** ADAM_STEP)
    d_ref[...] = -ADAM_LR * (mh / (jnp.sqrt(vh) + ADAM_EPS) + ADAM_WD * w_ref[...])
    nm_ref[...] = m2
    nv_ref[...] = v2


def adamw_many(ws, gs, ms, vs, name):
    n = len(ws)

    def body(*refs):
        for k in range(n):
            _adamw_update(*[refs[j * n + k] for j in range(7)])

    shapes = [jax.ShapeDtypeStruct(w.shape, F32) for w in ws]
    res = _call(body, name=name, out_shape=shapes * 3)(*ws, *gs, *ms, *vs)
    return res[:n], res[n:2 * n], res[2 * n:]


def adamw(w, g, m, v, name):
    r, cdim = w.shape
    tr = _row_tile(r, 4 * max(cdim, 128))

    def body(w_ref, g_ref, m_ref, v_ref, d_ref, nm_ref, nv_ref):
        _adamw_update(w_ref, g_ref, m_ref, v_ref, d_ref, nm_ref, nv_ref)

    tile = pl.BlockSpec((tr, cdim), lambda i: (i, 0))
    sh = jax.ShapeDtypeStruct((r, cdim), F32)
    return _call(body, name=name, grid=(r // tr,), in_specs=[tile] * 4, out_specs=[tile] * 3,
                 out_shape=[sh, sh, sh])(w, g, m, v)


def sum_lead(a, name, out_dtype=F32):
    n, r, cdim = a.shape
    tr = _row_tile(r, 4 * n * max(cdim, 128))

    def body(a_ref, o_ref):
        acc = a_ref[0].astype(F32)
        for k in range(1, n):
            acc = acc + a_ref[k].astype(F32)
        o_ref[...] = acc.astype(o_ref.dtype)

    return _call(body, name=name, grid=(r // tr,),
                 in_specs=[pl.BlockSpec((n, tr, cdim), lambda i: (0, i, 0))],
                 out_specs=pl.BlockSpec((tr, cdim), lambda i: (i, 0)),
                 out_shape=jax.ShapeDtypeStruct((r, cdim), out_dtype))(a)


_FLIPS = {"xy": [(1, 0, 0), (0, 1, 0), (1, 1, 0)], "c": [(0, 0, 1)],
          "all": [(0, 0, 1), (0, 1, 0), (0, 1, 1), (1, 0, 0), (1, 0, 1), (1, 1, 0), (1, 1, 1)]}
_GROUP_SIZE = {"xy": 4, "c": 2, "all": 8}


def _group_index(group, x, y, c):
    return {"xy": 2 * x + y, "c": c, "all": 4 * x + 2 * y + c}[group]


def exchange(items, name):
    plan = []
    for arr, group, kind in items:
        chunk = arr.shape if kind == "gather" else arr.shape[1:]
        plan.append((group, kind, chunk))
    ncopy = sum(len(_FLIPS[g]) for g, _, _ in plan)
    nitem = len(plan)

    def body(*refs):
        srcs, dsts = refs[:nitem], refs[nitem:2 * nitem]
        send_sems, recv_sems, local_sems = refs[2 * nitem:]
        x, y, c = lax.axis_index("x"), lax.axis_index("y"), lax.axis_index("c")
        sends, recvs, locals_ = [], [], []
        n = 0
        for k, (group, kind, _) in enumerate(plan):
            me = _group_index(group, x, y, c)
            own = srcs[k] if kind == "gather" else srcs[k].at[me]
            locals_.append(pltpu.make_async_copy(own, dsts[k].at[me], local_sems.at[k]))
            for fx, fy, fc in _FLIPS[group]:
                px, py, pc = (1 - x if fx else x), (1 - y if fy else y), (1 - c if fc else c)
                peer = _group_index(group, px, py, pc)
                src = srcs[k] if kind == "gather" else srcs[k].at[peer]
                sends.append(pltpu.make_async_remote_copy(
                    src_ref=src, dst_ref=dsts[k].at[me], send_sem=send_sems.at[n], recv_sem=recv_sems.at[n],
                    device_id=(px, py, pc), device_id_type=MESH))
                recvs.append(pltpu.make_async_remote_copy(
                    src_ref=src, dst_ref=dsts[k].at[peer], send_sem=send_sems.at[n], recv_sem=recv_sems.at[n],
                    device_id=(px, py, pc), device_id_type=MESH))
                n += 1
        for cp in locals_ + sends:
            cp.start()
        for cp in recvs:
            cp.wait_recv()
        for cp in sends:
            cp.wait_send()
        for cp in locals_:
            cp.wait()

    anyspec = pl.BlockSpec(memory_space=pl.ANY)
    outs = [jax.ShapeDtypeStruct((_GROUP_SIZE[g],) + tuple(chunk), arr.dtype)
            for (arr, _, _), (g, _, chunk) in zip(items, plan)]
    res = pl.pallas_call(
        body, name=name, out_shape=outs, in_specs=[anyspec] * nitem, out_specs=[anyspec] * nitem,
        scratch_shapes=[pltpu.SemaphoreType.DMA((ncopy,)), pltpu.SemaphoreType.DMA((ncopy,)),
                        pltpu.SemaphoreType.DMA((nitem,))],
        compiler_params=pltpu.CompilerParams(has_side_effects=True))(*[a for a, _, _ in items])
    return list(res)


D2D_PIECES = 4


def d2d(items, name):
    n = len(items)
    swaps = [k for k, (_, kind) in enumerate(items) if kind == "swap"]

    def pieces_of(rows):
        npc = D2D_PIECES if rows % (8 * D2D_PIECES) == 0 else 1
        return npc, rows // npc

    ncopy = sum(pieces_of(a.shape[0] if kind == "gather" else a.shape[1])[0] for a, kind in items)

    def body(*refs):
        srcs, outs = refs[:n], refs[n:2 * n]
        stages = dict(zip(swaps, refs[2 * n:2 * n + len(swaps)]))
        send_sems, recv_sems, local_sems = refs[2 * n + len(swaps):]
        x, y, c = lax.axis_index("x"), lax.axis_index("y"), lax.axis_index("c")
        sib = (x, y, 1 - c)

        def remote(src, dst, q):
            return pltpu.make_async_remote_copy(src_ref=src, dst_ref=dst, send_sem=send_sems.at[q],
                                                recv_sem=recv_sems.at[q], device_id=sib, device_id_type=MESH)

        copies = []
        q = 0
        for k, (arr, kind) in enumerate(items):
            npc, pr = pieces_of(arr.shape[0] if kind == "gather" else arr.shape[1])
            for pc in range(npc):
                rs = pl.ds(pc * pr, pr)
                if kind == "gather":
                    mine, theirs = outs[k].at[c, rs], outs[k].at[1 - c, rs]
                    copies.append((pltpu.make_async_copy(srcs[k].at[rs], mine, local_sems.at[q]),
                                   remote(mine, mine, q), remote(theirs, theirs, q)))
                else:
                    stage, land = stages[k].at[rs], outs[k].at[rs]
                    copies.append((pltpu.make_async_copy(srcs[k].at[1 - c, rs], stage, local_sems.at[q]),
                                   remote(stage, land, q), remote(stage, land, q)))
                q += 1
        for loc, _, _ in copies:
            loc.start()
        for loc, send, _ in copies:
            loc.wait()
            send.start()
        for _, _, recv in copies:
            recv.wait_recv()
        for _, send, _ in copies:
            send.wait_send()

    outs = [jax.ShapeDtypeStruct((2,) + a.shape if kind == "gather" else a.shape[1:], a.dtype) for a, kind in items]
    res = pl.pallas_call(
        body, name=name, out_shape=outs, in_specs=[pl.BlockSpec(memory_space=pl.ANY)] * n,
        out_specs=[pl.BlockSpec(memory_space=pltpu.VMEM)] * n,
        scratch_shapes=[pltpu.VMEM(items[k][0].shape[1:], items[k][0].dtype) for k in swaps]
        + [pltpu.SemaphoreType.DMA((ncopy,)), pltpu.SemaphoreType.DMA((ncopy,)), pltpu.SemaphoreType.DMA((ncopy,))],
        compiler_params=pltpu.CompilerParams(has_side_effects=True, vmem_limit_bytes=VMEM_LIMIT_BYTES))(
        *[a for a, _ in items])
    return list(res)


def sum_own(pair, got, name, out_dtype=F32):
    _, r, cdim = pair.shape
    tr = _row_tile(r, 4 * 2 * max(cdim, 128))

    def body(c_ref, p_ref, g_ref, o_ref):
        o_ref[...] = (p_ref[0] + g_ref[...]).astype(o_ref.dtype)

    me = lax.axis_index("c").astype(jnp.int32).reshape(1)
    return pl.pallas_call(
        body, name=name, out_shape=jax.ShapeDtypeStruct((r, cdim), out_dtype),
        grid_spec=pltpu.PrefetchScalarGridSpec(
            num_scalar_prefetch=1, grid=(r // tr,),
            in_specs=[pl.BlockSpec((1, tr, cdim), lambda i, c_ref: (c_ref[0], i, 0)),
                      pl.BlockSpec((tr, cdim), lambda i, c_ref: (i, 0))],
            out_specs=pl.BlockSpec((tr, cdim), lambda i, c_ref: (i, 0))),
        compiler_params=pltpu.CompilerParams(vmem_limit_bytes=VMEM_LIMIT_BYTES))(me, pair, got)


_SMALL = ["c_ctx", "norm_g", "b_mod", "gm_v_g", "gm_w_s", "gm_b_s", "s5_lam_re", "s5_lam_im", "s5_log_dt",
          "s5_b_re", "s5_b_im", "s5_c_re", "s5_c_im", "s5_d", "s5_b_glu", "q_norm_g", "k_norm_g", "final_g"]
_BIG = ["we_in", "we_out", "s5_w_glu", "wo_in", "wo_out"]
_WEIGHTS = ["c_ctx", "norm_g", "w_mod", "b_mod", "we_in", "we_out", "gm_v_g", "gm_w_s", "gm_b_s", "s5_lam_re",
            "s5_lam_im", "s5_log_dt", "s5_b_re", "s5_b_im", "s5_c_re", "s5_c_im", "s5_d", "s5_w_glu", "s5_b_glu",
            "wo_in", "wo_out", "q_norm_g", "k_norm_g", "final_g"]
_SMALL_ALIGN = 8 * 8 * 128


def _rope_tables(n_lat):
    rows = n_lat // GRID_W
    row = jnp.repeat(jnp.arange(rows), GRID_W)
    col = jnp.tile(jnp.arange(GRID_W), rows)
    freqs = ROPE_THETA ** (-jnp.arange(HD // 4, dtype=F32) / (HD // 4))
    ar, ac = row[:, None] * freqs, col[:, None] * freqs
    cos = jnp.concatenate([jnp.cos(ar), jnp.cos(ar), jnp.cos(ac), jnp.cos(ac)], axis=1)
    sins = jnp.concatenate([-jnp.sin(ar), jnp.sin(ar), -jnp.sin(ac), jnp.sin(ac)], axis=1)
    cos = jnp.concatenate([jnp.ones((NC, HD), F32), cos], axis=0)
    sins = jnp.concatenate([jnp.zeros((NC, HD), F32), sins], axis=0)
    return cos, sins


def _block_diag(v, transpose):
    gpb = SG // NBLK
    v = v.reshape(2, NBLK, gpb, SH, SP)
    eye = jnp.eye(gpb, dtype=v.dtype)
    if transpose:
        return jnp.einsum("djahp,ab->djapbh", v, eye).reshape(2, NBLK, LN, UB)
    return jnp.einsum("djahp,ab->djahbp", v, eye).reshape(2, NBLK, UB, LN)


def _diag_blocks(m):
    gpb = SG // NBLK
    return jnp.einsum("jahap->jahp", m.reshape(NBLK, gpb, SH, gpb, SP)).reshape(SG, SH, SP)


def _view2d(a):
    if a.ndim == 1:
        return a.reshape(1, -1)
    if a.shape[-1] < 64 and a.size % 1024 == 0:
        return a.reshape(-1, 1024)
    return a.reshape(-1, a.shape[-1])


def kernel(x, c, ctx, c_ctx, norm_g, w_mod, b_mod, we_in, we_out, gm_v_g, gm_w_s, gm_b_s, s5_lam_re, s5_lam_im, s5_log_dt, s5_b_re, s5_b_im, s5_c_re, s5_c_im, s5_d, s5_w_glu, s5_b_glu, wo_in, wo_out, q_norm_g, k_norm_g, final_g, loss_target, m_c_ctx, m_norm_g, m_w_mod, m_b_mod, m_we_in, m_we_out, m_gm_v_g, m_gm_w_s, m_gm_b_s, m_s5_lam_re, m_s5_lam_im, m_s5_log_dt, m_s5_b_re, m_s5_b_im, m_s5_c_re, m_s5_c_im, m_s5_d, m_s5_w_glu, m_s5_b_glu, m_wo_in, m_wo_out, m_q_norm_g, m_k_norm_g, m_final_g, v_c_ctx, v_norm_g, v_w_mod, v_b_mod, v_we_in, v_we_out, v_gm_v_g, v_gm_w_s, v_gm_b_s, v_s5_lam_re, v_s5_lam_im, v_s5_log_dt, v_s5_b_re, v_s5_b_im, v_s5_c_re, v_s5_c_im, v_s5_d, v_s5_w_glu, v_s5_b_glu, v_wo_in, v_wo_out, v_q_norm_g, v_k_norm_g, v_final_g):
    weights = dict(c_ctx=c_ctx, norm_g=norm_g, w_mod=w_mod, b_mod=b_mod, we_in=we_in, we_out=we_out, gm_v_g=gm_v_g,
                   gm_w_s=gm_w_s, gm_b_s=gm_b_s, s5_lam_re=s5_lam_re, s5_lam_im=s5_lam_im, s5_log_dt=s5_log_dt,
                   s5_b_re=s5_b_re, s5_b_im=s5_b_im, s5_c_re=s5_c_re, s5_c_im=s5_c_im, s5_d=s5_d, s5_w_glu=s5_w_glu,
                   s5_b_glu=s5_b_glu, wo_in=wo_in, wo_out=wo_out, q_norm_g=q_norm_g, k_norm_g=k_norm_g,
                   final_g=final_g)
    mom_m = dict(c_ctx=m_c_ctx, norm_g=m_norm_g, w_mod=m_w_mod, b_mod=m_b_mod, we_in=m_we_in, we_out=m_we_out,
                 gm_v_g=m_gm_v_g, gm_w_s=m_gm_w_s, gm_b_s=m_gm_b_s, s5_lam_re=m_s5_lam_re, s5_lam_im=m_s5_lam_im,
                 s5_log_dt=m_s5_log_dt, s5_b_re=m_s5_b_re, s5_b_im=m_s5_b_im, s5_c_re=m_s5_c_re, s5_c_im=m_s5_c_im,
                 s5_d=m_s5_d, s5_w_glu=m_s5_w_glu, s5_b_glu=m_s5_b_glu, wo_in=m_wo_in, wo_out=m_wo_out,
                 q_norm_g=m_q_norm_g, k_norm_g=m_k_norm_g, final_g=m_final_g)
    mom_v = dict(c_ctx=v_c_ctx, norm_g=v_norm_g, w_mod=v_w_mod, b_mod=v_b_mod, we_in=v_we_in, we_out=v_we_out,
                 gm_v_g=v_gm_v_g, gm_w_s=v_gm_w_s, gm_b_s=v_gm_b_s, s5_lam_re=v_s5_lam_re, s5_lam_im=v_s5_lam_im,
                 s5_log_dt=v_s5_log_dt, s5_b_re=v_s5_b_re, s5_b_im=v_s5_b_im, s5_c_re=v_s5_c_re, s5_c_im=v_s5_c_im,
                 s5_d=v_s5_d, s5_w_glu=v_s5_w_glu, s5_b_glu=v_s5_b_glu, wo_in=v_wo_in, wo_out=v_wo_out,
                 q_norm_g=v_q_norm_g, k_norm_g=v_k_norm_g, final_g=v_final_g)

    ixy = 2 * lax.axis_index("x") + lax.axis_index("y")
    n_lat = x.shape[1]
    nl = norm_g.shape[0]
    nmod = w_mod.shape[2]
    xin = (ctx[0], x[0])

    ic = lax.axis_index("c")
    mine = [lax.dynamic_index_in_dim(weights[n], ic, 0, keepdims=False).astype(BF) for n in _BIG]
    got = exchange([(m_, "xy", "gather") for m_ in mine] + [(c, "xy", "gather")], "gather_weights")
    both = d2d([(g_.reshape(-1, g_.shape[-1]), "gather") for g_ in got[:len(_BIG)]], "swap_weights")
    both = [b_.reshape((2,) + g_.shape) for b_, g_ in zip(both, got)]
    wein = [(both[0], l) for l in range(2)]
    weout = [(both[1].reshape(2, 1, D, D), l) for l in range(2)]
    wglu = [(both[2].reshape(2, AW, AW), l) for l in range(2)]
    woin = [(both[3], l) for l in range(2)]
    woout = [(both[4].reshape(2, 1, D, D), l) for l in range(2)]
    c_group = got[len(_BIG)].reshape(4, D)

    cond = jnp.concatenate([c_group, jnp.broadcast_to(c_ctx.reshape(1, D), (4, D))], axis=0)
    b_shard = lax.dynamic_slice(b_mod, (0, ixy * nmod), (nl, nmod)).reshape(nl, 1, nmod)
    mpart = ada_fwd(cond, w_mod, b_shard)
    m_lat, m_ctx = exchange([(jnp.transpose(mpart[:, 0:4], (1, 0, 2)), "xy", "scatter"),
                             (mpart[:, 4], "xy", "gather")], "exchange_mod")
    m_lat = jnp.transpose(m_lat, (1, 0, 2)).reshape(nl, 3, D)
    m_ctx = jnp.transpose(m_ctx, (1, 0, 2)).reshape(nl, 3, D)
    mods = [jnp.stack([m_ctx[l], m_lat[l]], axis=0) for l in range(nl)]

    loss_part, dx, g, d_norm_g, d_mod_lat, d_mod_ctx, d_final_g = _local_step(
        xin, loss_target[0], mods, wein, weout, wglu, woin, woout, weights)
    grad_x = dx.reshape(1, n_lat, D)

    d_mod_lat, d_mod_ctx = jnp.stack(d_mod_lat), jnp.stack(d_mod_ctx)
    dm_send = jnp.stack([d_mod_lat.reshape(nl, 4, nmod), d_mod_ctx.reshape(nl, 4, nmod)])
    (dm_got,) = exchange([(jnp.transpose(dm_send, (2, 0, 1, 3)), "xy", "scatter")], "exchange_dmod")
    dm_rows = jnp.concatenate([dm_got[:, 0], dm_got[:, 1]], axis=0)
    gw_mod, d_cctx = ada_bwd(cond, jnp.transpose(dm_rows, (1, 0, 2)), w_mod)
    g_small = dict(c_ctx=d_cctx.reshape(D), norm_g=jnp.stack(d_norm_g), b_mod=add2(d_mod_lat, d_mod_ctx, "add_dbmod"),
                   final_g=d_final_g.reshape(D))
    for name in _SMALL:
        if name not in g_small:
            g_small[name] = jnp.stack(g[name])

    flat = jnp.concatenate([g_small[n].reshape(-1) for n in _SMALL])
    nflat = flat.shape[0]
    npad = -(-nflat // _SMALL_ALIGN) * _SMALL_ALIGN
    flat = jnp.concatenate([flat, jnp.zeros((npad - nflat,), F32)]).reshape(8, npad // (8 * 128), 128)
    pairs = [gw_mod.reshape(2, nl // 2 * D, nmod)]
    for name in _BIG:
        st = jnp.stack(g[name]) if isinstance(g[name], list) else g[name]
        pairs.append(st.reshape(2, -1, st.shape[-1]))
    got_a = d2d([(pairs[k], "swap") for k in (1, 2, 3)], "reduce_chip_a")
    got_b = d2d([(pairs[k], "swap") for k in (0, 4, 5)], "reduce_chip_b")
    theirs = [got_b[0]] + got_a + got_b[1:]
    chip = [sum_own(pairs[k], theirs[k], f"sum_chip{k}", F32 if k == 0 else BF) for k in range(len(pairs))]
    parts = exchange([(flat, "all", "scatter")]
                     + [(s_.reshape(4, s_.shape[0] // 4, s_.shape[1]), "xy", "scatter") for s_ in chip[1:]],
                     "reduce_scatter")
    sums = [sum_lead(pt, f"sum_shard{k}") for k, pt in enumerate(parts)]
    full = d2d([(sums[0], "gather"), (chip[0], "gather")] + [(s_, "gather") for s_ in sums[1:]], "all_gather")
    (flat_full,) = exchange([(full[0], "xy", "gather")], "gather_small")
    full = [flat_full] + full[1:]
    flat = full[0].reshape(-1)
    grads = {}
    off = 0
    for name in _SMALL:
        sz = weights[name].size
        grads[name] = flat[off:off + sz].reshape(weights[name].shape)
        off += sz
    grads["w_mod"] = full[1].reshape(w_mod.shape)
    for k, name in enumerate(_BIG):
        grads[name] = full[2 + k].reshape(weights[name].shape)

    delta, new_m, new_v = {}, {}, {}
    views = [_view2d(weights[n]) for n in _SMALL]
    ds, nms, nvs = adamw_many(views, [grads[n].reshape(w2.shape) for n, w2 in zip(_SMALL, views)],
                              [mom_m[n].reshape(w2.shape) for n, w2 in zip(_SMALL, views)],
                              [mom_v[n].reshape(w2.shape) for n, w2 in zip(_SMALL, views)], "adamw_small")
    for n, d2, m2, v2 in zip(_SMALL, ds, nms, nvs):
        shp = weights[n].shape
        delta[n], new_m[n], new_v[n] = d2.reshape(shp), m2.reshape(shp), v2.reshape(shp)
    for name in ["w_mod"] + _BIG:
        w2 = _view2d(weights[name])
        d2, m2, v2 = adamw(w2, grads[name].reshape(w2.shape), mom_m[name].reshape(w2.shape),
                           mom_v[name].reshape(w2.shape), f"adamw_{name}")
        shp = weights[name].shape
        delta[name], new_m[name], new_v[name] = d2.reshape(shp), m2.reshape(shp), v2.reshape(shp)

    loss = lax.psum(loss_part[0, 0], ("x", "y", "c"))
    return (loss, grad_x, *[grads[n] for n in _WEIGHTS], *[delta[n] for n in _WEIGHTS],
            *[new_m[n] for n in _WEIGHTS], *[new_v[n] for n in _WEIGHTS])


def _local_step(xin, target, mods, wein, weout, wglu, woin, woout, w):
    norm_g, gm_v_g, gm_w_s, gm_b_s = w["norm_g"], w["gm_v_g"], w["gm_w_s"], w["gm_b_s"]
    s5_lam_re, s5_lam_im, s5_log_dt = w["s5_lam_re"], w["s5_lam_im"], w["s5_log_dt"]
    s5_b_re, s5_b_im, s5_c_re, s5_c_im = w["s5_b_re"], w["s5_b_im"], w["s5_c_re"], w["s5_c_im"]
    s5_d, s5_b_glu, q_norm_g, k_norm_g, final_g = w["s5_d"], w["s5_b_glu"], w["q_norm_g"], w["k_norm_g"], w["final_g"]
    nl = norm_g.shape[0]
    n_lat = xin[1].shape[0]

    cos, sins = _rope_tables(n_lat)

    s5p = []
    for i in range(2):
        lam_l = (s5_lam_re[i].reshape(2, SW), s5_lam_im[i].reshape(2, SW),
                 jnp.repeat(s5_log_dt[i], SP, axis=1))
        lam_r = (jnp.repeat(s5_lam_re[i].reshape(2 * SG, SP), SH, axis=0),
                 jnp.repeat(s5_lam_im[i].reshape(2 * SG, SP), SH, axis=0),
                 jnp.repeat(s5_log_dt[i].reshape(2 * SG, 1), SH, axis=0))
        b_r = (jnp.transpose(s5_b_re[i], (0, 1, 3, 2)).reshape(2 * SG * SH, SP),
               jnp.transpose(s5_b_im[i], (0, 1, 3, 2)).reshape(2 * SG * SH, SP))
        pw_re, pw_im, bbr, bbi = s5_disc(*lam_l, *lam_r, *b_r)
        s5p.append(dict(
            lam_r=lam_r, b_r=b_r, pw_re=pw_re, pw_im=pw_im,
            bb_re=_block_diag(bbr.reshape(2, SG, SH, SP), False).astype(BF),
            bb_im=_block_diag(bbi.reshape(2, SG, SH, SP), False).astype(BF),
            ct_re=_block_diag(s5_c_re[i], True).astype(BF), ct_im=_block_diag(s5_c_im[i], True).astype(BF)))

    saved = []
    xcur = xin
    h = pro_fwd(xin[0], xin[1], norm_g[0].reshape(1, D), mods[0], "pro_fwd0")
    for l in range(nl):
        i = l // 2
        sv = dict(x=xcur, h=h)
        if l % 2 == 0:
            p = mm_nn(h, wein[i], f"in_proj{l}")
            sp = s5p[i]
            for dr, rev in ((0, False), (1, True)):
                sv[f"y{dr}"], sv[f"hpr{dr}"], sv[f"hpi{dr}"] = s5q_fwd(
                    p, sp["bb_re"][dr], sp["bb_im"][dr], sp["ct_re"][dr], sp["ct_im"][dr],
                    sp["pw_re"][dr], sp["pw_im"][dr], rev, f"s5_fwd{l}_{dr}")
            mix = mix_fwd(p, sv["y0"], sv["y1"], gm_v_g[i].reshape(1, AW), gm_w_s[i].astype(BF),
                          gm_b_s[i].reshape(NGRP, CHUNK, 1), s5_d[i].reshape(1, AW), wglu[i],
                          s5_b_glu[i].reshape(1, AW), f"mix_fwd{l}")
            o = mm_nn(mix, weout[i], f"out_proj{l}")
        else:
            p = mm_nn(h, woin[i], f"in_proj{l}")
            sv["qkv"] = attn_prep(p, q_norm_g[i].reshape(1, HD), k_norm_g[i].reshape(1, HD), cos, sins, f"attn_prep{l}")
            sv["o_att"], mix, sv["lse"] = attn_fwd(sv["qkv"], p, f"attn_fwd{l}")
            o = mm_nn(mix, woout[i], f"out_proj{l}")
        sv.update(p=p, mix=mix, o=o)
        saved.append(sv)
        if l < nl - 1:
            xcur, h = res_pro_fwd(xcur, o, mods[l], True, norm_g[l + 1].reshape(1, D), mods[l + 1], f"res_pro_fwd{l}")
        else:
            xcur = res_fwd(xcur, o, mods[l], False, f"res_fwd{l}")

    loss_part, dx, d_final_g, do, dgt = final_loss(xcur, target, final_g.reshape(1, D), saved[-1]["o"], mods[-1])

    g = {}
    gbuf = {}
    d_norm_g, d_mod_lat, d_mod_ctx = [None] * nl, [None] * nl, [None] * nl
    for name in ("s5_w_glu", "gm_v_g", "gm_w_s", "gm_b_s", "s5_lam_re", "s5_lam_im",
                 "s5_log_dt", "s5_b_re", "s5_b_im", "s5_c_re", "s5_c_im", "s5_d", "s5_b_glu", "q_norm_g", "k_norm_g"):
        g[name] = [None, None]
    for l in reversed(range(nl)):
        i = l // 2
        sv = saved[l]
        w_out = weout[i] if l % 2 == 0 else woout[i]
        dmix = mm_nt(do, w_out, f"out_dgrad{l}")
        out_name, in_name = ("we_out", "we_in") if l % 2 == 0 else ("wo_out", "wo_in")
        gbuf[out_name] = mm_tn(sv["mix"], do, 1, f"out_wgrad{l}", slot=i, into=gbuf.get(out_name))
        if l % 2 == 0:
            sp = s5p[i]
            (dp, dy, g["gm_w_s"][i], dbs, dvg, dd, g["s5_w_glu"][i], dbg) = mix_bwd(
                sv["p"], sv["y0"], sv["y1"], dmix, gm_v_g[i].reshape(1, AW), gm_w_s[i].astype(BF),
                gm_b_s[i].reshape(NGRP, CHUNK, 1), s5_d[i].reshape(1, AW), wglu[i], s5_b_glu[i].reshape(1, AW),
                f"mix_bwd{l}")
            g["gm_b_s"][i], g["gm_v_g"][i] = dbs.reshape(NGRP, CHUNK), dvg.reshape(AW)
            g["s5_d"][i], g["s5_b_glu"][i] = dd.reshape(AW), dbg.reshape(AW)
            g["s5_w_glu"][i] = g["s5_w_glu"][i].reshape(4, AW // 4, AW)
            dxd, das_r, das_i, dbbs_r, dbbs_i, dcs_r, dcs_i = [], [], [], [], [], [], []
            for dr, rev in ((0, False), (1, True)):
                dxs_d, da_r, da_i, dbb_r, dbb_i, dc_r, dc_i = s5q_bwd(
                    sv["p"], sv[f"hpr{dr}"], sv[f"hpi{dr}"], dy, sp["bb_re"][dr], sp["bb_im"][dr],
                    sp["ct_re"][dr], sp["ct_im"][dr], sp["pw_re"][dr], -sp["pw_im"][dr], rev, f"s5_bwd{l}_{dr}")
                dxd.append(dxs_d)
                das_r.append(jnp.repeat(da_r.reshape(SG, SP), SH, axis=0))
                das_i.append(jnp.repeat(da_i.reshape(SG, SP), SH, axis=0))
                dbbs_r.append(_diag_blocks(dbb_r).reshape(SG * SH, SP))
                dbbs_i.append(_diag_blocks(dbb_i).reshape(SG * SH, SP))
                dcs_r.append(_diag_blocks(dc_r))
                dcs_i.append(_diag_blocks(dc_i))
            cat = lambda parts: jnp.concatenate(parts, axis=0)
            dlr, dli, dldt, dbr, dbi = s5_param_bwd(*sp["lam_r"], *sp["b_r"], cat(das_r), cat(das_i),
                                                    cat(dbbs_r), cat(dbbs_i))
            g["s5_lam_re"][i], g["s5_lam_im"][i] = dlr.reshape(2, SG, SP), dli.reshape(2, SG, SP)
            g["s5_log_dt"][i] = dldt.reshape(2, SG)
            g["s5_b_re"][i] = jnp.transpose(dbr.reshape(2, SG, SH, SP), (0, 1, 3, 2))
            g["s5_b_im"][i] = jnp.transpose(dbi.reshape(2, SG, SH, SP), (0, 1, 3, 2))
            g["s5_c_re"][i], g["s5_c_im"][i] = jnp.stack(dcs_r), jnp.stack(dcs_i)
            dp = s5p_dx_sum(dy, s5_d[i].reshape(1, AW), dxd[0], dxd[1], dp, f"s5_dx_sum{l}")
            w_in = wein[i]
        else:
            dq, dp, dk, dv = attn_bwd(sv["qkv"], sv["p"], dmix, sv["o_att"], sv["lse"], f"attn_bwd{l}")
            dp, dqg, dkg = attn_prep_bwd(sv["p"], dq, dk, dv, q_norm_g[i].reshape(1, HD),
                                         k_norm_g[i].reshape(1, HD), cos, sins, dp, f"attn_prep_bwd{l}")
            g["q_norm_g"][i], g["k_norm_g"][i] = dqg.reshape(HD), dkg.reshape(HD)
            w_in = woin[i]
        dh = mm_nt(dp, w_in, f"in_dgrad{l}")
        gbuf[in_name] = mm_tn(sv["h"], dp, 4, f"in_wgrad{l}", slot=i, into=gbuf.get(in_name))
        dgt_l = dgt
        if l > 0:
            dx, dmod2, dng, do, dgt = pro_res_bwd(sv["x"], dh, dx, norm_g[l].reshape(1, D), mods[l],
                                                  saved[l - 1]["o"], mods[l - 1], f"pro_res_bwd{l}")
        else:
            dx, dmod2, dng = pro_bwd(xin[0], xin[1], dh, dx, norm_g[l].reshape(1, D), mods[l], f"pro_bwd{l}")
        d_norm_g[l] = dng.reshape(D)
        d_mod_ctx[l] = jnp.concatenate([dmod2[0, 0], dmod2[0, 1], dgt_l[0]])
        d_mod_lat[l] = jnp.concatenate([dmod2[1, 0], dmod2[1, 1], dgt_l[1]])
    g.update(gbuf)
    return loss_part, dx, g, d_norm_g, d_mod_lat, d_mod_ctx, d_final_g
```

```python
import functools
import math

import numpy as np
import jax
import jax.numpy as jnp
from jax import lax
from jax.experimental import pallas as pl
from jax.experimental.pallas import tpu as pltpu

F32 = jnp.float32
BF = jnp.bfloat16
MESH = pl.DeviceIdType.MESH

D = 1024
NC = 256
SEQ = 4096
GRID_W = 64
TM = 256
CHUNK = 128
EPS = 1e-6
HD = 128
NQ = 8
NKV = 2
ROPE_THETA = 10000.0
SG = 32
SP = 64
SH = 16
SW = SG * SP
GELU_K = math.sqrt(2.0 / math.pi)
GELU_C = 0.044715
VMEM_LIMIT_BYTES = 56 * 1024 * 1024

ADAM_LR = 0.001
ADAM_B1 = 0.9
ADAM_B2 = 0.999
ADAM_EPS = 1e-08
ADAM_WD = 0.01
ADAM_STEP = 10


def _call(body, *, name, out_shape, grid=None, in_specs=None, out_specs=None, scratch=()):
    kw = {}
    if grid is not None:
        kw["grid"] = grid
    if in_specs is not None:
        kw["in_specs"] = in_specs
    if out_specs is not None:
        kw["out_specs"] = out_specs
    return pl.pallas_call(
        body, name=name, out_shape=out_shape, scratch_shapes=list(scratch),
        compiler_params=pltpu.CompilerParams(vmem_limit_bytes=VMEM_LIMIT_BYTES), **kw)


def _dot(a, b, ca=1, cb=0):
    return lax.dot_general(a, b, (((ca,), (cb,)), ((), ())), preferred_element_type=F32)


def _sig(x):
    return 1.0 / (1.0 + jnp.exp(-x))


def _full(shape):
    n = len(shape)
    return pl.BlockSpec(shape, lambda *_: (0,) * n)


def _mm_rows(t):
    for rows in (1088, 1024, 768, 512, 256):
        if t % rows == 0:
            return rows
    raise ValueError(t)


def _layer_of(w):
    return w if isinstance(w, tuple) else (w[None], 0)


def mm_nn(a, w, name, out_dtype=F32):
    w4, layer = _layer_of(w)
    t, k = a.shape
    _, j, _, nb = w4.shape
    tr = _mm_rows(t)

    def body(a_ref, w_ref, o_ref):
        o_ref[...] = _dot(a_ref[...], w_ref[0, 0]).astype(o_ref.dtype)

    return _call(body, name=name, grid=(j, t // tr),
                 in_specs=[pl.BlockSpec((tr, k), lambda jj, i: (i, 0)),
                           pl.BlockSpec((1, 1, k, nb), lambda jj, i: (layer, jj, 0, 0))],
                 out_specs=pl.BlockSpec((tr, nb), lambda jj, i: (i, jj)),
                 out_shape=jax.ShapeDtypeStruct((t, j * nb), out_dtype))(a, w4)


def mm_nt(a, w, name, out_dtype=F32):
    w4, layer = _layer_of(w)
    t, _ = a.shape
    _, j, k, nb = w4.shape
    tr = _mm_rows(t)

    def body(a_ref, w_ref, o_ref):
        acc = _dot(a_ref[:, 0:nb], w_ref[0, 0], 1, 1)
        for jj in range(1, j):
            acc = acc + _dot(a_ref[:, jj * nb:(jj + 1) * nb], w_ref[0, jj], 1, 1)
        o_ref[...] = acc.astype(o_ref.dtype)

    return _call(body, name=name, grid=(t // tr,),
                 in_specs=[pl.BlockSpec((tr, j * nb), lambda i: (i, 0)),
                           pl.BlockSpec((1, j, k, nb), lambda i: (layer, 0, 0, 0))],
                 out_specs=pl.BlockSpec((tr, k), lambda i: (i, 0)),
                 out_shape=jax.ShapeDtypeStruct((t, k), out_dtype))(a, w4)


def mm_tn(a, b, j, name, slot=0, into=None):
    t, m = a.shape
    nb = b.shape[1] // j
    tr = _mm_rows(t)

    def body(a_ref, b_ref, *rest):
        o_ref = rest[-1]

        @pl.when(pl.program_id(1) == 0)
        def _():
            o_ref[...] = jnp.zeros_like(o_ref)
        o_ref[0, 0] += _dot(a_ref[...], b_ref[...], 0, 0)

    in_specs = [pl.BlockSpec((tr, m), lambda jj, i: (i, 0)), pl.BlockSpec((tr, nb), lambda jj, i: (i, jj))]
    args = [a, b]
    alias = {}
    if into is not None:
        in_specs.append(pl.BlockSpec(memory_space=pl.ANY))
        args.append(into)
        alias = {2: 0}
    return pl.pallas_call(
        body, name=name, grid=(j, t // tr), in_specs=in_specs,
        out_specs=pl.BlockSpec((1, 1, m, nb), lambda jj, i: (slot, jj, 0, 0)),
        out_shape=jax.ShapeDtypeStruct((2, j, m, nb), F32), input_output_aliases=alias,
        compiler_params=pltpu.CompilerParams(vmem_limit_bytes=VMEM_LIMIT_BYTES))(*args)


def _mod_rows(mod_ref, i):
    ctx = i == 0
    sh = jnp.where(ctx, mod_ref[0, 0:1, :], mod_ref[1, 0:1, :])
    sc = jnp.where(ctx, mod_ref[0, 1:2, :], mod_ref[1, 1:2, :])
    gt = jnp.where(ctx, mod_ref[0, 2:3, :], mod_ref[1, 2:3, :])
    return sh, sc, gt


def _split_specs():
    return [pl.BlockSpec((TM, D), lambda i: (0, 0)), pl.BlockSpec((TM, D), lambda i: (jnp.maximum(i - 1, 0), 0))]


def _split_tile(c_ref, l_ref, i):
    return jnp.where(i == 0, c_ref[...], l_ref[...])


def pro_fwd(ctx, lat, g, mod, name):
    t = ctx.shape[0] + lat.shape[0]

    def body(c_ref, l_ref, g_ref, mod_ref, h_ref):
        i = pl.program_id(0)
        sh, sc, _ = _mod_rows(mod_ref, i)
        xv = _split_tile(c_ref, l_ref, i)
        r = lax.rsqrt(jnp.mean(xv * xv, axis=-1, keepdims=True) + EPS)
        h_ref[...] = ((xv * r) * g_ref[...] * (1.0 + sc) + sh).astype(BF)

    return _call(body, name=name, grid=(t // TM,),
                 in_specs=_split_specs() + [_full((1, D)), _full((2, 3, D))],
                 out_specs=pl.BlockSpec((TM, D), lambda i: (i, 0)),
                 out_shape=jax.ShapeDtypeStruct((t, D), BF))(ctx, lat, g, mod)


def pro_bwd(ctx, lat, dh, dxn, g, mod, name):
    t = ctx.shape[0] + lat.shape[0]

    def body(c_ref, l_ref, dh_ref, dxn_ref, g_ref, mod_ref, dx_ref, dmod_ref, dg_ref):
        i = pl.program_id(0)

        @pl.when(i == 0)
        def _():
            dmod_ref[...] = jnp.zeros_like(dmod_ref)
            dg_ref[...] = jnp.zeros_like(dg_ref)

        _, sc, _ = _mod_rows(mod_ref, i)
        xv = _split_tile(c_ref, l_ref, i)
        gv = g_ref[...]
        r = lax.rsqrt(jnp.mean(xv * xv, axis=-1, keepdims=True) + EPS)
        xn = xv * r
        dh_v = dh_ref[...]
        e = dh_v * (1.0 + sc)
        dsh = jnp.sum(dh_v, axis=0, keepdims=True)
        dsc = jnp.sum(dh_v * xn * gv, axis=0, keepdims=True)
        dg_ref[...] += jnp.sum(e * xn, axis=0, keepdims=True)
        dxh = e * gv

        @pl.when(i == 0)
        def _():
            dmod_ref[0, 0:1, :] += dsh
            dmod_ref[0, 1:2, :] += dsc

        @pl.when(i > 0)
        def _():
            dx_ref[...] = dxn_ref[...] + r * (dxh - xn * jnp.mean(dxh * xn, axis=-1, keepdims=True))
            dmod_ref[1, 0:1, :] += dsh
            dmod_ref[1, 1:2, :] += dsc

    tile = pl.BlockSpec((TM, D), lambda i: (i, 0))
    return _call(body, name=name, grid=(t // TM,),
                 in_specs=_split_specs() + [tile, tile, _full((1, D)), _full((2, 3, D))],
                 out_specs=[_split_specs()[1], _full((2, 2, D)), _full((1, D))],
                 out_shape=[jax.ShapeDtypeStruct(lat.shape, F32), jax.ShapeDtypeStruct((2, 2, D), F32),
                            jax.ShapeDtypeStruct((1, D), F32)])(ctx, lat, dh, dxn, g, mod)


def res_fwd(x, o, mod, update_ctx, name):
    t = x.shape[0]

    def body(x_ref, o_ref, mod_ref, y_ref):
        i = pl.program_id(0)
        _, _, gt = _mod_rows(mod_ref, i)
        upd = x_ref[...] + gt * o_ref[...]
        if update_ctx:
            y_ref[...] = upd
        else:
            y_ref[...] = jnp.where(i == 0, x_ref[...], upd)

    tile = pl.BlockSpec((TM, D), lambda i: (i, 0))
    return _call(body, name=name, grid=(t // TM,), in_specs=[tile, tile, _full((2, 3, D))],
                 out_specs=tile, out_shape=jax.ShapeDtypeStruct((t, D), F32))(x, o, mod)


def res_bwd(dxn, o, mod, update_ctx, name):
    t = dxn.shape[0]

    def body(dxn_ref, o_ref, mod_ref, do_ref, dgt_ref):
        i = pl.program_id(0)

        @pl.when(i == 0)
        def _():
            dgt_ref[...] = jnp.zeros_like(dgt_ref)

        _, _, gt = _mod_rows(mod_ref, i)
        dv = dxn_ref[...]
        do = gt * dv
        dgt = jnp.sum(dv * o_ref[...], axis=0, keepdims=True)
        if update_ctx:
            do_ref[...] = do.astype(BF)
        else:
            do_ref[...] = jnp.where(i == 0, jnp.zeros_like(do), do).astype(BF)

        if update_ctx:
            @pl.when(i == 0)
            def _():
                dgt_ref[0:1, :] += dgt

        @pl.when(i > 0)
        def _():
            dgt_ref[1:2, :] += dgt

    tile = pl.BlockSpec((TM, D), lambda i: (i, 0))
    return _call(body, name=name, grid=(t // TM,), in_specs=[tile, tile, _full((2, 3, D))],
                 out_specs=[tile, _full((2, D))],
                 out_shape=[jax.ShapeDtypeStruct((t, D), BF), jax.ShapeDtypeStruct((2, D), F32)])(dxn, o, mod)


def res_pro_fwd(x, o, mod, update_ctx, g_next, mod_next, name):
    split = isinstance(x, tuple)
    xs = list(x) if split else [x]
    t = o.shape[0]

    def body(*refs):
        x_refs = refs[:len(xs)]
        o_ref, mod_ref, g_ref, modn_ref, y_ref, h_ref = refs[len(xs):]
        i = pl.program_id(0)
        _, _, gt = _mod_rows(mod_ref, i)
        xv = _split_tile(x_refs[0], x_refs[1], i) if split else x_refs[0][...]
        xn = xv + gt * o_ref[...]
        if not update_ctx:
            xn = jnp.where(i == 0, xv, xn)
        y_ref[...] = xn
        sh, sc, _ = _mod_rows(modn_ref, i)
        r = lax.rsqrt(jnp.mean(xn * xn, axis=-1, keepdims=True) + EPS)
        h_ref[...] = ((xn * r) * g_ref[...] * (1.0 + sc) + sh).astype(BF)

    tile = pl.BlockSpec((TM, D), lambda i: (i, 0))
    return _call(body, name=name, grid=(t // TM,),
                 in_specs=(_split_specs() if split else [tile]) + [tile, _full((2, 3, D)), _full((1, D)), _full((2, 3, D))],
                 out_specs=[tile, tile],
                 out_shape=[jax.ShapeDtypeStruct((t, D), F32), jax.ShapeDtypeStruct((t, D), BF)])(
        *xs, o, mod, g_next, mod_next)


def _res_bwd_part(dx, o_ref, modp_ref, do_ref, dgt_ref, i, update_ctx):
    _, _, gtp = _mod_rows(modp_ref, i)
    do = gtp * dx
    dgt = jnp.sum(dx * o_ref[...], axis=0, keepdims=True)
    if update_ctx:
        do_ref[...] = do.astype(BF)

        @pl.when(i == 0)
        def _():
            dgt_ref[0:1, :] += dgt
    else:
        do_ref[...] = jnp.where(i == 0, jnp.zeros_like(do), do).astype(BF)

    @pl.when(i > 0)
    def _():
        dgt_ref[1:2, :] += dgt


def pro_res_bwd(x, dh, dxn, g, mod, o_prev, mod_prev, name):
    t = x.shape[0]

    def body(x_ref, dh_ref, dxn_ref, g_ref, mod_ref, o_ref, modp_ref, dx_ref, dmod_ref, dg_ref, do_ref, dgt_ref):
        i = pl.program_id(0)

        @pl.when(i == 0)
        def _():
            dmod_ref[...] = jnp.zeros_like(dmod_ref)
            dg_ref[...] = jnp.zeros_like(dg_ref)
            dgt_ref[...] = jnp.zeros_like(dgt_ref)

        _, sc, _ = _mod_rows(mod_ref, i)
        xv = x_ref[...]
        gv = g_ref[...]
        r = lax.rsqrt(jnp.mean(xv * xv, axis=-1, keepdims=True) + EPS)
        xn = xv * r
        dh_v = dh_ref[...]
        e = dh_v * (1.0 + sc)
        dsh = jnp.sum(dh_v, axis=0, keepdims=True)
        dsc = jnp.sum(dh_v * xn * gv, axis=0, keepdims=True)
        dg_ref[...] += jnp.sum(e * xn, axis=0, keepdims=True)
        dxh = e * gv
        dx = dxn_ref[...] + r * (dxh - xn * jnp.mean(dxh * xn, axis=-1, keepdims=True))
        dx_ref[...] = dx

        @pl.when(i == 0)
        def _():
            dmod_ref[0, 0:1, :] += dsh
            dmod_ref[0, 1:2, :] += dsc

        @pl.when(i > 0)
        def _():
            dmod_ref[1, 0:1, :] += dsh
            dmod_ref[1, 1:2, :] += dsc

        _res_bwd_part(dx, o_ref, modp_ref, do_ref, dgt_ref, i, True)

    tile = pl.BlockSpec((TM, D), lambda i: (i, 0))
    return _call(body, name=name, grid=(t // TM,),
                 in_specs=[tile, tile, tile, _full((1, D)), _full((2, 3, D)), tile, _full((2, 3, D))],
                 out_specs=[tile, _full((2, 2, D)), _full((1, D)), tile, _full((2, D))],
                 out_shape=[jax.ShapeDtypeStruct((t, D), F32), jax.ShapeDtypeStruct((2, 2, D), F32),
                            jax.ShapeDtypeStruct((1, D), F32), jax.ShapeDtypeStruct((t, D), BF),
                            jax.ShapeDtypeStruct((2, D), F32)])(x, dh, dxn, g, mod, o_prev, mod_prev)


def final_loss(x, target, g, o_last, mod_last):
    t = x.shape[0]

    def body(x_ref, t_ref, g_ref, o_ref, modp_ref, loss_ref, dx_ref, dg_ref, do_ref, dgt_ref):
        i = pl.program_id(0)

        @pl.when(i == 0)
        def _():
            loss_ref[...] = jnp.zeros_like(loss_ref)
            dg_ref[...] = jnp.zeros_like(dg_ref)
            dx_ref[...] = jnp.zeros_like(dx_ref)
            do_ref[...] = jnp.zeros_like(do_ref)
            dgt_ref[...] = jnp.zeros_like(dgt_ref)

        @pl.when(i > 0)
        def _():
            xv = x_ref[...]
            gv = g_ref[...]
            r = lax.rsqrt(jnp.mean(xv * xv, axis=-1, keepdims=True) + EPS)
            xn = xv * r
            err = xn * gv - t_ref[...]
            loss_ref[...] += (0.5 / D) * jnp.sum(jnp.sum(err * err, axis=1, keepdims=True), axis=0, keepdims=True)
            dy = err * (1.0 / D)
            dg_ref[...] += jnp.sum(dy * xn, axis=0, keepdims=True)
            dxh = dy * gv
            dx = r * (dxh - xn * jnp.mean(dxh * xn, axis=-1, keepdims=True))
            dx_ref[...] = dx
            do_ref[...] = (modp_ref[1, 2:3, :] * dx).astype(BF)
            dgt_ref[1:2, :] += jnp.sum(dx * o_ref[...], axis=0, keepdims=True)

    tile = pl.BlockSpec((TM, D), lambda i: (i, 0))
    return _call(body, name="final_loss", grid=(t // TM,),
                 in_specs=[tile, pl.BlockSpec((TM, D), lambda i: (jnp.maximum(i - 1, 0), 0)), _full((1, D)), tile,
                           _full((2, 3, D))],
                 out_specs=[_full((1, 1)), tile, _full((1, D)), tile, _full((2, D))],
                 out_shape=[jax.ShapeDtypeStruct((1, 1), F32), jax.ShapeDtypeStruct((t, D), F32),
                            jax.ShapeDtypeStruct((1, D), F32), jax.ShapeDtypeStruct((t, D), BF),
                            jax.ShapeDtypeStruct((2, D), F32)])(x, target, g, o_last, mod_last)


def ada_fwd(cond, w_mod, b_mod):
    nl, _, nw = w_mod.shape

    def body(c_ref, w_ref, b_ref, o_ref):
        cv = c_ref[...]
        s = (cv * _sig(cv)).astype(BF)
        o_ref[0] = _dot(s, w_ref[0].astype(BF)) + b_ref[0]

    return _call(body, name="ada_fwd", grid=(nl,),
                 in_specs=[_full((8, D)), pl.BlockSpec((1, D, nw), lambda l: (l, 0, 0)),
                           pl.BlockSpec((1, 1, nw), lambda l: (l, 0, 0))],
                 out_specs=pl.BlockSpec((1, 8, nw), lambda l: (l, 0, 0)),
                 out_shape=jax.ShapeDtypeStruct((nl, 8, nw), F32))(cond, w_mod, b_mod)


def ada_bwd(cond, dm, w_mod):
    nl, _, nw = w_mod.shape

    def body(c_ref, dm_ref, w_ref, gw_ref, dcc_ref, dc_ref):
        l = pl.program_id(0)

        @pl.when(l == 0)
        def _():
            dc_ref[...] = jnp.zeros_like(dc_ref)

        cv = c_ref[...]
        sg = _sig(cv)
        s = (cv * sg).astype(BF)
        dmv = dm_ref[0].astype(BF)
        gw_ref[0] = _dot(s, dmv, 0, 0)
        dc_ref[...] += _dot(dmv, w_ref[0].astype(BF), 1, 1)

        @pl.when(l == nl - 1)
        def _():
            dcond = dc_ref[...] * (sg * (1.0 + cv * (1.0 - sg)))
            dcc_ref[...] = jnp.sum(dcond[4:8], axis=0, keepdims=True)

    return _call(body, name="ada_bwd", grid=(nl,),
                 in_specs=[_full((8, D)), pl.BlockSpec((1, 8, nw), lambda l: (l, 0, 0)),
                           pl.BlockSpec((1, D, nw), lambda l: (l, 0, 0))],
                 out_specs=[pl.BlockSpec((1, D, nw), lambda l: (l, 0, 0)), _full((1, D))],
                 out_shape=[jax.ShapeDtypeStruct((nl, D, nw), F32), jax.ShapeDtypeStruct((1, D), F32)],
                 scratch=[pltpu.VMEM((8, D), F32)])(cond, dm, w_mod)


def add2(a, b, name):
    def body(a_ref, b_ref, o_ref):
        o_ref[...] = a_ref[...] + b_ref[...]

    return _call(body, name=name, out_shape=jax.ShapeDtypeStruct(a.shape, a.dtype))(a, b)


AW = 512
NGRP = 4


def Y4_SPEC():
    return pl.BlockSpec((AW // 128, TM, 128), lambda i: (0, i, 0))


def _cat_lanes(ref):
    return jnp.concatenate([ref[q] for q in range(ref.shape[0])], axis=1)


def _gelu(y):
    t = jnp.tanh(GELU_K * (y + GELU_C * y * y * y))
    return 0.5 * y * (1.0 + t), t


def _layer_norm_stats(v):
    mu = jnp.mean(v, axis=-1, keepdims=True)
    vc = v - mu
    rstd = lax.rsqrt(jnp.mean(vc * vc, axis=-1, keepdims=True) + EPS)
    return vc * rstd, rstd


def _spatial_mix(vn_ref, ws_ref, bs_ref, mixed_ref):
    for ch in range(TM // CHUNK):
        rows = slice(ch * CHUNK, (ch + 1) * CHUNK)
        for g in range(NGRP):
            cols = slice(g * CHUNK, (g + 1) * CHUNK)
            mixed_ref[rows, cols] = _dot(ws_ref[g], vn_ref[rows, cols]) + bs_ref[g]


def mix_fwd(p, yf, yb, vg, ws, bs, dsk, wglu, bglu, name):
    t = p.shape[0]
    wglu, glu_layer = _layer_of(wglu)

    def body(p_ref, yf_ref, yb_ref, vg_ref, ws_ref, bs_ref, d_ref, wg_ref, bg_ref, o_ref, vn_ref, mixed_ref):
        vhat, _ = _layer_norm_stats(p_ref[:, AW:2 * AW])
        vn_ref[...] = (vhat * vg_ref[...]).astype(BF)
        _spatial_mix(vn_ref, ws_ref, bs_ref, mixed_ref)
        ga = p_ref[:, 2 * AW:3 * AW]
        o_ref[:, 0:AW] = (p_ref[:, 0:AW] * mixed_ref[...] * (ga * _sig(ga))).astype(BF)
        y = _cat_lanes(yf_ref) + _cat_lanes(yb_ref) + d_ref[...] * p_ref[:, 3 * AW:4 * AW]
        y2, _ = _gelu(y)
        z = _dot(y2.astype(BF), wg_ref[0]) + bg_ref[...]
        gb = p_ref[:, 4 * AW:5 * AW]
        o_ref[:, AW:2 * AW] = (y2 * _sig(z) * (gb * _sig(gb))).astype(BF)

    tile = lambda w: pl.BlockSpec((TM, w), lambda i: (i, 0))
    return _call(body, name=name, grid=(t // TM,),
                 in_specs=[tile(5 * AW), Y4_SPEC(), Y4_SPEC(), _full((1, AW)), _full((NGRP, CHUNK, CHUNK)),
                           _full((NGRP, CHUNK, 1)), _full((1, AW)),
                           pl.BlockSpec((1, AW, AW), lambda i: (glu_layer, 0, 0)), _full((1, AW))],
                 out_specs=tile(2 * AW), out_shape=jax.ShapeDtypeStruct((t, 2 * AW), BF),
                 scratch=[pltpu.VMEM((TM, AW), BF), pltpu.VMEM((TM, AW), F32)])(p, yf, yb, vg, ws, bs, dsk, wglu, bglu)


def mix_bwd(p, yf, yb, dmix, vg, ws, bs, dsk, wglu, bglu, name):
    t = p.shape[0]
    wglu, glu_layer = _layer_of(wglu)

    def body(p_ref, yf_ref, yb_ref, dm_ref, vg_ref, ws_ref, bs_ref, d_ref, wg_ref, bg_ref,
             dpa_ref, dy_ref, dws_ref, dbs_ref, dvg_ref, dd_ref, dwg_ref, dbg_ref,
             vn_ref, mixed_ref, dmx_ref, dvn_ref):
        @pl.when(pl.program_id(0) == 0)
        def _():
            for r in (dws_ref, dbs_ref, dvg_ref, dd_ref, dwg_ref, dbg_ref):
                r[...] = jnp.zeros_like(r)

        vhat, rstd = _layer_norm_stats(p_ref[:, AW:2 * AW])
        vgv = vg_ref[...]
        vn_ref[...] = (vhat * vgv).astype(BF)
        _spatial_mix(vn_ref, ws_ref, bs_ref, mixed_ref)
        u = p_ref[:, 0:AW]
        ga = p_ref[:, 2 * AW:3 * AW]
        sga = _sig(ga)
        dya = dm_ref[:, 0:AW]
        mixed = mixed_ref[...]
        dpa_ref[:, 0:AW] = (dya * mixed * (ga * sga)).astype(BF)
        dpa_ref[:, 2 * AW:3 * AW] = (dya * u * mixed * (sga * (1.0 + ga * (1.0 - sga)))).astype(BF)
        dmx_ref[...] = dya * u * (ga * sga)
        for ch in range(TM // CHUNK):
            rows = slice(ch * CHUNK, (ch + 1) * CHUNK)
            for g in range(NGRP):
                cols = slice(g * CHUNK, (g + 1) * CHUNK)
                dmx = dmx_ref[rows, cols]
                dmxb = dmx.astype(BF)
                dws_ref[g] += _dot(dmxb, vn_ref[rows, cols], 1, 1)
                dbs_ref[g] += jnp.sum(dmx, axis=1, keepdims=True)
                dvn_ref[rows, cols] = _dot(ws_ref[g], dmxb, 0, 0)
        dvn = dvn_ref[...]
        dvg_ref[...] += jnp.sum(dvn * vhat, axis=0, keepdims=True)
        dvh = dvn * vgv
        dpa_ref[:, AW:2 * AW] = (rstd * (dvh - jnp.mean(dvh, axis=-1, keepdims=True)
                                         - vhat * jnp.mean(dvh * vhat, axis=-1, keepdims=True))).astype(BF)

        xs = p_ref[:, 3 * AW:4 * AW]
        y = _cat_lanes(yf_ref) + _cat_lanes(yb_ref) + d_ref[...] * xs
        y2, th = _gelu(y)
        y2b = y2.astype(BF)
        z = _dot(y2b, wg_ref[0]) + bg_ref[...]
        sz = _sig(z)
        gb = p_ref[:, 4 * AW:5 * AW]
        sgb = _sig(gb)
        dyb = dm_ref[:, AW:2 * AW]
        dpa_ref[:, 4 * AW:5 * AW] = (dyb * (y2 * sz) * (sgb * (1.0 + gb * (1.0 - sgb)))).astype(BF)
        dy3 = dyb * (gb * sgb)
        dz = dy3 * y2 * sz * (1.0 - sz)
        dzb = dz.astype(BF)
        dwg_ref[...] += _dot(y2b, dzb, 0, 0)
        dbg_ref[...] += jnp.sum(dz, axis=0, keepdims=True)
        dy2 = dy3 * sz + _dot(dzb, wg_ref[0], 1, 1)
        dgelu = 0.5 * (1.0 + th) + 0.5 * y * (1.0 - th * th) * GELU_K * (1.0 + 3.0 * GELU_C * y * y)
        dy = dy2 * dgelu
        dd_ref[...] += jnp.sum(dy * xs, axis=0, keepdims=True)
        dy_ref[...] = dy

    tile = lambda w: pl.BlockSpec((TM, w), lambda i: (i, 0))
    return _call(body, name=name, grid=(t // TM,),
                 in_specs=[tile(5 * AW), Y4_SPEC(), Y4_SPEC(), tile(2 * AW), _full((1, AW)), _full((NGRP, CHUNK, CHUNK)),
                           _full((NGRP, CHUNK, 1)), _full((1, AW)),
                           pl.BlockSpec((1, AW, AW), lambda i: (glu_layer, 0, 0)), _full((1, AW))],
                 out_specs=[tile(5 * AW), tile(AW), _full((NGRP, CHUNK, CHUNK)), _full((NGRP, CHUNK, 1)),
                            _full((1, AW)), _full((1, AW)), _full((AW, AW)), _full((1, AW))],
                 out_shape=[jax.ShapeDtypeStruct((t, 5 * AW), BF),
                            jax.ShapeDtypeStruct((t, AW), F32), jax.ShapeDtypeStruct((NGRP, CHUNK, CHUNK), F32),
                            jax.ShapeDtypeStruct((NGRP, CHUNK, 1), F32), jax.ShapeDtypeStruct((1, AW), F32),
                            jax.ShapeDtypeStruct((1, AW), F32), jax.ShapeDtypeStruct((AW, AW), F32),
                            jax.ShapeDtypeStruct((1, AW), F32)],
                 scratch=[pltpu.VMEM((TM, AW), BF), pltpu.VMEM((TM, AW), F32), pltpu.VMEM((TM, AW), F32),
                          pltpu.VMEM((TM, AW), F32)])(p, yf, yb, dmix, vg, ws, bs, dsk, wglu, bglu)


LN = 512
NBLK = SW // LN
UB = AW // NBLK
SCAN_R = 32
SCAN_G = TM // SCAN_R
PW_ROWS = SCAN_R
POW_EXP = list(range(1, SCAN_R + 1))
GROUPS_PER_TILE = TM // 8


def s5_disc(lam_re, lam_im, dt, lam_re_r, lam_im_r, dt_r, b_re, b_im):
    nexp = jnp.asarray(np.array(POW_EXP, np.float32).reshape(PW_ROWS, 1))

    def body(n_ref, lr_ref, li_ref, dt_ref, lrr_ref, lir_ref, dtr_ref, br_ref, bi_ref,
             pr_ref, pi_ref, bbr_ref, bbi_ref):
        for dr in range(2):
            dtl = jnp.exp(dt_ref[dr:dr + 1, :])
            zr = lr_ref[dr:dr + 1, :] * dtl
            zi = li_ref[dr:dr + 1, :] * dtl
            mag = jnp.exp(n_ref[...] * zr)
            ang = n_ref[...] * zi
            pr_ref[dr] = mag * jnp.cos(ang)
            pi_ref[dr] = mag * jnp.sin(ang)
        lr, li, dtv = lrr_ref[...], lir_ref[...], jnp.exp(dtr_ref[...])
        mag = jnp.exp(lr * dtv)
        nr = mag * jnp.cos(li * dtv) - 1.0
        ni = mag * jnp.sin(li * dtv)
        den = lr * lr + li * li
        fr = (nr * lr + ni * li) / den
        fi = (ni * lr - nr * li) / den
        bbr_ref[...] = fr * br_ref[...] - fi * bi_ref[...]
        bbi_ref[...] = fr * bi_ref[...] + fi * br_ref[...]

    rows = lam_re_r.shape[0]
    return _call(body, name="s5_disc",
                 out_shape=[jax.ShapeDtypeStruct((2, PW_ROWS, SW), F32), jax.ShapeDtypeStruct((2, PW_ROWS, SW), F32),
                            jax.ShapeDtypeStruct((rows, SP), F32), jax.ShapeDtypeStruct((rows, SP), F32)])(
        nexp, lam_re, lam_im, dt, lam_re_r, lam_im_r, dt_r, b_re, b_im)


def s5_param_bwd(lam_re_r, lam_im_r, dt_r, b_re, b_im, da_re, da_im, dbb_re, dbb_im):
    rows = lam_re_r.shape[0]
    ng = rows // SH
    seg = jnp.asarray(np.kron(np.eye(ng, dtype=np.float32), np.ones((1, SH), np.float32)))

    def body(seg_ref, lr_ref, li_ref, dt_ref, br_ref, bi_ref, dar_ref, dai_ref, dbbr_ref, dbbi_ref,
             dlr_ref, dli_ref, ddt_ref, dbr_ref, dbi_ref):
        lr, li, dtv = lr_ref[...], li_ref[...], jnp.exp(dt_ref[...])
        mag = jnp.exp(lr * dtv)
        lbr = mag * jnp.cos(li * dtv)
        lbi = mag * jnp.sin(li * dtv)
        den = lr * lr + li * li
        nr, ni = lbr - 1.0, lbi
        fr = (nr * lr + ni * li) / den
        fi = (ni * lr - nr * li) / den
        br, bi = br_ref[...], bi_ref[...]
        gbr, gbi = dbbr_ref[...], dbbi_ref[...]
        dbr_ref[...] = gbr * fr + gbi * fi
        dbi_ref[...] = gbi * fr - gbr * fi
        gfr = gbr * br + gbi * bi
        gfi = gbi * br - gbr * bi
        ilr, ili = lr / den, -li / den
        gnr = gfr * ilr + gfi * ili
        gni = gfi * ilr - gfr * ili
        qr = -(fr * ilr - fi * ili)
        qi = -(fr * ili + fi * ilr)
        glr = gfr * qr + gfi * qi
        gli = gfi * qr - gfr * qi
        first = (lax.broadcasted_iota(jnp.int32, (rows, 1), 0) % SH) == 0
        glbr = gnr + jnp.where(first, dar_ref[...], 0.0)
        glbi = gni + jnp.where(first, dai_ref[...], 0.0)
        gzr = glbr * lbr + glbi * lbi
        gzi = glbi * lbr - glbr * lbi
        glr = glr + gzr * dtv
        gli = gli + gzi * dtv
        gdt = (gzr * lr + gzi * li) * dtv
        hi = lax.Precision.HIGHEST
        sg = seg_ref[...]
        dlr_ref[...] = jnp.dot(sg, glr, precision=hi, preferred_element_type=F32)
        dli_ref[...] = jnp.dot(sg, gli, precision=hi, preferred_element_type=F32)
        ddt_ref[...] = jnp.sum(jnp.dot(sg, gdt, precision=hi, preferred_element_type=F32), axis=1, keepdims=True)

    return _call(body, name="s5_param_bwd",
                 out_shape=[jax.ShapeDtypeStruct((ng, SP), F32), jax.ShapeDtypeStruct((ng, SP), F32),
                            jax.ShapeDtypeStruct((ng, 1), F32), jax.ShapeDtypeStruct((rows, SP), F32),
                            jax.ShapeDtypeStruct((rows, SP), F32)])(
        seg, lam_re_r, lam_im_r, dt_r, b_re, b_im, da_re, da_im, dbb_re, dbb_im)


def _scan_tile(hr_ref, hi_ref, er_ref, ei_ref, st_ref, cr_ref, ci_ref, pr_ref, pi_ref, reverse):
    row8 = lax.broadcasted_iota(jnp.int32, (TM, 1), 0) % 8
    rowe = lax.broadcasted_iota(jnp.int32, (2 * GROUPS_PER_TILE, 1), 0)
    ne = 2 * GROUPS_PER_TILE
    for blk in range(NBLK):
        cols = slice(blk * LN, (blk + 1) * LN)
        hr, hi = hr_ref[:, cols], hi_ref[:, cols]
        for k, s in enumerate((1, 2, 4)):
            ar, ai = pr_ref[k:k + 1, cols], pi_ref[k:k + 1, cols]
            if reverse:
                m = row8 < 8 - s
                sr, si = pltpu.roll(hr, TM - s, 0), pltpu.roll(hi, TM - s, 0)
            else:
                m = row8 >= s
                sr, si = pltpu.roll(hr, s, 0), pltpu.roll(hi, s, 0)
            sr, si = jnp.where(m, sr, 0.0), jnp.where(m, si, 0.0)
            hr, hi = hr + ar * sr - ai * si, hi + ar * si + ai * sr
        hr_ref[:, cols] = hr
        hi_ref[:, cols] = hi
        edge = 0 if reverse else 7
        nq = LN // 128
        for q in range(nq):
            st_ref[q] = hr[:, q * 128:(q + 1) * 128]
            st_ref[nq + q] = hi[:, q * 128:(q + 1) * 128]
        gr = jnp.concatenate([st_ref[q, pl.ds(edge, GROUPS_PER_TILE, stride=8), :] for q in range(nq)], axis=1)
        gi = jnp.concatenate([st_ref[nq + q, pl.ds(edge, GROUPS_PER_TILE, stride=8), :] for q in range(nq)], axis=1)
        zero = jnp.zeros((GROUPS_PER_TILE, LN), F32)
        if reverse:
            er_ref[0:GROUPS_PER_TILE, :] = gr
            ei_ref[0:GROUPS_PER_TILE, :] = gi
            er_ref[GROUPS_PER_TILE:ne, :] = zero
            ei_ref[GROUPS_PER_TILE:ne, :] = zero
            er_ref[GROUPS_PER_TILE:GROUPS_PER_TILE + 1, :] = cr_ref[:, cols]
            ei_ref[GROUPS_PER_TILE:GROUPS_PER_TILE + 1, :] = ci_ref[:, cols]
        else:
            er_ref[0:GROUPS_PER_TILE, :] = zero
            ei_ref[0:GROUPS_PER_TILE, :] = zero
            er_ref[GROUPS_PER_TILE - 1:GROUPS_PER_TILE, :] = cr_ref[:, cols]
            ei_ref[GROUPS_PER_TILE - 1:GROUPS_PER_TILE, :] = ci_ref[:, cols]
            er_ref[GROUPS_PER_TILE:ne, :] = gr
            ei_ref[GROUPS_PER_TILE:ne, :] = gi
        evr, evi = er_ref[...], ei_ref[...]
        s = 1
        k = 3
        while s < ne:
            ar, ai = pr_ref[k:k + 1, cols], pi_ref[k:k + 1, cols]
            if reverse:
                m = rowe < ne - s
                sr, si = pltpu.roll(evr, ne - s, 0), pltpu.roll(evi, ne - s, 0)
            else:
                m = rowe >= s
                sr, si = pltpu.roll(evr, s, 0), pltpu.roll(evi, s, 0)
            sr, si = jnp.where(m, sr, 0.0), jnp.where(m, si, 0.0)
            evr, evi = evr + ar * sr - ai * si, evi + ar * si + ai * sr
            s *= 2
            k += 1
        er_ref[...] = evr
        ei_ref[...] = evi
        if reverse:
            cr_ref[:, cols] = er_ref[0:1, :]
            ci_ref[:, cols] = ei_ref[0:1, :]
            apr, api = pr_ref[24:32, cols], pi_ref[24:32, cols]
        else:
            cr_ref[:, cols] = er_ref[ne - 1:ne, :]
            ci_ref[:, cols] = ei_ref[ne - 1:ne, :]
            apr, api = pr_ref[16:24, cols], pi_ref[16:24, cols]
        for g in range(GROUPS_PER_TILE):
            e_row = g + 1 if reverse else GROUPS_PER_TILE - 1 + g
            kr, ki = er_ref[e_row:e_row + 1, :], ei_ref[e_row:e_row + 1, :]
            rows = slice(8 * g, 8 * g + 8)
            hr_ref[rows, cols] = hr_ref[rows, cols] + apr * kr - api * ki
            hi_ref[rows, cols] = hi_ref[rows, cols] + apr * ki + api * kr


def _tile_order(kind, nt):
    if kind == "fwd":
        return lambda i: i
    if kind == "bwd":
        return lambda i: jnp.where(i == 0, 0, nt - i)
    if kind == "fwd_adj":
        return lambda i: nt - 1 - i
    if kind == "bwd_adj":
        return lambda i: jnp.where(i == nt - 1, 0, i + 1)
    raise ValueError(kind)


def s5_fwd(p, bb_re, bb_im, ct_re, ct_im, pw_re, pw_im, reverse, name):
    t = p.shape[0]
    nt = t // TM
    order = _tile_order("bwd" if reverse else "fwd", nt)

    def body(x_ref, bbr_ref, bbi_ref, ctr_ref, cti_ref, pr_ref, pi_ref, y_ref, hpr_ref, hpi_ref,
             hr_ref, hi_ref, er_ref, ei_ref, st_ref, cr_ref, ci_ref, c0r_ref, c0i_ref):
        @pl.when(pl.program_id(0) == 0)
        def _():
            cr_ref[...] = jnp.zeros_like(cr_ref)
            ci_ref[...] = jnp.zeros_like(ci_ref)

        c0r_ref[...] = cr_ref[...]
        c0i_ref[...] = ci_ref[...]
        xb = x_ref[...].astype(BF)
        for j in range(NBLK):
            cols = slice(j * LN, (j + 1) * LN)
            hr_ref[:, cols] = _dot(xb[:, j * UB:(j + 1) * UB], bbr_ref[j])
            hi_ref[:, cols] = _dot(xb[:, j * UB:(j + 1) * UB], bbi_ref[j])
        _scan_tile(hr_ref, hi_ref, er_ref, ei_ref, st_ref, cr_ref, ci_ref, pr_ref, pi_ref, reverse)
        rowi = lax.broadcasted_iota(jnp.int32, (TM, 1), 0)
        for j in range(NBLK):
            cols = slice(j * LN, (j + 1) * LN)
            hr, hi = hr_ref[:, cols], hi_ref[:, cols]
            y_ref[:, j * UB:(j + 1) * UB] = _dot(hr.astype(BF), ctr_ref[j]) - _dot(hi.astype(BF), cti_ref[j])
            if reverse:
                first = rowi == TM - 1
                sr, si = pltpu.roll(hr, TM - 1, 0), pltpu.roll(hi, TM - 1, 0)
            else:
                first = rowi == 0
                sr, si = pltpu.roll(hr, 1, 0), pltpu.roll(hi, 1, 0)
            hpr_ref[:, cols] = jnp.where(first, c0r_ref[:, cols], sr)
            hpi_ref[:, cols] = jnp.where(first, c0i_ref[:, cols], si)

    state = lambda: pl.BlockSpec((TM, SW), lambda i: (order(i), 0))
    return _call(body, name=name, grid=(nt,),
                 in_specs=[pl.BlockSpec((TM, AW), lambda i: (order(i), 3)),
                           _full((NBLK, UB, LN)), _full((NBLK, UB, LN)), _full((NBLK, LN, UB)), _full((NBLK, LN, UB)),
                           _full((PW_ROWS, SW)), _full((PW_ROWS, SW))],
                 out_specs=[pl.BlockSpec((TM, AW), lambda i: (order(i), 0)), state(), state()],
                 out_shape=[jax.ShapeDtypeStruct((t, AW), F32), jax.ShapeDtypeStruct((t, SW), F32),
                            jax.ShapeDtypeStruct((t, SW), F32)],
                 scratch=[pltpu.VMEM((TM, SW), F32), pltpu.VMEM((TM, SW), F32),
                          pltpu.VMEM((2 * GROUPS_PER_TILE, LN), F32), pltpu.VMEM((2 * GROUPS_PER_TILE, LN), F32),
                          pltpu.VMEM((2 * LN // 128, TM, 128), F32),
                          pltpu.VMEM((1, SW), F32), pltpu.VMEM((1, SW), F32),
                          pltpu.VMEM((1, SW), F32), pltpu.VMEM((1, SW), F32)])(
        p, bb_re, bb_im, ct_re, ct_im, pw_re, pw_im)


def s5_bwd(p, hp_re, hp_im, dy, bb_re, bb_im, ct_re, ct_im, pw_re, pw_im_conj, a_re, a_im, reverse, name):
    t = p.shape[0]
    nt = t // TM
    order = _tile_order("bwd_adj" if reverse else "fwd_adj", nt)

    def body(x_ref, hpr_ref, hpi_ref, dy_ref, bbr_ref, bbi_ref, ctr_ref, cti_ref, pr_ref, pi_ref, ar_ref, ai_ref,
             dx_ref, dar_ref, dai_ref, dbbr_ref, dbbi_ref, dcr_ref, dci_ref,
             gr_ref, gi_ref, er_ref, ei_ref, st_ref, cr_ref, ci_ref):
        @pl.when(pl.program_id(0) == 0)
        def _():
            for r in (cr_ref, ci_ref, dar_ref, dai_ref, dbbr_ref, dbbi_ref, dcr_ref, dci_ref):
                r[...] = jnp.zeros_like(r)

        xb = x_ref[...].astype(BF)
        dyb = dy_ref[...].astype(BF)
        for j in range(NBLK):
            cols = slice(j * LN, (j + 1) * LN)
            gr_ref[:, cols] = _dot(dyb[:, j * UB:(j + 1) * UB], ctr_ref[j], 1, 1)
            gi_ref[:, cols] = -_dot(dyb[:, j * UB:(j + 1) * UB], cti_ref[j], 1, 1)
        _scan_tile(gr_ref, gi_ref, er_ref, ei_ref, st_ref, cr_ref, ci_ref, pr_ref, pi_ref, not reverse)
        for j in range(NBLK):
            cols = slice(j * LN, (j + 1) * LN)
            xj = xb[:, j * UB:(j + 1) * UB]
            dyj = dyb[:, j * UB:(j + 1) * UB]
            hpr, hpi = hpr_ref[:, cols], hpi_ref[:, cols]
            gr, gi = gr_ref[:, cols], gi_ref[:, cols]
            ar, ai = ar_ref[:, cols], ai_ref[:, cols]
            hr = ar * hpr - ai * hpi + _dot(xj, bbr_ref[j])
            hi = ar * hpi + ai * hpr + _dot(xj, bbi_ref[j])
            dar_ref[:, cols] += jnp.sum(gr * hpr + gi * hpi, axis=0, keepdims=True)
            dai_ref[:, cols] += jnp.sum(gi * hpr - gr * hpi, axis=0, keepdims=True)
            grb, gib = gr.astype(BF), gi.astype(BF)
            dcr_ref[j] += _dot(dyj, hr.astype(BF), 0, 0)
            dci_ref[j] += -_dot(dyj, hi.astype(BF), 0, 0)
            dbbr_ref[j] += _dot(xj, grb, 0, 0)
            dbbi_ref[j] += _dot(xj, gib, 0, 0)
            dx_ref[:, j * UB:(j + 1) * UB] = _dot(grb, bbr_ref[j], 1, 1) + _dot(gib, bbi_ref[j], 1, 1)

    state = lambda: pl.BlockSpec((TM, SW), lambda i: (order(i), 0))
    blockd = lambda: _full((NBLK, UB, LN))
    return _call(body, name=name, grid=(nt,),
                 in_specs=[pl.BlockSpec((TM, AW), lambda i: (order(i), 3)), state(), state(),
                           pl.BlockSpec((TM, AW), lambda i: (order(i), 0)),
                           blockd(), blockd(), _full((NBLK, LN, UB)), _full((NBLK, LN, UB)),
                           _full((PW_ROWS, SW)), _full((PW_ROWS, SW)), _full((1, SW)), _full((1, SW))],
                 out_specs=[pl.BlockSpec((TM, AW), lambda i: (order(i), 0)), _full((1, SW)), _full((1, SW)),
                            blockd(), blockd(), blockd(), blockd()],
                 out_shape=[jax.ShapeDtypeStruct((t, AW), F32), jax.ShapeDtypeStruct((1, SW), F32),
                            jax.ShapeDtypeStruct((1, SW), F32)] + [jax.ShapeDtypeStruct((NBLK, UB, LN), F32)] * 4,
                 scratch=[pltpu.VMEM((TM, SW), F32), pltpu.VMEM((TM, SW), F32),
                          pltpu.VMEM((2 * GROUPS_PER_TILE, LN), F32), pltpu.VMEM((2 * GROUPS_PER_TILE, LN), F32),
                          pltpu.VMEM((2 * LN // 128, TM, 128), F32),
                          pltpu.VMEM((1, SW), F32), pltpu.VMEM((1, SW), F32)])(
        p, hp_re, hp_im, dy, bb_re, bb_im, ct_re, ct_im, pw_re, pw_im_conj, a_re, a_im)


def s5_dx_sum(dy, dsk, dxf, dxb, name):
    t = dy.shape[0]

    def body(dy_ref, d_ref, f_ref, b_ref, o_ref):
        o_ref[...] = (dy_ref[...] * d_ref[...] + f_ref[...] + b_ref[...]).astype(BF)

    tile = pl.BlockSpec((TM, AW), lambda i: (i, 0))
    return _call(body, name=name, grid=(t // TM,), in_specs=[tile, _full((1, AW)), tile, tile], out_specs=tile,
                 out_shape=jax.ShapeDtypeStruct((t, AW), BF))(dy, dsk, dxf, dxb)


XS_BLK = 3 * AW // 128


def _load_perm(refs):
    return jnp.concatenate(
        [jnp.concatenate([ref[pl.ds(r, SCAN_G, stride=SCAN_R), :] for ref in refs], axis=1) for r in range(SCAN_R)],
        axis=0)


def _store_perm(out_ref, val):
    for r in range(SCAN_R):
        for q in range(AW // 128):
            out_ref[q, pl.ds(r, SCAN_G, stride=SCAN_R), :] = val[r * SCAN_G:(r + 1) * SCAN_G, q * 128:(q + 1) * 128]


def _scan_perm(hr_ref, hi_ref, er_ref, ei_ref, cr_ref, ci_ref, pr_ref, pi_ref, reverse, hpr_ref=None, hpi_ref=None):
    gpt = SCAN_G
    nr = SCAN_R
    offsets = list(range(nr))[::-1] if reverse else list(range(nr))
    blocks = [slice(b * LN, (b + 1) * LN) for b in range(NBLK)]
    slab = lambda r: slice(r * gpt, (r + 1) * gpt)
    a1 = [(pr_ref[0:1, c], pi_ref[0:1, c]) for c in blocks]
    x = [None] * NBLK
    for r in offsets:
        for b, c in enumerate(blocks):
            if x[b] is None:
                x[b] = (hr_ref[slab(r), c], hi_ref[slab(r), c])
            else:
                (ar, ai), (xr, xi) = a1[b], x[b]
                x[b] = (hr_ref[slab(r), c] + ar * xr - ai * xi, hi_ref[slab(r), c] + ar * xi + ai * xr)
                hr_ref[slab(r), c] = x[b][0]
                hi_ref[slab(r), c] = x[b][1]
    an = [(pr_ref[nr - 1:nr, c], pi_ref[nr - 1:nr, c]) for c in blocks]
    k = [(cr_ref[:, c], ci_ref[:, c]) for c in blocks]
    for g in (range(gpt - 1, -1, -1) if reverse else range(gpt)):
        for b, c in enumerate(blocks):
            (ar, ai), (kr, ki), (xr, xi) = an[b], k[b], x[b]
            er_ref[g:g + 1, c] = kr
            ei_ref[g:g + 1, c] = ki
            k[b] = (xr[g:g + 1, :] + ar * kr - ai * ki, xi[g:g + 1, :] + ar * ki + ai * kr)
    for b, c in enumerate(blocks):
        cr_ref[:, c] = k[b][0]
        ci_ref[:, c] = k[b][1]
    for r in range(nr):
        prow = nr - 1 - r if reverse else r
        dst = r - 1 if reverse else r + 1
        for c in blocks:
            apr, api = pr_ref[prow:prow + 1, c], pi_ref[prow:prow + 1, c]
            cinr, cini = er_ref[:, c], ei_ref[:, c]
            hr = hr_ref[slab(r), c] + apr * cinr - api * cini
            hi = hi_ref[slab(r), c] + apr * cini + api * cinr
            hr_ref[slab(r), c] = hr
            hi_ref[slab(r), c] = hi
            if hpr_ref is not None:
                if 0 <= dst < nr:
                    hpr_ref[slab(dst), c] = hr
                    hpi_ref[slab(dst), c] = hi
                if r == (nr - 1 if reverse else 0):
                    hpr_ref[slab(r), c] = cinr
                    hpi_ref[slab(r), c] = cini


def s5p_fwd(p, bb_re, bb_im, ct_re, ct_im, pw_re, pw_im, reverse, name):
    t = p.shape[0]
    nt = t // TM
    order = _tile_order("bwd" if reverse else "fwd", nt)
    nq = AW // 128

    def body(*refs):
        x_refs = refs[:nq]
        bbr_ref, bbi_ref, ctr_ref, cti_ref, pr_ref, pi_ref, y_ref, hpr_ref, hpi_ref = refs[nq:nq + 9]
        hr_ref, hi_ref, er_ref, ei_ref, cr_ref, ci_ref = refs[nq + 9:]

        @pl.when(pl.program_id(0) == 0)
        def _():
            cr_ref[...] = jnp.zeros_like(cr_ref)
            ci_ref[...] = jnp.zeros_like(ci_ref)

        xb = _load_perm(x_refs).astype(BF)
        for j in range(NBLK):
            cols = slice(j * LN, (j + 1) * LN)
            hr_ref[:, cols] = _dot(xb[:, j * UB:(j + 1) * UB], bbr_ref[j])
            hi_ref[:, cols] = _dot(xb[:, j * UB:(j + 1) * UB], bbi_ref[j])
        _scan_perm(hr_ref, hi_ref, er_ref, ei_ref, cr_ref, ci_ref, pr_ref, pi_ref, reverse, hpr_ref, hpi_ref)
        y = jnp.concatenate(
            [_dot(hr_ref[:, j * LN:(j + 1) * LN].astype(BF), ctr_ref[j])
             - _dot(hi_ref[:, j * LN:(j + 1) * LN].astype(BF), cti_ref[j]) for j in range(NBLK)], axis=1)
        _store_perm(y_ref, y)

    state = lambda: pl.BlockSpec((TM, SW), lambda i: (order(i), 0))
    xspec = lambda q: pl.BlockSpec((TM, 128), lambda i: (order(i), XS_BLK + q))
    return _call(body, name=name, grid=(nt,),
                 in_specs=[xspec(q) for q in range(nq)]
                 + [_full((NBLK, UB, LN)), _full((NBLK, UB, LN)), _full((NBLK, LN, UB)), _full((NBLK, LN, UB)),
                    _full((PW_ROWS, SW)), _full((PW_ROWS, SW))],
                 out_specs=[pl.BlockSpec((nq, TM, 128), lambda i: (0, order(i), 0)), state(), state()],
                 out_shape=[jax.ShapeDtypeStruct((nq, t, 128), F32), jax.ShapeDtypeStruct((t, SW), F32),
                            jax.ShapeDtypeStruct((t, SW), F32)],
                 scratch=[pltpu.VMEM((TM, SW), F32), pltpu.VMEM((TM, SW), F32),
                          pltpu.VMEM((SCAN_G, SW), F32), pltpu.VMEM((SCAN_G, SW), F32),
                          pltpu.VMEM((1, SW), F32), pltpu.VMEM((1, SW), F32)])(
        *([p] * nq), bb_re, bb_im, ct_re, ct_im, pw_re, pw_im)


def s5p_bwd(p, hp_re, hp_im, dy, bb_re, bb_im, ct_re, ct_im, pw_re, pw_im_conj, a_re, a_im, reverse, name):
    t = p.shape[0]
    nt = t // TM
    order = _tile_order("bwd_adj" if reverse else "fwd_adj", nt)
    nq = AW // 128

    def body(*refs):
        x_refs, dy_refs = refs[:nq], refs[nq:2 * nq]
        (hpr_ref, hpi_ref, bbr_ref, bbi_ref, ctr_ref, cti_ref, pr_ref, pi_ref, ar_ref, ai_ref,
         dx_ref, dar_ref, dai_ref, dbbr_ref, dbbi_ref, dcr_ref, dci_ref,
         gr_ref, gi_ref, er_ref, ei_ref, cr_ref, ci_ref) = refs[2 * nq:]

        @pl.when(pl.program_id(0) == 0)
        def _():
            for r in (cr_ref, ci_ref, dar_ref, dai_ref, dbbr_ref, dbbi_ref, dcr_ref, dci_ref):
                r[...] = jnp.zeros_like(r)

        xb = _load_perm(x_refs).astype(BF)
        dyb = _load_perm(dy_refs).astype(BF)
        for j in range(NBLK):
            cols = slice(j * LN, (j + 1) * LN)
            gr_ref[:, cols] = _dot(dyb[:, j * UB:(j + 1) * UB], ctr_ref[j], 1, 1)
            gi_ref[:, cols] = -_dot(dyb[:, j * UB:(j + 1) * UB], cti_ref[j], 1, 1)
        _scan_perm(gr_ref, gi_ref, er_ref, ei_ref, cr_ref, ci_ref, pr_ref, pi_ref, not reverse)
        dxs = []
        for j in range(NBLK):
            cols = slice(j * LN, (j + 1) * LN)
            xj = xb[:, j * UB:(j + 1) * UB]
            dyj = dyb[:, j * UB:(j + 1) * UB]
            hpr, hpi = hpr_ref[:, cols], hpi_ref[:, cols]
            gr, gi = gr_ref[:, cols], gi_ref[:, cols]
            ar, ai = ar_ref[:, cols], ai_ref[:, cols]
            hr = ar * hpr - ai * hpi + _dot(xj, bbr_ref[j])
            hi = ar * hpi + ai * hpr + _dot(xj, bbi_ref[j])
            dar_ref[:, cols] += jnp.sum(gr * hpr + gi * hpi, axis=0, keepdims=True)
            dai_ref[:, cols] += jnp.sum(gi * hpr - gr * hpi, axis=0, keepdims=True)
            grb, gib = gr.astype(BF), gi.astype(BF)
            dcr_ref[j] += _dot(dyj, hr.astype(BF), 0, 0)
            dci_ref[j] += -_dot(dyj, hi.astype(BF), 0, 0)
            dbbr_ref[j] += _dot(xj, grb, 0, 0)
            dbbi_ref[j] += _dot(xj, gib, 0, 0)
            dxs.append(_dot(grb, bbr_ref[j], 1, 1) + _dot(gib, bbi_ref[j], 1, 1))
        _store_perm(dx_ref, jnp.concatenate(dxs, axis=1))

    state = lambda: pl.BlockSpec((TM, SW), lambda i: (order(i), 0))
    blockd = lambda: _full((NBLK, UB, LN))
    xspec = lambda q: pl.BlockSpec((TM, 128), lambda i: (order(i), XS_BLK + q))
    dyspec = lambda q: pl.BlockSpec((TM, 128), lambda i: (order(i), q))
    return _call(body, name=name, grid=(nt,),
                 in_specs=[xspec(q) for q in range(nq)] + [dyspec(q) for q in range(nq)]
                 + [state(), state(), blockd(), blockd(), _full((NBLK, LN, UB)), _full((NBLK, LN, UB)),
                    _full((PW_ROWS, SW)), _full((PW_ROWS, SW)), _full((1, SW)), _full((1, SW))],
                 out_specs=[pl.BlockSpec((nq, TM, 128), lambda i: (0, order(i), 0)), _full((1, SW)), _full((1, SW)),
                            blockd(), blockd(), blockd(), blockd()],
                 out_shape=[jax.ShapeDtypeStruct((nq, t, 128), F32), jax.ShapeDtypeStruct((1, SW), F32),
                            jax.ShapeDtypeStruct((1, SW), F32)] + [jax.ShapeDtypeStruct((NBLK, UB, LN), F32)] * 4,
                 scratch=[pltpu.VMEM((TM, SW), F32), pltpu.VMEM((TM, SW), F32),
                          pltpu.VMEM((SCAN_G, SW), F32), pltpu.VMEM((SCAN_G, SW), F32),
                          pltpu.VMEM((1, SW), F32), pltpu.VMEM((1, SW), F32)])(
        *([p] * nq), *([dy] * nq), hp_re, hp_im, bb_re, bb_im, ct_re, ct_im, pw_re, pw_im_conj, a_re, a_im)


def _scan2(br_ref, bi_ref, or_ref, oi_ref, h_off, cin_off, er_ref, ei_ref, cr_ref, ci_ref, pr_ref, pi_ref, reverse):
    gpt = SCAN_G
    nr = SCAN_R
    offsets = list(range(nr))[::-1] if reverse else list(range(nr))
    blocks = [slice(b * LN, (b + 1) * LN) for b in range(NBLK)]
    slab = lambda r: slice(r * gpt, (r + 1) * gpt)
    a1 = [(pr_ref[0:1, c], pi_ref[0:1, c]) for c in blocks]
    x = [None] * NBLK
    for r in offsets:
        for b, c in enumerate(blocks):
            if x[b] is None:
                x[b] = (br_ref[slab(r), c], bi_ref[slab(r), c])
            else:
                (ar, ai), (xr, xi) = a1[b], x[b]
                x[b] = (br_ref[slab(r), c] + ar * xr - ai * xi, bi_ref[slab(r), c] + ar * xi + ai * xr)
    an = [(pr_ref[nr - 1:nr, c], pi_ref[nr - 1:nr, c]) for c in blocks]
    k = [(cr_ref[:, c], ci_ref[:, c]) for c in blocks]
    for g in (range(gpt - 1, -1, -1) if reverse else range(gpt)):
        for b, c in enumerate(blocks):
            (ar, ai), (kr, ki), (xr, xi) = an[b], k[b], x[b]
            er_ref[g:g + 1, c] = kr
            ei_ref[g:g + 1, c] = ki
            k[b] = (xr[g:g + 1, :] + ar * kr - ai * ki, xi[g:g + 1, :] + ar * ki + ai * kr)
    for b, c in enumerate(blocks):
        cr_ref[:, c] = k[b][0]
        ci_ref[:, c] = k[b][1]
        x[b] = (er_ref[:, c], ei_ref[:, c])
        if cin_off is not None:
            or_ref[cin_off:cin_off + gpt, c] = x[b][0]
            oi_ref[cin_off:cin_off + gpt, c] = x[b][1]
    for r in offsets:
        for b, c in enumerate(blocks):
            (ar, ai), (xr, xi) = a1[b], x[b]
            x[b] = (br_ref[slab(r), c] + ar * xr - ai * xi, bi_ref[slab(r), c] + ar * xi + ai * xr)
            or_ref[h_off + r * gpt:h_off + (r + 1) * gpt, c] = x[b][0]
            oi_ref[h_off + r * gpt:h_off + (r + 1) * gpt, c] = x[b][1]


HS_ROWS = TM + SCAN_G


def _hs_offsets(reverse):
    return (0, SCAN_G) if reverse else (SCAN_G, 0)


def _dir_spec(dr, shape):
    return pl.BlockSpec((1,) + shape, lambda i: (dr,) + (0,) * len(shape))


def s5q_fwd(p, bb_re, bb_im, ct_re, ct_im, pw_re, pw_im, dr, reverse, name):
    t = p.shape[0]
    nt = t // TM
    order = _tile_order("bwd" if reverse else "fwd", nt)
    nq = AW // 128
    h_off, p_off = _hs_offsets(reverse)

    def body(*refs):
        x_refs = refs[:nq]
        bbr_ref, bbi_ref, ctr_ref, cti_ref, pr_ref, pi_ref = [r.at[0] for r in refs[nq:nq + 6]]
        y_ref, hsr_ref, hsi_ref = refs[nq + 6:nq + 9]
        br_ref, bi_ref, er_ref, ei_ref, cr_ref, ci_ref = refs[nq + 9:]

        @pl.when(pl.program_id(0) == 0)
        def _():
            cr_ref[...] = jnp.zeros_like(cr_ref)
            ci_ref[...] = jnp.zeros_like(ci_ref)

        xb = _load_perm(x_refs).astype(BF)
        for j in range(NBLK):
            cols = slice(j * LN, (j + 1) * LN)
            br_ref[:, cols] = _dot(xb[:, j * UB:(j + 1) * UB], bbr_ref[j])
            bi_ref[:, cols] = _dot(xb[:, j * UB:(j + 1) * UB], bbi_ref[j])
        _scan2(br_ref, bi_ref, hsr_ref, hsi_ref, h_off, TM if reverse else 0, er_ref, ei_ref, cr_ref, ci_ref,
               pr_ref, pi_ref, reverse)
        y = jnp.concatenate(
            [_dot(hsr_ref[h_off:h_off + TM, j * LN:(j + 1) * LN].astype(BF), ctr_ref[j])
             - _dot(hsi_ref[h_off:h_off + TM, j * LN:(j + 1) * LN].astype(BF), cti_ref[j]) for j in range(NBLK)], axis=1)
        _store_perm(y_ref, y)

    state = lambda: pl.BlockSpec((HS_ROWS, SW), lambda i: (order(i), 0))
    xspec = lambda q: pl.BlockSpec((TM, 128), lambda i: (order(i), XS_BLK + q))
    return _call(body, name=name, grid=(nt,),
                 in_specs=[xspec(q) for q in range(nq)]
                 + [_dir_spec(dr, (NBLK, UB, LN)), _dir_spec(dr, (NBLK, UB, LN)), _dir_spec(dr, (NBLK, LN, UB)),
                    _dir_spec(dr, (NBLK, LN, UB)), _dir_spec(dr, (PW_ROWS, SW)), _dir_spec(dr, (PW_ROWS, SW))],
                 out_specs=[pl.BlockSpec((nq, TM, 128), lambda i: (0, order(i), 0)), state(), state()],
                 out_shape=[jax.ShapeDtypeStruct((nq, t, 128), F32), jax.ShapeDtypeStruct((nt * HS_ROWS, SW), F32),
                            jax.ShapeDtypeStruct((nt * HS_ROWS, SW), F32)],
                 scratch=[pltpu.VMEM((TM, SW), F32), pltpu.VMEM((TM, SW), F32),
                          pltpu.VMEM((SCAN_G, SW), F32), pltpu.VMEM((SCAN_G, SW), F32),
                          pltpu.VMEM((1, SW), F32), pltpu.VMEM((1, SW), F32)])(
        *([p] * nq), bb_re, bb_im, ct_re, ct_im, pw_re, pw_im)


def s5q_bwd(p, hs_re, hs_im, dy, bb_re, bb_im, ct_re, ct_im, pw_re, pw_im_conj, dr, reverse, name):
    t = p.shape[0]
    nt = t // TM
    order = _tile_order("bwd_adj" if reverse else "fwd_adj", nt)
    nq = AW // 128
    h_off, p_off = _hs_offsets(reverse)

    def body(*refs):
        x_refs, dy_refs = refs[:nq], refs[nq:2 * nq]
        (hsr_ref, hsi_ref, bbr_ref, bbi_ref, ctr_ref, cti_ref, pr_ref, pi_ref,
         dx_ref, dar_ref, dai_ref, dbbr_ref, dbbi_ref, dcr_ref, dci_ref,
         qr_ref, qi_ref, gr_ref, gi_ref, er_ref, ei_ref, cr_ref, ci_ref) = refs[2 * nq:]
        bbr_ref, bbi_ref, ctr_ref, cti_ref, pr_ref, pi_ref = [
            r.at[0] for r in (bbr_ref, bbi_ref, ctr_ref, cti_ref, pr_ref, pi_ref)]

        @pl.when(pl.program_id(0) == 0)
        def _():
            for r in (cr_ref, ci_ref, dar_ref, dai_ref, dbbr_ref, dbbi_ref, dcr_ref, dci_ref):
                r[...] = jnp.zeros_like(r)

        xb = _load_perm(x_refs).astype(BF)
        dyb = _load_perm(dy_refs).astype(BF)
        for j in range(NBLK):
            cols = slice(j * LN, (j + 1) * LN)
            qr_ref[:, cols] = _dot(dyb[:, j * UB:(j + 1) * UB], ctr_ref[j], 1, 1)
            qi_ref[:, cols] = -_dot(dyb[:, j * UB:(j + 1) * UB], cti_ref[j], 1, 1)
        _scan2(qr_ref, qi_ref, gr_ref, gi_ref, 0, None, er_ref, ei_ref, cr_ref, ci_ref, pr_ref, pi_ref, not reverse)
        dxs = []
        for j in range(NBLK):
            cols = slice(j * LN, (j + 1) * LN)
            xj = xb[:, j * UB:(j + 1) * UB]
            dyj = dyb[:, j * UB:(j + 1) * UB]
            hpr, hpi = hsr_ref[p_off:p_off + TM, cols], hsi_ref[p_off:p_off + TM, cols]
            gr, gi = gr_ref[:, cols], gi_ref[:, cols]
            dar_ref[:, cols] += jnp.sum(gr * hpr + gi * hpi, axis=0, keepdims=True)
            dai_ref[:, cols] += jnp.sum(gi * hpr - gr * hpi, axis=0, keepdims=True)
            grb, gib = gr.astype(BF), gi.astype(BF)
            dcr_ref[j] += _dot(dyj, hsr_ref[h_off:h_off + TM, cols].astype(BF), 0, 0)
            dci_ref[j] += -_dot(dyj, hsi_ref[h_off:h_off + TM, cols].astype(BF), 0, 0)
            dbbr_ref[j] += _dot(xj, grb, 0, 0)
            dbbi_ref[j] += _dot(xj, gib, 0, 0)
            dxs.append(_dot(grb, bbr_ref[j], 1, 1) + _dot(gib, bbi_ref[j], 1, 1))
        _store_perm(dx_ref, jnp.concatenate(dxs, axis=1))

    state = lambda: pl.BlockSpec((HS_ROWS, SW), lambda i: (order(i), 0))
    blockd = lambda: _full((NBLK, UB, LN))
    xspec = lambda q: pl.BlockSpec((TM, 128), lambda i: (order(i), XS_BLK + q))
    dyspec = lambda q: pl.BlockSpec((TM, 128), lambda i: (order(i), q))
    return _call(body, name=name, grid=(nt,),
                 in_specs=[xspec(q) for q in range(nq)] + [dyspec(q) for q in range(nq)]
                 + [state(), state(), _dir_spec(dr, (NBLK, UB, LN)), _dir_spec(dr, (NBLK, UB, LN)),
                    _dir_spec(dr, (NBLK, LN, UB)), _dir_spec(dr, (NBLK, LN, UB)),
                    _dir_spec(dr, (PW_ROWS, SW)), _dir_spec(dr, (PW_ROWS, SW))],
                 out_specs=[pl.BlockSpec((nq, TM, 128), lambda i: (0, order(i), 0)), _full((1, SW)), _full((1, SW)),
                            blockd(), blockd(), blockd(), blockd()],
                 out_shape=[jax.ShapeDtypeStruct((nq, t, 128), F32), jax.ShapeDtypeStruct((1, SW), F32),
                            jax.ShapeDtypeStruct((1, SW), F32)] + [jax.ShapeDtypeStruct((NBLK, UB, LN), F32)] * 4,
                 scratch=[pltpu.VMEM((TM, SW), F32), pltpu.VMEM((TM, SW), F32),
                          pltpu.VMEM((TM, SW), F32), pltpu.VMEM((TM, SW), F32),
                          pltpu.VMEM((SCAN_G, SW), F32), pltpu.VMEM((SCAN_G, SW), F32),
                          pltpu.VMEM((1, SW), F32), pltpu.VMEM((1, SW), F32)])(
        *([p] * nq), *([dy] * nq), hs_re, hs_im, bb_re, bb_im, ct_re, ct_im, pw_re, pw_im_conj)


def s5p_dx_sum(dy, dsk, dxf, dxb, dp, name):
    t = dy.shape[0]
    nq = AW // 128

    def body(dy_ref, d_ref, f_ref, b_ref, dp_ref, o_ref):
        o_ref[...] = (dy_ref[...] * d_ref[...] + _cat_lanes(f_ref) + _cat_lanes(b_ref)).astype(BF)

    tile = pl.BlockSpec((TM, AW), lambda i: (i, 0))
    blk4 = pl.BlockSpec((nq, TM, 128), lambda i: (0, i, 0))
    return pl.pallas_call(
        body, name=name, grid=(t // TM,),
        in_specs=[tile, _full((1, AW)), blk4, blk4, pl.BlockSpec(memory_space=pl.ANY)],
        out_specs=pl.BlockSpec((TM, AW), lambda i: (i, 3)), out_shape=jax.ShapeDtypeStruct(dp.shape, dp.dtype),
        input_output_aliases={4: 0},
        compiler_params=pltpu.CompilerParams(vmem_limit_bytes=VMEM_LIMIT_BYTES))(dy, dsk, dxf, dxb, dp)


SCALE = HD ** -0.5
NHEAD_NORM = NQ + NKV


def _partner(x):
    half0 = (lax.broadcasted_iota(jnp.int32, (1, HD), 1) % 64) < 32
    return jnp.where(half0, pltpu.roll(x, HD - 32, 1), pltpu.roll(x, 32, 1))


def attn_prep(p, qg, kg, cos, sins, name):
    t = p.shape[0]

    def body(p_ref, qg_ref, kg_ref, cos_ref, sin_ref, o_ref):
        cv, sv = cos_ref[...], sin_ref[...]
        for h in range(NHEAD_NORM):
            cols = slice(h * HD, (h + 1) * HD)
            blk = p_ref[:, cols]
            r = lax.rsqrt(jnp.mean(blk * blk, axis=-1, keepdims=True) + EPS)
            xn = blk * r * (qg_ref[...] if h < NQ else kg_ref[...])
            rot = xn * cv + _partner(xn) * sv
            o_ref[:, cols] = ((rot * SCALE) if h < NQ else rot).astype(BF)
        vcols = slice(NHEAD_NORM * HD, (NHEAD_NORM + NKV) * HD)
        o_ref[:, vcols] = p_ref[:, vcols].astype(BF)

    w = (NHEAD_NORM + NKV) * HD
    tile = lambda ww: pl.BlockSpec((TM, ww), lambda i: (i, 0))
    return _call(body, name=name, grid=(t // TM,),
                 in_specs=[tile(w), _full((1, HD)), _full((1, HD)), tile(HD), tile(HD)],
                 out_specs=tile(w), out_shape=jax.ShapeDtypeStruct((t, w), BF))(p, qg, kg, cos, sins)


def attn_prep_bwd(p, dq, dk, dv, qg, kg, cos, sins, dp, name):
    t = p.shape[0]

    def body(p_ref, dq_ref, dk_ref, dv_ref, qg_ref, kg_ref, cos_ref, sin_ref, dp_ref, o_ref, dqg_ref, dkg_ref):
        @pl.when(pl.program_id(0) == 0)
        def _():
            dqg_ref[...] = jnp.zeros_like(dqg_ref)
            dkg_ref[...] = jnp.zeros_like(dkg_ref)

        cv, sv = cos_ref[...], sin_ref[...]
        for h in range(NHEAD_NORM):
            cols = slice(h * HD, (h + 1) * HD)
            blk = p_ref[:, cols]
            r = lax.rsqrt(jnp.mean(blk * blk, axis=-1, keepdims=True) + EPS)
            xh = blk * r
            if h < NQ:
                drot = dq_ref[:, cols] * SCALE
                gv, dg_ref = qg_ref[...], dqg_ref
            else:
                drot = dk_ref[:, (h - NQ) * HD:(h - NQ + 1) * HD]
                gv, dg_ref = kg_ref[...], dkg_ref
            dxn = drot * cv + _partner(drot * sv)
            dg_ref[...] += jnp.sum(dxn * xh, axis=0, keepdims=True)
            dxh = dxn * gv
            o_ref[:, cols] = (r * (dxh - xh * jnp.mean(dxh * xh, axis=-1, keepdims=True))).astype(BF)
        o_ref[:, NHEAD_NORM * HD:(NHEAD_NORM + NKV) * HD] = dv_ref[...].astype(BF)

    w = (NHEAD_NORM + NKV) * HD
    tile = lambda ww: pl.BlockSpec((TM, ww), lambda i: (i, 0))
    return pl.pallas_call(
        body, name=name, grid=(t // TM,),
        in_specs=[tile(w), tile(NQ * HD), tile(NKV * HD), tile(NKV * HD), _full((1, HD)), _full((1, HD)),
                  tile(HD), tile(HD), pl.BlockSpec(memory_space=pl.ANY)],
        out_specs=[tile(w), _full((1, HD)), _full((1, HD))],
        out_shape=[jax.ShapeDtypeStruct(dp.shape, dp.dtype), jax.ShapeDtypeStruct((1, HD), F32),
                   jax.ShapeDtypeStruct((1, HD), F32)],
        input_output_aliases={8: 0},
        compiler_params=pltpu.CompilerParams(vmem_limit_bytes=VMEM_LIMIT_BYTES))(p, dq, dk, dv, qg, kg, cos, sins, dp)


KCOL = NQ
VCOL = NQ + NKV
GCOL = (NQ + 2 * NKV)
QPK = NQ // NKV
ATT_KCHUNK = 512


def attn_fwd(qkv, p, name):
    t = qkv.shape[0]

    def body(q_ref, k_ref, v_ref, g_ref, o_ref, mix_ref, lse_ref):
        def attend(nk):
            q = q_ref[...]
            chunks = [(k0, min(k0 + 2 * ATT_KCHUNK, nk)) for k0 in range(0, nk, 2 * ATT_KCHUNK)]
            s_next = _dot(q, k_ref[chunks[0][0]:chunks[0][1], :], 1, 1)
            m = l = acc = None
            for ci, (k0, k1) in enumerate(chunks):
                s = s_next
                if ci + 1 < len(chunks):
                    s_next = _dot(q, k_ref[chunks[ci + 1][0]:chunks[ci + 1][1], :], 1, 1)
                mc = jnp.max(s, axis=-1, keepdims=True)
                m_new = mc if m is None else jnp.maximum(m, mc)
                pe = jnp.exp(s - m_new)
                lc = jnp.sum(pe, axis=-1, keepdims=True)
                pv = _dot(pe.astype(BF), v_ref[k0:k1, :])
                if m is None:
                    l, acc = lc, pv
                else:
                    alpha = jnp.exp(m - m_new)
                    l, acc = alpha * l + lc, alpha * acc + pv
                m = m_new
            o = acc / l
            gt = g_ref[...]
            o_ref[...] = o
            mix_ref[...] = (o * (gt * _sig(gt))).astype(BF)
            lse_ref[...] = jnp.broadcast_to(m + jnp.log(l), (TM, HD))

        pl.when(pl.program_id(1) == 0)(lambda: attend(NC))
        pl.when(pl.program_id(1) > 0)(lambda: attend(t))

    blk = pl.BlockSpec((TM, HD), lambda h, i: (i, h))
    return _call(body, name=name, grid=(NQ, t // TM),
                 in_specs=[blk, pl.BlockSpec((t, HD), lambda h, i: (0, KCOL + h // QPK)),
                           pl.BlockSpec((t, HD), lambda h, i: (0, VCOL + h // QPK)),
                           pl.BlockSpec((TM, HD), lambda h, i: (i, GCOL + h))],
                 out_specs=[blk, blk, blk],
                 out_shape=[jax.ShapeDtypeStruct((t, NQ * HD), F32), jax.ShapeDtypeStruct((t, NQ * HD), BF),
                            jax.ShapeDtypeStruct((t, NQ * HD), F32)])(qkv, qkv, qkv, p)


def attn_bwd(qkv, p, dmix, o, lse, name):
    t = qkv.shape[0]

    def body(q_ref, k_ref, v_ref, g_ref, dm_ref, o_ref, lse_ref, dq_ref, dg_ref, dk_ref, dv_ref):
        i = pl.program_id(2)

        @pl.when((pl.program_id(1) == 0) & (i == 0))
        def _():
            dk_ref[...] = jnp.zeros_like(dk_ref)
            dv_ref[...] = jnp.zeros_like(dv_ref)

        gt = g_ref[...]
        sg = _sig(gt)
        ov = o_ref[...]
        dmv = dm_ref[...]
        dg_ref[...] = (dmv * ov * (sg * (1.0 + gt * (1.0 - sg)))).astype(BF)
        do = dmv * (gt * sg)
        dr = jnp.sum(do * ov, axis=-1, keepdims=True)
        dob = do.astype(BF)

        def bwd(nk):
            q = q_ref[...]
            lse = lse_ref[:, 0:1]
            chunks = [slice(k0, min(k0 + ATT_KCHUNK, nk)) for k0 in range(0, nk, ATT_KCHUNK)]
            nxt = (_dot(q, k_ref[chunks[0], :], 1, 1), _dot(dob, v_ref[chunks[0], :], 1, 1))
            dq = None
            for ci, keys in enumerate(chunks):
                s, dp = nxt
                if ci + 1 < len(chunks):
                    nxt = (_dot(q, k_ref[chunks[ci + 1], :], 1, 1), _dot(dob, v_ref[chunks[ci + 1], :], 1, 1))
                pe = jnp.exp(s - lse)
                dsb = (pe * (dp - dr)).astype(BF)
                part = _dot(dsb, k_ref[keys, :])
                dq = part if dq is None else dq + part
                dv_ref[keys, :] += _dot(pe.astype(BF), dob, 0, 0)
                dk_ref[keys, :] += _dot(dsb, q, 0, 0)
            dq_ref[...] = dq

        pl.when(i == 0)(lambda: bwd(NC))
        pl.when(i > 0)(lambda: bwd(t))

    blk = pl.BlockSpec((TM, HD), lambda kv, g, i: (i, kv * QPK + g))
    acc = pl.BlockSpec((t, HD), lambda kv, g, i: (0, kv))
    return _call(body, name=name, grid=(NKV, QPK, t // TM),
                 in_specs=[blk, pl.BlockSpec((t, HD), lambda kv, g, i: (0, KCOL + kv)),
                           pl.BlockSpec((t, HD), lambda kv, g, i: (0, VCOL + kv)),
                           pl.BlockSpec((TM, HD), lambda kv, g, i: (i, GCOL + kv * QPK + g)), blk, blk, blk],
                 out_specs=[blk, pl.BlockSpec((TM, HD), lambda kv, g, i: (i, GCOL + kv * QPK + g)), acc, acc],
                 out_shape=[jax.ShapeDtypeStruct((t, NQ * HD), F32), jax.ShapeDtypeStruct((t, (GCOL + NQ) * HD), BF),
                            jax.ShapeDtypeStruct((t, NKV * HD), F32), jax.ShapeDtypeStruct((t, NKV * HD), F32)])(
        qkv, qkv, qkv, p, dmix, o, lse)


def _row_tile(rows, row_bytes, cap=2 * 1024 * 1024):
    if rows * row_bytes <= cap or rows % 8:
        return rows
    tr = rows
    while tr * row_bytes > cap and tr % 16 == 0:
        tr //= 2
    return tr


def _adamw_update(w_ref, g_ref, m_ref, v_ref, d_ref, nm_ref, nv_ref):
    gv = g_ref[...]
    m2 = ADAM_B1 * m_ref[...] + (1.0 - ADAM_B1) * gv
    v2 = ADAM_B2 * v_ref[...] + (1.0 - ADAM_B2) * (gv * gv)
    mh = m2 / (1.0 - ADAM_B1 ** ADAM_STEP)
    vh = v2 / (1.0 - ADAM_B2 ** ADAM_STEP)
    d_ref[...] = -ADAM_LR * (mh / (jnp.sqrt(vh) + ADAM_EPS) + ADAM_WD * w_ref[...])
    nm_ref[...] = m2
    nv_ref[...] = v2


def adamw_many(ws, gs, ms, vs, name):
    n = len(ws)

    def body(*refs):
        for k in range(n):
            _adamw_update(*[refs[j * n + k] for j in range(7)])

    shapes = [jax.ShapeDtypeStruct(w.shape, F32) for w in ws]
    res = _call(body, name=name, out_shape=shapes * 3)(*ws, *gs, *ms, *vs)
    return res[:n], res[n:2 * n], res[2 * n:]


def adamw(w, g, m, v, name):
    r, cdim = w.shape
    tr = _row_tile(r, 4 * max(cdim, 128))

    def body(w_ref, g_ref, m_ref, v_ref, d_ref, nm_ref, nv_ref):
        _adamw_update(w_ref, g_ref, m_ref, v_ref, d_ref, nm_ref, nv_ref)

    tile = pl.BlockSpec((tr, cdim), lambda i: (i, 0))
    sh = jax.ShapeDtypeStruct((r, cdim), F32)
    return _call(body, name=name, grid=(r // tr,), in_specs=[tile] * 4, out_specs=[tile] * 3,
                 out_shape=[sh, sh, sh])(w, g, m, v)


def sum_lead(a, name, out_dtype=F32):
    n, r, cdim = a.shape
    tr = _row_tile(r, 4 * n * max(cdim, 128))

    def body(a_ref, o_ref):
        acc = a_ref[0].astype(F32)
        for k in range(1, n):
            acc = acc + a_ref[k].astype(F32)
        o_ref[...] = acc.astype(o_ref.dtype)

    return _call(body, name=name, grid=(r // tr,),
                 in_specs=[pl.BlockSpec((n, tr, cdim), lambda i: (0, i, 0))],
                 out_specs=pl.BlockSpec((tr, cdim), lambda i: (i, 0)),
                 out_shape=jax.ShapeDtypeStruct((r, cdim), out_dtype))(a)


_FLIPS = {"xy": [(1, 0, 0), (0, 1, 0), (1, 1, 0)], "c": [(0, 0, 1)],
          "all": [(0, 0, 1), (0, 1, 0), (0, 1, 1), (1, 0, 0), (1, 0, 1), (1, 1, 0), (1, 1, 1)]}
_GROUP_SIZE = {"xy": 4, "c": 2, "all": 8}


def _group_index(group, x, y, c):
    return {"xy": 2 * x + y, "c": c, "all": 4 * x + 2 * y + c}[group]


def exchange(items, name):
    plan = []
    for arr, group, kind in items:
        chunk = arr.shape if kind == "gather" else arr.shape[1:]
        plan.append((group, kind, chunk))
    ncopy = sum(len(_FLIPS[g]) for g, _, _ in plan)
    nitem = len(plan)

    def body(*refs):
        srcs, dsts = refs[:nitem], refs[nitem:2 * nitem]
        send_sems, recv_sems, local_sems = refs[2 * nitem:]
        x, y, c = lax.axis_index("x"), lax.axis_index("y"), lax.axis_index("c")
        sends, recvs, locals_ = [], [], []
        n = 0
        for k, (group, kind, _) in enumerate(plan):
            me = _group_index(group, x, y, c)
            own = srcs[k] if kind == "gather" else srcs[k].at[me]
            locals_.append(pltpu.make_async_copy(own, dsts[k].at[me], local_sems.at[k]))
            for fx, fy, fc in _FLIPS[group]:
                px, py, pc = (1 - x if fx else x), (1 - y if fy else y), (1 - c if fc else c)
                peer = _group_index(group, px, py, pc)
                src = srcs[k] if kind == "gather" else srcs[k].at[peer]
                sends.append(pltpu.make_async_remote_copy(
                    src_ref=src, dst_ref=dsts[k].at[me], send_sem=send_sems.at[n], recv_sem=recv_sems.at[n],
                    device_id=(px, py, pc), device_id_type=MESH))
                recvs.append(pltpu.make_async_remote_copy(
                    src_ref=src, dst_ref=dsts[k].at[peer], send_sem=send_sems.at[n], recv_sem=recv_sems.at[n],
                    device_id=(px, py, pc), device_id_type=MESH))
                n += 1
        for cp in locals_ + sends:
            cp.start()
        for cp in recvs:
            cp.wait_recv()
        for cp in sends:
            cp.wait_send()
        for cp in locals_:
            cp.wait()

    anyspec = pl.BlockSpec(memory_space=pl.ANY)
    outs = [jax.ShapeDtypeStruct((_GROUP_SIZE[g],) + tuple(chunk), arr.dtype)
            for (arr, _, _), (g, _, chunk) in zip(items, plan)]
    res = pl.pallas_call(
        body, name=name, out_shape=outs, in_specs=[anyspec] * nitem, out_specs=[anyspec] * nitem,
        scratch_shapes=[pltpu.SemaphoreType.DMA((ncopy,)), pltpu.SemaphoreType.DMA((ncopy,)),
                        pltpu.SemaphoreType.DMA((nitem,))],
        compiler_params=pltpu.CompilerParams(has_side_effects=True))(*[a for a, _, _ in items])
    return list(res)


D2D_PIECES = 4


def d2d(items, name):
    n = len(items)
    swaps = [k for k, (_, kind) in enumerate(items) if kind == "swap"]

    def pieces_of(rows):
        npc = D2D_PIECES if rows % (8 * D2D_PIECES) == 0 else 1
        return npc, rows // npc

    ncopy = sum(pieces_of(a.shape[0] if kind == "gather" else a.shape[1])[0] for a, kind in items)

    def body(*refs):
        srcs, outs = refs[:n], refs[n:2 * n]
        stages = dict(zip(swaps, refs[2 * n:2 * n + len(swaps)]))
        send_sems, recv_sems, local_sems = refs[2 * n + len(swaps):]
        x, y, c = lax.axis_index("x"), lax.axis_index("y"), lax.axis_index("c")
        sib = (x, y, 1 - c)

        def remote(src, dst, q):
            return pltpu.make_async_remote_copy(src_ref=src, dst_ref=dst, send_sem=send_sems.at[q],
                                                recv_sem=recv_sems.at[q], device_id=sib, device_id_type=MESH)

        copies = []
        q = 0
        for k, (arr, kind) in enumerate(items):
            npc, pr = pieces_of(arr.shape[0] if kind == "gather" else arr.shape[1])
            for pc in range(npc):
                rs = pl.ds(pc * pr, pr)
                if kind == "gather":
                    mine, theirs = outs[k].at[c, rs], outs[k].at[1 - c, rs]
                    copies.append((pltpu.make_async_copy(srcs[k].at[rs], mine, local_sems.at[q]),
                                   remote(mine, mine, q), remote(theirs, theirs, q)))
                else:
                    stage, land = stages[k].at[rs], outs[k].at[rs]
                    copies.append((pltpu.make_async_copy(srcs[k].at[1 - c, rs], stage, local_sems.at[q]),
                                   remote(stage, land, q), remote(stage, land, q)))
                q += 1
        for loc, _, _ in copies:
            loc.start()
        for loc, send, _ in copies:
            loc.wait()
            send.start()
        for _, _, recv in copies:
            recv.wait_recv()
        for _, send, _ in copies:
            send.wait_send()

    outs = [jax.ShapeDtypeStruct((2,) + a.shape if kind == "gather" else a.shape[1:], a.dtype) for a, kind in items]
    res = pl.pallas_call(
        body, name=name, out_shape=outs, in_specs=[pl.BlockSpec(memory_space=pl.ANY)] * n,
        out_specs=[pl.BlockSpec(memory_space=pltpu.VMEM)] * n,
        scratch_shapes=[pltpu.VMEM(items[k][0].shape[1:], items[k][0].dtype) for k in swaps]
        + [pltpu.SemaphoreType.DMA((ncopy,)), pltpu.SemaphoreType.DMA((ncopy,)), pltpu.SemaphoreType.DMA((ncopy,))],
        compiler_params=pltpu.CompilerParams(has_side_effects=True, vmem_limit_bytes=VMEM_LIMIT_BYTES))(
        *[a for a, _ in items])
    return list(res)


def sum_own(pair, got, name, out_dtype=F32):
    _, r, cdim = pair.shape
    tr = _row_tile(r, 4 * 2 * max(cdim, 128))

    def body(c_ref, p_ref, g_ref, o_ref):
        o_ref[...] = (p_ref[0] + g_ref[...]).astype(o_ref.dtype)

    me = lax.axis_index("c").astype(jnp.int32).reshape(1)
    return pl.pallas_call(
        body, name=name, out_shape=jax.ShapeDtypeStruct((r, cdim), out_dtype),
        grid_spec=pltpu.PrefetchScalarGridSpec(
            num_scalar_prefetch=1, grid=(r // tr,),
            in_specs=[pl.BlockSpec((1, tr, cdim), lambda i, c_ref: (c_ref[0], i, 0)),
                      pl.BlockSpec((tr, cdim), lambda i, c_ref: (i, 0))],
            out_specs=pl.BlockSpec((tr, cdim), lambda i, c_ref: (i, 0))),
        compiler_params=pltpu.CompilerParams(vmem_limit_bytes=VMEM_LIMIT_BYTES))(me, pair, got)


_SMALL = ["c_ctx", "norm_g", "b_mod", "gm_v_g", "gm_w_s", "gm_b_s", "s5_lam_re", "s5_lam_im", "s5_log_dt",
          "s5_b_re", "s5_b_im", "s5_c_re", "s5_c_im", "s5_d", "s5_b_glu", "q_norm_g", "k_norm_g", "final_g"]
_BIG = ["we_in", "we_out", "s5_w_glu", "wo_in", "wo_out"]
_WEIGHTS = ["c_ctx", "norm_g", "w_mod", "b_mod", "we_in", "we_out", "gm_v_g", "gm_w_s", "gm_b_s", "s5_lam_re",
            "s5_lam_im", "s5_log_dt", "s5_b_re", "s5_b_im", "s5_c_re", "s5_c_im", "s5_d", "s5_w_glu", "s5_b_glu",
            "wo_in", "wo_out", "q_norm_g", "k_norm_g", "final_g"]
_SMALL_ALIGN = 8 * 8 * 128


def _rope_tables(n_lat):
    rows = n_lat // GRID_W
    row = jnp.repeat(jnp.arange(rows), GRID_W)
    col = jnp.tile(jnp.arange(GRID_W), rows)
    freqs = ROPE_THETA ** (-jnp.arange(HD // 4, dtype=F32) / (HD // 4))
    ar, ac = row[:, None] * freqs, col[:, None] * freqs
    cos = jnp.concatenate([jnp.cos(ar), jnp.cos(ar), jnp.cos(ac), jnp.cos(ac)], axis=1)
    sins = jnp.concatenate([-jnp.sin(ar), jnp.sin(ar), -jnp.sin(ac), jnp.sin(ac)], axis=1)
    cos = jnp.concatenate([jnp.ones((NC, HD), F32), cos], axis=0)
    sins = jnp.concatenate([jnp.zeros((NC, HD), F32), sins], axis=0)
    return cos, sins


def _block_diag(v, transpose):
    gpb = SG // NBLK
    v = v.reshape(2, NBLK, gpb, SH, SP)
    eye = jnp.eye(gpb, dtype=v.dtype)
    if transpose:
        return jnp.einsum("djahp,ab->djapbh", v, eye).reshape(2, NBLK, LN, UB)
    return jnp.einsum("djahp,ab->djahbp", v, eye).reshape(2, NBLK, UB, LN)


def _diag_blocks(m):
    gpb = SG // NBLK
    return jnp.einsum("jahap->jahp", m.reshape(NBLK, gpb, SH, gpb, SP)).reshape(SG, SH, SP)


def _view2d(a):
    if a.ndim == 1:
        return a.reshape(1, -1)
    if a.shape[-1] < 64 and a.size % 1024 == 0:
        return a.reshape(-1, 1024)
    return a.reshape(-1, a.shape[-1])


def kernel(x, c, ctx, c_ctx, norm_g, w_mod, b_mod, we_in, we_out, gm_v_g, gm_w_s, gm_b_s, s5_lam_re, s5_lam_im, s5_log_dt, s5_b_re, s5_b_im, s5_c_re, s5_c_im, s5_d, s5_w_glu, s5_b_glu, wo_in, wo_out, q_norm_g, k_norm_g, final_g, loss_target, m_c_ctx, m_norm_g, m_w_mod, m_b_mod, m_we_in, m_we_out, m_gm_v_g, m_gm_w_s, m_gm_b_s, m_s5_lam_re, m_s5_lam_im, m_s5_log_dt, m_s5_b_re, m_s5_b_im, m_s5_c_re, m_s5_c_im, m_s5_d, m_s5_w_glu, m_s5_b_glu, m_wo_in, m_wo_out, m_q_norm_g, m_k_norm_g, m_final_g, v_c_ctx, v_norm_g, v_w_mod, v_b_mod, v_we_in, v_we_out, v_gm_v_g, v_gm_w_s, v_gm_b_s, v_s5_lam_re, v_s5_lam_im, v_s5_log_dt, v_s5_b_re, v_s5_b_im, v_s5_c_re, v_s5_c_im, v_s5_d, v_s5_w_glu, v_s5_b_glu, v_wo_in, v_wo_out, v_q_norm_g, v_k_norm_g, v_final_g):
    weights = dict(c_ctx=c_ctx, norm_g=norm_g, w_mod=w_mod, b_mod=b_mod, we_in=we_in, we_out=we_out, gm_v_g=gm_v_g,
                   gm_w_s=gm_w_s, gm_b_s=gm_b_s, s5_lam_re=s5_lam_re, s5_lam_im=s5_lam_im, s5_log_dt=s5_log_dt,
                   s5_b_re=s5_b_re, s5_b_im=s5_b_im, s5_c_re=s5_c_re, s5_c_im=s5_c_im, s5_d=s5_d, s5_w_glu=s5_w_glu,
                   s5_b_glu=s5_b_glu, wo_in=wo_in, wo_out=wo_out, q_norm_g=q_norm_g, k_norm_g=k_norm_g,
                   final_g=final_g)
    mom_m = dict(c_ctx=m_c_ctx, norm_g=m_norm_g, w_mod=m_w_mod, b_mod=m_b_mod, we_in=m_we_in, we_out=m_we_out,
                 gm_v_g=m_gm_v_g, gm_w_s=m_gm_w_s, gm_b_s=m_gm_b_s, s5_lam_re=m_s5_lam_re, s5_lam_im=m_s5_lam_im,
                 s5_log_dt=m_s5_log_dt, s5_b_re=m_s5_b_re, s5_b_im=m_s5_b_im, s5_c_re=m_s5_c_re, s5_c_im=m_s5_c_im,
                 s5_d=m_s5_d, s5_w_glu=m_s5_w_glu, s5_b_glu=m_s5_b_glu, wo_in=m_wo_in, wo_out=m_wo_out,
                 q_norm_g=m_q_norm_g, k_norm_g=m_k_norm_g, final_g=m_final_g)
    mom_v = dict(c_ctx=v_c_ctx, norm_g=v_norm_g, w_mod=v_w_mod, b_mod=v_b_mod, we_in=v_we_in, we_out=v_we_out,
                 gm_v_g=v_gm_v_g, gm_w_s=v_gm_w_s, gm_b_s=v_gm_b_s, s5_lam_re=v_s5_lam_re, s5_lam_im=v_s5_lam_im,
                 s5_log_dt=v_s5_log_dt, s5_b_re=v_s5_b_re, s5_b_im=v_s5_b_im, s5_c_re=v_s5_c_re, s5_c_im=v_s5_c_im,
                 s5_d=v_s5_d, s5_w_glu=v_s5_w_glu, s5_b_glu=v_s5_b_glu, wo_in=v_wo_in, wo_out=v_wo_out,
                 q_norm_g=v_q_norm_g, k_norm_g=v_k_norm_g, final_g=v_final_g)

    ixy = 2 * lax.axis_index("x") + lax.axis_index("y")
    n_lat = x.shape[1]
    nl = norm_g.shape[0]
    nmod = w_mod.shape[2]
    xin = (ctx[0], x[0])

    ic = lax.axis_index("c")
    mine = [lax.dynamic_index_in_dim(weights[n], ic, 0, keepdims=False).astype(BF) for n in _BIG]
    got = exchange([(m_, "xy", "gather") for m_ in mine] + [(c, "xy", "gather")], "gather_weights")
    both = d2d([(g_.reshape(-1, g_.shape[-1]), "gather") for g_ in got[:len(_BIG)]], "swap_weights")
    both = [b_.reshape((2,) + g_.shape) for b_, g_ in zip(both, got)]
    wein = [(both[0], l) for l in range(2)]
    weout = [(both[1].reshape(2, 1, D, D), l) for l in range(2)]
    wglu = [(both[2].reshape(2, AW, AW), l) for l in range(2)]
    woin = [(both[3], l) for l in range(2)]
    woout = [(both[4].reshape(2, 1, D, D), l) for l in range(2)]
    c_group = got[len(_BIG)].reshape(4, D)

    cond = jnp.concatenate([c_group, jnp.broadcast_to(c_ctx.reshape(1, D), (4, D))], axis=0)
    b_shard = lax.dynamic_slice(b_mod, (0, ixy * nmod), (nl, nmod)).reshape(nl, 1, nmod)
    mpart = ada_fwd(cond, w_mod, b_shard)
    m_lat, m_ctx = exchange([(jnp.transpose(mpart[:, 0:4], (1, 0, 2)), "xy", "scatter"),
                             (mpart[:, 4], "xy", "gather")], "exchange_mod")
    m_lat = jnp.transpose(m_lat, (1, 0, 2)).reshape(nl, 3, D)
    m_ctx = jnp.transpose(m_ctx, (1, 0, 2)).reshape(nl, 3, D)
    mods = [jnp.stack([m_ctx[l], m_lat[l]], axis=0) for l in range(nl)]

    loss_part, dx, g, d_norm_g, d_mod_lat, d_mod_ctx, d_final_g = _local_step(
        xin, loss_target[0], mods, wein, weout, wglu, woin, woout, weights)
    grad_x = dx.reshape(1, n_lat, D)

    d_mod_lat, d_mod_ctx = jnp.stack(d_mod_lat), jnp.stack(d_mod_ctx)
    dm_send = jnp.stack([d_mod_lat.reshape(nl, 4, nmod), d_mod_ctx.reshape(nl, 4, nmod)])
    (dm_got,) = exchange([(jnp.transpose(dm_send, (2, 0, 1, 3)), "xy", "scatter")], "exchange_dmod")
    dm_rows = jnp.concatenate([dm_got[:, 0], dm_got[:, 1]], axis=0)
    gw_mod, d_cctx = ada_bwd(cond, jnp.transpose(dm_rows, (1, 0, 2)), w_mod)
    g_small = dict(c_ctx=d_cctx.reshape(D), norm_g=jnp.stack(d_norm_g), b_mod=add2(d_mod_lat, d_mod_ctx, "add_dbmod"),
                   final_g=d_final_g.reshape(D))
    for name in _SMALL:
        if name not in g_small:
            g_small[name] = jnp.stack(g[name])

    flat = jnp.concatenate([g_small[n].reshape(-1) for n in _SMALL])
    nflat = flat.shape[0]
    npad = -(-nflat // _SMALL_ALIGN) * _SMALL_ALIGN
    flat = jnp.concatenate([flat, jnp.zeros((npad - nflat,), F32)]).reshape(8, npad // (8 * 128), 128)
    pairs = [gw_mod.reshape(2, nl // 2 * D, nmod)]
    for name in _BIG:
        st = jnp.stack(g[name]) if isinstance(g[name], list) else g[name]
        pairs.append(st.reshape(2, -1, st.shape[-1]))
    got_a = d2d([(pairs[k], "swap") for k in (1, 2, 3)], "reduce_chip_a")
    got_b = d2d([(pairs[k], "swap") for k in (0, 4, 5)], "reduce_chip_b")
    theirs = [got_b[0]] + got_a + got_b[1:]
    chip = [sum_own(pairs[k], theirs[k], f"sum_chip{k}", F32 if k == 0 else BF) for k in range(len(pairs))]
    parts = exchange([(flat, "all", "scatter")]
                     + [(s_.reshape(4, s_.shape[0] // 4, s_.shape[1]), "xy", "scatter") for s_ in chip[1:]],
                     "reduce_scatter")
    sums = [sum_lead(pt, f"sum_shard{k}") for k, pt in enumerate(parts)]
    full = d2d([(sums[0], "gather"), (chip[0], "gather")] + [(s_, "gather") for s_ in sums[1:]], "all_gather")
    (flat_full,) = exchange([(full[0], "xy", "gather")], "gather_small")
    full = [flat_full] + full[1:]
    flat = full[0].reshape(-1)
    grads = {}
    off = 0
    for name in _SMALL:
        sz = weights[name].size
        grads[name] = flat[off:off + sz].reshape(weights[name].shape)
        off += sz
    grads["w_mod"] = full[1].reshape(w_mod.shape)
    for k, name in enumerate(_BIG):
        grads[name] = full[2 + k].reshape(weights[name].shape)

    delta, new_m, new_v = {}, {}, {}
    views = [_view2d(weights[n]) for n in _SMALL]
    ds, nms, nvs = adamw_many(views, [grads[n].reshape(w2.shape) for n, w2 in zip(_SMALL, views)],
                              [mom_m[n].reshape(w2.shape) for n, w2 in zip(_SMALL, views)],
                              [mom_v[n].reshape(w2.shape) for n, w2 in zip(_SMALL, views)], "adamw_small")
    for n, d2, m2, v2 in zip(_SMALL, ds, nms, nvs):
        shp = weights[n].shape
        delta[n], new_m[n], new_v[n] = d2.reshape(shp), m2.reshape(shp), v2.reshape(shp)
    for name in ["w_mod"] + _BIG:
        w2 = _view2d(weights[name])
        d2, m2, v2 = adamw(w2, grads[name].reshape(w2.shape), mom_m[name].reshape(w2.shape),
                           mom_v[name].reshape(w2.shape), f"adamw_{name}")
        shp = weights[name].shape
        delta[name], new_m[name], new_v[name] = d2.reshape(shp), m2.reshape(shp), v2.reshape(shp)

    loss = lax.psum(loss_part[0, 0], ("x", "y", "c"))
    return (loss, grad_x, *[grads[n] for n in _WEIGHTS], *[delta[n] for n in _WEIGHTS],
            *[new_m[n] for n in _WEIGHTS], *[new_v[n] for n in _WEIGHTS])


def _local_step(xin, target, mods, wein, weout, wglu, woin, woout, w):
    norm_g, gm_v_g, gm_w_s, gm_b_s = w["norm_g"], w["gm_v_g"], w["gm_w_s"], w["gm_b_s"]
    s5_lam_re, s5_lam_im, s5_log_dt = w["s5_lam_re"], w["s5_lam_im"], w["s5_log_dt"]
    s5_b_re, s5_b_im, s5_c_re, s5_c_im = w["s5_b_re"], w["s5_b_im"], w["s5_c_re"], w["s5_c_im"]
    s5_d, s5_b_glu, q_norm_g, k_norm_g, final_g = w["s5_d"], w["s5_b_glu"], w["q_norm_g"], w["k_norm_g"], w["final_g"]
    nl = norm_g.shape[0]
    n_lat = xin[1].shape[0]

    cos, sins = _rope_tables(n_lat)

    s5p = []
    for i in range(2):
        lam_l = (s5_lam_re[i].reshape(2, SW), s5_lam_im[i].reshape(2, SW),
                 jnp.repeat(s5_log_dt[i], SP, axis=1))
        lam_r = (jnp.repeat(s5_lam_re[i].reshape(2 * SG, SP), SH, axis=0),
                 jnp.repeat(s5_lam_im[i].reshape(2 * SG, SP), SH, axis=0),
                 jnp.repeat(s5_log_dt[i].reshape(2 * SG, 1), SH, axis=0))
        b_r = (jnp.transpose(s5_b_re[i], (0, 1, 3, 2)).reshape(2 * SG * SH, SP),
               jnp.transpose(s5_b_im[i], (0, 1, 3, 2)).reshape(2 * SG * SH, SP))
        pw_re, pw_im, bbr, bbi = s5_disc(*lam_l, *lam_r, *b_r)
        s5p.append(dict(
            lam_r=lam_r, b_r=b_r, pw_re=pw_re, pw_im=pw_im, pw_im_conj=-pw_im,
            bb_re=_block_diag(bbr.reshape(2, SG, SH, SP), False).astype(BF),
            bb_im=_block_diag(bbi.reshape(2, SG, SH, SP), False).astype(BF),
            ct_re=_block_diag(s5_c_re[i], True).astype(BF), ct_im=_block_diag(s5_c_im[i], True).astype(BF)))

    saved = []
    xcur = xin
    h = pro_fwd(xin[0], xin[1], norm_g[0].reshape(1, D), mods[0], "pro_fwd0")
    for l in range(nl):
        i = l // 2
        sv = dict(x=xcur, h=h)
        if l % 2 == 0:
            p = mm_nn(h, wein[i], f"in_proj{l}")
            sp = s5p[i]
            for dr, rev in ((0, False), (1, True)):
                sv[f"y{dr}"], sv[f"hpr{dr}"], sv[f"hpi{dr}"] = s5q_fwd(
                    p, sp["bb_re"], sp["bb_im"], sp["ct_re"], sp["ct_im"], sp["pw_re"], sp["pw_im"], dr, rev,
                    f"s5_fwd{l}_{dr}")
            mix = mix_fwd(p, sv["y0"], sv["y1"], gm_v_g[i].reshape(1, AW), gm_w_s[i].astype(BF),
                          gm_b_s[i].reshape(NGRP, CHUNK, 1), s5_d[i].reshape(1, AW), wglu[i],
                          s5_b_glu[i].reshape(1, AW), f"mix_fwd{l}")
            o = mm_nn(mix, weout[i], f"out_proj{l}")
        else:
            p = mm_nn(h, woin[i], f"in_proj{l}")
            sv["qkv"] = attn_prep(p, q_norm_g[i].reshape(1, HD), k_norm_g[i].reshape(1, HD), cos, sins, f"attn_prep{l}")
            sv["o_att"], mix, sv["lse"] = attn_fwd(sv["qkv"], p, f"attn_fwd{l}")
            o = mm_nn(mix, woout[i], f"out_proj{l}")
        sv.update(p=p, mix=mix, o=o)
        saved.append(sv)
        if l < nl - 1:
            xcur, h = res_pro_fwd(xcur, o, mods[l], True, norm_g[l + 1].reshape(1, D), mods[l + 1], f"res_pro_fwd{l}")
        else:
            xcur = res_fwd(xcur, o, mods[l], False, f"res_fwd{l}")

    loss_part, dx, d_final_g, do, dgt = final_loss(xcur, target, final_g.reshape(1, D), saved[-1]["o"], mods[-1])

    g = {}
    gbuf = {}
    d_norm_g, d_mod_lat, d_mod_ctx = [None] * nl, [None] * nl, [None] * nl
    for name in ("s5_w_glu", "gm_v_g", "gm_w_s", "gm_b_s", "s5_lam_re", "s5_lam_im",
                 "s5_log_dt", "s5_b_re", "s5_b_im", "s5_c_re", "s5_c_im", "s5_d", "s5_b_glu", "q_norm_g", "k_norm_g"):
        g[name] = [None, None]
    for l in reversed(range(nl)):
        i = l // 2
        sv = saved[l]
        w_out = weout[i] if l % 2 == 0 else woout[i]
        dmix = mm_nt(do, w_out, f"out_dgrad{l}")
        out_name, in_name = ("we_out", "we_in") if l % 2 == 0 else ("wo_out", "wo_in")
        gbuf[out_name] = mm_tn(sv["mix"], do, 1, f"out_wgrad{l}", slot=i, into=gbuf.get(out_name))
        if l % 2 == 0:
            sp = s5p[i]
            (dp, dy, g["gm_w_s"][i], dbs, dvg, dd, g["s5_w_glu"][i], dbg) = mix_bwd(
                sv["p"], sv["y0"], sv["y1"], dmix, gm_v_g[i].reshape(1, AW), gm_w_s[i].astype(BF),
                gm_b_s[i].reshape(NGRP, CHUNK, 1), s5_d[i].reshape(1, AW), wglu[i], s5_b_glu[i].reshape(1, AW),
                f"mix_bwd{l}")
            g["gm_b_s"][i], g["gm_v_g"][i] = dbs.reshape(NGRP, CHUNK), dvg.reshape(AW)
            g["s5_d"][i], g["s5_b_glu"][i] = dd.reshape(AW), dbg.reshape(AW)
            g["s5_w_glu"][i] = g["s5_w_glu"][i].reshape(4, AW // 4, AW)
            dxd, das_r, das_i, dbbs_r, dbbs_i, dcs_r, dcs_i = [], [], [], [], [], [], []
            for dr, rev in ((0, False), (1, True)):
                dxs_d, da_r, da_i, dbb_r, dbb_i, dc_r, dc_i = s5q_bwd(
                    sv["p"], sv[f"hpr{dr}"], sv[f"hpi{dr}"], dy, sp["bb_re"], sp["bb_im"], sp["ct_re"], sp["ct_im"],
                    sp["pw_re"], sp["pw_im_conj"], dr, rev, f"s5_bwd{l}_{dr}")
                dxd.append(dxs_d)
                das_r.append(jnp.repeat(da_r.reshape(SG, SP), SH, axis=0))
                das_i.append(jnp.repeat(da_i.reshape(SG, SP), SH, axis=0))
                dbbs_r.append(_diag_blocks(dbb_r).reshape(SG * SH, SP))
                dbbs_i.append(_diag_blocks(dbb_i).reshape(SG * SH, SP))
                dcs_r.append(_diag_blocks(dc_r))
                dcs_i.append(_diag_blocks(dc_i))
            cat = lambda parts: jnp.concatenate(parts, axis=0)
            dlr, dli, dldt, dbr, dbi = s5_param_bwd(*sp["lam_r"], *sp["b_r"], cat(das_r), cat(das_i),
                                                    cat(dbbs_r), cat(dbbs_i))
            g["s5_lam_re"][i], g["s5_lam_im"][i] = dlr.reshape(2, SG, SP), dli.reshape(2, SG, SP)
            g["s5_log_dt"][i] = dldt.reshape(2, SG)
            g["s5_b_re"][i] = jnp.transpose(dbr.reshape(2, SG, SH, SP), (0, 1, 3, 2))
            g["s5_b_im"][i] = jnp.transpose(dbi.reshape(2, SG, SH, SP), (0, 1, 3, 2))
            g["s5_c_re"][i], g["s5_c_im"][i] = jnp.stack(dcs_r), jnp.stack(dcs_i)
            dp = s5p_dx_sum(dy, s5_d[i].reshape(1, AW), dxd[0], dxd[1], dp, f"s5_dx_sum{l}")
            w_in = wein[i]
        else:
            dq, dp, dk, dv = attn_bwd(sv["qkv"], sv["p"], dmix, sv["o_att"], sv["lse"], f"attn_bwd{l}")
            dp, dqg, dkg = attn_prep_bwd(sv["p"], dq, dk, dv, q_norm_g[i].reshape(1, HD),
                                         k_norm_g[i].reshape(1, HD), cos, sins, dp, f"attn_prep_bwd{l}")
            g["q_norm_g"][i], g["k_norm_g"][i] = dqg.reshape(HD), dkg.reshape(HD)
            w_in = woin[i]
        dh = mm_nt(dp, w_in, f"in_dgrad{l}")
        gbuf[in_name] = mm_tn(sv["h"], dp, 4, f"in_wgrad{l}", slot=i, into=gbuf.get(in_name))
        dgt_l = dgt
        if l > 0:
            dx, dmod2, dng, do, dgt = pro_res_bwd(sv["x"], dh, dx, norm_g[l].reshape(1, D), mods[l],
                                                  saved[l - 1]["o"], mods[l - 1], f"pro_res_bwd{l}")
        else:
            dx, dmod2, dng = pro_bwd(xin[0], xin[1], dh, dx, norm_g[l].reshape(1, D), mods[l], f"pro_bwd{l}")
        d_norm_g[l] = dng.reshape(D)
        d_mod_ctx[l] = jnp.concatenate([dmod2[0, 0], dmod2[0, 1], dgt_l[0]])
        d_mod_lat[l] = jnp.concatenate([dmod2[1, 0], dmod2[1, 1], dgt_l[1]])
    g.update(gbuf)
    return loss_part, dx, g, d_norm_g, d_mod_lat, d_mod_ctx, d_final_g
```

```python
import math

import numpy as np
import jax
import jax.numpy as jnp
from jax import lax
from jax.experimental import pallas as pl
from jax.experimental.pallas import tpu as pltpu

F32 = jnp.float32
BF = jnp.bfloat16
MESH = pl.DeviceIdType.MESH

D = 1024
NC = 256
SEQ = 4096
GRID_W = 64
TM = 256
CHUNK = 128
EPS = 1e-6
HD = 128
NQ = 8
NKV = 2
ROPE_THETA = 10000.0
SG = 32
SP = 64
SH = 16
SW = SG * SP
GELU_K = math.sqrt(2.0 / math.pi)
GELU_C = 0.044715
VMEM_LIMIT_BYTES = 56 * 1024 * 1024

ADAM_LR = 0.001
ADAM_B1 = 0.9
ADAM_B2 = 0.999
ADAM_EPS = 1e-08
ADAM_WD = 0.01
ADAM_STEP = 10


def _call(body, *, name, out_shape, grid=None, in_specs=None, out_specs=None, scratch=()):
    kw = {}
    if grid is not None:
        kw["grid"] = grid
    if in_specs is not None:
        kw["in_specs"] = in_specs
    if out_specs is not None:
        kw["out_specs"] = out_specs
    return pl.pallas_call(
        body, name=name, out_shape=out_shape, scratch_shapes=list(scratch),
        compiler_params=pltpu.CompilerParams(vmem_limit_bytes=VMEM_LIMIT_BYTES), **kw)


def _dot(a, b, ca=1, cb=0):
    return lax.dot_general(a, b, (((ca,), (cb,)), ((), ())), preferred_element_type=F32)


def _sig(x):
    return 1.0 / (1.0 + jnp.exp(-x))


def _full(shape):
    n = len(shape)
    return pl.BlockSpec(shape, lambda *_: (0,) * n)


def _mm_rows(t):
    for rows in (1088, 1024, 768, 512, 256):
        if t % rows == 0:
            return rows
    raise ValueError(t)


def _layer_of(w):
    return w if isinstance(w, tuple) else (w[None], 0)


def mm_nn(a, w, name, out_dtype=F32):
    w4, layer = _layer_of(w)
    t, k = a.shape
    _, j, _, nb = w4.shape
    tr = _mm_rows(t)

    def body(a_ref, w_ref, o_ref):
        o_ref[...] = _dot(a_ref[...], w_ref[0, 0]).astype(o_ref.dtype)

    return _call(body, name=name, grid=(j, t // tr),
                 in_specs=[pl.BlockSpec((tr, k), lambda jj, i: (i, 0)),
                           pl.BlockSpec((1, 1, k, nb), lambda jj, i: (layer, jj, 0, 0))],
                 out_specs=pl.BlockSpec((tr, nb), lambda jj, i: (i, jj)),
                 out_shape=jax.ShapeDtypeStruct((t, j * nb), out_dtype))(a, w4)


def mm_nt(a, w, name, out_dtype=F32):
    w4, layer = _layer_of(w)
    t, _ = a.shape
    _, j, k, nb = w4.shape
    tr = _mm_rows(t)

    def body(a_ref, w_ref, o_ref):
        acc = _dot(a_ref[:, 0:nb], w_ref[0, 0], 1, 1)
        for jj in range(1, j):
            acc = acc + _dot(a_ref[:, jj * nb:(jj + 1) * nb], w_ref[0, jj], 1, 1)
        o_ref[...] = acc.astype(o_ref.dtype)

    return _call(body, name=name, grid=(t // tr,),
                 in_specs=[pl.BlockSpec((tr, j * nb), lambda i: (i, 0)),
                           pl.BlockSpec((1, j, k, nb), lambda i: (layer, 0, 0, 0))],
                 out_specs=pl.BlockSpec((tr, k), lambda i: (i, 0)),
                 out_shape=jax.ShapeDtypeStruct((t, k), out_dtype))(a, w4)


def mm_tn(a, b, j, name, slot=0, into=None):
    t, m = a.shape
    nb = b.shape[1] // j
    tr = _mm_rows(t)

    def body(a_ref, b_ref, *rest):
        o_ref = rest[-1]

        @pl.when(pl.program_id(1) == 0)
        def _():
            o_ref[...] = jnp.zeros_like(o_ref)
        o_ref[0, 0] += _dot(a_ref[...], b_ref[...], 0, 0)

    in_specs = [pl.BlockSpec((tr, m), lambda jj, i: (i, 0)), pl.BlockSpec((tr, nb), lambda jj, i: (i, jj))]
    args = [a, b]
    alias = {}
    if into is not None:
        in_specs.append(pl.BlockSpec(memory_space=pl.ANY))
        args.append(into)
        alias = {2: 0}
    return pl.pallas_call(
        body, name=name, grid=(j, t // tr), in_specs=in_specs,
        out_specs=pl.BlockSpec((1, 1, m, nb), lambda jj, i: (slot, jj, 0, 0)),
        out_shape=jax.ShapeDtypeStruct((2, j, m, nb), F32), input_output_aliases=alias,
        compiler_params=pltpu.CompilerParams(vmem_limit_bytes=VMEM_LIMIT_BYTES))(*args)


def _mod_rows(mod_ref, i):
    ctx = i == 0
    sh = jnp.where(ctx, mod_ref[0, 0:1, :], mod_ref[1, 0:1, :])
    sc = jnp.where(ctx, mod_ref[0, 1:2, :], mod_ref[1, 1:2, :])
    gt = jnp.where(ctx, mod_ref[0, 2:3, :], mod_ref[1, 2:3, :])
    return sh, sc, gt


def _split_specs():
    return [pl.BlockSpec((TM, D), lambda i: (0, 0)), pl.BlockSpec((TM, D), lambda i: (jnp.maximum(i - 1, 0), 0))]


def _split_tile(c_ref, l_ref, i):
    return jnp.where(i == 0, c_ref[...], l_ref[...])


def pro_fwd(ctx, lat, g, mod, name):
    t = ctx.shape[0] + lat.shape[0]

    def body(c_ref, l_ref, g_ref, mod_ref, h_ref):
        i = pl.program_id(0)
        sh, sc, _ = _mod_rows(mod_ref, i)
        xv = _split_tile(c_ref, l_ref, i)
        r = lax.rsqrt(jnp.mean(xv * xv, axis=-1, keepdims=True) + EPS)
        h_ref[...] = ((xv * r) * g_ref[...] * (1.0 + sc) + sh).astype(BF)

    return _call(body, name=name, grid=(t // TM,),
                 in_specs=_split_specs() + [_full((1, D)), _full((2, 3, D))],
                 out_specs=pl.BlockSpec((TM, D), lambda i: (i, 0)),
                 out_shape=jax.ShapeDtypeStruct((t, D), BF))(ctx, lat, g, mod)


def pro_bwd(ctx, lat, dh, dxn, g, mod, name):
    t = ctx.shape[0] + lat.shape[0]

    def body(c_ref, l_ref, dh_ref, dxn_ref, g_ref, mod_ref, dx_ref, dmod_ref, dg_ref):
        i = pl.program_id(0)

        @pl.when(i == 0)
        def _():
            dmod_ref[...] = jnp.zeros_like(dmod_ref)
            dg_ref[...] = jnp.zeros_like(dg_ref)

        _, sc, _ = _mod_rows(mod_ref, i)
        xv = _split_tile(c_ref, l_ref, i)
        gv = g_ref[...]
        r = lax.rsqrt(jnp.mean(xv * xv, axis=-1, keepdims=True) + EPS)
        xn = xv * r
        dh_v = dh_ref[...]
        e = dh_v * (1.0 + sc)
        dsh = jnp.sum(dh_v, axis=0, keepdims=True)
        dsc = jnp.sum(dh_v * xn * gv, axis=0, keepdims=True)
        dg_ref[...] += jnp.sum(e * xn, axis=0, keepdims=True)
        dxh = e * gv

        @pl.when(i == 0)
        def _():
            dmod_ref[0, 0:1, :] += dsh
            dmod_ref[0, 1:2, :] += dsc

        @pl.when(i > 0)
        def _():
            dx_ref[...] = dxn_ref[...] + r * (dxh - xn * jnp.mean(dxh * xn, axis=-1, keepdims=True))
            dmod_ref[1, 0:1, :] += dsh
            dmod_ref[1, 1:2, :] += dsc

    tile = pl.BlockSpec((TM, D), lambda i: (i, 0))
    return _call(body, name=name, grid=(t // TM,),
                 in_specs=_split_specs() + [tile, tile, _full((1, D)), _full((2, 3, D))],
                 out_specs=[_split_specs()[1], _full((2, 2, D)), _full((1, D))],
                 out_shape=[jax.ShapeDtypeStruct(lat.shape, F32), jax.ShapeDtypeStruct((2, 2, D), F32),
                            jax.ShapeDtypeStruct((1, D), F32)])(ctx, lat, dh, dxn, g, mod)


def res_fwd(x, o, mod, update_ctx, name):
    t = x.shape[0]

    def body(x_ref, o_ref, mod_ref, y_ref):
        i = pl.program_id(0)
        _, _, gt = _mod_rows(mod_ref, i)
        upd = x_ref[...] + gt * o_ref[...]
        if update_ctx:
            y_ref[...] = upd
        else:
            y_ref[...] = jnp.where(i == 0, x_ref[...], upd)

    tile = pl.BlockSpec((TM, D), lambda i: (i, 0))
    return _call(body, name=name, grid=(t // TM,), in_specs=[tile, tile, _full((2, 3, D))],
                 out_specs=tile, out_shape=jax.ShapeDtypeStruct((t, D), F32))(x, o, mod)


def res_pro_fwd(x, o, mod, update_ctx, g_next, mod_next, name):
    split = isinstance(x, tuple)
    xs = list(x) if split else [x]
    t = o.shape[0]

    def body(*refs):
        x_refs = refs[:len(xs)]
        o_ref, mod_ref, g_ref, modn_ref, y_ref, h_ref = refs[len(xs):]
        i = pl.program_id(0)
        _, _, gt = _mod_rows(mod_ref, i)
        xv = _split_tile(x_refs[0], x_refs[1], i) if split else x_refs[0][...]
        xn = xv + gt * o_ref[...]
        if not update_ctx:
            xn = jnp.where(i == 0, xv, xn)
        y_ref[...] = xn
        sh, sc, _ = _mod_rows(modn_ref, i)
        r = lax.rsqrt(jnp.mean(xn * xn, axis=-1, keepdims=True) + EPS)
        h_ref[...] = ((xn * r) * g_ref[...] * (1.0 + sc) + sh).astype(BF)

    tile = pl.BlockSpec((TM, D), lambda i: (i, 0))
    return _call(body, name=name, grid=(t // TM,),
                 in_specs=(_split_specs() if split else [tile]) + [tile, _full((2, 3, D)), _full((1, D)), _full((2, 3, D))],
                 out_specs=[tile, tile],
                 out_shape=[jax.ShapeDtypeStruct((t, D), F32), jax.ShapeDtypeStruct((t, D), BF)])(
        *xs, o, mod, g_next, mod_next)


def _res_bwd_part(dx, o_ref, modp_ref, do_ref, dgt_ref, i, update_ctx):
    _, _, gtp = _mod_rows(modp_ref, i)
    do = gtp * dx
    dgt = jnp.sum(dx * o_ref[...], axis=0, keepdims=True)
    if update_ctx:
        do_ref[...] = do.astype(BF)

        @pl.when(i == 0)
        def _():
            dgt_ref[0:1, :] += dgt
    else:
        do_ref[...] = jnp.where(i == 0, jnp.zeros_like(do), do).astype(BF)

    @pl.when(i > 0)
    def _():
        dgt_ref[1:2, :] += dgt


def pro_res_bwd(x, dh, dxn, g, mod, o_prev, mod_prev, name):
    t = x.shape[0]

    def body(x_ref, dh_ref, dxn_ref, g_ref, mod_ref, o_ref, modp_ref, dx_ref, dmod_ref, dg_ref, do_ref, dgt_ref):
        i = pl.program_id(0)

        @pl.when(i == 0)
        def _():
            dmod_ref[...] = jnp.zeros_like(dmod_ref)
            dg_ref[...] = jnp.zeros_like(dg_ref)
            dgt_ref[...] = jnp.zeros_like(dgt_ref)

        _, sc, _ = _mod_rows(mod_ref, i)
        xv = x_ref[...]
        gv = g_ref[...]
        r = lax.rsqrt(jnp.mean(xv * xv, axis=-1, keepdims=True) + EPS)
        xn = xv * r
        dh_v = dh_ref[...]
        e = dh_v * (1.0 + sc)
        dsh = jnp.sum(dh_v, axis=0, keepdims=True)
        dsc = jnp.sum(dh_v * xn * gv, axis=0, keepdims=True)
        dg_ref[...] += jnp.sum(e * xn, axis=0, keepdims=True)
        dxh = e * gv
        dx = dxn_ref[...] + r * (dxh - xn * jnp.mean(dxh * xn, axis=-1, keepdims=True))
        dx_ref[...] = dx

        @pl.when(i == 0)
        def _():
            dmod_ref[0, 0:1, :] += dsh
            dmod_ref[0, 1:2, :] += dsc

        @pl.when(i > 0)
        def _():
            dmod_ref[1, 0:1, :] += dsh
            dmod_ref[1, 1:2, :] += dsc

        _res_bwd_part(dx, o_ref, modp_ref, do_ref, dgt_ref, i, True)

    tile = pl.BlockSpec((TM, D), lambda i: (i, 0))
    return _call(body, name=name, grid=(t // TM,),
                 in_specs=[tile, tile, tile, _full((1, D)), _full((2, 3, D)), tile, _full((2, 3, D))],
                 out_specs=[tile, _full((2, 2, D)), _full((1, D)), tile, _full((2, D))],
                 out_shape=[jax.ShapeDtypeStruct((t, D), F32), jax.ShapeDtypeStruct((2, 2, D), F32),
                            jax.ShapeDtypeStruct((1, D), F32), jax.ShapeDtypeStruct((t, D), BF),
                            jax.ShapeDtypeStruct((2, D), F32)])(x, dh, dxn, g, mod, o_prev, mod_prev)


def final_loss(x, target, g, o_last, mod_last):
    t = x.shape[0]

    def body(x_ref, t_ref, g_ref, o_ref, modp_ref, loss_ref, dx_ref, dg_ref, do_ref, dgt_ref):
        i = pl.program_id(0)

        @pl.when(i == 0)
        def _():
            loss_ref[...] = jnp.zeros_like(loss_ref)
            dg_ref[...] = jnp.zeros_like(dg_ref)
            dx_ref[...] = jnp.zeros_like(dx_ref)
            do_ref[...] = jnp.zeros_like(do_ref)
            dgt_ref[...] = jnp.zeros_like(dgt_ref)

        @pl.when(i > 0)
        def _():
            xv = x_ref[...]
            gv = g_ref[...]
            r = lax.rsqrt(jnp.mean(xv * xv, axis=-1, keepdims=True) + EPS)
            xn = xv * r
            err = xn * gv - t_ref[...]
            loss_ref[...] += (0.5 / D) * jnp.sum(jnp.sum(err * err, axis=1, keepdims=True), axis=0, keepdims=True)
            dy = err * (1.0 / D)
            dg_ref[...] += jnp.sum(dy * xn, axis=0, keepdims=True)
            dxh = dy * gv
            dx = r * (dxh - xn * jnp.mean(dxh * xn, axis=-1, keepdims=True))
            dx_ref[...] = dx
            do_ref[...] = (modp_ref[1, 2:3, :] * dx).astype(BF)
            dgt_ref[1:2, :] += jnp.sum(dx * o_ref[...], axis=0, keepdims=True)

    tile = pl.BlockSpec((TM, D), lambda i: (i, 0))
    return _call(body, name="final_loss", grid=(t // TM,),
                 in_specs=[tile, pl.BlockSpec((TM, D), lambda i: (jnp.maximum(i - 1, 0), 0)), _full((1, D)), tile,
                           _full((2, 3, D))],
                 out_specs=[_full((1, 1)), tile, _full((1, D)), tile, _full((2, D))],
                 out_shape=[jax.ShapeDtypeStruct((1, 1), F32), jax.ShapeDtypeStruct((t, D), F32),
                            jax.ShapeDtypeStruct((1, D), F32), jax.ShapeDtypeStruct((t, D), BF),
                            jax.ShapeDtypeStruct((2, D), F32)])(x, target, g, o_last, mod_last)


def ada_fwd(cond, w_mod, b_mod):
    nl, _, nw = w_mod.shape

    def body(c_ref, w_ref, b_ref, o_ref):
        cv = c_ref[...]
        s = (cv * _sig(cv)).astype(BF)
        o_ref[0] = _dot(s, w_ref[0].astype(BF)) + b_ref[0]

    return _call(body, name="ada_fwd", grid=(nl,),
                 in_specs=[_full((8, D)), pl.BlockSpec((1, D, nw), lambda l: (l, 0, 0)),
                           pl.BlockSpec((1, 1, nw), lambda l: (l, 0, 0))],
                 out_specs=pl.BlockSpec((1, 8, nw), lambda l: (l, 0, 0)),
                 out_shape=jax.ShapeDtypeStruct((nl, 8, nw), F32))(cond, w_mod, b_mod)


def ada_bwd(cond, dm, w_mod):
    nl, _, nw = w_mod.shape

    def body(c_ref, dm_ref, w_ref, gw_ref, dcc_ref, dc_ref):
        l = pl.program_id(0)

        @pl.when(l == 0)
        def _():
            dc_ref[...] = jnp.zeros_like(dc_ref)

        cv = c_ref[...]
        sg = _sig(cv)
        s = (cv * sg).astype(BF)
        dmv = dm_ref[0].astype(BF)
        gw_ref[0] = _dot(s, dmv, 0, 0)
        dc_ref[...] += _dot(dmv, w_ref[0].astype(BF), 1, 1)

        @pl.when(l == nl - 1)
        def _():
            dcond = dc_ref[...] * (sg * (1.0 + cv * (1.0 - sg)))
            dcc_ref[...] = jnp.sum(dcond[4:8], axis=0, keepdims=True)

    return _call(body, name="ada_bwd", grid=(nl,),
                 in_specs=[_full((8, D)), pl.BlockSpec((1, 8, nw), lambda l: (l, 0, 0)),
                           pl.BlockSpec((1, D, nw), lambda l: (l, 0, 0))],
                 out_specs=[pl.BlockSpec((1, D, nw), lambda l: (l, 0, 0)), _full((1, D))],
                 out_shape=[jax.ShapeDtypeStruct((nl, D, nw), F32), jax.ShapeDtypeStruct((1, D), F32)],
                 scratch=[pltpu.VMEM((8, D), F32)])(cond, dm, w_mod)


def add2(a, b, name):
    def body(a_ref, b_ref, o_ref):
        o_ref[...] = a_ref[...] + b_ref[...]

    return _call(body, name=name, out_shape=jax.ShapeDtypeStruct(a.shape, a.dtype))(a, b)


AW = 512
NGRP = 4


def Y4_SPEC():
    return pl.BlockSpec((AW // 128, TM, 128), lambda i: (0, i, 0))


def _cat_lanes(ref):
    return jnp.concatenate([ref[q] for q in range(ref.shape[0])], axis=1)


def _gelu(y):
    t = jnp.tanh(GELU_K * (y + GELU_C * y * y * y))
    return 0.5 * y * (1.0 + t), t


def _layer_norm_stats(v):
    mu = jnp.mean(v, axis=-1, keepdims=True)
    vc = v - mu
    rstd = lax.rsqrt(jnp.mean(vc * vc, axis=-1, keepdims=True) + EPS)
    return vc * rstd, rstd


def _spatial_mix(vn_ref, ws_ref, bs_ref, mixed_ref):
    for ch in range(TM // CHUNK):
        rows = slice(ch * CHUNK, (ch + 1) * CHUNK)
        for g in range(NGRP):
            cols = slice(g * CHUNK, (g + 1) * CHUNK)
            mixed_ref[rows, cols] = _dot(ws_ref[g], vn_ref[rows, cols]) + bs_ref[g]


def mix_fwd(p, yf, yb, vg, ws, bs, dsk, wglu, bglu, name):
    t = p.shape[0]
    wglu, glu_layer = _layer_of(wglu)

    def body(p_ref, yf_ref, yb_ref, vg_ref, ws_ref, bs_ref, d_ref, wg_ref, bg_ref, o_ref, vn_ref, mixed_ref):
        vhat, _ = _layer_norm_stats(p_ref[:, AW:2 * AW])
        vn_ref[...] = (vhat * vg_ref[...]).astype(BF)
        _spatial_mix(vn_ref, ws_ref, bs_ref, mixed_ref)
        ga = p_ref[:, 2 * AW:3 * AW]
        o_ref[:, 0:AW] = (p_ref[:, 0:AW] * mixed_ref[...] * (ga * _sig(ga))).astype(BF)
        y = _cat_lanes(yf_ref) + _cat_lanes(yb_ref) + d_ref[...] * p_ref[:, 3 * AW:4 * AW]
        y2, _ = _gelu(y)
        z = _dot(y2.astype(BF), wg_ref[0]) + bg_ref[...]
        gb = p_ref[:, 4 * AW:5 * AW]
        o_ref[:, AW:2 * AW] = (y2 * _sig(z) * (gb * _sig(gb))).astype(BF)

    tile = lambda w: pl.BlockSpec((TM, w), lambda i: (i, 0))
    return _call(body, name=name, grid=(t // TM,),
                 in_specs=[tile(5 * AW), Y4_SPEC(), Y4_SPEC(), _full((1, AW)), _full((NGRP, CHUNK, CHUNK)),
                           _full((NGRP, CHUNK, 1)), _full((1, AW)),
                           pl.BlockSpec((1, AW, AW), lambda i: (glu_layer, 0, 0)), _full((1, AW))],
                 out_specs=tile(2 * AW), out_shape=jax.ShapeDtypeStruct((t, 2 * AW), BF),
                 scratch=[pltpu.VMEM((TM, AW), BF), pltpu.VMEM((TM, AW), F32)])(p, yf, yb, vg, ws, bs, dsk, wglu, bglu)


def mix_bwd(p, yf, yb, dmix, vg, ws, bs, dsk, wglu, bglu, name):
    t = p.shape[0]
    wglu, glu_layer = _layer_of(wglu)

    def body(p_ref, yf_ref, yb_ref, dm_ref, vg_ref, ws_ref, bs_ref, d_ref, wg_ref, bg_ref,
             dpa_ref, dy_ref, dws_ref, dbs_ref, dvg_ref, dd_ref, dwg_ref, dbg_ref,
             vn_ref, mixed_ref, dmx_ref, dvn_ref):
        @pl.when(pl.program_id(0) == 0)
        def _():
            for r in (dws_ref, dbs_ref, dvg_ref, dd_ref, dwg_ref, dbg_ref):
                r[...] = jnp.zeros_like(r)

        vhat, rstd = _layer_norm_stats(p_ref[:, AW:2 * AW])
        vgv = vg_ref[...]
        vn_ref[...] = (vhat * vgv).astype(BF)
        _spatial_mix(vn_ref, ws_ref, bs_ref, mixed_ref)
        u = p_ref[:, 0:AW]
        ga = p_ref[:, 2 * AW:3 * AW]
        sga = _sig(ga)
        dya = dm_ref[:, 0:AW]
        mixed = mixed_ref[...]
        dpa_ref[:, 0:AW] = (dya * mixed * (ga * sga)).astype(BF)
        dpa_ref[:, 2 * AW:3 * AW] = (dya * u * mixed * (sga * (1.0 + ga * (1.0 - sga)))).astype(BF)
        dmx_ref[...] = dya * u * (ga * sga)
        for ch in range(TM // CHUNK):
            rows = slice(ch * CHUNK, (ch + 1) * CHUNK)
            for g in range(NGRP):
                cols = slice(g * CHUNK, (g + 1) * CHUNK)
                dmx = dmx_ref[rows, cols]
                dmxb = dmx.astype(BF)
                dws_ref[g] += _dot(dmxb, vn_ref[rows, cols], 1, 1)
                dbs_ref[g] += jnp.sum(dmx, axis=1, keepdims=True)
                dvn_ref[rows, cols] = _dot(ws_ref[g], dmxb, 0, 0)
        dvn = dvn_ref[...]
        dvg_ref[...] += jnp.sum(dvn * vhat, axis=0, keepdims=True)
        dvh = dvn * vgv
        dpa_ref[:, AW:2 * AW] = (rstd * (dvh - jnp.mean(dvh, axis=-1, keepdims=True)
                                         - vhat * jnp.mean(dvh * vhat, axis=-1, keepdims=True))).astype(BF)

        xs = p_ref[:, 3 * AW:4 * AW]
        y = _cat_lanes(yf_ref) + _cat_lanes(yb_ref) + d_ref[...] * xs
        y2, th = _gelu(y)
        y2b = y2.astype(BF)
        z = _dot(y2b, wg_ref[0]) + bg_ref[...]
        sz = _sig(z)
        gb = p_ref[:, 4 * AW:5 * AW]
        sgb = _sig(gb)
        dyb = dm_ref[:, AW:2 * AW]
        dpa_ref[:, 4 * AW:5 * AW] = (dyb * (y2 * sz) * (sgb * (1.0 + gb * (1.0 - sgb)))).astype(BF)
        dy3 = dyb * (gb * sgb)
        dz = dy3 * y2 * sz * (1.0 - sz)
        dzb = dz.astype(BF)
        dwg_ref[...] += _dot(y2b, dzb, 0, 0)
        dbg_ref[...] += jnp.sum(dz, axis=0, keepdims=True)
        dy2 = dy3 * sz + _dot(dzb, wg_ref[0], 1, 1)
        dgelu = 0.5 * (1.0 + th) + 0.5 * y * (1.0 - th * th) * GELU_K * (1.0 + 3.0 * GELU_C * y * y)
        dy = dy2 * dgelu
        dd_ref[...] += jnp.sum(dy * xs, axis=0, keepdims=True)
        dy_ref[...] = dy

    tile = lambda w: pl.BlockSpec((TM, w), lambda i: (i, 0))
    return _call(body, name=name, grid=(t // TM,),
                 in_specs=[tile(5 * AW), Y4_SPEC(), Y4_SPEC(), tile(2 * AW), _full((1, AW)), _full((NGRP, CHUNK, CHUNK)),
                           _full((NGRP, CHUNK, 1)), _full((1, AW)),
                           pl.BlockSpec((1, AW, AW), lambda i: (glu_layer, 0, 0)), _full((1, AW))],
                 out_specs=[tile(5 * AW), tile(AW), _full((NGRP, CHUNK, CHUNK)), _full((NGRP, CHUNK, 1)),
                            _full((1, AW)), _full((1, AW)), _full((AW, AW)), _full((1, AW))],
                 out_shape=[jax.ShapeDtypeStruct((t, 5 * AW), BF),
                            jax.ShapeDtypeStruct((t, AW), F32), jax.ShapeDtypeStruct((NGRP, CHUNK, CHUNK), F32),
                            jax.ShapeDtypeStruct((NGRP, CHUNK, 1), F32), jax.ShapeDtypeStruct((1, AW), F32),
                            jax.ShapeDtypeStruct((1, AW), F32), jax.ShapeDtypeStruct((AW, AW), F32),
                            jax.ShapeDtypeStruct((1, AW), F32)],
                 scratch=[pltpu.VMEM((TM, AW), BF), pltpu.VMEM((TM, AW), F32), pltpu.VMEM((TM, AW), F32),
                          pltpu.VMEM((TM, AW), F32)])(p, yf, yb, dmix, vg, ws, bs, dsk, wglu, bglu)


LN = 512
NBLK = SW // LN
UB = AW // NBLK
SCAN_R = 32
SCAN_G = TM // SCAN_R
PW_ROWS = SCAN_R
POW_EXP = list(range(1, SCAN_R + 1))


def s5_disc(lam_re, lam_im, dt, lam_re_r, lam_im_r, dt_r, b_re, b_im):
    nexp = jnp.asarray(np.array(POW_EXP, np.float32).reshape(PW_ROWS, 1))

    def body(n_ref, lr_ref, li_ref, dt_ref, lrr_ref, lir_ref, dtr_ref, br_ref, bi_ref,
             pr_ref, pi_ref, bbr_ref, bbi_ref):
        for dr in range(2):
            dtl = jnp.exp(dt_ref[dr:dr + 1, :])
            zr = lr_ref[dr:dr + 1, :] * dtl
            zi = li_ref[dr:dr + 1, :] * dtl
            mag = jnp.exp(n_ref[...] * zr)
            ang = n_ref[...] * zi
            pr_ref[dr] = mag * jnp.cos(ang)
            pi_ref[dr] = mag * jnp.sin(ang)
        lr, li, dtv = lrr_ref[...], lir_ref[...], jnp.exp(dtr_ref[...])
        mag = jnp.exp(lr * dtv)
        nr = mag * jnp.cos(li * dtv) - 1.0
        ni = mag * jnp.sin(li * dtv)
        den = lr * lr + li * li
        fr = (nr * lr + ni * li) / den
        fi = (ni * lr - nr * li) / den
        bbr_ref[...] = fr * br_ref[...] - fi * bi_ref[...]
        bbi_ref[...] = fr * bi_ref[...] + fi * br_ref[...]

    rows = lam_re_r.shape[0]
    return _call(body, name="s5_disc",
                 out_shape=[jax.ShapeDtypeStruct((2, PW_ROWS, SW), F32), jax.ShapeDtypeStruct((2, PW_ROWS, SW), F32),
                            jax.ShapeDtypeStruct((rows, SP), F32), jax.ShapeDtypeStruct((rows, SP), F32)])(
        nexp, lam_re, lam_im, dt, lam_re_r, lam_im_r, dt_r, b_re, b_im)


def s5_param_bwd(lam_re_r, lam_im_r, dt_r, b_re, b_im, da_re, da_im, dbb_re, dbb_im):
    rows = lam_re_r.shape[0]
    ng = rows // SH
    seg = jnp.asarray(np.kron(np.eye(ng, dtype=np.float32), np.ones((1, SH), np.float32)))

    def body(seg_ref, lr_ref, li_ref, dt_ref, br_ref, bi_ref, dar_ref, dai_ref, dbbr_ref, dbbi_ref,
             dlr_ref, dli_ref, ddt_ref, dbr_ref, dbi_ref):
        lr, li, dtv = lr_ref[...], li_ref[...], jnp.exp(dt_ref[...])
        mag = jnp.exp(lr * dtv)
        lbr = mag * jnp.cos(li * dtv)
        lbi = mag * jnp.sin(li * dtv)
        den = lr * lr + li * li
        nr, ni = lbr - 1.0, lbi
        fr = (nr * lr + ni * li) / den
        fi = (ni * lr - nr * li) / den
        br, bi = br_ref[...], bi_ref[...]
        gbr, gbi = dbbr_ref[...], dbbi_ref[...]
        dbr_ref[...] = gbr * fr + gbi * fi
        dbi_ref[...] = gbi * fr - gbr * fi
        gfr = gbr * br + gbi * bi
        gfi = gbi * br - gbr * bi
        ilr, ili = lr / den, -li / den
        gnr = gfr * ilr + gfi * ili
        gni = gfi * ilr - gfr * ili
        qr = -(fr * ilr - fi * ili)
        qi = -(fr * ili + fi * ilr)
        glr = gfr * qr + gfi * qi
        gli = gfi * qr - gfr * qi
        first = (lax.broadcasted_iota(jnp.int32, (rows, 1), 0) % SH) == 0
        glbr = gnr + jnp.where(first, dar_ref[...], 0.0)
        glbi = gni + jnp.where(first, dai_ref[...], 0.0)
        gzr = glbr * lbr + glbi * lbi
        gzi = glbi * lbr - glbr * lbi
        glr = glr + gzr * dtv
        gli = gli + gzi * dtv
        gdt = (gzr * lr + gzi * li) * dtv
        hi = lax.Precision.HIGHEST
        sg = seg_ref[...]
        dlr_ref[...] = jnp.dot(sg, glr, precision=hi, preferred_element_type=F32)
        dli_ref[...] = jnp.dot(sg, gli, precision=hi, preferred_element_type=F32)
        ddt_ref[...] = jnp.sum(jnp.dot(sg, gdt, precision=hi, preferred_element_type=F32), axis=1, keepdims=True)

    return _call(body, name="s5_param_bwd",
                 out_shape=[jax.ShapeDtypeStruct((ng, SP), F32), jax.ShapeDtypeStruct((ng, SP), F32),
                            jax.ShapeDtypeStruct((ng, 1), F32), jax.ShapeDtypeStruct((rows, SP), F32),
                            jax.ShapeDtypeStruct((rows, SP), F32)])(
        seg, lam_re_r, lam_im_r, dt_r, b_re, b_im, da_re, da_im, dbb_re, dbb_im)


def _tile_order(kind, nt):
    if kind == "fwd":
        return lambda i: i
    if kind == "bwd":
        return lambda i: jnp.where(i == 0, 0, nt - i)
    if kind == "fwd_adj":
        return lambda i: nt - 1 - i
    if kind == "bwd_adj":
        return lambda i: jnp.where(i == nt - 1, 0, i + 1)
    raise ValueError(kind)


XS_BLK = 3 * AW // 128


def _load_perm(refs):
    return jnp.concatenate(
        [jnp.concatenate([ref[pl.ds(r, SCAN_G, stride=SCAN_R), :] for ref in refs], axis=1) for r in range(SCAN_R)],
        axis=0)


def _store_perm(out_ref, val):
    for r in range(SCAN_R):
        for q in range(AW // 128):
            out_ref[q, pl.ds(r, SCAN_G, stride=SCAN_R), :] = val[r * SCAN_G:(r + 1) * SCAN_G, q * 128:(q + 1) * 128]


def _scan2(br_ref, bi_ref, or_ref, oi_ref, h_off, cin_off, er_ref, ei_ref, cr_ref, ci_ref, pr_ref, pi_ref, reverse,
           corr=None):
    gpt = SCAN_G
    nr = SCAN_R
    offsets = list(range(nr))[::-1] if reverse else list(range(nr))
    blocks = [slice(b * LN, (b + 1) * LN) for b in range(NBLK)]
    slab = lambda r: slice(r * gpt, (r + 1) * gpt)
    a1 = [(pr_ref[0:1, c], pi_ref[0:1, c]) for c in blocks]
    x = [None] * NBLK
    for r in offsets:
        for b, c in enumerate(blocks):
            if x[b] is None:
                x[b] = (br_ref[slab(r), c], bi_ref[slab(r), c])
            else:
                (ar, ai), (xr, xi) = a1[b], x[b]
                x[b] = (br_ref[slab(r), c] + ar * xr - ai * xi, bi_ref[slab(r), c] + ar * xi + ai * xr)
    an = [(pr_ref[nr - 1:nr, c], pi_ref[nr - 1:nr, c]) for c in blocks]
    k = [(cr_ref[:, c], ci_ref[:, c]) for c in blocks]
    for g in (range(gpt - 1, -1, -1) if reverse else range(gpt)):
        for b, c in enumerate(blocks):
            (ar, ai), (kr, ki), (xr, xi) = an[b], k[b], x[b]
            er_ref[g:g + 1, c] = kr
            ei_ref[g:g + 1, c] = ki
            k[b] = (xr[g:g + 1, :] + ar * kr - ai * ki, xi[g:g + 1, :] + ar * ki + ai * kr)
    for b, c in enumerate(blocks):
        cr_ref[:, c] = k[b][0]
        ci_ref[:, c] = k[b][1]
        x[b] = (er_ref[:, c], ei_ref[:, c])
        if cin_off is not None:
            or_ref[cin_off:cin_off + gpt, c] = x[b][0]
            oi_ref[cin_off:cin_off + gpt, c] = x[b][1]
    acc = [None] * NBLK
    for r in offsets:
        for b, c in enumerate(blocks):
            (ar, ai), (xr, xi) = a1[b], x[b]
            x[b] = (br_ref[slab(r), c] + ar * xr - ai * xi, bi_ref[slab(r), c] + ar * xi + ai * xr)
            or_ref[h_off + r * gpt:h_off + (r + 1) * gpt, c] = x[b][0]
            oi_ref[h_off + r * gpt:h_off + (r + 1) * gpt, c] = x[b][1]
            if corr is not None:
                wr_ref, wi_ref, w_off = corr[:3]
                wr, wi = wr_ref[w_off + r * gpt:w_off + (r + 1) * gpt, c], wi_ref[w_off + r * gpt:w_off + (r + 1) * gpt, c]
                pr_, pi_ = x[b][0] * wr + x[b][1] * wi, x[b][1] * wr - x[b][0] * wi
                acc[b] = (pr_, pi_) if acc[b] is None else (acc[b][0] + pr_, acc[b][1] + pi_)
    if corr is not None:
        sr_ref, si_ref = corr[3:]
        for b, c in enumerate(blocks):
            sr_ref[:, c] += jnp.sum(acc[b][0], axis=0, keepdims=True)
            si_ref[:, c] += jnp.sum(acc[b][1], axis=0, keepdims=True)


HS_ROWS = TM + SCAN_G


def _hs_offsets(reverse):
    return (0, SCAN_G) if reverse else (SCAN_G, 0)


def _dir_spec(dr, shape):
    return pl.BlockSpec((1,) + shape, lambda i: (dr,) + (0,) * len(shape))


def s5q_fwd(p, bb_re, bb_im, ct_re, ct_im, pw_re, pw_im, dr, reverse, name):
    t = p.shape[0]
    nt = t // TM
    order = _tile_order("bwd" if reverse else "fwd", nt)
    nq = AW // 128
    h_off, p_off = _hs_offsets(reverse)

    def body(*refs):
        x_refs = refs[:nq]
        bbr_ref, bbi_ref, ctr_ref, cti_ref, pr_ref, pi_ref = [r.at[0] for r in refs[nq:nq + 6]]
        y_ref, hsr_ref, hsi_ref = refs[nq + 6:nq + 9]
        br_ref, bi_ref, er_ref, ei_ref, cr_ref, ci_ref = refs[nq + 9:]

        @pl.when(pl.program_id(0) == 0)
        def _():
            cr_ref[...] = jnp.zeros_like(cr_ref)
            ci_ref[...] = jnp.zeros_like(ci_ref)

        xb = _load_perm(x_refs).astype(BF)
        for j in range(NBLK):
            cols = slice(j * LN, (j + 1) * LN)
            br_ref[:, cols] = _dot(xb[:, j * UB:(j + 1) * UB], bbr_ref[j])
            bi_ref[:, cols] = _dot(xb[:, j * UB:(j + 1) * UB], bbi_ref[j])
        _scan2(br_ref, bi_ref, hsr_ref, hsi_ref, h_off, TM if reverse else 0, er_ref, ei_ref, cr_ref, ci_ref,
               pr_ref, pi_ref, reverse)
        y = jnp.concatenate(
            [_dot(hsr_ref[h_off:h_off + TM, j * LN:(j + 1) * LN].astype(BF), ctr_ref[j])
             - _dot(hsi_ref[h_off:h_off + TM, j * LN:(j + 1) * LN].astype(BF), cti_ref[j]) for j in range(NBLK)], axis=1)
        _store_perm(y_ref, y)

    state = lambda: pl.BlockSpec((HS_ROWS, SW), lambda i: (order(i), 0))
    xspec = lambda q: pl.BlockSpec((TM, 128), lambda i: (order(i), XS_BLK + q))
    return _call(body, name=name, grid=(nt,),
                 in_specs=[xspec(q) for q in range(nq)]
                 + [_dir_spec(dr, (NBLK, UB, LN)), _dir_spec(dr, (NBLK, UB, LN)), _dir_spec(dr, (NBLK, LN, UB)),
                    _dir_spec(dr, (NBLK, LN, UB)), _dir_spec(dr, (PW_ROWS, SW)), _dir_spec(dr, (PW_ROWS, SW))],
                 out_specs=[pl.BlockSpec((nq, TM, 128), lambda i: (0, order(i), 0)), state(), state()],
                 out_shape=[jax.ShapeDtypeStruct((nq, t, 128), F32), jax.ShapeDtypeStruct((nt * HS_ROWS, SW), F32),
                            jax.ShapeDtypeStruct((nt * HS_ROWS, SW), F32)],
                 scratch=[pltpu.VMEM((TM, SW), F32), pltpu.VMEM((TM, SW), F32),
                          pltpu.VMEM((SCAN_G, SW), F32), pltpu.VMEM((SCAN_G, SW), F32),
                          pltpu.VMEM((1, SW), F32), pltpu.VMEM((1, SW), F32)])(
        *([p] * nq), bb_re, bb_im, ct_re, ct_im, pw_re, pw_im)


def s5q_bwd(p, hs_re, hs_im, dy, bb_re, bb_im, ct_re, ct_im, pw_re, pw_im_conj, dr, reverse, name):
    t = p.shape[0]
    nt = t // TM
    order = _tile_order("bwd_adj" if reverse else "fwd_adj", nt)
    nq = AW // 128
    h_off, p_off = _hs_offsets(reverse)

    def body(*refs):
        x_refs, dy_refs = refs[:nq], refs[nq:2 * nq]
        (hsr_ref, hsi_ref, bbr_ref, bbi_ref, ctr_ref, cti_ref, pr_ref, pi_ref,
         dx_ref, dar_ref, dai_ref, dbbr_ref, dbbi_ref, dcr_ref, dci_ref,
         qr_ref, qi_ref, gr_ref, gi_ref, er_ref, ei_ref, cr_ref, ci_ref) = refs[2 * nq:]
        bbr_ref, bbi_ref, ctr_ref, cti_ref, pr_ref, pi_ref = [
            r.at[0] for r in (bbr_ref, bbi_ref, ctr_ref, cti_ref, pr_ref, pi_ref)]

        @pl.when(pl.program_id(0) == 0)
        def _():
            for r in (cr_ref, ci_ref, dar_ref, dai_ref, dbbr_ref, dbbi_ref, dcr_ref, dci_ref):
                r[...] = jnp.zeros_like(r)

        xb = _load_perm(x_refs).astype(BF)
        dyb = _load_perm(dy_refs).astype(BF)
        for j in range(NBLK):
            cols = slice(j * LN, (j + 1) * LN)
            qr_ref[:, cols] = _dot(dyb[:, j * UB:(j + 1) * UB], ctr_ref[j], 1, 1)
            qi_ref[:, cols] = -_dot(dyb[:, j * UB:(j + 1) * UB], cti_ref[j], 1, 1)
        _scan2(qr_ref, qi_ref, gr_ref, gi_ref, 0, None, er_ref, ei_ref, cr_ref, ci_ref, pr_ref, pi_ref, not reverse,
               corr=(hsr_ref, hsi_ref, p_off, dar_ref, dai_ref))
        dxs = []
        for j in range(NBLK):
            cols = slice(j * LN, (j + 1) * LN)
            xj = xb[:, j * UB:(j + 1) * UB]
            dyj = dyb[:, j * UB:(j + 1) * UB]
            grb, gib = gr_ref[:, cols].astype(BF), gi_ref[:, cols].astype(BF)
            dcr_ref[j] += _dot(dyj, hsr_ref[h_off:h_off + TM, cols].astype(BF), 0, 0)
            dci_ref[j] += -_dot(dyj, hsi_ref[h_off:h_off + TM, cols].astype(BF), 0, 0)
            dbbr_ref[j] += _dot(xj, grb, 0, 0)
            dbbi_ref[j] += _dot(xj, gib, 0, 0)
            dxs.append(_dot(grb, bbr_ref[j], 1, 1) + _dot(gib, bbi_ref[j], 1, 1))
        _store_perm(dx_ref, jnp.concatenate(dxs, axis=1))

    state = lambda: pl.BlockSpec((HS_ROWS, SW), lambda i: (order(i), 0))
    blockd = lambda: _full((NBLK, UB, LN))
    xspec = lambda q: pl.BlockSpec((TM, 128), lambda i: (order(i), XS_BLK + q))
    dyspec = lambda q: pl.BlockSpec((TM, 128), lambda i: (order(i), q))
    return _call(body, name=name, grid=(nt,),
                 in_specs=[xspec(q) for q in range(nq)] + [dyspec(q) for q in range(nq)]
                 + [state(), state(), _dir_spec(dr, (NBLK, UB, LN)), _dir_spec(dr, (NBLK, UB, LN)),
                    _dir_spec(dr, (NBLK, LN, UB)), _dir_spec(dr, (NBLK, LN, UB)),
                    _dir_spec(dr, (PW_ROWS, SW)), _dir_spec(dr, (PW_ROWS, SW))],
                 out_specs=[pl.BlockSpec((nq, TM, 128), lambda i: (0, order(i), 0)), _full((1, SW)), _full((1, SW)),
                            blockd(), blockd(), blockd(), blockd()],
                 out_shape=[jax.ShapeDtypeStruct((nq, t, 128), F32), jax.ShapeDtypeStruct((1, SW), F32),
                            jax.ShapeDtypeStruct((1, SW), F32)] + [jax.ShapeDtypeStruct((NBLK, UB, LN), F32)] * 4,
                 scratch=[pltpu.VMEM((TM, SW), F32), pltpu.VMEM((TM, SW), F32),
                          pltpu.VMEM((TM, SW), F32), pltpu.VMEM((TM, SW), F32),
                          pltpu.VMEM((SCAN_G, SW), F32), pltpu.VMEM((SCAN_G, SW), F32),
                          pltpu.VMEM((1, SW), F32), pltpu.VMEM((1, SW), F32)])(
        *([p] * nq), *([dy] * nq), hs_re, hs_im, bb_re, bb_im, ct_re, ct_im, pw_re, pw_im_conj)


def s5p_dx_sum(dy, dsk, dxf, dxb, dp, name):
    t = dy.shape[0]
    nq = AW // 128

    def body(dy_ref, d_ref, f_ref, b_ref, dp_ref, o_ref):
        o_ref[...] = (dy_ref[...] * d_ref[...] + _cat_lanes(f_ref) + _cat_lanes(b_ref)).astype(BF)

    tile = pl.BlockSpec((TM, AW), lambda i: (i, 0))
    blk4 = pl.BlockSpec((nq, TM, 128), lambda i: (0, i, 0))
    return pl.pallas_call(
        body, name=name, grid=(t // TM,),
        in_specs=[tile, _full((1, AW)), blk4, blk4, pl.BlockSpec(memory_space=pl.ANY)],
        out_specs=pl.BlockSpec((TM, AW), lambda i: (i, 3)), out_shape=jax.ShapeDtypeStruct(dp.shape, dp.dtype),
        input_output_aliases={4: 0},
        compiler_params=pltpu.CompilerParams(vmem_limit_bytes=VMEM_LIMIT_BYTES))(dy, dsk, dxf, dxb, dp)


SCALE = HD ** -0.5
NHEAD_NORM = NQ + NKV


def _partner(x):
    half0 = (lax.broadcasted_iota(jnp.int32, (1, HD), 1) % 64) < 32
    return jnp.where(half0, pltpu.roll(x, HD - 32, 1), pltpu.roll(x, 32, 1))


def attn_prep(p, qg, kg, cos, sins, name):
    t = p.shape[0]

    def body(p_ref, qg_ref, kg_ref, cos_ref, sin_ref, o_ref):
        cv, sv = cos_ref[...], sin_ref[...]
        for h in range(NHEAD_NORM):
            cols = slice(h * HD, (h + 1) * HD)
            blk = p_ref[:, cols]
            r = lax.rsqrt(jnp.mean(blk * blk, axis=-1, keepdims=True) + EPS)
            xn = blk * r * (qg_ref[...] if h < NQ else kg_ref[...])
            rot = xn * cv + _partner(xn) * sv
            o_ref[:, cols] = ((rot * SCALE) if h < NQ else rot).astype(BF)
        vcols = slice(NHEAD_NORM * HD, (NHEAD_NORM + NKV) * HD)
        o_ref[:, vcols] = p_ref[:, vcols].astype(BF)

    w = (NHEAD_NORM + NKV) * HD
    tile = lambda ww: pl.BlockSpec((TM, ww), lambda i: (i, 0))
    return _call(body, name=name, grid=(t // TM,),
                 in_specs=[tile(w), _full((1, HD)), _full((1, HD)), tile(HD), tile(HD)],
                 out_specs=tile(w), out_shape=jax.ShapeDtypeStruct((t, w), BF))(p, qg, kg, cos, sins)


def attn_prep_bwd(p, dq, dk, dv, qg, kg, cos, sins, dp, name):
    t = p.shape[0]

    def body(p_ref, dq_ref, dk_ref, dv_ref, qg_ref, kg_ref, cos_ref, sin_ref, dp_ref, o_ref, dqg_ref, dkg_ref):
        @pl.when(pl.program_id(0) == 0)
        def _():
            dqg_ref[...] = jnp.zeros_like(dqg_ref)
            dkg_ref[...] = jnp.zeros_like(dkg_ref)

        cv, sv = cos_ref[...], sin_ref[...]
        for h in range(NHEAD_NORM):
            cols = slice(h * HD, (h + 1) * HD)
            blk = p_ref[:, cols]
            r = lax.rsqrt(jnp.mean(blk * blk, axis=-1, keepdims=True) + EPS)
            xh = blk * r
            if h < NQ:
                drot = dq_ref[:, cols] * SCALE
                gv, dg_ref = qg_ref[...], dqg_ref
            else:
                drot = dk_ref[:, (h - NQ) * HD:(h - NQ + 1) * HD]
                gv, dg_ref = kg_ref[...], dkg_ref
            dxn = drot * cv + _partner(drot * sv)
            dg_ref[...] += jnp.sum(dxn * xh, axis=0, keepdims=True)
            dxh = dxn * gv
            o_ref[:, cols] = (r * (dxh - xh * jnp.mean(dxh * xh, axis=-1, keepdims=True))).astype(BF)
        o_ref[:, NHEAD_NORM * HD:(NHEAD_NORM + NKV) * HD] = dv_ref[...].astype(BF)

    w = (NHEAD_NORM + NKV) * HD
    tile = lambda ww: pl.BlockSpec((TM, ww), lambda i: (i, 0))
    return pl.pallas_call(
        body, name=name, grid=(t // TM,),
        in_specs=[tile(w), tile(NQ * HD), tile(NKV * HD), tile(NKV * HD), _full((1, HD)), _full((1, HD)),
                  tile(HD), tile(HD), pl.BlockSpec(memory_space=pl.ANY)],
        out_specs=[tile(w), _full((1, HD)), _full((1, HD))],
        out_shape=[jax.ShapeDtypeStruct(dp.shape, dp.dtype), jax.ShapeDtypeStruct((1, HD), F32),
                   jax.ShapeDtypeStruct((1, HD), F32)],
        input_output_aliases={8: 0},
        compiler_params=pltpu.CompilerParams(vmem_limit_bytes=VMEM_LIMIT_BYTES))(p, dq, dk, dv, qg, kg, cos, sins, dp)


KCOL = NQ
VCOL = NQ + NKV
GCOL = (NQ + 2 * NKV)
QPK = NQ // NKV
ATT_KCHUNK = 512


def attn_fwd(qkv, p, name):
    t = qkv.shape[0]

    def body(q_ref, k_ref, v_ref, g_ref, o_ref, mix_ref, lse_ref):
        def attend(nk):
            q = q_ref[...]
            chunks = [(k0, min(k0 + 2 * ATT_KCHUNK, nk)) for k0 in range(0, nk, 2 * ATT_KCHUNK)]
            s_next = _dot(q, k_ref[chunks[0][0]:chunks[0][1], :], 1, 1)
            m = l = acc = None
            for ci, (k0, k1) in enumerate(chunks):
                s = s_next
                if ci + 1 < len(chunks):
                    s_next = _dot(q, k_ref[chunks[ci + 1][0]:chunks[ci + 1][1], :], 1, 1)
                mc = jnp.max(s, axis=-1, keepdims=True)
                m_new = mc if m is None else jnp.maximum(m, mc)
                pe = jnp.exp(s - m_new)
                lc = jnp.sum(pe, axis=-1, keepdims=True)
                pv = _dot(pe.astype(BF), v_ref[k0:k1, :])
                if m is None:
                    l, acc = lc, pv
                else:
                    alpha = jnp.exp(m - m_new)
                    l, acc = alpha * l + lc, alpha * acc + pv
                m = m_new
            o = acc / l
            gt = g_ref[...]
            o_ref[...] = o
            mix_ref[...] = (o * (gt * _sig(gt))).astype(BF)
            lse_ref[...] = jnp.broadcast_to(m + jnp.log(l), (TM, HD))

        pl.when(pl.program_id(1) == 0)(lambda: attend(NC))
        pl.when(pl.program_id(1) > 0)(lambda: attend(t))

    blk = pl.BlockSpec((TM, HD), lambda h, i: (i, h))
    return _call(body, name=name, grid=(NQ, t // TM),
                 in_specs=[blk, pl.BlockSpec((t, HD), lambda h, i: (0, KCOL + h // QPK)),
                           pl.BlockSpec((t, HD), lambda h, i: (0, VCOL + h // QPK)),
                           pl.BlockSpec((TM, HD), lambda h, i: (i, GCOL + h))],
                 out_specs=[blk, blk, blk],
                 out_shape=[jax.ShapeDtypeStruct((t, NQ * HD), F32), jax.ShapeDtypeStruct((t, NQ * HD), BF),
                            jax.ShapeDtypeStruct((t, NQ * HD), F32)])(qkv, qkv, qkv, p)


def attn_bwd(qkv, p, dmix, o, lse, name):
    t = qkv.shape[0]

    def body(q_ref, k_ref, v_ref, g_ref, dm_ref, o_ref, lse_ref, dq_ref, dg_ref, dk_ref, dv_ref):
        i = pl.program_id(2)

        @pl.when((pl.program_id(1) == 0) & (i == 0))
        def _():
            dk_ref[...] = jnp.zeros_like(dk_ref)
            dv_ref[...] = jnp.zeros_like(dv_ref)

        gt = g_ref[...]
        sg = _sig(gt)
        ov = o_ref[...]
        dmv = dm_ref[...]
        dg_ref[...] = (dmv * ov * (sg * (1.0 + gt * (1.0 - sg)))).astype(BF)
        do = dmv * (gt * sg)
        dr = jnp.sum(do * ov, axis=-1, keepdims=True)
        dob = do.astype(BF)

        def bwd(nk):
            q = q_ref[...]
            lse = lse_ref[:, 0:1]
            chunks = [slice(k0, min(k0 + ATT_KCHUNK, nk)) for k0 in range(0, nk, ATT_KCHUNK)]
            nxt = (_dot(q, k_ref[chunks[0], :], 1, 1), _dot(dob, v_ref[chunks[0], :], 1, 1))
            dq = None
            for ci, keys in enumerate(chunks):
                s, dp = nxt
                if ci + 1 < len(chunks):
                    nxt = (_dot(q, k_ref[chunks[ci + 1], :], 1, 1), _dot(dob, v_ref[chunks[ci + 1], :], 1, 1))
                pe = jnp.exp(s - lse)
                dsb = (pe * (dp - dr)).astype(BF)
                part = _dot(dsb, k_ref[keys, :])
                dq = part if dq is None else dq + part
                dv_ref[keys, :] += _dot(pe.astype(BF), dob, 0, 0)
                dk_ref[keys, :] += _dot(dsb, q, 0, 0)
            dq_ref[...] = dq

        pl.when(i == 0)(lambda: bwd(NC))
        pl.when(i > 0)(lambda: bwd(t))

    blk = pl.BlockSpec((TM, HD), lambda kv, g, i: (i, kv * QPK + g))
    acc = pl.BlockSpec((t, HD), lambda kv, g, i: (0, kv))
    return _call(body, name=name, grid=(NKV, QPK, t // TM),
                 in_specs=[blk, pl.BlockSpec((t, HD), lambda kv, g, i: (0, KCOL + kv)),
                           pl.BlockSpec((t, HD), lambda kv, g, i: (0, VCOL + kv)),
                           pl.BlockSpec((TM, HD), lambda kv, g, i: (i, GCOL + kv * QPK + g)), blk, blk, blk],
                 out_specs=[blk, pl.BlockSpec((TM, HD), lambda kv, g, i: (i, GCOL + kv * QPK + g)), acc, acc],
                 out_shape=[jax.ShapeDtypeStruct((t, NQ * HD), F32), jax.ShapeDtypeStruct((t, (GCOL + NQ) * HD), BF),
                            jax.ShapeDtypeStruct((t, NKV * HD), F32), jax.ShapeDtypeStruct((t, NKV * HD), F32)])(
        qkv, qkv, qkv, p, dmix, o, lse)


def _row_tile(rows, row_bytes, cap=2 * 1024 * 1024):
    if rows * row_bytes <= cap or rows % 8:
        return rows
    tr = rows
    while tr * row_bytes > cap and tr % 16 == 0:
        tr //= 2
    return tr


def _adamw_update(w_ref, g_ref, m_ref, v_ref, d_ref, nm_ref, nv_ref):
    gv = g_ref[...]
    m2 = ADAM_B1 * m_ref[...] + (1.0 - ADAM_B1) * gv
    v2 = ADAM_B2 * v_ref[...] + (1.0 - ADAM_B2) * (gv * gv)
    mh = m2 / (1.0 - ADAM_B1 ** ADAM_STEP)
    vh = v2 / (1.0 - ADAM_B2 ** ADAM_STEP)
    d_ref[...] = -ADAM_LR * (mh / (jnp.sqrt(vh) + ADAM_EPS) + ADAM_WD * w_ref[...])
    nm_ref[...] = m2
    nv_ref[...] = v2


def adamw_many(ws, gs, ms, vs, name):
    n = len(ws)

    def body(*refs):
        for k in range(n):
            _adamw_update(*[refs[j * n + k] for j in range(7)])

    shapes = [jax.ShapeDtypeStruct(w.shape, F32) for w in ws]
    res = _call(body, name=name, out_shape=shapes * 3)(*ws, *gs, *ms, *vs)
    return res[:n], res[n:2 * n], res[2 * n:]


def adamw(w, g, m, v, name):
    r, cdim = w.shape
    tr = _row_tile(r, 4 * max(cdim, 128))

    def body(w_ref, g_ref, m_ref, v_ref, d_ref, nm_ref, nv_ref):
        _adamw_update(w_ref, g_ref, m_ref, v_ref, d_ref, nm_ref, nv_ref)

    tile = pl.BlockSpec((tr, cdim), lambda i: (i, 0))
    sh = jax.ShapeDtypeStruct((r, cdim), F32)
    return _call(body, name=name, grid=(r // tr,), in_specs=[tile] * 4, out_specs=[tile] * 3,
                 out_shape=[sh, sh, sh])(w, g, m, v)


def sum_lead(a, name, out_dtype=F32):
    n, r, cdim = a.shape
    tr = _row_tile(r, 4 * n * max(cdim, 128))

    def body(a_ref, o_ref):
        acc = a_ref[0].astype(F32)
        for k in range(1, n):
            acc = acc + a_ref[k].astype(F32)
        o_ref[...] = acc.astype(o_ref.dtype)

    return _call(body, name=name, grid=(r // tr,),
                 in_specs=[pl.BlockSpec((n, tr, cdim), lambda i: (0, i, 0))],
                 out_specs=pl.BlockSpec((tr, cdim), lambda i: (i, 0)),
                 out_shape=jax.ShapeDtypeStruct((r, cdim), out_dtype))(a)


_FLIPS = {"xy": [(1, 0, 0), (0, 1, 0), (1, 1, 0)], "c": [(0, 0, 1)],
          "all": [(0, 0, 1), (0, 1, 0), (0, 1, 1), (1, 0, 0), (1, 0, 1), (1, 1, 0), (1, 1, 1)]}
_GROUP_SIZE = {"xy": 4, "c": 2, "all": 8}


def _group_index(group, x, y, c):
    return {"xy": 2 * x + y, "c": c, "all": 4 * x + 2 * y + c}[group]


def exchange(items, name):
    plan = []
    for arr, group, kind in items:
        chunk = arr.shape if kind == "gather" else arr.shape[1:]
        plan.append((group, kind, chunk))
    ncopy = sum(len(_FLIPS[g]) for g, _, _ in plan)
    nitem = len(plan)

    def body(*refs):
        srcs, dsts = refs[:nitem], refs[nitem:2 * nitem]
        send_sems, recv_sems, local_sems = refs[2 * nitem:]
        x, y, c = lax.axis_index("x"), lax.axis_index("y"), lax.axis_index("c")
        sends, recvs, locals_ = [], [], []
        n = 0
        for k, (group, kind, _) in enumerate(plan):
            me = _group_index(group, x, y, c)
            own = srcs[k] if kind == "gather" else srcs[k].at[me]
            locals_.append(pltpu.make_async_copy(own, dsts[k].at[me], local_sems.at[k]))
            for fx, fy, fc in _FLIPS[group]:
                px, py, pc = (1 - x if fx else x), (1 - y if fy else y), (1 - c if fc else c)
                peer = _group_index(group, px, py, pc)
                src = srcs[k] if kind == "gather" else srcs[k].at[peer]
                sends.append(pltpu.make_async_remote_copy(
                    src_ref=src, dst_ref=dsts[k].at[me], send_sem=send_sems.at[n], recv_sem=recv_sems.at[n],
                    device_id=(px, py, pc), device_id_type=MESH))
                recvs.append(pltpu.make_async_remote_copy(
                    src_ref=src, dst_ref=dsts[k].at[peer], send_sem=send_sems.at[n], recv_sem=recv_sems.at[n],
                    device_id=(px, py, pc), device_id_type=MESH))
                n += 1
        for cp in locals_ + sends:
            cp.start()
        for cp in recvs:
            cp.wait_recv()
        for cp in sends:
            cp.wait_send()
        for cp in locals_:
            cp.wait()

    anyspec = pl.BlockSpec(memory_space=pl.ANY)
    outs = [jax.ShapeDtypeStruct((_GROUP_SIZE[g],) + tuple(chunk), arr.dtype)
            for (arr, _, _), (g, _, chunk) in zip(items, plan)]
    res = pl.pallas_call(
        body, name=name, out_shape=outs, in_specs=[anyspec] * nitem, out_specs=[anyspec] * nitem,
        scratch_shapes=[pltpu.SemaphoreType.DMA((ncopy,)), pltpu.SemaphoreType.DMA((ncopy,)),
                        pltpu.SemaphoreType.DMA((nitem,))],
        compiler_params=pltpu.CompilerParams(has_side_effects=True))(*[a for a, _, _ in items])
    return list(res)


D2D_PIECES = 4


def d2d(items, name):
    n = len(items)
    swaps = [k for k, (_, kind) in enumerate(items) if kind == "swap"]

    def pieces_of(rows):
        npc = D2D_PIECES if rows % (8 * D2D_PIECES) == 0 else 1
        return npc, rows // npc

    ncopy = sum(pieces_of(a.shape[0] if kind == "gather" else a.shape[1])[0] for a, kind in items)

    def body(*refs):
        srcs, outs = refs[:n], refs[n:2 * n]
        stages = dict(zip(swaps, refs[2 * n:2 * n + len(swaps)]))
        send_sems, recv_sems, local_sems = refs[2 * n + len(swaps):]
        x, y, c = lax.axis_index("x"), lax.axis_index("y"), lax.axis_index("c")
        sib = (x, y, 1 - c)

        def remote(src, dst, q):
            return pltpu.make_async_remote_copy(src_ref=src, dst_ref=dst, send_sem=send_sems.at[q],
                                                recv_sem=recv_sems.at[q], device_id=sib, device_id_type=MESH)

        copies = []
        q = 0
        for k, (arr, kind) in enumerate(items):
            npc, pr = pieces_of(arr.shape[0] if kind == "gather" else arr.shape[1])
            for pc in range(npc):
                rs = pl.ds(pc * pr, pr)
                if kind == "gather":
                    mine, theirs = outs[k].at[c, rs], outs[k].at[1 - c, rs]
                    copies.append((pltpu.make_async_copy(srcs[k].at[rs], mine, local_sems.at[q]),
                                   remote(mine, mine, q), remote(theirs, theirs, q)))
                else:
                    stage, land = stages[k].at[rs], outs[k].at[rs]
                    copies.append((pltpu.make_async_copy(srcs[k].at[1 - c, rs], stage, local_sems.at[q]),
                                   remote(stage, land, q), remote(stage, land, q)))
                q += 1
        for loc, _, _ in copies:
            loc.start()
        for loc, send, _ in copies:
            loc.wait()
            send.start()
        for _, _, recv in copies:
            recv.wait_recv()
        for _, send, _ in copies:
            send.wait_send()

    outs = [jax.ShapeDtypeStruct((2,) + a.shape if kind == "gather" else a.shape[1:], a.dtype) for a, kind in items]
    res = pl.pallas_call(
        body, name=name, out_shape=outs, in_specs=[pl.BlockSpec(memory_space=pl.ANY)] * n,
        out_specs=[pl.BlockSpec(memory_space=pltpu.VMEM)] * n,
        scratch_shapes=[pltpu.VMEM(items[k][0].shape[1:], items[k][0].dtype) for k in swaps]
        + [pltpu.SemaphoreType.DMA((ncopy,)), pltpu.SemaphoreType.DMA((ncopy,)), pltpu.SemaphoreType.DMA((ncopy,))],
        compiler_params=pltpu.CompilerParams(has_side_effects=True, vmem_limit_bytes=VMEM_LIMIT_BYTES))(
        *[a for a, _ in items])
    return list(res)


def sum_own(pair, got, name, out_dtype=F32):
    _, r, cdim = pair.shape
    tr = _row_tile(r, 4 * 2 * max(cdim, 128))

    def body(c_ref, p_ref, g_ref, o_ref):
        o_ref[...] = (p_ref[0] + g_ref[...]).astype(o_ref.dtype)

    me = lax.axis_index("c").astype(jnp.int32).reshape(1)
    return pl.pallas_call(
        body, name=name, out_shape=jax.ShapeDtypeStruct((r, cdim), out_dtype),
        grid_spec=pltpu.PrefetchScalarGridSpec(
            num_scalar_prefetch=1, grid=(r // tr,),
            in_specs=[pl.BlockSpec((1, tr, cdim), lambda i, c_ref: (c_ref[0], i, 0)),
                      pl.BlockSpec((tr, cdim), lambda i, c_ref: (i, 0))],
            out_specs=pl.BlockSpec((tr, cdim), lambda i, c_ref: (i, 0))),
        compiler_params=pltpu.CompilerParams(vmem_limit_bytes=VMEM_LIMIT_BYTES))(me, pair, got)


_SMALL = ["c_ctx", "norm_g", "b_mod", "gm_v_g", "gm_w_s", "gm_b_s", "s5_lam_re", "s5_lam_im", "s5_log_dt",
          "s5_b_re", "s5_b_im", "s5_c_re", "s5_c_im", "s5_d", "s5_b_glu", "q_norm_g", "k_norm_g", "final_g"]
_BIG = ["we_in", "we_out", "s5_w_glu", "wo_in", "wo_out"]
_WEIGHTS = ["c_ctx", "norm_g", "w_mod", "b_mod", "we_in", "we_out", "gm_v_g", "gm_w_s", "gm_b_s", "s5_lam_re",
            "s5_lam_im", "s5_log_dt", "s5_b_re", "s5_b_im", "s5_c_re", "s5_c_im", "s5_d", "s5_w_glu", "s5_b_glu",
            "wo_in", "wo_out", "q_norm_g", "k_norm_g", "final_g"]
_SMALL_ALIGN = 8 * 8 * 128


def _rope_tables(n_lat):
    rows = n_lat // GRID_W
    row = jnp.repeat(jnp.arange(rows), GRID_W)
    col = jnp.tile(jnp.arange(GRID_W), rows)
    freqs = ROPE_THETA ** (-jnp.arange(HD // 4, dtype=F32) / (HD // 4))
    ar, ac = row[:, None] * freqs, col[:, None] * freqs
    cos = jnp.concatenate([jnp.cos(ar), jnp.cos(ar), jnp.cos(ac), jnp.cos(ac)], axis=1)
    sins = jnp.concatenate([-jnp.sin(ar), jnp.sin(ar), -jnp.sin(ac), jnp.sin(ac)], axis=1)
    cos = jnp.concatenate([jnp.ones((NC, HD), F32), cos], axis=0)
    sins = jnp.concatenate([jnp.zeros((NC, HD), F32), sins], axis=0)
    return cos, sins


def _block_diag(v, transpose):
    gpb = SG // NBLK
    v = v.reshape(2, NBLK, gpb, SH, SP)
    eye = jnp.eye(gpb, dtype=v.dtype)
    if transpose:
        return jnp.einsum("djahp,ab->djapbh", v, eye).reshape(2, NBLK, LN, UB)
    return jnp.einsum("djahp,ab->djahbp", v, eye).reshape(2, NBLK, UB, LN)


def _diag_blocks(m):
    gpb = SG // NBLK
    return jnp.einsum("jahap->jahp", m.reshape(NBLK, gpb, SH, gpb, SP)).reshape(SG, SH, SP)


def _view2d(a):
    if a.ndim == 1:
        return a.reshape(1, -1)
    if a.shape[-1] < 64 and a.size % 1024 == 0:
        return a.reshape(-1, 1024)
    return a.reshape(-1, a.shape[-1])


def kernel(x, c, ctx, c_ctx, norm_g, w_mod, b_mod, we_in, we_out, gm_v_g, gm_w_s, gm_b_s, s5_lam_re, s5_lam_im, s5_log_dt, s5_b_re, s5_b_im, s5_c_re, s5_c_im, s5_d, s5_w_glu, s5_b_glu, wo_in, wo_out, q_norm_g, k_norm_g, final_g, loss_target, m_c_ctx, m_norm_g, m_w_mod, m_b_mod, m_we_in, m_we_out, m_gm_v_g, m_gm_w_s, m_gm_b_s, m_s5_lam_re, m_s5_lam_im, m_s5_log_dt, m_s5_b_re, m_s5_b_im, m_s5_c_re, m_s5_c_im, m_s5_d, m_s5_w_glu, m_s5_b_glu, m_wo_in, m_wo_out, m_q_norm_g, m_k_norm_g, m_final_g, v_c_ctx, v_norm_g, v_w_mod, v_b_mod, v_we_in, v_we_out, v_gm_v_g, v_gm_w_s, v_gm_b_s, v_s5_lam_re, v_s5_lam_im, v_s5_log_dt, v_s5_b_re, v_s5_b_im, v_s5_c_re, v_s5_c_im, v_s5_d, v_s5_w_glu, v_s5_b_glu, v_wo_in, v_wo_out, v_q_norm_g, v_k_norm_g, v_final_g):
    weights = dict(c_ctx=c_ctx, norm_g=norm_g, w_mod=w_mod, b_mod=b_mod, we_in=we_in, we_out=we_out, gm_v_g=gm_v_g,
                   gm_w_s=gm_w_s, gm_b_s=gm_b_s, s5_lam_re=s5_lam_re, s5_lam_im=s5_lam_im, s5_log_dt=s5_log_dt,
                   s5_b_re=s5_b_re, s5_b_im=s5_b_im, s5_c_re=s5_c_re, s5_c_im=s5_c_im, s5_d=s5_d, s5_w_glu=s5_w_glu,
                   s5_b_glu=s5_b_glu, wo_in=wo_in, wo_out=wo_out, q_norm_g=q_norm_g, k_norm_g=k_norm_g,
                   final_g=final_g)
    mom_m = dict(c_ctx=m_c_ctx, norm_g=m_norm_g, w_mod=m_w_mod, b_mod=m_b_mod, we_in=m_we_in, we_out=m_we_out,
                 gm_v_g=m_gm_v_g, gm_w_s=m_gm_w_s, gm_b_s=m_gm_b_s, s5_lam_re=m_s5_lam_re, s5_lam_im=m_s5_lam_im,
                 s5_log_dt=m_s5_log_dt, s5_b_re=m_s5_b_re, s5_b_im=m_s5_b_im, s5_c_re=m_s5_c_re, s5_c_im=m_s5_c_im,
                 s5_d=m_s5_d, s5_w_glu=m_s5_w_glu, s5_b_glu=m_s5_b_glu, wo_in=m_wo_in, wo_out=m_wo_out,
                 q_norm_g=m_q_norm_g, k_norm_g=m_k_norm_g, final_g=m_final_g)
    mom_v = dict(c_ctx=v_c_ctx, norm_g=v_norm_g, w_mod=v_w_mod, b_mod=v_b_mod, we_in=v_we_in, we_out=v_we_out,
                 gm_v_g=v_gm_v_g, gm_w_s=v_gm_w_s, gm_b_s=v_gm_b_s, s5_lam_re=v_s5_lam_re, s5_lam_im=v_s5_lam_im,
                 s5_log_dt=v_s5_log_dt, s5_b_re=v_s5_b_re, s5_b_im=v_s5_b_im, s5_c_re=v_s5_c_re, s5_c_im=v_s5_c_im,
                 s5_d=v_s5_d, s5_w_glu=v_s5_w_glu, s5_b_glu=v_s5_b_glu, wo_in=v_wo_in, wo_out=v_wo_out,
                 q_norm_g=v_q_norm_g, k_norm_g=v_k_norm_g, final_g=v_final_g)

    ixy = 2 * lax.axis_index("x") + lax.axis_index("y")
    n_lat = x.shape[1]
    nl = norm_g.shape[0]
    nmod = w_mod.shape[2]
    xin = (ctx[0], x[0])

    ic = lax.axis_index("c")
    mine = [lax.dynamic_index_in_dim(weights[n], ic, 0, keepdims=False).astype(BF) for n in _BIG]
    got = exchange([(m_, "xy", "gather") for m_ in mine] + [(c, "xy", "gather")], "gather_weights")
    both = d2d([(g_.reshape(-1, g_.shape[-1]), "gather") for g_ in got[:len(_BIG)]], "swap_weights")
    both = [b_.reshape((2,) + g_.shape) for b_, g_ in zip(both, got)]
    wein = [(both[0], l) for l in range(2)]
    weout = [(both[1].reshape(2, 1, D, D), l) for l in range(2)]
    wglu = [(both[2].reshape(2, AW, AW), l) for l in range(2)]
    woin = [(both[3], l) for l in range(2)]
    woout = [(both[4].reshape(2, 1, D, D), l) for l in range(2)]
    c_group = got[len(_BIG)].reshape(4, D)

    cond = jnp.concatenate([c_group, jnp.broadcast_to(c_ctx.reshape(1, D), (4, D))], axis=0)
    b_shard = lax.dynamic_slice(b_mod, (0, ixy * nmod), (nl, nmod)).reshape(nl, 1, nmod)
    mpart = ada_fwd(cond, w_mod, b_shard)
    m_lat, m_ctx = exchange([(jnp.transpose(mpart[:, 0:4], (1, 0, 2)), "xy", "scatter"),
                             (mpart[:, 4], "xy", "gather")], "exchange_mod")
    m_lat = jnp.transpose(m_lat, (1, 0, 2)).reshape(nl, 3, D)
    m_ctx = jnp.transpose(m_ctx, (1, 0, 2)).reshape(nl, 3, D)
    mods = [jnp.stack([m_ctx[l], m_lat[l]], axis=0) for l in range(nl)]

    loss_part, dx, g, d_norm_g, d_mod_lat, d_mod_ctx, d_final_g = _local_step(
        xin, loss_target[0], mods, wein, weout, wglu, woin, woout, weights)
    grad_x = dx.reshape(1, n_lat, D)

    d_mod_lat, d_mod_ctx = jnp.stack(d_mod_lat), jnp.stack(d_mod_ctx)
    dm_send = jnp.stack([d_mod_lat.reshape(nl, 4, nmod), d_mod_ctx.reshape(nl, 4, nmod)])
    (dm_got,) = exchange([(jnp.transpose(dm_send, (2, 0, 1, 3)), "xy", "scatter")], "exchange_dmod")
    dm_rows = jnp.concatenate([dm_got[:, 0], dm_got[:, 1]], axis=0)
    gw_mod, d_cctx = ada_bwd(cond, jnp.transpose(dm_rows, (1, 0, 2)), w_mod)
    g_small = dict(c_ctx=d_cctx.reshape(D), norm_g=jnp.stack(d_norm_g), b_mod=add2(d_mod_lat, d_mod_ctx, "add_dbmod"),
                   final_g=d_final_g.reshape(D))
    for name in _SMALL:
        if name not in g_small:
            g_small[name] = jnp.stack(g[name])

    flat = jnp.concatenate([g_small[n].reshape(-1) for n in _SMALL])
    nflat = flat.shape[0]
    npad = -(-nflat // _SMALL_ALIGN) * _SMALL_ALIGN
    flat = jnp.concatenate([flat, jnp.zeros((npad - nflat,), F32)]).reshape(8, npad // (8 * 128), 128)
    pairs = [gw_mod.reshape(2, nl // 2 * D, nmod)]
    for name in _BIG:
        st = jnp.stack(g[name]) if isinstance(g[name], list) else g[name]
        pairs.append(st.reshape(2, -1, st.shape[-1]))
    got_a = d2d([(pairs[k], "swap") for k in (1, 2, 3)], "reduce_chip_a")
    got_b = d2d([(pairs[k], "swap") for k in (0, 4, 5)], "reduce_chip_b")
    theirs = [got_b[0]] + got_a + got_b[1:]
    chip = [sum_own(pairs[k], theirs[k], f"sum_chip{k}", F32 if k == 0 else BF) for k in range(len(pairs))]
    parts = exchange([(flat, "all", "scatter")]
                     + [(s_.reshape(4, s_.shape[0] // 4, s_.shape[1]), "xy", "scatter") for s_ in chip[1:]],
                     "reduce_scatter")
    sums = [sum_lead(pt, f"sum_shard{k}") for k, pt in enumerate(parts)]
    full = d2d([(sums[0], "gather"), (chip[0], "gather")] + [(s_, "gather") for s_ in sums[1:]], "all_gather")
    (flat_full,) = exchange([(full[0], "xy", "gather")], "gather_small")
    full = [flat_full] + full[1:]
    flat = full[0].reshape(-1)
    grads = {}
    off = 0
    for name in _SMALL:
        sz = weights[name].size
        grads[name] = flat[off:off + sz].reshape(weights[name].shape)
        off += sz
    grads["w_mod"] = full[1].reshape(w_mod.shape)
    for k, name in enumerate(_BIG):
        grads[name] = full[2 + k].reshape(weights[name].shape)

    delta, new_m, new_v = {}, {}, {}
    views = [_view2d(weights[n]) for n in _SMALL]
    ds, nms, nvs = adamw_many(views, [grads[n].reshape(w2.shape) for n, w2 in zip(_SMALL, views)],
                              [mom_m[n].reshape(w2.shape) for n, w2 in zip(_SMALL, views)],
                              [mom_v[n].reshape(w2.shape) for n, w2 in zip(_SMALL, views)], "adamw_small")
    for n, d2, m2, v2 in zip(_SMALL, ds, nms, nvs):
        shp = weights[n].shape
        delta[n], new_m[n], new_v[n] = d2.reshape(shp), m2.reshape(shp), v2.reshape(shp)
    for name in ["w_mod"] + _BIG:
        w2 = _view2d(weights[name])
        d2, m2, v2 = adamw(w2, grads[name].reshape(w2.shape), mom_m[name].reshape(w2.shape),
                           mom_v[name].reshape(w2.shape), f"adamw_{name}")
        shp = weights[name].shape
        delta[name], new_m[name], new_v[name] = d2.reshape(shp), m2.reshape(shp), v2.reshape(shp)

    loss = lax.psum(loss_part[0, 0], ("x", "y", "c"))
    return (loss, grad_x, *[grads[n] for n in _WEIGHTS], *[delta[n] for n in _WEIGHTS],
            *[new_m[n] for n in _WEIGHTS], *[new_v[n] for n in _WEIGHTS])


def _local_step(xin, target, mods, wein, weout, wglu, woin, woout, w):
    norm_g, gm_v_g, gm_w_s, gm_b_s = w["norm_g"], w["gm_v_g"], w["gm_w_s"], w["gm_b_s"]
    s5_lam_re, s5_lam_im, s5_log_dt = w["s5_lam_re"], w["s5_lam_im"], w["s5_log_dt"]
    s5_b_re, s5_b_im, s5_c_re, s5_c_im = w["s5_b_re"], w["s5_b_im"], w["s5_c_re"], w["s5_c_im"]
    s5_d, s5_b_glu, q_norm_g, k_norm_g, final_g = w["s5_d"], w["s5_b_glu"], w["q_norm_g"], w["k_norm_g"], w["final_g"]
    nl = norm_g.shape[0]
    n_lat = xin[1].shape[0]

    cos, sins = _rope_tables(n_lat)

    s5p = []
    for i in range(2):
        lam_l = (s5_lam_re[i].reshape(2, SW), s5_lam_im[i].reshape(2, SW),
                 jnp.repeat(s5_log_dt[i], SP, axis=1))
        lam_r = (jnp.repeat(s5_lam_re[i].reshape(2 * SG, SP), SH, axis=0),
                 jnp.repeat(s5_lam_im[i].reshape(2 * SG, SP), SH, axis=0),
                 jnp.repeat(s5_log_dt[i].reshape(2 * SG, 1), SH, axis=0))
        b_r = (jnp.transpose(s5_b_re[i], (0, 1, 3, 2)).reshape(2 * SG * SH, SP),
               jnp.transpose(s5_b_im[i], (0, 1, 3, 2)).reshape(2 * SG * SH, SP))
        pw_re, pw_im, bbr, bbi = s5_disc(*lam_l, *lam_r, *b_r)
        s5p.append(dict(
            lam_r=lam_r, b_r=b_r, pw_re=pw_re, pw_im=pw_im, pw_im_conj=-pw_im,
            bb_re=_block_diag(bbr.reshape(2, SG, SH, SP), False).astype(BF),
            bb_im=_block_diag(bbi.reshape(2, SG, SH, SP), False).astype(BF),
            ct_re=_block_diag(s5_c_re[i], True).astype(BF), ct_im=_block_diag(s5_c_im[i], True).astype(BF)))

    saved = []
    xcur = xin
    h = pro_fwd(xin[0], xin[1], norm_g[0].reshape(1, D), mods[0], "pro_fwd0")
    for l in range(nl):
        i = l // 2
        sv = dict(x=xcur, h=h)
        if l % 2 == 0:
            p = mm_nn(h, wein[i], f"in_proj{l}")
            sp = s5p[i]
            for dr, rev in ((0, False), (1, True)):
                sv[f"y{dr}"], sv[f"hpr{dr}"], sv[f"hpi{dr}"] = s5q_fwd(
                    p, sp["bb_re"], sp["bb_im"], sp["ct_re"], sp["ct_im"], sp["pw_re"], sp["pw_im"], dr, rev,
                    f"s5_fwd{l}_{dr}")
            mix = mix_fwd(p, sv["y0"], sv["y1"], gm_v_g[i].reshape(1, AW), gm_w_s[i].astype(BF),
                          gm_b_s[i].reshape(NGRP, CHUNK, 1), s5_d[i].reshape(1, AW), wglu[i],
                          s5_b_glu[i].reshape(1, AW), f"mix_fwd{l}")
            o = mm_nn(mix, weout[i], f"out_proj{l}")
        else:
            p = mm_nn(h, woin[i], f"in_proj{l}")
            sv["qkv"] = attn_prep(p, q_norm_g[i].reshape(1, HD), k_norm_g[i].reshape(1, HD), cos, sins, f"attn_prep{l}")
            sv["o_att"], mix, sv["lse"] = attn_fwd(sv["qkv"], p, f"attn_fwd{l}")
            o = mm_nn(mix, woout[i], f"out_proj{l}")
        sv.update(p=p, mix=mix, o=o)
        saved.append(sv)
        if l < nl - 1:
            xcur, h = res_pro_fwd(xcur, o, mods[l], True, norm_g[l + 1].reshape(1, D), mods[l + 1], f"res_pro_fwd{l}")
        else:
            xcur = res_fwd(xcur, o, mods[l], False, f"res_fwd{l}")

    loss_part, dx, d_final_g, do, dgt = final_loss(xcur, target, final_g.reshape(1, D), saved[-1]["o"], mods[-1])

    g = {}
    gbuf = {}
    d_norm_g, d_mod_lat, d_mod_ctx = [None] * nl, [None] * nl, [None] * nl
    for name in ("s5_w_glu", "gm_v_g", "gm_w_s", "gm_b_s", "s5_lam_re", "s5_lam_im",
                 "s5_log_dt", "s5_b_re", "s5_b_im", "s5_c_re", "s5_c_im", "s5_d", "s5_b_glu", "q_norm_g", "k_norm_g"):
        g[name] = [None, None]
    for l in reversed(range(nl)):
        i = l // 2
        sv = saved[l]
        w_out = weout[i] if l % 2 == 0 else woout[i]
        dmix = mm_nt(do, w_out, f"out_dgrad{l}")
        out_name, in_name = ("we_out", "we_in") if l % 2 == 0 else ("wo_out", "wo_in")
        gbuf[out_name] = mm_tn(sv["mix"], do, 1, f"out_wgrad{l}", slot=i, into=gbuf.get(out_name))
        if l % 2 == 0:
            sp = s5p[i]
            (dp, dy, g["gm_w_s"][i], dbs, dvg, dd, g["s5_w_glu"][i], dbg) = mix_bwd(
                sv["p"], sv["y0"], sv["y1"], dmix, gm_v_g[i].reshape(1, AW), gm_w_s[i].astype(BF),
                gm_b_s[i].reshape(NGRP, CHUNK, 1), s5_d[i].reshape(1, AW), wglu[i], s5_b_glu[i].reshape(1, AW),
                f"mix_bwd{l}")
            g["gm_b_s"][i], g["gm_v_g"][i] = dbs.reshape(NGRP, CHUNK), dvg.reshape(AW)
            g["s5_d"][i], g["s5_b_glu"][i] = dd.reshape(AW), dbg.reshape(AW)
            g["s5_w_glu"][i] = g["s5_w_glu"][i].reshape(4, AW // 4, AW)
            dxd, das_r, das_i, dbbs_r, dbbs_i, dcs_r, dcs_i = [], [], [], [], [], [], []
            for dr, rev in ((0, False), (1, True)):
                dxs_d, da_r, da_i, dbb_r, dbb_i, dc_r, dc_i = s5q_bwd(
                    sv["p"], sv[f"hpr{dr}"], sv[f"hpi{dr}"], dy, sp["bb_re"], sp["bb_im"], sp["ct_re"], sp["ct_im"],
                    sp["pw_re"], sp["pw_im_conj"], dr, rev, f"s5_bwd{l}_{dr}")
                dxd.append(dxs_d)
                das_r.append(jnp.repeat(da_r.reshape(SG, SP), SH, axis=0))
                das_i.append(jnp.repeat(da_i.reshape(SG, SP), SH, axis=0))
                dbbs_r.append(_diag_blocks(dbb_r).reshape(SG * SH, SP))
                dbbs_i.append(_diag_blocks(dbb_i).reshape(SG * SH, SP))
                dcs_r.append(_diag_blocks(dc_r))
                dcs_i.append(_diag_blocks(dc_i))
            cat = lambda parts: jnp.concatenate(parts, axis=0)
            dlr, dli, dldt, dbr, dbi = s5_param_bwd(*sp["lam_r"], *sp["b_r"], cat(das_r), cat(das_i),
                                                    cat(dbbs_r), cat(dbbs_i))
            g["s5_lam_re"][i], g["s5_lam_im"][i] = dlr.reshape(2, SG, SP), dli.reshape(2, SG, SP)
            g["s5_log_dt"][i] = dldt.reshape(2, SG)
            g["s5_b_re"][i] = jnp.transpose(dbr.reshape(2, SG, SH, SP), (0, 1, 3, 2))
            g["s5_b_im"][i] = jnp.transpose(dbi.reshape(2, SG, SH, SP), (0, 1, 3, 2))
            g["s5_c_re"][i], g["s5_c_im"][i] = jnp.stack(dcs_r), jnp.stack(dcs_i)
            dp = s5p_dx_sum(dy, s5_d[i].reshape(1, AW), dxd[0], dxd[1], dp, f"s5_dx_sum{l}")
            w_in = wein[i]
        else:
            dq, dp, dk, dv = attn_bwd(sv["qkv"], sv["p"], dmix, sv["o_att"], sv["lse"], f"attn_bwd{l}")
            dp, dqg, dkg = attn_prep_bwd(sv["p"], dq, dk, dv, q_norm_g[i].reshape(1, HD),
                                         k_norm_g[i].reshape(1, HD), cos, sins, dp, f"attn_prep_bwd{l}")
            g["q_norm_g"][i], g["k_norm_g"][i] = dqg.reshape(HD), dkg.reshape(HD)
            w_in = woin[i]
        dh = mm_nt(dp, w_in, f"in_dgrad{l}")
        gbuf[in_name] = mm_tn(sv["h"], dp, 4, f"in_wgrad{l}", slot=i, into=gbuf.get(in_name))
        dgt_l = dgt
        if l > 0:
            dx, dmod2, dng, do, dgt = pro_res_bwd(sv["x"], dh, dx, norm_g[l].reshape(1, D), mods[l],
                                                  saved[l - 1]["o"], mods[l - 1], f"pro_res_bwd{l}")
        else:
            dx, dmod2, dng = pro_bwd(xin[0], xin[1], dh, dx, norm_g[l].reshape(1, D), mods[l], f"pro_bwd{l}")
        d_norm_g[l] = dng.reshape(D)
        d_mod_ctx[l] = jnp.concatenate([dmod2[0, 0], dmod2[0, 1], dgt_l[0]])
        d_mod_lat[l] = jnp.concatenate([dmod2[1, 0], dmod2[1, 1], dgt_l[1]])
    g.update(gbuf)
    return loss_part, dx, g, d_norm_g, d_mod_lat, d_mod_ctx, d_final_g
```

```python
import math

import numpy as np
import jax
import jax.numpy as jnp
from jax import lax
from jax.experimental import pallas as pl
from jax.experimental.pallas import tpu as pltpu

F32 = jnp.float32
BF = jnp.bfloat16
MESH = pl.DeviceIdType.MESH

D = 1024
NC = 256
SEQ = 4096
GRID_W = 64
TM = 256
CHUNK = 128
EPS = 1e-6
HD = 128
NQ = 8
NKV = 2
ROPE_THETA = 10000.0
SG = 32
SP = 64
SH = 16
SW = SG * SP
GELU_K = math.sqrt(2.0 / math.pi)
GELU_C = 0.044715
VMEM_LIMIT_BYTES = 56 * 1024 * 1024

ADAM_LR = 0.001
ADAM_B1 = 0.9
ADAM_B2 = 0.999
ADAM_EPS = 1e-08
ADAM_WD = 0.01
ADAM_STEP = 10


def _call(body, *, name, out_shape, grid=None, in_specs=None, out_specs=None, scratch=()):
    kw = {}
    if grid is not None:
        kw["grid"] = grid
    if in_specs is not None:
        kw["in_specs"] = in_specs
    if out_specs is not None:
        kw["out_specs"] = out_specs
    return pl.pallas_call(
        body, name=name, out_shape=out_shape, scratch_shapes=list(scratch),
        compiler_params=pltpu.CompilerParams(vmem_limit_bytes=VMEM_LIMIT_BYTES), **kw)


def _dot(a, b, ca=1, cb=0):
    return lax.dot_general(a, b, (((ca,), (cb,)), ((), ())), preferred_element_type=F32)


def _sig(x):
    return 1.0 / (1.0 + jnp.exp(-x))


def _full(shape):
    n = len(shape)
    return pl.BlockSpec(shape, lambda *_: (0,) * n)


def _mm_rows(t):
    for rows in (1088, 1024, 768, 512, 256):
        if t % rows == 0:
            return rows
    raise ValueError(t)


def _layer_of(w):
    return w if isinstance(w, tuple) else (w[None], 0)


def mm_nn(a, w, name, out_dtype=F32):
    w4, layer = _layer_of(w)
    t, k = a.shape
    _, j, _, nb = w4.shape
    tr = _mm_rows(t)

    def body(a_ref, w_ref, o_ref):
        o_ref[...] = _dot(a_ref[...], w_ref[0, 0]).astype(o_ref.dtype)

    return _call(body, name=name, grid=(j, t // tr),
                 in_specs=[pl.BlockSpec((tr, k), lambda jj, i: (i, 0)),
                           pl.BlockSpec((1, 1, k, nb), lambda jj, i: (layer, jj, 0, 0))],
                 out_specs=pl.BlockSpec((tr, nb), lambda jj, i: (i, jj)),
                 out_shape=jax.ShapeDtypeStruct((t, j * nb), out_dtype))(a, w4)


def mm_nt(a, w, name, out_dtype=F32):
    w4, layer = _layer_of(w)
    t, _ = a.shape
    _, j, k, nb = w4.shape
    tr = _mm_rows(t)

    def body(a_ref, w_ref, o_ref):
        acc = _dot(a_ref[:, 0:nb], w_ref[0, 0], 1, 1)
        for jj in range(1, j):
            acc = acc + _dot(a_ref[:, jj * nb:(jj + 1) * nb], w_ref[0, jj], 1, 1)
        o_ref[...] = acc.astype(o_ref.dtype)

    return _call(body, name=name, grid=(t // tr,),
                 in_specs=[pl.BlockSpec((tr, j * nb), lambda i: (i, 0)),
                           pl.BlockSpec((1, j, k, nb), lambda i: (layer, 0, 0, 0))],
                 out_specs=pl.BlockSpec((tr, k), lambda i: (i, 0)),
                 out_shape=jax.ShapeDtypeStruct((t, k), out_dtype))(a, w4)


def mm_tn(a, b, j, name, slot=0, into=None):
    t, m = a.shape
    nb = b.shape[1] // j
    tr = _mm_rows(t)

    def body(a_ref, b_ref, *rest):
        o_ref = rest[-1]

        @pl.when(pl.program_id(1) == 0)
        def _():
            o_ref[...] = jnp.zeros_like(o_ref)
        o_ref[0, 0] += _dot(a_ref[...], b_ref[...], 0, 0)

    in_specs = [pl.BlockSpec((tr, m), lambda jj, i: (i, 0)), pl.BlockSpec((tr, nb), lambda jj, i: (i, jj))]
    args = [a, b]
    alias = {}
    if into is not None:
        in_specs.append(pl.BlockSpec(memory_space=pl.ANY))
        args.append(into)
        alias = {2: 0}
    return pl.pallas_call(
        body, name=name, grid=(j, t // tr), in_specs=in_specs,
        out_specs=pl.BlockSpec((1, 1, m, nb), lambda jj, i: (slot, jj, 0, 0)),
        out_shape=jax.ShapeDtypeStruct((2, j, m, nb), F32), input_output_aliases=alias,
        compiler_params=pltpu.CompilerParams(vmem_limit_bytes=VMEM_LIMIT_BYTES))(*args)


def _mod_rows(mod_ref, i):
    ctx = i == 0
    sh = jnp.where(ctx, mod_ref[0, 0:1, :], mod_ref[1, 0:1, :])
    sc = jnp.where(ctx, mod_ref[0, 1:2, :], mod_ref[1, 1:2, :])
    gt = jnp.where(ctx, mod_ref[0, 2:3, :], mod_ref[1, 2:3, :])
    return sh, sc, gt


def _split_specs():
    return [pl.BlockSpec((TM, D), lambda i: (0, 0)), pl.BlockSpec((TM, D), lambda i: (jnp.maximum(i - 1, 0), 0))]


def _split_tile(c_ref, l_ref, i):
    return jnp.where(i == 0, c_ref[...], l_ref[...])


def pro_fwd(ctx, lat, g, mod, name):
    t = ctx.shape[0] + lat.shape[0]

    def body(c_ref, l_ref, g_ref, mod_ref, h_ref):
        i = pl.program_id(0)
        sh, sc, _ = _mod_rows(mod_ref, i)
        xv = _split_tile(c_ref, l_ref, i)
        r = lax.rsqrt(jnp.mean(xv * xv, axis=-1, keepdims=True) + EPS)
        h_ref[...] = ((xv * r) * g_ref[...] * (1.0 + sc) + sh).astype(BF)

    return _call(body, name=name, grid=(t // TM,),
                 in_specs=_split_specs() + [_full((1, D)), _full((2, 3, D))],
                 out_specs=pl.BlockSpec((TM, D), lambda i: (i, 0)),
                 out_shape=jax.ShapeDtypeStruct((t, D), BF))(ctx, lat, g, mod)


def pro_bwd(ctx, lat, dh, dxn, g, mod, name):
    t = ctx.shape[0] + lat.shape[0]

    def body(c_ref, l_ref, dh_ref, dxn_ref, g_ref, mod_ref, dx_ref, dmod_ref, dg_ref):
        i = pl.program_id(0)

        @pl.when(i == 0)
        def _():
            dmod_ref[...] = jnp.zeros_like(dmod_ref)
            dg_ref[...] = jnp.zeros_like(dg_ref)

        _, sc, _ = _mod_rows(mod_ref, i)
        xv = _split_tile(c_ref, l_ref, i)
        gv = g_ref[...]
        r = lax.rsqrt(jnp.mean(xv * xv, axis=-1, keepdims=True) + EPS)
        xn = xv * r
        dh_v = dh_ref[...]
        e = dh_v * (1.0 + sc)
        dsh = jnp.sum(dh_v, axis=0, keepdims=True)
        dsc = jnp.sum(dh_v * xn * gv, axis=0, keepdims=True)
        dg_ref[...] += jnp.sum(e * xn, axis=0, keepdims=True)
        dxh = e * gv

        @pl.when(i == 0)
        def _():
            dmod_ref[0, 0:1, :] += dsh
            dmod_ref[0, 1:2, :] += dsc

        @pl.when(i > 0)
        def _():
            dx_ref[...] = dxn_ref[...] + r * (dxh - xn * jnp.mean(dxh * xn, axis=-1, keepdims=True))
            dmod_ref[1, 0:1, :] += dsh
            dmod_ref[1, 1:2, :] += dsc

    tile = pl.BlockSpec((TM, D), lambda i: (i, 0))
    return _call(body, name=name, grid=(t // TM,),
                 in_specs=_split_specs() + [tile, tile, _full((1, D)), _full((2, 3, D))],
                 out_specs=[_split_specs()[1], _full((2, 2, D)), _full((1, D))],
                 out_shape=[jax.ShapeDtypeStruct(lat.shape, F32), jax.ShapeDtypeStruct((2, 2, D), F32),
                            jax.ShapeDtypeStruct((1, D), F32)])(ctx, lat, dh, dxn, g, mod)


def res_fwd(x, o, mod, update_ctx, name):
    t = x.shape[0]

    def body(x_ref, o_ref, mod_ref, y_ref):
        i = pl.program_id(0)
        _, _, gt = _mod_rows(mod_ref, i)
        upd = x_ref[...] + gt * o_ref[...]
        if update_ctx:
            y_ref[...] = upd
        else:
            y_ref[...] = jnp.where(i == 0, x_ref[...], upd)

    tile = pl.BlockSpec((TM, D), lambda i: (i, 0))
    return _call(body, name=name, grid=(t // TM,), in_specs=[tile, tile, _full((2, 3, D))],
                 out_specs=tile, out_shape=jax.ShapeDtypeStruct((t, D), F32))(x, o, mod)


def res_pro_fwd(x, o, mod, update_ctx, g_next, mod_next, name):
    split = isinstance(x, tuple)
    xs = list(x) if split else [x]
    t = o.shape[0]

    def body(*refs):
        x_refs = refs[:len(xs)]
        o_ref, mod_ref, g_ref, modn_ref, y_ref, h_ref = refs[len(xs):]
        i = pl.program_id(0)
        _, _, gt = _mod_rows(mod_ref, i)
        xv = _split_tile(x_refs[0], x_refs[1], i) if split else x_refs[0][...]
        xn = xv + gt * o_ref[...]
        if not update_ctx:
            xn = jnp.where(i == 0, xv, xn)
        y_ref[...] = xn
        sh, sc, _ = _mod_rows(modn_ref, i)
        r = lax.rsqrt(jnp.mean(xn * xn, axis=-1, keepdims=True) + EPS)
        h_ref[...] = ((xn * r) * g_ref[...] * (1.0 + sc) + sh).astype(BF)

    tile = pl.BlockSpec((TM, D), lambda i: (i, 0))
    return _call(body, name=name, grid=(t // TM,),
                 in_specs=(_split_specs() if split else [tile]) + [tile, _full((2, 3, D)), _full((1, D)), _full((2, 3, D))],
                 out_specs=[tile, tile],
                 out_shape=[jax.ShapeDtypeStruct((t, D), F32), jax.ShapeDtypeStruct((t, D), BF)])(
        *xs, o, mod, g_next, mod_next)


def _res_bwd_part(dx, o_ref, modp_ref, do_ref, dgt_ref, i, update_ctx):
    _, _, gtp = _mod_rows(modp_ref, i)
    do = gtp * dx
    dgt = jnp.sum(dx * o_ref[...], axis=0, keepdims=True)
    if update_ctx:
        do_ref[...] = do.astype(BF)

        @pl.when(i == 0)
        def _():
            dgt_ref[0:1, :] += dgt
    else:
        do_ref[...] = jnp.where(i == 0, jnp.zeros_like(do), do).astype(BF)

    @pl.when(i > 0)
    def _():
        dgt_ref[1:2, :] += dgt


def pro_res_bwd(x, dh, dxn, g, mod, o_prev, mod_prev, name):
    t = x.shape[0]

    def body(x_ref, dh_ref, dxn_ref, g_ref, mod_ref, o_ref, modp_ref, dx_ref, dmod_ref, dg_ref, do_ref, dgt_ref):
        i = pl.program_id(0)

        @pl.when(i == 0)
        def _():
            dmod_ref[...] = jnp.zeros_like(dmod_ref)
            dg_ref[...] = jnp.zeros_like(dg_ref)
            dgt_ref[...] = jnp.zeros_like(dgt_ref)

        _, sc, _ = _mod_rows(mod_ref, i)
        xv = x_ref[...]
        gv = g_ref[...]
        r = lax.rsqrt(jnp.mean(xv * xv, axis=-1, keepdims=True) + EPS)
        xn = xv * r
        dh_v = dh_ref[...]
        e = dh_v * (1.0 + sc)
        dsh = jnp.sum(dh_v, axis=0, keepdims=True)
        dsc = jnp.sum(dh_v * xn * gv, axis=0, keepdims=True)
        dg_ref[...] += jnp.sum(e * xn, axis=0, keepdims=True)
        dxh = e * gv
        dx = dxn_ref[...] + r * (dxh - xn * jnp.mean(dxh * xn, axis=-1, keepdims=True))
        dx_ref[...] = dx

        @pl.when(i == 0)
        def _():
            dmod_ref[0, 0:1, :] += dsh
            dmod_ref[0, 1:2, :] += dsc

        @pl.when(i > 0)
        def _():
            dmod_ref[1, 0:1, :] += dsh
            dmod_ref[1, 1:2, :] += dsc

        _res_bwd_part(dx, o_ref, modp_ref, do_ref, dgt_ref, i, True)

    tile = pl.BlockSpec((TM, D), lambda i: (i, 0))
    return _call(body, name=name, grid=(t // TM,),
                 in_specs=[tile, tile, tile, _full((1, D)), _full((2, 3, D)), tile, _full((2, 3, D))],
                 out_specs=[tile, _full((2, 2, D)), _full((1, D)), tile, _full((2, D))],
                 out_shape=[jax.ShapeDtypeStruct((t, D), F32), jax.ShapeDtypeStruct((2, 2, D), F32),
                            jax.ShapeDtypeStruct((1, D), F32), jax.ShapeDtypeStruct((t, D), BF),
                            jax.ShapeDtypeStruct((2, D), F32)])(x, dh, dxn, g, mod, o_prev, mod_prev)


def final_loss(x, target, g, o_last, mod_last):
    t = x.shape[0]

    def body(x_ref, t_ref, g_ref, o_ref, modp_ref, loss_ref, dx_ref, dg_ref, do_ref, dgt_ref):
        i = pl.program_id(0)

        @pl.when(i == 0)
        def _():
            loss_ref[...] = jnp.zeros_like(loss_ref)
            dg_ref[...] = jnp.zeros_like(dg_ref)
            dx_ref[...] = jnp.zeros_like(dx_ref)
            do_ref[...] = jnp.zeros_like(do_ref)
            dgt_ref[...] = jnp.zeros_like(dgt_ref)

        @pl.when(i > 0)
        def _():
            xv = x_ref[...]
            gv = g_ref[...]
            r = lax.rsqrt(jnp.mean(xv * xv, axis=-1, keepdims=True) + EPS)
            xn = xv * r
            err = xn * gv - t_ref[...]
            loss_ref[...] += (0.5 / D) * jnp.sum(jnp.sum(err * err, axis=1, keepdims=True), axis=0, keepdims=True)
            dy = err * (1.0 / D)
            dg_ref[...] += jnp.sum(dy * xn, axis=0, keepdims=True)
            dxh = dy * gv
            dx = r * (dxh - xn * jnp.mean(dxh * xn, axis=-1, keepdims=True))
            dx_ref[...] = dx
            do_ref[...] = (modp_ref[1, 2:3, :] * dx).astype(BF)
            dgt_ref[1:2, :] += jnp.sum(dx * o_ref[...], axis=0, keepdims=True)

    tile = pl.BlockSpec((TM, D), lambda i: (i, 0))
    return _call(body, name="final_loss", grid=(t // TM,),
                 in_specs=[tile, pl.BlockSpec((TM, D), lambda i: (jnp.maximum(i - 1, 0), 0)), _full((1, D)), tile,
                           _full((2, 3, D))],
                 out_specs=[_full((1, 1)), tile, _full((1, D)), tile, _full((2, D))],
                 out_shape=[jax.ShapeDtypeStruct((1, 1), F32), jax.ShapeDtypeStruct((t, D), F32),
                            jax.ShapeDtypeStruct((1, D), F32), jax.ShapeDtypeStruct((t, D), BF),
                            jax.ShapeDtypeStruct((2, D), F32)])(x, target, g, o_last, mod_last)


def ada_fwd(cond, w_mod, b_mod):
    nl, _, nw = w_mod.shape

    def body(c_ref, w_ref, b_ref, o_ref):
        cv = c_ref[...]
        s = (cv * _sig(cv)).astype(BF)
        o_ref[0] = _dot(s, w_ref[0].astype(BF)) + b_ref[0]

    return _call(body, name="ada_fwd", grid=(nl,),
                 in_specs=[_full((8, D)), pl.BlockSpec((1, D, nw), lambda l: (l, 0, 0)),
                           pl.BlockSpec((1, 1, nw), lambda l: (l, 0, 0))],
                 out_specs=pl.BlockSpec((1, 8, nw), lambda l: (l, 0, 0)),
                 out_shape=jax.ShapeDtypeStruct((nl, 8, nw), F32))(cond, w_mod, b_mod)


def ada_bwd(cond, dm, w_mod):
    nl, _, nw = w_mod.shape

    def body(c_ref, dm_ref, w_ref, gw_ref, dcc_ref, dc_ref):
        l = pl.program_id(0)

        @pl.when(l == 0)
        def _():
            dc_ref[...] = jnp.zeros_like(dc_ref)

        cv = c_ref[...]
        sg = _sig(cv)
        s = (cv * sg).astype(BF)
        dmv = dm_ref[0].astype(BF)
        gw_ref[0] = _dot(s, dmv, 0, 0)
        dc_ref[...] += _dot(dmv, w_ref[0].astype(BF), 1, 1)

        @pl.when(l == nl - 1)
        def _():
            dcond = dc_ref[...] * (sg * (1.0 + cv * (1.0 - sg)))
            dcc_ref[...] = jnp.sum(dcond[4:8], axis=0, keepdims=True)

    return _call(body, name="ada_bwd", grid=(nl,),
                 in_specs=[_full((8, D)), pl.BlockSpec((1, 8, nw), lambda l: (l, 0, 0)),
                           pl.BlockSpec((1, D, nw), lambda l: (l, 0, 0))],
                 out_specs=[pl.BlockSpec((1, D, nw), lambda l: (l, 0, 0)), _full((1, D))],
                 out_shape=[jax.ShapeDtypeStruct((nl, D, nw), F32), jax.ShapeDtypeStruct((1, D), F32)],
                 scratch=[pltpu.VMEM((8, D), F32)])(cond, dm, w_mod)


def add2(a, b, name):
    def body(a_ref, b_ref, o_ref):
        o_ref[...] = a_ref[...] + b_ref[...]

    return _call(body, name=name, out_shape=jax.ShapeDtypeStruct(a.shape, a.dtype))(a, b)


AW = 512
NGRP = 4


def Y4_SPEC():
    return pl.BlockSpec((AW // 128, TM, 128), lambda i: (0, i, 0))


def _cat_lanes(ref):
    return jnp.concatenate([ref[q] for q in range(ref.shape[0])], axis=1)


def _gelu(y):
    t = jnp.tanh(GELU_K * (y + GELU_C * y * y * y))
    return 0.5 * y * (1.0 + t), t


def _layer_norm_stats(v):
    mu = jnp.mean(v, axis=-1, keepdims=True)
    vc = v - mu
    rstd = lax.rsqrt(jnp.mean(vc * vc, axis=-1, keepdims=True) + EPS)
    return vc * rstd, rstd


def _spatial_mix(vn_ref, ws_ref, bs_ref, mixed_ref):
    for ch in range(TM // CHUNK):
        rows = slice(ch * CHUNK, (ch + 1) * CHUNK)
        for g in range(NGRP):
            cols = slice(g * CHUNK, (g + 1) * CHUNK)
            mixed_ref[rows, cols] = _dot(ws_ref[g], vn_ref[rows, cols]) + bs_ref[g]


def mix_fwd(p, yf, yb, vg, ws, bs, dsk, wglu, bglu, name):
    t = p.shape[0]
    wglu, glu_layer = _layer_of(wglu)

    def body(p_ref, yf_ref, yb_ref, vg_ref, ws_ref, bs_ref, d_ref, wg_ref, bg_ref, o_ref, vn_ref, mixed_ref):
        vhat, _ = _layer_norm_stats(p_ref[:, AW:2 * AW])
        vn_ref[...] = (vhat * vg_ref[...]).astype(BF)
        _spatial_mix(vn_ref, ws_ref, bs_ref, mixed_ref)
        ga = p_ref[:, 2 * AW:3 * AW]
        o_ref[:, 0:AW] = (p_ref[:, 0:AW] * mixed_ref[...] * (ga * _sig(ga))).astype(BF)
        y = _cat_lanes(yf_ref) + _cat_lanes(yb_ref) + d_ref[...] * p_ref[:, 3 * AW:4 * AW]
        y2, _ = _gelu(y)
        z = _dot(y2.astype(BF), wg_ref[0]) + bg_ref[...]
        gb = p_ref[:, 4 * AW:5 * AW]
        o_ref[:, AW:2 * AW] = (y2 * _sig(z) * (gb * _sig(gb))).astype(BF)

    tile = lambda w: pl.BlockSpec((TM, w), lambda i: (i, 0))
    return _call(body, name=name, grid=(t // TM,),
                 in_specs=[tile(5 * AW), Y4_SPEC(), Y4_SPEC(), _full((1, AW)), _full((NGRP, CHUNK, CHUNK)),
                           _full((NGRP, CHUNK, 1)), _full((1, AW)),
                           pl.BlockSpec((1, AW, AW), lambda i: (glu_layer, 0, 0)), _full((1, AW))],
                 out_specs=tile(2 * AW), out_shape=jax.ShapeDtypeStruct((t, 2 * AW), BF),
                 scratch=[pltpu.VMEM((TM, AW), BF), pltpu.VMEM((TM, AW), F32)])(p, yf, yb, vg, ws, bs, dsk, wglu, bglu)


def mix_bwd(p, yf, yb, dmix, vg, ws, bs, dsk, wglu, bglu, name):
    t = p.shape[0]
    wglu, glu_layer = _layer_of(wglu)

    def body(p_ref, yf_ref, yb_ref, dm_ref, vg_ref, ws_ref, bs_ref, d_ref, wg_ref, bg_ref,
             dpa_ref, dy_ref, dws_ref, dbs_ref, dvg_ref, dd_ref, dwg_ref, dbg_ref,
             vn_ref, mixed_ref, dmx_ref, dvn_ref):
        @pl.when(pl.program_id(0) == 0)
        def _():
            for r in (dws_ref, dbs_ref, dvg_ref, dd_ref, dwg_ref, dbg_ref):
                r[...] = jnp.zeros_like(r)

        vhat, rstd = _layer_norm_stats(p_ref[:, AW:2 * AW])
        vgv = vg_ref[...]
        vn_ref[...] = (vhat * vgv).astype(BF)
        _spatial_mix(vn_ref, ws_ref, bs_ref, mixed_ref)
        u = p_ref[:, 0:AW]
        ga = p_ref[:, 2 * AW:3 * AW]
        sga = _sig(ga)
        dya = dm_ref[:, 0:AW]
        mixed = mixed_ref[...]
        dpa_ref[:, 0:AW] = (dya * mixed * (ga * sga)).astype(BF)
        dpa_ref[:, 2 * AW:3 * AW] = (dya * u * mixed * (sga * (1.0 + ga * (1.0 - sga)))).astype(BF)
        dmx_ref[...] = dya * u * (ga * sga)
        for ch in range(TM // CHUNK):
            rows = slice(ch * CHUNK, (ch + 1) * CHUNK)
            for g in range(NGRP):
                cols = slice(g * CHUNK, (g + 1) * CHUNK)
                dmx = dmx_ref[rows, cols]
                dmxb = dmx.astype(BF)
                dws_ref[g] += _dot(dmxb, vn_ref[rows, cols], 1, 1)
                dbs_ref[g] += jnp.sum(dmx, axis=1, keepdims=True)
                dvn_ref[rows, cols] = _dot(ws_ref[g], dmxb, 0, 0)
        dvn = dvn_ref[...]
        dvg_ref[...] += jnp.sum(dvn * vhat, axis=0, keepdims=True)
        dvh = dvn * vgv
        dpa_ref[:, AW:2 * AW] = (rstd * (dvh - jnp.mean(dvh, axis=-1, keepdims=True)
                                         - vhat * jnp.mean(dvh * vhat, axis=-1, keepdims=True))).astype(BF)

        xs = p_ref[:, 3 * AW:4 * AW]
        y = _cat_lanes(yf_ref) + _cat_lanes(yb_ref) + d_ref[...] * xs
        y2, th = _gelu(y)
        y2b = y2.astype(BF)
        z = _dot(y2b, wg_ref[0]) + bg_ref[...]
        sz = _sig(z)
        gb = p_ref[:, 4 * AW:5 * AW]
        sgb = _sig(gb)
        dyb = dm_ref[:, AW:2 * AW]
        dpa_ref[:, 4 * AW:5 * AW] = (dyb * (y2 * sz) * (sgb * (1.0 + gb * (1.0 - sgb)))).astype(BF)
        dy3 = dyb * (gb * sgb)
        dz = dy3 * y2 * sz * (1.0 - sz)
        dzb = dz.astype(BF)
        dwg_ref[...] += _dot(y2b, dzb, 0, 0)
        dbg_ref[...] += jnp.sum(dz, axis=0, keepdims=True)
        dy2 = dy3 * sz + _dot(dzb, wg_ref[0], 1, 1)
        dgelu = 0.5 * (1.0 + th) + 0.5 * y * (1.0 - th * th) * GELU_K * (1.0 + 3.0 * GELU_C * y * y)
        dy = dy2 * dgelu
        dd_ref[...] += jnp.sum(dy * xs, axis=0, keepdims=True)
        dy_ref[...] = dy

    tile = lambda w: pl.BlockSpec((TM, w), lambda i: (i, 0))
    return _call(body, name=name, grid=(t // TM,),
                 in_specs=[tile(5 * AW), Y4_SPEC(), Y4_SPEC(), tile(2 * AW), _full((1, AW)), _full((NGRP, CHUNK, CHUNK)),
                           _full((NGRP, CHUNK, 1)), _full((1, AW)),
                           pl.BlockSpec((1, AW, AW), lambda i: (glu_layer, 0, 0)), _full((1, AW))],
                 out_specs=[tile(5 * AW), tile(AW), _full((NGRP, CHUNK, CHUNK)), _full((NGRP, CHUNK, 1)),
                            _full((1, AW)), _full((1, AW)), _full((AW, AW)), _full((1, AW))],
                 out_shape=[jax.ShapeDtypeStruct((t, 5 * AW), BF),
                            jax.ShapeDtypeStruct((t, AW), F32), jax.ShapeDtypeStruct((NGRP, CHUNK, CHUNK), F32),
                            jax.ShapeDtypeStruct((NGRP, CHUNK, 1), F32), jax.ShapeDtypeStruct((1, AW), F32),
                            jax.ShapeDtypeStruct((1, AW), F32), jax.ShapeDtypeStruct((AW, AW), F32),
                            jax.ShapeDtypeStruct((1, AW), F32)],
                 scratch=[pltpu.VMEM((TM, AW), BF), pltpu.VMEM((TM, AW), F32), pltpu.VMEM((TM, AW), F32),
                          pltpu.VMEM((TM, AW), F32)])(p, yf, yb, dmix, vg, ws, bs, dsk, wglu, bglu)


LN = 512
NBLK = SW // LN
UB = AW // NBLK
SCAN_R = 32
SCAN_G = TM // SCAN_R
PW_ROWS = SCAN_R
POW_EXP = list(range(1, SCAN_R + 1))


def s5_disc(lam_re, lam_im, dt, lam_re_r, lam_im_r, dt_r, b_re, b_im):
    nexp = jnp.asarray(np.array(POW_EXP, np.float32).reshape(PW_ROWS, 1))

    def body(n_ref, lr_ref, li_ref, dt_ref, lrr_ref, lir_ref, dtr_ref, br_ref, bi_ref,
             pr_ref, pi_ref, bbr_ref, bbi_ref):
        for dr in range(2):
            dtl = jnp.exp(dt_ref[dr:dr + 1, :])
            zr = lr_ref[dr:dr + 1, :] * dtl
            zi = li_ref[dr:dr + 1, :] * dtl
            mag = jnp.exp(n_ref[...] * zr)
            ang = n_ref[...] * zi
            pr_ref[dr] = mag * jnp.cos(ang)
            pi_ref[dr] = mag * jnp.sin(ang)
        lr, li, dtv = lrr_ref[...], lir_ref[...], jnp.exp(dtr_ref[...])
        mag = jnp.exp(lr * dtv)
        nr = mag * jnp.cos(li * dtv) - 1.0
        ni = mag * jnp.sin(li * dtv)
        den = lr * lr + li * li
        fr = (nr * lr + ni * li) / den
        fi = (ni * lr - nr * li) / den
        bbr_ref[...] = fr * br_ref[...] - fi * bi_ref[...]
        bbi_ref[...] = fr * bi_ref[...] + fi * br_ref[...]

    rows = lam_re_r.shape[0]
    return _call(body, name="s5_disc",
                 out_shape=[jax.ShapeDtypeStruct((2, PW_ROWS, SW), F32), jax.ShapeDtypeStruct((2, PW_ROWS, SW), F32),
                            jax.ShapeDtypeStruct((rows, SP), F32), jax.ShapeDtypeStruct((rows, SP), F32)])(
        nexp, lam_re, lam_im, dt, lam_re_r, lam_im_r, dt_r, b_re, b_im)


def s5_param_bwd(lam_re_r, lam_im_r, dt_r, b_re, b_im, da_re, da_im, dbb_re, dbb_im):
    rows = lam_re_r.shape[0]
    ng = rows // SH
    seg = jnp.asarray(np.kron(np.eye(ng, dtype=np.float32), np.ones((1, SH), np.float32)))

    def body(seg_ref, lr_ref, li_ref, dt_ref, br_ref, bi_ref, dar_ref, dai_ref, dbbr_ref, dbbi_ref,
             dlr_ref, dli_ref, ddt_ref, dbr_ref, dbi_ref):
        lr, li, dtv = lr_ref[...], li_ref[...], jnp.exp(dt_ref[...])
        mag = jnp.exp(lr * dtv)
        lbr = mag * jnp.cos(li * dtv)
        lbi = mag * jnp.sin(li * dtv)
        den = lr * lr + li * li
        nr, ni = lbr - 1.0, lbi
        fr = (nr * lr + ni * li) / den
        fi = (ni * lr - nr * li) / den
        br, bi = br_ref[...], bi_ref[...]
        gbr, gbi = dbbr_ref[...], dbbi_ref[...]
        dbr_ref[...] = gbr * fr + gbi * fi
        dbi_ref[...] = gbi * fr - gbr * fi
        gfr = gbr * br + gbi * bi
        gfi = gbi * br - gbr * bi
        ilr, ili = lr / den, -li / den
        gnr = gfr * ilr + gfi * ili
        gni = gfi * ilr - gfr * ili
        qr = -(fr * ilr - fi * ili)
        qi = -(fr * ili + fi * ilr)
        glr = gfr * qr + gfi * qi
        gli = gfi * qr - gfr * qi
        first = (lax.broadcasted_iota(jnp.int32, (rows, 1), 0) % SH) == 0
        glbr = gnr + jnp.where(first, dar_ref[...], 0.0)
        glbi = gni + jnp.where(first, dai_ref[...], 0.0)
        gzr = glbr * lbr + glbi * lbi
        gzi = glbi * lbr - glbr * lbi
        glr = glr + gzr * dtv
        gli = gli + gzi * dtv
        gdt = (gzr * lr + gzi * li) * dtv
        hi = lax.Precision.HIGHEST
        sg = seg_ref[...]
        dlr_ref[...] = jnp.dot(sg, glr, precision=hi, preferred_element_type=F32)
        dli_ref[...] = jnp.dot(sg, gli, precision=hi, preferred_element_type=F32)
        ddt_ref[...] = jnp.sum(jnp.dot(sg, gdt, precision=hi, preferred_element_type=F32), axis=1, keepdims=True)

    return _call(body, name="s5_param_bwd",
                 out_shape=[jax.ShapeDtypeStruct((ng, SP), F32), jax.ShapeDtypeStruct((ng, SP), F32),
                            jax.ShapeDtypeStruct((ng, 1), F32), jax.ShapeDtypeStruct((rows, SP), F32),
                            jax.ShapeDtypeStruct((rows, SP), F32)])(
        seg, lam_re_r, lam_im_r, dt_r, b_re, b_im, da_re, da_im, dbb_re, dbb_im)


def _tile_order(kind, nt):
    if kind == "fwd":
        return lambda i: i
    if kind == "bwd":
        return lambda i: jnp.where(i == 0, 0, nt - i)
    if kind == "fwd_adj":
        return lambda i: nt - 1 - i
    if kind == "bwd_adj":
        return lambda i: jnp.where(i == nt - 1, 0, i + 1)
    raise ValueError(kind)


XS_BLK = 3 * AW // 128


def _load_perm(refs):
    return jnp.concatenate(
        [jnp.concatenate([ref[pl.ds(r, SCAN_G, stride=SCAN_R), :] for ref in refs], axis=1) for r in range(SCAN_R)],
        axis=0)


def _store_perm(out_ref, val):
    for r in range(SCAN_R):
        for q in range(AW // 128):
            out_ref[q, pl.ds(r, SCAN_G, stride=SCAN_R), :] = val[r * SCAN_G:(r + 1) * SCAN_G, q * 128:(q + 1) * 128]


def _scan2(br_ref, bi_ref, or_ref, oi_ref, h_off, cin_off, er_ref, ei_ref, cr_ref, ci_ref, pr_ref, pi_ref, reverse,
           corr=None):
    gpt = SCAN_G
    nr = SCAN_R
    offsets = list(range(nr))[::-1] if reverse else list(range(nr))
    blocks = [slice(b * LN, (b + 1) * LN) for b in range(NBLK)]
    slab = lambda r: slice(r * gpt, (r + 1) * gpt)
    a1 = [(pr_ref[0:1, c], pi_ref[0:1, c]) for c in blocks]
    x = [None] * NBLK
    for r in offsets:
        for b, c in enumerate(blocks):
            if x[b] is None:
                x[b] = (br_ref[slab(r), c], bi_ref[slab(r), c])
            else:
                (ar, ai), (xr, xi) = a1[b], x[b]
                x[b] = (br_ref[slab(r), c] + ar * xr - ai * xi, bi_ref[slab(r), c] + ar * xi + ai * xr)
    an = [(pr_ref[nr - 1:nr, c], pi_ref[nr - 1:nr, c]) for c in blocks]
    k = [(cr_ref[:, c], ci_ref[:, c]) for c in blocks]
    for g in (range(gpt - 1, -1, -1) if reverse else range(gpt)):
        for b, c in enumerate(blocks):
            (ar, ai), (kr, ki), (xr, xi) = an[b], k[b], x[b]
            er_ref[g:g + 1, c] = kr
            ei_ref[g:g + 1, c] = ki
            k[b] = (xr[g:g + 1, :] + ar * kr - ai * ki, xi[g:g + 1, :] + ar * ki + ai * kr)
    for b, c in enumerate(blocks):
        cr_ref[:, c] = k[b][0]
        ci_ref[:, c] = k[b][1]
        x[b] = (er_ref[:, c], ei_ref[:, c])
        if cin_off is not None:
            or_ref[cin_off:cin_off + gpt, c] = x[b][0]
            oi_ref[cin_off:cin_off + gpt, c] = x[b][1]
    acc = [None] * NBLK
    for r in offsets:
        for b, c in enumerate(blocks):
            (ar, ai), (xr, xi) = a1[b], x[b]
            x[b] = (br_ref[slab(r), c] + ar * xr - ai * xi, bi_ref[slab(r), c] + ar * xi + ai * xr)
            or_ref[h_off + r * gpt:h_off + (r + 1) * gpt, c] = x[b][0]
            oi_ref[h_off + r * gpt:h_off + (r + 1) * gpt, c] = x[b][1]
            if corr is not None:
                wr_ref, wi_ref, w_off = corr[:3]
                wr, wi = wr_ref[w_off + r * gpt:w_off + (r + 1) * gpt, c], wi_ref[w_off + r * gpt:w_off + (r + 1) * gpt, c]
                pr_, pi_ = x[b][0] * wr + x[b][1] * wi, x[b][1] * wr - x[b][0] * wi
                acc[b] = (pr_, pi_) if acc[b] is None else (acc[b][0] + pr_, acc[b][1] + pi_)
    if corr is not None:
        sr_ref, si_ref = corr[3:]
        for b, c in enumerate(blocks):
            sr_ref[:, c] += jnp.sum(acc[b][0], axis=0, keepdims=True)
            si_ref[:, c] += jnp.sum(acc[b][1], axis=0, keepdims=True)


HS_ROWS = TM + SCAN_G


def _hs_offsets(reverse):
    return (0, SCAN_G) if reverse else (SCAN_G, 0)


def _dir_spec(dr, shape):
    return pl.BlockSpec((1,) + shape, lambda i: (dr,) + (0,) * len(shape))


def s5q_fwd(p, bb_re, bb_im, ct_re, ct_im, pw_re, pw_im, dr, reverse, name):
    t = p.shape[0]
    nt = t // TM
    order = _tile_order("bwd" if reverse else "fwd", nt)
    nq = AW // 128
    h_off, p_off = _hs_offsets(reverse)

    def body(*refs):
        x_refs = refs[:nq]
        bbr_ref, bbi_ref, ctr_ref, cti_ref, pr_ref, pi_ref = [r.at[0] for r in refs[nq:nq + 6]]
        y_ref, hsr_ref, hsi_ref = refs[nq + 6:nq + 9]
        br_ref, bi_ref, er_ref, ei_ref, cr_ref, ci_ref = refs[nq + 9:]

        @pl.when(pl.program_id(0) == 0)
        def _():
            cr_ref[...] = jnp.zeros_like(cr_ref)
            ci_ref[...] = jnp.zeros_like(ci_ref)

        xb = _load_perm(x_refs).astype(BF)
        for j in range(NBLK):
            cols = slice(j * LN, (j + 1) * LN)
            br_ref[:, cols] = _dot(xb[:, j * UB:(j + 1) * UB], bbr_ref[j])
            bi_ref[:, cols] = _dot(xb[:, j * UB:(j + 1) * UB], bbi_ref[j])
        _scan2(br_ref, bi_ref, hsr_ref, hsi_ref, h_off, TM if reverse else 0, er_ref, ei_ref, cr_ref, ci_ref,
               pr_ref, pi_ref, reverse)
        y = jnp.concatenate(
            [_dot(hsr_ref[h_off:h_off + TM, j * LN:(j + 1) * LN].astype(BF), ctr_ref[j])
             - _dot(hsi_ref[h_off:h_off + TM, j * LN:(j + 1) * LN].astype(BF), cti_ref[j]) for j in range(NBLK)], axis=1)
        _store_perm(y_ref, y)

    state = lambda: pl.BlockSpec((HS_ROWS, SW), lambda i: (order(i), 0))
    xspec = lambda q: pl.BlockSpec((TM, 128), lambda i: (order(i), XS_BLK + q))
    return _call(body, name=name, grid=(nt,),
                 in_specs=[xspec(q) for q in range(nq)]
                 + [_dir_spec(dr, (NBLK, UB, LN)), _dir_spec(dr, (NBLK, UB, LN)), _dir_spec(dr, (NBLK, LN, UB)),
                    _dir_spec(dr, (NBLK, LN, UB)), _dir_spec(dr, (PW_ROWS, SW)), _dir_spec(dr, (PW_ROWS, SW))],
                 out_specs=[pl.BlockSpec((nq, TM, 128), lambda i: (0, order(i), 0)), state(), state()],
                 out_shape=[jax.ShapeDtypeStruct((nq, t, 128), F32), jax.ShapeDtypeStruct((nt * HS_ROWS, SW), F32),
                            jax.ShapeDtypeStruct((nt * HS_ROWS, SW), F32)],
                 scratch=[pltpu.VMEM((TM, SW), F32), pltpu.VMEM((TM, SW), F32),
                          pltpu.VMEM((SCAN_G, SW), F32), pltpu.VMEM((SCAN_G, SW), F32),
                          pltpu.VMEM((1, SW), F32), pltpu.VMEM((1, SW), F32)])(
        *([p] * nq), bb_re, bb_im, ct_re, ct_im, pw_re, pw_im)


def s5q_bwd(p, hs_re, hs_im, dy, bb_re, bb_im, ct_re, ct_im, pw_re, pw_im_conj, dr, reverse, name):
    t = p.shape[0]
    nt = t // TM
    order = _tile_order("bwd_adj" if reverse else "fwd_adj", nt)
    nq = AW // 128
    h_off, p_off = _hs_offsets(reverse)

    def body(*refs):
        x_refs, dy_refs = refs[:nq], refs[nq:2 * nq]
        (hsr_ref, hsi_ref, bbr_ref, bbi_ref, ctr_ref, cti_ref, pr_ref, pi_ref,
         dx_ref, dar_ref, dai_ref, dbbr_ref, dbbi_ref, dcr_ref, dci_ref,
         qr_ref, qi_ref, gr_ref, gi_ref, er_ref, ei_ref, cr_ref, ci_ref) = refs[2 * nq:]
        bbr_ref, bbi_ref, ctr_ref, cti_ref, pr_ref, pi_ref = [
            r.at[0] for r in (bbr_ref, bbi_ref, ctr_ref, cti_ref, pr_ref, pi_ref)]

        @pl.when(pl.program_id(0) == 0)
        def _():
            for r in (cr_ref, ci_ref, dar_ref, dai_ref, dbbr_ref, dbbi_ref, dcr_ref, dci_ref):
                r[...] = jnp.zeros_like(r)

        xb = _load_perm(x_refs).astype(BF)
        dyb = _load_perm(dy_refs).astype(BF)
        for j in range(NBLK):
            cols = slice(j * LN, (j + 1) * LN)
            qr_ref[:, cols] = _dot(dyb[:, j * UB:(j + 1) * UB], ctr_ref[j], 1, 1)
            qi_ref[:, cols] = -_dot(dyb[:, j * UB:(j + 1) * UB], cti_ref[j], 1, 1)
        _scan2(qr_ref, qi_ref, gr_ref, gi_ref, 0, None, er_ref, ei_ref, cr_ref, ci_ref, pr_ref, pi_ref, not reverse,
               corr=(hsr_ref, hsi_ref, p_off, dar_ref, dai_ref))
        dxs = []
        for j in range(NBLK):
            cols = slice(j * LN, (j + 1) * LN)
            xj = xb[:, j * UB:(j + 1) * UB]
            dyj = dyb[:, j * UB:(j + 1) * UB]
            grb, gib = gr_ref[:, cols].astype(BF), gi_ref[:, cols].astype(BF)
            dcr_ref[j] += _dot(dyj, hsr_ref[h_off:h_off + TM, cols].astype(BF), 0, 0)
            dci_ref[j] += -_dot(dyj, hsi_ref[h_off:h_off + TM, cols].astype(BF), 0, 0)
            dbbr_ref[j] += _dot(xj, grb, 0, 0)
            dbbi_ref[j] += _dot(xj, gib, 0, 0)
            dxs.append(_dot(grb, bbr_ref[j], 1, 1) + _dot(gib, bbi_ref[j], 1, 1))
        _store_perm(dx_ref, jnp.concatenate(dxs, axis=1))

    state = lambda: pl.BlockSpec((HS_ROWS, SW), lambda i: (order(i), 0))
    blockd = lambda: _full((NBLK, UB, LN))
    xspec = lambda q: pl.BlockSpec((TM, 128), lambda i: (order(i), XS_BLK + q))
    dyspec = lambda q: pl.BlockSpec((TM, 128), lambda i: (order(i), q))
    return _call(body, name=name, grid=(nt,),
                 in_specs=[xspec(q) for q in range(nq)] + [dyspec(q) for q in range(nq)]
                 + [state(), state(), _dir_spec(dr, (NBLK, UB, LN)), _dir_spec(dr, (NBLK, UB, LN)),
                    _dir_spec(dr, (NBLK, LN, UB)), _dir_spec(dr, (NBLK, LN, UB)),
                    _dir_spec(dr, (PW_ROWS, SW)), _dir_spec(dr, (PW_ROWS, SW))],
                 out_specs=[pl.BlockSpec((nq, TM, 128), lambda i: (0, order(i), 0)), _full((1, SW)), _full((1, SW)),
                            blockd(), blockd(), blockd(), blockd()],
                 out_shape=[jax.ShapeDtypeStruct((nq, t, 128), F32), jax.ShapeDtypeStruct((1, SW), F32),
                            jax.ShapeDtypeStruct((1, SW), F32)] + [jax.ShapeDtypeStruct((NBLK, UB, LN), F32)] * 4,
                 scratch=[pltpu.VMEM((TM, SW), F32), pltpu.VMEM((TM, SW), F32),
                          pltpu.VMEM((TM, SW), F32), pltpu.VMEM((TM, SW), F32),
                          pltpu.VMEM((SCAN_G, SW), F32), pltpu.VMEM((SCAN_G, SW), F32),
                          pltpu.VMEM((1, SW), F32), pltpu.VMEM((1, SW), F32)])(
        *([p] * nq), *([dy] * nq), hs_re, hs_im, bb_re, bb_im, ct_re, ct_im, pw_re, pw_im_conj)


def s5p_dx_sum(dy, dsk, dxf, dxb, dp, name):
    t = dy.shape[0]
    nq = AW // 128

    def body(dy_ref, d_ref, f_ref, b_ref, dp_ref, o_ref):
        o_ref[...] = (dy_ref[...] * d_ref[...] + _cat_lanes(f_ref) + _cat_lanes(b_ref)).astype(BF)

    tile = pl.BlockSpec((TM, AW), lambda i: (i, 0))
    blk4 = pl.BlockSpec((nq, TM, 128), lambda i: (0, i, 0))
    return pl.pallas_call(
        body, name=name, grid=(t // TM,),
        in_specs=[tile, _full((1, AW)), blk4, blk4, pl.BlockSpec(memory_space=pl.ANY)],
        out_specs=pl.BlockSpec((TM, AW), lambda i: (i, 3)), out_shape=jax.ShapeDtypeStruct(dp.shape, dp.dtype),
        input_output_aliases={4: 0},
        compiler_params=pltpu.CompilerParams(vmem_limit_bytes=VMEM_LIMIT_BYTES))(dy, dsk, dxf, dxb, dp)


SCALE = HD ** -0.5
NHEAD_NORM = NQ + NKV


def _partner(x):
    half0 = (lax.broadcasted_iota(jnp.int32, (1, HD), 1) % 64) < 32
    return jnp.where(half0, pltpu.roll(x, HD - 32, 1), pltpu.roll(x, 32, 1))


def attn_prep(p, qg, kg, cos, sins, name):
    t = p.shape[0]

    def body(p_ref, qg_ref, kg_ref, cos_ref, sin_ref, o_ref):
        cv, sv = cos_ref[...], sin_ref[...]
        for h in range(NHEAD_NORM):
            cols = slice(h * HD, (h + 1) * HD)
            blk = p_ref[:, cols]
            r = lax.rsqrt(jnp.mean(blk * blk, axis=-1, keepdims=True) + EPS)
            xn = blk * r * (qg_ref[...] if h < NQ else kg_ref[...])
            rot = xn * cv + _partner(xn) * sv
            o_ref[:, cols] = ((rot * SCALE) if h < NQ else rot).astype(BF)
        vcols = slice(NHEAD_NORM * HD, (NHEAD_NORM + NKV) * HD)
        o_ref[:, vcols] = p_ref[:, vcols].astype(BF)

    w = (NHEAD_NORM + NKV) * HD
    tile = lambda ww: pl.BlockSpec((TM, ww), lambda i: (i, 0))
    return _call(body, name=name, grid=(t // TM,),
                 in_specs=[tile(w), _full((1, HD)), _full((1, HD)), tile(HD), tile(HD)],
                 out_specs=tile(w), out_shape=jax.ShapeDtypeStruct((t, w), BF))(p, qg, kg, cos, sins)


def attn_prep_bwd(p, dq, dk, dv, qg, kg, cos, sins, dp, name):
    t = p.shape[0]

    def body(p_ref, dq_ref, dk_ref, dv_ref, qg_ref, kg_ref, cos_ref, sin_ref, dp_ref, o_ref, dqg_ref, dkg_ref):
        @pl.when(pl.program_id(0) == 0)
        def _():
            dqg_ref[...] = jnp.zeros_like(dqg_ref)
            dkg_ref[...] = jnp.zeros_like(dkg_ref)

        cv, sv = cos_ref[...], sin_ref[...]
        for h in range(NHEAD_NORM):
            cols = slice(h * HD, (h + 1) * HD)
            blk = p_ref[:, cols]
            r = lax.rsqrt(jnp.mean(blk * blk, axis=-1, keepdims=True) + EPS)
            xh = blk * r
            if h < NQ:
                drot = dq_ref[:, cols] * SCALE
                gv, dg_ref = qg_ref[...], dqg_ref
            else:
                drot = dk_ref[:, (h - NQ) * HD:(h - NQ + 1) * HD]
                gv, dg_ref = kg_ref[...], dkg_ref
            dxn = drot * cv + _partner(drot * sv)
            dg_ref[...] += jnp.sum(dxn * xh, axis=0, keepdims=True)
            dxh = dxn * gv
            o_ref[:, cols] = (r * (dxh - xh * jnp.mean(dxh * xh, axis=-1, keepdims=True))).astype(BF)
        o_ref[:, NHEAD_NORM * HD:(NHEAD_NORM + NKV) * HD] = dv_ref[...].astype(BF)

    w = (NHEAD_NORM + NKV) * HD
    tile = lambda ww: pl.BlockSpec((TM, ww), lambda i: (i, 0))
    return pl.pallas_call(
        body, name=name, grid=(t // TM,),
        in_specs=[tile(w), tile(NQ * HD), tile(NKV * HD), tile(NKV * HD), _full((1, HD)), _full((1, HD)),
                  tile(HD), tile(HD), pl.BlockSpec(memory_space=pl.ANY)],
        out_specs=[tile(w), _full((1, HD)), _full((1, HD))],
        out_shape=[jax.ShapeDtypeStruct(dp.shape, dp.dtype), jax.ShapeDtypeStruct((1, HD), F32),
                   jax.ShapeDtypeStruct((1, HD), F32)],
        input_output_aliases={8: 0},
        compiler_params=pltpu.CompilerParams(vmem_limit_bytes=VMEM_LIMIT_BYTES))(p, dq, dk, dv, qg, kg, cos, sins, dp)


KCOL = NQ
VCOL = NQ + NKV
GCOL = (NQ + 2 * NKV)
QPK = NQ // NKV
ATT_KCHUNK = 512


def attn_fwd(qkv, p, name):
    t = qkv.shape[0]

    def body(q_ref, k_ref, v_ref, g_ref, o_ref, mix_ref, lse_ref):
        def attend(nk):
            for hh in range(QPK):
                attend_head(nk, slice(hh * HD, (hh + 1) * HD))

        def attend_head(nk, cols):
            q = q_ref[:, cols]
            chunks = [(k0, min(k0 + 2 * ATT_KCHUNK, nk)) for k0 in range(0, nk, 2 * ATT_KCHUNK)]
            s_next = _dot(q, k_ref[chunks[0][0]:chunks[0][1], :], 1, 1)
            m = l = acc = None
            for ci, (k0, k1) in enumerate(chunks):
                s = s_next
                if ci + 1 < len(chunks):
                    s_next = _dot(q, k_ref[chunks[ci + 1][0]:chunks[ci + 1][1], :], 1, 1)
                mc = jnp.max(s, axis=-1, keepdims=True)
                m_new = mc if m is None else jnp.maximum(m, mc)
                pe = jnp.exp(s - m_new)
                lc = jnp.sum(pe, axis=-1, keepdims=True)
                pv = _dot(pe.astype(BF), v_ref[k0:k1, :])
                if m is None:
                    l, acc = lc, pv
                else:
                    alpha = jnp.exp(m - m_new)
                    l, acc = alpha * l + lc, alpha * acc + pv
                m = m_new
            o = acc / l
            gt = g_ref[:, cols]
            o_ref[:, cols] = o
            mix_ref[:, cols] = (o * (gt * _sig(gt))).astype(BF)
            lse_ref[:, cols] = jnp.broadcast_to(m + jnp.log(l), (TM, HD))

        pl.when(pl.program_id(1) == 0)(lambda: attend(NC))
        pl.when(pl.program_id(1) > 0)(lambda: attend(t))

    blk = pl.BlockSpec((TM, QPK * HD), lambda kv, i: (i, kv))
    return _call(body, name=name, grid=(NKV, t // TM),
                 in_specs=[blk, pl.BlockSpec((t, HD), lambda kv, i: (0, KCOL + kv)),
                           pl.BlockSpec((t, HD), lambda kv, i: (0, VCOL + kv)),
                           pl.BlockSpec((TM, QPK * HD), lambda kv, i: (i, GCOL // QPK + kv))],
                 out_specs=[blk, blk, blk],
                 out_shape=[jax.ShapeDtypeStruct((t, NQ * HD), F32), jax.ShapeDtypeStruct((t, NQ * HD), BF),
                            jax.ShapeDtypeStruct((t, NQ * HD), F32)])(qkv, qkv, qkv, p)


def attn_bwd(qkv, p, dmix, o, lse, name):
    t = qkv.shape[0]

    def body(q_ref, k_ref, v_ref, g_ref, dm_ref, o_ref, lse_ref, dq_ref, dg_ref, dk_ref, dv_ref):
        i = pl.program_id(1)

        @pl.when(i == 0)
        def _():
            dk_ref[...] = jnp.zeros_like(dk_ref)
            dv_ref[...] = jnp.zeros_like(dv_ref)

        def bwd(nk):
            for hh in range(QPK):
                bwd_head(nk, slice(hh * HD, (hh + 1) * HD))

        def bwd_head(nk, cols):
            gt = g_ref[:, cols]
            sg = _sig(gt)
            ov = o_ref[:, cols]
            dmv = dm_ref[:, cols]
            dg_ref[:, cols] = (dmv * ov * (sg * (1.0 + gt * (1.0 - sg)))).astype(BF)
            do = dmv * (gt * sg)
            dr = jnp.sum(do * ov, axis=-1, keepdims=True)
            dob = do.astype(BF)
            q = q_ref[:, cols]
            lse = lse_ref[:, cols][:, 0:1]
            chunks = [slice(k0, min(k0 + ATT_KCHUNK, nk)) for k0 in range(0, nk, ATT_KCHUNK)]
            nxt = (_dot(q, k_ref[chunks[0], :], 1, 1), _dot(dob, v_ref[chunks[0], :], 1, 1))
            dq = None
            for ci, keys in enumerate(chunks):
                s, dp = nxt
                if ci + 1 < len(chunks):
                    nxt = (_dot(q, k_ref[chunks[ci + 1], :], 1, 1), _dot(dob, v_ref[chunks[ci + 1], :], 1, 1))
                pe = jnp.exp(s - lse)
                dsb = (pe * (dp - dr)).astype(BF)
                part = _dot(dsb, k_ref[keys, :])
                dq = part if dq is None else dq + part
                dv_ref[keys, :] += _dot(pe.astype(BF), dob, 0, 0)
                dk_ref[keys, :] += _dot(dsb, q, 0, 0)
            dq_ref[:, cols] = dq

        pl.when(i == 0)(lambda: bwd(NC))
        pl.when(i > 0)(lambda: bwd(t))

    blk = pl.BlockSpec((TM, QPK * HD), lambda kv, i: (i, kv))
    gate = pl.BlockSpec((TM, QPK * HD), lambda kv, i: (i, GCOL // QPK + kv))
    acc = pl.BlockSpec((t, HD), lambda kv, i: (0, kv))
    return _call(body, name=name, grid=(NKV, t // TM),
                 in_specs=[blk, pl.BlockSpec((t, HD), lambda kv, i: (0, KCOL + kv)),
                           pl.BlockSpec((t, HD), lambda kv, i: (0, VCOL + kv)), gate, blk, blk, blk],
                 out_specs=[blk, gate, acc, acc],
                 out_shape=[jax.ShapeDtypeStruct((t, NQ * HD), F32), jax.ShapeDtypeStruct((t, (GCOL + NQ) * HD), BF),
                            jax.ShapeDtypeStruct((t, NKV * HD), F32), jax.ShapeDtypeStruct((t, NKV * HD), F32)])(
        qkv, qkv, qkv, p, dmix, o, lse)


def _row_tile(rows, row_bytes, cap=2 * 1024 * 1024):
    if rows * row_bytes <= cap or rows % 8:
        return rows
    tr = rows
    while tr * row_bytes > cap and tr % 16 == 0:
        tr //= 2
    return tr


def _adamw_update(w_ref, g_ref, m_ref, v_ref, d_ref, nm_ref, nv_ref):
    gv = g_ref[...]
    m2 = ADAM_B1 * m_ref[...] + (1.0 - ADAM_B1) * gv
    v2 = ADAM_B2 * v_ref[...] + (1.0 - ADAM_B2) * (gv * gv)
    mh = m2 / (1.0 - ADAM_B1 ** ADAM_STEP)
    vh = v2 / (1.0 - ADAM_B2 ** ADAM_STEP)
    d_ref[...] = -ADAM_LR * (mh / (jnp.sqrt(vh) + ADAM_EPS) + ADAM_WD * w_ref[...])
    nm_ref[...] = m2
    nv_ref[...] = v2


def adamw_many(ws, gs, ms, vs, name):
    n = len(ws)

    def body(*refs):
        for k in range(n):
            _adamw_update(*[refs[j * n + k] for j in range(7)])

    shapes = [jax.ShapeDtypeStruct(w.shape, F32) for w in ws]
    res = _call(body, name=name, out_shape=shapes * 3)(*ws, *gs, *ms, *vs)
    return res[:n], res[n:2 * n], res[2 * n:]


def adamw(w, g, m, v, name):
    r, cdim = w.shape
    tr = _row_tile(r, 4 * max(cdim, 128))

    def body(w_ref, g_ref, m_ref, v_ref, d_ref, nm_ref, nv_ref):
        _adamw_update(w_ref, g_ref, m_ref, v_ref, d_ref, nm_ref, nv_ref)

    tile = pl.BlockSpec((tr, cdim), lambda i: (i, 0))
    sh = jax.ShapeDtypeStruct((r, cdim), F32)
    return _call(body, name=name, grid=(r // tr,), in_specs=[tile] * 4, out_specs=[tile] * 3,
                 out_shape=[sh, sh, sh])(w, g, m, v)


def sum_lead(a, name, out_dtype=F32):
    n, r, cdim = a.shape
    tr = _row_tile(r, 4 * n * max(cdim, 128))

    def body(a_ref, o_ref):
        acc = a_ref[0].astype(F32)
        for k in range(1, n):
            acc = acc + a_ref[k].astype(F32)
        o_ref[...] = acc.astype(o_ref.dtype)

    return _call(body, name=name, grid=(r // tr,),
                 in_specs=[pl.BlockSpec((n, tr, cdim), lambda i: (0, i, 0))],
                 out_specs=pl.BlockSpec((tr, cdim), lambda i: (i, 0)),
                 out_shape=jax.ShapeDtypeStruct((r, cdim), out_dtype))(a)


_FLIPS = {"xy": [(1, 0, 0), (0, 1, 0), (1, 1, 0)], "c": [(0, 0, 1)],
          "all": [(0, 0, 1), (0, 1, 0), (0, 1, 1), (1, 0, 0), (1, 0, 1), (1, 1, 0), (1, 1, 1)]}
_GROUP_SIZE = {"xy": 4, "c": 2, "all": 8}


def _group_index(group, x, y, c):
    return {"xy": 2 * x + y, "c": c, "all": 4 * x + 2 * y + c}[group]


def exchange(items, name):
    plan = []
    for arr, group, kind in items:
        chunk = arr.shape if kind == "gather" else arr.shape[1:]
        plan.append((group, kind, chunk))
    ncopy = sum(len(_FLIPS[g]) for g, _, _ in plan)
    nitem = len(plan)

    def body(*refs):
        srcs, dsts = refs[:nitem], refs[nitem:2 * nitem]
        send_sems, recv_sems, local_sems = refs[2 * nitem:]
        x, y, c = lax.axis_index("x"), lax.axis_index("y"), lax.axis_index("c")
        sends, recvs, locals_ = [], [], []
        n = 0
        for k, (group, kind, _) in enumerate(plan):
            me = _group_index(group, x, y, c)
            own = srcs[k] if kind == "gather" else srcs[k].at[me]
            locals_.append(pltpu.make_async_copy(own, dsts[k].at[me], local_sems.at[k]))
            for fx, fy, fc in _FLIPS[group]:
                px, py, pc = (1 - x if fx else x), (1 - y if fy else y), (1 - c if fc else c)
                peer = _group_index(group, px, py, pc)
                src = srcs[k] if kind == "gather" else srcs[k].at[peer]
                sends.append(pltpu.make_async_remote_copy(
                    src_ref=src, dst_ref=dsts[k].at[me], send_sem=send_sems.at[n], recv_sem=recv_sems.at[n],
                    device_id=(px, py, pc), device_id_type=MESH))
                recvs.append(pltpu.make_async_remote_copy(
                    src_ref=src, dst_ref=dsts[k].at[peer], send_sem=send_sems.at[n], recv_sem=recv_sems.at[n],
                    device_id=(px, py, pc), device_id_type=MESH))
                n += 1
        for cp in locals_ + sends:
            cp.start()
        for cp in recvs:
            cp.wait_recv()
        for cp in sends:
            cp.wait_send()
        for cp in locals_:
            cp.wait()

    anyspec = pl.BlockSpec(memory_space=pl.ANY)
    outs = [jax.ShapeDtypeStruct((_GROUP_SIZE[g],) + tuple(chunk), arr.dtype)
            for (arr, _, _), (g, _, chunk) in zip(items, plan)]
    res = pl.pallas_call(
        body, name=name, out_shape=outs, in_specs=[anyspec] * nitem, out_specs=[anyspec] * nitem,
        scratch_shapes=[pltpu.SemaphoreType.DMA((ncopy,)), pltpu.SemaphoreType.DMA((ncopy,)),
                        pltpu.SemaphoreType.DMA((nitem,))],
        compiler_params=pltpu.CompilerParams(has_side_effects=True))(*[a for a, _, _ in items])
    return list(res)


D2D_PIECES = 4


def d2d(items, name):
    n = len(items)
    swaps = [k for k, (_, kind) in enumerate(items) if kind == "swap"]

    def pieces_of(rows):
        npc = D2D_PIECES if rows % (8 * D2D_PIECES) == 0 else 1
        return npc, rows // npc

    ncopy = sum(pieces_of(a.shape[0] if kind == "gather" else a.shape[1])[0] for a, kind in items)

    def body(*refs):
        srcs, outs = refs[:n], refs[n:2 * n]
        stages = dict(zip(swaps, refs[2 * n:2 * n + len(swaps)]))
        send_sems, recv_sems, local_sems = refs[2 * n + len(swaps):]
        x, y, c = lax.axis_index("x"), lax.axis_index("y"), lax.axis_index("c")
        sib = (x, y, 1 - c)

        def remote(src, dst, q):
            return pltpu.make_async_remote_copy(src_ref=src, dst_ref=dst, send_sem=send_sems.at[q],
                                                recv_sem=recv_sems.at[q], device_id=sib, device_id_type=MESH)

        copies = []
        q = 0
        for k, (arr, kind) in enumerate(items):
            npc, pr = pieces_of(arr.shape[0] if kind == "gather" else arr.shape[1])
            for pc in range(npc):
                rs = pl.ds(pc * pr, pr)
                if kind == "gather":
                    mine, theirs = outs[k].at[c, rs], outs[k].at[1 - c, rs]
                    copies.append((pltpu.make_async_copy(srcs[k].at[rs], mine, local_sems.at[q]),
                                   remote(mine, mine, q), remote(theirs, theirs, q)))
                else:
                    stage, land = stages[k].at[rs], outs[k].at[rs]
                    copies.append((pltpu.make_async_copy(srcs[k].at[1 - c, rs], stage, local_sems.at[q]),
                                   remote(stage, land, q), remote(stage, land, q)))
                q += 1
        for loc, _, _ in copies:
            loc.start()
        for loc, send, _ in copies:
            loc.wait()
            send.start()
        for _, _, recv in copies:
            recv.wait_recv()
        for _, send, _ in copies:
            send.wait_send()

    outs = [jax.ShapeDtypeStruct((2,) + a.shape if kind == "gather" else a.shape[1:], a.dtype) for a, kind in items]
    res = pl.pallas_call(
        body, name=name, out_shape=outs, in_specs=[pl.BlockSpec(memory_space=pl.ANY)] * n,
        out_specs=[pl.BlockSpec(memory_space=pltpu.VMEM)] * n,
        scratch_shapes=[pltpu.VMEM(items[k][0].shape[1:], items[k][0].dtype) for k in swaps]
        + [pltpu.SemaphoreType.DMA((ncopy,)), pltpu.SemaphoreType.DMA((ncopy,)), pltpu.SemaphoreType.DMA((ncopy,))],
        compiler_params=pltpu.CompilerParams(has_side_effects=True, vmem_limit_bytes=VMEM_LIMIT_BYTES))(
        *[a for a, _ in items])
    return list(res)


def sum_own(pair, got, name, out_dtype=F32):
    _, r, cdim = pair.shape
    tr = _row_tile(r, 4 * 2 * max(cdim, 128))

    def body(c_ref, p_ref, g_ref, o_ref):
        o_ref[...] = (p_ref[0] + g_ref[...]).astype(o_ref.dtype)

    me = lax.axis_index("c").astype(jnp.int32).reshape(1)
    return pl.pallas_call(
        body, name=name, out_shape=jax.ShapeDtypeStruct((r, cdim), out_dtype),
        grid_spec=pltpu.PrefetchScalarGridSpec(
            num_scalar_prefetch=1, grid=(r // tr,),
            in_specs=[pl.BlockSpec((1, tr, cdim), lambda i, c_ref: (c_ref[0], i, 0)),
                      pl.BlockSpec((tr, cdim), lambda i, c_ref: (i, 0))],
            out_specs=pl.BlockSpec((tr, cdim), lambda i, c_ref: (i, 0))),
        compiler_params=pltpu.CompilerParams(vmem_limit_bytes=VMEM_LIMIT_BYTES))(me, pair, got)


_SMALL = ["c_ctx", "norm_g", "b_mod", "gm_v_g", "gm_w_s", "gm_b_s", "s5_lam_re", "s5_lam_im", "s5_log_dt",
          "s5_b_re", "s5_b_im", "s5_c_re", "s5_c_im", "s5_d", "s5_b_glu", "q_norm_g", "k_norm_g", "final_g"]
_BIG = ["we_in", "we_out", "s5_w_glu", "wo_in", "wo_out"]
_WEIGHTS = ["c_ctx", "norm_g", "w_mod", "b_mod", "we_in", "we_out", "gm_v_g", "gm_w_s", "gm_b_s", "s5_lam_re",
            "s5_lam_im", "s5_log_dt", "s5_b_re", "s5_b_im", "s5_c_re", "s5_c_im", "s5_d", "s5_w_glu", "s5_b_glu",
            "wo_in", "wo_out", "q_norm_g", "k_norm_g", "final_g"]
_SMALL_ALIGN = 8 * 8 * 128


def _rope_tables(n_lat):
    rows = n_lat // GRID_W
    row = jnp.repeat(jnp.arange(rows), GRID_W)
    col = jnp.tile(jnp.arange(GRID_W), rows)
    freqs = ROPE_THETA ** (-jnp.arange(HD // 4, dtype=F32) / (HD // 4))
    ar, ac = row[:, None] * freqs, col[:, None] * freqs
    cos = jnp.concatenate([jnp.cos(ar), jnp.cos(ar), jnp.cos(ac), jnp.cos(ac)], axis=1)
    sins = jnp.concatenate([-jnp.sin(ar), jnp.sin(ar), -jnp.sin(ac), jnp.sin(ac)], axis=1)
    cos = jnp.concatenate([jnp.ones((NC, HD), F32), cos], axis=0)
    sins = jnp.concatenate([jnp.zeros((NC, HD), F32), sins], axis=0)
    return cos, sins


def _block_diag(v, transpose):
    gpb = SG // NBLK
    v = v.reshape(2, NBLK, gpb, SH, SP)
    eye = jnp.eye(gpb, dtype=v.dtype)
    if transpose:
        return jnp.einsum("djahp,ab->djapbh", v, eye).reshape(2, NBLK, LN, UB)
    return jnp.einsum("djahp,ab->djahbp", v, eye).reshape(2, NBLK, UB, LN)


def _diag_blocks(m):
    gpb = SG // NBLK
    return jnp.einsum("jahap->jahp", m.reshape(NBLK, gpb, SH, gpb, SP)).reshape(SG, SH, SP)


def _view2d(a):
    if a.ndim == 1:
        return a.reshape(1, -1)
    if a.shape[-1] < 64 and a.size % 1024 == 0:
        return a.reshape(-1, 1024)
    return a.reshape(-1, a.shape[-1])


def kernel(x, c, ctx, c_ctx, norm_g, w_mod, b_mod, we_in, we_out, gm_v_g, gm_w_s, gm_b_s, s5_lam_re, s5_lam_im, s5_log_dt, s5_b_re, s5_b_im, s5_c_re, s5_c_im, s5_d, s5_w_glu, s5_b_glu, wo_in, wo_out, q_norm_g, k_norm_g, final_g, loss_target, m_c_ctx, m_norm_g, m_w_mod, m_b_mod, m_we_in, m_we_out, m_gm_v_g, m_gm_w_s, m_gm_b_s, m_s5_lam_re, m_s5_lam_im, m_s5_log_dt, m_s5_b_re, m_s5_b_im, m_s5_c_re, m_s5_c_im, m_s5_d, m_s5_w_glu, m_s5_b_glu, m_wo_in, m_wo_out, m_q_norm_g, m_k_norm_g, m_final_g, v_c_ctx, v_norm_g, v_w_mod, v_b_mod, v_we_in, v_we_out, v_gm_v_g, v_gm_w_s, v_gm_b_s, v_s5_lam_re, v_s5_lam_im, v_s5_log_dt, v_s5_b_re, v_s5_b_im, v_s5_c_re, v_s5_c_im, v_s5_d, v_s5_w_glu, v_s5_b_glu, v_wo_in, v_wo_out, v_q_norm_g, v_k_norm_g, v_final_g):
    weights = dict(c_ctx=c_ctx, norm_g=norm_g, w_mod=w_mod, b_mod=b_mod, we_in=we_in, we_out=we_out, gm_v_g=gm_v_g,
                   gm_w_s=gm_w_s, gm_b_s=gm_b_s, s5_lam_re=s5_lam_re, s5_lam_im=s5_lam_im, s5_log_dt=s5_log_dt,
                   s5_b_re=s5_b_re, s5_b_im=s5_b_im, s5_c_re=s5_c_re, s5_c_im=s5_c_im, s5_d=s5_d, s5_w_glu=s5_w_glu,
                   s5_b_glu=s5_b_glu, wo_in=wo_in, wo_out=wo_out, q_norm_g=q_norm_g, k_norm_g=k_norm_g,
                   final_g=final_g)
    mom_m = dict(c_ctx=m_c_ctx, norm_g=m_norm_g, w_mod=m_w_mod, b_mod=m_b_mod, we_in=m_we_in, we_out=m_we_out,
                 gm_v_g=m_gm_v_g, gm_w_s=m_gm_w_s, gm_b_s=m_gm_b_s, s5_lam_re=m_s5_lam_re, s5_lam_im=m_s5_lam_im,
                 s5_log_dt=m_s5_log_dt, s5_b_re=m_s5_b_re, s5_b_im=m_s5_b_im, s5_c_re=m_s5_c_re, s5_c_im=m_s5_c_im,
                 s5_d=m_s5_d, s5_w_glu=m_s5_w_glu, s5_b_glu=m_s5_b_glu, wo_in=m_wo_in, wo_out=m_wo_out,
                 q_norm_g=m_q_norm_g, k_norm_g=m_k_norm_g, final_g=m_final_g)
    mom_v = dict(c_ctx=v_c_ctx, norm_g=v_norm_g, w_mod=v_w_mod, b_mod=v_b_mod, we_in=v_we_in, we_out=v_we_out,
                 gm_v_g=v_gm_v_g, gm_w_s=v_gm_w_s, gm_b_s=v_gm_b_s, s5_lam_re=v_s5_lam_re, s5_lam_im=v_s5_lam_im,
                 s5_log_dt=v_s5_log_dt, s5_b_re=v_s5_b_re, s5_b_im=v_s5_b_im, s5_c_re=v_s5_c_re, s5_c_im=v_s5_c_im,
                 s5_d=v_s5_d, s5_w_glu=v_s5_w_glu, s5_b_glu=v_s5_b_glu, wo_in=v_wo_in, wo_out=v_wo_out,
                 q_norm_g=v_q_norm_g, k_norm_g=v_k_norm_g, final_g=v_final_g)

    ixy = 2 * lax.axis_index("x") + lax.axis_index("y")
    n_lat = x.shape[1]
    nl = norm_g.shape[0]
    nmod = w_mod.shape[2]
    xin = (ctx[0], x[0])

    ic = lax.axis_index("c")
    mine = [lax.dynamic_index_in_dim(weights[n], ic, 0, keepdims=False).astype(BF) for n in _BIG]
    got = exchange([(m_, "xy", "gather") for m_ in mine] + [(c, "xy", "gather")], "gather_weights")
    both = d2d([(g_.reshape(-1, g_.shape[-1]), "gather") for g_ in got[:len(_BIG)]], "swap_weights")
    both = [b_.reshape((2,) + g_.shape) for b_, g_ in zip(both, got)]
    wein = [(both[0], l) for l in range(2)]
    weout = [(both[1].reshape(2, 1, D, D), l) for l in range(2)]
    wglu = [(both[2].reshape(2, AW, AW), l) for l in range(2)]
    woin = [(both[3], l) for l in range(2)]
    woout = [(both[4].reshape(2, 1, D, D), l) for l in range(2)]
    c_group = got[len(_BIG)].reshape(4, D)

    cond = jnp.concatenate([c_group, jnp.broadcast_to(c_ctx.reshape(1, D), (4, D))], axis=0)
    b_shard = lax.dynamic_slice(b_mod, (0, ixy * nmod), (nl, nmod)).reshape(nl, 1, nmod)
    mpart = ada_fwd(cond, w_mod, b_shard)
    m_lat, m_ctx = exchange([(jnp.transpose(mpart[:, 0:4], (1, 0, 2)), "xy", "scatter"),
                             (mpart[:, 4], "xy", "gather")], "exchange_mod")
    m_lat = jnp.transpose(m_lat, (1, 0, 2)).reshape(nl, 3, D)
    m_ctx = jnp.transpose(m_ctx, (1, 0, 2)).reshape(nl, 3, D)
    mods = [jnp.stack([m_ctx[l], m_lat[l]], axis=0) for l in range(nl)]

    loss_part, dx, g, d_norm_g, d_mod_lat, d_mod_ctx, d_final_g = _local_step(
        xin, loss_target[0], mods, wein, weout, wglu, woin, woout, weights)
    grad_x = dx.reshape(1, n_lat, D)

    d_mod_lat, d_mod_ctx = jnp.stack(d_mod_lat), jnp.stack(d_mod_ctx)
    dm_send = jnp.stack([d_mod_lat.reshape(nl, 4, nmod), d_mod_ctx.reshape(nl, 4, nmod)])
    (dm_got,) = exchange([(jnp.transpose(dm_send, (2, 0, 1, 3)), "xy", "scatter")], "exchange_dmod")
    dm_rows = jnp.concatenate([dm_got[:, 0], dm_got[:, 1]], axis=0)
    gw_mod, d_cctx = ada_bwd(cond, jnp.transpose(dm_rows, (1, 0, 2)), w_mod)
    g_small = dict(c_ctx=d_cctx.reshape(D), norm_g=jnp.stack(d_norm_g), b_mod=add2(d_mod_lat, d_mod_ctx, "add_dbmod"),
                   final_g=d_final_g.reshape(D))
    for name in _SMALL:
        if name not in g_small:
            g_small[name] = jnp.stack(g[name])

    flat = jnp.concatenate([g_small[n].reshape(-1) for n in _SMALL])
    nflat = flat.shape[0]
    npad = -(-nflat // _SMALL_ALIGN) * _SMALL_ALIGN
    flat = jnp.concatenate([flat, jnp.zeros((npad - nflat,), F32)]).reshape(8, npad // (8 * 128), 128)
    pairs = [gw_mod.reshape(2, nl // 2 * D, nmod)]
    for name in _BIG:
        st = jnp.stack(g[name]) if isinstance(g[name], list) else g[name]
        pairs.append(st.reshape(2, -1, st.shape[-1]))
    got_a = d2d([(pairs[k], "swap") for k in (1, 2, 3)], "reduce_chip_a")
    got_b = d2d([(pairs[k], "swap") for k in (0, 4, 5)], "reduce_chip_b")
    theirs = [got_b[0]] + got_a + got_b[1:]
    chip = [sum_own(pairs[k], theirs[k], f"sum_chip{k}", F32 if k == 0 else BF) for k in range(len(pairs))]
    parts = exchange([(flat, "all", "scatter")]
                     + [(s_.reshape(4, s_.shape[0] // 4, s_.shape[1]), "xy", "scatter") for s_ in chip[1:]],
                     "reduce_scatter")
    sums = [sum_lead(pt, f"sum_shard{k}") for k, pt in enumerate(parts)]
    full = d2d([(sums[0], "gather"), (chip[0], "gather")] + [(s_, "gather") for s_ in sums[1:]], "all_gather")
    (flat_full,) = exchange([(full[0], "xy", "gather")], "gather_small")
    full = [flat_full] + full[1:]
    flat = full[0].reshape(-1)
    grads = {}
    off = 0
    for name in _SMALL:
        sz = weights[name].size
        grads[name] = flat[off:off + sz].reshape(weights[name].shape)
        off += sz
    grads["w_mod"] = full[1].reshape(w_mod.shape)
    for k, name in enumerate(_BIG):
        grads[name] = full[2 + k].reshape(weights[name].shape)

    delta, new_m, new_v = {}, {}, {}
    views = [_view2d(weights[n]) for n in _SMALL]
    ds, nms, nvs = adamw_many(views, [grads[n].reshape(w2.shape) for n, w2 in zip(_SMALL, views)],
                              [mom_m[n].reshape(w2.shape) for n, w2 in zip(_SMALL, views)],
                              [mom_v[n].reshape(w2.shape) for n, w2 in zip(_SMALL, views)], "adamw_small")
    for n, d2, m2, v2 in zip(_SMALL, ds, nms, nvs):
        shp = weights[n].shape
        delta[n], new_m[n], new_v[n] = d2.reshape(shp), m2.reshape(shp), v2.reshape(shp)
    for name in ["w_mod"] + _BIG:
        w2 = _view2d(weights[name])
        d2, m2, v2 = adamw(w2, grads[name].reshape(w2.shape), mom_m[name].reshape(w2.shape),
                           mom_v[name].reshape(w2.shape), f"adamw_{name}")
        shp = weights[name].shape
        delta[name], new_m[name], new_v[name] = d2.reshape(shp), m2.reshape(shp), v2.reshape(shp)

    loss = lax.psum(loss_part[0, 0], ("x", "y", "c"))
    return (loss, grad_x, *[grads[n] for n in _WEIGHTS], *[delta[n] for n in _WEIGHTS],
            *[new_m[n] for n in _WEIGHTS], *[new_v[n] for n in _WEIGHTS])


def _local_step(xin, target, mods, wein, weout, wglu, woin, woout, w):
    norm_g, gm_v_g, gm_w_s, gm_b_s = w["norm_g"], w["gm_v_g"], w["gm_w_s"], w["gm_b_s"]
    s5_lam_re, s5_lam_im, s5_log_dt = w["s5_lam_re"], w["s5_lam_im"], w["s5_log_dt"]
    s5_b_re, s5_b_im, s5_c_re, s5_c_im = w["s5_b_re"], w["s5_b_im"], w["s5_c_re"], w["s5_c_im"]
    s5_d, s5_b_glu, q_norm_g, k_norm_g, final_g = w["s5_d"], w["s5_b_glu"], w["q_norm_g"], w["k_norm_g"], w["final_g"]
    nl = norm_g.shape[0]
    n_lat = xin[1].shape[0]

    cos, sins = _rope_tables(n_lat)

    s5p = []
    for i in range(2):
        lam_l = (s5_lam_re[i].reshape(2, SW), s5_lam_im[i].reshape(2, SW),
                 jnp.repeat(s5_log_dt[i], SP, axis=1))
        lam_r = (jnp.repeat(s5_lam_re[i].reshape(2 * SG, SP), SH, axis=0),
                 jnp.repeat(s5_lam_im[i].reshape(2 * SG, SP), SH, axis=0),
                 jnp.repeat(s5_log_dt[i].reshape(2 * SG, 1), SH, axis=0))
        b_r = (jnp.transpose(s5_b_re[i], (0, 1, 3, 2)).reshape(2 * SG * SH, SP),
               jnp.transpose(s5_b_im[i], (0, 1, 3, 2)).reshape(2 * SG * SH, SP))
        pw_re, pw_im, bbr, bbi = s5_disc(*lam_l, *lam_r, *b_r)
        s5p.append(dict(
            lam_r=lam_r, b_r=b_r, pw_re=pw_re, pw_im=pw_im, pw_im_conj=-pw_im,
            bb_re=_block_diag(bbr.reshape(2, SG, SH, SP), False).astype(BF),
            bb_im=_block_diag(bbi.reshape(2, SG, SH, SP), False).astype(BF),
            ct_re=_block_diag(s5_c_re[i], True).astype(BF), ct_im=_block_diag(s5_c_im[i], True).astype(BF)))

    saved = []
    xcur = xin
    h = pro_fwd(xin[0], xin[1], norm_g[0].reshape(1, D), mods[0], "pro_fwd0")
    for l in range(nl):
        i = l // 2
        sv = dict(x=xcur, h=h)
        if l % 2 == 0:
            p = mm_nn(h, wein[i], f"in_proj{l}")
            sp = s5p[i]
            for dr, rev in ((0, False), (1, True)):
                sv[f"y{dr}"], sv[f"hpr{dr}"], sv[f"hpi{dr}"] = s5q_fwd(
                    p, sp["bb_re"], sp["bb_im"], sp["ct_re"], sp["ct_im"], sp["pw_re"], sp["pw_im"], dr, rev,
                    f"s5_fwd{l}_{dr}")
            mix = mix_fwd(p, sv["y0"], sv["y1"], gm_v_g[i].reshape(1, AW), gm_w_s[i].astype(BF),
                          gm_b_s[i].reshape(NGRP, CHUNK, 1), s5_d[i].reshape(1, AW), wglu[i],
                          s5_b_glu[i].reshape(1, AW), f"mix_fwd{l}")
            o = mm_nn(mix, weout[i], f"out_proj{l}")
        else:
            p = mm_nn(h, woin[i], f"in_proj{l}")
            sv["qkv"] = attn_prep(p, q_norm_g[i].reshape(1, HD), k_norm_g[i].reshape(1, HD), cos, sins, f"attn_prep{l}")
            sv["o_att"], mix, sv["lse"] = attn_fwd(sv["qkv"], p, f"attn_fwd{l}")
            o = mm_nn(mix, woout[i], f"out_proj{l}")
        sv.update(p=p, mix=mix, o=o)
        saved.append(sv)
        if l < nl - 1:
            xcur, h = res_pro_fwd(xcur, o, mods[l], True, norm_g[l + 1].reshape(1, D), mods[l + 1], f"res_pro_fwd{l}")
        else:
            xcur = res_fwd(xcur, o, mods[l], False, f"res_fwd{l}")

    loss_part, dx, d_final_g, do, dgt = final_loss(xcur, target, final_g.reshape(1, D), saved[-1]["o"], mods[-1])

    g = {}
    gbuf = {}
    d_norm_g, d_mod_lat, d_mod_ctx = [None] * nl, [None] * nl, [None] * nl
    for name in ("s5_w_glu", "gm_v_g", "gm_w_s", "gm_b_s", "s5_lam_re", "s5_lam_im",
                 "s5_log_dt", "s5_b_re", "s5_b_im", "s5_c_re", "s5_c_im", "s5_d", "s5_b_glu", "q_norm_g", "k_norm_g"):
        g[name] = [None, None]
    for l in reversed(range(nl)):
        i = l // 2
        sv = saved[l]
        w_out = weout[i] if l % 2 == 0 else woout[i]
        dmix = mm_nt(do, w_out, f"out_dgrad{l}")
        out_name, in_name = ("we_out", "we_in") if l % 2 == 0 else ("wo_out", "wo_in")
        gbuf[out_name] = mm_tn(sv["mix"], do, 1, f"out_wgrad{l}", slot=i, into=gbuf.get(out_name))
        if l % 2 == 0:
            sp = s5p[i]
            (dp, dy, g["gm_w_s"][i], dbs, dvg, dd, g["s5_w_glu"][i], dbg) = mix_bwd(
                sv["p"], sv["y0"], sv["y1"], dmix, gm_v_g[i].reshape(1, AW), gm_w_s[i].astype(BF),
                gm_b_s[i].reshape(NGRP, CHUNK, 1), s5_d[i].reshape(1, AW), wglu[i], s5_b_glu[i].reshape(1, AW),
                f"mix_bwd{l}")
            g["gm_b_s"][i], g["gm_v_g"][i] = dbs.reshape(NGRP, CHUNK), dvg.reshape(AW)
            g["s5_d"][i], g["s5_b_glu"][i] = dd.reshape(AW), dbg.reshape(AW)
            g["s5_w_glu"][i] = g["s5_w_glu"][i].reshape(4, AW // 4, AW)
            dxd, das_r, das_i, dbbs_r, dbbs_i, dcs_r, dcs_i = [], [], [], [], [], [], []
            for dr, rev in ((0, False), (1, True)):
                dxs_d, da_r, da_i, dbb_r, dbb_i, dc_r, dc_i = s5q_bwd(
                    sv["p"], sv[f"hpr{dr}"], sv[f"hpi{dr}"], dy, sp["bb_re"], sp["bb_im"], sp["ct_re"], sp["ct_im"],
                    sp["pw_re"], sp["pw_im_conj"], dr, rev, f"s5_bwd{l}_{dr}")
                dxd.append(dxs_d)
                das_r.append(jnp.repeat(da_r.reshape(SG, SP), SH, axis=0))
                das_i.append(jnp.repeat(da_i.reshape(SG, SP), SH, axis=0))
                dbbs_r.append(_diag_blocks(dbb_r).reshape(SG * SH, SP))
                dbbs_i.append(_diag_blocks(dbb_i).reshape(SG * SH, SP))
                dcs_r.append(_diag_blocks(dc_r))
                dcs_i.append(_diag_blocks(dc_i))
            cat = lambda parts: jnp.concatenate(parts, axis=0)
            dlr, dli, dldt, dbr, dbi = s5_param_bwd(*sp["lam_r"], *sp["b_r"], cat(das_r), cat(das_i),
                                                    cat(dbbs_r), cat(dbbs_i))
            g["s5_lam_re"][i], g["s5_lam_im"][i] = dlr.reshape(2, SG, SP), dli.reshape(2, SG, SP)
            g["s5_log_dt"][i] = dldt.reshape(2, SG)
            g["s5_b_re"][i] = jnp.transpose(dbr.reshape(2, SG, SH, SP), (0, 1, 3, 2))
            g["s5_b_im"][i] = jnp.transpose(dbi.reshape(2, SG, SH, SP), (0, 1, 3, 2))
            g["s5_c_re"][i], g["s5_c_im"][i] = jnp.stack(dcs_r), jnp.stack(dcs_i)
            dp = s5p_dx_sum(dy, s5_d[i].reshape(1, AW), dxd[0], dxd[1], dp, f"s5_dx_sum{l}")
            w_in = wein[i]
        else:
            dq, dp, dk, dv = attn_bwd(sv["qkv"], sv["p"], dmix, sv["o_att"], sv["lse"], f"attn_bwd{l}")
            dp, dqg, dkg = attn_prep_bwd(sv["p"], dq, dk, dv, q_norm_g[i].reshape(1, HD),
                                         k_norm_g[i].reshape(1, HD), cos, sins, dp, f"attn_prep_bwd{l}")
            g["q_norm_g"][i], g["k_norm_g"][i] = dqg.reshape(HD), dkg.reshape(HD)
            w_in = woin[i]
        dh = mm_nt(dp, w_in, f"in_dgrad{l}")
        gbuf[in_name] = mm_tn(sv["h"], dp, 4, f"in_wgrad{l}", slot=i, into=gbuf.get(in_name))
        dgt_l = dgt
        if l > 0:
            dx, dmod2, dng, do, dgt = pro_res_bwd(sv["x"], dh, dx, norm_g[l].reshape(1, D), mods[l],
                                                  saved[l - 1]["o"], mods[l - 1], f"pro_res_bwd{l}")
        else:
            dx, dmod2, dng = pro_bwd(xin[0], xin[1], dh, dx, norm_g[l].reshape(1, D), mods[l], f"pro_bwd{l}")
        d_norm_g[l] = dng.reshape(D)
        d_mod_ctx[l] = jnp.concatenate([dmod2[0, 0], dmod2[0, 1], dgt_l[0]])
        d_mod_lat[l] = jnp.concatenate([dmod2[1, 0], dmod2[1, 1], dgt_l[1]])
    g.update(gbuf)
    return loss_part, dx, g, d_norm_g, d_mod_lat, d_mod_ctx, d_final_g
```

```python
import math

import numpy as np
import jax
import jax.numpy as jnp
from jax import lax
from jax.experimental import pallas as pl
from jax.experimental.pallas import tpu as pltpu

F32 = jnp.float32
BF = jnp.bfloat16
MESH = pl.DeviceIdType.MESH

D = 1024
NC = 256
SEQ = 4096
GRID_W = 64
TM = 256
CHUNK = 128
EPS = 1e-6
HD = 128
NQ = 8
NKV = 2
ROPE_THETA = 10000.0
SG = 32
SP = 64
SH = 16
SW = SG * SP
GELU_K = math.sqrt(2.0 / math.pi)
GELU_C = 0.044715
VMEM_LIMIT_BYTES = 56 * 1024 * 1024

ADAM_LR = 0.001
ADAM_B1 = 0.9
ADAM_B2 = 0.999
ADAM_EPS = 1e-08
ADAM_WD = 0.01
ADAM_STEP = 10


def _call(body, *, name, out_shape, grid=None, in_specs=None, out_specs=None, scratch=()):
    kw = {}
    if grid is not None:
        kw["grid"] = grid
    if in_specs is not None:
        kw["in_specs"] = in_specs
    if out_specs is not None:
        kw["out_specs"] = out_specs
    return pl.pallas_call(
        body, name=name, out_shape=out_shape, scratch_shapes=list(scratch),
        compiler_params=pltpu.CompilerParams(vmem_limit_bytes=VMEM_LIMIT_BYTES), **kw)


def _dot(a, b, ca=1, cb=0):
    return lax.dot_general(a, b, (((ca,), (cb,)), ((), ())), preferred_element_type=F32)


def _sig(x):
    return 1.0 / (1.0 + jnp.exp(-x))


def _full(shape):
    n = len(shape)
    return pl.BlockSpec(shape, lambda *_: (0,) * n)


def _mm_rows(t, most=1088):
    for rows in (2176, 1088, 1024, 768, 512, 256):
        if rows <= most and t % rows == 0:
            return rows
    raise ValueError(t)


def _layer_of(w):
    return w if isinstance(w, tuple) else (w[None], 0)


def mm_nn(a, w, name, out_dtype=F32):
    w4, layer = _layer_of(w)
    t, k = a.shape
    _, j, _, nb = w4.shape
    tr = _mm_rows(t, 2176)

    def body(a_ref, w_ref, o_ref):
        o_ref[...] = _dot(a_ref[...], w_ref[0, 0]).astype(o_ref.dtype)

    return _call(body, name=name, grid=(j, t // tr),
                 in_specs=[pl.BlockSpec((tr, k), lambda jj, i: (i, 0)),
                           pl.BlockSpec((1, 1, k, nb), lambda jj, i: (layer, jj, 0, 0))],
                 out_specs=pl.BlockSpec((tr, nb), lambda jj, i: (i, jj)),
                 out_shape=jax.ShapeDtypeStruct((t, j * nb), out_dtype))(a, w4)


def mm_nt(a, w, name, out_dtype=F32):
    w4, layer = _layer_of(w)
    t, _ = a.shape
    _, j, k, nb = w4.shape
    tr = _mm_rows(t)

    def body(a_ref, w_ref, o_ref):
        acc = _dot(a_ref[:, 0:nb], w_ref[0, 0], 1, 1)
        for jj in range(1, j):
            acc = acc + _dot(a_ref[:, jj * nb:(jj + 1) * nb], w_ref[0, jj], 1, 1)
        o_ref[...] = acc.astype(o_ref.dtype)

    return _call(body, name=name, grid=(t // tr,),
                 in_specs=[pl.BlockSpec((tr, j * nb), lambda i: (i, 0)),
                           pl.BlockSpec((1, j, k, nb), lambda i: (layer, 0, 0, 0))],
                 out_specs=pl.BlockSpec((tr, k), lambda i: (i, 0)),
                 out_shape=jax.ShapeDtypeStruct((t, k), out_dtype))(a, w4)


def mm_tn(a, b, j, name, slot=0, into=None):
    t, m = a.shape
    nb = b.shape[1] // j
    tr = _mm_rows(t, 2176)

    def body(a_ref, b_ref, *rest):
        o_ref = rest[-1]

        @pl.when(pl.program_id(1) == 0)
        def _():
            o_ref[...] = jnp.zeros_like(o_ref)
        o_ref[0, 0] += _dot(a_ref[...], b_ref[...], 0, 0)

    in_specs = [pl.BlockSpec((tr, m), lambda jj, i: (i, 0)), pl.BlockSpec((tr, nb), lambda jj, i: (i, jj))]
    args = [a, b]
    alias = {}
    if into is not None:
        in_specs.append(pl.BlockSpec(memory_space=pl.ANY))
        args.append(into)
        alias = {2: 0}
    return pl.pallas_call(
        body, name=name, grid=(j, t // tr), in_specs=in_specs,
        out_specs=pl.BlockSpec((1, 1, m, nb), lambda jj, i: (slot, jj, 0, 0)),
        out_shape=jax.ShapeDtypeStruct((2, j, m, nb), F32), input_output_aliases=alias,
        compiler_params=pltpu.CompilerParams(vmem_limit_bytes=VMEM_LIMIT_BYTES))(*args)


def _mod_rows(mod_ref, i):
    ctx = i == 0
    sh = jnp.where(ctx, mod_ref[0, 0:1, :], mod_ref[1, 0:1, :])
    sc = jnp.where(ctx, mod_ref[0, 1:2, :], mod_ref[1, 1:2, :])
    gt = jnp.where(ctx, mod_ref[0, 2:3, :], mod_ref[1, 2:3, :])
    return sh, sc, gt


def _split_specs():
    return [pl.BlockSpec((TM, D), lambda i: (0, 0)), pl.BlockSpec((TM, D), lambda i: (jnp.maximum(i - 1, 0), 0))]


def _split_tile(c_ref, l_ref, i):
    return jnp.where(i == 0, c_ref[...], l_ref[...])


def pro_fwd(ctx, lat, g, mod, name):
    t = ctx.shape[0] + lat.shape[0]

    def body(c_ref, l_ref, g_ref, mod_ref, h_ref):
        i = pl.program_id(0)
        sh, sc, _ = _mod_rows(mod_ref, i)
        xv = _split_tile(c_ref, l_ref, i)
        r = lax.rsqrt(jnp.mean(xv * xv, axis=-1, keepdims=True) + EPS)
        h_ref[...] = ((xv * r) * g_ref[...] * (1.0 + sc) + sh).astype(BF)

    return _call(body, name=name, grid=(t // TM,),
                 in_specs=_split_specs() + [_full((1, D)), _full((2, 3, D))],
                 out_specs=pl.BlockSpec((TM, D), lambda i: (i, 0)),
                 out_shape=jax.ShapeDtypeStruct((t, D), BF))(ctx, lat, g, mod)


def pro_bwd(ctx, lat, dh, dxn, g, mod, name):
    t = ctx.shape[0] + lat.shape[0]

    def body(c_ref, l_ref, dh_ref, dxn_ref, g_ref, mod_ref, dx_ref, dmod_ref, dg_ref):
        i = pl.program_id(0)

        @pl.when(i == 0)
        def _():
            dmod_ref[...] = jnp.zeros_like(dmod_ref)
            dg_ref[...] = jnp.zeros_like(dg_ref)

        _, sc, _ = _mod_rows(mod_ref, i)
        xv = _split_tile(c_ref, l_ref, i)
        gv = g_ref[...]
        r = lax.rsqrt(jnp.mean(xv * xv, axis=-1, keepdims=True) + EPS)
        xn = xv * r
        dh_v = dh_ref[...]
        e = dh_v * (1.0 + sc)
        dsh = jnp.sum(dh_v, axis=0, keepdims=True)
        dsc = jnp.sum(dh_v * xn * gv, axis=0, keepdims=True)
        dg_ref[...] += jnp.sum(e * xn, axis=0, keepdims=True)
        dxh = e * gv

        @pl.when(i == 0)
        def _():
            dmod_ref[0, 0:1, :] += dsh
            dmod_ref[0, 1:2, :] += dsc

        @pl.when(i > 0)
        def _():
            dx_ref[...] = dxn_ref[...] + r * (dxh - xn * jnp.mean(dxh * xn, axis=-1, keepdims=True))
            dmod_ref[1, 0:1, :] += dsh
            dmod_ref[1, 1:2, :] += dsc

    tile = pl.BlockSpec((TM, D), lambda i: (i, 0))
    return _call(body, name=name, grid=(t // TM,),
                 in_specs=_split_specs() + [tile, tile, _full((1, D)), _full((2, 3, D))],
                 out_specs=[_split_specs()[1], _full((2, 2, D)), _full((1, D))],
                 out_shape=[jax.ShapeDtypeStruct(lat.shape, F32), jax.ShapeDtypeStruct((2, 2, D), F32),
                            jax.ShapeDtypeStruct((1, D), F32)])(ctx, lat, dh, dxn, g, mod)


def res_fwd(x, o, mod, update_ctx, name):
    t = x.shape[0]

    def body(x_ref, o_ref, mod_ref, y_ref):
        i = pl.program_id(0)
        _, _, gt = _mod_rows(mod_ref, i)
        upd = x_ref[...] + gt * o_ref[...]
        if update_ctx:
            y_ref[...] = upd
        else:
            y_ref[...] = jnp.where(i == 0, x_ref[...], upd)

    tile = pl.BlockSpec((TM, D), lambda i: (i, 0))
    return _call(body, name=name, grid=(t // TM,), in_specs=[tile, tile, _full((2, 3, D))],
                 out_specs=tile, out_shape=jax.ShapeDtypeStruct((t, D), F32))(x, o, mod)


def res_pro_fwd(x, o, mod, update_ctx, g_next, mod_next, name):
    split = isinstance(x, tuple)
    xs = list(x) if split else [x]
    t = o.shape[0]

    def body(*refs):
        x_refs = refs[:len(xs)]
        o_ref, mod_ref, g_ref, modn_ref, y_ref, h_ref = refs[len(xs):]
        i = pl.program_id(0)
        _, _, gt = _mod_rows(mod_ref, i)
        xv = _split_tile(x_refs[0], x_refs[1], i) if split else x_refs[0][...]
        xn = xv + gt * o_ref[...]
        if not update_ctx:
            xn = jnp.where(i == 0, xv, xn)
        y_ref[...] = xn
        sh, sc, _ = _mod_rows(modn_ref, i)
        r = lax.rsqrt(jnp.mean(xn * xn, axis=-1, keepdims=True) + EPS)
        h_ref[...] = ((xn * r) * g_ref[...] * (1.0 + sc) + sh).astype(BF)

    tile = pl.BlockSpec((TM, D), lambda i: (i, 0))
    return _call(body, name=name, grid=(t // TM,),
                 in_specs=(_split_specs() if split else [tile]) + [tile, _full((2, 3, D)), _full((1, D)), _full((2, 3, D))],
                 out_specs=[tile, tile],
                 out_shape=[jax.ShapeDtypeStruct((t, D), F32), jax.ShapeDtypeStruct((t, D), BF)])(
        *xs, o, mod, g_next, mod_next)


def _res_bwd_part(dx, o_ref, modp_ref, do_ref, dgt_ref, i, update_ctx):
    _, _, gtp = _mod_rows(modp_ref, i)
    do = gtp * dx
    dgt = jnp.sum(dx * o_ref[...], axis=0, keepdims=True)
    if update_ctx:
        do_ref[...] = do.astype(BF)

        @pl.when(i == 0)
        def _():
            dgt_ref[0:1, :] += dgt
    else:
        do_ref[...] = jnp.where(i == 0, jnp.zeros_like(do), do).astype(BF)

    @pl.when(i > 0)
    def _():
        dgt_ref[1:2, :] += dgt


def pro_res_bwd(x, dh, dxn, g, mod, o_prev, mod_prev, name):
    t = x.shape[0]

    def body(x_ref, dh_ref, dxn_ref, g_ref, mod_ref, o_ref, modp_ref, dx_ref, dmod_ref, dg_ref, do_ref, dgt_ref):
        i = pl.program_id(0)

        @pl.when(i == 0)
        def _():
            dmod_ref[...] = jnp.zeros_like(dmod_ref)
            dg_ref[...] = jnp.zeros_like(dg_ref)
            dgt_ref[...] = jnp.zeros_like(dgt_ref)

        _, sc, _ = _mod_rows(mod_ref, i)
        xv = x_ref[...]
        gv = g_ref[...]
        r = lax.rsqrt(jnp.mean(xv * xv, axis=-1, keepdims=True) + EPS)
        xn = xv * r
        dh_v = dh_ref[...]
        e = dh_v * (1.0 + sc)
        dsh = jnp.sum(dh_v, axis=0, keepdims=True)
        dsc = jnp.sum(dh_v * xn * gv, axis=0, keepdims=True)
        dg_ref[...] += jnp.sum(e * xn, axis=0, keepdims=True)
        dxh = e * gv
        dx = dxn_ref[...] + r * (dxh - xn * jnp.mean(dxh * xn, axis=-1, keepdims=True))
        dx_ref[...] = dx

        @pl.when(i == 0)
        def _():
            dmod_ref[0, 0:1, :] += dsh
            dmod_ref[0, 1:2, :] += dsc

        @pl.when(i > 0)
        def _():
            dmod_ref[1, 0:1, :] += dsh
            dmod_ref[1, 1:2, :] += dsc

        _res_bwd_part(dx, o_ref, modp_ref, do_ref, dgt_ref, i, True)

    tile = pl.BlockSpec((TM, D), lambda i: (i, 0))
    return _call(body, name=name, grid=(t // TM,),
                 in_specs=[tile, tile, tile, _full((1, D)), _full((2, 3, D)), tile, _full((2, 3, D))],
                 out_specs=[tile, _full((2, 2, D)), _full((1, D)), tile, _full((2, D))],
                 out_shape=[jax.ShapeDtypeStruct((t, D), F32), jax.ShapeDtypeStruct((2, 2, D), F32),
                            jax.ShapeDtypeStruct((1, D), F32), jax.ShapeDtypeStruct((t, D), BF),
                            jax.ShapeDtypeStruct((2, D), F32)])(x, dh, dxn, g, mod, o_prev, mod_prev)


def final_loss(x, target, g, o_last, mod_last):
    t = x.shape[0]

    def body(x_ref, t_ref, g_ref, o_ref, modp_ref, loss_ref, dx_ref, dg_ref, do_ref, dgt_ref):
        i = pl.program_id(0)

        @pl.when(i == 0)
        def _():
            loss_ref[...] = jnp.zeros_like(loss_ref)
            dg_ref[...] = jnp.zeros_like(dg_ref)
            dx_ref[...] = jnp.zeros_like(dx_ref)
            do_ref[...] = jnp.zeros_like(do_ref)
            dgt_ref[...] = jnp.zeros_like(dgt_ref)

        @pl.when(i > 0)
        def _():
            xv = x_ref[...]
            gv = g_ref[...]
            r = lax.rsqrt(jnp.mean(xv * xv, axis=-1, keepdims=True) + EPS)
            xn = xv * r
            err = xn * gv - t_ref[...]
            loss_ref[...] += (0.5 / D) * jnp.sum(jnp.sum(err * err, axis=1, keepdims=True), axis=0, keepdims=True)
            dy = err * (1.0 / D)
            dg_ref[...] += jnp.sum(dy * xn, axis=0, keepdims=True)
            dxh = dy * gv
            dx = r * (dxh - xn * jnp.mean(dxh * xn, axis=-1, keepdims=True))
            dx_ref[...] = dx
            do_ref[...] = (modp_ref[1, 2:3, :] * dx).astype(BF)
            dgt_ref[1:2, :] += jnp.sum(dx * o_ref[...], axis=0, keepdims=True)

    tile = pl.BlockSpec((TM, D), lambda i: (i, 0))
    return _call(body, name="final_loss", grid=(t // TM,),
                 in_specs=[tile, pl.BlockSpec((TM, D), lambda i: (jnp.maximum(i - 1, 0), 0)), _full((1, D)), tile,
                           _full((2, 3, D))],
                 out_specs=[_full((1, 1)), tile, _full((1, D)), tile, _full((2, D))],
                 out_shape=[jax.ShapeDtypeStruct((1, 1), F32), jax.ShapeDtypeStruct((t, D), F32),
                            jax.ShapeDtypeStruct((1, D), F32), jax.ShapeDtypeStruct((t, D), BF),
                            jax.ShapeDtypeStruct((2, D), F32)])(x, target, g, o_last, mod_last)


def ada_fwd(cond, w_mod, b_mod):
    nl, _, nw = w_mod.shape

    def body(c_ref, w_ref, b_ref, o_ref):
        cv = c_ref[...]
        s = (cv * _sig(cv)).astype(BF)
        o_ref[0] = _dot(s, w_ref[0].astype(BF)) + b_ref[0]

    return _call(body, name="ada_fwd", grid=(nl,),
                 in_specs=[_full((8, D)), pl.BlockSpec((1, D, nw), lambda l: (l, 0, 0)),
                           pl.BlockSpec((1, 1, nw), lambda l: (l, 0, 0))],
                 out_specs=pl.BlockSpec((1, 8, nw), lambda l: (l, 0, 0)),
                 out_shape=jax.ShapeDtypeStruct((nl, 8, nw), F32))(cond, w_mod, b_mod)


def ada_bwd(cond, dm, w_mod):
    nl, _, nw = w_mod.shape

    def body(c_ref, dm_ref, w_ref, gw_ref, dcc_ref, dc_ref):
        l = pl.program_id(0)

        @pl.when(l == 0)
        def _():
            dc_ref[...] = jnp.zeros_like(dc_ref)

        cv = c_ref[...]
        sg = _sig(cv)
        s = (cv * sg).astype(BF)
        dmv = dm_ref[0].astype(BF)
        gw_ref[0] = _dot(s, dmv, 0, 0)
        dc_ref[...] += _dot(dmv, w_ref[0].astype(BF), 1, 1)

        @pl.when(l == nl - 1)
        def _():
            dcond = dc_ref[...] * (sg * (1.0 + cv * (1.0 - sg)))
            dcc_ref[...] = jnp.sum(dcond[4:8], axis=0, keepdims=True)

    return _call(body, name="ada_bwd", grid=(nl,),
                 in_specs=[_full((8, D)), pl.BlockSpec((1, 8, nw), lambda l: (l, 0, 0)),
                           pl.BlockSpec((1, D, nw), lambda l: (l, 0, 0))],
                 out_specs=[pl.BlockSpec((1, D, nw), lambda l: (l, 0, 0)), _full((1, D))],
                 out_shape=[jax.ShapeDtypeStruct((nl, D, nw), F32), jax.ShapeDtypeStruct((1, D), F32)],
                 scratch=[pltpu.VMEM((8, D), F32)])(cond, dm, w_mod)


def add2(a, b, name):
    def body(a_ref, b_ref, o_ref):
        o_ref[...] = a_ref[...] + b_ref[...]

    return _call(body, name=name, out_shape=jax.ShapeDtypeStruct(a.shape, a.dtype))(a, b)


AW = 512
NGRP = 4


def Y4_SPEC():
    return pl.BlockSpec((AW // 128, TM, 128), lambda i: (0, i, 0))


def _cat_lanes(ref):
    return jnp.concatenate([ref[q] for q in range(ref.shape[0])], axis=1)


def _gelu(y):
    t = jnp.tanh(GELU_K * (y + GELU_C * y * y * y))
    return 0.5 * y * (1.0 + t), t


def _layer_norm_stats(v):
    mu = jnp.mean(v, axis=-1, keepdims=True)
    vc = v - mu
    rstd = lax.rsqrt(jnp.mean(vc * vc, axis=-1, keepdims=True) + EPS)
    return vc * rstd, rstd


def _spatial_mix(vn_ref, ws_ref, bs_ref, mixed_ref):
    for ch in range(TM // CHUNK):
        rows = slice(ch * CHUNK, (ch + 1) * CHUNK)
        for g in range(NGRP):
            cols = slice(g * CHUNK, (g + 1) * CHUNK)
            mixed_ref[rows, cols] = _dot(ws_ref[g], vn_ref[rows, cols]) + bs_ref[g]


def mix_fwd(p, yf, yb, vg, ws, bs, dsk, wglu, bglu, name):
    t = p.shape[0]
    wglu, glu_layer = _layer_of(wglu)

    def body(p_ref, yf_ref, yb_ref, vg_ref, ws_ref, bs_ref, d_ref, wg_ref, bg_ref, o_ref, vn_ref, mixed_ref):
        vhat, _ = _layer_norm_stats(p_ref[:, AW:2 * AW])
        vn_ref[...] = (vhat * vg_ref[...]).astype(BF)
        _spatial_mix(vn_ref, ws_ref, bs_ref, mixed_ref)
        ga = p_ref[:, 2 * AW:3 * AW]
        o_ref[:, 0:AW] = (p_ref[:, 0:AW] * mixed_ref[...] * (ga * _sig(ga))).astype(BF)
        y = _cat_lanes(yf_ref) + _cat_lanes(yb_ref) + d_ref[...] * p_ref[:, 3 * AW:4 * AW]
        y2, _ = _gelu(y)
        z = _dot(y2.astype(BF), wg_ref[0]) + bg_ref[...]
        gb = p_ref[:, 4 * AW:5 * AW]
        o_ref[:, AW:2 * AW] = (y2 * _sig(z) * (gb * _sig(gb))).astype(BF)

    tile = lambda w: pl.BlockSpec((TM, w), lambda i: (i, 0))
    return _call(body, name=name, grid=(t // TM,),
                 in_specs=[tile(5 * AW), Y4_SPEC(), Y4_SPEC(), _full((1, AW)), _full((NGRP, CHUNK, CHUNK)),
                           _full((NGRP, CHUNK, 1)), _full((1, AW)),
                           pl.BlockSpec((1, AW, AW), lambda i: (glu_layer, 0, 0)), _full((1, AW))],
                 out_specs=tile(2 * AW), out_shape=jax.ShapeDtypeStruct((t, 2 * AW), BF),
                 scratch=[pltpu.VMEM((TM, AW), BF), pltpu.VMEM((TM, AW), F32)])(p, yf, yb, vg, ws, bs, dsk, wglu, bglu)


def mix_bwd(p, yf, yb, dmix, vg, ws, bs, dsk, wglu, bglu, name):
    t = p.shape[0]
    wglu, glu_layer = _layer_of(wglu)

    def body(p_ref, yf_ref, yb_ref, dm_ref, vg_ref, ws_ref, bs_ref, d_ref, wg_ref, bg_ref,
             dpa_ref, dy_ref, dws_ref, dbs_ref, dvg_ref, dd_ref, dwg_ref, dbg_ref,
             vn_ref, mixed_ref, dmx_ref, dvn_ref):
        @pl.when(pl.program_id(0) == 0)
        def _():
            for r in (dws_ref, dbs_ref, dvg_ref, dd_ref, dwg_ref, dbg_ref):
                r[...] = jnp.zeros_like(r)

        vhat, rstd = _layer_norm_stats(p_ref[:, AW:2 * AW])
        vgv = vg_ref[...]
        vn_ref[...] = (vhat * vgv).astype(BF)
        _spatial_mix(vn_ref, ws_ref, bs_ref, mixed_ref)
        u = p_ref[:, 0:AW]
        ga = p_ref[:, 2 * AW:3 * AW]
        sga = _sig(ga)
        dya = dm_ref[:, 0:AW]
        mixed = mixed_ref[...]
        dpa_ref[:, 0:AW] = (dya * mixed * (ga * sga)).astype(BF)
        dpa_ref[:, 2 * AW:3 * AW] = (dya * u * mixed * (sga * (1.0 + ga * (1.0 - sga)))).astype(BF)
        dmx_ref[...] = dya * u * (ga * sga)
        for ch in range(TM // CHUNK):
            rows = slice(ch * CHUNK, (ch + 1) * CHUNK)
            for g in range(NGRP):
                cols = slice(g * CHUNK, (g + 1) * CHUNK)
                dmx = dmx_ref[rows, cols]
                dmxb = dmx.astype(BF)
                dws_ref[g] += _dot(dmxb, vn_ref[rows, cols], 1, 1)
                dbs_ref[g] += jnp.sum(dmx, axis=1, keepdims=True)
                dvn_ref[rows, cols] = _dot(ws_ref[g], dmxb, 0, 0)
        dvn = dvn_ref[...]
        dvg_ref[...] += jnp.sum(dvn * vhat, axis=0, keepdims=True)
        dvh = dvn * vgv
        dpa_ref[:, AW:2 * AW] = (rstd * (dvh - jnp.mean(dvh, axis=-1, keepdims=True)
                                         - vhat * jnp.mean(dvh * vhat, axis=-1, keepdims=True))).astype(BF)

        xs = p_ref[:, 3 * AW:4 * AW]
        y = _cat_lanes(yf_ref) + _cat_lanes(yb_ref) + d_ref[...] * xs
        y2, th = _gelu(y)
        y2b = y2.astype(BF)
        z = _dot(y2b, wg_ref[0]) + bg_ref[...]
        sz = _sig(z)
        gb = p_ref[:, 4 * AW:5 * AW]
        sgb = _sig(gb)
        dyb = dm_ref[:, AW:2 * AW]
        dpa_ref[:, 4 * AW:5 * AW] = (dyb * (y2 * sz) * (sgb * (1.0 + gb * (1.0 - sgb)))).astype(BF)
        dy3 = dyb * (gb * sgb)
        dz = dy3 * y2 * sz * (1.0 - sz)
        dzb = dz.astype(BF)
        dwg_ref[...] += _dot(y2b, dzb, 0, 0)
        dbg_ref[...] += jnp.sum(dz, axis=0, keepdims=True)
        dy2 = dy3 * sz + _dot(dzb, wg_ref[0], 1, 1)
        dgelu = 0.5 * (1.0 + th) + 0.5 * y * (1.0 - th * th) * GELU_K * (1.0 + 3.0 * GELU_C * y * y)
        dy = dy2 * dgelu
        dd_ref[...] += jnp.sum(dy * xs, axis=0, keepdims=True)
        dy_ref[...] = dy

    tile = lambda w: pl.BlockSpec((TM, w), lambda i: (i, 0))
    return _call(body, name=name, grid=(t // TM,),
                 in_specs=[tile(5 * AW), Y4_SPEC(), Y4_SPEC(), tile(2 * AW), _full((1, AW)), _full((NGRP, CHUNK, CHUNK)),
                           _full((NGRP, CHUNK, 1)), _full((1, AW)),
                           pl.BlockSpec((1, AW, AW), lambda i: (glu_layer, 0, 0)), _full((1, AW))],
                 out_specs=[tile(5 * AW), tile(AW), _full((NGRP, CHUNK, CHUNK)), _full((NGRP, CHUNK, 1)),
                            _full((1, AW)), _full((1, AW)), _full((AW, AW)), _full((1, AW))],
                 out_shape=[jax.ShapeDtypeStruct((t, 5 * AW), BF),
                            jax.ShapeDtypeStruct((t, AW), F32), jax.ShapeDtypeStruct((NGRP, CHUNK, CHUNK), F32),
                            jax.ShapeDtypeStruct((NGRP, CHUNK, 1), F32), jax.ShapeDtypeStruct((1, AW), F32),
                            jax.ShapeDtypeStruct((1, AW), F32), jax.ShapeDtypeStruct((AW, AW), F32),
                            jax.ShapeDtypeStruct((1, AW), F32)],
                 scratch=[pltpu.VMEM((TM, AW), BF), pltpu.VMEM((TM, AW), F32), pltpu.VMEM((TM, AW), F32),
                          pltpu.VMEM((TM, AW), F32)])(p, yf, yb, dmix, vg, ws, bs, dsk, wglu, bglu)


LN = 512
NBLK = SW // LN
UB = AW // NBLK
SCAN_R = 32
SCAN_G = TM // SCAN_R
PW_ROWS = SCAN_R
POW_EXP = list(range(1, SCAN_R + 1))


def s5_disc(lam_re, lam_im, dt, lam_re_r, lam_im_r, dt_r, b_re, b_im):
    nexp = jnp.asarray(np.array(POW_EXP, np.float32).reshape(PW_ROWS, 1))

    def body(n_ref, lr_ref, li_ref, dt_ref, lrr_ref, lir_ref, dtr_ref, br_ref, bi_ref,
             pr_ref, pi_ref, bbr_ref, bbi_ref):
        for dr in range(2):
            dtl = jnp.exp(dt_ref[dr:dr + 1, :])
            zr = lr_ref[dr:dr + 1, :] * dtl
            zi = li_ref[dr:dr + 1, :] * dtl
            mag = jnp.exp(n_ref[...] * zr)
            ang = n_ref[...] * zi
            pr_ref[dr] = mag * jnp.cos(ang)
            pi_ref[dr] = mag * jnp.sin(ang)
        lr, li, dtv = lrr_ref[...], lir_ref[...], jnp.exp(dtr_ref[...])
        mag = jnp.exp(lr * dtv)
        nr = mag * jnp.cos(li * dtv) - 1.0
        ni = mag * jnp.sin(li * dtv)
        den = lr * lr + li * li
        fr = (nr * lr + ni * li) / den
        fi = (ni * lr - nr * li) / den
        bbr_ref[...] = fr * br_ref[...] - fi * bi_ref[...]
        bbi_ref[...] = fr * bi_ref[...] + fi * br_ref[...]

    rows = lam_re_r.shape[0]
    return _call(body, name="s5_disc",
                 out_shape=[jax.ShapeDtypeStruct((2, PW_ROWS, SW), F32), jax.ShapeDtypeStruct((2, PW_ROWS, SW), F32),
                            jax.ShapeDtypeStruct((rows, SP), F32), jax.ShapeDtypeStruct((rows, SP), F32)])(
        nexp, lam_re, lam_im, dt, lam_re_r, lam_im_r, dt_r, b_re, b_im)


def s5_param_bwd(lam_re_r, lam_im_r, dt_r, b_re, b_im, da_re, da_im, dbb_re, dbb_im):
    rows = lam_re_r.shape[0]
    ng = rows // SH
    seg = jnp.asarray(np.kron(np.eye(ng, dtype=np.float32), np.ones((1, SH), np.float32)))

    def body(seg_ref, lr_ref, li_ref, dt_ref, br_ref, bi_ref, dar_ref, dai_ref, dbbr_ref, dbbi_ref,
             dlr_ref, dli_ref, ddt_ref, dbr_ref, dbi_ref):
        lr, li, dtv = lr_ref[...], li_ref[...], jnp.exp(dt_ref[...])
        mag = jnp.exp(lr * dtv)
        lbr = mag * jnp.cos(li * dtv)
        lbi = mag * jnp.sin(li * dtv)
        den = lr * lr + li * li
        nr, ni = lbr - 1.0, lbi
        fr = (nr * lr + ni * li) / den
        fi = (ni * lr - nr * li) / den
        br, bi = br_ref[...], bi_ref[...]
        gbr, gbi = dbbr_ref[...], dbbi_ref[...]
        dbr_ref[...] = gbr * fr + gbi * fi
        dbi_ref[...] = gbi * fr - gbr * fi
        gfr = gbr * br + gbi * bi
        gfi = gbi * br - gbr * bi
        ilr, ili = lr / den, -li / den
        gnr = gfr * ilr + gfi * ili
        gni = gfi * ilr - gfr * ili
        qr = -(fr * ilr - fi * ili)
        qi = -(fr * ili + fi * ilr)
        glr = gfr * qr + gfi * qi
        gli = gfi * qr - gfr * qi
        first = (lax.broadcasted_iota(jnp.int32, (rows, 1), 0) % SH) == 0
        glbr = gnr + jnp.where(first, dar_ref[...], 0.0)
        glbi = gni + jnp.where(first, dai_ref[...], 0.0)
        gzr = glbr * lbr + glbi * lbi
        gzi = glbi * lbr - glbr * lbi
        glr = glr + gzr * dtv
        gli = gli + gzi * dtv
        gdt = (gzr * lr + gzi * li) * dtv
        hi = lax.Precision.HIGHEST
        sg = seg_ref[...]
        dlr_ref[...] = jnp.dot(sg, glr, precision=hi, preferred_element_type=F32)
        dli_ref[...] = jnp.dot(sg, gli, precision=hi, preferred_element_type=F32)
        ddt_ref[...] = jnp.sum(jnp.dot(sg, gdt, precision=hi, preferred_element_type=F32), axis=1, keepdims=True)

    return _call(body, name="s5_param_bwd",
                 out_shape=[jax.ShapeDtypeStruct((ng, SP), F32), jax.ShapeDtypeStruct((ng, SP), F32),
                            jax.ShapeDtypeStruct((ng, 1), F32), jax.ShapeDtypeStruct((rows, SP), F32),
                            jax.ShapeDtypeStruct((rows, SP), F32)])(
        seg, lam_re_r, lam_im_r, dt_r, b_re, b_im, da_re, da_im, dbb_re, dbb_im)


def _tile_order(kind, nt):
    if kind == "fwd":
        return lambda i: i
    if kind == "bwd":
        return lambda i: jnp.where(i == 0, 0, nt - i)
    if kind == "fwd_adj":
        return lambda i: nt - 1 - i
    if kind == "bwd_adj":
        return lambda i: jnp.where(i == nt - 1, 0, i + 1)
    raise ValueError(kind)


XS_BLK = 3 * AW // 128


def _load_perm(refs):
    return jnp.concatenate(
        [jnp.concatenate([ref[pl.ds(r, SCAN_G, stride=SCAN_R), :] for ref in refs], axis=1) for r in range(SCAN_R)],
        axis=0)


def _store_perm(out_ref, val):
    for r in range(SCAN_R):
        for q in range(AW // 128):
            out_ref[q, pl.ds(r, SCAN_G, stride=SCAN_R), :] = val[r * SCAN_G:(r + 1) * SCAN_G, q * 128:(q + 1) * 128]


def _scan2(br_ref, bi_ref, or_ref, oi_ref, h_off, cin_off, er_ref, ei_ref, cr_ref, ci_ref, pr_ref, pi_ref, reverse,
           corr=None):
    gpt = SCAN_G
    nr = SCAN_R
    offsets = list(range(nr))[::-1] if reverse else list(range(nr))
    blocks = [slice(b * LN, (b + 1) * LN) for b in range(NBLK)]
    slab = lambda r: slice(r * gpt, (r + 1) * gpt)
    a1 = [(pr_ref[0:1, c], pi_ref[0:1, c]) for c in blocks]
    x = [None] * NBLK
    for r in offsets:
        for b, c in enumerate(blocks):
            if x[b] is None:
                x[b] = (br_ref[slab(r), c], bi_ref[slab(r), c])
            else:
                (ar, ai), (xr, xi) = a1[b], x[b]
                x[b] = (br_ref[slab(r), c] + ar * xr - ai * xi, bi_ref[slab(r), c] + ar * xi + ai * xr)
    an = [(pr_ref[nr - 1:nr, c], pi_ref[nr - 1:nr, c]) for c in blocks]
    k = [(cr_ref[:, c], ci_ref[:, c]) for c in blocks]
    for g in (range(gpt - 1, -1, -1) if reverse else range(gpt)):
        for b, c in enumerate(blocks):
            (ar, ai), (kr, ki), (xr, xi) = an[b], k[b], x[b]
            er_ref[g:g + 1, c] = kr
            ei_ref[g:g + 1, c] = ki
            k[b] = (xr[g:g + 1, :] + ar * kr - ai * ki, xi[g:g + 1, :] + ar * ki + ai * kr)
    for b, c in enumerate(blocks):
        cr_ref[:, c] = k[b][0]
        ci_ref[:, c] = k[b][1]
        x[b] = (er_ref[:, c], ei_ref[:, c])
        if cin_off is not None:
            or_ref[cin_off:cin_off + gpt, c] = x[b][0]
            oi_ref[cin_off:cin_off + gpt, c] = x[b][1]
    acc = [None] * NBLK
    for r in offsets:
        for b, c in enumerate(blocks):
            (ar, ai), (xr, xi) = a1[b], x[b]
            x[b] = (br_ref[slab(r), c] + ar * xr - ai * xi, bi_ref[slab(r), c] + ar * xi + ai * xr)
            or_ref[h_off + r * gpt:h_off + (r + 1) * gpt, c] = x[b][0]
            oi_ref[h_off + r * gpt:h_off + (r + 1) * gpt, c] = x[b][1]
            if corr is not None:
                wr_ref, wi_ref, w_off = corr[:3]
                wr, wi = wr_ref[w_off + r * gpt:w_off + (r + 1) * gpt, c], wi_ref[w_off + r * gpt:w_off + (r + 1) * gpt, c]
                pr_, pi_ = x[b][0] * wr + x[b][1] * wi, x[b][1] * wr - x[b][0] * wi
                acc[b] = (pr_, pi_) if acc[b] is None else (acc[b][0] + pr_, acc[b][1] + pi_)
    if corr is not None:
        sr_ref, si_ref = corr[3:]
        for b, c in enumerate(blocks):
            sr_ref[:, c] += jnp.sum(acc[b][0], axis=0, keepdims=True)
            si_ref[:, c] += jnp.sum(acc[b][1], axis=0, keepdims=True)


HS_ROWS = TM + SCAN_G


def _hs_offsets(reverse):
    return (0, SCAN_G) if reverse else (SCAN_G, 0)


def _dir_spec(dr, shape):
    return pl.BlockSpec((1,) + shape, lambda i: (dr,) + (0,) * len(shape))


def s5q_fwd(p, bb_re, bb_im, ct_re, ct_im, pw_re, pw_im, dr, reverse, name):
    t = p.shape[0]
    nt = t // TM
    order = _tile_order("bwd" if reverse else "fwd", nt)
    nq = AW // 128
    h_off, p_off = _hs_offsets(reverse)

    def body(*refs):
        x_refs = refs[:nq]
        bbr_ref, bbi_ref, ctr_ref, cti_ref, pr_ref, pi_ref = [r.at[0] for r in refs[nq:nq + 6]]
        y_ref, hsr_ref, hsi_ref = refs[nq + 6:nq + 9]
        br_ref, bi_ref, er_ref, ei_ref, cr_ref, ci_ref = refs[nq + 9:]

        @pl.when(pl.program_id(0) == 0)
        def _():
            cr_ref[...] = jnp.zeros_like(cr_ref)
            ci_ref[...] = jnp.zeros_like(ci_ref)

        xb = _load_perm(x_refs).astype(BF)
        for j in range(NBLK):
            cols = slice(j * LN, (j + 1) * LN)
            br_ref[:, cols] = _dot(xb[:, j * UB:(j + 1) * UB], bbr_ref[j])
            bi_ref[:, cols] = _dot(xb[:, j * UB:(j + 1) * UB], bbi_ref[j])
        _scan2(br_ref, bi_ref, hsr_ref, hsi_ref, h_off, TM if reverse else 0, er_ref, ei_ref, cr_ref, ci_ref,
               pr_ref, pi_ref, reverse)
        y = jnp.concatenate(
            [_dot(hsr_ref[h_off:h_off + TM, j * LN:(j + 1) * LN].astype(BF), ctr_ref[j])
             - _dot(hsi_ref[h_off:h_off + TM, j * LN:(j + 1) * LN].astype(BF), cti_ref[j]) for j in range(NBLK)], axis=1)
        _store_perm(y_ref, y)

    state = lambda: pl.BlockSpec((HS_ROWS, SW), lambda i: (order(i), 0))
    xspec = lambda q: pl.BlockSpec((TM, 128), lambda i: (order(i), XS_BLK + q))
    return _call(body, name=name, grid=(nt,),
                 in_specs=[xspec(q) for q in range(nq)]
                 + [_dir_spec(dr, (NBLK, UB, LN)), _dir_spec(dr, (NBLK, UB, LN)), _dir_spec(dr, (NBLK, LN, UB)),
                    _dir_spec(dr, (NBLK, LN, UB)), _dir_spec(dr, (PW_ROWS, SW)), _dir_spec(dr, (PW_ROWS, SW))],
                 out_specs=[pl.BlockSpec((nq, TM, 128), lambda i: (0, order(i), 0)), state(), state()],
                 out_shape=[jax.ShapeDtypeStruct((nq, t, 128), F32), jax.ShapeDtypeStruct((nt * HS_ROWS, SW), F32),
                            jax.ShapeDtypeStruct((nt * HS_ROWS, SW), F32)],
                 scratch=[pltpu.VMEM((TM, SW), F32), pltpu.VMEM((TM, SW), F32),
                          pltpu.VMEM((SCAN_G, SW), F32), pltpu.VMEM((SCAN_G, SW), F32),
                          pltpu.VMEM((1, SW), F32), pltpu.VMEM((1, SW), F32)])(
        *([p] * nq), bb_re, bb_im, ct_re, ct_im, pw_re, pw_im)


def s5q_bwd(p, hs_re, hs_im, dy, bb_re, bb_im, ct_re, ct_im, pw_re, pw_im_conj, dr, reverse, name):
    t = p.shape[0]
    nt = t // TM
    order = _tile_order("bwd_adj" if reverse else "fwd_adj", nt)
    nq = AW // 128
    h_off, p_off = _hs_offsets(reverse)

    def body(*refs):
        x_refs, dy_refs = refs[:nq], refs[nq:2 * nq]
        (hsr_ref, hsi_ref, bbr_ref, bbi_ref, ctr_ref, cti_ref, pr_ref, pi_ref,
         dx_ref, dar_ref, dai_ref, dbbr_ref, dbbi_ref, dcr_ref, dci_ref,
         qr_ref, qi_ref, gr_ref, gi_ref, er_ref, ei_ref, cr_ref, ci_ref) = refs[2 * nq:]
        bbr_ref, bbi_ref, ctr_ref, cti_ref, pr_ref, pi_ref = [
            r.at[0] for r in (bbr_ref, bbi_ref, ctr_ref, cti_ref, pr_ref, pi_ref)]

        @pl.when(pl.program_id(0) == 0)
        def _():
            for r in (cr_ref, ci_ref, dar_ref, dai_ref, dbbr_ref, dbbi_ref, dcr_ref, dci_ref):
                r[...] = jnp.zeros_like(r)

        xb = _load_perm(x_refs).astype(BF)
        dyb = _load_perm(dy_refs).astype(BF)
        for j in range(NBLK):
            cols = slice(j * LN, (j + 1) * LN)
            qr_ref[:, cols] = _dot(dyb[:, j * UB:(j + 1) * UB], ctr_ref[j], 1, 1)
            qi_ref[:, cols] = -_dot(dyb[:, j * UB:(j + 1) * UB], cti_ref[j], 1, 1)
        _scan2(qr_ref, qi_ref, gr_ref, gi_ref, 0, None, er_ref, ei_ref, cr_ref, ci_ref, pr_ref, pi_ref, not reverse,
               corr=(hsr_ref, hsi_ref, p_off, dar_ref, dai_ref))
        dxs = []
        for j in range(NBLK):
            cols = slice(j * LN, (j + 1) * LN)
            xj = xb[:, j * UB:(j + 1) * UB]
            dyj = dyb[:, j * UB:(j + 1) * UB]
            grb, gib = gr_ref[:, cols].astype(BF), gi_ref[:, cols].astype(BF)
            dcr_ref[j] += _dot(dyj, hsr_ref[h_off:h_off + TM, cols].astype(BF), 0, 0)
            dci_ref[j] += -_dot(dyj, hsi_ref[h_off:h_off + TM, cols].astype(BF), 0, 0)
            dbbr_ref[j] += _dot(xj, grb, 0, 0)
            dbbi_ref[j] += _dot(xj, gib, 0, 0)
            dxs.append(_dot(grb, bbr_ref[j], 1, 1) + _dot(gib, bbi_ref[j], 1, 1))
        _store_perm(dx_ref, jnp.concatenate(dxs, axis=1))

    state = lambda: pl.BlockSpec((HS_ROWS, SW), lambda i: (order(i), 0))
    blockd = lambda: _full((NBLK, UB, LN))
    xspec = lambda q: pl.BlockSpec((TM, 128), lambda i: (order(i), XS_BLK + q))
    dyspec = lambda q: pl.BlockSpec((TM, 128), lambda i: (order(i), q))
    return _call(body, name=name, grid=(nt,),
                 in_specs=[xspec(q) for q in range(nq)] + [dyspec(q) for q in range(nq)]
                 + [state(), state(), _dir_spec(dr, (NBLK, UB, LN)), _dir_spec(dr, (NBLK, UB, LN)),
                    _dir_spec(dr, (NBLK, LN, UB)), _dir_spec(dr, (NBLK, LN, UB)),
                    _dir_spec(dr, (PW_ROWS, SW)), _dir_spec(dr, (PW_ROWS, SW))],
                 out_specs=[pl.BlockSpec((nq, TM, 128), lambda i: (0, order(i), 0)), _full((1, SW)), _full((1, SW)),
                            blockd(), blockd(), blockd(), blockd()],
                 out_shape=[jax.ShapeDtypeStruct((nq, t, 128), F32), jax.ShapeDtypeStruct((1, SW), F32),
                            jax.ShapeDtypeStruct((1, SW), F32)] + [jax.ShapeDtypeStruct((NBLK, UB, LN), F32)] * 4,
                 scratch=[pltpu.VMEM((TM, SW), F32), pltpu.VMEM((TM, SW), F32),
                          pltpu.VMEM((TM, SW), F32), pltpu.VMEM((TM, SW), F32),
                          pltpu.VMEM((SCAN_G, SW), F32), pltpu.VMEM((SCAN_G, SW), F32),
                          pltpu.VMEM((1, SW), F32), pltpu.VMEM((1, SW), F32)])(
        *([p] * nq), *([dy] * nq), hs_re, hs_im, bb_re, bb_im, ct_re, ct_im, pw_re, pw_im_conj)


def s5p_dx_sum(dy, dsk, dxf, dxb, dp, name):
    t = dy.shape[0]
    nq = AW // 128

    def body(dy_ref, d_ref, f_ref, b_ref, dp_ref, o_ref):
        o_ref[...] = (dy_ref[...] * d_ref[...] + _cat_lanes(f_ref) + _cat_lanes(b_ref)).astype(BF)

    tr = _mm_rows(t)
    tile = pl.BlockSpec((tr, AW), lambda i: (i, 0))
    blk4 = pl.BlockSpec((nq, tr, 128), lambda i: (0, i, 0))
    return pl.pallas_call(
        body, name=name, grid=(t // tr,),
        in_specs=[tile, _full((1, AW)), blk4, blk4, pl.BlockSpec(memory_space=pl.ANY)],
        out_specs=pl.BlockSpec((tr, AW), lambda i: (i, 3)), out_shape=jax.ShapeDtypeStruct(dp.shape, dp.dtype),
        input_output_aliases={4: 0},
        compiler_params=pltpu.CompilerParams(vmem_limit_bytes=VMEM_LIMIT_BYTES))(dy, dsk, dxf, dxb, dp)


SCALE = HD ** -0.5
NHEAD_NORM = NQ + NKV


def _partner(x):
    half0 = (lax.broadcasted_iota(jnp.int32, (1, HD), 1) % 64) < 32
    return jnp.where(half0, pltpu.roll(x, HD - 32, 1), pltpu.roll(x, 32, 1))


def attn_prep(p, qg, kg, cos, sins, name):
    t = p.shape[0]

    def body(p_ref, qg_ref, kg_ref, cos_ref, sin_ref, o_ref):
        cv, sv = cos_ref[...], sin_ref[...]
        for h in range(NHEAD_NORM):
            cols = slice(h * HD, (h + 1) * HD)
            blk = p_ref[:, cols]
            r = lax.rsqrt(jnp.mean(blk * blk, axis=-1, keepdims=True) + EPS)
            xn = blk * r * (qg_ref[...] if h < NQ else kg_ref[...])
            rot = xn * cv + _partner(xn) * sv
            o_ref[:, cols] = ((rot * SCALE) if h < NQ else rot).astype(BF)
        vcols = slice(NHEAD_NORM * HD, (NHEAD_NORM + NKV) * HD)
        o_ref[:, vcols] = p_ref[:, vcols].astype(BF)

    w = (NHEAD_NORM + NKV) * HD
    tr = _mm_rows(t)
    tile = lambda ww: pl.BlockSpec((tr, ww), lambda i: (i, 0))
    return _call(body, name=name, grid=(t // tr,),
                 in_specs=[tile(w), _full((1, HD)), _full((1, HD)), tile(HD), tile(HD)],
                 out_specs=tile(w), out_shape=jax.ShapeDtypeStruct((t, w), BF))(p, qg, kg, cos, sins)


def attn_prep_bwd(p, dq, dk, dv, qg, kg, cos, sins, dp, name):
    t = p.shape[0]

    def body(p_ref, dq_ref, dk_ref, dv_ref, qg_ref, kg_ref, cos_ref, sin_ref, dp_ref, o_ref, dqg_ref, dkg_ref):
        @pl.when(pl.program_id(0) == 0)
        def _():
            dqg_ref[...] = jnp.zeros_like(dqg_ref)
            dkg_ref[...] = jnp.zeros_like(dkg_ref)

        cv, sv = cos_ref[...], sin_ref[...]
        for h in range(NHEAD_NORM):
            cols = slice(h * HD, (h + 1) * HD)
            blk = p_ref[:, cols]
            r = lax.rsqrt(jnp.mean(blk * blk, axis=-1, keepdims=True) + EPS)
            xh = blk * r
            if h < NQ:
                drot = dq_ref[:, cols] * SCALE
                gv, dg_ref = qg_ref[...], dqg_ref
            else:
                drot = dk_ref[:, (h - NQ) * HD:(h - NQ + 1) * HD]
                gv, dg_ref = kg_ref[...], dkg_ref
            dxn = drot * cv + _partner(drot * sv)
            dg_ref[...] += jnp.sum(dxn * xh, axis=0, keepdims=True)
            dxh = dxn * gv
            o_ref[:, cols] = (r * (dxh - xh * jnp.mean(dxh * xh, axis=-1, keepdims=True))).astype(BF)
        o_ref[:, NHEAD_NORM * HD:(NHEAD_NORM + NKV) * HD] = dv_ref[...].astype(BF)

    w = (NHEAD_NORM + NKV) * HD
    tr = _mm_rows(t)
    tile = lambda ww: pl.BlockSpec((tr, ww), lambda i: (i, 0))
    return pl.pallas_call(
        body, name=name, grid=(t // tr,),
        in_specs=[tile(w), tile(NQ * HD), tile(NKV * HD), tile(NKV * HD), _full((1, HD)), _full((1, HD)),
                  tile(HD), tile(HD), pl.BlockSpec(memory_space=pl.ANY)],
        out_specs=[tile(w), _full((1, HD)), _full((1, HD))],
        out_shape=[jax.ShapeDtypeStruct(dp.shape, dp.dtype), jax.ShapeDtypeStruct((1, HD), F32),
                   jax.ShapeDtypeStruct((1, HD), F32)],
        input_output_aliases={8: 0},
        compiler_params=pltpu.CompilerParams(vmem_limit_bytes=VMEM_LIMIT_BYTES))(p, dq, dk, dv, qg, kg, cos, sins, dp)


KCOL = NQ
VCOL = NQ + NKV
GCOL = (NQ + 2 * NKV)
QPK = NQ // NKV
ATT_KCHUNK = 512


def attn_fwd(qkv, p, name):
    t = qkv.shape[0]

    def body(q_ref, k_ref, v_ref, g_ref, o_ref, mix_ref, lse_ref):
        def attend(nk):
            for hh in range(QPK):
                attend_head(nk, slice(hh * HD, (hh + 1) * HD))

        def attend_head(nk, cols):
            q = q_ref[:, cols]
            chunks = [(k0, min(k0 + 2 * ATT_KCHUNK, nk)) for k0 in range(0, nk, 2 * ATT_KCHUNK)]
            s_next = _dot(q, k_ref[chunks[0][0]:chunks[0][1], :], 1, 1)
            m = l = acc = None
            for ci, (k0, k1) in enumerate(chunks):
                s = s_next
                if ci + 1 < len(chunks):
                    s_next = _dot(q, k_ref[chunks[ci + 1][0]:chunks[ci + 1][1], :], 1, 1)
                mc = jnp.max(s, axis=-1, keepdims=True)
                m_new = mc if m is None else jnp.maximum(m, mc)
                pe = jnp.exp(s - m_new)
                lc = jnp.sum(pe, axis=-1, keepdims=True)
                pv = _dot(pe.astype(BF), v_ref[k0:k1, :])
                if m is None:
                    l, acc = lc, pv
                else:
                    alpha = jnp.exp(m - m_new)
                    l, acc = alpha * l + lc, alpha * acc + pv
                m = m_new
            o = acc / l
            gt = g_ref[:, cols]
            o_ref[:, cols] = o
            mix_ref[:, cols] = (o * (gt * _sig(gt))).astype(BF)
            lse_ref[:, cols] = jnp.broadcast_to(m + jnp.log(l), (TM, HD))

        pl.when(pl.program_id(1) == 0)(lambda: attend(NC))
        pl.when(pl.program_id(1) > 0)(lambda: attend(t))

    blk = pl.BlockSpec((TM, QPK * HD), lambda kv, i: (i, kv))
    return _call(body, name=name, grid=(NKV, t // TM),
                 in_specs=[blk, pl.BlockSpec((t, HD), lambda kv, i: (0, KCOL + kv)),
                           pl.BlockSpec((t, HD), lambda kv, i: (0, VCOL + kv)),
                           pl.BlockSpec((TM, QPK * HD), lambda kv, i: (i, GCOL // QPK + kv))],
                 out_specs=[blk, blk, blk],
                 out_shape=[jax.ShapeDtypeStruct((t, NQ * HD), F32), jax.ShapeDtypeStruct((t, NQ * HD), BF),
                            jax.ShapeDtypeStruct((t, NQ * HD), F32)])(qkv, qkv, qkv, p)


def attn_bwd(qkv, p, dmix, o, lse, name):
    t = qkv.shape[0]

    def body(q_ref, k_ref, v_ref, g_ref, dm_ref, o_ref, lse_ref, dq_ref, dg_ref, dk_ref, dv_ref):
        i = pl.program_id(1)

        @pl.when(i == 0)
        def _():
            dk_ref[...] = jnp.zeros_like(dk_ref)
            dv_ref[...] = jnp.zeros_like(dv_ref)

        def bwd(nk):
            for hh in range(QPK):
                bwd_head(nk, slice(hh * HD, (hh + 1) * HD))

        def bwd_head(nk, cols):
            gt = g_ref[:, cols]
            sg = _sig(gt)
            ov = o_ref[:, cols]
            dmv = dm_ref[:, cols]
            dg_ref[:, cols] = (dmv * ov * (sg * (1.0 + gt * (1.0 - sg)))).astype(BF)
            do = dmv * (gt * sg)
            dr = jnp.sum(do * ov, axis=-1, keepdims=True)
            dob = do.astype(BF)
            q = q_ref[:, cols]
            lse = lse_ref[:, cols][:, 0:1]
            chunks = [slice(k0, min(k0 + ATT_KCHUNK, nk)) for k0 in range(0, nk, ATT_KCHUNK)]
            nxt = (_dot(q, k_ref[chunks[0], :], 1, 1), _dot(dob, v_ref[chunks[0], :], 1, 1))
            dq = None
            for ci, keys in enumerate(chunks):
                s, dp = nxt
                if ci + 1 < len(chunks):
                    nxt = (_dot(q, k_ref[chunks[ci + 1], :], 1, 1), _dot(dob, v_ref[chunks[ci + 1], :], 1, 1))
                pe = jnp.exp(s - lse)
                dsb = (pe * (dp - dr)).astype(BF)
                part = _dot(dsb, k_ref[keys, :])
                dq = part if dq is None else dq + part
                dv_ref[keys, :] += _dot(pe.astype(BF), dob, 0, 0)
                dk_ref[keys, :] += _dot(dsb, q, 0, 0)
            dq_ref[:, cols] = dq

        pl.when(i == 0)(lambda: bwd(NC))
        pl.when(i > 0)(lambda: bwd(t))

    blk = pl.BlockSpec((TM, QPK * HD), lambda kv, i: (i, kv))
    gate = pl.BlockSpec((TM, QPK * HD), lambda kv, i: (i, GCOL // QPK + kv))
    acc = pl.BlockSpec((t, HD), lambda kv, i: (0, kv))
    return _call(body, name=name, grid=(NKV, t // TM),
                 in_specs=[blk, pl.BlockSpec((t, HD), lambda kv, i: (0, KCOL + kv)),
                           pl.BlockSpec((t, HD), lambda kv, i: (0, VCOL + kv)), gate, blk, blk, blk],
                 out_specs=[blk, gate, acc, acc],
                 out_shape=[jax.ShapeDtypeStruct((t, NQ * HD), F32), jax.ShapeDtypeStruct((t, (GCOL + NQ) * HD), BF),
                            jax.ShapeDtypeStruct((t, NKV * HD), F32), jax.ShapeDtypeStruct((t, NKV * HD), F32)])(
        qkv, qkv, qkv, p, dmix, o, lse)


def _row_tile(rows, row_bytes, cap=2 * 1024 * 1024):
    if rows * row_bytes <= cap or rows % 8:
        return rows
    tr = rows
    while tr * row_bytes > cap and tr % 16 == 0:
        tr //= 2
    return tr


def _adamw_update(w_ref, g_ref, m_ref, v_ref, d_ref, nm_ref, nv_ref):
    gv = g_ref[...]
    m2 = ADAM_B1 * m_ref[...] + (1.0 - ADAM_B1) * gv
    v2 = ADAM_B2 * v_ref[...] + (1.0 - ADAM_B2) * (gv * gv)
    mh = m2 / (1.0 - ADAM_B1 ** ADAM_STEP)
    vh = v2 / (1.0 - ADAM_B2 ** ADAM_STEP)
    d_ref[...] = -ADAM_LR * (mh / (jnp.sqrt(vh) + ADAM_EPS) + ADAM_WD * w_ref[...])
    nm_ref[...] = m2
    nv_ref[...] = v2


def adamw_many(ws, gs, ms, vs, name):
    n = len(ws)

    def body(*refs):
        for k in range(n):
            _adamw_update(*[refs[j * n + k] for j in range(7)])

    shapes = [jax.ShapeDtypeStruct(w.shape, F32) for w in ws]
    res = _call(body, name=name, out_shape=shapes * 3)(*ws, *gs, *ms, *vs)
    return res[:n], res[n:2 * n], res[2 * n:]


def adamw(w, g, m, v, name):
    r, cdim = w.shape
    tr = _row_tile(r, 4 * max(cdim, 128))

    def body(w_ref, g_ref, m_ref, v_ref, d_ref, nm_ref, nv_ref):
        _adamw_update(w_ref, g_ref, m_ref, v_ref, d_ref, nm_ref, nv_ref)

    tile = pl.BlockSpec((tr, cdim), lambda i: (i, 0))
    sh = jax.ShapeDtypeStruct((r, cdim), F32)
    return _call(body, name=name, grid=(r // tr,), in_specs=[tile] * 4, out_specs=[tile] * 3,
                 out_shape=[sh, sh, sh])(w, g, m, v)


def sum_lead(a, name, out_dtype=F32):
    n, r, cdim = a.shape
    tr = _row_tile(r, 4 * n * max(cdim, 128))

    def body(a_ref, o_ref):
        acc = a_ref[0].astype(F32)
        for k in range(1, n):
            acc = acc + a_ref[k].astype(F32)
        o_ref[...] = acc.astype(o_ref.dtype)

    return _call(body, name=name, grid=(r // tr,),
                 in_specs=[pl.BlockSpec((n, tr, cdim), lambda i: (0, i, 0))],
                 out_specs=pl.BlockSpec((tr, cdim), lambda i: (i, 0)),
                 out_shape=jax.ShapeDtypeStruct((r, cdim), out_dtype))(a)


_FLIPS = {"xy": [(1, 0, 0), (0, 1, 0), (1, 1, 0)], "c": [(0, 0, 1)],
          "all": [(0, 0, 1), (0, 1, 0), (0, 1, 1), (1, 0, 0), (1, 0, 1), (1, 1, 0), (1, 1, 1)]}
_GROUP_SIZE = {"xy": 4, "c": 2, "all": 8}


def _group_index(group, x, y, c):
    return {"xy": 2 * x + y, "c": c, "all": 4 * x + 2 * y + c}[group]


def exchange(items, name):
    plan = []
    for arr, group, kind in items:
        chunk = arr.shape if kind == "gather" else arr.shape[1:]
        plan.append((group, kind, chunk))
    ncopy = sum(len(_FLIPS[g]) for g, _, _ in plan)
    nitem = len(plan)

    def body(*refs):
        srcs, dsts = refs[:nitem], refs[nitem:2 * nitem]
        send_sems, recv_sems, local_sems = refs[2 * nitem:]
        x, y, c = lax.axis_index("x"), lax.axis_index("y"), lax.axis_index("c")
        sends, recvs, locals_ = [], [], []
        n = 0
        for k, (group, kind, _) in enumerate(plan):
            me = _group_index(group, x, y, c)
            own = srcs[k] if kind == "gather" else srcs[k].at[me]
            locals_.append(pltpu.make_async_copy(own, dsts[k].at[me], local_sems.at[k]))
            for fx, fy, fc in _FLIPS[group]:
                px, py, pc = (1 - x if fx else x), (1 - y if fy else y), (1 - c if fc else c)
                peer = _group_index(group, px, py, pc)
                src = srcs[k] if kind == "gather" else srcs[k].at[peer]
                sends.append(pltpu.make_async_remote_copy(
                    src_ref=src, dst_ref=dsts[k].at[me], send_sem=send_sems.at[n], recv_sem=recv_sems.at[n],
                    device_id=(px, py, pc), device_id_type=MESH))
                recvs.append(pltpu.make_async_remote_copy(
                    src_ref=src, dst_ref=dsts[k].at[peer], send_sem=send_sems.at[n], recv_sem=recv_sems.at[n],
                    device_id=(px, py, pc), device_id_type=MESH))
                n += 1
        for cp in locals_ + sends:
            cp.start()
        for cp in recvs:
            cp.wait_recv()
        for cp in sends:
            cp.wait_send()
        for cp in locals_:
            cp.wait()

    anyspec = pl.BlockSpec(memory_space=pl.ANY)
    outs = [jax.ShapeDtypeStruct((_GROUP_SIZE[g],) + tuple(chunk), arr.dtype)
            for (arr, _, _), (g, _, chunk) in zip(items, plan)]
    res = pl.pallas_call(
        body, name=name, out_shape=outs, in_specs=[anyspec] * nitem, out_specs=[anyspec] * nitem,
        scratch_shapes=[pltpu.SemaphoreType.DMA((ncopy,)), pltpu.SemaphoreType.DMA((ncopy,)),
                        pltpu.SemaphoreType.DMA((nitem,))],
        compiler_params=pltpu.CompilerParams(has_side_effects=True))(*[a for a, _, _ in items])
    return list(res)


D2D_PIECES = 4


def d2d(items, name):
    n = len(items)
    swaps = [k for k, (_, kind) in enumerate(items) if kind == "swap"]

    def pieces_of(rows):
        npc = D2D_PIECES if rows % (8 * D2D_PIECES) == 0 else 1
        return npc, rows // npc

    ncopy = sum(pieces_of(a.shape[0] if kind == "gather" else a.shape[1])[0] for a, kind in items)

    def body(*refs):
        srcs, outs = refs[:n], refs[n:2 * n]
        stages = dict(zip(swaps, refs[2 * n:2 * n + len(swaps)]))
        send_sems, recv_sems, local_sems = refs[2 * n + len(swaps):]
        x, y, c = lax.axis_index("x"), lax.axis_index("y"), lax.axis_index("c")
        sib = (x, y, 1 - c)

        def remote(src, dst, q):
            return pltpu.make_async_remote_copy(src_ref=src, dst_ref=dst, send_sem=send_sems.at[q],
                                                recv_sem=recv_sems.at[q], device_id=sib, device_id_type=MESH)

        copies = []
        q = 0
        for k, (arr, kind) in enumerate(items):
            npc, pr = pieces_of(arr.shape[0] if kind == "gather" else arr.shape[1])
            for pc in range(npc):
                rs = pl.ds(pc * pr, pr)
                if kind == "gather":
                    mine, theirs = outs[k].at[c, rs], outs[k].at[1 - c, rs]
                    copies.append((pltpu.make_async_copy(srcs[k].at[rs], mine, local_sems.at[q]),
                                   remote(mine, mine, q), remote(theirs, theirs, q)))
                else:
                    stage, land = stages[k].at[rs], outs[k].at[rs]
                    copies.append((pltpu.make_async_copy(srcs[k].at[1 - c, rs], stage, local_sems.at[q]),
                                   remote(stage, land, q), remote(stage, land, q)))
                q += 1
        for loc, _, _ in copies:
            loc.start()
        for loc, send, _ in copies:
            loc.wait()
            send.start()
        for _, _, recv in copies:
            recv.wait_recv()
        for _, send, _ in copies:
            send.wait_send()

    outs = [jax.ShapeDtypeStruct((2,) + a.shape if kind == "gather" else a.shape[1:], a.dtype) for a, kind in items]
    res = pl.pallas_call(
        body, name=name, out_shape=outs, in_specs=[pl.BlockSpec(memory_space=pl.ANY)] * n,
        out_specs=[pl.BlockSpec(memory_space=pltpu.VMEM)] * n,
        scratch_shapes=[pltpu.VMEM(items[k][0].shape[1:], items[k][0].dtype) for k in swaps]
        + [pltpu.SemaphoreType.DMA((ncopy,)), pltpu.SemaphoreType.DMA((ncopy,)), pltpu.SemaphoreType.DMA((ncopy,))],
        compiler_params=pltpu.CompilerParams(has_side_effects=True, vmem_limit_bytes=VMEM_LIMIT_BYTES))(
        *[a for a, _ in items])
    return list(res)


def sum_own(pair, got, name, out_dtype=F32):
    _, r, cdim = pair.shape
    tr = _row_tile(r, 4 * 2 * max(cdim, 128))

    def body(c_ref, p_ref, g_ref, o_ref):
        o_ref[...] = (p_ref[0] + g_ref[...]).astype(o_ref.dtype)

    me = lax.axis_index("c").astype(jnp.int32).reshape(1)
    return pl.pallas_call(
        body, name=name, out_shape=jax.ShapeDtypeStruct((r, cdim), out_dtype),
        grid_spec=pltpu.PrefetchScalarGridSpec(
            num_scalar_prefetch=1, grid=(r // tr,),
            in_specs=[pl.BlockSpec((1, tr, cdim), lambda i, c_ref: (c_ref[0], i, 0)),
                      pl.BlockSpec((tr, cdim), lambda i, c_ref: (i, 0))],
            out_specs=pl.BlockSpec((tr, cdim), lambda i, c_ref: (i, 0))),
        compiler_params=pltpu.CompilerParams(vmem_limit_bytes=VMEM_LIMIT_BYTES))(me, pair, got)


_SMALL = ["c_ctx", "norm_g", "b_mod", "gm_v_g", "gm_w_s", "gm_b_s", "s5_lam_re", "s5_lam_im", "s5_log_dt",
          "s5_b_re", "s5_b_im", "s5_c_re", "s5_c_im", "s5_d", "s5_b_glu", "q_norm_g", "k_norm_g", "final_g"]
_BIG = ["we_in", "we_out", "s5_w_glu", "wo_in", "wo_out"]
_WEIGHTS = ["c_ctx", "norm_g", "w_mod", "b_mod", "we_in", "we_out", "gm_v_g", "gm_w_s", "gm_b_s", "s5_lam_re",
            "s5_lam_im", "s5_log_dt", "s5_b_re", "s5_b_im", "s5_c_re", "s5_c_im", "s5_d", "s5_w_glu", "s5_b_glu",
            "wo_in", "wo_out", "q_norm_g", "k_norm_g", "final_g"]
_SMALL_ALIGN = 8 * 8 * 128


def _rope_tables(n_lat):
    rows = n_lat // GRID_W
    row = jnp.repeat(jnp.arange(rows), GRID_W)
    col = jnp.tile(jnp.arange(GRID_W), rows)
    freqs = ROPE_THETA ** (-jnp.arange(HD // 4, dtype=F32) / (HD // 4))
    ar, ac = row[:, None] * freqs, col[:, None] * freqs
    cos = jnp.concatenate([jnp.cos(ar), jnp.cos(ar), jnp.cos(ac), jnp.cos(ac)], axis=1)
    sins = jnp.concatenate([-jnp.sin(ar), jnp.sin(ar), -jnp.sin(ac), jnp.sin(ac)], axis=1)
    cos = jnp.concatenate([jnp.ones((NC, HD), F32), cos], axis=0)
    sins = jnp.concatenate([jnp.zeros((NC, HD), F32), sins], axis=0)
    return cos, sins


def _block_diag(v, transpose):
    gpb = SG // NBLK
    v = v.reshape(2, NBLK, gpb, SH, SP)
    eye = jnp.eye(gpb, dtype=v.dtype)
    if transpose:
        return jnp.einsum("djahp,ab->djapbh", v, eye).reshape(2, NBLK, LN, UB)
    return jnp.einsum("djahp,ab->djahbp", v, eye).reshape(2, NBLK, UB, LN)


def _diag_blocks(m):
    gpb = SG // NBLK
    return jnp.einsum("jahap->jahp", m.reshape(NBLK, gpb, SH, gpb, SP)).reshape(SG, SH, SP)


def _view2d(a):
    if a.ndim == 1:
        return a.reshape(1, -1)
    if a.shape[-1] < 64 and a.size % 1024 == 0:
        return a.reshape(-1, 1024)
    return a.reshape(-1, a.shape[-1])


def kernel(x, c, ctx, c_ctx, norm_g, w_mod, b_mod, we_in, we_out, gm_v_g, gm_w_s, gm_b_s, s5_lam_re, s5_lam_im, s5_log_dt, s5_b_re, s5_b_im, s5_c_re, s5_c_im, s5_d, s5_w_glu, s5_b_glu, wo_in, wo_out, q_norm_g, k_norm_g, final_g, loss_target, m_c_ctx, m_norm_g, m_w_mod, m_b_mod, m_we_in, m_we_out, m_gm_v_g, m_gm_w_s, m_gm_b_s, m_s5_lam_re, m_s5_lam_im, m_s5_log_dt, m_s5_b_re, m_s5_b_im, m_s5_c_re, m_s5_c_im, m_s5_d, m_s5_w_glu, m_s5_b_glu, m_wo_in, m_wo_out, m_q_norm_g, m_k_norm_g, m_final_g, v_c_ctx, v_norm_g, v_w_mod, v_b_mod, v_we_in, v_we_out, v_gm_v_g, v_gm_w_s, v_gm_b_s, v_s5_lam_re, v_s5_lam_im, v_s5_log_dt, v_s5_b_re, v_s5_b_im, v_s5_c_re, v_s5_c_im, v_s5_d, v_s5_w_glu, v_s5_b_glu, v_wo_in, v_wo_out, v_q_norm_g, v_k_norm_g, v_final_g):
    weights = dict(c_ctx=c_ctx, norm_g=norm_g, w_mod=w_mod, b_mod=b_mod, we_in=we_in, we_out=we_out, gm_v_g=gm_v_g,
                   gm_w_s=gm_w_s, gm_b_s=gm_b_s, s5_lam_re=s5_lam_re, s5_lam_im=s5_lam_im, s5_log_dt=s5_log_dt,
                   s5_b_re=s5_b_re, s5_b_im=s5_b_im, s5_c_re=s5_c_re, s5_c_im=s5_c_im, s5_d=s5_d, s5_w_glu=s5_w_glu,
                   s5_b_glu=s5_b_glu, wo_in=wo_in, wo_out=wo_out, q_norm_g=q_norm_g, k_norm_g=k_norm_g,
                   final_g=final_g)
    mom_m = dict(c_ctx=m_c_ctx, norm_g=m_norm_g, w_mod=m_w_mod, b_mod=m_b_mod, we_in=m_we_in, we_out=m_we_out,
                 gm_v_g=m_gm_v_g, gm_w_s=m_gm_w_s, gm_b_s=m_gm_b_s, s5_lam_re=m_s5_lam_re, s5_lam_im=m_s5_lam_im,
                 s5_log_dt=m_s5_log_dt, s5_b_re=m_s5_b_re, s5_b_im=m_s5_b_im, s5_c_re=m_s5_c_re, s5_c_im=m_s5_c_im,
                 s5_d=m_s5_d, s5_w_glu=m_s5_w_glu, s5_b_glu=m_s5_b_glu, wo_in=m_wo_in, wo_out=m_wo_out,
                 q_norm_g=m_q_norm_g, k_norm_g=m_k_norm_g, final_g=m_final_g)
    mom_v = dict(c_ctx=v_c_ctx, norm_g=v_norm_g, w_mod=v_w_mod, b_mod=v_b_mod, we_in=v_we_in, we_out=v_we_out,
                 gm_v_g=v_gm_v_g, gm_w_s=v_gm_w_s, gm_b_s=v_gm_b_s, s5_lam_re=v_s5_lam_re, s5_lam_im=v_s5_lam_im,
                 s5_log_dt=v_s5_log_dt, s5_b_re=v_s5_b_re, s5_b_im=v_s5_b_im, s5_c_re=v_s5_c_re, s5_c_im=v_s5_c_im,
                 s5_d=v_s5_d, s5_w_glu=v_s5_w_glu, s5_b_glu=v_s5_b_glu, wo_in=v_wo_in, wo_out=v_wo_out,
                 q_norm_g=v_q_norm_g, k_norm_g=v_k_norm_g, final_g=v_final_g)

    ixy = 2 * lax.axis_index("x") + lax.axis_index("y")
    n_lat = x.shape[1]
    nl = norm_g.shape[0]
    nmod = w_mod.shape[2]
    xin = (ctx[0], x[0])

    ic = lax.axis_index("c")
    mine = [lax.dynamic_index_in_dim(weights[n], ic, 0, keepdims=False).astype(BF) for n in _BIG]
    got = exchange([(m_, "xy", "gather") for m_ in mine] + [(c, "xy", "gather")], "gather_weights")
    both = d2d([(g_.reshape(-1, g_.shape[-1]), "gather") for g_ in got[:len(_BIG)]], "swap_weights")
    both = [b_.reshape((2,) + g_.shape) for b_, g_ in zip(both, got)]
    wein = [(both[0], l) for l in range(2)]
    weout = [(both[1].reshape(2, 1, D, D), l) for l in range(2)]
    wglu = [(both[2].reshape(2, AW, AW), l) for l in range(2)]
    woin = [(both[3], l) for l in range(2)]
    woout = [(both[4].reshape(2, 1, D, D), l) for l in range(2)]
    c_group = got[len(_BIG)].reshape(4, D)

    cond = jnp.concatenate([c_group, jnp.broadcast_to(c_ctx.reshape(1, D), (4, D))], axis=0)
    b_shard = lax.dynamic_slice(b_mod, (0, ixy * nmod), (nl, nmod)).reshape(nl, 1, nmod)
    mpart = ada_fwd(cond, w_mod, b_shard)
    m_lat, m_ctx = exchange([(jnp.transpose(mpart[:, 0:4], (1, 0, 2)), "xy", "scatter"),
                             (mpart[:, 4], "xy", "gather")], "exchange_mod")
    m_lat = jnp.transpose(m_lat, (1, 0, 2)).reshape(nl, 3, D)
    m_ctx = jnp.transpose(m_ctx, (1, 0, 2)).reshape(nl, 3, D)
    mods = [jnp.stack([m_ctx[l], m_lat[l]], axis=0) for l in range(nl)]

    loss_part, dx, g, d_norm_g, d_mod_lat, d_mod_ctx, d_final_g = _local_step(
        xin, loss_target[0], mods, wein, weout, wglu, woin, woout, weights)
    grad_x = dx.reshape(1, n_lat, D)

    d_mod_lat, d_mod_ctx = jnp.stack(d_mod_lat), jnp.stack(d_mod_ctx)
    dm_send = jnp.stack([d_mod_lat.reshape(nl, 4, nmod), d_mod_ctx.reshape(nl, 4, nmod)])
    (dm_got,) = exchange([(jnp.transpose(dm_send, (2, 0, 1, 3)), "xy", "scatter")], "exchange_dmod")
    dm_rows = jnp.concatenate([dm_got[:, 0], dm_got[:, 1]], axis=0)
    gw_mod, d_cctx = ada_bwd(cond, jnp.transpose(dm_rows, (1, 0, 2)), w_mod)
    g_small = dict(c_ctx=d_cctx.reshape(D), norm_g=jnp.stack(d_norm_g), b_mod=add2(d_mod_lat, d_mod_ctx, "add_dbmod"),
                   final_g=d_final_g.reshape(D))
    for name in _SMALL:
        if name not in g_small:
            g_small[name] = jnp.stack(g[name])

    flat = jnp.concatenate([g_small[n].reshape(-1) for n in _SMALL])
    nflat = flat.shape[0]
    npad = -(-nflat // _SMALL_ALIGN) * _SMALL_ALIGN
    flat = jnp.concatenate([flat, jnp.zeros((npad - nflat,), F32)]).reshape(8, npad // (8 * 128), 128)
    pairs = [gw_mod.reshape(2, nl // 2 * D, nmod)]
    for name in _BIG:
        st = jnp.stack(g[name]) if isinstance(g[name], list) else g[name]
        pairs.append(st.reshape(2, -1, st.shape[-1]))
    got_a = d2d([(pairs[k], "swap") for k in (1, 2, 3)], "reduce_chip_a")
    got_b = d2d([(pairs[k], "swap") for k in (0, 4, 5)], "reduce_chip_b")
    theirs = [got_b[0]] + got_a + got_b[1:]
    chip = [sum_own(pairs[k], theirs[k], f"sum_chip{k}", F32 if k == 0 else BF) for k in range(len(pairs))]
    parts = exchange([(flat, "all", "scatter")]
                     + [(s_.reshape(4, s_.shape[0] // 4, s_.shape[1]), "xy", "scatter") for s_ in chip[1:]],
                     "reduce_scatter")
    sums = [sum_lead(pt, f"sum_shard{k}") for k, pt in enumerate(parts)]
    full = d2d([(sums[0], "gather"), (chip[0], "gather")] + [(s_, "gather") for s_ in sums[1:]], "all_gather")
    (flat_full,) = exchange([(full[0], "xy", "gather")], "gather_small")
    full = [flat_full] + full[1:]
    flat = full[0].reshape(-1)
    grads = {}
    off = 0
    for name in _SMALL:
        sz = weights[name].size
        grads[name] = flat[off:off + sz].reshape(weights[name].shape)
        off += sz
    grads["w_mod"] = full[1].reshape(w_mod.shape)
    for k, name in enumerate(_BIG):
        grads[name] = full[2 + k].reshape(weights[name].shape)

    delta, new_m, new_v = {}, {}, {}
    views = [_view2d(weights[n]) for n in _SMALL]
    ds, nms, nvs = adamw_many(views, [grads[n].reshape(w2.shape) for n, w2 in zip(_SMALL, views)],
                              [mom_m[n].reshape(w2.shape) for n, w2 in zip(_SMALL, views)],
                              [mom_v[n].reshape(w2.shape) for n, w2 in zip(_SMALL, views)], "adamw_small")
    for n, d2, m2, v2 in zip(_SMALL, ds, nms, nvs):
        shp = weights[n].shape
        delta[n], new_m[n], new_v[n] = d2.reshape(shp), m2.reshape(shp), v2.reshape(shp)
    for name in ["w_mod"] + _BIG:
        w2 = _view2d(weights[name])
        d2, m2, v2 = adamw(w2, grads[name].reshape(w2.shape), mom_m[name].reshape(w2.shape),
                           mom_v[name].reshape(w2.shape), f"adamw_{name}")
        shp = weights[name].shape
        delta[name], new_m[name], new_v[name] = d2.reshape(shp), m2.reshape(shp), v2.reshape(shp)

    loss = lax.psum(loss_part[0, 0], ("x", "y", "c"))
    return (loss, grad_x, *[grads[n] for n in _WEIGHTS], *[delta[n] for n in _WEIGHTS],
            *[new_m[n] for n in _WEIGHTS], *[new_v[n] for n in _WEIGHTS])


def _local_step(xin, target, mods, wein, weout, wglu, woin, woout, w):
    norm_g, gm_v_g, gm_w_s, gm_b_s = w["norm_g"], w["gm_v_g"], w["gm_w_s"], w["gm_b_s"]
    s5_lam_re, s5_lam_im, s5_log_dt = w["s5_lam_re"], w["s5_lam_im"], w["s5_log_dt"]
    s5_b_re, s5_b_im, s5_c_re, s5_c_im = w["s5_b_re"], w["s5_b_im"], w["s5_c_re"], w["s5_c_im"]
    s5_d, s5_b_glu, q_norm_g, k_norm_g, final_g = w["s5_d"], w["s5_b_glu"], w["q_norm_g"], w["k_norm_g"], w["final_g"]
    nl = norm_g.shape[0]
    n_lat = xin[1].shape[0]

    cos, sins = _rope_tables(n_lat)

    s5p = []
    for i in range(2):
        lam_l = (s5_lam_re[i].reshape(2, SW), s5_lam_im[i].reshape(2, SW),
                 jnp.repeat(s5_log_dt[i], SP, axis=1))
        lam_r = (jnp.repeat(s5_lam_re[i].reshape(2 * SG, SP), SH, axis=0),
                 jnp.repeat(s5_lam_im[i].reshape(2 * SG, SP), SH, axis=0),
                 jnp.repeat(s5_log_dt[i].reshape(2 * SG, 1), SH, axis=0))
        b_r = (jnp.transpose(s5_b_re[i], (0, 1, 3, 2)).reshape(2 * SG * SH, SP),
               jnp.transpose(s5_b_im[i], (0, 1, 3, 2)).reshape(2 * SG * SH, SP))
        pw_re, pw_im, bbr, bbi = s5_disc(*lam_l, *lam_r, *b_r)
        s5p.append(dict(
            lam_r=lam_r, b_r=b_r, pw_re=pw_re, pw_im=pw_im, pw_im_conj=-pw_im,
            bb_re=_block_diag(bbr.reshape(2, SG, SH, SP), False).astype(BF),
            bb_im=_block_diag(bbi.reshape(2, SG, SH, SP), False).astype(BF),
            ct_re=_block_diag(s5_c_re[i], True).astype(BF), ct_im=_block_diag(s5_c_im[i], True).astype(BF)))

    saved = []
    xcur = xin
    h = pro_fwd(xin[0], xin[1], norm_g[0].reshape(1, D), mods[0], "pro_fwd0")
    for l in range(nl):
        i = l // 2
        sv = dict(x=xcur, h=h)
        if l % 2 == 0:
            p = mm_nn(h, wein[i], f"in_proj{l}")
            sp = s5p[i]
            for dr, rev in ((0, False), (1, True)):
                sv[f"y{dr}"], sv[f"hpr{dr}"], sv[f"hpi{dr}"] = s5q_fwd(
                    p, sp["bb_re"], sp["bb_im"], sp["ct_re"], sp["ct_im"], sp["pw_re"], sp["pw_im"], dr, rev,
                    f"s5_fwd{l}_{dr}")
            mix = mix_fwd(p, sv["y0"], sv["y1"], gm_v_g[i].reshape(1, AW), gm_w_s[i].astype(BF),
                          gm_b_s[i].reshape(NGRP, CHUNK, 1), s5_d[i].reshape(1, AW), wglu[i],
                          s5_b_glu[i].reshape(1, AW), f"mix_fwd{l}")
            o = mm_nn(mix, weout[i], f"out_proj{l}")
        else:
            p = mm_nn(h, woin[i], f"in_proj{l}")
            sv["qkv"] = attn_prep(p, q_norm_g[i].reshape(1, HD), k_norm_g[i].reshape(1, HD), cos, sins, f"attn_prep{l}")
            sv["o_att"], mix, sv["lse"] = attn_fwd(sv["qkv"], p, f"attn_fwd{l}")
            o = mm_nn(mix, woout[i], f"out_proj{l}")
        sv.update(p=p, mix=mix, o=o)
        saved.append(sv)
        if l < nl - 1:
            xcur, h = res_pro_fwd(xcur, o, mods[l], True, norm_g[l + 1].reshape(1, D), mods[l + 1], f"res_pro_fwd{l}")
        else:
            xcur = res_fwd(xcur, o, mods[l], False, f"res_fwd{l}")

    loss_part, dx, d_final_g, do, dgt = final_loss(xcur, target, final_g.reshape(1, D), saved[-1]["o"], mods[-1])

    g = {}
    gbuf = {}
    d_norm_g, d_mod_lat, d_mod_ctx = [None] * nl, [None] * nl, [None] * nl
    for name in ("s5_w_glu", "gm_v_g", "gm_w_s", "gm_b_s", "s5_lam_re", "s5_lam_im",
                 "s5_log_dt", "s5_b_re", "s5_b_im", "s5_c_re", "s5_c_im", "s5_d", "s5_b_glu", "q_norm_g", "k_norm_g"):
        g[name] = [None, None]
    for l in reversed(range(nl)):
        i = l // 2
        sv = saved[l]
        w_out = weout[i] if l % 2 == 0 else woout[i]
        dmix = mm_nt(do, w_out, f"out_dgrad{l}")
        out_name, in_name = ("we_out", "we_in") if l % 2 == 0 else ("wo_out", "wo_in")
        gbuf[out_name] = mm_tn(sv["mix"], do, 1, f"out_wgrad{l}", slot=i, into=gbuf.get(out_name))
        if l % 2 == 0:
            sp = s5p[i]
            (dp, dy, g["gm_w_s"][i], dbs, dvg, dd, g["s5_w_glu"][i], dbg) = mix_bwd(
                sv["p"], sv["y0"], sv["y1"], dmix, gm_v_g[i].reshape(1, AW), gm_w_s[i].astype(BF),
                gm_b_s[i].reshape(NGRP, CHUNK, 1), s5_d[i].reshape(1, AW), wglu[i], s5_b_glu[i].reshape(1, AW),
                f"mix_bwd{l}")
            g["gm_b_s"][i], g["gm_v_g"][i] = dbs.reshape(NGRP, CHUNK), dvg.reshape(AW)
            g["s5_d"][i], g["s5_b_glu"][i] = dd.reshape(AW), dbg.reshape(AW)
            g["s5_w_glu"][i] = g["s5_w_glu"][i].reshape(4, AW // 4, AW)
            dxd, das_r, das_i, dbbs_r, dbbs_i, dcs_r, dcs_i = [], [], [], [], [], [], []
            for dr, rev in ((0, False), (1, True)):
                dxs_d, da_r, da_i, dbb_r, dbb_i, dc_r, dc_i = s5q_bwd(
                    sv["p"], sv[f"hpr{dr}"], sv[f"hpi{dr}"], dy, sp["bb_re"], sp["bb_im"], sp["ct_re"], sp["ct_im"],
                    sp["pw_re"], sp["pw_im_conj"], dr, rev, f"s5_bwd{l}_{dr}")
                dxd.append(dxs_d)
                das_r.append(jnp.repeat(da_r.reshape(SG, SP), SH, axis=0))
                das_i.append(jnp.repeat(da_i.reshape(SG, SP), SH, axis=0))
                dbbs_r.append(_diag_blocks(dbb_r).reshape(SG * SH, SP))
                dbbs_i.append(_diag_blocks(dbb_i).reshape(SG * SH, SP))
                dcs_r.append(_diag_blocks(dc_r))
                dcs_i.append(_diag_blocks(dc_i))
            cat = lambda parts: jnp.concatenate(parts, axis=0)
            dlr, dli, dldt, dbr, dbi = s5_param_bwd(*sp["lam_r"], *sp["b_r"], cat(das_r), cat(das_i),
                                                    cat(dbbs_r), cat(dbbs_i))
            g["s5_lam_re"][i], g["s5_lam_im"][i] = dlr.reshape(2, SG, SP), dli.reshape(2, SG, SP)
            g["s5_log_dt"][i] = dldt.reshape(2, SG)
            g["s5_b_re"][i] = jnp.transpose(dbr.reshape(2, SG, SH, SP), (0, 1, 3, 2))
            g["s5_b_im"][i] = jnp.transpose(dbi.reshape(2, SG, SH, SP), (0, 1, 3, 2))
            g["s5_c_re"][i], g["s5_c_im"][i] = jnp.stack(dcs_r), jnp.stack(dcs_i)
            dp = s5p_dx_sum(dy, s5_d[i].reshape(1, AW), dxd[0], dxd[1], dp, f"s5_dx_sum{l}")
            w_in = wein[i]
        else:
            dq, dp, dk, dv = attn_bwd(sv["qkv"], sv["p"], dmix, sv["o_att"], sv["lse"], f"attn_bwd{l}")
            dp, dqg, dkg = attn_prep_bwd(sv["p"], dq, dk, dv, q_norm_g[i].reshape(1, HD),
                                         k_norm_g[i].reshape(1, HD), cos, sins, dp, f"attn_prep_bwd{l}")
            g["q_norm_g"][i], g["k_norm_g"][i] = dqg.reshape(HD), dkg.reshape(HD)
            w_in = woin[i]
        dh = mm_nt(dp, w_in, f"in_dgrad{l}")
        gbuf[in_name] = mm_tn(sv["h"], dp, 4, f"in_wgrad{l}", slot=i, into=gbuf.get(in_name))
        dgt_l = dgt
        if l > 0:
            dx, dmod2, dng, do, dgt = pro_res_bwd(sv["x"], dh, dx, norm_g[l].reshape(1, D), mods[l],
                                                  saved[l - 1]["o"], mods[l - 1], f"pro_res_bwd{l}")
        else:
            dx, dmod2, dng = pro_bwd(xin[0], xin[1], dh, dx, norm_g[l].reshape(1, D), mods[l], f"pro_bwd{l}")
        d_norm_g[l] = dng.reshape(D)
        d_mod_ctx[l] = jnp.concatenate([dmod2[0, 0], dmod2[0, 1], dgt_l[0]])
        d_mod_lat[l] = jnp.concatenate([dmod2[1, 0], dmod2[1, 1], dgt_l[1]])
    g.update(gbuf)
    return loss_part, dx, g, d_norm_g, d_mod_lat, d_mod_ctx, d_final_g
```

```python
import math

import numpy as np
import jax
import jax.numpy as jnp
from jax import lax
from jax.experimental import pallas as pl
from jax.experimental.pallas import tpu as pltpu

F32 = jnp.float32
BF = jnp.bfloat16
MESH = pl.DeviceIdType.MESH

D = 1024
NC = 256
SEQ = 4096
GRID_W = 64
TM = 256
CHUNK = 128
EPS = 1e-6
HD = 128
NQ = 8
NKV = 2
ROPE_THETA = 10000.0
SG = 32
SP = 64
SH = 16
SW = SG * SP
GELU_K = math.sqrt(2.0 / math.pi)
GELU_C = 0.044715
VMEM_LIMIT_BYTES = 56 * 1024 * 1024

ADAM_LR = 0.001
ADAM_B1 = 0.9
ADAM_B2 = 0.999
ADAM_EPS = 1e-08
ADAM_WD = 0.01
ADAM_STEP = 10


def _call(body, *, name, out_shape, grid=None, in_specs=None, out_specs=None, scratch=()):
    kw = {}
    if grid is not None:
        kw["grid"] = grid
    if in_specs is not None:
        kw["in_specs"] = in_specs
    if out_specs is not None:
        kw["out_specs"] = out_specs
    return pl.pallas_call(
        body, name=name, out_shape=out_shape, scratch_shapes=list(scratch),
        compiler_params=pltpu.CompilerParams(vmem_limit_bytes=VMEM_LIMIT_BYTES), **kw)


def _dot(a, b, ca=1, cb=0):
    return lax.dot_general(a, b, (((ca,), (cb,)), ((), ())), preferred_element_type=F32)


def _sig(x):
    return 1.0 / (1.0 + jnp.exp(-x))


def _full(shape):
    n = len(shape)
    return pl.BlockSpec(shape, lambda *_: (0,) * n)


def _mm_rows(t, most=1088):
    for rows in (2176, 1088, 1024, 768, 544, 512, 256):
        if rows <= most and t % rows == 0:
            return rows
    raise ValueError(t)


def _layer_of(w):
    return w if isinstance(w, tuple) else (w[None], 0)


def mm_nn(a, w, name, out_dtype=F32):
    w4, layer = _layer_of(w)
    t, k = a.shape
    _, j, _, nb = w4.shape
    tr = _mm_rows(t, 2176)

    def body(a_ref, w_ref, o_ref):
        o_ref[...] = _dot(a_ref[...], w_ref[0, 0]).astype(o_ref.dtype)

    return _call(body, name=name, grid=(j, t // tr),
                 in_specs=[pl.BlockSpec((tr, k), lambda jj, i: (i, 0)),
                           pl.BlockSpec((1, 1, k, nb), lambda jj, i: (layer, jj, 0, 0))],
                 out_specs=pl.BlockSpec((tr, nb), lambda jj, i: (i, jj)),
                 out_shape=jax.ShapeDtypeStruct((t, j * nb), out_dtype))(a, w4)


def mm_nt(a, w, name, out_dtype=F32):
    w4, layer = _layer_of(w)
    t, _ = a.shape
    _, j, k, nb = w4.shape
    tr = _mm_rows(t)

    def body(a_ref, w_ref, o_ref):
        acc = _dot(a_ref[:, 0:nb], w_ref[0, 0], 1, 1)
        for jj in range(1, j):
            acc = acc + _dot(a_ref[:, jj * nb:(jj + 1) * nb], w_ref[0, jj], 1, 1)
        o_ref[...] = acc.astype(o_ref.dtype)

    return _call(body, name=name, grid=(t // tr,),
                 in_specs=[pl.BlockSpec((tr, j * nb), lambda i: (i, 0)),
                           pl.BlockSpec((1, j, k, nb), lambda i: (layer, 0, 0, 0))],
                 out_specs=pl.BlockSpec((tr, k), lambda i: (i, 0)),
                 out_shape=jax.ShapeDtypeStruct((t, k), out_dtype))(a, w4)


def mm_tn(a, b, j, name, slot=0, into=None):
    t, m = a.shape
    nb = b.shape[1] // j
    tr = _mm_rows(t, 2176)

    def body(a_ref, b_ref, *rest):
        o_ref = rest[-1]

        @pl.when(pl.program_id(1) == 0)
        def _():
            o_ref[...] = jnp.zeros_like(o_ref)
        o_ref[0, 0] += _dot(a_ref[...], b_ref[...], 0, 0)

    in_specs = [pl.BlockSpec((tr, m), lambda jj, i: (i, 0)), pl.BlockSpec((tr, nb), lambda jj, i: (i, jj))]
    args = [a, b]
    alias = {}
    if into is not None:
        in_specs.append(pl.BlockSpec(memory_space=pl.ANY))
        args.append(into)
        alias = {2: 0}
    return pl.pallas_call(
        body, name=name, grid=(j, t // tr), in_specs=in_specs,
        out_specs=pl.BlockSpec((1, 1, m, nb), lambda jj, i: (slot, jj, 0, 0)),
        out_shape=jax.ShapeDtypeStruct((2, j, m, nb), F32), input_output_aliases=alias,
        compiler_params=pltpu.CompilerParams(vmem_limit_bytes=VMEM_LIMIT_BYTES))(*args)


def _mod_rows(mod_ref, i):
    ctx = i == 0
    sh = jnp.where(ctx, mod_ref[0, 0:1, :], mod_ref[1, 0:1, :])
    sc = jnp.where(ctx, mod_ref[0, 1:2, :], mod_ref[1, 1:2, :])
    gt = jnp.where(ctx, mod_ref[0, 2:3, :], mod_ref[1, 2:3, :])
    return sh, sc, gt


def _split_specs():
    return [pl.BlockSpec((TM, D), lambda i: (0, 0)), pl.BlockSpec((TM, D), lambda i: (jnp.maximum(i - 1, 0), 0))]


def _split_tile(c_ref, l_ref, i):
    return jnp.where(i == 0, c_ref[...], l_ref[...])


def pro_fwd(ctx, lat, g, mod, name):
    t = ctx.shape[0] + lat.shape[0]

    def body(c_ref, l_ref, g_ref, mod_ref, h_ref):
        i = pl.program_id(0)
        sh, sc, _ = _mod_rows(mod_ref, i)
        xv = _split_tile(c_ref, l_ref, i)
        r = lax.rsqrt(jnp.mean(xv * xv, axis=-1, keepdims=True) + EPS)
        h_ref[...] = ((xv * r) * g_ref[...] * (1.0 + sc) + sh).astype(BF)

    return _call(body, name=name, grid=(t // TM,),
                 in_specs=_split_specs() + [_full((1, D)), _full((2, 3, D))],
                 out_specs=pl.BlockSpec((TM, D), lambda i: (i, 0)),
                 out_shape=jax.ShapeDtypeStruct((t, D), BF))(ctx, lat, g, mod)


def pro_bwd(ctx, lat, dh, dxn, g, mod, name):
    t = ctx.shape[0] + lat.shape[0]

    def body(c_ref, l_ref, dh_ref, dxn_ref, g_ref, mod_ref, dx_ref, dmod_ref, dg_ref):
        i = pl.program_id(0)

        @pl.when(i == 0)
        def _():
            dmod_ref[...] = jnp.zeros_like(dmod_ref)
            dg_ref[...] = jnp.zeros_like(dg_ref)

        _, sc, _ = _mod_rows(mod_ref, i)
        xv = _split_tile(c_ref, l_ref, i)
        gv = g_ref[...]
        r = lax.rsqrt(jnp.mean(xv * xv, axis=-1, keepdims=True) + EPS)
        xn = xv * r
        dh_v = dh_ref[...]
        e = dh_v * (1.0 + sc)
        dsh = jnp.sum(dh_v, axis=0, keepdims=True)
        dsc = jnp.sum(dh_v * xn * gv, axis=0, keepdims=True)
        dg_ref[...] += jnp.sum(e * xn, axis=0, keepdims=True)
        dxh = e * gv

        @pl.when(i == 0)
        def _():
            dmod_ref[0, 0:1, :] += dsh
            dmod_ref[0, 1:2, :] += dsc

        @pl.when(i > 0)
        def _():
            dx_ref[...] = dxn_ref[...] + r * (dxh - xn * jnp.mean(dxh * xn, axis=-1, keepdims=True))
            dmod_ref[1, 0:1, :] += dsh
            dmod_ref[1, 1:2, :] += dsc

    tile = pl.BlockSpec((TM, D), lambda i: (i, 0))
    return _call(body, name=name, grid=(t // TM,),
                 in_specs=_split_specs() + [tile, tile, _full((1, D)), _full((2, 3, D))],
                 out_specs=[_split_specs()[1], _full((2, 2, D)), _full((1, D))],
                 out_shape=[jax.ShapeDtypeStruct(lat.shape, F32), jax.ShapeDtypeStruct((2, 2, D), F32),
                            jax.ShapeDtypeStruct((1, D), F32)])(ctx, lat, dh, dxn, g, mod)


def _mod_rows_at(mod_ref, i, tr):
    isctx = i * tr + lax.broadcasted_iota(jnp.int32, (tr, 1), 0) < NC
    pick = lambda k: jnp.where(isctx, mod_ref[0, k:k + 1, :], mod_ref[1, k:k + 1, :])
    return pick(0), pick(1), pick(2), isctx


def res_fwd(x, o, mod, update_ctx, name):
    t = x.shape[0]
    tr = _mm_rows(t)

    def body(x_ref, o_ref, mod_ref, y_ref):
        _, _, gt, isctx = _mod_rows_at(mod_ref, pl.program_id(0), tr)
        upd = x_ref[...] + gt * o_ref[...]
        y_ref[...] = upd if update_ctx else jnp.where(isctx, x_ref[...], upd)

    tile = pl.BlockSpec((tr, D), lambda i: (i, 0))
    return _call(body, name=name, grid=(t // tr,), in_specs=[tile, tile, _full((2, 3, D))],
                 out_specs=tile, out_shape=jax.ShapeDtypeStruct((t, D), F32))(x, o, mod)


def res_pro_fwd(x, o, mod, update_ctx, g_next, mod_next, name):
    split = isinstance(x, tuple)
    xs = list(x) if split else [x]
    t = o.shape[0]
    tr = TM if split else _mm_rows(t)

    def body(*refs):
        x_refs = refs[:len(xs)]
        o_ref, mod_ref, g_ref, modn_ref, y_ref, h_ref = refs[len(xs):]
        i = pl.program_id(0)
        _, _, gt, isctx = _mod_rows_at(mod_ref, i, tr)
        xv = _split_tile(x_refs[0], x_refs[1], i) if split else x_refs[0][...]
        xn = xv + gt * o_ref[...]
        if not update_ctx:
            xn = jnp.where(isctx, xv, xn)
        y_ref[...] = xn
        sh, sc, _, _ = _mod_rows_at(modn_ref, i, tr)
        r = lax.rsqrt(jnp.mean(xn * xn, axis=-1, keepdims=True) + EPS)
        h_ref[...] = ((xn * r) * g_ref[...] * (1.0 + sc) + sh).astype(BF)

    tile = pl.BlockSpec((tr, D), lambda i: (i, 0))
    return _call(body, name=name, grid=(t // tr,),
                 in_specs=(_split_specs() if split else [tile]) + [tile, _full((2, 3, D)), _full((1, D)), _full((2, 3, D))],
                 out_specs=[tile, tile],
                 out_shape=[jax.ShapeDtypeStruct((t, D), F32), jax.ShapeDtypeStruct((t, D), BF)])(
        *xs, o, mod, g_next, mod_next)


def pro_res_bwd(x, dh, dxn, g, mod, o_prev, mod_prev, name):
    t = x.shape[0]
    tr = _mm_rows(t, 544)

    def body(x_ref, dh_ref, dxn_ref, g_ref, mod_ref, o_ref, modp_ref, dx_ref, dmod_ref, dg_ref, do_ref, dgt_ref):
        i = pl.program_id(0)

        @pl.when(i == 0)
        def _():
            dmod_ref[...] = jnp.zeros_like(dmod_ref)
            dg_ref[...] = jnp.zeros_like(dg_ref)
            dgt_ref[...] = jnp.zeros_like(dgt_ref)

        _, sc, _, isctx = _mod_rows_at(mod_ref, i, tr)
        xv = x_ref[...]
        gv = g_ref[...]
        r = lax.rsqrt(jnp.mean(xv * xv, axis=-1, keepdims=True) + EPS)
        xn = xv * r
        dh_v = dh_ref[...]
        e = dh_v * (1.0 + sc)
        dsc_rows = dh_v * xn * gv
        dg_ref[...] += jnp.sum(e * xn, axis=0, keepdims=True)
        dxh = e * gv
        dx = dxn_ref[...] + r * (dxh - xn * jnp.mean(dxh * xn, axis=-1, keepdims=True))
        dx_ref[...] = dx
        _, _, gtp, _ = _mod_rows_at(modp_ref, i, tr)
        do_ref[...] = (gtp * dx).astype(BF)
        dgt_rows = dx * o_ref[...]
        dmod_ref[1, 0:1, :] += jnp.sum(dh_v, axis=0, keepdims=True)
        dmod_ref[1, 1:2, :] += jnp.sum(dsc_rows, axis=0, keepdims=True)
        dgt_ref[1:2, :] += jnp.sum(dgt_rows, axis=0, keepdims=True)

        @pl.when(i * tr < NC)
        def _():
            for ref, rows in ((dmod_ref.at[:, 0], dh_v), (dmod_ref.at[:, 1], dsc_rows), (dgt_ref, dgt_rows)):
                part = jnp.sum(jnp.where(isctx, rows, 0.0), axis=0, keepdims=True)
                ref[0:1, :] += part
                ref[1:2, :] += -part

    tile = pl.BlockSpec((tr, D), lambda i: (i, 0))
    return _call(body, name=name, grid=(t // tr,),
                 in_specs=[tile, tile, tile, _full((1, D)), _full((2, 3, D)), tile, _full((2, 3, D))],
                 out_specs=[tile, _full((2, 2, D)), _full((1, D)), tile, _full((2, D))],
                 out_shape=[jax.ShapeDtypeStruct((t, D), F32), jax.ShapeDtypeStruct((2, 2, D), F32),
                            jax.ShapeDtypeStruct((1, D), F32), jax.ShapeDtypeStruct((t, D), BF),
                            jax.ShapeDtypeStruct((2, D), F32)])(x, dh, dxn, g, mod, o_prev, mod_prev)


def final_loss(x, target, g, o_last, mod_last):
    t = x.shape[0]

    def body(x_ref, t_ref, g_ref, o_ref, modp_ref, loss_ref, dx_ref, dg_ref, do_ref, dgt_ref):
        i = pl.program_id(0)

        @pl.when(i == 0)
        def _():
            loss_ref[...] = jnp.zeros_like(loss_ref)
            dg_ref[...] = jnp.zeros_like(dg_ref)
            dx_ref[...] = jnp.zeros_like(dx_ref)
            do_ref[...] = jnp.zeros_like(do_ref)
            dgt_ref[...] = jnp.zeros_like(dgt_ref)

        @pl.when(i > 0)
        def _():
            xv = x_ref[...]
            gv = g_ref[...]
            r = lax.rsqrt(jnp.mean(xv * xv, axis=-1, keepdims=True) + EPS)
            xn = xv * r
            err = xn * gv - t_ref[...]
            loss_ref[...] += (0.5 / D) * jnp.sum(jnp.sum(err * err, axis=1, keepdims=True), axis=0, keepdims=True)
            dy = err * (1.0 / D)
            dg_ref[...] += jnp.sum(dy * xn, axis=0, keepdims=True)
            dxh = dy * gv
            dx = r * (dxh - xn * jnp.mean(dxh * xn, axis=-1, keepdims=True))
            dx_ref[...] = dx
            do_ref[...] = (modp_ref[1, 2:3, :] * dx).astype(BF)
            dgt_ref[1:2, :] += jnp.sum(dx * o_ref[...], axis=0, keepdims=True)

    tile = pl.BlockSpec((TM, D), lambda i: (i, 0))
    return _call(body, name="final_loss", grid=(t // TM,),
                 in_specs=[tile, pl.BlockSpec((TM, D), lambda i: (jnp.maximum(i - 1, 0), 0)), _full((1, D)), tile,
                           _full((2, 3, D))],
                 out_specs=[_full((1, 1)), tile, _full((1, D)), tile, _full((2, D))],
                 out_shape=[jax.ShapeDtypeStruct((1, 1), F32), jax.ShapeDtypeStruct((t, D), F32),
                            jax.ShapeDtypeStruct((1, D), F32), jax.ShapeDtypeStruct((t, D), BF),
                            jax.ShapeDtypeStruct((2, D), F32)])(x, target, g, o_last, mod_last)


def ada_fwd(cond, w_mod, b_mod):
    nl, _, nw = w_mod.shape

    def body(c_ref, w_ref, b_ref, o_ref):
        cv = c_ref[...]
        s = (cv * _sig(cv)).astype(BF)
        o_ref[0] = _dot(s, w_ref[0].astype(BF)) + b_ref[0]

    return _call(body, name="ada_fwd", grid=(nl,),
                 in_specs=[_full((8, D)), pl.BlockSpec((1, D, nw), lambda l: (l, 0, 0)),
                           pl.BlockSpec((1, 1, nw), lambda l: (l, 0, 0))],
                 out_specs=pl.BlockSpec((1, 8, nw), lambda l: (l, 0, 0)),
                 out_shape=jax.ShapeDtypeStruct((nl, 8, nw), F32))(cond, w_mod, b_mod)


def ada_bwd(cond, dm, w_mod):
    nl, _, nw = w_mod.shape

    def body(c_ref, dm_ref, w_ref, gw_ref, dcc_ref, dc_ref):
        l = pl.program_id(0)

        @pl.when(l == 0)
        def _():
            dc_ref[...] = jnp.zeros_like(dc_ref)

        cv = c_ref[...]
        sg = _sig(cv)
        s = (cv * sg).astype(BF)
        dmv = dm_ref[0].astype(BF)
        gw_ref[0] = _dot(s, dmv, 0, 0)
        dc_ref[...] += _dot(dmv, w_ref[0].astype(BF), 1, 1)

        @pl.when(l == nl - 1)
        def _():
            dcond = dc_ref[...] * (sg * (1.0 + cv * (1.0 - sg)))
            dcc_ref[...] = jnp.sum(dcond[4:8], axis=0, keepdims=True)

    return _call(body, name="ada_bwd", grid=(nl,),
                 in_specs=[_full((8, D)), pl.BlockSpec((1, 8, nw), lambda l: (l, 0, 0)),
                           pl.BlockSpec((1, D, nw), lambda l: (l, 0, 0))],
                 out_specs=[pl.BlockSpec((1, D, nw), lambda l: (l, 0, 0)), _full((1, D))],
                 out_shape=[jax.ShapeDtypeStruct((nl, D, nw), F32), jax.ShapeDtypeStruct((1, D), F32)],
                 scratch=[pltpu.VMEM((8, D), F32)])(cond, dm, w_mod)


def add2(a, b, name):
    def body(a_ref, b_ref, o_ref):
        o_ref[...] = a_ref[...] + b_ref[...]

    return _call(body, name=name, out_shape=jax.ShapeDtypeStruct(a.shape, a.dtype))(a, b)


AW = 512
NGRP = 4


def Y4_SPEC():
    return pl.BlockSpec((AW // 128, TM, 128), lambda i: (0, i, 0))


def _cat_lanes(ref):
    return jnp.concatenate([ref[q] for q in range(ref.shape[0])], axis=1)


def _gelu(y):
    t = jnp.tanh(GELU_K * (y + GELU_C * y * y * y))
    return 0.5 * y * (1.0 + t), t


def _layer_norm_stats(v):
    mu = jnp.mean(v, axis=-1, keepdims=True)
    vc = v - mu
    rstd = lax.rsqrt(jnp.mean(vc * vc, axis=-1, keepdims=True) + EPS)
    return vc * rstd, rstd


def _spatial_mix(vn_ref, ws_ref, bs_ref, mixed_ref):
    for ch in range(TM // CHUNK):
        rows = slice(ch * CHUNK, (ch + 1) * CHUNK)
        for g in range(NGRP):
            cols = slice(g * CHUNK, (g + 1) * CHUNK)
            mixed_ref[rows, cols] = _dot(ws_ref[g], vn_ref[rows, cols]) + bs_ref[g]


def mix_fwd(p, yf, yb, vg, ws, bs, dsk, wglu, bglu, name):
    t = p.shape[0]
    wglu, glu_layer = _layer_of(wglu)

    def body(p_ref, yf_ref, yb_ref, vg_ref, ws_ref, bs_ref, d_ref, wg_ref, bg_ref, o_ref, vn_ref, mixed_ref):
        vhat, _ = _layer_norm_stats(p_ref[:, AW:2 * AW])
        vn_ref[...] = (vhat * vg_ref[...]).astype(BF)
        _spatial_mix(vn_ref, ws_ref, bs_ref, mixed_ref)
        ga = p_ref[:, 2 * AW:3 * AW]
        o_ref[:, 0:AW] = (p_ref[:, 0:AW] * mixed_ref[...] * (ga * _sig(ga))).astype(BF)
        y = _cat_lanes(yf_ref) + _cat_lanes(yb_ref) + d_ref[...] * p_ref[:, 3 * AW:4 * AW]
        y2, _ = _gelu(y)
        z = _dot(y2.astype(BF), wg_ref[0]) + bg_ref[...]
        gb = p_ref[:, 4 * AW:5 * AW]
        o_ref[:, AW:2 * AW] = (y2 * _sig(z) * (gb * _sig(gb))).astype(BF)

    tile = lambda w: pl.BlockSpec((TM, w), lambda i: (i, 0))
    return _call(body, name=name, grid=(t // TM,),
                 in_specs=[tile(5 * AW), Y4_SPEC(), Y4_SPEC(), _full((1, AW)), _full((NGRP, CHUNK, CHUNK)),
                           _full((NGRP, CHUNK, 1)), _full((1, AW)),
                           pl.BlockSpec((1, AW, AW), lambda i: (glu_layer, 0, 0)), _full((1, AW))],
                 out_specs=tile(2 * AW), out_shape=jax.ShapeDtypeStruct((t, 2 * AW), BF),
                 scratch=[pltpu.VMEM((TM, AW), BF), pltpu.VMEM((TM, AW), F32)])(p, yf, yb, vg, ws, bs, dsk, wglu, bglu)


def mix_bwd(p, yf, yb, dmix, vg, ws, bs, dsk, wglu, bglu, name):
    t = p.shape[0]
    wglu, glu_layer = _layer_of(wglu)

    def body(p_ref, yf_ref, yb_ref, dm_ref, vg_ref, ws_ref, bs_ref, d_ref, wg_ref, bg_ref,
             dpa_ref, dy_ref, dws_ref, dbs_ref, dvg_ref, dd_ref, dwg_ref, dbg_ref,
             vn_ref, mixed_ref, dmx_ref, dvn_ref):
        @pl.when(pl.program_id(0) == 0)
        def _():
            for r in (dws_ref, dbs_ref, dvg_ref, dd_ref, dwg_ref, dbg_ref):
                r[...] = jnp.zeros_like(r)

        vhat, rstd = _layer_norm_stats(p_ref[:, AW:2 * AW])
        vgv = vg_ref[...]
        vn_ref[...] = (vhat * vgv).astype(BF)
        _spatial_mix(vn_ref, ws_ref, bs_ref, mixed_ref)
        u = p_ref[:, 0:AW]
        ga = p_ref[:, 2 * AW:3 * AW]
        sga = _sig(ga)
        dya = dm_ref[:, 0:AW]
        mixed = mixed_ref[...]
        dpa_ref[:, 0:AW] = (dya * mixed * (ga * sga)).astype(BF)
        dpa_ref[:, 2 * AW:3 * AW] = (dya * u * mixed * (sga * (1.0 + ga * (1.0 - sga)))).astype(BF)
        dmx_ref[...] = dya * u * (ga * sga)
        for ch in range(TM // CHUNK):
            rows = slice(ch * CHUNK, (ch + 1) * CHUNK)
            for g in range(NGRP):
                cols = slice(g * CHUNK, (g + 1) * CHUNK)
                dmx = dmx_ref[rows, cols]
                dmxb = dmx.astype(BF)
                dws_ref[g] += _dot(dmxb, vn_ref[rows, cols], 1, 1)
                dbs_ref[g] += jnp.sum(dmx, axis=1, keepdims=True)
                dvn_ref[rows, cols] = _dot(ws_ref[g], dmxb, 0, 0)
        dvn = dvn_ref[...]
        dvg_ref[...] += jnp.sum(dvn * vhat, axis=0, keepdims=True)
        dvh = dvn * vgv
        dpa_ref[:, AW:2 * AW] = (rstd * (dvh - jnp.mean(dvh, axis=-1, keepdims=True)
                                         - vhat * jnp.mean(dvh * vhat, axis=-1, keepdims=True))).astype(BF)

        xs = p_ref[:, 3 * AW:4 * AW]
        y = _cat_lanes(yf_ref) + _cat_lanes(yb_ref) + d_ref[...] * xs
        y2, th = _gelu(y)
        y2b = y2.astype(BF)
        z = _dot(y2b, wg_ref[0]) + bg_ref[...]
        sz = _sig(z)
        gb = p_ref[:, 4 * AW:5 * AW]
        sgb = _sig(gb)
        dyb = dm_ref[:, AW:2 * AW]
        dpa_ref[:, 4 * AW:5 * AW] = (dyb * (y2 * sz) * (sgb * (1.0 + gb * (1.0 - sgb)))).astype(BF)
        dy3 = dyb * (gb * sgb)
        dz = dy3 * y2 * sz * (1.0 - sz)
        dzb = dz.astype(BF)
        dwg_ref[...] += _dot(y2b, dzb, 0, 0)
        dbg_ref[...] += jnp.sum(dz, axis=0, keepdims=True)
        dy2 = dy3 * sz + _dot(dzb, wg_ref[0], 1, 1)
        dgelu = 0.5 * (1.0 + th) + 0.5 * y * (1.0 - th * th) * GELU_K * (1.0 + 3.0 * GELU_C * y * y)
        dy = dy2 * dgelu
        dd_ref[...] += jnp.sum(dy * xs, axis=0, keepdims=True)
        dy_ref[...] = dy

    tile = lambda w: pl.BlockSpec((TM, w), lambda i: (i, 0))
    return _call(body, name=name, grid=(t // TM,),
                 in_specs=[tile(5 * AW), Y4_SPEC(), Y4_SPEC(), tile(2 * AW), _full((1, AW)), _full((NGRP, CHUNK, CHUNK)),
                           _full((NGRP, CHUNK, 1)), _full((1, AW)),
                           pl.BlockSpec((1, AW, AW), lambda i: (glu_layer, 0, 0)), _full((1, AW))],
                 out_specs=[tile(5 * AW), tile(AW), _full((NGRP, CHUNK, CHUNK)), _full((NGRP, CHUNK, 1)),
                            _full((1, AW)), _full((1, AW)), _full((AW, AW)), _full((1, AW))],
                 out_shape=[jax.ShapeDtypeStruct((t, 5 * AW), BF),
                            jax.ShapeDtypeStruct((t, AW), F32), jax.ShapeDtypeStruct((NGRP, CHUNK, CHUNK), F32),
                            jax.ShapeDtypeStruct((NGRP, CHUNK, 1), F32), jax.ShapeDtypeStruct((1, AW), F32),
                            jax.ShapeDtypeStruct((1, AW), F32), jax.ShapeDtypeStruct((AW, AW), F32),
                            jax.ShapeDtypeStruct((1, AW), F32)],
                 scratch=[pltpu.VMEM((TM, AW), BF), pltpu.VMEM((TM, AW), F32), pltpu.VMEM((TM, AW), F32),
                          pltpu.VMEM((TM, AW), F32)])(p, yf, yb, dmix, vg, ws, bs, dsk, wglu, bglu)


LN = 512
NBLK = SW // LN
UB = AW // NBLK
SCAN_R = 32
SCAN_G = TM // SCAN_R
PW_ROWS = SCAN_R
POW_EXP = list(range(1, SCAN_R + 1))


def s5_disc(lam_re, lam_im, dt, lam_re_r, lam_im_r, dt_r, b_re, b_im):
    nexp = jnp.asarray(np.array(POW_EXP, np.float32).reshape(PW_ROWS, 1))

    def body(n_ref, lr_ref, li_ref, dt_ref, lrr_ref, lir_ref, dtr_ref, br_ref, bi_ref,
             pr_ref, pi_ref, bbr_ref, bbi_ref):
        for dr in range(2):
            dtl = jnp.exp(dt_ref[dr:dr + 1, :])
            zr = lr_ref[dr:dr + 1, :] * dtl
            zi = li_ref[dr:dr + 1, :] * dtl
            mag = jnp.exp(n_ref[...] * zr)
            ang = n_ref[...] * zi
            pr_ref[dr] = mag * jnp.cos(ang)
            pi_ref[dr] = mag * jnp.sin(ang)
        lr, li, dtv = lrr_ref[...], lir_ref[...], jnp.exp(dtr_ref[...])
        mag = jnp.exp(lr * dtv)
        nr = mag * jnp.cos(li * dtv) - 1.0
        ni = mag * jnp.sin(li * dtv)
        den = lr * lr + li * li
        fr = (nr * lr + ni * li) / den
        fi = (ni * lr - nr * li) / den
        bbr_ref[...] = fr * br_ref[...] - fi * bi_ref[...]
        bbi_ref[...] = fr * bi_ref[...] + fi * br_ref[...]

    rows = lam_re_r.shape[0]
    return _call(body, name="s5_disc",
                 out_shape=[jax.ShapeDtypeStruct((2, PW_ROWS, SW), F32), jax.ShapeDtypeStruct((2, PW_ROWS, SW), F32),
                            jax.ShapeDtypeStruct((rows, SP), F32), jax.ShapeDtypeStruct((rows, SP), F32)])(
        nexp, lam_re, lam_im, dt, lam_re_r, lam_im_r, dt_r, b_re, b_im)


def s5_param_bwd(lam_re_r, lam_im_r, dt_r, b_re, b_im, da_re, da_im, dbb_re, dbb_im):
    rows = lam_re_r.shape[0]
    ng = rows // SH
    seg = jnp.asarray(np.kron(np.eye(ng, dtype=np.float32), np.ones((1, SH), np.float32)))

    def body(seg_ref, lr_ref, li_ref, dt_ref, br_ref, bi_ref, dar_ref, dai_ref, dbbr_ref, dbbi_ref,
             dlr_ref, dli_ref, ddt_ref, dbr_ref, dbi_ref):
        lr, li, dtv = lr_ref[...], li_ref[...], jnp.exp(dt_ref[...])
        mag = jnp.exp(lr * dtv)
        lbr = mag * jnp.cos(li * dtv)
        lbi = mag * jnp.sin(li * dtv)
        den = lr * lr + li * li
        nr, ni = lbr - 1.0, lbi
        fr = (nr * lr + ni * li) / den
        fi = (ni * lr - nr * li) / den
        br, bi = br_ref[...], bi_ref[...]
        gbr, gbi = dbbr_ref[...], dbbi_ref[...]
        dbr_ref[...] = gbr * fr + gbi * fi
        dbi_ref[...] = gbi * fr - gbr * fi
        gfr = gbr * br + gbi * bi
        gfi = gbi * br - gbr * bi
        ilr, ili = lr / den, -li / den
        gnr = gfr * ilr + gfi * ili
        gni = gfi * ilr - gfr * ili
        qr = -(fr * ilr - fi * ili)
        qi = -(fr * ili + fi * ilr)
        glr = gfr * qr + gfi * qi
        gli = gfi * qr - gfr * qi
        first = (lax.broadcasted_iota(jnp.int32, (rows, 1), 0) % SH) == 0
        glbr = gnr + jnp.where(first, dar_ref[...], 0.0)
        glbi = gni + jnp.where(first, dai_ref[...], 0.0)
        gzr = glbr * lbr + glbi * lbi
        gzi = glbi * lbr - glbr * lbi
        glr = glr + gzr * dtv
        gli = gli + gzi * dtv
        gdt = (gzr * lr + gzi * li) * dtv
        hi = lax.Precision.HIGHEST
        sg = seg_ref[...]
        dlr_ref[...] = jnp.dot(sg, glr, precision=hi, preferred_element_type=F32)
        dli_ref[...] = jnp.dot(sg, gli, precision=hi, preferred_element_type=F32)
        ddt_ref[...] = jnp.sum(jnp.dot(sg, gdt, precision=hi, preferred_element_type=F32), axis=1, keepdims=True)

    return _call(body, name="s5_param_bwd",
                 out_shape=[jax.ShapeDtypeStruct((ng, SP), F32), jax.ShapeDtypeStruct((ng, SP), F32),
                            jax.ShapeDtypeStruct((ng, 1), F32), jax.ShapeDtypeStruct((rows, SP), F32),
                            jax.ShapeDtypeStruct((rows, SP), F32)])(
        seg, lam_re_r, lam_im_r, dt_r, b_re, b_im, da_re, da_im, dbb_re, dbb_im)


def _tile_order(kind, nt):
    if kind == "fwd":
        return lambda i: i
    if kind == "bwd":
        return lambda i: jnp.where(i == 0, 0, nt - i)
    if kind == "fwd_adj":
        return lambda i: nt - 1 - i
    if kind == "bwd_adj":
        return lambda i: jnp.where(i == nt - 1, 0, i + 1)
    raise ValueError(kind)


XS_BLK = 3 * AW // 128


def _load_perm(refs):
    return jnp.concatenate(
        [jnp.concatenate([ref[pl.ds(r, SCAN_G, stride=SCAN_R), :] for ref in refs], axis=1) for r in range(SCAN_R)],
        axis=0)


def _store_perm(out_ref, val):
    for r in range(SCAN_R):
        for q in range(AW // 128):
            out_ref[q, pl.ds(r, SCAN_G, stride=SCAN_R), :] = val[r * SCAN_G:(r + 1) * SCAN_G, q * 128:(q + 1) * 128]


def _scan2(br_ref, bi_ref, or_ref, oi_ref, h_off, cin_off, er_ref, ei_ref, cr_ref, ci_ref, pr_ref, pi_ref, reverse,
           corr=None):
    gpt = SCAN_G
    nr = SCAN_R
    offsets = list(range(nr))[::-1] if reverse else list(range(nr))
    blocks = [slice(b * LN, (b + 1) * LN) for b in range(NBLK)]
    slab = lambda r: slice(r * gpt, (r + 1) * gpt)
    a1 = [(pr_ref[0:1, c], pi_ref[0:1, c]) for c in blocks]
    x = [None] * NBLK
    for r in offsets:
        for b, c in enumerate(blocks):
            if x[b] is None:
                x[b] = (br_ref[slab(r), c], bi_ref[slab(r), c])
            else:
                (ar, ai), (xr, xi) = a1[b], x[b]
                x[b] = (br_ref[slab(r), c] + ar * xr - ai * xi, bi_ref[slab(r), c] + ar * xi + ai * xr)
    an = [(pr_ref[nr - 1:nr, c], pi_ref[nr - 1:nr, c]) for c in blocks]
    k = [(cr_ref[:, c], ci_ref[:, c]) for c in blocks]
    for g in (range(gpt - 1, -1, -1) if reverse else range(gpt)):
        for b, c in enumerate(blocks):
            (ar, ai), (kr, ki), (xr, xi) = an[b], k[b], x[b]
            er_ref[g:g + 1, c] = kr
            ei_ref[g:g + 1, c] = ki
            k[b] = (xr[g:g + 1, :] + ar * kr - ai * ki, xi[g:g + 1, :] + ar * ki + ai * kr)
    for b, c in enumerate(blocks):
        cr_ref[:, c] = k[b][0]
        ci_ref[:, c] = k[b][1]
        x[b] = (er_ref[:, c], ei_ref[:, c])
        if cin_off is not None:
            or_ref[cin_off:cin_off + gpt, c] = x[b][0]
            oi_ref[cin_off:cin_off + gpt, c] = x[b][1]
    acc = [None] * NBLK
    for r in offsets:
        for b, c in enumerate(blocks):
            (ar, ai), (xr, xi) = a1[b], x[b]
            x[b] = (br_ref[slab(r), c] + ar * xr - ai * xi, bi_ref[slab(r), c] + ar * xi + ai * xr)
            or_ref[h_off + r * gpt:h_off + (r + 1) * gpt, c] = x[b][0]
            oi_ref[h_off + r * gpt:h_off + (r + 1) * gpt, c] = x[b][1]
            if corr is not None:
                wr_ref, wi_ref, w_off = corr[:3]
                wr, wi = wr_ref[w_off + r * gpt:w_off + (r + 1) * gpt, c], wi_ref[w_off + r * gpt:w_off + (r + 1) * gpt, c]
                pr_, pi_ = x[b][0] * wr + x[b][1] * wi, x[b][1] * wr - x[b][0] * wi
                acc[b] = (pr_, pi_) if acc[b] is None else (acc[b][0] + pr_, acc[b][1] + pi_)
    if corr is not None:
        sr_ref, si_ref = corr[3:]
        for b, c in enumerate(blocks):
            sr_ref[:, c] += jnp.sum(acc[b][0], axis=0, keepdims=True)
            si_ref[:, c] += jnp.sum(acc[b][1], axis=0, keepdims=True)


HS_ROWS = TM + SCAN_G


def _hs_offsets(reverse):
    return (0, SCAN_G) if reverse else (SCAN_G, 0)


def _dir_spec(dr, shape):
    return pl.BlockSpec((1,) + shape, lambda i: (dr,) + (0,) * len(shape))


def s5q_fwd(p, bb_re, bb_im, ct_re, ct_im, pw_re, pw_im, dr, reverse, name):
    t = p.shape[0]
    nt = t // TM
    order = _tile_order("bwd" if reverse else "fwd", nt)
    nq = AW // 128
    h_off, p_off = _hs_offsets(reverse)

    def body(*refs):
        x_refs = refs[:nq]
        bbr_ref, bbi_ref, ctr_ref, cti_ref, pr_ref, pi_ref = [r.at[0] for r in refs[nq:nq + 6]]
        y_ref, hsr_ref, hsi_ref = refs[nq + 6:nq + 9]
        br_ref, bi_ref, er_ref, ei_ref, cr_ref, ci_ref = refs[nq + 9:]

        @pl.when(pl.program_id(0) == 0)
        def _():
            cr_ref[...] = jnp.zeros_like(cr_ref)
            ci_ref[...] = jnp.zeros_like(ci_ref)

        xb = _load_perm(x_refs).astype(BF)
        for j in range(NBLK):
            cols = slice(j * LN, (j + 1) * LN)
            br_ref[:, cols] = _dot(xb[:, j * UB:(j + 1) * UB], bbr_ref[j])
            bi_ref[:, cols] = _dot(xb[:, j * UB:(j + 1) * UB], bbi_ref[j])
        _scan2(br_ref, bi_ref, hsr_ref, hsi_ref, h_off, TM if reverse else 0, er_ref, ei_ref, cr_ref, ci_ref,
               pr_ref, pi_ref, reverse)
        y = jnp.concatenate(
            [_dot(hsr_ref[h_off:h_off + TM, j * LN:(j + 1) * LN].astype(BF), ctr_ref[j])
             - _dot(hsi_ref[h_off:h_off + TM, j * LN:(j + 1) * LN].astype(BF), cti_ref[j]) for j in range(NBLK)], axis=1)
        _store_perm(y_ref, y)

    state = lambda: pl.BlockSpec((HS_ROWS, SW), lambda i: (order(i), 0))
    xspec = lambda q: pl.BlockSpec((TM, 128), lambda i: (order(i), XS_BLK + q))
    return _call(body, name=name, grid=(nt,),
                 in_specs=[xspec(q) for q in range(nq)]
                 + [_dir_spec(dr, (NBLK, UB, LN)), _dir_spec(dr, (NBLK, UB, LN)), _dir_spec(dr, (NBLK, LN, UB)),
                    _dir_spec(dr, (NBLK, LN, UB)), _dir_spec(dr, (PW_ROWS, SW)), _dir_spec(dr, (PW_ROWS, SW))],
                 out_specs=[pl.BlockSpec((nq, TM, 128), lambda i: (0, order(i), 0)), state(), state()],
                 out_shape=[jax.ShapeDtypeStruct((nq, t, 128), F32), jax.ShapeDtypeStruct((nt * HS_ROWS, SW), F32),
                            jax.ShapeDtypeStruct((nt * HS_ROWS, SW), F32)],
                 scratch=[pltpu.VMEM((TM, SW), F32), pltpu.VMEM((TM, SW), F32),
                          pltpu.VMEM((SCAN_G, SW), F32), pltpu.VMEM((SCAN_G, SW), F32),
                          pltpu.VMEM((1, SW), F32), pltpu.VMEM((1, SW), F32)])(
        *([p] * nq), bb_re, bb_im, ct_re, ct_im, pw_re, pw_im)


def s5q_bwd(p, hs_re, hs_im, dy, bb_re, bb_im, ct_re, ct_im, pw_re, pw_im_conj, dr, reverse, name):
    t = p.shape[0]
    nt = t // TM
    order = _tile_order("bwd_adj" if reverse else "fwd_adj", nt)
    nq = AW // 128
    h_off, p_off = _hs_offsets(reverse)

    def body(*refs):
        x_refs, dy_refs = refs[:nq], refs[nq:2 * nq]
        (hsr_ref, hsi_ref, bbr_ref, bbi_ref, ctr_ref, cti_ref, pr_ref, pi_ref,
         dx_ref, dar_ref, dai_ref, dbbr_ref, dbbi_ref, dcr_ref, dci_ref,
         qr_ref, qi_ref, gr_ref, gi_ref, er_ref, ei_ref, cr_ref, ci_ref) = refs[2 * nq:]
        bbr_ref, bbi_ref, ctr_ref, cti_ref, pr_ref, pi_ref = [
            r.at[0] for r in (bbr_ref, bbi_ref, ctr_ref, cti_ref, pr_ref, pi_ref)]

        @pl.when(pl.program_id(0) == 0)
        def _():
            for r in (cr_ref, ci_ref, dar_ref, dai_ref, dbbr_ref, dbbi_ref, dcr_ref, dci_ref):
                r[...] = jnp.zeros_like(r)

        xb = _load_perm(x_refs).astype(BF)
        dyb = _load_perm(dy_refs).astype(BF)
        for j in range(NBLK):
            cols = slice(j * LN, (j + 1) * LN)
            qr_ref[:, cols] = _dot(dyb[:, j * UB:(j + 1) * UB], ctr_ref[j], 1, 1)
            qi_ref[:, cols] = -_dot(dyb[:, j * UB:(j + 1) * UB], cti_ref[j], 1, 1)
        _scan2(qr_ref, qi_ref, gr_ref, gi_ref, 0, None, er_ref, ei_ref, cr_ref, ci_ref, pr_ref, pi_ref, not reverse,
               corr=(hsr_ref, hsi_ref, p_off, dar_ref, dai_ref))
        dxs = []
        for j in range(NBLK):
            cols = slice(j * LN, (j + 1) * LN)
            xj = xb[:, j * UB:(j + 1) * UB]
            dyj = dyb[:, j * UB:(j + 1) * UB]
            grb, gib = gr_ref[:, cols].astype(BF), gi_ref[:, cols].astype(BF)
            dcr_ref[j] += _dot(dyj, hsr_ref[h_off:h_off + TM, cols].astype(BF), 0, 0)
            dci_ref[j] += -_dot(dyj, hsi_ref[h_off:h_off + TM, cols].astype(BF), 0, 0)
            dbbr_ref[j] += _dot(xj, grb, 0, 0)
            dbbi_ref[j] += _dot(xj, gib, 0, 0)
            dxs.append(_dot(grb, bbr_ref[j], 1, 1) + _dot(gib, bbi_ref[j], 1, 1))
        _store_perm(dx_ref, jnp.concatenate(dxs, axis=1))

    state = lambda: pl.BlockSpec((HS_ROWS, SW), lambda i: (order(i), 0))
    blockd = lambda: _full((NBLK, UB, LN))
    xspec = lambda q: pl.BlockSpec((TM, 128), lambda i: (order(i), XS_BLK + q))
    dyspec = lambda q: pl.BlockSpec((TM, 128), lambda i: (order(i), q))
    return _call(body, name=name, grid=(nt,),
                 in_specs=[xspec(q) for q in range(nq)] + [dyspec(q) for q in range(nq)]
                 + [state(), state(), _dir_spec(dr, (NBLK, UB, LN)), _dir_spec(dr, (NBLK, UB, LN)),
                    _dir_spec(dr, (NBLK, LN, UB)), _dir_spec(dr, (NBLK, LN, UB)),
                    _dir_spec(dr, (PW_ROWS, SW)), _dir_spec(dr, (PW_ROWS, SW))],
                 out_specs=[pl.BlockSpec((nq, TM, 128), lambda i: (0, order(i), 0)), _full((1, SW)), _full((1, SW)),
                            blockd(), blockd(), blockd(), blockd()],
                 out_shape=[jax.ShapeDtypeStruct((nq, t, 128), F32), jax.ShapeDtypeStruct((1, SW), F32),
                            jax.ShapeDtypeStruct((1, SW), F32)] + [jax.ShapeDtypeStruct((NBLK, UB, LN), F32)] * 4,
                 scratch=[pltpu.VMEM((TM, SW), F32), pltpu.VMEM((TM, SW), F32),
                          pltpu.VMEM((TM, SW), F32), pltpu.VMEM((TM, SW), F32),
                          pltpu.VMEM((SCAN_G, SW), F32), pltpu.VMEM((SCAN_G, SW), F32),
                          pltpu.VMEM((1, SW), F32), pltpu.VMEM((1, SW), F32)])(
        *([p] * nq), *([dy] * nq), hs_re, hs_im, bb_re, bb_im, ct_re, ct_im, pw_re, pw_im_conj)


def s5p_dx_sum(dy, dsk, dxf, dxb, dp, name):
    t = dy.shape[0]
    nq = AW // 128

    def body(dy_ref, d_ref, f_ref, b_ref, dp_ref, o_ref):
        o_ref[...] = (dy_ref[...] * d_ref[...] + _cat_lanes(f_ref) + _cat_lanes(b_ref)).astype(BF)

    tr = _mm_rows(t)
    tile = pl.BlockSpec((tr, AW), lambda i: (i, 0))
    blk4 = pl.BlockSpec((nq, tr, 128), lambda i: (0, i, 0))
    return pl.pallas_call(
        body, name=name, grid=(t // tr,),
        in_specs=[tile, _full((1, AW)), blk4, blk4, pl.BlockSpec(memory_space=pl.ANY)],
        out_specs=pl.BlockSpec((tr, AW), lambda i: (i, 3)), out_shape=jax.ShapeDtypeStruct(dp.shape, dp.dtype),
        input_output_aliases={4: 0},
        compiler_params=pltpu.CompilerParams(vmem_limit_bytes=VMEM_LIMIT_BYTES))(dy, dsk, dxf, dxb, dp)


SCALE = HD ** -0.5
NHEAD_NORM = NQ + NKV


def _partner(x):
    half0 = (lax.broadcasted_iota(jnp.int32, (1, HD), 1) % 64) < 32
    return jnp.where(half0, pltpu.roll(x, HD - 32, 1), pltpu.roll(x, 32, 1))


def attn_prep(p, qg, kg, cos, sins, name):
    t = p.shape[0]

    def body(p_ref, qg_ref, kg_ref, cos_ref, sin_ref, o_ref):
        cv, sv = cos_ref[...], sin_ref[...]
        for h in range(NHEAD_NORM):
            cols = slice(h * HD, (h + 1) * HD)
            blk = p_ref[:, cols]
            r = lax.rsqrt(jnp.mean(blk * blk, axis=-1, keepdims=True) + EPS)
            xn = blk * r * (qg_ref[...] if h < NQ else kg_ref[...])
            rot = xn * cv + _partner(xn) * sv
            o_ref[:, cols] = ((rot * SCALE) if h < NQ else rot).astype(BF)
        vcols = slice(NHEAD_NORM * HD, (NHEAD_NORM + NKV) * HD)
        o_ref[:, vcols] = p_ref[:, vcols].astype(BF)

    w = (NHEAD_NORM + NKV) * HD
    tr = _mm_rows(t)
    tile = lambda ww: pl.BlockSpec((tr, ww), lambda i: (i, 0))
    return _call(body, name=name, grid=(t // tr,),
                 in_specs=[tile(w), _full((1, HD)), _full((1, HD)), tile(HD), tile(HD)],
                 out_specs=tile(w), out_shape=jax.ShapeDtypeStruct((t, w), BF))(p, qg, kg, cos, sins)


def attn_prep_bwd(p, dq, dk, dv, qg, kg, cos, sins, dp, name):
    t = p.shape[0]

    def body(p_ref, dq_ref, dk_ref, dv_ref, qg_ref, kg_ref, cos_ref, sin_ref, dp_ref, o_ref, dqg_ref, dkg_ref):
        @pl.when(pl.program_id(0) == 0)
        def _():
            dqg_ref[...] = jnp.zeros_like(dqg_ref)
            dkg_ref[...] = jnp.zeros_like(dkg_ref)

        cv, sv = cos_ref[...], sin_ref[...]
        for h in range(NHEAD_NORM):
            cols = slice(h * HD, (h + 1) * HD)
            blk = p_ref[:, cols]
            r = lax.rsqrt(jnp.mean(blk * blk, axis=-1, keepdims=True) + EPS)
            xh = blk * r
            if h < NQ:
                drot = dq_ref[:, cols] * SCALE
                gv, dg_ref = qg_ref[...], dqg_ref
            else:
                drot = dk_ref[:, (h - NQ) * HD:(h - NQ + 1) * HD]
                gv, dg_ref = kg_ref[...], dkg_ref
            dxn = drot * cv + _partner(drot * sv)
            dg_ref[...] += jnp.sum(dxn * xh, axis=0, keepdims=True)
            dxh = dxn * gv
            o_ref[:, cols] = (r * (dxh - xh * jnp.mean(dxh * xh, axis=-1, keepdims=True))).astype(BF)
        o_ref[:, NHEAD_NORM * HD:(NHEAD_NORM + NKV) * HD] = dv_ref[...].astype(BF)

    w = (NHEAD_NORM + NKV) * HD
    tr = _mm_rows(t)
    tile = lambda ww: pl.BlockSpec((tr, ww), lambda i: (i, 0))
    return pl.pallas_call(
        body, name=name, grid=(t // tr,),
        in_specs=[tile(w), tile(NQ * HD), tile(NKV * HD), tile(NKV * HD), _full((1, HD)), _full((1, HD)),
                  tile(HD), tile(HD), pl.BlockSpec(memory_space=pl.ANY)],
        out_specs=[tile(w), _full((1, HD)), _full((1, HD))],
        out_shape=[jax.ShapeDtypeStruct(dp.shape, dp.dtype), jax.ShapeDtypeStruct((1, HD), F32),
                   jax.ShapeDtypeStruct((1, HD), F32)],
        input_output_aliases={8: 0},
        compiler_params=pltpu.CompilerParams(vmem_limit_bytes=VMEM_LIMIT_BYTES))(p, dq, dk, dv, qg, kg, cos, sins, dp)


KCOL = NQ
VCOL = NQ + NKV
GCOL = (NQ + 2 * NKV)
QPK = NQ // NKV
ATT_KCHUNK = 512


def attn_fwd(qkv, p, name):
    t = qkv.shape[0]

    def body(q_ref, k_ref, v_ref, g_ref, o_ref, mix_ref, lse_ref):
        def attend(nk):
            for hh in range(QPK):
                attend_head(nk, slice(hh * HD, (hh + 1) * HD))

        def attend_head(nk, cols):
            q = q_ref[:, cols]
            chunks = [(k0, min(k0 + 2 * ATT_KCHUNK, nk)) for k0 in range(0, nk, 2 * ATT_KCHUNK)]
            s_next = _dot(q, k_ref[chunks[0][0]:chunks[0][1], :], 1, 1)
            m = l = acc = None
            for ci, (k0, k1) in enumerate(chunks):
                s = s_next
                if ci + 1 < len(chunks):
                    s_next = _dot(q, k_ref[chunks[ci + 1][0]:chunks[ci + 1][1], :], 1, 1)
                mc = jnp.max(s, axis=-1, keepdims=True)
                m_new = mc if m is None else jnp.maximum(m, mc)
                pe = jnp.exp(s - m_new)
                lc = jnp.sum(pe, axis=-1, keepdims=True)
                pv = _dot(pe.astype(BF), v_ref[k0:k1, :])
                if m is None:
                    l, acc = lc, pv
                else:
                    alpha = jnp.exp(m - m_new)
                    l, acc = alpha * l + lc, alpha * acc + pv
                m = m_new
            o = acc / l
            gt = g_ref[:, cols]
            o_ref[:, cols] = o
            mix_ref[:, cols] = (o * (gt * _sig(gt))).astype(BF)
            lse_ref[:, cols] = jnp.broadcast_to(m + jnp.log(l), (TM, HD))

        pl.when(pl.program_id(1) == 0)(lambda: attend(NC))
        pl.when(pl.program_id(1) > 0)(lambda: attend(t))

    blk = pl.BlockSpec((TM, QPK * HD), lambda kv, i: (i, kv))
    return _call(body, name=name, grid=(NKV, t // TM),
                 in_specs=[blk, pl.BlockSpec((t, HD), lambda kv, i: (0, KCOL + kv)),
                           pl.BlockSpec((t, HD), lambda kv, i: (0, VCOL + kv)),
                           pl.BlockSpec((TM, QPK * HD), lambda kv, i: (i, GCOL // QPK + kv))],
                 out_specs=[blk, blk, blk],
                 out_shape=[jax.ShapeDtypeStruct((t, NQ * HD), F32), jax.ShapeDtypeStruct((t, NQ * HD), BF),
                            jax.ShapeDtypeStruct((t, NQ * HD), F32)])(qkv, qkv, qkv, p)


def attn_bwd(qkv, p, dmix, o, lse, name):
    t = qkv.shape[0]

    def body(q_ref, k_ref, v_ref, g_ref, dm_ref, o_ref, lse_ref, dq_ref, dg_ref, dk_ref, dv_ref):
        i = pl.program_id(1)

        @pl.when(i == 0)
        def _():
            dk_ref[...] = jnp.zeros_like(dk_ref)
            dv_ref[...] = jnp.zeros_like(dv_ref)

        def bwd(nk):
            for hh in range(QPK):
                bwd_head(nk, slice(hh * HD, (hh + 1) * HD))

        def bwd_head(nk, cols):
            gt = g_ref[:, cols]
            sg = _sig(gt)
            ov = o_ref[:, cols]
            dmv = dm_ref[:, cols]
            dg_ref[:, cols] = (dmv * ov * (sg * (1.0 + gt * (1.0 - sg)))).astype(BF)
            do = dmv * (gt * sg)
            dr = jnp.sum(do * ov, axis=-1, keepdims=True)
            dob = do.astype(BF)
            q = q_ref[:, cols]
            lse = lse_ref[:, cols][:, 0:1]
            chunks = [slice(k0, min(k0 + ATT_KCHUNK, nk)) for k0 in range(0, nk, ATT_KCHUNK)]
            nxt = (_dot(q, k_ref[chunks[0], :], 1, 1), _dot(dob, v_ref[chunks[0], :], 1, 1))
            dq = None
            for ci, keys in enumerate(chunks):
                s, dp = nxt
                if ci + 1 < len(chunks):
                    nxt = (_dot(q, k_ref[chunks[ci + 1], :], 1, 1), _dot(dob, v_ref[chunks[ci + 1], :], 1, 1))
                pe = jnp.exp(s - lse)
                dsb = (pe * (dp - dr)).astype(BF)
                part = _dot(dsb, k_ref[keys, :])
                dq = part if dq is None else dq + part
                dv_ref[keys, :] += _dot(pe.astype(BF), dob, 0, 0)
                dk_ref[keys, :] += _dot(dsb, q, 0, 0)
            dq_ref[:, cols] = dq

        pl.when(i == 0)(lambda: bwd(NC))
        pl.when(i > 0)(lambda: bwd(t))

    blk = pl.BlockSpec((TM, QPK * HD), lambda kv, i: (i, kv))
    gate = pl.BlockSpec((TM, QPK * HD), lambda kv, i: (i, GCOL // QPK + kv))
    acc = pl.BlockSpec((t, HD), lambda kv, i: (0, kv))
    return _call(body, name=name, grid=(NKV, t // TM),
                 in_specs=[blk, pl.BlockSpec((t, HD), lambda kv, i: (0, KCOL + kv)),
                           pl.BlockSpec((t, HD), lambda kv, i: (0, VCOL + kv)), gate, blk, blk, blk],
                 out_specs=[blk, gate, acc, acc],
                 out_shape=[jax.ShapeDtypeStruct((t, NQ * HD), F32), jax.ShapeDtypeStruct((t, (GCOL + NQ) * HD), BF),
                            jax.ShapeDtypeStruct((t, NKV * HD), F32), jax.ShapeDtypeStruct((t, NKV * HD), F32)])(
        qkv, qkv, qkv, p, dmix, o, lse)


def _row_tile(rows, row_bytes, cap=2 * 1024 * 1024):
    if rows * row_bytes <= cap or rows % 8:
        return rows
    tr = rows
    while tr * row_bytes > cap and tr % 16 == 0:
        tr //= 2
    return tr


def _adamw_update(w_ref, g_ref, m_ref, v_ref, d_ref, nm_ref, nv_ref):
    gv = g_ref[...]
    m2 = ADAM_B1 * m_ref[...] + (1.0 - ADAM_B1) * gv
    v2 = ADAM_B2 * v_ref[...] + (1.0 - ADAM_B2) * (gv * gv)
    mh = m2 / (1.0 - ADAM_B1 ** ADAM_STEP)
    vh = v2 / (1.0 - ADAM_B2 ** ADAM_STEP)
    d_ref[...] = -ADAM_LR * (mh / (jnp.sqrt(vh) + ADAM_EPS) + ADAM_WD * w_ref[...])
    nm_ref[...] = m2
    nv_ref[...] = v2


def adamw_many(ws, gs, ms, vs, name):
    n = len(ws)

    def body(*refs):
        for k in range(n):
            _adamw_update(*[refs[j * n + k] for j in range(7)])

    shapes = [jax.ShapeDtypeStruct(w.shape, F32) for w in ws]
    res = _call(body, name=name, out_shape=shapes * 3)(*ws, *gs, *ms, *vs)
    return res[:n], res[n:2 * n], res[2 * n:]


def adamw(w, g, m, v, name):
    r, cdim = w.shape
    tr = _row_tile(r, 4 * max(cdim, 128))

    def body(w_ref, g_ref, m_ref, v_ref, d_ref, nm_ref, nv_ref):
        _adamw_update(w_ref, g_ref, m_ref, v_ref, d_ref, nm_ref, nv_ref)

    tile = pl.BlockSpec((tr, cdim), lambda i: (i, 0))
    sh = jax.ShapeDtypeStruct((r, cdim), F32)
    return _call(body, name=name, grid=(r // tr,), in_specs=[tile] * 4, out_specs=[tile] * 3,
                 out_shape=[sh, sh, sh])(w, g, m, v)


def sum_lead(a, name, out_dtype=F32):
    n, r, cdim = a.shape
    tr = _row_tile(r, 4 * n * max(cdim, 128))

    def body(a_ref, o_ref):
        acc = a_ref[0].astype(F32)
        for k in range(1, n):
            acc = acc + a_ref[k].astype(F32)
        o_ref[...] = acc.astype(o_ref.dtype)

    return _call(body, name=name, grid=(r // tr,),
                 in_specs=[pl.BlockSpec((n, tr, cdim), lambda i: (0, i, 0))],
                 out_specs=pl.BlockSpec((tr, cdim), lambda i: (i, 0)),
                 out_shape=jax.ShapeDtypeStruct((r, cdim), out_dtype))(a)


_FLIPS = {"xy": [(1, 0, 0), (0, 1, 0), (1, 1, 0)], "c": [(0, 0, 1)],
          "all": [(0, 0, 1), (0, 1, 0), (0, 1, 1), (1, 0, 0), (1, 0, 1), (1, 1, 0), (1, 1, 1)]}
_GROUP_SIZE = {"xy": 4, "c": 2, "all": 8}


def _group_index(group, x, y, c):
    return {"xy": 2 * x + y, "c": c, "all": 4 * x + 2 * y + c}[group]


def exchange(items, name):
    plan = []
    for arr, group, kind in items:
        chunk = arr.shape if kind == "gather" else arr.shape[1:]
        plan.append((group, kind, chunk))
    ncopy = sum(len(_FLIPS[g]) for g, _, _ in plan)
    nitem = len(plan)

    def body(*refs):
        srcs, dsts = refs[:nitem], refs[nitem:2 * nitem]
        send_sems, recv_sems, local_sems = refs[2 * nitem:]
        x, y, c = lax.axis_index("x"), lax.axis_index("y"), lax.axis_index("c")
        sends, recvs, locals_ = [], [], []
        n = 0
        for k, (group, kind, _) in enumerate(plan):
            me = _group_index(group, x, y, c)
            own = srcs[k] if kind == "gather" else srcs[k].at[me]
            locals_.append(pltpu.make_async_copy(own, dsts[k].at[me], local_sems.at[k]))
            for fx, fy, fc in _FLIPS[group]:
                px, py, pc = (1 - x if fx else x), (1 - y if fy else y), (1 - c if fc else c)
                peer = _group_index(group, px, py, pc)
                src = srcs[k] if kind == "gather" else srcs[k].at[peer]
                sends.append(pltpu.make_async_remote_copy(
                    src_ref=src, dst_ref=dsts[k].at[me], send_sem=send_sems.at[n], recv_sem=recv_sems.at[n],
                    device_id=(px, py, pc), device_id_type=MESH))
                recvs.append(pltpu.make_async_remote_copy(
                    src_ref=src, dst_ref=dsts[k].at[peer], send_sem=send_sems.at[n], recv_sem=recv_sems.at[n],
                    device_id=(px, py, pc), device_id_type=MESH))
                n += 1
        for cp in locals_ + sends:
            cp.start()
        for cp in recvs:
            cp.wait_recv()
        for cp in sends:
            cp.wait_send()
        for cp in locals_:
            cp.wait()

    anyspec = pl.BlockSpec(memory_space=pl.ANY)
    outs = [jax.ShapeDtypeStruct((_GROUP_SIZE[g],) + tuple(chunk), arr.dtype)
            for (arr, _, _), (g, _, chunk) in zip(items, plan)]
    res = pl.pallas_call(
        body, name=name, out_shape=outs, in_specs=[anyspec] * nitem, out_specs=[anyspec] * nitem,
        scratch_shapes=[pltpu.SemaphoreType.DMA((ncopy,)), pltpu.SemaphoreType.DMA((ncopy,)),
                        pltpu.SemaphoreType.DMA((nitem,))],
        compiler_params=pltpu.CompilerParams(has_side_effects=True))(*[a for a, _, _ in items])
    return list(res)


D2D_PIECES = 4


def d2d(items, name):
    n = len(items)
    swaps = [k for k, (_, kind) in enumerate(items) if kind == "swap"]

    def pieces_of(rows):
        npc = D2D_PIECES if rows % (8 * D2D_PIECES) == 0 else 1
        return npc, rows // npc

    ncopy = sum(pieces_of(a.shape[0] if kind == "gather" else a.shape[1])[0] for a, kind in items)

    def body(*refs):
        srcs, outs = refs[:n], refs[n:2 * n]
        stages = dict(zip(swaps, refs[2 * n:2 * n + len(swaps)]))
        send_sems, recv_sems, local_sems = refs[2 * n + len(swaps):]
        x, y, c = lax.axis_index("x"), lax.axis_index("y"), lax.axis_index("c")
        sib = (x, y, 1 - c)

        def remote(src, dst, q):
            return pltpu.make_async_remote_copy(src_ref=src, dst_ref=dst, send_sem=send_sems.at[q],
                                                recv_sem=recv_sems.at[q], device_id=sib, device_id_type=MESH)

        copies = []
        q = 0
        for k, (arr, kind) in enumerate(items):
            npc, pr = pieces_of(arr.shape[0] if kind == "gather" else arr.shape[1])
            for pc in range(npc):
                rs = pl.ds(pc * pr, pr)
                if kind == "gather":
                    mine, theirs = outs[k].at[c, rs], outs[k].at[1 - c, rs]
                    copies.append((pltpu.make_async_copy(srcs[k].at[rs], mine, local_sems.at[q]),
                                   remote(mine, mine, q), remote(theirs, theirs, q)))
                else:
                    stage, land = stages[k].at[rs], outs[k].at[rs]
                    copies.append((pltpu.make_async_copy(srcs[k].at[1 - c, rs], stage, local_sems.at[q]),
                                   remote(stage, land, q), remote(stage, land, q)))
                q += 1
        for loc, _, _ in copies:
            loc.start()
        for loc, send, _ in copies:
            loc.wait()
            send.start()
        for _, _, recv in copies:
            recv.wait_recv()
        for _, send, _ in copies:
            send.wait_send()

    outs = [jax.ShapeDtypeStruct((2,) + a.shape if kind == "gather" else a.shape[1:], a.dtype) for a, kind in items]
    res = pl.pallas_call(
        body, name=name, out_shape=outs, in_specs=[pl.BlockSpec(memory_space=pl.ANY)] * n,
        out_specs=[pl.BlockSpec(memory_space=pltpu.VMEM)] * n,
        scratch_shapes=[pltpu.VMEM(items[k][0].shape[1:], items[k][0].dtype) for k in swaps]
        + [pltpu.SemaphoreType.DMA((ncopy,)), pltpu.SemaphoreType.DMA((ncopy,)), pltpu.SemaphoreType.DMA((ncopy,))],
        compiler_params=pltpu.CompilerParams(has_side_effects=True, vmem_limit_bytes=VMEM_LIMIT_BYTES))(
        *[a for a, _ in items])
    return list(res)


def sum_own(pair, got, name, out_dtype=F32):
    _, r, cdim = pair.shape
    tr = _row_tile(r, 4 * 2 * max(cdim, 128))

    def body(c_ref, p_ref, g_ref, o_ref):
        o_ref[...] = (p_ref[0] + g_ref[...]).astype(o_ref.dtype)

    me = lax.axis_index("c").astype(jnp.int32).reshape(1)
    return pl.pallas_call(
        body, name=name, out_shape=jax.ShapeDtypeStruct((r, cdim), out_dtype),
        grid_spec=pltpu.PrefetchScalarGridSpec(
            num_scalar_prefetch=1, grid=(r // tr,),
            in_specs=[pl.BlockSpec((1, tr, cdim), lambda i, c_ref: (c_ref[0], i, 0)),
                      pl.BlockSpec((tr, cdim), lambda i, c_ref: (i, 0))],
            out_specs=pl.BlockSpec((tr, cdim), lambda i, c_ref: (i, 0))),
        compiler_params=pltpu.CompilerParams(vmem_limit_bytes=VMEM_LIMIT_BYTES))(me, pair, got)


_SMALL = ["c_ctx", "norm_g", "b_mod", "gm_v_g", "gm_w_s", "gm_b_s", "s5_lam_re", "s5_lam_im", "s5_log_dt",
          "s5_b_re", "s5_b_im", "s5_c_re", "s5_c_im", "s5_d", "s5_b_glu", "q_norm_g", "k_norm_g", "final_g"]
_BIG = ["we_in", "we_out", "s5_w_glu", "wo_in", "wo_out"]
_WEIGHTS = ["c_ctx", "norm_g", "w_mod", "b_mod", "we_in", "we_out", "gm_v_g", "gm_w_s", "gm_b_s", "s5_lam_re",
            "s5_lam_im", "s5_log_dt", "s5_b_re", "s5_b_im", "s5_c_re", "s5_c_im", "s5_d", "s5_w_glu", "s5_b_glu",
            "wo_in", "wo_out", "q_norm_g", "k_norm_g", "final_g"]
_SMALL_ALIGN = 8 * 8 * 128


def _rope_tables(n_lat):
    rows = n_lat // GRID_W
    row = jnp.repeat(jnp.arange(rows), GRID_W)
    col = jnp.tile(jnp.arange(GRID_W), rows)
    freqs = ROPE_THETA ** (-jnp.arange(HD // 4, dtype=F32) / (HD // 4))
    ar, ac = row[:, None] * freqs, col[:, None] * freqs
    cos = jnp.concatenate([jnp.cos(ar), jnp.cos(ar), jnp.cos(ac), jnp.cos(ac)], axis=1)
    sins = jnp.concatenate([-jnp.sin(ar), jnp.sin(ar), -jnp.sin(ac), jnp.sin(ac)], axis=1)
    cos = jnp.concatenate([jnp.ones((NC, HD), F32), cos], axis=0)
    sins = jnp.concatenate([jnp.zeros((NC, HD), F32), sins], axis=0)
    return cos, sins


def _block_diag(v, transpose):
    gpb = SG // NBLK
    v = v.reshape(2, NBLK, gpb, SH, SP)
    eye = jnp.eye(gpb, dtype=v.dtype)
    if transpose:
        return jnp.einsum("djahp,ab->djapbh", v, eye).reshape(2, NBLK, LN, UB)
    return jnp.einsum("djahp,ab->djahbp", v, eye).reshape(2, NBLK, UB, LN)


def _diag_blocks(m):
    gpb = SG // NBLK
    return jnp.einsum("jahap->jahp", m.reshape(NBLK, gpb, SH, gpb, SP)).reshape(SG, SH, SP)


def _view2d(a):
    if a.ndim == 1:
        return a.reshape(1, -1)
    if a.shape[-1] < 64 and a.size % 1024 == 0:
        return a.reshape(-1, 1024)
    return a.reshape(-1, a.shape[-1])


def kernel(x, c, ctx, c_ctx, norm_g, w_mod, b_mod, we_in, we_out, gm_v_g, gm_w_s, gm_b_s, s5_lam_re, s5_lam_im, s5_log_dt, s5_b_re, s5_b_im, s5_c_re, s5_c_im, s5_d, s5_w_glu, s5_b_glu, wo_in, wo_out, q_norm_g, k_norm_g, final_g, loss_target, m_c_ctx, m_norm_g, m_w_mod, m_b_mod, m_we_in, m_we_out, m_gm_v_g, m_gm_w_s, m_gm_b_s, m_s5_lam_re, m_s5_lam_im, m_s5_log_dt, m_s5_b_re, m_s5_b_im, m_s5_c_re, m_s5_c_im, m_s5_d, m_s5_w_glu, m_s5_b_glu, m_wo_in, m_wo_out, m_q_norm_g, m_k_norm_g, m_final_g, v_c_ctx, v_norm_g, v_w_mod, v_b_mod, v_we_in, v_we_out, v_gm_v_g, v_gm_w_s, v_gm_b_s, v_s5_lam_re, v_s5_lam_im, v_s5_log_dt, v_s5_b_re, v_s5_b_im, v_s5_c_re, v_s5_c_im, v_s5_d, v_s5_w_glu, v_s5_b_glu, v_wo_in, v_wo_out, v_q_norm_g, v_k_norm_g, v_final_g):
    weights = dict(c_ctx=c_ctx, norm_g=norm_g, w_mod=w_mod, b_mod=b_mod, we_in=we_in, we_out=we_out, gm_v_g=gm_v_g,
                   gm_w_s=gm_w_s, gm_b_s=gm_b_s, s5_lam_re=s5_lam_re, s5_lam_im=s5_lam_im, s5_log_dt=s5_log_dt,
                   s5_b_re=s5_b_re, s5_b_im=s5_b_im, s5_c_re=s5_c_re, s5_c_im=s5_c_im, s5_d=s5_d, s5_w_glu=s5_w_glu,
                   s5_b_glu=s5_b_glu, wo_in=wo_in, wo_out=wo_out, q_norm_g=q_norm_g, k_norm_g=k_norm_g,
                   final_g=final_g)
    mom_m = dict(c_ctx=m_c_ctx, norm_g=m_norm_g, w_mod=m_w_mod, b_mod=m_b_mod, we_in=m_we_in, we_out=m_we_out,
                 gm_v_g=m_gm_v_g, gm_w_s=m_gm_w_s, gm_b_s=m_gm_b_s, s5_lam_re=m_s5_lam_re, s5_lam_im=m_s5_lam_im,
                 s5_log_dt=m_s5_log_dt, s5_b_re=m_s5_b_re, s5_b_im=m_s5_b_im, s5_c_re=m_s5_c_re, s5_c_im=m_s5_c_im,
                 s5_d=m_s5_d, s5_w_glu=m_s5_w_glu, s5_b_glu=m_s5_b_glu, wo_in=m_wo_in, wo_out=m_wo_out,
                 q_norm_g=m_q_norm_g, k_norm_g=m_k_norm_g, final_g=m_final_g)
    mom_v = dict(c_ctx=v_c_ctx, norm_g=v_norm_g, w_mod=v_w_mod, b_mod=v_b_mod, we_in=v_we_in, we_out=v_we_out,
                 gm_v_g=v_gm_v_g, gm_w_s=v_gm_w_s, gm_b_s=v_gm_b_s, s5_lam_re=v_s5_lam_re, s5_lam_im=v_s5_lam_im,
                 s5_log_dt=v_s5_log_dt, s5_b_re=v_s5_b_re, s5_b_im=v_s5_b_im, s5_c_re=v_s5_c_re, s5_c_im=v_s5_c_im,
                 s5_d=v_s5_d, s5_w_glu=v_s5_w_glu, s5_b_glu=v_s5_b_glu, wo_in=v_wo_in, wo_out=v_wo_out,
                 q_norm_g=v_q_norm_g, k_norm_g=v_k_norm_g, final_g=v_final_g)

    ixy = 2 * lax.axis_index("x") + lax.axis_index("y")
    n_lat = x.shape[1]
    nl = norm_g.shape[0]
    nmod = w_mod.shape[2]
    xin = (ctx[0], x[0])

    ic = lax.axis_index("c")
    mine = [lax.dynamic_index_in_dim(weights[n], ic, 0, keepdims=False).astype(BF) for n in _BIG]
    got = exchange([(m_, "xy", "gather") for m_ in mine] + [(c, "xy", "gather")], "gather_weights")
    both = d2d([(g_.reshape(-1, g_.shape[-1]), "gather") for g_ in got[:len(_BIG)]], "swap_weights")
    both = [b_.reshape((2,) + g_.shape) for b_, g_ in zip(both, got)]
    wein = [(both[0], l) for l in range(2)]
    weout = [(both[1].reshape(2, 1, D, D), l) for l in range(2)]
    wglu = [(both[2].reshape(2, AW, AW), l) for l in range(2)]
    woin = [(both[3], l) for l in range(2)]
    woout = [(both[4].reshape(2, 1, D, D), l) for l in range(2)]
    c_group = got[len(_BIG)].reshape(4, D)

    cond = jnp.concatenate([c_group, jnp.broadcast_to(c_ctx.reshape(1, D), (4, D))], axis=0)
    b_shard = lax.dynamic_slice(b_mod, (0, ixy * nmod), (nl, nmod)).reshape(nl, 1, nmod)
    mpart = ada_fwd(cond, w_mod, b_shard)
    m_lat, m_ctx = exchange([(jnp.transpose(mpart[:, 0:4], (1, 0, 2)), "xy", "scatter"),
                             (mpart[:, 4], "xy", "gather")], "exchange_mod")
    m_lat = jnp.transpose(m_lat, (1, 0, 2)).reshape(nl, 3, D)
    m_ctx = jnp.transpose(m_ctx, (1, 0, 2)).reshape(nl, 3, D)
    mods = [jnp.stack([m_ctx[l], m_lat[l]], axis=0) for l in range(nl)]

    loss_part, dx, g, d_norm_g, d_mod_lat, d_mod_ctx, d_final_g = _local_step(
        xin, loss_target[0], mods, wein, weout, wglu, woin, woout, weights)
    grad_x = dx.reshape(1, n_lat, D)

    d_mod_lat, d_mod_ctx = jnp.stack(d_mod_lat), jnp.stack(d_mod_ctx)
    dm_send = jnp.stack([d_mod_lat.reshape(nl, 4, nmod), d_mod_ctx.reshape(nl, 4, nmod)])
    (dm_got,) = exchange([(jnp.transpose(dm_send, (2, 0, 1, 3)), "xy", "scatter")], "exchange_dmod")
    dm_rows = jnp.concatenate([dm_got[:, 0], dm_got[:, 1]], axis=0)
    gw_mod, d_cctx = ada_bwd(cond, jnp.transpose(dm_rows, (1, 0, 2)), w_mod)
    g_small = dict(c_ctx=d_cctx.reshape(D), norm_g=jnp.stack(d_norm_g), b_mod=add2(d_mod_lat, d_mod_ctx, "add_dbmod"),
                   final_g=d_final_g.reshape(D))
    for name in _SMALL:
        if name not in g_small:
            g_small[name] = jnp.stack(g[name])

    flat = jnp.concatenate([g_small[n].reshape(-1) for n in _SMALL])
    nflat = flat.shape[0]
    npad = -(-nflat // _SMALL_ALIGN) * _SMALL_ALIGN
    flat = jnp.concatenate([flat, jnp.zeros((npad - nflat,), F32)]).reshape(8, npad // (8 * 128), 128)
    pairs = [gw_mod.reshape(2, nl // 2 * D, nmod)]
    for name in _BIG:
        st = jnp.stack(g[name]) if isinstance(g[name], list) else g[name]
        pairs.append(st.reshape(2, -1, st.shape[-1]))
    got_a = d2d([(pairs[k], "swap") for k in (1, 2, 3)], "reduce_chip_a")
    got_b = d2d([(pairs[k], "swap") for k in (0, 4, 5)], "reduce_chip_b")
    theirs = [got_b[0]] + got_a + got_b[1:]
    chip = [sum_own(pairs[k], theirs[k], f"sum_chip{k}", F32 if k == 0 else BF) for k in range(len(pairs))]
    parts = exchange([(flat, "all", "scatter")]
                     + [(s_.reshape(4, s_.shape[0] // 4, s_.shape[1]), "xy", "scatter") for s_ in chip[1:]],
                     "reduce_scatter")
    sums = [sum_lead(pt, f"sum_shard{k}") for k, pt in enumerate(parts)]
    full = d2d([(sums[0], "gather"), (chip[0], "gather")] + [(s_, "gather") for s_ in sums[1:]], "all_gather")
    (flat_full,) = exchange([(full[0], "xy", "gather")], "gather_small")
    full = [flat_full] + full[1:]
    flat = full[0].reshape(-1)
    grads = {}
    off = 0
    for name in _SMALL:
        sz = weights[name].size
        grads[name] = flat[off:off + sz].reshape(weights[name].shape)
        off += sz
    grads["w_mod"] = full[1].reshape(w_mod.shape)
    for k, name in enumerate(_BIG):
        grads[name] = full[2 + k].reshape(weights[name].shape)

    delta, new_m, new_v = {}, {}, {}
    views = [_view2d(weights[n]) for n in _SMALL]
    ds, nms, nvs = adamw_many(views, [grads[n].reshape(w2.shape) for n, w2 in zip(_SMALL, views)],
                              [mom_m[n].reshape(w2.shape) for n, w2 in zip(_SMALL, views)],
                              [mom_v[n].reshape(w2.shape) for n, w2 in zip(_SMALL, views)], "adamw_small")
    for n, d2, m2, v2 in zip(_SMALL, ds, nms, nvs):
        shp = weights[n].shape
        delta[n], new_m[n], new_v[n] = d2.reshape(shp), m2.reshape(shp), v2.reshape(shp)
    for name in ["w_mod"] + _BIG:
        w2 = _view2d(weights[name])
        d2, m2, v2 = adamw(w2, grads[name].reshape(w2.shape), mom_m[name].reshape(w2.shape),
                           mom_v[name].reshape(w2.shape), f"adamw_{name}")
        shp = weights[name].shape
        delta[name], new_m[name], new_v[name] = d2.reshape(shp), m2.reshape(shp), v2.reshape(shp)

    loss = lax.psum(loss_part[0, 0], ("x", "y", "c"))
    return (loss, grad_x, *[grads[n] for n in _WEIGHTS], *[delta[n] for n in _WEIGHTS],
            *[new_m[n] for n in _WEIGHTS], *[new_v[n] for n in _WEIGHTS])


def _local_step(xin, target, mods, wein, weout, wglu, woin, woout, w):
    norm_g, gm_v_g, gm_w_s, gm_b_s = w["norm_g"], w["gm_v_g"], w["gm_w_s"], w["gm_b_s"]
    s5_lam_re, s5_lam_im, s5_log_dt = w["s5_lam_re"], w["s5_lam_im"], w["s5_log_dt"]
    s5_b_re, s5_b_im, s5_c_re, s5_c_im = w["s5_b_re"], w["s5_b_im"], w["s5_c_re"], w["s5_c_im"]
    s5_d, s5_b_glu, q_norm_g, k_norm_g, final_g = w["s5_d"], w["s5_b_glu"], w["q_norm_g"], w["k_norm_g"], w["final_g"]
    nl = norm_g.shape[0]
    n_lat = xin[1].shape[0]

    cos, sins = _rope_tables(n_lat)

    s5p = []
    for i in range(2):
        lam_l = (s5_lam_re[i].reshape(2, SW), s5_lam_im[i].reshape(2, SW),
                 jnp.repeat(s5_log_dt[i], SP, axis=1))
        lam_r = (jnp.repeat(s5_lam_re[i].reshape(2 * SG, SP), SH, axis=0),
                 jnp.repeat(s5_lam_im[i].reshape(2 * SG, SP), SH, axis=0),
                 jnp.repeat(s5_log_dt[i].reshape(2 * SG, 1), SH, axis=0))
        b_r = (jnp.transpose(s5_b_re[i], (0, 1, 3, 2)).reshape(2 * SG * SH, SP),
               jnp.transpose(s5_b_im[i], (0, 1, 3, 2)).reshape(2 * SG * SH, SP))
        pw_re, pw_im, bbr, bbi = s5_disc(*lam_l, *lam_r, *b_r)
        s5p.append(dict(
            lam_r=lam_r, b_r=b_r, pw_re=pw_re, pw_im=pw_im, pw_im_conj=-pw_im,
            bb_re=_block_diag(bbr.reshape(2, SG, SH, SP), False).astype(BF),
            bb_im=_block_diag(bbi.reshape(2, SG, SH, SP), False).astype(BF),
            ct_re=_block_diag(s5_c_re[i], True).astype(BF), ct_im=_block_diag(s5_c_im[i], True).astype(BF)))

    saved = []
    xcur = xin
    h = pro_fwd(xin[0], xin[1], norm_g[0].reshape(1, D), mods[0], "pro_fwd0")
    for l in range(nl):
        i = l // 2
        sv = dict(x=xcur, h=h)
        if l % 2 == 0:
            p = mm_nn(h, wein[i], f"in_proj{l}")
            sp = s5p[i]
            for dr, rev in ((0, False), (1, True)):
                sv[f"y{dr}"], sv[f"hpr{dr}"], sv[f"hpi{dr}"] = s5q_fwd(
                    p, sp["bb_re"], sp["bb_im"], sp["ct_re"], sp["ct_im"], sp["pw_re"], sp["pw_im"], dr, rev,
                    f"s5_fwd{l}_{dr}")
            mix = mix_fwd(p, sv["y0"], sv["y1"], gm_v_g[i].reshape(1, AW), gm_w_s[i].astype(BF),
                          gm_b_s[i].reshape(NGRP, CHUNK, 1), s5_d[i].reshape(1, AW), wglu[i],
                          s5_b_glu[i].reshape(1, AW), f"mix_fwd{l}")
            o = mm_nn(mix, weout[i], f"out_proj{l}")
        else:
            p = mm_nn(h, woin[i], f"in_proj{l}")
            sv["qkv"] = attn_prep(p, q_norm_g[i].reshape(1, HD), k_norm_g[i].reshape(1, HD), cos, sins, f"attn_prep{l}")
            sv["o_att"], mix, sv["lse"] = attn_fwd(sv["qkv"], p, f"attn_fwd{l}")
            o = mm_nn(mix, woout[i], f"out_proj{l}")
        sv.update(p=p, mix=mix, o=o)
        saved.append(sv)
        if l < nl - 1:
            xcur, h = res_pro_fwd(xcur, o, mods[l], True, norm_g[l + 1].reshape(1, D), mods[l + 1], f"res_pro_fwd{l}")
        else:
            xcur = res_fwd(xcur, o, mods[l], False, f"res_fwd{l}")

    loss_part, dx, d_final_g, do, dgt = final_loss(xcur, target, final_g.reshape(1, D), saved[-1]["o"], mods[-1])

    g = {}
    gbuf = {}
    d_norm_g, d_mod_lat, d_mod_ctx = [None] * nl, [None] * nl, [None] * nl
    for name in ("s5_w_glu", "gm_v_g", "gm_w_s", "gm_b_s", "s5_lam_re", "s5_lam_im",
                 "s5_log_dt", "s5_b_re", "s5_b_im", "s5_c_re", "s5_c_im", "s5_d", "s5_b_glu", "q_norm_g", "k_norm_g"):
        g[name] = [None, None]
    for l in reversed(range(nl)):
        i = l // 2
        sv = saved[l]
        w_out = weout[i] if l % 2 == 0 else woout[i]
        dmix = mm_nt(do, w_out, f"out_dgrad{l}")
        out_name, in_name = ("we_out", "we_in") if l % 2 == 0 else ("wo_out", "wo_in")
        gbuf[out_name] = mm_tn(sv["mix"], do, 1, f"out_wgrad{l}", slot=i, into=gbuf.get(out_name))
        if l % 2 == 0:
            sp = s5p[i]
            (dp, dy, g["gm_w_s"][i], dbs, dvg, dd, g["s5_w_glu"][i], dbg) = mix_bwd(
                sv["p"], sv["y0"], sv["y1"], dmix, gm_v_g[i].reshape(1, AW), gm_w_s[i].astype(BF),
                gm_b_s[i].reshape(NGRP, CHUNK, 1), s5_d[i].reshape(1, AW), wglu[i], s5_b_glu[i].reshape(1, AW),
                f"mix_bwd{l}")
            g["gm_b_s"][i], g["gm_v_g"][i] = dbs.reshape(NGRP, CHUNK), dvg.reshape(AW)
            g["s5_d"][i], g["s5_b_glu"][i] = dd.reshape(AW), dbg.reshape(AW)
            g["s5_w_glu"][i] = g["s5_w_glu"][i].reshape(4, AW // 4, AW)
            dxd, das_r, das_i, dbbs_r, dbbs_i, dcs_r, dcs_i = [], [], [], [], [], [], []
            for dr, rev in ((0, False), (1, True)):
                dxs_d, da_r, da_i, dbb_r, dbb_i, dc_r, dc_i = s5q_bwd(
                    sv["p"], sv[f"hpr{dr}"], sv[f"hpi{dr}"], dy, sp["bb_re"], sp["bb_im"], sp["ct_re"], sp["ct_im"],
                    sp["pw_re"], sp["pw_im_conj"], dr, rev, f"s5_bwd{l}_{dr}")
                dxd.append(dxs_d)
                das_r.append(jnp.repeat(da_r.reshape(SG, SP), SH, axis=0))
                das_i.append(jnp.repeat(da_i.reshape(SG, SP), SH, axis=0))
                dbbs_r.append(_diag_blocks(dbb_r).reshape(SG * SH, SP))
                dbbs_i.append(_diag_blocks(dbb_i).reshape(SG * SH, SP))
                dcs_r.append(_diag_blocks(dc_r))
                dcs_i.append(_diag_blocks(dc_i))
            cat = lambda parts: jnp.concatenate(parts, axis=0)
            dlr, dli, dldt, dbr, dbi = s5_param_bwd(*sp["lam_r"], *sp["b_r"], cat(das_r), cat(das_i),
                                                    cat(dbbs_r), cat(dbbs_i))
            g["s5_lam_re"][i], g["s5_lam_im"][i] = dlr.reshape(2, SG, SP), dli.reshape(2, SG, SP)
            g["s5_log_dt"][i] = dldt.reshape(2, SG)
            g["s5_b_re"][i] = jnp.transpose(dbr.reshape(2, SG, SH, SP), (0, 1, 3, 2))
            g["s5_b_im"][i] = jnp.transpose(dbi.reshape(2, SG, SH, SP), (0, 1, 3, 2))
            g["s5_c_re"][i], g["s5_c_im"][i] = jnp.stack(dcs_r), jnp.stack(dcs_i)
            dp = s5p_dx_sum(dy, s5_d[i].reshape(1, AW), dxd[0], dxd[1], dp, f"s5_dx_sum{l}")
            w_in = wein[i]
        else:
            dq, dp, dk, dv = attn_bwd(sv["qkv"], sv["p"], dmix, sv["o_att"], sv["lse"], f"attn_bwd{l}")
            dp, dqg, dkg = attn_prep_bwd(sv["p"], dq, dk, dv, q_norm_g[i].reshape(1, HD),
                                         k_norm_g[i].reshape(1, HD), cos, sins, dp, f"attn_prep_bwd{l}")
            g["q_norm_g"][i], g["k_norm_g"][i] = dqg.reshape(HD), dkg.reshape(HD)
            w_in = woin[i]
        dh = mm_nt(dp, w_in, f"in_dgrad{l}")
        gbuf[in_name] = mm_tn(sv["h"], dp, 4, f"in_wgrad{l}", slot=i, into=gbuf.get(in_name))
        dgt_l = dgt
        if l > 0:
            dx, dmod2, dng, do, dgt = pro_res_bwd(sv["x"], dh, dx, norm_g[l].reshape(1, D), mods[l],
                                                  saved[l - 1]["o"], mods[l - 1], f"pro_res_bwd{l}")
        else:
            dx, dmod2, dng = pro_bwd(xin[0], xin[1], dh, dx, norm_g[l].reshape(1, D), mods[l], f"pro_bwd{l}")
        d_norm_g[l] = dng.reshape(D)
        d_mod_ctx[l] = jnp.concatenate([dmod2[0, 0], dmod2[0, 1], dgt_l[0]])
        d_mod_lat[l] = jnp.concatenate([dmod2[1, 0], dmod2[1, 1], dgt_l[1]])
    g.update(gbuf)
    return loss_part, dx, g, d_norm_g, d_mod_lat, d_mod_ctx, d_final_g
```

```python
import math

import numpy as np
import jax
import jax.numpy as jnp
from jax import lax
from jax.experimental import pallas as pl
from jax.experimental.pallas import tpu as pltpu

F32 = jnp.float32
BF = jnp.bfloat16
MESH = pl.DeviceIdType.MESH

D = 1024
NC = 256
SEQ = 4096
GRID_W = 64
TM = 256
CHUNK = 128
EPS = 1e-6
HD = 128
NQ = 8
NKV = 2
ROPE_THETA = 10000.0
SG = 32
SP = 64
SH = 16
SW = SG * SP
GELU_K = math.sqrt(2.0 / math.pi)
GELU_C = 0.044715
VMEM_LIMIT_BYTES = 56 * 1024 * 1024

ADAM_LR = 0.001
ADAM_B1 = 0.9
ADAM_B2 = 0.999
ADAM_EPS = 1e-08
ADAM_WD = 0.01
ADAM_STEP = 10


def _call(body, *, name, out_shape, grid=None, in_specs=None, out_specs=None, scratch=()):
    kw = {}
    if grid is not None:
        kw["grid"] = grid
    if in_specs is not None:
        kw["in_specs"] = in_specs
    if out_specs is not None:
        kw["out_specs"] = out_specs
    return pl.pallas_call(
        body, name=name, out_shape=out_shape, scratch_shapes=list(scratch),
        compiler_params=pltpu.CompilerParams(vmem_limit_bytes=VMEM_LIMIT_BYTES), **kw)


def _dot(a, b, ca=1, cb=0):
    return lax.dot_general(a, b, (((ca,), (cb,)), ((), ())), preferred_element_type=F32)


def _sig(x):
    return 1.0 / (1.0 + jnp.exp(-x))


def _full(shape):
    n = len(shape)
    return pl.BlockSpec(shape, lambda *_: (0,) * n)


def _mm_rows(t, most=1088):
    for rows in (2176, 1088, 1024, 768, 544, 512, 256):
        if rows <= most and t % rows == 0:
            return rows
    raise ValueError(t)


def _layer_of(w):
    return w if isinstance(w, tuple) else (w[None], 0)


def mm_nn(a, w, name, out_dtype=F32):
    w4, layer = _layer_of(w)
    t, k = a.shape
    _, j, _, nb = w4.shape
    tr = _mm_rows(t, 2176)

    def body(a_ref, w_ref, o_ref):
        o_ref[...] = _dot(a_ref[...], w_ref[0, 0]).astype(o_ref.dtype)

    return _call(body, name=name, grid=(j, t // tr),
                 in_specs=[pl.BlockSpec((tr, k), lambda jj, i: (i, 0)),
                           pl.BlockSpec((1, 1, k, nb), lambda jj, i: (layer, jj, 0, 0))],
                 out_specs=pl.BlockSpec((tr, nb), lambda jj, i: (i, jj)),
                 out_shape=jax.ShapeDtypeStruct((t, j * nb), out_dtype))(a, w4)


def mm_nt(a, w, name, out_dtype=F32):
    w4, layer = _layer_of(w)
    t, _ = a.shape
    _, j, k, nb = w4.shape
    tr = _mm_rows(t)

    def body(a_ref, w_ref, o_ref):
        acc = _dot(a_ref[:, 0:nb], w_ref[0, 0], 1, 1)
        for jj in range(1, j):
            acc = acc + _dot(a_ref[:, jj * nb:(jj + 1) * nb], w_ref[0, jj], 1, 1)
        o_ref[...] = acc.astype(o_ref.dtype)

    return _call(body, name=name, grid=(t // tr,),
                 in_specs=[pl.BlockSpec((tr, j * nb), lambda i: (i, 0)),
                           pl.BlockSpec((1, j, k, nb), lambda i: (layer, 0, 0, 0))],
                 out_specs=pl.BlockSpec((tr, k), lambda i: (i, 0)),
                 out_shape=jax.ShapeDtypeStruct((t, k), out_dtype))(a, w4)


def mm_tn(a, b, j, name, slot=0, into=None):
    t, m = a.shape
    nb = b.shape[1] // j
    tr = _mm_rows(t, 2176)

    def body(a_ref, b_ref, *rest):
        o_ref = rest[-1]

        @pl.when(pl.program_id(1) == 0)
        def _():
            o_ref[...] = jnp.zeros_like(o_ref)
        o_ref[0, 0] += _dot(a_ref[...], b_ref[...], 0, 0)

    in_specs = [pl.BlockSpec((tr, m), lambda jj, i: (i, 0)), pl.BlockSpec((tr, nb), lambda jj, i: (i, jj))]
    args = [a, b]
    alias = {}
    if into is not None:
        in_specs.append(pl.BlockSpec(memory_space=pl.ANY))
        args.append(into)
        alias = {2: 0}
    return pl.pallas_call(
        body, name=name, grid=(j, t // tr), in_specs=in_specs,
        out_specs=pl.BlockSpec((1, 1, m, nb), lambda jj, i: (slot, jj, 0, 0)),
        out_shape=jax.ShapeDtypeStruct((2, j, m, nb), F32), input_output_aliases=alias,
        compiler_params=pltpu.CompilerParams(vmem_limit_bytes=VMEM_LIMIT_BYTES))(*args)


def _mod_rows(mod_ref, i):
    ctx = i == 0
    sh = jnp.where(ctx, mod_ref[0, 0:1, :], mod_ref[1, 0:1, :])
    sc = jnp.where(ctx, mod_ref[0, 1:2, :], mod_ref[1, 1:2, :])
    gt = jnp.where(ctx, mod_ref[0, 2:3, :], mod_ref[1, 2:3, :])
    return sh, sc, gt


def _split_specs():
    return [pl.BlockSpec((TM, D), lambda i: (0, 0)), pl.BlockSpec((TM, D), lambda i: (jnp.maximum(i - 1, 0), 0))]


def _split_tile(c_ref, l_ref, i):
    return jnp.where(i == 0, c_ref[...], l_ref[...])


def pro_fwd(ctx, lat, g, mod, name):
    t = ctx.shape[0] + lat.shape[0]

    def body(c_ref, l_ref, g_ref, mod_ref, h_ref):
        i = pl.program_id(0)
        sh, sc, _ = _mod_rows(mod_ref, i)
        xv = _split_tile(c_ref, l_ref, i)
        r = lax.rsqrt(jnp.mean(xv * xv, axis=-1, keepdims=True) + EPS)
        h_ref[...] = ((xv * r) * g_ref[...] * (1.0 + sc) + sh).astype(BF)

    return _call(body, name=name, grid=(t // TM,),
                 in_specs=_split_specs() + [_full((1, D)), _full((2, 3, D))],
                 out_specs=pl.BlockSpec((TM, D), lambda i: (i, 0)),
                 out_shape=jax.ShapeDtypeStruct((t, D), BF))(ctx, lat, g, mod)


def pro_bwd(ctx, lat, dh, dxn, g, mod, name):
    t = ctx.shape[0] + lat.shape[0]

    def body(c_ref, l_ref, dh_ref, dxn_ref, g_ref, mod_ref, dx_ref, dmod_ref, dg_ref):
        i = pl.program_id(0)

        @pl.when(i == 0)
        def _():
            dmod_ref[...] = jnp.zeros_like(dmod_ref)
            dg_ref[...] = jnp.zeros_like(dg_ref)

        _, sc, _ = _mod_rows(mod_ref, i)
        xv = _split_tile(c_ref, l_ref, i)
        gv = g_ref[...]
        r = lax.rsqrt(jnp.mean(xv * xv, axis=-1, keepdims=True) + EPS)
        xn = xv * r
        dh_v = dh_ref[...]
        e = dh_v * (1.0 + sc)
        dsh = jnp.sum(dh_v, axis=0, keepdims=True)
        dsc = jnp.sum(dh_v * xn * gv, axis=0, keepdims=True)
        dg_ref[...] += jnp.sum(e * xn, axis=0, keepdims=True)
        dxh = e * gv

        @pl.when(i == 0)
        def _():
            dmod_ref[0, 0:1, :] += dsh
            dmod_ref[0, 1:2, :] += dsc

        @pl.when(i > 0)
        def _():
            dx_ref[...] = dxn_ref[...] + r * (dxh - xn * jnp.mean(dxh * xn, axis=-1, keepdims=True))
            dmod_ref[1, 0:1, :] += dsh
            dmod_ref[1, 1:2, :] += dsc

    tile = pl.BlockSpec((TM, D), lambda i: (i, 0))
    return _call(body, name=name, grid=(t // TM,),
                 in_specs=_split_specs() + [tile, tile, _full((1, D)), _full((2, 3, D))],
                 out_specs=[_split_specs()[1], _full((2, 2, D)), _full((1, D))],
                 out_shape=[jax.ShapeDtypeStruct(lat.shape, F32), jax.ShapeDtypeStruct((2, 2, D), F32),
                            jax.ShapeDtypeStruct((1, D), F32)])(ctx, lat, dh, dxn, g, mod)


def _mod_rows_at(mod_ref, i, tr):
    isctx = i * tr + lax.broadcasted_iota(jnp.int32, (tr, 1), 0) < NC
    pick = lambda k: jnp.where(isctx, mod_ref[0, k:k + 1, :], mod_ref[1, k:k + 1, :])
    return pick(0), pick(1), pick(2), isctx


def res_fwd(x, o, mod, update_ctx, name):
    t = x.shape[0]
    tr = _mm_rows(t)

    def body(x_ref, o_ref, mod_ref, y_ref):
        _, _, gt, isctx = _mod_rows_at(mod_ref, pl.program_id(0), tr)
        upd = x_ref[...] + gt * o_ref[...]
        y_ref[...] = upd if update_ctx else jnp.where(isctx, x_ref[...], upd)

    tile = pl.BlockSpec((tr, D), lambda i: (i, 0))
    return _call(body, name=name, grid=(t // tr,), in_specs=[tile, tile, _full((2, 3, D))],
                 out_specs=tile, out_shape=jax.ShapeDtypeStruct((t, D), F32))(x, o, mod)


def res_pro_fwd(x, o, mod, update_ctx, g_next, mod_next, name):
    split = isinstance(x, tuple)
    xs = list(x) if split else [x]
    t = o.shape[0]
    tr = TM if split else _mm_rows(t)

    def body(*refs):
        x_refs = refs[:len(xs)]
        o_ref, mod_ref, g_ref, modn_ref, y_ref, h_ref = refs[len(xs):]
        i = pl.program_id(0)
        _, _, gt, isctx = _mod_rows_at(mod_ref, i, tr)
        xv = _split_tile(x_refs[0], x_refs[1], i) if split else x_refs[0][...]
        xn = xv + gt * o_ref[...]
        if not update_ctx:
            xn = jnp.where(isctx, xv, xn)
        y_ref[...] = xn
        sh, sc, _, _ = _mod_rows_at(modn_ref, i, tr)
        r = lax.rsqrt(jnp.mean(xn * xn, axis=-1, keepdims=True) + EPS)
        h_ref[...] = ((xn * r) * g_ref[...] * (1.0 + sc) + sh).astype(BF)

    tile = pl.BlockSpec((tr, D), lambda i: (i, 0))
    return _call(body, name=name, grid=(t // tr,),
                 in_specs=(_split_specs() if split else [tile]) + [tile, _full((2, 3, D)), _full((1, D)), _full((2, 3, D))],
                 out_specs=[tile, tile],
                 out_shape=[jax.ShapeDtypeStruct((t, D), F32), jax.ShapeDtypeStruct((t, D), BF)])(
        *xs, o, mod, g_next, mod_next)


def pro_res_bwd(x, dh, dxn, g, mod, o_prev, mod_prev, name):
    t = x.shape[0]
    tr = _mm_rows(t, 544)

    def body(x_ref, dh_ref, dxn_ref, g_ref, mod_ref, o_ref, modp_ref, dx_ref, dmod_ref, dg_ref, do_ref, dgt_ref):
        i = pl.program_id(0)

        @pl.when(i == 0)
        def _():
            dmod_ref[...] = jnp.zeros_like(dmod_ref)
            dg_ref[...] = jnp.zeros_like(dg_ref)
            dgt_ref[...] = jnp.zeros_like(dgt_ref)

        _, sc, _, isctx = _mod_rows_at(mod_ref, i, tr)
        xv = x_ref[...]
        gv = g_ref[...]
        r = lax.rsqrt(jnp.mean(xv * xv, axis=-1, keepdims=True) + EPS)
        xn = xv * r
        dh_v = dh_ref[...]
        e = dh_v * (1.0 + sc)
        dsc_rows = dh_v * xn * gv
        dg_ref[...] += jnp.sum(e * xn, axis=0, keepdims=True)
        dxh = e * gv
        dx = dxn_ref[...] + r * (dxh - xn * jnp.mean(dxh * xn, axis=-1, keepdims=True))
        dx_ref[...] = dx
        _, _, gtp, _ = _mod_rows_at(modp_ref, i, tr)
        do_ref[...] = (gtp * dx).astype(BF)
        dgt_rows = dx * o_ref[...]
        dmod_ref[1, 0:1, :] += jnp.sum(dh_v, axis=0, keepdims=True)
        dmod_ref[1, 1:2, :] += jnp.sum(dsc_rows, axis=0, keepdims=True)
        dgt_ref[1:2, :] += jnp.sum(dgt_rows, axis=0, keepdims=True)

        @pl.when(i * tr < NC)
        def _():
            for ref, rows in ((dmod_ref.at[:, 0], dh_v), (dmod_ref.at[:, 1], dsc_rows), (dgt_ref, dgt_rows)):
                part = jnp.sum(jnp.where(isctx, rows, 0.0), axis=0, keepdims=True)
                ref[0:1, :] += part
                ref[1:2, :] += -part

    tile = pl.BlockSpec((tr, D), lambda i: (i, 0))
    return _call(body, name=name, grid=(t // tr,),
                 in_specs=[tile, tile, tile, _full((1, D)), _full((2, 3, D)), tile, _full((2, 3, D))],
                 out_specs=[tile, _full((2, 2, D)), _full((1, D)), tile, _full((2, D))],
                 out_shape=[jax.ShapeDtypeStruct((t, D), F32), jax.ShapeDtypeStruct((2, 2, D), F32),
                            jax.ShapeDtypeStruct((1, D), F32), jax.ShapeDtypeStruct((t, D), BF),
                            jax.ShapeDtypeStruct((2, D), F32)])(x, dh, dxn, g, mod, o_prev, mod_prev)


def final_loss(x, target, g, o_last, mod_last):
    t = x.shape[0]

    def body(x_ref, t_ref, g_ref, o_ref, modp_ref, loss_ref, dx_ref, dg_ref, do_ref, dgt_ref):
        i = pl.program_id(0)

        @pl.when(i == 0)
        def _():
            loss_ref[...] = jnp.zeros_like(loss_ref)
            dg_ref[...] = jnp.zeros_like(dg_ref)
            dx_ref[...] = jnp.zeros_like(dx_ref)
            do_ref[...] = jnp.zeros_like(do_ref)
            dgt_ref[...] = jnp.zeros_like(dgt_ref)

        @pl.when(i > 0)
        def _():
            xv = x_ref[...]
            gv = g_ref[...]
            r = lax.rsqrt(jnp.mean(xv * xv, axis=-1, keepdims=True) + EPS)
            xn = xv * r
            err = xn * gv - t_ref[...]
            loss_ref[...] += (0.5 / D) * jnp.sum(jnp.sum(err * err, axis=1, keepdims=True), axis=0, keepdims=True)
            dy = err * (1.0 / D)
            dg_ref[...] += jnp.sum(dy * xn, axis=0, keepdims=True)
            dxh = dy * gv
            dx = r * (dxh - xn * jnp.mean(dxh * xn, axis=-1, keepdims=True))
            dx_ref[...] = dx
            do_ref[...] = (modp_ref[1, 2:3, :] * dx).astype(BF)
            dgt_ref[1:2, :] += jnp.sum(dx * o_ref[...], axis=0, keepdims=True)

    tile = pl.BlockSpec((TM, D), lambda i: (i, 0))
    return _call(body, name="final_loss", grid=(t // TM,),
                 in_specs=[tile, pl.BlockSpec((TM, D), lambda i: (jnp.maximum(i - 1, 0), 0)), _full((1, D)), tile,
                           _full((2, 3, D))],
                 out_specs=[_full((1, 1)), tile, _full((1, D)), tile, _full((2, D))],
                 out_shape=[jax.ShapeDtypeStruct((1, 1), F32), jax.ShapeDtypeStruct((t, D), F32),
                            jax.ShapeDtypeStruct((1, D), F32), jax.ShapeDtypeStruct((t, D), BF),
                            jax.ShapeDtypeStruct((2, D), F32)])(x, target, g, o_last, mod_last)


def ada_fwd(cond, w_mod, b_mod):
    nl, _, nw = w_mod.shape

    def body(c_ref, w_ref, b_ref, o_ref):
        cv = c_ref[...]
        s = (cv * _sig(cv)).astype(BF)
        o_ref[0] = _dot(s, w_ref[0].astype(BF)) + b_ref[0]

    return _call(body, name="ada_fwd", grid=(nl,),
                 in_specs=[_full((8, D)), pl.BlockSpec((1, D, nw), lambda l: (l, 0, 0)),
                           pl.BlockSpec((1, 1, nw), lambda l: (l, 0, 0))],
                 out_specs=pl.BlockSpec((1, 8, nw), lambda l: (l, 0, 0)),
                 out_shape=jax.ShapeDtypeStruct((nl, 8, nw), F32))(cond, w_mod, b_mod)


def ada_bwd(cond, dm, w_mod):
    nl, _, nw = w_mod.shape

    def body(c_ref, dm_ref, w_ref, gw_ref, dcc_ref, dc_ref):
        l = pl.program_id(0)

        @pl.when(l == 0)
        def _():
            dc_ref[...] = jnp.zeros_like(dc_ref)

        cv = c_ref[...]
        sg = _sig(cv)
        s = (cv * sg).astype(BF)
        dmv = dm_ref[0].astype(BF)
        gw_ref[0] = _dot(s, dmv, 0, 0)
        dc_ref[...] += _dot(dmv, w_ref[0].astype(BF), 1, 1)

        @pl.when(l == nl - 1)
        def _():
            dcond = dc_ref[...] * (sg * (1.0 + cv * (1.0 - sg)))
            dcc_ref[...] = jnp.sum(dcond[4:8], axis=0, keepdims=True)

    return _call(body, name="ada_bwd", grid=(nl,),
                 in_specs=[_full((8, D)), pl.BlockSpec((1, 8, nw), lambda l: (l, 0, 0)),
                           pl.BlockSpec((1, D, nw), lambda l: (l, 0, 0))],
                 out_specs=[pl.BlockSpec((1, D, nw), lambda l: (l, 0, 0)), _full((1, D))],
                 out_shape=[jax.ShapeDtypeStruct((nl, D, nw), F32), jax.ShapeDtypeStruct((1, D), F32)],
                 scratch=[pltpu.VMEM((8, D), F32)])(cond, dm, w_mod)


def add2(a, b, name):
    def body(a_ref, b_ref, o_ref):
        o_ref[...] = a_ref[...] + b_ref[...]

    return _call(body, name=name, out_shape=jax.ShapeDtypeStruct(a.shape, a.dtype))(a, b)


AW = 512
NGRP = 4


def Y4_SPEC():
    return pl.BlockSpec((AW // 128, TM, 128), lambda i: (0, i, 0))


def _cat_lanes(ref):
    return jnp.concatenate([ref[q] for q in range(ref.shape[0])], axis=1)


def _gelu(y):
    t = jnp.tanh(GELU_K * (y + GELU_C * y * y * y))
    return 0.5 * y * (1.0 + t), t


def _layer_norm_stats(v):
    mu = jnp.mean(v, axis=-1, keepdims=True)
    vc = v - mu
    rstd = lax.rsqrt(jnp.mean(vc * vc, axis=-1, keepdims=True) + EPS)
    return vc * rstd, rstd


def _spatial_mix(vn_ref, ws_ref, bs_ref, mixed_ref):
    for ch in range(TM // CHUNK):
        rows = slice(ch * CHUNK, (ch + 1) * CHUNK)
        for g in range(NGRP):
            cols = slice(g * CHUNK, (g + 1) * CHUNK)
            mixed_ref[rows, cols] = _dot(ws_ref[g], vn_ref[rows, cols]) + bs_ref[g]


def mix_fwd(p, yf, yb, vg, ws, bs, dsk, wglu, bglu, name):
    t = p.shape[0]
    wglu, glu_layer = _layer_of(wglu)

    def body(p_ref, yf_ref, yb_ref, vg_ref, ws_ref, bs_ref, d_ref, wg_ref, bg_ref, o_ref, vn_ref, mixed_ref):
        vhat, _ = _layer_norm_stats(p_ref[:, AW:2 * AW])
        vn_ref[...] = (vhat * vg_ref[...]).astype(BF)
        _spatial_mix(vn_ref, ws_ref, bs_ref, mixed_ref)
        ga = p_ref[:, 2 * AW:3 * AW]
        o_ref[:, 0:AW] = (p_ref[:, 0:AW] * mixed_ref[...] * (ga * _sig(ga))).astype(BF)
        y = _cat_lanes(yf_ref) + _cat_lanes(yb_ref) + d_ref[...] * p_ref[:, 3 * AW:4 * AW]
        y2, _ = _gelu(y)
        z = _dot(y2.astype(BF), wg_ref[0]) + bg_ref[...]
        gb = p_ref[:, 4 * AW:5 * AW]
        o_ref[:, AW:2 * AW] = (y2 * _sig(z) * (gb * _sig(gb))).astype(BF)

    tile = lambda w: pl.BlockSpec((TM, w), lambda i: (i, 0))
    return _call(body, name=name, grid=(t // TM,),
                 in_specs=[tile(5 * AW), Y4_SPEC(), Y4_SPEC(), _full((1, AW)), _full((NGRP, CHUNK, CHUNK)),
                           _full((NGRP, CHUNK, 1)), _full((1, AW)),
                           pl.BlockSpec((1, AW, AW), lambda i: (glu_layer, 0, 0)), _full((1, AW))],
                 out_specs=tile(2 * AW), out_shape=jax.ShapeDtypeStruct((t, 2 * AW), BF),
                 scratch=[pltpu.VMEM((TM, AW), BF), pltpu.VMEM((TM, AW), F32)])(p, yf, yb, vg, ws, bs, dsk, wglu, bglu)


def mix_bwd(p, yf, yb, dmix, vg, ws, bs, dsk, wglu, bglu, name):
    t = p.shape[0]
    wglu, glu_layer = _layer_of(wglu)

    def body(p_ref, yf_ref, yb_ref, dm_ref, vg_ref, ws_ref, bs_ref, d_ref, wg_ref, bg_ref,
             dpa_ref, dy_ref, dws_ref, dbs_ref, dvg_ref, dd_ref, dwg_ref, dbg_ref,
             vn_ref, mixed_ref, dmx_ref, dvn_ref):
        @pl.when(pl.program_id(0) == 0)
        def _():
            for r in (dws_ref, dbs_ref, dvg_ref, dd_ref, dwg_ref, dbg_ref):
                r[...] = jnp.zeros_like(r)

        vhat, rstd = _layer_norm_stats(p_ref[:, AW:2 * AW])
        vgv = vg_ref[...]
        vn_ref[...] = (vhat * vgv).astype(BF)
        _spatial_mix(vn_ref, ws_ref, bs_ref, mixed_ref)
        u = p_ref[:, 0:AW]
        ga = p_ref[:, 2 * AW:3 * AW]
        sga = _sig(ga)
        dya = dm_ref[:, 0:AW]
        mixed = mixed_ref[...]
        dpa_ref[:, 0:AW] = (dya * mixed * (ga * sga)).astype(BF)
        dpa_ref[:, 2 * AW:3 * AW] = (dya * u * mixed * (sga * (1.0 + ga * (1.0 - sga)))).astype(BF)
        dmx_ref[...] = dya * u * (ga * sga)
        for ch in range(TM // CHUNK):
            rows = slice(ch * CHUNK, (ch + 1) * CHUNK)
            for g in range(NGRP):
                cols = slice(g * CHUNK, (g + 1) * CHUNK)
                dmx = dmx_ref[rows, cols]
                dmxb = dmx.astype(BF)
                dws_ref[g] += _dot(dmxb, vn_ref[rows, cols], 1, 1)
                dbs_ref[g] += jnp.sum(dmx, axis=1, keepdims=True)
                dvn_ref[rows, cols] = _dot(ws_ref[g], dmxb, 0, 0)
        dvn = dvn_ref[...]
        dvg_ref[...] += jnp.sum(dvn * vhat, axis=0, keepdims=True)
        dvh = dvn * vgv
        dpa_ref[:, AW:2 * AW] = (rstd * (dvh - jnp.mean(dvh, axis=-1, keepdims=True)
                                         - vhat * jnp.mean(dvh * vhat, axis=-1, keepdims=True))).astype(BF)

        xs = p_ref[:, 3 * AW:4 * AW]
        y = _cat_lanes(yf_ref) + _cat_lanes(yb_ref) + d_ref[...] * xs
        y2, th = _gelu(y)
        y2b = y2.astype(BF)
        z = _dot(y2b, wg_ref[0]) + bg_ref[...]
        sz = _sig(z)
        gb = p_ref[:, 4 * AW:5 * AW]
        sgb = _sig(gb)
        dyb = dm_ref[:, AW:2 * AW]
        dpa_ref[:, 4 * AW:5 * AW] = (dyb * (y2 * sz) * (sgb * (1.0 + gb * (1.0 - sgb)))).astype(BF)
        dy3 = dyb * (gb * sgb)
        dz = dy3 * y2 * sz * (1.0 - sz)
        dzb = dz.astype(BF)
        dwg_ref[...] += _dot(y2b, dzb, 0, 0)
        dbg_ref[...] += jnp.sum(dz, axis=0, keepdims=True)
        dy2 = dy3 * sz + _dot(dzb, wg_ref[0], 1, 1)
        dgelu = 0.5 * (1.0 + th) + 0.5 * y * (1.0 - th * th) * GELU_K * (1.0 + 3.0 * GELU_C * y * y)
        dy = dy2 * dgelu
        dd_ref[...] += jnp.sum(dy * xs, axis=0, keepdims=True)
        dy_ref[...] = dy

    tile = lambda w: pl.BlockSpec((TM, w), lambda i: (i, 0))
    return _call(body, name=name, grid=(t // TM,),
                 in_specs=[tile(5 * AW), Y4_SPEC(), Y4_SPEC(), tile(2 * AW), _full((1, AW)), _full((NGRP, CHUNK, CHUNK)),
                           _full((NGRP, CHUNK, 1)), _full((1, AW)),
                           pl.BlockSpec((1, AW, AW), lambda i: (glu_layer, 0, 0)), _full((1, AW))],
                 out_specs=[tile(5 * AW), tile(AW), _full((NGRP, CHUNK, CHUNK)), _full((NGRP, CHUNK, 1)),
                            _full((1, AW)), _full((1, AW)), _full((AW, AW)), _full((1, AW))],
                 out_shape=[jax.ShapeDtypeStruct((t, 5 * AW), BF),
                            jax.ShapeDtypeStruct((t, AW), F32), jax.ShapeDtypeStruct((NGRP, CHUNK, CHUNK), F32),
                            jax.ShapeDtypeStruct((NGRP, CHUNK, 1), F32), jax.ShapeDtypeStruct((1, AW), F32),
                            jax.ShapeDtypeStruct((1, AW), F32), jax.ShapeDtypeStruct((AW, AW), F32),
                            jax.ShapeDtypeStruct((1, AW), F32)],
                 scratch=[pltpu.VMEM((TM, AW), BF), pltpu.VMEM((TM, AW), F32), pltpu.VMEM((TM, AW), F32),
                          pltpu.VMEM((TM, AW), F32)])(p, yf, yb, dmix, vg, ws, bs, dsk, wglu, bglu)


LN = 512
NBLK = SW // LN
UB = AW // NBLK
SCAN_R = 32
SCAN_G = TM // SCAN_R
PW_ROWS = SCAN_R
POW_EXP = list(range(1, SCAN_R + 1))


def s5_disc(lam_re, lam_im, dt, lam_re_r, lam_im_r, dt_r, b_re, b_im):
    nexp = jnp.asarray(np.array(POW_EXP, np.float32).reshape(PW_ROWS, 1))

    def body(n_ref, lr_ref, li_ref, dt_ref, lrr_ref, lir_ref, dtr_ref, br_ref, bi_ref,
             pr_ref, pi_ref, bbr_ref, bbi_ref):
        for dr in range(2):
            dtl = jnp.exp(dt_ref[dr:dr + 1, :])
            zr = lr_ref[dr:dr + 1, :] * dtl
            zi = li_ref[dr:dr + 1, :] * dtl
            mag = jnp.exp(n_ref[...] * zr)
            ang = n_ref[...] * zi
            pr_ref[dr] = mag * jnp.cos(ang)
            pi_ref[dr] = mag * jnp.sin(ang)
        lr, li, dtv = lrr_ref[...], lir_ref[...], jnp.exp(dtr_ref[...])
        mag = jnp.exp(lr * dtv)
        nr = mag * jnp.cos(li * dtv) - 1.0
        ni = mag * jnp.sin(li * dtv)
        den = lr * lr + li * li
        fr = (nr * lr + ni * li) / den
        fi = (ni * lr - nr * li) / den
        bbr_ref[...] = fr * br_ref[...] - fi * bi_ref[...]
        bbi_ref[...] = fr * bi_ref[...] + fi * br_ref[...]

    rows = lam_re_r.shape[0]
    return _call(body, name="s5_disc",
                 out_shape=[jax.ShapeDtypeStruct((2, PW_ROWS, SW), F32), jax.ShapeDtypeStruct((2, PW_ROWS, SW), F32),
                            jax.ShapeDtypeStruct((rows, SP), F32), jax.ShapeDtypeStruct((rows, SP), F32)])(
        nexp, lam_re, lam_im, dt, lam_re_r, lam_im_r, dt_r, b_re, b_im)


def s5_param_bwd(lam_re_r, lam_im_r, dt_r, b_re, b_im, da_re, da_im, dbb_re, dbb_im):
    rows = lam_re_r.shape[0]
    ng = rows // SH
    seg = jnp.asarray(np.kron(np.eye(ng, dtype=np.float32), np.ones((1, SH), np.float32)))

    def body(seg_ref, lr_ref, li_ref, dt_ref, br_ref, bi_ref, dar_ref, dai_ref, dbbr_ref, dbbi_ref,
             dlr_ref, dli_ref, ddt_ref, dbr_ref, dbi_ref):
        lr, li, dtv = lr_ref[...], li_ref[...], jnp.exp(dt_ref[...])
        mag = jnp.exp(lr * dtv)
        lbr = mag * jnp.cos(li * dtv)
        lbi = mag * jnp.sin(li * dtv)
        den = lr * lr + li * li
        nr, ni = lbr - 1.0, lbi
        fr = (nr * lr + ni * li) / den
        fi = (ni * lr - nr * li) / den
        br, bi = br_ref[...], bi_ref[...]
        gbr, gbi = dbbr_ref[...], dbbi_ref[...]
        dbr_ref[...] = gbr * fr + gbi * fi
        dbi_ref[...] = gbi * fr - gbr * fi
        gfr = gbr * br + gbi * bi
        gfi = gbi * br - gbr * bi
        ilr, ili = lr / den, -li / den
        gnr = gfr * ilr + gfi * ili
        gni = gfi * ilr - gfr * ili
        qr = -(fr * ilr - fi * ili)
        qi = -(fr * ili + fi * ilr)
        glr = gfr * qr + gfi * qi
        gli = gfi * qr - gfr * qi
        first = (lax.broadcasted_iota(jnp.int32, (rows, 1), 0) % SH) == 0
        glbr = gnr + jnp.where(first, dar_ref[...], 0.0)
        glbi = gni + jnp.where(first, dai_ref[...], 0.0)
        gzr = glbr * lbr + glbi * lbi
        gzi = glbi * lbr - glbr * lbi
        glr = glr + gzr * dtv
        gli = gli + gzi * dtv
        gdt = (gzr * lr + gzi * li) * dtv
        hi = lax.Precision.HIGHEST
        sg = seg_ref[...]
        dlr_ref[...] = jnp.dot(sg, glr, precision=hi, preferred_element_type=F32)
        dli_ref[...] = jnp.dot(sg, gli, precision=hi, preferred_element_type=F32)
        ddt_ref[...] = jnp.sum(jnp.dot(sg, gdt, precision=hi, preferred_element_type=F32), axis=1, keepdims=True)

    return _call(body, name="s5_param_bwd",
                 out_shape=[jax.ShapeDtypeStruct((ng, SP), F32), jax.ShapeDtypeStruct((ng, SP), F32),
                            jax.ShapeDtypeStruct((ng, 1), F32), jax.ShapeDtypeStruct((rows, SP), F32),
                            jax.ShapeDtypeStruct((rows, SP), F32)])(
        seg, lam_re_r, lam_im_r, dt_r, b_re, b_im, da_re, da_im, dbb_re, dbb_im)


def _tile_order(kind, nt):
    if kind == "fwd":
        return lambda i: i
    if kind == "bwd":
        return lambda i: jnp.where(i == 0, 0, nt - i)
    if kind == "fwd_adj":
        return lambda i: nt - 1 - i
    if kind == "bwd_adj":
        return lambda i: jnp.where(i == nt - 1, 0, i + 1)
    raise ValueError(kind)


XS_BLK = 3 * AW // 128


def _load_perm(refs):
    return jnp.concatenate(
        [jnp.concatenate([ref[pl.ds(r, SCAN_G, stride=SCAN_R), :] for ref in refs], axis=1) for r in range(SCAN_R)],
        axis=0)


def _store_perm(out_ref, val):
    for r in range(SCAN_R):
        for q in range(AW // 128):
            out_ref[q, pl.ds(r, SCAN_G, stride=SCAN_R), :] = val[r * SCAN_G:(r + 1) * SCAN_G, q * 128:(q + 1) * 128]


def _scan2(br_ref, bi_ref, or_ref, oi_ref, h_off, cin_off, er_ref, ei_ref, cr_ref, ci_ref, pr_ref, pi_ref, reverse,
           corr=None):
    gpt = SCAN_G
    nr = SCAN_R
    offsets = list(range(nr))[::-1] if reverse else list(range(nr))
    blocks = [slice(b * LN, (b + 1) * LN) for b in range(NBLK)]
    slab = lambda r: slice(r * gpt, (r + 1) * gpt)
    a1 = [(pr_ref[0:1, c], pi_ref[0:1, c]) for c in blocks]
    x = [None] * NBLK
    for r in offsets:
        for b, c in enumerate(blocks):
            if x[b] is None:
                x[b] = (br_ref[slab(r), c], bi_ref[slab(r), c])
            else:
                (ar, ai), (xr, xi) = a1[b], x[b]
                x[b] = (br_ref[slab(r), c] + ar * xr - ai * xi, bi_ref[slab(r), c] + ar * xi + ai * xr)
    an = [(pr_ref[nr - 1:nr, c], pi_ref[nr - 1:nr, c]) for c in blocks]
    k = [(cr_ref[:, c], ci_ref[:, c]) for c in blocks]
    for g in (range(gpt - 1, -1, -1) if reverse else range(gpt)):
        for b, c in enumerate(blocks):
            (ar, ai), (kr, ki), (xr, xi) = an[b], k[b], x[b]
            er_ref[g:g + 1, c] = kr
            ei_ref[g:g + 1, c] = ki
            k[b] = (xr[g:g + 1, :] + ar * kr - ai * ki, xi[g:g + 1, :] + ar * ki + ai * kr)
    for b, c in enumerate(blocks):
        cr_ref[:, c] = k[b][0]
        ci_ref[:, c] = k[b][1]
        x[b] = (er_ref[:, c], ei_ref[:, c])
        if cin_off is not None:
            or_ref[cin_off:cin_off + gpt, c] = x[b][0]
            oi_ref[cin_off:cin_off + gpt, c] = x[b][1]
    acc = [None] * NBLK
    for r in offsets:
        for b, c in enumerate(blocks):
            (ar, ai), (xr, xi) = a1[b], x[b]
            x[b] = (br_ref[slab(r), c] + ar * xr - ai * xi, bi_ref[slab(r), c] + ar * xi + ai * xr)
            or_ref[h_off + r * gpt:h_off + (r + 1) * gpt, c] = x[b][0]
            oi_ref[h_off + r * gpt:h_off + (r + 1) * gpt, c] = x[b][1]
            if corr is not None:
                wr_ref, wi_ref, w_off = corr[:3]
                wr, wi = wr_ref[w_off + r * gpt:w_off + (r + 1) * gpt, c], wi_ref[w_off + r * gpt:w_off + (r + 1) * gpt, c]
                pr_, pi_ = x[b][0] * wr + x[b][1] * wi, x[b][1] * wr - x[b][0] * wi
                acc[b] = (pr_, pi_) if acc[b] is None else (acc[b][0] + pr_, acc[b][1] + pi_)
    if corr is not None:
        sr_ref, si_ref = corr[3:]
        for b, c in enumerate(blocks):
            sr_ref[:, c] += jnp.sum(acc[b][0], axis=0, keepdims=True)
            si_ref[:, c] += jnp.sum(acc[b][1], axis=0, keepdims=True)


HS_ROWS = TM + SCAN_G


def _hs_offsets(reverse):
    return (0, SCAN_G) if reverse else (SCAN_G, 0)


def _dir_spec(dr, shape):
    return pl.BlockSpec((1,) + shape, lambda i: (dr,) + (0,) * len(shape))


def s5q_fwd(p, bb_re, bb_im, ct_re, ct_im, pw_re, pw_im, dr, reverse, name):
    t = p.shape[0]
    nt = t // TM
    order = _tile_order("bwd" if reverse else "fwd", nt)
    nq = AW // 128
    h_off, p_off = _hs_offsets(reverse)

    def body(*refs):
        x_refs = refs[:nq]
        bbr_ref, bbi_ref, ctr_ref, cti_ref, pr_ref, pi_ref = [r.at[0] for r in refs[nq:nq + 6]]
        y_ref, hsr_ref, hsi_ref = refs[nq + 6:nq + 9]
        br_ref, bi_ref, er_ref, ei_ref, cr_ref, ci_ref = refs[nq + 9:]

        @pl.when(pl.program_id(0) == 0)
        def _():
            cr_ref[...] = jnp.zeros_like(cr_ref)
            ci_ref[...] = jnp.zeros_like(ci_ref)

        xb = _load_perm(x_refs).astype(BF)
        for j in range(NBLK):
            cols = slice(j * LN, (j + 1) * LN)
            br_ref[:, cols] = _dot(xb[:, j * UB:(j + 1) * UB], bbr_ref[j])
            bi_ref[:, cols] = _dot(xb[:, j * UB:(j + 1) * UB], bbi_ref[j])
        _scan2(br_ref, bi_ref, hsr_ref, hsi_ref, h_off, TM if reverse else 0, er_ref, ei_ref, cr_ref, ci_ref,
               pr_ref, pi_ref, reverse)
        y = jnp.concatenate(
            [_dot(hsr_ref[h_off:h_off + TM, j * LN:(j + 1) * LN].astype(BF), ctr_ref[j])
             - _dot(hsi_ref[h_off:h_off + TM, j * LN:(j + 1) * LN].astype(BF), cti_ref[j]) for j in range(NBLK)], axis=1)
        _store_perm(y_ref, y)

    state = lambda: pl.BlockSpec((HS_ROWS, SW), lambda i: (order(i), 0))
    xspec = lambda q: pl.BlockSpec((TM, 128), lambda i: (order(i), XS_BLK + q))
    return _call(body, name=name, grid=(nt,),
                 in_specs=[xspec(q) for q in range(nq)]
                 + [_dir_spec(dr, (NBLK, UB, LN)), _dir_spec(dr, (NBLK, UB, LN)), _dir_spec(dr, (NBLK, LN, UB)),
                    _dir_spec(dr, (NBLK, LN, UB)), _dir_spec(dr, (PW_ROWS, SW)), _dir_spec(dr, (PW_ROWS, SW))],
                 out_specs=[pl.BlockSpec((nq, TM, 128), lambda i: (0, order(i), 0)), state(), state()],
                 out_shape=[jax.ShapeDtypeStruct((nq, t, 128), F32), jax.ShapeDtypeStruct((nt * HS_ROWS, SW), F32),
                            jax.ShapeDtypeStruct((nt * HS_ROWS, SW), F32)],
                 scratch=[pltpu.VMEM((TM, SW), F32), pltpu.VMEM((TM, SW), F32),
                          pltpu.VMEM((SCAN_G, SW), F32), pltpu.VMEM((SCAN_G, SW), F32),
                          pltpu.VMEM((1, SW), F32), pltpu.VMEM((1, SW), F32)])(
        *([p] * nq), bb_re, bb_im, ct_re, ct_im, pw_re, pw_im)


def s5q_bwd(p, hs_re, hs_im, dy, bb_re, bb_im, ct_re, ct_im, pw_re, pw_im_conj, dr, reverse, name):
    t = p.shape[0]
    nt = t // TM
    order = _tile_order("bwd_adj" if reverse else "fwd_adj", nt)
    nq = AW // 128
    h_off, p_off = _hs_offsets(reverse)

    def body(*refs):
        x_refs, dy_refs = refs[:nq], refs[nq:2 * nq]
        (hsr_ref, hsi_ref, bbr_ref, bbi_ref, ctr_ref, cti_ref, pr_ref, pi_ref,
         dx_ref, dar_ref, dai_ref, dbbr_ref, dbbi_ref, dcr_ref, dci_ref,
         qr_ref, qi_ref, gr_ref, gi_ref, er_ref, ei_ref, cr_ref, ci_ref) = refs[2 * nq:]
        bbr_ref, bbi_ref, ctr_ref, cti_ref, pr_ref, pi_ref = [
            r.at[0] for r in (bbr_ref, bbi_ref, ctr_ref, cti_ref, pr_ref, pi_ref)]

        @pl.when(pl.program_id(0) == 0)
        def _():
            for r in (cr_ref, ci_ref, dar_ref, dai_ref, dbbr_ref, dbbi_ref, dcr_ref, dci_ref):
                r[...] = jnp.zeros_like(r)

        xb = _load_perm(x_refs).astype(BF)
        dyb = _load_perm(dy_refs).astype(BF)
        for j in range(NBLK):
            cols = slice(j * LN, (j + 1) * LN)
            qr_ref[:, cols] = _dot(dyb[:, j * UB:(j + 1) * UB], ctr_ref[j], 1, 1)
            qi_ref[:, cols] = -_dot(dyb[:, j * UB:(j + 1) * UB], cti_ref[j], 1, 1)
        _scan2(qr_ref, qi_ref, gr_ref, gi_ref, 0, None, er_ref, ei_ref, cr_ref, ci_ref, pr_ref, pi_ref, not reverse,
               corr=(hsr_ref, hsi_ref, p_off, dar_ref, dai_ref))
        dxs = []
        for j in range(NBLK):
            cols = slice(j * LN, (j + 1) * LN)
            xj = xb[:, j * UB:(j + 1) * UB]
            dyj = dyb[:, j * UB:(j + 1) * UB]
            grb, gib = gr_ref[:, cols].astype(BF), gi_ref[:, cols].astype(BF)
            dcr_ref[j] += _dot(dyj, hsr_ref[h_off:h_off + TM, cols].astype(BF), 0, 0)
            dci_ref[j] += -_dot(dyj, hsi_ref[h_off:h_off + TM, cols].astype(BF), 0, 0)
            dbbr_ref[j] += _dot(xj, grb, 0, 0)
            dbbi_ref[j] += _dot(xj, gib, 0, 0)
            dxs.append(_dot(grb, bbr_ref[j], 1, 1) + _dot(gib, bbi_ref[j], 1, 1))
        _store_perm(dx_ref, jnp.concatenate(dxs, axis=1))

    state = lambda: pl.BlockSpec((HS_ROWS, SW), lambda i: (order(i), 0))
    blockd = lambda: _full((NBLK, UB, LN))
    xspec = lambda q: pl.BlockSpec((TM, 128), lambda i: (order(i), XS_BLK + q))
    dyspec = lambda q: pl.BlockSpec((TM, 128), lambda i: (order(i), q))
    return _call(body, name=name, grid=(nt,),
                 in_specs=[xspec(q) for q in range(nq)] + [dyspec(q) for q in range(nq)]
                 + [state(), state(), _dir_spec(dr, (NBLK, UB, LN)), _dir_spec(dr, (NBLK, UB, LN)),
                    _dir_spec(dr, (NBLK, LN, UB)), _dir_spec(dr, (NBLK, LN, UB)),
                    _dir_spec(dr, (PW_ROWS, SW)), _dir_spec(dr, (PW_ROWS, SW))],
                 out_specs=[pl.BlockSpec((nq, TM, 128), lambda i: (0, order(i), 0)), _full((1, SW)), _full((1, SW)),
                            blockd(), blockd(), blockd(), blockd()],
                 out_shape=[jax.ShapeDtypeStruct((nq, t, 128), F32), jax.ShapeDtypeStruct((1, SW), F32),
                            jax.ShapeDtypeStruct((1, SW), F32)] + [jax.ShapeDtypeStruct((NBLK, UB, LN), F32)] * 4,
                 scratch=[pltpu.VMEM((TM, SW), F32), pltpu.VMEM((TM, SW), F32),
                          pltpu.VMEM((TM, SW), F32), pltpu.VMEM((TM, SW), F32),
                          pltpu.VMEM((SCAN_G, SW), F32), pltpu.VMEM((SCAN_G, SW), F32),
                          pltpu.VMEM((1, SW), F32), pltpu.VMEM((1, SW), F32)])(
        *([p] * nq), *([dy] * nq), hs_re, hs_im, bb_re, bb_im, ct_re, ct_im, pw_re, pw_im_conj)


def s5p_dx_sum(dy, dsk, dxf, dxb, dp, name):
    t = dy.shape[0]
    nq = AW // 128

    def body(dy_ref, d_ref, f_ref, b_ref, dp_ref, o_ref):
        o_ref[...] = (dy_ref[...] * d_ref[...] + _cat_lanes(f_ref) + _cat_lanes(b_ref)).astype(BF)

    tr = _mm_rows(t)
    tile = pl.BlockSpec((tr, AW), lambda i: (i, 0))
    blk4 = pl.BlockSpec((nq, tr, 128), lambda i: (0, i, 0))
    return pl.pallas_call(
        body, name=name, grid=(t // tr,),
        in_specs=[tile, _full((1, AW)), blk4, blk4, pl.BlockSpec(memory_space=pl.ANY)],
        out_specs=pl.BlockSpec((tr, AW), lambda i: (i, 3)), out_shape=jax.ShapeDtypeStruct(dp.shape, dp.dtype),
        input_output_aliases={4: 0},
        compiler_params=pltpu.CompilerParams(vmem_limit_bytes=VMEM_LIMIT_BYTES))(dy, dsk, dxf, dxb, dp)


SCALE = HD ** -0.5
NHEAD_NORM = NQ + NKV


def _partner(x):
    half0 = (lax.broadcasted_iota(jnp.int32, (1, HD), 1) % 64) < 32
    return jnp.where(half0, pltpu.roll(x, HD - 32, 1), pltpu.roll(x, 32, 1))


def attn_prep(p, qg, kg, cos, sins, name):
    t = p.shape[0]

    def body(p_ref, qg_ref, kg_ref, cos_ref, sin_ref, o_ref):
        cv, sv = cos_ref[...], sin_ref[...]
        for h in range(NHEAD_NORM):
            cols = slice(h * HD, (h + 1) * HD)
            blk = p_ref[:, cols]
            r = lax.rsqrt(jnp.mean(blk * blk, axis=-1, keepdims=True) + EPS)
            xn = blk * r * (qg_ref[...] if h < NQ else kg_ref[...])
            rot = xn * cv + _partner(xn) * sv
            o_ref[:, cols] = ((rot * SCALE) if h < NQ else rot).astype(BF)
        vcols = slice(NHEAD_NORM * HD, (NHEAD_NORM + NKV) * HD)
        o_ref[:, vcols] = p_ref[:, vcols].astype(BF)

    w = (NHEAD_NORM + NKV) * HD
    tr = _mm_rows(t)
    tile = lambda ww: pl.BlockSpec((tr, ww), lambda i: (i, 0))
    return _call(body, name=name, grid=(t // tr,),
                 in_specs=[tile(w), _full((1, HD)), _full((1, HD)), tile(HD), tile(HD)],
                 out_specs=tile(w), out_shape=jax.ShapeDtypeStruct((t, w), BF))(p, qg, kg, cos, sins)


def attn_prep_bwd(p, dq, dk, dv, qg, kg, cos, sins, dp, name):
    t = p.shape[0]

    def body(p_ref, dq_ref, dk_ref, dv_ref, qg_ref, kg_ref, cos_ref, sin_ref, dp_ref, o_ref, dqg_ref, dkg_ref):
        @pl.when(pl.program_id(0) == 0)
        def _():
            dqg_ref[...] = jnp.zeros_like(dqg_ref)
            dkg_ref[...] = jnp.zeros_like(dkg_ref)

        cv, sv = cos_ref[...], sin_ref[...]
        for h in range(NHEAD_NORM):
            cols = slice(h * HD, (h + 1) * HD)
            blk = p_ref[:, cols]
            r = lax.rsqrt(jnp.mean(blk * blk, axis=-1, keepdims=True) + EPS)
            xh = blk * r
            if h < NQ:
                drot = dq_ref[:, cols] * SCALE
                gv, dg_ref = qg_ref[...], dqg_ref
            else:
                drot = dk_ref[:, (h - NQ) * HD:(h - NQ + 1) * HD]
                gv, dg_ref = kg_ref[...], dkg_ref
            dxn = drot * cv + _partner(drot * sv)
            dg_ref[...] += jnp.sum(dxn * xh, axis=0, keepdims=True)
            dxh = dxn * gv
            o_ref[:, cols] = (r * (dxh - xh * jnp.mean(dxh * xh, axis=-1, keepdims=True))).astype(BF)
        o_ref[:, NHEAD_NORM * HD:(NHEAD_NORM + NKV) * HD] = dv_ref[...].astype(BF)

    w = (NHEAD_NORM + NKV) * HD
    tr = _mm_rows(t)
    tile = lambda ww: pl.BlockSpec((tr, ww), lambda i: (i, 0))
    return pl.pallas_call(
        body, name=name, grid=(t // tr,),
        in_specs=[tile(w), tile(NQ * HD), tile(NKV * HD), tile(NKV * HD), _full((1, HD)), _full((1, HD)),
                  tile(HD), tile(HD), pl.BlockSpec(memory_space=pl.ANY)],
        out_specs=[tile(w), _full((1, HD)), _full((1, HD))],
        out_shape=[jax.ShapeDtypeStruct(dp.shape, dp.dtype), jax.ShapeDtypeStruct((1, HD), F32),
                   jax.ShapeDtypeStruct((1, HD), F32)],
        input_output_aliases={8: 0},
        compiler_params=pltpu.CompilerParams(vmem_limit_bytes=VMEM_LIMIT_BYTES))(p, dq, dk, dv, qg, kg, cos, sins, dp)


KCOL = NQ
VCOL = NQ + NKV
GCOL = (NQ + 2 * NKV)
QPK = NQ // NKV
ATT_KCHUNK = 512


def attn_fwd(qkv, p, name):
    t = qkv.shape[0]

    def body(q_ref, k_ref, v_ref, g_ref, o_ref, mix_ref, lse_ref):
        def attend(nk):
            for hh in range(QPK):
                attend_head(nk, slice(hh * HD, (hh + 1) * HD))

        def attend_head(nk, cols):
            q = q_ref[:, cols]
            chunks = [(k0, min(k0 + 2 * ATT_KCHUNK, nk)) for k0 in range(0, nk, 2 * ATT_KCHUNK)]
            s_next = _dot(q, k_ref[chunks[0][0]:chunks[0][1], :], 1, 1)
            m = l = acc = None
            for ci, (k0, k1) in enumerate(chunks):
                s = s_next
                if ci + 1 < len(chunks):
                    s_next = _dot(q, k_ref[chunks[ci + 1][0]:chunks[ci + 1][1], :], 1, 1)
                mc = jnp.max(s, axis=-1, keepdims=True)
                m_new = mc if m is None else jnp.maximum(m, mc)
                pe = jnp.exp(s - m_new)
                lc = jnp.sum(pe, axis=-1, keepdims=True)
                pv = _dot(pe.astype(BF), v_ref[k0:k1, :])
                if m is None:
                    l, acc = lc, pv
                else:
                    alpha = jnp.exp(m - m_new)
                    l, acc = alpha * l + lc, alpha * acc + pv
                m = m_new
            o = acc / l
            gt = g_ref[:, cols]
            o_ref[:, cols] = o
            mix_ref[:, cols] = (o * (gt * _sig(gt))).astype(BF)
            lse_ref[:, cols] = jnp.broadcast_to(m + jnp.log(l), (TM, HD))

        pl.when(pl.program_id(1) == 0)(lambda: attend(NC))
        pl.when(pl.program_id(1) > 0)(lambda: attend(t))

    blk = pl.BlockSpec((TM, QPK * HD), lambda kv, i: (i, kv))
    return _call(body, name=name, grid=(NKV, t // TM),
                 in_specs=[blk, pl.BlockSpec((t, HD), lambda kv, i: (0, KCOL + kv)),
                           pl.BlockSpec((t, HD), lambda kv, i: (0, VCOL + kv)),
                           pl.BlockSpec((TM, QPK * HD), lambda kv, i: (i, GCOL // QPK + kv))],
                 out_specs=[blk, blk, blk],
                 out_shape=[jax.ShapeDtypeStruct((t, NQ * HD), F32), jax.ShapeDtypeStruct((t, NQ * HD), BF),
                            jax.ShapeDtypeStruct((t, NQ * HD), F32)])(qkv, qkv, qkv, p)


def attn_bwd(qkv, p, dmix, o, lse, name):
    t = qkv.shape[0]

    def body(q_ref, k_ref, v_ref, g_ref, dm_ref, o_ref, lse_ref, dq_ref, dg_ref, dk_ref, dv_ref):
        i = pl.program_id(1)

        @pl.when(i == 0)
        def _():
            dk_ref[...] = jnp.zeros_like(dk_ref)
            dv_ref[...] = jnp.zeros_like(dv_ref)

        def bwd(nk):
            for hh in range(QPK):
                bwd_head(nk, slice(hh * HD, (hh + 1) * HD))

        def bwd_head(nk, cols):
            gt = g_ref[:, cols]
            sg = _sig(gt)
            ov = o_ref[:, cols]
            dmv = dm_ref[:, cols]
            dg_ref[:, cols] = (dmv * ov * (sg * (1.0 + gt * (1.0 - sg)))).astype(BF)
            do = dmv * (gt * sg)
            dr = jnp.sum(do * ov, axis=-1, keepdims=True)
            dob = do.astype(BF)
            q = q_ref[:, cols]
            lse = lse_ref[:, cols][:, 0:1]
            chunks = [slice(k0, min(k0 + ATT_KCHUNK, nk)) for k0 in range(0, nk, ATT_KCHUNK)]
            nxt = (_dot(q, k_ref[chunks[0], :], 1, 1), _dot(dob, v_ref[chunks[0], :], 1, 1))
            dq = None
            for ci, keys in enumerate(chunks):
                s, dp = nxt
                if ci + 1 < len(chunks):
                    nxt = (_dot(q, k_ref[chunks[ci + 1], :], 1, 1), _dot(dob, v_ref[chunks[ci + 1], :], 1, 1))
                pe = jnp.exp(s - lse)
                dsb = (pe * (dp - dr)).astype(BF)
                part = _dot(dsb, k_ref[keys, :])
                dq = part if dq is None else dq + part
                dv_ref[keys, :] += _dot(pe.astype(BF), dob, 0, 0)
                dk_ref[keys, :] += _dot(dsb, q, 0, 0)
            dq_ref[:, cols] = dq

        pl.when(i == 0)(lambda: bwd(NC))
        pl.when(i > 0)(lambda: bwd(t))

    blk = pl.BlockSpec((TM, QPK * HD), lambda kv, i: (i, kv))
    gate = pl.BlockSpec((TM, QPK * HD), lambda kv, i: (i, GCOL // QPK + kv))
    acc = pl.BlockSpec((t, HD), lambda kv, i: (0, kv))
    return _call(body, name=name, grid=(NKV, t // TM),
                 in_specs=[blk, pl.BlockSpec((t, HD), lambda kv, i: (0, KCOL + kv)),
                           pl.BlockSpec((t, HD), lambda kv, i: (0, VCOL + kv)), gate, blk, blk, blk],
                 out_specs=[blk, gate, acc, acc],
                 out_shape=[jax.ShapeDtypeStruct((t, NQ * HD), F32), jax.ShapeDtypeStruct((t, (GCOL + NQ) * HD), BF),
                            jax.ShapeDtypeStruct((t, NKV * HD), F32), jax.ShapeDtypeStruct((t, NKV * HD), F32)])(
        qkv, qkv, qkv, p, dmix, o, lse)


def _row_tile(rows, row_bytes, cap=2 * 1024 * 1024):
    if rows * row_bytes <= cap or rows % 8:
        return rows
    tr = rows
    while tr * row_bytes > cap and tr % 16 == 0:
        tr //= 2
    return tr


def _adamw_update(w_ref, g_ref, m_ref, v_ref, d_ref, nm_ref, nv_ref):
    gv = g_ref[...]
    m2 = ADAM_B1 * m_ref[...] + (1.0 - ADAM_B1) * gv
    v2 = ADAM_B2 * v_ref[...] + (1.0 - ADAM_B2) * (gv * gv)
    mh = m2 / (1.0 - ADAM_B1 ** ADAM_STEP)
    vh = v2 / (1.0 - ADAM_B2 ** ADAM_STEP)
    d_ref[...] = -ADAM_LR * (mh / (jnp.sqrt(vh) + ADAM_EPS) + ADAM_WD * w_ref[...])
    nm_ref[...] = m2
    nv_ref[...] = v2


def adamw_many(ws, gs, ms, vs, name):
    n = len(ws)

    def body(*refs):
        for k in range(n):
            _adamw_update(*[refs[j * n + k] for j in range(7)])

    shapes = [jax.ShapeDtypeStruct(w.shape, F32) for w in ws]
    res = _call(body, name=name, out_shape=shapes * 3)(*ws, *gs, *ms, *vs)
    return res[:n], res[n:2 * n], res[2 * n:]


def adamw(w, g, m, v, name):
    r, cdim = w.shape
    tr = _row_tile(r, 4 * max(cdim, 128))

    def body(w_ref, g_ref, m_ref, v_ref, d_ref, nm_ref, nv_ref):
        _adamw_update(w_ref, g_ref, m_ref, v_ref, d_ref, nm_ref, nv_ref)

    tile = pl.BlockSpec((tr, cdim), lambda i: (i, 0))
    sh = jax.ShapeDtypeStruct((r, cdim), F32)
    return _call(body, name=name, grid=(r // tr,), in_specs=[tile] * 4, out_specs=[tile] * 3,
                 out_shape=[sh, sh, sh])(w, g, m, v)


def sum_lead(a, name, out_dtype=F32):
    n, r, cdim = a.shape
    tr = _row_tile(r, 4 * n * max(cdim, 128))

    def body(a_ref, o_ref):
        acc = a_ref[0].astype(F32)
        for k in range(1, n):
            acc = acc + a_ref[k].astype(F32)
        o_ref[...] = acc.astype(o_ref.dtype)

    return _call(body, name=name, grid=(r // tr,),
                 in_specs=[pl.BlockSpec((n, tr, cdim), lambda i: (0, i, 0))],
                 out_specs=pl.BlockSpec((tr, cdim), lambda i: (i, 0)),
                 out_shape=jax.ShapeDtypeStruct((r, cdim), out_dtype))(a)


_FLIPS = {"xy": [(1, 0, 0), (0, 1, 0), (1, 1, 0)], "c": [(0, 0, 1)],
          "all": [(0, 0, 1), (0, 1, 0), (0, 1, 1), (1, 0, 0), (1, 0, 1), (1, 1, 0), (1, 1, 1)]}
_GROUP_SIZE = {"xy": 4, "c": 2, "all": 8}


def _group_index(group, x, y, c):
    return {"xy": 2 * x + y, "c": c, "all": 4 * x + 2 * y + c}[group]


def exchange(items, name):
    plan = []
    for arr, group, kind in items:
        chunk = arr.shape if kind == "gather" else arr.shape[1:]
        plan.append((group, kind, chunk))
    ncopy = sum(len(_FLIPS[g]) for g, _, _ in plan)
    nitem = len(plan)

    def body(*refs):
        srcs, dsts = refs[:nitem], refs[nitem:2 * nitem]
        send_sems, recv_sems, local_sems = refs[2 * nitem:]
        x, y, c = lax.axis_index("x"), lax.axis_index("y"), lax.axis_index("c")
        sends, recvs, locals_ = [], [], []
        n = 0
        for k, (group, kind, _) in enumerate(plan):
            me = _group_index(group, x, y, c)
            own = srcs[k] if kind == "gather" else srcs[k].at[me]
            locals_.append(pltpu.make_async_copy(own, dsts[k].at[me], local_sems.at[k]))
            for fx, fy, fc in _FLIPS[group]:
                px, py, pc = (1 - x if fx else x), (1 - y if fy else y), (1 - c if fc else c)
                peer = _group_index(group, px, py, pc)
                src = srcs[k] if kind == "gather" else srcs[k].at[peer]
                sends.append(pltpu.make_async_remote_copy(
                    src_ref=src, dst_ref=dsts[k].at[me], send_sem=send_sems.at[n], recv_sem=recv_sems.at[n],
                    device_id=(px, py, pc), device_id_type=MESH))
                recvs.append(pltpu.make_async_remote_copy(
                    src_ref=src, dst_ref=dsts[k].at[peer], send_sem=send_sems.at[n], recv_sem=recv_sems.at[n],
                    device_id=(px, py, pc), device_id_type=MESH))
                n += 1
        for cp in locals_ + sends:
            cp.start()
        for cp in recvs:
            cp.wait_recv()
        for cp in sends:
            cp.wait_send()
        for cp in locals_:
            cp.wait()

    anyspec = pl.BlockSpec(memory_space=pl.ANY)
    outs = [jax.ShapeDtypeStruct((_GROUP_SIZE[g],) + tuple(chunk), arr.dtype)
            for (arr, _, _), (g, _, chunk) in zip(items, plan)]
    res = pl.pallas_call(
        body, name=name, out_shape=outs, in_specs=[anyspec] * nitem, out_specs=[anyspec] * nitem,
        scratch_shapes=[pltpu.SemaphoreType.DMA((ncopy,)), pltpu.SemaphoreType.DMA((ncopy,)),
                        pltpu.SemaphoreType.DMA((nitem,))],
        compiler_params=pltpu.CompilerParams(has_side_effects=True))(*[a for a, _, _ in items])
    return list(res)


D2D_PIECES = 4


def d2d(items, name):
    n = len(items)
    swaps = [k for k, (_, kind) in enumerate(items) if kind == "swap"]

    def pieces_of(rows):
        npc = D2D_PIECES if rows % (8 * D2D_PIECES) == 0 else 1
        return npc, rows // npc

    ncopy = sum(pieces_of(a.shape[0] if kind == "gather" else a.shape[1])[0] for a, kind in items)

    def body(*refs):
        srcs, outs = refs[:n], refs[n:2 * n]
        stages = dict(zip(swaps, refs[2 * n:2 * n + len(swaps)]))
        send_sems, recv_sems, local_sems = refs[2 * n + len(swaps):]
        x, y, c = lax.axis_index("x"), lax.axis_index("y"), lax.axis_index("c")
        sib = (x, y, 1 - c)

        def remote(src, dst, q):
            return pltpu.make_async_remote_copy(src_ref=src, dst_ref=dst, send_sem=send_sems.at[q],
                                                recv_sem=recv_sems.at[q], device_id=sib, device_id_type=MESH)

        copies = []
        q = 0
        for k, (arr, kind) in enumerate(items):
            npc, pr = pieces_of(arr.shape[0] if kind == "gather" else arr.shape[1])
            for pc in range(npc):
                rs = pl.ds(pc * pr, pr)
                if kind == "gather":
                    mine, theirs = outs[k].at[c, rs], outs[k].at[1 - c, rs]
                    copies.append((pltpu.make_async_copy(srcs[k].at[rs], mine, local_sems.at[q]),
                                   remote(mine, mine, q), remote(theirs, theirs, q)))
                else:
                    stage, land = stages[k].at[rs], outs[k].at[rs]
                    copies.append((pltpu.make_async_copy(srcs[k].at[1 - c, rs], stage, local_sems.at[q]),
                                   remote(stage, land, q), remote(stage, land, q)))
                q += 1
        for loc, _, _ in copies:
            loc.start()
        for loc, send, _ in copies:
            loc.wait()
            send.start()
        for _, _, recv in copies:
            recv.wait_recv()
        for _, send, _ in copies:
            send.wait_send()

    outs = [jax.ShapeDtypeStruct((2,) + a.shape if kind == "gather" else a.shape[1:], a.dtype) for a, kind in items]
    res = pl.pallas_call(
        body, name=name, out_shape=outs, in_specs=[pl.BlockSpec(memory_space=pl.ANY)] * n,
        out_specs=[pl.BlockSpec(memory_space=pltpu.VMEM)] * n,
        scratch_shapes=[pltpu.VMEM(items[k][0].shape[1:], items[k][0].dtype) for k in swaps]
        + [pltpu.SemaphoreType.DMA((ncopy,)), pltpu.SemaphoreType.DMA((ncopy,)), pltpu.SemaphoreType.DMA((ncopy,))],
        compiler_params=pltpu.CompilerParams(has_side_effects=True, vmem_limit_bytes=VMEM_LIMIT_BYTES))(
        *[a for a, _ in items])
    return list(res)


def sum_own(pair, got, name, out_dtype=F32):
    _, r, cdim = pair.shape
    tr = _row_tile(r, 4 * 2 * max(cdim, 128))

    def body(c_ref, p_ref, g_ref, o_ref):
        o_ref[...] = (p_ref[0] + g_ref[...]).astype(o_ref.dtype)

    me = lax.axis_index("c").astype(jnp.int32).reshape(1)
    return pl.pallas_call(
        body, name=name, out_shape=jax.ShapeDtypeStruct((r, cdim), out_dtype),
        grid_spec=pltpu.PrefetchScalarGridSpec(
            num_scalar_prefetch=1, grid=(r // tr,),
            in_specs=[pl.BlockSpec((1, tr, cdim), lambda i, c_ref: (c_ref[0], i, 0)),
                      pl.BlockSpec((tr, cdim), lambda i, c_ref: (i, 0))],
            out_specs=pl.BlockSpec((tr, cdim), lambda i, c_ref: (i, 0))),
        compiler_params=pltpu.CompilerParams(vmem_limit_bytes=VMEM_LIMIT_BYTES))(me, pair, got)


_SMALL = ["c_ctx", "norm_g", "b_mod", "gm_v_g", "gm_w_s", "gm_b_s", "s5_lam_re", "s5_lam_im", "s5_log_dt",
          "s5_b_re", "s5_b_im", "s5_c_re", "s5_c_im", "s5_d", "s5_b_glu", "q_norm_g", "k_norm_g", "final_g"]
_BIG = ["we_in", "we_out", "s5_w_glu", "wo_in", "wo_out"]
_WEIGHTS = ["c_ctx", "norm_g", "w_mod", "b_mod", "we_in", "we_out", "gm_v_g", "gm_w_s", "gm_b_s", "s5_lam_re",
            "s5_lam_im", "s5_log_dt", "s5_b_re", "s5_b_im", "s5_c_re", "s5_c_im", "s5_d", "s5_w_glu", "s5_b_glu",
            "wo_in", "wo_out", "q_norm_g", "k_norm_g", "final_g"]
_SMALL_ALIGN = 8 * 8 * 128


def _rope_tables(n_lat):
    rows = n_lat // GRID_W
    freqs = ROPE_THETA ** (-jnp.arange(HD // 4, dtype=F32) / (HD // 4))
    ar, ac = jnp.arange(rows)[:, None] * freqs, jnp.arange(GRID_W)[:, None] * freqs
    by_row = lambda v: jnp.repeat(v, GRID_W, axis=0)
    by_col = lambda v: jnp.tile(v, (rows, 1))
    cr, sr, cc, sc = by_row(jnp.cos(ar)), by_row(jnp.sin(ar)), by_col(jnp.cos(ac)), by_col(jnp.sin(ac))
    cos = jnp.concatenate([cr, cr, cc, cc], axis=1)
    sins = jnp.concatenate([-sr, sr, -sc, sc], axis=1)
    cos = jnp.concatenate([jnp.ones((NC, HD), F32), cos], axis=0)
    sins = jnp.concatenate([jnp.zeros((NC, HD), F32), sins], axis=0)
    return cos, sins


def _block_diag(v, transpose):
    gpb = SG // NBLK
    v = v.reshape(2, NBLK, gpb, SH, SP)
    eye = jnp.eye(gpb, dtype=v.dtype)
    if transpose:
        return jnp.einsum("djahp,ab->djapbh", v, eye).reshape(2, NBLK, LN, UB)
    return jnp.einsum("djahp,ab->djahbp", v, eye).reshape(2, NBLK, UB, LN)


def _diag_blocks(m):
    gpb = SG // NBLK
    return jnp.einsum("jahap->jahp", m.reshape(NBLK, gpb, SH, gpb, SP)).reshape(SG, SH, SP)


def _view2d(a):
    if a.ndim == 1:
        return a.reshape(1, -1)
    if a.shape[-1] < 64 and a.size % 1024 == 0:
        return a.reshape(-1, 1024)
    return a.reshape(-1, a.shape[-1])


def kernel(x, c, ctx, c_ctx, norm_g, w_mod, b_mod, we_in, we_out, gm_v_g, gm_w_s, gm_b_s, s5_lam_re, s5_lam_im, s5_log_dt, s5_b_re, s5_b_im, s5_c_re, s5_c_im, s5_d, s5_w_glu, s5_b_glu, wo_in, wo_out, q_norm_g, k_norm_g, final_g, loss_target, m_c_ctx, m_norm_g, m_w_mod, m_b_mod, m_we_in, m_we_out, m_gm_v_g, m_gm_w_s, m_gm_b_s, m_s5_lam_re, m_s5_lam_im, m_s5_log_dt, m_s5_b_re, m_s5_b_im, m_s5_c_re, m_s5_c_im, m_s5_d, m_s5_w_glu, m_s5_b_glu, m_wo_in, m_wo_out, m_q_norm_g, m_k_norm_g, m_final_g, v_c_ctx, v_norm_g, v_w_mod, v_b_mod, v_we_in, v_we_out, v_gm_v_g, v_gm_w_s, v_gm_b_s, v_s5_lam_re, v_s5_lam_im, v_s5_log_dt, v_s5_b_re, v_s5_b_im, v_s5_c_re, v_s5_c_im, v_s5_d, v_s5_w_glu, v_s5_b_glu, v_wo_in, v_wo_out, v_q_norm_g, v_k_norm_g, v_final_g):
    weights = dict(c_ctx=c_ctx, norm_g=norm_g, w_mod=w_mod, b_mod=b_mod, we_in=we_in, we_out=we_out, gm_v_g=gm_v_g,
                   gm_w_s=gm_w_s, gm_b_s=gm_b_s, s5_lam_re=s5_lam_re, s5_lam_im=s5_lam_im, s5_log_dt=s5_log_dt,
                   s5_b_re=s5_b_re, s5_b_im=s5_b_im, s5_c_re=s5_c_re, s5_c_im=s5_c_im, s5_d=s5_d, s5_w_glu=s5_w_glu,
                   s5_b_glu=s5_b_glu, wo_in=wo_in, wo_out=wo_out, q_norm_g=q_norm_g, k_norm_g=k_norm_g,
                   final_g=final_g)
    mom_m = dict(c_ctx=m_c_ctx, norm_g=m_norm_g, w_mod=m_w_mod, b_mod=m_b_mod, we_in=m_we_in, we_out=m_we_out,
                 gm_v_g=m_gm_v_g, gm_w_s=m_gm_w_s, gm_b_s=m_gm_b_s, s5_lam_re=m_s5_lam_re, s5_lam_im=m_s5_lam_im,
                 s5_log_dt=m_s5_log_dt, s5_b_re=m_s5_b_re, s5_b_im=m_s5_b_im, s5_c_re=m_s5_c_re, s5_c_im=m_s5_c_im,
                 s5_d=m_s5_d, s5_w_glu=m_s5_w_glu, s5_b_glu=m_s5_b_glu, wo_in=m_wo_in, wo_out=m_wo_out,
                 q_norm_g=m_q_norm_g, k_norm_g=m_k_norm_g, final_g=m_final_g)
    mom_v = dict(c_ctx=v_c_ctx, norm_g=v_norm_g, w_mod=v_w_mod, b_mod=v_b_mod, we_in=v_we_in, we_out=v_we_out,
                 gm_v_g=v_gm_v_g, gm_w_s=v_gm_w_s, gm_b_s=v_gm_b_s, s5_lam_re=v_s5_lam_re, s5_lam_im=v_s5_lam_im,
                 s5_log_dt=v_s5_log_dt, s5_b_re=v_s5_b_re, s5_b_im=v_s5_b_im, s5_c_re=v_s5_c_re, s5_c_im=v_s5_c_im,
                 s5_d=v_s5_d, s5_w_glu=v_s5_w_glu, s5_b_glu=v_s5_b_glu, wo_in=v_wo_in, wo_out=v_wo_out,
                 q_norm_g=v_q_norm_g, k_norm_g=v_k_norm_g, final_g=v_final_g)

    ixy = 2 * lax.axis_index("x") + lax.axis_index("y")
    n_lat = x.shape[1]
    nl = norm_g.shape[0]
    nmod = w_mod.shape[2]
    xin = (ctx.reshape(ctx.shape[1], D), x.reshape(n_lat, D))

    ic = lax.axis_index("c")
    mine = [lax.dynamic_index_in_dim(weights[n], ic, 0, keepdims=False).astype(BF) for n in _BIG]
    got = exchange([(m_, "xy", "gather") for m_ in mine] + [(c, "xy", "gather")], "gather_weights")
    both = d2d([(g_.reshape(-1, g_.shape[-1]), "gather") for g_ in got[:len(_BIG)]], "swap_weights")
    both = [b_.reshape((2,) + g_.shape) for b_, g_ in zip(both, got)]
    wein = [(both[0], l) for l in range(2)]
    weout = [(both[1].reshape(2, 1, D, D), l) for l in range(2)]
    wglu = [(both[2].reshape(2, AW, AW), l) for l in range(2)]
    woin = [(both[3], l) for l in range(2)]
    woout = [(both[4].reshape(2, 1, D, D), l) for l in range(2)]
    c_group = got[len(_BIG)].reshape(4, D)

    cond = jnp.concatenate([c_group, jnp.broadcast_to(c_ctx.reshape(1, D), (4, D))], axis=0)
    b_shard = lax.dynamic_slice(b_mod, (0, ixy * nmod), (nl, nmod)).reshape(nl, 1, nmod)
    mpart = ada_fwd(cond, w_mod, b_shard)
    m_lat, m_ctx = exchange([(jnp.transpose(mpart[:, 0:4], (1, 0, 2)), "xy", "scatter"),
                             (mpart[:, 4], "xy", "gather")], "exchange_mod")
    m_lat = jnp.transpose(m_lat, (1, 0, 2)).reshape(nl, 3, D)
    m_ctx = jnp.transpose(m_ctx, (1, 0, 2)).reshape(nl, 3, D)
    mods = [jnp.stack([m_ctx[l], m_lat[l]], axis=0) for l in range(nl)]

    loss_part, dx, g, d_norm_g, d_mod_lat, d_mod_ctx, d_final_g = _local_step(
        xin, loss_target.reshape(n_lat, D), mods, wein, weout, wglu, woin, woout, weights)
    grad_x = dx.reshape(1, n_lat, D)

    d_mod_lat, d_mod_ctx = jnp.stack(d_mod_lat), jnp.stack(d_mod_ctx)
    dm_send = jnp.stack([d_mod_lat.reshape(nl, 4, nmod), d_mod_ctx.reshape(nl, 4, nmod)])
    (dm_got,) = exchange([(jnp.transpose(dm_send, (2, 0, 1, 3)), "xy", "scatter")], "exchange_dmod")
    dm_rows = jnp.concatenate([dm_got[:, 0], dm_got[:, 1]], axis=0)
    gw_mod, d_cctx = ada_bwd(cond, jnp.transpose(dm_rows, (1, 0, 2)), w_mod)
    g_small = dict(c_ctx=d_cctx.reshape(D), norm_g=jnp.stack(d_norm_g), b_mod=add2(d_mod_lat, d_mod_ctx, "add_dbmod"),
                   final_g=d_final_g.reshape(D))
    for name in _SMALL:
        if name not in g_small:
            g_small[name] = jnp.stack(g[name])

    flat = jnp.concatenate([g_small[n].reshape(-1) for n in _SMALL])
    nflat = flat.shape[0]
    npad = -(-nflat // _SMALL_ALIGN) * _SMALL_ALIGN
    flat = jnp.concatenate([flat, jnp.zeros((npad - nflat,), F32)]).reshape(8, npad // (8 * 128), 128)
    pairs = [gw_mod.reshape(2, nl // 2 * D, nmod)]
    for name in _BIG:
        st = jnp.stack(g[name]) if isinstance(g[name], list) else g[name]
        pairs.append(st.reshape(2, -1, st.shape[-1]))
    got_a = d2d([(pairs[k], "swap") for k in (1, 2, 3)], "reduce_chip_a")
    got_b = d2d([(pairs[k], "swap") for k in (0, 4, 5)], "reduce_chip_b")
    theirs = [got_b[0]] + got_a + got_b[1:]
    chip = [sum_own(pairs[k], theirs[k], f"sum_chip{k}", F32 if k == 0 else BF) for k in range(len(pairs))]
    parts = exchange([(flat, "all", "scatter")]
                     + [(s_.reshape(4, s_.shape[0] // 4, s_.shape[1]), "xy", "scatter") for s_ in chip[1:]],
                     "reduce_scatter")
    sums = [sum_lead(pt, f"sum_shard{k}") for k, pt in enumerate(parts)]
    full = d2d([(sums[0], "gather"), (chip[0], "gather")] + [(s_, "gather") for s_ in sums[1:]], "all_gather")
    (flat_full,) = exchange([(full[0], "xy", "gather")], "gather_small")
    full = [flat_full] + full[1:]
    flat = full[0].reshape(-1)
    grads = {}
    off = 0
    for name in _SMALL:
        sz = weights[name].size
        grads[name] = flat[off:off + sz].reshape(weights[name].shape)
        off += sz
    grads["w_mod"] = full[1].reshape(w_mod.shape)
    for k, name in enumerate(_BIG):
        grads[name] = full[2 + k].reshape(weights[name].shape)

    delta, new_m, new_v = {}, {}, {}
    views = [_view2d(weights[n]) for n in _SMALL]
    ds, nms, nvs = adamw_many(views, [grads[n].reshape(w2.shape) for n, w2 in zip(_SMALL, views)],
                              [mom_m[n].reshape(w2.shape) for n, w2 in zip(_SMALL, views)],
                              [mom_v[n].reshape(w2.shape) for n, w2 in zip(_SMALL, views)], "adamw_small")
    for n, d2, m2, v2 in zip(_SMALL, ds, nms, nvs):
        shp = weights[n].shape
        delta[n], new_m[n], new_v[n] = d2.reshape(shp), m2.reshape(shp), v2.reshape(shp)
    for name in ["w_mod"] + _BIG:
        w2 = _view2d(weights[name])
        d2, m2, v2 = adamw(w2, grads[name].reshape(w2.shape), mom_m[name].reshape(w2.shape),
                           mom_v[name].reshape(w2.shape), f"adamw_{name}")
        shp = weights[name].shape
        delta[name], new_m[name], new_v[name] = d2.reshape(shp), m2.reshape(shp), v2.reshape(shp)

    loss = lax.psum(loss_part[0, 0], ("x", "y", "c"))
    return (loss, grad_x, *[grads[n] for n in _WEIGHTS], *[delta[n] for n in _WEIGHTS],
            *[new_m[n] for n in _WEIGHTS], *[new_v[n] for n in _WEIGHTS])


def _local_step(xin, target, mods, wein, weout, wglu, woin, woout, w):
    norm_g, gm_v_g, gm_w_s, gm_b_s = w["norm_g"], w["gm_v_g"], w["gm_w_s"], w["gm_b_s"]
    s5_lam_re, s5_lam_im, s5_log_dt = w["s5_lam_re"], w["s5_lam_im"], w["s5_log_dt"]
    s5_b_re, s5_b_im, s5_c_re, s5_c_im = w["s5_b_re"], w["s5_b_im"], w["s5_c_re"], w["s5_c_im"]
    s5_d, s5_b_glu, q_norm_g, k_norm_g, final_g = w["s5_d"], w["s5_b_glu"], w["q_norm_g"], w["k_norm_g"], w["final_g"]
    nl = norm_g.shape[0]
    n_lat = xin[1].shape[0]

    cos, sins = _rope_tables(n_lat)

    s5p = []
    for i in range(2):
        lam_l = (s5_lam_re[i].reshape(2, SW), s5_lam_im[i].reshape(2, SW),
                 jnp.repeat(s5_log_dt[i], SP, axis=1))
        lam_r = (jnp.repeat(s5_lam_re[i].reshape(2 * SG, SP), SH, axis=0),
                 jnp.repeat(s5_lam_im[i].reshape(2 * SG, SP), SH, axis=0),
                 jnp.repeat(s5_log_dt[i].reshape(2 * SG, 1), SH, axis=0))
        b_r = (jnp.transpose(s5_b_re[i], (0, 1, 3, 2)).reshape(2 * SG * SH, SP),
               jnp.transpose(s5_b_im[i], (0, 1, 3, 2)).reshape(2 * SG * SH, SP))
        pw_re, pw_im, bbr, bbi = s5_disc(*lam_l, *lam_r, *b_r)
        s5p.append(dict(
            lam_r=lam_r, b_r=b_r, pw_re=pw_re, pw_im=pw_im, pw_im_conj=-pw_im,
            bb_re=_block_diag(bbr.reshape(2, SG, SH, SP), False).astype(BF),
            bb_im=_block_diag(bbi.reshape(2, SG, SH, SP), False).astype(BF),
            ct_re=_block_diag(s5_c_re[i], True).astype(BF), ct_im=_block_diag(s5_c_im[i], True).astype(BF)))

    saved = []
    xcur = xin
    h = pro_fwd(xin[0], xin[1], norm_g[0].reshape(1, D), mods[0], "pro_fwd0")
    for l in range(nl):
        i = l // 2
        sv = dict(x=xcur, h=h)
        if l % 2 == 0:
            p = mm_nn(h, wein[i], f"in_proj{l}")
            sp = s5p[i]
            for dr, rev in ((0, False), (1, True)):
                sv[f"y{dr}"], sv[f"hpr{dr}"], sv[f"hpi{dr}"] = s5q_fwd(
                    p, sp["bb_re"], sp["bb_im"], sp["ct_re"], sp["ct_im"], sp["pw_re"], sp["pw_im"], dr, rev,
                    f"s5_fwd{l}_{dr}")
            mix = mix_fwd(p, sv["y0"], sv["y1"], gm_v_g[i].reshape(1, AW), gm_w_s[i].astype(BF),
                          gm_b_s[i].reshape(NGRP, CHUNK, 1), s5_d[i].reshape(1, AW), wglu[i],
                          s5_b_glu[i].reshape(1, AW), f"mix_fwd{l}")
            o = mm_nn(mix, weout[i], f"out_proj{l}")
        else:
            p = mm_nn(h, woin[i], f"in_proj{l}")
            sv["qkv"] = attn_prep(p, q_norm_g[i].reshape(1, HD), k_norm_g[i].reshape(1, HD), cos, sins, f"attn_prep{l}")
            sv["o_att"], mix, sv["lse"] = attn_fwd(sv["qkv"], p, f"attn_fwd{l}")
            o = mm_nn(mix, woout[i], f"out_proj{l}")
        sv.update(p=p, mix=mix, o=o)
        saved.append(sv)
        if l < nl - 1:
            xcur, h = res_pro_fwd(xcur, o, mods[l], True, norm_g[l + 1].reshape(1, D), mods[l + 1], f"res_pro_fwd{l}")
        else:
            xcur = res_fwd(xcur, o, mods[l], False, f"res_fwd{l}")

    loss_part, dx, d_final_g, do, dgt = final_loss(xcur, target, final_g.reshape(1, D), saved[-1]["o"], mods[-1])

    g = {}
    gbuf = {}
    d_norm_g, d_mod_lat, d_mod_ctx = [None] * nl, [None] * nl, [None] * nl
    for name in ("s5_w_glu", "gm_v_g", "gm_w_s", "gm_b_s", "s5_lam_re", "s5_lam_im",
                 "s5_log_dt", "s5_b_re", "s5_b_im", "s5_c_re", "s5_c_im", "s5_d", "s5_b_glu", "q_norm_g", "k_norm_g"):
        g[name] = [None, None]
    for l in reversed(range(nl)):
        i = l // 2
        sv = saved[l]
        w_out = weout[i] if l % 2 == 0 else woout[i]
        dmix = mm_nt(do, w_out, f"out_dgrad{l}")
        out_name, in_name = ("we_out", "we_in") if l % 2 == 0 else ("wo_out", "wo_in")
        gbuf[out_name] = mm_tn(sv["mix"], do, 1, f"out_wgrad{l}", slot=i, into=gbuf.get(out_name))
        if l % 2 == 0:
            sp = s5p[i]
            (dp, dy, g["gm_w_s"][i], dbs, dvg, dd, g["s5_w_glu"][i], dbg) = mix_bwd(
                sv["p"], sv["y0"], sv["y1"], dmix, gm_v_g[i].reshape(1, AW), gm_w_s[i].astype(BF),
                gm_b_s[i].reshape(NGRP, CHUNK, 1), s5_d[i].reshape(1, AW), wglu[i], s5_b_glu[i].reshape(1, AW),
                f"mix_bwd{l}")
            g["gm_b_s"][i], g["gm_v_g"][i] = dbs.reshape(NGRP, CHUNK), dvg.reshape(AW)
            g["s5_d"][i], g["s5_b_glu"][i] = dd.reshape(AW), dbg.reshape(AW)
            g["s5_w_glu"][i] = g["s5_w_glu"][i].reshape(4, AW // 4, AW)
            dxd, das_r, das_i, dbbs_r, dbbs_i, dcs_r, dcs_i = [], [], [], [], [], [], []
            for dr, rev in ((0, False), (1, True)):
                dxs_d, da_r, da_i, dbb_r, dbb_i, dc_r, dc_i = s5q_bwd(
                    sv["p"], sv[f"hpr{dr}"], sv[f"hpi{dr}"], dy, sp["bb_re"], sp["bb_im"], sp["ct_re"], sp["ct_im"],
                    sp["pw_re"], sp["pw_im_conj"], dr, rev, f"s5_bwd{l}_{dr}")
                dxd.append(dxs_d)
                das_r.append(jnp.repeat(da_r.reshape(SG, SP), SH, axis=0))
                das_i.append(jnp.repeat(da_i.reshape(SG, SP), SH, axis=0))
                dbbs_r.append(_diag_blocks(dbb_r).reshape(SG * SH, SP))
                dbbs_i.append(_diag_blocks(dbb_i).reshape(SG * SH, SP))
                dcs_r.append(_diag_blocks(dc_r))
                dcs_i.append(_diag_blocks(dc_i))
            cat = lambda parts: jnp.concatenate(parts, axis=0)
            dlr, dli, dldt, dbr, dbi = s5_param_bwd(*sp["lam_r"], *sp["b_r"], cat(das_r), cat(das_i),
                                                    cat(dbbs_r), cat(dbbs_i))
            g["s5_lam_re"][i], g["s5_lam_im"][i] = dlr.reshape(2, SG, SP), dli.reshape(2, SG, SP)
            g["s5_log_dt"][i] = dldt.reshape(2, SG)
            g["s5_b_re"][i] = jnp.transpose(dbr.reshape(2, SG, SH, SP), (0, 1, 3, 2))
            g["s5_b_im"][i] = jnp.transpose(dbi.reshape(2, SG, SH, SP), (0, 1, 3, 2))
            g["s5_c_re"][i], g["s5_c_im"][i] = jnp.stack(dcs_r), jnp.stack(dcs_i)
            dp = s5p_dx_sum(dy, s5_d[i].reshape(1, AW), dxd[0], dxd[1], dp, f"s5_dx_sum{l}")
            w_in = wein[i]
        else:
            dq, dp, dk, dv = attn_bwd(sv["qkv"], sv["p"], dmix, sv["o_att"], sv["lse"], f"attn_bwd{l}")
            dp, dqg, dkg = attn_prep_bwd(sv["p"], dq, dk, dv, q_norm_g[i].reshape(1, HD),
                                         k_norm_g[i].reshape(1, HD), cos, sins, dp, f"attn_prep_bwd{l}")
            g["q_norm_g"][i], g["k_norm_g"][i] = dqg.reshape(HD), dkg.reshape(HD)
            w_in = woin[i]
        dh = mm_nt(dp, w_in, f"in_dgrad{l}")
        gbuf[in_name] = mm_tn(sv["h"], dp, 4, f"in_wgrad{l}", slot=i, into=gbuf.get(in_name))
        dgt_l = dgt
        if l > 0:
            dx, dmod2, dng, do, dgt = pro_res_bwd(sv["x"], dh, dx, norm_g[l].reshape(1, D), mods[l],
                                                  saved[l - 1]["o"], mods[l - 1], f"pro_res_bwd{l}")
        else:
            dx, dmod2, dng = pro_bwd(xin[0], xin[1], dh, dx, norm_g[l].reshape(1, D), mods[l], f"pro_bwd{l}")
        d_norm_g[l] = dng.reshape(D)
        d_mod_ctx[l] = jnp.concatenate([dmod2[0, 0], dmod2[0, 1], dgt_l[0]])
        d_mod_lat[l] = jnp.concatenate([dmod2[1, 0], dmod2[1, 1], dgt_l[1]])
    g.update(gbuf)
    return loss_part, dx, g, d_norm_g, d_mod_lat, d_mod_ctx, d_final_g
```

```python
import math

import numpy as np
import jax
import jax.numpy as jnp
from jax import lax
from jax.experimental import pallas as pl
from jax.experimental.pallas import tpu as pltpu

F32 = jnp.float32
BF = jnp.bfloat16
MESH = pl.DeviceIdType.MESH

D = 1024
NC = 256
SEQ = 4096
GRID_W = 64
TM = 256
CHUNK = 128
EPS = 1e-6
HD = 128
NQ = 8
NKV = 2
ROPE_THETA = 10000.0
SG = 32
SP = 64
SH = 16
SW = SG * SP
GELU_K = math.sqrt(2.0 / math.pi)
GELU_C = 0.044715
VMEM_LIMIT_BYTES = 56 * 1024 * 1024

ADAM_LR = 0.001
ADAM_B1 = 0.9
ADAM_B2 = 0.999
ADAM_EPS = 1e-08
ADAM_WD = 0.01
ADAM_STEP = 10


def _call(body, *, name, out_shape, grid=None, in_specs=None, out_specs=None, scratch=()):
    kw = {}
    if grid is not None:
        kw["grid"] = grid
    if in_specs is not None:
        kw["in_specs"] = in_specs
    if out_specs is not None:
        kw["out_specs"] = out_specs
    return pl.pallas_call(
        body, name=name, out_shape=out_shape, scratch_shapes=list(scratch),
        compiler_params=pltpu.CompilerParams(vmem_limit_bytes=VMEM_LIMIT_BYTES), **kw)


def _dot(a, b, ca=1, cb=0):
    return lax.dot_general(a, b, (((ca,), (cb,)), ((), ())), preferred_element_type=F32)


def _sig(x):
    return 1.0 / (1.0 + jnp.exp(-x))


def _full(shape):
    n = len(shape)
    return pl.BlockSpec(shape, lambda *_: (0,) * n)


def _mm_rows(t, most=1088):
    for rows in (2176, 1088, 1024, 768, 544, 512, 256):
        if rows <= most and t % rows == 0:
            return rows
    raise ValueError(t)


def _layer_of(w):
    return w if isinstance(w, tuple) else (w[None], 0)


def mm_nn(a, w, name, out_dtype=F32):
    w4, layer = _layer_of(w)
    t, k = a.shape
    _, j, _, nb = w4.shape
    tr = _mm_rows(t, 2176)

    def body(a_ref, w_ref, o_ref):
        o_ref[...] = _dot(a_ref[...], w_ref[0, 0]).astype(o_ref.dtype)

    return _call(body, name=name, grid=(j, t // tr),
                 in_specs=[pl.BlockSpec((tr, k), lambda jj, i: (i, 0)),
                           pl.BlockSpec((1, 1, k, nb), lambda jj, i: (layer, jj, 0, 0))],
                 out_specs=pl.BlockSpec((tr, nb), lambda jj, i: (i, jj)),
                 out_shape=jax.ShapeDtypeStruct((t, j * nb), out_dtype))(a, w4)


def mm_nt(a, w, name, out_dtype=F32):
    w4, layer = _layer_of(w)
    t, _ = a.shape
    _, j, k, nb = w4.shape
    tr = _mm_rows(t, 2176 if j * nb <= 1024 else 1088)

    def body(a_ref, w_ref, o_ref):
        acc = _dot(a_ref[:, 0:nb], w_ref[0, 0], 1, 1)
        for jj in range(1, j):
            acc = acc + _dot(a_ref[:, jj * nb:(jj + 1) * nb], w_ref[0, jj], 1, 1)
        o_ref[...] = acc.astype(o_ref.dtype)

    return _call(body, name=name, grid=(t // tr,),
                 in_specs=[pl.BlockSpec((tr, j * nb), lambda i: (i, 0)),
                           pl.BlockSpec((1, j, k, nb), lambda i: (layer, 0, 0, 0))],
                 out_specs=pl.BlockSpec((tr, k), lambda i: (i, 0)),
                 out_shape=jax.ShapeDtypeStruct((t, k), out_dtype))(a, w4)


def mm_tn(a, b, j, name, slot=0, into=None):
    t, m = a.shape
    nb = b.shape[1] // j
    tr = _mm_rows(t, 2176)

    def body(a_ref, b_ref, *rest):
        o_ref = rest[-1]

        @pl.when(pl.program_id(1) == 0)
        def _():
            o_ref[...] = jnp.zeros_like(o_ref)
        o_ref[0, 0] += _dot(a_ref[...], b_ref[...], 0, 0)

    in_specs = [pl.BlockSpec((tr, m), lambda jj, i: (i, 0)), pl.BlockSpec((tr, nb), lambda jj, i: (i, jj))]
    args = [a, b]
    alias = {}
    if into is not None:
        in_specs.append(pl.BlockSpec(memory_space=pl.ANY))
        args.append(into)
        alias = {2: 0}
    return pl.pallas_call(
        body, name=name, grid=(j, t // tr), in_specs=in_specs,
        out_specs=pl.BlockSpec((1, 1, m, nb), lambda jj, i: (slot, jj, 0, 0)),
        out_shape=jax.ShapeDtypeStruct((2, j, m, nb), F32), input_output_aliases=alias,
        compiler_params=pltpu.CompilerParams(vmem_limit_bytes=VMEM_LIMIT_BYTES))(*args)


def _mod_rows(mod_ref, i):
    ctx = i == 0
    sh = jnp.where(ctx, mod_ref[0, 0:1, :], mod_ref[1, 0:1, :])
    sc = jnp.where(ctx, mod_ref[0, 1:2, :], mod_ref[1, 1:2, :])
    gt = jnp.where(ctx, mod_ref[0, 2:3, :], mod_ref[1, 2:3, :])
    return sh, sc, gt


def _split_specs():
    return [pl.BlockSpec((TM, D), lambda i: (0, 0)), pl.BlockSpec((TM, D), lambda i: (jnp.maximum(i - 1, 0), 0))]


def _split_tile(c_ref, l_ref, i):
    return jnp.where(i == 0, c_ref[...], l_ref[...])


def pro_fwd(ctx, lat, g, mod, name):
    t = ctx.shape[0] + lat.shape[0]

    def body(c_ref, l_ref, g_ref, mod_ref, h_ref):
        i = pl.program_id(0)
        sh, sc, _ = _mod_rows(mod_ref, i)
        xv = _split_tile(c_ref, l_ref, i)
        r = lax.rsqrt(jnp.mean(xv * xv, axis=-1, keepdims=True) + EPS)
        h_ref[...] = ((xv * r) * g_ref[...] * (1.0 + sc) + sh).astype(BF)

    return _call(body, name=name, grid=(t // TM,),
                 in_specs=_split_specs() + [_full((1, D)), _full((2, 3, D))],
                 out_specs=pl.BlockSpec((TM, D), lambda i: (i, 0)),
                 out_shape=jax.ShapeDtypeStruct((t, D), BF))(ctx, lat, g, mod)


def pro_bwd(ctx, lat, dh, dxn, g, mod, name):
    t = ctx.shape[0] + lat.shape[0]

    def body(c_ref, l_ref, dh_ref, dxn_ref, g_ref, mod_ref, dx_ref, dmod_ref, dg_ref):
        i = pl.program_id(0)

        @pl.when(i == 0)
        def _():
            dmod_ref[...] = jnp.zeros_like(dmod_ref)
            dg_ref[...] = jnp.zeros_like(dg_ref)

        _, sc, _ = _mod_rows(mod_ref, i)
        xv = _split_tile(c_ref, l_ref, i)
        gv = g_ref[...]
        r = lax.rsqrt(jnp.mean(xv * xv, axis=-1, keepdims=True) + EPS)
        xn = xv * r
        dh_v = dh_ref[...]
        e = dh_v * (1.0 + sc)
        dsh = jnp.sum(dh_v, axis=0, keepdims=True)
        dsc = jnp.sum(dh_v * xn * gv, axis=0, keepdims=True)
        dg_ref[...] += jnp.sum(e * xn, axis=0, keepdims=True)
        dxh = e * gv

        @pl.when(i == 0)
        def _():
            dmod_ref[0, 0:1, :] += dsh
            dmod_ref[0, 1:2, :] += dsc

        @pl.when(i > 0)
        def _():
            dx_ref[...] = dxn_ref[...] + r * (dxh - xn * jnp.mean(dxh * xn, axis=-1, keepdims=True))
            dmod_ref[1, 0:1, :] += dsh
            dmod_ref[1, 1:2, :] += dsc

    tile = pl.BlockSpec((TM, D), lambda i: (i, 0))
    return _call(body, name=name, grid=(t // TM,),
                 in_specs=_split_specs() + [tile, tile, _full((1, D)), _full((2, 3, D))],
                 out_specs=[_split_specs()[1], _full((2, 2, D)), _full((1, D))],
                 out_shape=[jax.ShapeDtypeStruct(lat.shape, F32), jax.ShapeDtypeStruct((2, 2, D), F32),
                            jax.ShapeDtypeStruct((1, D), F32)])(ctx, lat, dh, dxn, g, mod)


def _mod_rows_at(mod_ref, i, tr):
    isctx = i * tr + lax.broadcasted_iota(jnp.int32, (tr, 1), 0) < NC
    pick = lambda k: jnp.where(isctx, mod_ref[0, k:k + 1, :], mod_ref[1, k:k + 1, :])
    return pick(0), pick(1), pick(2), isctx


def res_fwd(x, o, mod, update_ctx, name):
    t = x.shape[0]
    tr = _mm_rows(t)

    def body(x_ref, o_ref, mod_ref, y_ref):
        _, _, gt, isctx = _mod_rows_at(mod_ref, pl.program_id(0), tr)
        upd = x_ref[...] + gt * o_ref[...]
        y_ref[...] = upd if update_ctx else jnp.where(isctx, x_ref[...], upd)

    tile = pl.BlockSpec((tr, D), lambda i: (i, 0))
    return _call(body, name=name, grid=(t // tr,), in_specs=[tile, tile, _full((2, 3, D))],
                 out_specs=tile, out_shape=jax.ShapeDtypeStruct((t, D), F32))(x, o, mod)


def res_pro_fwd(x, o, mod, update_ctx, g_next, mod_next, name):
    split = isinstance(x, tuple)
    xs = list(x) if split else [x]
    t = o.shape[0]
    tr = TM if split else _mm_rows(t)

    def body(*refs):
        x_refs = refs[:len(xs)]
        o_ref, mod_ref, g_ref, modn_ref, y_ref, h_ref = refs[len(xs):]
        i = pl.program_id(0)
        _, _, gt, isctx = _mod_rows_at(mod_ref, i, tr)
        xv = _split_tile(x_refs[0], x_refs[1], i) if split else x_refs[0][...]
        xn = xv + gt * o_ref[...]
        if not update_ctx:
            xn = jnp.where(isctx, xv, xn)
        y_ref[...] = xn
        sh, sc, _, _ = _mod_rows_at(modn_ref, i, tr)
        r = lax.rsqrt(jnp.mean(xn * xn, axis=-1, keepdims=True) + EPS)
        h_ref[...] = ((xn * r) * g_ref[...] * (1.0 + sc) + sh).astype(BF)

    tile = pl.BlockSpec((tr, D), lambda i: (i, 0))
    return _call(body, name=name, grid=(t // tr,),
                 in_specs=(_split_specs() if split else [tile]) + [tile, _full((2, 3, D)), _full((1, D)), _full((2, 3, D))],
                 out_specs=[tile, tile],
                 out_shape=[jax.ShapeDtypeStruct((t, D), F32), jax.ShapeDtypeStruct((t, D), BF)])(
        *xs, o, mod, g_next, mod_next)


def pro_res_bwd(x, dh, dxn, g, mod, o_prev, mod_prev, name):
    t = x.shape[0]
    tr = _mm_rows(t, 544)

    def body(x_ref, dh_ref, dxn_ref, g_ref, mod_ref, o_ref, modp_ref, dx_ref, dmod_ref, dg_ref, do_ref, dgt_ref):
        i = pl.program_id(0)

        @pl.when(i == 0)
        def _():
            dmod_ref[...] = jnp.zeros_like(dmod_ref)
            dg_ref[...] = jnp.zeros_like(dg_ref)
            dgt_ref[...] = jnp.zeros_like(dgt_ref)

        _, sc, _, isctx = _mod_rows_at(mod_ref, i, tr)
        xv = x_ref[...]
        gv = g_ref[...]
        r = lax.rsqrt(jnp.mean(xv * xv, axis=-1, keepdims=True) + EPS)
        xn = xv * r
        dh_v = dh_ref[...]
        e = dh_v * (1.0 + sc)
        dsc_rows = dh_v * xn * gv
        dg_ref[...] += jnp.sum(e * xn, axis=0, keepdims=True)
        dxh = e * gv
        dx = dxn_ref[...] + r * (dxh - xn * jnp.mean(dxh * xn, axis=-1, keepdims=True))
        dx_ref[...] = dx
        _, _, gtp, _ = _mod_rows_at(modp_ref, i, tr)
        do_ref[...] = (gtp * dx).astype(BF)
        dgt_rows = dx * o_ref[...]
        dmod_ref[1, 0:1, :] += jnp.sum(dh_v, axis=0, keepdims=True)
        dmod_ref[1, 1:2, :] += jnp.sum(dsc_rows, axis=0, keepdims=True)
        dgt_ref[1:2, :] += jnp.sum(dgt_rows, axis=0, keepdims=True)

        @pl.when(i * tr < NC)
        def _():
            for ref, rows in ((dmod_ref.at[:, 0], dh_v), (dmod_ref.at[:, 1], dsc_rows), (dgt_ref, dgt_rows)):
                part = jnp.sum(jnp.where(isctx, rows, 0.0), axis=0, keepdims=True)
                ref[0:1, :] += part
                ref[1:2, :] += -part

    tile = pl.BlockSpec((tr, D), lambda i: (i, 0))
    return _call(body, name=name, grid=(t // tr,),
                 in_specs=[tile, tile, tile, _full((1, D)), _full((2, 3, D)), tile, _full((2, 3, D))],
                 out_specs=[tile, _full((2, 2, D)), _full((1, D)), tile, _full((2, D))],
                 out_shape=[jax.ShapeDtypeStruct((t, D), F32), jax.ShapeDtypeStruct((2, 2, D), F32),
                            jax.ShapeDtypeStruct((1, D), F32), jax.ShapeDtypeStruct((t, D), BF),
                            jax.ShapeDtypeStruct((2, D), F32)])(x, dh, dxn, g, mod, o_prev, mod_prev)


def final_loss(x, target, g, o_last, mod_last):
    t = x.shape[0]

    def body(x_ref, t_ref, g_ref, o_ref, modp_ref, loss_ref, dx_ref, dg_ref, do_ref, dgt_ref):
        i = pl.program_id(0)

        @pl.when(i == 0)
        def _():
            loss_ref[...] = jnp.zeros_like(loss_ref)
            dg_ref[...] = jnp.zeros_like(dg_ref)
            dx_ref[...] = jnp.zeros_like(dx_ref)
            do_ref[...] = jnp.zeros_like(do_ref)
            dgt_ref[...] = jnp.zeros_like(dgt_ref)

        @pl.when(i > 0)
        def _():
            xv = x_ref[...]
            gv = g_ref[...]
            r = lax.rsqrt(jnp.mean(xv * xv, axis=-1, keepdims=True) + EPS)
            xn = xv * r
            err = xn * gv - t_ref[...]
            loss_ref[...] += (0.5 / D) * jnp.sum(jnp.sum(err * err, axis=1, keepdims=True), axis=0, keepdims=True)
            dy = err * (1.0 / D)
            dg_ref[...] += jnp.sum(dy * xn, axis=0, keepdims=True)
            dxh = dy * gv
            dx = r * (dxh - xn * jnp.mean(dxh * xn, axis=-1, keepdims=True))
            dx_ref[...] = dx
            do_ref[...] = (modp_ref[1, 2:3, :] * dx).astype(BF)
            dgt_ref[1:2, :] += jnp.sum(dx * o_ref[...], axis=0, keepdims=True)

    tile = pl.BlockSpec((TM, D), lambda i: (i, 0))
    return _call(body, name="final_loss", grid=(t // TM,),
                 in_specs=[tile, pl.BlockSpec((TM, D), lambda i: (jnp.maximum(i - 1, 0), 0)), _full((1, D)), tile,
                           _full((2, 3, D))],
                 out_specs=[_full((1, 1)), tile, _full((1, D)), tile, _full((2, D))],
                 out_shape=[jax.ShapeDtypeStruct((1, 1), F32), jax.ShapeDtypeStruct((t, D), F32),
                            jax.ShapeDtypeStruct((1, D), F32), jax.ShapeDtypeStruct((t, D), BF),
                            jax.ShapeDtypeStruct((2, D), F32)])(x, target, g, o_last, mod_last)


def ada_fwd(cond, w_mod, b_mod):
    nl, _, nw = w_mod.shape

    def body(c_ref, w_ref, b_ref, o_ref):
        cv = c_ref[...]
        s = (cv * _sig(cv)).astype(BF)
        o_ref[0] = _dot(s, w_ref[0].astype(BF)) + b_ref[0]

    return _call(body, name="ada_fwd", grid=(nl,),
                 in_specs=[_full((8, D)), pl.BlockSpec((1, D, nw), lambda l: (l, 0, 0)),
                           pl.BlockSpec((1, 1, nw), lambda l: (l, 0, 0))],
                 out_specs=pl.BlockSpec((1, 8, nw), lambda l: (l, 0, 0)),
                 out_shape=jax.ShapeDtypeStruct((nl, 8, nw), F32))(cond, w_mod, b_mod)


def ada_bwd(cond, dm, w_mod):
    nl, _, nw = w_mod.shape

    def body(c_ref, dm_ref, w_ref, gw_ref, dcc_ref, dc_ref):
        l = pl.program_id(0)

        @pl.when(l == 0)
        def _():
            dc_ref[...] = jnp.zeros_like(dc_ref)

        cv = c_ref[...]
        sg = _sig(cv)
        s = (cv * sg).astype(BF)
        dmv = dm_ref[0].astype(BF)
        gw_ref[0] = _dot(s, dmv, 0, 0)
        dc_ref[...] += _dot(dmv, w_ref[0].astype(BF), 1, 1)

        @pl.when(l == nl - 1)
        def _():
            dcond = dc_ref[...] * (sg * (1.0 + cv * (1.0 - sg)))
            dcc_ref[...] = jnp.sum(dcond[4:8], axis=0, keepdims=True)

    return _call(body, name="ada_bwd", grid=(nl,),
                 in_specs=[_full((8, D)), pl.BlockSpec((1, 8, nw), lambda l: (l, 0, 0)),
                           pl.BlockSpec((1, D, nw), lambda l: (l, 0, 0))],
                 out_specs=[pl.BlockSpec((1, D, nw), lambda l: (l, 0, 0)), _full((1, D))],
                 out_shape=[jax.ShapeDtypeStruct((nl, D, nw), F32), jax.ShapeDtypeStruct((1, D), F32)],
                 scratch=[pltpu.VMEM((8, D), F32)])(cond, dm, w_mod)


def add2(a, b, name):
    def body(a_ref, b_ref, o_ref):
        o_ref[...] = a_ref[...] + b_ref[...]

    return _call(body, name=name, out_shape=jax.ShapeDtypeStruct(a.shape, a.dtype))(a, b)


AW = 512
NGRP = 4


def Y4_SPEC():
    return pl.BlockSpec((AW // 128, TM, 128), lambda i: (0, i, 0))


def _cat_lanes(ref):
    return jnp.concatenate([ref[q] for q in range(ref.shape[0])], axis=1)


def _gelu(y):
    t = jnp.tanh(GELU_K * (y + GELU_C * y * y * y))
    return 0.5 * y * (1.0 + t), t


def _layer_norm_stats(v):
    mu = jnp.mean(v, axis=-1, keepdims=True)
    vc = v - mu
    rstd = lax.rsqrt(jnp.mean(vc * vc, axis=-1, keepdims=True) + EPS)
    return vc * rstd, rstd


def _spatial_mix(vn_ref, ws_ref, bs_ref, mixed_ref):
    for ch in range(TM // CHUNK):
        rows = slice(ch * CHUNK, (ch + 1) * CHUNK)
        for g in range(NGRP):
            cols = slice(g * CHUNK, (g + 1) * CHUNK)
            mixed_ref[rows, cols] = _dot(ws_ref[g], vn_ref[rows, cols]) + bs_ref[g]


def mix_fwd(p, yf, yb, vg, ws, bs, dsk, wglu, bglu, name):
    t = p.shape[0]
    wglu, glu_layer = _layer_of(wglu)

    def body(p_ref, yf_ref, yb_ref, vg_ref, ws_ref, bs_ref, d_ref, wg_ref, bg_ref, o_ref, vn_ref, mixed_ref):
        vhat, _ = _layer_norm_stats(p_ref[:, AW:2 * AW])
        vn_ref[...] = (vhat * vg_ref[...]).astype(BF)
        _spatial_mix(vn_ref, ws_ref, bs_ref, mixed_ref)
        ga = p_ref[:, 2 * AW:3 * AW]
        o_ref[:, 0:AW] = (p_ref[:, 0:AW] * mixed_ref[...] * (ga * _sig(ga))).astype(BF)
        y = _cat_lanes(yf_ref) + _cat_lanes(yb_ref) + d_ref[...] * p_ref[:, 3 * AW:4 * AW]
        y2, _ = _gelu(y)
        z = _dot(y2.astype(BF), wg_ref[0]) + bg_ref[...]
        gb = p_ref[:, 4 * AW:5 * AW]
        o_ref[:, AW:2 * AW] = (y2 * _sig(z) * (gb * _sig(gb))).astype(BF)

    tile = lambda w: pl.BlockSpec((TM, w), lambda i: (i, 0))
    return _call(body, name=name, grid=(t // TM,),
                 in_specs=[tile(5 * AW), Y4_SPEC(), Y4_SPEC(), _full((1, AW)), _full((NGRP, CHUNK, CHUNK)),
                           _full((NGRP, CHUNK, 1)), _full((1, AW)),
                           pl.BlockSpec((1, AW, AW), lambda i: (glu_layer, 0, 0)), _full((1, AW))],
                 out_specs=tile(2 * AW), out_shape=jax.ShapeDtypeStruct((t, 2 * AW), BF),
                 scratch=[pltpu.VMEM((TM, AW), BF), pltpu.VMEM((TM, AW), F32)])(p, yf, yb, vg, ws, bs, dsk, wglu, bglu)


def mix_bwd(p, yf, yb, dmix, vg, ws, bs, dsk, wglu, bglu, name):
    t = p.shape[0]
    wglu, glu_layer = _layer_of(wglu)

    def body(p_ref, yf_ref, yb_ref, dm_ref, vg_ref, ws_ref, bs_ref, d_ref, wg_ref, bg_ref,
             dpa_ref, dy_ref, dws_ref, dbs_ref, dvg_ref, dd_ref, dwg_ref, dbg_ref,
             vn_ref, mixed_ref, dmx_ref, dvn_ref):
        @pl.when(pl.program_id(0) == 0)
        def _():
            for r in (dws_ref, dbs_ref, dvg_ref, dd_ref, dwg_ref, dbg_ref):
                r[...] = jnp.zeros_like(r)

        vhat, rstd = _layer_norm_stats(p_ref[:, AW:2 * AW])
        vgv = vg_ref[...]
        vn_ref[...] = (vhat * vgv).astype(BF)
        _spatial_mix(vn_ref, ws_ref, bs_ref, mixed_ref)
        u = p_ref[:, 0:AW]
        ga = p_ref[:, 2 * AW:3 * AW]
        sga = _sig(ga)
        dya = dm_ref[:, 0:AW]
        mixed = mixed_ref[...]
        dpa_ref[:, 0:AW] = (dya * mixed * (ga * sga)).astype(BF)
        dpa_ref[:, 2 * AW:3 * AW] = (dya * u * mixed * (sga * (1.0 + ga * (1.0 - sga)))).astype(BF)
        dmx_ref[...] = dya * u * (ga * sga)
        for ch in range(TM // CHUNK):
            rows = slice(ch * CHUNK, (ch + 1) * CHUNK)
            for g in range(NGRP):
                cols = slice(g * CHUNK, (g + 1) * CHUNK)
                dmx = dmx_ref[rows, cols]
                dmxb = dmx.astype(BF)
                dws_ref[g] += _dot(dmxb, vn_ref[rows, cols], 1, 1)
                dbs_ref[g] += jnp.sum(dmx, axis=1, keepdims=True)
                dvn_ref[rows, cols] = _dot(ws_ref[g], dmxb, 0, 0)
        dvn = dvn_ref[...]
        dvg_ref[...] += jnp.sum(dvn * vhat, axis=0, keepdims=True)
        dvh = dvn * vgv
        dpa_ref[:, AW:2 * AW] = (rstd * (dvh - jnp.mean(dvh, axis=-1, keepdims=True)
                                         - vhat * jnp.mean(dvh * vhat, axis=-1, keepdims=True))).astype(BF)

        xs = p_ref[:, 3 * AW:4 * AW]
        y = _cat_lanes(yf_ref) + _cat_lanes(yb_ref) + d_ref[...] * xs
        y2, th = _gelu(y)
        y2b = y2.astype(BF)
        z = _dot(y2b, wg_ref[0]) + bg_ref[...]
        sz = _sig(z)
        gb = p_ref[:, 4 * AW:5 * AW]
        sgb = _sig(gb)
        dyb = dm_ref[:, AW:2 * AW]
        dpa_ref[:, 4 * AW:5 * AW] = (dyb * (y2 * sz) * (sgb * (1.0 + gb * (1.0 - sgb)))).astype(BF)
        dy3 = dyb * (gb * sgb)
        dz = dy3 * y2 * sz * (1.0 - sz)
        dzb = dz.astype(BF)
        dwg_ref[...] += _dot(y2b, dzb, 0, 0)
        dbg_ref[...] += jnp.sum(dz, axis=0, keepdims=True)
        dy2 = dy3 * sz + _dot(dzb, wg_ref[0], 1, 1)
        dgelu = 0.5 * (1.0 + th) + 0.5 * y * (1.0 - th * th) * GELU_K * (1.0 + 3.0 * GELU_C * y * y)
        dy = dy2 * dgelu
        dd_ref[...] += jnp.sum(dy * xs, axis=0, keepdims=True)
        dy_ref[...] = dy

    tile = lambda w: pl.BlockSpec((TM, w), lambda i: (i, 0))
    return _call(body, name=name, grid=(t // TM,),
                 in_specs=[tile(5 * AW), Y4_SPEC(), Y4_SPEC(), tile(2 * AW), _full((1, AW)), _full((NGRP, CHUNK, CHUNK)),
                           _full((NGRP, CHUNK, 1)), _full((1, AW)),
                           pl.BlockSpec((1, AW, AW), lambda i: (glu_layer, 0, 0)), _full((1, AW))],
                 out_specs=[tile(5 * AW), tile(AW), _full((NGRP, CHUNK, CHUNK)), _full((NGRP, CHUNK, 1)),
                            _full((1, AW)), _full((1, AW)), _full((AW, AW)), _full((1, AW))],
                 out_shape=[jax.ShapeDtypeStruct((t, 5 * AW), BF),
                            jax.ShapeDtypeStruct((t, AW), F32), jax.ShapeDtypeStruct((NGRP, CHUNK, CHUNK), F32),
                            jax.ShapeDtypeStruct((NGRP, CHUNK, 1), F32), jax.ShapeDtypeStruct((1, AW), F32),
                            jax.ShapeDtypeStruct((1, AW), F32), jax.ShapeDtypeStruct((AW, AW), F32),
                            jax.ShapeDtypeStruct((1, AW), F32)],
                 scratch=[pltpu.VMEM((TM, AW), BF), pltpu.VMEM((TM, AW), F32), pltpu.VMEM((TM, AW), F32),
                          pltpu.VMEM((TM, AW), F32)])(p, yf, yb, dmix, vg, ws, bs, dsk, wglu, bglu)


LN = 512
NBLK = SW // LN
UB = AW // NBLK
SCAN_R = 32
SCAN_G = TM // SCAN_R
PW_ROWS = SCAN_R
POW_EXP = list(range(1, SCAN_R + 1))


def s5_disc(lam_re, lam_im, dt, lam_re_r, lam_im_r, dt_r, b_re, b_im):
    nexp = jnp.asarray(np.array(POW_EXP, np.float32).reshape(PW_ROWS, 1))

    def body(n_ref, lr_ref, li_ref, dt_ref, lrr_ref, lir_ref, dtr_ref, br_ref, bi_ref,
             pr_ref, pi_ref, bbr_ref, bbi_ref):
        for dr in range(2):
            dtl = jnp.exp(dt_ref[dr:dr + 1, :])
            zr = lr_ref[dr:dr + 1, :] * dtl
            zi = li_ref[dr:dr + 1, :] * dtl
            mag = jnp.exp(n_ref[...] * zr)
            ang = n_ref[...] * zi
            pr_ref[dr] = mag * jnp.cos(ang)
            pi_ref[dr] = mag * jnp.sin(ang)
        lr, li, dtv = lrr_ref[...], lir_ref[...], jnp.exp(dtr_ref[...])
        mag = jnp.exp(lr * dtv)
        nr = mag * jnp.cos(li * dtv) - 1.0
        ni = mag * jnp.sin(li * dtv)
        den = lr * lr + li * li
        fr = (nr * lr + ni * li) / den
        fi = (ni * lr - nr * li) / den
        bbr_ref[...] = fr * br_ref[...] - fi * bi_ref[...]
        bbi_ref[...] = fr * bi_ref[...] + fi * br_ref[...]

    rows = lam_re_r.shape[0]
    return _call(body, name="s5_disc",
                 out_shape=[jax.ShapeDtypeStruct((2, PW_ROWS, SW), F32), jax.ShapeDtypeStruct((2, PW_ROWS, SW), F32),
                            jax.ShapeDtypeStruct((rows, SP), F32), jax.ShapeDtypeStruct((rows, SP), F32)])(
        nexp, lam_re, lam_im, dt, lam_re_r, lam_im_r, dt_r, b_re, b_im)


def s5_param_bwd(lam_re_r, lam_im_r, dt_r, b_re, b_im, da_re, da_im, dbb_re, dbb_im):
    rows = lam_re_r.shape[0]
    ng = rows // SH
    seg = jnp.asarray(np.kron(np.eye(ng, dtype=np.float32), np.ones((1, SH), np.float32)))

    def body(seg_ref, lr_ref, li_ref, dt_ref, br_ref, bi_ref, dar_ref, dai_ref, dbbr_ref, dbbi_ref,
             dlr_ref, dli_ref, ddt_ref, dbr_ref, dbi_ref):
        lr, li, dtv = lr_ref[...], li_ref[...], jnp.exp(dt_ref[...])
        mag = jnp.exp(lr * dtv)
        lbr = mag * jnp.cos(li * dtv)
        lbi = mag * jnp.sin(li * dtv)
        den = lr * lr + li * li
        nr, ni = lbr - 1.0, lbi
        fr = (nr * lr + ni * li) / den
        fi = (ni * lr - nr * li) / den
        br, bi = br_ref[...], bi_ref[...]
        gbr, gbi = dbbr_ref[...], dbbi_ref[...]
        dbr_ref[...] = gbr * fr + gbi * fi
        dbi_ref[...] = gbi * fr - gbr * fi
        gfr = gbr * br + gbi * bi
        gfi = gbi * br - gbr * bi
        ilr, ili = lr / den, -li / den
        gnr = gfr * ilr + gfi * ili
        gni = gfi * ilr - gfr * ili
        qr = -(fr * ilr - fi * ili)
        qi = -(fr * ili + fi * ilr)
        glr = gfr * qr + gfi * qi
        gli = gfi * qr - gfr * qi
        first = (lax.broadcasted_iota(jnp.int32, (rows, 1), 0) % SH) == 0
        glbr = gnr + jnp.where(first, dar_ref[...], 0.0)
        glbi = gni + jnp.where(first, dai_ref[...], 0.0)
        gzr = glbr * lbr + glbi * lbi
        gzi = glbi * lbr - glbr * lbi
        glr = glr + gzr * dtv
        gli = gli + gzi * dtv
        gdt = (gzr * lr + gzi * li) * dtv
        hi = lax.Precision.HIGHEST
        sg = seg_ref[...]
        dlr_ref[...] = jnp.dot(sg, glr, precision=hi, preferred_element_type=F32)
        dli_ref[...] = jnp.dot(sg, gli, precision=hi, preferred_element_type=F32)
        ddt_ref[...] = jnp.sum(jnp.dot(sg, gdt, precision=hi, preferred_element_type=F32), axis=1, keepdims=True)

    return _call(body, name="s5_param_bwd",
                 out_shape=[jax.ShapeDtypeStruct((ng, SP), F32), jax.ShapeDtypeStruct((ng, SP), F32),
                            jax.ShapeDtypeStruct((ng, 1), F32), jax.ShapeDtypeStruct((rows, SP), F32),
                            jax.ShapeDtypeStruct((rows, SP), F32)])(
        seg, lam_re_r, lam_im_r, dt_r, b_re, b_im, da_re, da_im, dbb_re, dbb_im)


def _tile_order(kind, nt):
    if kind == "fwd":
        return lambda i: i
    if kind == "bwd":
        return lambda i: jnp.where(i == 0, 0, nt - i)
    if kind == "fwd_adj":
        return lambda i: nt - 1 - i
    if kind == "bwd_adj":
        return lambda i: jnp.where(i == nt - 1, 0, i + 1)
    raise ValueError(kind)


XS_BLK = 3 * AW // 128


def _load_perm(refs):
    return jnp.concatenate(
        [jnp.concatenate([ref[pl.ds(r, SCAN_G, stride=SCAN_R), :] for ref in refs], axis=1) for r in range(SCAN_R)],
        axis=0)


def _store_perm(out_ref, val):
    for r in range(SCAN_R):
        for q in range(AW // 128):
            out_ref[q, pl.ds(r, SCAN_G, stride=SCAN_R), :] = val[r * SCAN_G:(r + 1) * SCAN_G, q * 128:(q + 1) * 128]


def _scan2(br_ref, bi_ref, or_ref, oi_ref, h_off, cin_off, er_ref, ei_ref, cr_ref, ci_ref, pr_ref, pi_ref, reverse,
           corr=None):
    gpt = SCAN_G
    nr = SCAN_R
    offsets = list(range(nr))[::-1] if reverse else list(range(nr))
    blocks = [slice(b * LN, (b + 1) * LN) for b in range(NBLK)]
    slab = lambda r: slice(r * gpt, (r + 1) * gpt)
    a1 = [(pr_ref[0:1, c], pi_ref[0:1, c]) for c in blocks]
    x = [None] * NBLK
    for r in offsets:
        for b, c in enumerate(blocks):
            if x[b] is None:
                x[b] = (br_ref[slab(r), c], bi_ref[slab(r), c])
            else:
                (ar, ai), (xr, xi) = a1[b], x[b]
                x[b] = (br_ref[slab(r), c] + ar * xr - ai * xi, bi_ref[slab(r), c] + ar * xi + ai * xr)
    an = [(pr_ref[nr - 1:nr, c], pi_ref[nr - 1:nr, c]) for c in blocks]
    k = [(cr_ref[:, c], ci_ref[:, c]) for c in blocks]
    for g in (range(gpt - 1, -1, -1) if reverse else range(gpt)):
        for b, c in enumerate(blocks):
            (ar, ai), (kr, ki), (xr, xi) = an[b], k[b], x[b]
            er_ref[g:g + 1, c] = kr
            ei_ref[g:g + 1, c] = ki
            k[b] = (xr[g:g + 1, :] + ar * kr - ai * ki, xi[g:g + 1, :] + ar * ki + ai * kr)
    for b, c in enumerate(blocks):
        cr_ref[:, c] = k[b][0]
        ci_ref[:, c] = k[b][1]
        x[b] = (er_ref[:, c], ei_ref[:, c])
        if cin_off is not None:
            or_ref[cin_off:cin_off + gpt, c] = x[b][0]
            oi_ref[cin_off:cin_off + gpt, c] = x[b][1]
    acc = [None] * NBLK
    for r in offsets:
        for b, c in enumerate(blocks):
            (ar, ai), (xr, xi) = a1[b], x[b]
            x[b] = (br_ref[slab(r), c] + ar * xr - ai * xi, bi_ref[slab(r), c] + ar * xi + ai * xr)
            or_ref[h_off + r * gpt:h_off + (r + 1) * gpt, c] = x[b][0]
            oi_ref[h_off + r * gpt:h_off + (r + 1) * gpt, c] = x[b][1]
            if corr is not None:
                wr_ref, wi_ref, w_off = corr[:3]
                wr, wi = wr_ref[w_off + r * gpt:w_off + (r + 1) * gpt, c], wi_ref[w_off + r * gpt:w_off + (r + 1) * gpt, c]
                pr_, pi_ = x[b][0] * wr + x[b][1] * wi, x[b][1] * wr - x[b][0] * wi
                acc[b] = (pr_, pi_) if acc[b] is None else (acc[b][0] + pr_, acc[b][1] + pi_)
    if corr is not None:
        sr_ref, si_ref = corr[3:]
        for b, c in enumerate(blocks):
            sr_ref[:, c] += jnp.sum(acc[b][0], axis=0, keepdims=True)
            si_ref[:, c] += jnp.sum(acc[b][1], axis=0, keepdims=True)


HS_ROWS = TM + SCAN_G


def _hs_offsets(reverse):
    return (0, SCAN_G) if reverse else (SCAN_G, 0)


def _dir_spec(dr, shape):
    return pl.BlockSpec((1,) + shape, lambda i: (dr,) + (0,) * len(shape))


def s5q_fwd(p, bb_re, bb_im, ct_re, ct_im, pw_re, pw_im, dr, reverse, name):
    t = p.shape[0]
    nt = t // TM
    order = _tile_order("bwd" if reverse else "fwd", nt)
    nq = AW // 128
    h_off, p_off = _hs_offsets(reverse)

    def body(*refs):
        x_refs = refs[:nq]
        bbr_ref, bbi_ref, ctr_ref, cti_ref, pr_ref, pi_ref = [r.at[0] for r in refs[nq:nq + 6]]
        y_ref, hsr_ref, hsi_ref = refs[nq + 6:nq + 9]
        br_ref, bi_ref, er_ref, ei_ref, cr_ref, ci_ref = refs[nq + 9:]

        @pl.when(pl.program_id(0) == 0)
        def _():
            cr_ref[...] = jnp.zeros_like(cr_ref)
            ci_ref[...] = jnp.zeros_like(ci_ref)

        xb = _load_perm(x_refs).astype(BF)
        for j in range(NBLK):
            cols = slice(j * LN, (j + 1) * LN)
            br_ref[:, cols] = _dot(xb[:, j * UB:(j + 1) * UB], bbr_ref[j])
            bi_ref[:, cols] = _dot(xb[:, j * UB:(j + 1) * UB], bbi_ref[j])
        _scan2(br_ref, bi_ref, hsr_ref, hsi_ref, h_off, TM if reverse else 0, er_ref, ei_ref, cr_ref, ci_ref,
               pr_ref, pi_ref, reverse)
        y = jnp.concatenate(
            [_dot(hsr_ref[h_off:h_off + TM, j * LN:(j + 1) * LN].astype(BF), ctr_ref[j])
             - _dot(hsi_ref[h_off:h_off + TM, j * LN:(j + 1) * LN].astype(BF), cti_ref[j]) for j in range(NBLK)], axis=1)
        _store_perm(y_ref, y)

    state = lambda: pl.BlockSpec((HS_ROWS, SW), lambda i: (order(i), 0))
    xspec = lambda q: pl.BlockSpec((TM, 128), lambda i: (order(i), XS_BLK + q))
    return _call(body, name=name, grid=(nt,),
                 in_specs=[xspec(q) for q in range(nq)]
                 + [_dir_spec(dr, (NBLK, UB, LN)), _dir_spec(dr, (NBLK, UB, LN)), _dir_spec(dr, (NBLK, LN, UB)),
                    _dir_spec(dr, (NBLK, LN, UB)), _dir_spec(dr, (PW_ROWS, SW)), _dir_spec(dr, (PW_ROWS, SW))],
                 out_specs=[pl.BlockSpec((nq, TM, 128), lambda i: (0, order(i), 0)), state(), state()],
                 out_shape=[jax.ShapeDtypeStruct((nq, t, 128), F32), jax.ShapeDtypeStruct((nt * HS_ROWS, SW), F32),
                            jax.ShapeDtypeStruct((nt * HS_ROWS, SW), F32)],
                 scratch=[pltpu.VMEM((TM, SW), F32), pltpu.VMEM((TM, SW), F32),
                          pltpu.VMEM((SCAN_G, SW), F32), pltpu.VMEM((SCAN_G, SW), F32),
                          pltpu.VMEM((1, SW), F32), pltpu.VMEM((1, SW), F32)])(
        *([p] * nq), bb_re, bb_im, ct_re, ct_im, pw_re, pw_im)


def s5q_bwd(p, hs_re, hs_im, dy, bb_re, bb_im, ct_re, ct_im, pw_re, pw_im_conj, dr, reverse, name):
    t = p.shape[0]
    nt = t // TM
    order = _tile_order("bwd_adj" if reverse else "fwd_adj", nt)
    nq = AW // 128
    h_off, p_off = _hs_offsets(reverse)

    def body(*refs):
        x_refs, dy_refs = refs[:nq], refs[nq:2 * nq]
        (hsr_ref, hsi_ref, bbr_ref, bbi_ref, ctr_ref, cti_ref, pr_ref, pi_ref,
         dx_ref, dar_ref, dai_ref, dbbr_ref, dbbi_ref, dcr_ref, dci_ref,
         qr_ref, qi_ref, gr_ref, gi_ref, er_ref, ei_ref, cr_ref, ci_ref) = refs[2 * nq:]
        bbr_ref, bbi_ref, ctr_ref, cti_ref, pr_ref, pi_ref = [
            r.at[0] for r in (bbr_ref, bbi_ref, ctr_ref, cti_ref, pr_ref, pi_ref)]

        @pl.when(pl.program_id(0) == 0)
        def _():
            for r in (cr_ref, ci_ref, dar_ref, dai_ref, dbbr_ref, dbbi_ref, dcr_ref, dci_ref):
                r[...] = jnp.zeros_like(r)

        xb = _load_perm(x_refs).astype(BF)
        dyb = _load_perm(dy_refs).astype(BF)
        for j in range(NBLK):
            cols = slice(j * LN, (j + 1) * LN)
            qr_ref[:, cols] = _dot(dyb[:, j * UB:(j + 1) * UB], ctr_ref[j], 1, 1)
            qi_ref[:, cols] = -_dot(dyb[:, j * UB:(j + 1) * UB], cti_ref[j], 1, 1)
        _scan2(qr_ref, qi_ref, gr_ref, gi_ref, 0, None, er_ref, ei_ref, cr_ref, ci_ref, pr_ref, pi_ref, not reverse,
               corr=(hsr_ref, hsi_ref, p_off, dar_ref, dai_ref))
        dxs = []
        for j in range(NBLK):
            cols = slice(j * LN, (j + 1) * LN)
            xj = xb[:, j * UB:(j + 1) * UB]
            dyj = dyb[:, j * UB:(j + 1) * UB]
            grb, gib = gr_ref[:, cols].astype(BF), gi_ref[:, cols].astype(BF)
            dcr_ref[j] += _dot(dyj, hsr_ref[h_off:h_off + TM, cols].astype(BF), 0, 0)
            dci_ref[j] += -_dot(dyj, hsi_ref[h_off:h_off + TM, cols].astype(BF), 0, 0)
            dbbr_ref[j] += _dot(xj, grb, 0, 0)
            dbbi_ref[j] += _dot(xj, gib, 0, 0)
            dxs.append(_dot(grb, bbr_ref[j], 1, 1) + _dot(gib, bbi_ref[j], 1, 1))
        _store_perm(dx_ref, jnp.concatenate(dxs, axis=1))

    state = lambda: pl.BlockSpec((HS_ROWS, SW), lambda i: (order(i), 0))
    blockd = lambda: _full((NBLK, UB, LN))
    xspec = lambda q: pl.BlockSpec((TM, 128), lambda i: (order(i), XS_BLK + q))
    dyspec = lambda q: pl.BlockSpec((TM, 128), lambda i: (order(i), q))
    return _call(body, name=name, grid=(nt,),
                 in_specs=[xspec(q) for q in range(nq)] + [dyspec(q) for q in range(nq)]
                 + [state(), state(), _dir_spec(dr, (NBLK, UB, LN)), _dir_spec(dr, (NBLK, UB, LN)),
                    _dir_spec(dr, (NBLK, LN, UB)), _dir_spec(dr, (NBLK, LN, UB)),
                    _dir_spec(dr, (PW_ROWS, SW)), _dir_spec(dr, (PW_ROWS, SW))],
                 out_specs=[pl.BlockSpec((nq, TM, 128), lambda i: (0, order(i), 0)), _full((1, SW)), _full((1, SW)),
                            blockd(), blockd(), blockd(), blockd()],
                 out_shape=[jax.ShapeDtypeStruct((nq, t, 128), F32), jax.ShapeDtypeStruct((1, SW), F32),
                            jax.ShapeDtypeStruct((1, SW), F32)] + [jax.ShapeDtypeStruct((NBLK, UB, LN), F32)] * 4,
                 scratch=[pltpu.VMEM((TM, SW), F32), pltpu.VMEM((TM, SW), F32),
                          pltpu.VMEM((TM, SW), F32), pltpu.VMEM((TM, SW), F32),
                          pltpu.VMEM((SCAN_G, SW), F32), pltpu.VMEM((SCAN_G, SW), F32),
                          pltpu.VMEM((1, SW), F32), pltpu.VMEM((1, SW), F32)])(
        *([p] * nq), *([dy] * nq), hs_re, hs_im, bb_re, bb_im, ct_re, ct_im, pw_re, pw_im_conj)


def s5p_dx_sum(dy, dsk, dxf, dxb, dp, name):
    t = dy.shape[0]
    nq = AW // 128

    def body(dy_ref, d_ref, f_ref, b_ref, dp_ref, o_ref):
        o_ref[...] = (dy_ref[...] * d_ref[...] + _cat_lanes(f_ref) + _cat_lanes(b_ref)).astype(BF)

    tr = _mm_rows(t)
    tile = pl.BlockSpec((tr, AW), lambda i: (i, 0))
    blk4 = pl.BlockSpec((nq, tr, 128), lambda i: (0, i, 0))
    return pl.pallas_call(
        body, name=name, grid=(t // tr,),
        in_specs=[tile, _full((1, AW)), blk4, blk4, pl.BlockSpec(memory_space=pl.ANY)],
        out_specs=pl.BlockSpec((tr, AW), lambda i: (i, 3)), out_shape=jax.ShapeDtypeStruct(dp.shape, dp.dtype),
        input_output_aliases={4: 0},
        compiler_params=pltpu.CompilerParams(vmem_limit_bytes=VMEM_LIMIT_BYTES))(dy, dsk, dxf, dxb, dp)


SCALE = HD ** -0.5
NHEAD_NORM = NQ + NKV


def _partner(x):
    half0 = (lax.broadcasted_iota(jnp.int32, (1, HD), 1) % 64) < 32
    return jnp.where(half0, pltpu.roll(x, HD - 32, 1), pltpu.roll(x, 32, 1))


def attn_prep(p, qg, kg, cos, sins, name):
    t = p.shape[0]

    def body(p_ref, qg_ref, kg_ref, cos_ref, sin_ref, o_ref):
        cv, sv = cos_ref[...], sin_ref[...]
        for h in range(NHEAD_NORM):
            cols = slice(h * HD, (h + 1) * HD)
            blk = p_ref[:, cols]
            r = lax.rsqrt(jnp.mean(blk * blk, axis=-1, keepdims=True) + EPS)
            xn = blk * r * (qg_ref[...] if h < NQ else kg_ref[...])
            rot = xn * cv + _partner(xn) * sv
            o_ref[:, cols] = ((rot * SCALE) if h < NQ else rot).astype(BF)
        vcols = slice(NHEAD_NORM * HD, (NHEAD_NORM + NKV) * HD)
        o_ref[:, vcols] = p_ref[:, vcols].astype(BF)

    w = (NHEAD_NORM + NKV) * HD
    tr = _mm_rows(t)
    tile = lambda ww: pl.BlockSpec((tr, ww), lambda i: (i, 0))
    return _call(body, name=name, grid=(t // tr,),
                 in_specs=[tile(w), _full((1, HD)), _full((1, HD)), tile(HD), tile(HD)],
                 out_specs=tile(w), out_shape=jax.ShapeDtypeStruct((t, w), BF))(p, qg, kg, cos, sins)


def attn_prep_bwd(p, dq, dk, dv, qg, kg, cos, sins, dp, name):
    t = p.shape[0]

    def body(p_ref, dq_ref, dk_ref, dv_ref, qg_ref, kg_ref, cos_ref, sin_ref, dp_ref, o_ref, dqg_ref, dkg_ref):
        @pl.when(pl.program_id(0) == 0)
        def _():
            dqg_ref[...] = jnp.zeros_like(dqg_ref)
            dkg_ref[...] = jnp.zeros_like(dkg_ref)

        cv, sv = cos_ref[...], sin_ref[...]
        for h in range(NHEAD_NORM):
            cols = slice(h * HD, (h + 1) * HD)
            blk = p_ref[:, cols]
            r = lax.rsqrt(jnp.mean(blk * blk, axis=-1, keepdims=True) + EPS)
            xh = blk * r
            if h < NQ:
                drot = dq_ref[:, cols] * SCALE
                gv, dg_ref = qg_ref[...], dqg_ref
            else:
                drot = dk_ref[:, (h - NQ) * HD:(h - NQ + 1) * HD]
                gv, dg_ref = kg_ref[...], dkg_ref
            dxn = drot * cv + _partner(drot * sv)
            dg_ref[...] += jnp.sum(dxn * xh, axis=0, keepdims=True)
            dxh = dxn * gv
            o_ref[:, cols] = (r * (dxh - xh * jnp.mean(dxh * xh, axis=-1, keepdims=True))).astype(BF)
        o_ref[:, NHEAD_NORM * HD:(NHEAD_NORM + NKV) * HD] = dv_ref[...].astype(BF)

    w = (NHEAD_NORM + NKV) * HD
    tr = _mm_rows(t)
    tile = lambda ww: pl.BlockSpec((tr, ww), lambda i: (i, 0))
    return pl.pallas_call(
        body, name=name, grid=(t // tr,),
        in_specs=[tile(w), tile(NQ * HD), tile(NKV * HD), tile(NKV * HD), _full((1, HD)), _full((1, HD)),
                  tile(HD), tile(HD), pl.BlockSpec(memory_space=pl.ANY)],
        out_specs=[tile(w), _full((1, HD)), _full((1, HD))],
        out_shape=[jax.ShapeDtypeStruct(dp.shape, dp.dtype), jax.ShapeDtypeStruct((1, HD), F32),
                   jax.ShapeDtypeStruct((1, HD), F32)],
        input_output_aliases={8: 0},
        compiler_params=pltpu.CompilerParams(vmem_limit_bytes=VMEM_LIMIT_BYTES))(p, dq, dk, dv, qg, kg, cos, sins, dp)


KCOL = NQ
VCOL = NQ + NKV
GCOL = (NQ + 2 * NKV)
QPK = NQ // NKV
ATT_KCHUNK = 512


def attn_fwd(qkv, p, name):
    t = qkv.shape[0]

    def body(q_ref, k_ref, v_ref, g_ref, o_ref, mix_ref, lse_ref):
        def attend(nk):
            for hh in range(QPK):
                attend_head(nk, slice(hh * HD, (hh + 1) * HD))

        def attend_head(nk, cols):
            q = q_ref[:, cols]
            chunks = [(k0, min(k0 + 2 * ATT_KCHUNK, nk)) for k0 in range(0, nk, 2 * ATT_KCHUNK)]
            s_next = _dot(q, k_ref[chunks[0][0]:chunks[0][1], :], 1, 1)
            m = l = acc = None
            for ci, (k0, k1) in enumerate(chunks):
                s = s_next
                if ci + 1 < len(chunks):
                    s_next = _dot(q, k_ref[chunks[ci + 1][0]:chunks[ci + 1][1], :], 1, 1)
                mc = jnp.max(s, axis=-1, keepdims=True)
                m_new = mc if m is None else jnp.maximum(m, mc)
                pe = jnp.exp(s - m_new)
                lc = jnp.sum(pe, axis=-1, keepdims=True)
                pv = _dot(pe.astype(BF), v_ref[k0:k1, :])
                if m is None:
                    l, acc = lc, pv
                else:
                    alpha = jnp.exp(m - m_new)
                    l, acc = alpha * l + lc, alpha * acc + pv
                m = m_new
            o = acc / l
            gt = g_ref[:, cols]
            o_ref[:, cols] = o
            mix_ref[:, cols] = (o * (gt * _sig(gt))).astype(BF)
            lse_ref[:, cols] = jnp.broadcast_to(m + jnp.log(l), (TM, HD))

        pl.when(pl.program_id(1) == 0)(lambda: attend(NC))
        pl.when(pl.program_id(1) > 0)(lambda: attend(t))

    blk = pl.BlockSpec((TM, QPK * HD), lambda kv, i: (i, kv))
    return _call(body, name=name, grid=(NKV, t // TM),
                 in_specs=[blk, pl.BlockSpec((t, HD), lambda kv, i: (0, KCOL + kv)),
                           pl.BlockSpec((t, HD), lambda kv, i: (0, VCOL + kv)),
                           pl.BlockSpec((TM, QPK * HD), lambda kv, i: (i, GCOL // QPK + kv))],
                 out_specs=[blk, blk, blk],
                 out_shape=[jax.ShapeDtypeStruct((t, NQ * HD), F32), jax.ShapeDtypeStruct((t, NQ * HD), BF),
                            jax.ShapeDtypeStruct((t, NQ * HD), F32)])(qkv, qkv, qkv, p)


def attn_bwd(qkv, p, dmix, o, lse, name):
    t = qkv.shape[0]

    def body(q_ref, k_ref, v_ref, g_ref, dm_ref, o_ref, lse_ref, dq_ref, dg_ref, dk_ref, dv_ref):
        i = pl.program_id(1)

        @pl.when(i == 0)
        def _():
            dk_ref[...] = jnp.zeros_like(dk_ref)
            dv_ref[...] = jnp.zeros_like(dv_ref)

        def bwd(nk):
            for hh in range(QPK):
                bwd_head(nk, slice(hh * HD, (hh + 1) * HD))

        def bwd_head(nk, cols):
            gt = g_ref[:, cols]
            sg = _sig(gt)
            ov = o_ref[:, cols]
            dmv = dm_ref[:, cols]
            dg_ref[:, cols] = (dmv * ov * (sg * (1.0 + gt * (1.0 - sg)))).astype(BF)
            do = dmv * (gt * sg)
            dr = jnp.sum(do * ov, axis=-1, keepdims=True)
            dob = do.astype(BF)
            q = q_ref[:, cols]
            lse = lse_ref[:, cols][:, 0:1]
            chunks = [slice(k0, min(k0 + ATT_KCHUNK, nk)) for k0 in range(0, nk, ATT_KCHUNK)]
            nxt = (_dot(q, k_ref[chunks[0], :], 1, 1), _dot(dob, v_ref[chunks[0], :], 1, 1))
            dq = None
            for ci, keys in enumerate(chunks):
                s, dp = nxt
                if ci + 1 < len(chunks):
                    nxt = (_dot(q, k_ref[chunks[ci + 1], :], 1, 1), _dot(dob, v_ref[chunks[ci + 1], :], 1, 1))
                pe = jnp.exp(s - lse)
                dsb = (pe * (dp - dr)).astype(BF)
                part = _dot(dsb, k_ref[keys, :])
                dq = part if dq is None else dq + part
                dv_ref[keys, :] += _dot(pe.astype(BF), dob, 0, 0)
                dk_ref[keys, :] += _dot(dsb, q, 0, 0)
            dq_ref[:, cols] = dq

        pl.when(i == 0)(lambda: bwd(NC))
        pl.when(i > 0)(lambda: bwd(t))

    blk = pl.BlockSpec((TM, QPK * HD), lambda kv, i: (i, kv))
    gate = pl.BlockSpec((TM, QPK * HD), lambda kv, i: (i, GCOL // QPK + kv))
    acc = pl.BlockSpec((t, HD), lambda kv, i: (0, kv))
    return _call(body, name=name, grid=(NKV, t // TM),
                 in_specs=[blk, pl.BlockSpec((t, HD), lambda kv, i: (0, KCOL + kv)),
                           pl.BlockSpec((t, HD), lambda kv, i: (0, VCOL + kv)), gate, blk, blk, blk],
                 out_specs=[blk, gate, acc, acc],
                 out_shape=[jax.ShapeDtypeStruct((t, NQ * HD), F32), jax.ShapeDtypeStruct((t, (GCOL + NQ) * HD), BF),
                            jax.ShapeDtypeStruct((t, NKV * HD), F32), jax.ShapeDtypeStruct((t, NKV * HD), F32)])(
        qkv, qkv, qkv, p, dmix, o, lse)


SUM_TILE_BYTES = 6 * 1024 * 1024


def _row_tile(rows, row_bytes, cap=2 * 1024 * 1024):
    if rows * row_bytes <= cap or rows % 8:
        return rows
    tr = rows
    while tr * row_bytes > cap and tr % 16 == 0:
        tr //= 2
    return tr


def _adamw_update(w_ref, g_ref, m_ref, v_ref, d_ref, nm_ref, nv_ref):
    gv = g_ref[...]
    m2 = ADAM_B1 * m_ref[...] + (1.0 - ADAM_B1) * gv
    v2 = ADAM_B2 * v_ref[...] + (1.0 - ADAM_B2) * (gv * gv)
    mh = m2 / (1.0 - ADAM_B1 ** ADAM_STEP)
    vh = v2 / (1.0 - ADAM_B2 ** ADAM_STEP)
    d_ref[...] = -ADAM_LR * (mh / (jnp.sqrt(vh) + ADAM_EPS) + ADAM_WD * w_ref[...])
    nm_ref[...] = m2
    nv_ref[...] = v2


def adamw_many(ws, gs, ms, vs, name):
    n = len(ws)

    def body(*refs):
        for k in range(n):
            _adamw_update(*[refs[j * n + k] for j in range(7)])

    shapes = [jax.ShapeDtypeStruct(w.shape, F32) for w in ws]
    res = _call(body, name=name, out_shape=shapes * 3)(*ws, *gs, *ms, *vs)
    return res[:n], res[n:2 * n], res[2 * n:]


def adamw(w, g, m, v, name):
    r, cdim = w.shape
    tr = _row_tile(r, 4 * max(cdim, 128))

    def body(w_ref, g_ref, m_ref, v_ref, d_ref, nm_ref, nv_ref):
        _adamw_update(w_ref, g_ref, m_ref, v_ref, d_ref, nm_ref, nv_ref)

    tile = pl.BlockSpec((tr, cdim), lambda i: (i, 0))
    sh = jax.ShapeDtypeStruct((r, cdim), F32)
    return _call(body, name=name, grid=(r // tr,), in_specs=[tile] * 4, out_specs=[tile] * 3,
                 out_shape=[sh, sh, sh])(w, g, m, v)


def sum_lead(a, name, out_dtype=F32):
    n, r, cdim = a.shape
    tr = _row_tile(r, 4 * n * max(cdim, 128), SUM_TILE_BYTES)

    def body(a_ref, o_ref):
        acc = a_ref[0].astype(F32)
        for k in range(1, n):
            acc = acc + a_ref[k].astype(F32)
        o_ref[...] = acc.astype(o_ref.dtype)

    return _call(body, name=name, grid=(r // tr,),
                 in_specs=[pl.BlockSpec((n, tr, cdim), lambda i: (0, i, 0))],
                 out_specs=pl.BlockSpec((tr, cdim), lambda i: (i, 0)),
                 out_shape=jax.ShapeDtypeStruct((r, cdim), out_dtype))(a)


_FLIPS = {"xy": [(1, 0, 0), (0, 1, 0), (1, 1, 0)], "c": [(0, 0, 1)],
          "all": [(0, 0, 1), (0, 1, 0), (0, 1, 1), (1, 0, 0), (1, 0, 1), (1, 1, 0), (1, 1, 1)]}
_GROUP_SIZE = {"xy": 4, "c": 2, "all": 8}


def _group_index(group, x, y, c):
    return {"xy": 2 * x + y, "c": c, "all": 4 * x + 2 * y + c}[group]


def exchange(items, name):
    plan = []
    for arr, group, kind in items:
        chunk = arr.shape if kind == "gather" else arr.shape[1:]
        plan.append((group, kind, chunk))
    ncopy = sum(len(_FLIPS[g]) for g, _, _ in plan)
    nitem = len(plan)

    def body(*refs):
        srcs, dsts = refs[:nitem], refs[nitem:2 * nitem]
        send_sems, recv_sems, local_sems = refs[2 * nitem:]
        x, y, c = lax.axis_index("x"), lax.axis_index("y"), lax.axis_index("c")
        sends, recvs, locals_ = [], [], []
        n = 0
        for k, (group, kind, _) in enumerate(plan):
            me = _group_index(group, x, y, c)
            own = srcs[k] if kind == "gather" else srcs[k].at[me]
            locals_.append(pltpu.make_async_copy(own, dsts[k].at[me], local_sems.at[k]))
            for fx, fy, fc in _FLIPS[group]:
                px, py, pc = (1 - x if fx else x), (1 - y if fy else y), (1 - c if fc else c)
                peer = _group_index(group, px, py, pc)
                src = srcs[k] if kind == "gather" else srcs[k].at[peer]
                sends.append(pltpu.make_async_remote_copy(
                    src_ref=src, dst_ref=dsts[k].at[me], send_sem=send_sems.at[n], recv_sem=recv_sems.at[n],
                    device_id=(px, py, pc), device_id_type=MESH))
                recvs.append(pltpu.make_async_remote_copy(
                    src_ref=src, dst_ref=dsts[k].at[peer], send_sem=send_sems.at[n], recv_sem=recv_sems.at[n],
                    device_id=(px, py, pc), device_id_type=MESH))
                n += 1
        for cp in locals_ + sends:
            cp.start()
        for cp in recvs:
            cp.wait_recv()
        for cp in sends:
            cp.wait_send()
        for cp in locals_:
            cp.wait()

    anyspec = pl.BlockSpec(memory_space=pl.ANY)
    outs = [jax.ShapeDtypeStruct((_GROUP_SIZE[g],) + tuple(chunk), arr.dtype)
            for (arr, _, _), (g, _, chunk) in zip(items, plan)]
    res = pl.pallas_call(
        body, name=name, out_shape=outs, in_specs=[anyspec] * nitem, out_specs=[anyspec] * nitem,
        scratch_shapes=[pltpu.SemaphoreType.DMA((ncopy,)), pltpu.SemaphoreType.DMA((ncopy,)),
                        pltpu.SemaphoreType.DMA((nitem,))],
        compiler_params=pltpu.CompilerParams(has_side_effects=True))(*[a for a, _, _ in items])
    return list(res)


D2D_PIECES = 4


def d2d(items, name):
    n = len(items)
    swaps = [k for k, (_, kind) in enumerate(items) if kind == "swap"]

    def pieces_of(rows):
        npc = D2D_PIECES if rows % (8 * D2D_PIECES) == 0 else 1
        return npc, rows // npc

    ncopy = sum(pieces_of(a.shape[0] if kind == "gather" else a.shape[1])[0] for a, kind in items)

    def body(*refs):
        srcs, outs = refs[:n], refs[n:2 * n]
        stages = dict(zip(swaps, refs[2 * n:2 * n + len(swaps)]))
        send_sems, recv_sems, local_sems = refs[2 * n + len(swaps):]
        x, y, c = lax.axis_index("x"), lax.axis_index("y"), lax.axis_index("c")
        sib = (x, y, 1 - c)

        def remote(src, dst, q):
            return pltpu.make_async_remote_copy(src_ref=src, dst_ref=dst, send_sem=send_sems.at[q],
                                                recv_sem=recv_sems.at[q], device_id=sib, device_id_type=MESH)

        copies = []
        q = 0
        for k, (arr, kind) in enumerate(items):
            npc, pr = pieces_of(arr.shape[0] if kind == "gather" else arr.shape[1])
            for pc in range(npc):
                rs = pl.ds(pc * pr, pr)
                if kind == "gather":
                    mine, theirs = outs[k].at[c, rs], outs[k].at[1 - c, rs]
                    copies.append((pltpu.make_async_copy(srcs[k].at[rs], mine, local_sems.at[q]),
                                   remote(mine, mine, q), remote(theirs, theirs, q)))
                else:
                    stage, land = stages[k].at[rs], outs[k].at[rs]
                    copies.append((pltpu.make_async_copy(srcs[k].at[1 - c, rs], stage, local_sems.at[q]),
                                   remote(stage, land, q), remote(stage, land, q)))
                q += 1
        for loc, _, _ in copies:
            loc.start()
        for loc, send, _ in copies:
            loc.wait()
            send.start()
        for _, _, recv in copies:
            recv.wait_recv()
        for _, send, _ in copies:
            send.wait_send()

    outs = [jax.ShapeDtypeStruct((2,) + a.shape if kind == "gather" else a.shape[1:], a.dtype) for a, kind in items]
    res = pl.pallas_call(
        body, name=name, out_shape=outs, in_specs=[pl.BlockSpec(memory_space=pl.ANY)] * n,
        out_specs=[pl.BlockSpec(memory_space=pltpu.VMEM)] * n,
        scratch_shapes=[pltpu.VMEM(items[k][0].shape[1:], items[k][0].dtype) for k in swaps]
        + [pltpu.SemaphoreType.DMA((ncopy,)), pltpu.SemaphoreType.DMA((ncopy,)), pltpu.SemaphoreType.DMA((ncopy,))],
        compiler_params=pltpu.CompilerParams(has_side_effects=True, vmem_limit_bytes=VMEM_LIMIT_BYTES))(
        *[a for a, _ in items])
    return list(res)


def sum_own(pair, got, name, out_dtype=F32):
    _, r, cdim = pair.shape
    tr = _row_tile(r, 4 * 2 * max(cdim, 128), SUM_TILE_BYTES)

    def body(c_ref, p_ref, g_ref, o_ref):
        o_ref[...] = (p_ref[0] + g_ref[...]).astype(o_ref.dtype)

    me = lax.axis_index("c").astype(jnp.int32).reshape(1)
    return pl.pallas_call(
        body, name=name, out_shape=jax.ShapeDtypeStruct((r, cdim), out_dtype),
        grid_spec=pltpu.PrefetchScalarGridSpec(
            num_scalar_prefetch=1, grid=(r // tr,),
            in_specs=[pl.BlockSpec((1, tr, cdim), lambda i, c_ref: (c_ref[0], i, 0)),
                      pl.BlockSpec((tr, cdim), lambda i, c_ref: (i, 0))],
            out_specs=pl.BlockSpec((tr, cdim), lambda i, c_ref: (i, 0))),
        compiler_params=pltpu.CompilerParams(vmem_limit_bytes=VMEM_LIMIT_BYTES))(me, pair, got)


_SMALL = ["c_ctx", "norm_g", "b_mod", "gm_v_g", "gm_w_s", "gm_b_s", "s5_lam_re", "s5_lam_im", "s5_log_dt",
          "s5_b_re", "s5_b_im", "s5_c_re", "s5_c_im", "s5_d", "s5_b_glu", "q_norm_g", "k_norm_g", "final_g"]
_BIG = ["we_in", "we_out", "s5_w_glu", "wo_in", "wo_out"]
_WEIGHTS = ["c_ctx", "norm_g", "w_mod", "b_mod", "we_in", "we_out", "gm_v_g", "gm_w_s", "gm_b_s", "s5_lam_re",
            "s5_lam_im", "s5_log_dt", "s5_b_re", "s5_b_im", "s5_c_re", "s5_c_im", "s5_d", "s5_w_glu", "s5_b_glu",
            "wo_in", "wo_out", "q_norm_g", "k_norm_g", "final_g"]
_SMALL_ALIGN = 8 * 8 * 128


def _rope_tables(n_lat):
    rows = n_lat // GRID_W
    freqs = ROPE_THETA ** (-jnp.arange(HD // 4, dtype=F32) / (HD // 4))
    ar, ac = jnp.arange(rows)[:, None] * freqs, jnp.arange(GRID_W)[:, None] * freqs
    by_row = lambda v: jnp.repeat(v, GRID_W, axis=0)
    by_col = lambda v: jnp.tile(v, (rows, 1))
    cr, sr, cc, sc = by_row(jnp.cos(ar)), by_row(jnp.sin(ar)), by_col(jnp.cos(ac)), by_col(jnp.sin(ac))
    cos = jnp.concatenate([cr, cr, cc, cc], axis=1)
    sins = jnp.concatenate([-sr, sr, -sc, sc], axis=1)
    cos = jnp.concatenate([jnp.ones((NC, HD), F32), cos], axis=0)
    sins = jnp.concatenate([jnp.zeros((NC, HD), F32), sins], axis=0)
    return cos, sins


def _block_diag(v, transpose):
    gpb = SG // NBLK
    v = v.reshape(2, NBLK, gpb, SH, SP)
    eye = jnp.eye(gpb, dtype=v.dtype)
    if transpose:
        return jnp.einsum("djahp,ab->djapbh", v, eye).reshape(2, NBLK, LN, UB)
    return jnp.einsum("djahp,ab->djahbp", v, eye).reshape(2, NBLK, UB, LN)


def _diag_blocks(m):
    gpb = SG // NBLK
    return jnp.einsum("jahap->jahp", m.reshape(NBLK, gpb, SH, gpb, SP)).reshape(SG, SH, SP)


def _view2d(a):
    if a.ndim == 1:
        return a.reshape(1, -1)
    if a.shape[-1] < 64 and a.size % 1024 == 0:
        return a.reshape(-1, 1024)
    return a.reshape(-1, a.shape[-1])


def kernel(x, c, ctx, c_ctx, norm_g, w_mod, b_mod, we_in, we_out, gm_v_g, gm_w_s, gm_b_s, s5_lam_re, s5_lam_im, s5_log_dt, s5_b_re, s5_b_im, s5_c_re, s5_c_im, s5_d, s5_w_glu, s5_b_glu, wo_in, wo_out, q_norm_g, k_norm_g, final_g, loss_target, m_c_ctx, m_norm_g, m_w_mod, m_b_mod, m_we_in, m_we_out, m_gm_v_g, m_gm_w_s, m_gm_b_s, m_s5_lam_re, m_s5_lam_im, m_s5_log_dt, m_s5_b_re, m_s5_b_im, m_s5_c_re, m_s5_c_im, m_s5_d, m_s5_w_glu, m_s5_b_glu, m_wo_in, m_wo_out, m_q_norm_g, m_k_norm_g, m_final_g, v_c_ctx, v_norm_g, v_w_mod, v_b_mod, v_we_in, v_we_out, v_gm_v_g, v_gm_w_s, v_gm_b_s, v_s5_lam_re, v_s5_lam_im, v_s5_log_dt, v_s5_b_re, v_s5_b_im, v_s5_c_re, v_s5_c_im, v_s5_d, v_s5_w_glu, v_s5_b_glu, v_wo_in, v_wo_out, v_q_norm_g, v_k_norm_g, v_final_g):
    weights = dict(c_ctx=c_ctx, norm_g=norm_g, w_mod=w_mod, b_mod=b_mod, we_in=we_in, we_out=we_out, gm_v_g=gm_v_g,
                   gm_w_s=gm_w_s, gm_b_s=gm_b_s, s5_lam_re=s5_lam_re, s5_lam_im=s5_lam_im, s5_log_dt=s5_log_dt,
                   s5_b_re=s5_b_re, s5_b_im=s5_b_im, s5_c_re=s5_c_re, s5_c_im=s5_c_im, s5_d=s5_d, s5_w_glu=s5_w_glu,
                   s5_b_glu=s5_b_glu, wo_in=wo_in, wo_out=wo_out, q_norm_g=q_norm_g, k_norm_g=k_norm_g,
                   final_g=final_g)
    mom_m = dict(c_ctx=m_c_ctx, norm_g=m_norm_g, w_mod=m_w_mod, b_mod=m_b_mod, we_in=m_we_in, we_out=m_we_out,
                 gm_v_g=m_gm_v_g, gm_w_s=m_gm_w_s, gm_b_s=m_gm_b_s, s5_lam_re=m_s5_lam_re, s5_lam_im=m_s5_lam_im,
                 s5_log_dt=m_s5_log_dt, s5_b_re=m_s5_b_re, s5_b_im=m_s5_b_im, s5_c_re=m_s5_c_re, s5_c_im=m_s5_c_im,
                 s5_d=m_s5_d, s5_w_glu=m_s5_w_glu, s5_b_glu=m_s5_b_glu, wo_in=m_wo_in, wo_out=m_wo_out,
                 q_norm_g=m_q_norm_g, k_norm_g=m_k_norm_g, final_g=m_final_g)
    mom_v = dict(c_ctx=v_c_ctx, norm_g=v_norm_g, w_mod=v_w_mod, b_mod=v_b_mod, we_in=v_we_in, we_out=v_we_out,
                 gm_v_g=v_gm_v_g, gm_w_s=v_gm_w_s, gm_b_s=v_gm_b_s, s5_lam_re=v_s5_lam_re, s5_lam_im=v_s5_lam_im,
                 s5_log_dt=v_s5_log_dt, s5_b_re=v_s5_b_re, s5_b_im=v_s5_b_im, s5_c_re=v_s5_c_re, s5_c_im=v_s5_c_im,
                 s5_d=v_s5_d, s5_w_glu=v_s5_w_glu, s5_b_glu=v_s5_b_glu, wo_in=v_wo_in, wo_out=v_wo_out,
                 q_norm_g=v_q_norm_g, k_norm_g=v_k_norm_g, final_g=v_final_g)

    ixy = 2 * lax.axis_index("x") + lax.axis_index("y")
    n_lat = x.shape[1]
    nl = norm_g.shape[0]
    nmod = w_mod.shape[2]
    xin = (ctx.reshape(ctx.shape[1], D), x.reshape(n_lat, D))

    ic = lax.axis_index("c")
    mine = [lax.dynamic_index_in_dim(weights[n], ic, 0, keepdims=False).astype(BF) for n in _BIG]
    got = exchange([(m_, "xy", "gather") for m_ in mine] + [(c, "xy", "gather")], "gather_weights")
    both = d2d([(g_.reshape(-1, g_.shape[-1]), "gather") for g_ in got[:len(_BIG)]], "swap_weights")
    both = [b_.reshape((2,) + g_.shape) for b_, g_ in zip(both, got)]
    wein = [(both[0], l) for l in range(2)]
    weout = [(both[1].reshape(2, 1, D, D), l) for l in range(2)]
    wglu = [(both[2].reshape(2, AW, AW), l) for l in range(2)]
    woin = [(both[3], l) for l in range(2)]
    woout = [(both[4].reshape(2, 1, D, D), l) for l in range(2)]
    c_group = got[len(_BIG)].reshape(4, D)

    cond = jnp.concatenate([c_group, jnp.broadcast_to(c_ctx.reshape(1, D), (4, D))], axis=0)
    b_shard = lax.dynamic_slice(b_mod, (0, ixy * nmod), (nl, nmod)).reshape(nl, 1, nmod)
    mpart = ada_fwd(cond, w_mod, b_shard)
    m_lat, m_ctx = exchange([(jnp.transpose(mpart[:, 0:4], (1, 0, 2)), "xy", "scatter"),
                             (mpart[:, 4], "xy", "gather")], "exchange_mod")
    m_lat = jnp.transpose(m_lat, (1, 0, 2)).reshape(nl, 3, D)
    m_ctx = jnp.transpose(m_ctx, (1, 0, 2)).reshape(nl, 3, D)
    mods = [jnp.stack([m_ctx[l], m_lat[l]], axis=0) for l in range(nl)]

    loss_part, dx, g, d_norm_g, d_mod_lat, d_mod_ctx, d_final_g = _local_step(
        xin, loss_target.reshape(n_lat, D), mods, wein, weout, wglu, woin, woout, weights)
    grad_x = dx.reshape(1, n_lat, D)

    d_mod_lat, d_mod_ctx = jnp.stack(d_mod_lat), jnp.stack(d_mod_ctx)
    dm_send = jnp.stack([d_mod_lat.reshape(nl, 4, nmod), d_mod_ctx.reshape(nl, 4, nmod)])
    (dm_got,) = exchange([(jnp.transpose(dm_send, (2, 0, 1, 3)), "xy", "scatter")], "exchange_dmod")
    dm_rows = jnp.concatenate([dm_got[:, 0], dm_got[:, 1]], axis=0)
    gw_mod, d_cctx = ada_bwd(cond, jnp.transpose(dm_rows, (1, 0, 2)), w_mod)
    g_small = dict(c_ctx=d_cctx.reshape(D), norm_g=jnp.stack(d_norm_g), b_mod=add2(d_mod_lat, d_mod_ctx, "add_dbmod"),
                   final_g=d_final_g.reshape(D))
    for name in _SMALL:
        if name not in g_small:
            g_small[name] = jnp.stack(g[name])

    flat = jnp.concatenate([g_small[n].reshape(-1) for n in _SMALL])
    nflat = flat.shape[0]
    npad = -(-nflat // _SMALL_ALIGN) * _SMALL_ALIGN
    flat = jnp.concatenate([flat, jnp.zeros((npad - nflat,), F32)]).reshape(8, npad // (8 * 128), 128)
    pairs = [gw_mod.reshape(2, nl // 2 * D, nmod)]
    for name in _BIG:
        st = jnp.stack(g[name]) if isinstance(g[name], list) else g[name]
        pairs.append(st.reshape(2, -1, st.shape[-1]))
    got_a = d2d([(pairs[k], "swap") for k in (1, 2, 3)], "reduce_chip_a")
    got_b = d2d([(pairs[k], "swap") for k in (0, 4, 5)], "reduce_chip_b")
    theirs = [got_b[0]] + got_a + got_b[1:]
    chip = [sum_own(pairs[k], theirs[k], f"sum_chip{k}", F32 if k == 0 else BF) for k in range(len(pairs))]
    parts = exchange([(flat, "all", "scatter")]
                     + [(s_.reshape(4, s_.shape[0] // 4, s_.shape[1]), "xy", "scatter") for s_ in chip[1:]],
                     "reduce_scatter")
    sums = [sum_lead(pt, f"sum_shard{k}") for k, pt in enumerate(parts)]
    full = d2d([(sums[0], "gather"), (chip[0], "gather")] + [(s_, "gather") for s_ in sums[1:]], "all_gather")
    (flat_full,) = exchange([(full[0], "xy", "gather")], "gather_small")
    full = [flat_full] + full[1:]
    flat = full[0].reshape(-1)
    grads = {}
    off = 0
    for name in _SMALL:
        sz = weights[name].size
        grads[name] = flat[off:off + sz].reshape(weights[name].shape)
        off += sz
    grads["w_mod"] = full[1].reshape(w_mod.shape)
    for k, name in enumerate(_BIG):
        grads[name] = full[2 + k].reshape(weights[name].shape)

    delta, new_m, new_v = {}, {}, {}
    views = [_view2d(weights[n]) for n in _SMALL]
    ds, nms, nvs = adamw_many(views, [grads[n].reshape(w2.shape) for n, w2 in zip(_SMALL, views)],
                              [mom_m[n].reshape(w2.shape) for n, w2 in zip(_SMALL, views)],
                              [mom_v[n].reshape(w2.shape) for n, w2 in zip(_SMALL, views)], "adamw_small")
    for n, d2, m2, v2 in zip(_SMALL, ds, nms, nvs):
        shp = weights[n].shape
        delta[n], new_m[n], new_v[n] = d2.reshape(shp), m2.reshape(shp), v2.reshape(shp)
    for name in ["w_mod"] + _BIG:
        w2 = _view2d(weights[name])
        d2, m2, v2 = adamw(w2, grads[name].reshape(w2.shape), mom_m[name].reshape(w2.shape),
                           mom_v[name].reshape(w2.shape), f"adamw_{name}")
        shp = weights[name].shape
        delta[name], new_m[name], new_v[name] = d2.reshape(shp), m2.reshape(shp), v2.reshape(shp)

    loss = lax.psum(loss_part[0, 0], ("x", "y", "c"))
    return (loss, grad_x, *[grads[n] for n in _WEIGHTS], *[delta[n] for n in _WEIGHTS],
            *[new_m[n] for n in _WEIGHTS], *[new_v[n] for n in _WEIGHTS])


def _local_step(xin, target, mods, wein, weout, wglu, woin, woout, w):
    norm_g, gm_v_g, gm_w_s, gm_b_s = w["norm_g"], w["gm_v_g"], w["gm_w_s"], w["gm_b_s"]
    s5_lam_re, s5_lam_im, s5_log_dt = w["s5_lam_re"], w["s5_lam_im"], w["s5_log_dt"]
    s5_b_re, s5_b_im, s5_c_re, s5_c_im = w["s5_b_re"], w["s5_b_im"], w["s5_c_re"], w["s5_c_im"]
    s5_d, s5_b_glu, q_norm_g, k_norm_g, final_g = w["s5_d"], w["s5_b_glu"], w["q_norm_g"], w["k_norm_g"], w["final_g"]
    nl = norm_g.shape[0]
    n_lat = xin[1].shape[0]

    cos, sins = _rope_tables(n_lat)

    s5p = []
    for i in range(2):
        lam_l = (s5_lam_re[i].reshape(2, SW), s5_lam_im[i].reshape(2, SW),
                 jnp.repeat(s5_log_dt[i], SP, axis=1))
        lam_r = (jnp.repeat(s5_lam_re[i].reshape(2 * SG, SP), SH, axis=0),
                 jnp.repeat(s5_lam_im[i].reshape(2 * SG, SP), SH, axis=0),
                 jnp.repeat(s5_log_dt[i].reshape(2 * SG, 1), SH, axis=0))
        b_r = (jnp.transpose(s5_b_re[i], (0, 1, 3, 2)).reshape(2 * SG * SH, SP),
               jnp.transpose(s5_b_im[i], (0, 1, 3, 2)).reshape(2 * SG * SH, SP))
        pw_re, pw_im, bbr, bbi = s5_disc(*lam_l, *lam_r, *b_r)
        s5p.append(dict(
            lam_r=lam_r, b_r=b_r, pw_re=pw_re, pw_im=pw_im, pw_im_conj=-pw_im,
            bb_re=_block_diag(bbr.reshape(2, SG, SH, SP), False).astype(BF),
            bb_im=_block_diag(bbi.reshape(2, SG, SH, SP), False).astype(BF),
            ct_re=_block_diag(s5_c_re[i], True).astype(BF), ct_im=_block_diag(s5_c_im[i], True).astype(BF)))

    saved = []
    xcur = xin
    h = pro_fwd(xin[0], xin[1], norm_g[0].reshape(1, D), mods[0], "pro_fwd0")
    for l in range(nl):
        i = l // 2
        sv = dict(x=xcur, h=h)
        if l % 2 == 0:
            p = mm_nn(h, wein[i], f"in_proj{l}")
            sp = s5p[i]
            for dr, rev in ((0, False), (1, True)):
                sv[f"y{dr}"], sv[f"hpr{dr}"], sv[f"hpi{dr}"] = s5q_fwd(
                    p, sp["bb_re"], sp["bb_im"], sp["ct_re"], sp["ct_im"], sp["pw_re"], sp["pw_im"], dr, rev,
                    f"s5_fwd{l}_{dr}")
            mix = mix_fwd(p, sv["y0"], sv["y1"], gm_v_g[i].reshape(1, AW), gm_w_s[i].astype(BF),
                          gm_b_s[i].reshape(NGRP, CHUNK, 1), s5_d[i].reshape(1, AW), wglu[i],
                          s5_b_glu[i].reshape(1, AW), f"mix_fwd{l}")
            o = mm_nn(mix, weout[i], f"out_proj{l}")
        else:
            p = mm_nn(h, woin[i], f"in_proj{l}")
            sv["qkv"] = attn_prep(p, q_norm_g[i].reshape(1, HD), k_norm_g[i].reshape(1, HD), cos, sins, f"attn_prep{l}")
            sv["o_att"], mix, sv["lse"] = attn_fwd(sv["qkv"], p, f"attn_fwd{l}")
            o = mm_nn(mix, woout[i], f"out_proj{l}")
        sv.update(p=p, mix=mix, o=o)
        saved.append(sv)
        if l < nl - 1:
            xcur, h = res_pro_fwd(xcur, o, mods[l], True, norm_g[l + 1].reshape(1, D), mods[l + 1], f"res_pro_fwd{l}")
        else:
            xcur = res_fwd(xcur, o, mods[l], False, f"res_fwd{l}")

    loss_part, dx, d_final_g, do, dgt = final_loss(xcur, target, final_g.reshape(1, D), saved[-1]["o"], mods[-1])

    g = {}
    gbuf = {}
    d_norm_g, d_mod_lat, d_mod_ctx = [None] * nl, [None] * nl, [None] * nl
    for name in ("s5_w_glu", "gm_v_g", "gm_w_s", "gm_b_s", "s5_lam_re", "s5_lam_im",
                 "s5_log_dt", "s5_b_re", "s5_b_im", "s5_c_re", "s5_c_im", "s5_d", "s5_b_glu", "q_norm_g", "k_norm_g"):
        g[name] = [None, None]
    for l in reversed(range(nl)):
        i = l // 2
        sv = saved[l]
        w_out = weout[i] if l % 2 == 0 else woout[i]
        dmix = mm_nt(do, w_out, f"out_dgrad{l}")
        out_name, in_name = ("we_out", "we_in") if l % 2 == 0 else ("wo_out", "wo_in")
        gbuf[out_name] = mm_tn(sv["mix"], do, 1, f"out_wgrad{l}", slot=i, into=gbuf.get(out_name))
        if l % 2 == 0:
            sp = s5p[i]
            (dp, dy, g["gm_w_s"][i], dbs, dvg, dd, g["s5_w_glu"][i], dbg) = mix_bwd(
                sv["p"], sv["y0"], sv["y1"], dmix, gm_v_g[i].reshape(1, AW), gm_w_s[i].astype(BF),
                gm_b_s[i].reshape(NGRP, CHUNK, 1), s5_d[i].reshape(1, AW), wglu[i], s5_b_glu[i].reshape(1, AW),
                f"mix_bwd{l}")
            g["gm_b_s"][i], g["gm_v_g"][i] = dbs.reshape(NGRP, CHUNK), dvg.reshape(AW)
            g["s5_d"][i], g["s5_b_glu"][i] = dd.reshape(AW), dbg.reshape(AW)
            g["s5_w_glu"][i] = g["s5_w_glu"][i].reshape(4, AW // 4, AW)
            dxd, das_r, das_i, dbbs_r, dbbs_i, dcs_r, dcs_i = [], [], [], [], [], [], []
            for dr, rev in ((0, False), (1, True)):
                dxs_d, da_r, da_i, dbb_r, dbb_i, dc_r, dc_i = s5q_bwd(
                    sv["p"], sv[f"hpr{dr}"], sv[f"hpi{dr}"], dy, sp["bb_re"], sp["bb_im"], sp["ct_re"], sp["ct_im"],
                    sp["pw_re"], sp["pw_im_conj"], dr, rev, f"s5_bwd{l}_{dr}")
                dxd.append(dxs_d)
                das_r.append(jnp.repeat(da_r.reshape(SG, SP), SH, axis=0))
                das_i.append(jnp.repeat(da_i.reshape(SG, SP), SH, axis=0))
                dbbs_r.append(_diag_blocks(dbb_r).reshape(SG * SH, SP))
                dbbs_i.append(_diag_blocks(dbb_i).reshape(SG * SH, SP))
                dcs_r.append(_diag_blocks(dc_r))
                dcs_i.append(_diag_blocks(dc_i))
            cat = lambda parts: jnp.concatenate(parts, axis=0)
            dlr, dli, dldt, dbr, dbi = s5_param_bwd(*sp["lam_r"], *sp["b_r"], cat(das_r), cat(das_i),
                                                    cat(dbbs_r), cat(dbbs_i))
            g["s5_lam_re"][i], g["s5_lam_im"][i] = dlr.reshape(2, SG, SP), dli.reshape(2, SG, SP)
            g["s5_log_dt"][i] = dldt.reshape(2, SG)
            g["s5_b_re"][i] = jnp.transpose(dbr.reshape(2, SG, SH, SP), (0, 1, 3, 2))
            g["s5_b_im"][i] = jnp.transpose(dbi.reshape(2, SG, SH, SP), (0, 1, 3, 2))
            g["s5_c_re"][i], g["s5_c_im"][i] = jnp.stack(dcs_r), jnp.stack(dcs_i)
            dp = s5p_dx_sum(dy, s5_d[i].reshape(1, AW), dxd[0], dxd[1], dp, f"s5_dx_sum{l}")
            w_in = wein[i]
        else:
            dq, dp, dk, dv = attn_bwd(sv["qkv"], sv["p"], dmix, sv["o_att"], sv["lse"], f"attn_bwd{l}")
            dp, dqg, dkg = attn_prep_bwd(sv["p"], dq, dk, dv, q_norm_g[i].reshape(1, HD),
                                         k_norm_g[i].reshape(1, HD), cos, sins, dp, f"attn_prep_bwd{l}")
            g["q_norm_g"][i], g["k_norm_g"][i] = dqg.reshape(HD), dkg.reshape(HD)
            w_in = woin[i]
        dh = mm_nt(dp, w_in, f"in_dgrad{l}")
        gbuf[in_name] = mm_tn(sv["h"], dp, 4, f"in_wgrad{l}", slot=i, into=gbuf.get(in_name))
        dgt_l = dgt
        if l > 0:
            dx, dmod2, dng, do, dgt = pro_res_bwd(sv["x"], dh, dx, norm_g[l].reshape(1, D), mods[l],
                                                  saved[l - 1]["o"], mods[l - 1], f"pro_res_bwd{l}")
        else:
            dx, dmod2, dng = pro_bwd(xin[0], xin[1], dh, dx, norm_g[l].reshape(1, D), mods[l], f"pro_bwd{l}")
        d_norm_g[l] = dng.reshape(D)
        d_mod_ctx[l] = jnp.concatenate([dmod2[0, 0], dmod2[0, 1], dgt_l[0]])
        d_mod_lat[l] = jnp.concatenate([dmod2[1, 0], dmod2[1, 1], dgt_l[1]])
    g.update(gbuf)
    return loss_part, dx, g, d_norm_g, d_mod_lat, d_mod_ctx, d_final_g
```

```python
import math

import numpy as np
import jax
import jax.numpy as jnp
from jax import lax
from jax.experimental import pallas as pl
from jax.experimental.pallas import tpu as pltpu

F32 = jnp.float32
BF = jnp.bfloat16
MESH = pl.DeviceIdType.MESH

D = 1024
NC = 256
SEQ = 4096
GRID_W = 64
TM = 256
CHUNK = 128
EPS = 1e-6
HD = 128
NQ = 8
NKV = 2
ROPE_THETA = 10000.0
SG = 32
SP = 64
SH = 16
SW = SG * SP
GELU_K = math.sqrt(2.0 / math.pi)
GELU_C = 0.044715
VMEM_LIMIT_BYTES = 56 * 1024 * 1024

ADAM_LR = 0.001
ADAM_B1 = 0.9
ADAM_B2 = 0.999
ADAM_EPS = 1e-08
ADAM_WD = 0.01
ADAM_STEP = 10


def _call(body, *, name, out_shape, grid=None, in_specs=None, out_specs=None, scratch=()):
    kw = {}
    if grid is not None:
        kw["grid"] = grid
    if in_specs is not None:
        kw["in_specs"] = in_specs
    if out_specs is not None:
        kw["out_specs"] = out_specs
    return pl.pallas_call(
        body, name=name, out_shape=out_shape, scratch_shapes=list(scratch),
        compiler_params=pltpu.CompilerParams(vmem_limit_bytes=VMEM_LIMIT_BYTES), **kw)


def _dot(a, b, ca=1, cb=0):
    return lax.dot_general(a, b, (((ca,), (cb,)), ((), ())), preferred_element_type=F32)


def _sig(x):
    return 1.0 / (1.0 + jnp.exp(-x))


def _full(shape):
    n = len(shape)
    return pl.BlockSpec(shape, lambda *_: (0,) * n)


def _mm_rows(t, most=1088):
    for rows in (2176, 1088, 1024, 768, 544, 512, 256):
        if rows <= most and t % rows == 0:
            return rows
    raise ValueError(t)


def _layer_of(w):
    return w if isinstance(w, tuple) else (w[None], 0)


def mm_nn(a, w, name, out_dtype=F32):
    w4, layer = _layer_of(w)
    t, k = a.shape
    _, j, _, nb = w4.shape
    tr = _mm_rows(t, 2176)

    def body(a_ref, w_ref, o_ref):
        o_ref[...] = _dot(a_ref[...], w_ref[0, 0]).astype(o_ref.dtype)

    return _call(body, name=name, grid=(j, t // tr),
                 in_specs=[pl.BlockSpec((tr, k), lambda jj, i: (i, 0)),
                           pl.BlockSpec((1, 1, k, nb), lambda jj, i: (layer, jj, 0, 0))],
                 out_specs=pl.BlockSpec((tr, nb), lambda jj, i: (i, jj)),
                 out_shape=jax.ShapeDtypeStruct((t, j * nb), out_dtype))(a, w4)


def mm_nt(a, w, name, out_dtype=F32):
    w4, layer = _layer_of(w)
    t, _ = a.shape
    _, j, k, nb = w4.shape
    tr = _mm_rows(t, 2176 if j * nb <= 1024 else 1088)

    def body(a_ref, w_ref, o_ref):
        acc = _dot(a_ref[:, 0:nb], w_ref[0, 0], 1, 1)
        for jj in range(1, j):
            acc = acc + _dot(a_ref[:, jj * nb:(jj + 1) * nb], w_ref[0, jj], 1, 1)
        o_ref[...] = acc.astype(o_ref.dtype)

    return _call(body, name=name, grid=(t // tr,),
                 in_specs=[pl.BlockSpec((tr, j * nb), lambda i: (i, 0)),
                           pl.BlockSpec((1, j, k, nb), lambda i: (layer, 0, 0, 0))],
                 out_specs=pl.BlockSpec((tr, k), lambda i: (i, 0)),
                 out_shape=jax.ShapeDtypeStruct((t, k), out_dtype))(a, w4)


def mm_tn(a, b, j, name, slot=0, into=None):
    t, m = a.shape
    nb = b.shape[1] // j
    tr = _mm_rows(t, 2176)

    def body(a_ref, b_ref, *rest):
        o_ref = rest[-1]

        @pl.when(pl.program_id(1) == 0)
        def _():
            o_ref[...] = jnp.zeros_like(o_ref)
        o_ref[0, 0] += _dot(a_ref[...], b_ref[...], 0, 0)

    in_specs = [pl.BlockSpec((tr, m), lambda jj, i: (i, 0)), pl.BlockSpec((tr, nb), lambda jj, i: (i, jj))]
    args = [a, b]
    alias = {}
    if into is not None:
        in_specs.append(pl.BlockSpec(memory_space=pl.ANY))
        args.append(into)
        alias = {2: 0}
    return pl.pallas_call(
        body, name=name, grid=(j, t // tr), in_specs=in_specs,
        out_specs=pl.BlockSpec((1, 1, m, nb), lambda jj, i: (slot, jj, 0, 0)),
        out_shape=jax.ShapeDtypeStruct((2, j, m, nb), F32), input_output_aliases=alias,
        compiler_params=pltpu.CompilerParams(vmem_limit_bytes=VMEM_LIMIT_BYTES))(*args)


def _mod_rows(mod_ref, i):
    ctx = i == 0
    sh = jnp.where(ctx, mod_ref[0, 0:1, :], mod_ref[1, 0:1, :])
    sc = jnp.where(ctx, mod_ref[0, 1:2, :], mod_ref[1, 1:2, :])
    gt = jnp.where(ctx, mod_ref[0, 2:3, :], mod_ref[1, 2:3, :])
    return sh, sc, gt


def _split_specs():
    return [pl.BlockSpec((TM, D), lambda i: (0, 0)), pl.BlockSpec((TM, D), lambda i: (jnp.maximum(i - 1, 0), 0))]


def _split_tile(c_ref, l_ref, i):
    return jnp.where(i == 0, c_ref[...], l_ref[...])


def pro_fwd(ctx, lat, g, mod, name):
    t = ctx.shape[0] + lat.shape[0]

    def body(c_ref, l_ref, g_ref, mod_ref, h_ref):
        i = pl.program_id(0)
        sh, sc, _ = _mod_rows(mod_ref, i)
        xv = _split_tile(c_ref, l_ref, i)
        r = lax.rsqrt(jnp.mean(xv * xv, axis=-1, keepdims=True) + EPS)
        h_ref[...] = ((xv * r) * g_ref[...] * (1.0 + sc) + sh).astype(BF)

    return _call(body, name=name, grid=(t // TM,),
                 in_specs=_split_specs() + [_full((1, D)), _full((2, 3, D))],
                 out_specs=pl.BlockSpec((TM, D), lambda i: (i, 0)),
                 out_shape=jax.ShapeDtypeStruct((t, D), BF))(ctx, lat, g, mod)


def pro_bwd(ctx, lat, dh, dxn, g, mod, name):
    t = ctx.shape[0] + lat.shape[0]

    def body(c_ref, l_ref, dh_ref, dxn_ref, g_ref, mod_ref, dx_ref, dmod_ref, dg_ref):
        i = pl.program_id(0)

        @pl.when(i == 0)
        def _():
            dmod_ref[...] = jnp.zeros_like(dmod_ref)
            dg_ref[...] = jnp.zeros_like(dg_ref)

        _, sc, _ = _mod_rows(mod_ref, i)
        xv = _split_tile(c_ref, l_ref, i)
        gv = g_ref[...]
        r = lax.rsqrt(jnp.mean(xv * xv, axis=-1, keepdims=True) + EPS)
        xn = xv * r
        dh_v = dh_ref[...]
        e = dh_v * (1.0 + sc)
        dsh = jnp.sum(dh_v, axis=0, keepdims=True)
        dsc = jnp.sum(dh_v * xn * gv, axis=0, keepdims=True)
        dg_ref[...] += jnp.sum(e * xn, axis=0, keepdims=True)
        dxh = e * gv

        @pl.when(i == 0)
        def _():
            dmod_ref[0, 0:1, :] += dsh
            dmod_ref[0, 1:2, :] += dsc

        @pl.when(i > 0)
        def _():
            dx_ref[...] = dxn_ref[...] + r * (dxh - xn * jnp.mean(dxh * xn, axis=-1, keepdims=True))
            dmod_ref[1, 0:1, :] += dsh
            dmod_ref[1, 1:2, :] += dsc

    tile = pl.BlockSpec((TM, D), lambda i: (i, 0))
    return _call(body, name=name, grid=(t // TM,),
                 in_specs=_split_specs() + [tile, tile, _full((1, D)), _full((2, 3, D))],
                 out_specs=[_split_specs()[1], _full((2, 2, D)), _full((1, D))],
                 out_shape=[jax.ShapeDtypeStruct(lat.shape, F32), jax.ShapeDtypeStruct((2, 2, D), F32),
                            jax.ShapeDtypeStruct((1, D), F32)])(ctx, lat, dh, dxn, g, mod)


def _mod_rows_at(mod_ref, i, tr):
    isctx = i * tr + lax.broadcasted_iota(jnp.int32, (tr, 1), 0) < NC
    pick = lambda k: jnp.where(isctx, mod_ref[0, k:k + 1, :], mod_ref[1, k:k + 1, :])
    return pick(0), pick(1), pick(2), isctx


def res_pro_fwd(x, o, mod, update_ctx, g_next, mod_next, name):
    split = isinstance(x, tuple)
    xs = list(x) if split else [x]
    t = o.shape[0]
    tr = TM if split else _mm_rows(t)

    def body(*refs):
        x_refs = refs[:len(xs)]
        o_ref, mod_ref, g_ref, modn_ref, y_ref, h_ref = refs[len(xs):]
        i = pl.program_id(0)
        _, _, gt, isctx = _mod_rows_at(mod_ref, i, tr)
        xv = _split_tile(x_refs[0], x_refs[1], i) if split else x_refs[0][...]
        xn = xv + gt * o_ref[...]
        if not update_ctx:
            xn = jnp.where(isctx, xv, xn)
        y_ref[...] = xn
        sh, sc, _, _ = _mod_rows_at(modn_ref, i, tr)
        r = lax.rsqrt(jnp.mean(xn * xn, axis=-1, keepdims=True) + EPS)
        h_ref[...] = ((xn * r) * g_ref[...] * (1.0 + sc) + sh).astype(BF)

    tile = pl.BlockSpec((tr, D), lambda i: (i, 0))
    return _call(body, name=name, grid=(t // tr,),
                 in_specs=(_split_specs() if split else [tile]) + [tile, _full((2, 3, D)), _full((1, D)), _full((2, 3, D))],
                 out_specs=[tile, tile],
                 out_shape=[jax.ShapeDtypeStruct((t, D), F32), jax.ShapeDtypeStruct((t, D), BF)])(
        *xs, o, mod, g_next, mod_next)


def pro_res_bwd(x, dh, dxn, g, mod, o_prev, mod_prev, name):
    t = x.shape[0]
    tr = _mm_rows(t, 544)

    def body(x_ref, dh_ref, dxn_ref, g_ref, mod_ref, o_ref, modp_ref, dx_ref, dmod_ref, dg_ref, do_ref, dgt_ref):
        i = pl.program_id(0)

        @pl.when(i == 0)
        def _():
            dmod_ref[...] = jnp.zeros_like(dmod_ref)
            dg_ref[...] = jnp.zeros_like(dg_ref)
            dgt_ref[...] = jnp.zeros_like(dgt_ref)

        _, sc, _, isctx = _mod_rows_at(mod_ref, i, tr)
        xv = x_ref[...]
        gv = g_ref[...]
        r = lax.rsqrt(jnp.mean(xv * xv, axis=-1, keepdims=True) + EPS)
        xn = xv * r
        dh_v = dh_ref[...]
        e = dh_v * (1.0 + sc)
        dsc_rows = dh_v * xn * gv
        dg_ref[...] += jnp.sum(e * xn, axis=0, keepdims=True)
        dxh = e * gv
        dx = dxn_ref[...] + r * (dxh - xn * jnp.mean(dxh * xn, axis=-1, keepdims=True))
        dx_ref[...] = dx
        _, _, gtp, _ = _mod_rows_at(modp_ref, i, tr)
        do_ref[...] = (gtp * dx).astype(BF)
        dgt_rows = dx * o_ref[...]
        dmod_ref[1, 0:1, :] += jnp.sum(dh_v, axis=0, keepdims=True)
        dmod_ref[1, 1:2, :] += jnp.sum(dsc_rows, axis=0, keepdims=True)
        dgt_ref[1:2, :] += jnp.sum(dgt_rows, axis=0, keepdims=True)

        @pl.when(i * tr < NC)
        def _():
            for ref, rows in ((dmod_ref.at[:, 0], dh_v), (dmod_ref.at[:, 1], dsc_rows), (dgt_ref, dgt_rows)):
                part = jnp.sum(jnp.where(isctx, rows, 0.0), axis=0, keepdims=True)
                ref[0:1, :] += part
                ref[1:2, :] += -part

    tile = pl.BlockSpec((tr, D), lambda i: (i, 0))
    return _call(body, name=name, grid=(t // tr,),
                 in_specs=[tile, tile, tile, _full((1, D)), _full((2, 3, D)), tile, _full((2, 3, D))],
                 out_specs=[tile, _full((2, 2, D)), _full((1, D)), tile, _full((2, D))],
                 out_shape=[jax.ShapeDtypeStruct((t, D), F32), jax.ShapeDtypeStruct((2, 2, D), F32),
                            jax.ShapeDtypeStruct((1, D), F32), jax.ShapeDtypeStruct((t, D), BF),
                            jax.ShapeDtypeStruct((2, D), F32)])(x, dh, dxn, g, mod, o_prev, mod_prev)


def final_loss(x, target, g, o_last, mod_last):
    t = x.shape[0]

    def body(x_ref, t_ref, g_ref, o_ref, modp_ref, loss_ref, dx_ref, dg_ref, do_ref, dgt_ref):
        i = pl.program_id(0)

        @pl.when(i == 0)
        def _():
            loss_ref[...] = jnp.zeros_like(loss_ref)
            dg_ref[...] = jnp.zeros_like(dg_ref)
            dx_ref[...] = jnp.zeros_like(dx_ref)
            do_ref[...] = jnp.zeros_like(do_ref)
            dgt_ref[...] = jnp.zeros_like(dgt_ref)

        @pl.when(i > 0)
        def _():
            xv = x_ref[...] + modp_ref[1, 2:3, :] * o_ref[...]
            gv = g_ref[...]
            r = lax.rsqrt(jnp.mean(xv * xv, axis=-1, keepdims=True) + EPS)
            xn = xv * r
            err = xn * gv - t_ref[...]
            loss_ref[...] += (0.5 / D) * jnp.sum(jnp.sum(err * err, axis=1, keepdims=True), axis=0, keepdims=True)
            dy = err * (1.0 / D)
            dg_ref[...] += jnp.sum(dy * xn, axis=0, keepdims=True)
            dxh = dy * gv
            dx = r * (dxh - xn * jnp.mean(dxh * xn, axis=-1, keepdims=True))
            dx_ref[...] = dx
            do_ref[...] = (modp_ref[1, 2:3, :] * dx).astype(BF)
            dgt_ref[1:2, :] += jnp.sum(dx * o_ref[...], axis=0, keepdims=True)

    tile = pl.BlockSpec((TM, D), lambda i: (i, 0))
    return _call(body, name="final_loss", grid=(t // TM,),
                 in_specs=[tile, pl.BlockSpec((TM, D), lambda i: (jnp.maximum(i - 1, 0), 0)), _full((1, D)), tile,
                           _full((2, 3, D))],
                 out_specs=[_full((1, 1)), tile, _full((1, D)), tile, _full((2, D))],
                 out_shape=[jax.ShapeDtypeStruct((1, 1), F32), jax.ShapeDtypeStruct((t, D), F32),
                            jax.ShapeDtypeStruct((1, D), F32), jax.ShapeDtypeStruct((t, D), BF),
                            jax.ShapeDtypeStruct((2, D), F32)])(x, target, g, o_last, mod_last)


def ada_fwd(cond, w_mod, b_mod):
    nl, _, nw = w_mod.shape

    def body(c_ref, w_ref, b_ref, o_ref):
        cv = c_ref[...]
        s = (cv * _sig(cv)).astype(BF)
        o_ref[0] = _dot(s, w_ref[0].astype(BF)) + b_ref[0]

    return _call(body, name="ada_fwd", grid=(nl,),
                 in_specs=[_full((8, D)), pl.BlockSpec((1, D, nw), lambda l: (l, 0, 0)),
                           pl.BlockSpec((1, 1, nw), lambda l: (l, 0, 0))],
                 out_specs=pl.BlockSpec((1, 8, nw), lambda l: (l, 0, 0)),
                 out_shape=jax.ShapeDtypeStruct((nl, 8, nw), F32))(cond, w_mod, b_mod)


def ada_bwd(cond, dm, w_mod):
    nl, _, nw = w_mod.shape

    def body(c_ref, dm_ref, w_ref, gw_ref, dcc_ref, dc_ref):
        l = pl.program_id(0)

        @pl.when(l == 0)
        def _():
            dc_ref[...] = jnp.zeros_like(dc_ref)

        cv = c_ref[...]
        sg = _sig(cv)
        s = (cv * sg).astype(BF)
        dmv = dm_ref[0].astype(BF)
        gw_ref[0] = _dot(s, dmv, 0, 0)
        dc_ref[...] += _dot(dmv, w_ref[0].astype(BF), 1, 1)

        @pl.when(l == nl - 1)
        def _():
            dcond = dc_ref[...] * (sg * (1.0 + cv * (1.0 - sg)))
            dcc_ref[...] = jnp.sum(dcond[4:8], axis=0, keepdims=True)

    return _call(body, name="ada_bwd", grid=(nl,),
                 in_specs=[_full((8, D)), pl.BlockSpec((1, 8, nw), lambda l: (l, 0, 0)),
                           pl.BlockSpec((1, D, nw), lambda l: (l, 0, 0))],
                 out_specs=[pl.BlockSpec((1, D, nw), lambda l: (l, 0, 0)), _full((1, D))],
                 out_shape=[jax.ShapeDtypeStruct((nl, D, nw), F32), jax.ShapeDtypeStruct((1, D), F32)],
                 scratch=[pltpu.VMEM((8, D), F32)])(cond, dm, w_mod)


def add2(a, b, name):
    def body(a_ref, b_ref, o_ref):
        o_ref[...] = a_ref[...] + b_ref[...]

    return _call(body, name=name, out_shape=jax.ShapeDtypeStruct(a.shape, a.dtype))(a, b)


AW = 512
NGRP = 4


def Y4_SPEC():
    return pl.BlockSpec((AW // 128, TM, 128), lambda i: (0, i, 0))


def _cat_lanes(ref):
    return jnp.concatenate([ref[q] for q in range(ref.shape[0])], axis=1)


def _gelu(y):
    t = jnp.tanh(GELU_K * (y + GELU_C * y * y * y))
    return 0.5 * y * (1.0 + t), t


def _layer_norm_stats(v):
    mu = jnp.mean(v, axis=-1, keepdims=True)
    vc = v - mu
    rstd = lax.rsqrt(jnp.mean(vc * vc, axis=-1, keepdims=True) + EPS)
    return vc * rstd, rstd


def _spatial_mix(vn_ref, ws_ref, bs_ref, mixed_ref):
    for ch in range(TM // CHUNK):
        rows = slice(ch * CHUNK, (ch + 1) * CHUNK)
        for g in range(NGRP):
            cols = slice(g * CHUNK, (g + 1) * CHUNK)
            mixed_ref[rows, cols] = _dot(ws_ref[g], vn_ref[rows, cols]) + bs_ref[g]


def mix_fwd(p, yf, yb, vg, ws, bs, dsk, wglu, bglu, name):
    t = p.shape[0]
    wglu, glu_layer = _layer_of(wglu)

    def body(p_ref, yf_ref, yb_ref, vg_ref, ws_ref, bs_ref, d_ref, wg_ref, bg_ref, o_ref, vn_ref, mixed_ref):
        vhat, _ = _layer_norm_stats(p_ref[:, AW:2 * AW])
        vn_ref[...] = (vhat * vg_ref[...]).astype(BF)
        _spatial_mix(vn_ref, ws_ref, bs_ref, mixed_ref)
        ga = p_ref[:, 2 * AW:3 * AW]
        o_ref[:, 0:AW] = (p_ref[:, 0:AW] * mixed_ref[...] * (ga * _sig(ga))).astype(BF)
        y = _cat_lanes(yf_ref) + _cat_lanes(yb_ref) + d_ref[...] * p_ref[:, 3 * AW:4 * AW]
        y2, _ = _gelu(y)
        z = _dot(y2.astype(BF), wg_ref[0]) + bg_ref[...]
        gb = p_ref[:, 4 * AW:5 * AW]
        o_ref[:, AW:2 * AW] = (y2 * _sig(z) * (gb * _sig(gb))).astype(BF)

    tile = lambda w: pl.BlockSpec((TM, w), lambda i: (i, 0))
    return _call(body, name=name, grid=(t // TM,),
                 in_specs=[tile(5 * AW), Y4_SPEC(), Y4_SPEC(), _full((1, AW)), _full((NGRP, CHUNK, CHUNK)),
                           _full((NGRP, CHUNK, 1)), _full((1, AW)),
                           pl.BlockSpec((1, AW, AW), lambda i: (glu_layer, 0, 0)), _full((1, AW))],
                 out_specs=tile(2 * AW), out_shape=jax.ShapeDtypeStruct((t, 2 * AW), BF),
                 scratch=[pltpu.VMEM((TM, AW), BF), pltpu.VMEM((TM, AW), F32)])(p, yf, yb, vg, ws, bs, dsk, wglu, bglu)


def mix_bwd(p, yf, yb, dmix, vg, ws, bs, dsk, wglu, bglu, name):
    t = p.shape[0]
    wglu, glu_layer = _layer_of(wglu)

    def body(p_ref, yf_ref, yb_ref, dm_ref, vg_ref, ws_ref, bs_ref, d_ref, wg_ref, bg_ref,
             dpa_ref, dy_ref, dws_ref, dbs_ref, dvg_ref, dd_ref, dwg_ref, dbg_ref,
             vn_ref, mixed_ref, dmx_ref, dvn_ref):
        @pl.when(pl.program_id(0) == 0)
        def _():
            for r in (dws_ref, dbs_ref, dvg_ref, dd_ref, dwg_ref, dbg_ref):
                r[...] = jnp.zeros_like(r)

        vhat, rstd = _layer_norm_stats(p_ref[:, AW:2 * AW])
        vgv = vg_ref[...]
        vn_ref[...] = (vhat * vgv).astype(BF)
        _spatial_mix(vn_ref, ws_ref, bs_ref, mixed_ref)
        u = p_ref[:, 0:AW]
        ga = p_ref[:, 2 * AW:3 * AW]
        sga = _sig(ga)
        dya = dm_ref[:, 0:AW]
        mixed = mixed_ref[...]
        dpa_ref[:, 0:AW] = (dya * mixed * (ga * sga)).astype(BF)
        dpa_ref[:, 2 * AW:3 * AW] = (dya * u * mixed * (sga * (1.0 + ga * (1.0 - sga)))).astype(BF)
        dmx_ref[...] = dya * u * (ga * sga)
        for ch in range(TM // CHUNK):
            rows = slice(ch * CHUNK, (ch + 1) * CHUNK)
            for g in range(NGRP):
                cols = slice(g * CHUNK, (g + 1) * CHUNK)
                dmx = dmx_ref[rows, cols]
                dmxb = dmx.astype(BF)
                dws_ref[g] += _dot(dmxb, vn_ref[rows, cols], 1, 1)
                dbs_ref[g] += jnp.sum(dmx, axis=1, keepdims=True)
                dvn_ref[rows, cols] = _dot(ws_ref[g], dmxb, 0, 0)
        dvn = dvn_ref[...]
        dvg_ref[...] += jnp.sum(dvn * vhat, axis=0, keepdims=True)
        dvh = dvn * vgv
        dpa_ref[:, AW:2 * AW] = (rstd * (dvh - jnp.mean(dvh, axis=-1, keepdims=True)
                                         - vhat * jnp.mean(dvh * vhat, axis=-1, keepdims=True))).astype(BF)

        xs = p_ref[:, 3 * AW:4 * AW]
        y = _cat_lanes(yf_ref) + _cat_lanes(yb_ref) + d_ref[...] * xs
        y2, th = _gelu(y)
        y2b = y2.astype(BF)
        z = _dot(y2b, wg_ref[0]) + bg_ref[...]
        sz = _sig(z)
        gb = p_ref[:, 4 * AW:5 * AW]
        sgb = _sig(gb)
        dyb = dm_ref[:, AW:2 * AW]
        dpa_ref[:, 4 * AW:5 * AW] = (dyb * (y2 * sz) * (sgb * (1.0 + gb * (1.0 - sgb)))).astype(BF)
        dy3 = dyb * (gb * sgb)
        dz = dy3 * y2 * sz * (1.0 - sz)
        dzb = dz.astype(BF)
        dwg_ref[...] += _dot(y2b, dzb, 0, 0)
        dbg_ref[...] += jnp.sum(dz, axis=0, keepdims=True)
        dy2 = dy3 * sz + _dot(dzb, wg_ref[0], 1, 1)
        dgelu = 0.5 * (1.0 + th) + 0.5 * y * (1.0 - th * th) * GELU_K * (1.0 + 3.0 * GELU_C * y * y)
        dy = dy2 * dgelu
        dd_ref[...] += jnp.sum(dy * xs, axis=0, keepdims=True)
        dy_ref[...] = dy

    tile = lambda w: pl.BlockSpec((TM, w), lambda i: (i, 0))
    return _call(body, name=name, grid=(t // TM,),
                 in_specs=[tile(5 * AW), Y4_SPEC(), Y4_SPEC(), tile(2 * AW), _full((1, AW)), _full((NGRP, CHUNK, CHUNK)),
                           _full((NGRP, CHUNK, 1)), _full((1, AW)),
                           pl.BlockSpec((1, AW, AW), lambda i: (glu_layer, 0, 0)), _full((1, AW))],
                 out_specs=[tile(5 * AW), tile(AW), _full((NGRP, CHUNK, CHUNK)), _full((NGRP, CHUNK, 1)),
                            _full((1, AW)), _full((1, AW)), _full((AW, AW)), _full((1, AW))],
                 out_shape=[jax.ShapeDtypeStruct((t, 5 * AW), BF),
                            jax.ShapeDtypeStruct((t, AW), F32), jax.ShapeDtypeStruct((NGRP, CHUNK, CHUNK), F32),
                            jax.ShapeDtypeStruct((NGRP, CHUNK, 1), F32), jax.ShapeDtypeStruct((1, AW), F32),
                            jax.ShapeDtypeStruct((1, AW), F32), jax.ShapeDtypeStruct((AW, AW), F32),
                            jax.ShapeDtypeStruct((1, AW), F32)],
                 scratch=[pltpu.VMEM((TM, AW), BF), pltpu.VMEM((TM, AW), F32), pltpu.VMEM((TM, AW), F32),
                          pltpu.VMEM((TM, AW), F32)])(p, yf, yb, dmix, vg, ws, bs, dsk, wglu, bglu)


LN = 512
NBLK = SW // LN
UB = AW // NBLK
SCAN_R = 32
SCAN_G = TM // SCAN_R
PW_ROWS = SCAN_R
POW_EXP = list(range(1, SCAN_R + 1))


def s5_disc(lam_re, lam_im, dt, lam_re_r, lam_im_r, dt_r, b_re, b_im):
    nexp = jnp.asarray(np.array(POW_EXP, np.float32).reshape(PW_ROWS, 1))

    def body(n_ref, lr_ref, li_ref, dt_ref, lrr_ref, lir_ref, dtr_ref, br_ref, bi_ref,
             pr_ref, pi_ref, bbr_ref, bbi_ref):
        for dr in range(2):
            dtl = jnp.exp(dt_ref[dr:dr + 1, :])
            zr = lr_ref[dr:dr + 1, :] * dtl
            zi = li_ref[dr:dr + 1, :] * dtl
            mag = jnp.exp(n_ref[...] * zr)
            ang = n_ref[...] * zi
            pr_ref[dr] = mag * jnp.cos(ang)
            pi_ref[dr] = mag * jnp.sin(ang)
        lr, li, dtv = lrr_ref[...], lir_ref[...], jnp.exp(dtr_ref[...])
        mag = jnp.exp(lr * dtv)
        nr = mag * jnp.cos(li * dtv) - 1.0
        ni = mag * jnp.sin(li * dtv)
        den = lr * lr + li * li
        fr = (nr * lr + ni * li) / den
        fi = (ni * lr - nr * li) / den
        bbr_ref[...] = fr * br_ref[...] - fi * bi_ref[...]
        bbi_ref[...] = fr * bi_ref[...] + fi * br_ref[...]

    rows = lam_re_r.shape[0]
    return _call(body, name="s5_disc",
                 out_shape=[jax.ShapeDtypeStruct((2, PW_ROWS, SW), F32), jax.ShapeDtypeStruct((2, PW_ROWS, SW), F32),
                            jax.ShapeDtypeStruct((rows, SP), F32), jax.ShapeDtypeStruct((rows, SP), F32)])(
        nexp, lam_re, lam_im, dt, lam_re_r, lam_im_r, dt_r, b_re, b_im)


def s5_param_bwd(lam_re_r, lam_im_r, dt_r, b_re, b_im, da_re, da_im, dbb_re, dbb_im):
    rows = lam_re_r.shape[0]
    ng = rows // SH
    seg = jnp.asarray(np.kron(np.eye(ng, dtype=np.float32), np.ones((1, SH), np.float32)))

    def body(seg_ref, lr_ref, li_ref, dt_ref, br_ref, bi_ref, dar_ref, dai_ref, dbbr_ref, dbbi_ref,
             dlr_ref, dli_ref, ddt_ref, dbr_ref, dbi_ref):
        lr, li, dtv = lr_ref[...], li_ref[...], jnp.exp(dt_ref[...])
        mag = jnp.exp(lr * dtv)
        lbr = mag * jnp.cos(li * dtv)
        lbi = mag * jnp.sin(li * dtv)
        den = lr * lr + li * li
        nr, ni = lbr - 1.0, lbi
        fr = (nr * lr + ni * li) / den
        fi = (ni * lr - nr * li) / den
        br, bi = br_ref[...], bi_ref[...]
        gbr, gbi = dbbr_ref[...], dbbi_ref[...]
        dbr_ref[...] = gbr * fr + gbi * fi
        dbi_ref[...] = gbi * fr - gbr * fi
        gfr = gbr * br + gbi * bi
        gfi = gbi * br - gbr * bi
        ilr, ili = lr / den, -li / den
        gnr = gfr * ilr + gfi * ili
        gni = gfi * ilr - gfr * ili
        qr = -(fr * ilr - fi * ili)
        qi = -(fr * ili + fi * ilr)
        glr = gfr * qr + gfi * qi
        gli = gfi * qr - gfr * qi
        first = (lax.broadcasted_iota(jnp.int32, (rows, 1), 0) % SH) == 0
        glbr = gnr + jnp.where(first, dar_ref[...], 0.0)
        glbi = gni + jnp.where(first, dai_ref[...], 0.0)
        gzr = glbr * lbr + glbi * lbi
        gzi = glbi * lbr - glbr * lbi
        glr = glr + gzr * dtv
        gli = gli + gzi * dtv
        gdt = (gzr * lr + gzi * li) * dtv
        hi = lax.Precision.HIGHEST
        sg = seg_ref[...]
        dlr_ref[...] = jnp.dot(sg, glr, precision=hi, preferred_element_type=F32)
        dli_ref[...] = jnp.dot(sg, gli, precision=hi, preferred_element_type=F32)
        ddt_ref[...] = jnp.sum(jnp.dot(sg, gdt, precision=hi, preferred_element_type=F32), axis=1, keepdims=True)

    return _call(body, name="s5_param_bwd",
                 out_shape=[jax.ShapeDtypeStruct((ng, SP), F32), jax.ShapeDtypeStruct((ng, SP), F32),
                            jax.ShapeDtypeStruct((ng, 1), F32), jax.ShapeDtypeStruct((rows, SP), F32),
                            jax.ShapeDtypeStruct((rows, SP), F32)])(
        seg, lam_re_r, lam_im_r, dt_r, b_re, b_im, da_re, da_im, dbb_re, dbb_im)


def _tile_order(kind, nt):
    if kind == "fwd":
        return lambda i: i
    if kind == "bwd":
        return lambda i: jnp.where(i == 0, 0, nt - i)
    if kind == "fwd_adj":
        return lambda i: nt - 1 - i
    if kind == "bwd_adj":
        return lambda i: jnp.where(i == nt - 1, 0, i + 1)
    raise ValueError(kind)


XS_BLK = 3 * AW // 128


def _load_perm(refs):
    return jnp.concatenate(
        [jnp.concatenate([ref[pl.ds(r, SCAN_G, stride=SCAN_R), :] for ref in refs], axis=1) for r in range(SCAN_R)],
        axis=0)


def _store_perm(out_ref, val):
    for r in range(SCAN_R):
        for q in range(AW // 128):
            out_ref[q, pl.ds(r, SCAN_G, stride=SCAN_R), :] = val[r * SCAN_G:(r + 1) * SCAN_G, q * 128:(q + 1) * 128]


def _scan2(br_ref, bi_ref, or_ref, oi_ref, h_off, cin_off, er_ref, ei_ref, cr_ref, ci_ref, pr_ref, pi_ref, reverse,
           corr=None):
    gpt = SCAN_G
    nr = SCAN_R
    offsets = list(range(nr))[::-1] if reverse else list(range(nr))
    blocks = [slice(b * LN, (b + 1) * LN) for b in range(NBLK)]
    slab = lambda r: slice(r * gpt, (r + 1) * gpt)
    a1 = [(pr_ref[0:1, c], pi_ref[0:1, c]) for c in blocks]
    x = [None] * NBLK
    for r in offsets:
        for b, c in enumerate(blocks):
            if x[b] is None:
                x[b] = (br_ref[slab(r), c], bi_ref[slab(r), c])
            else:
                (ar, ai), (xr, xi) = a1[b], x[b]
                x[b] = (br_ref[slab(r), c] + ar * xr - ai * xi, bi_ref[slab(r), c] + ar * xi + ai * xr)
    an = [(pr_ref[nr - 1:nr, c], pi_ref[nr - 1:nr, c]) for c in blocks]
    k = [(cr_ref[:, c], ci_ref[:, c]) for c in blocks]
    for g in (range(gpt - 1, -1, -1) if reverse else range(gpt)):
        for b, c in enumerate(blocks):
            (ar, ai), (kr, ki), (xr, xi) = an[b], k[b], x[b]
            er_ref[g:g + 1, c] = kr
            ei_ref[g:g + 1, c] = ki
            k[b] = (xr[g:g + 1, :] + ar * kr - ai * ki, xi[g:g + 1, :] + ar * ki + ai * kr)
    for b, c in enumerate(blocks):
        cr_ref[:, c] = k[b][0]
        ci_ref[:, c] = k[b][1]
        x[b] = (er_ref[:, c], ei_ref[:, c])
        if cin_off is not None:
            or_ref[cin_off:cin_off + gpt, c] = x[b][0]
            oi_ref[cin_off:cin_off + gpt, c] = x[b][1]
    acc = [None] * NBLK
    for r in offsets:
        for b, c in enumerate(blocks):
            (ar, ai), (xr, xi) = a1[b], x[b]
            x[b] = (br_ref[slab(r), c] + ar * xr - ai * xi, bi_ref[slab(r), c] + ar * xi + ai * xr)
            or_ref[h_off + r * gpt:h_off + (r + 1) * gpt, c] = x[b][0]
            oi_ref[h_off + r * gpt:h_off + (r + 1) * gpt, c] = x[b][1]
            if corr is not None:
                wr_ref, wi_ref, w_off = corr[:3]
                wr, wi = wr_ref[w_off + r * gpt:w_off + (r + 1) * gpt, c], wi_ref[w_off + r * gpt:w_off + (r + 1) * gpt, c]
                pr_, pi_ = x[b][0] * wr + x[b][1] * wi, x[b][1] * wr - x[b][0] * wi
                acc[b] = (pr_, pi_) if acc[b] is None else (acc[b][0] + pr_, acc[b][1] + pi_)
    if corr is not None:
        sr_ref, si_ref = corr[3:]
        for b, c in enumerate(blocks):
            sr_ref[:, c] += jnp.sum(acc[b][0], axis=0, keepdims=True)
            si_ref[:, c] += jnp.sum(acc[b][1], axis=0, keepdims=True)


HS_ROWS = TM + SCAN_G


def _hs_offsets(reverse):
    return (0, SCAN_G) if reverse else (SCAN_G, 0)


def _dir_spec(dr, shape):
    return pl.BlockSpec((1,) + shape, lambda i: (dr,) + (0,) * len(shape))


def s5q_fwd(p, bb_re, bb_im, ct_re, ct_im, pw_re, pw_im, dr, reverse, name):
    t = p.shape[0]
    nt = t // TM
    order = _tile_order("bwd" if reverse else "fwd", nt)
    nq = AW // 128
    h_off, p_off = _hs_offsets(reverse)

    def body(*refs):
        x_refs = refs[:nq]
        bbr_ref, bbi_ref, ctr_ref, cti_ref, pr_ref, pi_ref = [r.at[0] for r in refs[nq:nq + 6]]
        y_ref, hsr_ref, hsi_ref = refs[nq + 6:nq + 9]
        br_ref, bi_ref, er_ref, ei_ref, cr_ref, ci_ref = refs[nq + 9:]

        @pl.when(pl.program_id(0) == 0)
        def _():
            cr_ref[...] = jnp.zeros_like(cr_ref)
            ci_ref[...] = jnp.zeros_like(ci_ref)

        xb = _load_perm(x_refs).astype(BF)
        for j in range(NBLK):
            cols = slice(j * LN, (j + 1) * LN)
            br_ref[:, cols] = _dot(xb[:, j * UB:(j + 1) * UB], bbr_ref[j])
            bi_ref[:, cols] = _dot(xb[:, j * UB:(j + 1) * UB], bbi_ref[j])
        _scan2(br_ref, bi_ref, hsr_ref, hsi_ref, h_off, TM if reverse else 0, er_ref, ei_ref, cr_ref, ci_ref,
               pr_ref, pi_ref, reverse)
        y = jnp.concatenate(
            [_dot(hsr_ref[h_off:h_off + TM, j * LN:(j + 1) * LN].astype(BF), ctr_ref[j])
             - _dot(hsi_ref[h_off:h_off + TM, j * LN:(j + 1) * LN].astype(BF), cti_ref[j]) for j in range(NBLK)], axis=1)
        _store_perm(y_ref, y)

    state = lambda: pl.BlockSpec((HS_ROWS, SW), lambda i: (order(i), 0))
    xspec = lambda q: pl.BlockSpec((TM, 128), lambda i: (order(i), XS_BLK + q))
    return _call(body, name=name, grid=(nt,),
                 in_specs=[xspec(q) for q in range(nq)]
                 + [_dir_spec(dr, (NBLK, UB, LN)), _dir_spec(dr, (NBLK, UB, LN)), _dir_spec(dr, (NBLK, LN, UB)),
                    _dir_spec(dr, (NBLK, LN, UB)), _dir_spec(dr, (PW_ROWS, SW)), _dir_spec(dr, (PW_ROWS, SW))],
                 out_specs=[pl.BlockSpec((nq, TM, 128), lambda i: (0, order(i), 0)), state(), state()],
                 out_shape=[jax.ShapeDtypeStruct((nq, t, 128), F32), jax.ShapeDtypeStruct((nt * HS_ROWS, SW), F32),
                            jax.ShapeDtypeStruct((nt * HS_ROWS, SW), F32)],
                 scratch=[pltpu.VMEM((TM, SW), F32), pltpu.VMEM((TM, SW), F32),
                          pltpu.VMEM((SCAN_G, SW), F32), pltpu.VMEM((SCAN_G, SW), F32),
                          pltpu.VMEM((1, SW), F32), pltpu.VMEM((1, SW), F32)])(
        *([p] * nq), bb_re, bb_im, ct_re, ct_im, pw_re, pw_im)


def s5q_bwd(p, hs_re, hs_im, dy, bb_re, bb_im, ct_re, ct_im, pw_re, pw_im_conj, dr, reverse, name):
    t = p.shape[0]
    nt = t // TM
    order = _tile_order("bwd_adj" if reverse else "fwd_adj", nt)
    nq = AW // 128
    h_off, p_off = _hs_offsets(reverse)

    def body(*refs):
        x_refs, dy_refs = refs[:nq], refs[nq:2 * nq]
        (hsr_ref, hsi_ref, bbr_ref, bbi_ref, ctr_ref, cti_ref, pr_ref, pi_ref,
         dx_ref, dar_ref, dai_ref, dbbr_ref, dbbi_ref, dcr_ref, dci_ref,
         qr_ref, qi_ref, gr_ref, gi_ref, er_ref, ei_ref, cr_ref, ci_ref) = refs[2 * nq:]
        bbr_ref, bbi_ref, ctr_ref, cti_ref, pr_ref, pi_ref = [
            r.at[0] for r in (bbr_ref, bbi_ref, ctr_ref, cti_ref, pr_ref, pi_ref)]

        @pl.when(pl.program_id(0) == 0)
        def _():
            for r in (cr_ref, ci_ref, dar_ref, dai_ref, dbbr_ref, dbbi_ref, dcr_ref, dci_ref):
                r[...] = jnp.zeros_like(r)

        xb = _load_perm(x_refs).astype(BF)
        dyb = _load_perm(dy_refs).astype(BF)
        for j in range(NBLK):
            cols = slice(j * LN, (j + 1) * LN)
            qr_ref[:, cols] = _dot(dyb[:, j * UB:(j + 1) * UB], ctr_ref[j], 1, 1)
            qi_ref[:, cols] = -_dot(dyb[:, j * UB:(j + 1) * UB], cti_ref[j], 1, 1)
        _scan2(qr_ref, qi_ref, gr_ref, gi_ref, 0, None, er_ref, ei_ref, cr_ref, ci_ref, pr_ref, pi_ref, not reverse,
               corr=(hsr_ref, hsi_ref, p_off, dar_ref, dai_ref))
        dxs = []
        for j in range(NBLK):
            cols = slice(j * LN, (j + 1) * LN)
            xj = xb[:, j * UB:(j + 1) * UB]
            dyj = dyb[:, j * UB:(j + 1) * UB]
            grb, gib = gr_ref[:, cols].astype(BF), gi_ref[:, cols].astype(BF)
            dcr_ref[j] += _dot(dyj, hsr_ref[h_off:h_off + TM, cols].astype(BF), 0, 0)
            dci_ref[j] += -_dot(dyj, hsi_ref[h_off:h_off + TM, cols].astype(BF), 0, 0)
            dbbr_ref[j] += _dot(xj, grb, 0, 0)
            dbbi_ref[j] += _dot(xj, gib, 0, 0)
            dxs.append(_dot(grb, bbr_ref[j], 1, 1) + _dot(gib, bbi_ref[j], 1, 1))
        _store_perm(dx_ref, jnp.concatenate(dxs, axis=1))

    state = lambda: pl.BlockSpec((HS_ROWS, SW), lambda i: (order(i), 0))
    blockd = lambda: _full((NBLK, UB, LN))
    xspec = lambda q: pl.BlockSpec((TM, 128), lambda i: (order(i), XS_BLK + q))
    dyspec = lambda q: pl.BlockSpec((TM, 128), lambda i: (order(i), q))
    return _call(body, name=name, grid=(nt,),
                 in_specs=[xspec(q) for q in range(nq)] + [dyspec(q) for q in range(nq)]
                 + [state(), state(), _dir_spec(dr, (NBLK, UB, LN)), _dir_spec(dr, (NBLK, UB, LN)),
                    _dir_spec(dr, (NBLK, LN, UB)), _dir_spec(dr, (NBLK, LN, UB)),
                    _dir_spec(dr, (PW_ROWS, SW)), _dir_spec(dr, (PW_ROWS, SW))],
                 out_specs=[pl.BlockSpec((nq, TM, 128), lambda i: (0, order(i), 0)), _full((1, SW)), _full((1, SW)),
                            blockd(), blockd(), blockd(), blockd()],
                 out_shape=[jax.ShapeDtypeStruct((nq, t, 128), F32), jax.ShapeDtypeStruct((1, SW), F32),
                            jax.ShapeDtypeStruct((1, SW), F32)] + [jax.ShapeDtypeStruct((NBLK, UB, LN), F32)] * 4,
                 scratch=[pltpu.VMEM((TM, SW), F32), pltpu.VMEM((TM, SW), F32),
                          pltpu.VMEM((TM, SW), F32), pltpu.VMEM((TM, SW), F32),
                          pltpu.VMEM((SCAN_G, SW), F32), pltpu.VMEM((SCAN_G, SW), F32),
                          pltpu.VMEM((1, SW), F32), pltpu.VMEM((1, SW), F32)])(
        *([p] * nq), *([dy] * nq), hs_re, hs_im, bb_re, bb_im, ct_re, ct_im, pw_re, pw_im_conj)


def s5p_dx_sum(dy, dsk, dxf, dxb, dp, name):
    t = dy.shape[0]
    nq = AW // 128

    def body(dy_ref, d_ref, f_ref, b_ref, dp_ref, o_ref):
        o_ref[...] = (dy_ref[...] * d_ref[...] + _cat_lanes(f_ref) + _cat_lanes(b_ref)).astype(BF)

    tr = _mm_rows(t)
    tile = pl.BlockSpec((tr, AW), lambda i: (i, 0))
    blk4 = pl.BlockSpec((nq, tr, 128), lambda i: (0, i, 0))
    return pl.pallas_call(
        body, name=name, grid=(t // tr,),
        in_specs=[tile, _full((1, AW)), blk4, blk4, pl.BlockSpec(memory_space=pl.ANY)],
        out_specs=pl.BlockSpec((tr, AW), lambda i: (i, 3)), out_shape=jax.ShapeDtypeStruct(dp.shape, dp.dtype),
        input_output_aliases={4: 0},
        compiler_params=pltpu.CompilerParams(vmem_limit_bytes=VMEM_LIMIT_BYTES))(dy, dsk, dxf, dxb, dp)


SCALE = HD ** -0.5
NHEAD_NORM = NQ + NKV


def _partner(x):
    half0 = (lax.broadcasted_iota(jnp.int32, (1, HD), 1) % 64) < 32
    return jnp.where(half0, pltpu.roll(x, HD - 32, 1), pltpu.roll(x, 32, 1))


def attn_prep(p, qg, kg, cos, sins, name):
    t = p.shape[0]

    def body(p_ref, qg_ref, kg_ref, cos_ref, sin_ref, o_ref):
        cv, sv = cos_ref[...], sin_ref[...]
        for h in range(NHEAD_NORM):
            cols = slice(h * HD, (h + 1) * HD)
            blk = p_ref[:, cols]
            r = lax.rsqrt(jnp.mean(blk * blk, axis=-1, keepdims=True) + EPS)
            xn = blk * r * (qg_ref[...] if h < NQ else kg_ref[...])
            rot = xn * cv + _partner(xn) * sv
            o_ref[:, cols] = ((rot * SCALE) if h < NQ else rot).astype(BF)
        vcols = slice(NHEAD_NORM * HD, (NHEAD_NORM + NKV) * HD)
        o_ref[:, vcols] = p_ref[:, vcols].astype(BF)

    w = (NHEAD_NORM + NKV) * HD
    tr = _mm_rows(t)
    tile = lambda ww: pl.BlockSpec((tr, ww), lambda i: (i, 0))
    return _call(body, name=name, grid=(t // tr,),
                 in_specs=[tile(w), _full((1, HD)), _full((1, HD)), tile(HD), tile(HD)],
                 out_specs=tile(w), out_shape=jax.ShapeDtypeStruct((t, w), BF))(p, qg, kg, cos, sins)


def attn_prep_bwd(p, dq, dk, dv, qg, kg, cos, sins, dp, name):
    t = p.shape[0]

    def body(p_ref, dq_ref, dk_ref, dv_ref, qg_ref, kg_ref, cos_ref, sin_ref, dp_ref, o_ref, dqg_ref, dkg_ref):
        @pl.when(pl.program_id(0) == 0)
        def _():
            dqg_ref[...] = jnp.zeros_like(dqg_ref)
            dkg_ref[...] = jnp.zeros_like(dkg_ref)

        cv, sv = cos_ref[...], sin_ref[...]
        for h in range(NHEAD_NORM):
            cols = slice(h * HD, (h + 1) * HD)
            blk = p_ref[:, cols]
            r = lax.rsqrt(jnp.mean(blk * blk, axis=-1, keepdims=True) + EPS)
            xh = blk * r
            if h < NQ:
                drot = dq_ref[:, cols] * SCALE
                gv, dg_ref = qg_ref[...], dqg_ref
            else:
                drot = dk_ref[:, (h - NQ) * HD:(h - NQ + 1) * HD]
                gv, dg_ref = kg_ref[...], dkg_ref
            dxn = drot * cv + _partner(drot * sv)
            dg_ref[...] += jnp.sum(dxn * xh, axis=0, keepdims=True)
            dxh = dxn * gv
            o_ref[:, cols] = (r * (dxh - xh * jnp.mean(dxh * xh, axis=-1, keepdims=True))).astype(BF)
        o_ref[:, NHEAD_NORM * HD:(NHEAD_NORM + NKV) * HD] = dv_ref[...].astype(BF)

    w = (NHEAD_NORM + NKV) * HD
    tr = _mm_rows(t)
    tile = lambda ww: pl.BlockSpec((tr, ww), lambda i: (i, 0))
    return pl.pallas_call(
        body, name=name, grid=(t // tr,),
        in_specs=[tile(w), tile(NQ * HD), tile(NKV * HD), tile(NKV * HD), _full((1, HD)), _full((1, HD)),
                  tile(HD), tile(HD), pl.BlockSpec(memory_space=pl.ANY)],
        out_specs=[tile(w), _full((1, HD)), _full((1, HD))],
        out_shape=[jax.ShapeDtypeStruct(dp.shape, dp.dtype), jax.ShapeDtypeStruct((1, HD), F32),
                   jax.ShapeDtypeStruct((1, HD), F32)],
        input_output_aliases={8: 0},
        compiler_params=pltpu.CompilerParams(vmem_limit_bytes=VMEM_LIMIT_BYTES))(p, dq, dk, dv, qg, kg, cos, sins, dp)


KCOL = NQ
VCOL = NQ + NKV
GCOL = (NQ + 2 * NKV)
QPK = NQ // NKV
ATT_KCHUNK = 512


def attn_fwd(qkv, p, name):
    t = qkv.shape[0]

    def body(q_ref, k_ref, v_ref, g_ref, o_ref, mix_ref, lse_ref):
        def attend(nk):
            for hh in range(QPK):
                attend_head(nk, slice(hh * HD, (hh + 1) * HD))

        def attend_head(nk, cols):
            q = q_ref[:, cols]
            chunks = [(k0, min(k0 + 2 * ATT_KCHUNK, nk)) for k0 in range(0, nk, 2 * ATT_KCHUNK)]
            s_next = _dot(q, k_ref[chunks[0][0]:chunks[0][1], :], 1, 1)
            m = l = acc = None
            for ci, (k0, k1) in enumerate(chunks):
                s = s_next
                if ci + 1 < len(chunks):
                    s_next = _dot(q, k_ref[chunks[ci + 1][0]:chunks[ci + 1][1], :], 1, 1)
                mc = jnp.max(s, axis=-1, keepdims=True)
                m_new = mc if m is None else jnp.maximum(m, mc)
                pe = jnp.exp(s - m_new)
                lc = jnp.sum(pe, axis=-1, keepdims=True)
                pv = _dot(pe.astype(BF), v_ref[k0:k1, :])
                if m is None:
                    l, acc = lc, pv
                else:
                    alpha = jnp.exp(m - m_new)
                    l, acc = alpha * l + lc, alpha * acc + pv
                m = m_new
            o = acc / l
            gt = g_ref[:, cols]
            o_ref[:, cols] = o
            mix_ref[:, cols] = (o * (gt * _sig(gt))).astype(BF)
            lse_ref[:, cols] = jnp.broadcast_to(m + jnp.log(l), (TM, HD))

        pl.when(pl.program_id(1) == 0)(lambda: attend(NC))
        pl.when(pl.program_id(1) > 0)(lambda: attend(t))

    blk = pl.BlockSpec((TM, QPK * HD), lambda kv, i: (i, kv))
    return _call(body, name=name, grid=(NKV, t // TM),
                 in_specs=[blk, pl.BlockSpec((t, HD), lambda kv, i: (0, KCOL + kv)),
                           pl.BlockSpec((t, HD), lambda kv, i: (0, VCOL + kv)),
                           pl.BlockSpec((TM, QPK * HD), lambda kv, i: (i, GCOL // QPK + kv))],
                 out_specs=[blk, blk, blk],
                 out_shape=[jax.ShapeDtypeStruct((t, NQ * HD), F32), jax.ShapeDtypeStruct((t, NQ * HD), BF),
                            jax.ShapeDtypeStruct((t, NQ * HD), F32)])(qkv, qkv, qkv, p)


def attn_bwd(qkv, p, dmix, o, lse, name):
    t = qkv.shape[0]

    def body(q_ref, k_ref, v_ref, g_ref, dm_ref, o_ref, lse_ref, dq_ref, dg_ref, dk_ref, dv_ref):
        i = pl.program_id(1)

        @pl.when(i == 0)
        def _():
            dk_ref[...] = jnp.zeros_like(dk_ref)
            dv_ref[...] = jnp.zeros_like(dv_ref)

        def bwd(nk):
            for hh in range(QPK):
                bwd_head(nk, slice(hh * HD, (hh + 1) * HD))

        def bwd_head(nk, cols):
            gt = g_ref[:, cols]
            sg = _sig(gt)
            ov = o_ref[:, cols]
            dmv = dm_ref[:, cols]
            dg_ref[:, cols] = (dmv * ov * (sg * (1.0 + gt * (1.0 - sg)))).astype(BF)
            do = dmv * (gt * sg)
            dr = jnp.sum(do * ov, axis=-1, keepdims=True)
            dob = do.astype(BF)
            q = q_ref[:, cols]
            lse = lse_ref[:, cols][:, 0:1]
            chunks = [slice(k0, min(k0 + ATT_KCHUNK, nk)) for k0 in range(0, nk, ATT_KCHUNK)]
            nxt = (_dot(q, k_ref[chunks[0], :], 1, 1), _dot(dob, v_ref[chunks[0], :], 1, 1))
            dq = None
            for ci, keys in enumerate(chunks):
                s, dp = nxt
                if ci + 1 < len(chunks):
                    nxt = (_dot(q, k_ref[chunks[ci + 1], :], 1, 1), _dot(dob, v_ref[chunks[ci + 1], :], 1, 1))
                pe = jnp.exp(s - lse)
                dsb = (pe * (dp - dr)).astype(BF)
                part = _dot(dsb, k_ref[keys, :])
                dq = part if dq is None else dq + part
                dv_ref[keys, :] += _dot(pe.astype(BF), dob, 0, 0)
                dk_ref[keys, :] += _dot(dsb, q, 0, 0)
            dq_ref[:, cols] = dq

        pl.when(i == 0)(lambda: bwd(NC))
        pl.when(i > 0)(lambda: bwd(t))

    blk = pl.BlockSpec((TM, QPK * HD), lambda kv, i: (i, kv))
    gate = pl.BlockSpec((TM, QPK * HD), lambda kv, i: (i, GCOL // QPK + kv))
    acc = pl.BlockSpec((t, HD), lambda kv, i: (0, kv))
    return _call(body, name=name, grid=(NKV, t // TM),
                 in_specs=[blk, pl.BlockSpec((t, HD), lambda kv, i: (0, KCOL + kv)),
                           pl.BlockSpec((t, HD), lambda kv, i: (0, VCOL + kv)), gate, blk, blk, blk],
                 out_specs=[blk, gate, acc, acc],
                 out_shape=[jax.ShapeDtypeStruct((t, NQ * HD), F32), jax.ShapeDtypeStruct((t, (GCOL + NQ) * HD), BF),
                            jax.ShapeDtypeStruct((t, NKV * HD), F32), jax.ShapeDtypeStruct((t, NKV * HD), F32)])(
        qkv, qkv, qkv, p, dmix, o, lse)


SUM_TILE_BYTES = 6 * 1024 * 1024


def _row_tile(rows, row_bytes, cap=2 * 1024 * 1024):
    if rows * row_bytes <= cap or rows % 8:
        return rows
    tr = rows
    while tr * row_bytes > cap and tr % 16 == 0:
        tr //= 2
    return tr


def _adamw_update(w_ref, g_ref, m_ref, v_ref, d_ref, nm_ref, nv_ref):
    gv = g_ref[...]
    m2 = ADAM_B1 * m_ref[...] + (1.0 - ADAM_B1) * gv
    v2 = ADAM_B2 * v_ref[...] + (1.0 - ADAM_B2) * (gv * gv)
    mh = m2 / (1.0 - ADAM_B1 ** ADAM_STEP)
    vh = v2 / (1.0 - ADAM_B2 ** ADAM_STEP)
    d_ref[...] = -ADAM_LR * (mh / (jnp.sqrt(vh) + ADAM_EPS) + ADAM_WD * w_ref[...])
    nm_ref[...] = m2
    nv_ref[...] = v2


def adamw_many(ws, gs, ms, vs, name):
    n = len(ws)

    def body(*refs):
        for k in range(n):
            _adamw_update(*[refs[j * n + k] for j in range(7)])

    shapes = [jax.ShapeDtypeStruct(w.shape, F32) for w in ws]
    res = _call(body, name=name, out_shape=shapes * 3)(*ws, *gs, *ms, *vs)
    return res[:n], res[n:2 * n], res[2 * n:]


def adamw(w, g, m, v, name):
    r, cdim = w.shape
    tr = _row_tile(r, 4 * max(cdim, 128))

    def body(w_ref, g_ref, m_ref, v_ref, d_ref, nm_ref, nv_ref):
        _adamw_update(w_ref, g_ref, m_ref, v_ref, d_ref, nm_ref, nv_ref)

    tile = pl.BlockSpec((tr, cdim), lambda i: (i, 0))
    sh = jax.ShapeDtypeStruct((r, cdim), F32)
    return _call(body, name=name, grid=(r // tr,), in_specs=[tile] * 4, out_specs=[tile] * 3,
                 out_shape=[sh, sh, sh])(w, g, m, v)


def sum_lead(a, name, out_dtype=F32):
    n, r, cdim = a.shape
    tr = _row_tile(r, 4 * n * max(cdim, 128), SUM_TILE_BYTES)

    def body(a_ref, o_ref):
        acc = a_ref[0].astype(F32)
        for k in range(1, n):
            acc = acc + a_ref[k].astype(F32)
        o_ref[...] = acc.astype(o_ref.dtype)

    return _call(body, name=name, grid=(r // tr,),
                 in_specs=[pl.BlockSpec((n, tr, cdim), lambda i: (0, i, 0))],
                 out_specs=pl.BlockSpec((tr, cdim), lambda i: (i, 0)),
                 out_shape=jax.ShapeDtypeStruct((r, cdim), out_dtype))(a)


_FLIPS = {"xy": [(1, 0, 0), (0, 1, 0), (1, 1, 0)], "c": [(0, 0, 1)],
          "all": [(0, 0, 1), (0, 1, 0), (0, 1, 1), (1, 0, 0), (1, 0, 1), (1, 1, 0), (1, 1, 1)]}
_GROUP_SIZE = {"xy": 4, "c": 2, "all": 8}


def _group_index(group, x, y, c):
    return {"xy": 2 * x + y, "c": c, "all": 4 * x + 2 * y + c}[group]


def exchange(items, name):
    plan = []
    for arr, group, kind in items:
        chunk = arr.shape if kind == "gather" else arr.shape[1:]
        plan.append((group, kind, chunk))
    ncopy = sum(len(_FLIPS[g]) for g, _, _ in plan)
    nitem = len(plan)

    def body(*refs):
        srcs, dsts = refs[:nitem], refs[nitem:2 * nitem]
        send_sems, recv_sems, local_sems = refs[2 * nitem:]
        x, y, c = lax.axis_index("x"), lax.axis_index("y"), lax.axis_index("c")
        sends, recvs, locals_ = [], [], []
        n = 0
        for k, (group, kind, _) in enumerate(plan):
            me = _group_index(group, x, y, c)
            own = srcs[k] if kind == "gather" else srcs[k].at[me]
            locals_.append(pltpu.make_async_copy(own, dsts[k].at[me], local_sems.at[k]))
            for fx, fy, fc in _FLIPS[group]:
                px, py, pc = (1 - x if fx else x), (1 - y if fy else y), (1 - c if fc else c)
                peer = _group_index(group, px, py, pc)
                src = srcs[k] if kind == "gather" else srcs[k].at[peer]
                sends.append(pltpu.make_async_remote_copy(
                    src_ref=src, dst_ref=dsts[k].at[me], send_sem=send_sems.at[n], recv_sem=recv_sems.at[n],
                    device_id=(px, py, pc), device_id_type=MESH))
                recvs.append(pltpu.make_async_remote_copy(
                    src_ref=src, dst_ref=dsts[k].at[peer], send_sem=send_sems.at[n], recv_sem=recv_sems.at[n],
                    device_id=(px, py, pc), device_id_type=MESH))
                n += 1
        for cp in locals_ + sends:
            cp.start()
        for cp in recvs:
            cp.wait_recv()
        for cp in sends:
            cp.wait_send()
        for cp in locals_:
            cp.wait()

    anyspec = pl.BlockSpec(memory_space=pl.ANY)
    outs = [jax.ShapeDtypeStruct((_GROUP_SIZE[g],) + tuple(chunk), arr.dtype)
            for (arr, _, _), (g, _, chunk) in zip(items, plan)]
    res = pl.pallas_call(
        body, name=name, out_shape=outs, in_specs=[anyspec] * nitem, out_specs=[anyspec] * nitem,
        scratch_shapes=[pltpu.SemaphoreType.DMA((ncopy,)), pltpu.SemaphoreType.DMA((ncopy,)),
                        pltpu.SemaphoreType.DMA((nitem,))],
        compiler_params=pltpu.CompilerParams(has_side_effects=True))(*[a for a, _, _ in items])
    return list(res)


D2D_PIECES = 4


def d2d(items, name):
    n = len(items)
    swaps = [k for k, (_, kind) in enumerate(items) if kind == "swap"]

    def pieces_of(rows):
        npc = D2D_PIECES if rows % (8 * D2D_PIECES) == 0 else 1
        return npc, rows // npc

    ncopy = sum(pieces_of(a.shape[0] if kind == "gather" else a.shape[1])[0] for a, kind in items)

    def body(*refs):
        srcs, outs = refs[:n], refs[n:2 * n]
        stages = dict(zip(swaps, refs[2 * n:2 * n + len(swaps)]))
        send_sems, recv_sems, local_sems = refs[2 * n + len(swaps):]
        x, y, c = lax.axis_index("x"), lax.axis_index("y"), lax.axis_index("c")
        sib = (x, y, 1 - c)

        def remote(src, dst, q):
            return pltpu.make_async_remote_copy(src_ref=src, dst_ref=dst, send_sem=send_sems.at[q],
                                                recv_sem=recv_sems.at[q], device_id=sib, device_id_type=MESH)

        copies = []
        q = 0
        for k, (arr, kind) in enumerate(items):
            npc, pr = pieces_of(arr.shape[0] if kind == "gather" else arr.shape[1])
            for pc in range(npc):
                rs = pl.ds(pc * pr, pr)
                if kind == "gather":
                    mine, theirs = outs[k].at[c, rs], outs[k].at[1 - c, rs]
                    copies.append((pltpu.make_async_copy(srcs[k].at[rs], mine, local_sems.at[q]),
                                   remote(mine, mine, q), remote(theirs, theirs, q)))
                else:
                    stage, land = stages[k].at[rs], outs[k].at[rs]
                    copies.append((pltpu.make_async_copy(srcs[k].at[1 - c, rs], stage, local_sems.at[q]),
                                   remote(stage, land, q), remote(stage, land, q)))
                q += 1
        for loc, _, _ in copies:
            loc.start()
        for loc, send, _ in copies:
            loc.wait()
            send.start()
        for _, _, recv in copies:
            recv.wait_recv()
        for _, send, _ in copies:
            send.wait_send()

    outs = [jax.ShapeDtypeStruct((2,) + a.shape if kind == "gather" else a.shape[1:], a.dtype) for a, kind in items]
    res = pl.pallas_call(
        body, name=name, out_shape=outs, in_specs=[pl.BlockSpec(memory_space=pl.ANY)] * n,
        out_specs=[pl.BlockSpec(memory_space=pltpu.VMEM)] * n,
        scratch_shapes=[pltpu.VMEM(items[k][0].shape[1:], items[k][0].dtype) for k in swaps]
        + [pltpu.SemaphoreType.DMA((ncopy,)), pltpu.SemaphoreType.DMA((ncopy,)), pltpu.SemaphoreType.DMA((ncopy,))],
        compiler_params=pltpu.CompilerParams(has_side_effects=True, vmem_limit_bytes=VMEM_LIMIT_BYTES))(
        *[a for a, _ in items])
    return list(res)


def sum_own(pair, got, name, out_dtype=F32):
    _, r, cdim = pair.shape
    tr = _row_tile(r, 4 * 2 * max(cdim, 128), SUM_TILE_BYTES)

    def body(c_ref, p_ref, g_ref, o_ref):
        o_ref[...] = (p_ref[0] + g_ref[...]).astype(o_ref.dtype)

    me = lax.axis_index("c").astype(jnp.int32).reshape(1)
    return pl.pallas_call(
        body, name=name, out_shape=jax.ShapeDtypeStruct((r, cdim), out_dtype),
        grid_spec=pltpu.PrefetchScalarGridSpec(
            num_scalar_prefetch=1, grid=(r // tr,),
            in_specs=[pl.BlockSpec((1, tr, cdim), lambda i, c_ref: (c_ref[0], i, 0)),
                      pl.BlockSpec((tr, cdim), lambda i, c_ref: (i, 0))],
            out_specs=pl.BlockSpec((tr, cdim), lambda i, c_ref: (i, 0))),
        compiler_params=pltpu.CompilerParams(vmem_limit_bytes=VMEM_LIMIT_BYTES))(me, pair, got)


_SMALL = ["c_ctx", "norm_g", "b_mod", "gm_v_g", "gm_w_s", "gm_b_s", "s5_lam_re", "s5_lam_im", "s5_log_dt",
          "s5_b_re", "s5_b_im", "s5_c_re", "s5_c_im", "s5_d", "s5_b_glu", "q_norm_g", "k_norm_g", "final_g"]
_BIG = ["we_in", "we_out", "s5_w_glu", "wo_in", "wo_out"]
_WEIGHTS = ["c_ctx", "norm_g", "w_mod", "b_mod", "we_in", "we_out", "gm_v_g", "gm_w_s", "gm_b_s", "s5_lam_re",
            "s5_lam_im", "s5_log_dt", "s5_b_re", "s5_b_im", "s5_c_re", "s5_c_im", "s5_d", "s5_w_glu", "s5_b_glu",
            "wo_in", "wo_out", "q_norm_g", "k_norm_g", "final_g"]
_SMALL_ALIGN = 8 * 8 * 128


def _rope_tables(n_lat):
    rows = n_lat // GRID_W
    freqs = ROPE_THETA ** (-jnp.arange(HD // 4, dtype=F32) / (HD // 4))
    ar, ac = jnp.arange(rows)[:, None] * freqs, jnp.arange(GRID_W)[:, None] * freqs
    by_row = lambda v: jnp.repeat(v, GRID_W, axis=0)
    by_col = lambda v: jnp.tile(v, (rows, 1))
    cr, sr, cc, sc = by_row(jnp.cos(ar)), by_row(jnp.sin(ar)), by_col(jnp.cos(ac)), by_col(jnp.sin(ac))
    cos = jnp.concatenate([cr, cr, cc, cc], axis=1)
    sins = jnp.concatenate([-sr, sr, -sc, sc], axis=1)
    cos = jnp.concatenate([jnp.ones((NC, HD), F32), cos], axis=0)
    sins = jnp.concatenate([jnp.zeros((NC, HD), F32), sins], axis=0)
    return cos, sins


def _block_diag(v, transpose):
    gpb = SG // NBLK
    v = v.reshape(2, NBLK, gpb, SH, SP)
    eye = jnp.eye(gpb, dtype=v.dtype)
    if transpose:
        return jnp.einsum("djahp,ab->djapbh", v, eye).reshape(2, NBLK, LN, UB)
    return jnp.einsum("djahp,ab->djahbp", v, eye).reshape(2, NBLK, UB, LN)


def _diag_blocks(m):
    gpb = SG // NBLK
    return jnp.einsum("jahap->jahp", m.reshape(NBLK, gpb, SH, gpb, SP)).reshape(SG, SH, SP)


def _view2d(a):
    if a.ndim == 1:
        return a.reshape(1, -1)
    if a.shape[-1] < 64 and a.size % 1024 == 0:
        return a.reshape(-1, 1024)
    return a.reshape(-1, a.shape[-1])


def kernel(x, c, ctx, c_ctx, norm_g, w_mod, b_mod, we_in, we_out, gm_v_g, gm_w_s, gm_b_s, s5_lam_re, s5_lam_im, s5_log_dt, s5_b_re, s5_b_im, s5_c_re, s5_c_im, s5_d, s5_w_glu, s5_b_glu, wo_in, wo_out, q_norm_g, k_norm_g, final_g, loss_target, m_c_ctx, m_norm_g, m_w_mod, m_b_mod, m_we_in, m_we_out, m_gm_v_g, m_gm_w_s, m_gm_b_s, m_s5_lam_re, m_s5_lam_im, m_s5_log_dt, m_s5_b_re, m_s5_b_im, m_s5_c_re, m_s5_c_im, m_s5_d, m_s5_w_glu, m_s5_b_glu, m_wo_in, m_wo_out, m_q_norm_g, m_k_norm_g, m_final_g, v_c_ctx, v_norm_g, v_w_mod, v_b_mod, v_we_in, v_we_out, v_gm_v_g, v_gm_w_s, v_gm_b_s, v_s5_lam_re, v_s5_lam_im, v_s5_log_dt, v_s5_b_re, v_s5_b_im, v_s5_c_re, v_s5_c_im, v_s5_d, v_s5_w_glu, v_s5_b_glu, v_wo_in, v_wo_out, v_q_norm_g, v_k_norm_g, v_final_g):
    weights = dict(c_ctx=c_ctx, norm_g=norm_g, w_mod=w_mod, b_mod=b_mod, we_in=we_in, we_out=we_out, gm_v_g=gm_v_g,
                   gm_w_s=gm_w_s, gm_b_s=gm_b_s, s5_lam_re=s5_lam_re, s5_lam_im=s5_lam_im, s5_log_dt=s5_log_dt,
                   s5_b_re=s5_b_re, s5_b_im=s5_b_im, s5_c_re=s5_c_re, s5_c_im=s5_c_im, s5_d=s5_d, s5_w_glu=s5_w_glu,
                   s5_b_glu=s5_b_glu, wo_in=wo_in, wo_out=wo_out, q_norm_g=q_norm_g, k_norm_g=k_norm_g,
                   final_g=final_g)
    mom_m = dict(c_ctx=m_c_ctx, norm_g=m_norm_g, w_mod=m_w_mod, b_mod=m_b_mod, we_in=m_we_in, we_out=m_we_out,
                 gm_v_g=m_gm_v_g, gm_w_s=m_gm_w_s, gm_b_s=m_gm_b_s, s5_lam_re=m_s5_lam_re, s5_lam_im=m_s5_lam_im,
                 s5_log_dt=m_s5_log_dt, s5_b_re=m_s5_b_re, s5_b_im=m_s5_b_im, s5_c_re=m_s5_c_re, s5_c_im=m_s5_c_im,
                 s5_d=m_s5_d, s5_w_glu=m_s5_w_glu, s5_b_glu=m_s5_b_glu, wo_in=m_wo_in, wo_out=m_wo_out,
                 q_norm_g=m_q_norm_g, k_norm_g=m_k_norm_g, final_g=m_final_g)
    mom_v = dict(c_ctx=v_c_ctx, norm_g=v_norm_g, w_mod=v_w_mod, b_mod=v_b_mod, we_in=v_we_in, we_out=v_we_out,
                 gm_v_g=v_gm_v_g, gm_w_s=v_gm_w_s, gm_b_s=v_gm_b_s, s5_lam_re=v_s5_lam_re, s5_lam_im=v_s5_lam_im,
                 s5_log_dt=v_s5_log_dt, s5_b_re=v_s5_b_re, s5_b_im=v_s5_b_im, s5_c_re=v_s5_c_re, s5_c_im=v_s5_c_im,
                 s5_d=v_s5_d, s5_w_glu=v_s5_w_glu, s5_b_glu=v_s5_b_glu, wo_in=v_wo_in, wo_out=v_wo_out,
                 q_norm_g=v_q_norm_g, k_norm_g=v_k_norm_g, final_g=v_final_g)

    ixy = 2 * lax.axis_index("x") + lax.axis_index("y")
    n_lat = x.shape[1]
    nl = norm_g.shape[0]
    nmod = w_mod.shape[2]
    xin = (ctx.reshape(ctx.shape[1], D), x.reshape(n_lat, D))

    ic = lax.axis_index("c")
    mine = [lax.dynamic_index_in_dim(weights[n], ic, 0, keepdims=False).astype(BF) for n in _BIG]
    got = exchange([(m_, "xy", "gather") for m_ in mine] + [(c, "xy", "gather")], "gather_weights")
    both = d2d([(g_.reshape(-1, g_.shape[-1]), "gather") for g_ in got[:len(_BIG)]], "swap_weights")
    both = [b_.reshape((2,) + g_.shape) for b_, g_ in zip(both, got)]
    wein = [(both[0], l) for l in range(2)]
    weout = [(both[1].reshape(2, 1, D, D), l) for l in range(2)]
    wglu = [(both[2].reshape(2, AW, AW), l) for l in range(2)]
    woin = [(both[3], l) for l in range(2)]
    woout = [(both[4].reshape(2, 1, D, D), l) for l in range(2)]
    c_group = got[len(_BIG)].reshape(4, D)

    cond = jnp.concatenate([c_group, jnp.broadcast_to(c_ctx.reshape(1, D), (4, D))], axis=0)
    b_shard = lax.dynamic_slice(b_mod, (0, ixy * nmod), (nl, nmod)).reshape(nl, 1, nmod)
    mpart = ada_fwd(cond, w_mod, b_shard)
    m_lat, m_ctx = exchange([(jnp.transpose(mpart[:, 0:4], (1, 0, 2)), "xy", "scatter"),
                             (mpart[:, 4], "xy", "gather")], "exchange_mod")
    m_lat = jnp.transpose(m_lat, (1, 0, 2)).reshape(nl, 3, D)
    m_ctx = jnp.transpose(m_ctx, (1, 0, 2)).reshape(nl, 3, D)
    mods = [jnp.stack([m_ctx[l], m_lat[l]], axis=0) for l in range(nl)]

    loss_part, dx, g, d_norm_g, d_mod_lat, d_mod_ctx, d_final_g = _local_step(
        xin, loss_target.reshape(n_lat, D), mods, wein, weout, wglu, woin, woout, weights)
    grad_x = dx.reshape(1, n_lat, D)

    d_mod_lat, d_mod_ctx = jnp.stack(d_mod_lat), jnp.stack(d_mod_ctx)
    dm_send = jnp.stack([d_mod_lat.reshape(nl, 4, nmod), d_mod_ctx.reshape(nl, 4, nmod)])
    (dm_got,) = exchange([(jnp.transpose(dm_send, (2, 0, 1, 3)), "xy", "scatter")], "exchange_dmod")
    dm_rows = jnp.concatenate([dm_got[:, 0], dm_got[:, 1]], axis=0)
    gw_mod, d_cctx = ada_bwd(cond, jnp.transpose(dm_rows, (1, 0, 2)), w_mod)
    g_small = dict(c_ctx=d_cctx.reshape(D), norm_g=jnp.stack(d_norm_g), b_mod=add2(d_mod_lat, d_mod_ctx, "add_dbmod"),
                   final_g=d_final_g.reshape(D))
    for name in _SMALL:
        if name not in g_small:
            g_small[name] = jnp.stack(g[name])

    flat = jnp.concatenate([g_small[n].reshape(-1) for n in _SMALL])
    nflat = flat.shape[0]
    npad = -(-nflat // _SMALL_ALIGN) * _SMALL_ALIGN
    flat = jnp.concatenate([flat, jnp.zeros((npad - nflat,), F32)]).reshape(8, npad // (8 * 128), 128)
    pairs = [gw_mod.reshape(2, nl // 2 * D, nmod)]
    for name in _BIG:
        st = jnp.stack(g[name]) if isinstance(g[name], list) else g[name]
        pairs.append(st.reshape(2, -1, st.shape[-1]))
    got_a = d2d([(pairs[k], "swap") for k in (1, 2, 3)], "reduce_chip_a")
    got_b = d2d([(pairs[k], "swap") for k in (0, 4, 5)], "reduce_chip_b")
    theirs = [got_b[0]] + got_a + got_b[1:]
    chip = [sum_own(pairs[k], theirs[k], f"sum_chip{k}", F32 if k == 0 else BF) for k in range(len(pairs))]
    parts = exchange([(flat, "all", "scatter")]
                     + [(s_.reshape(4, s_.shape[0] // 4, s_.shape[1]), "xy", "scatter") for s_ in chip[1:]],
                     "reduce_scatter")
    sums = [sum_lead(pt, f"sum_shard{k}") for k, pt in enumerate(parts)]
    full = d2d([(sums[0], "gather"), (chip[0], "gather")] + [(s_, "gather") for s_ in sums[1:]], "all_gather")
    (flat_full,) = exchange([(full[0], "xy", "gather")], "gather_small")
    full = [flat_full] + full[1:]
    flat = full[0].reshape(-1)
    grads = {}
    off = 0
    for name in _SMALL:
        sz = weights[name].size
        grads[name] = flat[off:off + sz].reshape(weights[name].shape)
        off += sz
    grads["w_mod"] = full[1].reshape(w_mod.shape)
    for k, name in enumerate(_BIG):
        grads[name] = full[2 + k].reshape(weights[name].shape)

    delta, new_m, new_v = {}, {}, {}
    views = [_view2d(weights[n]) for n in _SMALL]
    ds, nms, nvs = adamw_many(views, [grads[n].reshape(w2.shape) for n, w2 in zip(_SMALL, views)],
                              [mom_m[n].reshape(w2.shape) for n, w2 in zip(_SMALL, views)],
                              [mom_v[n].reshape(w2.shape) for n, w2 in zip(_SMALL, views)], "adamw_small")
    for n, d2, m2, v2 in zip(_SMALL, ds, nms, nvs):
        shp = weights[n].shape
        delta[n], new_m[n], new_v[n] = d2.reshape(shp), m2.reshape(shp), v2.reshape(shp)
    for name in ["w_mod"] + _BIG:
        w2 = _view2d(weights[name])
        d2, m2, v2 = adamw(w2, grads[name].reshape(w2.shape), mom_m[name].reshape(w2.shape),
                           mom_v[name].reshape(w2.shape), f"adamw_{name}")
        shp = weights[name].shape
        delta[name], new_m[name], new_v[name] = d2.reshape(shp), m2.reshape(shp), v2.reshape(shp)

    loss = lax.psum(loss_part[0, 0], ("x", "y", "c"))
    return (loss, grad_x, *[grads[n] for n in _WEIGHTS], *[delta[n] for n in _WEIGHTS],
            *[new_m[n] for n in _WEIGHTS], *[new_v[n] for n in _WEIGHTS])


def _local_step(xin, target, mods, wein, weout, wglu, woin, woout, w):
    norm_g, gm_v_g, gm_w_s, gm_b_s = w["norm_g"], w["gm_v_g"], w["gm_w_s"], w["gm_b_s"]
    s5_lam_re, s5_lam_im, s5_log_dt = w["s5_lam_re"], w["s5_lam_im"], w["s5_log_dt"]
    s5_b_re, s5_b_im, s5_c_re, s5_c_im = w["s5_b_re"], w["s5_b_im"], w["s5_c_re"], w["s5_c_im"]
    s5_d, s5_b_glu, q_norm_g, k_norm_g, final_g = w["s5_d"], w["s5_b_glu"], w["q_norm_g"], w["k_norm_g"], w["final_g"]
    nl = norm_g.shape[0]
    n_lat = xin[1].shape[0]

    cos, sins = _rope_tables(n_lat)

    s5p = []
    for i in range(2):
        lam_l = (s5_lam_re[i].reshape(2, SW), s5_lam_im[i].reshape(2, SW),
                 jnp.repeat(s5_log_dt[i], SP, axis=1))
        lam_r = (jnp.repeat(s5_lam_re[i].reshape(2 * SG, SP), SH, axis=0),
                 jnp.repeat(s5_lam_im[i].reshape(2 * SG, SP), SH, axis=0),
                 jnp.repeat(s5_log_dt[i].reshape(2 * SG, 1), SH, axis=0))
        b_r = (jnp.transpose(s5_b_re[i], (0, 1, 3, 2)).reshape(2 * SG * SH, SP),
               jnp.transpose(s5_b_im[i], (0, 1, 3, 2)).reshape(2 * SG * SH, SP))
        pw_re, pw_im, bbr, bbi = s5_disc(*lam_l, *lam_r, *b_r)
        s5p.append(dict(
            lam_r=lam_r, b_r=b_r, pw_re=pw_re, pw_im=pw_im, pw_im_conj=-pw_im,
            bb_re=_block_diag(bbr.reshape(2, SG, SH, SP), False).astype(BF),
            bb_im=_block_diag(bbi.reshape(2, SG, SH, SP), False).astype(BF),
            ct_re=_block_diag(s5_c_re[i], True).astype(BF), ct_im=_block_diag(s5_c_im[i], True).astype(BF)))

    saved = []
    xcur = xin
    h = pro_fwd(xin[0], xin[1], norm_g[0].reshape(1, D), mods[0], "pro_fwd0")
    for l in range(nl):
        i = l // 2
        sv = dict(x=xcur, h=h)
        if l % 2 == 0:
            p = mm_nn(h, wein[i], f"in_proj{l}")
            sp = s5p[i]
            for dr, rev in ((0, False), (1, True)):
                sv[f"y{dr}"], sv[f"hpr{dr}"], sv[f"hpi{dr}"] = s5q_fwd(
                    p, sp["bb_re"], sp["bb_im"], sp["ct_re"], sp["ct_im"], sp["pw_re"], sp["pw_im"], dr, rev,
                    f"s5_fwd{l}_{dr}")
            mix = mix_fwd(p, sv["y0"], sv["y1"], gm_v_g[i].reshape(1, AW), gm_w_s[i].astype(BF),
                          gm_b_s[i].reshape(NGRP, CHUNK, 1), s5_d[i].reshape(1, AW), wglu[i],
                          s5_b_glu[i].reshape(1, AW), f"mix_fwd{l}")
            o = mm_nn(mix, weout[i], f"out_proj{l}")
        else:
            p = mm_nn(h, woin[i], f"in_proj{l}")
            sv["qkv"] = attn_prep(p, q_norm_g[i].reshape(1, HD), k_norm_g[i].reshape(1, HD), cos, sins, f"attn_prep{l}")
            sv["o_att"], mix, sv["lse"] = attn_fwd(sv["qkv"], p, f"attn_fwd{l}")
            o = mm_nn(mix, woout[i], f"out_proj{l}")
        sv.update(p=p, mix=mix, o=o)
        saved.append(sv)
        if l < nl - 1:
            xcur, h = res_pro_fwd(xcur, o, mods[l], True, norm_g[l + 1].reshape(1, D), mods[l + 1], f"res_pro_fwd{l}")

    loss_part, dx, d_final_g, do, dgt = final_loss(xcur, target, final_g.reshape(1, D), saved[-1]["o"], mods[-1])

    g = {}
    gbuf = {}
    d_norm_g, d_mod_lat, d_mod_ctx = [None] * nl, [None] * nl, [None] * nl
    for name in ("s5_w_glu", "gm_v_g", "gm_w_s", "gm_b_s", "s5_lam_re", "s5_lam_im",
                 "s5_log_dt", "s5_b_re", "s5_b_im", "s5_c_re", "s5_c_im", "s5_d", "s5_b_glu", "q_norm_g", "k_norm_g"):
        g[name] = [None, None]
    for l in reversed(range(nl)):
        i = l // 2
        sv = saved[l]
        w_out = weout[i] if l % 2 == 0 else woout[i]
        dmix = mm_nt(do, w_out, f"out_dgrad{l}")
        out_name, in_name = ("we_out", "we_in") if l % 2 == 0 else ("wo_out", "wo_in")
        gbuf[out_name] = mm_tn(sv["mix"], do, 1, f"out_wgrad{l}", slot=i, into=gbuf.get(out_name))
        if l % 2 == 0:
            sp = s5p[i]
            (dp, dy, g["gm_w_s"][i], dbs, dvg, dd, g["s5_w_glu"][i], dbg) = mix_bwd(
                sv["p"], sv["y0"], sv["y1"], dmix, gm_v_g[i].reshape(1, AW), gm_w_s[i].astype(BF),
                gm_b_s[i].reshape(NGRP, CHUNK, 1), s5_d[i].reshape(1, AW), wglu[i], s5_b_glu[i].reshape(1, AW),
                f"mix_bwd{l}")
            g["gm_b_s"][i], g["gm_v_g"][i] = dbs.reshape(NGRP, CHUNK), dvg.reshape(AW)
            g["s5_d"][i], g["s5_b_glu"][i] = dd.reshape(AW), dbg.reshape(AW)
            g["s5_w_glu"][i] = g["s5_w_glu"][i].reshape(4, AW // 4, AW)
            dxd, das_r, das_i, dbbs_r, dbbs_i, dcs_r, dcs_i = [], [], [], [], [], [], []
            for dr, rev in ((0, False), (1, True)):
                dxs_d, da_r, da_i, dbb_r, dbb_i, dc_r, dc_i = s5q_bwd(
                    sv["p"], sv[f"hpr{dr}"], sv[f"hpi{dr}"], dy, sp["bb_re"], sp["bb_im"], sp["ct_re"], sp["ct_im"],
                    sp["pw_re"], sp["pw_im_conj"], dr, rev, f"s5_bwd{l}_{dr}")
                dxd.append(dxs_d)
                das_r.append(jnp.repeat(da_r.reshape(SG, SP), SH, axis=0))
                das_i.append(jnp.repeat(da_i.reshape(SG, SP), SH, axis=0))
                dbbs_r.append(_diag_blocks(dbb_r).reshape(SG * SH, SP))
                dbbs_i.append(_diag_blocks(dbb_i).reshape(SG * SH, SP))
                dcs_r.append(_diag_blocks(dc_r))
                dcs_i.append(_diag_blocks(dc_i))
            cat = lambda parts: jnp.concatenate(parts, axis=0)
            dlr, dli, dldt, dbr, dbi = s5_param_bwd(*sp["lam_r"], *sp["b_r"], cat(das_r), cat(das_i),
                                                    cat(dbbs_r), cat(dbbs_i))
            g["s5_lam_re"][i], g["s5_lam_im"][i] = dlr.reshape(2, SG, SP), dli.reshape(2, SG, SP)
            g["s5_log_dt"][i] = dldt.reshape(2, SG)
            g["s5_b_re"][i] = jnp.transpose(dbr.reshape(2, SG, SH, SP), (0, 1, 3, 2))
            g["s5_b_im"][i] = jnp.transpose(dbi.reshape(2, SG, SH, SP), (0, 1, 3, 2))
            g["s5_c_re"][i], g["s5_c_im"][i] = jnp.stack(dcs_r), jnp.stack(dcs_i)
            dp = s5p_dx_sum(dy, s5_d[i].reshape(1, AW), dxd[0], dxd[1], dp, f"s5_dx_sum{l}")
            w_in = wein[i]
        else:
            dq, dp, dk, dv = attn_bwd(sv["qkv"], sv["p"], dmix, sv["o_att"], sv["lse"], f"attn_bwd{l}")
            dp, dqg, dkg = attn_prep_bwd(sv["p"], dq, dk, dv, q_norm_g[i].reshape(1, HD),
                                         k_norm_g[i].reshape(1, HD), cos, sins, dp, f"attn_prep_bwd{l}")
            g["q_norm_g"][i], g["k_norm_g"][i] = dqg.reshape(HD), dkg.reshape(HD)
            w_in = woin[i]
        dh = mm_nt(dp, w_in, f"in_dgrad{l}")
        gbuf[in_name] = mm_tn(sv["h"], dp, 4, f"in_wgrad{l}", slot=i, into=gbuf.get(in_name))
        dgt_l = dgt
        if l > 0:
            dx, dmod2, dng, do, dgt = pro_res_bwd(sv["x"], dh, dx, norm_g[l].reshape(1, D), mods[l],
                                                  saved[l - 1]["o"], mods[l - 1], f"pro_res_bwd{l}")
        else:
            dx, dmod2, dng = pro_bwd(xin[0], xin[1], dh, dx, norm_g[l].reshape(1, D), mods[l], f"pro_bwd{l}")
        d_norm_g[l] = dng.reshape(D)
        d_mod_ctx[l] = jnp.concatenate([dmod2[0, 0], dmod2[0, 1], dgt_l[0]])
        d_mod_lat[l] = jnp.concatenate([dmod2[1, 0], dmod2[1, 1], dgt_l[1]])
    g.update(gbuf)
    return loss_part, dx, g, d_norm_g, d_mod_lat, d_mod_ctx, d_final_g
```

```python
import math

import numpy as np
import jax
import jax.numpy as jnp
from jax import lax
from jax.experimental import pallas as pl
from jax.experimental.pallas import tpu as pltpu

F32 = jnp.float32
BF = jnp.bfloat16
MESH = pl.DeviceIdType.MESH

D = 1024
NC = 256
SEQ = 4096
GRID_W = 64
TM = 256
CHUNK = 128
EPS = 1e-6
HD = 128
NQ = 8
NKV = 2
ROPE_THETA = 10000.0
SG = 32
SP = 64
SH = 16
SW = SG * SP
GELU_K = math.sqrt(2.0 / math.pi)
GELU_C = 0.044715
VMEM_LIMIT_BYTES = 56 * 1024 * 1024

ADAM_LR = 0.001
ADAM_B1 = 0.9
ADAM_B2 = 0.999
ADAM_EPS = 1e-08
ADAM_WD = 0.01
ADAM_STEP = 10


def _call(body, *, name, out_shape, grid=None, in_specs=None, out_specs=None, scratch=()):
    kw = {}
    if grid is not None:
        kw["grid"] = grid
    if in_specs is not None:
        kw["in_specs"] = in_specs
    if out_specs is not None:
        kw["out_specs"] = out_specs
    return pl.pallas_call(
        body, name=name, out_shape=out_shape, scratch_shapes=list(scratch),
        compiler_params=pltpu.CompilerParams(vmem_limit_bytes=VMEM_LIMIT_BYTES), **kw)


def _dot(a, b, ca=1, cb=0):
    return lax.dot_general(a, b, (((ca,), (cb,)), ((), ())), preferred_element_type=F32)


def _sig(x):
    return 1.0 / (1.0 + jnp.exp(-x))


def _full(shape):
    n = len(shape)
    return pl.BlockSpec(shape, lambda *_: (0,) * n)


def _mm_rows(t, most=1088):
    for rows in (2176, 1088, 1024, 768, 544, 512, 256):
        if rows <= most and t % rows == 0:
            return rows
    raise ValueError(t)


def _layer_of(w):
    return w if isinstance(w, tuple) else (w[None], 0)


def mm_nn(a, w, name, out_dtype=F32):
    w4, layer = _layer_of(w)
    t, k = a.shape
    _, j, _, nb = w4.shape
    tr = _mm_rows(t, 2176)

    def body(a_ref, w_ref, o_ref):
        o_ref[...] = _dot(a_ref[...], w_ref[0, 0]).astype(o_ref.dtype)

    return _call(body, name=name, grid=(j, t // tr),
                 in_specs=[pl.BlockSpec((tr, k), lambda jj, i: (i, 0)),
                           pl.BlockSpec((1, 1, k, nb), lambda jj, i: (layer, jj, 0, 0))],
                 out_specs=pl.BlockSpec((tr, nb), lambda jj, i: (i, jj)),
                 out_shape=jax.ShapeDtypeStruct((t, j * nb), out_dtype))(a, w4)


def mm_nt(a, w, name, out_dtype=F32):
    w4, layer = _layer_of(w)
    t, _ = a.shape
    _, j, k, nb = w4.shape
    tr = _mm_rows(t, 2176)

    def body(a_ref, w_ref, o_ref):
        acc = _dot(a_ref[:, 0:nb], w_ref[0, 0], 1, 1)
        for jj in range(1, j):
            acc = acc + _dot(a_ref[:, jj * nb:(jj + 1) * nb], w_ref[0, jj], 1, 1)
        o_ref[...] = acc.astype(o_ref.dtype)

    return _call(body, name=name, grid=(t // tr,),
                 in_specs=[pl.BlockSpec((tr, j * nb), lambda i: (i, 0)),
                           pl.BlockSpec((1, j, k, nb), lambda i: (layer, 0, 0, 0))],
                 out_specs=pl.BlockSpec((tr, k), lambda i: (i, 0)),
                 out_shape=jax.ShapeDtypeStruct((t, k), out_dtype))(a, w4)


def mm_tn(a, b, j, name, slot=0, into=None):
    t, m = a.shape
    nb = b.shape[1] // j
    tr = _mm_rows(t, 2176)

    def body(a_ref, b_ref, *rest):
        o_ref = rest[-1]

        @pl.when(pl.program_id(1) == 0)
        def _():
            o_ref[...] = jnp.zeros_like(o_ref)
        o_ref[0, 0] += _dot(a_ref[...], b_ref[...], 0, 0)

    in_specs = [pl.BlockSpec((tr, m), lambda jj, i: (i, 0)), pl.BlockSpec((tr, nb), lambda jj, i: (i, jj))]
    args = [a, b]
    alias = {}
    if into is not None:
        in_specs.append(pl.BlockSpec(memory_space=pl.ANY))
        args.append(into)
        alias = {2: 0}
    return pl.pallas_call(
        body, name=name, grid=(j, t // tr), in_specs=in_specs,
        out_specs=pl.BlockSpec((1, 1, m, nb), lambda jj, i: (slot, jj, 0, 0)),
        out_shape=jax.ShapeDtypeStruct((2, j, m, nb), F32), input_output_aliases=alias,
        compiler_params=pltpu.CompilerParams(vmem_limit_bytes=VMEM_LIMIT_BYTES))(*args)


def _mod_rows(mod_ref, i):
    ctx = i == 0
    sh = jnp.where(ctx, mod_ref[0, 0:1, :], mod_ref[1, 0:1, :])
    sc = jnp.where(ctx, mod_ref[0, 1:2, :], mod_ref[1, 1:2, :])
    gt = jnp.where(ctx, mod_ref[0, 2:3, :], mod_ref[1, 2:3, :])
    return sh, sc, gt


def _split_specs():
    return [pl.BlockSpec((TM, D), lambda i: (0, 0)), pl.BlockSpec((TM, D), lambda i: (jnp.maximum(i - 1, 0), 0))]


def _split_tile(c_ref, l_ref, i):
    return jnp.where(i == 0, c_ref[...], l_ref[...])


def pro_fwd(ctx, lat, g, mod, name):
    t = ctx.shape[0] + lat.shape[0]

    def body(c_ref, l_ref, g_ref, mod_ref, h_ref):
        i = pl.program_id(0)
        sh, sc, _ = _mod_rows(mod_ref, i)
        xv = _split_tile(c_ref, l_ref, i)
        r = lax.rsqrt(jnp.mean(xv * xv, axis=-1, keepdims=True) + EPS)
        h_ref[...] = ((xv * r) * g_ref[...] * (1.0 + sc) + sh).astype(BF)

    return _call(body, name=name, grid=(t // TM,),
                 in_specs=_split_specs() + [_full((1, D)), _full((2, 3, D))],
                 out_specs=pl.BlockSpec((TM, D), lambda i: (i, 0)),
                 out_shape=jax.ShapeDtypeStruct((t, D), BF))(ctx, lat, g, mod)


def pro_bwd(ctx, lat, dh, dxn, g, mod, name):
    t = ctx.shape[0] + lat.shape[0]

    def body(c_ref, l_ref, dh_ref, dxn_ref, g_ref, mod_ref, dx_ref, dmod_ref, dg_ref):
        i = pl.program_id(0)

        @pl.when(i == 0)
        def _():
            dmod_ref[...] = jnp.zeros_like(dmod_ref)
            dg_ref[...] = jnp.zeros_like(dg_ref)

        _, sc, _ = _mod_rows(mod_ref, i)
        xv = _split_tile(c_ref, l_ref, i)
        gv = g_ref[...]
        r = lax.rsqrt(jnp.mean(xv * xv, axis=-1, keepdims=True) + EPS)
        xn = xv * r
        dh_v = dh_ref[...]
        e = dh_v * (1.0 + sc)
        dsh = jnp.sum(dh_v, axis=0, keepdims=True)
        dsc = jnp.sum(dh_v * xn * gv, axis=0, keepdims=True)
        dg_ref[...] += jnp.sum(e * xn, axis=0, keepdims=True)
        dxh = e * gv

        @pl.when(i == 0)
        def _():
            dmod_ref[0, 0:1, :] += dsh
            dmod_ref[0, 1:2, :] += dsc

        @pl.when(i > 0)
        def _():
            dx_ref[...] = dxn_ref[...] + r * (dxh - xn * jnp.mean(dxh * xn, axis=-1, keepdims=True))
            dmod_ref[1, 0:1, :] += dsh
            dmod_ref[1, 1:2, :] += dsc

    tile = pl.BlockSpec((TM, D), lambda i: (i, 0))
    return _call(body, name=name, grid=(t // TM,),
                 in_specs=_split_specs() + [tile, tile, _full((1, D)), _full((2, 3, D))],
                 out_specs=[_split_specs()[1], _full((2, 2, D)), _full((1, D))],
                 out_shape=[jax.ShapeDtypeStruct(lat.shape, F32), jax.ShapeDtypeStruct((2, 2, D), F32),
                            jax.ShapeDtypeStruct((1, D), F32)])(ctx, lat, dh, dxn, g, mod)


def _mod_rows_at(mod_ref, i, tr):
    isctx = i * tr + lax.broadcasted_iota(jnp.int32, (tr, 1), 0) < NC
    pick = lambda k: jnp.where(isctx, mod_ref[0, k:k + 1, :], mod_ref[1, k:k + 1, :])
    return pick(0), pick(1), pick(2), isctx


def res_pro_fwd(x, o, mod, update_ctx, g_next, mod_next, name):
    split = isinstance(x, tuple)
    xs = list(x) if split else [x]
    t = o.shape[0]
    tr = TM if split else _mm_rows(t)

    def body(*refs):
        x_refs = refs[:len(xs)]
        o_ref, mod_ref, g_ref, modn_ref, y_ref, h_ref = refs[len(xs):]
        i = pl.program_id(0)
        _, _, gt, isctx = _mod_rows_at(mod_ref, i, tr)
        xv = _split_tile(x_refs[0], x_refs[1], i) if split else x_refs[0][...]
        xn = xv + gt * o_ref[...]
        if not update_ctx:
            xn = jnp.where(isctx, xv, xn)
        y_ref[...] = xn
        sh, sc, _, _ = _mod_rows_at(modn_ref, i, tr)
        r = lax.rsqrt(jnp.mean(xn * xn, axis=-1, keepdims=True) + EPS)
        h_ref[...] = ((xn * r) * g_ref[...] * (1.0 + sc) + sh).astype(BF)

    tile = pl.BlockSpec((tr, D), lambda i: (i, 0))
    return _call(body, name=name, grid=(t // tr,),
                 in_specs=(_split_specs() if split else [tile]) + [tile, _full((2, 3, D)), _full((1, D)), _full((2, 3, D))],
                 out_specs=[tile, tile],
                 out_shape=[jax.ShapeDtypeStruct((t, D), F32), jax.ShapeDtypeStruct((t, D), BF)])(
        *xs, o, mod, g_next, mod_next)


def pro_res_bwd(x, dh, dxn, g, mod, o_prev, mod_prev, name):
    t = x.shape[0]
    tr = _mm_rows(t, 544)

    def body(x_ref, dh_ref, dxn_ref, g_ref, mod_ref, o_ref, modp_ref, dx_ref, dmod_ref, dg_ref, do_ref, dgt_ref):
        i = pl.program_id(0)

        @pl.when(i == 0)
        def _():
            dmod_ref[...] = jnp.zeros_like(dmod_ref)
            dg_ref[...] = jnp.zeros_like(dg_ref)
            dgt_ref[...] = jnp.zeros_like(dgt_ref)

        _, sc, _, isctx = _mod_rows_at(mod_ref, i, tr)
        xv = x_ref[...]
        gv = g_ref[...]
        r = lax.rsqrt(jnp.mean(xv * xv, axis=-1, keepdims=True) + EPS)
        xn = xv * r
        dh_v = dh_ref[...]
        e = dh_v * (1.0 + sc)
        dsc_rows = dh_v * xn * gv
        dg_ref[...] += jnp.sum(e * xn, axis=0, keepdims=True)
        dxh = e * gv
        dx = dxn_ref[...] + r * (dxh - xn * jnp.mean(dxh * xn, axis=-1, keepdims=True))
        dx_ref[...] = dx
        _, _, gtp, _ = _mod_rows_at(modp_ref, i, tr)
        do_ref[...] = (gtp * dx).astype(BF)
        dgt_rows = dx * o_ref[...]
        dmod_ref[1, 0:1, :] += jnp.sum(dh_v, axis=0, keepdims=True)
        dmod_ref[1, 1:2, :] += jnp.sum(dsc_rows, axis=0, keepdims=True)
        dgt_ref[1:2, :] += jnp.sum(dgt_rows, axis=0, keepdims=True)

        @pl.when(i * tr < NC)
        def _():
            for ref, rows in ((dmod_ref.at[:, 0], dh_v), (dmod_ref.at[:, 1], dsc_rows), (dgt_ref, dgt_rows)):
                part = jnp.sum(jnp.where(isctx, rows, 0.0), axis=0, keepdims=True)
                ref[0:1, :] += part
                ref[1:2, :] += -part

    tile = pl.BlockSpec((tr, D), lambda i: (i, 0))
    return _call(body, name=name, grid=(t // tr,),
                 in_specs=[tile, tile, tile, _full((1, D)), _full((2, 3, D)), tile, _full((2, 3, D))],
                 out_specs=[tile, _full((2, 2, D)), _full((1, D)), tile, _full((2, D))],
                 out_shape=[jax.ShapeDtypeStruct((t, D), F32), jax.ShapeDtypeStruct((2, 2, D), F32),
                            jax.ShapeDtypeStruct((1, D), F32), jax.ShapeDtypeStruct((t, D), BF),
                            jax.ShapeDtypeStruct((2, D), F32)])(x, dh, dxn, g, mod, o_prev, mod_prev)


def final_loss(x, target, g, o_last, mod_last):
    t = x.shape[0]

    def body(x_ref, t_ref, g_ref, o_ref, modp_ref, loss_ref, dx_ref, dg_ref, do_ref, dgt_ref):
        i = pl.program_id(0)

        @pl.when(i == 0)
        def _():
            loss_ref[...] = jnp.zeros_like(loss_ref)
            dg_ref[...] = jnp.zeros_like(dg_ref)
            dx_ref[...] = jnp.zeros_like(dx_ref)
            do_ref[...] = jnp.zeros_like(do_ref)
            dgt_ref[...] = jnp.zeros_like(dgt_ref)

        @pl.when(i > 0)
        def _():
            xv = x_ref[...] + modp_ref[1, 2:3, :] * o_ref[...]
            gv = g_ref[...]
            r = lax.rsqrt(jnp.mean(xv * xv, axis=-1, keepdims=True) + EPS)
            xn = xv * r
            err = xn * gv - t_ref[...]
            loss_ref[...] += (0.5 / D) * jnp.sum(jnp.sum(err * err, axis=1, keepdims=True), axis=0, keepdims=True)
            dy = err * (1.0 / D)
            dg_ref[...] += jnp.sum(dy * xn, axis=0, keepdims=True)
            dxh = dy * gv
            dx = r * (dxh - xn * jnp.mean(dxh * xn, axis=-1, keepdims=True))
            dx_ref[...] = dx
            do_ref[...] = (modp_ref[1, 2:3, :] * dx).astype(BF)
            dgt_ref[1:2, :] += jnp.sum(dx * o_ref[...], axis=0, keepdims=True)

    tile = pl.BlockSpec((TM, D), lambda i: (i, 0))
    return _call(body, name="final_loss", grid=(t // TM,),
                 in_specs=[tile, pl.BlockSpec((TM, D), lambda i: (jnp.maximum(i - 1, 0), 0)), _full((1, D)), tile,
                           _full((2, 3, D))],
                 out_specs=[_full((1, 1)), tile, _full((1, D)), tile, _full((2, D))],
                 out_shape=[jax.ShapeDtypeStruct((1, 1), F32), jax.ShapeDtypeStruct((t, D), F32),
                            jax.ShapeDtypeStruct((1, D), F32), jax.ShapeDtypeStruct((t, D), BF),
                            jax.ShapeDtypeStruct((2, D), F32)])(x, target, g, o_last, mod_last)


def ada_fwd(cond, w_mod, b_mod):
    nl, _, nw = w_mod.shape

    def body(c_ref, w_ref, b_ref, o_ref):
        cv = c_ref[...]
        s = (cv * _sig(cv)).astype(BF)
        o_ref[0] = _dot(s, w_ref[0].astype(BF)) + b_ref[0]

    return _call(body, name="ada_fwd", grid=(nl,),
                 in_specs=[_full((8, D)), pl.BlockSpec((1, D, nw), lambda l: (l, 0, 0)),
                           pl.BlockSpec((1, 1, nw), lambda l: (l, 0, 0))],
                 out_specs=pl.BlockSpec((1, 8, nw), lambda l: (l, 0, 0)),
                 out_shape=jax.ShapeDtypeStruct((nl, 8, nw), F32))(cond, w_mod, b_mod)


def ada_bwd(cond, dm, w_mod):
    nl, _, nw = w_mod.shape

    def body(c_ref, dm_ref, w_ref, gw_ref, dcc_ref, dc_ref):
        l = pl.program_id(0)

        @pl.when(l == 0)
        def _():
            dc_ref[...] = jnp.zeros_like(dc_ref)

        cv = c_ref[...]
        sg = _sig(cv)
        s = (cv * sg).astype(BF)
        dmv = dm_ref[0].astype(BF)
        gw_ref[0] = _dot(s, dmv, 0, 0)
        dc_ref[...] += _dot(dmv, w_ref[0].astype(BF), 1, 1)

        @pl.when(l == nl - 1)
        def _():
            dcond = dc_ref[...] * (sg * (1.0 + cv * (1.0 - sg)))
            dcc_ref[...] = jnp.sum(dcond[4:8], axis=0, keepdims=True)

    return _call(body, name="ada_bwd", grid=(nl,),
                 in_specs=[_full((8, D)), pl.BlockSpec((1, 8, nw), lambda l: (l, 0, 0)),
                           pl.BlockSpec((1, D, nw), lambda l: (l, 0, 0))],
                 out_specs=[pl.BlockSpec((1, D, nw), lambda l: (l, 0, 0)), _full((1, D))],
                 out_shape=[jax.ShapeDtypeStruct((nl, D, nw), F32), jax.ShapeDtypeStruct((1, D), F32)],
                 scratch=[pltpu.VMEM((8, D), F32)])(cond, dm, w_mod)


def add2(a, b, name):
    def body(a_ref, b_ref, o_ref):
        o_ref[...] = a_ref[...] + b_ref[...]

    return _call(body, name=name, out_shape=jax.ShapeDtypeStruct(a.shape, a.dtype))(a, b)


AW = 512
NGRP = 4


def Y4_SPEC():
    return pl.BlockSpec((AW // 128, TM, 128), lambda i: (0, i, 0))


def _cat_lanes(ref):
    return jnp.concatenate([ref[q] for q in range(ref.shape[0])], axis=1)


def _gelu(y):
    t = jnp.tanh(GELU_K * (y + GELU_C * y * y * y))
    return 0.5 * y * (1.0 + t), t


def _layer_norm_stats(v):
    mu = jnp.mean(v, axis=-1, keepdims=True)
    vc = v - mu
    rstd = lax.rsqrt(jnp.mean(vc * vc, axis=-1, keepdims=True) + EPS)
    return vc * rstd, rstd


def _spatial_mix(vn_ref, ws_ref, bs_ref, mixed_ref):
    for ch in range(TM // CHUNK):
        rows = slice(ch * CHUNK, (ch + 1) * CHUNK)
        for g in range(NGRP):
            cols = slice(g * CHUNK, (g + 1) * CHUNK)
            mixed_ref[rows, cols] = _dot(ws_ref[g], vn_ref[rows, cols]) + bs_ref[g]


def mix_fwd(p, yf, yb, vg, ws, bs, dsk, wglu, bglu, name):
    t = p.shape[0]
    wglu, glu_layer = _layer_of(wglu)

    def body(p_ref, yf_ref, yb_ref, vg_ref, ws_ref, bs_ref, d_ref, wg_ref, bg_ref, o_ref, vn_ref, mixed_ref):
        vhat, _ = _layer_norm_stats(p_ref[:, AW:2 * AW])
        vn_ref[...] = (vhat * vg_ref[...]).astype(BF)
        _spatial_mix(vn_ref, ws_ref, bs_ref, mixed_ref)
        ga = p_ref[:, 2 * AW:3 * AW]
        o_ref[:, 0:AW] = (p_ref[:, 0:AW] * mixed_ref[...] * (ga * _sig(ga))).astype(BF)
        y = _cat_lanes(yf_ref) + _cat_lanes(yb_ref) + d_ref[...] * p_ref[:, 3 * AW:4 * AW]
        y2, _ = _gelu(y)
        z = _dot(y2.astype(BF), wg_ref[0]) + bg_ref[...]
        gb = p_ref[:, 4 * AW:5 * AW]
        o_ref[:, AW:2 * AW] = (y2 * _sig(z) * (gb * _sig(gb))).astype(BF)

    tile = lambda w: pl.BlockSpec((TM, w), lambda i: (i, 0))
    return _call(body, name=name, grid=(t // TM,),
                 in_specs=[tile(5 * AW), Y4_SPEC(), Y4_SPEC(), _full((1, AW)), _full((NGRP, CHUNK, CHUNK)),
                           _full((NGRP, CHUNK, 1)), _full((1, AW)),
                           pl.BlockSpec((1, AW, AW), lambda i: (glu_layer, 0, 0)), _full((1, AW))],
                 out_specs=tile(2 * AW), out_shape=jax.ShapeDtypeStruct((t, 2 * AW), BF),
                 scratch=[pltpu.VMEM((TM, AW), BF), pltpu.VMEM((TM, AW), F32)])(p, yf, yb, vg, ws, bs, dsk, wglu, bglu)


def mix_bwd(p, yf, yb, dmix, vg, ws, bs, dsk, wglu, bglu, name):
    t = p.shape[0]
    wglu, glu_layer = _layer_of(wglu)

    def body(p_ref, yf_ref, yb_ref, dm_ref, vg_ref, ws_ref, bs_ref, d_ref, wg_ref, bg_ref,
             dpa_ref, dy_ref, dws_ref, dbs_ref, dvg_ref, dd_ref, dwg_ref, dbg_ref,
             vn_ref, mixed_ref, dmx_ref, dvn_ref):
        @pl.when(pl.program_id(0) == 0)
        def _():
            for r in (dws_ref, dbs_ref, dvg_ref, dd_ref, dwg_ref, dbg_ref):
                r[...] = jnp.zeros_like(r)

        vhat, rstd = _layer_norm_stats(p_ref[:, AW:2 * AW])
        vgv = vg_ref[...]
        vn_ref[...] = (vhat * vgv).astype(BF)
        _spatial_mix(vn_ref, ws_ref, bs_ref, mixed_ref)
        u = p_ref[:, 0:AW]
        ga = p_ref[:, 2 * AW:3 * AW]
        sga = _sig(ga)
        dya = dm_ref[:, 0:AW]
        mixed = mixed_ref[...]
        dpa_ref[:, 0:AW] = (dya * mixed * (ga * sga)).astype(BF)
        dpa_ref[:, 2 * AW:3 * AW] = (dya * u * mixed * (sga * (1.0 + ga * (1.0 - sga)))).astype(BF)
        dmx_ref[...] = dya * u * (ga * sga)
        for ch in range(TM // CHUNK):
            rows = slice(ch * CHUNK, (ch + 1) * CHUNK)
            for g in range(NGRP):
                cols = slice(g * CHUNK, (g + 1) * CHUNK)
                dmx = dmx_ref[rows, cols]
                dmxb = dmx.astype(BF)
                dws_ref[g] += _dot(dmxb, vn_ref[rows, cols], 1, 1)
                dbs_ref[g] += jnp.sum(dmx, axis=1, keepdims=True)
                dvn_ref[rows, cols] = _dot(ws_ref[g], dmxb, 0, 0)
        dvn = dvn_ref[...]
        dvg_ref[...] += jnp.sum(dvn * vhat, axis=0, keepdims=True)
        dvh = dvn * vgv
        dpa_ref[:, AW:2 * AW] = (rstd * (dvh - jnp.mean(dvh, axis=-1, keepdims=True)
                                         - vhat * jnp.mean(dvh * vhat, axis=-1, keepdims=True))).astype(BF)

        xs = p_ref[:, 3 * AW:4 * AW]
        y = _cat_lanes(yf_ref) + _cat_lanes(yb_ref) + d_ref[...] * xs
        y2, th = _gelu(y)
        y2b = y2.astype(BF)
        z = _dot(y2b, wg_ref[0]) + bg_ref[...]
        sz = _sig(z)
        gb = p_ref[:, 4 * AW:5 * AW]
        sgb = _sig(gb)
        dyb = dm_ref[:, AW:2 * AW]
        dpa_ref[:, 4 * AW:5 * AW] = (dyb * (y2 * sz) * (sgb * (1.0 + gb * (1.0 - sgb)))).astype(BF)
        dy3 = dyb * (gb * sgb)
        dz = dy3 * y2 * sz * (1.0 - sz)
        dzb = dz.astype(BF)
        dwg_ref[...] += _dot(y2b, dzb, 0, 0)
        dbg_ref[...] += jnp.sum(dz, axis=0, keepdims=True)
        dy2 = dy3 * sz + _dot(dzb, wg_ref[0], 1, 1)
        dgelu = 0.5 * (1.0 + th) + 0.5 * y * (1.0 - th * th) * GELU_K * (1.0 + 3.0 * GELU_C * y * y)
        dy = dy2 * dgelu
        dd_ref[...] += jnp.sum(dy * xs, axis=0, keepdims=True)
        dy_ref[...] = dy

    tile = lambda w: pl.BlockSpec((TM, w), lambda i: (i, 0))
    return _call(body, name=name, grid=(t // TM,),
                 in_specs=[tile(5 * AW), Y4_SPEC(), Y4_SPEC(), tile(2 * AW), _full((1, AW)), _full((NGRP, CHUNK, CHUNK)),
                           _full((NGRP, CHUNK, 1)), _full((1, AW)),
                           pl.BlockSpec((1, AW, AW), lambda i: (glu_layer, 0, 0)), _full((1, AW))],
                 out_specs=[tile(5 * AW), tile(AW), _full((NGRP, CHUNK, CHUNK)), _full((NGRP, CHUNK, 1)),
                            _full((1, AW)), _full((1, AW)), _full((AW, AW)), _full((1, AW))],
                 out_shape=[jax.ShapeDtypeStruct((t, 5 * AW), BF),
                            jax.ShapeDtypeStruct((t, AW), F32), jax.ShapeDtypeStruct((NGRP, CHUNK, CHUNK), F32),
                            jax.ShapeDtypeStruct((NGRP, CHUNK, 1), F32), jax.ShapeDtypeStruct((1, AW), F32),
                            jax.ShapeDtypeStruct((1, AW), F32), jax.ShapeDtypeStruct((AW, AW), F32),
                            jax.ShapeDtypeStruct((1, AW), F32)],
                 scratch=[pltpu.VMEM((TM, AW), BF), pltpu.VMEM((TM, AW), F32), pltpu.VMEM((TM, AW), F32),
                          pltpu.VMEM((TM, AW), F32)])(p, yf, yb, dmix, vg, ws, bs, dsk, wglu, bglu)


LN = 512
NBLK = SW // LN
UB = AW // NBLK
SCAN_R = 32
SCAN_G = TM // SCAN_R
PW_ROWS = SCAN_R
POW_EXP = list(range(1, SCAN_R + 1))


def s5_disc(lam_re, lam_im, dt, lam_re_r, lam_im_r, dt_r, b_re, b_im):
    nexp = jnp.asarray(np.array(POW_EXP, np.float32).reshape(PW_ROWS, 1))

    def body(n_ref, lr_ref, li_ref, dt_ref, lrr_ref, lir_ref, dtr_ref, br_ref, bi_ref,
             pr_ref, pi_ref, bbr_ref, bbi_ref):
        for dr in range(2):
            dtl = jnp.exp(dt_ref[dr:dr + 1, :])
            zr = lr_ref[dr:dr + 1, :] * dtl
            zi = li_ref[dr:dr + 1, :] * dtl
            mag = jnp.exp(n_ref[...] * zr)
            ang = n_ref[...] * zi
            pr_ref[dr] = mag * jnp.cos(ang)
            pi_ref[dr] = mag * jnp.sin(ang)
        lr, li, dtv = lrr_ref[...], lir_ref[...], jnp.exp(dtr_ref[...])
        mag = jnp.exp(lr * dtv)
        nr = mag * jnp.cos(li * dtv) - 1.0
        ni = mag * jnp.sin(li * dtv)
        den = lr * lr + li * li
        fr = (nr * lr + ni * li) / den
        fi = (ni * lr - nr * li) / den
        bbr_ref[...] = fr * br_ref[...] - fi * bi_ref[...]
        bbi_ref[...] = fr * bi_ref[...] + fi * br_ref[...]

    rows = lam_re_r.shape[0]
    return _call(body, name="s5_disc",
                 out_shape=[jax.ShapeDtypeStruct((2, PW_ROWS, SW), F32), jax.ShapeDtypeStruct((2, PW_ROWS, SW), F32),
                            jax.ShapeDtypeStruct((rows, SP), F32), jax.ShapeDtypeStruct((rows, SP), F32)])(
        nexp, lam_re, lam_im, dt, lam_re_r, lam_im_r, dt_r, b_re, b_im)


def s5_param_bwd(lam_re_r, lam_im_r, dt_r, b_re, b_im, da_re, da_im, dbb_re, dbb_im):
    rows = lam_re_r.shape[0]
    ng = rows // SH
    seg = jnp.asarray(np.kron(np.eye(ng, dtype=np.float32), np.ones((1, SH), np.float32)))

    def body(seg_ref, lr_ref, li_ref, dt_ref, br_ref, bi_ref, dar_ref, dai_ref, dbbr_ref, dbbi_ref,
             dlr_ref, dli_ref, ddt_ref, dbr_ref, dbi_ref):
        lr, li, dtv = lr_ref[...], li_ref[...], jnp.exp(dt_ref[...])
        mag = jnp.exp(lr * dtv)
        lbr = mag * jnp.cos(li * dtv)
        lbi = mag * jnp.sin(li * dtv)
        den = lr * lr + li * li
        nr, ni = lbr - 1.0, lbi
        fr = (nr * lr + ni * li) / den
        fi = (ni * lr - nr * li) / den
        br, bi = br_ref[...], bi_ref[...]
        gbr, gbi = dbbr_ref[...], dbbi_ref[...]
        dbr_ref[...] = gbr * fr + gbi * fi
        dbi_ref[...] = gbi * fr - gbr * fi
        gfr = gbr * br + gbi * bi
        gfi = gbi * br - gbr * bi
        ilr, ili = lr / den, -li / den
        gnr = gfr * ilr + gfi * ili
        gni = gfi * ilr - gfr * ili
        qr = -(fr * ilr - fi * ili)
        qi = -(fr * ili + fi * ilr)
        glr = gfr * qr + gfi * qi
        gli = gfi * qr - gfr * qi
        first = (lax.broadcasted_iota(jnp.int32, (rows, 1), 0) % SH) == 0
        glbr = gnr + jnp.where(first, dar_ref[...], 0.0)
        glbi = gni + jnp.where(first, dai_ref[...], 0.0)
        gzr = glbr * lbr + glbi * lbi
        gzi = glbi * lbr - glbr * lbi
        glr = glr + gzr * dtv
        gli = gli + gzi * dtv
        gdt = (gzr * lr + gzi * li) * dtv
        hi = lax.Precision.HIGHEST
        sg = seg_ref[...]
        dlr_ref[...] = jnp.dot(sg, glr, precision=hi, preferred_element_type=F32)
        dli_ref[...] = jnp.dot(sg, gli, precision=hi, preferred_element_type=F32)
        ddt_ref[...] = jnp.sum(jnp.dot(sg, gdt, precision=hi, preferred_element_type=F32), axis=1, keepdims=True)

    return _call(body, name="s5_param_bwd",
                 out_shape=[jax.ShapeDtypeStruct((ng, SP), F32), jax.ShapeDtypeStruct((ng, SP), F32),
                            jax.ShapeDtypeStruct((ng, 1), F32), jax.ShapeDtypeStruct((rows, SP), F32),
                            jax.ShapeDtypeStruct((rows, SP), F32)])(
        seg, lam_re_r, lam_im_r, dt_r, b_re, b_im, da_re, da_im, dbb_re, dbb_im)


def _tile_order(kind, nt):
    if kind == "fwd":
        return lambda i: i
    if kind == "bwd":
        return lambda i: jnp.where(i == 0, 0, nt - i)
    if kind == "fwd_adj":
        return lambda i: nt - 1 - i
    if kind == "bwd_adj":
        return lambda i: jnp.where(i == nt - 1, 0, i + 1)
    raise ValueError(kind)


XS_BLK = 3 * AW // 128


def _load_perm(refs):
    return jnp.concatenate(
        [jnp.concatenate([ref[pl.ds(r, SCAN_G, stride=SCAN_R), :] for ref in refs], axis=1) for r in range(SCAN_R)],
        axis=0)


def _store_perm(out_ref, val):
    for r in range(SCAN_R):
        for q in range(AW // 128):
            out_ref[q, pl.ds(r, SCAN_G, stride=SCAN_R), :] = val[r * SCAN_G:(r + 1) * SCAN_G, q * 128:(q + 1) * 128]


def _scan2(br_ref, bi_ref, or_ref, oi_ref, h_off, cin_off, er_ref, ei_ref, cr_ref, ci_ref, pr_ref, pi_ref, reverse,
           corr=None):
    gpt = SCAN_G
    nr = SCAN_R
    offsets = list(range(nr))[::-1] if reverse else list(range(nr))
    blocks = [slice(b * LN, (b + 1) * LN) for b in range(NBLK)]
    slab = lambda r: slice(r * gpt, (r + 1) * gpt)
    a1 = [(pr_ref[0:1, c], pi_ref[0:1, c]) for c in blocks]
    x = [None] * NBLK
    for r in offsets:
        for b, c in enumerate(blocks):
            if x[b] is None:
                x[b] = (br_ref[slab(r), c], bi_ref[slab(r), c])
            else:
                (ar, ai), (xr, xi) = a1[b], x[b]
                x[b] = (br_ref[slab(r), c] + ar * xr - ai * xi, bi_ref[slab(r), c] + ar * xi + ai * xr)
    an = [(pr_ref[nr - 1:nr, c], pi_ref[nr - 1:nr, c]) for c in blocks]
    k = [(cr_ref[:, c], ci_ref[:, c]) for c in blocks]
    for g in (range(gpt - 1, -1, -1) if reverse else range(gpt)):
        for b, c in enumerate(blocks):
            (ar, ai), (kr, ki), (xr, xi) = an[b], k[b], x[b]
            er_ref[g:g + 1, c] = kr
            ei_ref[g:g + 1, c] = ki
            k[b] = (xr[g:g + 1, :] + ar * kr - ai * ki, xi[g:g + 1, :] + ar * ki + ai * kr)
    for b, c in enumerate(blocks):
        cr_ref[:, c] = k[b][0]
        ci_ref[:, c] = k[b][1]
        x[b] = (er_ref[:, c], ei_ref[:, c])
        if cin_off is not None:
            or_ref[cin_off:cin_off + gpt, c] = x[b][0]
            oi_ref[cin_off:cin_off + gpt, c] = x[b][1]
    acc = [None] * NBLK
    for r in offsets:
        for b, c in enumerate(blocks):
            (ar, ai), (xr, xi) = a1[b], x[b]
            x[b] = (br_ref[slab(r), c] + ar * xr - ai * xi, bi_ref[slab(r), c] + ar * xi + ai * xr)
            or_ref[h_off + r * gpt:h_off + (r + 1) * gpt, c] = x[b][0]
            oi_ref[h_off + r * gpt:h_off + (r + 1) * gpt, c] = x[b][1]
            if corr is not None:
                wr_ref, wi_ref, w_off = corr[:3]
                wr, wi = wr_ref[w_off + r * gpt:w_off + (r + 1) * gpt, c], wi_ref[w_off + r * gpt:w_off + (r + 1) * gpt, c]
                pr_, pi_ = x[b][0] * wr + x[b][1] * wi, x[b][1] * wr - x[b][0] * wi
                acc[b] = (pr_, pi_) if acc[b] is None else (acc[b][0] + pr_, acc[b][1] + pi_)
    if corr is not None:
        sr_ref, si_ref = corr[3:]
        for b, c in enumerate(blocks):
            sr_ref[:, c] += jnp.sum(acc[b][0], axis=0, keepdims=True)
            si_ref[:, c] += jnp.sum(acc[b][1], axis=0, keepdims=True)


HS_ROWS = TM + SCAN_G


def _hs_offsets(reverse):
    return (0, SCAN_G) if reverse else (SCAN_G, 0)


def _dir_spec(dr, shape):
    return pl.BlockSpec((1,) + shape, lambda i: (dr,) + (0,) * len(shape))


def s5q_fwd(p, bb_re, bb_im, ct_re, ct_im, pw_re, pw_im, dr, reverse, name):
    t = p.shape[0]
    nt = t // TM
    order = _tile_order("bwd" if reverse else "fwd", nt)
    nq = AW // 128
    h_off, p_off = _hs_offsets(reverse)

    def body(*refs):
        x_refs = refs[:nq]
        bbr_ref, bbi_ref, ctr_ref, cti_ref, pr_ref, pi_ref = [r.at[0] for r in refs[nq:nq + 6]]
        y_ref, hsr_ref, hsi_ref = refs[nq + 6:nq + 9]
        br_ref, bi_ref, er_ref, ei_ref, cr_ref, ci_ref = refs[nq + 9:]

        @pl.when(pl.program_id(0) == 0)
        def _():
            cr_ref[...] = jnp.zeros_like(cr_ref)
            ci_ref[...] = jnp.zeros_like(ci_ref)

        xb = _load_perm(x_refs).astype(BF)
        for j in range(NBLK):
            cols = slice(j * LN, (j + 1) * LN)
            br_ref[:, cols] = _dot(xb[:, j * UB:(j + 1) * UB], bbr_ref[j])
            bi_ref[:, cols] = _dot(xb[:, j * UB:(j + 1) * UB], bbi_ref[j])
        _scan2(br_ref, bi_ref, hsr_ref, hsi_ref, h_off, TM if reverse else 0, er_ref, ei_ref, cr_ref, ci_ref,
               pr_ref, pi_ref, reverse)
        y = jnp.concatenate(
            [_dot(hsr_ref[h_off:h_off + TM, j * LN:(j + 1) * LN].astype(BF), ctr_ref[j])
             - _dot(hsi_ref[h_off:h_off + TM, j * LN:(j + 1) * LN].astype(BF), cti_ref[j]) for j in range(NBLK)], axis=1)
        _store_perm(y_ref, y)

    state = lambda: pl.BlockSpec((HS_ROWS, SW), lambda i: (order(i), 0))
    xspec = lambda q: pl.BlockSpec((TM, 128), lambda i: (order(i), XS_BLK + q))
    return _call(body, name=name, grid=(nt,),
                 in_specs=[xspec(q) for q in range(nq)]
                 + [_dir_spec(dr, (NBLK, UB, LN)), _dir_spec(dr, (NBLK, UB, LN)), _dir_spec(dr, (NBLK, LN, UB)),
                    _dir_spec(dr, (NBLK, LN, UB)), _dir_spec(dr, (PW_ROWS, SW)), _dir_spec(dr, (PW_ROWS, SW))],
                 out_specs=[pl.BlockSpec((nq, TM, 128), lambda i: (0, order(i), 0)), state(), state()],
                 out_shape=[jax.ShapeDtypeStruct((nq, t, 128), F32), jax.ShapeDtypeStruct((nt * HS_ROWS, SW), F32),
                            jax.ShapeDtypeStruct((nt * HS_ROWS, SW), F32)],
                 scratch=[pltpu.VMEM((TM, SW), F32), pltpu.VMEM((TM, SW), F32),
                          pltpu.VMEM((SCAN_G, SW), F32), pltpu.VMEM((SCAN_G, SW), F32),
                          pltpu.VMEM((1, SW), F32), pltpu.VMEM((1, SW), F32)])(
        *([p] * nq), bb_re, bb_im, ct_re, ct_im, pw_re, pw_im)


def s5q_bwd(p, hs_re, hs_im, dy, bb_re, bb_im, ct_re, ct_im, pw_re, pw_im_conj, dr, reverse, name):
    t = p.shape[0]
    nt = t // TM
    order = _tile_order("bwd_adj" if reverse else "fwd_adj", nt)
    nq = AW // 128
    h_off, p_off = _hs_offsets(reverse)

    def body(*refs):
        x_refs, dy_refs = refs[:nq], refs[nq:2 * nq]
        (hsr_ref, hsi_ref, bbr_ref, bbi_ref, ctr_ref, cti_ref, pr_ref, pi_ref,
         dx_ref, dar_ref, dai_ref, dbbr_ref, dbbi_ref, dcr_ref, dci_ref,
         qr_ref, qi_ref, gr_ref, gi_ref, er_ref, ei_ref, cr_ref, ci_ref) = refs[2 * nq:]
        bbr_ref, bbi_ref, ctr_ref, cti_ref, pr_ref, pi_ref = [
            r.at[0] for r in (bbr_ref, bbi_ref, ctr_ref, cti_ref, pr_ref, pi_ref)]

        @pl.when(pl.program_id(0) == 0)
        def _():
            for r in (cr_ref, ci_ref, dar_ref, dai_ref, dbbr_ref, dbbi_ref, dcr_ref, dci_ref):
                r[...] = jnp.zeros_like(r)

        xb = _load_perm(x_refs).astype(BF)
        dyb = _load_perm(dy_refs).astype(BF)
        for j in range(NBLK):
            cols = slice(j * LN, (j + 1) * LN)
            qr_ref[:, cols] = _dot(dyb[:, j * UB:(j + 1) * UB], ctr_ref[j], 1, 1)
            qi_ref[:, cols] = -_dot(dyb[:, j * UB:(j + 1) * UB], cti_ref[j], 1, 1)
        _scan2(qr_ref, qi_ref, gr_ref, gi_ref, 0, None, er_ref, ei_ref, cr_ref, ci_ref, pr_ref, pi_ref, not reverse,
               corr=(hsr_ref, hsi_ref, p_off, dar_ref, dai_ref))
        dxs = []
        for j in range(NBLK):
            cols = slice(j * LN, (j + 1) * LN)
            xj = xb[:, j * UB:(j + 1) * UB]
            dyj = dyb[:, j * UB:(j + 1) * UB]
            grb, gib = gr_ref[:, cols].astype(BF), gi_ref[:, cols].astype(BF)
            dcr_ref[j] += _dot(dyj, hsr_ref[h_off:h_off + TM, cols].astype(BF), 0, 0)
            dci_ref[j] += -_dot(dyj, hsi_ref[h_off:h_off + TM, cols].astype(BF), 0, 0)
            dbbr_ref[j] += _dot(xj, grb, 0, 0)
            dbbi_ref[j] += _dot(xj, gib, 0, 0)
            dxs.append(_dot(grb, bbr_ref[j], 1, 1) + _dot(gib, bbi_ref[j], 1, 1))
        _store_perm(dx_ref, jnp.concatenate(dxs, axis=1))

    state = lambda: pl.BlockSpec((HS_ROWS, SW), lambda i: (order(i), 0))
    blockd = lambda: _full((NBLK, UB, LN))
    xspec = lambda q: pl.BlockSpec((TM, 128), lambda i: (order(i), XS_BLK + q))
    dyspec = lambda q: pl.BlockSpec((TM, 128), lambda i: (order(i), q))
    return _call(body, name=name, grid=(nt,),
                 in_specs=[xspec(q) for q in range(nq)] + [dyspec(q) for q in range(nq)]
                 + [state(), state(), _dir_spec(dr, (NBLK, UB, LN)), _dir_spec(dr, (NBLK, UB, LN)),
                    _dir_spec(dr, (NBLK, LN, UB)), _dir_spec(dr, (NBLK, LN, UB)),
                    _dir_spec(dr, (PW_ROWS, SW)), _dir_spec(dr, (PW_ROWS, SW))],
                 out_specs=[pl.BlockSpec((nq, TM, 128), lambda i: (0, order(i), 0)), _full((1, SW)), _full((1, SW)),
                            blockd(), blockd(), blockd(), blockd()],
                 out_shape=[jax.ShapeDtypeStruct((nq, t, 128), F32), jax.ShapeDtypeStruct((1, SW), F32),
                            jax.ShapeDtypeStruct((1, SW), F32)] + [jax.ShapeDtypeStruct((NBLK, UB, LN), F32)] * 4,
                 scratch=[pltpu.VMEM((TM, SW), F32), pltpu.VMEM((TM, SW), F32),
                          pltpu.VMEM((TM, SW), F32), pltpu.VMEM((TM, SW), F32),
                          pltpu.VMEM((SCAN_G, SW), F32), pltpu.VMEM((SCAN_G, SW), F32),
                          pltpu.VMEM((1, SW), F32), pltpu.VMEM((1, SW), F32)])(
        *([p] * nq), *([dy] * nq), hs_re, hs_im, bb_re, bb_im, ct_re, ct_im, pw_re, pw_im_conj)


def s5p_dx_sum(dy, dsk, dxf, dxb, dp, name):
    t = dy.shape[0]
    nq = AW // 128

    def body(dy_ref, d_ref, f_ref, b_ref, dp_ref, o_ref):
        o_ref[...] = (dy_ref[...] * d_ref[...] + _cat_lanes(f_ref) + _cat_lanes(b_ref)).astype(BF)

    tr = _mm_rows(t)
    tile = pl.BlockSpec((tr, AW), lambda i: (i, 0))
    blk4 = pl.BlockSpec((nq, tr, 128), lambda i: (0, i, 0))
    return pl.pallas_call(
        body, name=name, grid=(t // tr,),
        in_specs=[tile, _full((1, AW)), blk4, blk4, pl.BlockSpec(memory_space=pl.ANY)],
        out_specs=pl.BlockSpec((tr, AW), lambda i: (i, 3)), out_shape=jax.ShapeDtypeStruct(dp.shape, dp.dtype),
        input_output_aliases={4: 0},
        compiler_params=pltpu.CompilerParams(vmem_limit_bytes=VMEM_LIMIT_BYTES))(dy, dsk, dxf, dxb, dp)


SCALE = HD ** -0.5
NHEAD_NORM = NQ + NKV


def _partner(x):
    half0 = (lax.broadcasted_iota(jnp.int32, (1, HD), 1) % 64) < 32
    return jnp.where(half0, pltpu.roll(x, HD - 32, 1), pltpu.roll(x, 32, 1))


def attn_prep(p, qg, kg, cos, sins, name):
    t = p.shape[0]

    def body(p_ref, qg_ref, kg_ref, cos_ref, sin_ref, o_ref):
        cv, sv = cos_ref[...], sin_ref[...]
        for h in range(NHEAD_NORM):
            cols = slice(h * HD, (h + 1) * HD)
            blk = p_ref[:, cols]
            r = lax.rsqrt(jnp.mean(blk * blk, axis=-1, keepdims=True) + EPS)
            xn = blk * r * (qg_ref[...] if h < NQ else kg_ref[...])
            rot = xn * cv + _partner(xn) * sv
            o_ref[:, cols] = ((rot * SCALE) if h < NQ else rot).astype(BF)
        vcols = slice(NHEAD_NORM * HD, (NHEAD_NORM + NKV) * HD)
        o_ref[:, vcols] = p_ref[:, vcols].astype(BF)

    w = (NHEAD_NORM + NKV) * HD
    tr = _mm_rows(t)
    tile = lambda ww: pl.BlockSpec((tr, ww), lambda i: (i, 0))
    return _call(body, name=name, grid=(t // tr,),
                 in_specs=[tile(w), _full((1, HD)), _full((1, HD)), tile(HD), tile(HD)],
                 out_specs=tile(w), out_shape=jax.ShapeDtypeStruct((t, w), BF))(p, qg, kg, cos, sins)


def attn_prep_bwd(p, dq, dk, dv, qg, kg, cos, sins, dp, name):
    t = p.shape[0]

    def body(p_ref, dq_ref, dk_ref, dv_ref, qg_ref, kg_ref, cos_ref, sin_ref, dp_ref, o_ref, dqg_ref, dkg_ref):
        @pl.when(pl.program_id(0) == 0)
        def _():
            dqg_ref[...] = jnp.zeros_like(dqg_ref)
            dkg_ref[...] = jnp.zeros_like(dkg_ref)

        cv, sv = cos_ref[...], sin_ref[...]
        for h in range(NHEAD_NORM):
            cols = slice(h * HD, (h + 1) * HD)
            blk = p_ref[:, cols]
            r = lax.rsqrt(jnp.mean(blk * blk, axis=-1, keepdims=True) + EPS)
            xh = blk * r
            if h < NQ:
                drot = dq_ref[:, cols] * SCALE
                gv, dg_ref = qg_ref[...], dqg_ref
            else:
                drot = dk_ref[:, (h - NQ) * HD:(h - NQ + 1) * HD]
                gv, dg_ref = kg_ref[...], dkg_ref
            dxn = drot * cv + _partner(drot * sv)
            dg_ref[...] += jnp.sum(dxn * xh, axis=0, keepdims=True)
            dxh = dxn * gv
            o_ref[:, cols] = (r * (dxh - xh * jnp.mean(dxh * xh, axis=-1, keepdims=True))).astype(BF)
        o_ref[:, NHEAD_NORM * HD:(NHEAD_NORM + NKV) * HD] = dv_ref[...].astype(BF)

    w = (NHEAD_NORM + NKV) * HD
    tr = _mm_rows(t)
    tile = lambda ww: pl.BlockSpec((tr, ww), lambda i: (i, 0))
    return pl.pallas_call(
        body, name=name, grid=(t // tr,),
        in_specs=[tile(w), tile(NQ * HD), tile(NKV * HD), tile(NKV * HD), _full((1, HD)), _full((1, HD)),
                  tile(HD), tile(HD), pl.BlockSpec(memory_space=pl.ANY)],
        out_specs=[tile(w), _full((1, HD)), _full((1, HD))],
        out_shape=[jax.ShapeDtypeStruct(dp.shape, dp.dtype), jax.ShapeDtypeStruct((1, HD), F32),
                   jax.ShapeDtypeStruct((1, HD), F32)],
        input_output_aliases={8: 0},
        compiler_params=pltpu.CompilerParams(vmem_limit_bytes=VMEM_LIMIT_BYTES))(p, dq, dk, dv, qg, kg, cos, sins, dp)


KCOL = NQ
VCOL = NQ + NKV
GCOL = (NQ + 2 * NKV)
QPK = NQ // NKV
ATT_KCHUNK = 512


def attn_fwd(qkv, p, name):
    t = qkv.shape[0]

    def body(q_ref, k_ref, v_ref, g_ref, o_ref, mix_ref, lse_ref):
        def attend(nk):
            for hh in range(QPK):
                attend_head(nk, slice(hh * HD, (hh + 1) * HD))

        def attend_head(nk, cols):
            q = q_ref[:, cols]
            chunks = [(k0, min(k0 + 2 * ATT_KCHUNK, nk)) for k0 in range(0, nk, 2 * ATT_KCHUNK)]
            s_next = _dot(q, k_ref[chunks[0][0]:chunks[0][1], :], 1, 1)
            m = l = acc = None
            for ci, (k0, k1) in enumerate(chunks):
                s = s_next
                if ci + 1 < len(chunks):
                    s_next = _dot(q, k_ref[chunks[ci + 1][0]:chunks[ci + 1][1], :], 1, 1)
                mc = jnp.max(s, axis=-1, keepdims=True)
                m_new = mc if m is None else jnp.maximum(m, mc)
                pe = jnp.exp(s - m_new)
                lc = jnp.sum(pe, axis=-1, keepdims=True)
                pv = _dot(pe.astype(BF), v_ref[k0:k1, :])
                if m is None:
                    l, acc = lc, pv
                else:
                    alpha = jnp.exp(m - m_new)
                    l, acc = alpha * l + lc, alpha * acc + pv
                m = m_new
            o = acc / l
            gt = g_ref[:, cols]
            o_ref[:, cols] = o
            mix_ref[:, cols] = (o * (gt * _sig(gt))).astype(BF)
            lse_ref[:, cols] = jnp.broadcast_to(m + jnp.log(l), (TM, HD))

        pl.when(pl.program_id(1) == 0)(lambda: attend(NC))
        pl.when(pl.program_id(1) > 0)(lambda: attend(t))

    blk = pl.BlockSpec((TM, QPK * HD), lambda kv, i: (i, kv))
    return _call(body, name=name, grid=(NKV, t // TM),
                 in_specs=[blk, pl.BlockSpec((t, HD), lambda kv, i: (0, KCOL + kv)),
                           pl.BlockSpec((t, HD), lambda kv, i: (0, VCOL + kv)),
                           pl.BlockSpec((TM, QPK * HD), lambda kv, i: (i, GCOL // QPK + kv))],
                 out_specs=[blk, blk, blk],
                 out_shape=[jax.ShapeDtypeStruct((t, NQ * HD), F32), jax.ShapeDtypeStruct((t, NQ * HD), BF),
                            jax.ShapeDtypeStruct((t, NQ * HD), F32)])(qkv, qkv, qkv, p)


def attn_bwd(qkv, p, dmix, o, lse, name):
    t = qkv.shape[0]

    def body(q_ref, k_ref, v_ref, g_ref, dm_ref, o_ref, lse_ref, dq_ref, dg_ref, dk_ref, dv_ref):
        i = pl.program_id(1)

        @pl.when(i == 0)
        def _():
            dk_ref[...] = jnp.zeros_like(dk_ref)
            dv_ref[...] = jnp.zeros_like(dv_ref)

        def bwd(nk):
            for hh in range(QPK):
                bwd_head(nk, slice(hh * HD, (hh + 1) * HD))

        def bwd_head(nk, cols):
            gt = g_ref[:, cols]
            sg = _sig(gt)
            ov = o_ref[:, cols]
            dmv = dm_ref[:, cols]
            dg_ref[:, cols] = (dmv * ov * (sg * (1.0 + gt * (1.0 - sg)))).astype(BF)
            do = dmv * (gt * sg)
            dr = jnp.sum(do * ov, axis=-1, keepdims=True)
            dob = do.astype(BF)
            q = q_ref[:, cols]
            lse = lse_ref[:, cols][:, 0:1]
            chunks = [slice(k0, min(k0 + ATT_KCHUNK, nk)) for k0 in range(0, nk, ATT_KCHUNK)]
            nxt = (_dot(q, k_ref[chunks[0], :], 1, 1), _dot(dob, v_ref[chunks[0], :], 1, 1))
            dq = None
            for ci, keys in enumerate(chunks):
                s, dp = nxt
                if ci + 1 < len(chunks):
                    nxt = (_dot(q, k_ref[chunks[ci + 1], :], 1, 1), _dot(dob, v_ref[chunks[ci + 1], :], 1, 1))
                pe = jnp.exp(s - lse)
                dsb = (pe * (dp - dr)).astype(BF)
                part = _dot(dsb, k_ref[keys, :])
                dq = part if dq is None else dq + part
                dv_ref[keys, :] += _dot(pe.astype(BF), dob, 0, 0)
                dk_ref[keys, :] += _dot(dsb, q, 0, 0)
            dq_ref[:, cols] = dq

        pl.when(i == 0)(lambda: bwd(NC))
        pl.when(i > 0)(lambda: bwd(t))

    blk = pl.BlockSpec((TM, QPK * HD), lambda kv, i: (i, kv))
    gate = pl.BlockSpec((TM, QPK * HD), lambda kv, i: (i, GCOL // QPK + kv))
    acc = pl.BlockSpec((t, HD), lambda kv, i: (0, kv))
    return _call(body, name=name, grid=(NKV, t // TM),
                 in_specs=[blk, pl.BlockSpec((t, HD), lambda kv, i: (0, KCOL + kv)),
                           pl.BlockSpec((t, HD), lambda kv, i: (0, VCOL + kv)), gate, blk, blk, blk],
                 out_specs=[blk, gate, acc, acc],
                 out_shape=[jax.ShapeDtypeStruct((t, NQ * HD), F32), jax.ShapeDtypeStruct((t, (GCOL + NQ) * HD), BF),
                            jax.ShapeDtypeStruct((t, NKV * HD), F32), jax.ShapeDtypeStruct((t, NKV * HD), F32)])(
        qkv, qkv, qkv, p, dmix, o, lse)


SUM_TILE_BYTES = 6 * 1024 * 1024


def _row_tile(rows, row_bytes, cap=2 * 1024 * 1024):
    if rows * row_bytes <= cap or rows % 8:
        return rows
    tr = rows
    while tr * row_bytes > cap and tr % 16 == 0:
        tr //= 2
    return tr


def _adamw_update(w_ref, g_ref, m_ref, v_ref, d_ref, nm_ref, nv_ref):
    gv = g_ref[...]
    m2 = ADAM_B1 * m_ref[...] + (1.0 - ADAM_B1) * gv
    v2 = ADAM_B2 * v_ref[...] + (1.0 - ADAM_B2) * (gv * gv)
    mh = m2 / (1.0 - ADAM_B1 ** ADAM_STEP)
    vh = v2 / (1.0 - ADAM_B2 ** ADAM_STEP)
    d_ref[...] = -ADAM_LR * (mh / (jnp.sqrt(vh) + ADAM_EPS) + ADAM_WD * w_ref[...])
    nm_ref[...] = m2
    nv_ref[...] = v2


def adamw_many(ws, gs, ms, vs, name):
    n = len(ws)

    def body(*refs):
        for k in range(n):
            _adamw_update(*[refs[j * n + k] for j in range(7)])

    shapes = [jax.ShapeDtypeStruct(w.shape, F32) for w in ws]
    res = _call(body, name=name, out_shape=shapes * 3)(*ws, *gs, *ms, *vs)
    return res[:n], res[n:2 * n], res[2 * n:]


def adamw(w, g, m, v, name):
    r, cdim = w.shape
    tr = _row_tile(r, 4 * max(cdim, 128))

    def body(w_ref, g_ref, m_ref, v_ref, d_ref, nm_ref, nv_ref):
        _adamw_update(w_ref, g_ref, m_ref, v_ref, d_ref, nm_ref, nv_ref)

    tile = pl.BlockSpec((tr, cdim), lambda i: (i, 0))
    sh = jax.ShapeDtypeStruct((r, cdim), F32)
    return _call(body, name=name, grid=(r // tr,), in_specs=[tile] * 4, out_specs=[tile] * 3,
                 out_shape=[sh, sh, sh])(w, g, m, v)


def sum_lead(a, name, out_dtype=F32):
    n, r, cdim = a.shape
    tr = _row_tile(r, 4 * n * max(cdim, 128), SUM_TILE_BYTES)

    def body(a_ref, o_ref):
        acc = a_ref[0].astype(F32)
        for k in range(1, n):
            acc = acc + a_ref[k].astype(F32)
        o_ref[...] = acc.astype(o_ref.dtype)

    return _call(body, name=name, grid=(r // tr,),
                 in_specs=[pl.BlockSpec((n, tr, cdim), lambda i: (0, i, 0))],
                 out_specs=pl.BlockSpec((tr, cdim), lambda i: (i, 0)),
                 out_shape=jax.ShapeDtypeStruct((r, cdim), out_dtype))(a)


_FLIPS = {"xy": [(1, 0, 0), (0, 1, 0), (1, 1, 0)], "c": [(0, 0, 1)],
          "all": [(0, 0, 1), (0, 1, 0), (0, 1, 1), (1, 0, 0), (1, 0, 1), (1, 1, 0), (1, 1, 1)]}
_GROUP_SIZE = {"xy": 4, "c": 2, "all": 8}


def _group_index(group, x, y, c):
    return {"xy": 2 * x + y, "c": c, "all": 4 * x + 2 * y + c}[group]


def exchange(items, name):
    plan = []
    for arr, group, kind in items:
        chunk = arr.shape if kind == "gather" else arr.shape[1:]
        plan.append((group, kind, chunk))
    ncopy = sum(len(_FLIPS[g]) for g, _, _ in plan)
    nitem = len(plan)

    def body(*refs):
        srcs, dsts = refs[:nitem], refs[nitem:2 * nitem]
        send_sems, recv_sems, local_sems = refs[2 * nitem:]
        x, y, c = lax.axis_index("x"), lax.axis_index("y"), lax.axis_index("c")
        sends, recvs, locals_ = [], [], []
        n = 0
        for k, (group, kind, _) in enumerate(plan):
            me = _group_index(group, x, y, c)
            own = srcs[k] if kind == "gather" else srcs[k].at[me]
            locals_.append(pltpu.make_async_copy(own, dsts[k].at[me], local_sems.at[k]))
            for fx, fy, fc in _FLIPS[group]:
                px, py, pc = (1 - x if fx else x), (1 - y if fy else y), (1 - c if fc else c)
                peer = _group_index(group, px, py, pc)
                src = srcs[k] if kind == "gather" else srcs[k].at[peer]
                sends.append(pltpu.make_async_remote_copy(
                    src_ref=src, dst_ref=dsts[k].at[me], send_sem=send_sems.at[n], recv_sem=recv_sems.at[n],
                    device_id=(px, py, pc), device_id_type=MESH))
                recvs.append(pltpu.make_async_remote_copy(
                    src_ref=src, dst_ref=dsts[k].at[peer], send_sem=send_sems.at[n], recv_sem=recv_sems.at[n],
                    device_id=(px, py, pc), device_id_type=MESH))
                n += 1
        for cp in locals_ + sends:
            cp.start()
        for cp in recvs:
            cp.wait_recv()
        for cp in sends:
            cp.wait_send()
        for cp in locals_:
            cp.wait()

    anyspec = pl.BlockSpec(memory_space=pl.ANY)
    outs = [jax.ShapeDtypeStruct((_GROUP_SIZE[g],) + tuple(chunk), arr.dtype)
            for (arr, _, _), (g, _, chunk) in zip(items, plan)]
    res = pl.pallas_call(
        body, name=name, out_shape=outs, in_specs=[anyspec] * nitem, out_specs=[anyspec] * nitem,
        scratch_shapes=[pltpu.SemaphoreType.DMA((ncopy,)), pltpu.SemaphoreType.DMA((ncopy,)),
                        pltpu.SemaphoreType.DMA((nitem,))],
        compiler_params=pltpu.CompilerParams(has_side_effects=True))(*[a for a, _, _ in items])
    return list(res)


D2D_PIECES = 4


def d2d(items, name):
    n = len(items)
    swaps = [k for k, (_, kind) in enumerate(items) if kind == "swap"]

    def pieces_of(rows):
        npc = D2D_PIECES if rows % (8 * D2D_PIECES) == 0 else 1
        return npc, rows // npc

    ncopy = sum(pieces_of(a.shape[0] if kind == "gather" else a.shape[1])[0] for a, kind in items)

    def body(*refs):
        srcs, outs = refs[:n], refs[n:2 * n]
        stages = dict(zip(swaps, refs[2 * n:2 * n + len(swaps)]))
        send_sems, recv_sems, local_sems = refs[2 * n + len(swaps):]
        x, y, c = lax.axis_index("x"), lax.axis_index("y"), lax.axis_index("c")
        sib = (x, y, 1 - c)

        def remote(src, dst, q):
            return pltpu.make_async_remote_copy(src_ref=src, dst_ref=dst, send_sem=send_sems.at[q],
                                                recv_sem=recv_sems.at[q], device_id=sib, device_id_type=MESH)

        copies = []
        q = 0
        for k, (arr, kind) in enumerate(items):
            npc, pr = pieces_of(arr.shape[0] if kind == "gather" else arr.shape[1])
            for pc in range(npc):
                rs = pl.ds(pc * pr, pr)
                if kind == "gather":
                    mine, theirs = outs[k].at[c, rs], outs[k].at[1 - c, rs]
                    copies.append((pltpu.make_async_copy(srcs[k].at[rs], mine, local_sems.at[q]),
                                   remote(mine, mine, q), remote(theirs, theirs, q)))
                else:
                    stage, land = stages[k].at[rs], outs[k].at[rs]
                    copies.append((pltpu.make_async_copy(srcs[k].at[1 - c, rs], stage, local_sems.at[q]),
                                   remote(stage, land, q), remote(stage, land, q)))
                q += 1
        for loc, _, _ in copies:
            loc.start()
        for loc, send, _ in copies:
            loc.wait()
            send.start()
        for _, _, recv in copies:
            recv.wait_recv()
        for _, send, _ in copies:
            send.wait_send()

    outs = [jax.ShapeDtypeStruct((2,) + a.shape if kind == "gather" else a.shape[1:], a.dtype) for a, kind in items]
    res = pl.pallas_call(
        body, name=name, out_shape=outs, in_specs=[pl.BlockSpec(memory_space=pl.ANY)] * n,
        out_specs=[pl.BlockSpec(memory_space=pltpu.VMEM)] * n,
        scratch_shapes=[pltpu.VMEM(items[k][0].shape[1:], items[k][0].dtype) for k in swaps]
        + [pltpu.SemaphoreType.DMA((ncopy,)), pltpu.SemaphoreType.DMA((ncopy,)), pltpu.SemaphoreType.DMA((ncopy,))],
        compiler_params=pltpu.CompilerParams(has_side_effects=True, vmem_limit_bytes=VMEM_LIMIT_BYTES))(
        *[a for a, _ in items])
    return list(res)


def sum_own(pair, got, name, out_dtype=F32):
    _, r, cdim = pair.shape
    tr = _row_tile(r, 4 * 2 * max(cdim, 128), SUM_TILE_BYTES)

    def body(c_ref, p_ref, g_ref, o_ref):
        o_ref[...] = (p_ref[0] + g_ref[...]).astype(o_ref.dtype)

    me = lax.axis_index("c").astype(jnp.int32).reshape(1)
    return pl.pallas_call(
        body, name=name, out_shape=jax.ShapeDtypeStruct((r, cdim), out_dtype),
        grid_spec=pltpu.PrefetchScalarGridSpec(
            num_scalar_prefetch=1, grid=(r // tr,),
            in_specs=[pl.BlockSpec((1, tr, cdim), lambda i, c_ref: (c_ref[0], i, 0)),
                      pl.BlockSpec((tr, cdim), lambda i, c_ref: (i, 0))],
            out_specs=pl.BlockSpec((tr, cdim), lambda i, c_ref: (i, 0))),
        compiler_params=pltpu.CompilerParams(vmem_limit_bytes=VMEM_LIMIT_BYTES))(me, pair, got)


_SMALL = ["c_ctx", "norm_g", "b_mod", "gm_v_g", "gm_w_s", "gm_b_s", "s5_lam_re", "s5_lam_im", "s5_log_dt",
          "s5_b_re", "s5_b_im", "s5_c_re", "s5_c_im", "s5_d", "s5_b_glu", "q_norm_g", "k_norm_g", "final_g"]
_BIG = ["we_in", "we_out", "s5_w_glu", "wo_in", "wo_out"]
_WEIGHTS = ["c_ctx", "norm_g", "w_mod", "b_mod", "we_in", "we_out", "gm_v_g", "gm_w_s", "gm_b_s", "s5_lam_re",
            "s5_lam_im", "s5_log_dt", "s5_b_re", "s5_b_im", "s5_c_re", "s5_c_im", "s5_d", "s5_w_glu", "s5_b_glu",
            "wo_in", "wo_out", "q_norm_g", "k_norm_g", "final_g"]
_SMALL_ALIGN = 8 * 8 * 128


def _rope_tables(n_lat):
    rows = n_lat // GRID_W
    freqs = ROPE_THETA ** (-jnp.arange(HD // 4, dtype=F32) / (HD // 4))
    ar, ac = jnp.arange(rows)[:, None] * freqs, jnp.arange(GRID_W)[:, None] * freqs
    by_row = lambda v: jnp.repeat(v, GRID_W, axis=0)
    by_col = lambda v: jnp.tile(v, (rows, 1))
    cr, sr, cc, sc = by_row(jnp.cos(ar)), by_row(jnp.sin(ar)), by_col(jnp.cos(ac)), by_col(jnp.sin(ac))
    cos = jnp.concatenate([cr, cr, cc, cc], axis=1)
    sins = jnp.concatenate([-sr, sr, -sc, sc], axis=1)
    cos = jnp.concatenate([jnp.ones((NC, HD), F32), cos], axis=0)
    sins = jnp.concatenate([jnp.zeros((NC, HD), F32), sins], axis=0)
    return cos, sins


def _block_diag(v, transpose):
    gpb = SG // NBLK
    v = v.reshape(2, NBLK, gpb, SH, SP)
    eye = jnp.eye(gpb, dtype=v.dtype)
    if transpose:
        return jnp.einsum("djahp,ab->djapbh", v, eye).reshape(2, NBLK, LN, UB)
    return jnp.einsum("djahp,ab->djahbp", v, eye).reshape(2, NBLK, UB, LN)


def _diag_blocks(m):
    gpb = SG // NBLK
    return jnp.einsum("jahap->jahp", m.reshape(NBLK, gpb, SH, gpb, SP)).reshape(SG, SH, SP)


def _view2d(a):
    if a.ndim == 1:
        return a.reshape(1, -1)
    if a.shape[-1] < 64 and a.size % 1024 == 0:
        return a.reshape(-1, 1024)
    return a.reshape(-1, a.shape[-1])


def kernel(x, c, ctx, c_ctx, norm_g, w_mod, b_mod, we_in, we_out, gm_v_g, gm_w_s, gm_b_s, s5_lam_re, s5_lam_im, s5_log_dt, s5_b_re, s5_b_im, s5_c_re, s5_c_im, s5_d, s5_w_glu, s5_b_glu, wo_in, wo_out, q_norm_g, k_norm_g, final_g, loss_target, m_c_ctx, m_norm_g, m_w_mod, m_b_mod, m_we_in, m_we_out, m_gm_v_g, m_gm_w_s, m_gm_b_s, m_s5_lam_re, m_s5_lam_im, m_s5_log_dt, m_s5_b_re, m_s5_b_im, m_s5_c_re, m_s5_c_im, m_s5_d, m_s5_w_glu, m_s5_b_glu, m_wo_in, m_wo_out, m_q_norm_g, m_k_norm_g, m_final_g, v_c_ctx, v_norm_g, v_w_mod, v_b_mod, v_we_in, v_we_out, v_gm_v_g, v_gm_w_s, v_gm_b_s, v_s5_lam_re, v_s5_lam_im, v_s5_log_dt, v_s5_b_re, v_s5_b_im, v_s5_c_re, v_s5_c_im, v_s5_d, v_s5_w_glu, v_s5_b_glu, v_wo_in, v_wo_out, v_q_norm_g, v_k_norm_g, v_final_g):
    weights = dict(c_ctx=c_ctx, norm_g=norm_g, w_mod=w_mod, b_mod=b_mod, we_in=we_in, we_out=we_out, gm_v_g=gm_v_g,
                   gm_w_s=gm_w_s, gm_b_s=gm_b_s, s5_lam_re=s5_lam_re, s5_lam_im=s5_lam_im, s5_log_dt=s5_log_dt,
                   s5_b_re=s5_b_re, s5_b_im=s5_b_im, s5_c_re=s5_c_re, s5_c_im=s5_c_im, s5_d=s5_d, s5_w_glu=s5_w_glu,
                   s5_b_glu=s5_b_glu, wo_in=wo_in, wo_out=wo_out, q_norm_g=q_norm_g, k_norm_g=k_norm_g,
                   final_g=final_g)
    mom_m = dict(c_ctx=m_c_ctx, norm_g=m_norm_g, w_mod=m_w_mod, b_mod=m_b_mod, we_in=m_we_in, we_out=m_we_out,
                 gm_v_g=m_gm_v_g, gm_w_s=m_gm_w_s, gm_b_s=m_gm_b_s, s5_lam_re=m_s5_lam_re, s5_lam_im=m_s5_lam_im,
                 s5_log_dt=m_s5_log_dt, s5_b_re=m_s5_b_re, s5_b_im=m_s5_b_im, s5_c_re=m_s5_c_re, s5_c_im=m_s5_c_im,
                 s5_d=m_s5_d, s5_w_glu=m_s5_w_glu, s5_b_glu=m_s5_b_glu, wo_in=m_wo_in, wo_out=m_wo_out,
                 q_norm_g=m_q_norm_g, k_norm_g=m_k_norm_g, final_g=m_final_g)
    mom_v = dict(c_ctx=v_c_ctx, norm_g=v_norm_g, w_mod=v_w_mod, b_mod=v_b_mod, we_in=v_we_in, we_out=v_we_out,
                 gm_v_g=v_gm_v_g, gm_w_s=v_gm_w_s, gm_b_s=v_gm_b_s, s5_lam_re=v_s5_lam_re, s5_lam_im=v_s5_lam_im,
                 s5_log_dt=v_s5_log_dt, s5_b_re=v_s5_b_re, s5_b_im=v_s5_b_im, s5_c_re=v_s5_c_re, s5_c_im=v_s5_c_im,
                 s5_d=v_s5_d, s5_w_glu=v_s5_w_glu, s5_b_glu=v_s5_b_glu, wo_in=v_wo_in, wo_out=v_wo_out,
                 q_norm_g=v_q_norm_g, k_norm_g=v_k_norm_g, final_g=v_final_g)

    ixy = 2 * lax.axis_index("x") + lax.axis_index("y")
    n_lat = x.shape[1]
    nl = norm_g.shape[0]
    nmod = w_mod.shape[2]
    xin = (ctx.reshape(ctx.shape[1], D), x.reshape(n_lat, D))

    ic = lax.axis_index("c")
    mine = [lax.dynamic_index_in_dim(weights[n], ic, 0, keepdims=False).astype(BF) for n in _BIG]
    got = exchange([(m_, "xy", "gather") for m_ in mine] + [(c, "xy", "gather")], "gather_weights")
    both = d2d([(g_.reshape(-1, g_.shape[-1]), "gather") for g_ in got[:len(_BIG)]], "swap_weights")
    both = [b_.reshape((2,) + g_.shape) for b_, g_ in zip(both, got)]
    wein = [(both[0], l) for l in range(2)]
    weout = [(both[1].reshape(2, 1, D, D), l) for l in range(2)]
    wglu = [(both[2].reshape(2, AW, AW), l) for l in range(2)]
    woin = [(both[3], l) for l in range(2)]
    woout = [(both[4].reshape(2, 1, D, D), l) for l in range(2)]
    c_group = got[len(_BIG)].reshape(4, D)

    cond = jnp.concatenate([c_group, jnp.broadcast_to(c_ctx.reshape(1, D), (4, D))], axis=0)
    b_shard = lax.dynamic_slice(b_mod, (0, ixy * nmod), (nl, nmod)).reshape(nl, 1, nmod)
    mpart = ada_fwd(cond, w_mod, b_shard)
    m_lat, m_ctx = exchange([(jnp.transpose(mpart[:, 0:4], (1, 0, 2)), "xy", "scatter"),
                             (mpart[:, 4], "xy", "gather")], "exchange_mod")
    m_lat = jnp.transpose(m_lat, (1, 0, 2)).reshape(nl, 3, D)
    m_ctx = jnp.transpose(m_ctx, (1, 0, 2)).reshape(nl, 3, D)
    mods = [jnp.stack([m_ctx[l], m_lat[l]], axis=0) for l in range(nl)]

    loss_part, dx, g, d_norm_g, d_mod_lat, d_mod_ctx, d_final_g = _local_step(
        xin, loss_target.reshape(n_lat, D), mods, wein, weout, wglu, woin, woout, weights)
    grad_x = dx.reshape(1, n_lat, D)

    d_mod_lat, d_mod_ctx = jnp.stack(d_mod_lat), jnp.stack(d_mod_ctx)
    dm_send = jnp.stack([d_mod_lat.reshape(nl, 4, nmod), d_mod_ctx.reshape(nl, 4, nmod)])
    (dm_got,) = exchange([(jnp.transpose(dm_send, (2, 0, 1, 3)), "xy", "scatter")], "exchange_dmod")
    dm_rows = jnp.concatenate([dm_got[:, 0], dm_got[:, 1]], axis=0)
    gw_mod, d_cctx = ada_bwd(cond, jnp.transpose(dm_rows, (1, 0, 2)), w_mod)
    g_small = dict(c_ctx=d_cctx.reshape(D), norm_g=jnp.stack(d_norm_g), b_mod=add2(d_mod_lat, d_mod_ctx, "add_dbmod"),
                   final_g=d_final_g.reshape(D))
    for name in _SMALL:
        if name not in g_small:
            g_small[name] = jnp.stack(g[name])

    flat = jnp.concatenate([g_small[n].reshape(-1) for n in _SMALL])
    nflat = flat.shape[0]
    npad = -(-nflat // _SMALL_ALIGN) * _SMALL_ALIGN
    flat = jnp.concatenate([flat, jnp.zeros((npad - nflat,), F32)]).reshape(8, npad // (8 * 128), 128)
    pairs = [gw_mod.reshape(2, nl // 2 * D, nmod)]
    for name in _BIG:
        st = jnp.stack(g[name]) if isinstance(g[name], list) else g[name]
        pairs.append(st.reshape(2, -1, st.shape[-1]))
    got_a = d2d([(pairs[k], "swap") for k in (1, 2, 3)], "reduce_chip_a")
    got_b = d2d([(pairs[k], "swap") for k in (0, 4, 5)], "reduce_chip_b")
    theirs = [got_b[0]] + got_a + got_b[1:]
    chip = [sum_own(pairs[k], theirs[k], f"sum_chip{k}", F32 if k == 0 else BF) for k in range(len(pairs))]
    parts = exchange([(flat, "all", "scatter")]
                     + [(s_.reshape(4, s_.shape[0] // 4, s_.shape[1]), "xy", "scatter") for s_ in chip[1:]],
                     "reduce_scatter")
    sums = [sum_lead(pt, f"sum_shard{k}") for k, pt in enumerate(parts)]
    full = d2d([(sums[0], "gather"), (chip[0], "gather")] + [(s_, "gather") for s_ in sums[1:]], "all_gather")
    (flat_full,) = exchange([(full[0], "xy", "gather")], "gather_small")
    full = [flat_full] + full[1:]
    flat = full[0].reshape(-1)
    grads = {}
    off = 0
    for name in _SMALL:
        sz = weights[name].size
        grads[name] = flat[off:off + sz].reshape(weights[name].shape)
        off += sz
    grads["w_mod"] = full[1].reshape(w_mod.shape)
    for k, name in enumerate(_BIG):
        grads[name] = full[2 + k].reshape(weights[name].shape)

    delta, new_m, new_v = {}, {}, {}
    views = [_view2d(weights[n]) for n in _SMALL]
    ds, nms, nvs = adamw_many(views, [grads[n].reshape(w2.shape) for n, w2 in zip(_SMALL, views)],
                              [mom_m[n].reshape(w2.shape) for n, w2 in zip(_SMALL, views)],
                              [mom_v[n].reshape(w2.shape) for n, w2 in zip(_SMALL, views)], "adamw_small")
    for n, d2, m2, v2 in zip(_SMALL, ds, nms, nvs):
        shp = weights[n].shape
        delta[n], new_m[n], new_v[n] = d2.reshape(shp), m2.reshape(shp), v2.reshape(shp)
    for name in ["w_mod"] + _BIG:
        w2 = _view2d(weights[name])
        d2, m2, v2 = adamw(w2, grads[name].reshape(w2.shape), mom_m[name].reshape(w2.shape),
                           mom_v[name].reshape(w2.shape), f"adamw_{name}")
        shp = weights[name].shape
        delta[name], new_m[name], new_v[name] = d2.reshape(shp), m2.reshape(shp), v2.reshape(shp)

    loss = lax.psum(loss_part[0, 0], ("x", "y", "c"))
    return (loss, grad_x, *[grads[n] for n in _WEIGHTS], *[delta[n] for n in _WEIGHTS],
            *[new_m[n] for n in _WEIGHTS], *[new_v[n] for n in _WEIGHTS])


def _local_step(xin, target, mods, wein, weout, wglu, woin, woout, w):
    norm_g, gm_v_g, gm_w_s, gm_b_s = w["norm_g"], w["gm_v_g"], w["gm_w_s"], w["gm_b_s"]
    s5_lam_re, s5_lam_im, s5_log_dt = w["s5_lam_re"], w["s5_lam_im"], w["s5_log_dt"]
    s5_b_re, s5_b_im, s5_c_re, s5_c_im = w["s5_b_re"], w["s5_b_im"], w["s5_c_re"], w["s5_c_im"]
    s5_d, s5_b_glu, q_norm_g, k_norm_g, final_g = w["s5_d"], w["s5_b_glu"], w["q_norm_g"], w["k_norm_g"], w["final_g"]
    nl = norm_g.shape[0]
    n_lat = xin[1].shape[0]

    cos, sins = _rope_tables(n_lat)

    s5p = []
    for i in range(2):
        lam_l = (s5_lam_re[i].reshape(2, SW), s5_lam_im[i].reshape(2, SW),
                 jnp.repeat(s5_log_dt[i], SP, axis=1))
        lam_r = (jnp.repeat(s5_lam_re[i].reshape(2 * SG, SP), SH, axis=0),
                 jnp.repeat(s5_lam_im[i].reshape(2 * SG, SP), SH, axis=0),
                 jnp.repeat(s5_log_dt[i].reshape(2 * SG, 1), SH, axis=0))
        b_r = (jnp.transpose(s5_b_re[i], (0, 1, 3, 2)).reshape(2 * SG * SH, SP),
               jnp.transpose(s5_b_im[i], (0, 1, 3, 2)).reshape(2 * SG * SH, SP))
        pw_re, pw_im, bbr, bbi = s5_disc(*lam_l, *lam_r, *b_r)
        s5p.append(dict(
            lam_r=lam_r, b_r=b_r, pw_re=pw_re, pw_im=pw_im, pw_im_conj=-pw_im,
            bb_re=_block_diag(bbr.reshape(2, SG, SH, SP), False).astype(BF),
            bb_im=_block_diag(bbi.reshape(2, SG, SH, SP), False).astype(BF),
            ct_re=_block_diag(s5_c_re[i], True).astype(BF), ct_im=_block_diag(s5_c_im[i], True).astype(BF)))

    saved = []
    xcur = xin
    h = pro_fwd(xin[0], xin[1], norm_g[0].reshape(1, D), mods[0], "pro_fwd0")
    for l in range(nl):
        i = l // 2
        sv = dict(x=xcur, h=h)
        if l % 2 == 0:
            p = mm_nn(h, wein[i], f"in_proj{l}")
            sp = s5p[i]
            for dr, rev in ((0, False), (1, True)):
                sv[f"y{dr}"], sv[f"hpr{dr}"], sv[f"hpi{dr}"] = s5q_fwd(
                    p, sp["bb_re"], sp["bb_im"], sp["ct_re"], sp["ct_im"], sp["pw_re"], sp["pw_im"], dr, rev,
                    f"s5_fwd{l}_{dr}")
            mix = mix_fwd(p, sv["y0"], sv["y1"], gm_v_g[i].reshape(1, AW), gm_w_s[i].astype(BF),
                          gm_b_s[i].reshape(NGRP, CHUNK, 1), s5_d[i].reshape(1, AW), wglu[i],
                          s5_b_glu[i].reshape(1, AW), f"mix_fwd{l}")
            o = mm_nn(mix, weout[i], f"out_proj{l}")
        else:
            p = mm_nn(h, woin[i], f"in_proj{l}")
            sv["qkv"] = attn_prep(p, q_norm_g[i].reshape(1, HD), k_norm_g[i].reshape(1, HD), cos, sins, f"attn_prep{l}")
            sv["o_att"], mix, sv["lse"] = attn_fwd(sv["qkv"], p, f"attn_fwd{l}")
            o = mm_nn(mix, woout[i], f"out_proj{l}")
        sv.update(p=p, mix=mix, o=o)
        saved.append(sv)
        if l < nl - 1:
            xcur, h = res_pro_fwd(xcur, o, mods[l], True, norm_g[l + 1].reshape(1, D), mods[l + 1], f"res_pro_fwd{l}")

    loss_part, dx, d_final_g, do, dgt = final_loss(xcur, target, final_g.reshape(1, D), saved[-1]["o"], mods[-1])

    g = {}
    gbuf = {}
    d_norm_g, d_mod_lat, d_mod_ctx = [None] * nl, [None] * nl, [None] * nl
    for name in ("s5_w_glu", "gm_v_g", "gm_w_s", "gm_b_s", "s5_lam_re", "s5_lam_im",
                 "s5_log_dt", "s5_b_re", "s5_b_im", "s5_c_re", "s5_c_im", "s5_d", "s5_b_glu", "q_norm_g", "k_norm_g"):
        g[name] = [None, None]
    for l in reversed(range(nl)):
        i = l // 2
        sv = saved[l]
        w_out = weout[i] if l % 2 == 0 else woout[i]
        dmix = mm_nt(do, w_out, f"out_dgrad{l}")
        out_name, in_name = ("we_out", "we_in") if l % 2 == 0 else ("wo_out", "wo_in")
        gbuf[out_name] = mm_tn(sv["mix"], do, 1, f"out_wgrad{l}", slot=i, into=gbuf.get(out_name))
        if l % 2 == 0:
            sp = s5p[i]
            (dp, dy, g["gm_w_s"][i], dbs, dvg, dd, g["s5_w_glu"][i], dbg) = mix_bwd(
                sv["p"], sv["y0"], sv["y1"], dmix, gm_v_g[i].reshape(1, AW), gm_w_s[i].astype(BF),
                gm_b_s[i].reshape(NGRP, CHUNK, 1), s5_d[i].reshape(1, AW), wglu[i], s5_b_glu[i].reshape(1, AW),
                f"mix_bwd{l}")
            g["gm_b_s"][i], g["gm_v_g"][i] = dbs.reshape(NGRP, CHUNK), dvg.reshape(AW)
            g["s5_d"][i], g["s5_b_glu"][i] = dd.reshape(AW), dbg.reshape(AW)
            g["s5_w_glu"][i] = g["s5_w_glu"][i].reshape(4, AW // 4, AW)
            dxd, das_r, das_i, dbbs_r, dbbs_i, dcs_r, dcs_i = [], [], [], [], [], [], []
            for dr, rev in ((0, False), (1, True)):
                dxs_d, da_r, da_i, dbb_r, dbb_i, dc_r, dc_i = s5q_bwd(
                    sv["p"], sv[f"hpr{dr}"], sv[f"hpi{dr}"], dy, sp["bb_re"], sp["bb_im"], sp["ct_re"], sp["ct_im"],
                    sp["pw_re"], sp["pw_im_conj"], dr, rev, f"s5_bwd{l}_{dr}")
                dxd.append(dxs_d)
                das_r.append(jnp.repeat(da_r.reshape(SG, SP), SH, axis=0))
                das_i.append(jnp.repeat(da_i.reshape(SG, SP), SH, axis=0))
                dbbs_r.append(_diag_blocks(dbb_r).reshape(SG * SH, SP))
                dbbs_i.append(_diag_blocks(dbb_i).reshape(SG * SH, SP))
                dcs_r.append(_diag_blocks(dc_r))
                dcs_i.append(_diag_blocks(dc_i))
            cat = lambda parts: jnp.concatenate(parts, axis=0)
            dlr, dli, dldt, dbr, dbi = s5_param_bwd(*sp["lam_r"], *sp["b_r"], cat(das_r), cat(das_i),
                                                    cat(dbbs_r), cat(dbbs_i))
            g["s5_lam_re"][i], g["s5_lam_im"][i] = dlr.reshape(2, SG, SP), dli.reshape(2, SG, SP)
            g["s5_log_dt"][i] = dldt.reshape(2, SG)
            g["s5_b_re"][i] = jnp.transpose(dbr.reshape(2, SG, SH, SP), (0, 1, 3, 2))
            g["s5_b_im"][i] = jnp.transpose(dbi.reshape(2, SG, SH, SP), (0, 1, 3, 2))
            g["s5_c_re"][i], g["s5_c_im"][i] = jnp.stack(dcs_r), jnp.stack(dcs_i)
            dp = s5p_dx_sum(dy, s5_d[i].reshape(1, AW), dxd[0], dxd[1], dp, f"s5_dx_sum{l}")
            w_in = wein[i]
        else:
            dq, dp, dk, dv = attn_bwd(sv["qkv"], sv["p"], dmix, sv["o_att"], sv["lse"], f"attn_bwd{l}")
            dp, dqg, dkg = attn_prep_bwd(sv["p"], dq, dk, dv, q_norm_g[i].reshape(1, HD),
                                         k_norm_g[i].reshape(1, HD), cos, sins, dp, f"attn_prep_bwd{l}")
            g["q_norm_g"][i], g["k_norm_g"][i] = dqg.reshape(HD), dkg.reshape(HD)
            w_in = woin[i]
        dh = mm_nt(dp, w_in, f"in_dgrad{l}")
        gbuf[in_name] = mm_tn(sv["h"], dp, 4, f"in_wgrad{l}", slot=i, into=gbuf.get(in_name))
        dgt_l = dgt
        if l > 0:
            dx, dmod2, dng, do, dgt = pro_res_bwd(sv["x"], dh, dx, norm_g[l].reshape(1, D), mods[l],
                                                  saved[l - 1]["o"], mods[l - 1], f"pro_res_bwd{l}")
        else:
            dx, dmod2, dng = pro_bwd(xin[0], xin[1], dh, dx, norm_g[l].reshape(1, D), mods[l], f"pro_bwd{l}")
        d_norm_g[l] = dng.reshape(D)
        d_mod_ctx[l] = jnp.concatenate([dmod2[0, 0], dmod2[0, 1], dgt_l[0]])
        d_mod_lat[l] = jnp.concatenate([dmod2[1, 0], dmod2[1, 1], dgt_l[1]])
    g.update(gbuf)
    return loss_part, dx, g, d_norm_g, d_mod_lat, d_mod_ctx, d_final_g
```
